```python
import jax, jax.numpy as jnp
from jax import lax
import numpy as np

D_MODEL = 2048
BATCH = 8
SEQ = 2048
DEPTH = 2

CHUNK = 64
D_RWKV = D_MODEL // 2
D_HGRN = D_MODEL - D_RWKV
RWKV_HEAD = 64
RWKV_HEADS = D_RWKV // RWKV_HEAD
DECAY_RANK = max(32, int(round(1.8 * D_RWKV ** 0.5 / 32)) * 32)
A_RANK = max(32, int(round(1.8 * D_RWKV ** 0.5 / 32)) * 32)
VRES_RANK = max(32, int(round(1.3 * D_RWKV ** 0.5 / 32)) * 32)
HGRN_EXPAND = 128
HGRN_HEADS = D_HGRN // HGRN_EXPAND
HGRN_HEAD_V = D_HGRN // HGRN_HEADS
RWKV_COLS = 4 * D_RWKV + DECAY_RANK + A_RANK
HGRN_COLS = 4 * D_HGRN
IN_COLS = RWKV_COLS + HGRN_COLS
ALPHA = (2 * DEPTH) ** 0.25
BETA = (8 * DEPTH) ** -0.25
LN_EPS = 1e-5
GN_EPS = 64e-5
RMS_EPS = 1e-5
LB_FLOOR = 1e-30

kernel_name = "rwkv7_hgrn2_parallel_deepnorm"


def _heads(t, n):
    return t.reshape(t.shape[:-1] + (-1, n))


def _layer_norm(x, w, b):
    x = x.astype(jnp.float32)
    mu = jnp.mean(x, -1, keepdims=True)
    var = jnp.mean(jnp.square(x - mu), -1, keepdims=True)
    return (x - mu) * lax.rsqrt(var + LN_EPS) * w + b


def _token_shift(y, mu):
    y_prev = jnp.pad(y, ((0, 0), (1, 0), (0, 0)))[:, :-1]
    return y + mu * (y_prev - y)


def _rwkv7_scan(r, w, k, v, a_vec, b_vec):
    Bsz, T, H, N = r.shape

    def step(S, inp):
        r_t, w_t, k_t, v_t, a_t, b_t = inp
        sa = jnp.einsum('bhij,bhj->bhi', S, a_t)
        S = S * w_t[:, :, None, :] + sa[..., None] * b_t[:, :, None, :] + v_t[..., None] * k_t[:, :, None, :]
        return S, jnp.einsum('bhij,bhj->bhi', S, r_t)

    S0 = jnp.zeros((Bsz, H, N, N), jnp.float32)
    xs = tuple(jnp.moveaxis(t, 1, 0) for t in (r, w, k, v, a_vec, b_vec))
    _, o = lax.scan(step, S0, xs)
    return jnp.moveaxis(o, 0, 1)


def _rwkv7_branch(rw, v_first, w0, w_up, a0, a_up, k_k, k_a, r_k, gn_w, gn_b, v_mix):
    r, k, v, z, wd, ad = jnp.split(
        rw, [D_RWKV, 2 * D_RWKV, 3 * D_RWKV, 4 * D_RWKV, 4 * D_RWKV + DECAY_RANK], axis=-1)
    w_raw = w0 + jnp.tanh(wd) @ w_up
    decay = jnp.exp(-jnp.exp(-jax.nn.softplus(-w_raw) - 0.5))
    a = jax.nn.sigmoid(a0 + ad @ a_up)
    if v_mix is None:
        v_first = v
    else:
        v0, v_down, v_up = v_mix
        v = v + (v_first - v) * jax.nn.sigmoid(v0 + (v @ v_down) @ v_up)
    kk = _heads(k * k_k, RWKV_HEAD)
    kk = kk / jnp.maximum(jnp.sqrt(jnp.sum(kk * kk, -1, keepdims=True)), 1e-12)
    k = k * (1.0 + (a - 1.0) * k_a)
    rh, kh, vh = _heads(r, RWKV_HEAD), _heads(k, RWKV_HEAD), _heads(v, RWKV_HEAD)
    ah = _heads(a, RWKV_HEAD)
    o = _rwkv7_scan(rh, _heads(decay, RWKV_HEAD), kh, vh, -kk, kk * ah)
    mu = jnp.mean(o, -1, keepdims=True)
    var = jnp.mean(jnp.square(o - mu), -1, keepdims=True)
    o = (o - mu) * lax.rsqrt(var + GN_EPS) * _heads(gn_w, RWKV_HEAD) + _heads(gn_b, RWKV_HEAD)
    o = o + jnp.sum(rh * kh * _heads(r_k, RWKV_HEAD), -1, keepdims=True) * vh
    return o.reshape(rw.shape[:-1] + (D_RWKV,)) * jax.nn.silu(z), v_first


def _hgrn2_chunkwise(q, log_f, k, i):
    Bsz, T, H, DK = q.shape
    DV = i.shape[-1]
    NC = T // CHUNK

    def to_chunks(t):
        return jnp.moveaxis(t.reshape(Bsz, NC, CHUNK, H, t.shape[-1]), 1, 0)

    causal = jnp.tril(jnp.ones((CHUNK, CHUNK), bool))[None, :, :, None, None]

    def step(S, inp):
        q_c, lf_c, k_c, i_c = inp
        b = jnp.cumsum(lf_c, axis=1)
        diff = b[:, :, None] - b[:, None, :]
        decay = jnp.where(causal, jnp.exp(jnp.where(causal, diff, 0.0)), 0.0)
        att = jnp.einsum('btshd,bshd->btsh', q_c[:, :, None] * decay, k_c)
        o_intra = jnp.einsum('btsh,bshv->bthv', att, i_c)
        o_inter = jnp.einsum('bthd,bhdv->bthv', q_c * jnp.exp(b), S)
        b_last = b[:, -1]
        k_dec = k_c * jnp.exp(b_last[:, None] - b)
        S = S * jnp.exp(b_last)[..., None] + jnp.einsum('bshd,bshv->bhdv', k_dec, i_c)
        return S, o_intra + o_inter

    S0 = jnp.zeros((Bsz, H, DK, DV), jnp.float32)
    _, o = lax.scan(step, S0, (to_chunks(q), to_chunks(log_f), to_chunks(k), to_chunks(i)))
    return jnp.moveaxis(o, 0, 1).reshape(Bsz, T, H, DV)


def _hgrn2_branch(hg, lb, g_norm_w):
    q, f_raw, i_in, z = jnp.split(hg, 4, axis=-1)
    q = jax.nn.silu(q)
    log_lb = jnp.log(jnp.maximum(lb, LB_FLOOR))
    log_f = jnp.logaddexp(log_lb, jnp.log1p(-lb) + jax.nn.log_sigmoid(f_raw))
    k = (1.0 - lb) * jax.nn.sigmoid(-f_raw)
    o = _hgrn2_chunkwise(_heads(q, HGRN_EXPAND), _heads(log_f, HGRN_EXPAND),
                         _heads(k, HGRN_EXPAND), _heads(i_in, HGRN_HEAD_V))
    o = o * lax.rsqrt(jnp.mean(o * o, -1, keepdims=True) + RMS_EPS)
    return o.reshape(hg.shape[:-1] + (D_HGRN,)) * g_norm_w * jax.nn.silu(z)


def _fwd_setup_inputs(seed: int = 0) -> dict:
    key = jax.random.key(seed)
    ks = jax.random.split(key, 24)
    L, L1 = DEPTH, DEPTH - 1

    def nrm(k, shape, s):
        return s * jax.random.normal(k, shape, jnp.float32)

    x = nrm(ks[0], (BATCH, SEQ, D_MODEL), 1.0)
    col_scale = jnp.concatenate([
        jnp.ones((2 * D_RWKV,), jnp.float32), jnp.full((D_RWKV,), BETA, jnp.float32),
        jnp.ones((D_RWKV + DECAY_RANK + A_RANK,), jnp.float32),
        jnp.ones((2 * D_HGRN,), jnp.float32), jnp.full((D_HGRN,), BETA, jnp.float32),
        jnp.ones((D_HGRN,), jnp.float32)])
    w_in = nrm(ks[1], (L, D_MODEL, IN_COLS), D_MODEL ** -0.5) * col_scale
    shift_mu = jax.random.uniform(ks[2], (L, RWKV_COLS), jnp.float32)
    ramp = (jnp.arange(D_RWKV, dtype=jnp.float32) / (D_RWKV - 1)) ** 0.85
    w_decay0 = -6.0 + 5.0 * ramp + nrm(ks[3], (L, D_RWKV), 0.1)
    w_decay_up = nrm(ks[4], (L, DECAY_RANK, D_RWKV), 0.3 * DECAY_RANK ** -0.5)
    a0 = nrm(ks[5], (L, D_RWKV), 0.1)
    a_up = nrm(ks[6], (L, A_RANK, D_RWKV), 0.3 * A_RANK ** -0.5)
    k_k = 0.85 + nrm(ks[7], (L, D_RWKV), 0.05)
    k_a = 1.0 + nrm(ks[8], (L, D_RWKV), 0.05)
    r_k = nrm(ks[9], (L, D_RWKV), 0.1)
    ln_x_w = 1.0 + nrm(ks[10], (L, D_RWKV), 0.05)
    ln_x_b = nrm(ks[11], (L, D_RWKV), 0.02)
    v_mix0 = 1.0 + nrm(ks[12], (L1, D_RWKV), 0.1)
    v_mix_down = nrm(ks[13], (L1, D_RWKV, VRES_RANK), D_RWKV ** -0.5)
    v_mix_up = nrm(ks[14], (L1, VRES_RANK, D_RWKV), 0.3 * VRES_RANK ** -0.5)
    lb_logits = nrm(ks[15], (L, D_HGRN), 0.5)
    g_norm_w = 1.0 + nrm(ks[16], (L, D_HGRN), 0.05)
    w_out = nrm(ks[17], (L, D_MODEL, D_MODEL), BETA * D_MODEL ** -0.5)
    ln_w = 1.0 + nrm(ks[18], (L, D_MODEL), 0.05)
    ln_b = nrm(ks[19], (L, D_MODEL), 0.02)
    return {"x": x, "w_in": w_in, "shift_mu": shift_mu, "w_decay0": w_decay0,
            "w_decay_up": w_decay_up, "a0": a0, "a_up": a_up, "k_k": k_k, "k_a": k_a,
            "r_k": r_k, "ln_x_w": ln_x_w, "ln_x_b": ln_x_b, "v_mix0": v_mix0,
            "v_mix_down": v_mix_down, "v_mix_up": v_mix_up, "lb_logits": lb_logits,
            "g_norm_w": g_norm_w, "w_out": w_out, "ln_w": ln_w, "ln_b": ln_b}


def _fwd_reference(x, w_in, shift_mu, w_decay0, w_decay_up, a0, a_up, k_k, k_a, r_k, ln_x_w, ln_x_b,
              v_mix0, v_mix_down, v_mix_up, lb_logits, g_norm_w, w_out, ln_w, ln_b):
    out_dtype = x.dtype
    lb_sm = jax.nn.softmax(lb_logits.astype(jnp.float32), axis=0)
    lower_bounds = jnp.cumsum(lb_sm, axis=0) - lb_sm[0]
    h = x.astype(jnp.float32)
    v_first = None
    for l in range(DEPTH):
        proj = jnp.einsum('btd,dc->btc', h, w_in[l].astype(jnp.float32))
        rw = _token_shift(proj[..., :RWKV_COLS], shift_mu[l])
        hg = proj[..., RWKV_COLS:]
        v_mix = None if l == 0 else (v_mix0[l - 1], v_mix_down[l - 1], v_mix_up[l - 1])
        o_rwkv, v_first = _rwkv7_branch(rw, v_first, w_decay0[l], w_decay_up[l], a0[l], a_up[l],
                                        k_k[l], k_a[l], r_k[l], ln_x_w[l], ln_x_b[l], v_mix)
        o_hgrn = _hgrn2_branch(hg, lower_bounds[l], g_norm_w[l])
        y = jnp.einsum('btc,cd->btd', jnp.concatenate([o_rwkv, o_hgrn], axis=-1), w_out[l])
        h = _layer_norm(ALPHA * h + y, ln_w[l], ln_b[l])
    return h.astype(out_dtype)


import jax as _jax
import jax.numpy as _jnp

TWIN_FORMAT = 'train_step'
FWD_PARAMS = ['x', 'w_in', 'shift_mu', 'w_decay0', 'w_decay_up', 'a0', 'a_up', 'k_k', 'k_a', 'r_k', 'ln_x_w', 'ln_x_b', 'v_mix0', 'v_mix_down', 'v_mix_up', 'lb_logits', 'g_norm_w', 'w_out', 'ln_w', 'ln_b']
TWIN_WEIGHTS = ['w_in', 'shift_mu', 'w_decay0', 'w_decay_up', 'a0', 'a_up', 'k_k', 'k_a', 'r_k', 'ln_x_w', 'ln_x_b', 'v_mix0', 'v_mix_down', 'v_mix_up', 'lb_logits', 'g_norm_w', 'w_out', 'ln_w', 'ln_b']
TWIN_DIFF_INPUT = 'x'
TWIN_INPUTS = ['x', 'w_in', 'shift_mu', 'w_decay0', 'w_decay_up', 'a0', 'a_up', 'k_k', 'k_a', 'r_k', 'ln_x_w', 'ln_x_b', 'v_mix0', 'v_mix_down', 'v_mix_up', 'lb_logits', 'g_norm_w', 'w_out', 'ln_w', 'ln_b', 'loss_target', 'm_w_in', 'm_shift_mu', 'm_w_decay0', 'm_w_decay_up', 'm_a0', 'm_a_up', 'm_k_k', 'm_k_a', 'm_r_k', 'm_ln_x_w', 'm_ln_x_b', 'm_v_mix0', 'm_v_mix_down', 'm_v_mix_up', 'm_lb_logits', 'm_g_norm_w', 'm_w_out', 'm_ln_w', 'm_ln_b', 'v_w_in', 'v_shift_mu', 'v_w_decay0', 'v_w_decay_up', 'v_a0', 'v_a_up', 'v_k_k', 'v_k_a', 'v_r_k', 'v_ln_x_w', 'v_ln_x_b', 'v_v_mix0', 'v_v_mix_down', 'v_v_mix_up', 'v_lb_logits', 'v_g_norm_w', 'v_w_out', 'v_ln_w', 'v_ln_b']
TWIN_OUTPUTS = ['loss', 'grad_x', 'grad_w_in', 'grad_shift_mu', 'grad_w_decay0', 'grad_w_decay_up', 'grad_a0', 'grad_a_up', 'grad_k_k', 'grad_k_a', 'grad_r_k', 'grad_ln_x_w', 'grad_ln_x_b', 'grad_v_mix0', 'grad_v_mix_down', 'grad_v_mix_up', 'grad_lb_logits', 'grad_g_norm_w', 'grad_w_out', 'grad_ln_w', 'grad_ln_b', 'delta_w_in', 'delta_shift_mu', 'delta_w_decay0', 'delta_w_decay_up', 'delta_a0', 'delta_a_up', 'delta_k_k', 'delta_k_a', 'delta_r_k', 'delta_ln_x_w', 'delta_ln_x_b', 'delta_v_mix0', 'delta_v_mix_down', 'delta_v_mix_up', 'delta_lb_logits', 'delta_g_norm_w', 'delta_w_out', 'delta_ln_w', 'delta_ln_b', 'new_m_w_in', 'new_m_shift_mu', 'new_m_w_decay0', 'new_m_w_decay_up', 'new_m_a0', 'new_m_a_up', 'new_m_k_k', 'new_m_k_a', 'new_m_r_k', 'new_m_ln_x_w', 'new_m_ln_x_b', 'new_m_v_mix0', 'new_m_v_mix_down', 'new_m_v_mix_up', 'new_m_lb_logits', 'new_m_g_norm_w', 'new_m_w_out', 'new_m_ln_w', 'new_m_ln_b', 'new_v_w_in', 'new_v_shift_mu', 'new_v_w_decay0', 'new_v_w_decay_up', 'new_v_a0', 'new_v_a_up', 'new_v_k_k', 'new_v_k_a', 'new_v_r_k', 'new_v_ln_x_w', 'new_v_ln_x_b', 'new_v_v_mix0', 'new_v_v_mix_down', 'new_v_v_mix_up', 'new_v_lb_logits', 'new_v_g_norm_w', 'new_v_w_out', 'new_v_ln_w', 'new_v_ln_b']
TWIN_LEAF_KINDS = {'loss': 'loss', 'grad_x': 'grad_x', 'grad_w_in': 'grad_w', 'grad_shift_mu': 'grad_w', 'grad_w_decay0': 'grad_w', 'grad_w_decay_up': 'grad_w', 'grad_a0': 'grad_w', 'grad_a_up': 'grad_w', 'grad_k_k': 'grad_w', 'grad_k_a': 'grad_w', 'grad_r_k': 'grad_w', 'grad_ln_x_w': 'grad_w', 'grad_ln_x_b': 'grad_w', 'grad_v_mix0': 'grad_w', 'grad_v_mix_down': 'grad_w', 'grad_v_mix_up': 'grad_w', 'grad_lb_logits': 'grad_w', 'grad_g_norm_w': 'grad_w', 'grad_w_out': 'grad_w', 'grad_ln_w': 'grad_w', 'grad_ln_b': 'grad_w', 'delta_w_in': 'delta_w', 'delta_shift_mu': 'delta_w', 'delta_w_decay0': 'delta_w', 'delta_w_decay_up': 'delta_w', 'delta_a0': 'delta_w', 'delta_a_up': 'delta_w', 'delta_k_k': 'delta_w', 'delta_k_a': 'delta_w', 'delta_r_k': 'delta_w', 'delta_ln_x_w': 'delta_w', 'delta_ln_x_b': 'delta_w', 'delta_v_mix0': 'delta_w', 'delta_v_mix_down': 'delta_w', 'delta_v_mix_up': 'delta_w', 'delta_lb_logits': 'delta_w', 'delta_g_norm_w': 'delta_w', 'delta_w_out': 'delta_w', 'delta_ln_w': 'delta_w', 'delta_ln_b': 'delta_w', 'new_m_w_in': 'new_m', 'new_m_shift_mu': 'new_m', 'new_m_w_decay0': 'new_m', 'new_m_w_decay_up': 'new_m', 'new_m_a0': 'new_m', 'new_m_a_up': 'new_m', 'new_m_k_k': 'new_m', 'new_m_k_a': 'new_m', 'new_m_r_k': 'new_m', 'new_m_ln_x_w': 'new_m', 'new_m_ln_x_b': 'new_m', 'new_m_v_mix0': 'new_m', 'new_m_v_mix_down': 'new_m', 'new_m_v_mix_up': 'new_m', 'new_m_lb_logits': 'new_m', 'new_m_g_norm_w': 'new_m', 'new_m_w_out': 'new_m', 'new_m_ln_w': 'new_m', 'new_m_ln_b': 'new_m', 'new_v_w_in': 'new_v', 'new_v_shift_mu': 'new_v', 'new_v_w_decay0': 'new_v', 'new_v_w_decay_up': 'new_v', 'new_v_a0': 'new_v', 'new_v_a_up': 'new_v', 'new_v_k_k': 'new_v', 'new_v_k_a': 'new_v', 'new_v_r_k': 'new_v', 'new_v_ln_x_w': 'new_v', 'new_v_ln_x_b': 'new_v', 'new_v_v_mix0': 'new_v', 'new_v_v_mix_down': 'new_v', 'new_v_v_mix_up': 'new_v', 'new_v_lb_logits': 'new_v', 'new_v_g_norm_w': 'new_v', 'new_v_w_out': 'new_v', 'new_v_ln_w': 'new_v', 'new_v_ln_b': 'new_v'}


def _forward(args):
    return _fwd_reference(*[args[k] for k in FWD_PARAMS])


def _output_shape():
    out = _jax.eval_shape(lambda: _forward(_fwd_setup_inputs(0)))
    return out.shape, out.dtype

N_MICROBATCH = 1
ADAM_LR = 0.001
ADAM_B1 = 0.9
ADAM_B2 = 0.999
ADAM_EPS = 1e-08
ADAM_WD = 0.01
ADAM_STEP = 10
PER_EXAMPLE_BATCH_AXIS = {'x': 0, 'loss_target': 0}
SHARED_INPUTS = []
_WEIGHT_DTYPES = {'w_in': _jnp.float32, 'shift_mu': _jnp.float32, 'w_decay0': _jnp.float32, 'w_decay_up': _jnp.float32, 'a0': _jnp.float32, 'a_up': _jnp.float32, 'k_k': _jnp.float32, 'k_a': _jnp.float32, 'r_k': _jnp.float32, 'ln_x_w': _jnp.float32, 'ln_x_b': _jnp.float32, 'v_mix0': _jnp.float32, 'v_mix_down': _jnp.float32, 'v_mix_up': _jnp.float32, 'lb_logits': _jnp.float32, 'g_norm_w': _jnp.float32, 'w_out': _jnp.float32, 'ln_w': _jnp.float32, 'ln_b': _jnp.float32}
MOMENT_SCALE = {'w_in': 1.475312e-02, 'shift_mu': 1.838474e-02, 'w_decay0': 5.736936e-03, 'w_decay_up': 8.765231e-04, 'a0': 5.021950e-03, 'a_up': 4.485893e-03, 'k_k': 9.149597e-03, 'k_a': 1.165315e-02, 'r_k': 1.072218e-02, 'ln_x_w': 1.063867e-02, 'ln_x_b': 1.162474e-02, 'v_mix0': 3.707998e-03, 'v_mix_down': 2.516969e-03, 'v_mix_up': 1.490168e-03, 'lb_logits': 1.216361e-03, 'g_norm_w': 1.354107e-02, 'w_out': 2.398358e-02, 'ln_w': 5.716732e+00, 'ln_b': 1.716845e-01}


def _to_microbatches(a, axis):
    t = _jnp.moveaxis(a, axis, 0)
    t = t.reshape((N_MICROBATCH, t.shape[0] // N_MICROBATCH) + t.shape[1:])
    return _jnp.moveaxis(t, 1, axis + 1)


def setup_inputs(seed: int = 0) -> dict:
    inp = _fwd_setup_inputs(seed)
    key = _jax.random.fold_in(_jax.random.key(seed), 7919)
    shape, _ = _output_shape()
    out = dict(inp)
    out["loss_target"] = _jax.random.normal(_jax.random.fold_in(key, 0), shape, _jnp.float32)
    for i, name in enumerate(TWIN_WEIGHTS):
        w = inp[name].astype(_jnp.float32)
        if MOMENT_SCALE is None:
            s = _jnp.sqrt(_jnp.mean(_jnp.square(w)) + 1e-30)
        else:
            s = MOMENT_SCALE[name]
        km, kv = _jax.random.split(_jax.random.fold_in(key, i + 1))
        out[name] = w
        out["m_" + name] = s * _jax.random.normal(km, w.shape, _jnp.float32)
        out["v_" + name] = (s * s) * _jax.random.uniform(kv, w.shape, _jnp.float32, 0.5, 1.5)
    if N_MICROBATCH > 1:
        for name, axis in PER_EXAMPLE_BATCH_AXIS.items():
            out[name] = _to_microbatches(out[name], axis)
    return {'x': out['x'], 'w_in': out['w_in'], 'shift_mu': out['shift_mu'], 'w_decay0': out['w_decay0'], 'w_decay_up': out['w_decay_up'], 'a0': out['a0'], 'a_up': out['a_up'], 'k_k': out['k_k'], 'k_a': out['k_a'], 'r_k': out['r_k'], 'ln_x_w': out['ln_x_w'], 'ln_x_b': out['ln_x_b'], 'v_mix0': out['v_mix0'], 'v_mix_down': out['v_mix_down'], 'v_mix_up': out['v_mix_up'], 'lb_logits': out['lb_logits'], 'g_norm_w': out['g_norm_w'], 'w_out': out['w_out'], 'ln_w': out['ln_w'], 'ln_b': out['ln_b'], 'loss_target': out['loss_target'], 'm_w_in': out['m_w_in'], 'm_shift_mu': out['m_shift_mu'], 'm_w_decay0': out['m_w_decay0'], 'm_w_decay_up': out['m_w_decay_up'], 'm_a0': out['m_a0'], 'm_a_up': out['m_a_up'], 'm_k_k': out['m_k_k'], 'm_k_a': out['m_k_a'], 'm_r_k': out['m_r_k'], 'm_ln_x_w': out['m_ln_x_w'], 'm_ln_x_b': out['m_ln_x_b'], 'm_v_mix0': out['m_v_mix0'], 'm_v_mix_down': out['m_v_mix_down'], 'm_v_mix_up': out['m_v_mix_up'], 'm_lb_logits': out['m_lb_logits'], 'm_g_norm_w': out['m_g_norm_w'], 'm_w_out': out['m_w_out'], 'm_ln_w': out['m_ln_w'], 'm_ln_b': out['m_ln_b'], 'v_w_in': out['v_w_in'], 'v_shift_mu': out['v_shift_mu'], 'v_w_decay0': out['v_w_decay0'], 'v_w_decay_up': out['v_w_decay_up'], 'v_a0': out['v_a0'], 'v_a_up': out['v_a_up'], 'v_k_k': out['v_k_k'], 'v_k_a': out['v_k_a'], 'v_r_k': out['v_r_k'], 'v_ln_x_w': out['v_ln_x_w'], 'v_ln_x_b': out['v_ln_x_b'], 'v_v_mix0': out['v_v_mix0'], 'v_v_mix_down': out['v_v_mix_down'], 'v_v_mix_up': out['v_v_mix_up'], 'v_lb_logits': out['v_lb_logits'], 'v_g_norm_w': out['v_g_norm_w'], 'v_w_out': out['v_w_out'], 'v_ln_w': out['v_ln_w'], 'v_ln_b': out['v_ln_b']}


def _loss(weights, diff, rest, loss_target):
    with _jax.named_scope("forward"):
        args = {**rest, TWIN_DIFF_INPUT: diff, **{k: w.astype(_WEIGHT_DTYPES[k]) for k, w in weights.items()}}
        y = _forward(args)
    with _jax.named_scope("loss_head"):
        err = _jnp.square(y.astype(_jnp.float32) - loss_target)
        return 0.5 * _jnp.sum(_jnp.mean(err, axis=-1)) if err.ndim else 0.5 * err


def _adamw(w, g, m, v):
    m = ADAM_B1 * m + (1.0 - ADAM_B1) * g
    v = ADAM_B2 * v + (1.0 - ADAM_B2) * _jnp.square(g)
    m_hat = m / (1.0 - ADAM_B1 ** ADAM_STEP)
    v_hat = v / (1.0 - ADAM_B2 ** ADAM_STEP)
    delta = -ADAM_LR * (m_hat / (_jnp.sqrt(v_hat) + ADAM_EPS) + ADAM_WD * w)
    return delta, m, v


def reference(x, w_in, shift_mu, w_decay0, w_decay_up, a0, a_up, k_k, k_a, r_k, ln_x_w, ln_x_b, v_mix0, v_mix_down, v_mix_up, lb_logits, g_norm_w, w_out, ln_w, ln_b, loss_target, m_w_in, m_shift_mu, m_w_decay0, m_w_decay_up, m_a0, m_a_up, m_k_k, m_k_a, m_r_k, m_ln_x_w, m_ln_x_b, m_v_mix0, m_v_mix_down, m_v_mix_up, m_lb_logits, m_g_norm_w, m_w_out, m_ln_w, m_ln_b, v_w_in, v_shift_mu, v_w_decay0, v_w_decay_up, v_a0, v_a_up, v_k_k, v_k_a, v_r_k, v_ln_x_w, v_ln_x_b, v_v_mix0, v_v_mix_down, v_v_mix_up, v_lb_logits, v_g_norm_w, v_w_out, v_ln_w, v_ln_b):
    given = dict(x=x, w_in=w_in, shift_mu=shift_mu, w_decay0=w_decay0, w_decay_up=w_decay_up, a0=a0, a_up=a_up, k_k=k_k, k_a=k_a, r_k=r_k, ln_x_w=ln_x_w, ln_x_b=ln_x_b, v_mix0=v_mix0, v_mix_down=v_mix_down, v_mix_up=v_mix_up, lb_logits=lb_logits, g_norm_w=g_norm_w, w_out=w_out, ln_w=ln_w, ln_b=ln_b, loss_target=loss_target, m_w_in=m_w_in, m_shift_mu=m_shift_mu, m_w_decay0=m_w_decay0, m_w_decay_up=m_w_decay_up, m_a0=m_a0, m_a_up=m_a_up, m_k_k=m_k_k, m_k_a=m_k_a, m_r_k=m_r_k, m_ln_x_w=m_ln_x_w, m_ln_x_b=m_ln_x_b, m_v_mix0=m_v_mix0, m_v_mix_down=m_v_mix_down, m_v_mix_up=m_v_mix_up, m_lb_logits=m_lb_logits, m_g_norm_w=m_g_norm_w, m_w_out=m_w_out, m_ln_w=m_ln_w, m_ln_b=m_ln_b, v_w_in=v_w_in, v_shift_mu=v_shift_mu, v_w_decay0=v_w_decay0, v_w_decay_up=v_w_decay_up, v_a0=v_a0, v_a_up=v_a_up, v_k_k=v_k_k, v_k_a=v_k_a, v_r_k=v_r_k, v_ln_x_w=v_ln_x_w, v_ln_x_b=v_ln_x_b, v_v_mix0=v_v_mix0, v_v_mix_down=v_v_mix_down, v_v_mix_up=v_v_mix_up, v_lb_logits=v_lb_logits, v_g_norm_w=v_g_norm_w, v_w_out=v_w_out, v_ln_w=v_ln_w, v_ln_b=v_ln_b)
    weights = {n: given[n] for n in TWIN_WEIGHTS}
    shared = {n: given[n] for n in SHARED_INPUTS}
    per_example = {n: given[n] for n in ['x']}
    grad_fn = _jax.value_and_grad(_loss, argnums=(0, 1))

    def one_microbatch(ex, loss_target):
        ex = dict(ex)
        diff = ex.pop(TWIN_DIFF_INPUT)
        return grad_fn(weights, diff, {**shared, **ex}, loss_target)

    if N_MICROBATCH == 1:
        loss, (grad_w, grad_x) = one_microbatch(per_example, given["loss_target"])
    else:
        def body(carry, xs):
            loss_sum, grad_sum = carry
            l_k, (gw_k, gx_k) = one_microbatch(xs[0], xs[1])
            with _jax.named_scope("update"):
                return (loss_sum + l_k, _jax.tree.map(_jnp.add, grad_sum, gw_k)), gx_k

        init = (_jnp.zeros((), _jnp.float32), _jax.tree.map(_jnp.zeros_like, weights))
        (loss, grad_w), grad_x = _jax.lax.scan(body, init, (per_example, given["loss_target"]))
    with _jax.named_scope("update"):
        delta_w, new_m, new_v = {}, {}, {}
        for n in TWIN_WEIGHTS:
            delta_w[n], new_m[n], new_v[n] = _adamw(weights[n], grad_w[n], given["m_" + n], given["v_" + n])
    return (loss, grad_x, *[grad_w[n] for n in TWIN_WEIGHTS], *[delta_w[n] for n in TWIN_WEIGHTS],
            *[new_m[n] for n in TWIN_WEIGHTS], *[new_v[n] for n in TWIN_WEIGHTS])
```

```python
import functools

import jax
import jax.numpy as jnp
from jax import lax
from jax.experimental import pallas as pl
from jax.experimental.pallas import tpu as pltpu

f32 = jnp.float32
bf16 = jnp.bfloat16
HI = lax.Precision.HIGHEST

N_DEV = 8
CHUNK = 64
LANES = 128
RWKV_HEAD = 64
DEPTH = 2
ALPHA = (2 * DEPTH) ** 0.25
LN_EPS = 1e-5
GN_EPS = 64e-5
RMS_EPS = 1e-5
LB_FLOOR = 1e-30
ADAM_LR, ADAM_B1, ADAM_B2, ADAM_EPS, ADAM_WD, ADAM_STEP = 0.001, 0.9, 0.999, 1e-08, 0.01, 10
MESH = pl.DeviceIdType.MESH


def _iota(shape, d):
    return lax.broadcasted_iota(jnp.int32, shape, d)


def _dot(a, b):
    return jnp.dot(a, b, precision=HI, preferred_element_type=f32)


def _dot_nt(a, b):
    return lax.dot_general(a, b, (((1,), (1,)), ((), ())), precision=HI, preferred_element_type=f32)


def _dot_tn(a, b):
    return lax.dot_general(a, b, (((0,), (0,)), ((), ())), precision=HI, preferred_element_type=f32)


def _softplus(x):
    return jnp.maximum(x, 0.0) + jnp.log1p(jnp.exp(-jnp.abs(x)))


def _log_sigmoid(x):
    return -_softplus(-x)


def _logaddexp(a, b):
    return jnp.maximum(a, b) + jnp.log1p(jnp.exp(-jnp.abs(a - b)))


def _silu(x):
    return x * jax.nn.sigmoid(x)


def _tril(c, strict):
    r, s = _iota((c, c), 0), _iota((c, c), 1)
    return (r > s) if strict else (r >= s)


def _last_row(a):
    c = a.shape[0]
    return jnp.sum(jnp.where(_iota(a.shape, 0) == c - 1, a, 0.0), axis=0, keepdims=True)


def _rwkv_pre(layer1, prm, y, prev, vf):
    c = y.shape[0]
    if layer1:
        mu, w0, a0, wup, aup, v0, vdown, vup = prm
    else:
        mu, w0, a0, wup, aup = prm
    dr = w0.shape[1]
    shift = (_iota((c, c), 0) == _iota((c, c), 1) + 1).astype(f32)
    y_prev = _dot(shift, y) + jnp.where(_iota((c, 1), 0) == 0, prev, 0.0)
    rw = y + mu * (y_prev - y)
    r, k, v, z = (rw[:, i * dr:(i + 1) * dr] for i in range(4))
    wdad = rw[:, 4 * dr:4 * dr + LANES]
    w_raw = w0 + _dot(jnp.tanh(wdad), wup)
    lw = -jnp.exp(-_softplus(-w_raw) - 0.5)
    asig = jax.nn.sigmoid(a0 + _dot(wdad, aup))
    if layer1:
        v = v + (vf - v) * jax.nn.sigmoid(v0 + _dot(_dot(v, vdown), vup))
    return r, k, v, z, lw, asig


def _rwkv_pair(pp, m0, xs):
    kkw, kaw, rkw, gnw, gnb = pp
    r, k, v, z, lw, asig = xs
    c = r.shape[0]
    lane = _iota((1, LANES), 1)
    head_masks = ((lane < RWKV_HEAD).astype(f32), (lane >= RWKV_HEAD).astype(f32))
    same_head = _iota((LANES, LANES), 0) // RWKV_HEAD == _iota((LANES, LANES), 1) // RWKV_HEAD
    g = same_head.astype(f32)
    kk = k * kkw
    kk = kk / jnp.maximum(jnp.sqrt(_dot(kk * kk, g)), 1e-12)
    k2 = k * (1.0 + (asig - 1.0) * kaw)
    a = -kk
    b = kk * asig
    cum = _dot(_tril(c, False).astype(f32), lw)
    at = a * jnp.exp(cum - lw)
    rt = r * jnp.exp(cum)
    en = jnp.exp(-cum)
    bt = b * en
    kt = k2 * en
    strict, incl = _tril(c, True), _tril(c, False)
    eye = (_iota((c, c), 0) == _iota((c, c), 1)).astype(f32)
    u = jnp.zeros((c, LANES), f32)
    o = jnp.zeros((c, LANES), f32)
    for mh in head_masks:
        ath, rth = at * mh, rt * mh
        aab = jnp.where(strict, _dot_nt(ath, bt), 0.0)
        aak = jnp.where(strict, _dot_nt(ath, kt), 0.0)
        arb = jnp.where(incl, _dot_nt(rth, bt), 0.0)
        ark = jnp.where(incl, _dot_nt(rth, kt), 0.0)
        tm, ak = eye + aab, aab
        for _ in range(5):
            ak = _dot(ak, ak)
            tm = tm + _dot(tm, ak)
        uh = _dot(tm, _dot(ath, m0) + _dot(aak, v)) * mh
        o = o + (_dot(rth, m0) + _dot(arb, uh) + _dot(ark, v)) * mh
        u = u + uh
    cum_last = _last_row(cum)
    dec_end = jnp.exp(cum_last - cum)
    diag = jnp.where(_iota((LANES, LANES), 0) == _iota((LANES, LANES), 1),
                     jnp.broadcast_to(jnp.exp(cum_last), (LANES, LANES)), 0.0)
    m_new = _dot(diag, m0) + (_dot_tn(b * dec_end, u) + _dot_tn(k2 * dec_end, v)) * g
    mean = _dot(o, g) * (1.0 / RWKV_HEAD)
    d = o - mean
    var = _dot(d * d, g) * (1.0 / RWKV_HEAD)
    on = d * lax.rsqrt(var + GN_EPS) * gnw + gnb
    bonus = _dot(r * k2 * rkw, g) * v
    return (on + bonus) * _silu(z), m_new


def _split_lanes(a, n):
    return [a[:, i * LANES:(i + 1) * LANES] for i in range(n)]


def _rwkv_specs(layer1, t, dr, rwc, n_pair, rev):
    nc = t // CHUNK

    def cidx(c):
        return (nc - 1 - c) if rev else c

    full = lambda shape: pl.BlockSpec(shape, lambda c, p: tuple(0 for _ in shape))
    specs = [
        pl.BlockSpec((CHUNK, rwc), lambda c, p: (cidx(c), 0)),
        pl.BlockSpec((8, rwc), lambda c, p: (jnp.maximum(cidx(c) * (CHUNK // 8) - 1, 0), 0)),
    ]
    if layer1:
        specs.append(pl.BlockSpec((CHUNK, dr), lambda c, p: (cidx(c), 0)))
    prm_shapes = [(1, rwc), (1, dr), (1, dr), (LANES, dr), (LANES, dr)]
    if layer1:
        prm_shapes += [(1, dr), (dr, LANES), (LANES, dr)]
    specs += [full(s) for s in prm_shapes]
    specs.append(pl.BlockSpec((1, 8, LANES), lambda c, p: (p, 0, 0)))
    return specs, prm_shapes, cidx, full


def _rwkv_fwd(layer1, proj, vf, prm, pp, cat_width):
    t = proj.shape[0]
    dr = prm[1].shape[1]
    rwc = prm[0].shape[1]
    n_pair = dr // LANES
    nc = t // CHUNK
    n_prm = len(prm)
    specs, _, _, _ = _rwkv_specs(layer1, t, dr, rwc, n_pair, False)

    def body(*refs):
        y_ref, prev_ref = refs[0], refs[1]
        i = 2
        vf_ref = None
        if layer1:
            vf_ref = refs[i]
            i += 1
        prm_refs = refs[i:i + n_prm]
        i += n_prm
        pp_ref = refs[i]
        i += 1
        cat_ref = refs[i]
        i += 1
        vout_ref = None
        if not layer1:
            vout_ref = refs[i]
            i += 1
        mck_ref, x_s, m_s = refs[i], refs[i + 1], refs[i + 2]
        c, p = pl.program_id(0), pl.program_id(1)

        @pl.when((c == 0) & (p == 0))
        def _():
            m_s[...] = jnp.zeros_like(m_s)

        @pl.when(p == 0)
        def _():
            prev = prev_ref[pl.ds(7, 1), :] * (c != 0).astype(f32)
            xs = _rwkv_pre(layer1, tuple(r[...] for r in prm_refs), y_ref[...], prev,
                           vf_ref[...] if layer1 else None)
            for q, a in enumerate(xs):
                for j, piece in enumerate(_split_lanes(a, n_pair)):
                    x_s[q * n_pair + j] = piece
            if not layer1:
                vout_ref[...] = xs[2]

        m0 = m_s[p]
        mck_ref[0, 0] = m0
        ppv = tuple(pp_ref[0, pl.ds(q, 1), :] for q in range(5))
        og, m_new = _rwkv_pair(ppv, m0, tuple(x_s[q * n_pair + p] for q in range(6)))
        cat_ref[...] = og
        m_s[p] = m_new

    out_shape = [jax.ShapeDtypeStruct((t, cat_width), f32)]
    out_specs = [pl.BlockSpec((CHUNK, LANES), lambda c, p: (c, p))]
    if not layer1:
        out_shape.append(jax.ShapeDtypeStruct((t, dr), f32))
        out_specs.append(pl.BlockSpec((CHUNK, dr), lambda c, p: (c, 0)))
    out_shape.append(jax.ShapeDtypeStruct((nc, n_pair, LANES, LANES), f32))
    out_specs.append(pl.BlockSpec((1, 1, LANES, LANES), lambda c, p: (c, p, 0, 0)))
    args = [proj, proj] + ([vf] if layer1 else []) + list(prm) + [pp]
    return pl.pallas_call(
        body, grid=(nc, n_pair), in_specs=specs, out_specs=out_specs, out_shape=out_shape,
        scratch_shapes=[pltpu.VMEM((6 * n_pair, CHUNK, LANES), f32), pltpu.VMEM((n_pair, LANES, LANES), f32)],
        compiler_params=pltpu.CompilerParams(dimension_semantics=("arbitrary", "arbitrary")),
        name=f"rwkv_fwd_l{int(layer1)}",
    )(*args)


def _rwkv_bwd(layer1, proj, vf, prm, pp, mck, dcat, dvout):
    t = proj.shape[0]
    dr = prm[1].shape[1]
    rwc = prm[0].shape[1]
    n_pair = dr // LANES
    nc = t // CHUNK
    n_prm = len(prm)
    specs, prm_shapes, cidx, full = _rwkv_specs(layer1, t, dr, rwc, n_pair, True)
    specs.append(pl.BlockSpec((1, 1, LANES, LANES), lambda c, p: (cidx(c), p, 0, 0)))
    specs.append(pl.BlockSpec((CHUNK, LANES), lambda c, p: (cidx(c), p)))
    if not layer1:
        specs.append(pl.BlockSpec((CHUNK, dr), lambda c, p: (cidx(c), 0)))

    def body(*refs):
        y_ref, prev_ref = refs[0], refs[1]
        i = 2
        vf_ref = None
        if layer1:
            vf_ref = refs[i]
            i += 1
        prm_refs = refs[i:i + n_prm]
        i += n_prm
        pp_ref, mck_ref, dog_ref = refs[i], refs[i + 1], refs[i + 2]
        i += 3
        dvout_ref = None
        if not layer1:
            dvout_ref = refs[i]
            i += 1
        dy_ref = refs[i]
        i += 1
        dvf_ref = None
        if layer1:
            dvf_ref = refs[i]
            i += 1
        dprm_refs = refs[i:i + n_prm]
        i += n_prm
        dpp_ref = refs[i]
        x_s, dx_s, dm_s, dprev_s = refs[i + 1:i + 5]
        c, p = pl.program_id(0), pl.program_id(1)
        cr = nc - 1 - c

        def prev_row():
            return prev_ref[pl.ds(7, 1), :] * (cr != 0).astype(f32)

        @pl.when((c == 0) & (p == 0))
        def _():
            dm_s[...] = jnp.zeros_like(dm_s)
            dprev_s[...] = jnp.zeros_like(dprev_s)
            dpp_ref[...] = jnp.zeros_like(dpp_ref)
            for r in dprm_refs:
                r[...] = jnp.zeros_like(r)

        @pl.when(p == 0)
        def _():
            xs = _rwkv_pre(layer1, tuple(r[...] for r in prm_refs), y_ref[...], prev_row(),
                           vf_ref[...] if layer1 else None)
            for q, a in enumerate(xs):
                for j, piece in enumerate(_split_lanes(a, n_pair)):
                    x_s[q * n_pair + j] = piece

        ppv = tuple(pp_ref[0, pl.ds(q, 1), :] for q in range(5))
        xs_p = tuple(x_s[q * n_pair + p] for q in range(6))
        _, vjp_pair = jax.vjp(_rwkv_pair, ppv, mck_ref[0, 0], xs_p)
        dppv, dm0, dxs = vjp_pair((dog_ref[...], dm_s[p]))
        dm_s[p] = dm0
        for q in range(6):
            dx_s[q * n_pair + p] = dxs[q]
        for q in range(5):
            dpp_ref[p, pl.ds(q, 1), :] += dppv[q]

        @pl.when(p == n_pair - 1)
        def _():
            dxs_full = [jnp.concatenate([dx_s[q * n_pair + j] for j in range(n_pair)], axis=1) for q in range(6)]
            if not layer1:
                dxs_full[2] = dxs_full[2] + dvout_ref[...]
            prm_v = tuple(r[...] for r in prm_refs)
            if layer1:
                _, vjp_pre = jax.vjp(functools.partial(_rwkv_pre, True), prm_v, y_ref[...], prev_row(), vf_ref[...])
                dprm, dy, dprev, dvf = vjp_pre(tuple(dxs_full))
                dvf_ref[...] = dvf
            else:
                _, vjp_pre = jax.vjp(lambda a, b, d: _rwkv_pre(False, a, b, d, None), prm_v, y_ref[...], prev_row())
                dprm, dy, dprev = vjp_pre(tuple(dxs_full))
            dy_ref[...] = dy + jnp.where(_iota((CHUNK, 1), 0) == CHUNK - 1, dprev_s[...], 0.0)
            dprev_s[...] = dprev
            for r, gval in zip(dprm_refs, dprm):
                r[...] += gval

    out_shape = [jax.ShapeDtypeStruct((t, rwc), f32)]
    out_specs = [pl.BlockSpec((CHUNK, rwc), lambda c, p: (cidx(c), 0))]
    if layer1:
        out_shape.append(jax.ShapeDtypeStruct((t, dr), f32))
        out_specs.append(pl.BlockSpec((CHUNK, dr), lambda c, p: (cidx(c), 0)))
    out_shape += [jax.ShapeDtypeStruct(s, f32) for s in prm_shapes]
    out_specs += [full(s) for s in prm_shapes]
    out_shape.append(jax.ShapeDtypeStruct((n_pair, 8, LANES), f32))
    out_specs.append(full((n_pair, 8, LANES)))
    args = [proj, proj] + ([vf] if layer1 else []) + list(prm) + [pp, mck, dcat] + ([] if layer1 else [dvout])
    return pl.pallas_call(
        body, grid=(nc, n_pair), in_specs=specs, out_specs=out_specs, out_shape=out_shape,
        scratch_shapes=[pltpu.VMEM((6 * n_pair, CHUNK, LANES), f32), pltpu.VMEM((6 * n_pair, CHUNK, LANES), f32),
                        pltpu.VMEM((n_pair, LANES, LANES), f32), pltpu.VMEM((1, rwc), f32)],
        compiler_params=pltpu.CompilerParams(dimension_semantics=("arbitrary", "arbitrary")),
        name=f"rwkv_bwd_l{int(layer1)}",
    )(*args)


def _hgrn_chunk(layer1, lbl, gnw, s0, q_raw, f_raw, i_in, z):
    c = q_raw.shape[0]
    q = _silu(q_raw)
    ls = _log_sigmoid(f_raw)
    if layer1:
        l0, l1 = lbl[0:1, :], lbl[1:2, :]
        mx = jnp.maximum(l0, l1)
        e0, e1 = jnp.exp(l0 - mx), jnp.exp(l1 - mx)
        sm0, sm1 = e0 / (e0 + e1), e1 / (e0 + e1)
        lb = (sm0 + sm1) - sm0
        log_f = _logaddexp(jnp.log(jnp.maximum(lb, LB_FLOOR)), jnp.log1p(-lb) + ls)
        k = (1.0 - lb) * jax.nn.sigmoid(-f_raw)
    else:
        log_f = _logaddexp(jnp.full_like(ls, jnp.log(jnp.float32(LB_FLOOR))), ls)
        k = jax.nn.sigmoid(-f_raw)
    b = _dot(_tril(c, False).astype(f32), log_f)
    row, col = _iota((c, c), 0), _iota((c, c), 1)
    trow = _iota((c, 1), 0)
    att = jnp.zeros((c, c), f32)
    half = c // 2
    while half >= 1:
        blk = 2 * half
        sel = (col == (row // blk) * blk + half - 1).astype(f32)
        bref = _dot(sel, b)
        upper = (trow % blk) >= half
        qh = q * jnp.exp(jnp.where(upper, b - bref, 0.0)) * upper.astype(f32)
        kh = k * jnp.exp(jnp.where(upper, 0.0, bref - b)) * (1.0 - upper.astype(f32))
        att = att + jnp.where(row // blk == col // blk, _dot_nt(qh, kh), 0.0)
        half //= 2
    o = _dot(att, i_in) + jnp.sum(q * k, axis=-1, keepdims=True) * i_in + _dot(q * jnp.exp(b), s0)
    b_last = _last_row(b)
    diag = jnp.where(_iota((LANES, LANES), 0) == _iota((LANES, LANES), 1),
                     jnp.broadcast_to(jnp.exp(b_last), (LANES, LANES)), 0.0)
    s_new = _dot(diag, s0) + _dot_tn(k * jnp.exp(b_last - b), i_in)
    o = o * lax.rsqrt(jnp.mean(o * o, axis=-1, keepdims=True) + RMS_EPS)
    return o * gnw * _silu(z), s_new


def _hgrn_in_specs(t, dh, col0, rev):
    nc = t // CHUNK
    nh = dh // LANES

    def cidx(c):
        return (nc - 1 - c) if rev else c

    specs = [pl.BlockSpec((CHUNK, LANES), functools.partial(lambda g, h, c: (cidx(c), col0 + g * nh + h), g))
             for g in range(4)]
    specs.append(pl.BlockSpec((2, LANES), lambda h, c: (0, h)))
    specs.append(pl.BlockSpec((1, LANES), lambda h, c: (0, h)))
    return specs, cidx


def _hgrn_fwd(layer1, proj, lbl, gnw, cat, rwc):
    t, d = cat.shape
    dh = gnw.shape[1]
    nh = dh // LANES
    nc = t // CHUNK
    col0 = rwc // LANES
    specs, _ = _hgrn_in_specs(t, dh, col0, False)
    specs.append(pl.BlockSpec(memory_space=pl.ANY))
    cat_col0 = (d - dh) // LANES

    def body(q_ref, f_ref, i_ref, z_ref, lbl_ref, gnw_ref, cat_in, cat_ref, sck_ref, s_s):
        del cat_in
        c = pl.program_id(1)

        @pl.when(c == 0)
        def _():
            s_s[...] = jnp.zeros_like(s_s)

        s0 = s_s[...]
        sck_ref[0, 0] = s0
        out, s_new = _hgrn_chunk(layer1, lbl_ref[...], gnw_ref[...], s0, q_ref[...], f_ref[...], i_ref[...], z_ref[...])
        cat_ref[...] = out
        s_s[...] = s_new

    return pl.pallas_call(
        body, grid=(nh, nc), in_specs=specs,
        out_specs=[pl.BlockSpec((CHUNK, LANES), lambda h, c: (c, cat_col0 + h)),
                   pl.BlockSpec((1, 1, LANES, LANES), lambda h, c: (h, c, 0, 0))],
        out_shape=[jax.ShapeDtypeStruct((t, d), f32), jax.ShapeDtypeStruct((nh, nc, LANES, LANES), f32)],
        scratch_shapes=[pltpu.VMEM((LANES, LANES), f32)],
        input_output_aliases={6: 0},
        compiler_params=pltpu.CompilerParams(dimension_semantics=("arbitrary", "arbitrary")),
        name=f"hgrn_fwd_l{int(layer1)}",
    )(proj, proj, proj, proj, lbl, gnw, cat)


def _hgrn_bwd(layer1, proj, lbl, gnw, sck, dcat, rwc):
    t, d = dcat.shape
    dh = gnw.shape[1]
    nh = dh // LANES
    nc = t // CHUNK
    col0 = rwc // LANES
    specs, cidx = _hgrn_in_specs(t, dh, col0, True)
    cat_col0 = (d - dh) // LANES
    specs.append(pl.BlockSpec((1, 1, LANES, LANES), lambda h, c: (h, cidx(c), 0, 0)))
    specs.append(pl.BlockSpec((CHUNK, LANES), lambda h, c: (cidx(c), cat_col0 + h)))

    def body(q_ref, f_ref, i_ref, z_ref, lbl_ref, gnw_ref, sck_ref, do_ref, dp_ref, dlbl_ref, dgnw_ref, ds_s):
        c = pl.program_id(1)

        @pl.when(c == 0)
        def _():
            ds_s[...] = jnp.zeros_like(ds_s)
            dlbl_ref[...] = jnp.zeros_like(dlbl_ref)
            dgnw_ref[...] = jnp.zeros_like(dgnw_ref)

        _, vjp = jax.vjp(functools.partial(_hgrn_chunk, layer1), lbl_ref[...], gnw_ref[...], sck_ref[0, 0],
                         q_ref[...], f_ref[...], i_ref[...], z_ref[...])
        dlbl, dgnw, ds0, dq, df, di, dz = vjp((do_ref[...], ds_s[...]))
        ds_s[...] = ds0
        dlbl_ref[...] += dlbl
        dgnw_ref[...] += dgnw
        dp_ref[0] = dq
        dp_ref[1] = df
        dp_ref[2] = di
        dp_ref[3] = dz

    return pl.pallas_call(
        body, grid=(nh, nc), in_specs=specs,
        out_specs=[pl.BlockSpec((4, CHUNK, LANES), lambda h, c: (0, cidx(c), h)),
                   pl.BlockSpec((2, LANES), lambda h, c: (0, h)),
                   pl.BlockSpec((1, LANES), lambda h, c: (0, h))],
        out_shape=[jax.ShapeDtypeStruct((4, t, dh), f32), jax.ShapeDtypeStruct((2, dh), f32),
                   jax.ShapeDtypeStruct((1, dh), f32)],
        scratch_shapes=[pltpu.VMEM((LANES, LANES), f32)],
        compiler_params=pltpu.CompilerParams(dimension_semantics=("arbitrary", "arbitrary")),
        name=f"hgrn_bwd_l{int(layer1)}",
    )(proj, proj, proj, proj, lbl, gnw, sck, dcat)


def _ln(h, y, w, b):
    u = ALPHA * h + y
    mu = jnp.mean(u, axis=-1, keepdims=True)
    var = jnp.mean(jnp.square(u - mu), axis=-1, keepdims=True)
    return (u - mu) * lax.rsqrt(var + LN_EPS) * w + b


def _row_tile(t):
    return 256 if t % 256 == 0 else t


def _ln_fwd(h, y, w, b):
    t, d = h.shape
    tr = _row_tile(t)

    def body(h_ref, y_ref, w_ref, b_ref, o_ref):
        o_ref[...] = _ln(h_ref[...], y_ref[...], w_ref[...], b_ref[...])

    row = pl.BlockSpec((tr, d), lambda i: (i, 0))
    vec = pl.BlockSpec((1, d), lambda i: (0, 0))
    return pl.pallas_call(body, grid=(t // tr,), in_specs=[row, row, vec, vec], out_specs=row,
                          out_shape=jax.ShapeDtypeStruct((t, d), f32), name="ln_fwd")(h, y, w, b)


def _ln_loss(h, y, w, b, tgt):
    t, d = h.shape
    tr = _row_tile(t)

    def body(h_ref, y_ref, w_ref, b_ref, t_ref, g_ref, loss_ref):
        @pl.when(pl.program_id(0) == 0)
        def _():
            loss_ref[...] = jnp.zeros_like(loss_ref)

        err = _ln(h_ref[...], y_ref[...], w_ref[...], b_ref[...]) - t_ref[...]
        g_ref[...] = err * (1.0 / d)
        loss_ref[...] += 0.5 * jnp.sum(jnp.mean(jnp.square(err), axis=-1, keepdims=True), axis=0, keepdims=True)

    row = pl.BlockSpec((tr, d), lambda i: (i, 0))
    vec = pl.BlockSpec((1, d), lambda i: (0, 0))
    return pl.pallas_call(
        body, grid=(t // tr,), in_specs=[row, row, vec, vec, row],
        out_specs=[row, pl.BlockSpec((1, LANES), lambda i: (0, 0))],
        out_shape=[jax.ShapeDtypeStruct((t, d), f32), jax.ShapeDtypeStruct((1, LANES), f32)],
        compiler_params=pltpu.CompilerParams(dimension_semantics=("arbitrary",)), name="ln_loss")(h, y, w, b, tgt)


def _ln_bwd(h, y, w, b, dout):
    t, d = h.shape
    tr = _row_tile(t)

    def body(h_ref, y_ref, w_ref, b_ref, do_ref, dy_ref, dw_ref, db_ref):
        @pl.when(pl.program_id(0) == 0)
        def _():
            dw_ref[...] = jnp.zeros_like(dw_ref)
            db_ref[...] = jnp.zeros_like(db_ref)

        _, vjp = jax.vjp(lambda yy, ww, bb: _ln(h_ref[...], yy, ww, bb), y_ref[...], w_ref[...], b_ref[...])
        dy, dw, db = vjp(do_ref[...])
        dy_ref[...] = dy
        dw_ref[...] += dw
        db_ref[...] += db

    row = pl.BlockSpec((tr, d), lambda i: (i, 0))
    vec = pl.BlockSpec((1, d), lambda i: (0, 0))
    return pl.pallas_call(
        body, grid=(t // tr,), in_specs=[row, row, vec, vec, row], out_specs=[row, vec, vec],
        out_shape=[jax.ShapeDtypeStruct((t, d), f32), jax.ShapeDtypeStruct((1, d), f32), jax.ShapeDtypeStruct((1, d), f32)],
        compiler_params=pltpu.CompilerParams(dimension_semantics=("arbitrary",)), name="ln_bwd")(h, y, w, b, dout)


def _pick(n, prefs):
    for p in prefs:
        if n % p == 0:
            return p
    return n


def _matmul(a, b, mode, name, add=None, add_scale=1.0):
    if mode == "nn":
        (m, k), n = a.shape, b.shape[1]
    elif mode == "nt":
        (m, k), n = a.shape, b.shape[0]
    else:
        (k, m), n = a.shape, b.shape[1]
    tm = _pick(m, (512, 640, 256, 128))
    tn = _pick(n, (1024, 640, 512, 256, 128))
    tk = _pick(k, (2048, 1024, 640, 512, 256, 128))
    nk = k // tk
    dims = {"nn": (((1,), (0,)), ((), ())), "nt": (((1,), (1,)), ((), ())), "tn": (((0,), (0,)), ((), ()))}[mode]

    def body(*refs):
        a_ref, b_ref = refs[0], refs[1]
        add_ref = refs[2] if add is not None else None
        o_ref, acc = refs[-2], refs[-1]
        kk = pl.program_id(2)

        @pl.when(kk == 0)
        def _():
            acc[...] = jnp.zeros_like(acc)

        acc[...] += lax.dot_general(a_ref[...].astype(bf16), b_ref[...].astype(bf16), dims, preferred_element_type=f32)

        @pl.when(kk == nk - 1)
        def _():
            res = acc[...]
            if add is not None:
                res = res + add_scale * add_ref[...]
            o_ref[...] = res

    a_spec = pl.BlockSpec((tk, tm), lambda i, j, kk: (kk, i)) if mode == "tn" else pl.BlockSpec((tm, tk), lambda i, j, kk: (i, kk))
    b_spec = pl.BlockSpec((tn, tk), lambda i, j, kk: (j, kk)) if mode == "nt" else pl.BlockSpec((tk, tn), lambda i, j, kk: (kk, j))
    o_spec = pl.BlockSpec((tm, tn), lambda i, j, kk: (i, j))
    in_specs = [a_spec, b_spec] + ([o_spec] if add is not None else [])
    args = [a, b] + ([add] if add is not None else [])
    return pl.pallas_call(
        body, grid=(m // tm, n // tn, nk), in_specs=in_specs, out_specs=o_spec,
        out_shape=jax.ShapeDtypeStruct((m, n), f32), scratch_shapes=[pltpu.VMEM((tm, tn), f32)],
        compiler_params=pltpu.CompilerParams(dimension_semantics=("parallel", "parallel", "arbitrary")),
        name=name,
    )(*args)


def _position():
    return lax.axis_index("x"), lax.axis_index("y"), lax.axis_index("c")


def _flip(pos, k):
    x, y, c = pos
    return (1 - x if k & 4 else x, 1 - y if k & 2 else y, 1 - c if k & 1 else c)


def _index(pos):
    return 4 * pos[0] + 2 * pos[1] + pos[2]


def _all_gather_rows(x, name):
    m_per, n = x.shape

    def body(x_ref, out_ref, send_sems, recv_sems, local_sem):
        me = _position()
        sibling = _flip(me, 1)
        chips = (2, 4, 6)

        def rows(pos):
            return out_ref.at[pl.ds(_index(pos) * m_per, m_per), :]

        def copy(sem, block, to, src=None):
            return pltpu.make_async_remote_copy(
                src_ref=rows(block) if src is None else src, dst_ref=rows(block),
                send_sem=send_sems.at[sem], recv_sem=recv_sems.at[sem], device_id=to, device_id_type=MESH)

        mine = pltpu.make_async_copy(x_ref, rows(me), local_sem)
        mine.start()
        first = [copy(0, me, sibling, src=x_ref)]
        first += [copy(1 + j, me, _flip(me, k), src=x_ref) for j, k in enumerate(chips)]
        for cp in first:
            cp.start()
        passed = [copy(4 + j, _flip(me, k), sibling) for j, k in enumerate(chips)]
        for j, k in enumerate(chips):
            copy(1 + j, _flip(me, k), me).wait_recv()
            passed[j].start()
        copy(0, sibling, me).wait_recv()
        for j, k in enumerate(chips):
            copy(4 + j, _flip(sibling, k), me).wait_recv()
        for cp in first + passed:
            cp.wait_send()
        mine.wait()

    return pl.pallas_call(
        body, out_shape=jax.ShapeDtypeStruct((N_DEV * m_per, n), x.dtype),
        in_specs=[pl.BlockSpec(memory_space=pl.ANY)], out_specs=pl.BlockSpec(memory_space=pl.ANY),
        scratch_shapes=[pltpu.SemaphoreType.DMA((7,)), pltpu.SemaphoreType.DMA((7,)), pltpu.SemaphoreType.DMA(())],
        name=name,
    )(x)


def _exchange_blocks(g, name):
    m_per = g.shape[0] // N_DEV
    n = g.shape[1]

    def body(g_ref, out_ref, send_sems, recv_sems, local_sem):
        me = _position()
        mine = pltpu.make_async_copy(g_ref.at[pl.ds(_index(me) * m_per, m_per), :], out_ref.at[_index(me)], local_sem)
        mine.start()

        def copy(k):
            peer = _flip(me, k)
            return pltpu.make_async_remote_copy(
                src_ref=g_ref.at[pl.ds(_index(peer) * m_per, m_per), :], dst_ref=out_ref.at[_index(me)],
                send_sem=send_sems.at[k - 1], recv_sem=recv_sems.at[k - 1], device_id=peer, device_id_type=MESH)

        def arrival(k):
            peer = _flip(me, k)
            return pltpu.make_async_remote_copy(
                src_ref=g_ref.at[pl.ds(_index(peer) * m_per, m_per), :], dst_ref=out_ref.at[_index(peer)],
                send_sem=send_sems.at[k - 1], recv_sem=recv_sems.at[k - 1], device_id=peer, device_id_type=MESH)

        sends = [copy(k) for k in range(1, N_DEV)]
        for cp in sends:
            cp.start()
        for k in range(1, N_DEV):
            arrival(k).wait_recv()
        for cp in sends:
            cp.wait_send()
        mine.wait()

    return pl.pallas_call(
        body, out_shape=jax.ShapeDtypeStruct((N_DEV, m_per, n), g.dtype),
        in_specs=[pl.BlockSpec(memory_space=pl.ANY)], out_specs=pl.BlockSpec(memory_space=pl.ANY),
        scratch_shapes=[pltpu.SemaphoreType.DMA((7,)), pltpu.SemaphoreType.DMA((7,)), pltpu.SemaphoreType.DMA(())],
        name=name,
    )(g)


def _sum_slots(parts, name):
    _, m, n = parts.shape
    tr = _pick(m, (104, 128, 64, 32, 16, 8))

    def body(p_ref, o_ref):
        acc = p_ref[0]
        for s in range(1, N_DEV):
            acc = acc + p_ref[s]
        o_ref[...] = acc

    return pl.pallas_call(
        body, grid=(m // tr,), in_specs=[pl.BlockSpec((N_DEV, tr, n), lambda i: (0, i, 0))],
        out_specs=pl.BlockSpec((tr, n), lambda i: (i, 0)), out_shape=jax.ShapeDtypeStruct((m, n), f32), name=name,
    )(parts)


def _adamw(w, g, m, v, name):
    shape = w.shape
    n = shape[-1]
    r = w.size // n
    w2, g2, m2, v2 = (a.reshape(r, n) for a in (w, g, m, v))
    tr = _pick(r, (256, 128, 64, 32, 16, 8)) if r * n > 65536 else r

    def body(w_ref, g_ref, m_ref, v_ref, d_ref, mo_ref, vo_ref):
        gg = g_ref[...]
        mm = ADAM_B1 * m_ref[...] + (1.0 - ADAM_B1) * gg
        vv = ADAM_B2 * v_ref[...] + (1.0 - ADAM_B2) * jnp.square(gg)
        m_hat = mm / (1.0 - ADAM_B1 ** ADAM_STEP)
        v_hat = vv / (1.0 - ADAM_B2 ** ADAM_STEP)
        d_ref[...] = -ADAM_LR * (m_hat / (jnp.sqrt(v_hat) + ADAM_EPS) + ADAM_WD * w_ref[...])
        mo_ref[...] = mm
        vo_ref[...] = vv

    spec = pl.BlockSpec((tr, n), lambda i: (i, 0))
    outs = pl.pallas_call(
        body, grid=(r // tr,), in_specs=[spec] * 4, out_specs=[spec] * 3,
        out_shape=[jax.ShapeDtypeStruct((r, n), f32)] * 3, name=name,
    )(w2, g2, m2, v2)
    return tuple(o.reshape(shape) for o in outs)


_SMALL = ("shift_mu", "w_decay0", "a0", "k_k", "k_a", "r_k", "ln_x_w", "ln_x_b", "v_mix0", "lb_logits",
          "g_norm_w", "ln_w", "ln_b", "w_decay_up", "a_up", "v_mix_down", "v_mix_up")
_NAMES = ("w_in", "shift_mu", "w_decay0", "w_decay_up", "a0", "a_up", "k_k", "k_a", "r_k", "ln_x_w", "ln_x_b",
          "v_mix0", "v_mix_down", "v_mix_up", "lb_logits", "g_norm_w", "w_out", "ln_w", "ln_b")


def _pad_rows(a, rows, at_end):
    z = jnp.zeros((rows - a.shape[0], a.shape[1]), a.dtype)
    return jnp.concatenate([a, z] if at_end else [z, a], axis=0)


def kernel(x, w_in, shift_mu, w_decay0, w_decay_up, a0, a_up, k_k, k_a, r_k, ln_x_w, ln_x_b, v_mix0, v_mix_down, v_mix_up, lb_logits, g_norm_w, w_out, ln_w, ln_b, loss_target, m_w_in, m_shift_mu, m_w_decay0, m_w_decay_up, m_a0, m_a_up, m_k_k, m_k_a, m_r_k, m_ln_x_w, m_ln_x_b, m_v_mix0, m_v_mix_down, m_v_mix_up, m_lb_logits, m_g_norm_w, m_w_out, m_ln_w, m_ln_b, v_w_in, v_shift_mu, v_w_decay0, v_w_decay_up, v_a0, v_a_up, v_k_k, v_k_a, v_r_k, v_ln_x_w, v_ln_x_b, v_v_mix0, v_v_mix_down, v_v_mix_up, v_lb_logits, v_g_norm_w, v_w_out, v_ln_w, v_ln_b):
    weights = dict(w_in=w_in, shift_mu=shift_mu, w_decay0=w_decay0, w_decay_up=w_decay_up, a0=a0, a_up=a_up, k_k=k_k,
                   k_a=k_a, r_k=r_k, ln_x_w=ln_x_w, ln_x_b=ln_x_b, v_mix0=v_mix0, v_mix_down=v_mix_down,
                   v_mix_up=v_mix_up, lb_logits=lb_logits, g_norm_w=g_norm_w, w_out=w_out, ln_w=ln_w, ln_b=ln_b)
    mom1 = dict(w_in=m_w_in, shift_mu=m_shift_mu, w_decay0=m_w_decay0, w_decay_up=m_w_decay_up, a0=m_a0, a_up=m_a_up,
                k_k=m_k_k, k_a=m_k_a, r_k=m_r_k, ln_x_w=m_ln_x_w, ln_x_b=m_ln_x_b, v_mix0=m_v_mix0,
                v_mix_down=m_v_mix_down, v_mix_up=m_v_mix_up, lb_logits=m_lb_logits, g_norm_w=m_g_norm_w,
                w_out=m_w_out, ln_w=m_ln_w, ln_b=m_ln_b)
    mom2 = dict(w_in=v_w_in, shift_mu=v_shift_mu, w_decay0=v_w_decay0, w_decay_up=v_w_decay_up, a0=v_a0, a_up=v_a_up,
                k_k=v_k_k, k_a=v_k_a, r_k=v_r_k, ln_x_w=v_ln_x_w, ln_x_b=v_ln_x_b, v_mix0=v_v_mix0,
                v_mix_down=v_v_mix_down, v_mix_up=v_v_mix_up, lb_logits=v_lb_logits, g_norm_w=v_g_norm_w,
                w_out=v_w_out, ln_w=v_ln_w, ln_b=v_ln_b)
    assert x.shape[0] == 1 and w_in.shape[0] == DEPTH
    t, d = x.shape[1], x.shape[2]
    dr = w_decay0.shape[1]
    dh = g_norm_w.shape[1]
    rank_w, rank_a, rank_v = w_decay_up.shape[1], a_up.shape[1], v_mix_up.shape[1]
    rwc = 4 * dr + rank_w + rank_a
    assert rank_w + rank_a == LANES and rank_v <= LANES and dr + dh == d
    assert t % CHUNK == 0 and dr % LANES == 0 and dh % LANES == 0 and shift_mu.shape[1] == rwc
    n_pair = dr // LANES
    me = _index(_position())

    win_t = [_all_gather_rows(w_in[l].T.astype(bf16), f"ag_w_in_{l}") for l in range(DEPTH)]
    wout = [_all_gather_rows(w_out[l].astype(bf16), f"ag_w_out_{l}") for l in range(DEPTH)]
    shard = dr // N_DEV
    pack = jnp.concatenate([w_decay_up[0], w_decay_up[1], a_up[0], a_up[1], v_mix_up[0], v_mix_down[0].T], axis=0)
    pack = _all_gather_rows(pack, "ag_small")
    pack = jnp.transpose(pack.reshape(N_DEV, -1, shard), (1, 0, 2)).reshape(-1, dr)
    offs = [0, rank_w, 2 * rank_w, 2 * rank_w + rank_a, 2 * rank_w + 2 * rank_a, 2 * rank_w + 2 * rank_a + rank_v,
            2 * rank_w + 2 * rank_a + 2 * rank_v]
    wdu_f = [pack[offs[0]:offs[1]], pack[offs[1]:offs[2]]]
    aup_f = [pack[offs[2]:offs[3]], pack[offs[3]:offs[4]]]
    vup_f = pack[offs[4]:offs[5]]
    vdown_f = pack[offs[5]:offs[6]].T

    def rwkv_params(l):
        prm = [shift_mu[l:l + 1], w_decay0[l:l + 1], a0[l:l + 1], _pad_rows(wdu_f[l], LANES, True),
               _pad_rows(aup_f[l], LANES, False)]
        if l == 1:
            prm += [v_mix0[0:1], _pad_rows(vdown_f.T, LANES, True).T, _pad_rows(vup_f, LANES, True)]
        rows = jnp.stack([k_k[l], k_a[l], r_k[l], ln_x_w[l], ln_x_b[l]] + [jnp.zeros((dr,), f32)] * 3, axis=0)
        pp = jnp.transpose(rows.reshape(8, n_pair, LANES), (1, 0, 2))
        return tuple(prm), pp

    h = x[0]
    tgt = loss_target[0]
    saved = []
    vfirst = None
    for l in range(DEPTH):
        prm, pp = rwkv_params(l)
        proj = _matmul(h, win_t[l], "nt", f"mm_proj_{l}")
        if l == 0:
            cat, vfirst, mck = _rwkv_fwd(False, proj, None, prm, pp, d)
        else:
            cat, mck = _rwkv_fwd(True, proj, vfirst, prm, pp, d)
        cat, sck = _hgrn_fwd(l == 1, proj, lb_logits, g_norm_w[l:l + 1], cat, rwc)
        y = _matmul(cat, wout[l], "nn", f"mm_out_{l}")
        saved.append((h, proj, prm, pp, mck, sck, cat, y))
        if l < DEPTH - 1:
            h = _ln_fwd(h, y, ln_w[l:l + 1], ln_b[l:l + 1])
        else:
            dh_out, loss_part = _ln_loss(h, y, ln_w[l:l + 1], ln_b[l:l + 1], tgt)
    loss = lax.psum(loss_part[0, 0], ("x", "y", "c"))

    grads = {}
    big = {}
    dvfirst = None
    d_lbl = None
    for l in reversed(range(DEPTH)):
        h_l, proj, prm, pp, mck, sck, cat, y = saved[l]
        dy, g_ln_w, g_ln_b = _ln_bwd(h_l, y, ln_w[l:l + 1], ln_b[l:l + 1], dh_out)
        dcat = _matmul(dy, wout[l], "nt", f"mm_dcat_{l}")
        big[("w_out", l)] = _matmul(cat, dy, "tn", f"mm_dwout_{l}")
        if l == 1:
            outs = _rwkv_bwd(True, proj, vfirst, prm, pp, mck, dcat, None)
            dproj_r, dvfirst = outs[0], outs[1]
            dprm, dpp = outs[2:-1], outs[-1]
        else:
            outs = _rwkv_bwd(False, proj, None, prm, pp, mck, dcat, dvfirst)
            dproj_r = outs[0]
            dprm, dpp = outs[1:-1], outs[-1]
        dproj_h, dlbl_l, dgnw = _hgrn_bwd(l == 1, proj, lb_logits, g_norm_w[l:l + 1], sck, dcat, rwc)
        dproj = jnp.concatenate([dproj_r] + [dproj_h[i] for i in range(4)], axis=1)
        dh_out = _matmul(dproj, win_t[l], "nn", f"mm_dh_{l}", add=dy, add_scale=ALPHA)
        big[("w_in", l)] = _matmul(dproj, h_l, "tn", f"mm_dwin_{l}")
        dpp = jnp.transpose(dpp, (1, 0, 2)).reshape(8, dr)
        grads[l] = dict(shift_mu=dprm[0][0], w_decay0=dprm[1][0], a0=dprm[2][0], w_decay_up=dprm[3][:rank_w],
                        a_up=dprm[4][rank_w:], k_k=dpp[0], k_a=dpp[1], r_k=dpp[2], ln_x_w=dpp[3], ln_x_b=dpp[4],
                        g_norm_w=dgnw[0], ln_w=g_ln_w[0], ln_b=g_ln_b[0])
        if l == 1:
            grads[l].update(v_mix0=dprm[5][0], v_mix_down=dprm[6][:, :rank_v], v_mix_up=dprm[7][:rank_v])
            d_lbl = dlbl_l
    grad_x = dh_out[None]

    def reduce_big(gfull, name):
        return _sum_slots(_exchange_blocks(gfull, "rs_" + name), "rs_sum_" + name)

    g_w_in = jnp.stack([reduce_big(big[("w_in", l)], f"w_in_{l}").T for l in range(DEPTH)])
    g_w_out = jnp.stack([reduce_big(big[("w_out", l)], f"w_out_{l}") for l in range(DEPTH)])

    def both(name):
        return jnp.stack([grads[0][name], grads[1][name]])

    small = dict(shift_mu=both("shift_mu"), w_decay0=both("w_decay0"), a0=both("a0"), k_k=both("k_k"), k_a=both("k_a"),
                 r_k=both("r_k"), ln_x_w=both("ln_x_w"), ln_x_b=both("ln_x_b"), v_mix0=grads[1]["v_mix0"][None],
                 lb_logits=d_lbl, g_norm_w=both("g_norm_w"), ln_w=both("ln_w"), ln_b=both("ln_b"),
                 w_decay_up=both("w_decay_up"), a_up=both("a_up"), v_mix_down=grads[1]["v_mix_down"][None],
                 v_mix_up=grads[1]["v_mix_up"][None])
    flat = jnp.concatenate([small[nm].reshape(-1) for nm in _SMALL])
    n_flat = flat.shape[0]
    rows = -(-n_flat // (8 * LANES)) * 8
    flat = jnp.concatenate([flat, jnp.zeros((rows * LANES - n_flat,), f32)]).reshape(rows, LANES)
    total = _sum_slots(_all_gather_rows(flat, "ag_small_grads").reshape(N_DEV, rows, LANES), "sum_small_grads").reshape(-1)
    gsm = {}
    off = 0
    for nm in _SMALL:
        size = small[nm].size
        gsm[nm] = total[off:off + size].reshape(small[nm].shape)
        off += size
    gsm["w_decay_up"] = lax.dynamic_slice_in_dim(gsm["w_decay_up"], me * shard, shard, axis=2)
    gsm["a_up"] = lax.dynamic_slice_in_dim(gsm["a_up"], me * shard, shard, axis=2)
    gsm["v_mix_up"] = lax.dynamic_slice_in_dim(gsm["v_mix_up"], me * shard, shard, axis=2)
    gsm["v_mix_down"] = lax.dynamic_slice_in_dim(gsm["v_mix_down"], me * shard, shard, axis=1)
    gsm["w_in"] = g_w_in
    gsm["w_out"] = g_w_out

    deltas, new_m, new_v = {}, {}, {}
    for nm in _NAMES:
        deltas[nm], new_m[nm], new_v[nm] = _adamw(weights[nm], gsm[nm], mom1[nm], mom2[nm], "adamw_" + nm)
    return (loss, grad_x, *[gsm[nm] for nm in _NAMES], *[deltas[nm] for nm in _NAMES],
            *[new_m[nm] for nm in _NAMES], *[new_v[nm] for nm in _NAMES])
```

```python
import functools

import jax
import jax.numpy as jnp
from jax import lax
from jax.experimental import pallas as pl
from jax.experimental.pallas import tpu as pltpu

f32 = jnp.float32
bf16 = jnp.bfloat16

N_DEV = 8
CHUNK = 64
LANES = 128
RWKV_HEAD = 64
DEPTH = 2
ALPHA = (2 * DEPTH) ** 0.25
LN_EPS = 1e-5
GN_EPS = 64e-5
RMS_EPS = 1e-5
LB_FLOOR = 1e-30
ADAM_LR, ADAM_B1, ADAM_B2, ADAM_EPS, ADAM_WD, ADAM_STEP = 0.001, 0.9, 0.999, 1e-08, 0.01, 10
MESH = pl.DeviceIdType.MESH


def _iota(shape, d):
    return lax.broadcasted_iota(jnp.int32, shape, d)


_DIMS = {"nn": (((1,), (0,)), ((), ())), "nt": (((1,), (1,)), ((), ())), "tn": (((0,), (0,)), ((), ()))}


def _mxu(a, b, mode):
    return lax.dot_general(a, b, _DIMS[mode], preferred_element_type=f32)


def _split(x):
    hi = x.astype(bf16)
    return hi, (x - hi.astype(f32)).astype(bf16)


def _mm3_impl(a, b, mode):
    ah, al = _split(a)
    bh, bl = _split(b)
    return _mxu(ah, bh, mode) + (_mxu(ah, bl, mode) + _mxu(al, bh, mode))


@functools.partial(jax.custom_vjp, nondiff_argnums=(2,))
def _mm3(a, b, mode):
    return _mm3_impl(a, b, mode)


def _mm3_fwd(a, b, mode):
    return _mm3_impl(a, b, mode), (a, b)


def _mm3_bwd(mode, res, g):
    a, b = res
    if mode == "nn":
        return _mm3_impl(g, b, "nt"), _mm3_impl(a, g, "tn")
    if mode == "nt":
        return _mm3_impl(g, b, "nn"), _mm3_impl(g, a, "tn")
    return _mm3_impl(b, g, "nt"), _mm3_impl(a, g, "nn")


_mm3.defvjp(_mm3_fwd, _mm3_bwd)


def _const_impl(cm, x, mode):
    hi, lo = _split(x)
    if mode == "r":
        return _mxu(hi, cm, "nn") + _mxu(lo, cm, "nn")
    if mode == "rt":
        return _mxu(hi, cm, "nt") + _mxu(lo, cm, "nt")
    return _mxu(cm, hi, mode) + _mxu(cm, lo, mode)


@jax.custom_vjp
def _const_left(cm, x):
    return _const_impl(cm, x, "nn")


_const_left.defvjp(lambda cm, x: (_const_impl(cm, x, "nn"), cm),
                   lambda cm, g: (jnp.zeros_like(cm), _const_impl(cm, g, "tn")))


@jax.custom_vjp
def _const_right(x, cm):
    return _const_impl(cm, x, "r")


_const_right.defvjp(lambda x, cm: (_const_impl(cm, x, "r"), cm),
                    lambda cm, g: (_const_impl(cm, g, "rt"), jnp.zeros_like(cm)))


def _tri_inv(a):
    n = a.shape[0]
    tm = (_iota((n, n), 0) == _iota((n, n), 1)).astype(f32) + a
    ak = a
    for _ in range(5):
        ak = _mm3_impl(ak, ak, "nn")
        tm = tm + _mm3_impl(tm, ak, "nn")
    return tm


@jax.custom_vjp
def _tri_solve(a, x):
    return _mm3_impl(_tri_inv(a), x, "nn")


def _tri_solve_fwd(a, x):
    tm = _tri_inv(a)
    u = _mm3_impl(tm, x, "nn")
    return u, (tm, u)


def _tri_solve_bwd(res, du):
    tm, u = res
    dx = _mm3_impl(tm, du, "tn")
    return _mm3_impl(dx, u, "nt"), dx


_tri_solve.defvjp(_tri_solve_fwd, _tri_solve_bwd)


def _col_of_row(row_vec):
    n = row_vec.shape[1]
    eye = _iota((n, n), 0) == _iota((n, n), 1)
    return jnp.sum(jnp.where(eye, jnp.broadcast_to(row_vec, (n, n)), 0.0), axis=1, keepdims=True)


def _softplus(x):
    return jnp.maximum(x, 0.0) + jnp.log1p(jnp.exp(-jnp.abs(x)))


def _log_sigmoid(x):
    return -_softplus(-x)


def _logaddexp(a, b):
    return jnp.maximum(a, b) + jnp.log1p(jnp.exp(-jnp.abs(a - b)))


def _silu(x):
    return x * jax.nn.sigmoid(x)


def _tril(c, strict):
    r, s = _iota((c, c), 0), _iota((c, c), 1)
    return (r > s) if strict else (r >= s)


def _last_row(a):
    c = a.shape[0]
    return jnp.sum(jnp.where(_iota(a.shape, 0) == c - 1, a, 0.0), axis=0, keepdims=True)


def _rwkv_pre(layer1, prm, y, prev, vf):
    c = y.shape[0]
    if layer1:
        mu, w0, a0, wup, aup, v0, vdown, vup = prm
    else:
        mu, w0, a0, wup, aup = prm
    dr = w0.shape[1]
    shift = (_iota((c, c), 0) == _iota((c, c), 1) + 1).astype(bf16)
    y_prev = _const_left(shift, y) + jnp.where(_iota((c, 1), 0) == 0, prev, 0.0)
    rw = y + mu * (y_prev - y)
    r, k, v, z = (rw[:, i * dr:(i + 1) * dr] for i in range(4))
    wdad = rw[:, 4 * dr:4 * dr + LANES]
    w_raw = w0 + _mm3(jnp.tanh(wdad), wup, "nn")
    lw = -jnp.exp(-_softplus(-w_raw) - 0.5)
    asig = jax.nn.sigmoid(a0 + _mm3(wdad, aup, "nn"))
    if layer1:
        v = v + (vf - v) * jax.nn.sigmoid(v0 + _mm3(_mm3(v, vdown, "nn"), vup, "nn"))
    return r, k, v, z, lw, asig


def _rwkv_pair(pp, m0, xs):
    kkw, kaw, rkw, gnw, gnb = pp
    r, k, v, z, lw, asig = xs
    c = r.shape[0]
    n2 = 2 * c
    lane = _iota((1, LANES), 1)
    mh0, mh1 = (lane < RWKV_HEAD).astype(f32), (lane >= RWKV_HEAD).astype(f32)
    same_head = _iota((LANES, LANES), 0) // RWKV_HEAD == _iota((LANES, LANES), 1) // RWKV_HEAD
    g = same_head.astype(bf16)

    def seg(x):
        return _const_right(x, g)

    def stack(x):
        return jnp.concatenate([x * mh0, x * mh1], axis=0)

    kk = k * kkw
    kk = kk / jnp.maximum(jnp.sqrt(seg(kk * kk)), 1e-12)
    k2 = k * (1.0 + (asig - 1.0) * kaw)
    a = -kk
    b = kk * asig
    cum = _const_left(_tril(c, False).astype(bf16), lw)
    at = stack(a * jnp.exp(cum - lw))
    rt = stack(r * jnp.exp(cum))
    en = jnp.exp(-cum)
    sc = _mm3(jnp.concatenate([at, rt], axis=0), jnp.concatenate([stack(b * en), stack(k2 * en)], axis=0), "nt")
    row, col = _iota((n2, n2), 0), _iota((n2, n2), 1)
    same = row // c == col // c
    strict = same & (row % c > col % c)
    incl = same & (row % c >= col % c)
    aab = jnp.where(strict, sc[:n2, :n2], 0.0)
    aak = jnp.where(strict, sc[:n2, n2:], 0.0)
    arb = jnp.where(incl, sc[n2:, :n2], 0.0)
    ark = jnp.where(incl, sc[n2:, n2:], 0.0)
    vv = jnp.concatenate([v, v], axis=0)
    mask_st = jnp.concatenate([jnp.broadcast_to(mh0, (c, LANES)), jnp.broadcast_to(mh1, (c, LANES))], axis=0)
    x_st = _mm3(jnp.concatenate([at, aak], axis=1), jnp.concatenate([m0, vv], axis=0), "nn")
    u_st = _tri_solve(aab, x_st) * mask_st
    o_st = _mm3(jnp.concatenate([rt, arb, ark], axis=1), jnp.concatenate([m0, u_st, vv], axis=0), "nn") * mask_st
    u = u_st[:c] + u_st[c:]
    o = o_st[:c] + o_st[c:]
    cum_last = _last_row(cum)
    dec_end = jnp.exp(cum_last - cum)
    m_new = _col_of_row(jnp.exp(cum_last)) * m0 + _mm3(
        jnp.concatenate([b * dec_end, k2 * dec_end], axis=0), jnp.concatenate([u, v], axis=0), "tn") * same_head.astype(f32)
    mean = seg(o) * (1.0 / RWKV_HEAD)
    d = o - mean
    var = seg(d * d) * (1.0 / RWKV_HEAD)
    on = d * lax.rsqrt(var + GN_EPS) * gnw + gnb
    bonus = seg(r * k2 * rkw) * v
    return (on + bonus) * _silu(z), m_new


def _split_lanes(a, n):
    return [a[:, i * LANES:(i + 1) * LANES] for i in range(n)]


def _rwkv_specs(layer1, t, dr, rwc, n_pair, rev):
    nc = t // CHUNK

    def cidx(c):
        return (nc - 1 - c) if rev else c

    full = lambda shape: pl.BlockSpec(shape, lambda c, p: tuple(0 for _ in shape))
    specs = [
        pl.BlockSpec((CHUNK, rwc), lambda c, p: (cidx(c), 0)),
        pl.BlockSpec((8, rwc), lambda c, p: (jnp.maximum(cidx(c) * (CHUNK // 8) - 1, 0), 0)),
    ]
    if layer1:
        specs.append(pl.BlockSpec((CHUNK, dr), lambda c, p: (cidx(c), 0)))
    prm_shapes = [(1, rwc), (1, dr), (1, dr), (LANES, dr), (LANES, dr)]
    if layer1:
        prm_shapes += [(1, dr), (dr, LANES), (LANES, dr)]
    specs += [full(s) for s in prm_shapes]
    specs.append(pl.BlockSpec((1, 8, LANES), lambda c, p: (p, 0, 0)))
    return specs, prm_shapes, cidx, full


def _rwkv_fwd(layer1, proj, vf, prm, pp, cat_width):
    t = proj.shape[0]
    dr = prm[1].shape[1]
    rwc = prm[0].shape[1]
    n_pair = dr // LANES
    nc = t // CHUNK
    n_prm = len(prm)
    specs, _, _, _ = _rwkv_specs(layer1, t, dr, rwc, n_pair, False)

    def body(*refs):
        y_ref, prev_ref = refs[0], refs[1]
        i = 2
        vf_ref = None
        if layer1:
            vf_ref = refs[i]
            i += 1
        prm_refs = refs[i:i + n_prm]
        i += n_prm
        pp_ref = refs[i]
        i += 1
        cat_ref = refs[i]
        i += 1
        vout_ref = None
        if not layer1:
            vout_ref = refs[i]
            i += 1
        mck_ref, x_s, m_s = refs[i], refs[i + 1], refs[i + 2]
        c, p = pl.program_id(0), pl.program_id(1)

        @pl.when((c == 0) & (p == 0))
        def _():
            m_s[...] = jnp.zeros_like(m_s)

        @pl.when(p == 0)
        def _():
            prev = prev_ref[pl.ds(7, 1), :] * (c != 0).astype(f32)
            xs = _rwkv_pre(layer1, tuple(r[...] for r in prm_refs), y_ref[...], prev,
                           vf_ref[...] if layer1 else None)
            for q, a in enumerate(xs):
                for j, piece in enumerate(_split_lanes(a, n_pair)):
                    x_s[q * n_pair + j] = piece
            if not layer1:
                vout_ref[...] = xs[2]

        m0 = m_s[p]
        mck_ref[0, 0] = m0
        ppv = tuple(pp_ref[0, pl.ds(q, 1), :] for q in range(5))
        og, m_new = _rwkv_pair(ppv, m0, tuple(x_s[q * n_pair + p] for q in range(6)))
        cat_ref[...] = og
        m_s[p] = m_new

    out_shape = [jax.ShapeDtypeStruct((t, cat_width), f32)]
    out_specs = [pl.BlockSpec((CHUNK, LANES), lambda c, p: (c, p))]
    if not layer1:
        out_shape.append(jax.ShapeDtypeStruct((t, dr), f32))
        out_specs.append(pl.BlockSpec((CHUNK, dr), lambda c, p: (c, 0)))
    out_shape.append(jax.ShapeDtypeStruct((nc, n_pair, LANES, LANES), f32))
    out_specs.append(pl.BlockSpec((1, 1, LANES, LANES), lambda c, p: (c, p, 0, 0)))
    args = [proj, proj] + ([vf] if layer1 else []) + list(prm) + [pp]
    return pl.pallas_call(
        body, grid=(nc, n_pair), in_specs=specs, out_specs=out_specs, out_shape=out_shape,
        scratch_shapes=[pltpu.VMEM((6 * n_pair, CHUNK, LANES), f32), pltpu.VMEM((n_pair, LANES, LANES), f32)],
        compiler_params=pltpu.CompilerParams(dimension_semantics=("arbitrary", "arbitrary")),
        name=f"rwkv_fwd_l{int(layer1)}",
    )(*args)


def _rwkv_bwd(layer1, proj, vf, prm, pp, mck, dcat, dvout):
    t = proj.shape[0]
    dr = prm[1].shape[1]
    rwc = prm[0].shape[1]
    n_pair = dr // LANES
    nc = t // CHUNK
    n_prm = len(prm)
    specs, prm_shapes, cidx, full = _rwkv_specs(layer1, t, dr, rwc, n_pair, True)
    specs.append(pl.BlockSpec((1, 1, LANES, LANES), lambda c, p: (cidx(c), p, 0, 0)))
    specs.append(pl.BlockSpec((CHUNK, LANES), lambda c, p: (cidx(c), p)))
    if not layer1:
        specs.append(pl.BlockSpec((CHUNK, dr), lambda c, p: (cidx(c), 0)))

    def body(*refs):
        y_ref, prev_ref = refs[0], refs[1]
        i = 2
        vf_ref = None
        if layer1:
            vf_ref = refs[i]
            i += 1
        prm_refs = refs[i:i + n_prm]
        i += n_prm
        pp_ref, mck_ref, dog_ref = refs[i], refs[i + 1], refs[i + 2]
        i += 3
        dvout_ref = None
        if not layer1:
            dvout_ref = refs[i]
            i += 1
        dy_ref = refs[i]
        i += 1
        dvf_ref = None
        if layer1:
            dvf_ref = refs[i]
            i += 1
        dprm_refs = refs[i:i + n_prm]
        i += n_prm
        dpp_ref = refs[i]
        x_s, dx_s, dm_s, dprev_s = refs[i + 1:i + 5]
        c, p = pl.program_id(0), pl.program_id(1)
        cr = nc - 1 - c

        def prev_row():
            return prev_ref[pl.ds(7, 1), :] * (cr != 0).astype(f32)

        @pl.when((c == 0) & (p == 0))
        def _():
            dm_s[...] = jnp.zeros_like(dm_s)
            dprev_s[...] = jnp.zeros_like(dprev_s)
            dpp_ref[...] = jnp.zeros_like(dpp_ref)
            for r in dprm_refs:
                r[...] = jnp.zeros_like(r)

        @pl.when(p == 0)
        def _():
            xs = _rwkv_pre(layer1, tuple(r[...] for r in prm_refs), y_ref[...], prev_row(),
                           vf_ref[...] if layer1 else None)
            for q, a in enumerate(xs):
                for j, piece in enumerate(_split_lanes(a, n_pair)):
                    x_s[q * n_pair + j] = piece

        ppv = tuple(pp_ref[0, pl.ds(q, 1), :] for q in range(5))
        xs_p = tuple(x_s[q * n_pair + p] for q in range(6))
        _, vjp_pair = jax.vjp(_rwkv_pair, ppv, mck_ref[0, 0], xs_p)
        dppv, dm0, dxs = vjp_pair((dog_ref[...], dm_s[p]))
        dm_s[p] = dm0
        for q in range(6):
            dx_s[q * n_pair + p] = dxs[q]
        for q in range(5):
            dpp_ref[p, pl.ds(q, 1), :] += dppv[q]

        @pl.when(p == n_pair - 1)
        def _():
            dxs_full = [jnp.concatenate([dx_s[q * n_pair + j] for j in range(n_pair)], axis=1) for q in range(6)]
            if not layer1:
                dxs_full[2] = dxs_full[2] + dvout_ref[...]
            prm_v = tuple(r[...] for r in prm_refs)
            if layer1:
                _, vjp_pre = jax.vjp(functools.partial(_rwkv_pre, True), prm_v, y_ref[...], prev_row(), vf_ref[...])
                dprm, dy, dprev, dvf = vjp_pre(tuple(dxs_full))
                dvf_ref[...] = dvf
            else:
                _, vjp_pre = jax.vjp(lambda a, b, d: _rwkv_pre(False, a, b, d, None), prm_v, y_ref[...], prev_row())
                dprm, dy, dprev = vjp_pre(tuple(dxs_full))
            dy_ref[...] = dy + jnp.where(_iota((CHUNK, 1), 0) == CHUNK - 1, dprev_s[...], 0.0)
            dprev_s[...] = dprev
            for r, gval in zip(dprm_refs, dprm):
                r[...] += gval

    out_shape = [jax.ShapeDtypeStruct((t, rwc), f32)]
    out_specs = [pl.BlockSpec((CHUNK, rwc), lambda c, p: (cidx(c), 0))]
    if layer1:
        out_shape.append(jax.ShapeDtypeStruct((t, dr), f32))
        out_specs.append(pl.BlockSpec((CHUNK, dr), lambda c, p: (cidx(c), 0)))
    out_shape += [jax.ShapeDtypeStruct(s, f32) for s in prm_shapes]
    out_specs += [full(s) for s in prm_shapes]
    out_shape.append(jax.ShapeDtypeStruct((n_pair, 8, LANES), f32))
    out_specs.append(full((n_pair, 8, LANES)))
    args = [proj, proj] + ([vf] if layer1 else []) + list(prm) + [pp, mck, dcat] + ([] if layer1 else [dvout])
    return pl.pallas_call(
        body, grid=(nc, n_pair), in_specs=specs, out_specs=out_specs, out_shape=out_shape,
        scratch_shapes=[pltpu.VMEM((6 * n_pair, CHUNK, LANES), f32), pltpu.VMEM((6 * n_pair, CHUNK, LANES), f32),
                        pltpu.VMEM((n_pair, LANES, LANES), f32), pltpu.VMEM((1, rwc), f32)],
        compiler_params=pltpu.CompilerParams(dimension_semantics=("arbitrary", "arbitrary")),
        name=f"rwkv_bwd_l{int(layer1)}",
    )(*args)


def _hgrn_chunk(layer1, lbl, gnw, s0, q_raw, f_raw, i_in, z):
    c = q_raw.shape[0]
    q = _silu(q_raw)
    ls = _log_sigmoid(f_raw)
    if layer1:
        l0, l1 = lbl[0:1, :], lbl[1:2, :]
        mx = jnp.maximum(l0, l1)
        e0, e1 = jnp.exp(l0 - mx), jnp.exp(l1 - mx)
        sm0, sm1 = e0 / (e0 + e1), e1 / (e0 + e1)
        lb = (sm0 + sm1) - sm0
        log_f = _logaddexp(jnp.log(jnp.maximum(lb, LB_FLOOR)), jnp.log1p(-lb) + ls)
        k = (1.0 - lb) * jax.nn.sigmoid(-f_raw)
    else:
        log_f = _logaddexp(jnp.full_like(ls, jnp.log(jnp.float32(LB_FLOOR))), ls)
        k = jax.nn.sigmoid(-f_raw)
    row, col = _iota((c, c), 0), _iota((c, c), 1)
    trow = _iota((c, 1), 0)
    halves = []
    half = c // 2
    while half >= 1:
        halves.append(half)
        half //= 2
    cmat = jnp.concatenate([(col <= row).astype(f32)]
                           + [(col <= (row // (2 * hf)) * (2 * hf) + hf - 1).astype(f32) for hf in halves], axis=0)
    ball = _const_left(cmat.astype(bf16), log_f)
    b = ball[:c]
    att = jnp.zeros((c, c), f32)
    for lvl, hf in enumerate(halves):
        blk = 2 * hf
        bref = ball[(lvl + 1) * c:(lvl + 2) * c]
        upper = (trow % blk) >= hf
        qh = q * jnp.exp(jnp.where(upper, b - bref, 0.0)) * upper.astype(f32)
        kh = k * jnp.exp(jnp.where(upper, 0.0, bref - b)) * (1.0 - upper.astype(f32))
        att = att + jnp.where(row // blk == col // blk, _mm3(qh, kh, "nt"), 0.0)
    o = (_mm3(jnp.concatenate([q * jnp.exp(b), att], axis=1), jnp.concatenate([s0, i_in], axis=0), "nn")
         + jnp.sum(q * k, axis=-1, keepdims=True) * i_in)
    b_last = _last_row(b)
    s_new = _col_of_row(jnp.exp(b_last)) * s0 + _mm3(k * jnp.exp(b_last - b), i_in, "tn")
    o = o * lax.rsqrt(jnp.mean(o * o, axis=-1, keepdims=True) + RMS_EPS)
    return o * gnw * _silu(z), s_new


def _hgrn_in_specs(t, dh, col0, rev):
    nc = t // CHUNK
    nh = dh // LANES

    def cidx(c):
        return (nc - 1 - c) if rev else c

    specs = [pl.BlockSpec((CHUNK, LANES), functools.partial(lambda g, h, c: (cidx(c), col0 + g * nh + h), g))
             for g in range(4)]
    specs.append(pl.BlockSpec((2, LANES), lambda h, c: (0, h)))
    specs.append(pl.BlockSpec((1, LANES), lambda h, c: (0, h)))
    return specs, cidx


def _hgrn_fwd(layer1, proj, lbl, gnw, cat, rwc):
    t, d = cat.shape
    dh = gnw.shape[1]
    nh = dh // LANES
    nc = t // CHUNK
    col0 = rwc // LANES
    specs, _ = _hgrn_in_specs(t, dh, col0, False)
    specs.append(pl.BlockSpec(memory_space=pl.ANY))
    cat_col0 = (d - dh) // LANES

    def body(q_ref, f_ref, i_ref, z_ref, lbl_ref, gnw_ref, cat_in, cat_ref, sck_ref, s_s):
        del cat_in
        c = pl.program_id(1)

        @pl.when(c == 0)
        def _():
            s_s[...] = jnp.zeros_like(s_s)

        s0 = s_s[...]
        sck_ref[0, 0] = s0
        out, s_new = _hgrn_chunk(layer1, lbl_ref[...], gnw_ref[...], s0, q_ref[...], f_ref[...], i_ref[...], z_ref[...])
        cat_ref[...] = out
        s_s[...] = s_new

    return pl.pallas_call(
        body, grid=(nh, nc), in_specs=specs,
        out_specs=[pl.BlockSpec((CHUNK, LANES), lambda h, c: (c, cat_col0 + h)),
                   pl.BlockSpec((1, 1, LANES, LANES), lambda h, c: (h, c, 0, 0))],
        out_shape=[jax.ShapeDtypeStruct((t, d), f32), jax.ShapeDtypeStruct((nh, nc, LANES, LANES), f32)],
        scratch_shapes=[pltpu.VMEM((LANES, LANES), f32)],
        input_output_aliases={6: 0},
        compiler_params=pltpu.CompilerParams(dimension_semantics=("arbitrary", "arbitrary")),
        name=f"hgrn_fwd_l{int(layer1)}",
    )(proj, proj, proj, proj, lbl, gnw, cat)


def _hgrn_bwd(layer1, proj, lbl, gnw, sck, dcat, rwc):
    t, d = dcat.shape
    dh = gnw.shape[1]
    nh = dh // LANES
    nc = t // CHUNK
    col0 = rwc // LANES
    specs, cidx = _hgrn_in_specs(t, dh, col0, True)
    cat_col0 = (d - dh) // LANES
    specs.append(pl.BlockSpec((1, 1, LANES, LANES), lambda h, c: (h, cidx(c), 0, 0)))
    specs.append(pl.BlockSpec((CHUNK, LANES), lambda h, c: (cidx(c), cat_col0 + h)))

    def body(q_ref, f_ref, i_ref, z_ref, lbl_ref, gnw_ref, sck_ref, do_ref, dp_ref, dlbl_ref, dgnw_ref, ds_s):
        c = pl.program_id(1)

        @pl.when(c == 0)
        def _():
            ds_s[...] = jnp.zeros_like(ds_s)
            dlbl_ref[...] = jnp.zeros_like(dlbl_ref)
            dgnw_ref[...] = jnp.zeros_like(dgnw_ref)

        _, vjp = jax.vjp(functools.partial(_hgrn_chunk, layer1), lbl_ref[...], gnw_ref[...], sck_ref[0, 0],
                         q_ref[...], f_ref[...], i_ref[...], z_ref[...])
        dlbl, dgnw, ds0, dq, df, di, dz = vjp((do_ref[...], ds_s[...]))
        ds_s[...] = ds0
        dlbl_ref[...] += dlbl
        dgnw_ref[...] += dgnw
        dp_ref[0] = dq
        dp_ref[1] = df
        dp_ref[2] = di
        dp_ref[3] = dz

    return pl.pallas_call(
        body, grid=(nh, nc), in_specs=specs,
        out_specs=[pl.BlockSpec((4, CHUNK, LANES), lambda h, c: (0, cidx(c), h)),
                   pl.BlockSpec((2, LANES), lambda h, c: (0, h)),
                   pl.BlockSpec((1, LANES), lambda h, c: (0, h))],
        out_shape=[jax.ShapeDtypeStruct((4, t, dh), f32), jax.ShapeDtypeStruct((2, dh), f32),
                   jax.ShapeDtypeStruct((1, dh), f32)],
        scratch_shapes=[pltpu.VMEM((LANES, LANES), f32)],
        compiler_params=pltpu.CompilerParams(dimension_semantics=("arbitrary", "arbitrary")),
        name=f"hgrn_bwd_l{int(layer1)}",
    )(proj, proj, proj, proj, lbl, gnw, sck, dcat)


def _ln(h, y, w, b):
    u = ALPHA * h + y
    mu = jnp.mean(u, axis=-1, keepdims=True)
    var = jnp.mean(jnp.square(u - mu), axis=-1, keepdims=True)
    return (u - mu) * lax.rsqrt(var + LN_EPS) * w + b


def _row_tile(t):
    return 256 if t % 256 == 0 else t


def _ln_fwd(h, y, w, b):
    t, d = h.shape
    tr = _row_tile(t)

    def body(h_ref, y_ref, w_ref, b_ref, o_ref):
        o_ref[...] = _ln(h_ref[...], y_ref[...], w_ref[...], b_ref[...])

    row = pl.BlockSpec((tr, d), lambda i: (i, 0))
    vec = pl.BlockSpec((1, d), lambda i: (0, 0))
    return pl.pallas_call(body, grid=(t // tr,), in_specs=[row, row, vec, vec], out_specs=row,
                          out_shape=jax.ShapeDtypeStruct((t, d), f32), name="ln_fwd")(h, y, w, b)


def _ln_loss(h, y, w, b, tgt):
    t, d = h.shape
    tr = _row_tile(t)

    def body(h_ref, y_ref, w_ref, b_ref, t_ref, g_ref, loss_ref):
        @pl.when(pl.program_id(0) == 0)
        def _():
            loss_ref[...] = jnp.zeros_like(loss_ref)

        err = _ln(h_ref[...], y_ref[...], w_ref[...], b_ref[...]) - t_ref[...]
        g_ref[...] = err * (1.0 / d)
        loss_ref[...] += 0.5 * jnp.sum(jnp.mean(jnp.square(err), axis=-1, keepdims=True), axis=0, keepdims=True)

    row = pl.BlockSpec((tr, d), lambda i: (i, 0))
    vec = pl.BlockSpec((1, d), lambda i: (0, 0))
    return pl.pallas_call(
        body, grid=(t // tr,), in_specs=[row, row, vec, vec, row],
        out_specs=[row, pl.BlockSpec((1, LANES), lambda i: (0, 0))],
        out_shape=[jax.ShapeDtypeStruct((t, d), f32), jax.ShapeDtypeStruct((1, LANES), f32)],
        compiler_params=pltpu.CompilerParams(dimension_semantics=("arbitrary",)), name="ln_loss")(h, y, w, b, tgt)


def _ln_bwd(h, y, w, b, dout):
    t, d = h.shape
    tr = _row_tile(t)

    def body(h_ref, y_ref, w_ref, b_ref, do_ref, dy_ref, dw_ref, db_ref):
        @pl.when(pl.program_id(0) == 0)
        def _():
            dw_ref[...] = jnp.zeros_like(dw_ref)
            db_ref[...] = jnp.zeros_like(db_ref)

        _, vjp = jax.vjp(lambda yy, ww, bb: _ln(h_ref[...], yy, ww, bb), y_ref[...], w_ref[...], b_ref[...])
        dy, dw, db = vjp(do_ref[...])
        dy_ref[...] = dy
        dw_ref[...] += dw
        db_ref[...] += db

    row = pl.BlockSpec((tr, d), lambda i: (i, 0))
    vec = pl.BlockSpec((1, d), lambda i: (0, 0))
    return pl.pallas_call(
        body, grid=(t // tr,), in_specs=[row, row, vec, vec, row], out_specs=[row, vec, vec],
        out_shape=[jax.ShapeDtypeStruct((t, d), f32), jax.ShapeDtypeStruct((1, d), f32), jax.ShapeDtypeStruct((1, d), f32)],
        compiler_params=pltpu.CompilerParams(dimension_semantics=("arbitrary",)), name="ln_bwd")(h, y, w, b, dout)


def _pick(n, prefs):
    for p in prefs:
        if n % p == 0:
            return p
    return n


def _matmul(a, b, mode, name, add=None, add_scale=1.0):
    if mode == "nn":
        (m, k), n = a.shape, b.shape[1]
    elif mode == "nt":
        (m, k), n = a.shape, b.shape[0]
    else:
        (k, m), n = a.shape, b.shape[1]
    tm = _pick(m, (512, 640, 256, 128))
    tn = _pick(n, (1024, 640, 512, 256, 128))
    tk = _pick(k, (2048, 1024, 640, 512, 256, 128))
    nk = k // tk
    dims = {"nn": (((1,), (0,)), ((), ())), "nt": (((1,), (1,)), ((), ())), "tn": (((0,), (0,)), ((), ()))}[mode]

    def body(*refs):
        a_ref, b_ref = refs[0], refs[1]
        add_ref = refs[2] if add is not None else None
        o_ref, acc = refs[-2], refs[-1]
        kk = pl.program_id(2)

        @pl.when(kk == 0)
        def _():
            acc[...] = jnp.zeros_like(acc)

        acc[...] += lax.dot_general(a_ref[...].astype(bf16), b_ref[...].astype(bf16), dims, preferred_element_type=f32)

        @pl.when(kk == nk - 1)
        def _():
            res = acc[...]
            if add is not None:
                res = res + add_scale * add_ref[...]
            o_ref[...] = res

    a_spec = pl.BlockSpec((tk, tm), lambda i, j, kk: (kk, i)) if mode == "tn" else pl.BlockSpec((tm, tk), lambda i, j, kk: (i, kk))
    b_spec = pl.BlockSpec((tn, tk), lambda i, j, kk: (j, kk)) if mode == "nt" else pl.BlockSpec((tk, tn), lambda i, j, kk: (kk, j))
    o_spec = pl.BlockSpec((tm, tn), lambda i, j, kk: (i, j))
    in_specs = [a_spec, b_spec] + ([o_spec] if add is not None else [])
    args = [a, b] + ([add] if add is not None else [])
    return pl.pallas_call(
        body, grid=(m // tm, n // tn, nk), in_specs=in_specs, out_specs=o_spec,
        out_shape=jax.ShapeDtypeStruct((m, n), f32), scratch_shapes=[pltpu.VMEM((tm, tn), f32)],
        compiler_params=pltpu.CompilerParams(dimension_semantics=("parallel", "parallel", "arbitrary")),
        name=name,
    )(*args)


def _position():
    return lax.axis_index("x"), lax.axis_index("y"), lax.axis_index("c")


def _flip(pos, k):
    x, y, c = pos
    return (1 - x if k & 4 else x, 1 - y if k & 2 else y, 1 - c if k & 1 else c)


def _index(pos):
    return 4 * pos[0] + 2 * pos[1] + pos[2]


def _all_gather_rows(x, name):
    m_per, n = x.shape

    def body(x_ref, out_ref, send_sems, recv_sems, local_sem):
        me = _position()
        sibling = _flip(me, 1)
        chips = (2, 4, 6)

        def rows(pos):
            return out_ref.at[pl.ds(_index(pos) * m_per, m_per), :]

        def copy(sem, block, to, src=None):
            return pltpu.make_async_remote_copy(
                src_ref=rows(block) if src is None else src, dst_ref=rows(block),
                send_sem=send_sems.at[sem], recv_sem=recv_sems.at[sem], device_id=to, device_id_type=MESH)

        mine = pltpu.make_async_copy(x_ref, rows(me), local_sem)
        mine.start()
        first = [copy(0, me, sibling, src=x_ref)]
        first += [copy(1 + j, me, _flip(me, k), src=x_ref) for j, k in enumerate(chips)]
        for cp in first:
            cp.start()
        passed = [copy(4 + j, _flip(me, k), sibling) for j, k in enumerate(chips)]
        for j, k in enumerate(chips):
            copy(1 + j, _flip(me, k), me).wait_recv()
            passed[j].start()
        copy(0, sibling, me).wait_recv()
        for j, k in enumerate(chips):
            copy(4 + j, _flip(sibling, k), me).wait_recv()
        for cp in first + passed:
            cp.wait_send()
        mine.wait()

    return pl.pallas_call(
        body, out_shape=jax.ShapeDtypeStruct((N_DEV * m_per, n), x.dtype),
        in_specs=[pl.BlockSpec(memory_space=pl.ANY)], out_specs=pl.BlockSpec(memory_space=pl.ANY),
        scratch_shapes=[pltpu.SemaphoreType.DMA((7,)), pltpu.SemaphoreType.DMA((7,)), pltpu.SemaphoreType.DMA(())],
        name=name,
    )(x)


def _exchange_blocks(g, name):
    m_per = g.shape[0] // N_DEV
    n = g.shape[1]

    def body(g_ref, out_ref, send_sems, recv_sems, local_sem):
        me = _position()
        mine = pltpu.make_async_copy(g_ref.at[pl.ds(_index(me) * m_per, m_per), :], out_ref.at[_index(me)], local_sem)
        mine.start()

        def copy(k):
            peer = _flip(me, k)
            return pltpu.make_async_remote_copy(
                src_ref=g_ref.at[pl.ds(_index(peer) * m_per, m_per), :], dst_ref=out_ref.at[_index(me)],
                send_sem=send_sems.at[k - 1], recv_sem=recv_sems.at[k - 1], device_id=peer, device_id_type=MESH)

        def arrival(k):
            peer = _flip(me, k)
            return pltpu.make_async_remote_copy(
                src_ref=g_ref.at[pl.ds(_index(peer) * m_per, m_per), :], dst_ref=out_ref.at[_index(peer)],
                send_sem=send_sems.at[k - 1], recv_sem=recv_sems.at[k - 1], device_id=peer, device_id_type=MESH)

        sends = [copy(k) for k in range(1, N_DEV)]
        for cp in sends:
            cp.start()
        for k in range(1, N_DEV):
            arrival(k).wait_recv()
        for cp in sends:
            cp.wait_send()
        mine.wait()

    return pl.pallas_call(
        body, out_shape=jax.ShapeDtypeStruct((N_DEV, m_per, n), g.dtype),
        in_specs=[pl.BlockSpec(memory_space=pl.ANY)], out_specs=pl.BlockSpec(memory_space=pl.ANY),
        scratch_shapes=[pltpu.SemaphoreType.DMA((7,)), pltpu.SemaphoreType.DMA((7,)), pltpu.SemaphoreType.DMA(())],
        name=name,
    )(g)


def _sum_slots(parts, name):
    _, m, n = parts.shape
    tr = _pick(m, (104, 128, 64, 32, 16, 8))

    def body(p_ref, o_ref):
        acc = p_ref[0]
        for s in range(1, N_DEV):
            acc = acc + p_ref[s]
        o_ref[...] = acc

    return pl.pallas_call(
        body, grid=(m // tr,), in_specs=[pl.BlockSpec((N_DEV, tr, n), lambda i: (0, i, 0))],
        out_specs=pl.BlockSpec((tr, n), lambda i: (i, 0)), out_shape=jax.ShapeDtypeStruct((m, n), f32), name=name,
    )(parts)


def _adamw(w, g, m, v, name):
    shape = w.shape
    n = shape[-1]
    r = w.size // n
    w2, g2, m2, v2 = (a.reshape(r, n) for a in (w, g, m, v))
    tr = _pick(r, (256, 128, 64, 32, 16, 8)) if r * n > 65536 else r

    def body(w_ref, g_ref, m_ref, v_ref, d_ref, mo_ref, vo_ref):
        gg = g_ref[...]
        mm = ADAM_B1 * m_ref[...] + (1.0 - ADAM_B1) * gg
        vv = ADAM_B2 * v_ref[...] + (1.0 - ADAM_B2) * jnp.square(gg)
        m_hat = mm / (1.0 - ADAM_B1 ** ADAM_STEP)
        v_hat = vv / (1.0 - ADAM_B2 ** ADAM_STEP)
        d_ref[...] = -ADAM_LR * (m_hat / (jnp.sqrt(v_hat) + ADAM_EPS) + ADAM_WD * w_ref[...])
        mo_ref[...] = mm
        vo_ref[...] = vv

    spec = pl.BlockSpec((tr, n), lambda i: (i, 0))
    outs = pl.pallas_call(
        body, grid=(r // tr,), in_specs=[spec] * 4, out_specs=[spec] * 3,
        out_shape=[jax.ShapeDtypeStruct((r, n), f32)] * 3, name=name,
    )(w2, g2, m2, v2)
    return tuple(o.reshape(shape) for o in outs)


_SMALL = ("shift_mu", "w_decay0", "a0", "k_k", "k_a", "r_k", "ln_x_w", "ln_x_b", "v_mix0", "lb_logits",
          "g_norm_w", "ln_w", "ln_b", "w_decay_up", "a_up", "v_mix_down", "v_mix_up")
_NAMES = ("w_in", "shift_mu", "w_decay0", "w_decay_up", "a0", "a_up", "k_k", "k_a", "r_k", "ln_x_w", "ln_x_b",
          "v_mix0", "v_mix_down", "v_mix_up", "lb_logits", "g_norm_w", "w_out", "ln_w", "ln_b")


def _pad_rows(a, rows, at_end):
    z = jnp.zeros((rows - a.shape[0], a.shape[1]), a.dtype)
    return jnp.concatenate([a, z] if at_end else [z, a], axis=0)


def kernel(x, w_in, shift_mu, w_decay0, w_decay_up, a0, a_up, k_k, k_a, r_k, ln_x_w, ln_x_b, v_mix0, v_mix_down, v_mix_up, lb_logits, g_norm_w, w_out, ln_w, ln_b, loss_target, m_w_in, m_shift_mu, m_w_decay0, m_w_decay_up, m_a0, m_a_up, m_k_k, m_k_a, m_r_k, m_ln_x_w, m_ln_x_b, m_v_mix0, m_v_mix_down, m_v_mix_up, m_lb_logits, m_g_norm_w, m_w_out, m_ln_w, m_ln_b, v_w_in, v_shift_mu, v_w_decay0, v_w_decay_up, v_a0, v_a_up, v_k_k, v_k_a, v_r_k, v_ln_x_w, v_ln_x_b, v_v_mix0, v_v_mix_down, v_v_mix_up, v_lb_logits, v_g_norm_w, v_w_out, v_ln_w, v_ln_b):
    weights = dict(w_in=w_in, shift_mu=shift_mu, w_decay0=w_decay0, w_decay_up=w_decay_up, a0=a0, a_up=a_up, k_k=k_k,
                   k_a=k_a, r_k=r_k, ln_x_w=ln_x_w, ln_x_b=ln_x_b, v_mix0=v_mix0, v_mix_down=v_mix_down,
                   v_mix_up=v_mix_up, lb_logits=lb_logits, g_norm_w=g_norm_w, w_out=w_out, ln_w=ln_w, ln_b=ln_b)
    mom1 = dict(w_in=m_w_in, shift_mu=m_shift_mu, w_decay0=m_w_decay0, w_decay_up=m_w_decay_up, a0=m_a0, a_up=m_a_up,
                k_k=m_k_k, k_a=m_k_a, r_k=m_r_k, ln_x_w=m_ln_x_w, ln_x_b=m_ln_x_b, v_mix0=m_v_mix0,
                v_mix_down=m_v_mix_down, v_mix_up=m_v_mix_up, lb_logits=m_lb_logits, g_norm_w=m_g_norm_w,
                w_out=m_w_out, ln_w=m_ln_w, ln_b=m_ln_b)
    mom2 = dict(w_in=v_w_in, shift_mu=v_shift_mu, w_decay0=v_w_decay0, w_decay_up=v_w_decay_up, a0=v_a0, a_up=v_a_up,
                k_k=v_k_k, k_a=v_k_a, r_k=v_r_k, ln_x_w=v_ln_x_w, ln_x_b=v_ln_x_b, v_mix0=v_v_mix0,
                v_mix_down=v_v_mix_down, v_mix_up=v_v_mix_up, lb_logits=v_lb_logits, g_norm_w=v_g_norm_w,
                w_out=v_w_out, ln_w=v_ln_w, ln_b=v_ln_b)
    assert x.shape[0] == 1 and w_in.shape[0] == DEPTH
    t, d = x.shape[1], x.shape[2]
    dr = w_decay0.shape[1]
    dh = g_norm_w.shape[1]
    rank_w, rank_a, rank_v = w_decay_up.shape[1], a_up.shape[1], v_mix_up.shape[1]
    rwc = 4 * dr + rank_w + rank_a
    assert rank_w + rank_a == LANES and rank_v <= LANES and dr + dh == d
    assert t % CHUNK == 0 and dr % LANES == 0 and dh % LANES == 0 and shift_mu.shape[1] == rwc
    n_pair = dr // LANES
    me = _index(_position())

    win_t = [_all_gather_rows(w_in[l].T.astype(bf16), f"ag_w_in_{l}") for l in range(DEPTH)]
    wout = [_all_gather_rows(w_out[l].astype(bf16), f"ag_w_out_{l}") for l in range(DEPTH)]
    shard = dr // N_DEV
    pack = jnp.concatenate([w_decay_up[0], w_decay_up[1], a_up[0], a_up[1], v_mix_up[0], v_mix_down[0].T], axis=0)
    pack = _all_gather_rows(pack, "ag_small")
    pack = jnp.transpose(pack.reshape(N_DEV, -1, shard), (1, 0, 2)).reshape(-1, dr)
    offs = [0, rank_w, 2 * rank_w, 2 * rank_w + rank_a, 2 * rank_w + 2 * rank_a, 2 * rank_w + 2 * rank_a + rank_v,
            2 * rank_w + 2 * rank_a + 2 * rank_v]
    wdu_f = [pack[offs[0]:offs[1]], pack[offs[1]:offs[2]]]
    aup_f = [pack[offs[2]:offs[3]], pack[offs[3]:offs[4]]]
    vup_f = pack[offs[4]:offs[5]]
    vdown_f = pack[offs[5]:offs[6]].T

    def rwkv_params(l):
        prm = [shift_mu[l:l + 1], w_decay0[l:l + 1], a0[l:l + 1], _pad_rows(wdu_f[l], LANES, True),
               _pad_rows(aup_f[l], LANES, False)]
        if l == 1:
            prm += [v_mix0[0:1], _pad_rows(vdown_f.T, LANES, True).T, _pad_rows(vup_f, LANES, True)]
        rows = jnp.stack([k_k[l], k_a[l], r_k[l], ln_x_w[l], ln_x_b[l]] + [jnp.zeros((dr,), f32)] * 3, axis=0)
        pp = jnp.transpose(rows.reshape(8, n_pair, LANES), (1, 0, 2))
        return tuple(prm), pp

    h = x[0]
    tgt = loss_target[0]
    saved = []
    vfirst = None
    for l in range(DEPTH):
        prm, pp = rwkv_params(l)
        proj = _matmul(h, win_t[l], "nt", f"mm_proj_{l}")
        if l == 0:
            cat, vfirst, mck = _rwkv_fwd(False, proj, None, prm, pp, d)
        else:
            cat, mck = _rwkv_fwd(True, proj, vfirst, prm, pp, d)
        cat, sck = _hgrn_fwd(l == 1, proj, lb_logits, g_norm_w[l:l + 1], cat, rwc)
        y = _matmul(cat, wout[l], "nn", f"mm_out_{l}")
        saved.append((h, proj, prm, pp, mck, sck, cat, y))
        if l < DEPTH - 1:
            h = _ln_fwd(h, y, ln_w[l:l + 1], ln_b[l:l + 1])
        else:
            dh_out, loss_part = _ln_loss(h, y, ln_w[l:l + 1], ln_b[l:l + 1], tgt)
    loss = lax.psum(loss_part[0, 0], ("x", "y", "c"))

    grads = {}
    big = {}
    dvfirst = None
    d_lbl = None
    for l in reversed(range(DEPTH)):
        h_l, proj, prm, pp, mck, sck, cat, y = saved[l]
        dy, g_ln_w, g_ln_b = _ln_bwd(h_l, y, ln_w[l:l + 1], ln_b[l:l + 1], dh_out)
        dcat = _matmul(dy, wout[l], "nt", f"mm_dcat_{l}")
        big[("w_out", l)] = _matmul(cat, dy, "tn", f"mm_dwout_{l}")
        if l == 1:
            outs = _rwkv_bwd(True, proj, vfirst, prm, pp, mck, dcat, None)
            dproj_r, dvfirst = outs[0], outs[1]
            dprm, dpp = outs[2:-1], outs[-1]
        else:
            outs = _rwkv_bwd(False, proj, None, prm, pp, mck, dcat, dvfirst)
            dproj_r = outs[0]
            dprm, dpp = outs[1:-1], outs[-1]
        dproj_h, dlbl_l, dgnw = _hgrn_bwd(l == 1, proj, lb_logits, g_norm_w[l:l + 1], sck, dcat, rwc)
        dproj = jnp.concatenate([dproj_r] + [dproj_h[i] for i in range(4)], axis=1)
        dh_out = _matmul(dproj, win_t[l], "nn", f"mm_dh_{l}", add=dy, add_scale=ALPHA)
        big[("w_in", l)] = _matmul(dproj, h_l, "tn", f"mm_dwin_{l}")
        dpp = jnp.transpose(dpp, (1, 0, 2)).reshape(8, dr)
        grads[l] = dict(shift_mu=dprm[0][0], w_decay0=dprm[1][0], a0=dprm[2][0], w_decay_up=dprm[3][:rank_w],
                        a_up=dprm[4][rank_w:], k_k=dpp[0], k_a=dpp[1], r_k=dpp[2], ln_x_w=dpp[3], ln_x_b=dpp[4],
                        g_norm_w=dgnw[0], ln_w=g_ln_w[0], ln_b=g_ln_b[0])
        if l == 1:
            grads[l].update(v_mix0=dprm[5][0], v_mix_down=dprm[6][:, :rank_v], v_mix_up=dprm[7][:rank_v])
            d_lbl = dlbl_l
    grad_x = dh_out[None]

    def reduce_big(gfull, name):
        return _sum_slots(_exchange_blocks(gfull, "rs_" + name), "rs_sum_" + name)

    g_w_in = jnp.stack([reduce_big(big[("w_in", l)], f"w_in_{l}").T for l in range(DEPTH)])
    g_w_out = jnp.stack([reduce_big(big[("w_out", l)], f"w_out_{l}") for l in range(DEPTH)])

    def both(name):
        return jnp.stack([grads[0][name], grads[1][name]])

    small = dict(shift_mu=both("shift_mu"), w_decay0=both("w_decay0"), a0=both("a0"), k_k=both("k_k"), k_a=both("k_a"),
                 r_k=both("r_k"), ln_x_w=both("ln_x_w"), ln_x_b=both("ln_x_b"), v_mix0=grads[1]["v_mix0"][None],
                 lb_logits=d_lbl, g_norm_w=both("g_norm_w"), ln_w=both("ln_w"), ln_b=both("ln_b"),
                 w_decay_up=both("w_decay_up"), a_up=both("a_up"), v_mix_down=grads[1]["v_mix_down"][None],
                 v_mix_up=grads[1]["v_mix_up"][None])
    flat = jnp.concatenate([small[nm].reshape(-1) for nm in _SMALL])
    n_flat = flat.shape[0]
    rows = -(-n_flat // (8 * LANES)) * 8
    flat = jnp.concatenate([flat, jnp.zeros((rows * LANES - n_flat,), f32)]).reshape(rows, LANES)
    total = _sum_slots(_all_gather_rows(flat, "ag_small_grads").reshape(N_DEV, rows, LANES), "sum_small_grads").reshape(-1)
    gsm = {}
    off = 0
    for nm in _SMALL:
        size = small[nm].size
        gsm[nm] = total[off:off + size].reshape(small[nm].shape)
        off += size
    gsm["w_decay_up"] = lax.dynamic_slice_in_dim(gsm["w_decay_up"], me * shard, shard, axis=2)
    gsm["a_up"] = lax.dynamic_slice_in_dim(gsm["a_up"], me * shard, shard, axis=2)
    gsm["v_mix_up"] = lax.dynamic_slice_in_dim(gsm["v_mix_up"], me * shard, shard, axis=2)
    gsm["v_mix_down"] = lax.dynamic_slice_in_dim(gsm["v_mix_down"], me * shard, shard, axis=1)
    gsm["w_in"] = g_w_in
    gsm["w_out"] = g_w_out

    deltas, new_m, new_v = {}, {}, {}
    for nm in _NAMES:
        deltas[nm], new_m[nm], new_v[nm] = _adamw(weights[nm], gsm[nm], mom1[nm], mom2[nm], "adamw_" + nm)
    return (loss, grad_x, *[gsm[nm] for nm in _NAMES], *[deltas[nm] for nm in _NAMES],
            *[new_m[nm] for nm in _NAMES], *[new_v[nm] for nm in _NAMES])
```

```python
import functools

import jax
import jax.numpy as jnp
from jax import lax
from jax.experimental import pallas as pl
from jax.experimental.pallas import tpu as pltpu

f32 = jnp.float32
bf16 = jnp.bfloat16

N_DEV = 8
CHUNK = 64
LANES = 128
RWKV_HEAD = 64
DEPTH = 2
ALPHA = (2 * DEPTH) ** 0.25
LN_EPS = 1e-5
GN_EPS = 64e-5
RMS_EPS = 1e-5
LB_FLOOR = 1e-30
ADAM_LR, ADAM_B1, ADAM_B2, ADAM_EPS, ADAM_WD, ADAM_STEP = 0.001, 0.9, 0.999, 1e-08, 0.01, 10
MESH = pl.DeviceIdType.MESH


def _iota(shape, d):
    return lax.broadcasted_iota(jnp.int32, shape, d)


_DIMS = {"nn": (((1,), (0,)), ((), ())), "nt": (((1,), (1,)), ((), ())), "tn": (((0,), (0,)), ((), ()))}


def _mxu(a, b, mode):
    return lax.dot_general(a, b, _DIMS[mode], preferred_element_type=f32)


def _split(x):
    hi = x.astype(bf16)
    return hi, (x - hi.astype(f32)).astype(bf16)


def _mm3_impl(a, b, mode):
    ah, al = _split(a)
    bh, bl = _split(b)
    return _mxu(ah, bh, mode) + (_mxu(ah, bl, mode) + _mxu(al, bh, mode))


@functools.partial(jax.custom_vjp, nondiff_argnums=(2,))
def _mm3(a, b, mode):
    return _mm3_impl(a, b, mode)


def _mm3_fwd(a, b, mode):
    return _mm3_impl(a, b, mode), (a, b)


def _mm3_bwd(mode, res, g):
    a, b = res
    if mode == "nn":
        return _mm3_impl(g, b, "nt"), _mm3_impl(a, g, "tn")
    if mode == "nt":
        return _mm3_impl(g, b, "nn"), _mm3_impl(g, a, "tn")
    return _mm3_impl(b, g, "nt"), _mm3_impl(a, g, "nn")


_mm3.defvjp(_mm3_fwd, _mm3_bwd)


def _const_impl(cm, x, mode):
    hi, lo = _split(x)
    if mode == "r":
        return _mxu(hi, cm, "nn") + _mxu(lo, cm, "nn")
    if mode == "rt":
        return _mxu(hi, cm, "nt") + _mxu(lo, cm, "nt")
    return _mxu(cm, hi, mode) + _mxu(cm, lo, mode)


@jax.custom_vjp
def _const_left(cm, x):
    return _const_impl(cm, x, "nn")


_const_left.defvjp(lambda cm, x: (_const_impl(cm, x, "nn"), cm),
                   lambda cm, g: (jnp.zeros_like(cm), _const_impl(cm, g, "tn")))


@jax.custom_vjp
def _const_right(x, cm):
    return _const_impl(cm, x, "r")


_const_right.defvjp(lambda x, cm: (_const_impl(cm, x, "r"), cm),
                    lambda cm, g: (_const_impl(cm, g, "rt"), jnp.zeros_like(cm)))


def _tri_inv(a):
    n = a.shape[0]
    tm = (_iota((n, n), 0) == _iota((n, n), 1)).astype(f32) + a
    ak = a
    for _ in range(5):
        ak = _mm3_impl(ak, ak, "nn")
        tm = tm + _mm3_impl(tm, ak, "nn")
    return tm


@jax.custom_vjp
def _tri_solve(a, x):
    return _mm3_impl(_tri_inv(a), x, "nn")


def _tri_solve_fwd(a, x):
    tm = _tri_inv(a)
    u = _mm3_impl(tm, x, "nn")
    return u, (tm, u)


def _tri_solve_bwd(res, du):
    tm, u = res
    dx = _mm3_impl(tm, du, "tn")
    return _mm3_impl(dx, u, "nt"), dx


_tri_solve.defvjp(_tri_solve_fwd, _tri_solve_bwd)


def _col_of_row(row_vec):
    n = row_vec.shape[1]
    eye = _iota((n, n), 0) == _iota((n, n), 1)
    return jnp.sum(jnp.where(eye, jnp.broadcast_to(row_vec, (n, n)), 0.0), axis=1, keepdims=True)


def _softplus(x):
    return jnp.maximum(x, 0.0) + jnp.log1p(jnp.exp(-jnp.abs(x)))


def _log_sigmoid(x):
    return -_softplus(-x)


def _logaddexp(a, b):
    return jnp.maximum(a, b) + jnp.log1p(jnp.exp(-jnp.abs(a - b)))


def _silu(x):
    return x * jax.nn.sigmoid(x)


def _tril(c, strict):
    r, s = _iota((c, c), 0), _iota((c, c), 1)
    return (r > s) if strict else (r >= s)


def _last_row(a):
    c = a.shape[0]
    return jnp.sum(jnp.where(_iota(a.shape, 0) == c - 1, a, 0.0), axis=0, keepdims=True)


def _rwkv_pre(layer1, prm, y, prev, vf):
    c = y.shape[0]
    if layer1:
        mu, w0, a0, wup, aup, v0, vdown, vup = prm
    else:
        mu, w0, a0, wup, aup = prm
    dr = w0.shape[1]
    shift = (_iota((c, c), 0) == _iota((c, c), 1) + 1).astype(bf16)
    y_prev = _const_left(shift, y) + jnp.where(_iota((c, 1), 0) == 0, prev, 0.0)
    rw = y + mu * (y_prev - y)
    r, k, v, z = (rw[:, i * dr:(i + 1) * dr] for i in range(4))
    wdad = rw[:, 4 * dr:4 * dr + LANES]
    w_raw = w0 + _mm3(jnp.tanh(wdad), wup, "nn")
    lw = -jnp.exp(-_softplus(-w_raw) - 0.5)
    asig = jax.nn.sigmoid(a0 + _mm3(wdad, aup, "nn"))
    if layer1:
        v = v + (vf - v) * jax.nn.sigmoid(v0 + _mm3(_mm3(v, vdown, "nn"), vup, "nn"))
    return r, k, v, z, lw, asig


def _rwkv_pair(pp, m0, xs):
    kkw, kaw, rkw, gnw, gnb = pp
    r, k, v, z, lw, asig = xs
    c = r.shape[0]
    n2 = 2 * c
    lane = _iota((1, LANES), 1)
    mh0, mh1 = (lane < RWKV_HEAD).astype(f32), (lane >= RWKV_HEAD).astype(f32)
    same_head = _iota((LANES, LANES), 0) // RWKV_HEAD == _iota((LANES, LANES), 1) // RWKV_HEAD
    g = same_head.astype(bf16)

    def seg(x):
        return _const_right(x, g)

    def stack(x):
        return jnp.concatenate([x * mh0, x * mh1], axis=0)

    kk = k * kkw
    kk = kk / jnp.maximum(jnp.sqrt(seg(kk * kk)), 1e-12)
    k2 = k * (1.0 + (asig - 1.0) * kaw)
    a = -kk
    b = kk * asig
    cum = _const_left(_tril(c, False).astype(bf16), lw)
    at = stack(a * jnp.exp(cum - lw))
    rt = stack(r * jnp.exp(cum))
    en = jnp.exp(-cum)
    sc = _mm3(jnp.concatenate([at, rt], axis=0), jnp.concatenate([stack(b * en), stack(k2 * en)], axis=0), "nt")
    row, col = _iota((n2, n2), 0), _iota((n2, n2), 1)
    same = row // c == col // c
    strict = same & (row % c > col % c)
    incl = same & (row % c >= col % c)
    aab = jnp.where(strict, sc[:n2, :n2], 0.0)
    aak = jnp.where(strict, sc[:n2, n2:], 0.0)
    arb = jnp.where(incl, sc[n2:, :n2], 0.0)
    ark = jnp.where(incl, sc[n2:, n2:], 0.0)
    vv = jnp.concatenate([v, v], axis=0)
    mask_st = jnp.concatenate([jnp.broadcast_to(mh0, (c, LANES)), jnp.broadcast_to(mh1, (c, LANES))], axis=0)
    x_st = _mm3(jnp.concatenate([at, aak], axis=1), jnp.concatenate([m0, vv], axis=0), "nn")
    u_st = _tri_solve(aab, x_st) * mask_st
    o_st = _mm3(jnp.concatenate([rt, arb, ark], axis=1), jnp.concatenate([m0, u_st, vv], axis=0), "nn") * mask_st
    u = u_st[:c] + u_st[c:]
    o = o_st[:c] + o_st[c:]
    cum_last = _last_row(cum)
    dec_end = jnp.exp(cum_last - cum)
    m_new = _col_of_row(jnp.exp(cum_last)) * m0 + _mm3(
        jnp.concatenate([b * dec_end, k2 * dec_end], axis=0), jnp.concatenate([u, v], axis=0), "tn") * same_head.astype(f32)
    mean = seg(o) * (1.0 / RWKV_HEAD)
    d = o - mean
    var = seg(d * d) * (1.0 / RWKV_HEAD)
    on = d * lax.rsqrt(var + GN_EPS) * gnw + gnb
    bonus = seg(r * k2 * rkw) * v
    return (on + bonus) * _silu(z), m_new


def _split_lanes(a, n):
    return [a[:, i * LANES:(i + 1) * LANES] for i in range(n)]


def _rwkv_specs(layer1, t, dr, rwc, n_pair, rev):
    nc = t // CHUNK

    def cidx(c):
        return (nc - 1 - c) if rev else c

    full = lambda shape: pl.BlockSpec(shape, lambda c, p: tuple(0 for _ in shape))
    specs = [
        pl.BlockSpec((CHUNK, rwc), lambda c, p: (cidx(c), 0)),
        pl.BlockSpec((8, rwc), lambda c, p: (jnp.maximum(cidx(c) * (CHUNK // 8) - 1, 0), 0)),
    ]
    if layer1:
        specs.append(pl.BlockSpec((CHUNK, dr), lambda c, p: (cidx(c), 0)))
    prm_shapes = [(1, rwc), (1, dr), (1, dr), (LANES, dr), (LANES, dr)]
    if layer1:
        prm_shapes += [(1, dr), (dr, LANES), (LANES, dr)]
    specs += [full(s) for s in prm_shapes]
    specs.append(pl.BlockSpec((1, 8, LANES), lambda c, p: (p, 0, 0)))
    return specs, prm_shapes, cidx, full


def _rwkv_fwd(layer1, proj, vf, prm, pp, cat_width):
    t = proj.shape[0]
    dr = prm[1].shape[1]
    rwc = prm[0].shape[1]
    n_pair = dr // LANES
    nc = t // CHUNK
    n_prm = len(prm)
    specs, _, _, _ = _rwkv_specs(layer1, t, dr, rwc, n_pair, False)

    def body(*refs):
        y_ref, prev_ref = refs[0], refs[1]
        i = 2
        vf_ref = None
        if layer1:
            vf_ref = refs[i]
            i += 1
        prm_refs = refs[i:i + n_prm]
        i += n_prm
        pp_ref = refs[i]
        i += 1
        cat_ref = refs[i]
        i += 1
        vout_ref = None
        if not layer1:
            vout_ref = refs[i]
            i += 1
        mck_ref, x_s, m_s = refs[i], refs[i + 1], refs[i + 2]
        c, p = pl.program_id(0), pl.program_id(1)

        @pl.when((c == 0) & (p == 0))
        def _():
            m_s[...] = jnp.zeros_like(m_s)

        @pl.when(p == 0)
        def _():
            prev = prev_ref[pl.ds(7, 1), :] * (c != 0).astype(f32)
            xs = _rwkv_pre(layer1, tuple(r[...] for r in prm_refs), y_ref[...], prev,
                           vf_ref[...] if layer1 else None)
            for q, a in enumerate(xs):
                for j, piece in enumerate(_split_lanes(a, n_pair)):
                    x_s[q * n_pair + j] = piece
            if not layer1:
                vout_ref[...] = xs[2]

        m0 = m_s[p]
        mck_ref[0, 0] = m0
        ppv = tuple(pp_ref[0, pl.ds(q, 1), :] for q in range(5))
        og, m_new = _rwkv_pair(ppv, m0, tuple(x_s[q * n_pair + p] for q in range(6)))
        cat_ref[...] = og
        m_s[p] = m_new

    out_shape = [jax.ShapeDtypeStruct((t, cat_width), f32)]
    out_specs = [pl.BlockSpec((CHUNK, LANES), lambda c, p: (c, p))]
    if not layer1:
        out_shape.append(jax.ShapeDtypeStruct((t, dr), f32))
        out_specs.append(pl.BlockSpec((CHUNK, dr), lambda c, p: (c, 0)))
    out_shape.append(jax.ShapeDtypeStruct((nc, n_pair, LANES, LANES), f32))
    out_specs.append(pl.BlockSpec((1, 1, LANES, LANES), lambda c, p: (c, p, 0, 0)))
    args = [proj, proj] + ([vf] if layer1 else []) + list(prm) + [pp]
    return pl.pallas_call(
        body, grid=(nc, n_pair), in_specs=specs, out_specs=out_specs, out_shape=out_shape,
        scratch_shapes=[pltpu.VMEM((6 * n_pair, CHUNK, LANES), f32), pltpu.VMEM((n_pair, LANES, LANES), f32)],
        compiler_params=pltpu.CompilerParams(dimension_semantics=("arbitrary", "arbitrary")),
        name=f"rwkv_fwd_l{int(layer1)}",
    )(*args)


def _rwkv_bwd(layer1, proj, vf, prm, pp, mck, dcat, dvout):
    t = proj.shape[0]
    dr = prm[1].shape[1]
    rwc = prm[0].shape[1]
    n_pair = dr // LANES
    nc = t // CHUNK
    n_prm = len(prm)
    specs, prm_shapes, cidx, full = _rwkv_specs(layer1, t, dr, rwc, n_pair, True)
    specs.append(pl.BlockSpec((1, 1, LANES, LANES), lambda c, p: (cidx(c), p, 0, 0)))
    specs.append(pl.BlockSpec((CHUNK, LANES), lambda c, p: (cidx(c), p)))
    if not layer1:
        specs.append(pl.BlockSpec((CHUNK, dr), lambda c, p: (cidx(c), 0)))

    def body(*refs):
        y_ref, prev_ref = refs[0], refs[1]
        i = 2
        vf_ref = None
        if layer1:
            vf_ref = refs[i]
            i += 1
        prm_refs = refs[i:i + n_prm]
        i += n_prm
        pp_ref, mck_ref, dog_ref = refs[i], refs[i + 1], refs[i + 2]
        i += 3
        dvout_ref = None
        if not layer1:
            dvout_ref = refs[i]
            i += 1
        dy_ref = refs[i]
        i += 1
        dvf_ref = None
        if layer1:
            dvf_ref = refs[i]
            i += 1
        dprm_refs = refs[i:i + n_prm]
        i += n_prm
        dpp_ref = refs[i]
        x_s, dx_s, dm_s, dprev_s = refs[i + 1:i + 5]
        c, p = pl.program_id(0), pl.program_id(1)
        cr = nc - 1 - c

        def prev_row():
            return prev_ref[pl.ds(7, 1), :] * (cr != 0).astype(f32)

        @pl.when((c == 0) & (p == 0))
        def _():
            dm_s[...] = jnp.zeros_like(dm_s)
            dprev_s[...] = jnp.zeros_like(dprev_s)
            dpp_ref[...] = jnp.zeros_like(dpp_ref)
            for r in dprm_refs:
                r[...] = jnp.zeros_like(r)

        @pl.when(p == 0)
        def _():
            xs = _rwkv_pre(layer1, tuple(r[...] for r in prm_refs), y_ref[...], prev_row(),
                           vf_ref[...] if layer1 else None)
            for q, a in enumerate(xs):
                for j, piece in enumerate(_split_lanes(a, n_pair)):
                    x_s[q * n_pair + j] = piece

        ppv = tuple(pp_ref[0, pl.ds(q, 1), :] for q in range(5))
        xs_p = tuple(x_s[q * n_pair + p] for q in range(6))
        _, vjp_pair = jax.vjp(_rwkv_pair, ppv, mck_ref[0, 0], xs_p)
        dppv, dm0, dxs = vjp_pair((dog_ref[...], dm_s[p]))
        dm_s[p] = dm0
        for q in range(6):
            dx_s[q * n_pair + p] = dxs[q]
        for q in range(5):
            dpp_ref[p, pl.ds(q, 1), :] += dppv[q]

        @pl.when(p == n_pair - 1)
        def _():
            dxs_full = [jnp.concatenate([dx_s[q * n_pair + j] for j in range(n_pair)], axis=1) for q in range(6)]
            if not layer1:
                dxs_full[2] = dxs_full[2] + dvout_ref[...]
            prm_v = tuple(r[...] for r in prm_refs)
            if layer1:
                _, vjp_pre = jax.vjp(functools.partial(_rwkv_pre, True), prm_v, y_ref[...], prev_row(), vf_ref[...])
                dprm, dy, dprev, dvf = vjp_pre(tuple(dxs_full))
                dvf_ref[...] = dvf
            else:
                _, vjp_pre = jax.vjp(lambda a, b, d: _rwkv_pre(False, a, b, d, None), prm_v, y_ref[...], prev_row())
                dprm, dy, dprev = vjp_pre(tuple(dxs_full))
            dy_ref[...] = dy + jnp.where(_iota((CHUNK, 1), 0) == CHUNK - 1, dprev_s[...], 0.0)
            dprev_s[...] = dprev
            for r, gval in zip(dprm_refs, dprm):
                r[...] += gval

    out_shape = [jax.ShapeDtypeStruct((t, rwc), f32)]
    out_specs = [pl.BlockSpec((CHUNK, rwc), lambda c, p: (cidx(c), 0))]
    if layer1:
        out_shape.append(jax.ShapeDtypeStruct((t, dr), f32))
        out_specs.append(pl.BlockSpec((CHUNK, dr), lambda c, p: (cidx(c), 0)))
    out_shape += [jax.ShapeDtypeStruct(s, f32) for s in prm_shapes]
    out_specs += [full(s) for s in prm_shapes]
    out_shape.append(jax.ShapeDtypeStruct((n_pair, 8, LANES), f32))
    out_specs.append(full((n_pair, 8, LANES)))
    args = [proj, proj] + ([vf] if layer1 else []) + list(prm) + [pp, mck, dcat] + ([] if layer1 else [dvout])
    return pl.pallas_call(
        body, grid=(nc, n_pair), in_specs=specs, out_specs=out_specs, out_shape=out_shape,
        scratch_shapes=[pltpu.VMEM((6 * n_pair, CHUNK, LANES), f32), pltpu.VMEM((6 * n_pair, CHUNK, LANES), f32),
                        pltpu.VMEM((n_pair, LANES, LANES), f32), pltpu.VMEM((1, rwc), f32)],
        compiler_params=pltpu.CompilerParams(dimension_semantics=("arbitrary", "arbitrary")),
        name=f"rwkv_bwd_l{int(layer1)}",
    )(*args)


def _hgrn_chunk(layer1, lbl, gnw, s0, q_raw, f_raw, i_in, z):
    c = q_raw.shape[0]
    q = _silu(q_raw)
    ls = _log_sigmoid(f_raw)
    if layer1:
        l0, l1 = lbl[0:1, :], lbl[1:2, :]
        mx = jnp.maximum(l0, l1)
        e0, e1 = jnp.exp(l0 - mx), jnp.exp(l1 - mx)
        sm0, sm1 = e0 / (e0 + e1), e1 / (e0 + e1)
        lb = (sm0 + sm1) - sm0
        log_f = _logaddexp(jnp.log(jnp.maximum(lb, LB_FLOOR)), jnp.log1p(-lb) + ls)
        k = (1.0 - lb) * jax.nn.sigmoid(-f_raw)
    else:
        log_f = _logaddexp(jnp.full_like(ls, jnp.log(jnp.float32(LB_FLOOR))), ls)
        k = jax.nn.sigmoid(-f_raw)
    row, col = _iota((c, c), 0), _iota((c, c), 1)
    trow = _iota((c, 1), 0)
    halves = []
    half = c // 2
    while half >= 1:
        halves.append(half)
        half //= 2
    cmat = jnp.concatenate([(col <= row).astype(f32)]
                           + [(col <= (row // (2 * hf)) * (2 * hf) + hf - 1).astype(f32) for hf in halves], axis=0)
    ball = _const_left(cmat.astype(bf16), log_f)
    b = ball[:c]
    att = jnp.zeros((c, c), f32)
    for lvl, hf in enumerate(halves):
        blk = 2 * hf
        bref = ball[(lvl + 1) * c:(lvl + 2) * c]
        upper = (trow % blk) >= hf
        qh = q * jnp.exp(jnp.where(upper, b - bref, 0.0)) * upper.astype(f32)
        kh = k * jnp.exp(jnp.where(upper, 0.0, bref - b)) * (1.0 - upper.astype(f32))
        att = att + jnp.where(row // blk == col // blk, _mm3(qh, kh, "nt"), 0.0)
    o = (_mm3(jnp.concatenate([q * jnp.exp(b), att], axis=1), jnp.concatenate([s0, i_in], axis=0), "nn")
         + jnp.sum(q * k, axis=-1, keepdims=True) * i_in)
    b_last = _last_row(b)
    s_new = _col_of_row(jnp.exp(b_last)) * s0 + _mm3(k * jnp.exp(b_last - b), i_in, "tn")
    o = o * lax.rsqrt(jnp.mean(o * o, axis=-1, keepdims=True) + RMS_EPS)
    return o * gnw * _silu(z), s_new


def _hgrn_in_specs(t, dh, col0, rev):
    nc = t // CHUNK
    nh = dh // LANES

    def cidx(c):
        return (nc - 1 - c) if rev else c

    specs = [pl.BlockSpec((CHUNK, LANES), functools.partial(lambda g, h, c: (cidx(c), col0 + g * nh + h), g))
             for g in range(4)]
    specs.append(pl.BlockSpec((2, LANES), lambda h, c: (0, h)))
    specs.append(pl.BlockSpec((1, LANES), lambda h, c: (0, h)))
    return specs, cidx


def _hgrn_fwd(layer1, proj, lbl, gnw, cat, rwc):
    t, d = cat.shape
    dh = gnw.shape[1]
    nh = dh // LANES
    nc = t // CHUNK
    col0 = rwc // LANES
    specs, _ = _hgrn_in_specs(t, dh, col0, False)
    specs.append(pl.BlockSpec(memory_space=pl.ANY))
    cat_col0 = (d - dh) // LANES

    def body(q_ref, f_ref, i_ref, z_ref, lbl_ref, gnw_ref, cat_in, cat_ref, sck_ref, s_s):
        del cat_in
        c = pl.program_id(1)

        @pl.when(c == 0)
        def _():
            s_s[...] = jnp.zeros_like(s_s)

        s0 = s_s[...]
        sck_ref[0, 0] = s0
        out, s_new = _hgrn_chunk(layer1, lbl_ref[...], gnw_ref[...], s0, q_ref[...], f_ref[...], i_ref[...], z_ref[...])
        cat_ref[...] = out
        s_s[...] = s_new

    return pl.pallas_call(
        body, grid=(nh, nc), in_specs=specs,
        out_specs=[pl.BlockSpec((CHUNK, LANES), lambda h, c: (c, cat_col0 + h)),
                   pl.BlockSpec((1, 1, LANES, LANES), lambda h, c: (h, c, 0, 0))],
        out_shape=[jax.ShapeDtypeStruct((t, d), f32), jax.ShapeDtypeStruct((nh, nc, LANES, LANES), f32)],
        scratch_shapes=[pltpu.VMEM((LANES, LANES), f32)],
        input_output_aliases={6: 0},
        compiler_params=pltpu.CompilerParams(dimension_semantics=("arbitrary", "arbitrary")),
        name=f"hgrn_fwd_l{int(layer1)}",
    )(proj, proj, proj, proj, lbl, gnw, cat)


def _hgrn_bwd(layer1, proj, lbl, gnw, sck, dcat, rwc):
    t, d = dcat.shape
    dh = gnw.shape[1]
    nh = dh // LANES
    nc = t // CHUNK
    col0 = rwc // LANES
    specs, cidx = _hgrn_in_specs(t, dh, col0, True)
    cat_col0 = (d - dh) // LANES
    specs.append(pl.BlockSpec((1, 1, LANES, LANES), lambda h, c: (h, cidx(c), 0, 0)))
    specs.append(pl.BlockSpec((CHUNK, LANES), lambda h, c: (cidx(c), cat_col0 + h)))

    def body(q_ref, f_ref, i_ref, z_ref, lbl_ref, gnw_ref, sck_ref, do_ref, dp_ref, dlbl_ref, dgnw_ref, ds_s):
        c = pl.program_id(1)

        @pl.when(c == 0)
        def _():
            ds_s[...] = jnp.zeros_like(ds_s)
            dlbl_ref[...] = jnp.zeros_like(dlbl_ref)
            dgnw_ref[...] = jnp.zeros_like(dgnw_ref)

        _, vjp = jax.vjp(functools.partial(_hgrn_chunk, layer1), lbl_ref[...], gnw_ref[...], sck_ref[0, 0],
                         q_ref[...], f_ref[...], i_ref[...], z_ref[...])
        dlbl, dgnw, ds0, dq, df, di, dz = vjp((do_ref[...], ds_s[...]))
        ds_s[...] = ds0
        dlbl_ref[...] += dlbl
        dgnw_ref[...] += dgnw
        dp_ref[0] = dq
        dp_ref[1] = df
        dp_ref[2] = di
        dp_ref[3] = dz

    return pl.pallas_call(
        body, grid=(nh, nc), in_specs=specs,
        out_specs=[pl.BlockSpec((4, CHUNK, LANES), lambda h, c: (0, cidx(c), h)),
                   pl.BlockSpec((2, LANES), lambda h, c: (0, h)),
                   pl.BlockSpec((1, LANES), lambda h, c: (0, h))],
        out_shape=[jax.ShapeDtypeStruct((4, t, dh), f32), jax.ShapeDtypeStruct((2, dh), f32),
                   jax.ShapeDtypeStruct((1, dh), f32)],
        scratch_shapes=[pltpu.VMEM((LANES, LANES), f32)],
        compiler_params=pltpu.CompilerParams(dimension_semantics=("arbitrary", "arbitrary")),
        name=f"hgrn_bwd_l{int(layer1)}",
    )(proj, proj, proj, proj, lbl, gnw, sck, dcat)


def _ln(h, y, w, b):
    u = ALPHA * h + y
    mu = jnp.mean(u, axis=-1, keepdims=True)
    var = jnp.mean(jnp.square(u - mu), axis=-1, keepdims=True)
    return (u - mu) * lax.rsqrt(var + LN_EPS) * w + b


def _row_tile(t):
    return 256 if t % 256 == 0 else t


def _ln_fwd(h, y, w, b):
    t, d = h.shape
    tr = _row_tile(t)

    def body(h_ref, y_ref, w_ref, b_ref, o_ref):
        o_ref[...] = _ln(h_ref[...], y_ref[...], w_ref[...], b_ref[...])

    row = pl.BlockSpec((tr, d), lambda i: (i, 0))
    vec = pl.BlockSpec((1, d), lambda i: (0, 0))
    return pl.pallas_call(body, grid=(t // tr,), in_specs=[row, row, vec, vec], out_specs=row,
                          out_shape=jax.ShapeDtypeStruct((t, d), f32), name="ln_fwd")(h, y, w, b)


def _ln_loss(h, y, w, b, tgt):
    t, d = h.shape
    tr = _row_tile(t)

    def body(h_ref, y_ref, w_ref, b_ref, t_ref, g_ref, loss_ref):
        @pl.when(pl.program_id(0) == 0)
        def _():
            loss_ref[...] = jnp.zeros_like(loss_ref)

        err = _ln(h_ref[...], y_ref[...], w_ref[...], b_ref[...]) - t_ref[...]
        g_ref[...] = err * (1.0 / d)
        loss_ref[...] += 0.5 * jnp.sum(jnp.mean(jnp.square(err), axis=-1, keepdims=True), axis=0, keepdims=True)

    row = pl.BlockSpec((tr, d), lambda i: (i, 0))
    vec = pl.BlockSpec((1, d), lambda i: (0, 0))
    return pl.pallas_call(
        body, grid=(t // tr,), in_specs=[row, row, vec, vec, row],
        out_specs=[row, pl.BlockSpec((1, LANES), lambda i: (0, 0))],
        out_shape=[jax.ShapeDtypeStruct((t, d), f32), jax.ShapeDtypeStruct((1, LANES), f32)],
        compiler_params=pltpu.CompilerParams(dimension_semantics=("arbitrary",)), name="ln_loss")(h, y, w, b, tgt)


def _ln_bwd(h, y, w, b, dout):
    t, d = h.shape
    tr = _row_tile(t)

    def body(h_ref, y_ref, w_ref, b_ref, do_ref, dy_ref, dw_ref, db_ref):
        @pl.when(pl.program_id(0) == 0)
        def _():
            dw_ref[...] = jnp.zeros_like(dw_ref)
            db_ref[...] = jnp.zeros_like(db_ref)

        _, vjp = jax.vjp(lambda yy, ww, bb: _ln(h_ref[...], yy, ww, bb), y_ref[...], w_ref[...], b_ref[...])
        dy, dw, db = vjp(do_ref[...])
        dy_ref[...] = dy
        dw_ref[...] += dw
        db_ref[...] += db

    row = pl.BlockSpec((tr, d), lambda i: (i, 0))
    vec = pl.BlockSpec((1, d), lambda i: (0, 0))
    return pl.pallas_call(
        body, grid=(t // tr,), in_specs=[row, row, vec, vec, row], out_specs=[row, vec, vec],
        out_shape=[jax.ShapeDtypeStruct((t, d), f32), jax.ShapeDtypeStruct((1, d), f32), jax.ShapeDtypeStruct((1, d), f32)],
        compiler_params=pltpu.CompilerParams(dimension_semantics=("arbitrary",)), name="ln_bwd")(h, y, w, b, dout)


def _pick(n, prefs):
    for p in prefs:
        if n % p == 0:
            return p
    return n


def _matmul(a, b, mode, name, add=None, add_scale=1.0, out_dtype=f32):
    if mode == "nn":
        (m, k), n = a.shape, b.shape[1]
    elif mode == "nt":
        (m, k), n = a.shape, b.shape[0]
    else:
        (k, m), n = a.shape, b.shape[1]
    tm = _pick(m, (512, 640, 256, 128))
    tn = _pick(n, (1024, 640, 512, 256, 128))
    tk = _pick(k, (2048, 1024, 640, 512, 256, 128))
    nk = k // tk
    dims = {"nn": (((1,), (0,)), ((), ())), "nt": (((1,), (1,)), ((), ())), "tn": (((0,), (0,)), ((), ()))}[mode]

    def body(*refs):
        a_ref, b_ref = refs[0], refs[1]
        add_ref = refs[2] if add is not None else None
        o_ref, acc = refs[-2], refs[-1]
        kk = pl.program_id(2)

        @pl.when(kk == 0)
        def _():
            acc[...] = jnp.zeros_like(acc)

        acc[...] += lax.dot_general(a_ref[...].astype(bf16), b_ref[...].astype(bf16), dims, preferred_element_type=f32)

        @pl.when(kk == nk - 1)
        def _():
            res = acc[...]
            if add is not None:
                res = res + add_scale * add_ref[...]
            o_ref[...] = res.astype(out_dtype)

    a_spec = pl.BlockSpec((tk, tm), lambda i, j, kk: (kk, i)) if mode == "tn" else pl.BlockSpec((tm, tk), lambda i, j, kk: (i, kk))
    b_spec = pl.BlockSpec((tn, tk), lambda i, j, kk: (j, kk)) if mode == "nt" else pl.BlockSpec((tk, tn), lambda i, j, kk: (kk, j))
    o_spec = pl.BlockSpec((tm, tn), lambda i, j, kk: (i, j))
    in_specs = [a_spec, b_spec] + ([o_spec] if add is not None else [])
    args = [a, b] + ([add] if add is not None else [])
    return pl.pallas_call(
        body, grid=(m // tm, n // tn, nk), in_specs=in_specs, out_specs=o_spec,
        out_shape=jax.ShapeDtypeStruct((m, n), out_dtype), scratch_shapes=[pltpu.VMEM((tm, tn), f32)],
        compiler_params=pltpu.CompilerParams(dimension_semantics=("parallel", "parallel", "arbitrary")),
        name=name,
    )(*args)


def _position():
    return lax.axis_index("x"), lax.axis_index("y"), lax.axis_index("c")


def _flip(pos, k):
    x, y, c = pos
    return (1 - x if k & 4 else x, 1 - y if k & 2 else y, 1 - c if k & 1 else c)


def _index(pos):
    return 4 * pos[0] + 2 * pos[1] + pos[2]


def _all_gather_rows(x, name):
    m_per, n = x.shape

    def body(x_ref, out_ref, send_sems, recv_sems, local_sem):
        me = _position()
        sibling = _flip(me, 1)
        chips = (2, 4, 6)

        def rows(pos):
            return out_ref.at[pl.ds(_index(pos) * m_per, m_per), :]

        def copy(sem, block, to, src=None):
            return pltpu.make_async_remote_copy(
                src_ref=rows(block) if src is None else src, dst_ref=rows(block),
                send_sem=send_sems.at[sem], recv_sem=recv_sems.at[sem], device_id=to, device_id_type=MESH)

        mine = pltpu.make_async_copy(x_ref, rows(me), local_sem)
        mine.start()
        first = [copy(0, me, sibling, src=x_ref)]
        first += [copy(1 + j, me, _flip(me, k), src=x_ref) for j, k in enumerate(chips)]
        for cp in first:
            cp.start()
        passed = [copy(4 + j, _flip(me, k), sibling) for j, k in enumerate(chips)]
        for j, k in enumerate(chips):
            copy(1 + j, _flip(me, k), me).wait_recv()
            passed[j].start()
        copy(0, sibling, me).wait_recv()
        for j, k in enumerate(chips):
            copy(4 + j, _flip(sibling, k), me).wait_recv()
        for cp in first + passed:
            cp.wait_send()
        mine.wait()

    return pl.pallas_call(
        body, out_shape=jax.ShapeDtypeStruct((N_DEV * m_per, n), x.dtype),
        in_specs=[pl.BlockSpec(memory_space=pl.ANY)], out_specs=pl.BlockSpec(memory_space=pl.ANY),
        scratch_shapes=[pltpu.SemaphoreType.DMA((7,)), pltpu.SemaphoreType.DMA((7,)), pltpu.SemaphoreType.DMA(())],
        name=name,
    )(x)


def _exchange_siblings(gs, name):
    n_arr = len(gs)

    def body(*refs):
        g_refs, out_refs = refs[:n_arr], refs[n_arr:2 * n_arr]
        send_sems, recv_sems, local_sems = refs[2 * n_arr:]
        me = _position()
        c = me[2]
        sibling = _flip(me, 1)
        local, sends, arrivals = [], [], []
        for i, (g_ref, out_ref) in enumerate(zip(g_refs, out_refs)):
            m_per = g_ref.shape[0] // N_DEV
            for q in range(4):
                sem = 4 * i + q
                local.append(pltpu.make_async_copy(g_ref.at[pl.ds((2 * q + c) * m_per, m_per), :], out_ref.at[c, q],
                                                   local_sems.at[sem]))
                sends.append(pltpu.make_async_remote_copy(
                    src_ref=g_ref.at[pl.ds((2 * q + 1 - c) * m_per, m_per), :], dst_ref=out_ref.at[c, q],
                    send_sem=send_sems.at[sem], recv_sem=recv_sems.at[sem], device_id=sibling, device_id_type=MESH))
                arrivals.append(pltpu.make_async_remote_copy(
                    src_ref=g_ref.at[pl.ds((2 * q + 1 - c) * m_per, m_per), :], dst_ref=out_ref.at[1 - c, q],
                    send_sem=send_sems.at[sem], recv_sem=recv_sems.at[sem], device_id=sibling, device_id_type=MESH))
        for cp in sends + local:
            cp.start()
        for cp in arrivals:
            cp.wait_recv()
        for cp in sends:
            cp.wait_send()
        for cp in local:
            cp.wait()

    anyspec = pl.BlockSpec(memory_space=pl.ANY)
    return pl.pallas_call(
        body, out_shape=[jax.ShapeDtypeStruct((2, 4, g.shape[0] // N_DEV, g.shape[1]), g.dtype) for g in gs],
        in_specs=[anyspec] * n_arr, out_specs=[anyspec] * n_arr,
        scratch_shapes=[pltpu.SemaphoreType.DMA((4 * n_arr,))] * 3, name=name,
    )(*gs)


def _exchange_chips(hs, name):
    n_arr = len(hs)
    chips = (2, 4, 6)

    def body(*refs):
        h_refs, out_refs = refs[:n_arr], refs[n_arr:2 * n_arr]
        send_sems, recv_sems, local_sems = refs[2 * n_arr:]
        me = _position()
        my_q = 2 * me[0] + me[1]
        local, sends, arrivals = [], [], []
        for i, (h_ref, out_ref) in enumerate(zip(h_refs, out_refs)):
            m_per = h_ref.shape[0] // 4
            local.append(pltpu.make_async_copy(h_ref.at[pl.ds(my_q * m_per, m_per), :], out_ref.at[my_q], local_sems.at[i]))
            for j, k in enumerate(chips):
                peer = _flip(me, k)
                peer_q = 2 * peer[0] + peer[1]
                src = h_ref.at[pl.ds(peer_q * m_per, m_per), :]
                sem = 3 * i + j
                sends.append(pltpu.make_async_remote_copy(
                    src_ref=src, dst_ref=out_ref.at[my_q], send_sem=send_sems.at[sem], recv_sem=recv_sems.at[sem],
                    device_id=peer, device_id_type=MESH))
                arrivals.append(pltpu.make_async_remote_copy(
                    src_ref=src, dst_ref=out_ref.at[peer_q], send_sem=send_sems.at[sem], recv_sem=recv_sems.at[sem],
                    device_id=peer, device_id_type=MESH))
        for cp in sends + local:
            cp.start()
        for cp in arrivals:
            cp.wait_recv()
        for cp in sends:
            cp.wait_send()
        for cp in local:
            cp.wait()

    anyspec = pl.BlockSpec(memory_space=pl.ANY)
    return pl.pallas_call(
        body, out_shape=[jax.ShapeDtypeStruct((4, h.shape[0] // 4, h.shape[1]), h.dtype) for h in hs],
        in_specs=[anyspec] * n_arr, out_specs=[anyspec] * n_arr,
        scratch_shapes=[pltpu.SemaphoreType.DMA((3 * n_arr,)), pltpu.SemaphoreType.DMA((3 * n_arr,)),
                        pltpu.SemaphoreType.DMA((n_arr,))], name=name,
    )(*hs)


def _sum_slots(parts, name, out_dtype=f32):
    n_slot, m, n = parts.shape
    tr = _pick(m, (208, 128, 64, 32, 16, 8))

    def body(p_ref, o_ref):
        acc = p_ref[0].astype(f32)
        for s in range(1, n_slot):
            acc = acc + p_ref[s].astype(f32)
        o_ref[...] = acc.astype(out_dtype)

    return pl.pallas_call(
        body, grid=(m // tr,), in_specs=[pl.BlockSpec((n_slot, tr, n), lambda i: (0, i, 0))],
        out_specs=pl.BlockSpec((tr, n), lambda i: (i, 0)), out_shape=jax.ShapeDtypeStruct((m, n), out_dtype), name=name,
    )(parts)


def _reduce_scatter(gs, name):
    halves = _exchange_siblings(gs, "rs_d2d_" + name)
    chip_sums = [_sum_slots(hv.reshape(2, -1, hv.shape[-1]), f"rs_sum2_{name}_{i}", bf16) for i, hv in enumerate(halves)]
    parts = _exchange_chips(chip_sums, "rs_ici_" + name)
    return [_sum_slots(pt, f"rs_sum4_{name}_{i}") for i, pt in enumerate(parts)]


def _adamw(w, g, m, v, name):
    shape = w.shape
    n = shape[-1]
    r = w.size // n
    w2, g2, m2, v2 = (a.reshape(r, n) for a in (w, g, m, v))
    tr = _pick(r, (256, 128, 64, 32, 16, 8)) if r * n > 65536 else r

    def body(w_ref, g_ref, m_ref, v_ref, d_ref, mo_ref, vo_ref):
        gg = g_ref[...]
        mm = ADAM_B1 * m_ref[...] + (1.0 - ADAM_B1) * gg
        vv = ADAM_B2 * v_ref[...] + (1.0 - ADAM_B2) * jnp.square(gg)
        m_hat = mm / (1.0 - ADAM_B1 ** ADAM_STEP)
        v_hat = vv / (1.0 - ADAM_B2 ** ADAM_STEP)
        d_ref[...] = -ADAM_LR * (m_hat / (jnp.sqrt(v_hat) + ADAM_EPS) + ADAM_WD * w_ref[...])
        mo_ref[...] = mm
        vo_ref[...] = vv

    spec = pl.BlockSpec((tr, n), lambda i: (i, 0))
    outs = pl.pallas_call(
        body, grid=(r // tr,), in_specs=[spec] * 4, out_specs=[spec] * 3,
        out_shape=[jax.ShapeDtypeStruct((r, n), f32)] * 3, name=name,
    )(w2, g2, m2, v2)
    return tuple(o.reshape(shape) for o in outs)


_SMALL = ("shift_mu", "w_decay0", "a0", "k_k", "k_a", "r_k", "ln_x_w", "ln_x_b", "v_mix0", "lb_logits",
          "g_norm_w", "ln_w", "ln_b", "w_decay_up", "a_up", "v_mix_down", "v_mix_up")
_NAMES = ("w_in", "shift_mu", "w_decay0", "w_decay_up", "a0", "a_up", "k_k", "k_a", "r_k", "ln_x_w", "ln_x_b",
          "v_mix0", "v_mix_down", "v_mix_up", "lb_logits", "g_norm_w", "w_out", "ln_w", "ln_b")


def _pad_rows(a, rows, at_end):
    z = jnp.zeros((rows - a.shape[0], a.shape[1]), a.dtype)
    return jnp.concatenate([a, z] if at_end else [z, a], axis=0)


def kernel(x, w_in, shift_mu, w_decay0, w_decay_up, a0, a_up, k_k, k_a, r_k, ln_x_w, ln_x_b, v_mix0, v_mix_down, v_mix_up, lb_logits, g_norm_w, w_out, ln_w, ln_b, loss_target, m_w_in, m_shift_mu, m_w_decay0, m_w_decay_up, m_a0, m_a_up, m_k_k, m_k_a, m_r_k, m_ln_x_w, m_ln_x_b, m_v_mix0, m_v_mix_down, m_v_mix_up, m_lb_logits, m_g_norm_w, m_w_out, m_ln_w, m_ln_b, v_w_in, v_shift_mu, v_w_decay0, v_w_decay_up, v_a0, v_a_up, v_k_k, v_k_a, v_r_k, v_ln_x_w, v_ln_x_b, v_v_mix0, v_v_mix_down, v_v_mix_up, v_lb_logits, v_g_norm_w, v_w_out, v_ln_w, v_ln_b):
    weights = dict(w_in=w_in, shift_mu=shift_mu, w_decay0=w_decay0, w_decay_up=w_decay_up, a0=a0, a_up=a_up, k_k=k_k,
                   k_a=k_a, r_k=r_k, ln_x_w=ln_x_w, ln_x_b=ln_x_b, v_mix0=v_mix0, v_mix_down=v_mix_down,
                   v_mix_up=v_mix_up, lb_logits=lb_logits, g_norm_w=g_norm_w, w_out=w_out, ln_w=ln_w, ln_b=ln_b)
    mom1 = dict(w_in=m_w_in, shift_mu=m_shift_mu, w_decay0=m_w_decay0, w_decay_up=m_w_decay_up, a0=m_a0, a_up=m_a_up,
                k_k=m_k_k, k_a=m_k_a, r_k=m_r_k, ln_x_w=m_ln_x_w, ln_x_b=m_ln_x_b, v_mix0=m_v_mix0,
                v_mix_down=m_v_mix_down, v_mix_up=m_v_mix_up, lb_logits=m_lb_logits, g_norm_w=m_g_norm_w,
                w_out=m_w_out, ln_w=m_ln_w, ln_b=m_ln_b)
    mom2 = dict(w_in=v_w_in, shift_mu=v_shift_mu, w_decay0=v_w_decay0, w_decay_up=v_w_decay_up, a0=v_a0, a_up=v_a_up,
                k_k=v_k_k, k_a=v_k_a, r_k=v_r_k, ln_x_w=v_ln_x_w, ln_x_b=v_ln_x_b, v_mix0=v_v_mix0,
                v_mix_down=v_v_mix_down, v_mix_up=v_v_mix_up, lb_logits=v_lb_logits, g_norm_w=v_g_norm_w,
                w_out=v_w_out, ln_w=v_ln_w, ln_b=v_ln_b)
    assert x.shape[0] == 1 and w_in.shape[0] == DEPTH
    t, d = x.shape[1], x.shape[2]
    dr = w_decay0.shape[1]
    dh = g_norm_w.shape[1]
    rank_w, rank_a, rank_v = w_decay_up.shape[1], a_up.shape[1], v_mix_up.shape[1]
    rwc = 4 * dr + rank_w + rank_a
    assert rank_w + rank_a == LANES and rank_v <= LANES and dr + dh == d
    assert t % CHUNK == 0 and dr % LANES == 0 and dh % LANES == 0 and shift_mu.shape[1] == rwc
    n_pair = dr // LANES
    me = _index(_position())

    win_t = [_all_gather_rows(w_in[l].T.astype(bf16), f"ag_w_in_{l}") for l in range(DEPTH)]
    wout = [_all_gather_rows(w_out[l].astype(bf16), f"ag_w_out_{l}") for l in range(DEPTH)]
    shard = dr // N_DEV
    pack = jnp.concatenate([w_decay_up[0], w_decay_up[1], a_up[0], a_up[1], v_mix_up[0], v_mix_down[0].T], axis=0)
    pack = _all_gather_rows(pack, "ag_small")
    pack = jnp.transpose(pack.reshape(N_DEV, -1, shard), (1, 0, 2)).reshape(-1, dr)
    offs = [0, rank_w, 2 * rank_w, 2 * rank_w + rank_a, 2 * rank_w + 2 * rank_a, 2 * rank_w + 2 * rank_a + rank_v,
            2 * rank_w + 2 * rank_a + 2 * rank_v]
    wdu_f = [pack[offs[0]:offs[1]], pack[offs[1]:offs[2]]]
    aup_f = [pack[offs[2]:offs[3]], pack[offs[3]:offs[4]]]
    vup_f = pack[offs[4]:offs[5]]
    vdown_f = pack[offs[5]:offs[6]].T

    def rwkv_params(l):
        prm = [shift_mu[l:l + 1], w_decay0[l:l + 1], a0[l:l + 1], _pad_rows(wdu_f[l], LANES, True),
               _pad_rows(aup_f[l], LANES, False)]
        if l == 1:
            prm += [v_mix0[0:1], _pad_rows(vdown_f.T, LANES, True).T, _pad_rows(vup_f, LANES, True)]
        rows = jnp.stack([k_k[l], k_a[l], r_k[l], ln_x_w[l], ln_x_b[l]] + [jnp.zeros((dr,), f32)] * 3, axis=0)
        pp = jnp.transpose(rows.reshape(8, n_pair, LANES), (1, 0, 2))
        return tuple(prm), pp

    h = x[0]
    tgt = loss_target[0]
    saved = []
    vfirst = None
    for l in range(DEPTH):
        prm, pp = rwkv_params(l)
        proj = _matmul(h, win_t[l], "nt", f"mm_proj_{l}")
        if l == 0:
            cat, vfirst, mck = _rwkv_fwd(False, proj, None, prm, pp, d)
        else:
            cat, mck = _rwkv_fwd(True, proj, vfirst, prm, pp, d)
        cat, sck = _hgrn_fwd(l == 1, proj, lb_logits, g_norm_w[l:l + 1], cat, rwc)
        y = _matmul(cat, wout[l], "nn", f"mm_out_{l}")
        saved.append((h, proj, prm, pp, mck, sck, cat, y))
        if l < DEPTH - 1:
            h = _ln_fwd(h, y, ln_w[l:l + 1], ln_b[l:l + 1])
        else:
            dh_out, loss_part = _ln_loss(h, y, ln_w[l:l + 1], ln_b[l:l + 1], tgt)
    loss = lax.psum(loss_part[0, 0], ("x", "y", "c"))

    grads = {}
    big = {}
    dvfirst = None
    d_lbl = None
    for l in reversed(range(DEPTH)):
        h_l, proj, prm, pp, mck, sck, cat, y = saved[l]
        dy, g_ln_w, g_ln_b = _ln_bwd(h_l, y, ln_w[l:l + 1], ln_b[l:l + 1], dh_out)
        dcat = _matmul(dy, wout[l], "nt", f"mm_dcat_{l}")
        big[("w_out", l)] = _matmul(cat, dy, "tn", f"mm_dwout_{l}", out_dtype=bf16)
        if l == 1:
            outs = _rwkv_bwd(True, proj, vfirst, prm, pp, mck, dcat, None)
            dproj_r, dvfirst = outs[0], outs[1]
            dprm, dpp = outs[2:-1], outs[-1]
        else:
            outs = _rwkv_bwd(False, proj, None, prm, pp, mck, dcat, dvfirst)
            dproj_r = outs[0]
            dprm, dpp = outs[1:-1], outs[-1]
        dproj_h, dlbl_l, dgnw = _hgrn_bwd(l == 1, proj, lb_logits, g_norm_w[l:l + 1], sck, dcat, rwc)
        dproj = jnp.concatenate([dproj_r] + [dproj_h[i] for i in range(4)], axis=1)
        dh_out = _matmul(dproj, win_t[l], "nn", f"mm_dh_{l}", add=dy, add_scale=ALPHA)
        big[("w_in", l)] = _matmul(dproj, h_l, "tn", f"mm_dwin_{l}", out_dtype=bf16)
        dpp = jnp.transpose(dpp, (1, 0, 2)).reshape(8, dr)
        grads[l] = dict(shift_mu=dprm[0][0], w_decay0=dprm[1][0], a0=dprm[2][0], w_decay_up=dprm[3][:rank_w],
                        a_up=dprm[4][rank_w:], k_k=dpp[0], k_a=dpp[1], r_k=dpp[2], ln_x_w=dpp[3], ln_x_b=dpp[4],
                        g_norm_w=dgnw[0], ln_w=g_ln_w[0], ln_b=g_ln_b[0])
        if l == 1:
            grads[l].update(v_mix0=dprm[5][0], v_mix_down=dprm[6][:, :rank_v], v_mix_up=dprm[7][:rank_v])
            d_lbl = dlbl_l
    grad_x = dh_out[None]

    reduced = [_reduce_scatter([big[("w_in", l)], big[("w_out", l)]], f"l{l}") for l in range(DEPTH)]
    g_w_in = jnp.stack([reduced[l][0].T for l in range(DEPTH)])
    g_w_out = jnp.stack([reduced[l][1] for l in range(DEPTH)])

    def both(name):
        return jnp.stack([grads[0][name], grads[1][name]])

    small = dict(shift_mu=both("shift_mu"), w_decay0=both("w_decay0"), a0=both("a0"), k_k=both("k_k"), k_a=both("k_a"),
                 r_k=both("r_k"), ln_x_w=both("ln_x_w"), ln_x_b=both("ln_x_b"), v_mix0=grads[1]["v_mix0"][None],
                 lb_logits=d_lbl, g_norm_w=both("g_norm_w"), ln_w=both("ln_w"), ln_b=both("ln_b"),
                 w_decay_up=both("w_decay_up"), a_up=both("a_up"), v_mix_down=grads[1]["v_mix_down"][None],
                 v_mix_up=grads[1]["v_mix_up"][None])
    flat = jnp.concatenate([small[nm].reshape(-1) for nm in _SMALL])
    n_flat = flat.shape[0]
    rows = -(-n_flat // (8 * LANES)) * 8
    flat = jnp.concatenate([flat, jnp.zeros((rows * LANES - n_flat,), f32)]).reshape(rows, LANES)
    total = _sum_slots(_all_gather_rows(flat, "ag_small_grads").reshape(N_DEV, rows, LANES), "sum_small_grads").reshape(-1)
    gsm = {}
    off = 0
    for nm in _SMALL:
        size = small[nm].size
        gsm[nm] = total[off:off + size].reshape(small[nm].shape)
        off += size
    gsm["w_decay_up"] = lax.dynamic_slice_in_dim(gsm["w_decay_up"], me * shard, shard, axis=2)
    gsm["a_up"] = lax.dynamic_slice_in_dim(gsm["a_up"], me * shard, shard, axis=2)
    gsm["v_mix_up"] = lax.dynamic_slice_in_dim(gsm["v_mix_up"], me * shard, shard, axis=2)
    gsm["v_mix_down"] = lax.dynamic_slice_in_dim(gsm["v_mix_down"], me * shard, shard, axis=1)
    gsm["w_in"] = g_w_in
    gsm["w_out"] = g_w_out

    deltas, new_m, new_v = {}, {}, {}
    for nm in _NAMES:
        deltas[nm], new_m[nm], new_v[nm] = _adamw(weights[nm], gsm[nm], mom1[nm], mom2[nm], "adamw_" + nm)
    return (loss, grad_x, *[gsm[nm] for nm in _NAMES], *[deltas[nm] for nm in _NAMES],
            *[new_m[nm] for nm in _NAMES], *[new_v[nm] for nm in _NAMES])
```

```python
import functools

import jax
import jax.numpy as jnp
from jax import lax
from jax.experimental import pallas as pl
from jax.experimental.pallas import tpu as pltpu

f32 = jnp.float32
bf16 = jnp.bfloat16

N_DEV = 8
CHUNK = 64
LANES = 128
RWKV_HEAD = 64
DEPTH = 2
ALPHA = (2 * DEPTH) ** 0.25
LN_EPS = 1e-5
GN_EPS = 64e-5
RMS_EPS = 1e-5
LB_FLOOR = 1e-30
ADAM_LR, ADAM_B1, ADAM_B2, ADAM_EPS, ADAM_WD, ADAM_STEP = 0.001, 0.9, 0.999, 1e-08, 0.01, 10
MESH = pl.DeviceIdType.MESH


def _iota(shape, d):
    return lax.broadcasted_iota(jnp.int32, shape, d)


_DIMS = {"nn": (((1,), (0,)), ((), ())), "nt": (((1,), (1,)), ((), ())), "tn": (((0,), (0,)), ((), ()))}


def _mxu(a, b, mode):
    return lax.dot_general(a, b, _DIMS[mode], preferred_element_type=f32)


def _split(x):
    hi = x.astype(bf16)
    return hi, (x - hi.astype(f32)).astype(bf16)


def _mm3_impl(a, b, mode):
    ah, al = _split(a)
    bh, bl = _split(b)
    return _mxu(ah, bh, mode) + (_mxu(ah, bl, mode) + _mxu(al, bh, mode))


@functools.partial(jax.custom_vjp, nondiff_argnums=(2,))
def _mm3(a, b, mode):
    return _mm3_impl(a, b, mode)


def _mm3_fwd(a, b, mode):
    return _mm3_impl(a, b, mode), (a, b)


def _mm3_bwd(mode, res, g):
    a, b = res
    if mode == "nn":
        return _mm3_impl(g, b, "nt"), _mm3_impl(a, g, "tn")
    if mode == "nt":
        return _mm3_impl(g, b, "nn"), _mm3_impl(g, a, "tn")
    return _mm3_impl(b, g, "nt"), _mm3_impl(a, g, "nn")


_mm3.defvjp(_mm3_fwd, _mm3_bwd)


def _const_impl(cm, x, mode):
    hi, lo = _split(x)
    if mode == "r":
        return _mxu(hi, cm, "nn") + _mxu(lo, cm, "nn")
    if mode == "rt":
        return _mxu(hi, cm, "nt") + _mxu(lo, cm, "nt")
    return _mxu(cm, hi, mode) + _mxu(cm, lo, mode)


@jax.custom_vjp
def _const_left(cm, x):
    return _const_impl(cm, x, "nn")


_const_left.defvjp(lambda cm, x: (_const_impl(cm, x, "nn"), cm),
                   lambda cm, g: (jnp.zeros_like(cm), _const_impl(cm, g, "tn")))


@jax.custom_vjp
def _const_right(x, cm):
    return _const_impl(cm, x, "r")


_const_right.defvjp(lambda x, cm: (_const_impl(cm, x, "r"), cm),
                    lambda cm, g: (_const_impl(cm, g, "rt"), jnp.zeros_like(cm)))


def _tri_inv(a):
    n = a.shape[0]
    tm = (_iota((n, n), 0) == _iota((n, n), 1)).astype(f32) + a
    ak = a
    for _ in range(5):
        ak = _mm3_impl(ak, ak, "nn")
        tm = tm + _mm3_impl(tm, ak, "nn")
    return tm


@jax.custom_vjp
def _tri_solve(a, x):
    return _mm3_impl(_tri_inv(a), x, "nn")


def _tri_solve_fwd(a, x):
    tm = _tri_inv(a)
    u = _mm3_impl(tm, x, "nn")
    return u, (tm, u)


def _tri_solve_bwd(res, du):
    tm, u = res
    dx = _mm3_impl(tm, du, "tn")
    return _mm3_impl(dx, u, "nt"), dx


_tri_solve.defvjp(_tri_solve_fwd, _tri_solve_bwd)


def _col_of_row(row_vec):
    n = row_vec.shape[1]
    eye = _iota((n, n), 0) == _iota((n, n), 1)
    return jnp.sum(jnp.where(eye, jnp.broadcast_to(row_vec, (n, n)), 0.0), axis=1, keepdims=True)


def _softplus(x):
    return jnp.maximum(x, 0.0) + jnp.log1p(jnp.exp(-jnp.abs(x)))


def _log_sigmoid(x):
    return -_softplus(-x)


def _logaddexp(a, b):
    return jnp.maximum(a, b) + jnp.log1p(jnp.exp(-jnp.abs(a - b)))


def _silu(x):
    return x * jax.nn.sigmoid(x)


def _tril(c, strict):
    r, s = _iota((c, c), 0), _iota((c, c), 1)
    return (r > s) if strict else (r >= s)


def _last_row(a):
    c = a.shape[0]
    return jnp.sum(jnp.where(_iota(a.shape, 0) == c - 1, a, 0.0), axis=0, keepdims=True)


def _rwkv_pre(layer1, prm, y, prev, vf):
    c = y.shape[0]
    if layer1:
        mu, w0, a0, wup, aup, v0, vdown, vup = prm
    else:
        mu, w0, a0, wup, aup = prm
    dr = w0.shape[1]
    shift = (_iota((c, c), 0) == _iota((c, c), 1) + 1).astype(bf16)
    y_prev = _const_left(shift, y) + jnp.where(_iota((c, 1), 0) == 0, prev, 0.0)
    rw = y + mu * (y_prev - y)
    r, k, v, z = (rw[:, i * dr:(i + 1) * dr] for i in range(4))
    wdad = rw[:, 4 * dr:4 * dr + LANES]
    w_raw = w0 + _mm3(jnp.tanh(wdad), wup, "nn")
    lw = -jnp.exp(-_softplus(-w_raw) - 0.5)
    asig = jax.nn.sigmoid(a0 + _mm3(wdad, aup, "nn"))
    if layer1:
        v = v + (vf - v) * jax.nn.sigmoid(v0 + _mm3(_mm3(v, vdown, "nn"), vup, "nn"))
    return r, k, v, z, lw, asig


def _rwkv_pair(pp, m0, xs):
    kkw, kaw, rkw, gnw, gnb = pp
    r, k, v, z, lw, asig = xs
    c = r.shape[0]
    n2 = 2 * c
    lane = _iota((1, LANES), 1)
    mh0, mh1 = (lane < RWKV_HEAD).astype(f32), (lane >= RWKV_HEAD).astype(f32)
    same_head = _iota((LANES, LANES), 0) // RWKV_HEAD == _iota((LANES, LANES), 1) // RWKV_HEAD
    g = same_head.astype(bf16)

    def seg(x):
        return _const_right(x, g)

    def stack(x):
        return jnp.concatenate([x * mh0, x * mh1], axis=0)

    kk = k * kkw
    kk = kk / jnp.maximum(jnp.sqrt(seg(kk * kk)), 1e-12)
    k2 = k * (1.0 + (asig - 1.0) * kaw)
    a = -kk
    b = kk * asig
    cum = _const_left(_tril(c, False).astype(bf16), lw)
    at = stack(a * jnp.exp(cum - lw))
    rt = stack(r * jnp.exp(cum))
    en = jnp.exp(-cum)
    sc = _mm3(jnp.concatenate([at, rt], axis=0), jnp.concatenate([stack(b * en), stack(k2 * en)], axis=0), "nt")
    row, col = _iota((n2, n2), 0), _iota((n2, n2), 1)
    same = row // c == col // c
    strict = same & (row % c > col % c)
    incl = same & (row % c >= col % c)
    aab = jnp.where(strict, sc[:n2, :n2], 0.0)
    aak = jnp.where(strict, sc[:n2, n2:], 0.0)
    arb = jnp.where(incl, sc[n2:, :n2], 0.0)
    ark = jnp.where(incl, sc[n2:, n2:], 0.0)
    vv = jnp.concatenate([v, v], axis=0)
    mask_st = jnp.concatenate([jnp.broadcast_to(mh0, (c, LANES)), jnp.broadcast_to(mh1, (c, LANES))], axis=0)
    x_st = _mm3(jnp.concatenate([at, aak], axis=1), jnp.concatenate([m0, vv], axis=0), "nn")
    u_st = _tri_solve(aab, x_st) * mask_st
    o_st = _mm3(jnp.concatenate([rt, arb, ark], axis=1), jnp.concatenate([m0, u_st, vv], axis=0), "nn") * mask_st
    u = u_st[:c] + u_st[c:]
    o = o_st[:c] + o_st[c:]
    cum_last = _last_row(cum)
    dec_end = jnp.exp(cum_last - cum)
    m_new = _col_of_row(jnp.exp(cum_last)) * m0 + _mm3(
        jnp.concatenate([b * dec_end, k2 * dec_end], axis=0), jnp.concatenate([u, v], axis=0), "tn") * same_head.astype(f32)
    mean = seg(o) * (1.0 / RWKV_HEAD)
    d = o - mean
    var = seg(d * d) * (1.0 / RWKV_HEAD)
    on = d * lax.rsqrt(var + GN_EPS) * gnw + gnb
    bonus = seg(r * k2 * rkw) * v
    return (on + bonus) * _silu(z), m_new


def _split_lanes(a, n):
    return [a[:, i * LANES:(i + 1) * LANES] for i in range(n)]


def _rwkv_specs(layer1, t, dr, rwc, n_pair, rev):
    nc = t // CHUNK

    def cidx(c):
        return (nc - 1 - c) if rev else c

    full = lambda shape: pl.BlockSpec(shape, lambda c, p: tuple(0 for _ in shape))
    specs = [
        pl.BlockSpec((CHUNK, rwc), lambda c, p: (cidx(c), 0)),
        pl.BlockSpec((8, rwc), lambda c, p: (jnp.maximum(cidx(c) * (CHUNK // 8) - 1, 0), 0)),
    ]
    if layer1:
        specs.append(pl.BlockSpec((CHUNK, dr), lambda c, p: (cidx(c), 0)))
    prm_shapes = [(1, rwc), (1, dr), (1, dr), (LANES, dr), (LANES, dr)]
    if layer1:
        prm_shapes += [(1, dr), (dr, LANES), (LANES, dr)]
    specs += [full(s) for s in prm_shapes]
    specs.append(pl.BlockSpec((1, 8, LANES), lambda c, p: (p, 0, 0)))
    return specs, prm_shapes, cidx, full


def _rwkv_fwd(layer1, proj, vf, prm, pp, cat_width):
    t = proj.shape[0]
    dr = prm[1].shape[1]
    rwc = prm[0].shape[1]
    n_pair = dr // LANES
    nc = t // CHUNK
    n_prm = len(prm)
    specs, _, _, _ = _rwkv_specs(layer1, t, dr, rwc, n_pair, False)

    def body(*refs):
        y_ref, prev_ref = refs[0], refs[1]
        i = 2
        vf_ref = None
        if layer1:
            vf_ref = refs[i]
            i += 1
        prm_refs = refs[i:i + n_prm]
        i += n_prm
        pp_ref = refs[i]
        i += 1
        cat_ref = refs[i]
        i += 1
        vout_ref = None
        if not layer1:
            vout_ref = refs[i]
            i += 1
        mck_ref, x_s, m_s = refs[i], refs[i + 1], refs[i + 2]
        c, p = pl.program_id(0), pl.program_id(1)

        @pl.when((c == 0) & (p == 0))
        def _():
            m_s[...] = jnp.zeros_like(m_s)

        @pl.when(p == 0)
        def _():
            prev = prev_ref[pl.ds(7, 1), :] * (c != 0).astype(f32)
            xs = _rwkv_pre(layer1, tuple(r[...] for r in prm_refs), y_ref[...], prev,
                           vf_ref[...] if layer1 else None)
            for q, a in enumerate(xs):
                for j, piece in enumerate(_split_lanes(a, n_pair)):
                    x_s[q * n_pair + j] = piece
            if not layer1:
                vout_ref[...] = xs[2]

        m0 = m_s[p]
        mck_ref[0, 0] = m0
        ppv = tuple(pp_ref[0, pl.ds(q, 1), :] for q in range(5))
        og, m_new = _rwkv_pair(ppv, m0, tuple(x_s[q * n_pair + p] for q in range(6)))
        cat_ref[...] = og
        m_s[p] = m_new

    out_shape = [jax.ShapeDtypeStruct((t, cat_width), f32)]
    out_specs = [pl.BlockSpec((CHUNK, LANES), lambda c, p: (c, p))]
    if not layer1:
        out_shape.append(jax.ShapeDtypeStruct((t, dr), f32))
        out_specs.append(pl.BlockSpec((CHUNK, dr), lambda c, p: (c, 0)))
    out_shape.append(jax.ShapeDtypeStruct((nc, n_pair, LANES, LANES), f32))
    out_specs.append(pl.BlockSpec((1, 1, LANES, LANES), lambda c, p: (c, p, 0, 0)))
    args = [proj, proj] + ([vf] if layer1 else []) + list(prm) + [pp]
    return pl.pallas_call(
        body, grid=(nc, n_pair), in_specs=specs, out_specs=out_specs, out_shape=out_shape,
        scratch_shapes=[pltpu.VMEM((6 * n_pair, CHUNK, LANES), f32), pltpu.VMEM((n_pair, LANES, LANES), f32)],
        compiler_params=pltpu.CompilerParams(dimension_semantics=("arbitrary", "arbitrary")),
        name=f"rwkv_fwd_l{int(layer1)}",
    )(*args)


def _rwkv_bwd(layer1, proj, vf, prm, pp, mck, dcat, dvout):
    t = proj.shape[0]
    dr = prm[1].shape[1]
    rwc = prm[0].shape[1]
    n_pair = dr // LANES
    nc = t // CHUNK
    n_prm = len(prm)
    specs, prm_shapes, cidx, full = _rwkv_specs(layer1, t, dr, rwc, n_pair, True)
    specs.append(pl.BlockSpec((1, 1, LANES, LANES), lambda c, p: (cidx(c), p, 0, 0)))
    specs.append(pl.BlockSpec((CHUNK, LANES), lambda c, p: (cidx(c), p)))
    if not layer1:
        specs.append(pl.BlockSpec((CHUNK, dr), lambda c, p: (cidx(c), 0)))

    def body(*refs):
        y_ref, prev_ref = refs[0], refs[1]
        i = 2
        vf_ref = None
        if layer1:
            vf_ref = refs[i]
            i += 1
        prm_refs = refs[i:i + n_prm]
        i += n_prm
        pp_ref, mck_ref, dog_ref = refs[i], refs[i + 1], refs[i + 2]
        i += 3
        dvout_ref = None
        if not layer1:
            dvout_ref = refs[i]
            i += 1
        dy_ref = refs[i]
        i += 1
        dvf_ref = None
        if layer1:
            dvf_ref = refs[i]
            i += 1
        dprm_refs = refs[i:i + n_prm]
        i += n_prm
        dpp_ref = refs[i]
        x_s, dx_s, dm_s, dprev_s = refs[i + 1:i + 5]
        c, p = pl.program_id(0), pl.program_id(1)
        cr = nc - 1 - c

        def prev_row():
            return prev_ref[pl.ds(7, 1), :] * (cr != 0).astype(f32)

        @pl.when((c == 0) & (p == 0))
        def _():
            dm_s[...] = jnp.zeros_like(dm_s)
            dprev_s[...] = jnp.zeros_like(dprev_s)
            dpp_ref[...] = jnp.zeros_like(dpp_ref)
            for r in dprm_refs:
                r[...] = jnp.zeros_like(r)

        @pl.when(p == 0)
        def _():
            xs = _rwkv_pre(layer1, tuple(r[...] for r in prm_refs), y_ref[...], prev_row(),
                           vf_ref[...] if layer1 else None)
            for q, a in enumerate(xs):
                for j, piece in enumerate(_split_lanes(a, n_pair)):
                    x_s[q * n_pair + j] = piece

        ppv = tuple(pp_ref[0, pl.ds(q, 1), :] for q in range(5))
        xs_p = tuple(x_s[q * n_pair + p] for q in range(6))
        _, vjp_pair = jax.vjp(_rwkv_pair, ppv, mck_ref[0, 0], xs_p)
        dppv, dm0, dxs = vjp_pair((dog_ref[...], dm_s[p]))
        dm_s[p] = dm0
        for q in range(6):
            dx_s[q * n_pair + p] = dxs[q]
        for q in range(5):
            dpp_ref[p, pl.ds(q, 1), :] += dppv[q]

        @pl.when(p == n_pair - 1)
        def _():
            dxs_full = [jnp.concatenate([dx_s[q * n_pair + j] for j in range(n_pair)], axis=1) for q in range(6)]
            if not layer1:
                dxs_full[2] = dxs_full[2] + dvout_ref[...]
            prm_v = tuple(r[...] for r in prm_refs)
            if layer1:
                _, vjp_pre = jax.vjp(functools.partial(_rwkv_pre, True), prm_v, y_ref[...], prev_row(), vf_ref[...])
                dprm, dy, dprev, dvf = vjp_pre(tuple(dxs_full))
                dvf_ref[...] = dvf
            else:
                _, vjp_pre = jax.vjp(lambda a, b, d: _rwkv_pre(False, a, b, d, None), prm_v, y_ref[...], prev_row())
                dprm, dy, dprev = vjp_pre(tuple(dxs_full))
            dy_ref[...] = dy + jnp.where(_iota((CHUNK, 1), 0) == CHUNK - 1, dprev_s[...], 0.0)
            dprev_s[...] = dprev
            for r, gval in zip(dprm_refs, dprm):
                r[...] += gval

    out_shape = [jax.ShapeDtypeStruct((t, rwc), f32)]
    out_specs = [pl.BlockSpec((CHUNK, rwc), lambda c, p: (cidx(c), 0))]
    if layer1:
        out_shape.append(jax.ShapeDtypeStruct((t, dr), f32))
        out_specs.append(pl.BlockSpec((CHUNK, dr), lambda c, p: (cidx(c), 0)))
    out_shape += [jax.ShapeDtypeStruct(s, f32) for s in prm_shapes]
    out_specs += [full(s) for s in prm_shapes]
    out_shape.append(jax.ShapeDtypeStruct((n_pair, 8, LANES), f32))
    out_specs.append(full((n_pair, 8, LANES)))
    args = [proj, proj] + ([vf] if layer1 else []) + list(prm) + [pp, mck, dcat] + ([] if layer1 else [dvout])
    return pl.pallas_call(
        body, grid=(nc, n_pair), in_specs=specs, out_specs=out_specs, out_shape=out_shape,
        scratch_shapes=[pltpu.VMEM((6 * n_pair, CHUNK, LANES), f32), pltpu.VMEM((6 * n_pair, CHUNK, LANES), f32),
                        pltpu.VMEM((n_pair, LANES, LANES), f32), pltpu.VMEM((1, rwc), f32)],
        compiler_params=pltpu.CompilerParams(dimension_semantics=("arbitrary", "arbitrary")),
        name=f"rwkv_bwd_l{int(layer1)}",
    )(*args)


def _hgrn_chunk(layer1, lbl, gnw, s0, q_raw, f_raw, i_in, z):
    c = q_raw.shape[0]
    q = _silu(q_raw)
    ls = _log_sigmoid(f_raw)
    if layer1:
        l0, l1 = lbl[0:1, :], lbl[1:2, :]
        mx = jnp.maximum(l0, l1)
        e0, e1 = jnp.exp(l0 - mx), jnp.exp(l1 - mx)
        sm0, sm1 = e0 / (e0 + e1), e1 / (e0 + e1)
        lb = (sm0 + sm1) - sm0
        log_f = _logaddexp(jnp.log(jnp.maximum(lb, LB_FLOOR)), jnp.log1p(-lb) + ls)
        k = (1.0 - lb) * jax.nn.sigmoid(-f_raw)
    else:
        log_f = _logaddexp(jnp.full_like(ls, jnp.log(jnp.float32(LB_FLOOR))), ls)
        k = jax.nn.sigmoid(-f_raw)
    row, col = _iota((c, c), 0), _iota((c, c), 1)
    trow = _iota((c, 1), 0)
    halves = []
    half = c // 2
    while half >= 1:
        halves.append(half)
        half //= 2
    cmat = jnp.concatenate([(col <= row).astype(f32)]
                           + [(col <= (row // (2 * hf)) * (2 * hf) + hf - 1).astype(f32) for hf in halves], axis=0)
    ball = _const_left(cmat.astype(bf16), log_f)
    b = ball[:c]
    att = jnp.zeros((c, c), f32)
    for lvl, hf in enumerate(halves):
        blk = 2 * hf
        bref = ball[(lvl + 1) * c:(lvl + 2) * c]
        upper = (trow % blk) >= hf
        qh = q * jnp.exp(jnp.where(upper, b - bref, 0.0)) * upper.astype(f32)
        kh = k * jnp.exp(jnp.where(upper, 0.0, bref - b)) * (1.0 - upper.astype(f32))
        att = att + jnp.where(row // blk == col // blk, _mm3(qh, kh, "nt"), 0.0)
    o = (_mm3(jnp.concatenate([q * jnp.exp(b), att], axis=1), jnp.concatenate([s0, i_in], axis=0), "nn")
         + jnp.sum(q * k, axis=-1, keepdims=True) * i_in)
    b_last = _last_row(b)
    s_new = _col_of_row(jnp.exp(b_last)) * s0 + _mm3(k * jnp.exp(b_last - b), i_in, "tn")
    o = o * lax.rsqrt(jnp.mean(o * o, axis=-1, keepdims=True) + RMS_EPS)
    return o * gnw * _silu(z), s_new


def _hgrn_in_specs(t, dh, col0, rev):
    nc = t // CHUNK
    nh = dh // LANES

    def cidx(c):
        return (nc - 1 - c) if rev else c

    specs = [pl.BlockSpec((CHUNK, LANES), functools.partial(lambda g, h, c: (cidx(c), col0 + g * nh + h), g))
             for g in range(4)]
    specs.append(pl.BlockSpec((2, LANES), lambda h, c: (0, h)))
    specs.append(pl.BlockSpec((1, LANES), lambda h, c: (0, h)))
    return specs, cidx


def _hgrn_fwd(layer1, proj, lbl, gnw, cat, rwc):
    t, d = cat.shape
    dh = gnw.shape[1]
    nh = dh // LANES
    nc = t // CHUNK
    col0 = rwc // LANES
    specs, _ = _hgrn_in_specs(t, dh, col0, False)
    specs.append(pl.BlockSpec(memory_space=pl.ANY))
    cat_col0 = (d - dh) // LANES

    def body(q_ref, f_ref, i_ref, z_ref, lbl_ref, gnw_ref, cat_in, cat_ref, sck_ref, s_s):
        del cat_in
        c = pl.program_id(1)

        @pl.when(c == 0)
        def _():
            s_s[...] = jnp.zeros_like(s_s)

        s0 = s_s[...]
        sck_ref[0, 0] = s0
        out, s_new = _hgrn_chunk(layer1, lbl_ref[...], gnw_ref[...], s0, q_ref[...], f_ref[...], i_ref[...], z_ref[...])
        cat_ref[...] = out
        s_s[...] = s_new

    return pl.pallas_call(
        body, grid=(nh, nc), in_specs=specs,
        out_specs=[pl.BlockSpec((CHUNK, LANES), lambda h, c: (c, cat_col0 + h)),
                   pl.BlockSpec((1, 1, LANES, LANES), lambda h, c: (h, c, 0, 0))],
        out_shape=[jax.ShapeDtypeStruct((t, d), f32), jax.ShapeDtypeStruct((nh, nc, LANES, LANES), f32)],
        scratch_shapes=[pltpu.VMEM((LANES, LANES), f32)],
        input_output_aliases={6: 0},
        compiler_params=pltpu.CompilerParams(dimension_semantics=("arbitrary", "arbitrary")),
        name=f"hgrn_fwd_l{int(layer1)}",
    )(proj, proj, proj, proj, lbl, gnw, cat)


def _hgrn_bwd(layer1, proj, lbl, gnw, sck, dcat, rwc):
    t, d = dcat.shape
    dh = gnw.shape[1]
    nh = dh // LANES
    nc = t // CHUNK
    col0 = rwc // LANES
    specs, cidx = _hgrn_in_specs(t, dh, col0, True)
    cat_col0 = (d - dh) // LANES
    specs.append(pl.BlockSpec((1, 1, LANES, LANES), lambda h, c: (h, cidx(c), 0, 0)))
    specs.append(pl.BlockSpec((CHUNK, LANES), lambda h, c: (cidx(c), cat_col0 + h)))

    def body(q_ref, f_ref, i_ref, z_ref, lbl_ref, gnw_ref, sck_ref, do_ref, dp_ref, dlbl_ref, dgnw_ref, ds_s):
        c = pl.program_id(1)

        @pl.when(c == 0)
        def _():
            ds_s[...] = jnp.zeros_like(ds_s)
            dlbl_ref[...] = jnp.zeros_like(dlbl_ref)
            dgnw_ref[...] = jnp.zeros_like(dgnw_ref)

        _, vjp = jax.vjp(functools.partial(_hgrn_chunk, layer1), lbl_ref[...], gnw_ref[...], sck_ref[0, 0],
                         q_ref[...], f_ref[...], i_ref[...], z_ref[...])
        dlbl, dgnw, ds0, dq, df, di, dz = vjp((do_ref[...], ds_s[...]))
        ds_s[...] = ds0
        dlbl_ref[...] += dlbl
        dgnw_ref[...] += dgnw
        dp_ref[0] = dq
        dp_ref[1] = df
        dp_ref[2] = di
        dp_ref[3] = dz

    return pl.pallas_call(
        body, grid=(nh, nc), in_specs=specs,
        out_specs=[pl.BlockSpec((4, CHUNK, LANES), lambda h, c: (0, cidx(c), h)),
                   pl.BlockSpec((2, LANES), lambda h, c: (0, h)),
                   pl.BlockSpec((1, LANES), lambda h, c: (0, h))],
        out_shape=[jax.ShapeDtypeStruct((4, t, dh), f32), jax.ShapeDtypeStruct((2, dh), f32),
                   jax.ShapeDtypeStruct((1, dh), f32)],
        scratch_shapes=[pltpu.VMEM((LANES, LANES), f32)],
        compiler_params=pltpu.CompilerParams(dimension_semantics=("arbitrary", "arbitrary")),
        name=f"hgrn_bwd_l{int(layer1)}",
    )(proj, proj, proj, proj, lbl, gnw, sck, dcat)


def _ln(h, y, w, b):
    u = ALPHA * h + y
    mu = jnp.mean(u, axis=-1, keepdims=True)
    var = jnp.mean(jnp.square(u - mu), axis=-1, keepdims=True)
    return (u - mu) * lax.rsqrt(var + LN_EPS) * w + b


def _row_tile(t):
    return 256 if t % 256 == 0 else t


def _ln_fwd(h, y, w, b):
    t, d = h.shape
    tr = _row_tile(t)

    def body(h_ref, y_ref, w_ref, b_ref, o_ref):
        o_ref[...] = _ln(h_ref[...], y_ref[...], w_ref[...], b_ref[...])

    row = pl.BlockSpec((tr, d), lambda i: (i, 0))
    vec = pl.BlockSpec((1, d), lambda i: (0, 0))
    return pl.pallas_call(body, grid=(t // tr,), in_specs=[row, row, vec, vec], out_specs=row,
                          out_shape=jax.ShapeDtypeStruct((t, d), f32), name="ln_fwd")(h, y, w, b)


def _ln_loss(h, y, w, b, tgt):
    t, d = h.shape
    tr = _row_tile(t)

    def body(h_ref, y_ref, w_ref, b_ref, t_ref, g_ref, loss_ref):
        @pl.when(pl.program_id(0) == 0)
        def _():
            loss_ref[...] = jnp.zeros_like(loss_ref)

        err = _ln(h_ref[...], y_ref[...], w_ref[...], b_ref[...]) - t_ref[...]
        g_ref[...] = err * (1.0 / d)
        loss_ref[...] += 0.5 * jnp.sum(jnp.mean(jnp.square(err), axis=-1, keepdims=True), axis=0, keepdims=True)

    row = pl.BlockSpec((tr, d), lambda i: (i, 0))
    vec = pl.BlockSpec((1, d), lambda i: (0, 0))
    return pl.pallas_call(
        body, grid=(t // tr,), in_specs=[row, row, vec, vec, row],
        out_specs=[row, pl.BlockSpec((1, LANES), lambda i: (0, 0))],
        out_shape=[jax.ShapeDtypeStruct((t, d), f32), jax.ShapeDtypeStruct((1, LANES), f32)],
        compiler_params=pltpu.CompilerParams(dimension_semantics=("arbitrary",)), name="ln_loss")(h, y, w, b, tgt)


def _ln_bwd(h, y, w, b, dout):
    t, d = h.shape
    tr = _row_tile(t)

    def body(h_ref, y_ref, w_ref, b_ref, do_ref, dy_ref, dw_ref, db_ref):
        @pl.when(pl.program_id(0) == 0)
        def _():
            dw_ref[...] = jnp.zeros_like(dw_ref)
            db_ref[...] = jnp.zeros_like(db_ref)

        _, vjp = jax.vjp(lambda yy, ww, bb: _ln(h_ref[...], yy, ww, bb), y_ref[...], w_ref[...], b_ref[...])
        dy, dw, db = vjp(do_ref[...])
        dy_ref[...] = dy
        dw_ref[...] += dw
        db_ref[...] += db

    row = pl.BlockSpec((tr, d), lambda i: (i, 0))
    vec = pl.BlockSpec((1, d), lambda i: (0, 0))
    return pl.pallas_call(
        body, grid=(t // tr,), in_specs=[row, row, vec, vec, row], out_specs=[row, vec, vec],
        out_shape=[jax.ShapeDtypeStruct((t, d), f32), jax.ShapeDtypeStruct((1, d), f32), jax.ShapeDtypeStruct((1, d), f32)],
        compiler_params=pltpu.CompilerParams(dimension_semantics=("arbitrary",)), name="ln_bwd")(h, y, w, b, dout)


def _pick(n, prefs):
    for p in prefs:
        if n % p == 0:
            return p
    return n


def _matmul(a, b, mode, name, add=None, add_scale=1.0, out_dtype=f32):
    if mode == "nn":
        (m, k), n = a.shape, b.shape[1]
    elif mode == "nt":
        (m, k), n = a.shape, b.shape[0]
    else:
        (k, m), n = a.shape, b.shape[1]
    tm = _pick(m, (512, 640, 256, 128))
    tn = _pick(n, (1024, 640, 512, 256, 128))
    tk = _pick(k, (2048, 1024, 640, 512, 256, 128))
    nk = k // tk
    dims = {"nn": (((1,), (0,)), ((), ())), "nt": (((1,), (1,)), ((), ())), "tn": (((0,), (0,)), ((), ()))}[mode]

    def body(*refs):
        a_ref, b_ref = refs[0], refs[1]
        add_ref = refs[2] if add is not None else None
        o_ref, acc = refs[-2], refs[-1]
        kk = pl.program_id(2)

        @pl.when(kk == 0)
        def _():
            acc[...] = jnp.zeros_like(acc)

        acc[...] += lax.dot_general(a_ref[...].astype(bf16), b_ref[...].astype(bf16), dims, preferred_element_type=f32)

        @pl.when(kk == nk - 1)
        def _():
            res = acc[...]
            if add is not None:
                res = res + add_scale * add_ref[...]
            o_ref[...] = res.astype(out_dtype)

    a_spec = pl.BlockSpec((tk, tm), lambda i, j, kk: (kk, i)) if mode == "tn" else pl.BlockSpec((tm, tk), lambda i, j, kk: (i, kk))
    b_spec = pl.BlockSpec((tn, tk), lambda i, j, kk: (j, kk)) if mode == "nt" else pl.BlockSpec((tk, tn), lambda i, j, kk: (kk, j))
    o_spec = pl.BlockSpec((tm, tn), lambda i, j, kk: (i, j))
    in_specs = [a_spec, b_spec] + ([o_spec] if add is not None else [])
    args = [a, b] + ([add] if add is not None else [])
    return pl.pallas_call(
        body, grid=(m // tm, n // tn, nk), in_specs=in_specs, out_specs=o_spec,
        out_shape=jax.ShapeDtypeStruct((m, n), out_dtype), scratch_shapes=[pltpu.VMEM((tm, tn), f32)],
        compiler_params=pltpu.CompilerParams(dimension_semantics=("parallel", "parallel", "arbitrary")),
        name=name,
    )(*args)


def _position():
    return lax.axis_index("x"), lax.axis_index("y"), lax.axis_index("c")


def _flip(pos, k):
    x, y, c = pos
    return (1 - x if k & 4 else x, 1 - y if k & 2 else y, 1 - c if k & 1 else c)


def _index(pos):
    return 4 * pos[0] + 2 * pos[1] + pos[2]


def _all_gather_rows(x, name):
    m_per, n = x.shape

    def body(x_ref, out_ref, send_sems, recv_sems, local_sem):
        me = _position()
        sibling = _flip(me, 1)
        chips = (2, 4, 6)

        def rows(pos):
            return out_ref.at[pl.ds(_index(pos) * m_per, m_per), :]

        def copy(sem, block, to, src=None):
            return pltpu.make_async_remote_copy(
                src_ref=rows(block) if src is None else src, dst_ref=rows(block),
                send_sem=send_sems.at[sem], recv_sem=recv_sems.at[sem], device_id=to, device_id_type=MESH)

        mine = pltpu.make_async_copy(x_ref, rows(me), local_sem)
        mine.start()
        first = [copy(0, me, sibling, src=x_ref)]
        first += [copy(1 + j, me, _flip(me, k), src=x_ref) for j, k in enumerate(chips)]
        for cp in first:
            cp.start()
        passed = [copy(4 + j, _flip(me, k), sibling) for j, k in enumerate(chips)]
        for j, k in enumerate(chips):
            copy(1 + j, _flip(me, k), me).wait_recv()
            passed[j].start()
        copy(0, sibling, me).wait_recv()
        for j, k in enumerate(chips):
            copy(4 + j, _flip(sibling, k), me).wait_recv()
        for cp in first + passed:
            cp.wait_send()
        mine.wait()

    return pl.pallas_call(
        body, out_shape=jax.ShapeDtypeStruct((N_DEV * m_per, n), x.dtype),
        in_specs=[pl.BlockSpec(memory_space=pl.ANY)], out_specs=pl.BlockSpec(memory_space=pl.ANY),
        scratch_shapes=[pltpu.SemaphoreType.DMA((7,)), pltpu.SemaphoreType.DMA((7,)), pltpu.SemaphoreType.DMA(())],
        name=name,
    )(x)


def _exchange_siblings(gs, name):
    n_arr = len(gs)

    def body(*refs):
        g_refs, out_refs = refs[:n_arr], refs[n_arr:2 * n_arr]
        send_sems, recv_sems = refs[2 * n_arr:]
        me = _position()
        c = me[2]
        sibling = _flip(me, 1)
        copies = []
        for i, (g_ref, out_ref) in enumerate(zip(g_refs, out_refs)):
            m_per = g_ref.shape[0] // N_DEV
            for q in range(4):
                copies.append(pltpu.make_async_remote_copy(
                    src_ref=g_ref.at[pl.ds((2 * q + 1 - c) * m_per, m_per), :], dst_ref=out_ref.at[q],
                    send_sem=send_sems.at[4 * i + q], recv_sem=recv_sems.at[4 * i + q],
                    device_id=sibling, device_id_type=MESH))
        for cp in copies:
            cp.start()
        for cp in copies:
            cp.wait_recv()
        for cp in copies:
            cp.wait_send()

    anyspec = pl.BlockSpec(memory_space=pl.ANY)
    return pl.pallas_call(
        body, out_shape=[jax.ShapeDtypeStruct((4, g.shape[0] // N_DEV, g.shape[1]), g.dtype) for g in gs],
        in_specs=[anyspec] * n_arr, out_specs=[anyspec] * n_arr,
        scratch_shapes=[pltpu.SemaphoreType.DMA((4 * n_arr,))] * 2, name=name,
    )(*gs)


def _sum_with_sibling(g, recv, name):
    m = g.shape[0] // N_DEV
    n = g.shape[1]
    tr = _pick(m, (208, 128, 64, 32, 16))
    nt = m // tr

    def body(g_ref, r_ref, o_ref):
        c = lax.axis_index("c")
        own = jnp.where(c == 0, g_ref[0, 0].astype(f32), g_ref[0, 1].astype(f32))
        o_ref[...] = (own + r_ref[0].astype(f32)).astype(o_ref.dtype)

    return pl.pallas_call(
        body, grid=(4, nt),
        in_specs=[pl.BlockSpec((1, 2, tr, n), lambda q, i: (q, 0, i, 0)), pl.BlockSpec((1, tr, n), lambda q, i: (q, i, 0))],
        out_specs=pl.BlockSpec((tr, n), lambda q, i: (q * nt + i, 0)),
        out_shape=jax.ShapeDtypeStruct((4 * m, n), bf16), name=name,
    )(g.reshape(4, 2, m, n), recv)


def _exchange_chips(hs, name):
    n_arr = len(hs)
    chips = (2, 4, 6)

    def body(*refs):
        h_refs, out_refs = refs[:n_arr], refs[n_arr:2 * n_arr]
        send_sems, recv_sems = refs[2 * n_arr:]
        me = _position()
        copies = []
        for i, (h_ref, out_ref) in enumerate(zip(h_refs, out_refs)):
            m_per = h_ref.shape[0] // 4
            for j, k in enumerate(chips):
                peer = _flip(me, k)
                peer_q = 2 * peer[0] + peer[1]
                copies.append(pltpu.make_async_remote_copy(
                    src_ref=h_ref.at[pl.ds(peer_q * m_per, m_per), :], dst_ref=out_ref.at[j],
                    send_sem=send_sems.at[3 * i + j], recv_sem=recv_sems.at[3 * i + j],
                    device_id=peer, device_id_type=MESH))
        for cp in copies:
            cp.start()
        for cp in copies:
            cp.wait_recv()
        for cp in copies:
            cp.wait_send()

    anyspec = pl.BlockSpec(memory_space=pl.ANY)
    return pl.pallas_call(
        body, out_shape=[jax.ShapeDtypeStruct((3, h.shape[0] // 4, h.shape[1]), h.dtype) for h in hs],
        in_specs=[anyspec] * n_arr, out_specs=[anyspec] * n_arr,
        scratch_shapes=[pltpu.SemaphoreType.DMA((3 * n_arr,))] * 2, name=name,
    )(*hs)


def _sum_with_chips(h, recv, name):
    m = h.shape[0] // 4
    n = h.shape[1]
    tr = _pick(m, (208, 128, 64, 32, 16))

    def body(h_ref, r_ref, o_ref):
        my_q = 2 * lax.axis_index("x") + lax.axis_index("y")
        own = h_ref[0].astype(f32)
        for q in range(1, 4):
            own = jnp.where(my_q == q, h_ref[q].astype(f32), own)
        o_ref[...] = ((own + r_ref[0].astype(f32)) + r_ref[1].astype(f32)) + r_ref[2].astype(f32)

    return pl.pallas_call(
        body, grid=(m // tr,),
        in_specs=[pl.BlockSpec((4, tr, n), lambda i: (0, i, 0)), pl.BlockSpec((3, tr, n), lambda i: (0, i, 0))],
        out_specs=pl.BlockSpec((tr, n), lambda i: (i, 0)), out_shape=jax.ShapeDtypeStruct((m, n), f32), name=name,
    )(h.reshape(4, m, n), recv)


def _sum_slots(parts, name):
    n_slot, m, n = parts.shape
    tr = _pick(m, (208, 128, 64, 32, 16, 8))

    def body(p_ref, o_ref):
        acc = p_ref[0]
        for s in range(1, n_slot):
            acc = acc + p_ref[s]
        o_ref[...] = acc

    return pl.pallas_call(
        body, grid=(m // tr,), in_specs=[pl.BlockSpec((n_slot, tr, n), lambda i: (0, i, 0))],
        out_specs=pl.BlockSpec((tr, n), lambda i: (i, 0)), out_shape=jax.ShapeDtypeStruct((m, n), parts.dtype), name=name,
    )(parts)


def _reduce_scatter(gs, name):
    from_sibling = _exchange_siblings(gs, "rs_d2d_" + name)
    chip_sums = [_sum_with_sibling(g, r, f"rs_sum2_{name}_{i}") for i, (g, r) in enumerate(zip(gs, from_sibling))]
    from_chips = _exchange_chips(chip_sums, "rs_ici_" + name)
    return [_sum_with_chips(h, r, f"rs_sum4_{name}_{i}") for i, (h, r) in enumerate(zip(chip_sums, from_chips))]


def _adamw(w, g, m, v, name):
    shape = w.shape
    n = shape[-1]
    r = w.size // n
    w2, g2, m2, v2 = (a.reshape(r, n) for a in (w, g, m, v))
    tr = _pick(r, (256, 128, 64, 32, 16, 8)) if r * n > 65536 else r

    def body(w_ref, g_ref, m_ref, v_ref, d_ref, mo_ref, vo_ref):
        gg = g_ref[...]
        mm = ADAM_B1 * m_ref[...] + (1.0 - ADAM_B1) * gg
        vv = ADAM_B2 * v_ref[...] + (1.0 - ADAM_B2) * jnp.square(gg)
        m_hat = mm / (1.0 - ADAM_B1 ** ADAM_STEP)
        v_hat = vv / (1.0 - ADAM_B2 ** ADAM_STEP)
        d_ref[...] = -ADAM_LR * (m_hat / (jnp.sqrt(v_hat) + ADAM_EPS) + ADAM_WD * w_ref[...])
        mo_ref[...] = mm
        vo_ref[...] = vv

    spec = pl.BlockSpec((tr, n), lambda i: (i, 0))
    outs = pl.pallas_call(
        body, grid=(r // tr,), in_specs=[spec] * 4, out_specs=[spec] * 3,
        out_shape=[jax.ShapeDtypeStruct((r, n), f32)] * 3, name=name,
    )(w2, g2, m2, v2)
    return tuple(o.reshape(shape) for o in outs)


_SMALL = ("shift_mu", "w_decay0", "a0", "k_k", "k_a", "r_k", "ln_x_w", "ln_x_b", "v_mix0", "lb_logits",
          "g_norm_w", "ln_w", "ln_b", "w_decay_up", "a_up", "v_mix_down", "v_mix_up")
_NAMES = ("w_in", "shift_mu", "w_decay0", "w_decay_up", "a0", "a_up", "k_k", "k_a", "r_k", "ln_x_w", "ln_x_b",
          "v_mix0", "v_mix_down", "v_mix_up", "lb_logits", "g_norm_w", "w_out", "ln_w", "ln_b")


def _pad_rows(a, rows, at_end):
    z = jnp.zeros((rows - a.shape[0], a.shape[1]), a.dtype)
    return jnp.concatenate([a, z] if at_end else [z, a], axis=0)


def kernel(x, w_in, shift_mu, w_decay0, w_decay_up, a0, a_up, k_k, k_a, r_k, ln_x_w, ln_x_b, v_mix0, v_mix_down, v_mix_up, lb_logits, g_norm_w, w_out, ln_w, ln_b, loss_target, m_w_in, m_shift_mu, m_w_decay0, m_w_decay_up, m_a0, m_a_up, m_k_k, m_k_a, m_r_k, m_ln_x_w, m_ln_x_b, m_v_mix0, m_v_mix_down, m_v_mix_up, m_lb_logits, m_g_norm_w, m_w_out, m_ln_w, m_ln_b, v_w_in, v_shift_mu, v_w_decay0, v_w_decay_up, v_a0, v_a_up, v_k_k, v_k_a, v_r_k, v_ln_x_w, v_ln_x_b, v_v_mix0, v_v_mix_down, v_v_mix_up, v_lb_logits, v_g_norm_w, v_w_out, v_ln_w, v_ln_b):
    weights = dict(w_in=w_in, shift_mu=shift_mu, w_decay0=w_decay0, w_decay_up=w_decay_up, a0=a0, a_up=a_up, k_k=k_k,
                   k_a=k_a, r_k=r_k, ln_x_w=ln_x_w, ln_x_b=ln_x_b, v_mix0=v_mix0, v_mix_down=v_mix_down,
                   v_mix_up=v_mix_up, lb_logits=lb_logits, g_norm_w=g_norm_w, w_out=w_out, ln_w=ln_w, ln_b=ln_b)
    mom1 = dict(w_in=m_w_in, shift_mu=m_shift_mu, w_decay0=m_w_decay0, w_decay_up=m_w_decay_up, a0=m_a0, a_up=m_a_up,
                k_k=m_k_k, k_a=m_k_a, r_k=m_r_k, ln_x_w=m_ln_x_w, ln_x_b=m_ln_x_b, v_mix0=m_v_mix0,
                v_mix_down=m_v_mix_down, v_mix_up=m_v_mix_up, lb_logits=m_lb_logits, g_norm_w=m_g_norm_w,
                w_out=m_w_out, ln_w=m_ln_w, ln_b=m_ln_b)
    mom2 = dict(w_in=v_w_in, shift_mu=v_shift_mu, w_decay0=v_w_decay0, w_decay_up=v_w_decay_up, a0=v_a0, a_up=v_a_up,
                k_k=v_k_k, k_a=v_k_a, r_k=v_r_k, ln_x_w=v_ln_x_w, ln_x_b=v_ln_x_b, v_mix0=v_v_mix0,
                v_mix_down=v_v_mix_down, v_mix_up=v_v_mix_up, lb_logits=v_lb_logits, g_norm_w=v_g_norm_w,
                w_out=v_w_out, ln_w=v_ln_w, ln_b=v_ln_b)
    assert x.shape[0] == 1 and w_in.shape[0] == DEPTH
    t, d = x.shape[1], x.shape[2]
    dr = w_decay0.shape[1]
    dh = g_norm_w.shape[1]
    rank_w, rank_a, rank_v = w_decay_up.shape[1], a_up.shape[1], v_mix_up.shape[1]
    rwc = 4 * dr + rank_w + rank_a
    assert rank_w + rank_a == LANES and rank_v <= LANES and dr + dh == d
    assert t % CHUNK == 0 and dr % LANES == 0 and dh % LANES == 0 and shift_mu.shape[1] == rwc
    n_pair = dr // LANES
    me = _index(_position())

    win_t = [_all_gather_rows(w_in[l].T.astype(bf16), f"ag_w_in_{l}") for l in range(DEPTH)]
    wout = [_all_gather_rows(w_out[l].astype(bf16), f"ag_w_out_{l}") for l in range(DEPTH)]
    shard = dr // N_DEV
    pack = jnp.concatenate([w_decay_up[0], w_decay_up[1], a_up[0], a_up[1], v_mix_up[0], v_mix_down[0].T], axis=0)
    pack = _all_gather_rows(pack, "ag_small")
    pack = jnp.transpose(pack.reshape(N_DEV, -1, shard), (1, 0, 2)).reshape(-1, dr)
    offs = [0, rank_w, 2 * rank_w, 2 * rank_w + rank_a, 2 * rank_w + 2 * rank_a, 2 * rank_w + 2 * rank_a + rank_v,
            2 * rank_w + 2 * rank_a + 2 * rank_v]
    wdu_f = [pack[offs[0]:offs[1]], pack[offs[1]:offs[2]]]
    aup_f = [pack[offs[2]:offs[3]], pack[offs[3]:offs[4]]]
    vup_f = pack[offs[4]:offs[5]]
    vdown_f = pack[offs[5]:offs[6]].T

    def rwkv_params(l):
        prm = [shift_mu[l:l + 1], w_decay0[l:l + 1], a0[l:l + 1], _pad_rows(wdu_f[l], LANES, True),
               _pad_rows(aup_f[l], LANES, False)]
        if l == 1:
            prm += [v_mix0[0:1], _pad_rows(vdown_f.T, LANES, True).T, _pad_rows(vup_f, LANES, True)]
        rows = jnp.stack([k_k[l], k_a[l], r_k[l], ln_x_w[l], ln_x_b[l]] + [jnp.zeros((dr,), f32)] * 3, axis=0)
        pp = jnp.transpose(rows.reshape(8, n_pair, LANES), (1, 0, 2))
        return tuple(prm), pp

    h = x[0]
    tgt = loss_target[0]
    saved = []
    vfirst = None
    for l in range(DEPTH):
        prm, pp = rwkv_params(l)
        proj = _matmul(h, win_t[l], "nt", f"mm_proj_{l}")
        if l == 0:
            cat, vfirst, mck = _rwkv_fwd(False, proj, None, prm, pp, d)
        else:
            cat, mck = _rwkv_fwd(True, proj, vfirst, prm, pp, d)
        cat, sck = _hgrn_fwd(l == 1, proj, lb_logits, g_norm_w[l:l + 1], cat, rwc)
        y = _matmul(cat, wout[l], "nn", f"mm_out_{l}")
        saved.append((h, proj, prm, pp, mck, sck, cat, y))
        if l < DEPTH - 1:
            h = _ln_fwd(h, y, ln_w[l:l + 1], ln_b[l:l + 1])
        else:
            dh_out, loss_part = _ln_loss(h, y, ln_w[l:l + 1], ln_b[l:l + 1], tgt)
    loss = lax.psum(loss_part[0, 0], ("x", "y", "c"))

    grads = {}
    big = {}
    dvfirst = None
    d_lbl = None
    for l in reversed(range(DEPTH)):
        h_l, proj, prm, pp, mck, sck, cat, y = saved[l]
        dy, g_ln_w, g_ln_b = _ln_bwd(h_l, y, ln_w[l:l + 1], ln_b[l:l + 1], dh_out)
        dcat = _matmul(dy, wout[l], "nt", f"mm_dcat_{l}")
        big[("w_out", l)] = _matmul(cat, dy, "tn", f"mm_dwout_{l}", out_dtype=bf16)
        if l == 1:
            outs = _rwkv_bwd(True, proj, vfirst, prm, pp, mck, dcat, None)
            dproj_r, dvfirst = outs[0], outs[1]
            dprm, dpp = outs[2:-1], outs[-1]
        else:
            outs = _rwkv_bwd(False, proj, None, prm, pp, mck, dcat, dvfirst)
            dproj_r = outs[0]
            dprm, dpp = outs[1:-1], outs[-1]
        dproj_h, dlbl_l, dgnw = _hgrn_bwd(l == 1, proj, lb_logits, g_norm_w[l:l + 1], sck, dcat, rwc)
        dproj = jnp.concatenate([dproj_r] + [dproj_h[i] for i in range(4)], axis=1)
        dh_out = _matmul(dproj, win_t[l], "nn", f"mm_dh_{l}", add=dy, add_scale=ALPHA)
        big[("w_in", l)] = _matmul(dproj, h_l, "tn", f"mm_dwin_{l}", out_dtype=bf16)
        dpp = jnp.transpose(dpp, (1, 0, 2)).reshape(8, dr)
        grads[l] = dict(shift_mu=dprm[0][0], w_decay0=dprm[1][0], a0=dprm[2][0], w_decay_up=dprm[3][:rank_w],
                        a_up=dprm[4][rank_w:], k_k=dpp[0], k_a=dpp[1], r_k=dpp[2], ln_x_w=dpp[3], ln_x_b=dpp[4],
                        g_norm_w=dgnw[0], ln_w=g_ln_w[0], ln_b=g_ln_b[0])
        if l == 1:
            grads[l].update(v_mix0=dprm[5][0], v_mix_down=dprm[6][:, :rank_v], v_mix_up=dprm[7][:rank_v])
            d_lbl = dlbl_l
    grad_x = dh_out[None]

    reduced = [_reduce_scatter([big[("w_in", l)], big[("w_out", l)]], f"l{l}") for l in range(DEPTH)]
    g_w_in = jnp.stack([reduced[l][0].T for l in range(DEPTH)])
    g_w_out = jnp.stack([reduced[l][1] for l in range(DEPTH)])

    def both(name):
        return jnp.stack([grads[0][name], grads[1][name]])

    small = dict(shift_mu=both("shift_mu"), w_decay0=both("w_decay0"), a0=both("a0"), k_k=both("k_k"), k_a=both("k_a"),
                 r_k=both("r_k"), ln_x_w=both("ln_x_w"), ln_x_b=both("ln_x_b"), v_mix0=grads[1]["v_mix0"][None],
                 lb_logits=d_lbl, g_norm_w=both("g_norm_w"), ln_w=both("ln_w"), ln_b=both("ln_b"),
                 w_decay_up=both("w_decay_up"), a_up=both("a_up"), v_mix_down=grads[1]["v_mix_down"][None],
                 v_mix_up=grads[1]["v_mix_up"][None])
    flat = jnp.concatenate([small[nm].reshape(-1) for nm in _SMALL])
    n_flat = flat.shape[0]
    rows = -(-n_flat // (8 * LANES)) * 8
    flat = jnp.concatenate([flat, jnp.zeros((rows * LANES - n_flat,), f32)]).reshape(rows, LANES)
    total = _sum_slots(_all_gather_rows(flat, "ag_small_grads").reshape(N_DEV, rows, LANES), "sum_small_grads").reshape(-1)
    gsm = {}
    off = 0
    for nm in _SMALL:
        size = small[nm].size
        gsm[nm] = total[off:off + size].reshape(small[nm].shape)
        off += size
    gsm["w_decay_up"] = lax.dynamic_slice_in_dim(gsm["w_decay_up"], me * shard, shard, axis=2)
    gsm["a_up"] = lax.dynamic_slice_in_dim(gsm["a_up"], me * shard, shard, axis=2)
    gsm["v_mix_up"] = lax.dynamic_slice_in_dim(gsm["v_mix_up"], me * shard, shard, axis=2)
    gsm["v_mix_down"] = lax.dynamic_slice_in_dim(gsm["v_mix_down"], me * shard, shard, axis=1)
    gsm["w_in"] = g_w_in
    gsm["w_out"] = g_w_out

    deltas, new_m, new_v = {}, {}, {}
    for nm in _NAMES:
        deltas[nm], new_m[nm], new_v[nm] = _adamw(weights[nm], gsm[nm], mom1[nm], mom2[nm], "adamw_" + nm)
    return (loss, grad_x, *[gsm[nm] for nm in _NAMES], *[deltas[nm] for nm in _NAMES],
            *[new_m[nm] for nm in _NAMES], *[new_v[nm] for nm in _NAMES])
```

```python
import functools

import jax
import jax.numpy as jnp
from jax import lax
from jax.experimental import pallas as pl
from jax.experimental.pallas import tpu as pltpu

f32 = jnp.float32
bf16 = jnp.bfloat16

N_DEV = 8
CHUNK = 64
LANES = 128
RWKV_HEAD = 64
DEPTH = 2
ALPHA = (2 * DEPTH) ** 0.25
LN_EPS = 1e-5
GN_EPS = 64e-5
RMS_EPS = 1e-5
LB_FLOOR = 1e-30
ADAM_LR, ADAM_B1, ADAM_B2, ADAM_EPS, ADAM_WD, ADAM_STEP = 0.001, 0.9, 0.999, 1e-08, 0.01, 10
MESH = pl.DeviceIdType.MESH


def _iota(shape, d):
    return lax.broadcasted_iota(jnp.int32, shape, d)


_DIMS = {"nn": (((1,), (0,)), ((), ())), "nt": (((1,), (1,)), ((), ())), "tn": (((0,), (0,)), ((), ()))}
_BATCH_DIMS = {"nn": (((2,), (1,)), ((0,), (0,))), "nt": (((2,), (2,)), ((0,), (0,))), "tn": (((1,), (1,)), ((0,), (0,)))}
_K_AXES = {"nn": (-1, -2), "nt": (-1, -1), "tn": (-2, -2)}


def _mxu(a, b, mode):
    return lax.dot_general(a, b, (_BATCH_DIMS if a.ndim == 3 else _DIMS)[mode], preferred_element_type=f32)


def _split(x):
    hi = x.astype(bf16)
    return hi, (x - hi.astype(f32)).astype(bf16)


def _mm3_impl(a, b, mode):
    ah, al = _split(a)
    bh, bl = _split(b)
    ka, kb = _K_AXES[mode]
    k = a.shape[ka]
    if k % (LANES if -1 in (ka, kb) else 16) == 0:
        return _mxu(jnp.concatenate([ah, ah, al], axis=ka), jnp.concatenate([bh, bl, bh], axis=kb), mode)
    return _mxu(ah, bh, mode) + (_mxu(ah, bl, mode) + _mxu(al, bh, mode))


@functools.partial(jax.custom_vjp, nondiff_argnums=(2,))
def _mm3(a, b, mode):
    return _mm3_impl(a, b, mode)


def _mm3_fwd(a, b, mode):
    return _mm3_impl(a, b, mode), (a, b)


def _mm3_bwd(mode, res, g):
    a, b = res
    if mode == "nn":
        return _mm3_impl(g, b, "nt"), _mm3_impl(a, g, "tn")
    if mode == "nt":
        return _mm3_impl(g, b, "nn"), _mm3_impl(g, a, "tn")
    return _mm3_impl(b, g, "nt"), _mm3_impl(a, g, "nn")


_mm3.defvjp(_mm3_fwd, _mm3_bwd)


def _const_impl(cm, x, mode):
    hi, lo = _split(x)
    if mode in ("r", "rt"):
        shape = x.shape
        hi, lo = hi.reshape(-1, shape[-1]), lo.reshape(-1, shape[-1])
        dims = "nn" if mode == "r" else "nt"
        out = _mxu(hi, cm, dims) + _mxu(lo, cm, dims)
        return out.reshape(shape[:-1] + (out.shape[-1],))
    if x.ndim == 3:
        cm = jnp.broadcast_to(cm, (x.shape[0],) + cm.shape)
    return _mxu(cm, hi, mode) + _mxu(cm, lo, mode)


@jax.custom_vjp
def _const_left(cm, x):
    return _const_impl(cm, x, "nn")


_const_left.defvjp(lambda cm, x: (_const_impl(cm, x, "nn"), cm),
                   lambda cm, g: (jnp.zeros_like(cm), _const_impl(cm, g, "tn")))


@jax.custom_vjp
def _const_right(x, cm):
    return _const_impl(cm, x, "r")


_const_right.defvjp(lambda x, cm: (_const_impl(cm, x, "r"), cm),
                    lambda cm, g: (_const_impl(cm, g, "rt"), jnp.zeros_like(cm)))


def _tri_inv(a):
    n = a.shape[-1]
    tm = (_iota((n, n), 0) == _iota((n, n), 1)).astype(f32) + a
    ak = a
    for _ in range(5):
        ak = _mm3_impl(ak, ak, "nn")
        tm = tm + _mm3_impl(tm, ak, "nn")
    return tm


@jax.custom_vjp
def _tri_solve(a, x):
    return _mm3_impl(_tri_inv(a), x, "nn")


def _tri_solve_fwd(a, x):
    tm = _tri_inv(a)
    u = _mm3_impl(tm, x, "nn")
    return u, (tm, u)


def _tri_solve_bwd(res, du):
    tm, u = res
    dx = _mm3_impl(tm, du, "tn")
    return _mm3_impl(dx, u, "nt"), dx


_tri_solve.defvjp(_tri_solve_fwd, _tri_solve_bwd)


def _col_of_row(row_vec):
    n = row_vec.shape[-1]
    eye = _iota((n, n), 0) == _iota((n, n), 1)
    return jnp.sum(jnp.where(eye, jnp.broadcast_to(row_vec, row_vec.shape[:-2] + (n, n)), 0.0), axis=-1, keepdims=True)


def _softplus(x):
    return jnp.maximum(x, 0.0) + jnp.log1p(jnp.exp(-jnp.abs(x)))


def _log_sigmoid(x):
    return -_softplus(-x)


def _logaddexp(a, b):
    return jnp.maximum(a, b) + jnp.log1p(jnp.exp(-jnp.abs(a - b)))


def _silu(x):
    return x * jax.nn.sigmoid(x)


def _tril(c, strict):
    r, s = _iota((c, c), 0), _iota((c, c), 1)
    return (r > s) if strict else (r >= s)


def _last_row(a):
    c = a.shape[-2]
    return jnp.sum(jnp.where(_iota(a.shape, a.ndim - 2) == c - 1, a, 0.0), axis=-2, keepdims=True)


def _rwkv_pre(layer1, prm, y, prev, vf):
    c = y.shape[0]
    if layer1:
        mu, w0, a0, wup, aup, v0, vdown, vup = prm
    else:
        mu, w0, a0, wup, aup = prm
    dr = w0.shape[1]
    shift = (_iota((c, c), 0) == _iota((c, c), 1) + 1).astype(bf16)
    y_prev = _const_left(shift, y) + jnp.where(_iota((c, 1), 0) == 0, prev, 0.0)
    rw = y + mu * (y_prev - y)
    r, k, v, z = (rw[:, i * dr:(i + 1) * dr] for i in range(4))
    wdad = rw[:, 4 * dr:4 * dr + LANES]
    w_raw = w0 + _mm3(jnp.tanh(wdad), wup, "nn")
    lw = -jnp.exp(-_softplus(-w_raw) - 0.5)
    asig = jax.nn.sigmoid(a0 + _mm3(wdad, aup, "nn"))
    if layer1:
        v = v + (vf - v) * jax.nn.sigmoid(v0 + _mm3(_mm3(v, vdown, "nn"), vup, "nn"))
    return r, k, v, z, lw, asig


def _rwkv_pair(pp, m0, xs):
    kkw, kaw, rkw, gnw, gnb = pp
    r, k, v, z, lw, asig = xs
    c = r.shape[-2]
    n2 = 2 * c
    lane = _iota((1, LANES), 1)
    mh0, mh1 = (lane < RWKV_HEAD).astype(f32), (lane >= RWKV_HEAD).astype(f32)
    same_head = _iota((LANES, LANES), 0) // RWKV_HEAD == _iota((LANES, LANES), 1) // RWKV_HEAD
    g = same_head.astype(bf16)

    def seg(x):
        return _const_right(x, g)

    def stack(x):
        return jnp.concatenate([x * mh0, x * mh1], axis=-2)

    kk = k * kkw
    kk = kk / jnp.maximum(jnp.sqrt(seg(kk * kk)), 1e-12)
    k2 = k * (1.0 + (asig - 1.0) * kaw)
    a = -kk
    b = kk * asig
    cum = _const_left(_tril(c, False).astype(bf16), lw)
    at = stack(a * jnp.exp(cum - lw))
    rt = stack(r * jnp.exp(cum))
    en = jnp.exp(-cum)
    sc = _mm3(jnp.concatenate([at, rt], axis=-2), jnp.concatenate([stack(b * en), stack(k2 * en)], axis=-2), "nt")
    row, col = _iota((n2, n2), 0), _iota((n2, n2), 1)
    same = row // c == col // c
    strict = same & (row % c > col % c)
    incl = same & (row % c >= col % c)
    aab = jnp.where(strict, sc[..., :n2, :n2], 0.0)
    aak = jnp.where(strict, sc[..., :n2, n2:], 0.0)
    arb = jnp.where(incl, sc[..., n2:, :n2], 0.0)
    ark = jnp.where(incl, sc[..., n2:, n2:], 0.0)
    vv = jnp.concatenate([v, v], axis=-2)
    mask_st = jnp.concatenate([jnp.broadcast_to(mh0, (c, LANES)), jnp.broadcast_to(mh1, (c, LANES))], axis=0)
    x_st = _mm3(jnp.concatenate([at, aak], axis=-1), jnp.concatenate([m0, vv], axis=-2), "nn")
    u_st = _tri_solve(aab, x_st) * mask_st
    o_st = _mm3(jnp.concatenate([rt, arb, ark], axis=-1), jnp.concatenate([m0, u_st, vv], axis=-2), "nn") * mask_st
    u = u_st[..., :c, :] + u_st[..., c:, :]
    o = o_st[..., :c, :] + o_st[..., c:, :]
    cum_last = _last_row(cum)
    dec_end = jnp.exp(cum_last - cum)
    m_new = _col_of_row(jnp.exp(cum_last)) * m0 + _mm3(
        jnp.concatenate([b * dec_end, k2 * dec_end], axis=-2), jnp.concatenate([u, v], axis=-2), "tn") * same_head.astype(f32)
    mean = seg(o) * (1.0 / RWKV_HEAD)
    d = o - mean
    var = seg(d * d) * (1.0 / RWKV_HEAD)
    on = d * lax.rsqrt(var + GN_EPS) * gnw + gnb
    bonus = seg(r * k2 * rkw) * v
    return (on + bonus) * _silu(z), m_new


def _split_lanes(a, n):
    return [a[:, i * LANES:(i + 1) * LANES] for i in range(n)]


def _group(n):
    return 4 if n % 4 == 0 else (2 if n % 2 == 0 else 1)


def _rwkv_specs(layer1, t, dr, rwc, n_pair, rev):
    nc = t // CHUNK
    grp = _group(n_pair)

    def cidx(c):
        return (nc - 1 - c) if rev else c

    full = lambda shape: pl.BlockSpec(shape, lambda c, p: tuple(0 for _ in shape))
    specs = [
        pl.BlockSpec((CHUNK, rwc), lambda c, p: (cidx(c), 0)),
        pl.BlockSpec((8, rwc), lambda c, p: (jnp.maximum(cidx(c) * (CHUNK // 8) - 1, 0), 0)),
    ]
    if layer1:
        specs.append(pl.BlockSpec((CHUNK, dr), lambda c, p: (cidx(c), 0)))
    prm_shapes = [(1, rwc), (1, dr), (1, dr), (LANES, dr), (LANES, dr)]
    if layer1:
        prm_shapes += [(1, dr), (dr, LANES), (LANES, dr)]
    specs += [full(s) for s in prm_shapes]
    specs.append(pl.BlockSpec((grp, 8, LANES), lambda c, p: (p, 0, 0)))
    return specs, prm_shapes, cidx, full


def _rwkv_fwd(layer1, proj, vf, prm, pp, cat_width):
    t = proj.shape[0]
    dr = prm[1].shape[1]
    rwc = prm[0].shape[1]
    n_pair = dr // LANES
    nc = t // CHUNK
    n_prm = len(prm)
    specs, _, _, _ = _rwkv_specs(layer1, t, dr, rwc, n_pair, False)

    def body(*refs):
        y_ref, prev_ref = refs[0], refs[1]
        i = 2
        vf_ref = None
        if layer1:
            vf_ref = refs[i]
            i += 1
        prm_refs = refs[i:i + n_prm]
        i += n_prm
        pp_ref = refs[i]
        i += 1
        cat_ref = refs[i]
        i += 1
        vout_ref = None
        if not layer1:
            vout_ref = refs[i]
            i += 1
        mck_ref, x_s, m_s = refs[i], refs[i + 1], refs[i + 2]
        c, p = pl.program_id(0), pl.program_id(1)

        @pl.when((c == 0) & (p == 0))
        def _():
            m_s[...] = jnp.zeros_like(m_s)

        @pl.when(p == 0)
        def _():
            prev = prev_ref[pl.ds(7, 1), :] * (c != 0).astype(f32)
            xs = _rwkv_pre(layer1, tuple(r[...] for r in prm_refs), y_ref[...], prev,
                           vf_ref[...] if layer1 else None)
            for q, a in enumerate(xs):
                for j, piece in enumerate(_split_lanes(a, n_pair)):
                    x_s[q * n_pair + j] = piece
            if not layer1:
                vout_ref[...] = xs[2]

        m0 = m_s[pl.ds(p * grp, grp)]
        mck_ref[0] = m0
        ppv = tuple(pp_ref[:, pl.ds(q, 1), :] for q in range(5))
        og, m_new = _rwkv_pair(ppv, m0, tuple(x_s[pl.ds(q * n_pair + p * grp, grp)] for q in range(6)))
        for j in range(grp):
            cat_ref[:, j * LANES:(j + 1) * LANES] = og[j]
        m_s[pl.ds(p * grp, grp)] = m_new

    grp = _group(n_pair)
    out_shape = [jax.ShapeDtypeStruct((t, cat_width), f32)]
    out_specs = [pl.BlockSpec((CHUNK, grp * LANES), lambda c, p: (c, p))]
    if not layer1:
        out_shape.append(jax.ShapeDtypeStruct((t, dr), f32))
        out_specs.append(pl.BlockSpec((CHUNK, dr), lambda c, p: (c, 0)))
    out_shape.append(jax.ShapeDtypeStruct((nc, n_pair, LANES, LANES), f32))
    out_specs.append(pl.BlockSpec((1, grp, LANES, LANES), lambda c, p: (c, p, 0, 0)))
    args = [proj, proj] + ([vf] if layer1 else []) + list(prm) + [pp]
    return pl.pallas_call(
        body, grid=(nc, n_pair // grp), in_specs=specs, out_specs=out_specs, out_shape=out_shape,
        scratch_shapes=[pltpu.VMEM((6 * n_pair, CHUNK, LANES), f32), pltpu.VMEM((n_pair, LANES, LANES), f32)],
        compiler_params=pltpu.CompilerParams(dimension_semantics=("arbitrary", "arbitrary")),
        name=f"rwkv_fwd_l{int(layer1)}",
    )(*args)


def _rwkv_bwd(layer1, proj, vf, prm, pp, mck, dcat, dvout):
    t = proj.shape[0]
    dr = prm[1].shape[1]
    rwc = prm[0].shape[1]
    n_pair = dr // LANES
    nc = t // CHUNK
    n_prm = len(prm)
    specs, prm_shapes, cidx, full = _rwkv_specs(layer1, t, dr, rwc, n_pair, True)
    grp = _group(n_pair)
    n_step = n_pair // grp
    specs.append(pl.BlockSpec((1, grp, LANES, LANES), lambda c, p: (cidx(c), p, 0, 0)))
    specs.append(pl.BlockSpec((CHUNK, grp * LANES), lambda c, p: (cidx(c), p)))
    if not layer1:
        specs.append(pl.BlockSpec((CHUNK, dr), lambda c, p: (cidx(c), 0)))

    def body(*refs):
        y_ref, prev_ref = refs[0], refs[1]
        i = 2
        vf_ref = None
        if layer1:
            vf_ref = refs[i]
            i += 1
        prm_refs = refs[i:i + n_prm]
        i += n_prm
        pp_ref, mck_ref, dog_ref = refs[i], refs[i + 1], refs[i + 2]
        i += 3
        dvout_ref = None
        if not layer1:
            dvout_ref = refs[i]
            i += 1
        dy_ref = refs[i]
        i += 1
        dvf_ref = None
        if layer1:
            dvf_ref = refs[i]
            i += 1
        dprm_refs = refs[i:i + n_prm]
        i += n_prm
        dpp_ref = refs[i]
        x_s, dx_s, dm_s, dprev_s = refs[i + 1:i + 5]
        c, p = pl.program_id(0), pl.program_id(1)
        cr = nc - 1 - c

        def prev_row():
            return prev_ref[pl.ds(7, 1), :] * (cr != 0).astype(f32)

        @pl.when((c == 0) & (p == 0))
        def _():
            dm_s[...] = jnp.zeros_like(dm_s)
            dprev_s[...] = jnp.zeros_like(dprev_s)
            dpp_ref[...] = jnp.zeros_like(dpp_ref)
            for r in dprm_refs:
                r[...] = jnp.zeros_like(r)

        @pl.when(p == 0)
        def _():
            xs = _rwkv_pre(layer1, tuple(r[...] for r in prm_refs), y_ref[...], prev_row(),
                           vf_ref[...] if layer1 else None)
            for q, a in enumerate(xs):
                for j, piece in enumerate(_split_lanes(a, n_pair)):
                    x_s[q * n_pair + j] = piece

        units = pl.ds(p * grp, grp)
        ppv = tuple(pp_ref[:, pl.ds(q, 1), :] for q in range(5))
        xs_p = tuple(x_s[pl.ds(q * n_pair + p * grp, grp)] for q in range(6))
        dog = jnp.stack([dog_ref[:, j * LANES:(j + 1) * LANES] for j in range(grp)], axis=0)
        _, vjp_pair = jax.vjp(_rwkv_pair, ppv, mck_ref[0], xs_p)
        dppv, dm0, dxs = vjp_pair((dog, dm_s[units]))
        dm_s[units] = dm0
        for q in range(6):
            dx_s[pl.ds(q * n_pair + p * grp, grp)] = dxs[q]
        for q in range(5):
            dpp_ref[units, pl.ds(q, 1), :] += dppv[q]

        @pl.when(p == n_step - 1)
        def _():
            dxs_full = [jnp.concatenate([dx_s[q * n_pair + j] for j in range(n_pair)], axis=1) for q in range(6)]
            if not layer1:
                dxs_full[2] = dxs_full[2] + dvout_ref[...]
            prm_v = tuple(r[...] for r in prm_refs)
            if layer1:
                _, vjp_pre = jax.vjp(functools.partial(_rwkv_pre, True), prm_v, y_ref[...], prev_row(), vf_ref[...])
                dprm, dy, dprev, dvf = vjp_pre(tuple(dxs_full))
                dvf_ref[...] = dvf
            else:
                _, vjp_pre = jax.vjp(lambda a, b, d: _rwkv_pre(False, a, b, d, None), prm_v, y_ref[...], prev_row())
                dprm, dy, dprev = vjp_pre(tuple(dxs_full))
            dy_ref[...] = dy + jnp.where(_iota((CHUNK, 1), 0) == CHUNK - 1, dprev_s[...], 0.0)
            dprev_s[...] = dprev
            for r, gval in zip(dprm_refs, dprm):
                r[...] += gval

    out_shape = [jax.ShapeDtypeStruct((t, rwc), f32)]
    out_specs = [pl.BlockSpec((CHUNK, rwc), lambda c, p: (cidx(c), 0))]
    if layer1:
        out_shape.append(jax.ShapeDtypeStruct((t, dr), f32))
        out_specs.append(pl.BlockSpec((CHUNK, dr), lambda c, p: (cidx(c), 0)))
    out_shape += [jax.ShapeDtypeStruct(s, f32) for s in prm_shapes]
    out_specs += [full(s) for s in prm_shapes]
    out_shape.append(jax.ShapeDtypeStruct((n_pair, 8, LANES), f32))
    out_specs.append(full((n_pair, 8, LANES)))
    args = [proj, proj] + ([vf] if layer1 else []) + list(prm) + [pp, mck, dcat] + ([] if layer1 else [dvout])
    return pl.pallas_call(
        body, grid=(nc, n_step), in_specs=specs, out_specs=out_specs, out_shape=out_shape,
        scratch_shapes=[pltpu.VMEM((6 * n_pair, CHUNK, LANES), f32), pltpu.VMEM((6 * n_pair, CHUNK, LANES), f32),
                        pltpu.VMEM((n_pair, LANES, LANES), f32), pltpu.VMEM((1, rwc), f32)],
        compiler_params=pltpu.CompilerParams(dimension_semantics=("arbitrary", "arbitrary")),
        name=f"rwkv_bwd_l{int(layer1)}",
    )(*args)


def _hgrn_chunk(layer1, lbl, gnw, s0, q_raw, f_raw, i_in, z):
    c = q_raw.shape[-2]
    q = _silu(q_raw)
    ls = _log_sigmoid(f_raw)
    if layer1:
        l0, l1 = lbl[..., 0:1, :], lbl[..., 1:2, :]
        mx = jnp.maximum(l0, l1)
        e0, e1 = jnp.exp(l0 - mx), jnp.exp(l1 - mx)
        sm0, sm1 = e0 / (e0 + e1), e1 / (e0 + e1)
        lb = (sm0 + sm1) - sm0
        log_f = _logaddexp(jnp.log(jnp.maximum(lb, LB_FLOOR)), jnp.log1p(-lb) + ls)
        k = (1.0 - lb) * jax.nn.sigmoid(-f_raw)
    else:
        log_f = _logaddexp(jnp.full_like(ls, jnp.log(jnp.float32(LB_FLOOR))), ls)
        k = jax.nn.sigmoid(-f_raw)
    row, col = _iota((c, c), 0), _iota((c, c), 1)
    trow = _iota((c, 1), 0)
    halves = []
    half = c // 2
    while half >= 1:
        halves.append(half)
        half //= 2
    cmat = jnp.concatenate([(col <= row).astype(f32)]
                           + [(col <= (row // (2 * hf)) * (2 * hf) + hf - 1).astype(f32) for hf in halves], axis=0)
    ball = _const_left(cmat.astype(bf16), log_f)
    b = ball[..., :c, :]
    att = None
    for lvl, hf in enumerate(halves):
        blk = 2 * hf
        bref = ball[..., (lvl + 1) * c:(lvl + 2) * c, :]
        upper = (trow % blk) >= hf
        qh = q * jnp.exp(jnp.where(upper, b - bref, 0.0)) * upper.astype(f32)
        kh = k * jnp.exp(jnp.where(upper, 0.0, bref - b)) * (1.0 - upper.astype(f32))
        term = jnp.where(row // blk == col // blk, _mm3(qh, kh, "nt"), 0.0)
        att = term if att is None else att + term
    lhs = jnp.concatenate([q * jnp.exp(b), att, jnp.zeros(att.shape[:-1] + (LANES - c,), f32)], axis=-1)
    rhs = jnp.concatenate([s0, i_in, jnp.zeros(i_in.shape[:-2] + (LANES - c, i_in.shape[-1]), f32)], axis=-2)
    o = _mm3(lhs, rhs, "nn") + jnp.sum(q * k, axis=-1, keepdims=True) * i_in
    b_last = _last_row(b)
    s_new = _col_of_row(jnp.exp(b_last)) * s0 + _mm3(k * jnp.exp(b_last - b), i_in, "tn")
    o = o * lax.rsqrt(jnp.mean(o * o, axis=-1, keepdims=True) + RMS_EPS)
    return o * gnw * _silu(z), s_new


def _hgrn_in_specs(t, dh, col0, rev):
    nc = t // CHUNK
    nh = dh // LANES

    def cidx(c):
        return (nc - 1 - c) if rev else c

    grp = _group(nh)
    specs = [pl.BlockSpec((CHUNK, LANES), functools.partial(lambda g, j, h, c: (cidx(c), col0 + g * nh + h * grp + j), g, j))
             for j in range(grp) for g in range(4)]
    specs.append(pl.BlockSpec((2, grp * LANES), lambda h, c: (0, h)))
    specs.append(pl.BlockSpec((1, grp * LANES), lambda h, c: (0, h)))
    return specs, cidx, grp


def _hgrn_fwd(layer1, proj, lbl, gnw, cat, rwc):
    t, d = cat.shape
    dh = gnw.shape[1]
    nh = dh // LANES
    nc = t // CHUNK
    col0 = rwc // LANES
    specs, _, grp = _hgrn_in_specs(t, dh, col0, False)
    specs.append(pl.BlockSpec(memory_space=pl.ANY))
    assert (d - dh) % (grp * LANES) == 0
    cat_col0 = (d - dh) // (grp * LANES)

    def body(*refs):
        x_refs = refs[:4 * grp]
        lbl_ref, gnw_ref, _, cat_ref, sck_ref, s_s = refs[4 * grp:]
        c = pl.program_id(1)

        @pl.when(c == 0)
        def _():
            s_s[...] = jnp.zeros_like(s_s)

        lanes = [slice(j * LANES, (j + 1) * LANES) for j in range(grp)]
        s0 = s_s[...]
        sck_ref[:, 0] = s0
        out, s_new = _hgrn_chunk(layer1, jnp.stack([lbl_ref[:, ln] for ln in lanes]), jnp.stack([gnw_ref[:, ln] for ln in lanes]),
                                 s0, *(jnp.stack([x_refs[4 * j + g][...] for j in range(grp)]) for g in range(4)))
        for j in range(grp):
            cat_ref[:, lanes[j]] = out[j]
        s_s[...] = s_new

    return pl.pallas_call(
        body, grid=(nh // grp, nc), in_specs=specs,
        out_specs=[pl.BlockSpec((CHUNK, grp * LANES), lambda h, c: (c, cat_col0 + h)),
                   pl.BlockSpec((grp, 1, LANES, LANES), lambda h, c: (h, c, 0, 0))],
        out_shape=[jax.ShapeDtypeStruct((t, d), f32), jax.ShapeDtypeStruct((nh, nc, LANES, LANES), f32)],
        scratch_shapes=[pltpu.VMEM((grp, LANES, LANES), f32)],
        input_output_aliases={4 * grp + 2: 0},
        compiler_params=pltpu.CompilerParams(dimension_semantics=("arbitrary", "arbitrary")),
        name=f"hgrn_fwd_l{int(layer1)}",
    )(*([proj] * (4 * grp)), lbl, gnw, cat)


def _hgrn_bwd(layer1, proj, lbl, gnw, sck, dcat, rwc):
    t, d = dcat.shape
    dh = gnw.shape[1]
    nh = dh // LANES
    nc = t // CHUNK
    col0 = rwc // LANES
    specs, cidx, grp = _hgrn_in_specs(t, dh, col0, True)
    assert (d - dh) % (grp * LANES) == 0
    cat_col0 = (d - dh) // (grp * LANES)
    specs.append(pl.BlockSpec((grp, 1, LANES, LANES), lambda h, c: (h, cidx(c), 0, 0)))
    specs.append(pl.BlockSpec((CHUNK, grp * LANES), lambda h, c: (cidx(c), cat_col0 + h)))

    def body(*refs):
        x_refs = refs[:4 * grp]
        lbl_ref, gnw_ref, sck_ref, do_ref, dp_ref, dlbl_ref, dgnw_ref, ds_s = refs[4 * grp:]
        c = pl.program_id(1)

        @pl.when(c == 0)
        def _():
            ds_s[...] = jnp.zeros_like(ds_s)
            dlbl_ref[...] = jnp.zeros_like(dlbl_ref)
            dgnw_ref[...] = jnp.zeros_like(dgnw_ref)

        lanes = [slice(j * LANES, (j + 1) * LANES) for j in range(grp)]
        _, vjp = jax.vjp(functools.partial(_hgrn_chunk, layer1),
                         jnp.stack([lbl_ref[:, ln] for ln in lanes]), jnp.stack([gnw_ref[:, ln] for ln in lanes]), sck_ref[:, 0],
                         *(jnp.stack([x_refs[4 * j + g][...] for j in range(grp)]) for g in range(4)))
        dlbl, dgnw, ds0, dq, df, di, dz = vjp((jnp.stack([do_ref[:, ln] for ln in lanes]), ds_s[...]))
        ds_s[...] = ds0
        for j in range(grp):
            dlbl_ref[:, lanes[j]] += dlbl[j]
            dgnw_ref[:, lanes[j]] += dgnw[j]
            for g, val in enumerate((dq, df, di, dz)):
                dp_ref[g, :, lanes[j]] = val[j]

    return pl.pallas_call(
        body, grid=(nh // grp, nc), in_specs=specs,
        out_specs=[pl.BlockSpec((4, CHUNK, grp * LANES), lambda h, c: (0, cidx(c), h)),
                   pl.BlockSpec((2, grp * LANES), lambda h, c: (0, h)),
                   pl.BlockSpec((1, grp * LANES), lambda h, c: (0, h))],
        out_shape=[jax.ShapeDtypeStruct((4, t, dh), f32), jax.ShapeDtypeStruct((2, dh), f32),
                   jax.ShapeDtypeStruct((1, dh), f32)],
        scratch_shapes=[pltpu.VMEM((grp, LANES, LANES), f32)],
        compiler_params=pltpu.CompilerParams(dimension_semantics=("arbitrary", "arbitrary")),
        name=f"hgrn_bwd_l{int(layer1)}",
    )(*([proj] * (4 * grp)), lbl, gnw, sck, dcat)


def _ln(h, y, w, b):
    u = ALPHA * h + y
    mu = jnp.mean(u, axis=-1, keepdims=True)
    var = jnp.mean(jnp.square(u - mu), axis=-1, keepdims=True)
    return (u - mu) * lax.rsqrt(var + LN_EPS) * w + b


def _row_tile(t):
    return 256 if t % 256 == 0 else t


def _ln_fwd(h, y, w, b):
    t, d = h.shape
    tr = _row_tile(t)

    def body(h_ref, y_ref, w_ref, b_ref, o_ref):
        o_ref[...] = _ln(h_ref[...], y_ref[...], w_ref[...], b_ref[...])

    row = pl.BlockSpec((tr, d), lambda i: (i, 0))
    vec = pl.BlockSpec((1, d), lambda i: (0, 0))
    return pl.pallas_call(body, grid=(t // tr,), in_specs=[row, row, vec, vec], out_specs=row,
                          out_shape=jax.ShapeDtypeStruct((t, d), f32), name="ln_fwd")(h, y, w, b)


def _ln_loss(h, y, w, b, tgt):
    t, d = h.shape
    tr = _row_tile(t)

    def body(h_ref, y_ref, w_ref, b_ref, t_ref, g_ref, loss_ref):
        @pl.when(pl.program_id(0) == 0)
        def _():
            loss_ref[...] = jnp.zeros_like(loss_ref)

        err = _ln(h_ref[...], y_ref[...], w_ref[...], b_ref[...]) - t_ref[...]
        g_ref[...] = err * (1.0 / d)
        loss_ref[...] += 0.5 * jnp.sum(jnp.mean(jnp.square(err), axis=-1, keepdims=True), axis=0, keepdims=True)

    row = pl.BlockSpec((tr, d), lambda i: (i, 0))
    vec = pl.BlockSpec((1, d), lambda i: (0, 0))
    return pl.pallas_call(
        body, grid=(t // tr,), in_specs=[row, row, vec, vec, row],
        out_specs=[row, pl.BlockSpec((1, LANES), lambda i: (0, 0))],
        out_shape=[jax.ShapeDtypeStruct((t, d), f32), jax.ShapeDtypeStruct((1, LANES), f32)],
        compiler_params=pltpu.CompilerParams(dimension_semantics=("arbitrary",)), name="ln_loss")(h, y, w, b, tgt)


def _ln_bwd(h, y, w, b, dout):
    t, d = h.shape
    tr = _row_tile(t)

    def body(h_ref, y_ref, w_ref, b_ref, do_ref, dy_ref, dw_ref, db_ref):
        @pl.when(pl.program_id(0) == 0)
        def _():
            dw_ref[...] = jnp.zeros_like(dw_ref)
            db_ref[...] = jnp.zeros_like(db_ref)

        _, vjp = jax.vjp(lambda yy, ww, bb: _ln(h_ref[...], yy, ww, bb), y_ref[...], w_ref[...], b_ref[...])
        dy, dw, db = vjp(do_ref[...])
        dy_ref[...] = dy
        dw_ref[...] += dw
        db_ref[...] += db

    row = pl.BlockSpec((tr, d), lambda i: (i, 0))
    vec = pl.BlockSpec((1, d), lambda i: (0, 0))
    return pl.pallas_call(
        body, grid=(t // tr,), in_specs=[row, row, vec, vec, row], out_specs=[row, vec, vec],
        out_shape=[jax.ShapeDtypeStruct((t, d), f32), jax.ShapeDtypeStruct((1, d), f32), jax.ShapeDtypeStruct((1, d), f32)],
        compiler_params=pltpu.CompilerParams(dimension_semantics=("arbitrary",)), name="ln_bwd")(h, y, w, b, dout)


def _pick(n, prefs):
    for p in prefs:
        if n % p == 0:
            return p
    return n


def _matmul(a, b, mode, name, add=None, add_scale=1.0, out_dtype=f32):
    if mode == "nn":
        (m, k), n = a.shape, b.shape[1]
    elif mode == "nt":
        (m, k), n = a.shape, b.shape[0]
    else:
        (k, m), n = a.shape, b.shape[1]
    tm = _pick(m, (512, 640, 256, 128))
    tn = _pick(n, (1024, 640, 512, 256, 128))
    tk = _pick(k, (2048, 1024, 640, 512, 256, 128))
    nk = k // tk
    dims = {"nn": (((1,), (0,)), ((), ())), "nt": (((1,), (1,)), ((), ())), "tn": (((0,), (0,)), ((), ()))}[mode]

    def body(*refs):
        a_ref, b_ref = refs[0], refs[1]
        add_ref = refs[2] if add is not None else None
        o_ref, acc = refs[-2], refs[-1]
        kk = pl.program_id(2)

        @pl.when(kk == 0)
        def _():
            acc[...] = jnp.zeros_like(acc)

        acc[...] += lax.dot_general(a_ref[...].astype(bf16), b_ref[...].astype(bf16), dims, preferred_element_type=f32)

        @pl.when(kk == nk - 1)
        def _():
            res = acc[...]
            if add is not None:
                res = res + add_scale * add_ref[...]
            o_ref[...] = res.astype(out_dtype)

    a_spec = pl.BlockSpec((tk, tm), lambda i, j, kk: (kk, i)) if mode == "tn" else pl.BlockSpec((tm, tk), lambda i, j, kk: (i, kk))
    b_spec = pl.BlockSpec((tn, tk), lambda i, j, kk: (j, kk)) if mode == "nt" else pl.BlockSpec((tk, tn), lambda i, j, kk: (kk, j))
    o_spec = pl.BlockSpec((tm, tn), lambda i, j, kk: (i, j))
    in_specs = [a_spec, b_spec] + ([o_spec] if add is not None else [])
    args = [a, b] + ([add] if add is not None else [])
    return pl.pallas_call(
        body, grid=(m // tm, n // tn, nk), in_specs=in_specs, out_specs=o_spec,
        out_shape=jax.ShapeDtypeStruct((m, n), out_dtype), scratch_shapes=[pltpu.VMEM((tm, tn), f32)],
        compiler_params=pltpu.CompilerParams(dimension_semantics=("parallel", "parallel", "arbitrary")),
        name=name,
    )(*args)


def _position():
    return lax.axis_index("x"), lax.axis_index("y"), lax.axis_index("c")


def _flip(pos, k):
    x, y, c = pos
    return (1 - x if k & 4 else x, 1 - y if k & 2 else y, 1 - c if k & 1 else c)


def _index(pos):
    return 4 * pos[0] + 2 * pos[1] + pos[2]


def _all_gather_rows(x, name):
    m_per, n = x.shape

    def body(x_ref, out_ref, send_sems, recv_sems, local_sem):
        me = _position()
        sibling = _flip(me, 1)
        chips = (2, 4, 6)

        def rows(pos):
            return out_ref.at[pl.ds(_index(pos) * m_per, m_per), :]

        def copy(sem, block, to, src=None):
            return pltpu.make_async_remote_copy(
                src_ref=rows(block) if src is None else src, dst_ref=rows(block),
                send_sem=send_sems.at[sem], recv_sem=recv_sems.at[sem], device_id=to, device_id_type=MESH)

        mine = pltpu.make_async_copy(x_ref, rows(me), local_sem)
        mine.start()
        first = [copy(0, me, sibling, src=x_ref)]
        first += [copy(1 + j, me, _flip(me, k), src=x_ref) for j, k in enumerate(chips)]
        for cp in first:
            cp.start()
        passed = [copy(4 + j, _flip(me, k), sibling) for j, k in enumerate(chips)]
        for j, k in enumerate(chips):
            copy(1 + j, _flip(me, k), me).wait_recv()
            passed[j].start()
        copy(0, sibling, me).wait_recv()
        for j, k in enumerate(chips):
            copy(4 + j, _flip(sibling, k), me).wait_recv()
        for cp in first + passed:
            cp.wait_send()
        mine.wait()

    return pl.pallas_call(
        body, out_shape=jax.ShapeDtypeStruct((N_DEV * m_per, n), x.dtype),
        in_specs=[pl.BlockSpec(memory_space=pl.ANY)], out_specs=pl.BlockSpec(memory_space=pl.ANY),
        scratch_shapes=[pltpu.SemaphoreType.DMA((7,)), pltpu.SemaphoreType.DMA((7,)), pltpu.SemaphoreType.DMA(())],
        name=name,
    )(x)


def _exchange_siblings(gs, name):
    n_arr = len(gs)

    def body(*refs):
        g_refs, out_refs = refs[:n_arr], refs[n_arr:2 * n_arr]
        send_sems, recv_sems = refs[2 * n_arr:]
        me = _position()
        c = me[2]
        sibling = _flip(me, 1)
        copies = []
        for i, (g_ref, out_ref) in enumerate(zip(g_refs, out_refs)):
            m_per = g_ref.shape[0] // N_DEV
            for q in range(4):
                copies.append(pltpu.make_async_remote_copy(
                    src_ref=g_ref.at[pl.ds((2 * q + 1 - c) * m_per, m_per), :], dst_ref=out_ref.at[q],
                    send_sem=send_sems.at[4 * i + q], recv_sem=recv_sems.at[4 * i + q],
                    device_id=sibling, device_id_type=MESH))
        for cp in copies:
            cp.start()
        for cp in copies:
            cp.wait_recv()
        for cp in copies:
            cp.wait_send()

    anyspec = pl.BlockSpec(memory_space=pl.ANY)
    return pl.pallas_call(
        body, out_shape=[jax.ShapeDtypeStruct((4, g.shape[0] // N_DEV, g.shape[1]), g.dtype) for g in gs],
        in_specs=[anyspec] * n_arr, out_specs=[anyspec] * n_arr,
        scratch_shapes=[pltpu.SemaphoreType.DMA((4 * n_arr,))] * 2, name=name,
    )(*gs)


def _sum_with_sibling(g, recv, name):
    m = g.shape[0] // N_DEV
    n = g.shape[1]
    tr = _pick(m, (208, 128, 64, 32, 16))
    nt = m // tr

    def body(g_ref, r_ref, o_ref):
        c = lax.axis_index("c")
        own = jnp.where(c == 0, g_ref[0, 0].astype(f32), g_ref[0, 1].astype(f32))
        o_ref[...] = (own + r_ref[0].astype(f32)).astype(o_ref.dtype)

    return pl.pallas_call(
        body, grid=(4, nt),
        in_specs=[pl.BlockSpec((1, 2, tr, n), lambda q, i: (q, 0, i, 0)), pl.BlockSpec((1, tr, n), lambda q, i: (q, i, 0))],
        out_specs=pl.BlockSpec((tr, n), lambda q, i: (q * nt + i, 0)),
        out_shape=jax.ShapeDtypeStruct((4 * m, n), bf16), name=name,
    )(g.reshape(4, 2, m, n), recv)


def _exchange_chips(hs, name):
    n_arr = len(hs)
    chips = (2, 4, 6)

    def body(*refs):
        h_refs, out_refs = refs[:n_arr], refs[n_arr:2 * n_arr]
        send_sems, recv_sems = refs[2 * n_arr:]
        me = _position()
        copies = []
        for i, (h_ref, out_ref) in enumerate(zip(h_refs, out_refs)):
            m_per = h_ref.shape[0] // 4
            for j, k in enumerate(chips):
                peer = _flip(me, k)
                peer_q = 2 * peer[0] + peer[1]
                copies.append(pltpu.make_async_remote_copy(
                    src_ref=h_ref.at[pl.ds(peer_q * m_per, m_per), :], dst_ref=out_ref.at[j],
                    send_sem=send_sems.at[3 * i + j], recv_sem=recv_sems.at[3 * i + j],
                    device_id=peer, device_id_type=MESH))
        for cp in copies:
            cp.start()
        for cp in copies:
            cp.wait_recv()
        for cp in copies:
            cp.wait_send()

    anyspec = pl.BlockSpec(memory_space=pl.ANY)
    return pl.pallas_call(
        body, out_shape=[jax.ShapeDtypeStruct((3, h.shape[0] // 4, h.shape[1]), h.dtype) for h in hs],
        in_specs=[anyspec] * n_arr, out_specs=[anyspec] * n_arr,
        scratch_shapes=[pltpu.SemaphoreType.DMA((3 * n_arr,))] * 2, name=name,
    )(*hs)


def _sum_with_chips(h, recv, name):
    m = h.shape[0] // 4
    n = h.shape[1]
    tr = _pick(m, (208, 128, 64, 32, 16))

    def body(h_ref, r_ref, o_ref):
        my_q = 2 * lax.axis_index("x") + lax.axis_index("y")
        own = h_ref[0].astype(f32)
        for q in range(1, 4):
            own = jnp.where(my_q == q, h_ref[q].astype(f32), own)
        o_ref[...] = ((own + r_ref[0].astype(f32)) + r_ref[1].astype(f32)) + r_ref[2].astype(f32)

    return pl.pallas_call(
        body, grid=(m // tr,),
        in_specs=[pl.BlockSpec((4, tr, n), lambda i: (0, i, 0)), pl.BlockSpec((3, tr, n), lambda i: (0, i, 0))],
        out_specs=pl.BlockSpec((tr, n), lambda i: (i, 0)), out_shape=jax.ShapeDtypeStruct((m, n), f32), name=name,
    )(h.reshape(4, m, n), recv)


def _sum_slots(parts, name):
    n_slot, m, n = parts.shape
    tr = _pick(m, (208, 128, 64, 32, 16, 8))

    def body(p_ref, o_ref):
        acc = p_ref[0]
        for s in range(1, n_slot):
            acc = acc + p_ref[s]
        o_ref[...] = acc

    return pl.pallas_call(
        body, grid=(m // tr,), in_specs=[pl.BlockSpec((n_slot, tr, n), lambda i: (0, i, 0))],
        out_specs=pl.BlockSpec((tr, n), lambda i: (i, 0)), out_shape=jax.ShapeDtypeStruct((m, n), parts.dtype), name=name,
    )(parts)


def _reduce_scatter(gs, name):
    from_sibling = _exchange_siblings(gs, "rs_d2d_" + name)
    chip_sums = [_sum_with_sibling(g, r, f"rs_sum2_{name}_{i}") for i, (g, r) in enumerate(zip(gs, from_sibling))]
    from_chips = _exchange_chips(chip_sums, "rs_ici_" + name)
    return [_sum_with_chips(h, r, f"rs_sum4_{name}_{i}") for i, (h, r) in enumerate(zip(chip_sums, from_chips))]


def _adamw(w, g, m, v, name):
    shape = w.shape
    n = shape[-1]
    r = w.size // n
    w2, g2, m2, v2 = (a.reshape(r, n) for a in (w, g, m, v))
    tr = _pick(r, (256, 128, 64, 32, 16, 8)) if r * n > 65536 else r

    def body(w_ref, g_ref, m_ref, v_ref, d_ref, mo_ref, vo_ref):
        gg = g_ref[...]
        mm = ADAM_B1 * m_ref[...] + (1.0 - ADAM_B1) * gg
        vv = ADAM_B2 * v_ref[...] + (1.0 - ADAM_B2) * jnp.square(gg)
        m_hat = mm / (1.0 - ADAM_B1 ** ADAM_STEP)
        v_hat = vv / (1.0 - ADAM_B2 ** ADAM_STEP)
        d_ref[...] = -ADAM_LR * (m_hat / (jnp.sqrt(v_hat) + ADAM_EPS) + ADAM_WD * w_ref[...])
        mo_ref[...] = mm
        vo_ref[...] = vv

    spec = pl.BlockSpec((tr, n), lambda i: (i, 0))
    outs = pl.pallas_call(
        body, grid=(r // tr,), in_specs=[spec] * 4, out_specs=[spec] * 3,
        out_shape=[jax.ShapeDtypeStruct((r, n), f32)] * 3, name=name,
    )(w2, g2, m2, v2)
    return tuple(o.reshape(shape) for o in outs)


_SMALL = ("shift_mu", "w_decay0", "a0", "k_k", "k_a", "r_k", "ln_x_w", "ln_x_b", "v_mix0", "lb_logits",
          "g_norm_w", "ln_w", "ln_b", "w_decay_up", "a_up", "v_mix_down", "v_mix_up")
_NAMES = ("w_in", "shift_mu", "w_decay0", "w_decay_up", "a0", "a_up", "k_k", "k_a", "r_k", "ln_x_w", "ln_x_b",
          "v_mix0", "v_mix_down", "v_mix_up", "lb_logits", "g_norm_w", "w_out", "ln_w", "ln_b")


def _pad_rows(a, rows, at_end):
    z = jnp.zeros((rows - a.shape[0], a.shape[1]), a.dtype)
    return jnp.concatenate([a, z] if at_end else [z, a], axis=0)


def kernel(x, w_in, shift_mu, w_decay0, w_decay_up, a0, a_up, k_k, k_a, r_k, ln_x_w, ln_x_b, v_mix0, v_mix_down, v_mix_up, lb_logits, g_norm_w, w_out, ln_w, ln_b, loss_target, m_w_in, m_shift_mu, m_w_decay0, m_w_decay_up, m_a0, m_a_up, m_k_k, m_k_a, m_r_k, m_ln_x_w, m_ln_x_b, m_v_mix0, m_v_mix_down, m_v_mix_up, m_lb_logits, m_g_norm_w, m_w_out, m_ln_w, m_ln_b, v_w_in, v_shift_mu, v_w_decay0, v_w_decay_up, v_a0, v_a_up, v_k_k, v_k_a, v_r_k, v_ln_x_w, v_ln_x_b, v_v_mix0, v_v_mix_down, v_v_mix_up, v_lb_logits, v_g_norm_w, v_w_out, v_ln_w, v_ln_b):
    weights = dict(w_in=w_in, shift_mu=shift_mu, w_decay0=w_decay0, w_decay_up=w_decay_up, a0=a0, a_up=a_up, k_k=k_k,
                   k_a=k_a, r_k=r_k, ln_x_w=ln_x_w, ln_x_b=ln_x_b, v_mix0=v_mix0, v_mix_down=v_mix_down,
                   v_mix_up=v_mix_up, lb_logits=lb_logits, g_norm_w=g_norm_w, w_out=w_out, ln_w=ln_w, ln_b=ln_b)
    mom1 = dict(w_in=m_w_in, shift_mu=m_shift_mu, w_decay0=m_w_decay0, w_decay_up=m_w_decay_up, a0=m_a0, a_up=m_a_up,
                k_k=m_k_k, k_a=m_k_a, r_k=m_r_k, ln_x_w=m_ln_x_w, ln_x_b=m_ln_x_b, v_mix0=m_v_mix0,
                v_mix_down=m_v_mix_down, v_mix_up=m_v_mix_up, lb_logits=m_lb_logits, g_norm_w=m_g_norm_w,
                w_out=m_w_out, ln_w=m_ln_w, ln_b=m_ln_b)
    mom2 = dict(w_in=v_w_in, shift_mu=v_shift_mu, w_decay0=v_w_decay0, w_decay_up=v_w_decay_up, a0=v_a0, a_up=v_a_up,
                k_k=v_k_k, k_a=v_k_a, r_k=v_r_k, ln_x_w=v_ln_x_w, ln_x_b=v_ln_x_b, v_mix0=v_v_mix0,
                v_mix_down=v_v_mix_down, v_mix_up=v_v_mix_up, lb_logits=v_lb_logits, g_norm_w=v_g_norm_w,
                w_out=v_w_out, ln_w=v_ln_w, ln_b=v_ln_b)
    assert x.shape[0] == 1 and w_in.shape[0] == DEPTH
    t, d = x.shape[1], x.shape[2]
    dr = w_decay0.shape[1]
    dh = g_norm_w.shape[1]
    rank_w, rank_a, rank_v = w_decay_up.shape[1], a_up.shape[1], v_mix_up.shape[1]
    rwc = 4 * dr + rank_w + rank_a
    assert rank_w + rank_a == LANES and rank_v <= LANES and dr + dh == d
    assert t % CHUNK == 0 and dr % LANES == 0 and dh % LANES == 0 and shift_mu.shape[1] == rwc
    n_pair = dr // LANES
    me = _index(_position())

    win_t = [_all_gather_rows(w_in[l].T.astype(bf16), f"ag_w_in_{l}") for l in range(DEPTH)]
    wout = [_all_gather_rows(w_out[l].astype(bf16), f"ag_w_out_{l}") for l in range(DEPTH)]
    shard = dr // N_DEV
    pack = jnp.concatenate([w_decay_up[0], w_decay_up[1], a_up[0], a_up[1], v_mix_up[0], v_mix_down[0].T], axis=0)
    pack = _all_gather_rows(pack, "ag_small")
    pack = jnp.transpose(pack.reshape(N_DEV, -1, shard), (1, 0, 2)).reshape(-1, dr)
    offs = [0, rank_w, 2 * rank_w, 2 * rank_w + rank_a, 2 * rank_w + 2 * rank_a, 2 * rank_w + 2 * rank_a + rank_v,
            2 * rank_w + 2 * rank_a + 2 * rank_v]
    wdu_f = [pack[offs[0]:offs[1]], pack[offs[1]:offs[2]]]
    aup_f = [pack[offs[2]:offs[3]], pack[offs[3]:offs[4]]]
    vup_f = pack[offs[4]:offs[5]]
    vdown_f = pack[offs[5]:offs[6]].T

    def rwkv_params(l):
        prm = [shift_mu[l:l + 1], w_decay0[l:l + 1], a0[l:l + 1], _pad_rows(wdu_f[l], LANES, True),
               _pad_rows(aup_f[l], LANES, False)]
        if l == 1:
            prm += [v_mix0[0:1], _pad_rows(vdown_f.T, LANES, True).T, _pad_rows(vup_f, LANES, True)]
        rows = jnp.stack([k_k[l], k_a[l], r_k[l], ln_x_w[l], ln_x_b[l]] + [jnp.zeros((dr,), f32)] * 3, axis=0)
        pp = jnp.transpose(rows.reshape(8, n_pair, LANES), (1, 0, 2))
        return tuple(prm), pp

    h = x[0]
    tgt = loss_target[0]
    saved = []
    vfirst = None
    for l in range(DEPTH):
        prm, pp = rwkv_params(l)
        proj = _matmul(h, win_t[l], "nt", f"mm_proj_{l}")
        if l == 0:
            cat, vfirst, mck = _rwkv_fwd(False, proj, None, prm, pp, d)
        else:
            cat, mck = _rwkv_fwd(True, proj, vfirst, prm, pp, d)
        cat, sck = _hgrn_fwd(l == 1, proj, lb_logits, g_norm_w[l:l + 1], cat, rwc)
        y = _matmul(cat, wout[l], "nn", f"mm_out_{l}")
        saved.append((h, proj, prm, pp, mck, sck, cat, y))
        if l < DEPTH - 1:
            h = _ln_fwd(h, y, ln_w[l:l + 1], ln_b[l:l + 1])
        else:
            dh_out, loss_part = _ln_loss(h, y, ln_w[l:l + 1], ln_b[l:l + 1], tgt)
    loss = lax.psum(loss_part[0, 0], ("x", "y", "c"))

    grads = {}
    big = {}
    dvfirst = None
    d_lbl = None
    for l in reversed(range(DEPTH)):
        h_l, proj, prm, pp, mck, sck, cat, y = saved[l]
        dy, g_ln_w, g_ln_b = _ln_bwd(h_l, y, ln_w[l:l + 1], ln_b[l:l + 1], dh_out)
        dcat = _matmul(dy, wout[l], "nt", f"mm_dcat_{l}")
        big[("w_out", l)] = _matmul(cat, dy, "tn", f"mm_dwout_{l}", out_dtype=bf16)
        if l == 1:
            outs = _rwkv_bwd(True, proj, vfirst, prm, pp, mck, dcat, None)
            dproj_r, dvfirst = outs[0], outs[1]
            dprm, dpp = outs[2:-1], outs[-1]
        else:
            outs = _rwkv_bwd(False, proj, None, prm, pp, mck, dcat, dvfirst)
            dproj_r = outs[0]
            dprm, dpp = outs[1:-1], outs[-1]
        dproj_h, dlbl_l, dgnw = _hgrn_bwd(l == 1, proj, lb_logits, g_norm_w[l:l + 1], sck, dcat, rwc)
        dproj = jnp.concatenate([dproj_r] + [dproj_h[i] for i in range(4)], axis=1)
        dh_out = _matmul(dproj, win_t[l], "nn", f"mm_dh_{l}", add=dy, add_scale=ALPHA)
        big[("w_in", l)] = _matmul(dproj, h_l, "tn", f"mm_dwin_{l}", out_dtype=bf16)
        dpp = jnp.transpose(dpp, (1, 0, 2)).reshape(8, dr)
        grads[l] = dict(shift_mu=dprm[0][0], w_decay0=dprm[1][0], a0=dprm[2][0], w_decay_up=dprm[3][:rank_w],
                        a_up=dprm[4][rank_w:], k_k=dpp[0], k_a=dpp[1], r_k=dpp[2], ln_x_w=dpp[3], ln_x_b=dpp[4],
                        g_norm_w=dgnw[0], ln_w=g_ln_w[0], ln_b=g_ln_b[0])
        if l == 1:
            grads[l].update(v_mix0=dprm[5][0], v_mix_down=dprm[6][:, :rank_v], v_mix_up=dprm[7][:rank_v])
            d_lbl = dlbl_l
    grad_x = dh_out[None]

    reduced = [_reduce_scatter([big[("w_in", l)], big[("w_out", l)]], f"l{l}") for l in range(DEPTH)]
    g_w_in = jnp.stack([reduced[l][0].T for l in range(DEPTH)])
    g_w_out = jnp.stack([reduced[l][1] for l in range(DEPTH)])

    def both(name):
        return jnp.stack([grads[0][name], grads[1][name]])

    small = dict(shift_mu=both("shift_mu"), w_decay0=both("w_decay0"), a0=both("a0"), k_k=both("k_k"), k_a=both("k_a"),
                 r_k=both("r_k"), ln_x_w=both("ln_x_w"), ln_x_b=both("ln_x_b"), v_mix0=grads[1]["v_mix0"][None],
                 lb_logits=d_lbl, g_norm_w=both("g_norm_w"), ln_w=both("ln_w"), ln_b=both("ln_b"),
                 w_decay_up=both("w_decay_up"), a_up=both("a_up"), v_mix_down=grads[1]["v_mix_down"][None],
                 v_mix_up=grads[1]["v_mix_up"][None])
    flat = jnp.concatenate([small[nm].reshape(-1) for nm in _SMALL])
    n_flat = flat.shape[0]
    rows = -(-n_flat // (8 * LANES)) * 8
    flat = jnp.concatenate([flat, jnp.zeros((rows * LANES - n_flat,), f32)]).reshape(rows, LANES)
    total = _sum_slots(_all_gather_rows(flat, "ag_small_grads").reshape(N_DEV, rows, LANES), "sum_small_grads").reshape(-1)
    gsm = {}
    off = 0
    for nm in _SMALL:
        size = small[nm].size
        gsm[nm] = total[off:off + size].reshape(small[nm].shape)
        off += size
    gsm["w_decay_up"] = lax.dynamic_slice_in_dim(gsm["w_decay_up"], me * shard, shard, axis=2)
    gsm["a_up"] = lax.dynamic_slice_in_dim(gsm["a_up"], me * shard, shard, axis=2)
    gsm["v_mix_up"] = lax.dynamic_slice_in_dim(gsm["v_mix_up"], me * shard, shard, axis=2)
    gsm["v_mix_down"] = lax.dynamic_slice_in_dim(gsm["v_mix_down"], me * shard, shard, axis=1)
    gsm["w_in"] = g_w_in
    gsm["w_out"] = g_w_out

    deltas, new_m, new_v = {}, {}, {}
    for nm in _NAMES:
        deltas[nm], new_m[nm], new_v[nm] = _adamw(weights[nm], gsm[nm], mom1[nm], mom2[nm], "adamw_" + nm)
    return (loss, grad_x, *[gsm[nm] for nm in _NAMES], *[deltas[nm] for nm in _NAMES],
            *[new_m[nm] for nm in _NAMES], *[new_v[nm] for nm in _NAMES])
```

```python
import functools

import jax
import jax.numpy as jnp
from jax import lax
from jax.experimental import pallas as pl
from jax.experimental.pallas import tpu as pltpu

f32 = jnp.float32
bf16 = jnp.bfloat16

N_DEV = 8
CHUNK = 64
LANES = 128
RWKV_HEAD = 64
DEPTH = 2
ALPHA = (2 * DEPTH) ** 0.25
LN_EPS = 1e-5
GN_EPS = 64e-5
RMS_EPS = 1e-5
LB_FLOOR = 1e-30
ADAM_LR, ADAM_B1, ADAM_B2, ADAM_EPS, ADAM_WD, ADAM_STEP = 0.001, 0.9, 0.999, 1e-08, 0.01, 10
MESH = pl.DeviceIdType.MESH


def _iota(shape, d):
    return lax.broadcasted_iota(jnp.int32, shape, d)


_DIMS = {"nn": (((1,), (0,)), ((), ())), "nt": (((1,), (1,)), ((), ())), "tn": (((0,), (0,)), ((), ()))}
_BATCH_DIMS = {"nn": (((2,), (1,)), ((0,), (0,))), "nt": (((2,), (2,)), ((0,), (0,))), "tn": (((1,), (1,)), ((0,), (0,)))}
_K_AXES = {"nn": (-1, -2), "nt": (-1, -1), "tn": (-2, -2)}


def _mxu(a, b, mode):
    return lax.dot_general(a, b, (_BATCH_DIMS if a.ndim == 3 else _DIMS)[mode], preferred_element_type=f32)


def _split(x):
    hi = x.astype(bf16)
    return hi, (x - hi.astype(f32)).astype(bf16)


def _mm3_impl(a, b, mode):
    ah, al = _split(a)
    bh, bl = _split(b)
    ka, kb = _K_AXES[mode]
    k = a.shape[ka]
    if k % (LANES if -1 in (ka, kb) else 16) == 0:
        return _mxu(jnp.concatenate([ah, ah, al], axis=ka), jnp.concatenate([bh, bl, bh], axis=kb), mode)
    return _mxu(ah, bh, mode) + (_mxu(ah, bl, mode) + _mxu(al, bh, mode))


@functools.partial(jax.custom_vjp, nondiff_argnums=(2,))
def _mm3(a, b, mode):
    return _mm3_impl(a, b, mode)


def _mm3_fwd(a, b, mode):
    return _mm3_impl(a, b, mode), (a, b)


def _mm3_bwd(mode, res, g):
    a, b = res
    if mode == "nn":
        return _mm3_impl(g, b, "nt"), _mm3_impl(a, g, "tn")
    if mode == "nt":
        return _mm3_impl(g, b, "nn"), _mm3_impl(g, a, "tn")
    return _mm3_impl(b, g, "nt"), _mm3_impl(a, g, "nn")


_mm3.defvjp(_mm3_fwd, _mm3_bwd)


def _const_impl(cm, x, mode):
    hi, lo = _split(x)
    if mode in ("r", "rt"):
        shape = x.shape
        hi, lo = hi.reshape(-1, shape[-1]), lo.reshape(-1, shape[-1])
        dims = "nn" if mode == "r" else "nt"
        out = _mxu(hi, cm, dims) + _mxu(lo, cm, dims)
        return out.reshape(shape[:-1] + (out.shape[-1],))
    if x.ndim == 3:
        cm = jnp.broadcast_to(cm, (x.shape[0],) + cm.shape)
    return _mxu(cm, hi, mode) + _mxu(cm, lo, mode)


@jax.custom_vjp
def _const_left(cm, x):
    return _const_impl(cm, x, "nn")


_const_left.defvjp(lambda cm, x: (_const_impl(cm, x, "nn"), cm),
                   lambda cm, g: (jnp.zeros_like(cm), _const_impl(cm, g, "tn")))


@jax.custom_vjp
def _const_right(x, cm):
    return _const_impl(cm, x, "r")


_const_right.defvjp(lambda x, cm: (_const_impl(cm, x, "r"), cm),
                    lambda cm, g: (_const_impl(cm, g, "rt"), jnp.zeros_like(cm)))


def _tri_inv(a):
    n = a.shape[-1]
    tm = (_iota((n, n), 0) == _iota((n, n), 1)).astype(f32) + a
    ak = a
    for _ in range(5):
        ak = _mm3_impl(ak, ak, "nn")
        tm = tm + _mm3_impl(tm, ak, "nn")
    return tm


@jax.custom_vjp
def _tri_solve(a, x):
    return _mm3_impl(_tri_inv(a), x, "nn")


def _tri_solve_fwd(a, x):
    tm = _tri_inv(a)
    u = _mm3_impl(tm, x, "nn")
    return u, (tm, u)


def _tri_solve_bwd(res, du):
    tm, u = res
    dx = _mm3_impl(tm, du, "tn")
    return _mm3_impl(dx, u, "nt"), dx


_tri_solve.defvjp(_tri_solve_fwd, _tri_solve_bwd)


def _col_of_row(row_vec):
    n = row_vec.shape[-1]
    eye = _iota((n, n), 0) == _iota((n, n), 1)
    return jnp.sum(jnp.where(eye, jnp.broadcast_to(row_vec, row_vec.shape[:-2] + (n, n)), 0.0), axis=-1, keepdims=True)


def _softplus(x):
    return jnp.maximum(x, 0.0) + jnp.log1p(jnp.exp(-jnp.abs(x)))


def _log_sigmoid(x):
    return -_softplus(-x)


def _logaddexp(a, b):
    return jnp.maximum(a, b) + jnp.log1p(jnp.exp(-jnp.abs(a - b)))


def _silu(x):
    return x * jax.nn.sigmoid(x)


def _tril(c, strict):
    r, s = _iota((c, c), 0), _iota((c, c), 1)
    return (r > s) if strict else (r >= s)


def _last_row(a):
    c = a.shape[-2]
    return jnp.sum(jnp.where(_iota(a.shape, a.ndim - 2) == c - 1, a, 0.0), axis=-2, keepdims=True)


def _rwkv_pre(layer1, prm, y, prev, vf):
    c = y.shape[0]
    if layer1:
        mu, w0, a0, wup, aup, v0, vdown, vup = prm
    else:
        mu, w0, a0, wup, aup = prm
    dr = w0.shape[1]
    shift = (_iota((c, c), 0) == _iota((c, c), 1) + 1).astype(bf16)
    y_prev = _const_left(shift, y) + jnp.where(_iota((c, 1), 0) == 0, prev, 0.0)
    rw = y + mu * (y_prev - y)
    r, k, v, z = (rw[:, i * dr:(i + 1) * dr] for i in range(4))
    wdad = rw[:, 4 * dr:4 * dr + LANES]
    w_raw = w0 + _mm3(jnp.tanh(wdad), wup, "nn")
    lw = -jnp.exp(-_softplus(-w_raw) - 0.5)
    asig = jax.nn.sigmoid(a0 + _mm3(wdad, aup, "nn"))
    if layer1:
        v = v + (vf - v) * jax.nn.sigmoid(v0 + _mm3(_mm3(v, vdown, "nn"), vup, "nn"))
    return r, k, v, z, lw, asig


def _rwkv_pair(pp, m0, xs):
    kkw, kaw, rkw, gnw, gnb = pp
    r, k, v, z, lw, asig = xs
    c = r.shape[-2]
    n2 = 2 * c
    lane = _iota((1, LANES), 1)
    mh0, mh1 = (lane < RWKV_HEAD).astype(f32), (lane >= RWKV_HEAD).astype(f32)
    same_head = _iota((LANES, LANES), 0) // RWKV_HEAD == _iota((LANES, LANES), 1) // RWKV_HEAD
    g = same_head.astype(bf16)

    def seg(x):
        return _const_right(x, g)

    def stack(x):
        return jnp.concatenate([x * mh0, x * mh1], axis=-2)

    kk = k * kkw
    kk = kk / jnp.maximum(jnp.sqrt(seg(kk * kk)), 1e-12)
    k2 = k * (1.0 + (asig - 1.0) * kaw)
    a = -kk
    b = kk * asig
    cum = _const_left(_tril(c, False).astype(bf16), lw)
    at = stack(a * jnp.exp(cum - lw))
    rt = stack(r * jnp.exp(cum))
    en = jnp.exp(-cum)
    sc = _mm3(jnp.concatenate([at, rt], axis=-2), jnp.concatenate([stack(b * en), stack(k2 * en)], axis=-2), "nt")
    row, col = _iota((n2, n2), 0), _iota((n2, n2), 1)
    same = row // c == col // c
    strict = same & (row % c > col % c)
    incl = same & (row % c >= col % c)
    aab = jnp.where(strict, sc[..., :n2, :n2], 0.0)
    aak = jnp.where(strict, sc[..., :n2, n2:], 0.0)
    arb = jnp.where(incl, sc[..., n2:, :n2], 0.0)
    ark = jnp.where(incl, sc[..., n2:, n2:], 0.0)
    vv = jnp.concatenate([v, v], axis=-2)
    mask_st = jnp.concatenate([jnp.broadcast_to(mh0, (c, LANES)), jnp.broadcast_to(mh1, (c, LANES))], axis=0)
    x_st = _mm3(jnp.concatenate([at, aak], axis=-1), jnp.concatenate([m0, vv], axis=-2), "nn")
    u_st = _tri_solve(aab, x_st) * mask_st
    o_st = _mm3(jnp.concatenate([rt, arb, ark], axis=-1), jnp.concatenate([m0, u_st, vv], axis=-2), "nn") * mask_st
    u = u_st[..., :c, :] + u_st[..., c:, :]
    o = o_st[..., :c, :] + o_st[..., c:, :]
    cum_last = _last_row(cum)
    dec_end = jnp.exp(cum_last - cum)
    m_new = _col_of_row(jnp.exp(cum_last)) * m0 + _mm3(
        jnp.concatenate([b * dec_end, k2 * dec_end], axis=-2), jnp.concatenate([u, v], axis=-2), "tn") * same_head.astype(f32)
    mean = seg(o) * (1.0 / RWKV_HEAD)
    d = o - mean
    var = seg(d * d) * (1.0 / RWKV_HEAD)
    on = d * lax.rsqrt(var + GN_EPS) * gnw + gnb
    bonus = seg(r * k2 * rkw) * v
    return (on + bonus) * _silu(z), m_new


def _split_lanes(a, n):
    return [a[:, i * LANES:(i + 1) * LANES] for i in range(n)]


def _group(n):
    return 8 if n % 8 == 0 else (4 if n % 4 == 0 else (2 if n % 2 == 0 else 1))


def _rwkv_specs(layer1, t, dr, rwc, n_pair, rev):
    nc = t // CHUNK
    grp = _group(n_pair)

    def cidx(c):
        return (nc - 1 - c) if rev else c

    full = lambda shape: pl.BlockSpec(shape, lambda c, p: tuple(0 for _ in shape))
    specs = [
        pl.BlockSpec((CHUNK, rwc), lambda c, p: (cidx(c), 0)),
        pl.BlockSpec((8, rwc), lambda c, p: (jnp.maximum(cidx(c) * (CHUNK // 8) - 1, 0), 0)),
    ]
    if layer1:
        specs.append(pl.BlockSpec((CHUNK, dr), lambda c, p: (cidx(c), 0)))
    prm_shapes = [(1, rwc), (1, dr), (1, dr), (LANES, dr), (LANES, dr)]
    if layer1:
        prm_shapes += [(1, dr), (dr, LANES), (LANES, dr)]
    specs += [full(s) for s in prm_shapes]
    specs.append(pl.BlockSpec((grp, 8, LANES), lambda c, p: (p, 0, 0)))
    return specs, prm_shapes, cidx, full


def _rwkv_fwd(layer1, proj, vf, prm, pp, cat_width):
    t = proj.shape[0]
    dr = prm[1].shape[1]
    rwc = prm[0].shape[1]
    n_pair = dr // LANES
    nc = t // CHUNK
    n_prm = len(prm)
    specs, _, _, _ = _rwkv_specs(layer1, t, dr, rwc, n_pair, False)

    def body(*refs):
        y_ref, prev_ref = refs[0], refs[1]
        i = 2
        vf_ref = None
        if layer1:
            vf_ref = refs[i]
            i += 1
        prm_refs = refs[i:i + n_prm]
        i += n_prm
        pp_ref = refs[i]
        i += 1
        cat_ref = refs[i]
        i += 1
        vout_ref = None
        if not layer1:
            vout_ref = refs[i]
            i += 1
        mck_ref, x_s, m_s = refs[i], refs[i + 1], refs[i + 2]
        c, p = pl.program_id(0), pl.program_id(1)

        @pl.when((c == 0) & (p == 0))
        def _():
            m_s[...] = jnp.zeros_like(m_s)

        @pl.when(p == 0)
        def _():
            prev = prev_ref[pl.ds(7, 1), :] * (c != 0).astype(f32)
            xs = _rwkv_pre(layer1, tuple(r[...] for r in prm_refs), y_ref[...], prev,
                           vf_ref[...] if layer1 else None)
            for q, a in enumerate(xs):
                for j, piece in enumerate(_split_lanes(a, n_pair)):
                    x_s[q * n_pair + j] = piece
            if not layer1:
                vout_ref[...] = xs[2]

        m0 = m_s[pl.ds(p * grp, grp)]
        mck_ref[0] = m0
        ppv = tuple(pp_ref[:, pl.ds(q, 1), :] for q in range(5))
        og, m_new = _rwkv_pair(ppv, m0, tuple(x_s[pl.ds(q * n_pair + p * grp, grp)] for q in range(6)))
        for j in range(grp):
            cat_ref[:, j * LANES:(j + 1) * LANES] = og[j]
        m_s[pl.ds(p * grp, grp)] = m_new

    grp = _group(n_pair)
    out_shape = [jax.ShapeDtypeStruct((t, cat_width), f32)]
    out_specs = [pl.BlockSpec((CHUNK, grp * LANES), lambda c, p: (c, p))]
    if not layer1:
        out_shape.append(jax.ShapeDtypeStruct((t, dr), f32))
        out_specs.append(pl.BlockSpec((CHUNK, dr), lambda c, p: (c, 0)))
    out_shape.append(jax.ShapeDtypeStruct((nc, n_pair, LANES, LANES), f32))
    out_specs.append(pl.BlockSpec((1, grp, LANES, LANES), lambda c, p: (c, p, 0, 0)))
    args = [proj, proj] + ([vf] if layer1 else []) + list(prm) + [pp]
    return pl.pallas_call(
        body, grid=(nc, n_pair // grp), in_specs=specs, out_specs=out_specs, out_shape=out_shape,
        scratch_shapes=[pltpu.VMEM((6 * n_pair, CHUNK, LANES), f32), pltpu.VMEM((n_pair, LANES, LANES), f32)],
        compiler_params=pltpu.CompilerParams(dimension_semantics=("arbitrary", "arbitrary")),
        name=f"rwkv_fwd_l{int(layer1)}",
    )(*args)


def _rwkv_bwd(layer1, proj, vf, prm, pp, mck, dcat, dvout):
    t = proj.shape[0]
    dr = prm[1].shape[1]
    rwc = prm[0].shape[1]
    n_pair = dr // LANES
    nc = t // CHUNK
    n_prm = len(prm)
    specs, prm_shapes, cidx, full = _rwkv_specs(layer1, t, dr, rwc, n_pair, True)
    grp = _group(n_pair)
    n_step = n_pair // grp
    specs.append(pl.BlockSpec((1, grp, LANES, LANES), lambda c, p: (cidx(c), p, 0, 0)))
    specs.append(pl.BlockSpec((CHUNK, grp * LANES), lambda c, p: (cidx(c), p)))
    if not layer1:
        specs.append(pl.BlockSpec((CHUNK, dr), lambda c, p: (cidx(c), 0)))

    def body(*refs):
        y_ref, prev_ref = refs[0], refs[1]
        i = 2
        vf_ref = None
        if layer1:
            vf_ref = refs[i]
            i += 1
        prm_refs = refs[i:i + n_prm]
        i += n_prm
        pp_ref, mck_ref, dog_ref = refs[i], refs[i + 1], refs[i + 2]
        i += 3
        dvout_ref = None
        if not layer1:
            dvout_ref = refs[i]
            i += 1
        dy_ref = refs[i]
        i += 1
        dvf_ref = None
        if layer1:
            dvf_ref = refs[i]
            i += 1
        dprm_refs = refs[i:i + n_prm]
        i += n_prm
        dpp_ref = refs[i]
        x_s, dx_s, dm_s, dprev_s = refs[i + 1:i + 5]
        c, p = pl.program_id(0), pl.program_id(1)
        cr = nc - 1 - c

        def prev_row():
            return prev_ref[pl.ds(7, 1), :] * (cr != 0).astype(f32)

        @pl.when((c == 0) & (p == 0))
        def _():
            dm_s[...] = jnp.zeros_like(dm_s)
            dprev_s[...] = jnp.zeros_like(dprev_s)
            dpp_ref[...] = jnp.zeros_like(dpp_ref)
            for r in dprm_refs:
                r[...] = jnp.zeros_like(r)

        @pl.when(p == 0)
        def _():
            xs = _rwkv_pre(layer1, tuple(r[...] for r in prm_refs), y_ref[...], prev_row(),
                           vf_ref[...] if layer1 else None)
            for q, a in enumerate(xs):
                for j, piece in enumerate(_split_lanes(a, n_pair)):
                    x_s[q * n_pair + j] = piece

        units = pl.ds(p * grp, grp)
        ppv = tuple(pp_ref[:, pl.ds(q, 1), :] for q in range(5))
        xs_p = tuple(x_s[pl.ds(q * n_pair + p * grp, grp)] for q in range(6))
        dog = jnp.stack([dog_ref[:, j * LANES:(j + 1) * LANES] for j in range(grp)], axis=0)
        _, vjp_pair = jax.vjp(_rwkv_pair, ppv, mck_ref[0], xs_p)
        dppv, dm0, dxs = vjp_pair((dog, dm_s[units]))
        dm_s[units] = dm0
        for q in range(6):
            dx_s[pl.ds(q * n_pair + p * grp, grp)] = dxs[q]
        for q in range(5):
            dpp_ref[units, pl.ds(q, 1), :] += dppv[q]

        @pl.when(p == n_step - 1)
        def _():
            dxs_full = [jnp.concatenate([dx_s[q * n_pair + j] for j in range(n_pair)], axis=1) for q in range(6)]
            if not layer1:
                dxs_full[2] = dxs_full[2] + dvout_ref[...]
            prm_v = tuple(r[...] for r in prm_refs)
            if layer1:
                _, vjp_pre = jax.vjp(functools.partial(_rwkv_pre, True), prm_v, y_ref[...], prev_row(), vf_ref[...])
                dprm, dy, dprev, dvf = vjp_pre(tuple(dxs_full))
                dvf_ref[...] = dvf
            else:
                _, vjp_pre = jax.vjp(lambda a, b, d: _rwkv_pre(False, a, b, d, None), prm_v, y_ref[...], prev_row())
                dprm, dy, dprev = vjp_pre(tuple(dxs_full))
            dy_ref[...] = (dy + jnp.where(_iota((CHUNK, 1), 0) == CHUNK - 1, dprev_s[...], 0.0)).astype(bf16)
            dprev_s[...] = dprev
            for r, gval in zip(dprm_refs, dprm):
                r[...] += gval

    out_shape = [jax.ShapeDtypeStruct((t, rwc), bf16)]
    out_specs = [pl.BlockSpec((CHUNK, rwc), lambda c, p: (cidx(c), 0))]
    if layer1:
        out_shape.append(jax.ShapeDtypeStruct((t, dr), f32))
        out_specs.append(pl.BlockSpec((CHUNK, dr), lambda c, p: (cidx(c), 0)))
    out_shape += [jax.ShapeDtypeStruct(s, f32) for s in prm_shapes]
    out_specs += [full(s) for s in prm_shapes]
    out_shape.append(jax.ShapeDtypeStruct((n_pair, 8, LANES), f32))
    out_specs.append(full((n_pair, 8, LANES)))
    args = [proj, proj] + ([vf] if layer1 else []) + list(prm) + [pp, mck, dcat] + ([] if layer1 else [dvout])
    return pl.pallas_call(
        body, grid=(nc, n_step), in_specs=specs, out_specs=out_specs, out_shape=out_shape,
        scratch_shapes=[pltpu.VMEM((6 * n_pair, CHUNK, LANES), f32), pltpu.VMEM((6 * n_pair, CHUNK, LANES), f32),
                        pltpu.VMEM((n_pair, LANES, LANES), f32), pltpu.VMEM((1, rwc), f32)],
        compiler_params=pltpu.CompilerParams(dimension_semantics=("arbitrary", "arbitrary")),
        name=f"rwkv_bwd_l{int(layer1)}",
    )(*args)


def _hgrn_chunk(layer1, lbl, gnw, s0, q_raw, f_raw, i_in, z):
    c = q_raw.shape[-2]
    q = _silu(q_raw)
    ls = _log_sigmoid(f_raw)
    if layer1:
        l0, l1 = lbl[..., 0:1, :], lbl[..., 1:2, :]
        mx = jnp.maximum(l0, l1)
        e0, e1 = jnp.exp(l0 - mx), jnp.exp(l1 - mx)
        sm0, sm1 = e0 / (e0 + e1), e1 / (e0 + e1)
        lb = (sm0 + sm1) - sm0
        log_f = _logaddexp(jnp.log(jnp.maximum(lb, LB_FLOOR)), jnp.log1p(-lb) + ls)
        k = (1.0 - lb) * jax.nn.sigmoid(-f_raw)
    else:
        log_f = _logaddexp(jnp.full_like(ls, jnp.log(jnp.float32(LB_FLOOR))), ls)
        k = jax.nn.sigmoid(-f_raw)
    row, col = _iota((c, c), 0), _iota((c, c), 1)
    trow = _iota((c, 1), 0)
    halves = []
    half = c // 2
    while half >= 1:
        halves.append(half)
        half //= 2
    cmat = jnp.concatenate([(col <= row).astype(f32)]
                           + [(col <= (row // (2 * hf)) * (2 * hf) + hf - 1).astype(f32) for hf in halves], axis=0)
    ball = _const_left(cmat.astype(bf16), log_f)
    b = ball[..., :c, :]
    att = None
    for lvl, hf in enumerate(halves):
        blk = 2 * hf
        bref = ball[..., (lvl + 1) * c:(lvl + 2) * c, :]
        upper = (trow % blk) >= hf
        qh = q * jnp.exp(jnp.where(upper, b - bref, 0.0)) * upper.astype(f32)
        kh = k * jnp.exp(jnp.where(upper, 0.0, bref - b)) * (1.0 - upper.astype(f32))
        term = jnp.where(row // blk == col // blk, _mm3(qh, kh, "nt"), 0.0)
        att = term if att is None else att + term
    lhs = jnp.concatenate([q * jnp.exp(b), att, jnp.zeros(att.shape[:-1] + (LANES - c,), f32)], axis=-1)
    rhs = jnp.concatenate([s0, i_in, jnp.zeros(i_in.shape[:-2] + (LANES - c, i_in.shape[-1]), f32)], axis=-2)
    o = _mm3(lhs, rhs, "nn") + jnp.sum(q * k, axis=-1, keepdims=True) * i_in
    b_last = _last_row(b)
    s_new = _col_of_row(jnp.exp(b_last)) * s0 + _mm3(k * jnp.exp(b_last - b), i_in, "tn")
    o = o * lax.rsqrt(jnp.mean(o * o, axis=-1, keepdims=True) + RMS_EPS)
    return o * gnw * _silu(z), s_new


def _hgrn_in_specs(t, dh, col0, rev):
    nc = t // CHUNK
    nh = dh // LANES

    def cidx(c):
        return (nc - 1 - c) if rev else c

    grp = _group(nh)
    specs = [pl.BlockSpec((CHUNK, LANES), functools.partial(lambda g, j, h, c: (cidx(c), col0 + g * nh + h * grp + j), g, j))
             for j in range(grp) for g in range(4)]
    specs.append(pl.BlockSpec((2, grp * LANES), lambda h, c: (0, h)))
    specs.append(pl.BlockSpec((1, grp * LANES), lambda h, c: (0, h)))
    return specs, cidx, grp


def _hgrn_fwd(layer1, proj, lbl, gnw, cat, rwc):
    t, d = cat.shape
    dh = gnw.shape[1]
    nh = dh // LANES
    nc = t // CHUNK
    col0 = rwc // LANES
    specs, _, grp = _hgrn_in_specs(t, dh, col0, False)
    specs.append(pl.BlockSpec(memory_space=pl.ANY))
    assert (d - dh) % (grp * LANES) == 0
    cat_col0 = (d - dh) // (grp * LANES)

    def body(*refs):
        x_refs = refs[:4 * grp]
        lbl_ref, gnw_ref, _, cat_ref, sck_ref, s_s = refs[4 * grp:]
        c = pl.program_id(1)

        @pl.when(c == 0)
        def _():
            s_s[...] = jnp.zeros_like(s_s)

        lanes = [slice(j * LANES, (j + 1) * LANES) for j in range(grp)]
        s0 = s_s[...]
        sck_ref[:, 0] = s0
        out, s_new = _hgrn_chunk(layer1, jnp.stack([lbl_ref[:, ln] for ln in lanes]), jnp.stack([gnw_ref[:, ln] for ln in lanes]),
                                 s0, *(jnp.stack([x_refs[4 * j + g][...] for j in range(grp)]) for g in range(4)))
        for j in range(grp):
            cat_ref[:, lanes[j]] = out[j]
        s_s[...] = s_new

    return pl.pallas_call(
        body, grid=(nh // grp, nc), in_specs=specs,
        out_specs=[pl.BlockSpec((CHUNK, grp * LANES), lambda h, c: (c, cat_col0 + h)),
                   pl.BlockSpec((grp, 1, LANES, LANES), lambda h, c: (h, c, 0, 0))],
        out_shape=[jax.ShapeDtypeStruct((t, d), f32), jax.ShapeDtypeStruct((nh, nc, LANES, LANES), f32)],
        scratch_shapes=[pltpu.VMEM((grp, LANES, LANES), f32)],
        input_output_aliases={4 * grp + 2: 0},
        compiler_params=pltpu.CompilerParams(dimension_semantics=("arbitrary", "arbitrary")),
        name=f"hgrn_fwd_l{int(layer1)}",
    )(*([proj] * (4 * grp)), lbl, gnw, cat)


def _hgrn_bwd(layer1, proj, lbl, gnw, sck, dcat, rwc):
    t, d = dcat.shape
    dh = gnw.shape[1]
    nh = dh // LANES
    nc = t // CHUNK
    col0 = rwc // LANES
    specs, cidx, grp = _hgrn_in_specs(t, dh, col0, True)
    assert (d - dh) % (grp * LANES) == 0
    cat_col0 = (d - dh) // (grp * LANES)
    specs.append(pl.BlockSpec((grp, 1, LANES, LANES), lambda h, c: (h, cidx(c), 0, 0)))
    specs.append(pl.BlockSpec((CHUNK, grp * LANES), lambda h, c: (cidx(c), cat_col0 + h)))

    def body(*refs):
        x_refs = refs[:4 * grp]
        lbl_ref, gnw_ref, sck_ref, do_ref, dp_ref, dlbl_ref, dgnw_ref, ds_s = refs[4 * grp:]
        c = pl.program_id(1)

        @pl.when(c == 0)
        def _():
            ds_s[...] = jnp.zeros_like(ds_s)
            dlbl_ref[...] = jnp.zeros_like(dlbl_ref)
            dgnw_ref[...] = jnp.zeros_like(dgnw_ref)

        lanes = [slice(j * LANES, (j + 1) * LANES) for j in range(grp)]
        _, vjp = jax.vjp(functools.partial(_hgrn_chunk, layer1),
                         jnp.stack([lbl_ref[:, ln] for ln in lanes]), jnp.stack([gnw_ref[:, ln] for ln in lanes]), sck_ref[:, 0],
                         *(jnp.stack([x_refs[4 * j + g][...] for j in range(grp)]) for g in range(4)))
        dlbl, dgnw, ds0, dq, df, di, dz = vjp((jnp.stack([do_ref[:, ln] for ln in lanes]), ds_s[...]))
        ds_s[...] = ds0
        for j in range(grp):
            dlbl_ref[:, lanes[j]] += dlbl[j]
            dgnw_ref[:, lanes[j]] += dgnw[j]
            for g, val in enumerate((dq, df, di, dz)):
                dp_ref[g, :, lanes[j]] = val[j].astype(bf16)

    return pl.pallas_call(
        body, grid=(nh // grp, nc), in_specs=specs,
        out_specs=[pl.BlockSpec((4, CHUNK, grp * LANES), lambda h, c: (0, cidx(c), h)),
                   pl.BlockSpec((2, grp * LANES), lambda h, c: (0, h)),
                   pl.BlockSpec((1, grp * LANES), lambda h, c: (0, h))],
        out_shape=[jax.ShapeDtypeStruct((4, t, dh), bf16), jax.ShapeDtypeStruct((2, dh), f32),
                   jax.ShapeDtypeStruct((1, dh), f32)],
        scratch_shapes=[pltpu.VMEM((grp, LANES, LANES), f32)],
        compiler_params=pltpu.CompilerParams(dimension_semantics=("arbitrary", "arbitrary")),
        name=f"hgrn_bwd_l{int(layer1)}",
    )(*([proj] * (4 * grp)), lbl, gnw, sck, dcat)


def _ln(h, y, w, b):
    u = ALPHA * h + y
    mu = jnp.mean(u, axis=-1, keepdims=True)
    var = jnp.mean(jnp.square(u - mu), axis=-1, keepdims=True)
    return (u - mu) * lax.rsqrt(var + LN_EPS) * w + b


def _row_tile(t):
    return 256 if t % 256 == 0 else t


def _ln_fwd(h, y, w, b):
    t, d = h.shape
    tr = _row_tile(t)

    def body(h_ref, y_ref, w_ref, b_ref, o_ref, o16_ref):
        out = _ln(h_ref[...], y_ref[...], w_ref[...], b_ref[...])
        o_ref[...] = out
        o16_ref[...] = out.astype(bf16)

    row = pl.BlockSpec((tr, d), lambda i: (i, 0))
    vec = pl.BlockSpec((1, d), lambda i: (0, 0))
    return pl.pallas_call(body, grid=(t // tr,), in_specs=[row, row, vec, vec], out_specs=[row, row],
                          out_shape=[jax.ShapeDtypeStruct((t, d), f32), jax.ShapeDtypeStruct((t, d), bf16)],
                          name="ln_fwd")(h, y, w, b)


def _ln_loss(h, y, w, b, tgt):
    t, d = h.shape
    tr = _row_tile(t)

    def body(h_ref, y_ref, w_ref, b_ref, t_ref, g_ref, loss_ref):
        @pl.when(pl.program_id(0) == 0)
        def _():
            loss_ref[...] = jnp.zeros_like(loss_ref)

        err = _ln(h_ref[...], y_ref[...], w_ref[...], b_ref[...]) - t_ref[...]
        g_ref[...] = err * (1.0 / d)
        loss_ref[...] += 0.5 * jnp.sum(jnp.mean(jnp.square(err), axis=-1, keepdims=True), axis=0, keepdims=True)

    row = pl.BlockSpec((tr, d), lambda i: (i, 0))
    vec = pl.BlockSpec((1, d), lambda i: (0, 0))
    return pl.pallas_call(
        body, grid=(t // tr,), in_specs=[row, row, vec, vec, row],
        out_specs=[row, pl.BlockSpec((1, LANES), lambda i: (0, 0))],
        out_shape=[jax.ShapeDtypeStruct((t, d), f32), jax.ShapeDtypeStruct((1, LANES), f32)],
        compiler_params=pltpu.CompilerParams(dimension_semantics=("arbitrary",)), name="ln_loss")(h, y, w, b, tgt)


def _ln_bwd(h, y, w, b, dout):
    t, d = h.shape
    tr = _row_tile(t)

    def body(h_ref, y_ref, w_ref, b_ref, do_ref, dy_ref, dy16_ref, dw_ref, db_ref):
        @pl.when(pl.program_id(0) == 0)
        def _():
            dw_ref[...] = jnp.zeros_like(dw_ref)
            db_ref[...] = jnp.zeros_like(db_ref)

        _, vjp = jax.vjp(lambda yy, ww, bb: _ln(h_ref[...], yy, ww, bb), y_ref[...], w_ref[...], b_ref[...])
        dy, dw, db = vjp(do_ref[...])
        dy_ref[...] = dy
        dy16_ref[...] = dy.astype(bf16)
        dw_ref[...] += dw
        db_ref[...] += db

    row = pl.BlockSpec((tr, d), lambda i: (i, 0))
    vec = pl.BlockSpec((1, d), lambda i: (0, 0))
    return pl.pallas_call(
        body, grid=(t // tr,), in_specs=[row, row, vec, vec, row], out_specs=[row, row, vec, vec],
        out_shape=[jax.ShapeDtypeStruct((t, d), f32), jax.ShapeDtypeStruct((t, d), bf16),
                   jax.ShapeDtypeStruct((1, d), f32), jax.ShapeDtypeStruct((1, d), f32)],
        compiler_params=pltpu.CompilerParams(dimension_semantics=("arbitrary",)), name="ln_bwd")(h, y, w, b, dout)


def _pick(n, prefs):
    for p in prefs:
        if n % p == 0:
            return p
    return n


def _tile(n, want):
    if n <= want:
        return n
    for cand in range(want - want % LANES, 0, -LANES):
        if n % cand == 0:
            return cand
    return n


def _matmul(a, b, mode, name, tiles, add=None, add_scale=1.0, out_dtype=f32):
    if mode == "nn":
        (m, k), n = a.shape, b.shape[1]
    elif mode == "nt":
        (m, k), n = a.shape, b.shape[0]
    else:
        (k, m), n = a.shape, b.shape[1]
    tm, tn, tk = _tile(m, tiles[0]), _tile(n, tiles[1]), _tile(k, tiles[2])
    nk = k // tk
    cache_a = nk == 1 and a.dtype != bf16 and n // tn > 1

    def body(*refs):
        a_ref, b_ref = refs[0], refs[1]
        add_ref = refs[2] if add is not None else None
        n_in = 3 if add is not None else 2
        o_ref = refs[n_in]
        scratch = refs[n_in + 1:]

        def finish(res):
            if add is not None:
                res = res + add_scale * add_ref[...]
            o_ref[...] = res.astype(out_dtype)

        if cache_a:
            a_bf = scratch[0]

            @pl.when(pl.program_id(1) == 0)
            def _():
                a_bf[...] = a_ref[...].astype(bf16)

            a_val = a_bf[...]
        else:
            a_val = a_ref[...].astype(bf16)
        prod = lax.dot_general(a_val, b_ref[...].astype(bf16), _DIMS[mode], preferred_element_type=f32)
        if nk == 1:
            finish(prod)
        else:
            acc = scratch[-1]
            kk = pl.program_id(2)

            @pl.when(kk == 0)
            def _():
                acc[...] = prod

            @pl.when(kk != 0)
            def _():
                acc[...] += prod

            @pl.when(kk == nk - 1)
            def _():
                finish(acc[...])

    a_shape = (tk, tm) if mode == "tn" else (tm, tk)
    a_spec = pl.BlockSpec(a_shape, (lambda i, j, kk: (kk, i)) if mode == "tn" else (lambda i, j, kk: (i, kk)))
    b_spec = pl.BlockSpec((tn, tk), lambda i, j, kk: (j, kk)) if mode == "nt" else pl.BlockSpec((tk, tn), lambda i, j, kk: (kk, j))
    o_spec = pl.BlockSpec((tm, tn), lambda i, j, kk: (i, j))
    in_specs = [a_spec, b_spec] + ([o_spec] if add is not None else [])
    args = [a, b] + ([add] if add is not None else [])
    scratch_shapes = ([pltpu.VMEM(a_shape, bf16)] if cache_a else []) + ([pltpu.VMEM((tm, tn), f32)] if nk > 1 else [])
    return pl.pallas_call(
        body, grid=(m // tm, n // tn, nk), in_specs=in_specs, out_specs=o_spec,
        out_shape=jax.ShapeDtypeStruct((m, n), out_dtype), scratch_shapes=scratch_shapes,
        compiler_params=pltpu.CompilerParams(dimension_semantics=("parallel", "arbitrary", "arbitrary")),
        name=name,
    )(*args)


def _position():
    return lax.axis_index("x"), lax.axis_index("y"), lax.axis_index("c")


def _flip(pos, k):
    x, y, c = pos
    return (1 - x if k & 4 else x, 1 - y if k & 2 else y, 1 - c if k & 1 else c)


def _index(pos):
    return 4 * pos[0] + 2 * pos[1] + pos[2]


def _all_gather_rows(x, name):
    m_per, n = x.shape

    def body(x_ref, out_ref, send_sems, recv_sems, local_sem):
        me = _position()
        sibling = _flip(me, 1)
        chips = (2, 4, 6)

        def rows(pos):
            return out_ref.at[pl.ds(_index(pos) * m_per, m_per), :]

        def copy(sem, block, to, src=None):
            return pltpu.make_async_remote_copy(
                src_ref=rows(block) if src is None else src, dst_ref=rows(block),
                send_sem=send_sems.at[sem], recv_sem=recv_sems.at[sem], device_id=to, device_id_type=MESH)

        mine = pltpu.make_async_copy(x_ref, rows(me), local_sem)
        mine.start()
        first = [copy(0, me, sibling, src=x_ref)]
        first += [copy(1 + j, me, _flip(me, k), src=x_ref) for j, k in enumerate(chips)]
        for cp in first:
            cp.start()
        passed = [copy(4 + j, _flip(me, k), sibling) for j, k in enumerate(chips)]
        for j, k in enumerate(chips):
            copy(1 + j, _flip(me, k), me).wait_recv()
            passed[j].start()
        copy(0, sibling, me).wait_recv()
        for j, k in enumerate(chips):
            copy(4 + j, _flip(sibling, k), me).wait_recv()
        for cp in first + passed:
            cp.wait_send()
        mine.wait()

    return pl.pallas_call(
        body, out_shape=jax.ShapeDtypeStruct((N_DEV * m_per, n), x.dtype),
        in_specs=[pl.BlockSpec(memory_space=pl.ANY)], out_specs=pl.BlockSpec(memory_space=pl.ANY),
        scratch_shapes=[pltpu.SemaphoreType.DMA((7,)), pltpu.SemaphoreType.DMA((7,)), pltpu.SemaphoreType.DMA(())],
        name=name,
    )(x)


def _exchange_siblings(gs, name):
    n_arr = len(gs)

    def body(*refs):
        g_refs, out_refs = refs[:n_arr], refs[n_arr:2 * n_arr]
        send_sems, recv_sems = refs[2 * n_arr:]
        me = _position()
        c = me[2]
        sibling = _flip(me, 1)
        copies = []
        for i, (g_ref, out_ref) in enumerate(zip(g_refs, out_refs)):
            m_per = g_ref.shape[0] // N_DEV
            for q in range(4):
                copies.append(pltpu.make_async_remote_copy(
                    src_ref=g_ref.at[pl.ds((2 * q + 1 - c) * m_per, m_per), :], dst_ref=out_ref.at[q],
                    send_sem=send_sems.at[4 * i + q], recv_sem=recv_sems.at[4 * i + q],
                    device_id=sibling, device_id_type=MESH))
        for cp in copies:
            cp.start()
        for cp in copies:
            cp.wait_recv()
        for cp in copies:
            cp.wait_send()

    anyspec = pl.BlockSpec(memory_space=pl.ANY)
    return pl.pallas_call(
        body, out_shape=[jax.ShapeDtypeStruct((4, g.shape[0] // N_DEV, g.shape[1]), g.dtype) for g in gs],
        in_specs=[anyspec] * n_arr, out_specs=[anyspec] * n_arr,
        scratch_shapes=[pltpu.SemaphoreType.DMA((4 * n_arr,))] * 2, name=name,
    )(*gs)


def _sum_with_sibling(g, recv, name):
    m = g.shape[0] // N_DEV
    n = g.shape[1]
    tr = _pick(m, (208, 128, 64, 32, 16))
    nt = m // tr

    def body(g_ref, r_ref, o_ref):
        c = lax.axis_index("c")
        own = jnp.where(c == 0, g_ref[0, 0].astype(f32), g_ref[0, 1].astype(f32))
        o_ref[...] = (own + r_ref[0].astype(f32)).astype(o_ref.dtype)

    return pl.pallas_call(
        body, grid=(4, nt),
        in_specs=[pl.BlockSpec((1, 2, tr, n), lambda q, i: (q, 0, i, 0)), pl.BlockSpec((1, tr, n), lambda q, i: (q, i, 0))],
        out_specs=pl.BlockSpec((tr, n), lambda q, i: (q * nt + i, 0)),
        out_shape=jax.ShapeDtypeStruct((4 * m, n), bf16), name=name,
    )(g.reshape(4, 2, m, n), recv)


def _exchange_chips(hs, name):
    n_arr = len(hs)
    chips = (2, 4, 6)

    def body(*refs):
        h_refs, out_refs = refs[:n_arr], refs[n_arr:2 * n_arr]
        send_sems, recv_sems = refs[2 * n_arr:]
        me = _position()
        copies = []
        for i, (h_ref, out_ref) in enumerate(zip(h_refs, out_refs)):
            m_per = h_ref.shape[0] // 4
            for j, k in enumerate(chips):
                peer = _flip(me, k)
                peer_q = 2 * peer[0] + peer[1]
                copies.append(pltpu.make_async_remote_copy(
                    src_ref=h_ref.at[pl.ds(peer_q * m_per, m_per), :], dst_ref=out_ref.at[j],
                    send_sem=send_sems.at[3 * i + j], recv_sem=recv_sems.at[3 * i + j],
                    device_id=peer, device_id_type=MESH))
        for cp in copies:
            cp.start()
        for cp in copies:
            cp.wait_recv()
        for cp in copies:
            cp.wait_send()

    anyspec = pl.BlockSpec(memory_space=pl.ANY)
    return pl.pallas_call(
        body, out_shape=[jax.ShapeDtypeStruct((3, h.shape[0] // 4, h.shape[1]), h.dtype) for h in hs],
        in_specs=[anyspec] * n_arr, out_specs=[anyspec] * n_arr,
        scratch_shapes=[pltpu.SemaphoreType.DMA((3 * n_arr,))] * 2, name=name,
    )(*hs)


def _sum_with_chips(h, recv, name):
    m = h.shape[0] // 4
    n = h.shape[1]
    tr = _pick(m, (208, 128, 64, 32, 16))

    def body(h_ref, r_ref, o_ref):
        my_q = 2 * lax.axis_index("x") + lax.axis_index("y")
        own = h_ref[0].astype(f32)
        for q in range(1, 4):
            own = jnp.where(my_q == q, h_ref[q].astype(f32), own)
        o_ref[...] = ((own + r_ref[0].astype(f32)) + r_ref[1].astype(f32)) + r_ref[2].astype(f32)

    return pl.pallas_call(
        body, grid=(m // tr,),
        in_specs=[pl.BlockSpec((4, tr, n), lambda i: (0, i, 0)), pl.BlockSpec((3, tr, n), lambda i: (0, i, 0))],
        out_specs=pl.BlockSpec((tr, n), lambda i: (i, 0)), out_shape=jax.ShapeDtypeStruct((m, n), f32), name=name,
    )(h.reshape(4, m, n), recv)


def _sum_slots(parts, name):
    n_slot, m, n = parts.shape
    tr = _pick(m, (208, 128, 64, 32, 16, 8))

    def body(p_ref, o_ref):
        acc = p_ref[0]
        for s in range(1, n_slot):
            acc = acc + p_ref[s]
        o_ref[...] = acc

    return pl.pallas_call(
        body, grid=(m // tr,), in_specs=[pl.BlockSpec((n_slot, tr, n), lambda i: (0, i, 0))],
        out_specs=pl.BlockSpec((tr, n), lambda i: (i, 0)), out_shape=jax.ShapeDtypeStruct((m, n), parts.dtype), name=name,
    )(parts)


def _reduce_scatter(gs, name):
    from_sibling = _exchange_siblings(gs, "rs_d2d_" + name)
    chip_sums = [_sum_with_sibling(g, r, f"rs_sum2_{name}_{i}") for i, (g, r) in enumerate(zip(gs, from_sibling))]
    from_chips = _exchange_chips(chip_sums, "rs_ici_" + name)
    return [_sum_with_chips(h, r, f"rs_sum4_{name}_{i}") for i, (h, r) in enumerate(zip(chip_sums, from_chips))]


def _adamw(w, g, m, v, name):
    shape = w.shape
    n = shape[-1]
    r = w.size // n
    w2, g2, m2, v2 = (a.reshape(r, n) for a in (w, g, m, v))
    tr = _pick(r, (256, 128, 64, 32, 16, 8)) if r * n > 65536 else r

    def body(w_ref, g_ref, m_ref, v_ref, d_ref, mo_ref, vo_ref):
        gg = g_ref[...]
        mm = ADAM_B1 * m_ref[...] + (1.0 - ADAM_B1) * gg
        vv = ADAM_B2 * v_ref[...] + (1.0 - ADAM_B2) * jnp.square(gg)
        m_hat = mm / (1.0 - ADAM_B1 ** ADAM_STEP)
        v_hat = vv / (1.0 - ADAM_B2 ** ADAM_STEP)
        d_ref[...] = -ADAM_LR * (m_hat / (jnp.sqrt(v_hat) + ADAM_EPS) + ADAM_WD * w_ref[...])
        mo_ref[...] = mm
        vo_ref[...] = vv

    spec = pl.BlockSpec((tr, n), lambda i: (i, 0))
    outs = pl.pallas_call(
        body, grid=(r // tr,), in_specs=[spec] * 4, out_specs=[spec] * 3,
        out_shape=[jax.ShapeDtypeStruct((r, n), f32)] * 3, name=name,
    )(w2, g2, m2, v2)
    return tuple(o.reshape(shape) for o in outs)


_SMALL = ("shift_mu", "w_decay0", "a0", "k_k", "k_a", "r_k", "ln_x_w", "ln_x_b", "v_mix0", "lb_logits",
          "g_norm_w", "ln_w", "ln_b", "w_decay_up", "a_up", "v_mix_down", "v_mix_up")
_NAMES = ("w_in", "shift_mu", "w_decay0", "w_decay_up", "a0", "a_up", "k_k", "k_a", "r_k", "ln_x_w", "ln_x_b",
          "v_mix0", "v_mix_down", "v_mix_up", "lb_logits", "g_norm_w", "w_out", "ln_w", "ln_b")


def _pad_rows(a, rows, at_end):
    z = jnp.zeros((rows - a.shape[0], a.shape[1]), a.dtype)
    return jnp.concatenate([a, z] if at_end else [z, a], axis=0)


def kernel(x, w_in, shift_mu, w_decay0, w_decay_up, a0, a_up, k_k, k_a, r_k, ln_x_w, ln_x_b, v_mix0, v_mix_down, v_mix_up, lb_logits, g_norm_w, w_out, ln_w, ln_b, loss_target, m_w_in, m_shift_mu, m_w_decay0, m_w_decay_up, m_a0, m_a_up, m_k_k, m_k_a, m_r_k, m_ln_x_w, m_ln_x_b, m_v_mix0, m_v_mix_down, m_v_mix_up, m_lb_logits, m_g_norm_w, m_w_out, m_ln_w, m_ln_b, v_w_in, v_shift_mu, v_w_decay0, v_w_decay_up, v_a0, v_a_up, v_k_k, v_k_a, v_r_k, v_ln_x_w, v_ln_x_b, v_v_mix0, v_v_mix_down, v_v_mix_up, v_lb_logits, v_g_norm_w, v_w_out, v_ln_w, v_ln_b):
    weights = dict(w_in=w_in, shift_mu=shift_mu, w_decay0=w_decay0, w_decay_up=w_decay_up, a0=a0, a_up=a_up, k_k=k_k,
                   k_a=k_a, r_k=r_k, ln_x_w=ln_x_w, ln_x_b=ln_x_b, v_mix0=v_mix0, v_mix_down=v_mix_down,
                   v_mix_up=v_mix_up, lb_logits=lb_logits, g_norm_w=g_norm_w, w_out=w_out, ln_w=ln_w, ln_b=ln_b)
    mom1 = dict(w_in=m_w_in, shift_mu=m_shift_mu, w_decay0=m_w_decay0, w_decay_up=m_w_decay_up, a0=m_a0, a_up=m_a_up,
                k_k=m_k_k, k_a=m_k_a, r_k=m_r_k, ln_x_w=m_ln_x_w, ln_x_b=m_ln_x_b, v_mix0=m_v_mix0,
                v_mix_down=m_v_mix_down, v_mix_up=m_v_mix_up, lb_logits=m_lb_logits, g_norm_w=m_g_norm_w,
                w_out=m_w_out, ln_w=m_ln_w, ln_b=m_ln_b)
    mom2 = dict(w_in=v_w_in, shift_mu=v_shift_mu, w_decay0=v_w_decay0, w_decay_up=v_w_decay_up, a0=v_a0, a_up=v_a_up,
                k_k=v_k_k, k_a=v_k_a, r_k=v_r_k, ln_x_w=v_ln_x_w, ln_x_b=v_ln_x_b, v_mix0=v_v_mix0,
                v_mix_down=v_v_mix_down, v_mix_up=v_v_mix_up, lb_logits=v_lb_logits, g_norm_w=v_g_norm_w,
                w_out=v_w_out, ln_w=v_ln_w, ln_b=v_ln_b)
    assert x.shape[0] == 1 and w_in.shape[0] == DEPTH
    t, d = x.shape[1], x.shape[2]
    dr = w_decay0.shape[1]
    dh = g_norm_w.shape[1]
    rank_w, rank_a, rank_v = w_decay_up.shape[1], a_up.shape[1], v_mix_up.shape[1]
    rwc = 4 * dr + rank_w + rank_a
    assert rank_w + rank_a == LANES and rank_v <= LANES and dr + dh == d
    assert t % CHUNK == 0 and dr % LANES == 0 and dh % LANES == 0 and shift_mu.shape[1] == rwc
    n_pair = dr // LANES
    me = _index(_position())

    win_t = [_all_gather_rows(w_in[l].T.astype(bf16), f"ag_w_in_{l}") for l in range(DEPTH)]
    wout = [_all_gather_rows(w_out[l].astype(bf16), f"ag_w_out_{l}") for l in range(DEPTH)]
    shard = dr // N_DEV
    pack = jnp.concatenate([w_decay_up[0], w_decay_up[1], a_up[0], a_up[1], v_mix_up[0], v_mix_down[0].T], axis=0)
    pack = _all_gather_rows(pack, "ag_small")
    pack = jnp.transpose(pack.reshape(N_DEV, -1, shard), (1, 0, 2)).reshape(-1, dr)
    offs = [0, rank_w, 2 * rank_w, 2 * rank_w + rank_a, 2 * rank_w + 2 * rank_a, 2 * rank_w + 2 * rank_a + rank_v,
            2 * rank_w + 2 * rank_a + 2 * rank_v]
    wdu_f = [pack[offs[0]:offs[1]], pack[offs[1]:offs[2]]]
    aup_f = [pack[offs[2]:offs[3]], pack[offs[3]:offs[4]]]
    vup_f = pack[offs[4]:offs[5]]
    vdown_f = pack[offs[5]:offs[6]].T

    def rwkv_params(l):
        prm = [shift_mu[l:l + 1], w_decay0[l:l + 1], a0[l:l + 1], _pad_rows(wdu_f[l], LANES, True),
               _pad_rows(aup_f[l], LANES, False)]
        if l == 1:
            prm += [v_mix0[0:1], _pad_rows(vdown_f.T, LANES, True).T, _pad_rows(vup_f, LANES, True)]
        rows = jnp.stack([k_k[l], k_a[l], r_k[l], ln_x_w[l], ln_x_b[l]] + [jnp.zeros((dr,), f32)] * 3, axis=0)
        pp = jnp.transpose(rows.reshape(8, n_pair, LANES), (1, 0, 2))
        return tuple(prm), pp

    h = x[0]
    h16 = h.astype(bf16)
    tgt = loss_target[0]
    saved = []
    vfirst = None
    for l in range(DEPTH):
        prm, pp = rwkv_params(l)
        proj = _matmul(h16, win_t[l], "nt", f"mm_proj_{l}", (2048, 640, 2048))
        if l == 0:
            cat, vfirst, mck = _rwkv_fwd(False, proj, None, prm, pp, d)
        else:
            cat, mck = _rwkv_fwd(True, proj, vfirst, prm, pp, d)
        cat, sck = _hgrn_fwd(l == 1, proj, lb_logits, g_norm_w[l:l + 1], cat, rwc)
        y = _matmul(cat, wout[l], "nn", f"mm_out_{l}", (1024, 1024, 2048))
        saved.append((h, h16, proj, prm, pp, mck, sck, cat, y))
        if l < DEPTH - 1:
            h, h16 = _ln_fwd(h, y, ln_w[l:l + 1], ln_b[l:l + 1])
        else:
            dh_out, loss_part = _ln_loss(h, y, ln_w[l:l + 1], ln_b[l:l + 1], tgt)
    loss = lax.psum(loss_part[0, 0], ("x", "y", "c"))

    grads = {}
    big = {}
    dvfirst = None
    d_lbl = None
    for l in reversed(range(DEPTH)):
        h_l, h16_l, proj, prm, pp, mck, sck, cat, y = saved[l]
        dy, dy16, g_ln_w, g_ln_b = _ln_bwd(h_l, y, ln_w[l:l + 1], ln_b[l:l + 1], dh_out)
        dcat = _matmul(dy16, wout[l], "nt", f"mm_dcat_{l}", (1024, 1024, 2048))
        big[("w_out", l)] = _matmul(cat, dy16, "tn", f"mm_dwout_{l}", (512, 2048, 2048), out_dtype=bf16)
        if l == 1:
            outs = _rwkv_bwd(True, proj, vfirst, prm, pp, mck, dcat, None)
            dproj_r, dvfirst = outs[0], outs[1]
            dprm, dpp = outs[2:-1], outs[-1]
        else:
            outs = _rwkv_bwd(False, proj, None, prm, pp, mck, dcat, dvfirst)
            dproj_r = outs[0]
            dprm, dpp = outs[1:-1], outs[-1]
        dproj_h, dlbl_l, dgnw = _hgrn_bwd(l == 1, proj, lb_logits, g_norm_w[l:l + 1], sck, dcat, rwc)
        dproj = jnp.concatenate([dproj_r] + [dproj_h[i] for i in range(4)], axis=1)
        dh_out = _matmul(dproj, win_t[l], "nn", f"mm_dh_{l}", (1024, 1024, 640), add=dy, add_scale=ALPHA)
        big[("w_in", l)] = _matmul(dproj, h16_l, "tn", f"mm_dwin_{l}", (640, 2048, 2048), out_dtype=bf16)
        dpp = jnp.transpose(dpp, (1, 0, 2)).reshape(8, dr)
        grads[l] = dict(shift_mu=dprm[0][0], w_decay0=dprm[1][0], a0=dprm[2][0], w_decay_up=dprm[3][:rank_w],
                        a_up=dprm[4][rank_w:], k_k=dpp[0], k_a=dpp[1], r_k=dpp[2], ln_x_w=dpp[3], ln_x_b=dpp[4],
                        g_norm_w=dgnw[0], ln_w=g_ln_w[0], ln_b=g_ln_b[0])
        if l == 1:
            grads[l].update(v_mix0=dprm[5][0], v_mix_down=dprm[6][:, :rank_v], v_mix_up=dprm[7][:rank_v])
            d_lbl = dlbl_l
    grad_x = dh_out[None]

    reduced = [_reduce_scatter([big[("w_in", l)], big[("w_out", l)]], f"l{l}") for l in range(DEPTH)]
    g_w_in = jnp.stack([reduced[l][0].T for l in range(DEPTH)])
    g_w_out = jnp.stack([reduced[l][1] for l in range(DEPTH)])

    def both(name):
        return jnp.stack([grads[0][name], grads[1][name]])

    small = dict(shift_mu=both("shift_mu"), w_decay0=both("w_decay0"), a0=both("a0"), k_k=both("k_k"), k_a=both("k_a"),
                 r_k=both("r_k"), ln_x_w=both("ln_x_w"), ln_x_b=both("ln_x_b"), v_mix0=grads[1]["v_mix0"][None],
                 lb_logits=d_lbl, g_norm_w=both("g_norm_w"), ln_w=both("ln_w"), ln_b=both("ln_b"),
                 w_decay_up=both("w_decay_up"), a_up=both("a_up"), v_mix_down=grads[1]["v_mix_down"][None],
                 v_mix_up=grads[1]["v_mix_up"][None])
    flat = jnp.concatenate([small[nm].reshape(-1) for nm in _SMALL])
    n_flat = flat.shape[0]
    rows = -(-n_flat // (8 * LANES)) * 8
    flat = jnp.concatenate([flat, jnp.zeros((rows * LANES - n_flat,), f32)]).reshape(rows, LANES)
    total = _sum_slots(_all_gather_rows(flat, "ag_small_grads").reshape(N_DEV, rows, LANES), "sum_small_grads").reshape(-1)
    gsm = {}
    off = 0
    for nm in _SMALL:
        size = small[nm].size
        gsm[nm] = total[off:off + size].reshape(small[nm].shape)
        off += size
    gsm["w_decay_up"] = lax.dynamic_slice_in_dim(gsm["w_decay_up"], me * shard, shard, axis=2)
    gsm["a_up"] = lax.dynamic_slice_in_dim(gsm["a_up"], me * shard, shard, axis=2)
    gsm["v_mix_up"] = lax.dynamic_slice_in_dim(gsm["v_mix_up"], me * shard, shard, axis=2)
    gsm["v_mix_down"] = lax.dynamic_slice_in_dim(gsm["v_mix_down"], me * shard, shard, axis=1)
    gsm["w_in"] = g_w_in
    gsm["w_out"] = g_w_out

    deltas, new_m, new_v = {}, {}, {}
    for nm in _NAMES:
        deltas[nm], new_m[nm], new_v[nm] = _adamw(weights[nm], gsm[nm], mom1[nm], mom2[nm], "adamw_" + nm)
    return (loss, grad_x, *[gsm[nm] for nm in _NAMES], *[deltas[nm] for nm in _NAMES],
            *[new_m[nm] for nm in _NAMES], *[new_v[nm] for nm in _NAMES])
```

```python
import functools

import jax
import jax.numpy as jnp
from jax import lax
from jax.experimental import pallas as pl
from jax.experimental.pallas import tpu as pltpu

f32 = jnp.float32
bf16 = jnp.bfloat16

N_DEV = 8
CHUNK = 64
LANES = 128
RWKV_HEAD = 64
DEPTH = 2
ALPHA = (2 * DEPTH) ** 0.25
LN_EPS = 1e-5
GN_EPS = 64e-5
RMS_EPS = 1e-5
LB_FLOOR = 1e-30
ADAM_LR, ADAM_B1, ADAM_B2, ADAM_EPS, ADAM_WD, ADAM_STEP = 0.001, 0.9, 0.999, 1e-08, 0.01, 10
MESH = pl.DeviceIdType.MESH


def _iota(shape, d):
    return lax.broadcasted_iota(jnp.int32, shape, d)


_DIMS = {"nn": (((1,), (0,)), ((), ())), "nt": (((1,), (1,)), ((), ())), "tn": (((0,), (0,)), ((), ()))}
_BATCH_DIMS = {"nn": (((2,), (1,)), ((0,), (0,))), "nt": (((2,), (2,)), ((0,), (0,))), "tn": (((1,), (1,)), ((0,), (0,)))}
_K_AXES = {"nn": (-1, -2), "nt": (-1, -1), "tn": (-2, -2)}


def _mxu(a, b, mode):
    return lax.dot_general(a, b, (_BATCH_DIMS if a.ndim == 3 else _DIMS)[mode], preferred_element_type=f32)


def _split(x):
    hi = x.astype(bf16)
    return hi, (x - hi.astype(f32)).astype(bf16)


def _mm3_impl(a, b, mode):
    ah, al = _split(a)
    bh, bl = _split(b)
    ka, kb = _K_AXES[mode]
    k = a.shape[ka]
    if k % (LANES if -1 in (ka, kb) else 16) == 0:
        return _mxu(jnp.concatenate([ah, ah, al], axis=ka), jnp.concatenate([bh, bl, bh], axis=kb), mode)
    return _mxu(ah, bh, mode) + (_mxu(ah, bl, mode) + _mxu(al, bh, mode))


@functools.partial(jax.custom_vjp, nondiff_argnums=(2,))
def _mm3(a, b, mode):
    return _mm3_impl(a, b, mode)


def _mm3_fwd(a, b, mode):
    return _mm3_impl(a, b, mode), (a, b)


def _mm3_bwd(mode, res, g):
    a, b = res
    if mode == "nn":
        return _mm3_impl(g, b, "nt"), _mm3_impl(a, g, "tn")
    if mode == "nt":
        return _mm3_impl(g, b, "nn"), _mm3_impl(g, a, "tn")
    return _mm3_impl(b, g, "nt"), _mm3_impl(a, g, "nn")


_mm3.defvjp(_mm3_fwd, _mm3_bwd)


def _const_impl(cm, x, mode):
    hi, lo = _split(x)
    if mode in ("r", "rt"):
        shape = x.shape
        hi, lo = hi.reshape(-1, shape[-1]), lo.reshape(-1, shape[-1])
        dims = "nn" if mode == "r" else "nt"
        out = _mxu(hi, cm, dims) + _mxu(lo, cm, dims)
        return out.reshape(shape[:-1] + (out.shape[-1],))
    if x.ndim == 3:
        cm = jnp.broadcast_to(cm, (x.shape[0],) + cm.shape)
    return _mxu(cm, hi, mode) + _mxu(cm, lo, mode)


@jax.custom_vjp
def _const_left(cm, x):
    return _const_impl(cm, x, "nn")


_const_left.defvjp(lambda cm, x: (_const_impl(cm, x, "nn"), cm),
                   lambda cm, g: (jnp.zeros_like(cm), _const_impl(cm, g, "tn")))


@jax.custom_vjp
def _const_right(x, cm):
    return _const_impl(cm, x, "r")


_const_right.defvjp(lambda x, cm: (_const_impl(cm, x, "r"), cm),
                    lambda cm, g: (_const_impl(cm, g, "rt"), jnp.zeros_like(cm)))


def _tri_inv(a):
    n = a.shape[-1]
    tm = (_iota((n, n), 0) == _iota((n, n), 1)).astype(f32) + a
    ak = a
    for _ in range(5):
        ak = _mm3_impl(ak, ak, "nn")
        tm = tm + _mm3_impl(tm, ak, "nn")
    return tm


@jax.custom_vjp
def _tri_solve(a, x):
    return _mm3_impl(_tri_inv(a), x, "nn")


def _tri_solve_fwd(a, x):
    tm = _tri_inv(a)
    u = _mm3_impl(tm, x, "nn")
    return u, (tm, u)


def _tri_solve_bwd(res, du):
    tm, u = res
    dx = _mm3_impl(tm, du, "tn")
    return _mm3_impl(dx, u, "nt"), dx


_tri_solve.defvjp(_tri_solve_fwd, _tri_solve_bwd)


def _col_of_row(row_vec):
    n = row_vec.shape[-1]
    eye = _iota((n, n), 0) == _iota((n, n), 1)
    return jnp.sum(jnp.where(eye, jnp.broadcast_to(row_vec, row_vec.shape[:-2] + (n, n)), 0.0), axis=-1, keepdims=True)


def _softplus(x):
    return jnp.maximum(x, 0.0) + jnp.log1p(jnp.exp(-jnp.abs(x)))


def _log_sigmoid(x):
    return -_softplus(-x)


def _logaddexp(a, b):
    return jnp.maximum(a, b) + jnp.log1p(jnp.exp(-jnp.abs(a - b)))


def _silu(x):
    return x * jax.nn.sigmoid(x)


def _tril(c, strict):
    r, s = _iota((c, c), 0), _iota((c, c), 1)
    return (r > s) if strict else (r >= s)


def _last_row(a):
    c = a.shape[-2]
    return jnp.sum(jnp.where(_iota(a.shape, a.ndim - 2) == c - 1, a, 0.0), axis=-2, keepdims=True)


def _rwkv_pre(layer1, prm, y, prev, vf):
    c = y.shape[0]
    if layer1:
        mu, w0, a0, wup, aup, v0, vdown, vup = prm
    else:
        mu, w0, a0, wup, aup = prm
    dr = w0.shape[1]
    shift = (_iota((c, c), 0) == _iota((c, c), 1) + 1).astype(bf16)
    y_prev = _const_left(shift, y) + jnp.where(_iota((c, 1), 0) == 0, prev, 0.0)
    rw = y + mu * (y_prev - y)
    r, k, v, z = (rw[:, i * dr:(i + 1) * dr] for i in range(4))
    wdad = rw[:, 4 * dr:4 * dr + LANES]
    w_raw = w0 + _mm3(jnp.tanh(wdad), wup, "nn")
    lw = -jnp.exp(-_softplus(-w_raw) - 0.5)
    asig = jax.nn.sigmoid(a0 + _mm3(wdad, aup, "nn"))
    if layer1:
        v = v + (vf - v) * jax.nn.sigmoid(v0 + _mm3(_mm3(v, vdown, "nn"), vup, "nn"))
    return r, k, v, z, lw, asig


def _rwkv_pair(pp, m0, xs):
    kkw, kaw, rkw, gnw, gnb = pp
    r, k, v, z, lw, asig = xs
    c = r.shape[-2]
    n2 = 2 * c
    lane = _iota((1, LANES), 1)
    mh0, mh1 = (lane < RWKV_HEAD).astype(f32), (lane >= RWKV_HEAD).astype(f32)
    same_head = _iota((LANES, LANES), 0) // RWKV_HEAD == _iota((LANES, LANES), 1) // RWKV_HEAD
    g = same_head.astype(bf16)

    def seg(x):
        return _const_right(x, g)

    def stack(x):
        return jnp.concatenate([x * mh0, x * mh1], axis=-2)

    kk = k * kkw
    kk = kk / jnp.maximum(jnp.sqrt(seg(kk * kk)), 1e-12)
    k2 = k * (1.0 + (asig - 1.0) * kaw)
    a = -kk
    b = kk * asig
    cum = _const_left(_tril(c, False).astype(bf16), lw)
    at = stack(a * jnp.exp(cum - lw))
    rt = stack(r * jnp.exp(cum))
    en = jnp.exp(-cum)
    sc = _mm3(jnp.concatenate([at, rt], axis=-2), jnp.concatenate([stack(b * en), stack(k2 * en)], axis=-2), "nt")
    row, col = _iota((n2, n2), 0), _iota((n2, n2), 1)
    same = row // c == col // c
    strict = same & (row % c > col % c)
    incl = same & (row % c >= col % c)
    aab = jnp.where(strict, sc[..., :n2, :n2], 0.0)
    aak = jnp.where(strict, sc[..., :n2, n2:], 0.0)
    arb = jnp.where(incl, sc[..., n2:, :n2], 0.0)
    ark = jnp.where(incl, sc[..., n2:, n2:], 0.0)
    vv = jnp.concatenate([v, v], axis=-2)
    mask_st = jnp.concatenate([jnp.broadcast_to(mh0, (c, LANES)), jnp.broadcast_to(mh1, (c, LANES))], axis=0)
    x_st = _mm3(jnp.concatenate([at, aak], axis=-1), jnp.concatenate([m0, vv], axis=-2), "nn")
    u_st = _tri_solve(aab, x_st) * mask_st
    o_st = _mm3(jnp.concatenate([rt, arb, ark], axis=-1), jnp.concatenate([m0, u_st, vv], axis=-2), "nn") * mask_st
    u = u_st[..., :c, :] + u_st[..., c:, :]
    o = o_st[..., :c, :] + o_st[..., c:, :]
    cum_last = _last_row(cum)
    dec_end = jnp.exp(cum_last - cum)
    m_new = _col_of_row(jnp.exp(cum_last)) * m0 + _mm3(
        jnp.concatenate([b * dec_end, k2 * dec_end], axis=-2), jnp.concatenate([u, v], axis=-2), "tn") * same_head.astype(f32)
    mean = seg(o) * (1.0 / RWKV_HEAD)
    d = o - mean
    var = seg(d * d) * (1.0 / RWKV_HEAD)
    on = d * lax.rsqrt(var + GN_EPS) * gnw + gnb
    bonus = seg(r * k2 * rkw) * v
    return (on + bonus) * _silu(z), m_new


def _split_lanes(a, n):
    return [a[:, i * LANES:(i + 1) * LANES] for i in range(n)]


def _group(n):
    return 8 if n % 8 == 0 else (4 if n % 4 == 0 else (2 if n % 2 == 0 else 1))


def _rwkv_specs(layer1, t, dr, rwc, n_pair, rev):
    nc = t // CHUNK
    grp = _group(n_pair)

    def cidx(c):
        return (nc - 1 - c) if rev else c

    full = lambda shape: pl.BlockSpec(shape, lambda c, p: tuple(0 for _ in shape))
    specs = [
        pl.BlockSpec((CHUNK, rwc), lambda c, p: (cidx(c), 0)),
        pl.BlockSpec((8, rwc), lambda c, p: (jnp.maximum(cidx(c) * (CHUNK // 8) - 1, 0), 0)),
    ]
    if layer1:
        specs.append(pl.BlockSpec((CHUNK, dr), lambda c, p: (cidx(c), 0)))
    prm_shapes = [(1, rwc), (1, dr), (1, dr), (LANES, dr), (LANES, dr)]
    if layer1:
        prm_shapes += [(1, dr), (dr, LANES), (LANES, dr)]
    specs += [full(s) for s in prm_shapes]
    specs.append(pl.BlockSpec((grp, 8, LANES), lambda c, p: (p, 0, 0)))
    return specs, prm_shapes, cidx, full


def _rwkv_fwd(layer1, proj, vf, prm, pp, cat_width):
    t = proj.shape[0]
    dr = prm[1].shape[1]
    rwc = prm[0].shape[1]
    n_pair = dr // LANES
    nc = t // CHUNK
    n_prm = len(prm)
    specs, _, _, _ = _rwkv_specs(layer1, t, dr, rwc, n_pair, False)

    def body(*refs):
        y_ref, prev_ref = refs[0], refs[1]
        i = 2
        vf_ref = None
        if layer1:
            vf_ref = refs[i]
            i += 1
        prm_refs = refs[i:i + n_prm]
        i += n_prm
        pp_ref = refs[i]
        i += 1
        cat_ref = refs[i]
        i += 1
        vout_ref = None
        if not layer1:
            vout_ref = refs[i]
            i += 1
        mck_ref, x_s, m_s = refs[i], refs[i + 1], refs[i + 2]
        c, p = pl.program_id(0), pl.program_id(1)

        @pl.when((c == 0) & (p == 0))
        def _():
            m_s[...] = jnp.zeros_like(m_s)

        @pl.when(p == 0)
        def _():
            prev = prev_ref[pl.ds(7, 1), :] * (c != 0).astype(f32)
            xs = _rwkv_pre(layer1, tuple(r[...] for r in prm_refs), y_ref[...], prev,
                           vf_ref[...] if layer1 else None)
            for q, a in enumerate(xs):
                for j, piece in enumerate(_split_lanes(a, n_pair)):
                    x_s[q * n_pair + j] = piece
            if not layer1:
                vout_ref[...] = xs[2]

        m0 = m_s[pl.ds(p * grp, grp)]
        mck_ref[0] = m0
        ppv = tuple(pp_ref[:, pl.ds(q, 1), :] for q in range(5))
        og, m_new = _rwkv_pair(ppv, m0, tuple(x_s[pl.ds(q * n_pair + p * grp, grp)] for q in range(6)))
        for j in range(grp):
            cat_ref[:, j * LANES:(j + 1) * LANES] = og[j]
        m_s[pl.ds(p * grp, grp)] = m_new

    grp = _group(n_pair)
    out_shape = [jax.ShapeDtypeStruct((t, cat_width), f32)]
    out_specs = [pl.BlockSpec((CHUNK, grp * LANES), lambda c, p: (c, p))]
    if not layer1:
        out_shape.append(jax.ShapeDtypeStruct((t, dr), f32))
        out_specs.append(pl.BlockSpec((CHUNK, dr), lambda c, p: (c, 0)))
    out_shape.append(jax.ShapeDtypeStruct((nc, n_pair, LANES, LANES), f32))
    out_specs.append(pl.BlockSpec((1, grp, LANES, LANES), lambda c, p: (c, p, 0, 0)))
    args = [proj, proj] + ([vf] if layer1 else []) + list(prm) + [pp]
    return pl.pallas_call(
        body, grid=(nc, n_pair // grp), in_specs=specs, out_specs=out_specs, out_shape=out_shape,
        scratch_shapes=[pltpu.VMEM((6 * n_pair, CHUNK, LANES), f32), pltpu.VMEM((n_pair, LANES, LANES), f32)],
        compiler_params=pltpu.CompilerParams(dimension_semantics=("arbitrary", "arbitrary")),
        name=f"rwkv_fwd_l{int(layer1)}",
    )(*args)


def _rwkv_bwd(layer1, proj, vf, prm, pp, mck, dcat, dvout):
    t = proj.shape[0]
    dr = prm[1].shape[1]
    rwc = prm[0].shape[1]
    n_pair = dr // LANES
    nc = t // CHUNK
    n_prm = len(prm)
    specs, prm_shapes, cidx, full = _rwkv_specs(layer1, t, dr, rwc, n_pair, True)
    grp = _group(n_pair)
    n_step = n_pair // grp
    specs.append(pl.BlockSpec((1, grp, LANES, LANES), lambda c, p: (cidx(c), p, 0, 0)))
    specs.append(pl.BlockSpec((CHUNK, grp * LANES), lambda c, p: (cidx(c), p)))
    if not layer1:
        specs.append(pl.BlockSpec((CHUNK, dr), lambda c, p: (cidx(c), 0)))

    def body(*refs):
        y_ref, prev_ref = refs[0], refs[1]
        i = 2
        vf_ref = None
        if layer1:
            vf_ref = refs[i]
            i += 1
        prm_refs = refs[i:i + n_prm]
        i += n_prm
        pp_ref, mck_ref, dog_ref = refs[i], refs[i + 1], refs[i + 2]
        i += 3
        dvout_ref = None
        if not layer1:
            dvout_ref = refs[i]
            i += 1
        dy_ref = refs[i]
        i += 1
        dvf_ref = None
        if layer1:
            dvf_ref = refs[i]
            i += 1
        dprm_refs = refs[i:i + n_prm]
        i += n_prm
        dpp_ref = refs[i]
        x_s, dx_s, dm_s, dprev_s = refs[i + 1:i + 5]
        c, p = pl.program_id(0), pl.program_id(1)
        cr = nc - 1 - c

        def prev_row():
            return prev_ref[pl.ds(7, 1), :] * (cr != 0).astype(f32)

        @pl.when((c == 0) & (p == 0))
        def _():
            dm_s[...] = jnp.zeros_like(dm_s)
            dprev_s[...] = jnp.zeros_like(dprev_s)
            dpp_ref[...] = jnp.zeros_like(dpp_ref)
            for r in dprm_refs:
                r[...] = jnp.zeros_like(r)

        @pl.when(p == 0)
        def _():
            xs = _rwkv_pre(layer1, tuple(r[...] for r in prm_refs), y_ref[...], prev_row(),
                           vf_ref[...] if layer1 else None)
            for q, a in enumerate(xs):
                for j, piece in enumerate(_split_lanes(a, n_pair)):
                    x_s[q * n_pair + j] = piece

        units = pl.ds(p * grp, grp)
        ppv = tuple(pp_ref[:, pl.ds(q, 1), :] for q in range(5))
        xs_p = tuple(x_s[pl.ds(q * n_pair + p * grp, grp)] for q in range(6))
        dog = jnp.stack([dog_ref[:, j * LANES:(j + 1) * LANES] for j in range(grp)], axis=0)
        _, vjp_pair = jax.vjp(_rwkv_pair, ppv, mck_ref[0], xs_p)
        dppv, dm0, dxs = vjp_pair((dog, dm_s[units]))
        dm_s[units] = dm0
        for q in range(6):
            dx_s[pl.ds(q * n_pair + p * grp, grp)] = dxs[q]
        for q in range(5):
            dpp_ref[units, pl.ds(q, 1), :] += dppv[q]

        @pl.when(p == n_step - 1)
        def _():
            dxs_full = [jnp.concatenate([dx_s[q * n_pair + j] for j in range(n_pair)], axis=1) for q in range(6)]
            if not layer1:
                dxs_full[2] = dxs_full[2] + dvout_ref[...]
            prm_v = tuple(r[...] for r in prm_refs)
            if layer1:
                _, vjp_pre = jax.vjp(functools.partial(_rwkv_pre, True), prm_v, y_ref[...], prev_row(), vf_ref[...])
                dprm, dy, dprev, dvf = vjp_pre(tuple(dxs_full))
                dvf_ref[...] = dvf
            else:
                _, vjp_pre = jax.vjp(lambda a, b, d: _rwkv_pre(False, a, b, d, None), prm_v, y_ref[...], prev_row())
                dprm, dy, dprev = vjp_pre(tuple(dxs_full))
            dy_ref[...] = (dy + jnp.where(_iota((CHUNK, 1), 0) == CHUNK - 1, dprev_s[...], 0.0)).astype(bf16)
            dprev_s[...] = dprev
            for r, gval in zip(dprm_refs, dprm):
                r[...] += gval

    out_shape = [jax.ShapeDtypeStruct((t, rwc), bf16)]
    out_specs = [pl.BlockSpec((CHUNK, rwc), lambda c, p: (cidx(c), 0))]
    if layer1:
        out_shape.append(jax.ShapeDtypeStruct((t, dr), f32))
        out_specs.append(pl.BlockSpec((CHUNK, dr), lambda c, p: (cidx(c), 0)))
    out_shape += [jax.ShapeDtypeStruct(s, f32) for s in prm_shapes]
    out_specs += [full(s) for s in prm_shapes]
    out_shape.append(jax.ShapeDtypeStruct((n_pair, 8, LANES), f32))
    out_specs.append(full((n_pair, 8, LANES)))
    args = [proj, proj] + ([vf] if layer1 else []) + list(prm) + [pp, mck, dcat] + ([] if layer1 else [dvout])
    return pl.pallas_call(
        body, grid=(nc, n_step), in_specs=specs, out_specs=out_specs, out_shape=out_shape,
        scratch_shapes=[pltpu.VMEM((6 * n_pair, CHUNK, LANES), f32), pltpu.VMEM((6 * n_pair, CHUNK, LANES), f32),
                        pltpu.VMEM((n_pair, LANES, LANES), f32), pltpu.VMEM((1, rwc), f32)],
        compiler_params=pltpu.CompilerParams(dimension_semantics=("arbitrary", "arbitrary")),
        name=f"rwkv_bwd_l{int(layer1)}",
    )(*args)


def _hgrn_chunk(layer1, lbl, gnw, s0, q_raw, f_raw, i_in, z):
    c = q_raw.shape[-2]
    q = _silu(q_raw)
    ls = _log_sigmoid(f_raw)
    if layer1:
        l0, l1 = lbl[..., 0:1, :], lbl[..., 1:2, :]
        mx = jnp.maximum(l0, l1)
        e0, e1 = jnp.exp(l0 - mx), jnp.exp(l1 - mx)
        sm0, sm1 = e0 / (e0 + e1), e1 / (e0 + e1)
        lb = (sm0 + sm1) - sm0
        log_f = _logaddexp(jnp.log(jnp.maximum(lb, LB_FLOOR)), jnp.log1p(-lb) + ls)
        k = (1.0 - lb) * jax.nn.sigmoid(-f_raw)
    else:
        log_f = _logaddexp(jnp.full_like(ls, jnp.log(jnp.float32(LB_FLOOR))), ls)
        k = jax.nn.sigmoid(-f_raw)
    row, col = _iota((c, c), 0), _iota((c, c), 1)
    trow = _iota((c, 1), 0)
    halves = []
    half = c // 2
    while half >= 1:
        halves.append(half)
        half //= 2
    cmat = jnp.concatenate([(col <= row).astype(f32)]
                           + [(col <= (row // (2 * hf)) * (2 * hf) + hf - 1).astype(f32) for hf in halves], axis=0)
    ball = _const_left(cmat.astype(bf16), log_f)
    b = ball[..., :c, :]
    att = None
    for lvl, hf in enumerate(halves):
        blk = 2 * hf
        bref = ball[..., (lvl + 1) * c:(lvl + 2) * c, :]
        upper = (trow % blk) >= hf
        qh = q * jnp.exp(jnp.where(upper, b - bref, 0.0)) * upper.astype(f32)
        kh = k * jnp.exp(jnp.where(upper, 0.0, bref - b)) * (1.0 - upper.astype(f32))
        term = jnp.where(row // blk == col // blk, _mm3(qh, kh, "nt"), 0.0)
        att = term if att is None else att + term
    lhs = jnp.concatenate([q * jnp.exp(b), att, jnp.zeros(att.shape[:-1] + (LANES - c,), f32)], axis=-1)
    rhs = jnp.concatenate([s0, i_in, jnp.zeros(i_in.shape[:-2] + (LANES - c, i_in.shape[-1]), f32)], axis=-2)
    o = _mm3(lhs, rhs, "nn") + jnp.sum(q * k, axis=-1, keepdims=True) * i_in
    b_last = _last_row(b)
    s_new = _col_of_row(jnp.exp(b_last)) * s0 + _mm3(k * jnp.exp(b_last - b), i_in, "tn")
    o = o * lax.rsqrt(jnp.mean(o * o, axis=-1, keepdims=True) + RMS_EPS)
    return o * gnw * _silu(z), s_new


def _hgrn_in_specs(t, dh, col0, rev):
    nc = t // CHUNK
    nh = dh // LANES

    def cidx(c):
        return (nc - 1 - c) if rev else c

    grp = _group(nh)
    specs = [pl.BlockSpec((CHUNK, LANES), functools.partial(lambda g, j, h, c: (cidx(c), col0 + g * nh + h * grp + j), g, j))
             for j in range(grp) for g in range(4)]
    specs.append(pl.BlockSpec((2, grp * LANES), lambda h, c: (0, h)))
    specs.append(pl.BlockSpec((1, grp * LANES), lambda h, c: (0, h)))
    return specs, cidx, grp


def _hgrn_fwd(layer1, proj, lbl, gnw, cat, rwc):
    t, d = cat.shape
    dh = gnw.shape[1]
    nh = dh // LANES
    nc = t // CHUNK
    col0 = rwc // LANES
    specs, _, grp = _hgrn_in_specs(t, dh, col0, False)
    specs.append(pl.BlockSpec(memory_space=pl.ANY))
    assert (d - dh) % (grp * LANES) == 0
    cat_col0 = (d - dh) // (grp * LANES)

    def body(*refs):
        x_refs = refs[:4 * grp]
        lbl_ref, gnw_ref, _, cat_ref, sck_ref, s_s = refs[4 * grp:]
        c = pl.program_id(1)

        @pl.when(c == 0)
        def _():
            s_s[...] = jnp.zeros_like(s_s)

        lanes = [slice(j * LANES, (j + 1) * LANES) for j in range(grp)]
        s0 = s_s[...]
        sck_ref[:, 0] = s0
        out, s_new = _hgrn_chunk(layer1, jnp.stack([lbl_ref[:, ln] for ln in lanes]), jnp.stack([gnw_ref[:, ln] for ln in lanes]),
                                 s0, *(jnp.stack([x_refs[4 * j + g][...] for j in range(grp)]) for g in range(4)))
        for j in range(grp):
            cat_ref[:, lanes[j]] = out[j]
        s_s[...] = s_new

    return pl.pallas_call(
        body, grid=(nh // grp, nc), in_specs=specs,
        out_specs=[pl.BlockSpec((CHUNK, grp * LANES), lambda h, c: (c, cat_col0 + h)),
                   pl.BlockSpec((grp, 1, LANES, LANES), lambda h, c: (h, c, 0, 0))],
        out_shape=[jax.ShapeDtypeStruct((t, d), f32), jax.ShapeDtypeStruct((nh, nc, LANES, LANES), f32)],
        scratch_shapes=[pltpu.VMEM((grp, LANES, LANES), f32)],
        input_output_aliases={4 * grp + 2: 0},
        compiler_params=pltpu.CompilerParams(dimension_semantics=("arbitrary", "arbitrary")),
        name=f"hgrn_fwd_l{int(layer1)}",
    )(*([proj] * (4 * grp)), lbl, gnw, cat)


def _hgrn_bwd(layer1, proj, lbl, gnw, sck, dcat, rwc):
    t, d = dcat.shape
    dh = gnw.shape[1]
    nh = dh // LANES
    nc = t // CHUNK
    col0 = rwc // LANES
    specs, cidx, grp = _hgrn_in_specs(t, dh, col0, True)
    assert (d - dh) % (grp * LANES) == 0
    cat_col0 = (d - dh) // (grp * LANES)
    specs.append(pl.BlockSpec((grp, 1, LANES, LANES), lambda h, c: (h, cidx(c), 0, 0)))
    specs.append(pl.BlockSpec((CHUNK, grp * LANES), lambda h, c: (cidx(c), cat_col0 + h)))

    def body(*refs):
        x_refs = refs[:4 * grp]
        lbl_ref, gnw_ref, sck_ref, do_ref, dp_ref, dlbl_ref, dgnw_ref, ds_s = refs[4 * grp:]
        c = pl.program_id(1)

        @pl.when(c == 0)
        def _():
            ds_s[...] = jnp.zeros_like(ds_s)
            dlbl_ref[...] = jnp.zeros_like(dlbl_ref)
            dgnw_ref[...] = jnp.zeros_like(dgnw_ref)

        lanes = [slice(j * LANES, (j + 1) * LANES) for j in range(grp)]
        _, vjp = jax.vjp(functools.partial(_hgrn_chunk, layer1),
                         jnp.stack([lbl_ref[:, ln] for ln in lanes]), jnp.stack([gnw_ref[:, ln] for ln in lanes]), sck_ref[:, 0],
                         *(jnp.stack([x_refs[4 * j + g][...] for j in range(grp)]) for g in range(4)))
        dlbl, dgnw, ds0, dq, df, di, dz = vjp((jnp.stack([do_ref[:, ln] for ln in lanes]), ds_s[...]))
        ds_s[...] = ds0
        for j in range(grp):
            dlbl_ref[:, lanes[j]] += dlbl[j]
            dgnw_ref[:, lanes[j]] += dgnw[j]
            for g, val in enumerate((dq, df, di, dz)):
                dp_ref[g, :, lanes[j]] = val[j].astype(bf16)

    return pl.pallas_call(
        body, grid=(nh // grp, nc), in_specs=specs,
        out_specs=[pl.BlockSpec((4, CHUNK, grp * LANES), lambda h, c: (0, cidx(c), h)),
                   pl.BlockSpec((2, grp * LANES), lambda h, c: (0, h)),
                   pl.BlockSpec((1, grp * LANES), lambda h, c: (0, h))],
        out_shape=[jax.ShapeDtypeStruct((4, t, dh), bf16), jax.ShapeDtypeStruct((2, dh), f32),
                   jax.ShapeDtypeStruct((1, dh), f32)],
        scratch_shapes=[pltpu.VMEM((grp, LANES, LANES), f32)],
        compiler_params=pltpu.CompilerParams(dimension_semantics=("arbitrary", "arbitrary")),
        name=f"hgrn_bwd_l{int(layer1)}",
    )(*([proj] * (4 * grp)), lbl, gnw, sck, dcat)


def _ln(h, y, w, b):
    u = ALPHA * h + y
    mu = jnp.mean(u, axis=-1, keepdims=True)
    var = jnp.mean(jnp.square(u - mu), axis=-1, keepdims=True)
    return (u - mu) * lax.rsqrt(var + LN_EPS) * w + b


def _row_tile(t):
    return 256 if t % 256 == 0 else t


def _ln_fwd(h, y, w, b):
    t, d = h.shape
    tr = _row_tile(t)

    def body(h_ref, y_ref, w_ref, b_ref, o_ref, o16_ref):
        out = _ln(h_ref[...], y_ref[...], w_ref[...], b_ref[...])
        o_ref[...] = out
        o16_ref[...] = out.astype(bf16)

    row = pl.BlockSpec((tr, d), lambda i: (i, 0))
    vec = pl.BlockSpec((1, d), lambda i: (0, 0))
    return pl.pallas_call(body, grid=(t // tr,), in_specs=[row, row, vec, vec], out_specs=[row, row],
                          out_shape=[jax.ShapeDtypeStruct((t, d), f32), jax.ShapeDtypeStruct((t, d), bf16)],
                          name="ln_fwd")(h, y, w, b)


def _ln_loss(h, y, w, b, tgt):
    t, d = h.shape
    tr = _row_tile(t)

    def body(h_ref, y_ref, w_ref, b_ref, t_ref, g_ref, loss_ref):
        @pl.when(pl.program_id(0) == 0)
        def _():
            loss_ref[...] = jnp.zeros_like(loss_ref)

        err = _ln(h_ref[...], y_ref[...], w_ref[...], b_ref[...]) - t_ref[...]
        g_ref[...] = err * (1.0 / d)
        loss_ref[...] += 0.5 * jnp.sum(jnp.mean(jnp.square(err), axis=-1, keepdims=True), axis=0, keepdims=True)

    row = pl.BlockSpec((tr, d), lambda i: (i, 0))
    vec = pl.BlockSpec((1, d), lambda i: (0, 0))
    return pl.pallas_call(
        body, grid=(t // tr,), in_specs=[row, row, vec, vec, row],
        out_specs=[row, pl.BlockSpec((1, LANES), lambda i: (0, 0))],
        out_shape=[jax.ShapeDtypeStruct((t, d), f32), jax.ShapeDtypeStruct((1, LANES), f32)],
        compiler_params=pltpu.CompilerParams(dimension_semantics=("arbitrary",)), name="ln_loss")(h, y, w, b, tgt)


def _ln_bwd(h, y, w, b, dout):
    t, d = h.shape
    tr = _row_tile(t)

    def body(h_ref, y_ref, w_ref, b_ref, do_ref, dy_ref, dy16_ref, dw_ref, db_ref):
        @pl.when(pl.program_id(0) == 0)
        def _():
            dw_ref[...] = jnp.zeros_like(dw_ref)
            db_ref[...] = jnp.zeros_like(db_ref)

        _, vjp = jax.vjp(lambda yy, ww, bb: _ln(h_ref[...], yy, ww, bb), y_ref[...], w_ref[...], b_ref[...])
        dy, dw, db = vjp(do_ref[...])
        dy_ref[...] = dy
        dy16_ref[...] = dy.astype(bf16)
        dw_ref[...] += dw
        db_ref[...] += db

    row = pl.BlockSpec((tr, d), lambda i: (i, 0))
    vec = pl.BlockSpec((1, d), lambda i: (0, 0))
    return pl.pallas_call(
        body, grid=(t // tr,), in_specs=[row, row, vec, vec, row], out_specs=[row, row, vec, vec],
        out_shape=[jax.ShapeDtypeStruct((t, d), f32), jax.ShapeDtypeStruct((t, d), bf16),
                   jax.ShapeDtypeStruct((1, d), f32), jax.ShapeDtypeStruct((1, d), f32)],
        compiler_params=pltpu.CompilerParams(dimension_semantics=("arbitrary",)), name="ln_bwd")(h, y, w, b, dout)


def _pick(n, prefs):
    for p in prefs:
        if n % p == 0:
            return p
    return n


def _tile(n, want):
    if n <= want:
        return n
    for cand in range(want - want % LANES, 0, -LANES):
        if n % cand == 0:
            return cand
    return n


def _matmul(a, b, mode, name, tiles, add=None, add_scale=1.0, out_dtype=f32):
    if mode == "nn":
        (m, k), n = a.shape, b.shape[1]
    elif mode == "nt":
        (m, k), n = a.shape, b.shape[0]
    else:
        (k, m), n = a.shape, b.shape[1]
    tm, tn, tk = _tile(m, tiles[0]), _tile(n, tiles[1]), _tile(k, tiles[2])
    nk = k // tk
    cache_a = nk == 1 and a.dtype != bf16 and n // tn > 1

    def body(*refs):
        a_ref, b_ref = refs[0], refs[1]
        add_ref = refs[2] if add is not None else None
        n_in = 3 if add is not None else 2
        o_ref = refs[n_in]
        scratch = refs[n_in + 1:]

        def finish(res):
            if add is not None:
                res = res + add_scale * add_ref[...]
            o_ref[...] = res.astype(out_dtype)

        if cache_a:
            a_bf = scratch[0]

            @pl.when(pl.program_id(1) == 0)
            def _():
                a_bf[...] = a_ref[...].astype(bf16)

            a_val = a_bf[...]
        else:
            a_val = a_ref[...].astype(bf16)
        prod = lax.dot_general(a_val, b_ref[...].astype(bf16), _DIMS[mode], preferred_element_type=f32)
        if nk == 1:
            finish(prod)
        else:
            acc = scratch[-1]
            kk = pl.program_id(2)

            @pl.when(kk == 0)
            def _():
                acc[...] = prod

            @pl.when(kk != 0)
            def _():
                acc[...] += prod

            @pl.when(kk == nk - 1)
            def _():
                finish(acc[...])

    a_shape = (tk, tm) if mode == "tn" else (tm, tk)
    a_spec = pl.BlockSpec(a_shape, (lambda i, j, kk: (kk, i)) if mode == "tn" else (lambda i, j, kk: (i, kk)))
    b_spec = pl.BlockSpec((tn, tk), lambda i, j, kk: (j, kk)) if mode == "nt" else pl.BlockSpec((tk, tn), lambda i, j, kk: (kk, j))
    o_spec = pl.BlockSpec((tm, tn), lambda i, j, kk: (i, j))
    in_specs = [a_spec, b_spec] + ([o_spec] if add is not None else [])
    args = [a, b] + ([add] if add is not None else [])
    scratch_shapes = ([pltpu.VMEM(a_shape, bf16)] if cache_a else []) + ([pltpu.VMEM((tm, tn), f32)] if nk > 1 else [])
    return pl.pallas_call(
        body, grid=(m // tm, n // tn, nk), in_specs=in_specs, out_specs=o_spec,
        out_shape=jax.ShapeDtypeStruct((m, n), out_dtype), scratch_shapes=scratch_shapes,
        compiler_params=pltpu.CompilerParams(dimension_semantics=("parallel", "arbitrary", "arbitrary")),
        name=name,
    )(*args)


def _position():
    return lax.axis_index("x"), lax.axis_index("y"), lax.axis_index("c")


def _flip(pos, k):
    x, y, c = pos
    return (1 - x if k & 4 else x, 1 - y if k & 2 else y, 1 - c if k & 1 else c)


def _index(pos):
    return 4 * pos[0] + 2 * pos[1] + pos[2]


def _all_gather_rows(x, name):
    m_per, n = x.shape

    def body(x_ref, out_ref, send_sems, recv_sems, local_sem):
        me = _position()
        sibling = _flip(me, 1)
        chips = (2, 4, 6)

        def rows(pos):
            return out_ref.at[pl.ds(_index(pos) * m_per, m_per), :]

        def copy(sem, block, to, src=None):
            return pltpu.make_async_remote_copy(
                src_ref=rows(block) if src is None else src, dst_ref=rows(block),
                send_sem=send_sems.at[sem], recv_sem=recv_sems.at[sem], device_id=to, device_id_type=MESH)

        mine = pltpu.make_async_copy(x_ref, rows(me), local_sem)
        mine.start()
        first = [copy(0, me, sibling, src=x_ref)]
        first += [copy(1 + j, me, _flip(me, k), src=x_ref) for j, k in enumerate(chips)]
        for cp in first:
            cp.start()
        passed = [copy(4 + j, _flip(me, k), sibling) for j, k in enumerate(chips)]
        for j, k in enumerate(chips):
            copy(1 + j, _flip(me, k), me).wait_recv()
            passed[j].start()
        copy(0, sibling, me).wait_recv()
        for j, k in enumerate(chips):
            copy(4 + j, _flip(sibling, k), me).wait_recv()
        for cp in first + passed:
            cp.wait_send()
        mine.wait()

    return pl.pallas_call(
        body, out_shape=jax.ShapeDtypeStruct((N_DEV * m_per, n), x.dtype),
        in_specs=[pl.BlockSpec(memory_space=pl.ANY)], out_specs=pl.BlockSpec(memory_space=pl.ANY),
        scratch_shapes=[pltpu.SemaphoreType.DMA((7,)), pltpu.SemaphoreType.DMA((7,)), pltpu.SemaphoreType.DMA(())],
        name=name,
    )(x)


def _split_start(srcs, lands, plan, n_copies, name):
    n_arr = len(srcs)
    hbm = pl.BlockSpec(memory_space=pltpu.HBM)
    sem = pl.BlockSpec(memory_space=pltpu.SEMAPHORE)

    def body(*refs):
        src_refs, land_refs = refs[:n_arr], refs[n_arr:2 * n_arr]
        send_sems, recv_sems = refs[2 * n_arr:3 * n_arr], refs[3 * n_arr:4 * n_arr]
        token = refs[-1]
        me = _position()
        for i in range(n_arr):
            for j, (src, dst, peer, _) in enumerate(plan(i, src_refs[i], land_refs[i], me)):
                pltpu.make_async_remote_copy(src_ref=src, dst_ref=dst, send_sem=send_sems[i].at[j], recv_sem=recv_sems[i].at[j],
                                             device_id=peer, device_id_type=MESH).start()
        token[...] = jnp.zeros_like(token)

    outs = pl.pallas_call(
        body, name=name,
        out_shape=([pltpu.SemaphoreType.DMA((n_copies,))] * (2 * n_arr)
                   + [pltpu.HBM(a.shape, a.dtype) for a in list(srcs) + list(lands)]
                   + [jax.ShapeDtypeStruct((8, LANES), f32)]),
        in_specs=[hbm] * (2 * n_arr),
        out_specs=[sem] * (2 * n_arr) + [hbm] * (2 * n_arr) + [pl.BlockSpec(memory_space=pltpu.VMEM)],
        input_output_aliases={i: 2 * n_arr + i for i in range(2 * n_arr)},
        compiler_params=pltpu.CompilerParams(has_side_effects=pltpu.SideEffectType.DATAFLOW_SIDE_EFFECTING),
    )(*[pltpu.with_memory_space_constraint(a, pltpu.HBM) for a in list(srcs) + list(lands)])
    return (outs[:n_arr], outs[n_arr:2 * n_arr], outs[2 * n_arr:3 * n_arr], outs[3 * n_arr:4 * n_arr], outs[-1])


def _split_wait(started, plan, after, name):
    send_sems, recv_sems, srcs, lands, _ = started
    n_arr = len(srcs)
    hbm = pl.BlockSpec(memory_space=pltpu.HBM)
    sem = pl.BlockSpec(memory_space=pltpu.SEMAPHORE)

    def body(*refs):
        src_refs, land_refs = refs[:n_arr], refs[n_arr:2 * n_arr]
        s_sems, r_sems = refs[2 * n_arr:3 * n_arr], refs[3 * n_arr:4 * n_arr]
        me = _position()
        for i in range(n_arr):
            for j, (src, _, peer, arrival) in enumerate(plan(i, src_refs[i], land_refs[i], me)):
                cp = pltpu.make_async_remote_copy(src_ref=src, dst_ref=arrival, send_sem=s_sems[i].at[j], recv_sem=r_sems[i].at[j],
                                                  device_id=peer, device_id_type=MESH)
                cp.wait_send()
                cp.wait_recv()

    outs = pl.pallas_call(
        body, name=name,
        out_shape=[pltpu.HBM(a.shape, a.dtype) for a in list(srcs) + list(lands)],
        in_specs=[hbm] * (2 * n_arr) + [sem] * (2 * n_arr) + [pl.BlockSpec(memory_space=pl.ANY)],
        out_specs=[hbm] * (2 * n_arr),
        input_output_aliases={i: i for i in range(2 * n_arr)},
        compiler_params=pltpu.CompilerParams(has_side_effects=pltpu.SideEffectType.DATAFLOW_SIDE_EFFECTING),
    )(*srcs, *lands, *send_sems, *recv_sems, after)
    return outs[:n_arr], outs[n_arr:]


_GATHER_FLIPS = (1, 2, 4, 6)


def _gather_plan(i, src_ref, land_ref, me):
    m = src_ref.shape[0]

    def rows(pos):
        return land_ref.at[pl.ds(_index(pos) * m, m), :]

    return [(src_ref, rows(me), _flip(me, k), rows(_flip(me, k))) for k in _GATHER_FLIPS]


def _gather_forward(lands, name):
    n_arr = len(lands)
    chips = (2, 4, 6)

    def body(*refs):
        out_refs = refs[n_arr:2 * n_arr]
        send_sems, recv_sems = refs[2 * n_arr:]
        me = _position()
        sibling = _flip(me, 1)
        sends, arrivals = [], []
        for i, out_ref in enumerate(out_refs):
            m = out_ref.shape[0] // N_DEV

            def copy(pos, j):
                blk = out_ref.at[pl.ds(_index(pos) * m, m), :]
                return pltpu.make_async_remote_copy(src_ref=blk, dst_ref=blk, send_sem=send_sems.at[3 * i + j],
                                                    recv_sem=recv_sems.at[3 * i + j], device_id=sibling, device_id_type=MESH)

            for j, k in enumerate(chips):
                sends.append(copy(_flip(me, k), j))
                arrivals.append(copy(_flip(sibling, k), j))
        for cp in sends:
            cp.start()
        for cp in arrivals:
            cp.wait_recv()
        for cp in sends:
            cp.wait_send()

    anyspec = pl.BlockSpec(memory_space=pl.ANY)
    return pl.pallas_call(
        body, out_shape=[jax.ShapeDtypeStruct(a.shape, a.dtype) for a in lands],
        in_specs=[anyspec] * n_arr, out_specs=[anyspec] * n_arr, input_output_aliases={i: i for i in range(n_arr)},
        scratch_shapes=[pltpu.SemaphoreType.DMA((3 * n_arr,))] * 2, name=name,
    )(*lands)


def _chips_plan(i, src_ref, land_ref, me):
    m = src_ref.shape[0] // 4
    plan = []
    for j, k in enumerate((2, 4, 6)):
        peer = _flip(me, k)
        plan.append((src_ref.at[pl.ds((2 * peer[0] + peer[1]) * m, m), :], land_ref.at[j], peer, land_ref.at[j]))
    return plan


def _exchange_siblings(gs, name):
    n_arr = len(gs)

    def body(*refs):
        g_refs, out_refs = refs[:n_arr], refs[n_arr:2 * n_arr]
        send_sems, recv_sems = refs[2 * n_arr:]
        me = _position()
        c = me[2]
        sibling = _flip(me, 1)
        copies = []
        for i, (g_ref, out_ref) in enumerate(zip(g_refs, out_refs)):
            m_per = g_ref.shape[0] // N_DEV
            for q in range(4):
                copies.append(pltpu.make_async_remote_copy(
                    src_ref=g_ref.at[pl.ds((2 * q + 1 - c) * m_per, m_per), :], dst_ref=out_ref.at[q],
                    send_sem=send_sems.at[4 * i + q], recv_sem=recv_sems.at[4 * i + q],
                    device_id=sibling, device_id_type=MESH))
        for cp in copies:
            cp.start()
        for cp in copies:
            cp.wait_recv()
        for cp in copies:
            cp.wait_send()

    anyspec = pl.BlockSpec(memory_space=pl.ANY)
    return pl.pallas_call(
        body, out_shape=[jax.ShapeDtypeStruct((4, g.shape[0] // N_DEV, g.shape[1]), g.dtype) for g in gs],
        in_specs=[anyspec] * n_arr, out_specs=[anyspec] * n_arr,
        scratch_shapes=[pltpu.SemaphoreType.DMA((4 * n_arr,))] * 2, name=name,
    )(*gs)


def _sum_with_sibling(g, recv, name):
    m = g.shape[0] // N_DEV
    n = g.shape[1]
    tr = _pick(m, (208, 128, 64, 32, 16))
    nt = m // tr

    def body(g_ref, r_ref, o_ref):
        c = lax.axis_index("c")
        own = jnp.where(c == 0, g_ref[0, 0].astype(f32), g_ref[0, 1].astype(f32))
        o_ref[...] = (own + r_ref[0].astype(f32)).astype(o_ref.dtype)

    return pl.pallas_call(
        body, grid=(4, nt),
        in_specs=[pl.BlockSpec((1, 2, tr, n), lambda q, i: (q, 0, i, 0)), pl.BlockSpec((1, tr, n), lambda q, i: (q, i, 0))],
        out_specs=pl.BlockSpec((tr, n), lambda q, i: (q * nt + i, 0)),
        out_shape=jax.ShapeDtypeStruct((4 * m, n), bf16), name=name,
    )(g.reshape(4, 2, m, n), recv)


def _exchange_chips(hs, name):
    n_arr = len(hs)
    chips = (2, 4, 6)

    def body(*refs):
        h_refs, out_refs = refs[:n_arr], refs[n_arr:2 * n_arr]
        send_sems, recv_sems = refs[2 * n_arr:]
        me = _position()
        copies = []
        for i, (h_ref, out_ref) in enumerate(zip(h_refs, out_refs)):
            m_per = h_ref.shape[0] // 4
            for j, k in enumerate(chips):
                peer = _flip(me, k)
                peer_q = 2 * peer[0] + peer[1]
                copies.append(pltpu.make_async_remote_copy(
                    src_ref=h_ref.at[pl.ds(peer_q * m_per, m_per), :], dst_ref=out_ref.at[j],
                    send_sem=send_sems.at[3 * i + j], recv_sem=recv_sems.at[3 * i + j],
                    device_id=peer, device_id_type=MESH))
        for cp in copies:
            cp.start()
        for cp in copies:
            cp.wait_recv()
        for cp in copies:
            cp.wait_send()

    anyspec = pl.BlockSpec(memory_space=pl.ANY)
    return pl.pallas_call(
        body, out_shape=[jax.ShapeDtypeStruct((3, h.shape[0] // 4, h.shape[1]), h.dtype) for h in hs],
        in_specs=[anyspec] * n_arr, out_specs=[anyspec] * n_arr,
        scratch_shapes=[pltpu.SemaphoreType.DMA((3 * n_arr,))] * 2, name=name,
    )(*hs)


def _sum_with_chips(h, recv, name):
    m = h.shape[0] // 4
    n = h.shape[1]
    tr = _pick(m, (208, 128, 64, 32, 16))

    def body(h_ref, r_ref, o_ref):
        my_q = 2 * lax.axis_index("x") + lax.axis_index("y")
        own = h_ref[0].astype(f32)
        for q in range(1, 4):
            own = jnp.where(my_q == q, h_ref[q].astype(f32), own)
        o_ref[...] = ((own + r_ref[0].astype(f32)) + r_ref[1].astype(f32)) + r_ref[2].astype(f32)

    return pl.pallas_call(
        body, grid=(m // tr,),
        in_specs=[pl.BlockSpec((4, tr, n), lambda i: (0, i, 0)), pl.BlockSpec((3, tr, n), lambda i: (0, i, 0))],
        out_specs=pl.BlockSpec((tr, n), lambda i: (i, 0)), out_shape=jax.ShapeDtypeStruct((m, n), f32), name=name,
    )(h.reshape(4, m, n), recv)


def _sum_slots(parts, name):
    n_slot, m, n = parts.shape
    tr = _pick(m, (208, 128, 64, 32, 16, 8))

    def body(p_ref, o_ref):
        acc = p_ref[0]
        for s in range(1, n_slot):
            acc = acc + p_ref[s]
        o_ref[...] = acc

    return pl.pallas_call(
        body, grid=(m // tr,), in_specs=[pl.BlockSpec((n_slot, tr, n), lambda i: (0, i, 0))],
        out_specs=pl.BlockSpec((tr, n), lambda i: (i, 0)), out_shape=jax.ShapeDtypeStruct((m, n), parts.dtype), name=name,
    )(parts)


def _reduce_scatter_begin(gs, name):
    from_sibling = _exchange_siblings(gs, "rs_d2d_" + name)
    chip_sums = [_sum_with_sibling(g, r, f"rs_sum2_{name}_{i}") for i, (g, r) in enumerate(zip(gs, from_sibling))]
    lands = [lax.empty((3, h.shape[0] // 4, h.shape[1]), h.dtype) for h in chip_sums]
    return _split_start(chip_sums, lands, _chips_plan, 3, "rs_ici_start_" + name)


def _reduce_scatter_end(started, after, name):
    chip_sums, from_chips = _split_wait(started, _chips_plan, after, "rs_ici_wait_" + name)
    return [_sum_with_chips(h, r, f"rs_sum4_{name}_{i}") for i, (h, r) in enumerate(zip(chip_sums, from_chips))]


def _adamw(w, g, m, v, name):
    shape = w.shape
    n = shape[-1]
    r = w.size // n
    w2, g2, m2, v2 = (a.reshape(r, n) for a in (w, g, m, v))
    tr = _pick(r, (256, 128, 64, 32, 16, 8)) if r * n > 65536 else r

    def body(w_ref, g_ref, m_ref, v_ref, d_ref, mo_ref, vo_ref):
        gg = g_ref[...]
        mm = ADAM_B1 * m_ref[...] + (1.0 - ADAM_B1) * gg
        vv = ADAM_B2 * v_ref[...] + (1.0 - ADAM_B2) * jnp.square(gg)
        m_hat = mm / (1.0 - ADAM_B1 ** ADAM_STEP)
        v_hat = vv / (1.0 - ADAM_B2 ** ADAM_STEP)
        d_ref[...] = -ADAM_LR * (m_hat / (jnp.sqrt(v_hat) + ADAM_EPS) + ADAM_WD * w_ref[...])
        mo_ref[...] = mm
        vo_ref[...] = vv

    spec = pl.BlockSpec((tr, n), lambda i: (i, 0))
    outs = pl.pallas_call(
        body, grid=(r // tr,), in_specs=[spec] * 4, out_specs=[spec] * 3,
        out_shape=[jax.ShapeDtypeStruct((r, n), f32)] * 3, name=name,
    )(w2, g2, m2, v2)
    return tuple(o.reshape(shape) for o in outs)


_SMALL = ("shift_mu", "w_decay0", "a0", "k_k", "k_a", "r_k", "ln_x_w", "ln_x_b", "v_mix0", "lb_logits",
          "g_norm_w", "ln_w", "ln_b", "w_decay_up", "a_up", "v_mix_down", "v_mix_up")
_NAMES = ("w_in", "shift_mu", "w_decay0", "w_decay_up", "a0", "a_up", "k_k", "k_a", "r_k", "ln_x_w", "ln_x_b",
          "v_mix0", "v_mix_down", "v_mix_up", "lb_logits", "g_norm_w", "w_out", "ln_w", "ln_b")


def _pad_rows(a, rows, at_end):
    z = jnp.zeros((rows - a.shape[0], a.shape[1]), a.dtype)
    return jnp.concatenate([a, z] if at_end else [z, a], axis=0)


def kernel(x, w_in, shift_mu, w_decay0, w_decay_up, a0, a_up, k_k, k_a, r_k, ln_x_w, ln_x_b, v_mix0, v_mix_down, v_mix_up, lb_logits, g_norm_w, w_out, ln_w, ln_b, loss_target, m_w_in, m_shift_mu, m_w_decay0, m_w_decay_up, m_a0, m_a_up, m_k_k, m_k_a, m_r_k, m_ln_x_w, m_ln_x_b, m_v_mix0, m_v_mix_down, m_v_mix_up, m_lb_logits, m_g_norm_w, m_w_out, m_ln_w, m_ln_b, v_w_in, v_shift_mu, v_w_decay0, v_w_decay_up, v_a0, v_a_up, v_k_k, v_k_a, v_r_k, v_ln_x_w, v_ln_x_b, v_v_mix0, v_v_mix_down, v_v_mix_up, v_lb_logits, v_g_norm_w, v_w_out, v_ln_w, v_ln_b):
    weights = dict(w_in=w_in, shift_mu=shift_mu, w_decay0=w_decay0, w_decay_up=w_decay_up, a0=a0, a_up=a_up, k_k=k_k,
                   k_a=k_a, r_k=r_k, ln_x_w=ln_x_w, ln_x_b=ln_x_b, v_mix0=v_mix0, v_mix_down=v_mix_down,
                   v_mix_up=v_mix_up, lb_logits=lb_logits, g_norm_w=g_norm_w, w_out=w_out, ln_w=ln_w, ln_b=ln_b)
    mom1 = dict(w_in=m_w_in, shift_mu=m_shift_mu, w_decay0=m_w_decay0, w_decay_up=m_w_decay_up, a0=m_a0, a_up=m_a_up,
                k_k=m_k_k, k_a=m_k_a, r_k=m_r_k, ln_x_w=m_ln_x_w, ln_x_b=m_ln_x_b, v_mix0=m_v_mix0,
                v_mix_down=m_v_mix_down, v_mix_up=m_v_mix_up, lb_logits=m_lb_logits, g_norm_w=m_g_norm_w,
                w_out=m_w_out, ln_w=m_ln_w, ln_b=m_ln_b)
    mom2 = dict(w_in=v_w_in, shift_mu=v_shift_mu, w_decay0=v_w_decay0, w_decay_up=v_w_decay_up, a0=v_a0, a_up=v_a_up,
                k_k=v_k_k, k_a=v_k_a, r_k=v_r_k, ln_x_w=v_ln_x_w, ln_x_b=v_ln_x_b, v_mix0=v_v_mix0,
                v_mix_down=v_v_mix_down, v_mix_up=v_v_mix_up, lb_logits=v_lb_logits, g_norm_w=v_g_norm_w,
                w_out=v_w_out, ln_w=v_ln_w, ln_b=v_ln_b)
    assert x.shape[0] == 1 and w_in.shape[0] == DEPTH
    t, d = x.shape[1], x.shape[2]
    dr = w_decay0.shape[1]
    dh = g_norm_w.shape[1]
    rank_w, rank_a, rank_v = w_decay_up.shape[1], a_up.shape[1], v_mix_up.shape[1]
    rwc = 4 * dr + rank_w + rank_a
    assert rank_w + rank_a == LANES and rank_v <= LANES and dr + dh == d
    assert t % CHUNK == 0 and dr % LANES == 0 and dh % LANES == 0 and shift_mu.shape[1] == rwc
    n_pair = dr // LANES
    me = _index(_position())

    win_t = [_all_gather_rows(w_in[0].T.astype(bf16), "ag_w_in_0"), None]
    wout = [None, None]
    late_blocks = [w_out[0].astype(bf16), w_in[1].T.astype(bf16), w_out[1].astype(bf16)]
    late_lands = [lax.dynamic_update_slice(lax.empty((N_DEV * blk.shape[0], blk.shape[1]), bf16), blk, (me * blk.shape[0], 0))
                  for blk in late_blocks]
    late_gather = _split_start(late_blocks, late_lands, _gather_plan, len(_GATHER_FLIPS), "ag_late_start")
    shard = dr // N_DEV
    pack = jnp.concatenate([w_decay_up[0], w_decay_up[1], a_up[0], a_up[1], v_mix_up[0], v_mix_down[0].T], axis=0)
    pack = _all_gather_rows(pack, "ag_small")
    pack = jnp.transpose(pack.reshape(N_DEV, -1, shard), (1, 0, 2)).reshape(-1, dr)
    offs = [0, rank_w, 2 * rank_w, 2 * rank_w + rank_a, 2 * rank_w + 2 * rank_a, 2 * rank_w + 2 * rank_a + rank_v,
            2 * rank_w + 2 * rank_a + 2 * rank_v]
    wdu_f = [pack[offs[0]:offs[1]], pack[offs[1]:offs[2]]]
    aup_f = [pack[offs[2]:offs[3]], pack[offs[3]:offs[4]]]
    vup_f = pack[offs[4]:offs[5]]
    vdown_f = pack[offs[5]:offs[6]].T

    def after_start(a, started):
        return a + started[-1][0:1, 0:1]

    def rwkv_params(l):
        mu = after_start(shift_mu[0:1], late_gather) if l == 0 else shift_mu[l:l + 1]
        prm = [mu, w_decay0[l:l + 1], a0[l:l + 1], _pad_rows(wdu_f[l], LANES, True),
               _pad_rows(aup_f[l], LANES, False)]
        if l == 1:
            prm += [v_mix0[0:1], _pad_rows(vdown_f.T, LANES, True).T, _pad_rows(vup_f, LANES, True)]
        rows = jnp.stack([k_k[l], k_a[l], r_k[l], ln_x_w[l], ln_x_b[l]] + [jnp.zeros((dr,), f32)] * 3, axis=0)
        pp = jnp.transpose(rows.reshape(8, n_pair, LANES), (1, 0, 2))
        return tuple(prm), pp

    h = x[0]
    h16 = h.astype(bf16)
    tgt = loss_target[0]
    saved = []
    vfirst = None
    for l in range(DEPTH):
        prm, pp = rwkv_params(l)
        proj = _matmul(h16, win_t[l], "nt", f"mm_proj_{l}", (2048, 640, 2048))
        if l == 0:
            cat, vfirst, mck = _rwkv_fwd(False, proj, None, prm, pp, d)
        else:
            cat, mck = _rwkv_fwd(True, proj, vfirst, prm, pp, d)
        cat, sck = _hgrn_fwd(l == 1, proj, lb_logits, g_norm_w[l:l + 1], cat, rwc)
        if l == 0:
            _, arrived = _split_wait(late_gather, _gather_plan, cat, "ag_late_wait")
            wout[0], win_t[1], wout[1] = _gather_forward(arrived, "ag_late_forward")
        y = _matmul(cat, wout[l], "nn", f"mm_out_{l}", (1024, 1024, 2048))
        saved.append((h, h16, proj, prm, pp, mck, sck, cat, y))
        if l < DEPTH - 1:
            h, h16 = _ln_fwd(h, y, ln_w[l:l + 1], ln_b[l:l + 1])
        else:
            dh_out, loss_part = _ln_loss(h, y, ln_w[l:l + 1], ln_b[l:l + 1], tgt)
    loss = lax.psum(loss_part[0, 0], ("x", "y", "c"))

    grads = {}
    big = {}
    dvfirst = None
    d_lbl = None
    rs_started = {}
    for l in reversed(range(DEPTH)):
        h_l, h16_l, proj, prm, pp, mck, sck, cat, y = saved[l]
        ln_w_l = ln_w[l:l + 1] if l == DEPTH - 1 else after_start(ln_w[l:l + 1], rs_started[l + 1])
        dy, dy16, g_ln_w, g_ln_b = _ln_bwd(h_l, y, ln_w_l, ln_b[l:l + 1], dh_out)
        dcat = _matmul(dy16, wout[l], "nt", f"mm_dcat_{l}", (1024, 1024, 2048))
        big[("w_out", l)] = _matmul(cat, dy16, "tn", f"mm_dwout_{l}", (512, 2048, 2048), out_dtype=bf16)
        if l == 1:
            outs = _rwkv_bwd(True, proj, vfirst, prm, pp, mck, dcat, None)
            dproj_r, dvfirst = outs[0], outs[1]
            dprm, dpp = outs[2:-1], outs[-1]
        else:
            outs = _rwkv_bwd(False, proj, None, prm, pp, mck, dcat, dvfirst)
            dproj_r = outs[0]
            dprm, dpp = outs[1:-1], outs[-1]
        dproj_h, dlbl_l, dgnw = _hgrn_bwd(l == 1, proj, lb_logits, g_norm_w[l:l + 1], sck, dcat, rwc)
        dproj = jnp.concatenate([dproj_r] + [dproj_h[i] for i in range(4)], axis=1)
        big[("w_in", l)] = _matmul(dproj, h16_l, "tn", f"mm_dwin_{l}", (640, 2048, 2048), out_dtype=bf16)
        rs_started[l] = _reduce_scatter_begin([big[("w_in", l)], big[("w_out", l)]], f"l{l}")
        dy_res = after_start(dy, rs_started[l]) if l == 0 else dy
        dh_out = _matmul(dproj, win_t[l], "nn", f"mm_dh_{l}", (1024, 1024, 640), add=dy_res, add_scale=ALPHA)
        dpp = jnp.transpose(dpp, (1, 0, 2)).reshape(8, dr)
        grads[l] = dict(shift_mu=dprm[0][0], w_decay0=dprm[1][0], a0=dprm[2][0], w_decay_up=dprm[3][:rank_w],
                        a_up=dprm[4][rank_w:], k_k=dpp[0], k_a=dpp[1], r_k=dpp[2], ln_x_w=dpp[3], ln_x_b=dpp[4],
                        g_norm_w=dgnw[0], ln_w=g_ln_w[0], ln_b=g_ln_b[0])
        if l == 1:
            grads[l].update(v_mix0=dprm[5][0], v_mix_down=dprm[6][:, :rank_v], v_mix_up=dprm[7][:rank_v])
            d_lbl = dlbl_l
    grad_x = dh_out[None]

    def both(name):
        return jnp.stack([grads[0][name], grads[1][name]])

    small = dict(shift_mu=both("shift_mu"), w_decay0=both("w_decay0"), a0=both("a0"), k_k=both("k_k"), k_a=both("k_a"),
                 r_k=both("r_k"), ln_x_w=both("ln_x_w"), ln_x_b=both("ln_x_b"), v_mix0=grads[1]["v_mix0"][None],
                 lb_logits=d_lbl, g_norm_w=both("g_norm_w"), ln_w=both("ln_w"), ln_b=both("ln_b"),
                 w_decay_up=both("w_decay_up"), a_up=both("a_up"), v_mix_down=grads[1]["v_mix_down"][None],
                 v_mix_up=grads[1]["v_mix_up"][None])
    flat = jnp.concatenate([small[nm].reshape(-1) for nm in _SMALL])
    n_flat = flat.shape[0]
    rows = -(-n_flat // (8 * LANES)) * 8
    flat = jnp.concatenate([flat, jnp.zeros((rows * LANES - n_flat,), f32)]).reshape(rows, LANES)
    total = _sum_slots(_all_gather_rows(flat, "ag_small_grads").reshape(N_DEV, rows, LANES), "sum_small_grads").reshape(-1)
    gsm = {}
    off = 0
    for nm in _SMALL:
        size = small[nm].size
        gsm[nm] = total[off:off + size].reshape(small[nm].shape)
        off += size
    gsm["w_decay_up"] = lax.dynamic_slice_in_dim(gsm["w_decay_up"], me * shard, shard, axis=2)
    gsm["a_up"] = lax.dynamic_slice_in_dim(gsm["a_up"], me * shard, shard, axis=2)
    gsm["v_mix_up"] = lax.dynamic_slice_in_dim(gsm["v_mix_up"], me * shard, shard, axis=2)
    gsm["v_mix_down"] = lax.dynamic_slice_in_dim(gsm["v_mix_down"], me * shard, shard, axis=1)
    reduced = {1: _reduce_scatter_end(rs_started[1], dh_out, "l1")}
    reduced[0] = _reduce_scatter_end(rs_started[0], total, "l0")
    gsm["w_in"] = jnp.stack([reduced[l][0].T for l in range(DEPTH)])
    gsm["w_out"] = jnp.stack([reduced[l][1] for l in range(DEPTH)])

    deltas, new_m, new_v = {}, {}, {}
    for nm in _NAMES:
        deltas[nm], new_m[nm], new_v[nm] = _adamw(weights[nm], gsm[nm], mom1[nm], mom2[nm], "adamw_" + nm)
    return (loss, grad_x, *[gsm[nm] for nm in _NAMES], *[deltas[nm] for nm in _NAMES],
            *[new_m[nm] for nm in _NAMES], *[new_v[nm] for nm in _NAMES])
```

```python
import functools

import jax
import jax.numpy as jnp
from jax import lax
from jax.experimental import pallas as pl
from jax.experimental.pallas import tpu as pltpu

f32 = jnp.float32
bf16 = jnp.bfloat16

N_DEV = 8
CHUNK = 64
LANES = 128
RWKV_HEAD = 64
DEPTH = 2
ALPHA = (2 * DEPTH) ** 0.25
LN_EPS = 1e-5
GN_EPS = 64e-5
RMS_EPS = 1e-5
LB_FLOOR = 1e-30
ADAM_LR, ADAM_B1, ADAM_B2, ADAM_EPS, ADAM_WD, ADAM_STEP = 0.001, 0.9, 0.999, 1e-08, 0.01, 10
MESH = pl.DeviceIdType.MESH


def _iota(shape, d):
    return lax.broadcasted_iota(jnp.int32, shape, d)


_DIMS = {"nn": (((1,), (0,)), ((), ())), "nt": (((1,), (1,)), ((), ())), "tn": (((0,), (0,)), ((), ()))}
_BATCH_DIMS = {"nn": (((2,), (1,)), ((0,), (0,))), "nt": (((2,), (2,)), ((0,), (0,))), "tn": (((1,), (1,)), ((0,), (0,)))}
_K_AXES = {"nn": (-1, -2), "nt": (-1, -1), "tn": (-2, -2)}


def _mxu(a, b, mode):
    return lax.dot_general(a, b, (_BATCH_DIMS if a.ndim == 3 else _DIMS)[mode], preferred_element_type=f32)


def _split(x):
    hi = x.astype(bf16)
    return hi, (x - hi.astype(f32)).astype(bf16)


def _mm2_impl(a, b, mode, passes=3):
    ah, al = _split(a)
    if passes == 3:
        bh, bl = _split(b)
        lhs, rhs = [ah, ah, al], [bh, bl, bh]
    else:
        bh = b.astype(bf16)
        lhs, rhs = [ah, al], [bh, bh]
    ka, kb = _K_AXES[mode]
    k = a.shape[ka]
    if k % (LANES if -1 in (ka, kb) else 16) == 0:
        return _mxu(jnp.concatenate(lhs, axis=ka), jnp.concatenate(rhs, axis=kb), mode)
    out = _mxu(lhs[0], rhs[0], mode)
    for x, y in zip(lhs[1:], rhs[1:]):
        out = out + _mxu(x, y, mode)
    return out


@functools.partial(jax.custom_vjp, nondiff_argnums=(2, 3))
def _mm2(a, b, mode, passes=3):
    return _mm2_impl(a, b, mode, passes)


def _mm2_fwd(a, b, mode, passes):
    return _mm2_impl(a, b, mode, passes), (a, b)


def _mm2_bwd(mode, passes, res, g):
    a, b = res
    if mode == "nn":
        return _mm2_impl(g, b, "nt", passes), _mm2_impl(a, g, "tn", passes)
    if mode == "nt":
        return _mm2_impl(g, b, "nn", passes), _mm2_impl(g, a, "tn", passes)
    return _mm2_impl(b, g, "nt", passes), _mm2_impl(a, g, "nn", passes)


_mm2.defvjp(_mm2_fwd, _mm2_bwd)

TRI_PASSES = 2
APPLY_PASSES = 2


def _const_impl(cm, x, mode):
    hi, lo = _split(x)
    if mode in ("r", "rt"):
        shape = x.shape
        hi, lo = hi.reshape(-1, shape[-1]), lo.reshape(-1, shape[-1])
        dims = "nn" if mode == "r" else "nt"
        out = _mxu(hi, cm, dims) + _mxu(lo, cm, dims)
        return out.reshape(shape[:-1] + (out.shape[-1],))
    if x.ndim == 3:
        cm = jnp.broadcast_to(cm, (x.shape[0],) + cm.shape)
    return _mxu(cm, hi, mode) + _mxu(cm, lo, mode)


@jax.custom_vjp
def _const_left(cm, x):
    return _const_impl(cm, x, "nn")


_const_left.defvjp(lambda cm, x: (_const_impl(cm, x, "nn"), cm),
                   lambda cm, g: (jnp.zeros_like(cm), _const_impl(cm, g, "tn")))


@jax.custom_vjp
def _const_right(x, cm):
    return _const_impl(cm, x, "r")


_const_right.defvjp(lambda x, cm: (_const_impl(cm, x, "r"), cm),
                    lambda cm, g: (_const_impl(cm, g, "rt"), jnp.zeros_like(cm)))


def _tri_inv(a):
    n = a.shape[-1]
    tm = (_iota((n, n), 0) == _iota((n, n), 1)).astype(f32) + a
    ak = a
    for _ in range(5):
        ak = _mm2_impl(ak, ak, "nn", TRI_PASSES)
        tm = tm + _mm2_impl(tm, ak, "nn", TRI_PASSES)
    return tm


@jax.custom_vjp
def _tri_solve(a, x):
    return _mm2_impl(_tri_inv(a), x, "nn")


def _tri_solve_fwd(a, x):
    tm = _tri_inv(a)
    u = _mm2_impl(tm, x, "nn")
    return u, (tm, u)


def _tri_solve_bwd(res, du):
    tm, u = res
    dx = _mm2_impl(tm, du, "tn")
    return _mm2_impl(dx, u, "nt"), dx


_tri_solve.defvjp(_tri_solve_fwd, _tri_solve_bwd)


def _col_of_row(row_vec):
    n = row_vec.shape[-1]
    eye = _iota((n, n), 0) == _iota((n, n), 1)
    return jnp.sum(jnp.where(eye, jnp.broadcast_to(row_vec, row_vec.shape[:-2] + (n, n)), 0.0), axis=-1, keepdims=True)


def _softplus(x):
    return jnp.maximum(x, 0.0) + jnp.log1p(jnp.exp(-jnp.abs(x)))


def _log_sigmoid(x):
    return -_softplus(-x)


def _logaddexp(a, b):
    return jnp.maximum(a, b) + jnp.log1p(jnp.exp(-jnp.abs(a - b)))


def _silu(x):
    return x * jax.nn.sigmoid(x)


def _tril(c, strict):
    r, s = _iota((c, c), 0), _iota((c, c), 1)
    return (r > s) if strict else (r >= s)


def _last_row(a):
    c = a.shape[-2]
    return jnp.sum(jnp.where(_iota(a.shape, a.ndim - 2) == c - 1, a, 0.0), axis=-2, keepdims=True)


def _rwkv_pre(layer1, prm, y, prev, vf):
    c = y.shape[0]
    if layer1:
        mu, w0, a0, wup, aup, v0, vdown, vup = prm
    else:
        mu, w0, a0, wup, aup = prm
    dr = w0.shape[1]
    shift = (_iota((c, c), 0) == _iota((c, c), 1) + 1).astype(bf16)
    y_prev = _const_left(shift, y) + jnp.where(_iota((c, 1), 0) == 0, prev, 0.0)
    rw = y + mu * (y_prev - y)
    r, k, v, z = (rw[:, i * dr:(i + 1) * dr] for i in range(4))
    wdad = rw[:, 4 * dr:4 * dr + LANES]
    w_raw = w0 + _mm2(jnp.tanh(wdad), wup, "nn")
    lw = -jnp.exp(-_softplus(-w_raw) - 0.5)
    asig = jax.nn.sigmoid(a0 + _mm2(wdad, aup, "nn"))
    if layer1:
        v = v + (vf - v) * jax.nn.sigmoid(v0 + _mm2(_mm2(v, vdown, "nn"), vup, "nn"))
    return r, k, v, z, lw, asig


def _rwkv_pair(pp, m0, xs):
    kkw, kaw, rkw, gnw, gnb = pp
    r, k, v, z, lw, asig = xs
    c = r.shape[-2]
    n2 = 2 * c
    lane = _iota((1, LANES), 1)
    mh0, mh1 = (lane < RWKV_HEAD).astype(f32), (lane >= RWKV_HEAD).astype(f32)
    same_head = _iota((LANES, LANES), 0) // RWKV_HEAD == _iota((LANES, LANES), 1) // RWKV_HEAD
    g = same_head.astype(bf16)

    def seg(x):
        return _const_right(x, g)

    def stack(x):
        return jnp.concatenate([x * mh0, x * mh1], axis=-2)

    kk = k * kkw
    kk = kk / jnp.maximum(jnp.sqrt(seg(kk * kk)), 1e-12)
    k2 = k * (1.0 + (asig - 1.0) * kaw)
    a = -kk
    b = kk * asig
    cum = _const_left(_tril(c, False).astype(bf16), lw)
    at = stack(a * jnp.exp(cum - lw))
    rt = stack(r * jnp.exp(cum))
    en = jnp.exp(-cum)
    sc = _mm2(jnp.concatenate([at, rt], axis=-2), jnp.concatenate([stack(b * en), stack(k2 * en)], axis=-2), "nt")
    row, col = _iota((n2, n2), 0), _iota((n2, n2), 1)
    same = row // c == col // c
    strict = same & (row % c > col % c)
    incl = same & (row % c >= col % c)
    aab = jnp.where(strict, sc[..., :n2, :n2], 0.0)
    aak = jnp.where(strict, sc[..., :n2, n2:], 0.0)
    arb = jnp.where(incl, sc[..., n2:, :n2], 0.0)
    ark = jnp.where(incl, sc[..., n2:, n2:], 0.0)
    vv = jnp.concatenate([v, v], axis=-2)
    mask_st = jnp.concatenate([jnp.broadcast_to(mh0, (c, LANES)), jnp.broadcast_to(mh1, (c, LANES))], axis=0)
    x_st = _mm2(jnp.concatenate([at, aak], axis=-1), jnp.concatenate([m0, vv], axis=-2), "nn", APPLY_PASSES)
    u_st = _tri_solve(aab, x_st) * mask_st
    o_st = _mm2(jnp.concatenate([rt, arb, ark], axis=-1), jnp.concatenate([m0, u_st, vv], axis=-2), "nn", APPLY_PASSES) * mask_st
    u = u_st[..., :c, :] + u_st[..., c:, :]
    o = o_st[..., :c, :] + o_st[..., c:, :]
    cum_last = _last_row(cum)
    dec_end = jnp.exp(cum_last - cum)
    m_new = _col_of_row(jnp.exp(cum_last)) * m0 + _mm2(
        jnp.concatenate([b * dec_end, k2 * dec_end], axis=-2), jnp.concatenate([u, v], axis=-2), "tn", APPLY_PASSES) * same_head.astype(f32)
    mean = seg(o) * (1.0 / RWKV_HEAD)
    d = o - mean
    var = seg(d * d) * (1.0 / RWKV_HEAD)
    on = d * lax.rsqrt(var + GN_EPS) * gnw + gnb
    bonus = seg(r * k2 * rkw) * v
    return (on + bonus) * _silu(z), m_new


def _split_lanes(a, n):
    return [a[:, i * LANES:(i + 1) * LANES] for i in range(n)]


def _group(n):
    return 8 if n % 8 == 0 else (4 if n % 4 == 0 else (2 if n % 2 == 0 else 1))


def _rwkv_specs(layer1, t, dr, rwc, n_pair, rev):
    nc = t // CHUNK
    grp = _group(n_pair)

    def cidx(c):
        return (nc - 1 - c) if rev else c

    full = lambda shape: pl.BlockSpec(shape, lambda c, p: tuple(0 for _ in shape))
    specs = [
        pl.BlockSpec((CHUNK, rwc), lambda c, p: (cidx(c), 0)),
        pl.BlockSpec((8, rwc), lambda c, p: (jnp.maximum(cidx(c) * (CHUNK // 8) - 1, 0), 0)),
    ]
    if layer1:
        specs.append(pl.BlockSpec((CHUNK, dr), lambda c, p: (cidx(c), 0)))
    prm_shapes = [(1, rwc), (1, dr), (1, dr), (LANES, dr), (LANES, dr)]
    if layer1:
        prm_shapes += [(1, dr), (dr, LANES), (LANES, dr)]
    specs += [full(s) for s in prm_shapes]
    specs.append(pl.BlockSpec((grp, 8, LANES), lambda c, p: (p, 0, 0)))
    return specs, prm_shapes, cidx, full


def _rwkv_fwd(layer1, proj, vf, prm, pp, cat_width):
    t = proj.shape[0]
    dr = prm[1].shape[1]
    rwc = prm[0].shape[1]
    n_pair = dr // LANES
    nc = t // CHUNK
    n_prm = len(prm)
    specs, _, _, _ = _rwkv_specs(layer1, t, dr, rwc, n_pair, False)

    def body(*refs):
        y_ref, prev_ref = refs[0], refs[1]
        i = 2
        vf_ref = None
        if layer1:
            vf_ref = refs[i]
            i += 1
        prm_refs = refs[i:i + n_prm]
        i += n_prm
        pp_ref = refs[i]
        i += 1
        cat_ref = refs[i]
        i += 1
        vout_ref = None
        if not layer1:
            vout_ref = refs[i]
            i += 1
        mck_ref, x_s, m_s = refs[i], refs[i + 1], refs[i + 2]
        c, p = pl.program_id(0), pl.program_id(1)

        @pl.when((c == 0) & (p == 0))
        def _():
            m_s[...] = jnp.zeros_like(m_s)

        @pl.when(p == 0)
        def _():
            prev = prev_ref[pl.ds(7, 1), :] * (c != 0).astype(f32)
            xs = _rwkv_pre(layer1, tuple(r[...] for r in prm_refs), y_ref[...], prev,
                           vf_ref[...] if layer1 else None)
            for q, a in enumerate(xs):
                for j, piece in enumerate(_split_lanes(a, n_pair)):
                    x_s[q * n_pair + j] = piece
            if not layer1:
                vout_ref[...] = xs[2]

        m0 = m_s[pl.ds(p * grp, grp)]
        mck_ref[0] = m0
        ppv = tuple(pp_ref[:, pl.ds(q, 1), :] for q in range(5))
        og, m_new = _rwkv_pair(ppv, m0, tuple(x_s[pl.ds(q * n_pair + p * grp, grp)] for q in range(6)))
        for j in range(grp):
            cat_ref[:, j * LANES:(j + 1) * LANES] = og[j]
        m_s[pl.ds(p * grp, grp)] = m_new

    grp = _group(n_pair)
    out_shape = [jax.ShapeDtypeStruct((t, cat_width), f32)]
    out_specs = [pl.BlockSpec((CHUNK, grp * LANES), lambda c, p: (c, p))]
    if not layer1:
        out_shape.append(jax.ShapeDtypeStruct((t, dr), f32))
        out_specs.append(pl.BlockSpec((CHUNK, dr), lambda c, p: (c, 0)))
    out_shape.append(jax.ShapeDtypeStruct((nc, n_pair, LANES, LANES), f32))
    out_specs.append(pl.BlockSpec((1, grp, LANES, LANES), lambda c, p: (c, p, 0, 0)))
    args = [proj, proj] + ([vf] if layer1 else []) + list(prm) + [pp]
    return pl.pallas_call(
        body, grid=(nc, n_pair // grp), in_specs=specs, out_specs=out_specs, out_shape=out_shape,
        scratch_shapes=[pltpu.VMEM((6 * n_pair, CHUNK, LANES), f32), pltpu.VMEM((n_pair, LANES, LANES), f32)],
        compiler_params=pltpu.CompilerParams(dimension_semantics=("arbitrary", "arbitrary")),
        name=f"rwkv_fwd_l{int(layer1)}",
    )(*args)


def _rwkv_bwd(layer1, proj, vf, prm, pp, mck, dcat, dvout):
    t = proj.shape[0]
    dr = prm[1].shape[1]
    rwc = prm[0].shape[1]
    n_pair = dr // LANES
    nc = t // CHUNK
    n_prm = len(prm)
    specs, prm_shapes, cidx, full = _rwkv_specs(layer1, t, dr, rwc, n_pair, True)
    grp = _group(n_pair)
    n_step = n_pair // grp
    specs.append(pl.BlockSpec((1, grp, LANES, LANES), lambda c, p: (cidx(c), p, 0, 0)))
    specs.append(pl.BlockSpec((CHUNK, grp * LANES), lambda c, p: (cidx(c), p)))
    if not layer1:
        specs.append(pl.BlockSpec((CHUNK, dr), lambda c, p: (cidx(c), 0)))

    def body(*refs):
        y_ref, prev_ref = refs[0], refs[1]
        i = 2
        vf_ref = None
        if layer1:
            vf_ref = refs[i]
            i += 1
        prm_refs = refs[i:i + n_prm]
        i += n_prm
        pp_ref, mck_ref, dog_ref = refs[i], refs[i + 1], refs[i + 2]
        i += 3
        dvout_ref = None
        if not layer1:
            dvout_ref = refs[i]
            i += 1
        dy_ref = refs[i]
        i += 1
        dvf_ref = None
        if layer1:
            dvf_ref = refs[i]
            i += 1
        dprm_refs = refs[i:i + n_prm]
        i += n_prm
        dpp_ref = refs[i]
        x_s, dx_s, dm_s, dprev_s = refs[i + 1:i + 5]
        c, p = pl.program_id(0), pl.program_id(1)
        cr = nc - 1 - c

        def prev_row():
            return prev_ref[pl.ds(7, 1), :] * (cr != 0).astype(f32)

        @pl.when((c == 0) & (p == 0))
        def _():
            dm_s[...] = jnp.zeros_like(dm_s)
            dprev_s[...] = jnp.zeros_like(dprev_s)
            dpp_ref[...] = jnp.zeros_like(dpp_ref)
            for r in dprm_refs:
                r[...] = jnp.zeros_like(r)

        @pl.when(p == 0)
        def _():
            xs = _rwkv_pre(layer1, tuple(r[...] for r in prm_refs), y_ref[...], prev_row(),
                           vf_ref[...] if layer1 else None)
            for q, a in enumerate(xs):
                for j, piece in enumerate(_split_lanes(a, n_pair)):
                    x_s[q * n_pair + j] = piece

        units = pl.ds(p * grp, grp)
        ppv = tuple(pp_ref[:, pl.ds(q, 1), :] for q in range(5))
        xs_p = tuple(x_s[pl.ds(q * n_pair + p * grp, grp)] for q in range(6))
        dog = jnp.stack([dog_ref[:, j * LANES:(j + 1) * LANES] for j in range(grp)], axis=0)
        _, vjp_pair = jax.vjp(_rwkv_pair, ppv, mck_ref[0], xs_p)
        dppv, dm0, dxs = vjp_pair((dog, dm_s[units]))
        dm_s[units] = dm0
        for q in range(6):
            dx_s[pl.ds(q * n_pair + p * grp, grp)] = dxs[q]
        for q in range(5):
            dpp_ref[units, pl.ds(q, 1), :] += dppv[q]

        @pl.when(p == n_step - 1)
        def _():
            dxs_full = [jnp.concatenate([dx_s[q * n_pair + j] for j in range(n_pair)], axis=1) for q in range(6)]
            if not layer1:
                dxs_full[2] = dxs_full[2] + dvout_ref[...]
            prm_v = tuple(r[...] for r in prm_refs)
            if layer1:
                _, vjp_pre = jax.vjp(functools.partial(_rwkv_pre, True), prm_v, y_ref[...], prev_row(), vf_ref[...])
                dprm, dy, dprev, dvf = vjp_pre(tuple(dxs_full))
                dvf_ref[...] = dvf
            else:
                _, vjp_pre = jax.vjp(lambda a, b, d: _rwkv_pre(False, a, b, d, None), prm_v, y_ref[...], prev_row())
                dprm, dy, dprev = vjp_pre(tuple(dxs_full))
            dy_ref[...] = (dy + jnp.where(_iota((CHUNK, 1), 0) == CHUNK - 1, dprev_s[...], 0.0)).astype(bf16)
            dprev_s[...] = dprev
            for r, gval in zip(dprm_refs, dprm):
                r[...] += gval

    out_shape = [jax.ShapeDtypeStruct((t, rwc), bf16)]
    out_specs = [pl.BlockSpec((CHUNK, rwc), lambda c, p: (cidx(c), 0))]
    if layer1:
        out_shape.append(jax.ShapeDtypeStruct((t, dr), f32))
        out_specs.append(pl.BlockSpec((CHUNK, dr), lambda c, p: (cidx(c), 0)))
    out_shape += [jax.ShapeDtypeStruct(s, f32) for s in prm_shapes]
    out_specs += [full(s) for s in prm_shapes]
    out_shape.append(jax.ShapeDtypeStruct((n_pair, 8, LANES), f32))
    out_specs.append(full((n_pair, 8, LANES)))
    args = [proj, proj] + ([vf] if layer1 else []) + list(prm) + [pp, mck, dcat] + ([] if layer1 else [dvout])
    return pl.pallas_call(
        body, grid=(nc, n_step), in_specs=specs, out_specs=out_specs, out_shape=out_shape,
        scratch_shapes=[pltpu.VMEM((6 * n_pair, CHUNK, LANES), f32), pltpu.VMEM((6 * n_pair, CHUNK, LANES), f32),
                        pltpu.VMEM((n_pair, LANES, LANES), f32), pltpu.VMEM((1, rwc), f32)],
        compiler_params=pltpu.CompilerParams(dimension_semantics=("arbitrary", "arbitrary")),
        name=f"rwkv_bwd_l{int(layer1)}",
    )(*args)


def _hgrn_chunk(layer1, lbl, gnw, s0, q_raw, f_raw, i_in, z):
    c = q_raw.shape[-2]
    q = _silu(q_raw)
    ls = _log_sigmoid(f_raw)
    if layer1:
        l0, l1 = lbl[..., 0:1, :], lbl[..., 1:2, :]
        mx = jnp.maximum(l0, l1)
        e0, e1 = jnp.exp(l0 - mx), jnp.exp(l1 - mx)
        sm0, sm1 = e0 / (e0 + e1), e1 / (e0 + e1)
        lb = (sm0 + sm1) - sm0
        log_f = _logaddexp(jnp.log(jnp.maximum(lb, LB_FLOOR)), jnp.log1p(-lb) + ls)
        k = (1.0 - lb) * jax.nn.sigmoid(-f_raw)
    else:
        log_f = _logaddexp(jnp.full_like(ls, jnp.log(jnp.float32(LB_FLOOR))), ls)
        k = jax.nn.sigmoid(-f_raw)
    row, col = _iota((c, c), 0), _iota((c, c), 1)
    trow = _iota((c, 1), 0)
    halves = []
    half = c // 2
    while half >= 1:
        halves.append(half)
        half //= 2
    cmat = jnp.concatenate([(col <= row).astype(f32)]
                           + [(col <= (row // (2 * hf)) * (2 * hf) + hf - 1).astype(f32) for hf in halves], axis=0)
    ball = _const_left(cmat.astype(bf16), log_f)
    b = ball[..., :c, :]
    att = None
    for lvl, hf in enumerate(halves):
        blk = 2 * hf
        bref = ball[..., (lvl + 1) * c:(lvl + 2) * c, :]
        upper = (trow % blk) >= hf
        qh = q * jnp.exp(jnp.where(upper, b - bref, 0.0)) * upper.astype(f32)
        kh = k * jnp.exp(jnp.where(upper, 0.0, bref - b)) * (1.0 - upper.astype(f32))
        term = jnp.where(row // blk == col // blk, _mm2(qh, kh, "nt", APPLY_PASSES), 0.0)
        att = term if att is None else att + term
    lhs = jnp.concatenate([q * jnp.exp(b), att, jnp.zeros(att.shape[:-1] + (LANES - c,), f32)], axis=-1)
    rhs = jnp.concatenate([s0, i_in, jnp.zeros(i_in.shape[:-2] + (LANES - c, i_in.shape[-1]), f32)], axis=-2)
    o = _mm2(lhs, rhs, "nn", APPLY_PASSES) + jnp.sum(q * k, axis=-1, keepdims=True) * i_in
    b_last = _last_row(b)
    s_new = _col_of_row(jnp.exp(b_last)) * s0 + _mm2(k * jnp.exp(b_last - b), i_in, "tn", APPLY_PASSES)
    o = o * lax.rsqrt(jnp.mean(o * o, axis=-1, keepdims=True) + RMS_EPS)
    return o * gnw * _silu(z), s_new


def _hgrn_in_specs(t, dh, col0, rev):
    nc = t // CHUNK
    nh = dh // LANES

    def cidx(c):
        return (nc - 1 - c) if rev else c

    grp = _group(nh)
    specs = [pl.BlockSpec((CHUNK, LANES), functools.partial(lambda g, j, h, c: (cidx(c), col0 + g * nh + h * grp + j), g, j))
             for j in range(grp) for g in range(4)]
    specs.append(pl.BlockSpec((2, grp * LANES), lambda h, c: (0, h)))
    specs.append(pl.BlockSpec((1, grp * LANES), lambda h, c: (0, h)))
    return specs, cidx, grp


def _hgrn_fwd(layer1, proj, lbl, gnw, cat, rwc):
    t, d = cat.shape
    dh = gnw.shape[1]
    nh = dh // LANES
    nc = t // CHUNK
    col0 = rwc // LANES
    specs, _, grp = _hgrn_in_specs(t, dh, col0, False)
    specs.append(pl.BlockSpec(memory_space=pl.ANY))
    assert (d - dh) % (grp * LANES) == 0
    cat_col0 = (d - dh) // (grp * LANES)

    def body(*refs):
        x_refs = refs[:4 * grp]
        lbl_ref, gnw_ref, _, cat_ref, sck_ref, s_s = refs[4 * grp:]
        c = pl.program_id(1)

        @pl.when(c == 0)
        def _():
            s_s[...] = jnp.zeros_like(s_s)

        lanes = [slice(j * LANES, (j + 1) * LANES) for j in range(grp)]
        s0 = s_s[...]
        sck_ref[:, 0] = s0
        out, s_new = _hgrn_chunk(layer1, jnp.stack([lbl_ref[:, ln] for ln in lanes]), jnp.stack([gnw_ref[:, ln] for ln in lanes]),
                                 s0, *(jnp.stack([x_refs[4 * j + g][...] for j in range(grp)]) for g in range(4)))
        for j in range(grp):
            cat_ref[:, lanes[j]] = out[j]
        s_s[...] = s_new

    return pl.pallas_call(
        body, grid=(nh // grp, nc), in_specs=specs,
        out_specs=[pl.BlockSpec((CHUNK, grp * LANES), lambda h, c: (c, cat_col0 + h)),
                   pl.BlockSpec((grp, 1, LANES, LANES), lambda h, c: (h, c, 0, 0))],
        out_shape=[jax.ShapeDtypeStruct((t, d), f32), jax.ShapeDtypeStruct((nh, nc, LANES, LANES), f32)],
        scratch_shapes=[pltpu.VMEM((grp, LANES, LANES), f32)],
        input_output_aliases={4 * grp + 2: 0},
        compiler_params=pltpu.CompilerParams(dimension_semantics=("arbitrary", "arbitrary")),
        name=f"hgrn_fwd_l{int(layer1)}",
    )(*([proj] * (4 * grp)), lbl, gnw, cat)


def _hgrn_bwd(layer1, proj, lbl, gnw, sck, dcat, rwc):
    t, d = dcat.shape
    dh = gnw.shape[1]
    nh = dh // LANES
    nc = t // CHUNK
    col0 = rwc // LANES
    specs, cidx, grp = _hgrn_in_specs(t, dh, col0, True)
    assert (d - dh) % (grp * LANES) == 0
    cat_col0 = (d - dh) // (grp * LANES)
    specs.append(pl.BlockSpec((grp, 1, LANES, LANES), lambda h, c: (h, cidx(c), 0, 0)))
    specs.append(pl.BlockSpec((CHUNK, grp * LANES), lambda h, c: (cidx(c), cat_col0 + h)))

    def body(*refs):
        x_refs = refs[:4 * grp]
        lbl_ref, gnw_ref, sck_ref, do_ref, dp_ref, dlbl_ref, dgnw_ref, ds_s = refs[4 * grp:]
        c = pl.program_id(1)

        @pl.when(c == 0)
        def _():
            ds_s[...] = jnp.zeros_like(ds_s)
            dlbl_ref[...] = jnp.zeros_like(dlbl_ref)
            dgnw_ref[...] = jnp.zeros_like(dgnw_ref)

        lanes = [slice(j * LANES, (j + 1) * LANES) for j in range(grp)]
        _, vjp = jax.vjp(functools.partial(_hgrn_chunk, layer1),
                         jnp.stack([lbl_ref[:, ln] for ln in lanes]), jnp.stack([gnw_ref[:, ln] for ln in lanes]), sck_ref[:, 0],
                         *(jnp.stack([x_refs[4 * j + g][...] for j in range(grp)]) for g in range(4)))
        dlbl, dgnw, ds0, dq, df, di, dz = vjp((jnp.stack([do_ref[:, ln] for ln in lanes]), ds_s[...]))
        ds_s[...] = ds0
        for j in range(grp):
            dlbl_ref[:, lanes[j]] += dlbl[j]
            dgnw_ref[:, lanes[j]] += dgnw[j]
            for g, val in enumerate((dq, df, di, dz)):
                dp_ref[g, :, lanes[j]] = val[j].astype(bf16)

    return pl.pallas_call(
        body, grid=(nh // grp, nc), in_specs=specs,
        out_specs=[pl.BlockSpec((4, CHUNK, grp * LANES), lambda h, c: (0, cidx(c), h)),
                   pl.BlockSpec((2, grp * LANES), lambda h, c: (0, h)),
                   pl.BlockSpec((1, grp * LANES), lambda h, c: (0, h))],
        out_shape=[jax.ShapeDtypeStruct((4, t, dh), bf16), jax.ShapeDtypeStruct((2, dh), f32),
                   jax.ShapeDtypeStruct((1, dh), f32)],
        scratch_shapes=[pltpu.VMEM((grp, LANES, LANES), f32)],
        compiler_params=pltpu.CompilerParams(dimension_semantics=("arbitrary", "arbitrary")),
        name=f"hgrn_bwd_l{int(layer1)}",
    )(*([proj] * (4 * grp)), lbl, gnw, sck, dcat)


def _ln(h, y, w, b):
    u = ALPHA * h + y
    mu = jnp.mean(u, axis=-1, keepdims=True)
    var = jnp.mean(jnp.square(u - mu), axis=-1, keepdims=True)
    return (u - mu) * lax.rsqrt(var + LN_EPS) * w + b


def _row_tile(t):
    return 256 if t % 256 == 0 else t


def _ln_fwd(h, y, w, b):
    t, d = h.shape
    tr = _row_tile(t)

    def body(h_ref, y_ref, w_ref, b_ref, o_ref, o16_ref):
        out = _ln(h_ref[...], y_ref[...], w_ref[...], b_ref[...])
        o_ref[...] = out
        o16_ref[...] = out.astype(bf16)

    row = pl.BlockSpec((tr, d), lambda i: (i, 0))
    vec = pl.BlockSpec((1, d), lambda i: (0, 0))
    return pl.pallas_call(body, grid=(t // tr,), in_specs=[row, row, vec, vec], out_specs=[row, row],
                          out_shape=[jax.ShapeDtypeStruct((t, d), f32), jax.ShapeDtypeStruct((t, d), bf16)],
                          name="ln_fwd")(h, y, w, b)


def _ln_loss(h, y, w, b, tgt):
    t, d = h.shape
    tr = _row_tile(t)

    def body(h_ref, y_ref, w_ref, b_ref, t_ref, g_ref, loss_ref):
        @pl.when(pl.program_id(0) == 0)
        def _():
            loss_ref[...] = jnp.zeros_like(loss_ref)

        err = _ln(h_ref[...], y_ref[...], w_ref[...], b_ref[...]) - t_ref[...]
        g_ref[...] = err * (1.0 / d)
        loss_ref[...] += 0.5 * jnp.sum(jnp.mean(jnp.square(err), axis=-1, keepdims=True), axis=0, keepdims=True)

    row = pl.BlockSpec((tr, d), lambda i: (i, 0))
    vec = pl.BlockSpec((1, d), lambda i: (0, 0))
    return pl.pallas_call(
        body, grid=(t // tr,), in_specs=[row, row, vec, vec, row],
        out_specs=[row, pl.BlockSpec((1, LANES), lambda i: (0, 0))],
        out_shape=[jax.ShapeDtypeStruct((t, d), f32), jax.ShapeDtypeStruct((1, LANES), f32)],
        compiler_params=pltpu.CompilerParams(dimension_semantics=("arbitrary",)), name="ln_loss")(h, y, w, b, tgt)


def _ln_bwd(h, y, w, b, dout):
    t, d = h.shape
    tr = _row_tile(t)

    def body(h_ref, y_ref, w_ref, b_ref, do_ref, dy_ref, dy16_ref, dw_ref, db_ref):
        @pl.when(pl.program_id(0) == 0)
        def _():
            dw_ref[...] = jnp.zeros_like(dw_ref)
            db_ref[...] = jnp.zeros_like(db_ref)

        _, vjp = jax.vjp(lambda yy, ww, bb: _ln(h_ref[...], yy, ww, bb), y_ref[...], w_ref[...], b_ref[...])
        dy, dw, db = vjp(do_ref[...])
        dy_ref[...] = dy
        dy16_ref[...] = dy.astype(bf16)
        dw_ref[...] += dw
        db_ref[...] += db

    row = pl.BlockSpec((tr, d), lambda i: (i, 0))
    vec = pl.BlockSpec((1, d), lambda i: (0, 0))
    return pl.pallas_call(
        body, grid=(t // tr,), in_specs=[row, row, vec, vec, row], out_specs=[row, row, vec, vec],
        out_shape=[jax.ShapeDtypeStruct((t, d), f32), jax.ShapeDtypeStruct((t, d), bf16),
                   jax.ShapeDtypeStruct((1, d), f32), jax.ShapeDtypeStruct((1, d), f32)],
        compiler_params=pltpu.CompilerParams(dimension_semantics=("arbitrary",)), name="ln_bwd")(h, y, w, b, dout)


def _pick(n, prefs):
    for p in prefs:
        if n % p == 0:
            return p
    return n


def _tile(n, want):
    if n <= want:
        return n
    for cand in range(want - want % LANES, 0, -LANES):
        if n % cand == 0:
            return cand
    return n


def _matmul(a, b, mode, name, tiles, add=None, add_scale=1.0, out_dtype=f32):
    if mode == "nn":
        (m, k), n = a.shape, b.shape[1]
    elif mode == "nt":
        (m, k), n = a.shape, b.shape[0]
    else:
        (k, m), n = a.shape, b.shape[1]
    tm, tn, tk = _tile(m, tiles[0]), _tile(n, tiles[1]), _tile(k, tiles[2])
    nk = k // tk
    cache_a = nk == 1 and a.dtype != bf16 and n // tn > 1

    def body(*refs):
        a_ref, b_ref = refs[0], refs[1]
        add_ref = refs[2] if add is not None else None
        n_in = 3 if add is not None else 2
        o_ref = refs[n_in]
        scratch = refs[n_in + 1:]

        def finish(res):
            if add is not None:
                res = res + add_scale * add_ref[...]
            o_ref[...] = res.astype(out_dtype)

        if cache_a:
            a_bf = scratch[0]

            @pl.when(pl.program_id(1) == 0)
            def _():
                a_bf[...] = a_ref[...].astype(bf16)

            a_val = a_bf[...]
        else:
            a_val = a_ref[...].astype(bf16)
        prod = lax.dot_general(a_val, b_ref[...].astype(bf16), _DIMS[mode], preferred_element_type=f32)
        if nk == 1:
            finish(prod)
        else:
            acc = scratch[-1]
            kk = pl.program_id(2)

            @pl.when(kk == 0)
            def _():
                acc[...] = prod

            @pl.when(kk != 0)
            def _():
                acc[...] += prod

            @pl.when(kk == nk - 1)
            def _():
                finish(acc[...])

    a_shape = (tk, tm) if mode == "tn" else (tm, tk)
    a_spec = pl.BlockSpec(a_shape, (lambda i, j, kk: (kk, i)) if mode == "tn" else (lambda i, j, kk: (i, kk)))
    b_spec = pl.BlockSpec((tn, tk), lambda i, j, kk: (j, kk)) if mode == "nt" else pl.BlockSpec((tk, tn), lambda i, j, kk: (kk, j))
    o_spec = pl.BlockSpec((tm, tn), lambda i, j, kk: (i, j))
    in_specs = [a_spec, b_spec] + ([o_spec] if add is not None else [])
    args = [a, b] + ([add] if add is not None else [])
    scratch_shapes = ([pltpu.VMEM(a_shape, bf16)] if cache_a else []) + ([pltpu.VMEM((tm, tn), f32)] if nk > 1 else [])
    return pl.pallas_call(
        body, grid=(m // tm, n // tn, nk), in_specs=in_specs, out_specs=o_spec,
        out_shape=jax.ShapeDtypeStruct((m, n), out_dtype), scratch_shapes=scratch_shapes,
        compiler_params=pltpu.CompilerParams(dimension_semantics=("parallel", "arbitrary", "arbitrary")),
        name=name,
    )(*args)


def _position():
    return lax.axis_index("x"), lax.axis_index("y"), lax.axis_index("c")


def _flip(pos, k):
    x, y, c = pos
    return (1 - x if k & 4 else x, 1 - y if k & 2 else y, 1 - c if k & 1 else c)


def _index(pos):
    return 4 * pos[0] + 2 * pos[1] + pos[2]


def _all_gather_rows(x, name):
    m_per, n = x.shape

    def body(x_ref, out_ref, send_sems, recv_sems, local_sem):
        me = _position()
        sibling = _flip(me, 1)
        chips = (2, 4, 6)

        def rows(pos):
            return out_ref.at[pl.ds(_index(pos) * m_per, m_per), :]

        def copy(sem, block, to, src=None):
            return pltpu.make_async_remote_copy(
                src_ref=rows(block) if src is None else src, dst_ref=rows(block),
                send_sem=send_sems.at[sem], recv_sem=recv_sems.at[sem], device_id=to, device_id_type=MESH)

        mine = pltpu.make_async_copy(x_ref, rows(me), local_sem)
        mine.start()
        first = [copy(0, me, sibling, src=x_ref)]
        first += [copy(1 + j, me, _flip(me, k), src=x_ref) for j, k in enumerate(chips)]
        for cp in first:
            cp.start()
        passed = [copy(4 + j, _flip(me, k), sibling) for j, k in enumerate(chips)]
        for j, k in enumerate(chips):
            copy(1 + j, _flip(me, k), me).wait_recv()
            passed[j].start()
        copy(0, sibling, me).wait_recv()
        for j, k in enumerate(chips):
            copy(4 + j, _flip(sibling, k), me).wait_recv()
        for cp in first + passed:
            cp.wait_send()
        mine.wait()

    return pl.pallas_call(
        body, out_shape=jax.ShapeDtypeStruct((N_DEV * m_per, n), x.dtype),
        in_specs=[pl.BlockSpec(memory_space=pl.ANY)], out_specs=pl.BlockSpec(memory_space=pl.ANY),
        scratch_shapes=[pltpu.SemaphoreType.DMA((7,)), pltpu.SemaphoreType.DMA((7,)), pltpu.SemaphoreType.DMA(())],
        name=name,
    )(x)


def _split_start(srcs, lands, plan, n_copies, name):
    n_arr = len(srcs)
    hbm = pl.BlockSpec(memory_space=pltpu.HBM)
    sem = pl.BlockSpec(memory_space=pltpu.SEMAPHORE)

    def body(*refs):
        src_refs, land_refs = refs[:n_arr], refs[n_arr:2 * n_arr]
        send_sems, recv_sems = refs[2 * n_arr:3 * n_arr], refs[3 * n_arr:4 * n_arr]
        token = refs[-1]
        me = _position()
        for i in range(n_arr):
            for j, (src, dst, peer, _) in enumerate(plan(i, src_refs[i], land_refs[i], me)):
                pltpu.make_async_remote_copy(src_ref=src, dst_ref=dst, send_sem=send_sems[i].at[j], recv_sem=recv_sems[i].at[j],
                                             device_id=peer, device_id_type=MESH).start()
        token[...] = jnp.zeros_like(token)

    outs = pl.pallas_call(
        body, name=name,
        out_shape=([pltpu.SemaphoreType.DMA((n_copies,))] * (2 * n_arr)
                   + [pltpu.HBM(a.shape, a.dtype) for a in list(srcs) + list(lands)]
                   + [jax.ShapeDtypeStruct((8, LANES), f32)]),
        in_specs=[hbm] * (2 * n_arr),
        out_specs=[sem] * (2 * n_arr) + [hbm] * (2 * n_arr) + [pl.BlockSpec(memory_space=pltpu.VMEM)],
        input_output_aliases={i: 2 * n_arr + i for i in range(2 * n_arr)},
        compiler_params=pltpu.CompilerParams(has_side_effects=pltpu.SideEffectType.DATAFLOW_SIDE_EFFECTING),
    )(*[pltpu.with_memory_space_constraint(a, pltpu.HBM) for a in list(srcs) + list(lands)])
    return (outs[:n_arr], outs[n_arr:2 * n_arr], outs[2 * n_arr:3 * n_arr], outs[3 * n_arr:4 * n_arr], outs[-1])


def _split_wait(started, plan, after, name):
    send_sems, recv_sems, srcs, lands, _ = started
    n_arr = len(srcs)
    hbm = pl.BlockSpec(memory_space=pltpu.HBM)
    sem = pl.BlockSpec(memory_space=pltpu.SEMAPHORE)

    def body(*refs):
        src_refs, land_refs = refs[:n_arr], refs[n_arr:2 * n_arr]
        s_sems, r_sems = refs[2 * n_arr:3 * n_arr], refs[3 * n_arr:4 * n_arr]
        me = _position()
        for i in range(n_arr):
            for j, (src, _, peer, arrival) in enumerate(plan(i, src_refs[i], land_refs[i], me)):
                cp = pltpu.make_async_remote_copy(src_ref=src, dst_ref=arrival, send_sem=s_sems[i].at[j], recv_sem=r_sems[i].at[j],
                                                  device_id=peer, device_id_type=MESH)
                cp.wait_send()
                cp.wait_recv()

    outs = pl.pallas_call(
        body, name=name,
        out_shape=[pltpu.HBM(a.shape, a.dtype) for a in list(srcs) + list(lands)],
        in_specs=[hbm] * (2 * n_arr) + [sem] * (2 * n_arr) + [pl.BlockSpec(memory_space=pl.ANY)],
        out_specs=[hbm] * (2 * n_arr),
        input_output_aliases={i: i for i in range(2 * n_arr)},
        compiler_params=pltpu.CompilerParams(has_side_effects=pltpu.SideEffectType.DATAFLOW_SIDE_EFFECTING),
    )(*srcs, *lands, *send_sems, *recv_sems, after)
    return outs[:n_arr], outs[n_arr:]


_GATHER_FLIPS = (1, 2, 4, 6)


def _gather_plan(i, src_ref, land_ref, me):
    m = src_ref.shape[0]

    def rows(pos):
        return land_ref.at[pl.ds(_index(pos) * m, m), :]

    return [(src_ref, rows(me), _flip(me, k), rows(_flip(me, k))) for k in _GATHER_FLIPS]


def _gather_forward(lands, name):
    n_arr = len(lands)
    chips = (2, 4, 6)

    def body(*refs):
        out_refs = refs[n_arr:2 * n_arr]
        send_sems, recv_sems = refs[2 * n_arr:]
        me = _position()
        sibling = _flip(me, 1)
        sends, arrivals = [], []
        for i, out_ref in enumerate(out_refs):
            m = out_ref.shape[0] // N_DEV

            def copy(pos, j):
                blk = out_ref.at[pl.ds(_index(pos) * m, m), :]
                return pltpu.make_async_remote_copy(src_ref=blk, dst_ref=blk, send_sem=send_sems.at[3 * i + j],
                                                    recv_sem=recv_sems.at[3 * i + j], device_id=sibling, device_id_type=MESH)

            for j, k in enumerate(chips):
                sends.append(copy(_flip(me, k), j))
                arrivals.append(copy(_flip(sibling, k), j))
        for cp in sends:
            cp.start()
        for cp in arrivals:
            cp.wait_recv()
        for cp in sends:
            cp.wait_send()

    anyspec = pl.BlockSpec(memory_space=pl.ANY)
    return pl.pallas_call(
        body, out_shape=[jax.ShapeDtypeStruct(a.shape, a.dtype) for a in lands],
        in_specs=[anyspec] * n_arr, out_specs=[anyspec] * n_arr, input_output_aliases={i: i for i in range(n_arr)},
        scratch_shapes=[pltpu.SemaphoreType.DMA((3 * n_arr,))] * 2, name=name,
    )(*lands)


def _chips_plan(i, src_ref, land_ref, me):
    m = src_ref.shape[0] // 4
    plan = []
    for j, k in enumerate((2, 4, 6)):
        peer = _flip(me, k)
        plan.append((src_ref.at[pl.ds((2 * peer[0] + peer[1]) * m, m), :], land_ref.at[j], peer, land_ref.at[j]))
    return plan


def _exchange_siblings(gs, name):
    n_arr = len(gs)

    def body(*refs):
        g_refs, out_refs = refs[:n_arr], refs[n_arr:2 * n_arr]
        send_sems, recv_sems = refs[2 * n_arr:]
        me = _position()
        c = me[2]
        sibling = _flip(me, 1)
        copies = []
        for i, (g_ref, out_ref) in enumerate(zip(g_refs, out_refs)):
            m_per = g_ref.shape[0] // N_DEV
            for q in range(4):
                copies.append(pltpu.make_async_remote_copy(
                    src_ref=g_ref.at[pl.ds((2 * q + 1 - c) * m_per, m_per), :], dst_ref=out_ref.at[q],
                    send_sem=send_sems.at[4 * i + q], recv_sem=recv_sems.at[4 * i + q],
                    device_id=sibling, device_id_type=MESH))
        for cp in copies:
            cp.start()
        for cp in copies:
            cp.wait_recv()
        for cp in copies:
            cp.wait_send()

    anyspec = pl.BlockSpec(memory_space=pl.ANY)
    return pl.pallas_call(
        body, out_shape=[jax.ShapeDtypeStruct((4, g.shape[0] // N_DEV, g.shape[1]), g.dtype) for g in gs],
        in_specs=[anyspec] * n_arr, out_specs=[anyspec] * n_arr,
        scratch_shapes=[pltpu.SemaphoreType.DMA((4 * n_arr,))] * 2, name=name,
    )(*gs)


def _sum_with_sibling(g, recv, name):
    m = g.shape[0] // N_DEV
    n = g.shape[1]
    tr = _pick(m, (208, 128, 64, 32, 16))
    nt = m // tr

    def body(g_ref, r_ref, o_ref):
        c = lax.axis_index("c")
        own = jnp.where(c == 0, g_ref[0, 0].astype(f32), g_ref[0, 1].astype(f32))
        o_ref[...] = (own + r_ref[0].astype(f32)).astype(o_ref.dtype)

    return pl.pallas_call(
        body, grid=(4, nt),
        in_specs=[pl.BlockSpec((1, 2, tr, n), lambda q, i: (q, 0, i, 0)), pl.BlockSpec((1, tr, n), lambda q, i: (q, i, 0))],
        out_specs=pl.BlockSpec((tr, n), lambda q, i: (q * nt + i, 0)),
        out_shape=jax.ShapeDtypeStruct((4 * m, n), bf16), name=name,
    )(g.reshape(4, 2, m, n), recv)


def _sum_with_chips(h, recv, name):
    m = h.shape[0] // 4
    n = h.shape[1]
    tr = _pick(m, (208, 128, 64, 32, 16))

    def body(h_ref, r_ref, o_ref):
        my_q = 2 * lax.axis_index("x") + lax.axis_index("y")
        own = h_ref[0].astype(f32)
        for q in range(1, 4):
            own = jnp.where(my_q == q, h_ref[q].astype(f32), own)
        o_ref[...] = ((own + r_ref[0].astype(f32)) + r_ref[1].astype(f32)) + r_ref[2].astype(f32)

    return pl.pallas_call(
        body, grid=(m // tr,),
        in_specs=[pl.BlockSpec((4, tr, n), lambda i: (0, i, 0)), pl.BlockSpec((3, tr, n), lambda i: (0, i, 0))],
        out_specs=pl.BlockSpec((tr, n), lambda i: (i, 0)), out_shape=jax.ShapeDtypeStruct((m, n), f32), name=name,
    )(h.reshape(4, m, n), recv)


def _sum_slots(parts, name):
    n_slot, m, n = parts.shape
    tr = _pick(m, (208, 128, 64, 32, 16, 8))

    def body(p_ref, o_ref):
        acc = p_ref[0]
        for s in range(1, n_slot):
            acc = acc + p_ref[s]
        o_ref[...] = acc

    return pl.pallas_call(
        body, grid=(m // tr,), in_specs=[pl.BlockSpec((n_slot, tr, n), lambda i: (0, i, 0))],
        out_specs=pl.BlockSpec((tr, n), lambda i: (i, 0)), out_shape=jax.ShapeDtypeStruct((m, n), parts.dtype), name=name,
    )(parts)


def _reduce_scatter_begin(gs, name):
    from_sibling = _exchange_siblings(gs, "rs_d2d_" + name)
    chip_sums = [_sum_with_sibling(g, r, f"rs_sum2_{name}_{i}") for i, (g, r) in enumerate(zip(gs, from_sibling))]
    lands = [lax.empty((3, h.shape[0] // 4, h.shape[1]), h.dtype) for h in chip_sums]
    return _split_start(chip_sums, lands, _chips_plan, 3, "rs_ici_start_" + name)


def _reduce_scatter_end(started, after, name):
    chip_sums, from_chips = _split_wait(started, _chips_plan, after, "rs_ici_wait_" + name)
    return [_sum_with_chips(h, r, f"rs_sum4_{name}_{i}") for i, (h, r) in enumerate(zip(chip_sums, from_chips))]


def _adamw(w, g, m, v, name):
    shape = w.shape
    n = shape[-1]
    r = w.size // n
    w2, g2, m2, v2 = (a.reshape(r, n) for a in (w, g, m, v))
    tr = _pick(r, (256, 128, 64, 32, 16, 8)) if r * n > 65536 else r

    def body(w_ref, g_ref, m_ref, v_ref, d_ref, mo_ref, vo_ref):
        gg = g_ref[...]
        mm = ADAM_B1 * m_ref[...] + (1.0 - ADAM_B1) * gg
        vv = ADAM_B2 * v_ref[...] + (1.0 - ADAM_B2) * jnp.square(gg)
        m_hat = mm / (1.0 - ADAM_B1 ** ADAM_STEP)
        v_hat = vv / (1.0 - ADAM_B2 ** ADAM_STEP)
        d_ref[...] = -ADAM_LR * (m_hat / (jnp.sqrt(v_hat) + ADAM_EPS) + ADAM_WD * w_ref[...])
        mo_ref[...] = mm
        vo_ref[...] = vv

    spec = pl.BlockSpec((tr, n), lambda i: (i, 0))
    outs = pl.pallas_call(
        body, grid=(r // tr,), in_specs=[spec] * 4, out_specs=[spec] * 3,
        out_shape=[jax.ShapeDtypeStruct((r, n), f32)] * 3, name=name,
    )(w2, g2, m2, v2)
    return tuple(o.reshape(shape) for o in outs)


_SMALL = ("shift_mu", "w_decay0", "a0", "k_k", "k_a", "r_k", "ln_x_w", "ln_x_b", "v_mix0", "lb_logits",
          "g_norm_w", "ln_w", "ln_b", "w_decay_up", "a_up", "v_mix_down", "v_mix_up")
_NAMES = ("w_in", "shift_mu", "w_decay0", "w_decay_up", "a0", "a_up", "k_k", "k_a", "r_k", "ln_x_w", "ln_x_b",
          "v_mix0", "v_mix_down", "v_mix_up", "lb_logits", "g_norm_w", "w_out", "ln_w", "ln_b")


def _pad_rows(a, rows, at_end):
    z = jnp.zeros((rows - a.shape[0], a.shape[1]), a.dtype)
    return jnp.concatenate([a, z] if at_end else [z, a], axis=0)


def kernel(x, w_in, shift_mu, w_decay0, w_decay_up, a0, a_up, k_k, k_a, r_k, ln_x_w, ln_x_b, v_mix0, v_mix_down, v_mix_up, lb_logits, g_norm_w, w_out, ln_w, ln_b, loss_target, m_w_in, m_shift_mu, m_w_decay0, m_w_decay_up, m_a0, m_a_up, m_k_k, m_k_a, m_r_k, m_ln_x_w, m_ln_x_b, m_v_mix0, m_v_mix_down, m_v_mix_up, m_lb_logits, m_g_norm_w, m_w_out, m_ln_w, m_ln_b, v_w_in, v_shift_mu, v_w_decay0, v_w_decay_up, v_a0, v_a_up, v_k_k, v_k_a, v_r_k, v_ln_x_w, v_ln_x_b, v_v_mix0, v_v_mix_down, v_v_mix_up, v_lb_logits, v_g_norm_w, v_w_out, v_ln_w, v_ln_b):
    weights = dict(w_in=w_in, shift_mu=shift_mu, w_decay0=w_decay0, w_decay_up=w_decay_up, a0=a0, a_up=a_up, k_k=k_k,
                   k_a=k_a, r_k=r_k, ln_x_w=ln_x_w, ln_x_b=ln_x_b, v_mix0=v_mix0, v_mix_down=v_mix_down,
                   v_mix_up=v_mix_up, lb_logits=lb_logits, g_norm_w=g_norm_w, w_out=w_out, ln_w=ln_w, ln_b=ln_b)
    mom1 = dict(w_in=m_w_in, shift_mu=m_shift_mu, w_decay0=m_w_decay0, w_decay_up=m_w_decay_up, a0=m_a0, a_up=m_a_up,
                k_k=m_k_k, k_a=m_k_a, r_k=m_r_k, ln_x_w=m_ln_x_w, ln_x_b=m_ln_x_b, v_mix0=m_v_mix0,
                v_mix_down=m_v_mix_down, v_mix_up=m_v_mix_up, lb_logits=m_lb_logits, g_norm_w=m_g_norm_w,
                w_out=m_w_out, ln_w=m_ln_w, ln_b=m_ln_b)
    mom2 = dict(w_in=v_w_in, shift_mu=v_shift_mu, w_decay0=v_w_decay0, w_decay_up=v_w_decay_up, a0=v_a0, a_up=v_a_up,
                k_k=v_k_k, k_a=v_k_a, r_k=v_r_k, ln_x_w=v_ln_x_w, ln_x_b=v_ln_x_b, v_mix0=v_v_mix0,
                v_mix_down=v_v_mix_down, v_mix_up=v_v_mix_up, lb_logits=v_lb_logits, g_norm_w=v_g_norm_w,
                w_out=v_w_out, ln_w=v_ln_w, ln_b=v_ln_b)
    assert x.shape[0] == 1 and w_in.shape[0] == DEPTH
    t, d = x.shape[1], x.shape[2]
    dr = w_decay0.shape[1]
    dh = g_norm_w.shape[1]
    rank_w, rank_a, rank_v = w_decay_up.shape[1], a_up.shape[1], v_mix_up.shape[1]
    rwc = 4 * dr + rank_w + rank_a
    assert rank_w + rank_a == LANES and rank_v <= LANES and dr + dh == d
    assert t % CHUNK == 0 and dr % LANES == 0 and dh % LANES == 0 and shift_mu.shape[1] == rwc
    n_pair = dr // LANES
    me = _index(_position())

    win_t = [_all_gather_rows(w_in[0].T.astype(bf16), "ag_w_in_0"), None]
    wout = [None, None]
    late_blocks = [w_out[0].astype(bf16), w_in[1].T.astype(bf16), w_out[1].astype(bf16)]
    late_lands = [lax.dynamic_update_slice(lax.empty((N_DEV * blk.shape[0], blk.shape[1]), bf16), blk, (me * blk.shape[0], 0))
                  for blk in late_blocks]
    late_gather = _split_start(late_blocks, late_lands, _gather_plan, len(_GATHER_FLIPS), "ag_late_start")
    shard = dr // N_DEV
    pack = jnp.concatenate([w_decay_up[0], w_decay_up[1], a_up[0], a_up[1], v_mix_up[0], v_mix_down[0].T], axis=0)
    pack = _all_gather_rows(pack, "ag_small")
    pack = jnp.transpose(pack.reshape(N_DEV, -1, shard), (1, 0, 2)).reshape(-1, dr)
    offs = [0, rank_w, 2 * rank_w, 2 * rank_w + rank_a, 2 * rank_w + 2 * rank_a, 2 * rank_w + 2 * rank_a + rank_v,
            2 * rank_w + 2 * rank_a + 2 * rank_v]
    wdu_f = [pack[offs[0]:offs[1]], pack[offs[1]:offs[2]]]
    aup_f = [pack[offs[2]:offs[3]], pack[offs[3]:offs[4]]]
    vup_f = pack[offs[4]:offs[5]]
    vdown_f = pack[offs[5]:offs[6]].T

    def after_start(a, started):
        return a + started[-1][0:1, 0:1]

    def rwkv_params(l):
        mu = after_start(shift_mu[0:1], late_gather) if l == 0 else shift_mu[l:l + 1]
        prm = [mu, w_decay0[l:l + 1], a0[l:l + 1], _pad_rows(wdu_f[l], LANES, True),
               _pad_rows(aup_f[l], LANES, False)]
        if l == 1:
            prm += [v_mix0[0:1], _pad_rows(vdown_f.T, LANES, True).T, _pad_rows(vup_f, LANES, True)]
        rows = jnp.stack([k_k[l], k_a[l], r_k[l], ln_x_w[l], ln_x_b[l]] + [jnp.zeros((dr,), f32)] * 3, axis=0)
        pp = jnp.transpose(rows.reshape(8, n_pair, LANES), (1, 0, 2))
        return tuple(prm), pp

    h = x[0]
    h16 = h.astype(bf16)
    tgt = loss_target[0]
    saved = []
    vfirst = None
    for l in range(DEPTH):
        prm, pp = rwkv_params(l)
        proj = _matmul(h16, win_t[l], "nt", f"mm_proj_{l}", (2048, 640, 2048))
        if l == 0:
            cat, vfirst, mck = _rwkv_fwd(False, proj, None, prm, pp, d)
        else:
            cat, mck = _rwkv_fwd(True, proj, vfirst, prm, pp, d)
        cat, sck = _hgrn_fwd(l == 1, proj, lb_logits, g_norm_w[l:l + 1], cat, rwc)
        if l == 0:
            _, arrived = _split_wait(late_gather, _gather_plan, cat, "ag_late_wait")
            wout[0], win_t[1], wout[1] = _gather_forward(arrived, "ag_late_forward")
        y = _matmul(cat, wout[l], "nn", f"mm_out_{l}", (1024, 1024, 2048))
        saved.append((h, h16, proj, prm, pp, mck, sck, cat, y))
        if l < DEPTH - 1:
            h, h16 = _ln_fwd(h, y, ln_w[l:l + 1], ln_b[l:l + 1])
        else:
            dh_out, loss_part = _ln_loss(h, y, ln_w[l:l + 1], ln_b[l:l + 1], tgt)
    loss = lax.psum(loss_part[0, 0], ("x", "y", "c"))

    grads = {}
    big = {}
    dvfirst = None
    d_lbl = None
    rs_started = {}
    for l in reversed(range(DEPTH)):
        h_l, h16_l, proj, prm, pp, mck, sck, cat, y = saved[l]
        ln_w_l = ln_w[l:l + 1] if l == DEPTH - 1 else after_start(ln_w[l:l + 1], rs_started[l + 1])
        dy, dy16, g_ln_w, g_ln_b = _ln_bwd(h_l, y, ln_w_l, ln_b[l:l + 1], dh_out)
        dcat = _matmul(dy16, wout[l], "nt", f"mm_dcat_{l}", (1024, 1024, 2048))
        big[("w_out", l)] = _matmul(cat, dy16, "tn", f"mm_dwout_{l}", (512, 2048, 2048), out_dtype=bf16)
        if l == 1:
            outs = _rwkv_bwd(True, proj, vfirst, prm, pp, mck, dcat, None)
            dproj_r, dvfirst = outs[0], outs[1]
            dprm, dpp = outs[2:-1], outs[-1]
        else:
            outs = _rwkv_bwd(False, proj, None, prm, pp, mck, dcat, dvfirst)
            dproj_r = outs[0]
            dprm, dpp = outs[1:-1], outs[-1]
        dproj_h, dlbl_l, dgnw = _hgrn_bwd(l == 1, proj, lb_logits, g_norm_w[l:l + 1], sck, dcat, rwc)
        dproj = jnp.concatenate([dproj_r] + [dproj_h[i] for i in range(4)], axis=1)
        big[("w_in", l)] = _matmul(dproj, h16_l, "tn", f"mm_dwin_{l}", (640, 2048, 2048), out_dtype=bf16)
        rs_started[l] = _reduce_scatter_begin([big[("w_in", l)], big[("w_out", l)]], f"l{l}")
        dy_res = after_start(dy, rs_started[l]) if l == 0 else dy
        dh_out = _matmul(dproj, win_t[l], "nn", f"mm_dh_{l}", (1024, 1024, 1664), add=dy_res, add_scale=ALPHA)
        dpp = jnp.transpose(dpp, (1, 0, 2)).reshape(8, dr)
        grads[l] = dict(shift_mu=dprm[0][0], w_decay0=dprm[1][0], a0=dprm[2][0], w_decay_up=dprm[3][:rank_w],
                        a_up=dprm[4][rank_w:], k_k=dpp[0], k_a=dpp[1], r_k=dpp[2], ln_x_w=dpp[3], ln_x_b=dpp[4],
                        g_norm_w=dgnw[0], ln_w=g_ln_w[0], ln_b=g_ln_b[0])
        if l == 1:
            grads[l].update(v_mix0=dprm[5][0], v_mix_down=dprm[6][:, :rank_v], v_mix_up=dprm[7][:rank_v])
            d_lbl = dlbl_l
    grad_x = dh_out[None]

    def both(name):
        return jnp.stack([grads[0][name], grads[1][name]])

    small = dict(shift_mu=both("shift_mu"), w_decay0=both("w_decay0"), a0=both("a0"), k_k=both("k_k"), k_a=both("k_a"),
                 r_k=both("r_k"), ln_x_w=both("ln_x_w"), ln_x_b=both("ln_x_b"), v_mix0=grads[1]["v_mix0"][None],
                 lb_logits=d_lbl, g_norm_w=both("g_norm_w"), ln_w=both("ln_w"), ln_b=both("ln_b"),
                 w_decay_up=both("w_decay_up"), a_up=both("a_up"), v_mix_down=grads[1]["v_mix_down"][None],
                 v_mix_up=grads[1]["v_mix_up"][None])
    flat = jnp.concatenate([small[nm].reshape(-1) for nm in _SMALL])
    n_flat = flat.shape[0]
    rows = -(-n_flat // (8 * LANES)) * 8
    flat = jnp.concatenate([flat, jnp.zeros((rows * LANES - n_flat,), f32)]).reshape(rows, LANES)
    total = _sum_slots(_all_gather_rows(flat, "ag_small_grads").reshape(N_DEV, rows, LANES), "sum_small_grads").reshape(-1)
    gsm = {}
    off = 0
    for nm in _SMALL:
        size = small[nm].size
        gsm[nm] = total[off:off + size].reshape(small[nm].shape)
        off += size
    gsm["w_decay_up"] = lax.dynamic_slice_in_dim(gsm["w_decay_up"], me * shard, shard, axis=2)
    gsm["a_up"] = lax.dynamic_slice_in_dim(gsm["a_up"], me * shard, shard, axis=2)
    gsm["v_mix_up"] = lax.dynamic_slice_in_dim(gsm["v_mix_up"], me * shard, shard, axis=2)
    gsm["v_mix_down"] = lax.dynamic_slice_in_dim(gsm["v_mix_down"], me * shard, shard, axis=1)
    reduced = {1: _reduce_scatter_end(rs_started[1], dh_out, "l1")}
    reduced[0] = _reduce_scatter_end(rs_started[0], total, "l0")
    gsm["w_in"] = jnp.stack([reduced[l][0].T for l in range(DEPTH)])
    gsm["w_out"] = jnp.stack([reduced[l][1] for l in range(DEPTH)])

    deltas, new_m, new_v = {}, {}, {}
    for nm in _NAMES:
        deltas[nm], new_m[nm], new_v[nm] = _adamw(weights[nm], gsm[nm], mom1[nm], mom2[nm], "adamw_" + nm)
    return (loss, grad_x, *[gsm[nm] for nm in _NAMES], *[deltas[nm] for nm in _NAMES],
            *[new_m[nm] for nm in _NAMES], *[new_v[nm] for nm in _NAMES])
```

```python
import functools

import jax
import jax.numpy as jnp
from jax import lax
from jax.experimental import pallas as pl
from jax.experimental.pallas import tpu as pltpu

f32 = jnp.float32
bf16 = jnp.bfloat16

N_DEV = 8
CHUNK = 64
LANES = 128
RWKV_HEAD = 64
DEPTH = 2
ALPHA = (2 * DEPTH) ** 0.25
LN_EPS = 1e-5
GN_EPS = 64e-5
RMS_EPS = 1e-5
LB_FLOOR = 1e-30
ADAM_LR, ADAM_B1, ADAM_B2, ADAM_EPS, ADAM_WD, ADAM_STEP = 0.001, 0.9, 0.999, 1e-08, 0.01, 10
MESH = pl.DeviceIdType.MESH


def _iota(shape, d):
    return lax.broadcasted_iota(jnp.int32, shape, d)


_DIMS = {"nn": (((1,), (0,)), ((), ())), "nt": (((1,), (1,)), ((), ())), "tn": (((0,), (0,)), ((), ()))}
_BATCH_DIMS = {"nn": (((2,), (1,)), ((0,), (0,))), "nt": (((2,), (2,)), ((0,), (0,))), "tn": (((1,), (1,)), ((0,), (0,)))}
_K_AXES = {"nn": (-1, -2), "nt": (-1, -1), "tn": (-2, -2)}


def _mxu(a, b, mode):
    return lax.dot_general(a, b, (_BATCH_DIMS if a.ndim == 3 else _DIMS)[mode], preferred_element_type=f32)


def _split(x):
    hi = x.astype(bf16)
    return hi, (x - hi.astype(f32)).astype(bf16)


def _mm2_impl(a, b, mode, passes=3):
    ah, al = _split(a)
    if passes == 3:
        bh, bl = _split(b)
        lhs, rhs = [ah, ah, al], [bh, bl, bh]
    else:
        bh = b.astype(bf16)
        lhs, rhs = [ah, al], [bh, bh]
    ka, kb = _K_AXES[mode]
    k = a.shape[ka]
    if k % (LANES if -1 in (ka, kb) else 16) == 0:
        return _mxu(jnp.concatenate(lhs, axis=ka), jnp.concatenate(rhs, axis=kb), mode)
    out = _mxu(lhs[0], rhs[0], mode)
    for x, y in zip(lhs[1:], rhs[1:]):
        out = out + _mxu(x, y, mode)
    return out


@functools.partial(jax.custom_vjp, nondiff_argnums=(2, 3))
def _mm2(a, b, mode, passes=3):
    return _mm2_impl(a, b, mode, passes)


def _mm2_fwd(a, b, mode, passes):
    return _mm2_impl(a, b, mode, passes), (a, b)


def _mm2_bwd(mode, passes, res, g):
    a, b = res
    if mode == "nn":
        return _mm2_impl(g, b, "nt", passes), _mm2_impl(a, g, "tn", passes)
    if mode == "nt":
        return _mm2_impl(g, b, "nn", passes), _mm2_impl(g, a, "tn", passes)
    return _mm2_impl(b, g, "nt", passes), _mm2_impl(a, g, "nn", passes)


_mm2.defvjp(_mm2_fwd, _mm2_bwd)

TRI_PASSES = 2
APPLY_PASSES = 2


def _const_impl(cm, x, mode):
    hi, lo = _split(x)
    if mode in ("r", "rt"):
        shape = x.shape
        hi, lo = hi.reshape(-1, shape[-1]), lo.reshape(-1, shape[-1])
        dims = "nn" if mode == "r" else "nt"
        out = _mxu(hi, cm, dims) + _mxu(lo, cm, dims)
        return out.reshape(shape[:-1] + (out.shape[-1],))
    if x.ndim == 3:
        cm = jnp.broadcast_to(cm, (x.shape[0],) + cm.shape)
    return _mxu(cm, hi, mode) + _mxu(cm, lo, mode)


@jax.custom_vjp
def _const_left(cm, x):
    return _const_impl(cm, x, "nn")


_const_left.defvjp(lambda cm, x: (_const_impl(cm, x, "nn"), cm),
                   lambda cm, g: (jnp.zeros_like(cm), _const_impl(cm, g, "tn")))


@jax.custom_vjp
def _const_right(x, cm):
    return _const_impl(cm, x, "r")


_const_right.defvjp(lambda x, cm: (_const_impl(cm, x, "r"), cm),
                    lambda cm, g: (_const_impl(cm, g, "rt"), jnp.zeros_like(cm)))


def _tri_inv(a):
    n = a.shape[-1]
    tm = (_iota((n, n), 0) == _iota((n, n), 1)).astype(f32) + a
    ak = a
    for _ in range(5):
        ak = _mm2_impl(ak, ak, "nn", TRI_PASSES)
        tm = tm + _mm2_impl(tm, ak, "nn", TRI_PASSES)
    return tm


@jax.custom_vjp
def _tri_solve(a, x):
    return _mm2_impl(_tri_inv(a), x, "nn")


def _tri_solve_fwd(a, x):
    tm = _tri_inv(a)
    u = _mm2_impl(tm, x, "nn")
    return u, (tm, u)


def _tri_solve_bwd(res, du):
    tm, u = res
    dx = _mm2_impl(tm, du, "tn")
    return _mm2_impl(dx, u, "nt"), dx


_tri_solve.defvjp(_tri_solve_fwd, _tri_solve_bwd)


def _col_of_row(row_vec):
    n = row_vec.shape[-1]
    eye = _iota((n, n), 0) == _iota((n, n), 1)
    return jnp.sum(jnp.where(eye, jnp.broadcast_to(row_vec, row_vec.shape[:-2] + (n, n)), 0.0), axis=-1, keepdims=True)


def _softplus(x):
    return jnp.maximum(x, 0.0) + jnp.log1p(jnp.exp(-jnp.abs(x)))


def _log_sigmoid(x):
    return -_softplus(-x)


def _logaddexp(a, b):
    return jnp.maximum(a, b) + jnp.log1p(jnp.exp(-jnp.abs(a - b)))


def _silu(x):
    return x * jax.nn.sigmoid(x)


def _tril(c, strict):
    r, s = _iota((c, c), 0), _iota((c, c), 1)
    return (r > s) if strict else (r >= s)


def _last_row(a):
    c = a.shape[-2]
    return jnp.sum(jnp.where(_iota(a.shape, a.ndim - 2) == c - 1, a, 0.0), axis=-2, keepdims=True)


def _rwkv_pre(layer1, prm, y, prev, vf):
    c = y.shape[0]
    if layer1:
        mu, w0, a0, wup, aup, v0, vdown, vup = prm
    else:
        mu, w0, a0, wup, aup = prm
    dr = w0.shape[1]
    shift = (_iota((c, c), 0) == _iota((c, c), 1) + 1).astype(bf16)
    y_prev = _const_left(shift, y) + jnp.where(_iota((c, 1), 0) == 0, prev, 0.0)
    rw = y + mu * (y_prev - y)
    r, k, v, z = (rw[:, i * dr:(i + 1) * dr] for i in range(4))
    wdad = rw[:, 4 * dr:4 * dr + LANES]
    w_raw = w0 + _mm2(jnp.tanh(wdad), wup, "nn")
    lw = -jnp.exp(-_softplus(-w_raw) - 0.5)
    asig = jax.nn.sigmoid(a0 + _mm2(wdad, aup, "nn"))
    if layer1:
        v = v + (vf - v) * jax.nn.sigmoid(v0 + _mm2(_mm2(v, vdown, "nn"), vup, "nn"))
    return r, k, v, z, lw, asig


def _rwkv_pair(pp, m0, xs):
    kkw, kaw, rkw, gnw, gnb = pp
    r, k, v, z, lw, asig = xs
    c = r.shape[-2]
    n2 = 2 * c
    lane = _iota((1, LANES), 1)
    mh0, mh1 = (lane < RWKV_HEAD).astype(f32), (lane >= RWKV_HEAD).astype(f32)
    same_head = _iota((LANES, LANES), 0) // RWKV_HEAD == _iota((LANES, LANES), 1) // RWKV_HEAD
    g = same_head.astype(bf16)

    def seg(x):
        return _const_right(x, g)

    def stack(x):
        return jnp.concatenate([x * mh0, x * mh1], axis=-2)

    kk = k * kkw
    kk = kk / jnp.maximum(jnp.sqrt(seg(kk * kk)), 1e-12)
    k2 = k * (1.0 + (asig - 1.0) * kaw)
    a = -kk
    b = kk * asig
    cum = _const_left(_tril(c, False).astype(bf16), lw)
    at = stack(a * jnp.exp(cum - lw))
    rt = stack(r * jnp.exp(cum))
    en = jnp.exp(-cum)
    sc = _mm2(jnp.concatenate([at, rt], axis=-2), jnp.concatenate([stack(b * en), stack(k2 * en)], axis=-2), "nt")
    row, col = _iota((n2, n2), 0), _iota((n2, n2), 1)
    same = row // c == col // c
    strict = same & (row % c > col % c)
    incl = same & (row % c >= col % c)
    aab = jnp.where(strict, sc[..., :n2, :n2], 0.0)
    aak = jnp.where(strict, sc[..., :n2, n2:], 0.0)
    arb = jnp.where(incl, sc[..., n2:, :n2], 0.0)
    ark = jnp.where(incl, sc[..., n2:, n2:], 0.0)
    vv = jnp.concatenate([v, v], axis=-2)
    mask_st = jnp.concatenate([jnp.broadcast_to(mh0, (c, LANES)), jnp.broadcast_to(mh1, (c, LANES))], axis=0)
    x_st = _mm2(jnp.concatenate([at, aak], axis=-1), jnp.concatenate([m0, vv], axis=-2), "nn", APPLY_PASSES)
    u_st = _tri_solve(aab, x_st) * mask_st
    o_st = _mm2(jnp.concatenate([rt, arb, ark], axis=-1), jnp.concatenate([m0, u_st, vv], axis=-2), "nn", APPLY_PASSES) * mask_st
    u = u_st[..., :c, :] + u_st[..., c:, :]
    o = o_st[..., :c, :] + o_st[..., c:, :]
    cum_last = _last_row(cum)
    dec_end = jnp.exp(cum_last - cum)
    m_new = _col_of_row(jnp.exp(cum_last)) * m0 + _mm2(
        jnp.concatenate([b * dec_end, k2 * dec_end], axis=-2), jnp.concatenate([u, v], axis=-2), "tn", APPLY_PASSES) * same_head.astype(f32)
    mean = seg(o) * (1.0 / RWKV_HEAD)
    d = o - mean
    var = seg(d * d) * (1.0 / RWKV_HEAD)
    on = d * lax.rsqrt(var + GN_EPS) * gnw + gnb
    bonus = seg(r * k2 * rkw) * v
    return (on + bonus) * _silu(z), m_new


def _split_lanes(a, n):
    return [a[:, i * LANES:(i + 1) * LANES] for i in range(n)]


def _group(n):
    return 8 if n % 8 == 0 else (4 if n % 4 == 0 else (2 if n % 2 == 0 else 1))


def _rwkv_specs(layer1, t, dr, rwc, n_pair, rev):
    nc = t // CHUNK
    grp = _group(n_pair)

    def cidx(c):
        return (nc - 1 - c) if rev else c

    full = lambda shape: pl.BlockSpec(shape, lambda c, p: tuple(0 for _ in shape))
    specs = [
        pl.BlockSpec((CHUNK, rwc), lambda c, p: (cidx(c), 0)),
        pl.BlockSpec((8, rwc), lambda c, p: (jnp.maximum(cidx(c) * (CHUNK // 8) - 1, 0), 0)),
    ]
    if layer1:
        specs.append(pl.BlockSpec((CHUNK, dr), lambda c, p: (cidx(c), 0)))
    prm_shapes = [(1, rwc), (1, dr), (1, dr), (LANES, dr), (LANES, dr)]
    if layer1:
        prm_shapes += [(1, dr), (dr, LANES), (LANES, dr)]
    specs += [full(s) for s in prm_shapes]
    specs.append(pl.BlockSpec((grp, 8, LANES), lambda c, p: (p, 0, 0)))
    return specs, prm_shapes, cidx, full


def _rwkv_fwd(layer1, proj, vf, prm, pp, cat_width):
    t = proj.shape[0]
    dr = prm[1].shape[1]
    rwc = prm[0].shape[1]
    n_pair = dr // LANES
    nc = t // CHUNK
    n_prm = len(prm)
    specs, _, _, _ = _rwkv_specs(layer1, t, dr, rwc, n_pair, False)

    def body(*refs):
        y_ref, prev_ref = refs[0], refs[1]
        i = 2
        vf_ref = None
        if layer1:
            vf_ref = refs[i]
            i += 1
        prm_refs = refs[i:i + n_prm]
        i += n_prm
        pp_ref = refs[i]
        i += 1
        cat_ref = refs[i]
        i += 1
        vout_ref = None
        if not layer1:
            vout_ref = refs[i]
            i += 1
        mck_ref, x_s, m_s = refs[i], refs[i + 1], refs[i + 2]
        c, p = pl.program_id(0), pl.program_id(1)

        @pl.when((c == 0) & (p == 0))
        def _():
            m_s[...] = jnp.zeros_like(m_s)

        @pl.when(p == 0)
        def _():
            prev = prev_ref[pl.ds(7, 1), :] * (c != 0).astype(f32)
            xs = _rwkv_pre(layer1, tuple(r[...] for r in prm_refs), y_ref[...], prev,
                           vf_ref[...] if layer1 else None)
            for q, a in enumerate(xs):
                for j, piece in enumerate(_split_lanes(a, n_pair)):
                    x_s[q * n_pair + j] = piece
            if not layer1:
                vout_ref[...] = xs[2]

        m0 = m_s[pl.ds(p * grp, grp)]
        mck_ref[0] = m0
        ppv = tuple(pp_ref[:, pl.ds(q, 1), :] for q in range(5))
        og, m_new = _rwkv_pair(ppv, m0, tuple(x_s[pl.ds(q * n_pair + p * grp, grp)] for q in range(6)))
        for j in range(grp):
            cat_ref[:, j * LANES:(j + 1) * LANES] = og[j]
        m_s[pl.ds(p * grp, grp)] = m_new

    grp = _group(n_pair)
    out_shape = [jax.ShapeDtypeStruct((t, cat_width), f32)]
    out_specs = [pl.BlockSpec((CHUNK, grp * LANES), lambda c, p: (c, p))]
    if not layer1:
        out_shape.append(jax.ShapeDtypeStruct((t, dr), f32))
        out_specs.append(pl.BlockSpec((CHUNK, dr), lambda c, p: (c, 0)))
    out_shape.append(jax.ShapeDtypeStruct((nc, n_pair, LANES, LANES), f32))
    out_specs.append(pl.BlockSpec((1, grp, LANES, LANES), lambda c, p: (c, p, 0, 0)))
    args = [proj, proj] + ([vf] if layer1 else []) + list(prm) + [pp]
    return pl.pallas_call(
        body, grid=(nc, n_pair // grp), in_specs=specs, out_specs=out_specs, out_shape=out_shape,
        scratch_shapes=[pltpu.VMEM((6 * n_pair, CHUNK, LANES), f32), pltpu.VMEM((n_pair, LANES, LANES), f32)],
        compiler_params=pltpu.CompilerParams(dimension_semantics=("arbitrary", "arbitrary")),
        name=f"rwkv_fwd_l{int(layer1)}",
    )(*args)


def _rwkv_bwd(layer1, proj, vf, prm, pp, mck, dcat, dvout):
    t = proj.shape[0]
    dr = prm[1].shape[1]
    rwc = prm[0].shape[1]
    n_pair = dr // LANES
    nc = t // CHUNK
    n_prm = len(prm)
    specs, prm_shapes, cidx, full = _rwkv_specs(layer1, t, dr, rwc, n_pair, True)
    grp = _group(n_pair)
    n_step = n_pair // grp
    specs.append(pl.BlockSpec((1, grp, LANES, LANES), lambda c, p: (cidx(c), p, 0, 0)))
    specs.append(pl.BlockSpec((CHUNK, grp * LANES), lambda c, p: (cidx(c), p)))
    if not layer1:
        specs.append(pl.BlockSpec((CHUNK, dr), lambda c, p: (cidx(c), 0)))

    def body(*refs):
        y_ref, prev_ref = refs[0], refs[1]
        i = 2
        vf_ref = None
        if layer1:
            vf_ref = refs[i]
            i += 1
        prm_refs = refs[i:i + n_prm]
        i += n_prm
        pp_ref, mck_ref, dog_ref = refs[i], refs[i + 1], refs[i + 2]
        i += 3
        dvout_ref = None
        if not layer1:
            dvout_ref = refs[i]
            i += 1
        dy_ref = refs[i]
        i += 1
        dvf_ref = None
        if layer1:
            dvf_ref = refs[i]
            i += 1
        dprm_refs = refs[i:i + n_prm]
        i += n_prm
        dpp_ref = refs[i]
        x_s, dx_s, dm_s, dprev_s = refs[i + 1:i + 5]
        c, p = pl.program_id(0), pl.program_id(1)
        cr = nc - 1 - c

        def prev_row():
            return prev_ref[pl.ds(7, 1), :] * (cr != 0).astype(f32)

        @pl.when((c == 0) & (p == 0))
        def _():
            dm_s[...] = jnp.zeros_like(dm_s)
            dprev_s[...] = jnp.zeros_like(dprev_s)
            dpp_ref[...] = jnp.zeros_like(dpp_ref)
            for r in dprm_refs:
                r[...] = jnp.zeros_like(r)

        @pl.when(p == 0)
        def _():
            xs = _rwkv_pre(layer1, tuple(r[...] for r in prm_refs), y_ref[...], prev_row(),
                           vf_ref[...] if layer1 else None)
            for q, a in enumerate(xs):
                for j, piece in enumerate(_split_lanes(a, n_pair)):
                    x_s[q * n_pair + j] = piece

        units = pl.ds(p * grp, grp)
        ppv = tuple(pp_ref[:, pl.ds(q, 1), :] for q in range(5))
        xs_p = tuple(x_s[pl.ds(q * n_pair + p * grp, grp)] for q in range(6))
        dog = jnp.stack([dog_ref[:, j * LANES:(j + 1) * LANES] for j in range(grp)], axis=0)
        _, vjp_pair = jax.vjp(_rwkv_pair, ppv, mck_ref[0], xs_p)
        dppv, dm0, dxs = vjp_pair((dog, dm_s[units]))
        dm_s[units] = dm0
        for q in range(6):
            dx_s[pl.ds(q * n_pair + p * grp, grp)] = dxs[q]
        for q in range(5):
            dpp_ref[units, pl.ds(q, 1), :] += dppv[q]

        @pl.when(p == n_step - 1)
        def _():
            dxs_full = [jnp.concatenate([dx_s[q * n_pair + j] for j in range(n_pair)], axis=1) for q in range(6)]
            if not layer1:
                dxs_full[2] = dxs_full[2] + dvout_ref[...]
            prm_v = tuple(r[...] for r in prm_refs)
            if layer1:
                _, vjp_pre = jax.vjp(functools.partial(_rwkv_pre, True), prm_v, y_ref[...], prev_row(), vf_ref[...])
                dprm, dy, dprev, dvf = vjp_pre(tuple(dxs_full))
                dvf_ref[...] = dvf
            else:
                _, vjp_pre = jax.vjp(lambda a, b, d: _rwkv_pre(False, a, b, d, None), prm_v, y_ref[...], prev_row())
                dprm, dy, dprev = vjp_pre(tuple(dxs_full))
            dy_ref[...] = (dy + jnp.where(_iota((CHUNK, 1), 0) == CHUNK - 1, dprev_s[...], 0.0)).astype(bf16)
            dprev_s[...] = dprev
            for r, gval in zip(dprm_refs, dprm):
                r[...] += gval

    out_shape = [jax.ShapeDtypeStruct((t, rwc), bf16)]
    out_specs = [pl.BlockSpec((CHUNK, rwc), lambda c, p: (cidx(c), 0))]
    if layer1:
        out_shape.append(jax.ShapeDtypeStruct((t, dr), f32))
        out_specs.append(pl.BlockSpec((CHUNK, dr), lambda c, p: (cidx(c), 0)))
    out_shape += [jax.ShapeDtypeStruct(s, f32) for s in prm_shapes]
    out_specs += [full(s) for s in prm_shapes]
    out_shape.append(jax.ShapeDtypeStruct((n_pair, 8, LANES), f32))
    out_specs.append(full((n_pair, 8, LANES)))
    args = [proj, proj] + ([vf] if layer1 else []) + list(prm) + [pp, mck, dcat] + ([] if layer1 else [dvout])
    return pl.pallas_call(
        body, grid=(nc, n_step), in_specs=specs, out_specs=out_specs, out_shape=out_shape,
        scratch_shapes=[pltpu.VMEM((6 * n_pair, CHUNK, LANES), f32), pltpu.VMEM((6 * n_pair, CHUNK, LANES), f32),
                        pltpu.VMEM((n_pair, LANES, LANES), f32), pltpu.VMEM((1, rwc), f32)],
        compiler_params=pltpu.CompilerParams(dimension_semantics=("arbitrary", "arbitrary")),
        name=f"rwkv_bwd_l{int(layer1)}",
    )(*args)


def _hgrn_chunk(layer1, lbl, gnw, s0, q_raw, f_raw, i_in, z):
    c = q_raw.shape[-2]
    q = _silu(q_raw)
    ls = _log_sigmoid(f_raw)
    if layer1:
        l0, l1 = lbl[..., 0:1, :], lbl[..., 1:2, :]
        mx = jnp.maximum(l0, l1)
        e0, e1 = jnp.exp(l0 - mx), jnp.exp(l1 - mx)
        sm0, sm1 = e0 / (e0 + e1), e1 / (e0 + e1)
        lb = (sm0 + sm1) - sm0
        log_f = _logaddexp(jnp.log(jnp.maximum(lb, LB_FLOOR)), jnp.log1p(-lb) + ls)
        k = (1.0 - lb) * jax.nn.sigmoid(-f_raw)
    else:
        log_f = _logaddexp(jnp.full_like(ls, jnp.log(jnp.float32(LB_FLOOR))), ls)
        k = jax.nn.sigmoid(-f_raw)
    row, col = _iota((c, c), 0), _iota((c, c), 1)
    trow = _iota((c, 1), 0)
    halves = []
    half = c // 2
    while half >= 1:
        halves.append(half)
        half //= 2
    cmat = jnp.concatenate([(col <= row).astype(f32)]
                           + [(col <= (row // (2 * hf)) * (2 * hf) + hf - 1).astype(f32) for hf in halves], axis=0)
    ball = _const_left(cmat.astype(bf16), log_f)
    b = ball[..., :c, :]
    att = None
    for lvl, hf in enumerate(halves):
        blk = 2 * hf
        bref = ball[..., (lvl + 1) * c:(lvl + 2) * c, :]
        upper = (trow % blk) >= hf
        qh = q * jnp.exp(jnp.where(upper, b - bref, 0.0)) * upper.astype(f32)
        kh = k * jnp.exp(jnp.where(upper, 0.0, bref - b)) * (1.0 - upper.astype(f32))
        term = jnp.where(row // blk == col // blk, _mm2(qh, kh, "nt", APPLY_PASSES), 0.0)
        att = term if att is None else att + term
    lhs = jnp.concatenate([q * jnp.exp(b), att, jnp.zeros(att.shape[:-1] + (LANES - c,), f32)], axis=-1)
    rhs = jnp.concatenate([s0, i_in, jnp.zeros(i_in.shape[:-2] + (LANES - c, i_in.shape[-1]), f32)], axis=-2)
    o = _mm2(lhs, rhs, "nn", APPLY_PASSES) + jnp.sum(q * k, axis=-1, keepdims=True) * i_in
    b_last = _last_row(b)
    s_new = _col_of_row(jnp.exp(b_last)) * s0 + _mm2(k * jnp.exp(b_last - b), i_in, "tn", APPLY_PASSES)
    o = o * lax.rsqrt(jnp.mean(o * o, axis=-1, keepdims=True) + RMS_EPS)
    return o * gnw * _silu(z), s_new


def _hgrn_in_specs(t, dh, col0, rev):
    nc = t // CHUNK
    nh = dh // LANES

    def cidx(c):
        return (nc - 1 - c) if rev else c

    grp = _group(nh)
    specs = [pl.BlockSpec((CHUNK, LANES), functools.partial(lambda g, j, h, c: (cidx(c), col0 + g * nh + h * grp + j), g, j))
             for j in range(grp) for g in range(4)]
    specs.append(pl.BlockSpec((2, grp * LANES), lambda h, c: (0, h)))
    specs.append(pl.BlockSpec((1, grp * LANES), lambda h, c: (0, h)))
    return specs, cidx, grp


def _hgrn_fwd(layer1, proj, lbl, gnw, cat, rwc):
    t, d = cat.shape
    dh = gnw.shape[1]
    nh = dh // LANES
    nc = t // CHUNK
    col0 = rwc // LANES
    specs, _, grp = _hgrn_in_specs(t, dh, col0, False)
    specs.append(pl.BlockSpec(memory_space=pl.ANY))
    assert (d - dh) % (grp * LANES) == 0
    cat_col0 = (d - dh) // (grp * LANES)

    def body(*refs):
        x_refs = refs[:4 * grp]
        lbl_ref, gnw_ref, _, cat_ref, sck_ref, s_s = refs[4 * grp:]
        c = pl.program_id(1)

        @pl.when(c == 0)
        def _():
            s_s[...] = jnp.zeros_like(s_s)

        lanes = [slice(j * LANES, (j + 1) * LANES) for j in range(grp)]
        s0 = s_s[...]
        sck_ref[:, 0] = s0
        out, s_new = _hgrn_chunk(layer1, jnp.stack([lbl_ref[:, ln] for ln in lanes]), jnp.stack([gnw_ref[:, ln] for ln in lanes]),
                                 s0, *(jnp.stack([x_refs[4 * j + g][...] for j in range(grp)]) for g in range(4)))
        for j in range(grp):
            cat_ref[:, lanes[j]] = out[j]
        s_s[...] = s_new

    return pl.pallas_call(
        body, grid=(nh // grp, nc), in_specs=specs,
        out_specs=[pl.BlockSpec((CHUNK, grp * LANES), lambda h, c: (c, cat_col0 + h)),
                   pl.BlockSpec((grp, 1, LANES, LANES), lambda h, c: (h, c, 0, 0))],
        out_shape=[jax.ShapeDtypeStruct((t, d), f32), jax.ShapeDtypeStruct((nh, nc, LANES, LANES), f32)],
        scratch_shapes=[pltpu.VMEM((grp, LANES, LANES), f32)],
        input_output_aliases={4 * grp + 2: 0},
        compiler_params=pltpu.CompilerParams(dimension_semantics=("arbitrary", "arbitrary")),
        name=f"hgrn_fwd_l{int(layer1)}",
    )(*([proj] * (4 * grp)), lbl, gnw, cat)


def _hgrn_bwd(layer1, proj, lbl, gnw, sck, dcat, rwc):
    t, d = dcat.shape
    dh = gnw.shape[1]
    nh = dh // LANES
    nc = t // CHUNK
    col0 = rwc // LANES
    specs, cidx, grp = _hgrn_in_specs(t, dh, col0, True)
    assert (d - dh) % (grp * LANES) == 0
    cat_col0 = (d - dh) // (grp * LANES)
    specs.append(pl.BlockSpec((grp, 1, LANES, LANES), lambda h, c: (h, cidx(c), 0, 0)))
    specs.append(pl.BlockSpec((CHUNK, grp * LANES), lambda h, c: (cidx(c), cat_col0 + h)))

    def body(*refs):
        x_refs = refs[:4 * grp]
        lbl_ref, gnw_ref, sck_ref, do_ref, dp_ref, dlbl_ref, dgnw_ref, ds_s = refs[4 * grp:]
        c = pl.program_id(1)

        @pl.when(c == 0)
        def _():
            ds_s[...] = jnp.zeros_like(ds_s)
            dlbl_ref[...] = jnp.zeros_like(dlbl_ref)
            dgnw_ref[...] = jnp.zeros_like(dgnw_ref)

        lanes = [slice(j * LANES, (j + 1) * LANES) for j in range(grp)]
        _, vjp = jax.vjp(functools.partial(_hgrn_chunk, layer1),
                         jnp.stack([lbl_ref[:, ln] for ln in lanes]), jnp.stack([gnw_ref[:, ln] for ln in lanes]), sck_ref[:, 0],
                         *(jnp.stack([x_refs[4 * j + g][...] for j in range(grp)]) for g in range(4)))
        dlbl, dgnw, ds0, dq, df, di, dz = vjp((jnp.stack([do_ref[:, ln] for ln in lanes]), ds_s[...]))
        ds_s[...] = ds0
        for j in range(grp):
            dlbl_ref[:, lanes[j]] += dlbl[j]
            dgnw_ref[:, lanes[j]] += dgnw[j]
            for g, val in enumerate((dq, df, di, dz)):
                dp_ref[g, :, lanes[j]] = val[j].astype(bf16)

    return pl.pallas_call(
        body, grid=(nh // grp, nc), in_specs=specs,
        out_specs=[pl.BlockSpec((4, CHUNK, grp * LANES), lambda h, c: (0, cidx(c), h)),
                   pl.BlockSpec((2, grp * LANES), lambda h, c: (0, h)),
                   pl.BlockSpec((1, grp * LANES), lambda h, c: (0, h))],
        out_shape=[jax.ShapeDtypeStruct((4, t, dh), bf16), jax.ShapeDtypeStruct((2, dh), f32),
                   jax.ShapeDtypeStruct((1, dh), f32)],
        scratch_shapes=[pltpu.VMEM((grp, LANES, LANES), f32)],
        compiler_params=pltpu.CompilerParams(dimension_semantics=("arbitrary", "arbitrary")),
        name=f"hgrn_bwd_l{int(layer1)}",
    )(*([proj] * (4 * grp)), lbl, gnw, sck, dcat)


def _ln(h, y, w, b):
    u = ALPHA * h + y
    mu = jnp.mean(u, axis=-1, keepdims=True)
    var = jnp.mean(jnp.square(u - mu), axis=-1, keepdims=True)
    return (u - mu) * lax.rsqrt(var + LN_EPS) * w + b


def _row_tile(t):
    return 256 if t % 256 == 0 else t


def _ln_fwd(h, y, w, b):
    t, d = h.shape
    tr = _row_tile(t)

    def body(h_ref, y_ref, w_ref, b_ref, o_ref, o16_ref):
        out = _ln(h_ref[...], y_ref[...], w_ref[...], b_ref[...])
        o_ref[...] = out
        o16_ref[...] = out.astype(bf16)

    row = pl.BlockSpec((tr, d), lambda i: (i, 0))
    vec = pl.BlockSpec((1, d), lambda i: (0, 0))
    return pl.pallas_call(body, grid=(t // tr,), in_specs=[row, row, vec, vec], out_specs=[row, row],
                          out_shape=[jax.ShapeDtypeStruct((t, d), f32), jax.ShapeDtypeStruct((t, d), bf16)],
                          name="ln_fwd")(h, y, w, b)


def _ln_loss(h, y, w, b, tgt):
    t, d = h.shape
    tr = _row_tile(t)

    def body(h_ref, y_ref, w_ref, b_ref, t_ref, g_ref, loss_ref):
        @pl.when(pl.program_id(0) == 0)
        def _():
            loss_ref[...] = jnp.zeros_like(loss_ref)

        err = _ln(h_ref[...], y_ref[...], w_ref[...], b_ref[...]) - t_ref[...]
        g_ref[...] = err * (1.0 / d)
        loss_ref[...] += 0.5 * jnp.sum(jnp.mean(jnp.square(err), axis=-1, keepdims=True), axis=0, keepdims=True)

    row = pl.BlockSpec((tr, d), lambda i: (i, 0))
    vec = pl.BlockSpec((1, d), lambda i: (0, 0))
    return pl.pallas_call(
        body, grid=(t // tr,), in_specs=[row, row, vec, vec, row],
        out_specs=[row, pl.BlockSpec((1, LANES), lambda i: (0, 0))],
        out_shape=[jax.ShapeDtypeStruct((t, d), f32), jax.ShapeDtypeStruct((1, LANES), f32)],
        compiler_params=pltpu.CompilerParams(dimension_semantics=("arbitrary",)), name="ln_loss")(h, y, w, b, tgt)


def _ln_bwd(h, y, w, b, dout):
    t, d = h.shape
    tr = _row_tile(t)

    def body(h_ref, y_ref, w_ref, b_ref, do_ref, dy_ref, dy16_ref, dw_ref, db_ref):
        @pl.when(pl.program_id(0) == 0)
        def _():
            dw_ref[...] = jnp.zeros_like(dw_ref)
            db_ref[...] = jnp.zeros_like(db_ref)

        _, vjp = jax.vjp(lambda yy, ww, bb: _ln(h_ref[...], yy, ww, bb), y_ref[...], w_ref[...], b_ref[...])
        dy, dw, db = vjp(do_ref[...])
        dy_ref[...] = dy
        dy16_ref[...] = dy.astype(bf16)
        dw_ref[...] += dw
        db_ref[...] += db

    row = pl.BlockSpec((tr, d), lambda i: (i, 0))
    vec = pl.BlockSpec((1, d), lambda i: (0, 0))
    return pl.pallas_call(
        body, grid=(t // tr,), in_specs=[row, row, vec, vec, row], out_specs=[row, row, vec, vec],
        out_shape=[jax.ShapeDtypeStruct((t, d), f32), jax.ShapeDtypeStruct((t, d), bf16),
                   jax.ShapeDtypeStruct((1, d), f32), jax.ShapeDtypeStruct((1, d), f32)],
        compiler_params=pltpu.CompilerParams(dimension_semantics=("arbitrary",)), name="ln_bwd")(h, y, w, b, dout)


def _pick(n, prefs):
    for p in prefs:
        if n % p == 0:
            return p
    return n


def _tile(n, want):
    if n <= want:
        return n
    for cand in range(want - want % LANES, 0, -LANES):
        if n % cand == 0:
            return cand
    return n


def _matmul(a, b, mode, name, tiles, add=None, add_scale=1.0, out_dtype=f32):
    if mode == "nn":
        (m, k), n = a.shape, b.shape[1]
    elif mode == "nt":
        (m, k), n = a.shape, b.shape[0]
    else:
        (k, m), n = a.shape, b.shape[1]
    tm, tn, tk = _tile(m, tiles[0]), _tile(n, tiles[1]), _tile(k, tiles[2])
    nk = k // tk
    cache_a = nk == 1 and a.dtype != bf16 and n // tn > 1

    def body(*refs):
        a_ref, b_ref = refs[0], refs[1]
        add_ref = refs[2] if add is not None else None
        n_in = 3 if add is not None else 2
        o_ref = refs[n_in]
        scratch = refs[n_in + 1:]

        def finish(res):
            if add is not None:
                res = res + add_scale * add_ref[...]
            o_ref[...] = res.astype(out_dtype)

        if cache_a:
            a_bf = scratch[0]

            @pl.when(pl.program_id(1) == 0)
            def _():
                a_bf[...] = a_ref[...].astype(bf16)

            a_val = a_bf[...]
        else:
            a_val = a_ref[...].astype(bf16)
        prod = lax.dot_general(a_val, b_ref[...].astype(bf16), _DIMS[mode], preferred_element_type=f32)
        if nk == 1:
            finish(prod)
        else:
            acc = scratch[-1]
            kk = pl.program_id(2)

            @pl.when(kk == 0)
            def _():
                acc[...] = prod

            @pl.when(kk != 0)
            def _():
                acc[...] += prod

            @pl.when(kk == nk - 1)
            def _():
                finish(acc[...])

    a_shape = (tk, tm) if mode == "tn" else (tm, tk)
    a_spec = pl.BlockSpec(a_shape, (lambda i, j, kk: (kk, i)) if mode == "tn" else (lambda i, j, kk: (i, kk)))
    b_spec = pl.BlockSpec((tn, tk), lambda i, j, kk: (j, kk)) if mode == "nt" else pl.BlockSpec((tk, tn), lambda i, j, kk: (kk, j))
    o_spec = pl.BlockSpec((tm, tn), lambda i, j, kk: (i, j))
    in_specs = [a_spec, b_spec] + ([o_spec] if add is not None else [])
    args = [a, b] + ([add] if add is not None else [])
    scratch_shapes = ([pltpu.VMEM(a_shape, bf16)] if cache_a else []) + ([pltpu.VMEM((tm, tn), f32)] if nk > 1 else [])
    return pl.pallas_call(
        body, grid=(m // tm, n // tn, nk), in_specs=in_specs, out_specs=o_spec,
        out_shape=jax.ShapeDtypeStruct((m, n), out_dtype), scratch_shapes=scratch_shapes,
        compiler_params=pltpu.CompilerParams(dimension_semantics=("parallel", "arbitrary", "arbitrary")),
        name=name,
    )(*args)


def _position():
    return lax.axis_index("x"), lax.axis_index("y"), lax.axis_index("c")


def _flip(pos, k):
    x, y, c = pos
    return (1 - x if k & 4 else x, 1 - y if k & 2 else y, 1 - c if k & 1 else c)


def _index(pos):
    return 4 * pos[0] + 2 * pos[1] + pos[2]


def _all_gather_rows(x, name):
    m_per, n = x.shape

    def body(x_ref, out_ref, send_sems, recv_sems, local_sem):
        me = _position()
        sibling = _flip(me, 1)
        chips = (2, 4, 6)

        def rows(pos):
            return out_ref.at[pl.ds(_index(pos) * m_per, m_per), :]

        def copy(sem, block, to, src=None):
            return pltpu.make_async_remote_copy(
                src_ref=rows(block) if src is None else src, dst_ref=rows(block),
                send_sem=send_sems.at[sem], recv_sem=recv_sems.at[sem], device_id=to, device_id_type=MESH)

        mine = pltpu.make_async_copy(x_ref, rows(me), local_sem)
        mine.start()
        first = [copy(0, me, sibling, src=x_ref)]
        first += [copy(1 + j, me, _flip(me, k), src=x_ref) for j, k in enumerate(chips)]
        for cp in first:
            cp.start()
        passed = [copy(4 + j, _flip(me, k), sibling) for j, k in enumerate(chips)]
        for j, k in enumerate(chips):
            copy(1 + j, _flip(me, k), me).wait_recv()
            passed[j].start()
        copy(0, sibling, me).wait_recv()
        for j, k in enumerate(chips):
            copy(4 + j, _flip(sibling, k), me).wait_recv()
        for cp in first + passed:
            cp.wait_send()
        mine.wait()

    return pl.pallas_call(
        body, out_shape=jax.ShapeDtypeStruct((N_DEV * m_per, n), x.dtype),
        in_specs=[pl.BlockSpec(memory_space=pl.ANY)], out_specs=pl.BlockSpec(memory_space=pl.ANY),
        scratch_shapes=[pltpu.SemaphoreType.DMA((7,)), pltpu.SemaphoreType.DMA((7,)), pltpu.SemaphoreType.DMA(())],
        name=name,
    )(x)


def _split_start(srcs, lands, plan, n_copies, name):
    n_arr = len(srcs)
    hbm = pl.BlockSpec(memory_space=pltpu.HBM)
    sem = pl.BlockSpec(memory_space=pltpu.SEMAPHORE)

    def body(*refs):
        src_refs, land_refs = refs[:n_arr], refs[n_arr:2 * n_arr]
        send_sems, recv_sems = refs[2 * n_arr:3 * n_arr], refs[3 * n_arr:4 * n_arr]
        token = refs[-1]
        me = _position()
        for i in range(n_arr):
            for j, (src, dst, peer, _) in enumerate(plan(i, src_refs[i], land_refs[i], me)):
                pltpu.make_async_remote_copy(src_ref=src, dst_ref=dst, send_sem=send_sems[i].at[j], recv_sem=recv_sems[i].at[j],
                                             device_id=peer, device_id_type=MESH).start()
        token[...] = jnp.zeros_like(token)

    outs = pl.pallas_call(
        body, name=name,
        out_shape=([pltpu.SemaphoreType.DMA((n_copies,))] * (2 * n_arr)
                   + [pltpu.HBM(a.shape, a.dtype) for a in list(srcs) + list(lands)]
                   + [jax.ShapeDtypeStruct((8, LANES), f32)]),
        in_specs=[hbm] * (2 * n_arr),
        out_specs=[sem] * (2 * n_arr) + [hbm] * (2 * n_arr) + [pl.BlockSpec(memory_space=pltpu.VMEM)],
        input_output_aliases={i: 2 * n_arr + i for i in range(2 * n_arr)},
        compiler_params=pltpu.CompilerParams(has_side_effects=pltpu.SideEffectType.DATAFLOW_SIDE_EFFECTING),
    )(*[pltpu.with_memory_space_constraint(a, pltpu.HBM) for a in list(srcs) + list(lands)])
    return (outs[:n_arr], outs[n_arr:2 * n_arr], outs[2 * n_arr:3 * n_arr], outs[3 * n_arr:4 * n_arr], outs[-1])


def _split_wait(started, plan, after, name):
    send_sems, recv_sems, srcs, lands, _ = started
    n_arr = len(srcs)
    hbm = pl.BlockSpec(memory_space=pltpu.HBM)
    sem = pl.BlockSpec(memory_space=pltpu.SEMAPHORE)

    def body(*refs):
        src_refs, land_refs = refs[:n_arr], refs[n_arr:2 * n_arr]
        s_sems, r_sems = refs[2 * n_arr:3 * n_arr], refs[3 * n_arr:4 * n_arr]
        me = _position()
        for i in range(n_arr):
            for j, (src, _, peer, arrival) in enumerate(plan(i, src_refs[i], land_refs[i], me)):
                cp = pltpu.make_async_remote_copy(src_ref=src, dst_ref=arrival, send_sem=s_sems[i].at[j], recv_sem=r_sems[i].at[j],
                                                  device_id=peer, device_id_type=MESH)
                cp.wait_send()
                cp.wait_recv()

    outs = pl.pallas_call(
        body, name=name,
        out_shape=[pltpu.HBM(a.shape, a.dtype) for a in list(srcs) + list(lands)],
        in_specs=[hbm] * (2 * n_arr) + [sem] * (2 * n_arr) + [pl.BlockSpec(memory_space=pl.ANY)],
        out_specs=[hbm] * (2 * n_arr),
        input_output_aliases={i: i for i in range(2 * n_arr)},
        compiler_params=pltpu.CompilerParams(has_side_effects=pltpu.SideEffectType.DATAFLOW_SIDE_EFFECTING),
    )(*srcs, *lands, *send_sems, *recv_sems, after)
    return outs[:n_arr], outs[n_arr:]


_GATHER_FLIPS = (1, 2, 4, 6)


def _gather_plan(i, src_ref, land_ref, me):
    m = src_ref.shape[0]

    def rows(pos):
        return land_ref.at[pl.ds(_index(pos) * m, m), :]

    return [(src_ref, rows(me), _flip(me, k), rows(_flip(me, k))) for k in _GATHER_FLIPS]


def _gather_forward(lands, name):
    n_arr = len(lands)
    chips = (2, 4, 6)

    def body(*refs):
        out_refs = refs[n_arr:2 * n_arr]
        send_sems, recv_sems = refs[2 * n_arr:]
        me = _position()
        sibling = _flip(me, 1)
        sends, arrivals = [], []
        for i, out_ref in enumerate(out_refs):
            m = out_ref.shape[0] // N_DEV

            def copy(pos, j):
                blk = out_ref.at[pl.ds(_index(pos) * m, m), :]
                return pltpu.make_async_remote_copy(src_ref=blk, dst_ref=blk, send_sem=send_sems.at[3 * i + j],
                                                    recv_sem=recv_sems.at[3 * i + j], device_id=sibling, device_id_type=MESH)

            for j, k in enumerate(chips):
                sends.append(copy(_flip(me, k), j))
                arrivals.append(copy(_flip(sibling, k), j))
        for cp in sends:
            cp.start()
        for cp in arrivals:
            cp.wait_recv()
        for cp in sends:
            cp.wait_send()

    anyspec = pl.BlockSpec(memory_space=pl.ANY)
    return pl.pallas_call(
        body, out_shape=[jax.ShapeDtypeStruct(a.shape, a.dtype) for a in lands],
        in_specs=[anyspec] * n_arr, out_specs=[anyspec] * n_arr, input_output_aliases={i: i for i in range(n_arr)},
        scratch_shapes=[pltpu.SemaphoreType.DMA((3 * n_arr,))] * 2, name=name,
    )(*lands)


def _chips_plan(i, src_ref, land_ref, me):
    m = src_ref.shape[0] // 4
    plan = []
    for j, k in enumerate((2, 4, 6)):
        peer = _flip(me, k)
        plan.append((src_ref.at[pl.ds((2 * peer[0] + peer[1]) * m, m), :], land_ref.at[j], peer, land_ref.at[j]))
    return plan


def _exchange_siblings(gs, name):
    n_arr = len(gs)

    def body(*refs):
        g_refs, out_refs = refs[:n_arr], refs[n_arr:2 * n_arr]
        send_sems, recv_sems = refs[2 * n_arr:]
        me = _position()
        c = me[2]
        sibling = _flip(me, 1)
        copies = []
        for i, (g_ref, out_ref) in enumerate(zip(g_refs, out_refs)):
            m_per = g_ref.shape[0] // N_DEV
            for q in range(4):
                copies.append(pltpu.make_async_remote_copy(
                    src_ref=g_ref.at[pl.ds((2 * q + 1 - c) * m_per, m_per), :], dst_ref=out_ref.at[q],
                    send_sem=send_sems.at[4 * i + q], recv_sem=recv_sems.at[4 * i + q],
                    device_id=sibling, device_id_type=MESH))
        for cp in copies:
            cp.start()
        for cp in copies:
            cp.wait_recv()
        for cp in copies:
            cp.wait_send()

    anyspec = pl.BlockSpec(memory_space=pl.ANY)
    return pl.pallas_call(
        body, out_shape=[jax.ShapeDtypeStruct((4, g.shape[0] // N_DEV, g.shape[1]), g.dtype) for g in gs],
        in_specs=[anyspec] * n_arr, out_specs=[anyspec] * n_arr,
        scratch_shapes=[pltpu.SemaphoreType.DMA((4 * n_arr,))] * 2, name=name,
    )(*gs)


def _sum_with_sibling(g, recv, name):
    m = g.shape[0] // N_DEV
    n = g.shape[1]
    tr = _pick(m, (208, 128, 64, 32, 16))
    nt = m // tr

    def body(g_ref, r_ref, o_ref):
        c = lax.axis_index("c")
        own = jnp.where(c == 0, g_ref[0, 0].astype(f32), g_ref[0, 1].astype(f32))
        o_ref[...] = (own + r_ref[0].astype(f32)).astype(o_ref.dtype)

    return pl.pallas_call(
        body, grid=(4, nt),
        in_specs=[pl.BlockSpec((1, 2, tr, n), lambda q, i: (q, 0, i, 0)), pl.BlockSpec((1, tr, n), lambda q, i: (q, i, 0))],
        out_specs=pl.BlockSpec((tr, n), lambda q, i: (q * nt + i, 0)),
        out_shape=jax.ShapeDtypeStruct((4 * m, n), bf16), name=name,
    )(g.reshape(4, 2, m, n), recv)


def _sum_with_chips(h, recv, name):
    m = h.shape[0] // 4
    n = h.shape[1]
    tr = _pick(m, (208, 128, 64, 32, 16))

    def body(h_ref, r_ref, o_ref):
        my_q = 2 * lax.axis_index("x") + lax.axis_index("y")
        own = h_ref[0].astype(f32)
        for q in range(1, 4):
            own = jnp.where(my_q == q, h_ref[q].astype(f32), own)
        o_ref[...] = ((own + r_ref[0].astype(f32)) + r_ref[1].astype(f32)) + r_ref[2].astype(f32)

    return pl.pallas_call(
        body, grid=(m // tr,),
        in_specs=[pl.BlockSpec((4, tr, n), lambda i: (0, i, 0)), pl.BlockSpec((3, tr, n), lambda i: (0, i, 0))],
        out_specs=pl.BlockSpec((tr, n), lambda i: (i, 0)), out_shape=jax.ShapeDtypeStruct((m, n), f32), name=name,
    )(h.reshape(4, m, n), recv)


def _sum_slots(parts, name):
    n_slot, m, n = parts.shape
    tr = _pick(m, (208, 128, 64, 32, 16, 8))

    def body(p_ref, o_ref):
        acc = p_ref[0]
        for s in range(1, n_slot):
            acc = acc + p_ref[s]
        o_ref[...] = acc

    return pl.pallas_call(
        body, grid=(m // tr,), in_specs=[pl.BlockSpec((n_slot, tr, n), lambda i: (0, i, 0))],
        out_specs=pl.BlockSpec((tr, n), lambda i: (i, 0)), out_shape=jax.ShapeDtypeStruct((m, n), parts.dtype), name=name,
    )(parts)


def _reduce_scatter_begin(gs, name):
    from_sibling = _exchange_siblings(gs, "rs_d2d_" + name)
    chip_sums = [_sum_with_sibling(g, r, f"rs_sum2_{name}_{i}") for i, (g, r) in enumerate(zip(gs, from_sibling))]
    lands = [lax.empty((3, h.shape[0] // 4, h.shape[1]), h.dtype) for h in chip_sums]
    return _split_start(chip_sums, lands, _chips_plan, 3, "rs_ici_start_" + name)


def _reduce_scatter_end(started, after, name):
    chip_sums, from_chips = _split_wait(started, _chips_plan, after, "rs_ici_wait_" + name)
    return [_sum_with_chips(h, r, f"rs_sum4_{name}_{i}") for i, (h, r) in enumerate(zip(chip_sums, from_chips))]


def _adamw(w, g, m, v, name):
    shape = w.shape
    n = shape[-1]
    r = w.size // n
    w2, g2, m2, v2 = (a.reshape(r, n) for a in (w, g, m, v))
    tr = _pick(r, (256, 128, 64, 32, 16, 8)) if r * n > 65536 else r

    def body(w_ref, g_ref, m_ref, v_ref, d_ref, mo_ref, vo_ref):
        gg = g_ref[...]
        mm = ADAM_B1 * m_ref[...] + (1.0 - ADAM_B1) * gg
        vv = ADAM_B2 * v_ref[...] + (1.0 - ADAM_B2) * jnp.square(gg)
        m_hat = mm / (1.0 - ADAM_B1 ** ADAM_STEP)
        v_hat = vv / (1.0 - ADAM_B2 ** ADAM_STEP)
        d_ref[...] = -ADAM_LR * (m_hat / (jnp.sqrt(v_hat) + ADAM_EPS) + ADAM_WD * w_ref[...])
        mo_ref[...] = mm
        vo_ref[...] = vv

    spec = pl.BlockSpec((tr, n), lambda i: (i, 0))
    outs = pl.pallas_call(
        body, grid=(r // tr,), in_specs=[spec] * 4, out_specs=[spec] * 3,
        out_shape=[jax.ShapeDtypeStruct((r, n), f32)] * 3, name=name,
    )(w2, g2, m2, v2)
    return tuple(o.reshape(shape) for o in outs)


_SMALL = ("shift_mu", "w_decay0", "a0", "k_k", "k_a", "r_k", "ln_x_w", "ln_x_b", "v_mix0", "lb_logits",
          "g_norm_w", "ln_w", "ln_b")
_NAMES = ("w_in", "shift_mu", "w_decay0", "w_decay_up", "a0", "a_up", "k_k", "k_a", "r_k", "ln_x_w", "ln_x_b",
          "v_mix0", "v_mix_down", "v_mix_up", "lb_logits", "g_norm_w", "w_out", "ln_w", "ln_b")


def _pad_rows(a, rows, at_end):
    z = jnp.zeros((rows - a.shape[0], a.shape[1]), a.dtype)
    return jnp.concatenate([a, z] if at_end else [z, a], axis=0)


def kernel(x, w_in, shift_mu, w_decay0, w_decay_up, a0, a_up, k_k, k_a, r_k, ln_x_w, ln_x_b, v_mix0, v_mix_down, v_mix_up, lb_logits, g_norm_w, w_out, ln_w, ln_b, loss_target, m_w_in, m_shift_mu, m_w_decay0, m_w_decay_up, m_a0, m_a_up, m_k_k, m_k_a, m_r_k, m_ln_x_w, m_ln_x_b, m_v_mix0, m_v_mix_down, m_v_mix_up, m_lb_logits, m_g_norm_w, m_w_out, m_ln_w, m_ln_b, v_w_in, v_shift_mu, v_w_decay0, v_w_decay_up, v_a0, v_a_up, v_k_k, v_k_a, v_r_k, v_ln_x_w, v_ln_x_b, v_v_mix0, v_v_mix_down, v_v_mix_up, v_lb_logits, v_g_norm_w, v_w_out, v_ln_w, v_ln_b):
    weights = dict(w_in=w_in, shift_mu=shift_mu, w_decay0=w_decay0, w_decay_up=w_decay_up, a0=a0, a_up=a_up, k_k=k_k,
                   k_a=k_a, r_k=r_k, ln_x_w=ln_x_w, ln_x_b=ln_x_b, v_mix0=v_mix0, v_mix_down=v_mix_down,
                   v_mix_up=v_mix_up, lb_logits=lb_logits, g_norm_w=g_norm_w, w_out=w_out, ln_w=ln_w, ln_b=ln_b)
    mom1 = dict(w_in=m_w_in, shift_mu=m_shift_mu, w_decay0=m_w_decay0, w_decay_up=m_w_decay_up, a0=m_a0, a_up=m_a_up,
                k_k=m_k_k, k_a=m_k_a, r_k=m_r_k, ln_x_w=m_ln_x_w, ln_x_b=m_ln_x_b, v_mix0=m_v_mix0,
                v_mix_down=m_v_mix_down, v_mix_up=m_v_mix_up, lb_logits=m_lb_logits, g_norm_w=m_g_norm_w,
                w_out=m_w_out, ln_w=m_ln_w, ln_b=m_ln_b)
    mom2 = dict(w_in=v_w_in, shift_mu=v_shift_mu, w_decay0=v_w_decay0, w_decay_up=v_w_decay_up, a0=v_a0, a_up=v_a_up,
                k_k=v_k_k, k_a=v_k_a, r_k=v_r_k, ln_x_w=v_ln_x_w, ln_x_b=v_ln_x_b, v_mix0=v_v_mix0,
                v_mix_down=v_v_mix_down, v_mix_up=v_v_mix_up, lb_logits=v_lb_logits, g_norm_w=v_g_norm_w,
                w_out=v_w_out, ln_w=v_ln_w, ln_b=v_ln_b)
    assert x.shape[0] == 1 and w_in.shape[0] == DEPTH
    t, d = x.shape[1], x.shape[2]
    dr = w_decay0.shape[1]
    dh = g_norm_w.shape[1]
    rank_w, rank_a, rank_v = w_decay_up.shape[1], a_up.shape[1], v_mix_up.shape[1]
    rwc = 4 * dr + rank_w + rank_a
    assert rank_w + rank_a == LANES and rank_v <= LANES and dr + dh == d
    assert t % CHUNK == 0 and dr % LANES == 0 and dh % LANES == 0 and shift_mu.shape[1] == rwc
    n_pair = dr // LANES
    me = _index(_position())

    win_t = [_all_gather_rows(w_in[0].T.astype(bf16), "ag_w_in_0"), None]
    wout = [None, None]
    late_blocks = [w_out[0].astype(bf16), w_in[1].T.astype(bf16), w_out[1].astype(bf16)]
    late_lands = [lax.dynamic_update_slice(lax.empty((N_DEV * blk.shape[0], blk.shape[1]), bf16), blk, (me * blk.shape[0], 0))
                  for blk in late_blocks]
    late_gather = _split_start(late_blocks, late_lands, _gather_plan, len(_GATHER_FLIPS), "ag_late_start")
    shard = dr // N_DEV
    pack = jnp.concatenate([w_decay_up[0], w_decay_up[1], a_up[0], a_up[1], v_mix_up[0], v_mix_down[0].T], axis=0)
    pack = _all_gather_rows(pack, "ag_small")
    pack = jnp.transpose(pack.reshape(N_DEV, -1, shard), (1, 0, 2)).reshape(-1, dr)
    offs = [0, rank_w, 2 * rank_w, 2 * rank_w + rank_a, 2 * rank_w + 2 * rank_a, 2 * rank_w + 2 * rank_a + rank_v,
            2 * rank_w + 2 * rank_a + 2 * rank_v]
    wdu_f = [pack[offs[0]:offs[1]], pack[offs[1]:offs[2]]]
    aup_f = [pack[offs[2]:offs[3]], pack[offs[3]:offs[4]]]
    vup_f = pack[offs[4]:offs[5]]
    vdown_f = pack[offs[5]:offs[6]].T

    def after_start(a, started):
        return a + started[-1][0:1, 0:1]

    def rwkv_params(l):
        mu = after_start(shift_mu[0:1], late_gather) if l == 0 else shift_mu[l:l + 1]
        prm = [mu, w_decay0[l:l + 1], a0[l:l + 1], _pad_rows(wdu_f[l], LANES, True),
               _pad_rows(aup_f[l], LANES, False)]
        if l == 1:
            prm += [v_mix0[0:1], _pad_rows(vdown_f.T, LANES, True).T, _pad_rows(vup_f, LANES, True)]
        rows = jnp.stack([k_k[l], k_a[l], r_k[l], ln_x_w[l], ln_x_b[l]] + [jnp.zeros((dr,), f32)] * 3, axis=0)
        pp = jnp.transpose(rows.reshape(8, n_pair, LANES), (1, 0, 2))
        return tuple(prm), pp

    h = x[0]
    h16 = h.astype(bf16)
    tgt = loss_target[0]
    saved = []
    vfirst = None
    for l in range(DEPTH):
        prm, pp = rwkv_params(l)
        proj = _matmul(h16, win_t[l], "nt", f"mm_proj_{l}", (2048, 640, 2048))
        if l == 0:
            cat, vfirst, mck = _rwkv_fwd(False, proj, None, prm, pp, d)
        else:
            cat, mck = _rwkv_fwd(True, proj, vfirst, prm, pp, d)
        cat, sck = _hgrn_fwd(l == 1, proj, lb_logits, g_norm_w[l:l + 1], cat, rwc)
        if l == 0:
            _, arrived = _split_wait(late_gather, _gather_plan, cat, "ag_late_wait")
            wout[0], win_t[1], wout[1] = _gather_forward(arrived, "ag_late_forward")
        y = _matmul(cat, wout[l], "nn", f"mm_out_{l}", (1024, 1024, 2048))
        saved.append((h, h16, proj, prm, pp, mck, sck, cat, y))
        if l < DEPTH - 1:
            h, h16 = _ln_fwd(h, y, ln_w[l:l + 1], ln_b[l:l + 1])
        else:
            dh_out, loss_part = _ln_loss(h, y, ln_w[l:l + 1], ln_b[l:l + 1], tgt)
    loss = lax.psum(loss_part[0, 0], ("x", "y", "c"))

    grads = {}
    big = {}
    dvfirst = None
    d_lbl = None
    rs_started = {}
    for l in reversed(range(DEPTH)):
        h_l, h16_l, proj, prm, pp, mck, sck, cat, y = saved[l]
        ln_w_l = ln_w[l:l + 1] if l == DEPTH - 1 else after_start(ln_w[l:l + 1], rs_started[l + 1])
        dy, dy16, g_ln_w, g_ln_b = _ln_bwd(h_l, y, ln_w_l, ln_b[l:l + 1], dh_out)
        dcat = _matmul(dy16, wout[l], "nt", f"mm_dcat_{l}", (1024, 1024, 2048))
        big[("w_out", l)] = _matmul(cat, dy16, "tn", f"mm_dwout_{l}", (512, 2048, 2048), out_dtype=bf16)
        if l == 1:
            outs = _rwkv_bwd(True, proj, vfirst, prm, pp, mck, dcat, None)
            dproj_r, dvfirst = outs[0], outs[1]
            dprm, dpp = outs[2:-1], outs[-1]
        else:
            outs = _rwkv_bwd(False, proj, None, prm, pp, mck, dcat, dvfirst)
            dproj_r = outs[0]
            dprm, dpp = outs[1:-1], outs[-1]
        dproj_h, dlbl_l, dgnw = _hgrn_bwd(l == 1, proj, lb_logits, g_norm_w[l:l + 1], sck, dcat, rwc)
        dproj = jnp.concatenate([dproj_r] + [dproj_h[i] for i in range(4)], axis=1)
        big[("w_in", l)] = _matmul(dproj, h16_l, "tn", f"mm_dwin_{l}", (640, 2048, 2048), out_dtype=bf16)
        sharded = [dprm[3][:rank_w].T, dprm[4][rank_w:].T]
        if l == 1:
            sharded += [dprm[6][:, :rank_v], dprm[7][:rank_v].T,
                        jnp.zeros((dr, LANES - 2 * rank_v), f32)]
        sharded = jnp.concatenate(sharded, axis=1).astype(bf16)
        rs_started[l] = _reduce_scatter_begin([big[("w_in", l)], big[("w_out", l)], sharded], f"l{l}")
        dy_res = after_start(dy, rs_started[l]) if l == 0 else dy
        dh_out = _matmul(dproj, win_t[l], "nn", f"mm_dh_{l}", (1024, 1024, 1664), add=dy_res, add_scale=ALPHA)
        dpp = jnp.transpose(dpp, (1, 0, 2)).reshape(8, dr)
        grads[l] = dict(shift_mu=dprm[0][0], w_decay0=dprm[1][0], a0=dprm[2][0],
                        k_k=dpp[0], k_a=dpp[1], r_k=dpp[2], ln_x_w=dpp[3], ln_x_b=dpp[4],
                        g_norm_w=dgnw[0], ln_w=g_ln_w[0], ln_b=g_ln_b[0])
        if l == 1:
            grads[l].update(v_mix0=dprm[5][0])
            d_lbl = dlbl_l
    grad_x = dh_out[None]

    def both(name):
        return jnp.stack([grads[0][name], grads[1][name]])

    small = dict(shift_mu=both("shift_mu"), w_decay0=both("w_decay0"), a0=both("a0"), k_k=both("k_k"), k_a=both("k_a"),
                 r_k=both("r_k"), ln_x_w=both("ln_x_w"), ln_x_b=both("ln_x_b"), v_mix0=grads[1]["v_mix0"][None],
                 lb_logits=d_lbl, g_norm_w=both("g_norm_w"), ln_w=both("ln_w"), ln_b=both("ln_b"))
    flat = jnp.concatenate([small[nm].reshape(-1) for nm in _SMALL])
    n_flat = flat.shape[0]
    rows = -(-n_flat // (8 * LANES)) * 8
    flat = jnp.concatenate([flat, jnp.zeros((rows * LANES - n_flat,), f32)]).reshape(rows, LANES)
    total = _sum_slots(_all_gather_rows(flat, "ag_small_grads").reshape(N_DEV, rows, LANES), "sum_small_grads").reshape(-1)
    gsm = {}
    off = 0
    for nm in _SMALL:
        size = small[nm].size
        gsm[nm] = total[off:off + size].reshape(small[nm].shape)
        off += size
    reduced = {1: _reduce_scatter_end(rs_started[1], dh_out, "l1")}
    reduced[0] = _reduce_scatter_end(rs_started[0], total, "l0")
    gsm["w_in"] = jnp.stack([reduced[l][0].T for l in range(DEPTH)])
    gsm["w_out"] = jnp.stack([reduced[l][1] for l in range(DEPTH)])
    gsm["w_decay_up"] = jnp.stack([reduced[l][2][:, :rank_w].T for l in range(DEPTH)])
    gsm["a_up"] = jnp.stack([reduced[l][2][:, rank_w:rank_w + rank_a].T for l in range(DEPTH)])
    gsm["v_mix_down"] = reduced[1][2][:, LANES:LANES + rank_v][None]
    gsm["v_mix_up"] = reduced[1][2][:, LANES + rank_v:LANES + 2 * rank_v].T[None]

    deltas, new_m, new_v = {}, {}, {}
    for nm in _NAMES:
        deltas[nm], new_m[nm], new_v[nm] = _adamw(weights[nm], gsm[nm], mom1[nm], mom2[nm], "adamw_" + nm)
    return (loss, grad_x, *[gsm[nm] for nm in _NAMES], *[deltas[nm] for nm in _NAMES],
            *[new_m[nm] for nm in _NAMES], *[new_v[nm] for nm in _NAMES])
```

```python
import functools

import jax
import jax.numpy as jnp
from jax import lax
from jax.experimental import pallas as pl
from jax.experimental.pallas import tpu as pltpu

f32 = jnp.float32
bf16 = jnp.bfloat16

N_DEV = 8
CHUNK = 64
LANES = 128
RWKV_HEAD = 64
DEPTH = 2
ALPHA = (2 * DEPTH) ** 0.25
LN_EPS = 1e-5
GN_EPS = 64e-5
RMS_EPS = 1e-5
LB_FLOOR = 1e-30
ADAM_LR, ADAM_B1, ADAM_B2, ADAM_EPS, ADAM_WD, ADAM_STEP = 0.001, 0.9, 0.999, 1e-08, 0.01, 10
MESH = pl.DeviceIdType.MESH


def _iota(shape, d):
    return lax.broadcasted_iota(jnp.int32, shape, d)


_DIMS = {"nn": (((1,), (0,)), ((), ())), "nt": (((1,), (1,)), ((), ())), "tn": (((0,), (0,)), ((), ()))}
_BATCH_DIMS = {"nn": (((2,), (1,)), ((0,), (0,))), "nt": (((2,), (2,)), ((0,), (0,))), "tn": (((1,), (1,)), ((0,), (0,)))}
_K_AXES = {"nn": (-1, -2), "nt": (-1, -1), "tn": (-2, -2)}


def _mxu(a, b, mode):
    return lax.dot_general(a, b, (_BATCH_DIMS if a.ndim == 3 else _DIMS)[mode], preferred_element_type=f32)


def _split(x):
    hi = x.astype(bf16)
    return hi, (x - hi.astype(f32)).astype(bf16)


def _mm2_impl(a, b, mode, passes=3):
    ah, al = _split(a)
    if passes == 3:
        bh, bl = _split(b)
        lhs, rhs = [ah, ah, al], [bh, bl, bh]
    else:
        bh = b.astype(bf16)
        lhs, rhs = [ah, al], [bh, bh]
    ka, kb = _K_AXES[mode]
    k = a.shape[ka]
    if k % (LANES if -1 in (ka, kb) else 16) == 0:
        return _mxu(jnp.concatenate(lhs, axis=ka), jnp.concatenate(rhs, axis=kb), mode)
    out = _mxu(lhs[0], rhs[0], mode)
    for x, y in zip(lhs[1:], rhs[1:]):
        out = out + _mxu(x, y, mode)
    return out


@functools.partial(jax.custom_vjp, nondiff_argnums=(2, 3))
def _mm2(a, b, mode, passes=3):
    return _mm2_impl(a, b, mode, passes)


def _mm2_fwd(a, b, mode, passes):
    return _mm2_impl(a, b, mode, passes), (a, b)


def _mm2_bwd(mode, passes, res, g):
    a, b = res
    if mode == "nn":
        return _mm2_impl(g, b, "nt", passes), _mm2_impl(a, g, "tn", passes)
    if mode == "nt":
        return _mm2_impl(g, b, "nn", passes), _mm2_impl(g, a, "tn", passes)
    return _mm2_impl(b, g, "nt", passes), _mm2_impl(a, g, "nn", passes)


_mm2.defvjp(_mm2_fwd, _mm2_bwd)

TRI_PASSES = 2
APPLY_PASSES = 2


def _const_impl(cm, x, mode):
    hi, lo = _split(x)
    if mode in ("r", "rt"):
        shape = x.shape
        hi, lo = hi.reshape(-1, shape[-1]), lo.reshape(-1, shape[-1])
        dims = "nn" if mode == "r" else "nt"
        out = _mxu(hi, cm, dims) + _mxu(lo, cm, dims)
        return out.reshape(shape[:-1] + (out.shape[-1],))
    if x.ndim == 3:
        cm = jnp.broadcast_to(cm, (x.shape[0],) + cm.shape)
    return _mxu(cm, hi, mode) + _mxu(cm, lo, mode)


@jax.custom_vjp
def _const_left(cm, x):
    return _const_impl(cm, x, "nn")


_const_left.defvjp(lambda cm, x: (_const_impl(cm, x, "nn"), cm),
                   lambda cm, g: (jnp.zeros_like(cm), _const_impl(cm, g, "tn")))


@jax.custom_vjp
def _const_right(x, cm):
    return _const_impl(cm, x, "r")


_const_right.defvjp(lambda x, cm: (_const_impl(cm, x, "r"), cm),
                    lambda cm, g: (_const_impl(cm, g, "rt"), jnp.zeros_like(cm)))


def _tri_inv(a):
    n = a.shape[-1]
    tm = (_iota((n, n), 0) == _iota((n, n), 1)).astype(f32) + a
    ak = a
    for _ in range(5):
        ak = _mm2_impl(ak, ak, "nn", TRI_PASSES)
        tm = tm + _mm2_impl(tm, ak, "nn", TRI_PASSES)
    return tm


@jax.custom_vjp
def _tri_solve(a, x):
    return _mm2_impl(_tri_inv(a), x, "nn")


def _tri_solve_fwd(a, x):
    tm = _tri_inv(a)
    u = _mm2_impl(tm, x, "nn")
    return u, (tm, u)


def _tri_solve_bwd(res, du):
    tm, u = res
    dx = _mm2_impl(tm, du, "tn")
    return _mm2_impl(dx, u, "nt"), dx


_tri_solve.defvjp(_tri_solve_fwd, _tri_solve_bwd)


def _col_of_row(row_vec):
    n = row_vec.shape[-1]
    eye = _iota((n, n), 0) == _iota((n, n), 1)
    return jnp.sum(jnp.where(eye, jnp.broadcast_to(row_vec, row_vec.shape[:-2] + (n, n)), 0.0), axis=-1, keepdims=True)


def _softplus(x):
    return jnp.maximum(x, 0.0) + jnp.log1p(jnp.exp(-jnp.abs(x)))


def _log_sigmoid(x):
    return -_softplus(-x)


def _logaddexp(a, b):
    return jnp.maximum(a, b) + jnp.log1p(jnp.exp(-jnp.abs(a - b)))


def _silu(x):
    return x * jax.nn.sigmoid(x)


def _tril(c, strict):
    r, s = _iota((c, c), 0), _iota((c, c), 1)
    return (r > s) if strict else (r >= s)


def _last_row(a):
    c = a.shape[-2]
    return jnp.sum(jnp.where(_iota(a.shape, a.ndim - 2) == c - 1, a, 0.0), axis=-2, keepdims=True)


def _rwkv_pre(layer1, prm, y, prev, vf):
    c = y.shape[0]
    if layer1:
        mu, w0, a0, wup, aup, v0, vdown, vup = prm
    else:
        mu, w0, a0, wup, aup = prm
    dr = w0.shape[1]
    shift = (_iota((c, c), 0) == _iota((c, c), 1) + 1).astype(bf16)
    y_prev = _const_left(shift, y) + jnp.where(_iota((c, 1), 0) == 0, prev, 0.0)
    rw = y + mu * (y_prev - y)
    r, k, v, z = (rw[:, i * dr:(i + 1) * dr] for i in range(4))
    wdad = rw[:, 4 * dr:4 * dr + LANES]
    w_raw = w0 + _mm2(jnp.tanh(wdad), wup, "nn")
    lw = -jnp.exp(-_softplus(-w_raw) - 0.5)
    asig = jax.nn.sigmoid(a0 + _mm2(wdad, aup, "nn"))
    if layer1:
        v = v + (vf - v) * jax.nn.sigmoid(v0 + _mm2(_mm2(v, vdown, "nn"), vup, "nn"))
    return r, k, v, z, lw, asig


def _rwkv_pair(pp, m0, xs):
    kkw, kaw, rkw, gnw, gnb = pp
    r, k, v, z, lw, asig = xs
    c = r.shape[-2]
    n2 = 2 * c
    lane = _iota((1, LANES), 1)
    mh0, mh1 = (lane < RWKV_HEAD).astype(f32), (lane >= RWKV_HEAD).astype(f32)
    same_head = _iota((LANES, LANES), 0) // RWKV_HEAD == _iota((LANES, LANES), 1) // RWKV_HEAD
    g = same_head.astype(bf16)

    def seg(x):
        return _const_right(x, g)

    def stack(x):
        return jnp.concatenate([x * mh0, x * mh1], axis=-2)

    kk = k * kkw
    kk = kk / jnp.maximum(jnp.sqrt(seg(kk * kk)), 1e-12)
    k2 = k * (1.0 + (asig - 1.0) * kaw)
    a = -kk
    b = kk * asig
    cum = _const_left(_tril(c, False).astype(bf16), lw)
    at = stack(a * jnp.exp(cum - lw))
    rt = stack(r * jnp.exp(cum))
    en = jnp.exp(-cum)
    sc = _mm2(jnp.concatenate([at, rt], axis=-2), jnp.concatenate([stack(b * en), stack(k2 * en)], axis=-2), "nt")
    row, col = _iota((n2, n2), 0), _iota((n2, n2), 1)
    same = row // c == col // c
    strict = same & (row % c > col % c)
    incl = same & (row % c >= col % c)
    aab = jnp.where(strict, sc[..., :n2, :n2], 0.0)
    aak = jnp.where(strict, sc[..., :n2, n2:], 0.0)
    arb = jnp.where(incl, sc[..., n2:, :n2], 0.0)
    ark = jnp.where(incl, sc[..., n2:, n2:], 0.0)
    vv = jnp.concatenate([v, v], axis=-2)
    mask_st = jnp.concatenate([jnp.broadcast_to(mh0, (c, LANES)), jnp.broadcast_to(mh1, (c, LANES))], axis=0)
    x_st = _mm2(jnp.concatenate([at, aak], axis=-1), jnp.concatenate([m0, vv], axis=-2), "nn", APPLY_PASSES)
    u_st = _tri_solve(aab, x_st) * mask_st
    o_st = _mm2(jnp.concatenate([rt, arb, ark], axis=-1), jnp.concatenate([m0, u_st, vv], axis=-2), "nn", APPLY_PASSES) * mask_st
    u = u_st[..., :c, :] + u_st[..., c:, :]
    o = o_st[..., :c, :] + o_st[..., c:, :]
    cum_last = _last_row(cum)
    dec_end = jnp.exp(cum_last - cum)
    m_new = _col_of_row(jnp.exp(cum_last)) * m0 + _mm2(
        jnp.concatenate([b * dec_end, k2 * dec_end], axis=-2), jnp.concatenate([u, v], axis=-2), "tn", APPLY_PASSES) * same_head.astype(f32)
    mean = seg(o) * (1.0 / RWKV_HEAD)
    d = o - mean
    var = seg(d * d) * (1.0 / RWKV_HEAD)
    on = d * lax.rsqrt(var + GN_EPS) * gnw + gnb
    bonus = seg(r * k2 * rkw) * v
    return (on + bonus) * _silu(z), m_new


def _split_lanes(a, n):
    return [a[:, i * LANES:(i + 1) * LANES] for i in range(n)]


def _rwkv_step(layer1, prm, y, prev, vf, pp, m0):
    xs = _rwkv_pre(layer1, prm, y, prev, vf)
    n_pair = m0.shape[0]
    og, m_new = _rwkv_pair(pp, m0, tuple(jnp.concatenate([p[None] for p in _split_lanes(a, n_pair)], axis=0) for a in xs))
    return og, m_new, xs[2]


def _group(n):
    return n


def _rwkv_specs(layer1, t, dr, rwc, n_pair, rev):
    nc = t // CHUNK
    grp = _group(n_pair)

    def cidx(c):
        return (nc - 1 - c) if rev else c

    full = lambda shape: pl.BlockSpec(shape, lambda c, p: tuple(0 for _ in shape))
    specs = [
        pl.BlockSpec((CHUNK, rwc), lambda c, p: (cidx(c), 0)),
        pl.BlockSpec((8, rwc), lambda c, p: (jnp.maximum(cidx(c) * (CHUNK // 8) - 1, 0), 0)),
    ]
    if layer1:
        specs.append(pl.BlockSpec((CHUNK, dr), lambda c, p: (cidx(c), 0)))
    prm_shapes = [(1, rwc), (1, dr), (1, dr), (LANES, dr), (LANES, dr)]
    if layer1:
        prm_shapes += [(1, dr), (dr, LANES), (LANES, dr)]
    specs += [full(s) for s in prm_shapes]
    specs.append(pl.BlockSpec((grp, 8, LANES), lambda c, p: (p, 0, 0)))
    return specs, prm_shapes, cidx, full


def _rwkv_fwd(layer1, proj, vf, prm, pp, cat_width):
    t = proj.shape[0]
    dr = prm[1].shape[1]
    rwc = prm[0].shape[1]
    n_pair = dr // LANES
    nc = t // CHUNK
    n_prm = len(prm)
    specs, _, _, _ = _rwkv_specs(layer1, t, dr, rwc, n_pair, False)

    def body(*refs):
        y_ref, prev_ref = refs[0], refs[1]
        i = 2
        vf_ref = None
        if layer1:
            vf_ref = refs[i]
            i += 1
        prm_refs = refs[i:i + n_prm]
        i += n_prm
        pp_ref = refs[i]
        i += 1
        cat_ref = refs[i]
        i += 1
        vout_ref = None
        if not layer1:
            vout_ref = refs[i]
            i += 1
        mck_ref, m_s = refs[i], refs[i + 1]
        c = pl.program_id(0)

        @pl.when(c == 0)
        def _():
            m_s[...] = jnp.zeros_like(m_s)

        prev = prev_ref[pl.ds(7, 1), :] * (c != 0).astype(f32)
        m0 = m_s[...]
        mck_ref[0] = m0
        ppv = tuple(pp_ref[:, pl.ds(q, 1), :] for q in range(5))
        og, m_new, v = _rwkv_step(layer1, tuple(r[...] for r in prm_refs), y_ref[...], prev,
                                  vf_ref[...] if layer1 else None, ppv, m0)
        if not layer1:
            vout_ref[...] = v
        for j in range(n_pair):
            cat_ref[:, j * LANES:(j + 1) * LANES] = og[j]
        m_s[...] = m_new

    grp = _group(n_pair)
    assert grp == n_pair
    out_shape = [jax.ShapeDtypeStruct((t, cat_width), f32)]
    out_specs = [pl.BlockSpec((CHUNK, grp * LANES), lambda c, p: (c, p))]
    if not layer1:
        out_shape.append(jax.ShapeDtypeStruct((t, dr), f32))
        out_specs.append(pl.BlockSpec((CHUNK, dr), lambda c, p: (c, 0)))
    out_shape.append(jax.ShapeDtypeStruct((nc, n_pair, LANES, LANES), f32))
    out_specs.append(pl.BlockSpec((1, grp, LANES, LANES), lambda c, p: (c, p, 0, 0)))
    args = [proj, proj] + ([vf] if layer1 else []) + list(prm) + [pp]
    return pl.pallas_call(
        body, grid=(nc, 1), in_specs=specs, out_specs=out_specs, out_shape=out_shape,
        scratch_shapes=[pltpu.VMEM((n_pair, LANES, LANES), f32)],
        compiler_params=pltpu.CompilerParams(dimension_semantics=("arbitrary", "arbitrary")),
        name=f"rwkv_fwd_l{int(layer1)}",
    )(*args)


def _rwkv_bwd(layer1, proj, vf, prm, pp, mck, dcat, dvout):
    t = proj.shape[0]
    dr = prm[1].shape[1]
    rwc = prm[0].shape[1]
    n_pair = dr // LANES
    nc = t // CHUNK
    n_prm = len(prm)
    specs, prm_shapes, cidx, full = _rwkv_specs(layer1, t, dr, rwc, n_pair, True)
    grp = _group(n_pair)
    assert grp == n_pair
    specs.append(pl.BlockSpec((1, grp, LANES, LANES), lambda c, p: (cidx(c), p, 0, 0)))
    specs.append(pl.BlockSpec((CHUNK, grp * LANES), lambda c, p: (cidx(c), p)))
    if not layer1:
        specs.append(pl.BlockSpec((CHUNK, dr), lambda c, p: (cidx(c), 0)))

    def body(*refs):
        y_ref, prev_ref = refs[0], refs[1]
        i = 2
        vf_ref = None
        if layer1:
            vf_ref = refs[i]
            i += 1
        prm_refs = refs[i:i + n_prm]
        i += n_prm
        pp_ref, mck_ref, dog_ref = refs[i], refs[i + 1], refs[i + 2]
        i += 3
        dvout_ref = None
        if not layer1:
            dvout_ref = refs[i]
            i += 1
        dy_ref = refs[i]
        i += 1
        dvf_ref = None
        if layer1:
            dvf_ref = refs[i]
            i += 1
        dprm_refs = refs[i:i + n_prm]
        i += n_prm
        dpp_ref = refs[i]
        dm_s, dprev_s = refs[i + 1:i + 3]
        c = pl.program_id(0)
        cr = nc - 1 - c

        @pl.when(c == 0)
        def _():
            dm_s[...] = jnp.zeros_like(dm_s)
            dprev_s[...] = jnp.zeros_like(dprev_s)
            dpp_ref[...] = jnp.zeros_like(dpp_ref)
            for r in dprm_refs:
                r[...] = jnp.zeros_like(r)

        prev = prev_ref[pl.ds(7, 1), :] * (cr != 0).astype(f32)
        prm_v = tuple(r[...] for r in prm_refs)
        ppv = tuple(pp_ref[:, pl.ds(q, 1), :] for q in range(5))
        dog = jnp.stack([dog_ref[:, j * LANES:(j + 1) * LANES] for j in range(n_pair)], axis=0)
        if layer1:
            _, vjp = jax.vjp(functools.partial(_rwkv_step, True), prm_v, y_ref[...], prev, vf_ref[...], ppv, mck_ref[0])
            dprm, dy, dprev, dvf, dppv, dm0 = vjp((dog, dm_s[...], jnp.zeros((CHUNK, dr), f32)))
            dvf_ref[...] = dvf
        else:
            _, vjp = jax.vjp(lambda a, b, d, e, g: _rwkv_step(False, a, b, d, None, e, g), prm_v, y_ref[...], prev, ppv, mck_ref[0])
            dprm, dy, dprev, dppv, dm0 = vjp((dog, dm_s[...], dvout_ref[...]))
        dm_s[...] = dm0
        for q in range(5):
            dpp_ref[:, pl.ds(q, 1), :] += dppv[q]
        dy_ref[...] = (dy + jnp.where(_iota((CHUNK, 1), 0) == CHUNK - 1, dprev_s[...], 0.0)).astype(bf16)
        dprev_s[...] = dprev
        for r, gval in zip(dprm_refs, dprm):
            r[...] += gval

    out_shape = [jax.ShapeDtypeStruct((t, rwc), bf16)]
    out_specs = [pl.BlockSpec((CHUNK, rwc), lambda c, p: (cidx(c), 0))]
    if layer1:
        out_shape.append(jax.ShapeDtypeStruct((t, dr), f32))
        out_specs.append(pl.BlockSpec((CHUNK, dr), lambda c, p: (cidx(c), 0)))
    out_shape += [jax.ShapeDtypeStruct(s, f32) for s in prm_shapes]
    out_specs += [full(s) for s in prm_shapes]
    out_shape.append(jax.ShapeDtypeStruct((n_pair, 8, LANES), f32))
    out_specs.append(full((n_pair, 8, LANES)))
    args = [proj, proj] + ([vf] if layer1 else []) + list(prm) + [pp, mck, dcat] + ([] if layer1 else [dvout])
    return pl.pallas_call(
        body, grid=(nc, 1), in_specs=specs, out_specs=out_specs, out_shape=out_shape,
        scratch_shapes=[pltpu.VMEM((n_pair, LANES, LANES), f32), pltpu.VMEM((1, rwc), f32)],
        compiler_params=pltpu.CompilerParams(dimension_semantics=("arbitrary", "arbitrary")),
        name=f"rwkv_bwd_l{int(layer1)}",
    )(*args)


def _hgrn_chunk(layer1, lbl, gnw, s0, q_raw, f_raw, i_in, z):
    c = q_raw.shape[-2]
    q = _silu(q_raw)
    ls = _log_sigmoid(f_raw)
    if layer1:
        l0, l1 = lbl[..., 0:1, :], lbl[..., 1:2, :]
        mx = jnp.maximum(l0, l1)
        e0, e1 = jnp.exp(l0 - mx), jnp.exp(l1 - mx)
        sm0, sm1 = e0 / (e0 + e1), e1 / (e0 + e1)
        lb = (sm0 + sm1) - sm0
        log_f = _logaddexp(jnp.log(jnp.maximum(lb, LB_FLOOR)), jnp.log1p(-lb) + ls)
        k = (1.0 - lb) * jax.nn.sigmoid(-f_raw)
    else:
        log_f = _logaddexp(jnp.full_like(ls, jnp.log(jnp.float32(LB_FLOOR))), ls)
        k = jax.nn.sigmoid(-f_raw)
    row, col = _iota((c, c), 0), _iota((c, c), 1)
    trow = _iota((c, 1), 0)
    halves = []
    half = c // 2
    while half >= 1:
        halves.append(half)
        half //= 2
    cmat = jnp.concatenate([(col <= row).astype(f32)]
                           + [(col <= (row // (2 * hf)) * (2 * hf) + hf - 1).astype(f32) for hf in halves], axis=0)
    ball = _const_left(cmat.astype(bf16), log_f)
    b = ball[..., :c, :]
    att = None
    for lvl, hf in enumerate(halves):
        blk = 2 * hf
        bref = ball[..., (lvl + 1) * c:(lvl + 2) * c, :]
        upper = (trow % blk) >= hf
        qh = q * jnp.exp(jnp.where(upper, b - bref, 0.0)) * upper.astype(f32)
        kh = k * jnp.exp(jnp.where(upper, 0.0, bref - b)) * (1.0 - upper.astype(f32))
        term = jnp.where(row // blk == col // blk, _mm2(qh, kh, "nt", APPLY_PASSES), 0.0)
        att = term if att is None else att + term
    lhs = jnp.concatenate([q * jnp.exp(b), att, jnp.zeros(att.shape[:-1] + (LANES - c,), f32)], axis=-1)
    rhs = jnp.concatenate([s0, i_in, jnp.zeros(i_in.shape[:-2] + (LANES - c, i_in.shape[-1]), f32)], axis=-2)
    o = _mm2(lhs, rhs, "nn", APPLY_PASSES) + jnp.sum(q * k, axis=-1, keepdims=True) * i_in
    b_last = _last_row(b)
    s_new = _col_of_row(jnp.exp(b_last)) * s0 + _mm2(k * jnp.exp(b_last - b), i_in, "tn", APPLY_PASSES)
    o = o * lax.rsqrt(jnp.mean(o * o, axis=-1, keepdims=True) + RMS_EPS)
    return o * gnw * _silu(z), s_new


def _hgrn_in_specs(t, dh, col0, rev):
    nc = t // CHUNK
    nh = dh // LANES

    def cidx(c):
        return (nc - 1 - c) if rev else c

    grp = _group(nh)
    specs = [pl.BlockSpec((CHUNK, LANES), functools.partial(lambda g, j, h, c: (cidx(c), col0 + g * nh + h * grp + j), g, j))
             for j in range(grp) for g in range(4)]
    specs.append(pl.BlockSpec((2, grp * LANES), lambda h, c: (0, h)))
    specs.append(pl.BlockSpec((1, grp * LANES), lambda h, c: (0, h)))
    return specs, cidx, grp


def _hgrn_fwd(layer1, proj, lbl, gnw, cat, rwc):
    t, d = cat.shape
    dh = gnw.shape[1]
    nh = dh // LANES
    nc = t // CHUNK
    col0 = rwc // LANES
    specs, _, grp = _hgrn_in_specs(t, dh, col0, False)
    specs.append(pl.BlockSpec(memory_space=pl.ANY))
    assert (d - dh) % (grp * LANES) == 0
    cat_col0 = (d - dh) // (grp * LANES)

    def body(*refs):
        x_refs = refs[:4 * grp]
        lbl_ref, gnw_ref, _, cat_ref, sck_ref, s_s = refs[4 * grp:]
        c = pl.program_id(1)

        @pl.when(c == 0)
        def _():
            s_s[...] = jnp.zeros_like(s_s)

        lanes = [slice(j * LANES, (j + 1) * LANES) for j in range(grp)]
        s0 = s_s[...]
        sck_ref[:, 0] = s0
        out, s_new = _hgrn_chunk(layer1, jnp.stack([lbl_ref[:, ln] for ln in lanes]), jnp.stack([gnw_ref[:, ln] for ln in lanes]),
                                 s0, *(jnp.stack([x_refs[4 * j + g][...] for j in range(grp)]) for g in range(4)))
        for j in range(grp):
            cat_ref[:, lanes[j]] = out[j]
        s_s[...] = s_new

    return pl.pallas_call(
        body, grid=(nh // grp, nc), in_specs=specs,
        out_specs=[pl.BlockSpec((CHUNK, grp * LANES), lambda h, c: (c, cat_col0 + h)),
                   pl.BlockSpec((grp, 1, LANES, LANES), lambda h, c: (h, c, 0, 0))],
        out_shape=[jax.ShapeDtypeStruct((t, d), f32), jax.ShapeDtypeStruct((nh, nc, LANES, LANES), f32)],
        scratch_shapes=[pltpu.VMEM((grp, LANES, LANES), f32)],
        input_output_aliases={4 * grp + 2: 0},
        compiler_params=pltpu.CompilerParams(dimension_semantics=("arbitrary", "arbitrary")),
        name=f"hgrn_fwd_l{int(layer1)}",
    )(*([proj] * (4 * grp)), lbl, gnw, cat)


def _hgrn_bwd(layer1, proj, lbl, gnw, sck, dcat, rwc):
    t, d = dcat.shape
    dh = gnw.shape[1]
    nh = dh // LANES
    nc = t // CHUNK
    col0 = rwc // LANES
    specs, cidx, grp = _hgrn_in_specs(t, dh, col0, True)
    assert (d - dh) % (grp * LANES) == 0
    cat_col0 = (d - dh) // (grp * LANES)
    specs.append(pl.BlockSpec((grp, 1, LANES, LANES), lambda h, c: (h, cidx(c), 0, 0)))
    specs.append(pl.BlockSpec((CHUNK, grp * LANES), lambda h, c: (cidx(c), cat_col0 + h)))

    def body(*refs):
        x_refs = refs[:4 * grp]
        lbl_ref, gnw_ref, sck_ref, do_ref, dp_ref, dlbl_ref, dgnw_ref, ds_s = refs[4 * grp:]
        c = pl.program_id(1)

        @pl.when(c == 0)
        def _():
            ds_s[...] = jnp.zeros_like(ds_s)
            dlbl_ref[...] = jnp.zeros_like(dlbl_ref)
            dgnw_ref[...] = jnp.zeros_like(dgnw_ref)

        lanes = [slice(j * LANES, (j + 1) * LANES) for j in range(grp)]
        _, vjp = jax.vjp(functools.partial(_hgrn_chunk, layer1),
                         jnp.stack([lbl_ref[:, ln] for ln in lanes]), jnp.stack([gnw_ref[:, ln] for ln in lanes]), sck_ref[:, 0],
                         *(jnp.stack([x_refs[4 * j + g][...] for j in range(grp)]) for g in range(4)))
        dlbl, dgnw, ds0, dq, df, di, dz = vjp((jnp.stack([do_ref[:, ln] for ln in lanes]), ds_s[...]))
        ds_s[...] = ds0
        for j in range(grp):
            dlbl_ref[:, lanes[j]] += dlbl[j]
            dgnw_ref[:, lanes[j]] += dgnw[j]
            for g, val in enumerate((dq, df, di, dz)):
                dp_ref[g, :, lanes[j]] = val[j].astype(bf16)

    return pl.pallas_call(
        body, grid=(nh // grp, nc), in_specs=specs,
        out_specs=[pl.BlockSpec((4, CHUNK, grp * LANES), lambda h, c: (0, cidx(c), h)),
                   pl.BlockSpec((2, grp * LANES), lambda h, c: (0, h)),
                   pl.BlockSpec((1, grp * LANES), lambda h, c: (0, h))],
        out_shape=[jax.ShapeDtypeStruct((4, t, dh), bf16), jax.ShapeDtypeStruct((2, dh), f32),
                   jax.ShapeDtypeStruct((1, dh), f32)],
        scratch_shapes=[pltpu.VMEM((grp, LANES, LANES), f32)],
        compiler_params=pltpu.CompilerParams(dimension_semantics=("arbitrary", "arbitrary")),
        name=f"hgrn_bwd_l{int(layer1)}",
    )(*([proj] * (4 * grp)), lbl, gnw, sck, dcat)


def _ln(h, y, w, b):
    u = ALPHA * h + y
    mu = jnp.mean(u, axis=-1, keepdims=True)
    var = jnp.mean(jnp.square(u - mu), axis=-1, keepdims=True)
    return (u - mu) * lax.rsqrt(var + LN_EPS) * w + b


def _row_tile(t):
    return 256 if t % 256 == 0 else t


def _ln_fwd(h, y, w, b):
    t, d = h.shape
    tr = _row_tile(t)

    def body(h_ref, y_ref, w_ref, b_ref, o_ref, o16_ref):
        out = _ln(h_ref[...], y_ref[...], w_ref[...], b_ref[...])
        o_ref[...] = out
        o16_ref[...] = out.astype(bf16)

    row = pl.BlockSpec((tr, d), lambda i: (i, 0))
    vec = pl.BlockSpec((1, d), lambda i: (0, 0))
    return pl.pallas_call(body, grid=(t // tr,), in_specs=[row, row, vec, vec], out_specs=[row, row],
                          out_shape=[jax.ShapeDtypeStruct((t, d), f32), jax.ShapeDtypeStruct((t, d), bf16)],
                          name="ln_fwd")(h, y, w, b)


def _ln_loss(h, y, w, b, tgt):
    t, d = h.shape
    tr = _row_tile(t)

    def body(h_ref, y_ref, w_ref, b_ref, t_ref, g_ref, loss_ref):
        @pl.when(pl.program_id(0) == 0)
        def _():
            loss_ref[...] = jnp.zeros_like(loss_ref)

        err = _ln(h_ref[...], y_ref[...], w_ref[...], b_ref[...]) - t_ref[...]
        g_ref[...] = err * (1.0 / d)
        loss_ref[...] += 0.5 * jnp.sum(jnp.mean(jnp.square(err), axis=-1, keepdims=True), axis=0, keepdims=True)

    row = pl.BlockSpec((tr, d), lambda i: (i, 0))
    vec = pl.BlockSpec((1, d), lambda i: (0, 0))
    return pl.pallas_call(
        body, grid=(t // tr,), in_specs=[row, row, vec, vec, row],
        out_specs=[row, pl.BlockSpec((1, LANES), lambda i: (0, 0))],
        out_shape=[jax.ShapeDtypeStruct((t, d), f32), jax.ShapeDtypeStruct((1, LANES), f32)],
        compiler_params=pltpu.CompilerParams(dimension_semantics=("arbitrary",)), name="ln_loss")(h, y, w, b, tgt)


def _ln_bwd(h, y, w, b, dout):
    t, d = h.shape
    tr = _row_tile(t)

    def body(h_ref, y_ref, w_ref, b_ref, do_ref, dy_ref, dy16_ref, dw_ref, db_ref):
        @pl.when(pl.program_id(0) == 0)
        def _():
            dw_ref[...] = jnp.zeros_like(dw_ref)
            db_ref[...] = jnp.zeros_like(db_ref)

        _, vjp = jax.vjp(lambda yy, ww, bb: _ln(h_ref[...], yy, ww, bb), y_ref[...], w_ref[...], b_ref[...])
        dy, dw, db = vjp(do_ref[...])
        dy_ref[...] = dy
        dy16_ref[...] = dy.astype(bf16)
        dw_ref[...] += dw
        db_ref[...] += db

    row = pl.BlockSpec((tr, d), lambda i: (i, 0))
    vec = pl.BlockSpec((1, d), lambda i: (0, 0))
    return pl.pallas_call(
        body, grid=(t // tr,), in_specs=[row, row, vec, vec, row], out_specs=[row, row, vec, vec],
        out_shape=[jax.ShapeDtypeStruct((t, d), f32), jax.ShapeDtypeStruct((t, d), bf16),
                   jax.ShapeDtypeStruct((1, d), f32), jax.ShapeDtypeStruct((1, d), f32)],
        compiler_params=pltpu.CompilerParams(dimension_semantics=("arbitrary",)), name="ln_bwd")(h, y, w, b, dout)


def _pick(n, prefs):
    for p in prefs:
        if n % p == 0:
            return p
    return n


def _tile(n, want):
    if n <= want:
        return n
    for cand in range(want - want % LANES, 0, -LANES):
        if n % cand == 0:
            return cand
    return n


def _matmul(a, b, mode, name, tiles, add=None, add_scale=1.0, out_dtype=f32):
    if mode == "nn":
        (m, k), n = a.shape, b.shape[1]
    elif mode == "nt":
        (m, k), n = a.shape, b.shape[0]
    else:
        (k, m), n = a.shape, b.shape[1]
    tm, tn, tk = _tile(m, tiles[0]), _tile(n, tiles[1]), _tile(k, tiles[2])
    nk = k // tk
    cache_a = nk == 1 and a.dtype != bf16 and n // tn > 1

    def body(*refs):
        a_ref, b_ref = refs[0], refs[1]
        add_ref = refs[2] if add is not None else None
        n_in = 3 if add is not None else 2
        o_ref = refs[n_in]
        scratch = refs[n_in + 1:]

        def finish(res):
            if add is not None:
                res = res + add_scale * add_ref[...]
            o_ref[...] = res.astype(out_dtype)

        if cache_a:
            a_bf = scratch[0]

            @pl.when(pl.program_id(1) == 0)
            def _():
                a_bf[...] = a_ref[...].astype(bf16)

            a_val = a_bf[...]
        else:
            a_val = a_ref[...].astype(bf16)
        prod = lax.dot_general(a_val, b_ref[...].astype(bf16), _DIMS[mode], preferred_element_type=f32)
        if nk == 1:
            finish(prod)
        else:
            acc = scratch[-1]
            kk = pl.program_id(2)

            @pl.when(kk == 0)
            def _():
                acc[...] = prod

            @pl.when(kk != 0)
            def _():
                acc[...] += prod

            @pl.when(kk == nk - 1)
            def _():
                finish(acc[...])

    a_shape = (tk, tm) if mode == "tn" else (tm, tk)
    a_spec = pl.BlockSpec(a_shape, (lambda i, j, kk: (kk, i)) if mode == "tn" else (lambda i, j, kk: (i, kk)))
    b_spec = pl.BlockSpec((tn, tk), lambda i, j, kk: (j, kk)) if mode == "nt" else pl.BlockSpec((tk, tn), lambda i, j, kk: (kk, j))
    o_spec = pl.BlockSpec((tm, tn), lambda i, j, kk: (i, j))
    in_specs = [a_spec, b_spec] + ([o_spec] if add is not None else [])
    args = [a, b] + ([add] if add is not None else [])
    scratch_shapes = ([pltpu.VMEM(a_shape, bf16)] if cache_a else []) + ([pltpu.VMEM((tm, tn), f32)] if nk > 1 else [])
    return pl.pallas_call(
        body, grid=(m // tm, n // tn, nk), in_specs=in_specs, out_specs=o_spec,
        out_shape=jax.ShapeDtypeStruct((m, n), out_dtype), scratch_shapes=scratch_shapes,
        compiler_params=pltpu.CompilerParams(dimension_semantics=("parallel", "arbitrary", "arbitrary")),
        name=name,
    )(*args)


def _position():
    return lax.axis_index("x"), lax.axis_index("y"), lax.axis_index("c")


def _flip(pos, k):
    x, y, c = pos
    return (1 - x if k & 4 else x, 1 - y if k & 2 else y, 1 - c if k & 1 else c)


def _index(pos):
    return 4 * pos[0] + 2 * pos[1] + pos[2]


def _all_gather_rows(x, name):
    m_per, n = x.shape

    def body(x_ref, out_ref, send_sems, recv_sems, local_sem):
        me = _position()
        sibling = _flip(me, 1)
        chips = (2, 4, 6)

        def rows(pos):
            return out_ref.at[pl.ds(_index(pos) * m_per, m_per), :]

        def copy(sem, block, to, src=None):
            return pltpu.make_async_remote_copy(
                src_ref=rows(block) if src is None else src, dst_ref=rows(block),
                send_sem=send_sems.at[sem], recv_sem=recv_sems.at[sem], device_id=to, device_id_type=MESH)

        mine = pltpu.make_async_copy(x_ref, rows(me), local_sem)
        mine.start()
        first = [copy(0, me, sibling, src=x_ref)]
        first += [copy(1 + j, me, _flip(me, k), src=x_ref) for j, k in enumerate(chips)]
        for cp in first:
            cp.start()
        passed = [copy(4 + j, _flip(me, k), sibling) for j, k in enumerate(chips)]
        for j, k in enumerate(chips):
            copy(1 + j, _flip(me, k), me).wait_recv()
            passed[j].start()
        copy(0, sibling, me).wait_recv()
        for j, k in enumerate(chips):
            copy(4 + j, _flip(sibling, k), me).wait_recv()
        for cp in first + passed:
            cp.wait_send()
        mine.wait()

    return pl.pallas_call(
        body, out_shape=jax.ShapeDtypeStruct((N_DEV * m_per, n), x.dtype),
        in_specs=[pl.BlockSpec(memory_space=pl.ANY)], out_specs=pl.BlockSpec(memory_space=pl.ANY),
        scratch_shapes=[pltpu.SemaphoreType.DMA((7,)), pltpu.SemaphoreType.DMA((7,)), pltpu.SemaphoreType.DMA(())],
        name=name,
    )(x)


def _split_start(srcs, lands, plan, n_copies, name):
    n_arr = len(srcs)
    hbm = pl.BlockSpec(memory_space=pltpu.HBM)
    sem = pl.BlockSpec(memory_space=pltpu.SEMAPHORE)

    def body(*refs):
        src_refs, land_refs = refs[:n_arr], refs[n_arr:2 * n_arr]
        send_sems, recv_sems = refs[2 * n_arr:3 * n_arr], refs[3 * n_arr:4 * n_arr]
        token = refs[-1]
        me = _position()
        for i in range(n_arr):
            for j, (src, dst, peer, _) in enumerate(plan(i, src_refs[i], land_refs[i], me)):
                pltpu.make_async_remote_copy(src_ref=src, dst_ref=dst, send_sem=send_sems[i].at[j], recv_sem=recv_sems[i].at[j],
                                             device_id=peer, device_id_type=MESH).start()
        token[...] = jnp.zeros_like(token)

    outs = pl.pallas_call(
        body, name=name,
        out_shape=([pltpu.SemaphoreType.DMA((n_copies,))] * (2 * n_arr)
                   + [pltpu.HBM(a.shape, a.dtype) for a in list(srcs) + list(lands)]
                   + [jax.ShapeDtypeStruct((8, LANES), f32)]),
        in_specs=[hbm] * (2 * n_arr),
        out_specs=[sem] * (2 * n_arr) + [hbm] * (2 * n_arr) + [pl.BlockSpec(memory_space=pltpu.VMEM)],
        input_output_aliases={i: 2 * n_arr + i for i in range(2 * n_arr)},
        compiler_params=pltpu.CompilerParams(has_side_effects=pltpu.SideEffectType.DATAFLOW_SIDE_EFFECTING),
    )(*[pltpu.with_memory_space_constraint(a, pltpu.HBM) for a in list(srcs) + list(lands)])
    return (outs[:n_arr], outs[n_arr:2 * n_arr], outs[2 * n_arr:3 * n_arr], outs[3 * n_arr:4 * n_arr], outs[-1])


def _split_wait(started, plan, after, name):
    send_sems, recv_sems, srcs, lands, _ = started
    n_arr = len(srcs)
    hbm = pl.BlockSpec(memory_space=pltpu.HBM)
    sem = pl.BlockSpec(memory_space=pltpu.SEMAPHORE)

    def body(*refs):
        src_refs, land_refs = refs[:n_arr], refs[n_arr:2 * n_arr]
        s_sems, r_sems = refs[2 * n_arr:3 * n_arr], refs[3 * n_arr:4 * n_arr]
        me = _position()
        for i in range(n_arr):
            for j, (src, _, peer, arrival) in enumerate(plan(i, src_refs[i], land_refs[i], me)):
                cp = pltpu.make_async_remote_copy(src_ref=src, dst_ref=arrival, send_sem=s_sems[i].at[j], recv_sem=r_sems[i].at[j],
                                                  device_id=peer, device_id_type=MESH)
                cp.wait_send()
                cp.wait_recv()

    outs = pl.pallas_call(
        body, name=name,
        out_shape=[pltpu.HBM(a.shape, a.dtype) for a in list(srcs) + list(lands)],
        in_specs=[hbm] * (2 * n_arr) + [sem] * (2 * n_arr) + [pl.BlockSpec(memory_space=pl.ANY)],
        out_specs=[hbm] * (2 * n_arr),
        input_output_aliases={i: i for i in range(2 * n_arr)},
        compiler_params=pltpu.CompilerParams(has_side_effects=pltpu.SideEffectType.DATAFLOW_SIDE_EFFECTING),
    )(*srcs, *lands, *send_sems, *recv_sems, after)
    return outs[:n_arr], outs[n_arr:]


_GATHER_FLIPS = (1, 2, 4, 6)


def _gather_plan(i, src_ref, land_ref, me):
    m = src_ref.shape[0]

    def rows(pos):
        return land_ref.at[pl.ds(_index(pos) * m, m), :]

    return [(src_ref, rows(me), _flip(me, k), rows(_flip(me, k))) for k in _GATHER_FLIPS]


def _gather_forward(lands, name):
    n_arr = len(lands)
    chips = (2, 4, 6)

    def body(*refs):
        out_refs = refs[n_arr:2 * n_arr]
        send_sems, recv_sems = refs[2 * n_arr:]
        me = _position()
        sibling = _flip(me, 1)
        sends, arrivals = [], []
        for i, out_ref in enumerate(out_refs):
            m = out_ref.shape[0] // N_DEV

            def copy(pos, j):
                blk = out_ref.at[pl.ds(_index(pos) * m, m), :]
                return pltpu.make_async_remote_copy(src_ref=blk, dst_ref=blk, send_sem=send_sems.at[3 * i + j],
                                                    recv_sem=recv_sems.at[3 * i + j], device_id=sibling, device_id_type=MESH)

            for j, k in enumerate(chips):
                sends.append(copy(_flip(me, k), j))
                arrivals.append(copy(_flip(sibling, k), j))
        for cp in sends:
            cp.start()
        for cp in arrivals:
            cp.wait_recv()
        for cp in sends:
            cp.wait_send()

    anyspec = pl.BlockSpec(memory_space=pl.ANY)
    return pl.pallas_call(
        body, out_shape=[jax.ShapeDtypeStruct(a.shape, a.dtype) for a in lands],
        in_specs=[anyspec] * n_arr, out_specs=[anyspec] * n_arr, input_output_aliases={i: i for i in range(n_arr)},
        scratch_shapes=[pltpu.SemaphoreType.DMA((3 * n_arr,))] * 2, name=name,
    )(*lands)


def _chips_plan(i, src_ref, land_ref, me):
    m = src_ref.shape[0] // 4
    plan = []
    for j, k in enumerate((2, 4, 6)):
        peer = _flip(me, k)
        plan.append((src_ref.at[pl.ds((2 * peer[0] + peer[1]) * m, m), :], land_ref.at[j], peer, land_ref.at[j]))
    return plan


def _exchange_siblings(gs, name):
    n_arr = len(gs)

    def body(*refs):
        g_refs, out_refs = refs[:n_arr], refs[n_arr:2 * n_arr]
        send_sems, recv_sems = refs[2 * n_arr:]
        me = _position()
        c = me[2]
        sibling = _flip(me, 1)
        copies = []
        for i, (g_ref, out_ref) in enumerate(zip(g_refs, out_refs)):
            m_per = g_ref.shape[0] // N_DEV
            for q in range(4):
                copies.append(pltpu.make_async_remote_copy(
                    src_ref=g_ref.at[pl.ds((2 * q + 1 - c) * m_per, m_per), :], dst_ref=out_ref.at[q],
                    send_sem=send_sems.at[4 * i + q], recv_sem=recv_sems.at[4 * i + q],
                    device_id=sibling, device_id_type=MESH))
        for cp in copies:
            cp.start()
        for cp in copies:
            cp.wait_recv()
        for cp in copies:
            cp.wait_send()

    anyspec = pl.BlockSpec(memory_space=pl.ANY)
    return pl.pallas_call(
        body, out_shape=[jax.ShapeDtypeStruct((4, g.shape[0] // N_DEV, g.shape[1]), g.dtype) for g in gs],
        in_specs=[anyspec] * n_arr, out_specs=[anyspec] * n_arr,
        scratch_shapes=[pltpu.SemaphoreType.DMA((4 * n_arr,))] * 2, name=name,
    )(*gs)


def _sum_with_sibling(g, recv, name):
    m = g.shape[0] // N_DEV
    n = g.shape[1]
    tr = _pick(m, (208, 128, 64, 32, 16))
    nt = m // tr

    def body(g_ref, r_ref, o_ref):
        c = lax.axis_index("c")
        own = jnp.where(c == 0, g_ref[0, 0].astype(f32), g_ref[0, 1].astype(f32))
        o_ref[...] = (own + r_ref[0].astype(f32)).astype(o_ref.dtype)

    return pl.pallas_call(
        body, grid=(4, nt),
        in_specs=[pl.BlockSpec((1, 2, tr, n), lambda q, i: (q, 0, i, 0)), pl.BlockSpec((1, tr, n), lambda q, i: (q, i, 0))],
        out_specs=pl.BlockSpec((tr, n), lambda q, i: (q * nt + i, 0)),
        out_shape=jax.ShapeDtypeStruct((4 * m, n), bf16), name=name,
    )(g.reshape(4, 2, m, n), recv)


def _sum_with_chips(h, recv, name):
    m = h.shape[0] // 4
    n = h.shape[1]
    tr = _pick(m, (208, 128, 64, 32, 16))

    def body(h_ref, r_ref, o_ref):
        my_q = 2 * lax.axis_index("x") + lax.axis_index("y")
        own = h_ref[0].astype(f32)
        for q in range(1, 4):
            own = jnp.where(my_q == q, h_ref[q].astype(f32), own)
        o_ref[...] = ((own + r_ref[0].astype(f32)) + r_ref[1].astype(f32)) + r_ref[2].astype(f32)

    return pl.pallas_call(
        body, grid=(m // tr,),
        in_specs=[pl.BlockSpec((4, tr, n), lambda i: (0, i, 0)), pl.BlockSpec((3, tr, n), lambda i: (0, i, 0))],
        out_specs=pl.BlockSpec((tr, n), lambda i: (i, 0)), out_shape=jax.ShapeDtypeStruct((m, n), f32), name=name,
    )(h.reshape(4, m, n), recv)


def _sum_slots(parts, name):
    n_slot, m, n = parts.shape
    tr = _pick(m, (208, 128, 64, 32, 16, 8))

    def body(p_ref, o_ref):
        acc = p_ref[0]
        for s in range(1, n_slot):
            acc = acc + p_ref[s]
        o_ref[...] = acc

    return pl.pallas_call(
        body, grid=(m // tr,), in_specs=[pl.BlockSpec((n_slot, tr, n), lambda i: (0, i, 0))],
        out_specs=pl.BlockSpec((tr, n), lambda i: (i, 0)), out_shape=jax.ShapeDtypeStruct((m, n), parts.dtype), name=name,
    )(parts)


def _reduce_scatter_begin(gs, name):
    from_sibling = _exchange_siblings(gs, "rs_d2d_" + name)
    chip_sums = [_sum_with_sibling(g, r, f"rs_sum2_{name}_{i}") for i, (g, r) in enumerate(zip(gs, from_sibling))]
    lands = [lax.empty((3, h.shape[0] // 4, h.shape[1]), h.dtype) for h in chip_sums]
    return _split_start(chip_sums, lands, _chips_plan, 3, "rs_ici_start_" + name)


def _reduce_scatter_end(started, after, name):
    chip_sums, from_chips = _split_wait(started, _chips_plan, after, "rs_ici_wait_" + name)
    return [_sum_with_chips(h, r, f"rs_sum4_{name}_{i}") for i, (h, r) in enumerate(zip(chip_sums, from_chips))]


def _adamw(w, g, m, v, name):
    shape = w.shape
    n = shape[-1]
    r = w.size // n
    w2, g2, m2, v2 = (a.reshape(r, n) for a in (w, g, m, v))
    tr = _pick(r, (256, 128, 64, 32, 16, 8)) if r * n > 65536 else r

    def body(w_ref, g_ref, m_ref, v_ref, d_ref, mo_ref, vo_ref):
        gg = g_ref[...]
        mm = ADAM_B1 * m_ref[...] + (1.0 - ADAM_B1) * gg
        vv = ADAM_B2 * v_ref[...] + (1.0 - ADAM_B2) * jnp.square(gg)
        m_hat = mm / (1.0 - ADAM_B1 ** ADAM_STEP)
        v_hat = vv / (1.0 - ADAM_B2 ** ADAM_STEP)
        d_ref[...] = -ADAM_LR * (m_hat / (jnp.sqrt(v_hat) + ADAM_EPS) + ADAM_WD * w_ref[...])
        mo_ref[...] = mm
        vo_ref[...] = vv

    spec = pl.BlockSpec((tr, n), lambda i: (i, 0))
    outs = pl.pallas_call(
        body, grid=(r // tr,), in_specs=[spec] * 4, out_specs=[spec] * 3,
        out_shape=[jax.ShapeDtypeStruct((r, n), f32)] * 3, name=name,
    )(w2, g2, m2, v2)
    return tuple(o.reshape(shape) for o in outs)


_SMALL = ("shift_mu", "w_decay0", "a0", "k_k", "k_a", "r_k", "ln_x_w", "ln_x_b", "v_mix0", "lb_logits",
          "g_norm_w", "ln_w", "ln_b")
_NAMES = ("w_in", "shift_mu", "w_decay0", "w_decay_up", "a0", "a_up", "k_k", "k_a", "r_k", "ln_x_w", "ln_x_b",
          "v_mix0", "v_mix_down", "v_mix_up", "lb_logits", "g_norm_w", "w_out", "ln_w", "ln_b")


def _pad_rows(a, rows, at_end):
    z = jnp.zeros((rows - a.shape[0], a.shape[1]), a.dtype)
    return jnp.concatenate([a, z] if at_end else [z, a], axis=0)


def kernel(x, w_in, shift_mu, w_decay0, w_decay_up, a0, a_up, k_k, k_a, r_k, ln_x_w, ln_x_b, v_mix0, v_mix_down, v_mix_up, lb_logits, g_norm_w, w_out, ln_w, ln_b, loss_target, m_w_in, m_shift_mu, m_w_decay0, m_w_decay_up, m_a0, m_a_up, m_k_k, m_k_a, m_r_k, m_ln_x_w, m_ln_x_b, m_v_mix0, m_v_mix_down, m_v_mix_up, m_lb_logits, m_g_norm_w, m_w_out, m_ln_w, m_ln_b, v_w_in, v_shift_mu, v_w_decay0, v_w_decay_up, v_a0, v_a_up, v_k_k, v_k_a, v_r_k, v_ln_x_w, v_ln_x_b, v_v_mix0, v_v_mix_down, v_v_mix_up, v_lb_logits, v_g_norm_w, v_w_out, v_ln_w, v_ln_b):
    weights = dict(w_in=w_in, shift_mu=shift_mu, w_decay0=w_decay0, w_decay_up=w_decay_up, a0=a0, a_up=a_up, k_k=k_k,
                   k_a=k_a, r_k=r_k, ln_x_w=ln_x_w, ln_x_b=ln_x_b, v_mix0=v_mix0, v_mix_down=v_mix_down,
                   v_mix_up=v_mix_up, lb_logits=lb_logits, g_norm_w=g_norm_w, w_out=w_out, ln_w=ln_w, ln_b=ln_b)
    mom1 = dict(w_in=m_w_in, shift_mu=m_shift_mu, w_decay0=m_w_decay0, w_decay_up=m_w_decay_up, a0=m_a0, a_up=m_a_up,
                k_k=m_k_k, k_a=m_k_a, r_k=m_r_k, ln_x_w=m_ln_x_w, ln_x_b=m_ln_x_b, v_mix0=m_v_mix0,
                v_mix_down=m_v_mix_down, v_mix_up=m_v_mix_up, lb_logits=m_lb_logits, g_norm_w=m_g_norm_w,
                w_out=m_w_out, ln_w=m_ln_w, ln_b=m_ln_b)
    mom2 = dict(w_in=v_w_in, shift_mu=v_shift_mu, w_decay0=v_w_decay0, w_decay_up=v_w_decay_up, a0=v_a0, a_up=v_a_up,
                k_k=v_k_k, k_a=v_k_a, r_k=v_r_k, ln_x_w=v_ln_x_w, ln_x_b=v_ln_x_b, v_mix0=v_v_mix0,
                v_mix_down=v_v_mix_down, v_mix_up=v_v_mix_up, lb_logits=v_lb_logits, g_norm_w=v_g_norm_w,
                w_out=v_w_out, ln_w=v_ln_w, ln_b=v_ln_b)
    assert x.shape[0] == 1 and w_in.shape[0] == DEPTH
    t, d = x.shape[1], x.shape[2]
    dr = w_decay0.shape[1]
    dh = g_norm_w.shape[1]
    rank_w, rank_a, rank_v = w_decay_up.shape[1], a_up.shape[1], v_mix_up.shape[1]
    rwc = 4 * dr + rank_w + rank_a
    assert rank_w + rank_a == LANES and rank_v <= LANES and dr + dh == d
    assert t % CHUNK == 0 and dr % LANES == 0 and dh % LANES == 0 and shift_mu.shape[1] == rwc
    n_pair = dr // LANES
    me = _index(_position())

    win_t = [_all_gather_rows(w_in[0].T.astype(bf16), "ag_w_in_0"), None]
    wout = [None, None]
    late_blocks = [w_out[0].astype(bf16), w_in[1].T.astype(bf16), w_out[1].astype(bf16)]
    late_lands = [lax.dynamic_update_slice(lax.empty((N_DEV * blk.shape[0], blk.shape[1]), bf16), blk, (me * blk.shape[0], 0))
                  for blk in late_blocks]
    late_gather = _split_start(late_blocks, late_lands, _gather_plan, len(_GATHER_FLIPS), "ag_late_start")
    shard = dr // N_DEV
    pack = jnp.concatenate([w_decay_up[0], w_decay_up[1], a_up[0], a_up[1], v_mix_up[0], v_mix_down[0].T], axis=0)
    pack = _all_gather_rows(pack, "ag_small")
    pack = jnp.transpose(pack.reshape(N_DEV, -1, shard), (1, 0, 2)).reshape(-1, dr)
    offs = [0, rank_w, 2 * rank_w, 2 * rank_w + rank_a, 2 * rank_w + 2 * rank_a, 2 * rank_w + 2 * rank_a + rank_v,
            2 * rank_w + 2 * rank_a + 2 * rank_v]
    wdu_f = [pack[offs[0]:offs[1]], pack[offs[1]:offs[2]]]
    aup_f = [pack[offs[2]:offs[3]], pack[offs[3]:offs[4]]]
    vup_f = pack[offs[4]:offs[5]]
    vdown_f = pack[offs[5]:offs[6]].T

    def after_start(a, started):
        return a + started[-1][0:1, 0:1]

    def rwkv_params(l):
        mu = after_start(shift_mu[0:1], late_gather) if l == 0 else shift_mu[l:l + 1]
        prm = [mu, w_decay0[l:l + 1], a0[l:l + 1], _pad_rows(wdu_f[l], LANES, True),
               _pad_rows(aup_f[l], LANES, False)]
        if l == 1:
            prm += [v_mix0[0:1], _pad_rows(vdown_f.T, LANES, True).T, _pad_rows(vup_f, LANES, True)]
        rows = jnp.stack([k_k[l], k_a[l], r_k[l], ln_x_w[l], ln_x_b[l]] + [jnp.zeros((dr,), f32)] * 3, axis=0)
        pp = jnp.transpose(rows.reshape(8, n_pair, LANES), (1, 0, 2))
        return tuple(prm), pp

    h = x[0]
    h16 = h.astype(bf16)
    tgt = loss_target[0]
    saved = []
    vfirst = None
    for l in range(DEPTH):
        prm, pp = rwkv_params(l)
        proj = _matmul(h16, win_t[l], "nt", f"mm_proj_{l}", (2048, 640, 2048))
        if l == 0:
            cat, vfirst, mck = _rwkv_fwd(False, proj, None, prm, pp, d)
        else:
            cat, mck = _rwkv_fwd(True, proj, vfirst, prm, pp, d)
        cat, sck = _hgrn_fwd(l == 1, proj, lb_logits, g_norm_w[l:l + 1], cat, rwc)
        if l == 0:
            _, arrived = _split_wait(late_gather, _gather_plan, cat, "ag_late_wait")
            wout[0], win_t[1], wout[1] = _gather_forward(arrived, "ag_late_forward")
        y = _matmul(cat, wout[l], "nn", f"mm_out_{l}", (1024, 1024, 2048))
        saved.append((h, h16, proj, prm, pp, mck, sck, cat, y))
        if l < DEPTH - 1:
            h, h16 = _ln_fwd(h, y, ln_w[l:l + 1], ln_b[l:l + 1])
        else:
            dh_out, loss_part = _ln_loss(h, y, ln_w[l:l + 1], ln_b[l:l + 1], tgt)
    loss = lax.psum(loss_part[0, 0], ("x", "y", "c"))

    grads = {}
    big = {}
    dvfirst = None
    d_lbl = None
    rs_started = {}
    for l in reversed(range(DEPTH)):
        h_l, h16_l, proj, prm, pp, mck, sck, cat, y = saved[l]
        ln_w_l = ln_w[l:l + 1] if l == DEPTH - 1 else after_start(ln_w[l:l + 1], rs_started[l + 1])
        dy, dy16, g_ln_w, g_ln_b = _ln_bwd(h_l, y, ln_w_l, ln_b[l:l + 1], dh_out)
        dcat = _matmul(dy16, wout[l], "nt", f"mm_dcat_{l}", (1024, 1024, 2048))
        big[("w_out", l)] = _matmul(cat, dy16, "tn", f"mm_dwout_{l}", (512, 2048, 2048), out_dtype=bf16)
        if l == 1:
            outs = _rwkv_bwd(True, proj, vfirst, prm, pp, mck, dcat, None)
            dproj_r, dvfirst = outs[0], outs[1]
            dprm, dpp = outs[2:-1], outs[-1]
        else:
            outs = _rwkv_bwd(False, proj, None, prm, pp, mck, dcat, dvfirst)
            dproj_r = outs[0]
            dprm, dpp = outs[1:-1], outs[-1]
        dproj_h, dlbl_l, dgnw = _hgrn_bwd(l == 1, proj, lb_logits, g_norm_w[l:l + 1], sck, dcat, rwc)
        dproj = jnp.concatenate([dproj_r] + [dproj_h[i] for i in range(4)], axis=1)
        big[("w_in", l)] = _matmul(dproj, h16_l, "tn", f"mm_dwin_{l}", (640, 2048, 2048), out_dtype=bf16)
        sharded = [dprm[3][:rank_w].T, dprm[4][rank_w:].T]
        if l == 1:
            sharded += [dprm[6][:, :rank_v], dprm[7][:rank_v].T,
                        jnp.zeros((dr, LANES - 2 * rank_v), f32)]
        sharded = jnp.concatenate(sharded, axis=1).astype(bf16)
        rs_started[l] = _reduce_scatter_begin([big[("w_in", l)], big[("w_out", l)], sharded], f"l{l}")
        dy_res = after_start(dy, rs_started[l]) if l == 0 else dy
        dh_out = _matmul(dproj, win_t[l], "nn", f"mm_dh_{l}", (1024, 1024, 1664), add=dy_res, add_scale=ALPHA)
        dpp = jnp.transpose(dpp, (1, 0, 2)).reshape(8, dr)
        grads[l] = dict(shift_mu=dprm[0][0], w_decay0=dprm[1][0], a0=dprm[2][0],
                        k_k=dpp[0], k_a=dpp[1], r_k=dpp[2], ln_x_w=dpp[3], ln_x_b=dpp[4],
                        g_norm_w=dgnw[0], ln_w=g_ln_w[0], ln_b=g_ln_b[0])
        if l == 1:
            grads[l].update(v_mix0=dprm[5][0])
            d_lbl = dlbl_l
    grad_x = dh_out[None]

    def both(name):
        return jnp.stack([grads[0][name], grads[1][name]])

    small = dict(shift_mu=both("shift_mu"), w_decay0=both("w_decay0"), a0=both("a0"), k_k=both("k_k"), k_a=both("k_a"),
                 r_k=both("r_k"), ln_x_w=both("ln_x_w"), ln_x_b=both("ln_x_b"), v_mix0=grads[1]["v_mix0"][None],
                 lb_logits=d_lbl, g_norm_w=both("g_norm_w"), ln_w=both("ln_w"), ln_b=both("ln_b"))
    flat = jnp.concatenate([small[nm].reshape(-1) for nm in _SMALL])
    n_flat = flat.shape[0]
    rows = -(-n_flat // (8 * LANES)) * 8
    flat = jnp.concatenate([flat, jnp.zeros((rows * LANES - n_flat,), f32)]).reshape(rows, LANES)
    total = _sum_slots(_all_gather_rows(flat, "ag_small_grads").reshape(N_DEV, rows, LANES), "sum_small_grads").reshape(-1)
    gsm = {}
    off = 0
    for nm in _SMALL:
        size = small[nm].size
        gsm[nm] = total[off:off + size].reshape(small[nm].shape)
        off += size
    reduced = {1: _reduce_scatter_end(rs_started[1], dh_out, "l1")}
    reduced[0] = _reduce_scatter_end(rs_started[0], total, "l0")
    gsm["w_in"] = jnp.stack([reduced[l][0].T for l in range(DEPTH)])
    gsm["w_out"] = jnp.stack([reduced[l][1] for l in range(DEPTH)])
    gsm["w_decay_up"] = jnp.stack([reduced[l][2][:, :rank_w].T for l in range(DEPTH)])
    gsm["a_up"] = jnp.stack([reduced[l][2][:, rank_w:rank_w + rank_a].T for l in range(DEPTH)])
    gsm["v_mix_down"] = reduced[1][2][:, LANES:LANES + rank_v][None]
    gsm["v_mix_up"] = reduced[1][2][:, LANES + rank_v:LANES + 2 * rank_v].T[None]

    deltas, new_m, new_v = {}, {}, {}
    for nm in _NAMES:
        deltas[nm], new_m[nm], new_v[nm] = _adamw(weights[nm], gsm[nm], mom1[nm], mom2[nm], "adamw_" + nm)
    return (loss, grad_x, *[gsm[nm] for nm in _NAMES], *[deltas[nm] for nm in _NAMES],
            *[new_m[nm] for nm in _NAMES], *[new_v[nm] for nm in _NAMES])
```

```python
import functools

import jax
import jax.numpy as jnp
from jax import lax
from jax.experimental import pallas as pl
from jax.experimental.pallas import tpu as pltpu

f32 = jnp.float32
bf16 = jnp.bfloat16

N_DEV = 8
CHUNK = 64
LANES = 128
RWKV_HEAD = 64
DEPTH = 2
ALPHA = (2 * DEPTH) ** 0.25
LN_EPS = 1e-5
GN_EPS = 64e-5
RMS_EPS = 1e-5
LB_FLOOR = 1e-30
ADAM_LR, ADAM_B1, ADAM_B2, ADAM_EPS, ADAM_WD, ADAM_STEP = 0.001, 0.9, 0.999, 1e-08, 0.01, 10
MESH = pl.DeviceIdType.MESH


def _iota(shape, d):
    return lax.broadcasted_iota(jnp.int32, shape, d)


_DIMS = {"nn": (((1,), (0,)), ((), ())), "nt": (((1,), (1,)), ((), ())), "tn": (((0,), (0,)), ((), ()))}
_BATCH_DIMS = {"nn": (((2,), (1,)), ((0,), (0,))), "nt": (((2,), (2,)), ((0,), (0,))), "tn": (((1,), (1,)), ((0,), (0,)))}
_K_AXES = {"nn": (-1, -2), "nt": (-1, -1), "tn": (-2, -2)}


def _mxu(a, b, mode):
    return lax.dot_general(a, b, (_BATCH_DIMS if a.ndim == 3 else _DIMS)[mode], preferred_element_type=f32)


def _split(x):
    hi = x.astype(bf16)
    return hi, (x - hi.astype(f32)).astype(bf16)


def _mm2_impl(a, b, mode, passes=3):
    ah, al = _split(a)
    if passes == 3:
        bh, bl = _split(b)
        lhs, rhs = [ah, ah, al], [bh, bl, bh]
    else:
        bh = b.astype(bf16)
        lhs, rhs = [ah, al], [bh, bh]
    ka, kb = _K_AXES[mode]
    k = a.shape[ka]
    if k % (LANES if -1 in (ka, kb) else 16) == 0:
        return _mxu(jnp.concatenate(lhs, axis=ka), jnp.concatenate(rhs, axis=kb), mode)
    out = _mxu(lhs[0], rhs[0], mode)
    for x, y in zip(lhs[1:], rhs[1:]):
        out = out + _mxu(x, y, mode)
    return out


@functools.partial(jax.custom_vjp, nondiff_argnums=(2, 3))
def _mm2(a, b, mode, passes=3):
    return _mm2_impl(a, b, mode, passes)


def _mm2_fwd(a, b, mode, passes):
    return _mm2_impl(a, b, mode, passes), (a, b)


def _mm2_bwd(mode, passes, res, g):
    a, b = res
    if mode == "nn":
        return _mm2_impl(g, b, "nt", passes), _mm2_impl(a, g, "tn", passes)
    if mode == "nt":
        return _mm2_impl(g, b, "nn", passes), _mm2_impl(g, a, "tn", passes)
    return _mm2_impl(b, g, "nt", passes), _mm2_impl(a, g, "nn", passes)


_mm2.defvjp(_mm2_fwd, _mm2_bwd)

TRI_PASSES = 2
APPLY_PASSES = 2


def _const_impl(cm, x, mode):
    hi, lo = _split(x)
    if mode in ("r", "rt"):
        shape = x.shape
        hi, lo = hi.reshape(-1, shape[-1]), lo.reshape(-1, shape[-1])
        dims = "nn" if mode == "r" else "nt"
        out = _mxu(hi, cm, dims) + _mxu(lo, cm, dims)
        return out.reshape(shape[:-1] + (out.shape[-1],))
    if x.ndim == 3:
        cm = jnp.broadcast_to(cm, (x.shape[0],) + cm.shape)
    return _mxu(cm, hi, mode) + _mxu(cm, lo, mode)


@jax.custom_vjp
def _const_left(cm, x):
    return _const_impl(cm, x, "nn")


_const_left.defvjp(lambda cm, x: (_const_impl(cm, x, "nn"), cm),
                   lambda cm, g: (jnp.zeros_like(cm), _const_impl(cm, g, "tn")))


@jax.custom_vjp
def _const_right(x, cm):
    return _const_impl(cm, x, "r")


_const_right.defvjp(lambda x, cm: (_const_impl(cm, x, "r"), cm),
                    lambda cm, g: (_const_impl(cm, g, "rt"), jnp.zeros_like(cm)))


def _tri_inv(a):
    n = a.shape[-1]
    tm = (_iota((n, n), 0) == _iota((n, n), 1)).astype(f32) + a
    ak = a
    for _ in range(5):
        ak = _mm2_impl(ak, ak, "nn", TRI_PASSES)
        tm = tm + _mm2_impl(tm, ak, "nn", TRI_PASSES)
    return tm


@jax.custom_vjp
def _tri_solve(tm, a, x):
    del a
    return _mm2_impl(tm, x, "nn")


def _tri_solve_fwd(tm, a, x):
    u = _mm2_impl(tm, x, "nn")
    return u, (tm, u)


def _tri_solve_bwd(res, du):
    tm, u = res
    dx = _mm2_impl(tm, du, "tn")
    return jnp.zeros_like(tm), _mm2_impl(dx, u, "nt"), dx


_tri_solve.defvjp(_tri_solve_fwd, _tri_solve_bwd)


def _col_of_row(row_vec):
    n = row_vec.shape[-1]
    eye = _iota((n, n), 0) == _iota((n, n), 1)
    return jnp.sum(jnp.where(eye, jnp.broadcast_to(row_vec, row_vec.shape[:-2] + (n, n)), 0.0), axis=-1, keepdims=True)


def _softplus(x):
    return jnp.maximum(x, 0.0) + jnp.log1p(jnp.exp(-jnp.abs(x)))


def _log_sigmoid(x):
    return -_softplus(-x)


def _logaddexp(a, b):
    return jnp.maximum(a, b) + jnp.log1p(jnp.exp(-jnp.abs(a - b)))


def _silu(x):
    return x * jax.nn.sigmoid(x)


def _tril(c, strict):
    r, s = _iota((c, c), 0), _iota((c, c), 1)
    return (r > s) if strict else (r >= s)


def _last_row(a):
    c = a.shape[-2]
    return jnp.sum(jnp.where(_iota(a.shape, a.ndim - 2) == c - 1, a, 0.0), axis=-2, keepdims=True)


def _rwkv_pre(layer1, prm, y, prev, vf):
    c = y.shape[0]
    if layer1:
        mu, w0, a0, wup, aup, v0, vdown, vup = prm
    else:
        mu, w0, a0, wup, aup = prm
    dr = w0.shape[1]
    shift = (_iota((c, c), 0) == _iota((c, c), 1) + 1).astype(bf16)
    y_prev = _const_left(shift, y) + jnp.where(_iota((c, 1), 0) == 0, prev, 0.0)
    rw = y + mu * (y_prev - y)
    r, k, v, z = (rw[:, i * dr:(i + 1) * dr] for i in range(4))
    wdad = rw[:, 4 * dr:4 * dr + LANES]
    w_raw = w0 + _mm2(jnp.tanh(wdad), wup, "nn")
    lw = -jnp.exp(-_softplus(-w_raw) - 0.5)
    asig = jax.nn.sigmoid(a0 + _mm2(wdad, aup, "nn"))
    if layer1:
        v = v + (vf - v) * jax.nn.sigmoid(v0 + _mm2(_mm2(v, vdown, "nn"), vup, "nn"))
    return r, k, v, z, lw, asig


def _rwkv_pair(pp, m0, xs, tm=None):
    kkw, kaw, rkw, gnw, gnb = pp
    r, k, v, z, lw, asig = xs
    c = r.shape[-2]
    n2 = 2 * c
    lane = _iota((1, LANES), 1)
    mh0, mh1 = (lane < RWKV_HEAD).astype(f32), (lane >= RWKV_HEAD).astype(f32)
    same_head = _iota((LANES, LANES), 0) // RWKV_HEAD == _iota((LANES, LANES), 1) // RWKV_HEAD
    g = same_head.astype(bf16)

    def seg(x):
        return _const_right(x, g)

    def stack(x):
        return jnp.concatenate([x * mh0, x * mh1], axis=-2)

    kk = k * kkw
    kk = kk / jnp.maximum(jnp.sqrt(seg(kk * kk)), 1e-12)
    k2 = k * (1.0 + (asig - 1.0) * kaw)
    a = -kk
    b = kk * asig
    cum = _const_left(_tril(c, False).astype(bf16), lw)
    at = stack(a * jnp.exp(cum - lw))
    rt = stack(r * jnp.exp(cum))
    en = jnp.exp(-cum)
    sc = _mm2(jnp.concatenate([at, rt], axis=-2), jnp.concatenate([stack(b * en), stack(k2 * en)], axis=-2), "nt")
    row, col = _iota((n2, n2), 0), _iota((n2, n2), 1)
    same = row // c == col // c
    strict = same & (row % c > col % c)
    incl = same & (row % c >= col % c)
    aab = jnp.where(strict, sc[..., :n2, :n2], 0.0)
    aak = jnp.where(strict, sc[..., :n2, n2:], 0.0)
    arb = jnp.where(incl, sc[..., n2:, :n2], 0.0)
    ark = jnp.where(incl, sc[..., n2:, n2:], 0.0)
    vv = jnp.concatenate([v, v], axis=-2)
    mask_st = jnp.concatenate([jnp.broadcast_to(mh0, (c, LANES)), jnp.broadcast_to(mh1, (c, LANES))], axis=0)
    x_st = _mm2(jnp.concatenate([at, aak], axis=-1), jnp.concatenate([m0, vv], axis=-2), "nn", APPLY_PASSES)
    if tm is None:
        tm = _tri_inv(lax.stop_gradient(aab))
    u_st = _tri_solve(tm, aab, x_st) * mask_st
    o_st = _mm2(jnp.concatenate([rt, arb, ark], axis=-1), jnp.concatenate([m0, u_st, vv], axis=-2), "nn", APPLY_PASSES) * mask_st
    u = u_st[..., :c, :] + u_st[..., c:, :]
    o = o_st[..., :c, :] + o_st[..., c:, :]
    cum_last = _last_row(cum)
    dec_end = jnp.exp(cum_last - cum)
    m_new = _col_of_row(jnp.exp(cum_last)) * m0 + _mm2(
        jnp.concatenate([b * dec_end, k2 * dec_end], axis=-2), jnp.concatenate([u, v], axis=-2), "tn", APPLY_PASSES) * same_head.astype(f32)
    mean = seg(o) * (1.0 / RWKV_HEAD)
    d = o - mean
    var = seg(d * d) * (1.0 / RWKV_HEAD)
    on = d * lax.rsqrt(var + GN_EPS) * gnw + gnb
    bonus = seg(r * k2 * rkw) * v
    return (on + bonus) * _silu(z), m_new, tm


def _split_lanes(a, n):
    return [a[:, i * LANES:(i + 1) * LANES] for i in range(n)]


def _rwkv_step(layer1, prm, y, prev, vf, pp, m0, tm=None):
    xs = _rwkv_pre(layer1, prm, y, prev, vf)
    n_pair = m0.shape[0]
    og, m_new, tm = _rwkv_pair(pp, m0, tuple(jnp.concatenate([p[None] for p in _split_lanes(a, n_pair)], axis=0) for a in xs), tm)
    return og, m_new, xs[2], tm


def _group(n):
    return n


def _rwkv_specs(layer1, t, dr, rwc, n_pair, rev):
    nc = t // CHUNK
    grp = _group(n_pair)

    def cidx(c):
        return (nc - 1 - c) if rev else c

    full = lambda shape: pl.BlockSpec(shape, lambda c, p: tuple(0 for _ in shape))
    specs = [
        pl.BlockSpec((CHUNK, rwc), lambda c, p: (cidx(c), 0)),
        pl.BlockSpec((8, rwc), lambda c, p: (jnp.maximum(cidx(c) * (CHUNK // 8) - 1, 0), 0)),
    ]
    if layer1:
        specs.append(pl.BlockSpec((CHUNK, dr), lambda c, p: (cidx(c), 0)))
    prm_shapes = [(1, rwc), (1, dr), (1, dr), (LANES, dr), (LANES, dr)]
    if layer1:
        prm_shapes += [(1, dr), (dr, LANES), (LANES, dr)]
    specs += [full(s) for s in prm_shapes]
    specs.append(pl.BlockSpec((grp, 8, LANES), lambda c, p: (p, 0, 0)))
    return specs, prm_shapes, cidx, full


def _rwkv_fwd(layer1, proj, vf, prm, pp, cat_width):
    t = proj.shape[0]
    dr = prm[1].shape[1]
    rwc = prm[0].shape[1]
    n_pair = dr // LANES
    nc = t // CHUNK
    n_prm = len(prm)
    specs, _, _, _ = _rwkv_specs(layer1, t, dr, rwc, n_pair, False)

    def body(*refs):
        y_ref, prev_ref = refs[0], refs[1]
        i = 2
        vf_ref = None
        if layer1:
            vf_ref = refs[i]
            i += 1
        prm_refs = refs[i:i + n_prm]
        i += n_prm
        pp_ref = refs[i]
        i += 1
        cat_ref = refs[i]
        i += 1
        vout_ref = None
        if not layer1:
            vout_ref = refs[i]
            i += 1
        mck_ref, m_s = refs[i], refs[i + 1]
        c = pl.program_id(0)

        @pl.when(c == 0)
        def _():
            m_s[...] = jnp.zeros_like(m_s)

        prev = prev_ref[pl.ds(7, 1), :] * (c != 0).astype(f32)
        m0 = m_s[...]
        ppv = tuple(pp_ref[:, pl.ds(q, 1), :] for q in range(5))
        og, m_new, v, tm = _rwkv_step(layer1, tuple(r[...] for r in prm_refs), y_ref[...], prev,
                                      vf_ref[...] if layer1 else None, ppv, m0)
        mck_ref[0, :n_pair] = m0
        mck_ref[0, n_pair:] = tm
        if not layer1:
            vout_ref[...] = v
        for j in range(n_pair):
            cat_ref[:, j * LANES:(j + 1) * LANES] = og[j]
        m_s[...] = m_new

    grp = _group(n_pair)
    assert grp == n_pair
    out_shape = [jax.ShapeDtypeStruct((t, cat_width), f32)]
    out_specs = [pl.BlockSpec((CHUNK, grp * LANES), lambda c, p: (c, p))]
    if not layer1:
        out_shape.append(jax.ShapeDtypeStruct((t, dr), f32))
        out_specs.append(pl.BlockSpec((CHUNK, dr), lambda c, p: (c, 0)))
    out_shape.append(jax.ShapeDtypeStruct((nc, 2 * n_pair, LANES, LANES), f32))
    out_specs.append(pl.BlockSpec((1, 2 * grp, LANES, LANES), lambda c, p: (c, p, 0, 0)))
    args = [proj, proj] + ([vf] if layer1 else []) + list(prm) + [pp]
    return pl.pallas_call(
        body, grid=(nc, 1), in_specs=specs, out_specs=out_specs, out_shape=out_shape,
        scratch_shapes=[pltpu.VMEM((n_pair, LANES, LANES), f32)],
        compiler_params=pltpu.CompilerParams(dimension_semantics=("arbitrary", "arbitrary")),
        name=f"rwkv_fwd_l{int(layer1)}",
    )(*args)


def _rwkv_bwd(layer1, proj, vf, prm, pp, mck, dcat, dvout):
    t = proj.shape[0]
    dr = prm[1].shape[1]
    rwc = prm[0].shape[1]
    n_pair = dr // LANES
    nc = t // CHUNK
    n_prm = len(prm)
    specs, prm_shapes, cidx, full = _rwkv_specs(layer1, t, dr, rwc, n_pair, True)
    grp = _group(n_pair)
    assert grp == n_pair
    specs.append(pl.BlockSpec((1, 2 * grp, LANES, LANES), lambda c, p: (cidx(c), p, 0, 0)))
    specs.append(pl.BlockSpec((CHUNK, grp * LANES), lambda c, p: (cidx(c), p)))
    if not layer1:
        specs.append(pl.BlockSpec((CHUNK, dr), lambda c, p: (cidx(c), 0)))

    def body(*refs):
        y_ref, prev_ref = refs[0], refs[1]
        i = 2
        vf_ref = None
        if layer1:
            vf_ref = refs[i]
            i += 1
        prm_refs = refs[i:i + n_prm]
        i += n_prm
        pp_ref, mck_ref, dog_ref = refs[i], refs[i + 1], refs[i + 2]
        i += 3
        dvout_ref = None
        if not layer1:
            dvout_ref = refs[i]
            i += 1
        dy_ref = refs[i]
        i += 1
        dvf_ref = None
        if layer1:
            dvf_ref = refs[i]
            i += 1
        dprm_refs = refs[i:i + n_prm]
        i += n_prm
        dpp_ref = refs[i]
        dm_s, dprev_s = refs[i + 1:i + 3]
        c = pl.program_id(0)
        cr = nc - 1 - c

        @pl.when(c == 0)
        def _():
            dm_s[...] = jnp.zeros_like(dm_s)
            dprev_s[...] = jnp.zeros_like(dprev_s)
            dpp_ref[...] = jnp.zeros_like(dpp_ref)
            for r in dprm_refs:
                r[...] = jnp.zeros_like(r)

        prev = prev_ref[pl.ds(7, 1), :] * (cr != 0).astype(f32)
        prm_v = tuple(r[...] for r in prm_refs)
        ppv = tuple(pp_ref[:, pl.ds(q, 1), :] for q in range(5))
        dog = jnp.stack([dog_ref[:, j * LANES:(j + 1) * LANES] for j in range(n_pair)], axis=0)
        m0, tm = mck_ref[0, :n_pair], mck_ref[0, n_pair:]
        no_tm = jnp.zeros_like(tm)
        if layer1:
            _, vjp = jax.vjp(lambda a, b, d, e, g, h: _rwkv_step(True, a, b, d, e, g, h, tm),
                             prm_v, y_ref[...], prev, vf_ref[...], ppv, m0)
            dprm, dy, dprev, dvf, dppv, dm0 = vjp((dog, dm_s[...], jnp.zeros((CHUNK, dr), f32), no_tm))
            dvf_ref[...] = dvf
        else:
            _, vjp = jax.vjp(lambda a, b, d, e, g: _rwkv_step(False, a, b, d, None, e, g, tm), prm_v, y_ref[...], prev, ppv, m0)
            dprm, dy, dprev, dppv, dm0 = vjp((dog, dm_s[...], dvout_ref[...], no_tm))
        dm_s[...] = dm0
        for q in range(5):
            dpp_ref[:, pl.ds(q, 1), :] += dppv[q]
        dy_ref[...] = (dy + jnp.where(_iota((CHUNK, 1), 0) == CHUNK - 1, dprev_s[...], 0.0)).astype(bf16)
        dprev_s[...] = dprev
        for r, gval in zip(dprm_refs, dprm):
            r[...] += gval

    out_shape = [jax.ShapeDtypeStruct((t, rwc), bf16)]
    out_specs = [pl.BlockSpec((CHUNK, rwc), lambda c, p: (cidx(c), 0))]
    if layer1:
        out_shape.append(jax.ShapeDtypeStruct((t, dr), f32))
        out_specs.append(pl.BlockSpec((CHUNK, dr), lambda c, p: (cidx(c), 0)))
    out_shape += [jax.ShapeDtypeStruct(s, f32) for s in prm_shapes]
    out_specs += [full(s) for s in prm_shapes]
    out_shape.append(jax.ShapeDtypeStruct((n_pair, 8, LANES), f32))
    out_specs.append(full((n_pair, 8, LANES)))
    args = [proj, proj] + ([vf] if layer1 else []) + list(prm) + [pp, mck, dcat] + ([] if layer1 else [dvout])
    return pl.pallas_call(
        body, grid=(nc, 1), in_specs=specs, out_specs=out_specs, out_shape=out_shape,
        scratch_shapes=[pltpu.VMEM((n_pair, LANES, LANES), f32), pltpu.VMEM((1, rwc), f32)],
        compiler_params=pltpu.CompilerParams(dimension_semantics=("arbitrary", "arbitrary")),
        name=f"rwkv_bwd_l{int(layer1)}",
    )(*args)


def _hgrn_chunk(layer1, lbl, gnw, s0, q_raw, f_raw, i_in, z):
    c = q_raw.shape[-2]
    q = _silu(q_raw)
    ls = _log_sigmoid(f_raw)
    if layer1:
        l0, l1 = lbl[..., 0:1, :], lbl[..., 1:2, :]
        mx = jnp.maximum(l0, l1)
        e0, e1 = jnp.exp(l0 - mx), jnp.exp(l1 - mx)
        sm0, sm1 = e0 / (e0 + e1), e1 / (e0 + e1)
        lb = (sm0 + sm1) - sm0
        log_f = _logaddexp(jnp.log(jnp.maximum(lb, LB_FLOOR)), jnp.log1p(-lb) + ls)
        k = (1.0 - lb) * jax.nn.sigmoid(-f_raw)
    else:
        log_f = _logaddexp(jnp.full_like(ls, jnp.log(jnp.float32(LB_FLOOR))), ls)
        k = jax.nn.sigmoid(-f_raw)
    row, col = _iota((c, c), 0), _iota((c, c), 1)
    trow = _iota((c, 1), 0)
    halves = []
    half = c // 2
    while half >= 1:
        halves.append(half)
        half //= 2
    cmat = jnp.concatenate([(col <= row).astype(f32)]
                           + [(col <= (row // (2 * hf)) * (2 * hf) + hf - 1).astype(f32) for hf in halves], axis=0)
    ball = _const_left(cmat.astype(bf16), log_f)
    b = ball[..., :c, :]
    att = None
    for lvl, hf in enumerate(halves):
        blk = 2 * hf
        bref = ball[..., (lvl + 1) * c:(lvl + 2) * c, :]
        upper = (trow % blk) >= hf
        qh = q * jnp.exp(jnp.where(upper, b - bref, 0.0)) * upper.astype(f32)
        kh = k * jnp.exp(jnp.where(upper, 0.0, bref - b)) * (1.0 - upper.astype(f32))
        term = jnp.where(row // blk == col // blk, _mm2(qh, kh, "nt", APPLY_PASSES), 0.0)
        att = term if att is None else att + term
    lhs = jnp.concatenate([q * jnp.exp(b), att, jnp.zeros(att.shape[:-1] + (LANES - c,), f32)], axis=-1)
    rhs = jnp.concatenate([s0, i_in, jnp.zeros(i_in.shape[:-2] + (LANES - c, i_in.shape[-1]), f32)], axis=-2)
    o = _mm2(lhs, rhs, "nn", APPLY_PASSES) + jnp.sum(q * k, axis=-1, keepdims=True) * i_in
    b_last = _last_row(b)
    s_new = _col_of_row(jnp.exp(b_last)) * s0 + _mm2(k * jnp.exp(b_last - b), i_in, "tn", APPLY_PASSES)
    o = o * lax.rsqrt(jnp.mean(o * o, axis=-1, keepdims=True) + RMS_EPS)
    return o * gnw * _silu(z), s_new


def _hgrn_in_specs(t, dh, col0, rev):
    nc = t // CHUNK
    nh = dh // LANES

    def cidx(c):
        return (nc - 1 - c) if rev else c

    grp = _group(nh)
    specs = [pl.BlockSpec((CHUNK, LANES), functools.partial(lambda g, j, h, c: (cidx(c), col0 + g * nh + h * grp + j), g, j))
             for j in range(grp) for g in range(4)]
    specs.append(pl.BlockSpec((2, grp * LANES), lambda h, c: (0, h)))
    specs.append(pl.BlockSpec((1, grp * LANES), lambda h, c: (0, h)))
    return specs, cidx, grp


def _hgrn_fwd(layer1, proj, lbl, gnw, cat, rwc):
    t, d = cat.shape
    dh = gnw.shape[1]
    nh = dh // LANES
    nc = t // CHUNK
    col0 = rwc // LANES
    specs, _, grp = _hgrn_in_specs(t, dh, col0, False)
    specs.append(pl.BlockSpec(memory_space=pl.ANY))
    assert (d - dh) % (grp * LANES) == 0
    cat_col0 = (d - dh) // (grp * LANES)

    def body(*refs):
        x_refs = refs[:4 * grp]
        lbl_ref, gnw_ref, _, cat_ref, sck_ref, s_s = refs[4 * grp:]
        c = pl.program_id(1)

        @pl.when(c == 0)
        def _():
            s_s[...] = jnp.zeros_like(s_s)

        lanes = [slice(j * LANES, (j + 1) * LANES) for j in range(grp)]
        s0 = s_s[...]
        sck_ref[:, 0] = s0
        out, s_new = _hgrn_chunk(layer1, jnp.stack([lbl_ref[:, ln] for ln in lanes]), jnp.stack([gnw_ref[:, ln] for ln in lanes]),
                                 s0, *(jnp.stack([x_refs[4 * j + g][...] for j in range(grp)]) for g in range(4)))
        for j in range(grp):
            cat_ref[:, lanes[j]] = out[j]
        s_s[...] = s_new

    return pl.pallas_call(
        body, grid=(nh // grp, nc), in_specs=specs,
        out_specs=[pl.BlockSpec((CHUNK, grp * LANES), lambda h, c: (c, cat_col0 + h)),
                   pl.BlockSpec((grp, 1, LANES, LANES), lambda h, c: (h, c, 0, 0))],
        out_shape=[jax.ShapeDtypeStruct((t, d), f32), jax.ShapeDtypeStruct((nh, nc, LANES, LANES), f32)],
        scratch_shapes=[pltpu.VMEM((grp, LANES, LANES), f32)],
        input_output_aliases={4 * grp + 2: 0},
        compiler_params=pltpu.CompilerParams(dimension_semantics=("arbitrary", "arbitrary")),
        name=f"hgrn_fwd_l{int(layer1)}",
    )(*([proj] * (4 * grp)), lbl, gnw, cat)


def _hgrn_bwd(layer1, proj, lbl, gnw, sck, dcat, rwc):
    t, d = dcat.shape
    dh = gnw.shape[1]
    nh = dh // LANES
    nc = t // CHUNK
    col0 = rwc // LANES
    specs, cidx, grp = _hgrn_in_specs(t, dh, col0, True)
    assert (d - dh) % (grp * LANES) == 0
    cat_col0 = (d - dh) // (grp * LANES)
    specs.append(pl.BlockSpec((grp, 1, LANES, LANES), lambda h, c: (h, cidx(c), 0, 0)))
    specs.append(pl.BlockSpec((CHUNK, grp * LANES), lambda h, c: (cidx(c), cat_col0 + h)))

    def body(*refs):
        x_refs = refs[:4 * grp]
        lbl_ref, gnw_ref, sck_ref, do_ref, dp_ref, dlbl_ref, dgnw_ref, ds_s = refs[4 * grp:]
        c = pl.program_id(1)

        @pl.when(c == 0)
        def _():
            ds_s[...] = jnp.zeros_like(ds_s)
            dlbl_ref[...] = jnp.zeros_like(dlbl_ref)
            dgnw_ref[...] = jnp.zeros_like(dgnw_ref)

        lanes = [slice(j * LANES, (j + 1) * LANES) for j in range(grp)]
        _, vjp = jax.vjp(functools.partial(_hgrn_chunk, layer1),
                         jnp.stack([lbl_ref[:, ln] for ln in lanes]), jnp.stack([gnw_ref[:, ln] for ln in lanes]), sck_ref[:, 0],
                         *(jnp.stack([x_refs[4 * j + g][...] for j in range(grp)]) for g in range(4)))
        dlbl, dgnw, ds0, dq, df, di, dz = vjp((jnp.stack([do_ref[:, ln] for ln in lanes]), ds_s[...]))
        ds_s[...] = ds0
        for j in range(grp):
            dlbl_ref[:, lanes[j]] += dlbl[j]
            dgnw_ref[:, lanes[j]] += dgnw[j]
            for g, val in enumerate((dq, df, di, dz)):
                dp_ref[g, :, lanes[j]] = val[j].astype(bf16)

    return pl.pallas_call(
        body, grid=(nh // grp, nc), in_specs=specs,
        out_specs=[pl.BlockSpec((4, CHUNK, grp * LANES), lambda h, c: (0, cidx(c), h)),
                   pl.BlockSpec((2, grp * LANES), lambda h, c: (0, h)),
                   pl.BlockSpec((1, grp * LANES), lambda h, c: (0, h))],
        out_shape=[jax.ShapeDtypeStruct((4, t, dh), bf16), jax.ShapeDtypeStruct((2, dh), f32),
                   jax.ShapeDtypeStruct((1, dh), f32)],
        scratch_shapes=[pltpu.VMEM((grp, LANES, LANES), f32)],
        compiler_params=pltpu.CompilerParams(dimension_semantics=("arbitrary", "arbitrary")),
        name=f"hgrn_bwd_l{int(layer1)}",
    )(*([proj] * (4 * grp)), lbl, gnw, sck, dcat)


def _ln(h, y, w, b):
    u = ALPHA * h + y
    mu = jnp.mean(u, axis=-1, keepdims=True)
    var = jnp.mean(jnp.square(u - mu), axis=-1, keepdims=True)
    return (u - mu) * lax.rsqrt(var + LN_EPS) * w + b


def _row_tile(t):
    return 256 if t % 256 == 0 else t


def _ln_fwd(h, y, w, b):
    t, d = h.shape
    tr = _row_tile(t)

    def body(h_ref, y_ref, w_ref, b_ref, o_ref, o16_ref):
        out = _ln(h_ref[...], y_ref[...], w_ref[...], b_ref[...])
        o_ref[...] = out
        o16_ref[...] = out.astype(bf16)

    row = pl.BlockSpec((tr, d), lambda i: (i, 0))
    vec = pl.BlockSpec((1, d), lambda i: (0, 0))
    return pl.pallas_call(body, grid=(t // tr,), in_specs=[row, row, vec, vec], out_specs=[row, row],
                          out_shape=[jax.ShapeDtypeStruct((t, d), f32), jax.ShapeDtypeStruct((t, d), bf16)],
                          name="ln_fwd")(h, y, w, b)


def _ln_loss(h, y, w, b, tgt):
    t, d = h.shape
    tr = _row_tile(t)

    def body(h_ref, y_ref, w_ref, b_ref, t_ref, g_ref, loss_ref):
        @pl.when(pl.program_id(0) == 0)
        def _():
            loss_ref[...] = jnp.zeros_like(loss_ref)

        err = _ln(h_ref[...], y_ref[...], w_ref[...], b_ref[...]) - t_ref[...]
        g_ref[...] = err * (1.0 / d)
        loss_ref[...] += 0.5 * jnp.sum(jnp.mean(jnp.square(err), axis=-1, keepdims=True), axis=0, keepdims=True)

    row = pl.BlockSpec((tr, d), lambda i: (i, 0))
    vec = pl.BlockSpec((1, d), lambda i: (0, 0))
    return pl.pallas_call(
        body, grid=(t // tr,), in_specs=[row, row, vec, vec, row],
        out_specs=[row, pl.BlockSpec((1, LANES), lambda i: (0, 0))],
        out_shape=[jax.ShapeDtypeStruct((t, d), f32), jax.ShapeDtypeStruct((1, LANES), f32)],
        compiler_params=pltpu.CompilerParams(dimension_semantics=("arbitrary",)), name="ln_loss")(h, y, w, b, tgt)


def _ln_bwd(h, y, w, b, dout):
    t, d = h.shape
    tr = _row_tile(t)

    def body(h_ref, y_ref, w_ref, b_ref, do_ref, dy_ref, dy16_ref, dw_ref, db_ref):
        @pl.when(pl.program_id(0) == 0)
        def _():
            dw_ref[...] = jnp.zeros_like(dw_ref)
            db_ref[...] = jnp.zeros_like(db_ref)

        _, vjp = jax.vjp(lambda yy, ww, bb: _ln(h_ref[...], yy, ww, bb), y_ref[...], w_ref[...], b_ref[...])
        dy, dw, db = vjp(do_ref[...])
        dy_ref[...] = dy
        dy16_ref[...] = dy.astype(bf16)
        dw_ref[...] += dw
        db_ref[...] += db

    row = pl.BlockSpec((tr, d), lambda i: (i, 0))
    vec = pl.BlockSpec((1, d), lambda i: (0, 0))
    return pl.pallas_call(
        body, grid=(t // tr,), in_specs=[row, row, vec, vec, row], out_specs=[row, row, vec, vec],
        out_shape=[jax.ShapeDtypeStruct((t, d), f32), jax.ShapeDtypeStruct((t, d), bf16),
                   jax.ShapeDtypeStruct((1, d), f32), jax.ShapeDtypeStruct((1, d), f32)],
        compiler_params=pltpu.CompilerParams(dimension_semantics=("arbitrary",)), name="ln_bwd")(h, y, w, b, dout)


def _pick(n, prefs):
    for p in prefs:
        if n % p == 0:
            return p
    return n


def _tile(n, want):
    if n <= want:
        return n
    for cand in range(want - want % LANES, 0, -LANES):
        if n % cand == 0:
            return cand
    return n


def _matmul(a, b, mode, name, tiles, add=None, add_scale=1.0, out_dtype=f32):
    if mode == "nn":
        (m, k), n = a.shape, b.shape[1]
    elif mode == "nt":
        (m, k), n = a.shape, b.shape[0]
    else:
        (k, m), n = a.shape, b.shape[1]
    tm, tn, tk = _tile(m, tiles[0]), _tile(n, tiles[1]), _tile(k, tiles[2])
    nk = k // tk
    cache_a = nk == 1 and a.dtype != bf16 and n // tn > 1

    def body(*refs):
        a_ref, b_ref = refs[0], refs[1]
        add_ref = refs[2] if add is not None else None
        n_in = 3 if add is not None else 2
        o_ref = refs[n_in]
        scratch = refs[n_in + 1:]

        def finish(res):
            if add is not None:
                res = res + add_scale * add_ref[...]
            o_ref[...] = res.astype(out_dtype)

        if cache_a:
            a_bf = scratch[0]

            @pl.when(pl.program_id(1) == 0)
            def _():
                a_bf[...] = a_ref[...].astype(bf16)

            a_val = a_bf[...]
        else:
            a_val = a_ref[...].astype(bf16)
        prod = lax.dot_general(a_val, b_ref[...].astype(bf16), _DIMS[mode], preferred_element_type=f32)
        if nk == 1:
            finish(prod)
        else:
            acc = scratch[-1]
            kk = pl.program_id(2)

            @pl.when(kk == 0)
            def _():
                acc[...] = prod

            @pl.when(kk != 0)
            def _():
                acc[...] += prod

            @pl.when(kk == nk - 1)
            def _():
                finish(acc[...])

    a_shape = (tk, tm) if mode == "tn" else (tm, tk)
    a_spec = pl.BlockSpec(a_shape, (lambda i, j, kk: (kk, i)) if mode == "tn" else (lambda i, j, kk: (i, kk)))
    b_spec = pl.BlockSpec((tn, tk), lambda i, j, kk: (j, kk)) if mode == "nt" else pl.BlockSpec((tk, tn), lambda i, j, kk: (kk, j))
    o_spec = pl.BlockSpec((tm, tn), lambda i, j, kk: (i, j))
    in_specs = [a_spec, b_spec] + ([o_spec] if add is not None else [])
    args = [a, b] + ([add] if add is not None else [])
    scratch_shapes = ([pltpu.VMEM(a_shape, bf16)] if cache_a else []) + ([pltpu.VMEM((tm, tn), f32)] if nk > 1 else [])
    return pl.pallas_call(
        body, grid=(m // tm, n // tn, nk), in_specs=in_specs, out_specs=o_spec,
        out_shape=jax.ShapeDtypeStruct((m, n), out_dtype), scratch_shapes=scratch_shapes,
        compiler_params=pltpu.CompilerParams(dimension_semantics=("parallel", "arbitrary", "arbitrary")),
        name=name,
    )(*args)


def _position():
    return lax.axis_index("x"), lax.axis_index("y"), lax.axis_index("c")


def _flip(pos, k):
    x, y, c = pos
    return (1 - x if k & 4 else x, 1 - y if k & 2 else y, 1 - c if k & 1 else c)


def _index(pos):
    return 4 * pos[0] + 2 * pos[1] + pos[2]


def _all_gather_rows(x, name):
    m_per, n = x.shape

    def body(x_ref, out_ref, send_sems, recv_sems, local_sem):
        me = _position()
        sibling = _flip(me, 1)
        chips = (2, 4, 6)

        def rows(pos):
            return out_ref.at[pl.ds(_index(pos) * m_per, m_per), :]

        def copy(sem, block, to, src=None):
            return pltpu.make_async_remote_copy(
                src_ref=rows(block) if src is None else src, dst_ref=rows(block),
                send_sem=send_sems.at[sem], recv_sem=recv_sems.at[sem], device_id=to, device_id_type=MESH)

        mine = pltpu.make_async_copy(x_ref, rows(me), local_sem)
        mine.start()
        first = [copy(0, me, sibling, src=x_ref)]
        first += [copy(1 + j, me, _flip(me, k), src=x_ref) for j, k in enumerate(chips)]
        for cp in first:
            cp.start()
        passed = [copy(4 + j, _flip(me, k), sibling) for j, k in enumerate(chips)]
        for j, k in enumerate(chips):
            copy(1 + j, _flip(me, k), me).wait_recv()
            passed[j].start()
        copy(0, sibling, me).wait_recv()
        for j, k in enumerate(chips):
            copy(4 + j, _flip(sibling, k), me).wait_recv()
        for cp in first + passed:
            cp.wait_send()
        mine.wait()

    return pl.pallas_call(
        body, out_shape=jax.ShapeDtypeStruct((N_DEV * m_per, n), x.dtype),
        in_specs=[pl.BlockSpec(memory_space=pl.ANY)], out_specs=pl.BlockSpec(memory_space=pl.ANY),
        scratch_shapes=[pltpu.SemaphoreType.DMA((7,)), pltpu.SemaphoreType.DMA((7,)), pltpu.SemaphoreType.DMA(())],
        name=name,
    )(x)


def _split_start(srcs, lands, plan, n_copies, name):
    n_arr = len(srcs)
    hbm = pl.BlockSpec(memory_space=pltpu.HBM)
    sem = pl.BlockSpec(memory_space=pltpu.SEMAPHORE)

    def body(*refs):
        src_refs, land_refs = refs[:n_arr], refs[n_arr:2 * n_arr]
        send_sems, recv_sems = refs[2 * n_arr:3 * n_arr], refs[3 * n_arr:4 * n_arr]
        token = refs[-1]
        me = _position()
        for i in range(n_arr):
            for j, (src, dst, peer, _) in enumerate(plan(i, src_refs[i], land_refs[i], me)):
                pltpu.make_async_remote_copy(src_ref=src, dst_ref=dst, send_sem=send_sems[i].at[j], recv_sem=recv_sems[i].at[j],
                                             device_id=peer, device_id_type=MESH).start()
        token[...] = jnp.zeros_like(token)

    outs = pl.pallas_call(
        body, name=name,
        out_shape=([pltpu.SemaphoreType.DMA((n_copies,))] * (2 * n_arr)
                   + [pltpu.HBM(a.shape, a.dtype) for a in list(srcs) + list(lands)]
                   + [jax.ShapeDtypeStruct((8, LANES), f32)]),
        in_specs=[hbm] * (2 * n_arr),
        out_specs=[sem] * (2 * n_arr) + [hbm] * (2 * n_arr) + [pl.BlockSpec(memory_space=pltpu.VMEM)],
        input_output_aliases={i: 2 * n_arr + i for i in range(2 * n_arr)},
        compiler_params=pltpu.CompilerParams(has_side_effects=pltpu.SideEffectType.DATAFLOW_SIDE_EFFECTING),
    )(*[pltpu.with_memory_space_constraint(a, pltpu.HBM) for a in list(srcs) + list(lands)])
    return (outs[:n_arr], outs[n_arr:2 * n_arr], outs[2 * n_arr:3 * n_arr], outs[3 * n_arr:4 * n_arr], outs[-1])


def _split_wait(started, plan, after, name):
    send_sems, recv_sems, srcs, lands, _ = started
    n_arr = len(srcs)
    hbm = pl.BlockSpec(memory_space=pltpu.HBM)
    sem = pl.BlockSpec(memory_space=pltpu.SEMAPHORE)

    def body(*refs):
        src_refs, land_refs = refs[:n_arr], refs[n_arr:2 * n_arr]
        s_sems, r_sems = refs[2 * n_arr:3 * n_arr], refs[3 * n_arr:4 * n_arr]
        me = _position()
        for i in range(n_arr):
            for j, (src, _, peer, arrival) in enumerate(plan(i, src_refs[i], land_refs[i], me)):
                cp = pltpu.make_async_remote_copy(src_ref=src, dst_ref=arrival, send_sem=s_sems[i].at[j], recv_sem=r_sems[i].at[j],
                                                  device_id=peer, device_id_type=MESH)
                cp.wait_send()
                cp.wait_recv()

    outs = pl.pallas_call(
        body, name=name,
        out_shape=[pltpu.HBM(a.shape, a.dtype) for a in list(srcs) + list(lands)],
        in_specs=[hbm] * (2 * n_arr) + [sem] * (2 * n_arr) + [pl.BlockSpec(memory_space=pl.ANY)],
        out_specs=[hbm] * (2 * n_arr),
        input_output_aliases={i: i for i in range(2 * n_arr)},
        compiler_params=pltpu.CompilerParams(has_side_effects=pltpu.SideEffectType.DATAFLOW_SIDE_EFFECTING),
    )(*srcs, *lands, *send_sems, *recv_sems, after)
    return outs[:n_arr], outs[n_arr:]


_GATHER_FLIPS = (1, 2, 4, 6)


def _gather_plan(i, src_ref, land_ref, me):
    m = src_ref.shape[0]

    def rows(pos):
        return land_ref.at[pl.ds(_index(pos) * m, m), :]

    return [(src_ref, rows(me), _flip(me, k), rows(_flip(me, k))) for k in _GATHER_FLIPS]


def _gather_forward(lands, name):
    n_arr = len(lands)
    chips = (2, 4, 6)

    def body(*refs):
        out_refs = refs[n_arr:2 * n_arr]
        send_sems, recv_sems = refs[2 * n_arr:]
        me = _position()
        sibling = _flip(me, 1)
        sends, arrivals = [], []
        for i, out_ref in enumerate(out_refs):
            m = out_ref.shape[0] // N_DEV

            def copy(pos, j):
                blk = out_ref.at[pl.ds(_index(pos) * m, m), :]
                return pltpu.make_async_remote_copy(src_ref=blk, dst_ref=blk, send_sem=send_sems.at[3 * i + j],
                                                    recv_sem=recv_sems.at[3 * i + j], device_id=sibling, device_id_type=MESH)

            for j, k in enumerate(chips):
                sends.append(copy(_flip(me, k), j))
                arrivals.append(copy(_flip(sibling, k), j))
        for cp in sends:
            cp.start()
        for cp in arrivals:
            cp.wait_recv()
        for cp in sends:
            cp.wait_send()

    anyspec = pl.BlockSpec(memory_space=pl.ANY)
    return pl.pallas_call(
        body, out_shape=[jax.ShapeDtypeStruct(a.shape, a.dtype) for a in lands],
        in_specs=[anyspec] * n_arr, out_specs=[anyspec] * n_arr, input_output_aliases={i: i for i in range(n_arr)},
        scratch_shapes=[pltpu.SemaphoreType.DMA((3 * n_arr,))] * 2, name=name,
    )(*lands)


def _chips_plan(i, src_ref, land_ref, me):
    m = src_ref.shape[0] // 4
    plan = []
    for j, k in enumerate((2, 4, 6)):
        peer = _flip(me, k)
        plan.append((src_ref.at[pl.ds((2 * peer[0] + peer[1]) * m, m), :], land_ref.at[j], peer, land_ref.at[j]))
    return plan


def _exchange_siblings(gs, name):
    n_arr = len(gs)

    def body(*refs):
        g_refs, out_refs = refs[:n_arr], refs[n_arr:2 * n_arr]
        send_sems, recv_sems = refs[2 * n_arr:]
        me = _position()
        c = me[2]
        sibling = _flip(me, 1)
        copies = []
        for i, (g_ref, out_ref) in enumerate(zip(g_refs, out_refs)):
            m_per = g_ref.shape[0] // N_DEV
            for q in range(4):
                copies.append(pltpu.make_async_remote_copy(
                    src_ref=g_ref.at[pl.ds((2 * q + 1 - c) * m_per, m_per), :], dst_ref=out_ref.at[q],
                    send_sem=send_sems.at[4 * i + q], recv_sem=recv_sems.at[4 * i + q],
                    device_id=sibling, device_id_type=MESH))
        for cp in copies:
            cp.start()
        for cp in copies:
            cp.wait_recv()
        for cp in copies:
            cp.wait_send()

    anyspec = pl.BlockSpec(memory_space=pl.ANY)
    return pl.pallas_call(
        body, out_shape=[jax.ShapeDtypeStruct((4, g.shape[0] // N_DEV, g.shape[1]), g.dtype) for g in gs],
        in_specs=[anyspec] * n_arr, out_specs=[anyspec] * n_arr,
        scratch_shapes=[pltpu.SemaphoreType.DMA((4 * n_arr,))] * 2, name=name,
    )(*gs)


def _sum_with_sibling(g, recv, name):
    m = g.shape[0] // N_DEV
    n = g.shape[1]
    tr = _pick(m, (208, 128, 64, 32, 16))
    nt = m // tr

    def body(g_ref, r_ref, o_ref):
        c = lax.axis_index("c")
        own = jnp.where(c == 0, g_ref[0, 0].astype(f32), g_ref[0, 1].astype(f32))
        o_ref[...] = (own + r_ref[0].astype(f32)).astype(o_ref.dtype)

    return pl.pallas_call(
        body, grid=(4, nt),
        in_specs=[pl.BlockSpec((1, 2, tr, n), lambda q, i: (q, 0, i, 0)), pl.BlockSpec((1, tr, n), lambda q, i: (q, i, 0))],
        out_specs=pl.BlockSpec((tr, n), lambda q, i: (q * nt + i, 0)),
        out_shape=jax.ShapeDtypeStruct((4 * m, n), bf16), name=name,
    )(g.reshape(4, 2, m, n), recv)


def _sum_with_chips(h, recv, name):
    m = h.shape[0] // 4
    n = h.shape[1]
    tr = _pick(m, (208, 128, 64, 32, 16))

    def body(h_ref, r_ref, o_ref):
        my_q = 2 * lax.axis_index("x") + lax.axis_index("y")
        own = h_ref[0].astype(f32)
        for q in range(1, 4):
            own = jnp.where(my_q == q, h_ref[q].astype(f32), own)
        o_ref[...] = ((own + r_ref[0].astype(f32)) + r_ref[1].astype(f32)) + r_ref[2].astype(f32)

    return pl.pallas_call(
        body, grid=(m // tr,),
        in_specs=[pl.BlockSpec((4, tr, n), lambda i: (0, i, 0)), pl.BlockSpec((3, tr, n), lambda i: (0, i, 0))],
        out_specs=pl.BlockSpec((tr, n), lambda i: (i, 0)), out_shape=jax.ShapeDtypeStruct((m, n), f32), name=name,
    )(h.reshape(4, m, n), recv)


def _sum_slots(parts, name):
    n_slot, m, n = parts.shape
    tr = _pick(m, (208, 128, 64, 32, 16, 8))

    def body(p_ref, o_ref):
        acc = p_ref[0]
        for s in range(1, n_slot):
            acc = acc + p_ref[s]
        o_ref[...] = acc

    return pl.pallas_call(
        body, grid=(m // tr,), in_specs=[pl.BlockSpec((n_slot, tr, n), lambda i: (0, i, 0))],
        out_specs=pl.BlockSpec((tr, n), lambda i: (i, 0)), out_shape=jax.ShapeDtypeStruct((m, n), parts.dtype), name=name,
    )(parts)


def _reduce_scatter_begin(gs, name):
    from_sibling = _exchange_siblings(gs, "rs_d2d_" + name)
    chip_sums = [_sum_with_sibling(g, r, f"rs_sum2_{name}_{i}") for i, (g, r) in enumerate(zip(gs, from_sibling))]
    lands = [lax.empty((3, h.shape[0] // 4, h.shape[1]), h.dtype) for h in chip_sums]
    return _split_start(chip_sums, lands, _chips_plan, 3, "rs_ici_start_" + name)


def _reduce_scatter_end(started, after, name):
    chip_sums, from_chips = _split_wait(started, _chips_plan, after, "rs_ici_wait_" + name)
    return [_sum_with_chips(h, r, f"rs_sum4_{name}_{i}") for i, (h, r) in enumerate(zip(chip_sums, from_chips))]


def _adamw(w, g, m, v, name):
    shape = w.shape
    n = shape[-1]
    r = w.size // n
    w2, g2, m2, v2 = (a.reshape(r, n) for a in (w, g, m, v))
    tr = _pick(r, (256, 128, 64, 32, 16, 8)) if r * n > 65536 else r

    def body(w_ref, g_ref, m_ref, v_ref, d_ref, mo_ref, vo_ref):
        gg = g_ref[...]
        mm = ADAM_B1 * m_ref[...] + (1.0 - ADAM_B1) * gg
        vv = ADAM_B2 * v_ref[...] + (1.0 - ADAM_B2) * jnp.square(gg)
        m_hat = mm / (1.0 - ADAM_B1 ** ADAM_STEP)
        v_hat = vv / (1.0 - ADAM_B2 ** ADAM_STEP)
        d_ref[...] = -ADAM_LR * (m_hat / (jnp.sqrt(v_hat) + ADAM_EPS) + ADAM_WD * w_ref[...])
        mo_ref[...] = mm
        vo_ref[...] = vv

    spec = pl.BlockSpec((tr, n), lambda i: (i, 0))
    outs = pl.pallas_call(
        body, grid=(r // tr,), in_specs=[spec] * 4, out_specs=[spec] * 3,
        out_shape=[jax.ShapeDtypeStruct((r, n), f32)] * 3, name=name,
    )(w2, g2, m2, v2)
    return tuple(o.reshape(shape) for o in outs)


_SMALL = ("shift_mu", "w_decay0", "a0", "k_k", "k_a", "r_k", "ln_x_w", "ln_x_b", "v_mix0", "lb_logits",
          "g_norm_w", "ln_w", "ln_b")
_NAMES = ("w_in", "shift_mu", "w_decay0", "w_decay_up", "a0", "a_up", "k_k", "k_a", "r_k", "ln_x_w", "ln_x_b",
          "v_mix0", "v_mix_down", "v_mix_up", "lb_logits", "g_norm_w", "w_out", "ln_w", "ln_b")


def _pad_rows(a, rows, at_end):
    z = jnp.zeros((rows - a.shape[0], a.shape[1]), a.dtype)
    return jnp.concatenate([a, z] if at_end else [z, a], axis=0)


def kernel(x, w_in, shift_mu, w_decay0, w_decay_up, a0, a_up, k_k, k_a, r_k, ln_x_w, ln_x_b, v_mix0, v_mix_down, v_mix_up, lb_logits, g_norm_w, w_out, ln_w, ln_b, loss_target, m_w_in, m_shift_mu, m_w_decay0, m_w_decay_up, m_a0, m_a_up, m_k_k, m_k_a, m_r_k, m_ln_x_w, m_ln_x_b, m_v_mix0, m_v_mix_down, m_v_mix_up, m_lb_logits, m_g_norm_w, m_w_out, m_ln_w, m_ln_b, v_w_in, v_shift_mu, v_w_decay0, v_w_decay_up, v_a0, v_a_up, v_k_k, v_k_a, v_r_k, v_ln_x_w, v_ln_x_b, v_v_mix0, v_v_mix_down, v_v_mix_up, v_lb_logits, v_g_norm_w, v_w_out, v_ln_w, v_ln_b):
    weights = dict(w_in=w_in, shift_mu=shift_mu, w_decay0=w_decay0, w_decay_up=w_decay_up, a0=a0, a_up=a_up, k_k=k_k,
                   k_a=k_a, r_k=r_k, ln_x_w=ln_x_w, ln_x_b=ln_x_b, v_mix0=v_mix0, v_mix_down=v_mix_down,
                   v_mix_up=v_mix_up, lb_logits=lb_logits, g_norm_w=g_norm_w, w_out=w_out, ln_w=ln_w, ln_b=ln_b)
    mom1 = dict(w_in=m_w_in, shift_mu=m_shift_mu, w_decay0=m_w_decay0, w_decay_up=m_w_decay_up, a0=m_a0, a_up=m_a_up,
                k_k=m_k_k, k_a=m_k_a, r_k=m_r_k, ln_x_w=m_ln_x_w, ln_x_b=m_ln_x_b, v_mix0=m_v_mix0,
                v_mix_down=m_v_mix_down, v_mix_up=m_v_mix_up, lb_logits=m_lb_logits, g_norm_w=m_g_norm_w,
                w_out=m_w_out, ln_w=m_ln_w, ln_b=m_ln_b)
    mom2 = dict(w_in=v_w_in, shift_mu=v_shift_mu, w_decay0=v_w_decay0, w_decay_up=v_w_decay_up, a0=v_a0, a_up=v_a_up,
                k_k=v_k_k, k_a=v_k_a, r_k=v_r_k, ln_x_w=v_ln_x_w, ln_x_b=v_ln_x_b, v_mix0=v_v_mix0,
                v_mix_down=v_v_mix_down, v_mix_up=v_v_mix_up, lb_logits=v_lb_logits, g_norm_w=v_g_norm_w,
                w_out=v_w_out, ln_w=v_ln_w, ln_b=v_ln_b)
    assert x.shape[0] == 1 and w_in.shape[0] == DEPTH
    t, d = x.shape[1], x.shape[2]
    dr = w_decay0.shape[1]
    dh = g_norm_w.shape[1]
    rank_w, rank_a, rank_v = w_decay_up.shape[1], a_up.shape[1], v_mix_up.shape[1]
    rwc = 4 * dr + rank_w + rank_a
    assert rank_w + rank_a == LANES and rank_v <= LANES and dr + dh == d
    assert t % CHUNK == 0 and dr % LANES == 0 and dh % LANES == 0 and shift_mu.shape[1] == rwc
    n_pair = dr // LANES
    me = _index(_position())

    win_t = [_all_gather_rows(w_in[0].T.astype(bf16), "ag_w_in_0"), None]
    wout = [None, None]
    late_blocks = [w_out[0].astype(bf16), w_in[1].T.astype(bf16), w_out[1].astype(bf16)]
    late_lands = [lax.dynamic_update_slice(lax.empty((N_DEV * blk.shape[0], blk.shape[1]), bf16), blk, (me * blk.shape[0], 0))
                  for blk in late_blocks]
    late_gather = _split_start(late_blocks, late_lands, _gather_plan, len(_GATHER_FLIPS), "ag_late_start")
    shard = dr // N_DEV
    pack = jnp.concatenate([w_decay_up[0], w_decay_up[1], a_up[0], a_up[1], v_mix_up[0], v_mix_down[0].T], axis=0)
    pack = _all_gather_rows(pack, "ag_small")
    pack = jnp.transpose(pack.reshape(N_DEV, -1, shard), (1, 0, 2)).reshape(-1, dr)
    offs = [0, rank_w, 2 * rank_w, 2 * rank_w + rank_a, 2 * rank_w + 2 * rank_a, 2 * rank_w + 2 * rank_a + rank_v,
            2 * rank_w + 2 * rank_a + 2 * rank_v]
    wdu_f = [pack[offs[0]:offs[1]], pack[offs[1]:offs[2]]]
    aup_f = [pack[offs[2]:offs[3]], pack[offs[3]:offs[4]]]
    vup_f = pack[offs[4]:offs[5]]
    vdown_f = pack[offs[5]:offs[6]].T

    def after_start(a, started):
        return a + started[-1][0:1, 0:1]

    def rwkv_params(l):
        mu = after_start(shift_mu[0:1], late_gather) if l == 0 else shift_mu[l:l + 1]
        prm = [mu, w_decay0[l:l + 1], a0[l:l + 1], _pad_rows(wdu_f[l], LANES, True),
               _pad_rows(aup_f[l], LANES, False)]
        if l == 1:
            prm += [v_mix0[0:1], _pad_rows(vdown_f.T, LANES, True).T, _pad_rows(vup_f, LANES, True)]
        rows = jnp.stack([k_k[l], k_a[l], r_k[l], ln_x_w[l], ln_x_b[l]] + [jnp.zeros((dr,), f32)] * 3, axis=0)
        pp = jnp.transpose(rows.reshape(8, n_pair, LANES), (1, 0, 2))
        return tuple(prm), pp

    h = x[0]
    h16 = h.astype(bf16)
    tgt = loss_target[0]
    saved = []
    vfirst = None
    for l in range(DEPTH):
        prm, pp = rwkv_params(l)
        proj = _matmul(h16, win_t[l], "nt", f"mm_proj_{l}", (2048, 640, 2048))
        if l == 0:
            cat, vfirst, mck = _rwkv_fwd(False, proj, None, prm, pp, d)
        else:
            cat, mck = _rwkv_fwd(True, proj, vfirst, prm, pp, d)
        cat, sck = _hgrn_fwd(l == 1, proj, lb_logits, g_norm_w[l:l + 1], cat, rwc)
        if l == 0:
            _, arrived = _split_wait(late_gather, _gather_plan, cat, "ag_late_wait")
            wout[0], win_t[1], wout[1] = _gather_forward(arrived, "ag_late_forward")
        y = _matmul(cat, wout[l], "nn", f"mm_out_{l}", (1024, 1024, 2048))
        saved.append((h, h16, proj, prm, pp, mck, sck, cat, y))
        if l < DEPTH - 1:
            h, h16 = _ln_fwd(h, y, ln_w[l:l + 1], ln_b[l:l + 1])
        else:
            dh_out, loss_part = _ln_loss(h, y, ln_w[l:l + 1], ln_b[l:l + 1], tgt)
    loss = lax.psum(loss_part[0, 0], ("x", "y", "c"))

    grads = {}
    big = {}
    dvfirst = None
    d_lbl = None
    rs_started = {}
    for l in reversed(range(DEPTH)):
        h_l, h16_l, proj, prm, pp, mck, sck, cat, y = saved[l]
        ln_w_l = ln_w[l:l + 1] if l == DEPTH - 1 else after_start(ln_w[l:l + 1], rs_started[l + 1])
        dy, dy16, g_ln_w, g_ln_b = _ln_bwd(h_l, y, ln_w_l, ln_b[l:l + 1], dh_out)
        dcat = _matmul(dy16, wout[l], "nt", f"mm_dcat_{l}", (1024, 1024, 2048))
        big[("w_out", l)] = _matmul(cat, dy16, "tn", f"mm_dwout_{l}", (512, 2048, 2048), out_dtype=bf16)
        if l == 1:
            outs = _rwkv_bwd(True, proj, vfirst, prm, pp, mck, dcat, None)
            dproj_r, dvfirst = outs[0], outs[1]
            dprm, dpp = outs[2:-1], outs[-1]
        else:
            outs = _rwkv_bwd(False, proj, None, prm, pp, mck, dcat, dvfirst)
            dproj_r = outs[0]
            dprm, dpp = outs[1:-1], outs[-1]
        dproj_h, dlbl_l, dgnw = _hgrn_bwd(l == 1, proj, lb_logits, g_norm_w[l:l + 1], sck, dcat, rwc)
        dproj = jnp.concatenate([dproj_r] + [dproj_h[i] for i in range(4)], axis=1)
        big[("w_in", l)] = _matmul(dproj, h16_l, "tn", f"mm_dwin_{l}", (640, 2048, 2048), out_dtype=bf16)
        sharded = [dprm[3][:rank_w].T, dprm[4][rank_w:].T]
        if l == 1:
            sharded += [dprm[6][:, :rank_v], dprm[7][:rank_v].T,
                        jnp.zeros((dr, LANES - 2 * rank_v), f32)]
        sharded = jnp.concatenate(sharded, axis=1).astype(bf16)
        rs_started[l] = _reduce_scatter_begin([big[("w_in", l)], big[("w_out", l)], sharded], f"l{l}")
        dy_res = after_start(dy, rs_started[l]) if l == 0 else dy
        dh_out = _matmul(dproj, win_t[l], "nn", f"mm_dh_{l}", (1024, 1024, 1664), add=dy_res, add_scale=ALPHA)
        dpp = jnp.transpose(dpp, (1, 0, 2)).reshape(8, dr)
        grads[l] = dict(shift_mu=dprm[0][0], w_decay0=dprm[1][0], a0=dprm[2][0],
                        k_k=dpp[0], k_a=dpp[1], r_k=dpp[2], ln_x_w=dpp[3], ln_x_b=dpp[4],
                        g_norm_w=dgnw[0], ln_w=g_ln_w[0], ln_b=g_ln_b[0])
        if l == 1:
            grads[l].update(v_mix0=dprm[5][0])
            d_lbl = dlbl_l
    grad_x = dh_out[None]

    def both(name):
        return jnp.stack([grads[0][name], grads[1][name]])

    small = dict(shift_mu=both("shift_mu"), w_decay0=both("w_decay0"), a0=both("a0"), k_k=both("k_k"), k_a=both("k_a"),
                 r_k=both("r_k"), ln_x_w=both("ln_x_w"), ln_x_b=both("ln_x_b"), v_mix0=grads[1]["v_mix0"][None],
                 lb_logits=d_lbl, g_norm_w=both("g_norm_w"), ln_w=both("ln_w"), ln_b=both("ln_b"))
    flat = jnp.concatenate([small[nm].reshape(-1) for nm in _SMALL])
    n_flat = flat.shape[0]
    rows = -(-n_flat // (8 * LANES)) * 8
    flat = jnp.concatenate([flat, jnp.zeros((rows * LANES - n_flat,), f32)]).reshape(rows, LANES)
    total = _sum_slots(_all_gather_rows(flat, "ag_small_grads").reshape(N_DEV, rows, LANES), "sum_small_grads").reshape(-1)
    gsm = {}
    off = 0
    for nm in _SMALL:
        size = small[nm].size
        gsm[nm] = total[off:off + size].reshape(small[nm].shape)
        off += size
    reduced = {1: _reduce_scatter_end(rs_started[1], dh_out, "l1")}
    reduced[0] = _reduce_scatter_end(rs_started[0], total, "l0")
    gsm["w_in"] = jnp.stack([reduced[l][0].T for l in range(DEPTH)])
    gsm["w_out"] = jnp.stack([reduced[l][1] for l in range(DEPTH)])
    gsm["w_decay_up"] = jnp.stack([reduced[l][2][:, :rank_w].T for l in range(DEPTH)])
    gsm["a_up"] = jnp.stack([reduced[l][2][:, rank_w:rank_w + rank_a].T for l in range(DEPTH)])
    gsm["v_mix_down"] = reduced[1][2][:, LANES:LANES + rank_v][None]
    gsm["v_mix_up"] = reduced[1][2][:, LANES + rank_v:LANES + 2 * rank_v].T[None]

    deltas, new_m, new_v = {}, {}, {}
    for nm in _NAMES:
        deltas[nm], new_m[nm], new_v[nm] = _adamw(weights[nm], gsm[nm], mom1[nm], mom2[nm], "adamw_" + nm)
    return (loss, grad_x, *[gsm[nm] for nm in _NAMES], *[deltas[nm] for nm in _NAMES],
            *[new_m[nm] for nm in _NAMES], *[new_v[nm] for nm in _NAMES])
```

```python
import functools

import jax
import jax.numpy as jnp
from jax import lax
from jax.experimental import pallas as pl
from jax.experimental.pallas import tpu as pltpu

f32 = jnp.float32
bf16 = jnp.bfloat16

N_DEV = 8
CHUNK = 64
LANES = 128
RWKV_HEAD = 64
DEPTH = 2
ALPHA = (2 * DEPTH) ** 0.25
LN_EPS = 1e-5
GN_EPS = 64e-5
RMS_EPS = 1e-5
LB_FLOOR = 1e-30
ADAM_LR, ADAM_B1, ADAM_B2, ADAM_EPS, ADAM_WD, ADAM_STEP = 0.001, 0.9, 0.999, 1e-08, 0.01, 10
MESH = pl.DeviceIdType.MESH


def _iota(shape, d):
    return lax.broadcasted_iota(jnp.int32, shape, d)


_DIMS = {"nn": (((1,), (0,)), ((), ())), "nt": (((1,), (1,)), ((), ())), "tn": (((0,), (0,)), ((), ()))}
_BATCH_DIMS = {"nn": (((2,), (1,)), ((0,), (0,))), "nt": (((2,), (2,)), ((0,), (0,))), "tn": (((1,), (1,)), ((0,), (0,)))}
_K_AXES = {"nn": (-1, -2), "nt": (-1, -1), "tn": (-2, -2)}


def _mxu(a, b, mode):
    return lax.dot_general(a, b, (_BATCH_DIMS if a.ndim == 3 else _DIMS)[mode], preferred_element_type=f32)


def _split(x):
    hi = x.astype(bf16)
    return hi, (x - hi.astype(f32)).astype(bf16)


def _mm2_impl(a, b, mode, passes=3):
    ah, al = _split(a)
    if passes == 3:
        bh, bl = _split(b)
        lhs, rhs = [ah, ah, al], [bh, bl, bh]
    else:
        bh = b.astype(bf16)
        lhs, rhs = [ah, al], [bh, bh]
    ka, kb = _K_AXES[mode]
    k = a.shape[ka]
    if k % (LANES if -1 in (ka, kb) else 16) == 0:
        return _mxu(jnp.concatenate(lhs, axis=ka), jnp.concatenate(rhs, axis=kb), mode)
    out = _mxu(lhs[0], rhs[0], mode)
    for x, y in zip(lhs[1:], rhs[1:]):
        out = out + _mxu(x, y, mode)
    return out


@functools.partial(jax.custom_vjp, nondiff_argnums=(2, 3))
def _mm2(a, b, mode, passes=3):
    return _mm2_impl(a, b, mode, passes)


def _mm2_fwd(a, b, mode, passes):
    return _mm2_impl(a, b, mode, passes), (a, b)


def _mm2_bwd(mode, passes, res, g):
    a, b = res
    if mode == "nn":
        return _mm2_impl(g, b, "nt", passes), _mm2_impl(a, g, "tn", passes)
    if mode == "nt":
        return _mm2_impl(g, b, "nn", passes), _mm2_impl(g, a, "tn", passes)
    return _mm2_impl(b, g, "nt", passes), _mm2_impl(a, g, "nn", passes)


_mm2.defvjp(_mm2_fwd, _mm2_bwd)

TRI_PASSES = 2
APPLY_PASSES = 2


def _const_impl(cm, x, mode):
    hi, lo = _split(x)
    if mode in ("r", "rt"):
        shape = x.shape
        hi, lo = hi.reshape(-1, shape[-1]), lo.reshape(-1, shape[-1])
        dims = "nn" if mode == "r" else "nt"
        out = _mxu(hi, cm, dims) + _mxu(lo, cm, dims)
        return out.reshape(shape[:-1] + (out.shape[-1],))
    if x.ndim == 3:
        cm = jnp.broadcast_to(cm, (x.shape[0],) + cm.shape)
    return _mxu(cm, hi, mode) + _mxu(cm, lo, mode)


@jax.custom_vjp
def _const_left(cm, x):
    return _const_impl(cm, x, "nn")


_const_left.defvjp(lambda cm, x: (_const_impl(cm, x, "nn"), cm),
                   lambda cm, g: (jnp.zeros_like(cm), _const_impl(cm, g, "tn")))


@jax.custom_vjp
def _const_right(x, cm):
    return _const_impl(cm, x, "r")


_const_right.defvjp(lambda x, cm: (_const_impl(cm, x, "r"), cm),
                    lambda cm, g: (_const_impl(cm, g, "rt"), jnp.zeros_like(cm)))


def _tri_inv(a):
    n = a.shape[-1]
    tm = (_iota((n, n), 0) == _iota((n, n), 1)).astype(f32) + a
    ak = a
    for _ in range(5):
        ak = _mm2_impl(ak, ak, "nn", TRI_PASSES)
        tm = tm + _mm2_impl(tm, ak, "nn", TRI_PASSES)
    return tm


@jax.custom_vjp
def _tri_solve(tm, a, x):
    del a
    return _mm2_impl(tm, x, "nn")


def _tri_solve_fwd(tm, a, x):
    u = _mm2_impl(tm, x, "nn")
    return u, (tm, u)


def _tri_solve_bwd(res, du):
    tm, u = res
    dx = _mm2_impl(tm, du, "tn")
    return jnp.zeros_like(tm), _mm2_impl(dx, u, "nt"), dx


_tri_solve.defvjp(_tri_solve_fwd, _tri_solve_bwd)


def _col_of_row(row_vec):
    n = row_vec.shape[-1]
    eye = _iota((n, n), 0) == _iota((n, n), 1)
    return jnp.sum(jnp.where(eye, jnp.broadcast_to(row_vec, row_vec.shape[:-2] + (n, n)), 0.0), axis=-1, keepdims=True)


def _softplus(x):
    return jnp.maximum(x, 0.0) + jnp.log1p(jnp.exp(-jnp.abs(x)))


def _log_sigmoid(x):
    return -_softplus(-x)


def _logaddexp(a, b):
    return jnp.maximum(a, b) + jnp.log1p(jnp.exp(-jnp.abs(a - b)))


def _silu(x):
    return x * jax.nn.sigmoid(x)


def _tril(c, strict):
    r, s = _iota((c, c), 0), _iota((c, c), 1)
    return (r > s) if strict else (r >= s)


def _last_row(a):
    c = a.shape[-2]
    return jnp.sum(jnp.where(_iota(a.shape, a.ndim - 2) == c - 1, a, 0.0), axis=-2, keepdims=True)


def _rwkv_pre(layer1, prm, y, prev, vf):
    c = y.shape[0]
    if layer1:
        mu, w0, a0, wup, aup, v0, vdown, vup = prm
    else:
        mu, w0, a0, wup, aup = prm
    dr = w0.shape[1]
    shift = (_iota((c, c), 0) == _iota((c, c), 1) + 1).astype(bf16)
    y_prev = _const_left(shift, y) + jnp.where(_iota((c, 1), 0) == 0, prev, 0.0)
    rw = y + mu * (y_prev - y)
    r, k, v, z = (rw[:, i * dr:(i + 1) * dr] for i in range(4))
    wdad = rw[:, 4 * dr:4 * dr + LANES]
    w_raw = w0 + _mm2(jnp.tanh(wdad), wup, "nn")
    lw = -jnp.exp(-_softplus(-w_raw) - 0.5)
    asig = jax.nn.sigmoid(a0 + _mm2(wdad, aup, "nn"))
    if layer1:
        v = v + (vf - v) * jax.nn.sigmoid(v0 + _mm2(_mm2(v, vdown, "nn"), vup, "nn"))
    return r, k, v, z, lw, asig


def _rwkv_pair(pp, m0, xs, tm=None):
    kkw, kaw, rkw, gnw, gnb = pp
    r, k, v, z, lw, asig = xs
    c = r.shape[-2]
    n2 = 2 * c
    lane = _iota((1, LANES), 1)
    mh0, mh1 = (lane < RWKV_HEAD).astype(f32), (lane >= RWKV_HEAD).astype(f32)
    same_head = _iota((LANES, LANES), 0) // RWKV_HEAD == _iota((LANES, LANES), 1) // RWKV_HEAD
    g = same_head.astype(bf16)

    def seg(x):
        return _const_right(x, g)

    def stack(x):
        return jnp.concatenate([x * mh0, x * mh1], axis=-2)

    kk = k * kkw
    kk = kk / jnp.maximum(jnp.sqrt(seg(kk * kk)), 1e-12)
    k2 = k * (1.0 + (asig - 1.0) * kaw)
    a = -kk
    b = kk * asig
    cum = _const_left(_tril(c, False).astype(bf16), lw)
    at = stack(a * jnp.exp(cum - lw))
    rt = stack(r * jnp.exp(cum))
    en = jnp.exp(-cum)
    sc = _mm2(jnp.concatenate([at, rt], axis=-2), jnp.concatenate([stack(b * en), stack(k2 * en)], axis=-2), "nt")
    row, col = _iota((n2, n2), 0), _iota((n2, n2), 1)
    same = row // c == col // c
    strict = same & (row % c > col % c)
    incl = same & (row % c >= col % c)
    aab = jnp.where(strict, sc[..., :n2, :n2], 0.0)
    aak = jnp.where(strict, sc[..., :n2, n2:], 0.0)
    arb = jnp.where(incl, sc[..., n2:, :n2], 0.0)
    ark = jnp.where(incl, sc[..., n2:, n2:], 0.0)
    vv = jnp.concatenate([v, v], axis=-2)
    mask_st = jnp.concatenate([jnp.broadcast_to(mh0, (c, LANES)), jnp.broadcast_to(mh1, (c, LANES))], axis=0)
    x_st = _mm2(jnp.concatenate([at, aak], axis=-1), jnp.concatenate([m0, vv], axis=-2), "nn", APPLY_PASSES)
    if tm is None:
        tm = _tri_inv(lax.stop_gradient(aab))
    u_st = _tri_solve(tm, aab, x_st) * mask_st
    o_st = _mm2(jnp.concatenate([rt, arb, ark], axis=-1), jnp.concatenate([m0, u_st, vv], axis=-2), "nn", APPLY_PASSES) * mask_st
    u = u_st[..., :c, :] + u_st[..., c:, :]
    o = o_st[..., :c, :] + o_st[..., c:, :]
    cum_last = _last_row(cum)
    dec_end = jnp.exp(cum_last - cum)
    m_new = _col_of_row(jnp.exp(cum_last)) * m0 + _mm2(
        jnp.concatenate([b * dec_end, k2 * dec_end], axis=-2), jnp.concatenate([u, v], axis=-2), "tn", APPLY_PASSES) * same_head.astype(f32)
    mean = seg(o) * (1.0 / RWKV_HEAD)
    d = o - mean
    var = seg(d * d) * (1.0 / RWKV_HEAD)
    on = d * lax.rsqrt(var + GN_EPS) * gnw + gnb
    bonus = seg(r * k2 * rkw) * v
    return (on + bonus) * _silu(z), m_new, tm


def _split_lanes(a, n):
    return [a[:, i * LANES:(i + 1) * LANES] for i in range(n)]


def _rwkv_step(layer1, prm, y, prev, vf, pp, m0, tm=None):
    xs = _rwkv_pre(layer1, prm, y, prev, vf)
    n_pair = m0.shape[0]
    og, m_new, tm = _rwkv_pair(pp, m0, tuple(jnp.concatenate([p[None] for p in _split_lanes(a, n_pair)], axis=0) for a in xs), tm)
    return og, m_new, xs[2], tm


def _group(n):
    return n


def _rwkv_specs(layer1, t, dr, rwc, n_pair, rev):
    nc = t // CHUNK
    grp = _group(n_pair)

    def cidx(c):
        return (nc - 1 - c) if rev else c

    full = lambda shape: pl.BlockSpec(shape, lambda c, p: tuple(0 for _ in shape))
    specs = [
        pl.BlockSpec((CHUNK, rwc), lambda c, p: (cidx(c), 0)),
        pl.BlockSpec((8, rwc), lambda c, p: (jnp.maximum(cidx(c) * (CHUNK // 8) - 1, 0), 0)),
    ]
    if layer1:
        specs.append(pl.BlockSpec((CHUNK, dr), lambda c, p: (cidx(c), 0)))
    prm_shapes = [(1, rwc), (1, dr), (1, dr), (LANES, dr), (LANES, dr)]
    if layer1:
        prm_shapes += [(1, dr), (dr, LANES), (LANES, dr)]
    specs += [full(s) for s in prm_shapes]
    specs.append(pl.BlockSpec((grp, 8, LANES), lambda c, p: (p, 0, 0)))
    return specs, prm_shapes, cidx, full


def _rwkv_fwd(layer1, proj, vf, prm, pp, cat_width):
    t = proj.shape[0]
    dr = prm[1].shape[1]
    rwc = prm[0].shape[1]
    n_pair = dr // LANES
    nc = t // CHUNK
    n_prm = len(prm)
    specs, _, _, _ = _rwkv_specs(layer1, t, dr, rwc, n_pair, False)

    def body(*refs):
        y_ref, prev_ref = refs[0], refs[1]
        i = 2
        vf_ref = None
        if layer1:
            vf_ref = refs[i]
            i += 1
        prm_refs = refs[i:i + n_prm]
        i += n_prm
        pp_ref = refs[i]
        i += 1
        cat_ref = refs[i]
        i += 1
        vout_ref = None
        if not layer1:
            vout_ref = refs[i]
            i += 1
        mck_ref, m_s = refs[i], refs[i + 1]
        c = pl.program_id(0)

        @pl.when(c == 0)
        def _():
            m_s[...] = jnp.zeros_like(m_s)

        prev = prev_ref[pl.ds(7, 1), :] * (c != 0).astype(f32)
        m0 = m_s[...]
        ppv = tuple(pp_ref[:, pl.ds(q, 1), :] for q in range(5))
        og, m_new, v, tm = _rwkv_step(layer1, tuple(r[...] for r in prm_refs), y_ref[...], prev,
                                      vf_ref[...] if layer1 else None, ppv, m0)
        mck_ref[0, :n_pair] = m0
        mck_ref[0, n_pair:] = tm
        if not layer1:
            vout_ref[...] = v
        for j in range(n_pair):
            cat_ref[:, j * LANES:(j + 1) * LANES] = og[j]
        m_s[...] = m_new

    grp = _group(n_pair)
    assert grp == n_pair
    out_shape = [jax.ShapeDtypeStruct((t, cat_width), f32)]
    out_specs = [pl.BlockSpec((CHUNK, grp * LANES), lambda c, p: (c, p))]
    if not layer1:
        out_shape.append(jax.ShapeDtypeStruct((t, dr), f32))
        out_specs.append(pl.BlockSpec((CHUNK, dr), lambda c, p: (c, 0)))
    out_shape.append(jax.ShapeDtypeStruct((nc, 2 * n_pair, LANES, LANES), f32))
    out_specs.append(pl.BlockSpec((1, 2 * grp, LANES, LANES), lambda c, p: (c, p, 0, 0)))
    args = [proj, proj] + ([vf] if layer1 else []) + list(prm) + [pp]
    return pl.pallas_call(
        body, grid=(nc, 1), in_specs=specs, out_specs=out_specs, out_shape=out_shape,
        scratch_shapes=[pltpu.VMEM((n_pair, LANES, LANES), f32)],
        compiler_params=pltpu.CompilerParams(dimension_semantics=("arbitrary", "arbitrary")),
        name=f"rwkv_fwd_l{int(layer1)}",
    )(*args)


def _rwkv_bwd(layer1, proj, vf, prm, pp, mck, dcat, dvout):
    t = proj.shape[0]
    dr = prm[1].shape[1]
    rwc = prm[0].shape[1]
    n_pair = dr // LANES
    nc = t // CHUNK
    n_prm = len(prm)
    specs, prm_shapes, cidx, full = _rwkv_specs(layer1, t, dr, rwc, n_pair, True)
    grp = _group(n_pair)
    assert grp == n_pair
    specs.append(pl.BlockSpec((1, 2 * grp, LANES, LANES), lambda c, p: (cidx(c), p, 0, 0)))
    specs.append(pl.BlockSpec((CHUNK, grp * LANES), lambda c, p: (cidx(c), p)))
    if not layer1:
        specs.append(pl.BlockSpec((CHUNK, dr), lambda c, p: (cidx(c), 0)))

    def body(*refs):
        y_ref, prev_ref = refs[0], refs[1]
        i = 2
        vf_ref = None
        if layer1:
            vf_ref = refs[i]
            i += 1
        prm_refs = refs[i:i + n_prm]
        i += n_prm
        pp_ref, mck_ref, dog_ref = refs[i], refs[i + 1], refs[i + 2]
        i += 3
        dvout_ref = None
        if not layer1:
            dvout_ref = refs[i]
            i += 1
        dy_ref = refs[i]
        i += 1
        dvf_ref = None
        if layer1:
            dvf_ref = refs[i]
            i += 1
        dprm_refs = refs[i:i + n_prm]
        i += n_prm
        dpp_ref = refs[i]
        dm_s, dprev_s = refs[i + 1:i + 3]
        c = pl.program_id(0)
        cr = nc - 1 - c

        @pl.when(c == 0)
        def _():
            dm_s[...] = jnp.zeros_like(dm_s)
            dprev_s[...] = jnp.zeros_like(dprev_s)
            dpp_ref[...] = jnp.zeros_like(dpp_ref)
            for r in dprm_refs:
                r[...] = jnp.zeros_like(r)

        prev = prev_ref[pl.ds(7, 1), :] * (cr != 0).astype(f32)
        prm_v = tuple(r[...] for r in prm_refs)
        ppv = tuple(pp_ref[:, pl.ds(q, 1), :] for q in range(5))
        dog = jnp.stack([dog_ref[:, j * LANES:(j + 1) * LANES] for j in range(n_pair)], axis=0)
        m0, tm = mck_ref[0, :n_pair], mck_ref[0, n_pair:]
        no_tm = jnp.zeros_like(tm)
        if layer1:
            _, vjp = jax.vjp(lambda a, b, d, e, g, h: _rwkv_step(True, a, b, d, e, g, h, tm),
                             prm_v, y_ref[...], prev, vf_ref[...], ppv, m0)
            dprm, dy, dprev, dvf, dppv, dm0 = vjp((dog, dm_s[...], jnp.zeros((CHUNK, dr), f32), no_tm))
            dvf_ref[...] = dvf
        else:
            _, vjp = jax.vjp(lambda a, b, d, e, g: _rwkv_step(False, a, b, d, None, e, g, tm), prm_v, y_ref[...], prev, ppv, m0)
            dprm, dy, dprev, dppv, dm0 = vjp((dog, dm_s[...], dvout_ref[...], no_tm))
        dm_s[...] = dm0
        for q in range(5):
            dpp_ref[:, pl.ds(q, 1), :] += dppv[q]
        dy_ref[...] = (dy + jnp.where(_iota((CHUNK, 1), 0) == CHUNK - 1, dprev_s[...], 0.0)).astype(bf16)
        dprev_s[...] = dprev
        for r, gval in zip(dprm_refs, dprm):
            r[...] += gval

    out_shape = [jax.ShapeDtypeStruct((t, proj.shape[1]), bf16)]
    out_specs = [pl.BlockSpec((CHUNK, rwc), lambda c, p: (cidx(c), 0))]
    if layer1:
        out_shape.append(jax.ShapeDtypeStruct((t, dr), f32))
        out_specs.append(pl.BlockSpec((CHUNK, dr), lambda c, p: (cidx(c), 0)))
    out_shape += [jax.ShapeDtypeStruct(s, f32) for s in prm_shapes]
    out_specs += [full(s) for s in prm_shapes]
    out_shape.append(jax.ShapeDtypeStruct((n_pair, 8, LANES), f32))
    out_specs.append(full((n_pair, 8, LANES)))
    args = [proj, proj] + ([vf] if layer1 else []) + list(prm) + [pp, mck, dcat] + ([] if layer1 else [dvout])
    return pl.pallas_call(
        body, grid=(nc, 1), in_specs=specs, out_specs=out_specs, out_shape=out_shape,
        scratch_shapes=[pltpu.VMEM((n_pair, LANES, LANES), f32), pltpu.VMEM((1, rwc), f32)],
        compiler_params=pltpu.CompilerParams(dimension_semantics=("arbitrary", "arbitrary")),
        name=f"rwkv_bwd_l{int(layer1)}",
    )(*args)


def _hgrn_chunk(layer1, lbl, gnw, s0, q_raw, f_raw, i_in, z):
    c = q_raw.shape[-2]
    q = _silu(q_raw)
    ls = _log_sigmoid(f_raw)
    if layer1:
        l0, l1 = lbl[..., 0:1, :], lbl[..., 1:2, :]
        mx = jnp.maximum(l0, l1)
        e0, e1 = jnp.exp(l0 - mx), jnp.exp(l1 - mx)
        sm0, sm1 = e0 / (e0 + e1), e1 / (e0 + e1)
        lb = (sm0 + sm1) - sm0
        log_f = _logaddexp(jnp.log(jnp.maximum(lb, LB_FLOOR)), jnp.log1p(-lb) + ls)
        k = (1.0 - lb) * jax.nn.sigmoid(-f_raw)
    else:
        log_f = _logaddexp(jnp.full_like(ls, jnp.log(jnp.float32(LB_FLOOR))), ls)
        k = jax.nn.sigmoid(-f_raw)
    row, col = _iota((c, c), 0), _iota((c, c), 1)
    trow = _iota((c, 1), 0)
    halves = []
    half = c // 2
    while half >= 1:
        halves.append(half)
        half //= 2
    cmat = jnp.concatenate([(col <= row).astype(f32)]
                           + [(col <= (row // (2 * hf)) * (2 * hf) + hf - 1).astype(f32) for hf in halves], axis=0)
    ball = _const_left(cmat.astype(bf16), log_f)
    b = ball[..., :c, :]
    att = None
    for lvl, hf in enumerate(halves):
        blk = 2 * hf
        bref = ball[..., (lvl + 1) * c:(lvl + 2) * c, :]
        upper = (trow % blk) >= hf
        qh = q * jnp.exp(jnp.where(upper, b - bref, 0.0)) * upper.astype(f32)
        kh = k * jnp.exp(jnp.where(upper, 0.0, bref - b)) * (1.0 - upper.astype(f32))
        term = jnp.where(row // blk == col // blk, _mm2(qh, kh, "nt", APPLY_PASSES), 0.0)
        att = term if att is None else att + term
    lhs = jnp.concatenate([q * jnp.exp(b), att, jnp.zeros(att.shape[:-1] + (LANES - c,), f32)], axis=-1)
    rhs = jnp.concatenate([s0, i_in, jnp.zeros(i_in.shape[:-2] + (LANES - c, i_in.shape[-1]), f32)], axis=-2)
    o = _mm2(lhs, rhs, "nn", APPLY_PASSES) + jnp.sum(q * k, axis=-1, keepdims=True) * i_in
    b_last = _last_row(b)
    s_new = _col_of_row(jnp.exp(b_last)) * s0 + _mm2(k * jnp.exp(b_last - b), i_in, "tn", APPLY_PASSES)
    o = o * lax.rsqrt(jnp.mean(o * o, axis=-1, keepdims=True) + RMS_EPS)
    return o * gnw * _silu(z), s_new


def _hgrn_in_specs(t, dh, col0, rev):
    nc = t // CHUNK
    nh = dh // LANES

    def cidx(c):
        return (nc - 1 - c) if rev else c

    grp = _group(nh)
    specs = [pl.BlockSpec((CHUNK, LANES), functools.partial(lambda g, j, h, c: (cidx(c), col0 + g * nh + h * grp + j), g, j))
             for j in range(grp) for g in range(4)]
    specs.append(pl.BlockSpec((2, grp * LANES), lambda h, c: (0, h)))
    specs.append(pl.BlockSpec((1, grp * LANES), lambda h, c: (0, h)))
    return specs, cidx, grp


def _hgrn_fwd(layer1, proj, lbl, gnw, cat, rwc):
    t, d = cat.shape
    dh = gnw.shape[1]
    nh = dh // LANES
    nc = t // CHUNK
    col0 = rwc // LANES
    specs, _, grp = _hgrn_in_specs(t, dh, col0, False)
    specs.append(pl.BlockSpec(memory_space=pl.ANY))
    assert (d - dh) % (grp * LANES) == 0
    cat_col0 = (d - dh) // (grp * LANES)

    def body(*refs):
        x_refs = refs[:4 * grp]
        lbl_ref, gnw_ref, _, cat_ref, sck_ref, s_s = refs[4 * grp:]
        c = pl.program_id(1)

        @pl.when(c == 0)
        def _():
            s_s[...] = jnp.zeros_like(s_s)

        lanes = [slice(j * LANES, (j + 1) * LANES) for j in range(grp)]
        s0 = s_s[...]
        sck_ref[:, 0] = s0
        out, s_new = _hgrn_chunk(layer1, jnp.stack([lbl_ref[:, ln] for ln in lanes]), jnp.stack([gnw_ref[:, ln] for ln in lanes]),
                                 s0, *(jnp.stack([x_refs[4 * j + g][...] for j in range(grp)]) for g in range(4)))
        for j in range(grp):
            cat_ref[:, lanes[j]] = out[j]
        s_s[...] = s_new

    return pl.pallas_call(
        body, grid=(nh // grp, nc), in_specs=specs,
        out_specs=[pl.BlockSpec((CHUNK, grp * LANES), lambda h, c: (c, cat_col0 + h)),
                   pl.BlockSpec((grp, 1, LANES, LANES), lambda h, c: (h, c, 0, 0))],
        out_shape=[jax.ShapeDtypeStruct((t, d), f32), jax.ShapeDtypeStruct((nh, nc, LANES, LANES), f32)],
        scratch_shapes=[pltpu.VMEM((grp, LANES, LANES), f32)],
        input_output_aliases={4 * grp + 2: 0},
        compiler_params=pltpu.CompilerParams(dimension_semantics=("arbitrary", "arbitrary")),
        name=f"hgrn_fwd_l{int(layer1)}",
    )(*([proj] * (4 * grp)), lbl, gnw, cat)


def _hgrn_bwd(layer1, proj, lbl, gnw, sck, dcat, rwc, dproj):
    t, d = dcat.shape
    dh = gnw.shape[1]
    nh = dh // LANES
    nc = t // CHUNK
    col0 = rwc // LANES
    specs, cidx, grp = _hgrn_in_specs(t, dh, col0, True)
    assert grp == nh and (d - dh) % (grp * LANES) == 0
    cat_col0 = (d - dh) // (grp * LANES)
    specs.append(pl.BlockSpec((grp, 1, LANES, LANES), lambda h, c: (h, cidx(c), 0, 0)))
    specs.append(pl.BlockSpec((CHUNK, grp * LANES), lambda h, c: (cidx(c), cat_col0 + h)))
    specs.append(pl.BlockSpec(memory_space=pl.ANY))

    def body(*refs):
        x_refs = refs[:4 * grp]
        lbl_ref, gnw_ref, sck_ref, do_ref, _, dp_hbm, dlbl_ref, dgnw_ref, ds_s, stage, sems = refs[4 * grp:]
        c = pl.program_id(1)
        slot = c % 2

        def put(s, g, chunk):
            return pltpu.make_async_copy(stage.at[s, g], dp_hbm.at[pl.ds(chunk * CHUNK, CHUNK), pl.ds(rwc + g * dh, dh)],
                                         sems.at[s, g])

        @pl.when(c == 0)
        def _():
            ds_s[...] = jnp.zeros_like(ds_s)
            dlbl_ref[...] = jnp.zeros_like(dlbl_ref)
            dgnw_ref[...] = jnp.zeros_like(dgnw_ref)

        @pl.when(c >= 2)
        def _():
            for g in range(4):
                put(slot, g, 0).wait()

        lanes = [slice(j * LANES, (j + 1) * LANES) for j in range(grp)]
        _, vjp = jax.vjp(functools.partial(_hgrn_chunk, layer1),
                         jnp.stack([lbl_ref[:, ln] for ln in lanes]), jnp.stack([gnw_ref[:, ln] for ln in lanes]), sck_ref[:, 0],
                         *(jnp.stack([x_refs[4 * j + g][...] for j in range(grp)]) for g in range(4)))
        dlbl, dgnw, ds0, dq, df, di, dz = vjp((jnp.stack([do_ref[:, ln] for ln in lanes]), ds_s[...]))
        ds_s[...] = ds0
        for j in range(grp):
            dlbl_ref[:, lanes[j]] += dlbl[j]
            dgnw_ref[:, lanes[j]] += dgnw[j]
            for g, val in enumerate((dq, df, di, dz)):
                stage[slot, g, :, lanes[j]] = val[j].astype(bf16)
        for g in range(4):
            put(slot, g, nc - 1 - c).start()

        @pl.when(c == nc - 1)
        def _():
            for g in range(4):
                put(slot, g, 0).wait()
                if nc >= 2:
                    put(1 - slot, g, 0).wait()

    return pl.pallas_call(
        body, grid=(1, nc), in_specs=specs,
        out_specs=[pl.BlockSpec(memory_space=pl.ANY),
                   pl.BlockSpec((2, grp * LANES), lambda h, c: (0, h)),
                   pl.BlockSpec((1, grp * LANES), lambda h, c: (0, h))],
        out_shape=[jax.ShapeDtypeStruct(dproj.shape, dproj.dtype), jax.ShapeDtypeStruct((2, dh), f32),
                   jax.ShapeDtypeStruct((1, dh), f32)],
        scratch_shapes=[pltpu.VMEM((grp, LANES, LANES), f32), pltpu.VMEM((2, 4, CHUNK, dh), bf16),
                        pltpu.SemaphoreType.DMA((2, 4))],
        input_output_aliases={4 * grp + 4: 0},
        compiler_params=pltpu.CompilerParams(dimension_semantics=("arbitrary", "arbitrary")),
        name=f"hgrn_bwd_l{int(layer1)}",
    )(*([proj] * (4 * grp)), lbl, gnw, sck, dcat, dproj)


def _ln(h, y, w, b):
    u = ALPHA * h + y
    mu = jnp.mean(u, axis=-1, keepdims=True)
    var = jnp.mean(jnp.square(u - mu), axis=-1, keepdims=True)
    return (u - mu) * lax.rsqrt(var + LN_EPS) * w + b


def _row_tile(t):
    return 256 if t % 256 == 0 else t


def _ln_fwd(h, y, w, b):
    t, d = h.shape
    tr = _row_tile(t)

    def body(h_ref, y_ref, w_ref, b_ref, o_ref, o16_ref):
        out = _ln(h_ref[...], y_ref[...], w_ref[...], b_ref[...])
        o_ref[...] = out
        o16_ref[...] = out.astype(bf16)

    row = pl.BlockSpec((tr, d), lambda i: (i, 0))
    vec = pl.BlockSpec((1, d), lambda i: (0, 0))
    return pl.pallas_call(body, grid=(t // tr,), in_specs=[row, row, vec, vec], out_specs=[row, row],
                          out_shape=[jax.ShapeDtypeStruct((t, d), f32), jax.ShapeDtypeStruct((t, d), bf16)],
                          name="ln_fwd")(h, y, w, b)


def _ln_loss_bwd(h, y, w, b, tgt):
    t, d = h.shape
    tr = _row_tile(t)

    def body(h_ref, y_ref, w_ref, b_ref, t_ref, dy_ref, dy16_ref, dw_ref, db_ref, loss_ref):
        @pl.when(pl.program_id(0) == 0)
        def _():
            dw_ref[...] = jnp.zeros_like(dw_ref)
            db_ref[...] = jnp.zeros_like(db_ref)
            loss_ref[...] = jnp.zeros_like(loss_ref)

        out, vjp = jax.vjp(lambda yy, ww, bb: _ln(h_ref[...], yy, ww, bb), y_ref[...], w_ref[...], b_ref[...])
        err = out - t_ref[...]
        loss_ref[...] += 0.5 * jnp.sum(jnp.mean(jnp.square(err), axis=-1, keepdims=True), axis=0, keepdims=True)
        dy, dw, db = vjp(err * (1.0 / d))
        dy_ref[...] = dy
        dy16_ref[...] = dy.astype(bf16)
        dw_ref[...] += dw
        db_ref[...] += db

    row = pl.BlockSpec((tr, d), lambda i: (i, 0))
    vec = pl.BlockSpec((1, d), lambda i: (0, 0))
    return pl.pallas_call(
        body, grid=(t // tr,), in_specs=[row, row, vec, vec, row],
        out_specs=[row, row, vec, vec, pl.BlockSpec((1, LANES), lambda i: (0, 0))],
        out_shape=[jax.ShapeDtypeStruct((t, d), f32), jax.ShapeDtypeStruct((t, d), bf16), jax.ShapeDtypeStruct((1, d), f32),
                   jax.ShapeDtypeStruct((1, d), f32), jax.ShapeDtypeStruct((1, LANES), f32)],
        compiler_params=pltpu.CompilerParams(dimension_semantics=("arbitrary",)), name="ln_loss_bwd")(h, y, w, b, tgt)


def _ln_bwd(h, y, w, b, dout):
    t, d = h.shape
    tr = _row_tile(t)

    def body(h_ref, y_ref, w_ref, b_ref, do_ref, dy_ref, dy16_ref, dw_ref, db_ref):
        @pl.when(pl.program_id(0) == 0)
        def _():
            dw_ref[...] = jnp.zeros_like(dw_ref)
            db_ref[...] = jnp.zeros_like(db_ref)

        _, vjp = jax.vjp(lambda yy, ww, bb: _ln(h_ref[...], yy, ww, bb), y_ref[...], w_ref[...], b_ref[...])
        dy, dw, db = vjp(do_ref[...])
        dy_ref[...] = dy
        dy16_ref[...] = dy.astype(bf16)
        dw_ref[...] += dw
        db_ref[...] += db

    row = pl.BlockSpec((tr, d), lambda i: (i, 0))
    vec = pl.BlockSpec((1, d), lambda i: (0, 0))
    return pl.pallas_call(
        body, grid=(t // tr,), in_specs=[row, row, vec, vec, row], out_specs=[row, row, vec, vec],
        out_shape=[jax.ShapeDtypeStruct((t, d), f32), jax.ShapeDtypeStruct((t, d), bf16),
                   jax.ShapeDtypeStruct((1, d), f32), jax.ShapeDtypeStruct((1, d), f32)],
        compiler_params=pltpu.CompilerParams(dimension_semantics=("arbitrary",)), name="ln_bwd")(h, y, w, b, dout)


def _pick(n, prefs):
    for p in prefs:
        if n % p == 0:
            return p
    return n


def _tile(n, want):
    if n <= want:
        return n
    for cand in range(want - want % LANES, 0, -LANES):
        if n % cand == 0:
            return cand
    return n


def _matmul(a, b, mode, name, tiles, add=None, add_scale=1.0, out_dtype=f32):
    if mode == "nn":
        (m, k), n = a.shape, b.shape[1]
    elif mode == "nt":
        (m, k), n = a.shape, b.shape[0]
    else:
        (k, m), n = a.shape, b.shape[1]
    tm, tn, tk = _tile(m, tiles[0]), _tile(n, tiles[1]), _tile(k, tiles[2])
    nk = k // tk
    cache_a = nk == 1 and a.dtype != bf16 and n // tn > 1

    def body(*refs):
        a_ref, b_ref = refs[0], refs[1]
        add_ref = refs[2] if add is not None else None
        n_in = 3 if add is not None else 2
        o_ref = refs[n_in]
        scratch = refs[n_in + 1:]

        def finish(res):
            if add is not None:
                res = res + add_scale * add_ref[...]
            o_ref[...] = res.astype(out_dtype)

        if cache_a:
            a_bf = scratch[0]

            @pl.when(pl.program_id(1) == 0)
            def _():
                a_bf[...] = a_ref[...].astype(bf16)

            a_val = a_bf[...]
        else:
            a_val = a_ref[...].astype(bf16)
        prod = lax.dot_general(a_val, b_ref[...].astype(bf16), _DIMS[mode], preferred_element_type=f32)
        if nk == 1:
            finish(prod)
        else:
            acc = scratch[-1]
            kk = pl.program_id(2)

            @pl.when(kk == 0)
            def _():
                acc[...] = prod

            @pl.when(kk != 0)
            def _():
                acc[...] += prod

            @pl.when(kk == nk - 1)
            def _():
                finish(acc[...])

    a_shape = (tk, tm) if mode == "tn" else (tm, tk)
    a_spec = pl.BlockSpec(a_shape, (lambda i, j, kk: (kk, i)) if mode == "tn" else (lambda i, j, kk: (i, kk)))
    b_spec = pl.BlockSpec((tn, tk), lambda i, j, kk: (j, kk)) if mode == "nt" else pl.BlockSpec((tk, tn), lambda i, j, kk: (kk, j))
    o_spec = pl.BlockSpec((tm, tn), lambda i, j, kk: (i, j))
    in_specs = [a_spec, b_spec] + ([o_spec] if add is not None else [])
    args = [a, b] + ([add] if add is not None else [])
    scratch_shapes = ([pltpu.VMEM(a_shape, bf16)] if cache_a else []) + ([pltpu.VMEM((tm, tn), f32)] if nk > 1 else [])
    return pl.pallas_call(
        body, grid=(m // tm, n // tn, nk), in_specs=in_specs, out_specs=o_spec,
        out_shape=jax.ShapeDtypeStruct((m, n), out_dtype), scratch_shapes=scratch_shapes,
        compiler_params=pltpu.CompilerParams(dimension_semantics=("parallel", "arbitrary", "arbitrary")),
        name=name,
    )(*args)


def _position():
    return lax.axis_index("x"), lax.axis_index("y"), lax.axis_index("c")


def _flip(pos, k):
    x, y, c = pos
    return (1 - x if k & 4 else x, 1 - y if k & 2 else y, 1 - c if k & 1 else c)


def _index(pos):
    return 4 * pos[0] + 2 * pos[1] + pos[2]


def _all_gather_rows(x, name):
    m_per, n = x.shape

    def body(x_ref, out_ref, send_sems, recv_sems, local_sem):
        me = _position()
        sibling = _flip(me, 1)
        chips = (2, 4, 6)

        def rows(pos):
            return out_ref.at[pl.ds(_index(pos) * m_per, m_per), :]

        def copy(sem, block, to, src=None):
            return pltpu.make_async_remote_copy(
                src_ref=rows(block) if src is None else src, dst_ref=rows(block),
                send_sem=send_sems.at[sem], recv_sem=recv_sems.at[sem], device_id=to, device_id_type=MESH)

        mine = pltpu.make_async_copy(x_ref, rows(me), local_sem)
        mine.start()
        first = [copy(0, me, sibling, src=x_ref)]
        first += [copy(1 + j, me, _flip(me, k), src=x_ref) for j, k in enumerate(chips)]
        for cp in first:
            cp.start()
        passed = [copy(4 + j, _flip(me, k), sibling) for j, k in enumerate(chips)]
        for j, k in enumerate(chips):
            copy(1 + j, _flip(me, k), me).wait_recv()
            passed[j].start()
        copy(0, sibling, me).wait_recv()
        for j, k in enumerate(chips):
            copy(4 + j, _flip(sibling, k), me).wait_recv()
        for cp in first + passed:
            cp.wait_send()
        mine.wait()

    return pl.pallas_call(
        body, out_shape=jax.ShapeDtypeStruct((N_DEV * m_per, n), x.dtype),
        in_specs=[pl.BlockSpec(memory_space=pl.ANY)], out_specs=pl.BlockSpec(memory_space=pl.ANY),
        scratch_shapes=[pltpu.SemaphoreType.DMA((7,)), pltpu.SemaphoreType.DMA((7,)), pltpu.SemaphoreType.DMA(())],
        name=name,
    )(x)


def _split_start(srcs, lands, plan, n_copies, name):
    n_arr = len(srcs)
    hbm = pl.BlockSpec(memory_space=pltpu.HBM)
    sem = pl.BlockSpec(memory_space=pltpu.SEMAPHORE)

    def body(*refs):
        src_refs, land_refs = refs[:n_arr], refs[n_arr:2 * n_arr]
        send_sems, recv_sems = refs[2 * n_arr:3 * n_arr], refs[3 * n_arr:4 * n_arr]
        token = refs[-1]
        me = _position()
        for i in range(n_arr):
            for j, (src, dst, peer, _) in enumerate(plan(i, src_refs[i], land_refs[i], me)):
                pltpu.make_async_remote_copy(src_ref=src, dst_ref=dst, send_sem=send_sems[i].at[j], recv_sem=recv_sems[i].at[j],
                                             device_id=peer, device_id_type=MESH).start()
        token[...] = jnp.zeros_like(token)

    outs = pl.pallas_call(
        body, name=name,
        out_shape=([pltpu.SemaphoreType.DMA((n_copies,))] * (2 * n_arr)
                   + [pltpu.HBM(a.shape, a.dtype) for a in list(srcs) + list(lands)]
                   + [jax.ShapeDtypeStruct((8, LANES), f32)]),
        in_specs=[hbm] * (2 * n_arr),
        out_specs=[sem] * (2 * n_arr) + [hbm] * (2 * n_arr) + [pl.BlockSpec(memory_space=pltpu.VMEM)],
        input_output_aliases={i: 2 * n_arr + i for i in range(2 * n_arr)},
        compiler_params=pltpu.CompilerParams(has_side_effects=pltpu.SideEffectType.DATAFLOW_SIDE_EFFECTING),
    )(*[pltpu.with_memory_space_constraint(a, pltpu.HBM) for a in list(srcs) + list(lands)])
    return (outs[:n_arr], outs[n_arr:2 * n_arr], outs[2 * n_arr:3 * n_arr], outs[3 * n_arr:4 * n_arr], outs[-1])


def _split_wait(started, plan, after, name):
    send_sems, recv_sems, srcs, lands, _ = started
    n_arr = len(srcs)
    hbm = pl.BlockSpec(memory_space=pltpu.HBM)
    sem = pl.BlockSpec(memory_space=pltpu.SEMAPHORE)

    def body(*refs):
        src_refs, land_refs = refs[:n_arr], refs[n_arr:2 * n_arr]
        s_sems, r_sems = refs[2 * n_arr:3 * n_arr], refs[3 * n_arr:4 * n_arr]
        me = _position()
        for i in range(n_arr):
            for j, (src, _, peer, arrival) in enumerate(plan(i, src_refs[i], land_refs[i], me)):
                cp = pltpu.make_async_remote_copy(src_ref=src, dst_ref=arrival, send_sem=s_sems[i].at[j], recv_sem=r_sems[i].at[j],
                                                  device_id=peer, device_id_type=MESH)
                cp.wait_send()
                cp.wait_recv()

    outs = pl.pallas_call(
        body, name=name,
        out_shape=[pltpu.HBM(a.shape, a.dtype) for a in list(srcs) + list(lands)],
        in_specs=[hbm] * (2 * n_arr) + [sem] * (2 * n_arr) + [pl.BlockSpec(memory_space=pl.ANY)],
        out_specs=[hbm] * (2 * n_arr),
        input_output_aliases={i: i for i in range(2 * n_arr)},
        compiler_params=pltpu.CompilerParams(has_side_effects=pltpu.SideEffectType.DATAFLOW_SIDE_EFFECTING),
    )(*srcs, *lands, *send_sems, *recv_sems, after)
    return outs[:n_arr], outs[n_arr:]


_GATHER_FLIPS = (1, 2, 4, 6)


def _gather_plan(i, src_ref, land_ref, me):
    m = src_ref.shape[0]

    def rows(pos):
        return land_ref.at[pl.ds(_index(pos) * m, m), :]

    return [(src_ref, rows(me), _flip(me, k), rows(_flip(me, k))) for k in _GATHER_FLIPS]


def _gather_forward(lands, name):
    n_arr = len(lands)
    chips = (2, 4, 6)

    def body(*refs):
        out_refs = refs[n_arr:2 * n_arr]
        send_sems, recv_sems = refs[2 * n_arr:]
        me = _position()
        sibling = _flip(me, 1)
        sends, arrivals = [], []
        for i, out_ref in enumerate(out_refs):
            m = out_ref.shape[0] // N_DEV

            def copy(pos, j):
                blk = out_ref.at[pl.ds(_index(pos) * m, m), :]
                return pltpu.make_async_remote_copy(src_ref=blk, dst_ref=blk, send_sem=send_sems.at[3 * i + j],
                                                    recv_sem=recv_sems.at[3 * i + j], device_id=sibling, device_id_type=MESH)

            for j, k in enumerate(chips):
                sends.append(copy(_flip(me, k), j))
                arrivals.append(copy(_flip(sibling, k), j))
        for cp in sends:
            cp.start()
        for cp in arrivals:
            cp.wait_recv()
        for cp in sends:
            cp.wait_send()

    anyspec = pl.BlockSpec(memory_space=pl.ANY)
    return pl.pallas_call(
        body, out_shape=[jax.ShapeDtypeStruct(a.shape, a.dtype) for a in lands],
        in_specs=[anyspec] * n_arr, out_specs=[anyspec] * n_arr, input_output_aliases={i: i for i in range(n_arr)},
        scratch_shapes=[pltpu.SemaphoreType.DMA((3 * n_arr,))] * 2, name=name,
    )(*lands)


def _chips_plan(i, src_ref, land_ref, me):
    m = src_ref.shape[0] // 4
    plan = []
    for j, k in enumerate((2, 4, 6)):
        peer = _flip(me, k)
        plan.append((src_ref.at[pl.ds((2 * peer[0] + peer[1]) * m, m), :], land_ref.at[j], peer, land_ref.at[j]))
    return plan


def _exchange_siblings(gs, name):
    n_arr = len(gs)

    def body(*refs):
        g_refs, out_refs = refs[:n_arr], refs[n_arr:2 * n_arr]
        send_sems, recv_sems = refs[2 * n_arr:]
        me = _position()
        c = me[2]
        sibling = _flip(me, 1)
        copies = []
        for i, (g_ref, out_ref) in enumerate(zip(g_refs, out_refs)):
            m_per = g_ref.shape[0] // N_DEV
            for q in range(4):
                copies.append(pltpu.make_async_remote_copy(
                    src_ref=g_ref.at[pl.ds((2 * q + 1 - c) * m_per, m_per), :], dst_ref=out_ref.at[q],
                    send_sem=send_sems.at[4 * i + q], recv_sem=recv_sems.at[4 * i + q],
                    device_id=sibling, device_id_type=MESH))
        for cp in copies:
            cp.start()
        for cp in copies:
            cp.wait_recv()
        for cp in copies:
            cp.wait_send()

    anyspec = pl.BlockSpec(memory_space=pl.ANY)
    return pl.pallas_call(
        body, out_shape=[jax.ShapeDtypeStruct((4, g.shape[0] // N_DEV, g.shape[1]), g.dtype) for g in gs],
        in_specs=[anyspec] * n_arr, out_specs=[anyspec] * n_arr,
        scratch_shapes=[pltpu.SemaphoreType.DMA((4 * n_arr,))] * 2, name=name,
    )(*gs)


def _sum_with_sibling(g, recv, name):
    m = g.shape[0] // N_DEV
    n = g.shape[1]
    tr = _pick(m, (208, 128, 64, 32, 16))
    nt = m // tr

    def body(g_ref, r_ref, o_ref):
        c = lax.axis_index("c")
        own = jnp.where(c == 0, g_ref[0, 0].astype(f32), g_ref[0, 1].astype(f32))
        o_ref[...] = (own + r_ref[0].astype(f32)).astype(o_ref.dtype)

    return pl.pallas_call(
        body, grid=(4, nt),
        in_specs=[pl.BlockSpec((1, 2, tr, n), lambda q, i: (q, 0, i, 0)), pl.BlockSpec((1, tr, n), lambda q, i: (q, i, 0))],
        out_specs=pl.BlockSpec((tr, n), lambda q, i: (q * nt + i, 0)),
        out_shape=jax.ShapeDtypeStruct((4 * m, n), bf16), name=name,
    )(g.reshape(4, 2, m, n), recv)


def _sum_with_chips(h, recv, name):
    m = h.shape[0] // 4
    n = h.shape[1]
    tr = _pick(m, (208, 128, 64, 32, 16))

    def body(h_ref, r_ref, o_ref):
        my_q = 2 * lax.axis_index("x") + lax.axis_index("y")
        own = h_ref[0].astype(f32)
        for q in range(1, 4):
            own = jnp.where(my_q == q, h_ref[q].astype(f32), own)
        o_ref[...] = ((own + r_ref[0].astype(f32)) + r_ref[1].astype(f32)) + r_ref[2].astype(f32)

    return pl.pallas_call(
        body, grid=(m // tr,),
        in_specs=[pl.BlockSpec((4, tr, n), lambda i: (0, i, 0)), pl.BlockSpec((3, tr, n), lambda i: (0, i, 0))],
        out_specs=pl.BlockSpec((tr, n), lambda i: (i, 0)), out_shape=jax.ShapeDtypeStruct((m, n), f32), name=name,
    )(h.reshape(4, m, n), recv)


def _sum_slots(parts, name):
    n_slot, m, n = parts.shape
    tr = _pick(m, (208, 128, 64, 32, 16, 8))

    def body(p_ref, o_ref):
        acc = p_ref[0]
        for s in range(1, n_slot):
            acc = acc + p_ref[s]
        o_ref[...] = acc

    return pl.pallas_call(
        body, grid=(m // tr,), in_specs=[pl.BlockSpec((n_slot, tr, n), lambda i: (0, i, 0))],
        out_specs=pl.BlockSpec((tr, n), lambda i: (i, 0)), out_shape=jax.ShapeDtypeStruct((m, n), parts.dtype), name=name,
    )(parts)


def _reduce_scatter_begin(gs, name):
    from_sibling = _exchange_siblings(gs, "rs_d2d_" + name)
    chip_sums = [_sum_with_sibling(g, r, f"rs_sum2_{name}_{i}") for i, (g, r) in enumerate(zip(gs, from_sibling))]
    lands = [lax.empty((3, h.shape[0] // 4, h.shape[1]), h.dtype) for h in chip_sums]
    return _split_start(chip_sums, lands, _chips_plan, 3, "rs_ici_start_" + name)


def _reduce_scatter_end(started, after, name):
    chip_sums, from_chips = _split_wait(started, _chips_plan, after, "rs_ici_wait_" + name)
    return [_sum_with_chips(h, r, f"rs_sum4_{name}_{i}") for i, (h, r) in enumerate(zip(chip_sums, from_chips))]


def _adamw_update(w, g, m, v):
    mm = ADAM_B1 * m + (1.0 - ADAM_B1) * g
    vv = ADAM_B2 * v + (1.0 - ADAM_B2) * jnp.square(g)
    m_hat = mm / (1.0 - ADAM_B1 ** ADAM_STEP)
    v_hat = vv / (1.0 - ADAM_B2 ** ADAM_STEP)
    return -ADAM_LR * (m_hat / (jnp.sqrt(v_hat) + ADAM_EPS) + ADAM_WD * w), mm, vv


def _adamw_many(ws, gs, ms, vs, name):
    k = len(ws)
    shapes = [w.shape for w in ws]
    flat = [[a.reshape(-1, a.shape[-1]) for a in group] for group in (ws, gs, ms, vs)]

    def body(*refs):
        for i in range(k):
            d, mm, vv = _adamw_update(*(refs[j * k + i][...] for j in range(4)))
            refs[4 * k + i][...] = d
            refs[5 * k + i][...] = mm
            refs[6 * k + i][...] = vv

    outs = pl.pallas_call(
        body, out_shape=[jax.ShapeDtypeStruct(a.shape, f32) for a in flat[0]] * 3, name=name,
    )(*flat[0], *flat[1], *flat[2], *flat[3])
    return tuple([outs[j * k + i].reshape(shapes[i]) for i in range(k)] for j in range(3))


def _adamw(w, g, m, v, name):
    shape = w.shape
    n = shape[-1]
    r = w.size // n
    w2, g2, m2, v2 = (a.reshape(r, n) for a in (w, g, m, v))
    tr = _pick(r, (256, 128, 64, 32, 16, 8))

    def body(w_ref, g_ref, m_ref, v_ref, d_ref, mo_ref, vo_ref):
        d_ref[...], mo_ref[...], vo_ref[...] = _adamw_update(w_ref[...], g_ref[...], m_ref[...], v_ref[...])

    spec = pl.BlockSpec((tr, n), lambda i: (i, 0))
    outs = pl.pallas_call(
        body, grid=(r // tr,), in_specs=[spec] * 4, out_specs=[spec] * 3,
        out_shape=[jax.ShapeDtypeStruct((r, n), f32)] * 3, name=name,
    )(w2, g2, m2, v2)
    return tuple(o.reshape(shape) for o in outs)


_SMALL = ("shift_mu", "w_decay0", "a0", "k_k", "k_a", "r_k", "ln_x_w", "ln_x_b", "v_mix0", "lb_logits",
          "g_norm_w", "ln_w", "ln_b")
_NAMES = ("w_in", "shift_mu", "w_decay0", "w_decay_up", "a0", "a_up", "k_k", "k_a", "r_k", "ln_x_w", "ln_x_b",
          "v_mix0", "v_mix_down", "v_mix_up", "lb_logits", "g_norm_w", "w_out", "ln_w", "ln_b")


def _pad_rows(a, rows, at_end):
    z = jnp.zeros((rows - a.shape[0], a.shape[1]), a.dtype)
    return jnp.concatenate([a, z] if at_end else [z, a], axis=0)


def kernel(x, w_in, shift_mu, w_decay0, w_decay_up, a0, a_up, k_k, k_a, r_k, ln_x_w, ln_x_b, v_mix0, v_mix_down, v_mix_up, lb_logits, g_norm_w, w_out, ln_w, ln_b, loss_target, m_w_in, m_shift_mu, m_w_decay0, m_w_decay_up, m_a0, m_a_up, m_k_k, m_k_a, m_r_k, m_ln_x_w, m_ln_x_b, m_v_mix0, m_v_mix_down, m_v_mix_up, m_lb_logits, m_g_norm_w, m_w_out, m_ln_w, m_ln_b, v_w_in, v_shift_mu, v_w_decay0, v_w_decay_up, v_a0, v_a_up, v_k_k, v_k_a, v_r_k, v_ln_x_w, v_ln_x_b, v_v_mix0, v_v_mix_down, v_v_mix_up, v_lb_logits, v_g_norm_w, v_w_out, v_ln_w, v_ln_b):
    weights = dict(w_in=w_in, shift_mu=shift_mu, w_decay0=w_decay0, w_decay_up=w_decay_up, a0=a0, a_up=a_up, k_k=k_k,
                   k_a=k_a, r_k=r_k, ln_x_w=ln_x_w, ln_x_b=ln_x_b, v_mix0=v_mix0, v_mix_down=v_mix_down,
                   v_mix_up=v_mix_up, lb_logits=lb_logits, g_norm_w=g_norm_w, w_out=w_out, ln_w=ln_w, ln_b=ln_b)
    mom1 = dict(w_in=m_w_in, shift_mu=m_shift_mu, w_decay0=m_w_decay0, w_decay_up=m_w_decay_up, a0=m_a0, a_up=m_a_up,
                k_k=m_k_k, k_a=m_k_a, r_k=m_r_k, ln_x_w=m_ln_x_w, ln_x_b=m_ln_x_b, v_mix0=m_v_mix0,
                v_mix_down=m_v_mix_down, v_mix_up=m_v_mix_up, lb_logits=m_lb_logits, g_norm_w=m_g_norm_w,
                w_out=m_w_out, ln_w=m_ln_w, ln_b=m_ln_b)
    mom2 = dict(w_in=v_w_in, shift_mu=v_shift_mu, w_decay0=v_w_decay0, w_decay_up=v_w_decay_up, a0=v_a0, a_up=v_a_up,
                k_k=v_k_k, k_a=v_k_a, r_k=v_r_k, ln_x_w=v_ln_x_w, ln_x_b=v_ln_x_b, v_mix0=v_v_mix0,
                v_mix_down=v_v_mix_down, v_mix_up=v_v_mix_up, lb_logits=v_lb_logits, g_norm_w=v_g_norm_w,
                w_out=v_w_out, ln_w=v_ln_w, ln_b=v_ln_b)
    assert x.shape[0] == 1 and w_in.shape[0] == DEPTH
    t, d = x.shape[1], x.shape[2]
    dr = w_decay0.shape[1]
    dh = g_norm_w.shape[1]
    rank_w, rank_a, rank_v = w_decay_up.shape[1], a_up.shape[1], v_mix_up.shape[1]
    rwc = 4 * dr + rank_w + rank_a
    assert rank_w + rank_a == LANES and rank_v <= LANES and dr + dh == d
    assert t % CHUNK == 0 and dr % LANES == 0 and dh % LANES == 0 and shift_mu.shape[1] == rwc
    n_pair = dr // LANES
    me = _index(_position())

    win_t = [_all_gather_rows(w_in[0].T.astype(bf16), "ag_w_in_0"), None]
    wout = [None, None]
    late_blocks = [w_out[0].astype(bf16), w_in[1].T.astype(bf16), w_out[1].astype(bf16)]
    late_lands = [lax.dynamic_update_slice(lax.empty((N_DEV * blk.shape[0], blk.shape[1]), bf16), blk, (me * blk.shape[0], 0))
                  for blk in late_blocks]
    late_gather = _split_start(late_blocks, late_lands, _gather_plan, len(_GATHER_FLIPS), "ag_late_start")
    shard = dr // N_DEV
    pack = jnp.concatenate([w_decay_up[0], w_decay_up[1], a_up[0], a_up[1], v_mix_up[0], v_mix_down[0].T], axis=0)
    pack = _all_gather_rows(pack, "ag_small")
    pack = jnp.transpose(pack.reshape(N_DEV, -1, shard), (1, 0, 2)).reshape(-1, dr)
    offs = [0, rank_w, 2 * rank_w, 2 * rank_w + rank_a, 2 * rank_w + 2 * rank_a, 2 * rank_w + 2 * rank_a + rank_v,
            2 * rank_w + 2 * rank_a + 2 * rank_v]
    wdu_f = [pack[offs[0]:offs[1]], pack[offs[1]:offs[2]]]
    aup_f = [pack[offs[2]:offs[3]], pack[offs[3]:offs[4]]]
    vup_f = pack[offs[4]:offs[5]]
    vdown_f = pack[offs[5]:offs[6]].T

    def after_start(a, started):
        return a + started[-1][0:1, 0:1]

    def rwkv_params(l):
        mu = after_start(shift_mu[0:1], late_gather) if l == 0 else shift_mu[l:l + 1]
        prm = [mu, w_decay0[l:l + 1], a0[l:l + 1], _pad_rows(wdu_f[l], LANES, True),
               _pad_rows(aup_f[l], LANES, False)]
        if l == 1:
            prm += [v_mix0[0:1], _pad_rows(vdown_f.T, LANES, True).T, _pad_rows(vup_f, LANES, True)]
        rows = jnp.stack([k_k[l], k_a[l], r_k[l], ln_x_w[l], ln_x_b[l]] + [jnp.zeros((dr,), f32)] * 3, axis=0)
        pp = jnp.transpose(rows.reshape(8, n_pair, LANES), (1, 0, 2))
        return tuple(prm), pp

    h = x[0]
    h16 = h.astype(bf16)
    tgt = loss_target[0]
    saved = []
    vfirst = None
    for l in range(DEPTH):
        prm, pp = rwkv_params(l)
        proj = _matmul(h16, win_t[l], "nt", f"mm_proj_{l}", (2048, 640, 2048))
        if l == 0:
            cat, vfirst, mck = _rwkv_fwd(False, proj, None, prm, pp, d)
        else:
            cat, mck = _rwkv_fwd(True, proj, vfirst, prm, pp, d)
        cat, sck = _hgrn_fwd(l == 1, proj, lb_logits, g_norm_w[l:l + 1], cat, rwc)
        if l == 0:
            _, arrived = _split_wait(late_gather, _gather_plan, cat, "ag_late_wait")
            wout[0], win_t[1], wout[1] = _gather_forward(arrived, "ag_late_forward")
        y = _matmul(cat, wout[l], "nn", f"mm_out_{l}", (1024, 1024, 2048))
        saved.append((h, h16, proj, prm, pp, mck, sck, cat, y))
        if l < DEPTH - 1:
            h, h16 = _ln_fwd(h, y, ln_w[l:l + 1], ln_b[l:l + 1])
        else:
            top = _ln_loss_bwd(h, y, ln_w[l:l + 1], ln_b[l:l + 1], tgt)
    loss = lax.psum(top[4][0, 0], ("x", "y", "c"))

    grads = {}
    big = {}
    dvfirst = None
    d_lbl = None
    rs_started = {}
    for l in reversed(range(DEPTH)):
        h_l, h16_l, proj, prm, pp, mck, sck, cat, y = saved[l]
        if l == DEPTH - 1:
            dy, dy16, g_ln_w, g_ln_b = top[:4]
        else:
            dy, dy16, g_ln_w, g_ln_b = _ln_bwd(h_l, y, after_start(ln_w[l:l + 1], rs_started[l + 1]), ln_b[l:l + 1], dh_out)
        dcat = _matmul(dy16, wout[l], "nt", f"mm_dcat_{l}", (1024, 1024, 2048))
        big[("w_out", l)] = _matmul(cat, dy16, "tn", f"mm_dwout_{l}", (512, 2048, 2048), out_dtype=bf16)
        if l == 1:
            outs = _rwkv_bwd(True, proj, vfirst, prm, pp, mck, dcat, None)
            dproj_r, dvfirst = outs[0], outs[1]
            dprm, dpp = outs[2:-1], outs[-1]
        else:
            outs = _rwkv_bwd(False, proj, None, prm, pp, mck, dcat, dvfirst)
            dproj_r = outs[0]
            dprm, dpp = outs[1:-1], outs[-1]
        dproj, dlbl_l, dgnw = _hgrn_bwd(l == 1, proj, lb_logits, g_norm_w[l:l + 1], sck, dcat, rwc, dproj_r)
        big[("w_in", l)] = _matmul(dproj, h16_l, "tn", f"mm_dwin_{l}", (640, 2048, 2048), out_dtype=bf16)
        sharded = [dprm[3][:rank_w].T, dprm[4][rank_w:].T]
        if l == 1:
            sharded += [dprm[6][:, :rank_v], dprm[7][:rank_v].T,
                        jnp.zeros((dr, LANES - 2 * rank_v), f32)]
        sharded = jnp.concatenate(sharded, axis=1).astype(bf16)
        rs_started[l] = _reduce_scatter_begin([big[("w_in", l)], big[("w_out", l)], sharded], f"l{l}")
        dy_res = after_start(dy, rs_started[l]) if l == 0 else dy
        dh_out = _matmul(dproj, win_t[l], "nn", f"mm_dh_{l}", (1024, 1024, 1664), add=dy_res, add_scale=ALPHA)
        dpp = jnp.transpose(dpp, (1, 0, 2)).reshape(8, dr)
        grads[l] = dict(shift_mu=dprm[0][0], w_decay0=dprm[1][0], a0=dprm[2][0],
                        k_k=dpp[0], k_a=dpp[1], r_k=dpp[2], ln_x_w=dpp[3], ln_x_b=dpp[4],
                        g_norm_w=dgnw[0], ln_w=g_ln_w[0], ln_b=g_ln_b[0])
        if l == 1:
            grads[l].update(v_mix0=dprm[5][0])
            d_lbl = dlbl_l
    grad_x = dh_out[None]

    def both(name):
        return jnp.stack([grads[0][name], grads[1][name]])

    small = dict(shift_mu=both("shift_mu"), w_decay0=both("w_decay0"), a0=both("a0"), k_k=both("k_k"), k_a=both("k_a"),
                 r_k=both("r_k"), ln_x_w=both("ln_x_w"), ln_x_b=both("ln_x_b"), v_mix0=grads[1]["v_mix0"][None],
                 lb_logits=d_lbl, g_norm_w=both("g_norm_w"), ln_w=both("ln_w"), ln_b=both("ln_b"))
    flat = jnp.concatenate([small[nm].reshape(-1) for nm in _SMALL])
    n_flat = flat.shape[0]
    rows = -(-n_flat // (8 * LANES)) * 8
    flat = jnp.concatenate([flat, jnp.zeros((rows * LANES - n_flat,), f32)]).reshape(rows, LANES)
    total = _sum_slots(_all_gather_rows(flat, "ag_small_grads").reshape(N_DEV, rows, LANES), "sum_small_grads").reshape(-1)
    gsm = {}
    off = 0
    for nm in _SMALL:
        size = small[nm].size
        gsm[nm] = total[off:off + size].reshape(small[nm].shape)
        off += size
    reduced = {1: _reduce_scatter_end(rs_started[1], dh_out, "l1")}
    reduced[0] = _reduce_scatter_end(rs_started[0], total, "l0")
    gsm["w_in"] = jnp.stack([reduced[l][0].T for l in range(DEPTH)])
    gsm["w_out"] = jnp.stack([reduced[l][1] for l in range(DEPTH)])
    gsm["w_decay_up"] = jnp.stack([reduced[l][2][:, :rank_w].T for l in range(DEPTH)])
    gsm["a_up"] = jnp.stack([reduced[l][2][:, rank_w:rank_w + rank_a].T for l in range(DEPTH)])
    gsm["v_mix_down"] = reduced[1][2][:, LANES:LANES + rank_v][None]
    gsm["v_mix_up"] = reduced[1][2][:, LANES + rank_v:LANES + 2 * rank_v].T[None]

    deltas, new_m, new_v = {}, {}, {}
    for nm in ("w_in", "w_out"):
        deltas[nm], new_m[nm], new_v[nm] = _adamw(weights[nm], gsm[nm], mom1[nm], mom2[nm], "adamw_" + nm)
    rest = [nm for nm in _NAMES if nm not in ("w_in", "w_out")]
    d_rest, m_rest, v_rest = _adamw_many([weights[nm] for nm in rest], [gsm[nm] for nm in rest],
                                         [mom1[nm] for nm in rest], [mom2[nm] for nm in rest], "adamw_small")
    for i, nm in enumerate(rest):
        deltas[nm], new_m[nm], new_v[nm] = d_rest[i], m_rest[i], v_rest[i]
    return (loss, grad_x, *[gsm[nm] for nm in _NAMES], *[deltas[nm] for nm in _NAMES],
            *[new_m[nm] for nm in _NAMES], *[new_v[nm] for nm in _NAMES])
```

```python
import functools

import jax
import jax.numpy as jnp
from jax import lax
from jax.experimental import pallas as pl
from jax.experimental.pallas import tpu as pltpu

f32 = jnp.float32
bf16 = jnp.bfloat16

N_DEV = 8
CHUNK = 64
LANES = 128
RWKV_HEAD = 64
DEPTH = 2
ALPHA = (2 * DEPTH) ** 0.25
LN_EPS = 1e-5
GN_EPS = 64e-5
RMS_EPS = 1e-5
LB_FLOOR = 1e-30
ADAM_LR, ADAM_B1, ADAM_B2, ADAM_EPS, ADAM_WD, ADAM_STEP = 0.001, 0.9, 0.999, 1e-08, 0.01, 10
MESH = pl.DeviceIdType.MESH


def _iota(shape, d):
    return lax.broadcasted_iota(jnp.int32, shape, d)


_DIMS = {"nn": (((1,), (0,)), ((), ())), "nt": (((1,), (1,)), ((), ())), "tn": (((0,), (0,)), ((), ()))}
_BATCH_DIMS = {"nn": (((2,), (1,)), ((0,), (0,))), "nt": (((2,), (2,)), ((0,), (0,))), "tn": (((1,), (1,)), ((0,), (0,)))}
_K_AXES = {"nn": (-1, -2), "nt": (-1, -1), "tn": (-2, -2)}


def _mxu(a, b, mode):
    return lax.dot_general(a, b, (_BATCH_DIMS if a.ndim == 3 else _DIMS)[mode], preferred_element_type=f32)


def _split(x):
    hi = x.astype(bf16)
    return hi, (x - hi.astype(f32)).astype(bf16)


def _mm2_impl(a, b, mode, passes=3):
    ah, al = _split(a)
    if passes == 3:
        bh, bl = _split(b)
        lhs, rhs = [ah, ah, al], [bh, bl, bh]
    else:
        bh = b.astype(bf16)
        lhs, rhs = [ah, al], [bh, bh]
    ka, kb = _K_AXES[mode]
    k = a.shape[ka]
    if k % (LANES if -1 in (ka, kb) else 16) == 0:
        return _mxu(jnp.concatenate(lhs, axis=ka), jnp.concatenate(rhs, axis=kb), mode)
    out = _mxu(lhs[0], rhs[0], mode)
    for x, y in zip(lhs[1:], rhs[1:]):
        out = out + _mxu(x, y, mode)
    return out


@functools.partial(jax.custom_vjp, nondiff_argnums=(2, 3))
def _mm2(a, b, mode, passes=3):
    return _mm2_impl(a, b, mode, passes)


def _mm2_fwd(a, b, mode, passes):
    return _mm2_impl(a, b, mode, passes), (a, b)


def _mm2_bwd(mode, passes, res, g):
    a, b = res
    if mode == "nn":
        return _mm2_impl(g, b, "nt", passes), _mm2_impl(a, g, "tn", passes)
    if mode == "nt":
        return _mm2_impl(g, b, "nn", passes), _mm2_impl(g, a, "tn", passes)
    return _mm2_impl(b, g, "nt", passes), _mm2_impl(a, g, "nn", passes)


_mm2.defvjp(_mm2_fwd, _mm2_bwd)

TRI_PASSES = 2
APPLY_PASSES = 2


def _const_impl(cm, x, mode):
    hi, lo = _split(x)
    if mode in ("r", "rt"):
        shape = x.shape
        hi, lo = hi.reshape(-1, shape[-1]), lo.reshape(-1, shape[-1])
        dims = "nn" if mode == "r" else "nt"
        out = _mxu(hi, cm, dims) + _mxu(lo, cm, dims)
        return out.reshape(shape[:-1] + (out.shape[-1],))
    if x.ndim == 3:
        cm = jnp.broadcast_to(cm, (x.shape[0],) + cm.shape)
    return _mxu(cm, hi, mode) + _mxu(cm, lo, mode)


@jax.custom_vjp
def _const_left(cm, x):
    return _const_impl(cm, x, "nn")


_const_left.defvjp(lambda cm, x: (_const_impl(cm, x, "nn"), cm),
                   lambda cm, g: (jnp.zeros_like(cm), _const_impl(cm, g, "tn")))


@jax.custom_vjp
def _const_right(x, cm):
    return _const_impl(cm, x, "r")


_const_right.defvjp(lambda x, cm: (_const_impl(cm, x, "r"), cm),
                    lambda cm, g: (_const_impl(cm, g, "rt"), jnp.zeros_like(cm)))


def _tri_inv(a):
    n = a.shape[-1]
    tm = (_iota((n, n), 0) == _iota((n, n), 1)).astype(f32) + a
    ak = a
    for _ in range(5):
        ak = _mm2_impl(ak, ak, "nn", TRI_PASSES)
        tm = tm + _mm2_impl(tm, ak, "nn", TRI_PASSES)
    return tm


@jax.custom_vjp
def _tri_solve(tm, a, x):
    del a
    return _mm2_impl(tm, x, "nn")


def _tri_solve_fwd(tm, a, x):
    u = _mm2_impl(tm, x, "nn")
    return u, (tm, u)


def _tri_solve_bwd(res, du):
    tm, u = res
    dx = _mm2_impl(tm, du, "tn")
    return jnp.zeros_like(tm), _mm2_impl(dx, u, "nt"), dx


_tri_solve.defvjp(_tri_solve_fwd, _tri_solve_bwd)


def _col_of_row(row_vec):
    n = row_vec.shape[-1]
    eye = _iota((n, n), 0) == _iota((n, n), 1)
    return jnp.sum(jnp.where(eye, jnp.broadcast_to(row_vec, row_vec.shape[:-2] + (n, n)), 0.0), axis=-1, keepdims=True)


def _softplus(x):
    return jnp.maximum(x, 0.0) + jnp.log1p(jnp.exp(-jnp.abs(x)))


def _log_sigmoid(x):
    return -_softplus(-x)


def _logaddexp(a, b):
    return jnp.maximum(a, b) + jnp.log1p(jnp.exp(-jnp.abs(a - b)))


def _silu(x):
    return x * jax.nn.sigmoid(x)


def _tril(c, strict):
    r, s = _iota((c, c), 0), _iota((c, c), 1)
    return (r > s) if strict else (r >= s)


def _last_row(a):
    c = a.shape[-2]
    return jnp.sum(jnp.where(_iota(a.shape, a.ndim - 2) == c - 1, a, 0.0), axis=-2, keepdims=True)


def _rwkv_pre(layer1, prm, y, prev, vf):
    c = y.shape[0]
    if layer1:
        mu, w0, a0, wup, aup, v0, vdown, vup = prm
    else:
        mu, w0, a0, wup, aup = prm
    dr = w0.shape[1]
    shift = (_iota((c, c), 0) == _iota((c, c), 1) + 1).astype(bf16)
    y_prev = _const_left(shift, y) + jnp.where(_iota((c, 1), 0) == 0, prev, 0.0)
    rw = y + mu * (y_prev - y)
    r, k, v, z = (rw[:, i * dr:(i + 1) * dr] for i in range(4))
    wdad = rw[:, 4 * dr:4 * dr + LANES]
    w_raw = w0 + _mm2(jnp.tanh(wdad), wup, "nn")
    lw = -jnp.exp(-_softplus(-w_raw) - 0.5)
    asig = jax.nn.sigmoid(a0 + _mm2(wdad, aup, "nn"))
    if layer1:
        v = v + (vf - v) * jax.nn.sigmoid(v0 + _mm2(_mm2(v, vdown, "nn"), vup, "nn"))
    return r, k, v, z, lw, asig


def _rwkv_pair(pp, m0, xs, tm=None):
    kkw, kaw, rkw, gnw, gnb = pp
    r, k, v, z, lw, asig = xs
    c = r.shape[-2]
    n2 = 2 * c
    lane = _iota((1, LANES), 1)
    mh0, mh1 = (lane < RWKV_HEAD).astype(f32), (lane >= RWKV_HEAD).astype(f32)
    same_head = _iota((LANES, LANES), 0) // RWKV_HEAD == _iota((LANES, LANES), 1) // RWKV_HEAD
    g = same_head.astype(bf16)

    def seg(x):
        return _const_right(x, g)

    def stack(x):
        return jnp.concatenate([x * mh0, x * mh1], axis=-2)

    kk = k * kkw
    kk = kk / jnp.maximum(jnp.sqrt(seg(kk * kk)), 1e-12)
    k2 = k * (1.0 + (asig - 1.0) * kaw)
    a = -kk
    b = kk * asig
    cum = _const_left(_tril(c, False).astype(bf16), lw)
    at = stack(a * jnp.exp(cum - lw))
    rt = stack(r * jnp.exp(cum))
    en = jnp.exp(-cum)
    sc = _mm2(jnp.concatenate([at, rt], axis=-2), jnp.concatenate([stack(b * en), stack(k2 * en)], axis=-2), "nt")
    row, col = _iota((n2, n2), 0), _iota((n2, n2), 1)
    same = row // c == col // c
    strict = same & (row % c > col % c)
    incl = same & (row % c >= col % c)
    aab = jnp.where(strict, sc[..., :n2, :n2], 0.0)
    aak = jnp.where(strict, sc[..., :n2, n2:], 0.0)
    arb = jnp.where(incl, sc[..., n2:, :n2], 0.0)
    ark = jnp.where(incl, sc[..., n2:, n2:], 0.0)
    vv = jnp.concatenate([v, v], axis=-2)
    mask_st = jnp.concatenate([jnp.broadcast_to(mh0, (c, LANES)), jnp.broadcast_to(mh1, (c, LANES))], axis=0)
    x_st = _mm2(jnp.concatenate([at, aak], axis=-1), jnp.concatenate([m0, vv], axis=-2), "nn", APPLY_PASSES)
    if tm is None:
        tm = _tri_inv(lax.stop_gradient(aab))
    u_st = _tri_solve(tm, aab, x_st) * mask_st
    o_st = _mm2(jnp.concatenate([rt, arb, ark], axis=-1), jnp.concatenate([m0, u_st, vv], axis=-2), "nn", APPLY_PASSES) * mask_st
    u = u_st[..., :c, :] + u_st[..., c:, :]
    o = o_st[..., :c, :] + o_st[..., c:, :]
    cum_last = _last_row(cum)
    dec_end = jnp.exp(cum_last - cum)
    m_new = _col_of_row(jnp.exp(cum_last)) * m0 + _mm2(
        jnp.concatenate([b * dec_end, k2 * dec_end], axis=-2), jnp.concatenate([u, v], axis=-2), "tn", APPLY_PASSES) * same_head.astype(f32)
    mean = seg(o) * (1.0 / RWKV_HEAD)
    d = o - mean
    var = seg(d * d) * (1.0 / RWKV_HEAD)
    on = d * lax.rsqrt(var + GN_EPS) * gnw + gnb
    bonus = seg(r * k2 * rkw) * v
    return (on + bonus) * _silu(z), m_new, tm


def _split_lanes(a, n):
    return [a[:, i * LANES:(i + 1) * LANES] for i in range(n)]


def _rwkv_step(layer1, prm, y, prev, vf, pp, m0, tm=None):
    xs = _rwkv_pre(layer1, prm, y, prev, vf)
    n_pair = m0.shape[0]
    og, m_new, tm = _rwkv_pair(pp, m0, tuple(jnp.concatenate([p[None] for p in _split_lanes(a, n_pair)], axis=0) for a in xs), tm)
    return og, m_new, xs[2], tm


def _group(n):
    return n


def _rwkv_specs(layer1, t, dr, rwc, n_pair, rev):
    nc = t // CHUNK
    grp = _group(n_pair)

    def cidx(c):
        return (nc - 1 - c) if rev else c

    full = lambda shape: pl.BlockSpec(shape, lambda c, p: tuple(0 for _ in shape))
    specs = [
        pl.BlockSpec((CHUNK, rwc), lambda c, p: (cidx(c), 0)),
        pl.BlockSpec((8, rwc), lambda c, p: (jnp.maximum(cidx(c) * (CHUNK // 8) - 1, 0), 0)),
    ]
    if layer1:
        specs.append(pl.BlockSpec((CHUNK, dr), lambda c, p: (cidx(c), 0)))
    prm_shapes = [(1, rwc), (1, dr), (1, dr), (LANES, dr), (LANES, dr)]
    if layer1:
        prm_shapes += [(1, dr), (dr, LANES), (LANES, dr)]
    specs += [full(s) for s in prm_shapes]
    specs.append(pl.BlockSpec((grp, 8, LANES), lambda c, p: (p, 0, 0)))
    return specs, prm_shapes, cidx, full


def _rwkv_fwd(layer1, proj, vf, prm, pp, cat_width):
    t = proj.shape[0]
    dr = prm[1].shape[1]
    rwc = prm[0].shape[1]
    n_pair = dr // LANES
    nc = t // CHUNK
    n_prm = len(prm)
    specs, _, _, _ = _rwkv_specs(layer1, t, dr, rwc, n_pair, False)

    def body(*refs):
        y_ref, prev_ref = refs[0], refs[1]
        i = 2
        vf_ref = None
        if layer1:
            vf_ref = refs[i]
            i += 1
        prm_refs = refs[i:i + n_prm]
        i += n_prm
        pp_ref = refs[i]
        i += 1
        cat_ref = refs[i]
        i += 1
        vout_ref = None
        if not layer1:
            vout_ref = refs[i]
            i += 1
        mck_ref, m_s = refs[i], refs[i + 1]
        c = pl.program_id(0)

        @pl.when(c == 0)
        def _():
            m_s[...] = jnp.zeros_like(m_s)

        prev = prev_ref[pl.ds(7, 1), :] * (c != 0).astype(f32)
        m0 = m_s[...]
        ppv = tuple(pp_ref[:, pl.ds(q, 1), :] for q in range(5))
        og, m_new, v, tm = _rwkv_step(layer1, tuple(r[...] for r in prm_refs), y_ref[...], prev,
                                      vf_ref[...] if layer1 else None, ppv, m0)
        mck_ref[0, :n_pair] = m0
        mck_ref[0, n_pair:] = tm
        if not layer1:
            vout_ref[...] = v
        for j in range(n_pair):
            cat_ref[:, j * LANES:(j + 1) * LANES] = og[j]
        m_s[...] = m_new

    grp = _group(n_pair)
    assert grp == n_pair
    out_shape = [jax.ShapeDtypeStruct((t, cat_width), f32)]
    out_specs = [pl.BlockSpec((CHUNK, grp * LANES), lambda c, p: (c, p))]
    if not layer1:
        out_shape.append(jax.ShapeDtypeStruct((t, dr), f32))
        out_specs.append(pl.BlockSpec((CHUNK, dr), lambda c, p: (c, 0)))
    out_shape.append(jax.ShapeDtypeStruct((nc, 2 * n_pair, LANES, LANES), f32))
    out_specs.append(pl.BlockSpec((1, 2 * grp, LANES, LANES), lambda c, p: (c, p, 0, 0)))
    args = [proj, proj] + ([vf] if layer1 else []) + list(prm) + [pp]
    return pl.pallas_call(
        body, grid=(nc, 1), in_specs=specs, out_specs=out_specs, out_shape=out_shape,
        scratch_shapes=[pltpu.VMEM((n_pair, LANES, LANES), f32)],
        compiler_params=pltpu.CompilerParams(dimension_semantics=("arbitrary", "arbitrary")),
        name=f"rwkv_fwd_l{int(layer1)}",
    )(*args)


def _rwkv_bwd(layer1, proj, vf, prm, pp, mck, dcat, dvout):
    t = proj.shape[0]
    dr = prm[1].shape[1]
    rwc = prm[0].shape[1]
    n_pair = dr // LANES
    nc = t // CHUNK
    n_prm = len(prm)
    specs, prm_shapes, cidx, full = _rwkv_specs(layer1, t, dr, rwc, n_pair, True)
    grp = _group(n_pair)
    assert grp == n_pair
    specs.append(pl.BlockSpec((1, 2 * grp, LANES, LANES), lambda c, p: (cidx(c), p, 0, 0)))
    specs.append(pl.BlockSpec((CHUNK, grp * LANES), lambda c, p: (cidx(c), p)))
    if not layer1:
        specs.append(pl.BlockSpec((CHUNK, dr), lambda c, p: (cidx(c), 0)))

    def body(*refs):
        y_ref, prev_ref = refs[0], refs[1]
        i = 2
        vf_ref = None
        if layer1:
            vf_ref = refs[i]
            i += 1
        prm_refs = refs[i:i + n_prm]
        i += n_prm
        pp_ref, mck_ref, dog_ref = refs[i], refs[i + 1], refs[i + 2]
        i += 3
        dvout_ref = None
        if not layer1:
            dvout_ref = refs[i]
            i += 1
        dy_ref = refs[i]
        i += 1
        dvf_ref = None
        if layer1:
            dvf_ref = refs[i]
            i += 1
        dprm_refs = refs[i:i + n_prm]
        i += n_prm
        dpp_ref = refs[i]
        dm_s, dprev_s = refs[i + 1:i + 3]
        c = pl.program_id(0)
        cr = nc - 1 - c

        @pl.when(c == 0)
        def _():
            dm_s[...] = jnp.zeros_like(dm_s)
            dprev_s[...] = jnp.zeros_like(dprev_s)
            dpp_ref[...] = jnp.zeros_like(dpp_ref)
            for r in dprm_refs:
                r[...] = jnp.zeros_like(r)

        prev = prev_ref[pl.ds(7, 1), :] * (cr != 0).astype(f32)
        prm_v = tuple(r[...] for r in prm_refs)
        ppv = tuple(pp_ref[:, pl.ds(q, 1), :] for q in range(5))
        dog = jnp.stack([dog_ref[:, j * LANES:(j + 1) * LANES] for j in range(n_pair)], axis=0)
        m0, tm = mck_ref[0, :n_pair], mck_ref[0, n_pair:]
        no_tm = jnp.zeros_like(tm)
        if layer1:
            _, vjp = jax.vjp(lambda a, b, d, e, g, h: _rwkv_step(True, a, b, d, e, g, h, tm),
                             prm_v, y_ref[...], prev, vf_ref[...], ppv, m0)
            dprm, dy, dprev, dvf, dppv, dm0 = vjp((dog, dm_s[...], jnp.zeros((CHUNK, dr), f32), no_tm))
            dvf_ref[...] = dvf
        else:
            _, vjp = jax.vjp(lambda a, b, d, e, g: _rwkv_step(False, a, b, d, None, e, g, tm), prm_v, y_ref[...], prev, ppv, m0)
            dprm, dy, dprev, dppv, dm0 = vjp((dog, dm_s[...], dvout_ref[...], no_tm))
        dm_s[...] = dm0
        for q in range(5):
            dpp_ref[:, pl.ds(q, 1), :] += dppv[q]
        dy_ref[...] = (dy + jnp.where(_iota((CHUNK, 1), 0) == CHUNK - 1, dprev_s[...], 0.0)).astype(bf16)
        dprev_s[...] = dprev
        for r, gval in zip(dprm_refs, dprm):
            r[...] += gval

    out_shape = [jax.ShapeDtypeStruct((t, proj.shape[1]), bf16)]
    out_specs = [pl.BlockSpec((CHUNK, rwc), lambda c, p: (cidx(c), 0))]
    if layer1:
        out_shape.append(jax.ShapeDtypeStruct((t, dr), f32))
        out_specs.append(pl.BlockSpec((CHUNK, dr), lambda c, p: (cidx(c), 0)))
    out_shape += [jax.ShapeDtypeStruct(s, f32) for s in prm_shapes]
    out_specs += [full(s) for s in prm_shapes]
    out_shape.append(jax.ShapeDtypeStruct((n_pair, 8, LANES), f32))
    out_specs.append(full((n_pair, 8, LANES)))
    args = [proj, proj] + ([vf] if layer1 else []) + list(prm) + [pp, mck, dcat] + ([] if layer1 else [dvout])
    return pl.pallas_call(
        body, grid=(nc, 1), in_specs=specs, out_specs=out_specs, out_shape=out_shape,
        scratch_shapes=[pltpu.VMEM((n_pair, LANES, LANES), f32), pltpu.VMEM((1, rwc), f32)],
        compiler_params=pltpu.CompilerParams(dimension_semantics=("arbitrary", "arbitrary")),
        name=f"rwkv_bwd_l{int(layer1)}",
    )(*args)


def _hgrn_chunk(layer1, lbl, gnw, s0, q_raw, f_raw, i_in, z):
    c = q_raw.shape[-2]
    q = _silu(q_raw)
    ls = _log_sigmoid(f_raw)
    if layer1:
        l0, l1 = lbl[..., 0:1, :], lbl[..., 1:2, :]
        mx = jnp.maximum(l0, l1)
        e0, e1 = jnp.exp(l0 - mx), jnp.exp(l1 - mx)
        sm0, sm1 = e0 / (e0 + e1), e1 / (e0 + e1)
        lb = (sm0 + sm1) - sm0
        log_f = _logaddexp(jnp.log(jnp.maximum(lb, LB_FLOOR)), jnp.log1p(-lb) + ls)
        k = (1.0 - lb) * jax.nn.sigmoid(-f_raw)
    else:
        log_f = _logaddexp(jnp.full_like(ls, jnp.log(jnp.float32(LB_FLOOR))), ls)
        k = jax.nn.sigmoid(-f_raw)
    row, col = _iota((c, c), 0), _iota((c, c), 1)
    trow = _iota((c, 1), 0)
    halves = []
    half = c // 2
    while half >= 1:
        halves.append(half)
        half //= 2
    cmat = jnp.concatenate([(col <= row).astype(f32)]
                           + [(col <= (row // (2 * hf)) * (2 * hf) + hf - 1).astype(f32) for hf in halves], axis=0)
    ball = _const_left(cmat.astype(bf16), log_f)
    b = ball[..., :c, :]
    att = None
    for lvl, hf in enumerate(halves):
        blk = 2 * hf
        bref = ball[..., (lvl + 1) * c:(lvl + 2) * c, :]
        upper = (trow % blk) >= hf
        qh = q * jnp.exp(jnp.where(upper, b - bref, 0.0)) * upper.astype(f32)
        kh = k * jnp.exp(jnp.where(upper, 0.0, bref - b)) * (1.0 - upper.astype(f32))
        term = jnp.where(row // blk == col // blk, _mm2(qh, kh, "nt", APPLY_PASSES), 0.0)
        att = term if att is None else att + term
    lhs = jnp.concatenate([q * jnp.exp(b), att, jnp.zeros(att.shape[:-1] + (LANES - c,), f32)], axis=-1)
    rhs = jnp.concatenate([s0, i_in, jnp.zeros(i_in.shape[:-2] + (LANES - c, i_in.shape[-1]), f32)], axis=-2)
    o = _mm2(lhs, rhs, "nn", APPLY_PASSES) + jnp.sum(q * k, axis=-1, keepdims=True) * i_in
    b_last = _last_row(b)
    s_new = _col_of_row(jnp.exp(b_last)) * s0 + _mm2(k * jnp.exp(b_last - b), i_in, "tn", APPLY_PASSES)
    o = o * lax.rsqrt(jnp.mean(o * o, axis=-1, keepdims=True) + RMS_EPS)
    return o * gnw * _silu(z), s_new


def _hgrn_in_specs(t, dh, col0, rev):
    nc = t // CHUNK
    nh = dh // LANES

    def cidx(c):
        return (nc - 1 - c) if rev else c

    grp = _group(nh)
    specs = [pl.BlockSpec((CHUNK, LANES), functools.partial(lambda g, j, h, c: (cidx(c), col0 + g * nh + h * grp + j), g, j))
             for j in range(grp) for g in range(4)]
    specs.append(pl.BlockSpec((2, grp * LANES), lambda h, c: (0, h)))
    specs.append(pl.BlockSpec((1, grp * LANES), lambda h, c: (0, h)))
    return specs, cidx, grp


def _hgrn_fwd(layer1, proj, lbl, gnw, cat, rwc):
    t, d = cat.shape
    dh = gnw.shape[1]
    nh = dh // LANES
    nc = t // CHUNK
    col0 = rwc // LANES
    specs, _, grp = _hgrn_in_specs(t, dh, col0, False)
    specs.append(pl.BlockSpec(memory_space=pl.ANY))
    assert (d - dh) % (grp * LANES) == 0
    cat_col0 = (d - dh) // (grp * LANES)

    def body(*refs):
        x_refs = refs[:4 * grp]
        lbl_ref, gnw_ref, _, cat_ref, sck_ref, s_s = refs[4 * grp:]
        c = pl.program_id(1)

        @pl.when(c == 0)
        def _():
            s_s[...] = jnp.zeros_like(s_s)

        lanes = [slice(j * LANES, (j + 1) * LANES) for j in range(grp)]
        s0 = s_s[...]
        sck_ref[:, 0] = s0
        out, s_new = _hgrn_chunk(layer1, jnp.stack([lbl_ref[:, ln] for ln in lanes]), jnp.stack([gnw_ref[:, ln] for ln in lanes]),
                                 s0, *(jnp.stack([x_refs[4 * j + g][...] for j in range(grp)]) for g in range(4)))
        for j in range(grp):
            cat_ref[:, lanes[j]] = out[j]
        s_s[...] = s_new

    return pl.pallas_call(
        body, grid=(nh // grp, nc), in_specs=specs,
        out_specs=[pl.BlockSpec((CHUNK, grp * LANES), lambda h, c: (c, cat_col0 + h)),
                   pl.BlockSpec((grp, 1, LANES, LANES), lambda h, c: (h, c, 0, 0))],
        out_shape=[jax.ShapeDtypeStruct((t, d), f32), jax.ShapeDtypeStruct((nh, nc, LANES, LANES), f32)],
        scratch_shapes=[pltpu.VMEM((grp, LANES, LANES), f32)],
        input_output_aliases={4 * grp + 2: 0},
        compiler_params=pltpu.CompilerParams(dimension_semantics=("arbitrary", "arbitrary")),
        name=f"hgrn_fwd_l{int(layer1)}",
    )(*([proj] * (4 * grp)), lbl, gnw, cat)


def _hgrn_bwd(layer1, proj, lbl, gnw, sck, dcat, rwc, dproj):
    t, d = dcat.shape
    dh = gnw.shape[1]
    nh = dh // LANES
    nc = t // CHUNK
    col0 = rwc // LANES
    specs, cidx, grp = _hgrn_in_specs(t, dh, col0, True)
    assert grp == nh and (d - dh) % (grp * LANES) == 0
    cat_col0 = (d - dh) // (grp * LANES)
    specs.append(pl.BlockSpec((grp, 1, LANES, LANES), lambda h, c: (h, cidx(c), 0, 0)))
    specs.append(pl.BlockSpec((CHUNK, grp * LANES), lambda h, c: (cidx(c), cat_col0 + h)))
    specs.append(pl.BlockSpec(memory_space=pl.ANY))

    def body(*refs):
        x_refs = refs[:4 * grp]
        lbl_ref, gnw_ref, sck_ref, do_ref, _, dp_hbm, dlbl_ref, dgnw_ref, ds_s, stage, sems = refs[4 * grp:]
        c = pl.program_id(1)
        slot = c % 2

        def put(s, g, chunk):
            return pltpu.make_async_copy(stage.at[s, g], dp_hbm.at[pl.ds(chunk * CHUNK, CHUNK), pl.ds(rwc + g * dh, dh)],
                                         sems.at[s, g])

        @pl.when(c == 0)
        def _():
            ds_s[...] = jnp.zeros_like(ds_s)
            dlbl_ref[...] = jnp.zeros_like(dlbl_ref)
            dgnw_ref[...] = jnp.zeros_like(dgnw_ref)

        @pl.when(c >= 2)
        def _():
            for g in range(4):
                put(slot, g, 0).wait()

        lanes = [slice(j * LANES, (j + 1) * LANES) for j in range(grp)]
        _, vjp = jax.vjp(functools.partial(_hgrn_chunk, layer1),
                         jnp.stack([lbl_ref[:, ln] for ln in lanes]), jnp.stack([gnw_ref[:, ln] for ln in lanes]), sck_ref[:, 0],
                         *(jnp.stack([x_refs[4 * j + g][...] for j in range(grp)]) for g in range(4)))
        dlbl, dgnw, ds0, dq, df, di, dz = vjp((jnp.stack([do_ref[:, ln] for ln in lanes]), ds_s[...]))
        ds_s[...] = ds0
        for j in range(grp):
            dlbl_ref[:, lanes[j]] += dlbl[j]
            dgnw_ref[:, lanes[j]] += dgnw[j]
            for g, val in enumerate((dq, df, di, dz)):
                stage[slot, g, :, lanes[j]] = val[j].astype(bf16)
        for g in range(4):
            put(slot, g, nc - 1 - c).start()

        @pl.when(c == nc - 1)
        def _():
            for g in range(4):
                put(slot, g, 0).wait()
                if nc >= 2:
                    put(1 - slot, g, 0).wait()

    return pl.pallas_call(
        body, grid=(1, nc), in_specs=specs,
        out_specs=[pl.BlockSpec(memory_space=pl.ANY),
                   pl.BlockSpec((2, grp * LANES), lambda h, c: (0, h)),
                   pl.BlockSpec((1, grp * LANES), lambda h, c: (0, h))],
        out_shape=[jax.ShapeDtypeStruct(dproj.shape, dproj.dtype), jax.ShapeDtypeStruct((2, dh), f32),
                   jax.ShapeDtypeStruct((1, dh), f32)],
        scratch_shapes=[pltpu.VMEM((grp, LANES, LANES), f32), pltpu.VMEM((2, 4, CHUNK, dh), bf16),
                        pltpu.SemaphoreType.DMA((2, 4))],
        input_output_aliases={4 * grp + 4: 0},
        compiler_params=pltpu.CompilerParams(dimension_semantics=("arbitrary", "arbitrary")),
        name=f"hgrn_bwd_l{int(layer1)}",
    )(*([proj] * (4 * grp)), lbl, gnw, sck, dcat, dproj)


def _ln(h, y, w, b):
    u = ALPHA * h + y
    mu = jnp.mean(u, axis=-1, keepdims=True)
    var = jnp.mean(jnp.square(u - mu), axis=-1, keepdims=True)
    return (u - mu) * lax.rsqrt(var + LN_EPS) * w + b


def _row_tile(t):
    return 256 if t % 256 == 0 else t


def _ln_fwd(h, y, w, b):
    t, d = h.shape
    tr = _row_tile(t)

    def body(h_ref, y_ref, w_ref, b_ref, o_ref, o16_ref):
        out = _ln(h_ref[...], y_ref[...], w_ref[...], b_ref[...])
        o_ref[...] = out
        o16_ref[...] = out.astype(bf16)

    row = pl.BlockSpec((tr, d), lambda i: (i, 0))
    vec = pl.BlockSpec((1, d), lambda i: (0, 0))
    return pl.pallas_call(body, grid=(t // tr,), in_specs=[row, row, vec, vec], out_specs=[row, row],
                          out_shape=[jax.ShapeDtypeStruct((t, d), f32), jax.ShapeDtypeStruct((t, d), bf16)],
                          name="ln_fwd")(h, y, w, b)


def _ln_loss_bwd(h, y, w, b, tgt):
    t, d = h.shape
    tr = _row_tile(t)

    def body(h_ref, y_ref, w_ref, b_ref, t_ref, dy_ref, dy16_ref, dw_ref, db_ref, loss_ref):
        @pl.when(pl.program_id(0) == 0)
        def _():
            dw_ref[...] = jnp.zeros_like(dw_ref)
            db_ref[...] = jnp.zeros_like(db_ref)
            loss_ref[...] = jnp.zeros_like(loss_ref)

        out, vjp = jax.vjp(lambda yy, ww, bb: _ln(h_ref[...], yy, ww, bb), y_ref[...], w_ref[...], b_ref[...])
        err = out - t_ref[...]
        loss_ref[...] += 0.5 * jnp.sum(jnp.mean(jnp.square(err), axis=-1, keepdims=True), axis=0, keepdims=True)
        dy, dw, db = vjp(err * (1.0 / d))
        dy_ref[...] = dy
        dy16_ref[...] = dy.astype(bf16)
        dw_ref[...] += dw
        db_ref[...] += db

    row = pl.BlockSpec((tr, d), lambda i: (i, 0))
    vec = pl.BlockSpec((1, d), lambda i: (0, 0))
    return pl.pallas_call(
        body, grid=(t // tr,), in_specs=[row, row, vec, vec, row],
        out_specs=[row, row, vec, vec, pl.BlockSpec((1, LANES), lambda i: (0, 0))],
        out_shape=[jax.ShapeDtypeStruct((t, d), f32), jax.ShapeDtypeStruct((t, d), bf16), jax.ShapeDtypeStruct((1, d), f32),
                   jax.ShapeDtypeStruct((1, d), f32), jax.ShapeDtypeStruct((1, LANES), f32)],
        compiler_params=pltpu.CompilerParams(dimension_semantics=("arbitrary",)), name="ln_loss_bwd")(h, y, w, b, tgt)


def _ln_bwd(h, y, w, b, dout):
    t, d = h.shape
    tr = _row_tile(t)

    def body(h_ref, y_ref, w_ref, b_ref, do_ref, dy_ref, dy16_ref, dw_ref, db_ref):
        @pl.when(pl.program_id(0) == 0)
        def _():
            dw_ref[...] = jnp.zeros_like(dw_ref)
            db_ref[...] = jnp.zeros_like(db_ref)

        _, vjp = jax.vjp(lambda yy, ww, bb: _ln(h_ref[...], yy, ww, bb), y_ref[...], w_ref[...], b_ref[...])
        dy, dw, db = vjp(do_ref[...])
        dy_ref[...] = dy
        dy16_ref[...] = dy.astype(bf16)
        dw_ref[...] += dw
        db_ref[...] += db

    row = pl.BlockSpec((tr, d), lambda i: (i, 0))
    vec = pl.BlockSpec((1, d), lambda i: (0, 0))
    return pl.pallas_call(
        body, grid=(t // tr,), in_specs=[row, row, vec, vec, row], out_specs=[row, row, vec, vec],
        out_shape=[jax.ShapeDtypeStruct((t, d), f32), jax.ShapeDtypeStruct((t, d), bf16),
                   jax.ShapeDtypeStruct((1, d), f32), jax.ShapeDtypeStruct((1, d), f32)],
        compiler_params=pltpu.CompilerParams(dimension_semantics=("arbitrary",)), name="ln_bwd")(h, y, w, b, dout)


def _pick(n, prefs):
    for p in prefs:
        if n % p == 0:
            return p
    return n


def _tile(n, want):
    if n <= want:
        return n
    for cand in range(want - want % LANES, 0, -LANES):
        if n % cand == 0:
            return cand
    return n


def _matmul(a, b, mode, name, tiles, add=None, add_scale=1.0, out_dtype=f32):
    if mode == "nn":
        (m, k), n = a.shape, b.shape[1]
    elif mode == "nt":
        (m, k), n = a.shape, b.shape[0]
    else:
        (k, m), n = a.shape, b.shape[1]
    tm, tn, tk = _tile(m, tiles[0]), _tile(n, tiles[1]), _tile(k, tiles[2])
    nk = k // tk
    cache_a = nk == 1 and a.dtype != bf16 and n // tn > 1

    def body(*refs):
        a_ref, b_ref = refs[0], refs[1]
        add_ref = refs[2] if add is not None else None
        n_in = 3 if add is not None else 2
        o_ref = refs[n_in]
        scratch = refs[n_in + 1:]

        def finish(res):
            if add is not None:
                res = res + add_scale * add_ref[...]
            o_ref[...] = res.astype(out_dtype)

        if cache_a:
            a_bf = scratch[0]

            @pl.when(pl.program_id(1) == 0)
            def _():
                a_bf[...] = a_ref[...].astype(bf16)

            a_val = a_bf[...]
        else:
            a_val = a_ref[...].astype(bf16)
        prod = lax.dot_general(a_val, b_ref[...].astype(bf16), _DIMS[mode], preferred_element_type=f32)
        if nk == 1:
            finish(prod)
        else:
            acc = scratch[-1]
            kk = pl.program_id(2)

            @pl.when(kk == 0)
            def _():
                acc[...] = prod

            @pl.when(kk != 0)
            def _():
                acc[...] += prod

            @pl.when(kk == nk - 1)
            def _():
                finish(acc[...])

    a_shape = (tk, tm) if mode == "tn" else (tm, tk)
    a_spec = pl.BlockSpec(a_shape, (lambda i, j, kk: (kk, i)) if mode == "tn" else (lambda i, j, kk: (i, kk)))
    b_spec = pl.BlockSpec((tn, tk), lambda i, j, kk: (j, kk)) if mode == "nt" else pl.BlockSpec((tk, tn), lambda i, j, kk: (kk, j))
    o_spec = pl.BlockSpec((tm, tn), lambda i, j, kk: (i, j))
    in_specs = [a_spec, b_spec] + ([o_spec] if add is not None else [])
    args = [a, b] + ([add] if add is not None else [])
    scratch_shapes = ([pltpu.VMEM(a_shape, bf16)] if cache_a else []) + ([pltpu.VMEM((tm, tn), f32)] if nk > 1 else [])
    return pl.pallas_call(
        body, grid=(m // tm, n // tn, nk), in_specs=in_specs, out_specs=o_spec,
        out_shape=jax.ShapeDtypeStruct((m, n), out_dtype), scratch_shapes=scratch_shapes,
        compiler_params=pltpu.CompilerParams(dimension_semantics=("parallel", "arbitrary", "arbitrary")),
        name=name,
    )(*args)


def _position():
    return lax.axis_index("x"), lax.axis_index("y"), lax.axis_index("c")


def _flip(pos, k):
    x, y, c = pos
    return (1 - x if k & 4 else x, 1 - y if k & 2 else y, 1 - c if k & 1 else c)


def _index(pos):
    return 4 * pos[0] + 2 * pos[1] + pos[2]


def _all_gather_rows(x, name):
    m_per, n = x.shape

    def body(x_ref, out_ref, send_sems, recv_sems, local_sem):
        me = _position()
        sibling = _flip(me, 1)
        chips = (2, 4, 6)

        def rows(pos):
            return out_ref.at[pl.ds(_index(pos) * m_per, m_per), :]

        def copy(sem, block, to, src=None):
            return pltpu.make_async_remote_copy(
                src_ref=rows(block) if src is None else src, dst_ref=rows(block),
                send_sem=send_sems.at[sem], recv_sem=recv_sems.at[sem], device_id=to, device_id_type=MESH)

        mine = pltpu.make_async_copy(x_ref, rows(me), local_sem)
        mine.start()
        first = [copy(0, me, sibling, src=x_ref)]
        first += [copy(1 + j, me, _flip(me, k), src=x_ref) for j, k in enumerate(chips)]
        for cp in first:
            cp.start()
        passed = [copy(4 + j, _flip(me, k), sibling) for j, k in enumerate(chips)]
        for j, k in enumerate(chips):
            copy(1 + j, _flip(me, k), me).wait_recv()
            passed[j].start()
        copy(0, sibling, me).wait_recv()
        for j, k in enumerate(chips):
            copy(4 + j, _flip(sibling, k), me).wait_recv()
        for cp in first + passed:
            cp.wait_send()
        mine.wait()

    return pl.pallas_call(
        body, out_shape=jax.ShapeDtypeStruct((N_DEV * m_per, n), x.dtype),
        in_specs=[pl.BlockSpec(memory_space=pl.ANY)], out_specs=pl.BlockSpec(memory_space=pl.ANY),
        scratch_shapes=[pltpu.SemaphoreType.DMA((7,)), pltpu.SemaphoreType.DMA((7,)), pltpu.SemaphoreType.DMA(())],
        name=name,
    )(x)


def _split_start(srcs, lands, plan, n_copies, name):
    n_arr = len(srcs)
    hbm = pl.BlockSpec(memory_space=pltpu.HBM)
    sem = pl.BlockSpec(memory_space=pltpu.SEMAPHORE)

    def body(*refs):
        src_refs, land_refs = refs[:n_arr], refs[n_arr:2 * n_arr]
        send_sems, recv_sems = refs[2 * n_arr:3 * n_arr], refs[3 * n_arr:4 * n_arr]
        token = refs[-1]
        me = _position()
        for i in range(n_arr):
            for j, (src, dst, peer, _) in enumerate(plan(i, src_refs[i], land_refs[i], me)):
                pltpu.make_async_remote_copy(src_ref=src, dst_ref=dst, send_sem=send_sems[i].at[j], recv_sem=recv_sems[i].at[j],
                                             device_id=peer, device_id_type=MESH).start()
        token[...] = jnp.zeros_like(token)

    outs = pl.pallas_call(
        body, name=name,
        out_shape=([pltpu.SemaphoreType.DMA((n_copies,))] * (2 * n_arr)
                   + [pltpu.HBM(a.shape, a.dtype) for a in list(srcs) + list(lands)]
                   + [jax.ShapeDtypeStruct((8, LANES), f32)]),
        in_specs=[hbm] * (2 * n_arr),
        out_specs=[sem] * (2 * n_arr) + [hbm] * (2 * n_arr) + [pl.BlockSpec(memory_space=pltpu.VMEM)],
        input_output_aliases={i: 2 * n_arr + i for i in range(2 * n_arr)},
        compiler_params=pltpu.CompilerParams(has_side_effects=pltpu.SideEffectType.DATAFLOW_SIDE_EFFECTING),
    )(*[pltpu.with_memory_space_constraint(a, pltpu.HBM) for a in list(srcs) + list(lands)])
    return (outs[:n_arr], outs[n_arr:2 * n_arr], outs[2 * n_arr:3 * n_arr], outs[3 * n_arr:4 * n_arr], outs[-1])


def _split_wait(started, plan, after, name):
    send_sems, recv_sems, srcs, lands, _ = started
    n_arr = len(srcs)
    hbm = pl.BlockSpec(memory_space=pltpu.HBM)
    sem = pl.BlockSpec(memory_space=pltpu.SEMAPHORE)

    def body(*refs):
        src_refs, land_refs = refs[:n_arr], refs[n_arr:2 * n_arr]
        s_sems, r_sems = refs[2 * n_arr:3 * n_arr], refs[3 * n_arr:4 * n_arr]
        me = _position()
        for i in range(n_arr):
            for j, (src, _, peer, arrival) in enumerate(plan(i, src_refs[i], land_refs[i], me)):
                cp = pltpu.make_async_remote_copy(src_ref=src, dst_ref=arrival, send_sem=s_sems[i].at[j], recv_sem=r_sems[i].at[j],
                                                  device_id=peer, device_id_type=MESH)
                cp.wait_send()
                cp.wait_recv()

    outs = pl.pallas_call(
        body, name=name,
        out_shape=[pltpu.HBM(a.shape, a.dtype) for a in list(srcs) + list(lands)],
        in_specs=[hbm] * (2 * n_arr) + [sem] * (2 * n_arr) + [pl.BlockSpec(memory_space=pl.ANY)],
        out_specs=[hbm] * (2 * n_arr),
        input_output_aliases={i: i for i in range(2 * n_arr)},
        compiler_params=pltpu.CompilerParams(has_side_effects=pltpu.SideEffectType.DATAFLOW_SIDE_EFFECTING),
    )(*srcs, *lands, *send_sems, *recv_sems, after)
    return outs[:n_arr], outs[n_arr:]


_GATHER_FLIPS = (1, 2, 4, 6)


def _gather_plan(i, src_ref, land_ref, me):
    m = src_ref.shape[0]

    def rows(pos):
        return land_ref.at[pl.ds(_index(pos) * m, m), :]

    return [(src_ref, rows(me), _flip(me, k), rows(_flip(me, k))) for k in _GATHER_FLIPS]


def _gather_forward(lands, name):
    n_arr = len(lands)
    chips = (2, 4, 6)

    def body(*refs):
        out_refs = refs[n_arr:2 * n_arr]
        send_sems, recv_sems = refs[2 * n_arr:]
        me = _position()
        sibling = _flip(me, 1)
        sends, arrivals = [], []
        for i, out_ref in enumerate(out_refs):
            m = out_ref.shape[0] // N_DEV

            def copy(pos, j):
                blk = out_ref.at[pl.ds(_index(pos) * m, m), :]
                return pltpu.make_async_remote_copy(src_ref=blk, dst_ref=blk, send_sem=send_sems.at[3 * i + j],
                                                    recv_sem=recv_sems.at[3 * i + j], device_id=sibling, device_id_type=MESH)

            for j, k in enumerate(chips):
                sends.append(copy(_flip(me, k), j))
                arrivals.append(copy(_flip(sibling, k), j))
        for cp in sends:
            cp.start()
        for cp in arrivals:
            cp.wait_recv()
        for cp in sends:
            cp.wait_send()

    anyspec = pl.BlockSpec(memory_space=pl.ANY)
    return pl.pallas_call(
        body, out_shape=[jax.ShapeDtypeStruct(a.shape, a.dtype) for a in lands],
        in_specs=[anyspec] * n_arr, out_specs=[anyspec] * n_arr, input_output_aliases={i: i for i in range(n_arr)},
        scratch_shapes=[pltpu.SemaphoreType.DMA((3 * n_arr,))] * 2, name=name,
    )(*lands)


def _chips_plan(i, src_ref, land_ref, me):
    m = src_ref.shape[0] // 4
    plan = []
    for j, k in enumerate((2, 4, 6)):
        peer = _flip(me, k)
        plan.append((src_ref.at[pl.ds((2 * peer[0] + peer[1]) * m, m), :], land_ref.at[j], peer, land_ref.at[j]))
    return plan


def _exchange_siblings(gs, name):
    n_arr = len(gs)

    def body(*refs):
        g_refs, out_refs = refs[:n_arr], refs[n_arr:2 * n_arr]
        send_sems, recv_sems = refs[2 * n_arr:]
        me = _position()
        c = me[2]
        sibling = _flip(me, 1)
        copies = []
        for i, (g_ref, out_ref) in enumerate(zip(g_refs, out_refs)):
            m_per = g_ref.shape[0] // N_DEV
            for q in range(4):
                copies.append(pltpu.make_async_remote_copy(
                    src_ref=g_ref.at[pl.ds((2 * q + 1 - c) * m_per, m_per), :], dst_ref=out_ref.at[q],
                    send_sem=send_sems.at[4 * i + q], recv_sem=recv_sems.at[4 * i + q],
                    device_id=sibling, device_id_type=MESH))
        for cp in copies:
            cp.start()
        for cp in copies:
            cp.wait_recv()
        for cp in copies:
            cp.wait_send()

    anyspec = pl.BlockSpec(memory_space=pl.ANY)
    return pl.pallas_call(
        body, out_shape=[jax.ShapeDtypeStruct((4, g.shape[0] // N_DEV, g.shape[1]), g.dtype) for g in gs],
        in_specs=[anyspec] * n_arr, out_specs=[anyspec] * n_arr,
        scratch_shapes=[pltpu.SemaphoreType.DMA((4 * n_arr,))] * 2, name=name,
    )(*gs)


def _sum_with_sibling(g, recv, name):
    m = g.shape[0] // N_DEV
    n = g.shape[1]
    tr = _pick(m, (208, 128, 64, 32, 16))
    nt = m // tr

    def body(g_ref, r_ref, o_ref):
        c = lax.axis_index("c")
        own = jnp.where(c == 0, g_ref[0, 0].astype(f32), g_ref[0, 1].astype(f32))
        o_ref[...] = (own + r_ref[0].astype(f32)).astype(o_ref.dtype)

    return pl.pallas_call(
        body, grid=(4, nt),
        in_specs=[pl.BlockSpec((1, 2, tr, n), lambda q, i: (q, 0, i, 0)), pl.BlockSpec((1, tr, n), lambda q, i: (q, i, 0))],
        out_specs=pl.BlockSpec((tr, n), lambda q, i: (q * nt + i, 0)),
        out_shape=jax.ShapeDtypeStruct((4 * m, n), bf16), name=name,
    )(g.reshape(4, 2, m, n), recv)


def _sum_with_chips(h, recv, name):
    m = h.shape[0] // 4
    n = h.shape[1]
    tr = _pick(m, (208, 128, 64, 32, 16))

    def body(h_ref, r_ref, o_ref):
        my_q = 2 * lax.axis_index("x") + lax.axis_index("y")
        own = h_ref[0].astype(f32)
        for q in range(1, 4):
            own = jnp.where(my_q == q, h_ref[q].astype(f32), own)
        o_ref[...] = ((own + r_ref[0].astype(f32)) + r_ref[1].astype(f32)) + r_ref[2].astype(f32)

    return pl.pallas_call(
        body, grid=(m // tr,),
        in_specs=[pl.BlockSpec((4, tr, n), lambda i: (0, i, 0)), pl.BlockSpec((3, tr, n), lambda i: (0, i, 0))],
        out_specs=pl.BlockSpec((tr, n), lambda i: (i, 0)), out_shape=jax.ShapeDtypeStruct((m, n), f32), name=name,
    )(h.reshape(4, m, n), recv)


def _sum_slots(parts, name):
    n_slot, m, n = parts.shape
    tr = _pick(m, (208, 128, 64, 32, 16, 8))

    def body(p_ref, o_ref):
        acc = p_ref[0]
        for s in range(1, n_slot):
            acc = acc + p_ref[s]
        o_ref[...] = acc

    return pl.pallas_call(
        body, grid=(m // tr,), in_specs=[pl.BlockSpec((n_slot, tr, n), lambda i: (0, i, 0))],
        out_specs=pl.BlockSpec((tr, n), lambda i: (i, 0)), out_shape=jax.ShapeDtypeStruct((m, n), parts.dtype), name=name,
    )(parts)


def _reduce_scatter_begin(gs, name):
    from_sibling = _exchange_siblings(gs, "rs_d2d_" + name)
    chip_sums = [_sum_with_sibling(g, r, f"rs_sum2_{name}_{i}") for i, (g, r) in enumerate(zip(gs, from_sibling))]
    lands = [lax.empty((3, h.shape[0] // 4, h.shape[1]), h.dtype) for h in chip_sums]
    return _split_start(chip_sums, lands, _chips_plan, 3, "rs_ici_start_" + name)


def _reduce_scatter_end(started, after, name):
    chip_sums, from_chips = _split_wait(started, _chips_plan, after, "rs_ici_wait_" + name)
    return [_sum_with_chips(h, r, f"rs_sum4_{name}_{i}") for i, (h, r) in enumerate(zip(chip_sums, from_chips))]


def _adamw_update(w, g, m, v):
    mm = ADAM_B1 * m + (1.0 - ADAM_B1) * g
    vv = ADAM_B2 * v + (1.0 - ADAM_B2) * jnp.square(g)
    m_hat = mm / (1.0 - ADAM_B1 ** ADAM_STEP)
    v_hat = vv / (1.0 - ADAM_B2 ** ADAM_STEP)
    return -ADAM_LR * (m_hat / (jnp.sqrt(v_hat) + ADAM_EPS) + ADAM_WD * w), mm, vv


def _adamw_many(ws, gs, ms, vs, name):
    k = len(ws)
    shapes = [w.shape for w in ws]
    flat = [[a.reshape(-1, a.shape[-1]) for a in group] for group in (ws, gs, ms, vs)]

    def body(*refs):
        for i in range(k):
            d, mm, vv = _adamw_update(*(refs[j * k + i][...] for j in range(4)))
            refs[4 * k + i][...] = d
            refs[5 * k + i][...] = mm
            refs[6 * k + i][...] = vv

    outs = pl.pallas_call(
        body, out_shape=[jax.ShapeDtypeStruct(a.shape, f32) for a in flat[0]] * 3, name=name,
    )(*flat[0], *flat[1], *flat[2], *flat[3])
    return tuple([outs[j * k + i].reshape(shapes[i]) for i in range(k)] for j in range(3))


def _adamw(w, g, m, v, name):
    shape = w.shape
    n = shape[-1]
    r = w.size // n
    w2, g2, m2, v2 = (a.reshape(r, n) for a in (w, g, m, v))
    tr = _pick(r, (256, 208, 128, 64, 32, 16, 8))

    def body(w_ref, g_ref, m_ref, v_ref, d_ref, mo_ref, vo_ref):
        d_ref[...], mo_ref[...], vo_ref[...] = _adamw_update(w_ref[...], g_ref[...], m_ref[...], v_ref[...])

    spec = pl.BlockSpec((tr, n), lambda i: (i, 0))
    outs = pl.pallas_call(
        body, grid=(r // tr,), in_specs=[spec] * 4, out_specs=[spec] * 3,
        out_shape=[jax.ShapeDtypeStruct((r, n), f32)] * 3, name=name,
    )(w2, g2, m2, v2)
    return tuple(o.reshape(shape) for o in outs)


_SMALL = ("shift_mu", "w_decay0", "a0", "k_k", "k_a", "r_k", "ln_x_w", "ln_x_b", "v_mix0", "lb_logits",
          "g_norm_w", "ln_w", "ln_b")
_NAMES = ("w_in", "shift_mu", "w_decay0", "w_decay_up", "a0", "a_up", "k_k", "k_a", "r_k", "ln_x_w", "ln_x_b",
          "v_mix0", "v_mix_down", "v_mix_up", "lb_logits", "g_norm_w", "w_out", "ln_w", "ln_b")


def _pad_rows(a, rows, at_end):
    z = jnp.zeros((rows - a.shape[0], a.shape[1]), a.dtype)
    return jnp.concatenate([a, z] if at_end else [z, a], axis=0)


def kernel(x, w_in, shift_mu, w_decay0, w_decay_up, a0, a_up, k_k, k_a, r_k, ln_x_w, ln_x_b, v_mix0, v_mix_down, v_mix_up, lb_logits, g_norm_w, w_out, ln_w, ln_b, loss_target, m_w_in, m_shift_mu, m_w_decay0, m_w_decay_up, m_a0, m_a_up, m_k_k, m_k_a, m_r_k, m_ln_x_w, m_ln_x_b, m_v_mix0, m_v_mix_down, m_v_mix_up, m_lb_logits, m_g_norm_w, m_w_out, m_ln_w, m_ln_b, v_w_in, v_shift_mu, v_w_decay0, v_w_decay_up, v_a0, v_a_up, v_k_k, v_k_a, v_r_k, v_ln_x_w, v_ln_x_b, v_v_mix0, v_v_mix_down, v_v_mix_up, v_lb_logits, v_g_norm_w, v_w_out, v_ln_w, v_ln_b):
    weights = dict(w_in=w_in, shift_mu=shift_mu, w_decay0=w_decay0, w_decay_up=w_decay_up, a0=a0, a_up=a_up, k_k=k_k,
                   k_a=k_a, r_k=r_k, ln_x_w=ln_x_w, ln_x_b=ln_x_b, v_mix0=v_mix0, v_mix_down=v_mix_down,
                   v_mix_up=v_mix_up, lb_logits=lb_logits, g_norm_w=g_norm_w, w_out=w_out, ln_w=ln_w, ln_b=ln_b)
    mom1 = dict(w_in=m_w_in, shift_mu=m_shift_mu, w_decay0=m_w_decay0, w_decay_up=m_w_decay_up, a0=m_a0, a_up=m_a_up,
                k_k=m_k_k, k_a=m_k_a, r_k=m_r_k, ln_x_w=m_ln_x_w, ln_x_b=m_ln_x_b, v_mix0=m_v_mix0,
                v_mix_down=m_v_mix_down, v_mix_up=m_v_mix_up, lb_logits=m_lb_logits, g_norm_w=m_g_norm_w,
                w_out=m_w_out, ln_w=m_ln_w, ln_b=m_ln_b)
    mom2 = dict(w_in=v_w_in, shift_mu=v_shift_mu, w_decay0=v_w_decay0, w_decay_up=v_w_decay_up, a0=v_a0, a_up=v_a_up,
                k_k=v_k_k, k_a=v_k_a, r_k=v_r_k, ln_x_w=v_ln_x_w, ln_x_b=v_ln_x_b, v_mix0=v_v_mix0,
                v_mix_down=v_v_mix_down, v_mix_up=v_v_mix_up, lb_logits=v_lb_logits, g_norm_w=v_g_norm_w,
                w_out=v_w_out, ln_w=v_ln_w, ln_b=v_ln_b)
    assert x.shape[0] == 1 and w_in.shape[0] == DEPTH
    t, d = x.shape[1], x.shape[2]
    dr = w_decay0.shape[1]
    dh = g_norm_w.shape[1]
    rank_w, rank_a, rank_v = w_decay_up.shape[1], a_up.shape[1], v_mix_up.shape[1]
    rwc = 4 * dr + rank_w + rank_a
    assert rank_w + rank_a == LANES and rank_v <= LANES and dr + dh == d
    assert t % CHUNK == 0 and dr % LANES == 0 and dh % LANES == 0 and shift_mu.shape[1] == rwc
    n_pair = dr // LANES
    me = _index(_position())

    win_t = [_all_gather_rows(w_in[0].T.astype(bf16), "ag_w_in_0"), None]
    wout = [None, None]
    late_blocks = [w_out[0].astype(bf16), w_in[1].T.astype(bf16), w_out[1].astype(bf16)]
    late_lands = [lax.dynamic_update_slice(lax.empty((N_DEV * blk.shape[0], blk.shape[1]), bf16), blk, (me * blk.shape[0], 0))
                  for blk in late_blocks]
    late_gather = _split_start(late_blocks, late_lands, _gather_plan, len(_GATHER_FLIPS), "ag_late_start")
    shard = dr // N_DEV
    pack = jnp.concatenate([w_decay_up[0], w_decay_up[1], a_up[0], a_up[1], v_mix_up[0], v_mix_down[0].T], axis=0)
    pack = _all_gather_rows(pack, "ag_small")
    pack = jnp.transpose(pack.reshape(N_DEV, -1, shard), (1, 0, 2)).reshape(-1, dr)
    offs = [0, rank_w, 2 * rank_w, 2 * rank_w + rank_a, 2 * rank_w + 2 * rank_a, 2 * rank_w + 2 * rank_a + rank_v,
            2 * rank_w + 2 * rank_a + 2 * rank_v]
    wdu_f = [pack[offs[0]:offs[1]], pack[offs[1]:offs[2]]]
    aup_f = [pack[offs[2]:offs[3]], pack[offs[3]:offs[4]]]
    vup_f = pack[offs[4]:offs[5]]
    vdown_f = pack[offs[5]:offs[6]].T

    def after_start(a, started):
        return a + started[-1][0:1, 0:1]

    def rwkv_params(l):
        mu = after_start(shift_mu[0:1], late_gather) if l == 0 else shift_mu[l:l + 1]
        prm = [mu, w_decay0[l:l + 1], a0[l:l + 1], _pad_rows(wdu_f[l], LANES, True),
               _pad_rows(aup_f[l], LANES, False)]
        if l == 1:
            prm += [v_mix0[0:1], _pad_rows(vdown_f.T, LANES, True).T, _pad_rows(vup_f, LANES, True)]
        rows = jnp.stack([k_k[l], k_a[l], r_k[l], ln_x_w[l], ln_x_b[l]] + [jnp.zeros((dr,), f32)] * 3, axis=0)
        pp = jnp.transpose(rows.reshape(8, n_pair, LANES), (1, 0, 2))
        return tuple(prm), pp

    h = x[0]
    h16 = h.astype(bf16)
    tgt = loss_target[0]
    saved = []
    vfirst = None
    for l in range(DEPTH):
        prm, pp = rwkv_params(l)
        proj = _matmul(h16, win_t[l], "nt", f"mm_proj_{l}", (2048, 640, 2048))
        if l == 0:
            cat, vfirst, mck = _rwkv_fwd(False, proj, None, prm, pp, d)
        else:
            cat, mck = _rwkv_fwd(True, proj, vfirst, prm, pp, d)
        cat, sck = _hgrn_fwd(l == 1, proj, lb_logits, g_norm_w[l:l + 1], cat, rwc)
        if l == 0:
            _, arrived = _split_wait(late_gather, _gather_plan, cat, "ag_late_wait")
            wout[0], win_t[1], wout[1] = _gather_forward(arrived, "ag_late_forward")
        y = _matmul(cat, wout[l], "nn", f"mm_out_{l}", (1024, 1024, 2048))
        saved.append((h, h16, proj, prm, pp, mck, sck, cat, y))
        if l < DEPTH - 1:
            h, h16 = _ln_fwd(h, y, ln_w[l:l + 1], ln_b[l:l + 1])
        else:
            top = _ln_loss_bwd(h, y, ln_w[l:l + 1], ln_b[l:l + 1], tgt)
    loss = lax.psum(top[4][0, 0], ("x", "y", "c"))

    grads = {}
    big = {}
    dvfirst = None
    d_lbl = None
    rs_started = {}
    for l in reversed(range(DEPTH)):
        h_l, h16_l, proj, prm, pp, mck, sck, cat, y = saved[l]
        if l == DEPTH - 1:
            dy, dy16, g_ln_w, g_ln_b = top[:4]
        else:
            dy, dy16, g_ln_w, g_ln_b = _ln_bwd(h_l, y, after_start(ln_w[l:l + 1], rs_started[l + 1]), ln_b[l:l + 1], dh_out)
        dcat = _matmul(dy16, wout[l], "nt", f"mm_dcat_{l}", (1024, 1024, 2048))
        big[("w_out", l)] = _matmul(cat, dy16, "tn", f"mm_dwout_{l}", (512, 2048, 2048), out_dtype=bf16)
        if l == 1:
            outs = _rwkv_bwd(True, proj, vfirst, prm, pp, mck, dcat, None)
            dproj_r, dvfirst = outs[0], outs[1]
            dprm, dpp = outs[2:-1], outs[-1]
        else:
            outs = _rwkv_bwd(False, proj, None, prm, pp, mck, dcat, dvfirst)
            dproj_r = outs[0]
            dprm, dpp = outs[1:-1], outs[-1]
        dproj, dlbl_l, dgnw = _hgrn_bwd(l == 1, proj, lb_logits, g_norm_w[l:l + 1], sck, dcat, rwc, dproj_r)
        big[("w_in", l)] = _matmul(dproj, h16_l, "tn", f"mm_dwin_{l}", (640, 2048, 2048), out_dtype=bf16)
        sharded = [dprm[3][:rank_w].T, dprm[4][rank_w:].T]
        if l == 1:
            sharded += [dprm[6][:, :rank_v], dprm[7][:rank_v].T,
                        jnp.zeros((dr, LANES - 2 * rank_v), f32)]
        sharded = jnp.concatenate(sharded, axis=1).astype(bf16)
        rs_started[l] = _reduce_scatter_begin([big[("w_in", l)], big[("w_out", l)], sharded], f"l{l}")
        dy_res = after_start(dy, rs_started[l]) if l == 0 else dy
        dh_out = _matmul(dproj, win_t[l], "nn", f"mm_dh_{l}", (1024, 1024, 1664), add=dy_res, add_scale=ALPHA)
        dpp = jnp.transpose(dpp, (1, 0, 2)).reshape(8, dr)
        grads[l] = dict(shift_mu=dprm[0][0], w_decay0=dprm[1][0], a0=dprm[2][0],
                        k_k=dpp[0], k_a=dpp[1], r_k=dpp[2], ln_x_w=dpp[3], ln_x_b=dpp[4],
                        g_norm_w=dgnw[0], ln_w=g_ln_w[0], ln_b=g_ln_b[0])
        if l == 1:
            grads[l].update(v_mix0=dprm[5][0])
            d_lbl = dlbl_l
    grad_x = dh_out[None]

    def both(name):
        return jnp.stack([grads[0][name], grads[1][name]])

    small = dict(shift_mu=both("shift_mu"), w_decay0=both("w_decay0"), a0=both("a0"), k_k=both("k_k"), k_a=both("k_a"),
                 r_k=both("r_k"), ln_x_w=both("ln_x_w"), ln_x_b=both("ln_x_b"), v_mix0=grads[1]["v_mix0"][None],
                 lb_logits=d_lbl, g_norm_w=both("g_norm_w"), ln_w=both("ln_w"), ln_b=both("ln_b"))
    flat = jnp.concatenate([small[nm].reshape(-1) for nm in _SMALL])
    n_flat = flat.shape[0]
    rows = -(-n_flat // (8 * LANES)) * 8
    flat = jnp.concatenate([flat, jnp.zeros((rows * LANES - n_flat,), f32)]).reshape(rows, LANES)
    total = _sum_slots(_all_gather_rows(flat, "ag_small_grads").reshape(N_DEV, rows, LANES), "sum_small_grads").reshape(-1)
    gsm = {}
    off = 0
    for nm in _SMALL:
        size = small[nm].size
        gsm[nm] = total[off:off + size].reshape(small[nm].shape)
        off += size
    reduced = {1: _reduce_scatter_end(rs_started[1], dh_out, "l1")}
    reduced[0] = _reduce_scatter_end(rs_started[0], total, "l0")
    g_w_in_t = jnp.stack([reduced[l][0] for l in range(DEPTH)])
    gsm["w_in"] = jnp.transpose(g_w_in_t, (0, 2, 1))
    gsm["w_out"] = jnp.stack([reduced[l][1] for l in range(DEPTH)])
    gsm["w_decay_up"] = jnp.stack([reduced[l][2][:, :rank_w].T for l in range(DEPTH)])
    gsm["a_up"] = jnp.stack([reduced[l][2][:, rank_w:rank_w + rank_a].T for l in range(DEPTH)])
    gsm["v_mix_down"] = reduced[1][2][:, LANES:LANES + rank_v][None]
    gsm["v_mix_up"] = reduced[1][2][:, LANES + rank_v:LANES + 2 * rank_v].T[None]

    deltas, new_m, new_v = {}, {}, {}
    swap = lambda a: jnp.transpose(a, (0, 2, 1))
    deltas["w_in"], new_m["w_in"], new_v["w_in"] = (
        swap(a) for a in _adamw(swap(w_in), g_w_in_t, swap(m_w_in), swap(v_w_in), "adamw_w_in"))
    deltas["w_out"], new_m["w_out"], new_v["w_out"] = _adamw(w_out, gsm["w_out"], m_w_out, v_w_out, "adamw_w_out")
    rest = [nm for nm in _NAMES if nm not in ("w_in", "w_out")]
    d_rest, m_rest, v_rest = _adamw_many([weights[nm] for nm in rest], [gsm[nm] for nm in rest],
                                         [mom1[nm] for nm in rest], [mom2[nm] for nm in rest], "adamw_small")
    for i, nm in enumerate(rest):
        deltas[nm], new_m[nm], new_v[nm] = d_rest[i], m_rest[i], v_rest[i]
    return (loss, grad_x, *[gsm[nm] for nm in _NAMES], *[deltas[nm] for nm in _NAMES],
            *[new_m[nm] for nm in _NAMES], *[new_v[nm] for nm in _NAMES])
```

```python
import functools

import jax
import jax.numpy as jnp
from jax import lax
from jax.experimental import pallas as pl
from jax.experimental.pallas import tpu as pltpu

f32 = jnp.float32
bf16 = jnp.bfloat16

N_DEV = 8
CHUNK = 64
LANES = 128
RWKV_HEAD = 64
DEPTH = 2
ALPHA = (2 * DEPTH) ** 0.25
LN_EPS = 1e-5
GN_EPS = 64e-5
RMS_EPS = 1e-5
LB_FLOOR = 1e-30
ADAM_LR, ADAM_B1, ADAM_B2, ADAM_EPS, ADAM_WD, ADAM_STEP = 0.001, 0.9, 0.999, 1e-08, 0.01, 10
MESH = pl.DeviceIdType.MESH


def _iota(shape, d):
    return lax.broadcasted_iota(jnp.int32, shape, d)


_DIMS = {"nn": (((1,), (0,)), ((), ())), "nt": (((1,), (1,)), ((), ())), "tn": (((0,), (0,)), ((), ()))}
_BATCH_DIMS = {"nn": (((2,), (1,)), ((0,), (0,))), "nt": (((2,), (2,)), ((0,), (0,))), "tn": (((1,), (1,)), ((0,), (0,)))}
_K_AXES = {"nn": (-1, -2), "nt": (-1, -1), "tn": (-2, -2)}


def _mxu(a, b, mode):
    return lax.dot_general(a, b, (_BATCH_DIMS if a.ndim == 3 else _DIMS)[mode], preferred_element_type=f32)


def _split(x):
    hi = x.astype(bf16)
    return hi, (x - hi.astype(f32)).astype(bf16)


def _mm2_impl(a, b, mode, passes=3):
    ah, al = _split(a)
    if passes == 3:
        bh, bl = _split(b)
        lhs, rhs = [ah, ah, al], [bh, bl, bh]
    else:
        bh = b.astype(bf16)
        lhs, rhs = [ah, al], [bh, bh]
    ka, kb = _K_AXES[mode]
    k = a.shape[ka]
    if k % (LANES if -1 in (ka, kb) else 16) == 0:
        return _mxu(jnp.concatenate(lhs, axis=ka), jnp.concatenate(rhs, axis=kb), mode)
    out = _mxu(lhs[0], rhs[0], mode)
    for x, y in zip(lhs[1:], rhs[1:]):
        out = out + _mxu(x, y, mode)
    return out


@functools.partial(jax.custom_vjp, nondiff_argnums=(2, 3))
def _mm2(a, b, mode, passes=3):
    return _mm2_impl(a, b, mode, passes)


def _mm2_fwd(a, b, mode, passes):
    return _mm2_impl(a, b, mode, passes), (a, b)


def _mm2_bwd(mode, passes, res, g):
    a, b = res
    if mode == "nn":
        return _mm2_impl(g, b, "nt", passes), _mm2_impl(a, g, "tn", passes)
    if mode == "nt":
        return _mm2_impl(g, b, "nn", passes), _mm2_impl(g, a, "tn", passes)
    return _mm2_impl(b, g, "nt", passes), _mm2_impl(a, g, "nn", passes)


_mm2.defvjp(_mm2_fwd, _mm2_bwd)

TRI_PASSES = 2
APPLY_PASSES = 2


def _const_impl(cm, x, mode):
    hi, lo = _split(x)
    if mode in ("r", "rt"):
        shape = x.shape
        hi, lo = hi.reshape(-1, shape[-1]), lo.reshape(-1, shape[-1])
        dims = "nn" if mode == "r" else "nt"
        out = _mxu(hi, cm, dims) + _mxu(lo, cm, dims)
        return out.reshape(shape[:-1] + (out.shape[-1],))
    if x.ndim == 3:
        cm = jnp.broadcast_to(cm, (x.shape[0],) + cm.shape)
    return _mxu(cm, hi, mode) + _mxu(cm, lo, mode)


@jax.custom_vjp
def _const_left(cm, x):
    return _const_impl(cm, x, "nn")


_const_left.defvjp(lambda cm, x: (_const_impl(cm, x, "nn"), cm),
                   lambda cm, g: (jnp.zeros_like(cm), _const_impl(cm, g, "tn")))


@jax.custom_vjp
def _const_right(x, cm):
    return _const_impl(cm, x, "r")


_const_right.defvjp(lambda x, cm: (_const_impl(cm, x, "r"), cm),
                    lambda cm, g: (_const_impl(cm, g, "rt"), jnp.zeros_like(cm)))


def _tri_inv(a):
    n = a.shape[-1]
    tm = (_iota((n, n), 0) == _iota((n, n), 1)).astype(f32) + a
    ak = a
    for _ in range(5):
        ak = _mm2_impl(ak, ak, "nn", TRI_PASSES)
        tm = tm + _mm2_impl(tm, ak, "nn", TRI_PASSES)
    return tm


@jax.custom_vjp
def _tri_solve(tm, a, x):
    del a
    return _mm2_impl(tm, x, "nn")


def _tri_solve_fwd(tm, a, x):
    u = _mm2_impl(tm, x, "nn")
    return u, (tm, u)


def _tri_solve_bwd(res, du):
    tm, u = res
    dx = _mm2_impl(tm, du, "tn")
    return jnp.zeros_like(tm), _mm2_impl(dx, u, "nt"), dx


_tri_solve.defvjp(_tri_solve_fwd, _tri_solve_bwd)


def _col_of_row(row_vec):
    n = row_vec.shape[-1]
    eye = _iota((n, n), 0) == _iota((n, n), 1)
    return jnp.sum(jnp.where(eye, jnp.broadcast_to(row_vec, row_vec.shape[:-2] + (n, n)), 0.0), axis=-1, keepdims=True)


def _softplus(x):
    return jnp.maximum(x, 0.0) + jnp.log1p(jnp.exp(-jnp.abs(x)))


def _log_sigmoid(x):
    return -_softplus(-x)


def _logaddexp(a, b):
    return jnp.maximum(a, b) + jnp.log1p(jnp.exp(-jnp.abs(a - b)))


def _silu(x):
    return x * jax.nn.sigmoid(x)


def _tril(c, strict):
    r, s = _iota((c, c), 0), _iota((c, c), 1)
    return (r > s) if strict else (r >= s)


def _last_row(a):
    c = a.shape[-2]
    return jnp.sum(jnp.where(_iota(a.shape, a.ndim - 2) == c - 1, a, 0.0), axis=-2, keepdims=True)


def _rwkv_pre(layer1, prm, y, prev, vf):
    c = y.shape[0]
    if layer1:
        mu, w0, a0, wup, aup, v0, vdown, vup = prm
    else:
        mu, w0, a0, wup, aup = prm
    dr = w0.shape[1]
    shift = (_iota((c, c), 0) == _iota((c, c), 1) + 1).astype(bf16)
    y_prev = _const_left(shift, y) + jnp.where(_iota((c, 1), 0) == 0, prev, 0.0)
    rw = y + mu * (y_prev - y)
    r, k, v, z = (rw[:, i * dr:(i + 1) * dr] for i in range(4))
    wdad = rw[:, 4 * dr:4 * dr + LANES]
    w_raw = w0 + _mm2(jnp.tanh(wdad), wup, "nn")
    lw = -jnp.exp(-_softplus(-w_raw) - 0.5)
    asig = jax.nn.sigmoid(a0 + _mm2(wdad, aup, "nn"))
    if layer1:
        v = v + (vf - v) * jax.nn.sigmoid(v0 + _mm2(_mm2(v, vdown, "nn"), vup, "nn"))
    return r, k, v, z, lw, asig


def _rwkv_pair(pp, m0, xs, tm=None):
    kkw, kaw, rkw, gnw, gnb = pp
    r, k, v, z, lw, asig = xs
    c = r.shape[-2]
    n2 = 2 * c
    lane = _iota((1, LANES), 1)
    mh0, mh1 = (lane < RWKV_HEAD).astype(f32), (lane >= RWKV_HEAD).astype(f32)
    same_head = _iota((LANES, LANES), 0) // RWKV_HEAD == _iota((LANES, LANES), 1) // RWKV_HEAD
    g = same_head.astype(bf16)

    def seg(x):
        return _const_right(x, g)

    def stack(x):
        return jnp.concatenate([x * mh0, x * mh1], axis=-2)

    kk = k * kkw
    kk = kk / jnp.maximum(jnp.sqrt(seg(kk * kk)), 1e-12)
    k2 = k * (1.0 + (asig - 1.0) * kaw)
    a = -kk
    b = kk * asig
    cum = _const_left(_tril(c, False).astype(bf16), lw)
    at = stack(a * jnp.exp(cum - lw))
    rt = stack(r * jnp.exp(cum))
    en = jnp.exp(-cum)
    sc = _mm2(jnp.concatenate([at, rt], axis=-2), jnp.concatenate([stack(b * en), stack(k2 * en)], axis=-2), "nt")
    row, col = _iota((n2, n2), 0), _iota((n2, n2), 1)
    same = row // c == col // c
    strict = same & (row % c > col % c)
    incl = same & (row % c >= col % c)
    aab = jnp.where(strict, sc[..., :n2, :n2], 0.0)
    aak = jnp.where(strict, sc[..., :n2, n2:], 0.0)
    arb = jnp.where(incl, sc[..., n2:, :n2], 0.0)
    ark = jnp.where(incl, sc[..., n2:, n2:], 0.0)
    vv = jnp.concatenate([v, v], axis=-2)
    mask_st = jnp.concatenate([jnp.broadcast_to(mh0, (c, LANES)), jnp.broadcast_to(mh1, (c, LANES))], axis=0)
    x_st = _mm2(jnp.concatenate([at, aak], axis=-1), jnp.concatenate([m0, vv], axis=-2), "nn", APPLY_PASSES)
    if tm is None:
        tm = _tri_inv(lax.stop_gradient(aab))
    u_st = _tri_solve(tm, aab, x_st) * mask_st
    o_st = _mm2(jnp.concatenate([rt, arb, ark], axis=-1), jnp.concatenate([m0, u_st, vv], axis=-2), "nn", APPLY_PASSES) * mask_st
    u = u_st[..., :c, :] + u_st[..., c:, :]
    o = o_st[..., :c, :] + o_st[..., c:, :]
    cum_last = _last_row(cum)
    dec_end = jnp.exp(cum_last - cum)
    m_new = _col_of_row(jnp.exp(cum_last)) * m0 + _mm2(
        jnp.concatenate([b * dec_end, k2 * dec_end], axis=-2), jnp.concatenate([u, v], axis=-2), "tn", APPLY_PASSES) * same_head.astype(f32)
    mean = seg(o) * (1.0 / RWKV_HEAD)
    d = o - mean
    var = seg(d * d) * (1.0 / RWKV_HEAD)
    on = d * lax.rsqrt(var + GN_EPS) * gnw + gnb
    bonus = seg(r * k2 * rkw) * v
    return (on + bonus) * _silu(z), m_new, tm


def _split_lanes(a, n):
    return [a[:, i * LANES:(i + 1) * LANES] for i in range(n)]


def _rwkv_step(layer1, prm, y, prev, vf, pp, m0, tm=None):
    xs = _rwkv_pre(layer1, prm, y, prev, vf)
    n_pair = m0.shape[0]
    og, m_new, tm = _rwkv_pair(pp, m0, tuple(jnp.concatenate([p[None] for p in _split_lanes(a, n_pair)], axis=0) for a in xs), tm)
    return og, m_new, xs[2], tm


def _group(n):
    return n


def _rwkv_specs(layer1, t, dr, rwc, n_pair, rev):
    nc = t // CHUNK
    grp = _group(n_pair)

    def cidx(c):
        return (nc - 1 - c) if rev else c

    full = lambda shape: pl.BlockSpec(shape, lambda c, p: tuple(0 for _ in shape))
    specs = [
        pl.BlockSpec((CHUNK, rwc), lambda c, p: (cidx(c), 0)),
        pl.BlockSpec((8, rwc), lambda c, p: (jnp.maximum(cidx(c) * (CHUNK // 8) - 1, 0), 0)),
    ]
    if layer1:
        specs.append(pl.BlockSpec((CHUNK, dr), lambda c, p: (cidx(c), 0)))
    prm_shapes = [(1, rwc), (1, dr), (1, dr), (LANES, dr), (LANES, dr)]
    if layer1:
        prm_shapes += [(1, dr), (dr, LANES), (LANES, dr)]
    specs += [full(s) for s in prm_shapes]
    specs.append(pl.BlockSpec((grp, 8, LANES), lambda c, p: (p, 0, 0)))
    return specs, prm_shapes, cidx, full


def _rwkv_fwd(layer1, proj, vf, prm, pp, cat_width):
    t = proj.shape[0]
    dr = prm[1].shape[1]
    rwc = prm[0].shape[1]
    n_pair = dr // LANES
    nc = t // CHUNK
    n_prm = len(prm)
    specs, _, _, _ = _rwkv_specs(layer1, t, dr, rwc, n_pair, False)

    def body(*refs):
        y_ref, prev_ref = refs[0], refs[1]
        i = 2
        vf_ref = None
        if layer1:
            vf_ref = refs[i]
            i += 1
        prm_refs = refs[i:i + n_prm]
        i += n_prm
        pp_ref = refs[i]
        i += 1
        cat_ref = refs[i]
        i += 1
        vout_ref = None
        if not layer1:
            vout_ref = refs[i]
            i += 1
        mck_ref, m_s = refs[i], refs[i + 1]
        c = pl.program_id(0)

        @pl.when(c == 0)
        def _():
            m_s[...] = jnp.zeros_like(m_s)

        prev = prev_ref[pl.ds(7, 1), :] * (c != 0).astype(f32)
        m0 = m_s[...]
        ppv = tuple(pp_ref[:, pl.ds(q, 1), :] for q in range(5))
        og, m_new, v, tm = _rwkv_step(layer1, tuple(r[...] for r in prm_refs), y_ref[...], prev,
                                      vf_ref[...] if layer1 else None, ppv, m0)
        mck_ref[0, :n_pair] = m0
        mck_ref[0, n_pair:] = tm
        if not layer1:
            vout_ref[...] = v
        for j in range(n_pair):
            cat_ref[:, j * LANES:(j + 1) * LANES] = og[j]
        m_s[...] = m_new

    grp = _group(n_pair)
    assert grp == n_pair
    out_shape = [jax.ShapeDtypeStruct((t, cat_width), f32)]
    out_specs = [pl.BlockSpec((CHUNK, grp * LANES), lambda c, p: (c, p))]
    if not layer1:
        out_shape.append(jax.ShapeDtypeStruct((t, dr), f32))
        out_specs.append(pl.BlockSpec((CHUNK, dr), lambda c, p: (c, 0)))
    out_shape.append(jax.ShapeDtypeStruct((nc, 2 * n_pair, LANES, LANES), f32))
    out_specs.append(pl.BlockSpec((1, 2 * grp, LANES, LANES), lambda c, p: (c, p, 0, 0)))
    args = [proj, proj] + ([vf] if layer1 else []) + list(prm) + [pp]
    return pl.pallas_call(
        body, grid=(nc, 1), in_specs=specs, out_specs=out_specs, out_shape=out_shape,
        scratch_shapes=[pltpu.VMEM((n_pair, LANES, LANES), f32)],
        compiler_params=pltpu.CompilerParams(dimension_semantics=("arbitrary", "arbitrary")),
        name=f"rwkv_fwd_l{int(layer1)}",
    )(*args)


def _rwkv_bwd(layer1, proj, vf, prm, pp, mck, dcat, dvout):
    t = proj.shape[0]
    dr = prm[1].shape[1]
    rwc = prm[0].shape[1]
    n_pair = dr // LANES
    nc = t // CHUNK
    n_prm = len(prm)
    specs, prm_shapes, cidx, full = _rwkv_specs(layer1, t, dr, rwc, n_pair, True)
    grp = _group(n_pair)
    assert grp == n_pair
    specs.append(pl.BlockSpec((1, 2 * grp, LANES, LANES), lambda c, p: (cidx(c), p, 0, 0)))
    specs.append(pl.BlockSpec((CHUNK, grp * LANES), lambda c, p: (cidx(c), p)))
    if not layer1:
        specs.append(pl.BlockSpec((CHUNK, dr), lambda c, p: (cidx(c), 0)))

    def body(*refs):
        y_ref, prev_ref = refs[0], refs[1]
        i = 2
        vf_ref = None
        if layer1:
            vf_ref = refs[i]
            i += 1
        prm_refs = refs[i:i + n_prm]
        i += n_prm
        pp_ref, mck_ref, dog_ref = refs[i], refs[i + 1], refs[i + 2]
        i += 3
        dvout_ref = None
        if not layer1:
            dvout_ref = refs[i]
            i += 1
        dy_ref = refs[i]
        i += 1
        dvf_ref = None
        if layer1:
            dvf_ref = refs[i]
            i += 1
        dprm_refs = refs[i:i + n_prm]
        i += n_prm
        dpp_ref = refs[i]
        dm_s, dprev_s = refs[i + 1:i + 3]
        c = pl.program_id(0)
        cr = nc - 1 - c

        @pl.when(c == 0)
        def _():
            dm_s[...] = jnp.zeros_like(dm_s)
            dprev_s[...] = jnp.zeros_like(dprev_s)
            dpp_ref[...] = jnp.zeros_like(dpp_ref)
            for r in dprm_refs:
                r[...] = jnp.zeros_like(r)

        prev = prev_ref[pl.ds(7, 1), :] * (cr != 0).astype(f32)
        prm_v = tuple(r[...] for r in prm_refs)
        ppv = tuple(pp_ref[:, pl.ds(q, 1), :] for q in range(5))
        dog = jnp.stack([dog_ref[:, j * LANES:(j + 1) * LANES] for j in range(n_pair)], axis=0)
        m0, tm = mck_ref[0, :n_pair], mck_ref[0, n_pair:]
        no_tm = jnp.zeros_like(tm)
        if layer1:
            _, vjp = jax.vjp(lambda a, b, d, e, g, h: _rwkv_step(True, a, b, d, e, g, h, tm),
                             prm_v, y_ref[...], prev, vf_ref[...], ppv, m0)
            dprm, dy, dprev, dvf, dppv, dm0 = vjp((dog, dm_s[...], jnp.zeros((CHUNK, dr), f32), no_tm))
            dvf_ref[...] = dvf
        else:
            _, vjp = jax.vjp(lambda a, b, d, e, g: _rwkv_step(False, a, b, d, None, e, g, tm), prm_v, y_ref[...], prev, ppv, m0)
            dprm, dy, dprev, dppv, dm0 = vjp((dog, dm_s[...], dvout_ref[...], no_tm))
        dm_s[...] = dm0
        for q in range(5):
            dpp_ref[:, pl.ds(q, 1), :] += dppv[q]
        dy_ref[...] = (dy + jnp.where(_iota((CHUNK, 1), 0) == CHUNK - 1, dprev_s[...], 0.0)).astype(bf16)
        dprev_s[...] = dprev
        for r, gval in zip(dprm_refs, dprm):
            r[...] += gval

    out_shape = [jax.ShapeDtypeStruct((t, proj.shape[1]), bf16)]
    out_specs = [pl.BlockSpec((CHUNK, rwc), lambda c, p: (cidx(c), 0))]
    if layer1:
        out_shape.append(jax.ShapeDtypeStruct((t, dr), f32))
        out_specs.append(pl.BlockSpec((CHUNK, dr), lambda c, p: (cidx(c), 0)))
    out_shape += [jax.ShapeDtypeStruct(s, f32) for s in prm_shapes]
    out_specs += [full(s) for s in prm_shapes]
    out_shape.append(jax.ShapeDtypeStruct((n_pair, 8, LANES), f32))
    out_specs.append(full((n_pair, 8, LANES)))
    args = [proj, proj] + ([vf] if layer1 else []) + list(prm) + [pp, mck, dcat] + ([] if layer1 else [dvout])
    return pl.pallas_call(
        body, grid=(nc, 1), in_specs=specs, out_specs=out_specs, out_shape=out_shape,
        scratch_shapes=[pltpu.VMEM((n_pair, LANES, LANES), f32), pltpu.VMEM((1, rwc), f32)],
        compiler_params=pltpu.CompilerParams(dimension_semantics=("arbitrary", "arbitrary")),
        name=f"rwkv_bwd_l{int(layer1)}",
    )(*args)


def _hgrn_chunk(layer1, lbl, gnw, s0, q_raw, f_raw, i_in, z):
    c = q_raw.shape[-2]
    q = _silu(q_raw)
    ls = _log_sigmoid(f_raw)
    if layer1:
        l0, l1 = lbl[..., 0:1, :], lbl[..., 1:2, :]
        mx = jnp.maximum(l0, l1)
        e0, e1 = jnp.exp(l0 - mx), jnp.exp(l1 - mx)
        sm0, sm1 = e0 / (e0 + e1), e1 / (e0 + e1)
        lb = (sm0 + sm1) - sm0
        log_f = _logaddexp(jnp.log(jnp.maximum(lb, LB_FLOOR)), jnp.log1p(-lb) + ls)
        k = (1.0 - lb) * jax.nn.sigmoid(-f_raw)
    else:
        log_f = _logaddexp(jnp.full_like(ls, jnp.log(jnp.float32(LB_FLOOR))), ls)
        k = jax.nn.sigmoid(-f_raw)
    row, col = _iota((c, c), 0), _iota((c, c), 1)
    trow = _iota((c, 1), 0)
    halves = []
    half = c // 2
    while half >= 1:
        halves.append(half)
        half //= 2
    cmat = jnp.concatenate([(col <= row).astype(f32)]
                           + [(col <= (row // (2 * hf)) * (2 * hf) + hf - 1).astype(f32) for hf in halves], axis=0)
    ball = _const_left(cmat.astype(bf16), log_f)
    b = ball[..., :c, :]
    att = None
    for lvl, hf in enumerate(halves):
        blk = 2 * hf
        bref = ball[..., (lvl + 1) * c:(lvl + 2) * c, :]
        upper = (trow % blk) >= hf
        qh = q * jnp.exp(jnp.where(upper, b - bref, 0.0)) * upper.astype(f32)
        kh = k * jnp.exp(jnp.where(upper, 0.0, bref - b)) * (1.0 - upper.astype(f32))
        term = jnp.where(row // blk == col // blk, _mm2(qh, kh, "nt", APPLY_PASSES), 0.0)
        att = term if att is None else att + term
    lhs = jnp.concatenate([q * jnp.exp(b), att, jnp.zeros(att.shape[:-1] + (LANES - c,), f32)], axis=-1)
    rhs = jnp.concatenate([s0, i_in, jnp.zeros(i_in.shape[:-2] + (LANES - c, i_in.shape[-1]), f32)], axis=-2)
    o = _mm2(lhs, rhs, "nn", APPLY_PASSES) + jnp.sum(q * k, axis=-1, keepdims=True) * i_in
    b_last = _last_row(b)
    s_new = _col_of_row(jnp.exp(b_last)) * s0 + _mm2(k * jnp.exp(b_last - b), i_in, "tn", APPLY_PASSES)
    o = o * lax.rsqrt(jnp.mean(o * o, axis=-1, keepdims=True) + RMS_EPS)
    return o * gnw * _silu(z), s_new


def _hgrn_in_specs(t, dh, col0, rev):
    nc = t // CHUNK
    nh = dh // LANES

    def cidx(c):
        return (nc - 1 - c) if rev else c

    grp = _group(nh)
    specs = [pl.BlockSpec((CHUNK, LANES), functools.partial(lambda g, j, h, c: (cidx(c), col0 + g * nh + h * grp + j), g, j))
             for j in range(grp) for g in range(4)]
    specs.append(pl.BlockSpec((2, grp * LANES), lambda h, c: (0, h)))
    specs.append(pl.BlockSpec((1, grp * LANES), lambda h, c: (0, h)))
    return specs, cidx, grp


def _hgrn_fwd(layer1, proj, lbl, gnw, cat, rwc):
    t, d = cat.shape
    dh = gnw.shape[1]
    nh = dh // LANES
    nc = t // CHUNK
    col0 = rwc // LANES
    specs, _, grp = _hgrn_in_specs(t, dh, col0, False)
    specs.append(pl.BlockSpec(memory_space=pl.ANY))
    assert (d - dh) % (grp * LANES) == 0
    cat_col0 = (d - dh) // (grp * LANES)

    def body(*refs):
        x_refs = refs[:4 * grp]
        lbl_ref, gnw_ref, _, cat_ref, sck_ref, s_s = refs[4 * grp:]
        c = pl.program_id(1)

        @pl.when(c == 0)
        def _():
            s_s[...] = jnp.zeros_like(s_s)

        lanes = [slice(j * LANES, (j + 1) * LANES) for j in range(grp)]
        s0 = s_s[...]
        sck_ref[:, 0] = s0
        out, s_new = _hgrn_chunk(layer1, jnp.stack([lbl_ref[:, ln] for ln in lanes]), jnp.stack([gnw_ref[:, ln] for ln in lanes]),
                                 s0, *(jnp.stack([x_refs[4 * j + g][...] for j in range(grp)]) for g in range(4)))
        for j in range(grp):
            cat_ref[:, lanes[j]] = out[j]
        s_s[...] = s_new

    return pl.pallas_call(
        body, grid=(nh // grp, nc), in_specs=specs,
        out_specs=[pl.BlockSpec((CHUNK, grp * LANES), lambda h, c: (c, cat_col0 + h)),
                   pl.BlockSpec((grp, 1, LANES, LANES), lambda h, c: (h, c, 0, 0))],
        out_shape=[jax.ShapeDtypeStruct((t, d), f32), jax.ShapeDtypeStruct((nh, nc, LANES, LANES), f32)],
        scratch_shapes=[pltpu.VMEM((grp, LANES, LANES), f32)],
        input_output_aliases={4 * grp + 2: 0},
        compiler_params=pltpu.CompilerParams(dimension_semantics=("arbitrary", "arbitrary")),
        name=f"hgrn_fwd_l{int(layer1)}",
    )(*([proj] * (4 * grp)), lbl, gnw, cat)


def _hgrn_bwd(layer1, proj, lbl, gnw, sck, dcat, rwc, dproj):
    t, d = dcat.shape
    dh = gnw.shape[1]
    nh = dh // LANES
    nc = t // CHUNK
    col0 = rwc // LANES
    specs, cidx, grp = _hgrn_in_specs(t, dh, col0, True)
    assert grp == nh and (d - dh) % (grp * LANES) == 0
    cat_col0 = (d - dh) // (grp * LANES)
    specs.append(pl.BlockSpec((grp, 1, LANES, LANES), lambda h, c: (h, cidx(c), 0, 0)))
    specs.append(pl.BlockSpec((CHUNK, grp * LANES), lambda h, c: (cidx(c), cat_col0 + h)))
    specs.append(pl.BlockSpec(memory_space=pl.ANY))

    def body(*refs):
        x_refs = refs[:4 * grp]
        lbl_ref, gnw_ref, sck_ref, do_ref, _, dp_hbm, dlbl_ref, dgnw_ref, ds_s, stage, sems = refs[4 * grp:]
        c = pl.program_id(1)
        slot = c % 2

        def put(s, g, chunk):
            return pltpu.make_async_copy(stage.at[s, g], dp_hbm.at[pl.ds(chunk * CHUNK, CHUNK), pl.ds(rwc + g * dh, dh)],
                                         sems.at[s, g])

        @pl.when(c == 0)
        def _():
            ds_s[...] = jnp.zeros_like(ds_s)
            dlbl_ref[...] = jnp.zeros_like(dlbl_ref)
            dgnw_ref[...] = jnp.zeros_like(dgnw_ref)

        @pl.when(c >= 2)
        def _():
            for g in range(4):
                put(slot, g, 0).wait()

        lanes = [slice(j * LANES, (j + 1) * LANES) for j in range(grp)]
        _, vjp = jax.vjp(functools.partial(_hgrn_chunk, layer1),
                         jnp.stack([lbl_ref[:, ln] for ln in lanes]), jnp.stack([gnw_ref[:, ln] for ln in lanes]), sck_ref[:, 0],
                         *(jnp.stack([x_refs[4 * j + g][...] for j in range(grp)]) for g in range(4)))
        dlbl, dgnw, ds0, dq, df, di, dz = vjp((jnp.stack([do_ref[:, ln] for ln in lanes]), ds_s[...]))
        ds_s[...] = ds0
        for j in range(grp):
            dlbl_ref[:, lanes[j]] += dlbl[j]
            dgnw_ref[:, lanes[j]] += dgnw[j]
            for g, val in enumerate((dq, df, di, dz)):
                stage[slot, g, :, lanes[j]] = val[j].astype(bf16)
        for g in range(4):
            put(slot, g, nc - 1 - c).start()

        @pl.when(c == nc - 1)
        def _():
            for g in range(4):
                put(slot, g, 0).wait()
                if nc >= 2:
                    put(1 - slot, g, 0).wait()

    return pl.pallas_call(
        body, grid=(1, nc), in_specs=specs,
        out_specs=[pl.BlockSpec(memory_space=pl.ANY),
                   pl.BlockSpec((2, grp * LANES), lambda h, c: (0, h)),
                   pl.BlockSpec((1, grp * LANES), lambda h, c: (0, h))],
        out_shape=[jax.ShapeDtypeStruct(dproj.shape, dproj.dtype), jax.ShapeDtypeStruct((2, dh), f32),
                   jax.ShapeDtypeStruct((1, dh), f32)],
        scratch_shapes=[pltpu.VMEM((grp, LANES, LANES), f32), pltpu.VMEM((2, 4, CHUNK, dh), bf16),
                        pltpu.SemaphoreType.DMA((2, 4))],
        input_output_aliases={4 * grp + 4: 0},
        compiler_params=pltpu.CompilerParams(dimension_semantics=("arbitrary", "arbitrary")),
        name=f"hgrn_bwd_l{int(layer1)}",
    )(*([proj] * (4 * grp)), lbl, gnw, sck, dcat, dproj)


def _ln(h, y, w, b):
    u = ALPHA * h + y
    mu = jnp.mean(u, axis=-1, keepdims=True)
    var = jnp.mean(jnp.square(u - mu), axis=-1, keepdims=True)
    return (u - mu) * lax.rsqrt(var + LN_EPS) * w + b


def _row_tile(t):
    return 256 if t % 256 == 0 else t


def _ln_fwd(h, y, w, b):
    t, d = h.shape
    tr = _row_tile(t)

    def body(h_ref, y_ref, w_ref, b_ref, o_ref, o16_ref):
        out = _ln(h_ref[...], y_ref[...], w_ref[...], b_ref[...])
        o_ref[...] = out
        o16_ref[...] = out.astype(bf16)

    row = pl.BlockSpec((tr, d), lambda i: (i, 0))
    vec = pl.BlockSpec((1, d), lambda i: (0, 0))
    return pl.pallas_call(body, grid=(t // tr,), in_specs=[row, row, vec, vec], out_specs=[row, row],
                          out_shape=[jax.ShapeDtypeStruct((t, d), f32), jax.ShapeDtypeStruct((t, d), bf16)],
                          name="ln_fwd")(h, y, w, b)


def _ln_loss_bwd(h, y, w, b, tgt):
    t, d = h.shape
    tr = _row_tile(t)

    def body(h_ref, y_ref, w_ref, b_ref, t_ref, dy_ref, dy16_ref, dw_ref, db_ref, loss_ref):
        @pl.when(pl.program_id(0) == 0)
        def _():
            dw_ref[...] = jnp.zeros_like(dw_ref)
            db_ref[...] = jnp.zeros_like(db_ref)
            loss_ref[...] = jnp.zeros_like(loss_ref)

        out, vjp = jax.vjp(lambda yy, ww, bb: _ln(h_ref[...], yy, ww, bb), y_ref[...], w_ref[...], b_ref[...])
        err = out - t_ref[...]
        loss_ref[...] += 0.5 * jnp.sum(jnp.mean(jnp.square(err), axis=-1, keepdims=True), axis=0, keepdims=True)
        dy, dw, db = vjp(err * (1.0 / d))
        dy_ref[...] = dy
        dy16_ref[...] = dy.astype(bf16)
        dw_ref[...] += dw
        db_ref[...] += db

    row = pl.BlockSpec((tr, d), lambda i: (i, 0))
    vec = pl.BlockSpec((1, d), lambda i: (0, 0))
    return pl.pallas_call(
        body, grid=(t // tr,), in_specs=[row, row, vec, vec, row],
        out_specs=[row, row, vec, vec, pl.BlockSpec((1, LANES), lambda i: (0, 0))],
        out_shape=[jax.ShapeDtypeStruct((t, d), f32), jax.ShapeDtypeStruct((t, d), bf16), jax.ShapeDtypeStruct((1, d), f32),
                   jax.ShapeDtypeStruct((1, d), f32), jax.ShapeDtypeStruct((1, LANES), f32)],
        compiler_params=pltpu.CompilerParams(dimension_semantics=("arbitrary",)), name="ln_loss_bwd")(h, y, w, b, tgt)


def _ln_bwd(h, y, w, b, dout):
    t, d = h.shape
    tr = _row_tile(t)

    def body(h_ref, y_ref, w_ref, b_ref, do_ref, dy_ref, dy16_ref, dw_ref, db_ref):
        @pl.when(pl.program_id(0) == 0)
        def _():
            dw_ref[...] = jnp.zeros_like(dw_ref)
            db_ref[...] = jnp.zeros_like(db_ref)

        _, vjp = jax.vjp(lambda yy, ww, bb: _ln(h_ref[...], yy, ww, bb), y_ref[...], w_ref[...], b_ref[...])
        dy, dw, db = vjp(do_ref[...])
        dy_ref[...] = dy
        dy16_ref[...] = dy.astype(bf16)
        dw_ref[...] += dw
        db_ref[...] += db

    row = pl.BlockSpec((tr, d), lambda i: (i, 0))
    vec = pl.BlockSpec((1, d), lambda i: (0, 0))
    return pl.pallas_call(
        body, grid=(t // tr,), in_specs=[row, row, vec, vec, row], out_specs=[row, row, vec, vec],
        out_shape=[jax.ShapeDtypeStruct((t, d), f32), jax.ShapeDtypeStruct((t, d), bf16),
                   jax.ShapeDtypeStruct((1, d), f32), jax.ShapeDtypeStruct((1, d), f32)],
        compiler_params=pltpu.CompilerParams(dimension_semantics=("arbitrary",)), name="ln_bwd")(h, y, w, b, dout)


def _pick(n, prefs):
    for p in prefs:
        if n % p == 0:
            return p
    return n


def _tile(n, want):
    if n <= want:
        return n
    for cand in range(want - want % LANES, 0, -LANES):
        if n % cand == 0:
            return cand
    return n


def _matmul(a, b, mode, name, tiles, add=None, add_scale=1.0, out_dtype=f32):
    if mode == "nn":
        (m, k), n = a.shape, b.shape[1]
    elif mode == "nt":
        (m, k), n = a.shape, b.shape[0]
    else:
        (k, m), n = a.shape, b.shape[1]
    tm, tn, tk = _tile(m, tiles[0]), _tile(n, tiles[1]), _tile(k, tiles[2])
    nk = k // tk
    cache_a = nk == 1 and a.dtype != bf16 and n // tn > 1

    def body(*refs):
        a_ref, b_ref = refs[0], refs[1]
        add_ref = refs[2] if add is not None else None
        n_in = 3 if add is not None else 2
        o_ref = refs[n_in]
        scratch = refs[n_in + 1:]

        def finish(res):
            if add is not None:
                res = res + add_scale * add_ref[...]
            o_ref[...] = res.astype(out_dtype)

        if cache_a:
            a_bf = scratch[0]

            @pl.when(pl.program_id(1) == 0)
            def _():
                a_bf[...] = a_ref[...].astype(bf16)

            a_val = a_bf[...]
        else:
            a_val = a_ref[...].astype(bf16)
        prod = lax.dot_general(a_val, b_ref[...].astype(bf16), _DIMS[mode], preferred_element_type=f32)
        if nk == 1:
            finish(prod)
        else:
            acc = scratch[-1]
            kk = pl.program_id(2)

            @pl.when(kk == 0)
            def _():
                acc[...] = prod

            @pl.when(kk != 0)
            def _():
                acc[...] += prod

            @pl.when(kk == nk - 1)
            def _():
                finish(acc[...])

    a_shape = (tk, tm) if mode == "tn" else (tm, tk)
    a_spec = pl.BlockSpec(a_shape, (lambda i, j, kk: (kk, i)) if mode == "tn" else (lambda i, j, kk: (i, kk)))
    b_spec = pl.BlockSpec((tn, tk), lambda i, j, kk: (j, kk)) if mode == "nt" else pl.BlockSpec((tk, tn), lambda i, j, kk: (kk, j))
    o_spec = pl.BlockSpec((tm, tn), lambda i, j, kk: (i, j))
    in_specs = [a_spec, b_spec] + ([o_spec] if add is not None else [])
    args = [a, b] + ([add] if add is not None else [])
    scratch_shapes = ([pltpu.VMEM(a_shape, bf16)] if cache_a else []) + ([pltpu.VMEM((tm, tn), f32)] if nk > 1 else [])
    return pl.pallas_call(
        body, grid=(m // tm, n // tn, nk), in_specs=in_specs, out_specs=o_spec,
        out_shape=jax.ShapeDtypeStruct((m, n), out_dtype), scratch_shapes=scratch_shapes,
        compiler_params=pltpu.CompilerParams(dimension_semantics=("parallel", "arbitrary", "arbitrary")),
        name=name,
    )(*args)


def _position():
    return lax.axis_index("x"), lax.axis_index("y"), lax.axis_index("c")


def _flip(pos, k):
    x, y, c = pos
    return (1 - x if k & 4 else x, 1 - y if k & 2 else y, 1 - c if k & 1 else c)


def _index(pos):
    return 4 * pos[0] + 2 * pos[1] + pos[2]


def _all_gather_rows(x, name):
    m_per, n = x.shape

    def body(x_ref, out_ref, send_sems, recv_sems, local_sem):
        me = _position()
        sibling = _flip(me, 1)
        chips = (2, 4, 6)

        def rows(pos):
            return out_ref.at[pl.ds(_index(pos) * m_per, m_per), :]

        def copy(sem, block, to, src=None):
            return pltpu.make_async_remote_copy(
                src_ref=rows(block) if src is None else src, dst_ref=rows(block),
                send_sem=send_sems.at[sem], recv_sem=recv_sems.at[sem], device_id=to, device_id_type=MESH)

        mine = pltpu.make_async_copy(x_ref, rows(me), local_sem)
        mine.start()
        first = [copy(0, me, sibling, src=x_ref)]
        first += [copy(1 + j, me, _flip(me, k), src=x_ref) for j, k in enumerate(chips)]
        for cp in first:
            cp.start()
        passed = [copy(4 + j, _flip(me, k), sibling) for j, k in enumerate(chips)]
        for j, k in enumerate(chips):
            copy(1 + j, _flip(me, k), me).wait_recv()
            passed[j].start()
        copy(0, sibling, me).wait_recv()
        for j, k in enumerate(chips):
            copy(4 + j, _flip(sibling, k), me).wait_recv()
        for cp in first + passed:
            cp.wait_send()
        mine.wait()

    return pl.pallas_call(
        body, out_shape=jax.ShapeDtypeStruct((N_DEV * m_per, n), x.dtype),
        in_specs=[pl.BlockSpec(memory_space=pl.ANY)], out_specs=pl.BlockSpec(memory_space=pl.ANY),
        scratch_shapes=[pltpu.SemaphoreType.DMA((7,)), pltpu.SemaphoreType.DMA((7,)), pltpu.SemaphoreType.DMA(())],
        name=name,
    )(x)


def _split_start(srcs, lands, plan, n_copies, name, after=()):
    n_arr = len(srcs)
    n_after = len(after)
    hbm = pl.BlockSpec(memory_space=pltpu.HBM)
    sem = pl.BlockSpec(memory_space=pltpu.SEMAPHORE)

    def body(*refs):
        src_refs, land_refs = refs[:n_arr], refs[n_arr:2 * n_arr]
        outs_at = 2 * n_arr + n_after
        send_sems, recv_sems = refs[outs_at:outs_at + n_arr], refs[outs_at + n_arr:outs_at + 2 * n_arr]
        token = refs[-1]
        me = _position()
        for i in range(n_arr):
            for j, (src, dst, peer, _) in enumerate(plan(i, src_refs[i], land_refs[i], me)):
                pltpu.make_async_remote_copy(src_ref=src, dst_ref=dst, send_sem=send_sems[i].at[j], recv_sem=recv_sems[i].at[j],
                                             device_id=peer, device_id_type=MESH).start()
        token[...] = jnp.zeros_like(token)

    outs = pl.pallas_call(
        body, name=name,
        out_shape=([pltpu.SemaphoreType.DMA((n_copies,))] * (2 * n_arr)
                   + [pltpu.HBM(a.shape, a.dtype) for a in list(srcs) + list(lands)]
                   + [jax.ShapeDtypeStruct((8, LANES), f32)]),
        in_specs=[hbm] * (2 * n_arr) + [pl.BlockSpec(memory_space=pl.ANY)] * n_after,
        out_specs=[sem] * (2 * n_arr) + [hbm] * (2 * n_arr) + [pl.BlockSpec(memory_space=pltpu.VMEM)],
        input_output_aliases={i: 2 * n_arr + i for i in range(2 * n_arr)},
        compiler_params=pltpu.CompilerParams(has_side_effects=pltpu.SideEffectType.DATAFLOW_SIDE_EFFECTING),
    )(*[pltpu.with_memory_space_constraint(a, pltpu.HBM) for a in list(srcs) + list(lands)], *after)
    return (outs[:n_arr], outs[n_arr:2 * n_arr], outs[2 * n_arr:3 * n_arr], outs[3 * n_arr:4 * n_arr], outs[-1])


def _split_wait(started, plan, after, name):
    send_sems, recv_sems, srcs, lands, _ = started
    n_arr = len(srcs)
    hbm = pl.BlockSpec(memory_space=pltpu.HBM)
    sem = pl.BlockSpec(memory_space=pltpu.SEMAPHORE)

    def body(*refs):
        src_refs, land_refs = refs[:n_arr], refs[n_arr:2 * n_arr]
        s_sems, r_sems = refs[2 * n_arr:3 * n_arr], refs[3 * n_arr:4 * n_arr]
        me = _position()
        for i in range(n_arr):
            for j, (src, _, peer, arrival) in enumerate(plan(i, src_refs[i], land_refs[i], me)):
                cp = pltpu.make_async_remote_copy(src_ref=src, dst_ref=arrival, send_sem=s_sems[i].at[j], recv_sem=r_sems[i].at[j],
                                                  device_id=peer, device_id_type=MESH)
                cp.wait_send()
                cp.wait_recv()

    outs = pl.pallas_call(
        body, name=name,
        out_shape=[pltpu.HBM(a.shape, a.dtype) for a in list(srcs) + list(lands)],
        in_specs=[hbm] * (2 * n_arr) + [sem] * (2 * n_arr) + [pl.BlockSpec(memory_space=pl.ANY)],
        out_specs=[hbm] * (2 * n_arr),
        input_output_aliases={i: i for i in range(2 * n_arr)},
        compiler_params=pltpu.CompilerParams(has_side_effects=pltpu.SideEffectType.DATAFLOW_SIDE_EFFECTING),
    )(*srcs, *lands, *send_sems, *recv_sems, after)
    return outs[:n_arr], outs[n_arr:]


_GATHER_FLIPS = (1, 2, 4, 6)


def _gather_plan(i, src_ref, land_ref, me):
    m = src_ref.shape[0]

    def rows(pos):
        return land_ref.at[pl.ds(_index(pos) * m, m), :]

    return [(src_ref, rows(me), _flip(me, k), rows(_flip(me, k))) for k in _GATHER_FLIPS]


def _gather_forward(lands, name):
    n_arr = len(lands)
    chips = (2, 4, 6)

    def body(*refs):
        out_refs = refs[n_arr:2 * n_arr]
        send_sems, recv_sems = refs[2 * n_arr:]
        me = _position()
        sibling = _flip(me, 1)
        sends, arrivals = [], []
        for i, out_ref in enumerate(out_refs):
            m = out_ref.shape[0] // N_DEV

            def copy(pos, j):
                blk = out_ref.at[pl.ds(_index(pos) * m, m), :]
                return pltpu.make_async_remote_copy(src_ref=blk, dst_ref=blk, send_sem=send_sems.at[3 * i + j],
                                                    recv_sem=recv_sems.at[3 * i + j], device_id=sibling, device_id_type=MESH)

            for j, k in enumerate(chips):
                sends.append(copy(_flip(me, k), j))
                arrivals.append(copy(_flip(sibling, k), j))
        for cp in sends:
            cp.start()
        for cp in arrivals:
            cp.wait_recv()
        for cp in sends:
            cp.wait_send()

    anyspec = pl.BlockSpec(memory_space=pl.ANY)
    return pl.pallas_call(
        body, out_shape=[jax.ShapeDtypeStruct(a.shape, a.dtype) for a in lands],
        in_specs=[anyspec] * n_arr, out_specs=[anyspec] * n_arr, input_output_aliases={i: i for i in range(n_arr)},
        scratch_shapes=[pltpu.SemaphoreType.DMA((3 * n_arr,))] * 2, name=name,
    )(*lands)


def _chips_plan(i, src_ref, land_ref, me):
    m = src_ref.shape[0] // 4
    plan = []
    for j, k in enumerate((2, 4, 6)):
        peer = _flip(me, k)
        plan.append((src_ref.at[pl.ds((2 * peer[0] + peer[1]) * m, m), :], land_ref.at[j], peer, land_ref.at[j]))
    return plan


def _exchange_siblings(gs, name):
    n_arr = len(gs)

    def body(*refs):
        g_refs, out_refs = refs[:n_arr], refs[n_arr:2 * n_arr]
        send_sems, recv_sems = refs[2 * n_arr:]
        me = _position()
        c = me[2]
        sibling = _flip(me, 1)
        copies = []
        for i, (g_ref, out_ref) in enumerate(zip(g_refs, out_refs)):
            m_per = g_ref.shape[0] // N_DEV
            for q in range(4):
                copies.append(pltpu.make_async_remote_copy(
                    src_ref=g_ref.at[pl.ds((2 * q + 1 - c) * m_per, m_per), :], dst_ref=out_ref.at[q],
                    send_sem=send_sems.at[4 * i + q], recv_sem=recv_sems.at[4 * i + q],
                    device_id=sibling, device_id_type=MESH))
        for cp in copies:
            cp.start()
        for cp in copies:
            cp.wait_recv()
        for cp in copies:
            cp.wait_send()

    anyspec = pl.BlockSpec(memory_space=pl.ANY)
    return pl.pallas_call(
        body, out_shape=[jax.ShapeDtypeStruct((4, g.shape[0] // N_DEV, g.shape[1]), g.dtype) for g in gs],
        in_specs=[anyspec] * n_arr, out_specs=[anyspec] * n_arr,
        scratch_shapes=[pltpu.SemaphoreType.DMA((4 * n_arr,))] * 2, name=name,
    )(*gs)


def _sum_with_sibling(g, recv, name):
    m = g.shape[0] // N_DEV
    n = g.shape[1]
    tr = _pick(m, (208, 128, 64, 32, 16))
    nt = m // tr

    def body(g_ref, r_ref, o_ref):
        c = lax.axis_index("c")
        own = jnp.where(c == 0, g_ref[0, 0].astype(f32), g_ref[0, 1].astype(f32))
        o_ref[...] = (own + r_ref[0].astype(f32)).astype(o_ref.dtype)

    return pl.pallas_call(
        body, grid=(4, nt),
        in_specs=[pl.BlockSpec((1, 2, tr, n), lambda q, i: (q, 0, i, 0)), pl.BlockSpec((1, tr, n), lambda q, i: (q, i, 0))],
        out_specs=pl.BlockSpec((tr, n), lambda q, i: (q * nt + i, 0)),
        out_shape=jax.ShapeDtypeStruct((4 * m, n), bf16), name=name,
    )(g.reshape(4, 2, m, n), recv)


def _sum_with_chips(h, recv, name):
    m = h.shape[0] // 4
    n = h.shape[1]
    tr = _pick(m, (208, 128, 64, 32, 16))

    def body(h_ref, r_ref, o_ref):
        my_q = 2 * lax.axis_index("x") + lax.axis_index("y")
        own = h_ref[0].astype(f32)
        for q in range(1, 4):
            own = jnp.where(my_q == q, h_ref[q].astype(f32), own)
        o_ref[...] = ((own + r_ref[0].astype(f32)) + r_ref[1].astype(f32)) + r_ref[2].astype(f32)

    return pl.pallas_call(
        body, grid=(m // tr,),
        in_specs=[pl.BlockSpec((4, tr, n), lambda i: (0, i, 0)), pl.BlockSpec((3, tr, n), lambda i: (0, i, 0))],
        out_specs=pl.BlockSpec((tr, n), lambda i: (i, 0)), out_shape=jax.ShapeDtypeStruct((m, n), f32), name=name,
    )(h.reshape(4, m, n), recv)


def _sum_slots(parts, name):
    n_slot, m, n = parts.shape
    tr = _pick(m, (208, 128, 64, 32, 16, 8))

    def body(p_ref, o_ref):
        acc = p_ref[0]
        for s in range(1, n_slot):
            acc = acc + p_ref[s]
        o_ref[...] = acc

    return pl.pallas_call(
        body, grid=(m // tr,), in_specs=[pl.BlockSpec((n_slot, tr, n), lambda i: (0, i, 0))],
        out_specs=pl.BlockSpec((tr, n), lambda i: (i, 0)), out_shape=jax.ShapeDtypeStruct((m, n), parts.dtype), name=name,
    )(parts)


def _reduce_scatter_begin(gs, name):
    from_sibling = _exchange_siblings(gs, "rs_d2d_" + name)
    chip_sums = [_sum_with_sibling(g, r, f"rs_sum2_{name}_{i}") for i, (g, r) in enumerate(zip(gs, from_sibling))]
    lands = [lax.empty((3, h.shape[0] // 4, h.shape[1]), h.dtype) for h in chip_sums]
    return _split_start(chip_sums, lands, _chips_plan, 3, "rs_ici_start_" + name)


def _reduce_scatter_end(started, after, name):
    chip_sums, from_chips = _split_wait(started, _chips_plan, after, "rs_ici_wait_" + name)
    return [_sum_with_chips(h, r, f"rs_sum4_{name}_{i}") for i, (h, r) in enumerate(zip(chip_sums, from_chips))]


def _adamw_update(w, g, m, v):
    mm = ADAM_B1 * m + (1.0 - ADAM_B1) * g
    vv = ADAM_B2 * v + (1.0 - ADAM_B2) * jnp.square(g)
    m_hat = mm / (1.0 - ADAM_B1 ** ADAM_STEP)
    v_hat = vv / (1.0 - ADAM_B2 ** ADAM_STEP)
    return -ADAM_LR * (m_hat / (jnp.sqrt(v_hat) + ADAM_EPS) + ADAM_WD * w), mm, vv


def _adamw_many(ws, gs, ms, vs, name):
    k = len(ws)
    shapes = [w.shape for w in ws]
    flat = [[a.reshape(-1, a.shape[-1]) for a in group] for group in (ws, gs, ms, vs)]

    def body(*refs):
        for i in range(k):
            d, mm, vv = _adamw_update(*(refs[j * k + i][...] for j in range(4)))
            refs[4 * k + i][...] = d
            refs[5 * k + i][...] = mm
            refs[6 * k + i][...] = vv

    outs = pl.pallas_call(
        body, out_shape=[jax.ShapeDtypeStruct(a.shape, f32) for a in flat[0]] * 3, name=name,
    )(*flat[0], *flat[1], *flat[2], *flat[3])
    return tuple([outs[j * k + i].reshape(shapes[i]) for i in range(k)] for j in range(3))


def _adamw(w, g, m, v, name):
    shape = w.shape
    n = shape[-1]
    r = w.size // n
    w2, g2, m2, v2 = (a.reshape(r, n) for a in (w, g, m, v))
    tr = _pick(r, (256, 208, 128, 64, 32, 16, 8))

    def body(w_ref, g_ref, m_ref, v_ref, d_ref, mo_ref, vo_ref):
        d_ref[...], mo_ref[...], vo_ref[...] = _adamw_update(w_ref[...], g_ref[...], m_ref[...], v_ref[...])

    spec = pl.BlockSpec((tr, n), lambda i: (i, 0))
    outs = pl.pallas_call(
        body, grid=(r // tr,), in_specs=[spec] * 4, out_specs=[spec] * 3,
        out_shape=[jax.ShapeDtypeStruct((r, n), f32)] * 3, name=name,
    )(w2, g2, m2, v2)
    return tuple(o.reshape(shape) for o in outs)


_SMALL = ("shift_mu", "w_decay0", "a0", "k_k", "k_a", "r_k", "ln_x_w", "ln_x_b", "v_mix0", "lb_logits",
          "g_norm_w", "ln_w", "ln_b")
_NAMES = ("w_in", "shift_mu", "w_decay0", "w_decay_up", "a0", "a_up", "k_k", "k_a", "r_k", "ln_x_w", "ln_x_b",
          "v_mix0", "v_mix_down", "v_mix_up", "lb_logits", "g_norm_w", "w_out", "ln_w", "ln_b")


def _pad_rows(a, rows, at_end):
    z = jnp.zeros((rows - a.shape[0], a.shape[1]), a.dtype)
    return jnp.concatenate([a, z] if at_end else [z, a], axis=0)


def kernel(x, w_in, shift_mu, w_decay0, w_decay_up, a0, a_up, k_k, k_a, r_k, ln_x_w, ln_x_b, v_mix0, v_mix_down, v_mix_up, lb_logits, g_norm_w, w_out, ln_w, ln_b, loss_target, m_w_in, m_shift_mu, m_w_decay0, m_w_decay_up, m_a0, m_a_up, m_k_k, m_k_a, m_r_k, m_ln_x_w, m_ln_x_b, m_v_mix0, m_v_mix_down, m_v_mix_up, m_lb_logits, m_g_norm_w, m_w_out, m_ln_w, m_ln_b, v_w_in, v_shift_mu, v_w_decay0, v_w_decay_up, v_a0, v_a_up, v_k_k, v_k_a, v_r_k, v_ln_x_w, v_ln_x_b, v_v_mix0, v_v_mix_down, v_v_mix_up, v_lb_logits, v_g_norm_w, v_w_out, v_ln_w, v_ln_b):
    weights = dict(w_in=w_in, shift_mu=shift_mu, w_decay0=w_decay0, w_decay_up=w_decay_up, a0=a0, a_up=a_up, k_k=k_k,
                   k_a=k_a, r_k=r_k, ln_x_w=ln_x_w, ln_x_b=ln_x_b, v_mix0=v_mix0, v_mix_down=v_mix_down,
                   v_mix_up=v_mix_up, lb_logits=lb_logits, g_norm_w=g_norm_w, w_out=w_out, ln_w=ln_w, ln_b=ln_b)
    mom1 = dict(w_in=m_w_in, shift_mu=m_shift_mu, w_decay0=m_w_decay0, w_decay_up=m_w_decay_up, a0=m_a0, a_up=m_a_up,
                k_k=m_k_k, k_a=m_k_a, r_k=m_r_k, ln_x_w=m_ln_x_w, ln_x_b=m_ln_x_b, v_mix0=m_v_mix0,
                v_mix_down=m_v_mix_down, v_mix_up=m_v_mix_up, lb_logits=m_lb_logits, g_norm_w=m_g_norm_w,
                w_out=m_w_out, ln_w=m_ln_w, ln_b=m_ln_b)
    mom2 = dict(w_in=v_w_in, shift_mu=v_shift_mu, w_decay0=v_w_decay0, w_decay_up=v_w_decay_up, a0=v_a0, a_up=v_a_up,
                k_k=v_k_k, k_a=v_k_a, r_k=v_r_k, ln_x_w=v_ln_x_w, ln_x_b=v_ln_x_b, v_mix0=v_v_mix0,
                v_mix_down=v_v_mix_down, v_mix_up=v_v_mix_up, lb_logits=v_lb_logits, g_norm_w=v_g_norm_w,
                w_out=v_w_out, ln_w=v_ln_w, ln_b=v_ln_b)
    assert x.shape[0] == 1 and w_in.shape[0] == DEPTH
    t, d = x.shape[1], x.shape[2]
    dr = w_decay0.shape[1]
    dh = g_norm_w.shape[1]
    rank_w, rank_a, rank_v = w_decay_up.shape[1], a_up.shape[1], v_mix_up.shape[1]
    rwc = 4 * dr + rank_w + rank_a
    assert rank_w + rank_a == LANES and rank_v <= LANES and dr + dh == d
    assert t % CHUNK == 0 and dr % LANES == 0 and dh % LANES == 0 and shift_mu.shape[1] == rwc
    n_pair = dr // LANES
    me = _index(_position())

    win_t = [_all_gather_rows(w_in[0].T.astype(bf16), "ag_w_in_0"), None]
    wout = [None, None]
    shard = dr // N_DEV
    pack = jnp.concatenate([w_decay_up[0], w_decay_up[1], a_up[0], a_up[1], v_mix_up[0], v_mix_down[0].T], axis=0)
    pack = _all_gather_rows(pack, "ag_small")
    late_blocks = [w_out[0].astype(bf16), w_in[1].T.astype(bf16), w_out[1].astype(bf16)]
    late_lands = [lax.dynamic_update_slice(lax.empty((N_DEV * blk.shape[0], blk.shape[1]), bf16), blk, (me * blk.shape[0], 0))
                  for blk in late_blocks]
    late_gather = _split_start(late_blocks, late_lands, _gather_plan, len(_GATHER_FLIPS), "ag_late_start",
                               after=(win_t[0], pack))
    pack = jnp.transpose(pack.reshape(N_DEV, -1, shard), (1, 0, 2)).reshape(-1, dr)
    offs = [0, rank_w, 2 * rank_w, 2 * rank_w + rank_a, 2 * rank_w + 2 * rank_a, 2 * rank_w + 2 * rank_a + rank_v,
            2 * rank_w + 2 * rank_a + 2 * rank_v]
    wdu_f = [pack[offs[0]:offs[1]], pack[offs[1]:offs[2]]]
    aup_f = [pack[offs[2]:offs[3]], pack[offs[3]:offs[4]]]
    vup_f = pack[offs[4]:offs[5]]
    vdown_f = pack[offs[5]:offs[6]].T

    def after_start(a, started):
        return a + started[-1][0:1, 0:1]

    def rwkv_params(l):
        mu = after_start(shift_mu[0:1], late_gather) if l == 0 else shift_mu[l:l + 1]
        prm = [mu, w_decay0[l:l + 1], a0[l:l + 1], _pad_rows(wdu_f[l], LANES, True),
               _pad_rows(aup_f[l], LANES, False)]
        if l == 1:
            prm += [v_mix0[0:1], _pad_rows(vdown_f.T, LANES, True).T, _pad_rows(vup_f, LANES, True)]
        rows = jnp.stack([k_k[l], k_a[l], r_k[l], ln_x_w[l], ln_x_b[l]] + [jnp.zeros((dr,), f32)] * 3, axis=0)
        pp = jnp.transpose(rows.reshape(8, n_pair, LANES), (1, 0, 2))
        return tuple(prm), pp

    h = x[0]
    h16 = h.astype(bf16)
    tgt = loss_target[0]
    saved = []
    vfirst = None
    for l in range(DEPTH):
        prm, pp = rwkv_params(l)
        proj = _matmul(h16, win_t[l], "nt", f"mm_proj_{l}", (2048, 640, 2048))
        if l == 0:
            cat, vfirst, mck = _rwkv_fwd(False, proj, None, prm, pp, d)
        else:
            cat, mck = _rwkv_fwd(True, proj, vfirst, prm, pp, d)
        cat, sck = _hgrn_fwd(l == 1, proj, lb_logits, g_norm_w[l:l + 1], cat, rwc)
        if l == 0:
            _, arrived = _split_wait(late_gather, _gather_plan, cat, "ag_late_wait")
            wout[0], win_t[1], wout[1] = _gather_forward(arrived, "ag_late_forward")
        y = _matmul(cat, wout[l], "nn", f"mm_out_{l}", (1024, 1024, 2048))
        saved.append((h, h16, proj, prm, pp, mck, sck, cat, y))
        if l < DEPTH - 1:
            h, h16 = _ln_fwd(h, y, ln_w[l:l + 1], ln_b[l:l + 1])
        else:
            top = _ln_loss_bwd(h, y, ln_w[l:l + 1], ln_b[l:l + 1], tgt)
    loss = lax.psum(top[4][0, 0], ("x", "y", "c"))

    grads = {}
    big = {}
    dvfirst = None
    d_lbl = None
    rs_started = {}
    for l in reversed(range(DEPTH)):
        h_l, h16_l, proj, prm, pp, mck, sck, cat, y = saved[l]
        if l == DEPTH - 1:
            dy, dy16, g_ln_w, g_ln_b = top[:4]
        else:
            dy, dy16, g_ln_w, g_ln_b = _ln_bwd(h_l, y, after_start(ln_w[l:l + 1], rs_started[l + 1]), ln_b[l:l + 1], dh_out)
        dcat = _matmul(dy16, wout[l], "nt", f"mm_dcat_{l}", (1024, 1024, 2048))
        big[("w_out", l)] = _matmul(cat, dy16, "tn", f"mm_dwout_{l}", (512, 2048, 2048), out_dtype=bf16)
        if l == 1:
            outs = _rwkv_bwd(True, proj, vfirst, prm, pp, mck, dcat, None)
            dproj_r, dvfirst = outs[0], outs[1]
            dprm, dpp = outs[2:-1], outs[-1]
        else:
            outs = _rwkv_bwd(False, proj, None, prm, pp, mck, dcat, dvfirst)
            dproj_r = outs[0]
            dprm, dpp = outs[1:-1], outs[-1]
        dproj, dlbl_l, dgnw = _hgrn_bwd(l == 1, proj, lb_logits, g_norm_w[l:l + 1], sck, dcat, rwc, dproj_r)
        big[("w_in", l)] = _matmul(dproj, h16_l, "tn", f"mm_dwin_{l}", (640, 2048, 2048), out_dtype=bf16)
        sharded = [dprm[3][:rank_w].T, dprm[4][rank_w:].T]
        if l == 1:
            sharded += [dprm[6][:, :rank_v], dprm[7][:rank_v].T,
                        jnp.zeros((dr, LANES - 2 * rank_v), f32)]
        sharded = jnp.concatenate(sharded, axis=1).astype(bf16)
        rs_started[l] = _reduce_scatter_begin([big[("w_in", l)], big[("w_out", l)], sharded], f"l{l}")
        dy_res = after_start(dy, rs_started[l]) if l == 0 else dy
        dh_out = _matmul(dproj, win_t[l], "nn", f"mm_dh_{l}", (1024, 1024, 1664), add=dy_res, add_scale=ALPHA)
        dpp = jnp.transpose(dpp, (1, 0, 2)).reshape(8, dr)
        grads[l] = dict(shift_mu=dprm[0][0], w_decay0=dprm[1][0], a0=dprm[2][0],
                        k_k=dpp[0], k_a=dpp[1], r_k=dpp[2], ln_x_w=dpp[3], ln_x_b=dpp[4],
                        g_norm_w=dgnw[0], ln_w=g_ln_w[0], ln_b=g_ln_b[0])
        if l == 1:
            grads[l].update(v_mix0=dprm[5][0])
            d_lbl = dlbl_l
    grad_x = dh_out[None]

    def both(name):
        return jnp.stack([grads[0][name], grads[1][name]])

    small = dict(shift_mu=both("shift_mu"), w_decay0=both("w_decay0"), a0=both("a0"), k_k=both("k_k"), k_a=both("k_a"),
                 r_k=both("r_k"), ln_x_w=both("ln_x_w"), ln_x_b=both("ln_x_b"), v_mix0=grads[1]["v_mix0"][None],
                 lb_logits=d_lbl, g_norm_w=both("g_norm_w"), ln_w=both("ln_w"), ln_b=both("ln_b"))
    flat = jnp.concatenate([small[nm].reshape(-1) for nm in _SMALL])
    n_flat = flat.shape[0]
    rows = -(-n_flat // (8 * LANES)) * 8
    flat = jnp.concatenate([flat, jnp.zeros((rows * LANES - n_flat,), f32)]).reshape(rows, LANES)
    total = _sum_slots(_all_gather_rows(flat, "ag_small_grads").reshape(N_DEV, rows, LANES), "sum_small_grads").reshape(-1)
    gsm = {}
    off = 0
    for nm in _SMALL:
        size = small[nm].size
        gsm[nm] = total[off:off + size].reshape(small[nm].shape)
        off += size
    reduced = {1: _reduce_scatter_end(rs_started[1], dh_out, "l1")}
    reduced[0] = _reduce_scatter_end(rs_started[0], total, "l0")
    g_w_in_t = jnp.stack([reduced[l][0] for l in range(DEPTH)])
    gsm["w_in"] = jnp.transpose(g_w_in_t, (0, 2, 1))
    gsm["w_out"] = jnp.stack([reduced[l][1] for l in range(DEPTH)])
    gsm["w_decay_up"] = jnp.stack([reduced[l][2][:, :rank_w].T for l in range(DEPTH)])
    gsm["a_up"] = jnp.stack([reduced[l][2][:, rank_w:rank_w + rank_a].T for l in range(DEPTH)])
    gsm["v_mix_down"] = reduced[1][2][:, LANES:LANES + rank_v][None]
    gsm["v_mix_up"] = reduced[1][2][:, LANES + rank_v:LANES + 2 * rank_v].T[None]

    deltas, new_m, new_v = {}, {}, {}
    swap = lambda a: jnp.transpose(a, (0, 2, 1))
    deltas["w_in"], new_m["w_in"], new_v["w_in"] = (
        swap(a) for a in _adamw(swap(w_in), g_w_in_t, swap(m_w_in), swap(v_w_in), "adamw_w_in"))
    deltas["w_out"], new_m["w_out"], new_v["w_out"] = _adamw(w_out, gsm["w_out"], m_w_out, v_w_out, "adamw_w_out")
    rest = [nm for nm in _NAMES if nm not in ("w_in", "w_out")]
    d_rest, m_rest, v_rest = _adamw_many([weights[nm] for nm in rest], [gsm[nm] for nm in rest],
                                         [mom1[nm] for nm in rest], [mom2[nm] for nm in rest], "adamw_small")
    for i, nm in enumerate(rest):
        deltas[nm], new_m[nm], new_v[nm] = d_rest[i], m_rest[i], v_rest[i]
    return (loss, grad_x, *[gsm[nm] for nm in _NAMES], *[deltas[nm] for nm in _NAMES],
            *[new_m[nm] for nm in _NAMES], *[new_v[nm] for nm in _NAMES])
```

```python
import functools

import jax
import jax.numpy as jnp
from jax import lax
from jax.experimental import pallas as pl
from jax.experimental.pallas import tpu as pltpu

f32 = jnp.float32
bf16 = jnp.bfloat16

N_DEV = 8
CHUNK = 64
LANES = 128
RWKV_HEAD = 64
DEPTH = 2
ALPHA = (2 * DEPTH) ** 0.25
LN_EPS = 1e-5
GN_EPS = 64e-5
RMS_EPS = 1e-5
LB_FLOOR = 1e-30
ADAM_LR, ADAM_B1, ADAM_B2, ADAM_EPS, ADAM_WD, ADAM_STEP = 0.001, 0.9, 0.999, 1e-08, 0.01, 10
MESH = pl.DeviceIdType.MESH


def _iota(shape, d):
    return lax.broadcasted_iota(jnp.int32, shape, d)


_DIMS = {"nn": (((1,), (0,)), ((), ())), "nt": (((1,), (1,)), ((), ())), "tn": (((0,), (0,)), ((), ()))}
_BATCH_DIMS = {"nn": (((2,), (1,)), ((0,), (0,))), "nt": (((2,), (2,)), ((0,), (0,))), "tn": (((1,), (1,)), ((0,), (0,)))}
_K_AXES = {"nn": (-1, -2), "nt": (-1, -1), "tn": (-2, -2)}


def _mxu(a, b, mode):
    return lax.dot_general(a, b, (_BATCH_DIMS if a.ndim == 3 else _DIMS)[mode], preferred_element_type=f32)


def _split(x):
    hi = x.astype(bf16)
    return hi, (x - hi.astype(f32)).astype(bf16)


def _mm2_impl(a, b, mode, passes=3):
    ah, al = _split(a)
    if passes == 3:
        bh, bl = _split(b)
        lhs, rhs = [ah, ah, al], [bh, bl, bh]
    else:
        bh = b.astype(bf16)
        lhs, rhs = [ah, al], [bh, bh]
    ka, kb = _K_AXES[mode]
    k = a.shape[ka]
    if k % (LANES if -1 in (ka, kb) else 16) == 0:
        return _mxu(jnp.concatenate(lhs, axis=ka), jnp.concatenate(rhs, axis=kb), mode)
    out = _mxu(lhs[0], rhs[0], mode)
    for x, y in zip(lhs[1:], rhs[1:]):
        out = out + _mxu(x, y, mode)
    return out


@functools.partial(jax.custom_vjp, nondiff_argnums=(2, 3))
def _mm2(a, b, mode, passes=3):
    return _mm2_impl(a, b, mode, passes)


def _mm2_fwd(a, b, mode, passes):
    return _mm2_impl(a, b, mode, passes), (a, b)


def _mm2_bwd(mode, passes, res, g):
    a, b = res
    if mode == "nn":
        return _mm2_impl(g, b, "nt", passes), _mm2_impl(a, g, "tn", passes)
    if mode == "nt":
        return _mm2_impl(g, b, "nn", passes), _mm2_impl(g, a, "tn", passes)
    return _mm2_impl(b, g, "nt", passes), _mm2_impl(a, g, "nn", passes)


_mm2.defvjp(_mm2_fwd, _mm2_bwd)

TRI_PASSES = 2
APPLY_PASSES = 2


def _const_impl(cm, x, mode):
    hi, lo = _split(x)
    if mode in ("r", "rt"):
        shape = x.shape
        hi, lo = hi.reshape(-1, shape[-1]), lo.reshape(-1, shape[-1])
        dims = "nn" if mode == "r" else "nt"
        out = _mxu(hi, cm, dims) + _mxu(lo, cm, dims)
        return out.reshape(shape[:-1] + (out.shape[-1],))
    if x.ndim == 3:
        cm = jnp.broadcast_to(cm, (x.shape[0],) + cm.shape)
    return _mxu(cm, hi, mode) + _mxu(cm, lo, mode)


@jax.custom_vjp
def _const_left(cm, x):
    return _const_impl(cm, x, "nn")


_const_left.defvjp(lambda cm, x: (_const_impl(cm, x, "nn"), cm),
                   lambda cm, g: (jnp.zeros_like(cm), _const_impl(cm, g, "tn")))


@jax.custom_vjp
def _const_right(x, cm):
    return _const_impl(cm, x, "r")


_const_right.defvjp(lambda x, cm: (_const_impl(cm, x, "r"), cm),
                    lambda cm, g: (_const_impl(cm, g, "rt"), jnp.zeros_like(cm)))


def _tri_inv(a):
    n = a.shape[-1]
    tm = (_iota((n, n), 0) == _iota((n, n), 1)).astype(f32) + a
    ak = a
    for _ in range(5):
        ak = _mm2_impl(ak, ak, "nn", TRI_PASSES)
        tm = tm + _mm2_impl(tm, ak, "nn", TRI_PASSES)
    return tm


@jax.custom_vjp
def _tri_solve(tm, a, x):
    del a
    return _mm2_impl(tm, x, "nn")


def _tri_solve_fwd(tm, a, x):
    u = _mm2_impl(tm, x, "nn")
    return u, (tm, u)


def _tri_solve_bwd(res, du):
    tm, u = res
    dx = _mm2_impl(tm, du, "tn")
    return jnp.zeros_like(tm), _mm2_impl(dx, u, "nt"), dx


_tri_solve.defvjp(_tri_solve_fwd, _tri_solve_bwd)


def _col_of_row(row_vec):
    n = row_vec.shape[-1]
    eye = _iota((n, n), 0) == _iota((n, n), 1)
    return jnp.sum(jnp.where(eye, jnp.broadcast_to(row_vec, row_vec.shape[:-2] + (n, n)), 0.0), axis=-1, keepdims=True)


def _softplus(x):
    return jnp.maximum(x, 0.0) + jnp.log1p(jnp.exp(-jnp.abs(x)))


def _log_sigmoid(x):
    return -_softplus(-x)


def _logaddexp(a, b):
    return jnp.maximum(a, b) + jnp.log1p(jnp.exp(-jnp.abs(a - b)))


def _silu(x):
    return x * jax.nn.sigmoid(x)


def _tril(c, strict):
    r, s = _iota((c, c), 0), _iota((c, c), 1)
    return (r > s) if strict else (r >= s)


def _last_row(a):
    c = a.shape[-2]
    return jnp.sum(jnp.where(_iota(a.shape, a.ndim - 2) == c - 1, a, 0.0), axis=-2, keepdims=True)


def _rwkv_pre(layer1, prm, y, prev, vf):
    c = y.shape[0]
    if layer1:
        mu, w0, a0, wup, aup, v0, vdown, vup = prm
    else:
        mu, w0, a0, wup, aup = prm
    dr = w0.shape[1]
    shift = (_iota((c, c), 0) == _iota((c, c), 1) + 1).astype(bf16)
    y_prev = _const_left(shift, y) + jnp.where(_iota((c, 1), 0) == 0, prev, 0.0)
    rw = y + mu * (y_prev - y)
    r, k, v, z = (rw[:, i * dr:(i + 1) * dr] for i in range(4))
    wdad = rw[:, 4 * dr:4 * dr + LANES]
    w_raw = w0 + _mm2(jnp.tanh(wdad), wup, "nn")
    lw = -jnp.exp(-_softplus(-w_raw) - 0.5)
    asig = jax.nn.sigmoid(a0 + _mm2(wdad, aup, "nn"))
    if layer1:
        v = v + (vf - v) * jax.nn.sigmoid(v0 + _mm2(_mm2(v, vdown, "nn"), vup, "nn"))
    return r, k, v, z, lw, asig


def _rwkv_pair(pp, m0, xs, tm=None):
    kkw, kaw, rkw, gnw, gnb = pp
    r, k, v, z, lw, asig = xs
    c = r.shape[-2]
    n2 = 2 * c
    lane = _iota((1, LANES), 1)
    mh0, mh1 = (lane < RWKV_HEAD).astype(f32), (lane >= RWKV_HEAD).astype(f32)
    same_head = _iota((LANES, LANES), 0) // RWKV_HEAD == _iota((LANES, LANES), 1) // RWKV_HEAD
    g = same_head.astype(bf16)

    def seg(x):
        return _const_right(x, g)

    def stack(x):
        return jnp.concatenate([x * mh0, x * mh1], axis=-2)

    kk = k * kkw
    kk = kk / jnp.maximum(jnp.sqrt(seg(kk * kk)), 1e-12)
    k2 = k * (1.0 + (asig - 1.0) * kaw)
    a = -kk
    b = kk * asig
    cum = _const_left(_tril(c, False).astype(bf16), lw)
    at = stack(a * jnp.exp(cum - lw))
    rt = stack(r * jnp.exp(cum))
    en = jnp.exp(-cum)
    sc = _mm2(jnp.concatenate([at, rt], axis=-2), jnp.concatenate([stack(b * en), stack(k2 * en)], axis=-2), "nt")
    row, col = _iota((n2, n2), 0), _iota((n2, n2), 1)
    same = row // c == col // c
    strict = same & (row % c > col % c)
    incl = same & (row % c >= col % c)
    aab = jnp.where(strict, sc[..., :n2, :n2], 0.0)
    aak = jnp.where(strict, sc[..., :n2, n2:], 0.0)
    arb = jnp.where(incl, sc[..., n2:, :n2], 0.0)
    ark = jnp.where(incl, sc[..., n2:, n2:], 0.0)
    vv = jnp.concatenate([v, v], axis=-2)
    mask_st = jnp.concatenate([jnp.broadcast_to(mh0, (c, LANES)), jnp.broadcast_to(mh1, (c, LANES))], axis=0)
    x_st = _mm2(jnp.concatenate([at, aak], axis=-1), jnp.concatenate([m0, vv], axis=-2), "nn", APPLY_PASSES)
    if tm is None:
        tm = _tri_inv(lax.stop_gradient(aab))
    u_st = _tri_solve(tm, aab, x_st) * mask_st
    o_st = _mm2(jnp.concatenate([rt, arb, ark], axis=-1), jnp.concatenate([m0, u_st, vv], axis=-2), "nn", APPLY_PASSES) * mask_st
    u = u_st[..., :c, :] + u_st[..., c:, :]
    o = o_st[..., :c, :] + o_st[..., c:, :]
    cum_last = _last_row(cum)
    dec_end = jnp.exp(cum_last - cum)
    m_new = _col_of_row(jnp.exp(cum_last)) * m0 + _mm2(
        jnp.concatenate([b * dec_end, k2 * dec_end], axis=-2), jnp.concatenate([u, v], axis=-2), "tn", APPLY_PASSES) * same_head.astype(f32)
    mean = seg(o) * (1.0 / RWKV_HEAD)
    d = o - mean
    var = seg(d * d) * (1.0 / RWKV_HEAD)
    on = d * lax.rsqrt(var + GN_EPS) * gnw + gnb
    bonus = seg(r * k2 * rkw) * v
    return (on + bonus) * _silu(z), m_new, tm


def _split_lanes(a, n):
    return [a[:, i * LANES:(i + 1) * LANES] for i in range(n)]


def _rwkv_step(layer1, prm, y, prev, vf, pp, m0, tm=None):
    xs = _rwkv_pre(layer1, prm, y, prev, vf)
    n_pair = m0.shape[0]
    og, m_new, tm = _rwkv_pair(pp, m0, tuple(jnp.concatenate([p[None] for p in _split_lanes(a, n_pair)], axis=0) for a in xs), tm)
    return og, m_new, xs[2], tm


def _group(n):
    return n


def _rwkv_specs(layer1, t, dr, rwc, n_pair, rev):
    nc = t // CHUNK
    grp = _group(n_pair)

    def cidx(c):
        return (nc - 1 - c) if rev else c

    full = lambda shape: pl.BlockSpec(shape, lambda c, p: tuple(0 for _ in shape))
    specs = [
        pl.BlockSpec((CHUNK, rwc), lambda c, p: (cidx(c), 0)),
        pl.BlockSpec((8, rwc), lambda c, p: (jnp.maximum(cidx(c) * (CHUNK // 8) - 1, 0), 0)),
    ]
    if layer1:
        specs.append(pl.BlockSpec((CHUNK, dr), lambda c, p: (cidx(c), 0)))
    prm_shapes = [(1, rwc), (1, dr), (1, dr), (LANES, dr), (LANES, dr)]
    if layer1:
        prm_shapes += [(1, dr), (dr, LANES), (LANES, dr)]
    specs += [full(s) for s in prm_shapes]
    specs.append(pl.BlockSpec((grp, 8, LANES), lambda c, p: (p, 0, 0)))
    return specs, prm_shapes, cidx, full


def _rwkv_fwd(layer1, proj, vf, prm, pp, cat_width):
    t = proj.shape[0]
    dr = prm[1].shape[1]
    rwc = prm[0].shape[1]
    n_pair = dr // LANES
    nc = t // CHUNK
    n_prm = len(prm)
    specs, _, _, _ = _rwkv_specs(layer1, t, dr, rwc, n_pair, False)

    def body(*refs):
        y_ref, prev_ref = refs[0], refs[1]
        i = 2
        vf_ref = None
        if layer1:
            vf_ref = refs[i]
            i += 1
        prm_refs = refs[i:i + n_prm]
        i += n_prm
        pp_ref = refs[i]
        i += 1
        cat_ref = refs[i]
        i += 1
        vout_ref = None
        if not layer1:
            vout_ref = refs[i]
            i += 1
        mck_ref, m_s = refs[i], refs[i + 1]
        c = pl.program_id(0)

        @pl.when(c == 0)
        def _():
            m_s[...] = jnp.zeros_like(m_s)

        prev = prev_ref[pl.ds(7, 1), :] * (c != 0).astype(f32)
        m0 = m_s[...]
        ppv = tuple(pp_ref[:, pl.ds(q, 1), :] for q in range(5))
        og, m_new, v, tm = _rwkv_step(layer1, tuple(r[...] for r in prm_refs), y_ref[...], prev,
                                      vf_ref[...] if layer1 else None, ppv, m0)
        mck_ref[0, :n_pair] = m0
        mck_ref[0, n_pair:] = tm
        if not layer1:
            vout_ref[...] = v
        for j in range(n_pair):
            cat_ref[:, j * LANES:(j + 1) * LANES] = og[j]
        m_s[...] = m_new

    grp = _group(n_pair)
    assert grp == n_pair
    out_shape = [jax.ShapeDtypeStruct((t, cat_width), f32)]
    out_specs = [pl.BlockSpec((CHUNK, grp * LANES), lambda c, p: (c, p))]
    if not layer1:
        out_shape.append(jax.ShapeDtypeStruct((t, dr), f32))
        out_specs.append(pl.BlockSpec((CHUNK, dr), lambda c, p: (c, 0)))
    out_shape.append(jax.ShapeDtypeStruct((nc, 2 * n_pair, LANES, LANES), f32))
    out_specs.append(pl.BlockSpec((1, 2 * grp, LANES, LANES), lambda c, p: (c, p, 0, 0)))
    args = [proj, proj] + ([vf] if layer1 else []) + list(prm) + [pp]
    return pl.pallas_call(
        body, grid=(nc, 1), in_specs=specs, out_specs=out_specs, out_shape=out_shape,
        scratch_shapes=[pltpu.VMEM((n_pair, LANES, LANES), f32)],
        compiler_params=pltpu.CompilerParams(dimension_semantics=("arbitrary", "arbitrary")),
        name=f"rwkv_fwd_l{int(layer1)}",
    )(*args)


def _rwkv_bwd(layer1, proj, vf, prm, pp, mck, dcat, dvout):
    t = proj.shape[0]
    dr = prm[1].shape[1]
    rwc = prm[0].shape[1]
    n_pair = dr // LANES
    nc = t // CHUNK
    n_prm = len(prm)
    specs, prm_shapes, cidx, full = _rwkv_specs(layer1, t, dr, rwc, n_pair, True)
    grp = _group(n_pair)
    assert grp == n_pair
    specs.append(pl.BlockSpec((1, 2 * grp, LANES, LANES), lambda c, p: (cidx(c), p, 0, 0)))
    specs.append(pl.BlockSpec((CHUNK, grp * LANES), lambda c, p: (cidx(c), p)))
    if not layer1:
        specs.append(pl.BlockSpec((CHUNK, dr), lambda c, p: (cidx(c), 0)))

    def body(*refs):
        y_ref, prev_ref = refs[0], refs[1]
        i = 2
        vf_ref = None
        if layer1:
            vf_ref = refs[i]
            i += 1
        prm_refs = refs[i:i + n_prm]
        i += n_prm
        pp_ref, mck_ref, dog_ref = refs[i], refs[i + 1], refs[i + 2]
        i += 3
        dvout_ref = None
        if not layer1:
            dvout_ref = refs[i]
            i += 1
        dy_ref = refs[i]
        i += 1
        dvf_ref = None
        if layer1:
            dvf_ref = refs[i]
            i += 1
        dprm_refs = refs[i:i + n_prm]
        i += n_prm
        dpp_ref = refs[i]
        dm_s, dprev_s = refs[i + 1:i + 3]
        c = pl.program_id(0)
        cr = nc - 1 - c

        @pl.when(c == 0)
        def _():
            dm_s[...] = jnp.zeros_like(dm_s)
            dprev_s[...] = jnp.zeros_like(dprev_s)
            dpp_ref[...] = jnp.zeros_like(dpp_ref)
            for r in dprm_refs:
                r[...] = jnp.zeros_like(r)

        prev = prev_ref[pl.ds(7, 1), :] * (cr != 0).astype(f32)
        prm_v = tuple(r[...] for r in prm_refs)
        ppv = tuple(pp_ref[:, pl.ds(q, 1), :] for q in range(5))
        dog = jnp.stack([dog_ref[:, j * LANES:(j + 1) * LANES] for j in range(n_pair)], axis=0)
        m0, tm = mck_ref[0, :n_pair], mck_ref[0, n_pair:]
        no_tm = jnp.zeros_like(tm)
        if layer1:
            _, vjp = jax.vjp(lambda a, b, d, e, g, h: _rwkv_step(True, a, b, d, e, g, h, tm),
                             prm_v, y_ref[...], prev, vf_ref[...], ppv, m0)
            dprm, dy, dprev, dvf, dppv, dm0 = vjp((dog, dm_s[...], jnp.zeros((CHUNK, dr), f32), no_tm))
            dvf_ref[...] = dvf
        else:
            _, vjp = jax.vjp(lambda a, b, d, e, g: _rwkv_step(False, a, b, d, None, e, g, tm), prm_v, y_ref[...], prev, ppv, m0)
            dprm, dy, dprev, dppv, dm0 = vjp((dog, dm_s[...], dvout_ref[...], no_tm))
        dm_s[...] = dm0
        for q in range(5):
            dpp_ref[:, pl.ds(q, 1), :] += dppv[q]
        dy_ref[...] = (dy + jnp.where(_iota((CHUNK, 1), 0) == CHUNK - 1, dprev_s[...], 0.0)).astype(bf16)
        dprev_s[...] = dprev
        for r, gval in zip(dprm_refs, dprm):
            r[...] += gval

    out_shape = [jax.ShapeDtypeStruct((t, proj.shape[1]), bf16)]
    out_specs = [pl.BlockSpec((CHUNK, rwc), lambda c, p: (cidx(c), 0))]
    if layer1:
        out_shape.append(jax.ShapeDtypeStruct((t, dr), f32))
        out_specs.append(pl.BlockSpec((CHUNK, dr), lambda c, p: (cidx(c), 0)))
    out_shape += [jax.ShapeDtypeStruct(s, f32) for s in prm_shapes]
    out_specs += [full(s) for s in prm_shapes]
    out_shape.append(jax.ShapeDtypeStruct((n_pair, 8, LANES), f32))
    out_specs.append(full((n_pair, 8, LANES)))
    args = [proj, proj] + ([vf] if layer1 else []) + list(prm) + [pp, mck, dcat] + ([] if layer1 else [dvout])
    return pl.pallas_call(
        body, grid=(nc, 1), in_specs=specs, out_specs=out_specs, out_shape=out_shape,
        scratch_shapes=[pltpu.VMEM((n_pair, LANES, LANES), f32), pltpu.VMEM((1, rwc), f32)],
        compiler_params=pltpu.CompilerParams(dimension_semantics=("arbitrary", "arbitrary")),
        name=f"rwkv_bwd_l{int(layer1)}",
    )(*args)


def _hgrn_chunk(layer1, lbl, gnw, s0, q_raw, f_raw, i_in, z):
    c = q_raw.shape[-2]
    q = _silu(q_raw)
    ls = _log_sigmoid(f_raw)
    if layer1:
        l0, l1 = lbl[..., 0:1, :], lbl[..., 1:2, :]
        mx = jnp.maximum(l0, l1)
        e0, e1 = jnp.exp(l0 - mx), jnp.exp(l1 - mx)
        sm0, sm1 = e0 / (e0 + e1), e1 / (e0 + e1)
        lb = (sm0 + sm1) - sm0
        log_f = _logaddexp(jnp.log(jnp.maximum(lb, LB_FLOOR)), jnp.log1p(-lb) + ls)
        k = (1.0 - lb) * jax.nn.sigmoid(-f_raw)
    else:
        log_f = _logaddexp(jnp.full_like(ls, jnp.log(jnp.float32(LB_FLOOR))), ls)
        k = jax.nn.sigmoid(-f_raw)
    row, col = _iota((c, c), 0), _iota((c, c), 1)
    trow = _iota((c, 1), 0)
    halves = []
    half = c // 2
    while half >= 1:
        halves.append(half)
        half //= 2
    cmat = jnp.concatenate([(col <= row).astype(f32)]
                           + [(col <= (row // (2 * hf)) * (2 * hf) + hf - 1).astype(f32) for hf in halves], axis=0)
    ball = _const_left(cmat.astype(bf16), log_f)
    b = ball[..., :c, :]
    att = None
    for lvl, hf in enumerate(halves):
        blk = 2 * hf
        bref = ball[..., (lvl + 1) * c:(lvl + 2) * c, :]
        upper = (trow % blk) >= hf
        dec = jnp.exp(jnp.where(upper, b - bref, bref - b))
        qh = jnp.where(upper, q * dec, 0.0)
        kh = jnp.where(upper, 0.0, k * dec)
        term = jnp.where(row // blk == col // blk, _mm2(qh, kh, "nt", APPLY_PASSES), 0.0)
        att = term if att is None else att + term
    lhs = jnp.concatenate([q * jnp.exp(b), att, jnp.zeros(att.shape[:-1] + (LANES - c,), f32)], axis=-1)
    rhs = jnp.concatenate([s0, i_in, jnp.zeros(i_in.shape[:-2] + (LANES - c, i_in.shape[-1]), f32)], axis=-2)
    o = _mm2(lhs, rhs, "nn", APPLY_PASSES) + jnp.sum(q * k, axis=-1, keepdims=True) * i_in
    b_last = _last_row(b)
    s_new = _col_of_row(jnp.exp(b_last)) * s0 + _mm2(k * jnp.exp(b_last - b), i_in, "tn", APPLY_PASSES)
    o = o * lax.rsqrt(jnp.mean(o * o, axis=-1, keepdims=True) + RMS_EPS)
    return o * gnw * _silu(z), s_new


def _hgrn_in_specs(t, dh, col0, rev):
    nc = t // CHUNK
    nh = dh // LANES

    def cidx(c):
        return (nc - 1 - c) if rev else c

    grp = _group(nh)
    specs = [pl.BlockSpec((CHUNK, LANES), functools.partial(lambda g, j, h, c: (cidx(c), col0 + g * nh + h * grp + j), g, j))
             for j in range(grp) for g in range(4)]
    specs.append(pl.BlockSpec((2, grp * LANES), lambda h, c: (0, h)))
    specs.append(pl.BlockSpec((1, grp * LANES), lambda h, c: (0, h)))
    return specs, cidx, grp


def _hgrn_fwd(layer1, proj, lbl, gnw, cat, rwc):
    t, d = cat.shape
    dh = gnw.shape[1]
    nh = dh // LANES
    nc = t // CHUNK
    col0 = rwc // LANES
    specs, _, grp = _hgrn_in_specs(t, dh, col0, False)
    specs.append(pl.BlockSpec(memory_space=pl.ANY))
    assert (d - dh) % (grp * LANES) == 0
    cat_col0 = (d - dh) // (grp * LANES)

    def body(*refs):
        x_refs = refs[:4 * grp]
        lbl_ref, gnw_ref, _, cat_ref, sck_ref, s_s = refs[4 * grp:]
        c = pl.program_id(1)

        @pl.when(c == 0)
        def _():
            s_s[...] = jnp.zeros_like(s_s)

        lanes = [slice(j * LANES, (j + 1) * LANES) for j in range(grp)]
        s0 = s_s[...]
        sck_ref[:, 0] = s0
        out, s_new = _hgrn_chunk(layer1, jnp.stack([lbl_ref[:, ln] for ln in lanes]), jnp.stack([gnw_ref[:, ln] for ln in lanes]),
                                 s0, *(jnp.stack([x_refs[4 * j + g][...] for j in range(grp)]) for g in range(4)))
        for j in range(grp):
            cat_ref[:, lanes[j]] = out[j]
        s_s[...] = s_new

    return pl.pallas_call(
        body, grid=(nh // grp, nc), in_specs=specs,
        out_specs=[pl.BlockSpec((CHUNK, grp * LANES), lambda h, c: (c, cat_col0 + h)),
                   pl.BlockSpec((grp, 1, LANES, LANES), lambda h, c: (h, c, 0, 0))],
        out_shape=[jax.ShapeDtypeStruct((t, d), f32), jax.ShapeDtypeStruct((nh, nc, LANES, LANES), f32)],
        scratch_shapes=[pltpu.VMEM((grp, LANES, LANES), f32)],
        input_output_aliases={4 * grp + 2: 0},
        compiler_params=pltpu.CompilerParams(dimension_semantics=("arbitrary", "arbitrary")),
        name=f"hgrn_fwd_l{int(layer1)}",
    )(*([proj] * (4 * grp)), lbl, gnw, cat)


def _hgrn_bwd(layer1, proj, lbl, gnw, sck, dcat, rwc, dproj):
    t, d = dcat.shape
    dh = gnw.shape[1]
    nh = dh // LANES
    nc = t // CHUNK
    col0 = rwc // LANES
    specs, cidx, grp = _hgrn_in_specs(t, dh, col0, True)
    assert grp == nh and (d - dh) % (grp * LANES) == 0
    cat_col0 = (d - dh) // (grp * LANES)
    specs.append(pl.BlockSpec((grp, 1, LANES, LANES), lambda h, c: (h, cidx(c), 0, 0)))
    specs.append(pl.BlockSpec((CHUNK, grp * LANES), lambda h, c: (cidx(c), cat_col0 + h)))
    specs.append(pl.BlockSpec(memory_space=pl.ANY))

    def body(*refs):
        x_refs = refs[:4 * grp]
        lbl_ref, gnw_ref, sck_ref, do_ref, _, dp_hbm, dlbl_ref, dgnw_ref, ds_s, stage, sems = refs[4 * grp:]
        c = pl.program_id(1)
        slot = c % 2

        def put(s, g, chunk):
            return pltpu.make_async_copy(stage.at[s, g], dp_hbm.at[pl.ds(chunk * CHUNK, CHUNK), pl.ds(rwc + g * dh, dh)],
                                         sems.at[s, g])

        @pl.when(c == 0)
        def _():
            ds_s[...] = jnp.zeros_like(ds_s)
            dlbl_ref[...] = jnp.zeros_like(dlbl_ref)
            dgnw_ref[...] = jnp.zeros_like(dgnw_ref)

        @pl.when(c >= 2)
        def _():
            for g in range(4):
                put(slot, g, 0).wait()

        lanes = [slice(j * LANES, (j + 1) * LANES) for j in range(grp)]
        _, vjp = jax.vjp(functools.partial(_hgrn_chunk, layer1),
                         jnp.stack([lbl_ref[:, ln] for ln in lanes]), jnp.stack([gnw_ref[:, ln] for ln in lanes]), sck_ref[:, 0],
                         *(jnp.stack([x_refs[4 * j + g][...] for j in range(grp)]) for g in range(4)))
        dlbl, dgnw, ds0, dq, df, di, dz = vjp((jnp.stack([do_ref[:, ln] for ln in lanes]), ds_s[...]))
        ds_s[...] = ds0
        for j in range(grp):
            dlbl_ref[:, lanes[j]] += dlbl[j]
            dgnw_ref[:, lanes[j]] += dgnw[j]
            for g, val in enumerate((dq, df, di, dz)):
                stage[slot, g, :, lanes[j]] = val[j].astype(bf16)
        for g in range(4):
            put(slot, g, nc - 1 - c).start()

        @pl.when(c == nc - 1)
        def _():
            for g in range(4):
                put(slot, g, 0).wait()
                if nc >= 2:
                    put(1 - slot, g, 0).wait()

    return pl.pallas_call(
        body, grid=(1, nc), in_specs=specs,
        out_specs=[pl.BlockSpec(memory_space=pl.ANY),
                   pl.BlockSpec((2, grp * LANES), lambda h, c: (0, h)),
                   pl.BlockSpec((1, grp * LANES), lambda h, c: (0, h))],
        out_shape=[jax.ShapeDtypeStruct(dproj.shape, dproj.dtype), jax.ShapeDtypeStruct((2, dh), f32),
                   jax.ShapeDtypeStruct((1, dh), f32)],
        scratch_shapes=[pltpu.VMEM((grp, LANES, LANES), f32), pltpu.VMEM((2, 4, CHUNK, dh), bf16),
                        pltpu.SemaphoreType.DMA((2, 4))],
        input_output_aliases={4 * grp + 4: 0},
        compiler_params=pltpu.CompilerParams(dimension_semantics=("arbitrary", "arbitrary")),
        name=f"hgrn_bwd_l{int(layer1)}",
    )(*([proj] * (4 * grp)), lbl, gnw, sck, dcat, dproj)


def _ln(h, y, w, b):
    u = ALPHA * h + y
    mu = jnp.mean(u, axis=-1, keepdims=True)
    var = jnp.mean(jnp.square(u - mu), axis=-1, keepdims=True)
    return (u - mu) * lax.rsqrt(var + LN_EPS) * w + b


def _row_tile(t):
    return 256 if t % 256 == 0 else t


def _ln_fwd(h, y, w, b):
    t, d = h.shape
    tr = _row_tile(t)

    def body(h_ref, y_ref, w_ref, b_ref, o_ref, o16_ref):
        out = _ln(h_ref[...], y_ref[...], w_ref[...], b_ref[...])
        o_ref[...] = out
        o16_ref[...] = out.astype(bf16)

    row = pl.BlockSpec((tr, d), lambda i: (i, 0))
    vec = pl.BlockSpec((1, d), lambda i: (0, 0))
    return pl.pallas_call(body, grid=(t // tr,), in_specs=[row, row, vec, vec], out_specs=[row, row],
                          out_shape=[jax.ShapeDtypeStruct((t, d), f32), jax.ShapeDtypeStruct((t, d), bf16)],
                          name="ln_fwd")(h, y, w, b)


def _ln_loss_bwd(h, y, w, b, tgt):
    t, d = h.shape
    tr = _row_tile(t)

    def body(h_ref, y_ref, w_ref, b_ref, t_ref, dy_ref, dy16_ref, dw_ref, db_ref, loss_ref):
        @pl.when(pl.program_id(0) == 0)
        def _():
            dw_ref[...] = jnp.zeros_like(dw_ref)
            db_ref[...] = jnp.zeros_like(db_ref)
            loss_ref[...] = jnp.zeros_like(loss_ref)

        out, vjp = jax.vjp(lambda yy, ww, bb: _ln(h_ref[...], yy, ww, bb), y_ref[...], w_ref[...], b_ref[...])
        err = out - t_ref[...]
        loss_ref[...] += 0.5 * jnp.sum(jnp.mean(jnp.square(err), axis=-1, keepdims=True), axis=0, keepdims=True)
        dy, dw, db = vjp(err * (1.0 / d))
        dy_ref[...] = dy
        dy16_ref[...] = dy.astype(bf16)
        dw_ref[...] += dw
        db_ref[...] += db

    row = pl.BlockSpec((tr, d), lambda i: (i, 0))
    vec = pl.BlockSpec((1, d), lambda i: (0, 0))
    return pl.pallas_call(
        body, grid=(t // tr,), in_specs=[row, row, vec, vec, row],
        out_specs=[row, row, vec, vec, pl.BlockSpec((1, LANES), lambda i: (0, 0))],
        out_shape=[jax.ShapeDtypeStruct((t, d), f32), jax.ShapeDtypeStruct((t, d), bf16), jax.ShapeDtypeStruct((1, d), f32),
                   jax.ShapeDtypeStruct((1, d), f32), jax.ShapeDtypeStruct((1, LANES), f32)],
        compiler_params=pltpu.CompilerParams(dimension_semantics=("arbitrary",)), name="ln_loss_bwd")(h, y, w, b, tgt)


def _ln_bwd(h, y, w, b, dout):
    t, d = h.shape
    tr = _row_tile(t)

    def body(h_ref, y_ref, w_ref, b_ref, do_ref, dy_ref, dy16_ref, dw_ref, db_ref):
        @pl.when(pl.program_id(0) == 0)
        def _():
            dw_ref[...] = jnp.zeros_like(dw_ref)
            db_ref[...] = jnp.zeros_like(db_ref)

        _, vjp = jax.vjp(lambda yy, ww, bb: _ln(h_ref[...], yy, ww, bb), y_ref[...], w_ref[...], b_ref[...])
        dy, dw, db = vjp(do_ref[...])
        dy_ref[...] = dy
        dy16_ref[...] = dy.astype(bf16)
        dw_ref[...] += dw
        db_ref[...] += db

    row = pl.BlockSpec((tr, d), lambda i: (i, 0))
    vec = pl.BlockSpec((1, d), lambda i: (0, 0))
    return pl.pallas_call(
        body, grid=(t // tr,), in_specs=[row, row, vec, vec, row], out_specs=[row, row, vec, vec],
        out_shape=[jax.ShapeDtypeStruct((t, d), f32), jax.ShapeDtypeStruct((t, d), bf16),
                   jax.ShapeDtypeStruct((1, d), f32), jax.ShapeDtypeStruct((1, d), f32)],
        compiler_params=pltpu.CompilerParams(dimension_semantics=("arbitrary",)), name="ln_bwd")(h, y, w, b, dout)


def _pick(n, prefs):
    for p in prefs:
        if n % p == 0:
            return p
    return n


def _tile(n, want):
    if n <= want:
        return n
    for cand in range(want - want % LANES, 0, -LANES):
        if n % cand == 0:
            return cand
    return n


def _matmul(a, b, mode, name, tiles, add=None, add_scale=1.0, out_dtype=f32):
    if mode == "nn":
        (m, k), n = a.shape, b.shape[1]
    elif mode == "nt":
        (m, k), n = a.shape, b.shape[0]
    else:
        (k, m), n = a.shape, b.shape[1]
    tm, tn, tk = _tile(m, tiles[0]), _tile(n, tiles[1]), _tile(k, tiles[2])
    nk = k // tk
    cache_a = nk == 1 and a.dtype != bf16 and n // tn > 1

    def body(*refs):
        a_ref, b_ref = refs[0], refs[1]
        add_ref = refs[2] if add is not None else None
        n_in = 3 if add is not None else 2
        o_ref = refs[n_in]
        scratch = refs[n_in + 1:]

        def finish(res):
            if add is not None:
                res = res + add_scale * add_ref[...]
            o_ref[...] = res.astype(out_dtype)

        if cache_a:
            a_bf = scratch[0]

            @pl.when(pl.program_id(1) == 0)
            def _():
                a_bf[...] = a_ref[...].astype(bf16)

            a_val = a_bf[...]
        else:
            a_val = a_ref[...].astype(bf16)
        prod = lax.dot_general(a_val, b_ref[...].astype(bf16), _DIMS[mode], preferred_element_type=f32)
        if nk == 1:
            finish(prod)
        else:
            acc = scratch[-1]
            kk = pl.program_id(2)

            @pl.when(kk == 0)
            def _():
                acc[...] = prod

            @pl.when(kk != 0)
            def _():
                acc[...] += prod

            @pl.when(kk == nk - 1)
            def _():
                finish(acc[...])

    a_shape = (tk, tm) if mode == "tn" else (tm, tk)
    a_spec = pl.BlockSpec(a_shape, (lambda i, j, kk: (kk, i)) if mode == "tn" else (lambda i, j, kk: (i, kk)))
    b_spec = pl.BlockSpec((tn, tk), lambda i, j, kk: (j, kk)) if mode == "nt" else pl.BlockSpec((tk, tn), lambda i, j, kk: (kk, j))
    o_spec = pl.BlockSpec((tm, tn), lambda i, j, kk: (i, j))
    in_specs = [a_spec, b_spec] + ([o_spec] if add is not None else [])
    args = [a, b] + ([add] if add is not None else [])
    scratch_shapes = ([pltpu.VMEM(a_shape, bf16)] if cache_a else []) + ([pltpu.VMEM((tm, tn), f32)] if nk > 1 else [])
    return pl.pallas_call(
        body, grid=(m // tm, n // tn, nk), in_specs=in_specs, out_specs=o_spec,
        out_shape=jax.ShapeDtypeStruct((m, n), out_dtype), scratch_shapes=scratch_shapes,
        compiler_params=pltpu.CompilerParams(dimension_semantics=("parallel", "arbitrary", "arbitrary")),
        name=name,
    )(*args)


def _position():
    return lax.axis_index("x"), lax.axis_index("y"), lax.axis_index("c")


def _flip(pos, k):
    x, y, c = pos
    return (1 - x if k & 4 else x, 1 - y if k & 2 else y, 1 - c if k & 1 else c)


def _index(pos):
    return 4 * pos[0] + 2 * pos[1] + pos[2]


def _all_gather_rows(x, name):
    m_per, n = x.shape

    def body(x_ref, out_ref, send_sems, recv_sems, local_sem):
        me = _position()
        sibling = _flip(me, 1)
        chips = (2, 4, 6)

        def rows(pos):
            return out_ref.at[pl.ds(_index(pos) * m_per, m_per), :]

        def copy(sem, block, to, src=None):
            return pltpu.make_async_remote_copy(
                src_ref=rows(block) if src is None else src, dst_ref=rows(block),
                send_sem=send_sems.at[sem], recv_sem=recv_sems.at[sem], device_id=to, device_id_type=MESH)

        mine = pltpu.make_async_copy(x_ref, rows(me), local_sem)
        mine.start()
        first = [copy(0, me, sibling, src=x_ref)]
        first += [copy(1 + j, me, _flip(me, k), src=x_ref) for j, k in enumerate(chips)]
        for cp in first:
            cp.start()
        passed = [copy(4 + j, _flip(me, k), sibling) for j, k in enumerate(chips)]
        for j, k in enumerate(chips):
            copy(1 + j, _flip(me, k), me).wait_recv()
            passed[j].start()
        copy(0, sibling, me).wait_recv()
        for j, k in enumerate(chips):
            copy(4 + j, _flip(sibling, k), me).wait_recv()
        for cp in first + passed:
            cp.wait_send()
        mine.wait()

    return pl.pallas_call(
        body, out_shape=jax.ShapeDtypeStruct((N_DEV * m_per, n), x.dtype),
        in_specs=[pl.BlockSpec(memory_space=pl.ANY)], out_specs=pl.BlockSpec(memory_space=pl.ANY),
        scratch_shapes=[pltpu.SemaphoreType.DMA((7,)), pltpu.SemaphoreType.DMA((7,)), pltpu.SemaphoreType.DMA(())],
        name=name,
    )(x)


def _split_start(srcs, lands, plan, n_copies, name, after=()):
    n_arr = len(srcs)
    n_after = len(after)
    hbm = pl.BlockSpec(memory_space=pltpu.HBM)
    sem = pl.BlockSpec(memory_space=pltpu.SEMAPHORE)

    def body(*refs):
        src_refs, land_refs = refs[:n_arr], refs[n_arr:2 * n_arr]
        outs_at = 2 * n_arr + n_after
        send_sems, recv_sems = refs[outs_at:outs_at + n_arr], refs[outs_at + n_arr:outs_at + 2 * n_arr]
        token = refs[-1]
        me = _position()
        for i in range(n_arr):
            for j, (src, dst, peer, _) in enumerate(plan(i, src_refs[i], land_refs[i], me)):
                pltpu.make_async_remote_copy(src_ref=src, dst_ref=dst, send_sem=send_sems[i].at[j], recv_sem=recv_sems[i].at[j],
                                             device_id=peer, device_id_type=MESH).start()
        token[...] = jnp.zeros_like(token)

    outs = pl.pallas_call(
        body, name=name,
        out_shape=([pltpu.SemaphoreType.DMA((n_copies,))] * (2 * n_arr)
                   + [pltpu.HBM(a.shape, a.dtype) for a in list(srcs) + list(lands)]
                   + [jax.ShapeDtypeStruct((8, LANES), f32)]),
        in_specs=[hbm] * (2 * n_arr) + [pl.BlockSpec(memory_space=pl.ANY)] * n_after,
        out_specs=[sem] * (2 * n_arr) + [hbm] * (2 * n_arr) + [pl.BlockSpec(memory_space=pltpu.VMEM)],
        input_output_aliases={i: 2 * n_arr + i for i in range(2 * n_arr)},
        compiler_params=pltpu.CompilerParams(has_side_effects=pltpu.SideEffectType.DATAFLOW_SIDE_EFFECTING),
    )(*[pltpu.with_memory_space_constraint(a, pltpu.HBM) for a in list(srcs) + list(lands)], *after)
    return (outs[:n_arr], outs[n_arr:2 * n_arr], outs[2 * n_arr:3 * n_arr], outs[3 * n_arr:4 * n_arr], outs[-1])


def _split_wait(started, plan, after, name):
    send_sems, recv_sems, srcs, lands, _ = started
    n_arr = len(srcs)
    hbm = pl.BlockSpec(memory_space=pltpu.HBM)
    sem = pl.BlockSpec(memory_space=pltpu.SEMAPHORE)

    def body(*refs):
        src_refs, land_refs = refs[:n_arr], refs[n_arr:2 * n_arr]
        s_sems, r_sems = refs[2 * n_arr:3 * n_arr], refs[3 * n_arr:4 * n_arr]
        me = _position()
        for i in range(n_arr):
            for j, (src, _, peer, arrival) in enumerate(plan(i, src_refs[i], land_refs[i], me)):
                cp = pltpu.make_async_remote_copy(src_ref=src, dst_ref=arrival, send_sem=s_sems[i].at[j], recv_sem=r_sems[i].at[j],
                                                  device_id=peer, device_id_type=MESH)
                cp.wait_send()
                cp.wait_recv()

    outs = pl.pallas_call(
        body, name=name,
        out_shape=[pltpu.HBM(a.shape, a.dtype) for a in list(srcs) + list(lands)],
        in_specs=[hbm] * (2 * n_arr) + [sem] * (2 * n_arr) + [pl.BlockSpec(memory_space=pl.ANY)],
        out_specs=[hbm] * (2 * n_arr),
        input_output_aliases={i: i for i in range(2 * n_arr)},
        compiler_params=pltpu.CompilerParams(has_side_effects=pltpu.SideEffectType.DATAFLOW_SIDE_EFFECTING),
    )(*srcs, *lands, *send_sems, *recv_sems, after)
    return outs[:n_arr], outs[n_arr:]


_GATHER_FLIPS = (1, 2, 4, 6)


def _gather_plan(i, src_ref, land_ref, me):
    m = src_ref.shape[0]

    def rows(pos):
        return land_ref.at[pl.ds(_index(pos) * m, m), :]

    return [(src_ref, rows(me), _flip(me, k), rows(_flip(me, k))) for k in _GATHER_FLIPS]


def _gather_forward(lands, name):
    n_arr = len(lands)
    chips = (2, 4, 6)

    def body(*refs):
        out_refs = refs[n_arr:2 * n_arr]
        send_sems, recv_sems = refs[2 * n_arr:]
        me = _position()
        sibling = _flip(me, 1)
        sends, arrivals = [], []
        for i, out_ref in enumerate(out_refs):
            m = out_ref.shape[0] // N_DEV

            def copy(pos, j):
                blk = out_ref.at[pl.ds(_index(pos) * m, m), :]
                return pltpu.make_async_remote_copy(src_ref=blk, dst_ref=blk, send_sem=send_sems.at[3 * i + j],
                                                    recv_sem=recv_sems.at[3 * i + j], device_id=sibling, device_id_type=MESH)

            for j, k in enumerate(chips):
                sends.append(copy(_flip(me, k), j))
                arrivals.append(copy(_flip(sibling, k), j))
        for cp in sends:
            cp.start()
        for cp in arrivals:
            cp.wait_recv()
        for cp in sends:
            cp.wait_send()

    anyspec = pl.BlockSpec(memory_space=pl.ANY)
    return pl.pallas_call(
        body, out_shape=[jax.ShapeDtypeStruct(a.shape, a.dtype) for a in lands],
        in_specs=[anyspec] * n_arr, out_specs=[anyspec] * n_arr, input_output_aliases={i: i for i in range(n_arr)},
        scratch_shapes=[pltpu.SemaphoreType.DMA((3 * n_arr,))] * 2, name=name,
    )(*lands)


def _chips_plan(i, src_ref, land_ref, me):
    m = src_ref.shape[0] // 4
    plan = []
    for j, k in enumerate((2, 4, 6)):
        peer = _flip(me, k)
        plan.append((src_ref.at[pl.ds((2 * peer[0] + peer[1]) * m, m), :], land_ref.at[j], peer, land_ref.at[j]))
    return plan


def _exchange_siblings(gs, name):
    n_arr = len(gs)

    def body(*refs):
        g_refs, out_refs = refs[:n_arr], refs[n_arr:2 * n_arr]
        send_sems, recv_sems = refs[2 * n_arr:]
        me = _position()
        c = me[2]
        sibling = _flip(me, 1)
        copies = []
        for i, (g_ref, out_ref) in enumerate(zip(g_refs, out_refs)):
            m_per = g_ref.shape[0] // N_DEV
            for q in range(4):
                copies.append(pltpu.make_async_remote_copy(
                    src_ref=g_ref.at[pl.ds((2 * q + 1 - c) * m_per, m_per), :], dst_ref=out_ref.at[q],
                    send_sem=send_sems.at[4 * i + q], recv_sem=recv_sems.at[4 * i + q],
                    device_id=sibling, device_id_type=MESH))
        for cp in copies:
            cp.start()
        for cp in copies:
            cp.wait_recv()
        for cp in copies:
            cp.wait_send()

    anyspec = pl.BlockSpec(memory_space=pl.ANY)
    return pl.pallas_call(
        body, out_shape=[jax.ShapeDtypeStruct((4, g.shape[0] // N_DEV, g.shape[1]), g.dtype) for g in gs],
        in_specs=[anyspec] * n_arr, out_specs=[anyspec] * n_arr,
        scratch_shapes=[pltpu.SemaphoreType.DMA((4 * n_arr,))] * 2, name=name,
    )(*gs)


def _sum_with_sibling(g, recv, name):
    m = g.shape[0] // N_DEV
    n = g.shape[1]
    tr = _pick(m, (208, 128, 64, 32, 16))
    nt = m // tr

    def body(g_ref, r_ref, o_ref):
        c = lax.axis_index("c")
        own = jnp.where(c == 0, g_ref[0, 0].astype(f32), g_ref[0, 1].astype(f32))
        o_ref[...] = (own + r_ref[0].astype(f32)).astype(o_ref.dtype)

    return pl.pallas_call(
        body, grid=(4, nt),
        in_specs=[pl.BlockSpec((1, 2, tr, n), lambda q, i: (q, 0, i, 0)), pl.BlockSpec((1, tr, n), lambda q, i: (q, i, 0))],
        out_specs=pl.BlockSpec((tr, n), lambda q, i: (q * nt + i, 0)),
        out_shape=jax.ShapeDtypeStruct((4 * m, n), bf16), name=name,
    )(g.reshape(4, 2, m, n), recv)


def _sum_with_chips(h, recv, name):
    m = h.shape[0] // 4
    n = h.shape[1]
    tr = _pick(m, (208, 128, 64, 32, 16))

    def body(h_ref, r_ref, o_ref):
        my_q = 2 * lax.axis_index("x") + lax.axis_index("y")
        own = h_ref[0].astype(f32)
        for q in range(1, 4):
            own = jnp.where(my_q == q, h_ref[q].astype(f32), own)
        o_ref[...] = ((own + r_ref[0].astype(f32)) + r_ref[1].astype(f32)) + r_ref[2].astype(f32)

    return pl.pallas_call(
        body, grid=(m // tr,),
        in_specs=[pl.BlockSpec((4, tr, n), lambda i: (0, i, 0)), pl.BlockSpec((3, tr, n), lambda i: (0, i, 0))],
        out_specs=pl.BlockSpec((tr, n), lambda i: (i, 0)), out_shape=jax.ShapeDtypeStruct((m, n), f32), name=name,
    )(h.reshape(4, m, n), recv)


def _sum_slots(parts, name):
    n_slot, m, n = parts.shape
    tr = _pick(m, (208, 128, 64, 32, 16, 8))

    def body(p_ref, o_ref):
        acc = p_ref[0]
        for s in range(1, n_slot):
            acc = acc + p_ref[s]
        o_ref[...] = acc

    return pl.pallas_call(
        body, grid=(m // tr,), in_specs=[pl.BlockSpec((n_slot, tr, n), lambda i: (0, i, 0))],
        out_specs=pl.BlockSpec((tr, n), lambda i: (i, 0)), out_shape=jax.ShapeDtypeStruct((m, n), parts.dtype), name=name,
    )(parts)


def _reduce_scatter_begin(gs, name):
    from_sibling = _exchange_siblings(gs, "rs_d2d_" + name)
    chip_sums = [_sum_with_sibling(g, r, f"rs_sum2_{name}_{i}") for i, (g, r) in enumerate(zip(gs, from_sibling))]
    lands = [lax.empty((3, h.shape[0] // 4, h.shape[1]), h.dtype) for h in chip_sums]
    return _split_start(chip_sums, lands, _chips_plan, 3, "rs_ici_start_" + name)


def _reduce_scatter_end(started, after, name):
    chip_sums, from_chips = _split_wait(started, _chips_plan, after, "rs_ici_wait_" + name)
    return [_sum_with_chips(h, r, f"rs_sum4_{name}_{i}") for i, (h, r) in enumerate(zip(chip_sums, from_chips))]


def _adamw_update(w, g, m, v):
    mm = ADAM_B1 * m + (1.0 - ADAM_B1) * g
    vv = ADAM_B2 * v + (1.0 - ADAM_B2) * jnp.square(g)
    m_hat = mm / (1.0 - ADAM_B1 ** ADAM_STEP)
    v_hat = vv / (1.0 - ADAM_B2 ** ADAM_STEP)
    return -ADAM_LR * (m_hat / (jnp.sqrt(v_hat) + ADAM_EPS) + ADAM_WD * w), mm, vv


def _adamw_many(ws, gs, ms, vs, name):
    k = len(ws)
    shapes = [w.shape for w in ws]
    flat = [[a.reshape(-1, a.shape[-1]) for a in group] for group in (ws, gs, ms, vs)]

    def body(*refs):
        for i in range(k):
            d, mm, vv = _adamw_update(*(refs[j * k + i][...] for j in range(4)))
            refs[4 * k + i][...] = d
            refs[5 * k + i][...] = mm
            refs[6 * k + i][...] = vv

    outs = pl.pallas_call(
        body, out_shape=[jax.ShapeDtypeStruct(a.shape, f32) for a in flat[0]] * 3, name=name,
    )(*flat[0], *flat[1], *flat[2], *flat[3])
    return tuple([outs[j * k + i].reshape(shapes[i]) for i in range(k)] for j in range(3))


def _adamw(w, g, m, v, name):
    shape = w.shape
    n = shape[-1]
    r = w.size // n
    w2, g2, m2, v2 = (a.reshape(r, n) for a in (w, g, m, v))
    tr = _pick(r, (256, 208, 128, 64, 32, 16, 8))

    def body(w_ref, g_ref, m_ref, v_ref, d_ref, mo_ref, vo_ref):
        d_ref[...], mo_ref[...], vo_ref[...] = _adamw_update(w_ref[...], g_ref[...], m_ref[...], v_ref[...])

    spec = pl.BlockSpec((tr, n), lambda i: (i, 0))
    outs = pl.pallas_call(
        body, grid=(r // tr,), in_specs=[spec] * 4, out_specs=[spec] * 3,
        out_shape=[jax.ShapeDtypeStruct((r, n), f32)] * 3, name=name,
    )(w2, g2, m2, v2)
    return tuple(o.reshape(shape) for o in outs)


_SMALL = ("shift_mu", "w_decay0", "a0", "k_k", "k_a", "r_k", "ln_x_w", "ln_x_b", "v_mix0", "lb_logits",
          "g_norm_w", "ln_w", "ln_b")
_NAMES = ("w_in", "shift_mu", "w_decay0", "w_decay_up", "a0", "a_up", "k_k", "k_a", "r_k", "ln_x_w", "ln_x_b",
          "v_mix0", "v_mix_down", "v_mix_up", "lb_logits", "g_norm_w", "w_out", "ln_w", "ln_b")


def _pad_rows(a, rows, at_end):
    z = jnp.zeros((rows - a.shape[0], a.shape[1]), a.dtype)
    return jnp.concatenate([a, z] if at_end else [z, a], axis=0)


def kernel(x, w_in, shift_mu, w_decay0, w_decay_up, a0, a_up, k_k, k_a, r_k, ln_x_w, ln_x_b, v_mix0, v_mix_down, v_mix_up, lb_logits, g_norm_w, w_out, ln_w, ln_b, loss_target, m_w_in, m_shift_mu, m_w_decay0, m_w_decay_up, m_a0, m_a_up, m_k_k, m_k_a, m_r_k, m_ln_x_w, m_ln_x_b, m_v_mix0, m_v_mix_down, m_v_mix_up, m_lb_logits, m_g_norm_w, m_w_out, m_ln_w, m_ln_b, v_w_in, v_shift_mu, v_w_decay0, v_w_decay_up, v_a0, v_a_up, v_k_k, v_k_a, v_r_k, v_ln_x_w, v_ln_x_b, v_v_mix0, v_v_mix_down, v_v_mix_up, v_lb_logits, v_g_norm_w, v_w_out, v_ln_w, v_ln_b):
    weights = dict(w_in=w_in, shift_mu=shift_mu, w_decay0=w_decay0, w_decay_up=w_decay_up, a0=a0, a_up=a_up, k_k=k_k,
                   k_a=k_a, r_k=r_k, ln_x_w=ln_x_w, ln_x_b=ln_x_b, v_mix0=v_mix0, v_mix_down=v_mix_down,
                   v_mix_up=v_mix_up, lb_logits=lb_logits, g_norm_w=g_norm_w, w_out=w_out, ln_w=ln_w, ln_b=ln_b)
    mom1 = dict(w_in=m_w_in, shift_mu=m_shift_mu, w_decay0=m_w_decay0, w_decay_up=m_w_decay_up, a0=m_a0, a_up=m_a_up,
                k_k=m_k_k, k_a=m_k_a, r_k=m_r_k, ln_x_w=m_ln_x_w, ln_x_b=m_ln_x_b, v_mix0=m_v_mix0,
                v_mix_down=m_v_mix_down, v_mix_up=m_v_mix_up, lb_logits=m_lb_logits, g_norm_w=m_g_norm_w,
                w_out=m_w_out, ln_w=m_ln_w, ln_b=m_ln_b)
    mom2 = dict(w_in=v_w_in, shift_mu=v_shift_mu, w_decay0=v_w_decay0, w_decay_up=v_w_decay_up, a0=v_a0, a_up=v_a_up,
                k_k=v_k_k, k_a=v_k_a, r_k=v_r_k, ln_x_w=v_ln_x_w, ln_x_b=v_ln_x_b, v_mix0=v_v_mix0,
                v_mix_down=v_v_mix_down, v_mix_up=v_v_mix_up, lb_logits=v_lb_logits, g_norm_w=v_g_norm_w,
                w_out=v_w_out, ln_w=v_ln_w, ln_b=v_ln_b)
    assert x.shape[0] == 1 and w_in.shape[0] == DEPTH
    t, d = x.shape[1], x.shape[2]
    dr = w_decay0.shape[1]
    dh = g_norm_w.shape[1]
    rank_w, rank_a, rank_v = w_decay_up.shape[1], a_up.shape[1], v_mix_up.shape[1]
    rwc = 4 * dr + rank_w + rank_a
    assert rank_w + rank_a == LANES and rank_v <= LANES and dr + dh == d
    assert t % CHUNK == 0 and dr % LANES == 0 and dh % LANES == 0 and shift_mu.shape[1] == rwc
    n_pair = dr // LANES
    me = _index(_position())

    win_t = [_all_gather_rows(w_in[0].T.astype(bf16), "ag_w_in_0"), None]
    wout = [None, None]
    shard = dr // N_DEV
    pack = jnp.concatenate([w_decay_up[0], w_decay_up[1], a_up[0], a_up[1], v_mix_up[0], v_mix_down[0].T], axis=0)
    pack = _all_gather_rows(pack, "ag_small")
    late_blocks = [w_out[0].astype(bf16), w_in[1].T.astype(bf16), w_out[1].astype(bf16)]
    late_lands = [lax.dynamic_update_slice(lax.empty((N_DEV * blk.shape[0], blk.shape[1]), bf16), blk, (me * blk.shape[0], 0))
                  for blk in late_blocks]
    late_gather = _split_start(late_blocks, late_lands, _gather_plan, len(_GATHER_FLIPS), "ag_late_start",
                               after=(win_t[0], pack))
    pack = jnp.transpose(pack.reshape(N_DEV, -1, shard), (1, 0, 2)).reshape(-1, dr)
    offs = [0, rank_w, 2 * rank_w, 2 * rank_w + rank_a, 2 * rank_w + 2 * rank_a, 2 * rank_w + 2 * rank_a + rank_v,
            2 * rank_w + 2 * rank_a + 2 * rank_v]
    wdu_f = [pack[offs[0]:offs[1]], pack[offs[1]:offs[2]]]
    aup_f = [pack[offs[2]:offs[3]], pack[offs[3]:offs[4]]]
    vup_f = pack[offs[4]:offs[5]]
    vdown_f = pack[offs[5]:offs[6]].T

    def after_start(a, started):
        return a + started[-1][0:1, 0:1]

    def rwkv_params(l):
        mu = after_start(shift_mu[0:1], late_gather) if l == 0 else shift_mu[l:l + 1]
        prm = [mu, w_decay0[l:l + 1], a0[l:l + 1], _pad_rows(wdu_f[l], LANES, True),
               _pad_rows(aup_f[l], LANES, False)]
        if l == 1:
            prm += [v_mix0[0:1], _pad_rows(vdown_f.T, LANES, True).T, _pad_rows(vup_f, LANES, True)]
        rows = jnp.stack([k_k[l], k_a[l], r_k[l], ln_x_w[l], ln_x_b[l]] + [jnp.zeros((dr,), f32)] * 3, axis=0)
        pp = jnp.transpose(rows.reshape(8, n_pair, LANES), (1, 0, 2))
        return tuple(prm), pp

    h = x[0]
    h16 = h.astype(bf16)
    tgt = loss_target[0]
    saved = []
    vfirst = None
    for l in range(DEPTH):
        prm, pp = rwkv_params(l)
        proj = _matmul(h16, win_t[l], "nt", f"mm_proj_{l}", (2048, 640, 2048))
        if l == 0:
            cat, vfirst, mck = _rwkv_fwd(False, proj, None, prm, pp, d)
        else:
            cat, mck = _rwkv_fwd(True, proj, vfirst, prm, pp, d)
        cat, sck = _hgrn_fwd(l == 1, proj, lb_logits, g_norm_w[l:l + 1], cat, rwc)
        if l == 0:
            _, arrived = _split_wait(late_gather, _gather_plan, cat, "ag_late_wait")
            wout[0], win_t[1], wout[1] = _gather_forward(arrived, "ag_late_forward")
        y = _matmul(cat, wout[l], "nn", f"mm_out_{l}", (1024, 1024, 2048))
        saved.append((h, h16, proj, prm, pp, mck, sck, cat, y))
        if l < DEPTH - 1:
            h, h16 = _ln_fwd(h, y, ln_w[l:l + 1], ln_b[l:l + 1])
        else:
            top = _ln_loss_bwd(h, y, ln_w[l:l + 1], ln_b[l:l + 1], tgt)
    loss = lax.psum(top[4][0, 0], ("x", "y", "c"))

    grads = {}
    big = {}
    dvfirst = None
    d_lbl = None
    rs_started = {}
    for l in reversed(range(DEPTH)):
        h_l, h16_l, proj, prm, pp, mck, sck, cat, y = saved[l]
        if l == DEPTH - 1:
            dy, dy16, g_ln_w, g_ln_b = top[:4]
        else:
            dy, dy16, g_ln_w, g_ln_b = _ln_bwd(h_l, y, after_start(ln_w[l:l + 1], rs_started[l + 1]), ln_b[l:l + 1], dh_out)
        dcat = _matmul(dy16, wout[l], "nt", f"mm_dcat_{l}", (1024, 1024, 2048))
        big[("w_out", l)] = _matmul(cat, dy16, "tn", f"mm_dwout_{l}", (512, 2048, 2048), out_dtype=bf16)
        if l == 1:
            outs = _rwkv_bwd(True, proj, vfirst, prm, pp, mck, dcat, None)
            dproj_r, dvfirst = outs[0], outs[1]
            dprm, dpp = outs[2:-1], outs[-1]
        else:
            outs = _rwkv_bwd(False, proj, None, prm, pp, mck, dcat, dvfirst)
            dproj_r = outs[0]
            dprm, dpp = outs[1:-1], outs[-1]
        dproj, dlbl_l, dgnw = _hgrn_bwd(l == 1, proj, lb_logits, g_norm_w[l:l + 1], sck, dcat, rwc, dproj_r)
        big[("w_in", l)] = _matmul(dproj, h16_l, "tn", f"mm_dwin_{l}", (640, 2048, 2048), out_dtype=bf16)
        sharded = [dprm[3][:rank_w].T, dprm[4][rank_w:].T]
        if l == 1:
            sharded += [dprm[6][:, :rank_v], dprm[7][:rank_v].T,
                        jnp.zeros((dr, LANES - 2 * rank_v), f32)]
        sharded = jnp.concatenate(sharded, axis=1).astype(bf16)
        rs_started[l] = _reduce_scatter_begin([big[("w_in", l)], big[("w_out", l)], sharded], f"l{l}")
        dy_res = after_start(dy, rs_started[l]) if l == 0 else dy
        dh_out = _matmul(dproj, win_t[l], "nn", f"mm_dh_{l}", (1024, 1024, 1664), add=dy_res, add_scale=ALPHA)
        dpp = jnp.transpose(dpp, (1, 0, 2)).reshape(8, dr)
        grads[l] = dict(shift_mu=dprm[0][0], w_decay0=dprm[1][0], a0=dprm[2][0],
                        k_k=dpp[0], k_a=dpp[1], r_k=dpp[2], ln_x_w=dpp[3], ln_x_b=dpp[4],
                        g_norm_w=dgnw[0], ln_w=g_ln_w[0], ln_b=g_ln_b[0])
        if l == 1:
            grads[l].update(v_mix0=dprm[5][0])
            d_lbl = dlbl_l
    grad_x = dh_out[None]

    def both(name):
        return jnp.stack([grads[0][name], grads[1][name]])

    small = dict(shift_mu=both("shift_mu"), w_decay0=both("w_decay0"), a0=both("a0"), k_k=both("k_k"), k_a=both("k_a"),
                 r_k=both("r_k"), ln_x_w=both("ln_x_w"), ln_x_b=both("ln_x_b"), v_mix0=grads[1]["v_mix0"][None],
                 lb_logits=d_lbl, g_norm_w=both("g_norm_w"), ln_w=both("ln_w"), ln_b=both("ln_b"))
    flat = jnp.concatenate([small[nm].reshape(-1) for nm in _SMALL])
    n_flat = flat.shape[0]
    rows = -(-n_flat // (8 * LANES)) * 8
    flat = jnp.concatenate([flat, jnp.zeros((rows * LANES - n_flat,), f32)]).reshape(rows, LANES)
    total = _sum_slots(_all_gather_rows(flat, "ag_small_grads").reshape(N_DEV, rows, LANES), "sum_small_grads").reshape(-1)
    gsm = {}
    off = 0
    for nm in _SMALL:
        size = small[nm].size
        gsm[nm] = total[off:off + size].reshape(small[nm].shape)
        off += size
    reduced = {1: _reduce_scatter_end(rs_started[1], dh_out, "l1")}
    reduced[0] = _reduce_scatter_end(rs_started[0], total, "l0")
    g_w_in_t = jnp.stack([reduced[l][0] for l in range(DEPTH)])
    gsm["w_in"] = jnp.transpose(g_w_in_t, (0, 2, 1))
    gsm["w_out"] = jnp.stack([reduced[l][1] for l in range(DEPTH)])
    gsm["w_decay_up"] = jnp.stack([reduced[l][2][:, :rank_w].T for l in range(DEPTH)])
    gsm["a_up"] = jnp.stack([reduced[l][2][:, rank_w:rank_w + rank_a].T for l in range(DEPTH)])
    gsm["v_mix_down"] = reduced[1][2][:, LANES:LANES + rank_v][None]
    gsm["v_mix_up"] = reduced[1][2][:, LANES + rank_v:LANES + 2 * rank_v].T[None]

    deltas, new_m, new_v = {}, {}, {}
    swap = lambda a: jnp.transpose(a, (0, 2, 1))
    deltas["w_in"], new_m["w_in"], new_v["w_in"] = (
        swap(a) for a in _adamw(swap(w_in), g_w_in_t, swap(m_w_in), swap(v_w_in), "adamw_w_in"))
    deltas["w_out"], new_m["w_out"], new_v["w_out"] = _adamw(w_out, gsm["w_out"], m_w_out, v_w_out, "adamw_w_out")
    rest = [nm for nm in _NAMES if nm not in ("w_in", "w_out")]
    d_rest, m_rest, v_rest = _adamw_many([weights[nm] for nm in rest], [gsm[nm] for nm in rest],
                                         [mom1[nm] for nm in rest], [mom2[nm] for nm in rest], "adamw_small")
    for i, nm in enumerate(rest):
        deltas[nm], new_m[nm], new_v[nm] = d_rest[i], m_rest[i], v_rest[i]
    return (loss, grad_x, *[gsm[nm] for nm in _NAMES], *[deltas[nm] for nm in _NAMES],
            *[new_m[nm] for nm in _NAMES], *[new_v[nm] for nm in _NAMES])
```

```python
import functools

import jax
import jax.numpy as jnp
from jax import lax
from jax.experimental import pallas as pl
from jax.experimental.pallas import tpu as pltpu

f32 = jnp.float32
bf16 = jnp.bfloat16

N_DEV = 8
CHUNK = 64
LANES = 128
RWKV_HEAD = 64
DEPTH = 2
ALPHA = (2 * DEPTH) ** 0.25
LN_EPS = 1e-5
GN_EPS = 64e-5
RMS_EPS = 1e-5
LB_FLOOR = 1e-30
ADAM_LR, ADAM_B1, ADAM_B2, ADAM_EPS, ADAM_WD, ADAM_STEP = 0.001, 0.9, 0.999, 1e-08, 0.01, 10
MESH = pl.DeviceIdType.MESH


def _iota(shape, d):
    return lax.broadcasted_iota(jnp.int32, shape, d)


_DIMS = {"nn": (((1,), (0,)), ((), ())), "nt": (((1,), (1,)), ((), ())), "tn": (((0,), (0,)), ((), ()))}
_BATCH_DIMS = {"nn": (((2,), (1,)), ((0,), (0,))), "nt": (((2,), (2,)), ((0,), (0,))), "tn": (((1,), (1,)), ((0,), (0,)))}
_K_AXES = {"nn": (-1, -2), "nt": (-1, -1), "tn": (-2, -2)}


def _mxu(a, b, mode):
    return lax.dot_general(a, b, (_BATCH_DIMS if a.ndim == 3 else _DIMS)[mode], preferred_element_type=f32)


def _split(x):
    hi = x.astype(bf16)
    return hi, (x - hi.astype(f32)).astype(bf16)


def _mm2_impl(a, b, mode, passes=3):
    ah, al = _split(a)
    if passes == 3:
        bh, bl = _split(b)
        lhs, rhs = [ah, ah, al], [bh, bl, bh]
    else:
        bh = b.astype(bf16)
        lhs, rhs = [ah, al], [bh, bh]
    ka, kb = _K_AXES[mode]
    k = a.shape[ka]
    if k % (LANES if -1 in (ka, kb) else 16) == 0:
        return _mxu(jnp.concatenate(lhs, axis=ka), jnp.concatenate(rhs, axis=kb), mode)
    out = _mxu(lhs[0], rhs[0], mode)
    for x, y in zip(lhs[1:], rhs[1:]):
        out = out + _mxu(x, y, mode)
    return out


@functools.partial(jax.custom_vjp, nondiff_argnums=(2, 3))
def _mm2(a, b, mode, passes=3):
    return _mm2_impl(a, b, mode, passes)


def _mm2_fwd(a, b, mode, passes):
    return _mm2_impl(a, b, mode, passes), (a, b)


def _mm2_bwd(mode, passes, res, g):
    a, b = res
    if mode == "nn":
        return _mm2_impl(g, b, "nt", passes), _mm2_impl(a, g, "tn", passes)
    if mode == "nt":
        return _mm2_impl(g, b, "nn", passes), _mm2_impl(g, a, "tn", passes)
    return _mm2_impl(b, g, "nt", passes), _mm2_impl(a, g, "nn", passes)


_mm2.defvjp(_mm2_fwd, _mm2_bwd)

TRI_PASSES = 2
APPLY_PASSES = 2


def _const_impl(cm, x, mode):
    hi, lo = _split(x)
    if mode in ("r", "rt"):
        shape = x.shape
        hi, lo = hi.reshape(-1, shape[-1]), lo.reshape(-1, shape[-1])
        dims = "nn" if mode == "r" else "nt"
        out = _mxu(hi, cm, dims) + _mxu(lo, cm, dims)
        return out.reshape(shape[:-1] + (out.shape[-1],))
    if x.ndim == 3:
        cm = jnp.broadcast_to(cm, (x.shape[0],) + cm.shape)
    return _mxu(cm, hi, mode) + _mxu(cm, lo, mode)


@jax.custom_vjp
def _const_left(cm, x):
    return _const_impl(cm, x, "nn")


_const_left.defvjp(lambda cm, x: (_const_impl(cm, x, "nn"), cm),
                   lambda cm, g: (jnp.zeros_like(cm), _const_impl(cm, g, "tn")))


@jax.custom_vjp
def _const_right(x, cm):
    return _const_impl(cm, x, "r")


_const_right.defvjp(lambda x, cm: (_const_impl(cm, x, "r"), cm),
                    lambda cm, g: (_const_impl(cm, g, "rt"), jnp.zeros_like(cm)))


def _tri_inv(a):
    n = a.shape[-1]
    tm = (_iota((n, n), 0) == _iota((n, n), 1)).astype(f32) + a
    ak = a
    for _ in range(5):
        ak = _mm2_impl(ak, ak, "nn", TRI_PASSES)
        tm = tm + _mm2_impl(tm, ak, "nn", TRI_PASSES)
    return tm


@jax.custom_vjp
def _tri_solve(tm, a, x):
    del a
    return _mm2_impl(tm, x, "nn")


def _tri_solve_fwd(tm, a, x):
    u = _mm2_impl(tm, x, "nn")
    return u, (tm, u)


def _tri_solve_bwd(res, du):
    tm, u = res
    dx = _mm2_impl(tm, du, "tn")
    return jnp.zeros_like(tm), _mm2_impl(dx, u, "nt"), dx


_tri_solve.defvjp(_tri_solve_fwd, _tri_solve_bwd)


def _col_of_row(row_vec):
    n = row_vec.shape[-1]
    eye = _iota((n, n), 0) == _iota((n, n), 1)
    return jnp.sum(jnp.where(eye, jnp.broadcast_to(row_vec, row_vec.shape[:-2] + (n, n)), 0.0), axis=-1, keepdims=True)


def _softplus(x):
    return jnp.maximum(x, 0.0) + jnp.log1p(jnp.exp(-jnp.abs(x)))


def _log_sigmoid(x):
    return -_softplus(-x)


def _logaddexp(a, b):
    return jnp.maximum(a, b) + jnp.log1p(jnp.exp(-jnp.abs(a - b)))


def _silu(x):
    return x * jax.nn.sigmoid(x)


def _tril(c, strict):
    r, s = _iota((c, c), 0), _iota((c, c), 1)
    return (r > s) if strict else (r >= s)


def _last_row(a):
    c = a.shape[-2]
    return jnp.sum(jnp.where(_iota(a.shape, a.ndim - 2) == c - 1, a, 0.0), axis=-2, keepdims=True)


def _rwkv_pre(layer1, prm, y, prev, vf):
    c = y.shape[0]
    if layer1:
        mu, w0, a0, wup, aup, v0, vdown, vup = prm
    else:
        mu, w0, a0, wup, aup = prm
    dr = w0.shape[1]
    shift = (_iota((c, c), 0) == _iota((c, c), 1) + 1).astype(bf16)
    y_prev = _const_left(shift, y) + jnp.where(_iota((c, 1), 0) == 0, prev, 0.0)
    rw = y + mu * (y_prev - y)
    r, k, v, z = (rw[:, i * dr:(i + 1) * dr] for i in range(4))
    wdad = rw[:, 4 * dr:4 * dr + LANES]
    w_raw = w0 + _mm2(jnp.tanh(wdad), wup, "nn")
    lw = -jnp.exp(-_softplus(-w_raw) - 0.5)
    asig = jax.nn.sigmoid(a0 + _mm2(wdad, aup, "nn"))
    if layer1:
        v = v + (vf - v) * jax.nn.sigmoid(v0 + _mm2(_mm2(v, vdown, "nn"), vup, "nn"))
    return r, k, v, z, lw, asig


def _rwkv_pair(pp, m0, xs, tm=None):
    kkw, kaw, rkw, gnw, gnb = pp
    r, k, v, z, lw, asig = xs
    c = r.shape[-2]
    n2 = 2 * c
    lane = _iota((1, LANES), 1)
    mh0, mh1 = (lane < RWKV_HEAD).astype(f32), (lane >= RWKV_HEAD).astype(f32)
    same_head = _iota((LANES, LANES), 0) // RWKV_HEAD == _iota((LANES, LANES), 1) // RWKV_HEAD
    g = same_head.astype(bf16)

    def seg(x):
        return _const_right(x, g)

    def stack(x):
        return jnp.concatenate([x * mh0, x * mh1], axis=-2)

    kk = k * kkw
    kk = kk / jnp.maximum(jnp.sqrt(seg(kk * kk)), 1e-12)
    k2 = k * (1.0 + (asig - 1.0) * kaw)
    a = -kk
    b = kk * asig
    cum = _const_left(_tril(c, False).astype(bf16), lw)
    at = stack(a * jnp.exp(cum - lw))
    rt = stack(r * jnp.exp(cum))
    en = jnp.exp(-cum)
    sc = _mm2(jnp.concatenate([at, rt], axis=-2), jnp.concatenate([stack(b * en), stack(k2 * en)], axis=-2), "nt")
    row, col = _iota((n2, n2), 0), _iota((n2, n2), 1)
    same = row // c == col // c
    strict = same & (row % c > col % c)
    incl = same & (row % c >= col % c)
    aab = jnp.where(strict, sc[..., :n2, :n2], 0.0)
    aak = jnp.where(strict, sc[..., :n2, n2:], 0.0)
    arb = jnp.where(incl, sc[..., n2:, :n2], 0.0)
    ark = jnp.where(incl, sc[..., n2:, n2:], 0.0)
    vv = jnp.concatenate([v, v], axis=-2)
    mask_st = jnp.concatenate([jnp.broadcast_to(mh0, (c, LANES)), jnp.broadcast_to(mh1, (c, LANES))], axis=0)
    x_st = _mm2(jnp.concatenate([at, aak], axis=-1), jnp.concatenate([m0, vv], axis=-2), "nn", APPLY_PASSES)
    if tm is None:
        tm = _tri_inv(lax.stop_gradient(aab))
    u_st = _tri_solve(tm, aab, x_st) * mask_st
    o_st = _mm2(jnp.concatenate([rt, arb, ark], axis=-1), jnp.concatenate([m0, u_st, vv], axis=-2), "nn", APPLY_PASSES) * mask_st
    u = u_st[..., :c, :] + u_st[..., c:, :]
    o = o_st[..., :c, :] + o_st[..., c:, :]
    cum_last = _last_row(cum)
    dec_end = jnp.exp(cum_last - cum)
    m_new = _col_of_row(jnp.exp(cum_last)) * m0 + _mm2(
        jnp.concatenate([b * dec_end, k2 * dec_end], axis=-2), jnp.concatenate([u, v], axis=-2), "tn", APPLY_PASSES) * same_head.astype(f32)
    mean = seg(o) * (1.0 / RWKV_HEAD)
    d = o - mean
    var = seg(d * d) * (1.0 / RWKV_HEAD)
    on = d * lax.rsqrt(var + GN_EPS) * gnw + gnb
    bonus = seg(r * k2 * rkw) * v
    return (on + bonus) * _silu(z), m_new, tm


def _split_lanes(a, n):
    return [a[:, i * LANES:(i + 1) * LANES] for i in range(n)]


def _rwkv_step(layer1, prm, y, prev, vf, pp, m0, tm=None):
    xs = _rwkv_pre(layer1, prm, y, prev, vf)
    n_pair = m0.shape[0]
    og, m_new, tm = _rwkv_pair(pp, m0, tuple(jnp.concatenate([p[None] for p in _split_lanes(a, n_pair)], axis=0) for a in xs), tm)
    return og, m_new, xs[2], tm


def _group(n):
    return n


def _rwkv_specs(layer1, t, dr, rwc, n_pair, rev):
    nc = t // CHUNK
    grp = _group(n_pair)

    def cidx(c):
        return (nc - 1 - c) if rev else c

    full = lambda shape: pl.BlockSpec(shape, lambda c, p: tuple(0 for _ in shape))
    specs = [
        pl.BlockSpec((CHUNK, rwc), lambda c, p: (cidx(c), 0)),
        pl.BlockSpec((8, rwc), lambda c, p: (jnp.maximum(cidx(c) * (CHUNK // 8) - 1, 0), 0)),
    ]
    if layer1:
        specs.append(pl.BlockSpec((CHUNK, dr), lambda c, p: (cidx(c), 0)))
    prm_shapes = [(1, rwc), (1, dr), (1, dr), (LANES, dr), (LANES, dr)]
    if layer1:
        prm_shapes += [(1, dr), (dr, LANES), (LANES, dr)]
    specs += [full(s) for s in prm_shapes]
    specs.append(pl.BlockSpec((grp, 8, LANES), lambda c, p: (p, 0, 0)))
    return specs, prm_shapes, cidx, full


def _rwkv_fwd(layer1, proj, vf, prm, pp, cat_width):
    t = proj.shape[0]
    dr = prm[1].shape[1]
    rwc = prm[0].shape[1]
    n_pair = dr // LANES
    nc = t // CHUNK
    n_prm = len(prm)
    specs, _, _, _ = _rwkv_specs(layer1, t, dr, rwc, n_pair, False)

    def body(*refs):
        y_ref, prev_ref = refs[0], refs[1]
        i = 2
        vf_ref = None
        if layer1:
            vf_ref = refs[i]
            i += 1
        prm_refs = refs[i:i + n_prm]
        i += n_prm
        pp_ref = refs[i]
        i += 1
        cat_ref = refs[i]
        i += 1
        vout_ref = None
        if not layer1:
            vout_ref = refs[i]
            i += 1
        mck_ref, m_s = refs[i], refs[i + 1]
        c = pl.program_id(0)

        @pl.when(c == 0)
        def _():
            m_s[...] = jnp.zeros_like(m_s)

        prev = prev_ref[pl.ds(7, 1), :] * (c != 0).astype(f32)
        m0 = m_s[...]
        ppv = tuple(pp_ref[:, pl.ds(q, 1), :] for q in range(5))
        og, m_new, v, tm = _rwkv_step(layer1, tuple(r[...] for r in prm_refs), y_ref[...], prev,
                                      vf_ref[...] if layer1 else None, ppv, m0)
        mck_ref[0, :n_pair] = m0
        mck_ref[0, n_pair:] = tm
        if not layer1:
            vout_ref[...] = v
        for j in range(n_pair):
            cat_ref[:, j * LANES:(j + 1) * LANES] = og[j]
        m_s[...] = m_new

    grp = _group(n_pair)
    assert grp == n_pair
    out_shape = [jax.ShapeDtypeStruct((t, cat_width), f32)]
    out_specs = [pl.BlockSpec((CHUNK, grp * LANES), lambda c, p: (c, p))]
    if not layer1:
        out_shape.append(jax.ShapeDtypeStruct((t, dr), f32))
        out_specs.append(pl.BlockSpec((CHUNK, dr), lambda c, p: (c, 0)))
    out_shape.append(jax.ShapeDtypeStruct((nc, 2 * n_pair, LANES, LANES), f32))
    out_specs.append(pl.BlockSpec((1, 2 * grp, LANES, LANES), lambda c, p: (c, p, 0, 0)))
    args = [proj, proj] + ([vf] if layer1 else []) + list(prm) + [pp]
    return pl.pallas_call(
        body, grid=(nc, 1), in_specs=specs, out_specs=out_specs, out_shape=out_shape,
        scratch_shapes=[pltpu.VMEM((n_pair, LANES, LANES), f32)],
        compiler_params=pltpu.CompilerParams(dimension_semantics=("arbitrary", "arbitrary")),
        name=f"rwkv_fwd_l{int(layer1)}",
    )(*args)


def _rwkv_bwd(layer1, proj, vf, prm, pp, mck, dcat, dvout):
    t = proj.shape[0]
    dr = prm[1].shape[1]
    rwc = prm[0].shape[1]
    n_pair = dr // LANES
    nc = t // CHUNK
    n_prm = len(prm)
    specs, prm_shapes, cidx, full = _rwkv_specs(layer1, t, dr, rwc, n_pair, True)
    grp = _group(n_pair)
    assert grp == n_pair
    specs.append(pl.BlockSpec((1, 2 * grp, LANES, LANES), lambda c, p: (cidx(c), p, 0, 0)))
    specs.append(pl.BlockSpec((CHUNK, grp * LANES), lambda c, p: (cidx(c), p)))
    if not layer1:
        specs.append(pl.BlockSpec((CHUNK, dr), lambda c, p: (cidx(c), 0)))

    def body(*refs):
        y_ref, prev_ref = refs[0], refs[1]
        i = 2
        vf_ref = None
        if layer1:
            vf_ref = refs[i]
            i += 1
        prm_refs = refs[i:i + n_prm]
        i += n_prm
        pp_ref, mck_ref, dog_ref = refs[i], refs[i + 1], refs[i + 2]
        i += 3
        dvout_ref = None
        if not layer1:
            dvout_ref = refs[i]
            i += 1
        dy_ref = refs[i]
        i += 1
        dvf_ref = None
        if layer1:
            dvf_ref = refs[i]
            i += 1
        dprm_refs = refs[i:i + n_prm]
        i += n_prm
        dpp_ref = refs[i]
        dm_s, dprev_s = refs[i + 1:i + 3]
        c = pl.program_id(0)
        cr = nc - 1 - c

        @pl.when(c == 0)
        def _():
            dm_s[...] = jnp.zeros_like(dm_s)
            dprev_s[...] = jnp.zeros_like(dprev_s)
            dpp_ref[...] = jnp.zeros_like(dpp_ref)
            for r in dprm_refs:
                r[...] = jnp.zeros_like(r)

        prev = prev_ref[pl.ds(7, 1), :] * (cr != 0).astype(f32)
        prm_v = tuple(r[...] for r in prm_refs)
        ppv = tuple(pp_ref[:, pl.ds(q, 1), :] for q in range(5))
        dog = jnp.stack([dog_ref[:, j * LANES:(j + 1) * LANES] for j in range(n_pair)], axis=0)
        m0, tm = mck_ref[0, :n_pair], mck_ref[0, n_pair:]
        no_tm = jnp.zeros_like(tm)
        if layer1:
            _, vjp = jax.vjp(lambda a, b, d, e, g, h: _rwkv_step(True, a, b, d, e, g, h, tm),
                             prm_v, y_ref[...], prev, vf_ref[...], ppv, m0)
            dprm, dy, dprev, dvf, dppv, dm0 = vjp((dog, dm_s[...], jnp.zeros((CHUNK, dr), f32), no_tm))
            dvf_ref[...] = dvf
        else:
            _, vjp = jax.vjp(lambda a, b, d, e, g: _rwkv_step(False, a, b, d, None, e, g, tm), prm_v, y_ref[...], prev, ppv, m0)
            dprm, dy, dprev, dppv, dm0 = vjp((dog, dm_s[...], dvout_ref[...], no_tm))
        dm_s[...] = dm0
        for q in range(5):
            dpp_ref[:, pl.ds(q, 1), :] += dppv[q]
        dy_ref[...] = (dy + jnp.where(_iota((CHUNK, 1), 0) == CHUNK - 1, dprev_s[...], 0.0)).astype(bf16)
        dprev_s[...] = dprev
        for r, gval in zip(dprm_refs, dprm):
            r[...] += gval

    out_shape = [jax.ShapeDtypeStruct((t, proj.shape[1]), bf16)]
    out_specs = [pl.BlockSpec((CHUNK, rwc), lambda c, p: (cidx(c), 0))]
    if layer1:
        out_shape.append(jax.ShapeDtypeStruct((t, dr), f32))
        out_specs.append(pl.BlockSpec((CHUNK, dr), lambda c, p: (cidx(c), 0)))
    out_shape += [jax.ShapeDtypeStruct(s, f32) for s in prm_shapes]
    out_specs += [full(s) for s in prm_shapes]
    out_shape.append(jax.ShapeDtypeStruct((n_pair, 8, LANES), f32))
    out_specs.append(full((n_pair, 8, LANES)))
    args = [proj, proj] + ([vf] if layer1 else []) + list(prm) + [pp, mck, dcat] + ([] if layer1 else [dvout])
    return pl.pallas_call(
        body, grid=(nc, 1), in_specs=specs, out_specs=out_specs, out_shape=out_shape,
        scratch_shapes=[pltpu.VMEM((n_pair, LANES, LANES), f32), pltpu.VMEM((1, rwc), f32)],
        compiler_params=pltpu.CompilerParams(dimension_semantics=("arbitrary", "arbitrary")),
        name=f"rwkv_bwd_l{int(layer1)}",
    )(*args)


def _hgrn_chunk(layer1, lbl, gnw, s0, q_raw, f_raw, i_in, z):
    c = q_raw.shape[-2]
    q = _silu(q_raw)
    ls = _log_sigmoid(f_raw)
    if layer1:
        l0, l1 = lbl[..., 0:1, :], lbl[..., 1:2, :]
        mx = jnp.maximum(l0, l1)
        e0, e1 = jnp.exp(l0 - mx), jnp.exp(l1 - mx)
        sm0, sm1 = e0 / (e0 + e1), e1 / (e0 + e1)
        lb = (sm0 + sm1) - sm0
        log_f = _logaddexp(jnp.log(jnp.maximum(lb, LB_FLOOR)), jnp.log1p(-lb) + ls)
        k = (1.0 - lb) * jax.nn.sigmoid(-f_raw)
    else:
        log_f = _logaddexp(jnp.full_like(ls, jnp.log(jnp.float32(LB_FLOOR))), ls)
        k = jax.nn.sigmoid(-f_raw)
    row, col = _iota((c, c), 0), _iota((c, c), 1)
    trow = _iota((c, 1), 0)
    halves = []
    half = c // 2
    while half >= 1:
        halves.append(half)
        half //= 2
    cmat = jnp.concatenate([(col <= row).astype(f32)]
                           + [(col <= (row // (2 * hf)) * (2 * hf) + hf - 1).astype(f32) for hf in halves], axis=0)
    ball = _const_left(cmat.astype(bf16), log_f)
    b = ball[..., :c, :]
    att = None
    for lvl, hf in enumerate(halves):
        blk = 2 * hf
        bref = ball[..., (lvl + 1) * c:(lvl + 2) * c, :]
        upper = (trow % blk) >= hf
        dec = jnp.exp(jnp.where(upper, b - bref, bref - b))
        qh = jnp.where(upper, q * dec, 0.0)
        kh = jnp.where(upper, 0.0, k * dec)
        term = jnp.where(row // blk == col // blk, _mm2(qh, kh, "nt", APPLY_PASSES), 0.0)
        att = term if att is None else att + term
    lhs = jnp.concatenate([q * jnp.exp(b), att, jnp.zeros(att.shape[:-1] + (LANES - c,), f32)], axis=-1)
    rhs = jnp.concatenate([s0, i_in, jnp.zeros(i_in.shape[:-2] + (LANES - c, i_in.shape[-1]), f32)], axis=-2)
    o = _mm2(lhs, rhs, "nn", APPLY_PASSES) + jnp.sum(q * k, axis=-1, keepdims=True) * i_in
    b_last = _last_row(b)
    s_new = _col_of_row(jnp.exp(b_last)) * s0 + _mm2(k * jnp.exp(b_last - b), i_in, "tn", APPLY_PASSES)
    o = o * lax.rsqrt(jnp.mean(o * o, axis=-1, keepdims=True) + RMS_EPS)
    return o * gnw * _silu(z), s_new


def _hgrn_in_specs(t, dh, col0, rev):
    nc = t // CHUNK
    nh = dh // LANES

    def cidx(c):
        return (nc - 1 - c) if rev else c

    grp = _group(nh)
    specs = [pl.BlockSpec((CHUNK, LANES), functools.partial(lambda g, j, h, c: (cidx(c), col0 + g * nh + h * grp + j), g, j))
             for j in range(grp) for g in range(4)]
    specs.append(pl.BlockSpec((2, grp * LANES), lambda h, c: (0, h)))
    specs.append(pl.BlockSpec((1, grp * LANES), lambda h, c: (0, h)))
    return specs, cidx, grp


def _hgrn_fwd(layer1, proj, lbl, gnw, cat, rwc):
    t, d = cat.shape
    dh = gnw.shape[1]
    nh = dh // LANES
    nc = t // CHUNK
    col0 = rwc // LANES
    specs, _, grp = _hgrn_in_specs(t, dh, col0, False)
    specs.append(pl.BlockSpec(memory_space=pl.ANY))
    assert (d - dh) % (grp * LANES) == 0
    cat_col0 = (d - dh) // (grp * LANES)

    def body(*refs):
        x_refs = refs[:4 * grp]
        lbl_ref, gnw_ref, _, cat_ref, sck_ref, s_s = refs[4 * grp:]
        c = pl.program_id(1)

        @pl.when(c == 0)
        def _():
            s_s[...] = jnp.zeros_like(s_s)

        lanes = [slice(j * LANES, (j + 1) * LANES) for j in range(grp)]
        s0 = s_s[...]
        sck_ref[:, 0] = s0
        out, s_new = _hgrn_chunk(layer1, jnp.stack([lbl_ref[:, ln] for ln in lanes]), jnp.stack([gnw_ref[:, ln] for ln in lanes]),
                                 s0, *(jnp.stack([x_refs[4 * j + g][...] for j in range(grp)]) for g in range(4)))
        for j in range(grp):
            cat_ref[:, lanes[j]] = out[j]
        s_s[...] = s_new

    return pl.pallas_call(
        body, grid=(nh // grp, nc), in_specs=specs,
        out_specs=[pl.BlockSpec((CHUNK, grp * LANES), lambda h, c: (c, cat_col0 + h)),
                   pl.BlockSpec((grp, 1, LANES, LANES), lambda h, c: (h, c, 0, 0))],
        out_shape=[jax.ShapeDtypeStruct((t, d), f32), jax.ShapeDtypeStruct((nh, nc, LANES, LANES), f32)],
        scratch_shapes=[pltpu.VMEM((grp, LANES, LANES), f32)],
        input_output_aliases={4 * grp + 2: 0},
        compiler_params=pltpu.CompilerParams(dimension_semantics=("arbitrary", "arbitrary")),
        name=f"hgrn_fwd_l{int(layer1)}",
    )(*([proj] * (4 * grp)), lbl, gnw, cat)


def _hgrn_bwd(layer1, proj, lbl, gnw, sck, dcat, rwc, dproj):
    t, d = dcat.shape
    dh = gnw.shape[1]
    nh = dh // LANES
    nc = t // CHUNK
    col0 = rwc // LANES
    specs, cidx, grp = _hgrn_in_specs(t, dh, col0, True)
    assert grp == nh and (d - dh) % (grp * LANES) == 0
    cat_col0 = (d - dh) // (grp * LANES)
    specs.append(pl.BlockSpec((grp, 1, LANES, LANES), lambda h, c: (h, cidx(c), 0, 0)))
    specs.append(pl.BlockSpec((CHUNK, grp * LANES), lambda h, c: (cidx(c), cat_col0 + h)))
    specs.append(pl.BlockSpec(memory_space=pl.ANY))

    def body(*refs):
        x_refs = refs[:4 * grp]
        lbl_ref, gnw_ref, sck_ref, do_ref, _, dp_hbm, dlbl_ref, dgnw_ref, ds_s, stage, sems = refs[4 * grp:]
        c = pl.program_id(1)
        slot = c % 2

        def put(s, g, chunk):
            return pltpu.make_async_copy(stage.at[s, g], dp_hbm.at[pl.ds(chunk * CHUNK, CHUNK), pl.ds(rwc + g * dh, dh)],
                                         sems.at[s, g])

        @pl.when(c == 0)
        def _():
            ds_s[...] = jnp.zeros_like(ds_s)
            dlbl_ref[...] = jnp.zeros_like(dlbl_ref)
            dgnw_ref[...] = jnp.zeros_like(dgnw_ref)

        @pl.when(c >= 2)
        def _():
            for g in range(4):
                put(slot, g, 0).wait()

        lanes = [slice(j * LANES, (j + 1) * LANES) for j in range(grp)]
        _, vjp = jax.vjp(functools.partial(_hgrn_chunk, layer1),
                         jnp.stack([lbl_ref[:, ln] for ln in lanes]), jnp.stack([gnw_ref[:, ln] for ln in lanes]), sck_ref[:, 0],
                         *(jnp.stack([x_refs[4 * j + g][...] for j in range(grp)]) for g in range(4)))
        dlbl, dgnw, ds0, dq, df, di, dz = vjp((jnp.stack([do_ref[:, ln] for ln in lanes]), ds_s[...]))
        ds_s[...] = ds0
        for j in range(grp):
            dlbl_ref[:, lanes[j]] += dlbl[j]
            dgnw_ref[:, lanes[j]] += dgnw[j]
            for g, val in enumerate((dq, df, di, dz)):
                stage[slot, g, :, lanes[j]] = val[j].astype(bf16)
        for g in range(4):
            put(slot, g, nc - 1 - c).start()

        @pl.when(c == nc - 1)
        def _():
            for g in range(4):
                put(slot, g, 0).wait()
                if nc >= 2:
                    put(1 - slot, g, 0).wait()

    return pl.pallas_call(
        body, grid=(1, nc), in_specs=specs,
        out_specs=[pl.BlockSpec(memory_space=pl.ANY),
                   pl.BlockSpec((2, grp * LANES), lambda h, c: (0, h)),
                   pl.BlockSpec((1, grp * LANES), lambda h, c: (0, h))],
        out_shape=[jax.ShapeDtypeStruct(dproj.shape, dproj.dtype), jax.ShapeDtypeStruct((2, dh), f32),
                   jax.ShapeDtypeStruct((1, dh), f32)],
        scratch_shapes=[pltpu.VMEM((grp, LANES, LANES), f32), pltpu.VMEM((2, 4, CHUNK, dh), bf16),
                        pltpu.SemaphoreType.DMA((2, 4))],
        input_output_aliases={4 * grp + 4: 0},
        compiler_params=pltpu.CompilerParams(dimension_semantics=("arbitrary", "arbitrary")),
        name=f"hgrn_bwd_l{int(layer1)}",
    )(*([proj] * (4 * grp)), lbl, gnw, sck, dcat, dproj)


def _ln(h, y, w, b):
    u = ALPHA * h + y
    mu = jnp.mean(u, axis=-1, keepdims=True)
    var = jnp.mean(jnp.square(u - mu), axis=-1, keepdims=True)
    return (u - mu) * lax.rsqrt(var + LN_EPS) * w + b


def _row_tile(t):
    return 256 if t % 256 == 0 else t


def _ln_fwd(h, y, w, b):
    t, d = h.shape
    tr = _row_tile(t)

    def body(h_ref, y_ref, w_ref, b_ref, o_ref, o16_ref):
        out = _ln(h_ref[...], y_ref[...], w_ref[...], b_ref[...])
        o_ref[...] = out
        o16_ref[...] = out.astype(bf16)

    row = pl.BlockSpec((tr, d), lambda i: (i, 0))
    vec = pl.BlockSpec((1, d), lambda i: (0, 0))
    return pl.pallas_call(body, grid=(t // tr,), in_specs=[row, row, vec, vec], out_specs=[row, row],
                          out_shape=[jax.ShapeDtypeStruct((t, d), f32), jax.ShapeDtypeStruct((t, d), bf16)],
                          name="ln_fwd")(h, y, w, b)


def _ln_loss_bwd(h, y, w, b, tgt):
    t, d = h.shape
    tr = _row_tile(t)

    def body(h_ref, y_ref, w_ref, b_ref, t_ref, dy_ref, dy16_ref, dw_ref, db_ref, loss_ref):
        @pl.when(pl.program_id(0) == 0)
        def _():
            dw_ref[...] = jnp.zeros_like(dw_ref)
            db_ref[...] = jnp.zeros_like(db_ref)
            loss_ref[...] = jnp.zeros_like(loss_ref)

        out, vjp = jax.vjp(lambda yy, ww, bb: _ln(h_ref[...], yy, ww, bb), y_ref[...], w_ref[...], b_ref[...])
        err = out - t_ref[...]
        loss_ref[...] += 0.5 * jnp.sum(jnp.mean(jnp.square(err), axis=-1, keepdims=True), axis=0, keepdims=True)
        dy, dw, db = vjp(err * (1.0 / d))
        dy_ref[...] = dy
        dy16_ref[...] = dy.astype(bf16)
        dw_ref[...] += dw
        db_ref[...] += db

    row = pl.BlockSpec((tr, d), lambda i: (i, 0))
    vec = pl.BlockSpec((1, d), lambda i: (0, 0))
    return pl.pallas_call(
        body, grid=(t // tr,), in_specs=[row, row, vec, vec, row],
        out_specs=[row, row, vec, vec, pl.BlockSpec((1, LANES), lambda i: (0, 0))],
        out_shape=[jax.ShapeDtypeStruct((t, d), f32), jax.ShapeDtypeStruct((t, d), bf16), jax.ShapeDtypeStruct((1, d), f32),
                   jax.ShapeDtypeStruct((1, d), f32), jax.ShapeDtypeStruct((1, LANES), f32)],
        compiler_params=pltpu.CompilerParams(dimension_semantics=("arbitrary",)), name="ln_loss_bwd")(h, y, w, b, tgt)


def _ln_bwd(h, y, w, b, dout):
    t, d = h.shape
    tr = _row_tile(t)

    def body(h_ref, y_ref, w_ref, b_ref, do_ref, dy_ref, dy16_ref, dw_ref, db_ref):
        @pl.when(pl.program_id(0) == 0)
        def _():
            dw_ref[...] = jnp.zeros_like(dw_ref)
            db_ref[...] = jnp.zeros_like(db_ref)

        _, vjp = jax.vjp(lambda yy, ww, bb: _ln(h_ref[...], yy, ww, bb), y_ref[...], w_ref[...], b_ref[...])
        dy, dw, db = vjp(do_ref[...])
        dy_ref[...] = dy
        dy16_ref[...] = dy.astype(bf16)
        dw_ref[...] += dw
        db_ref[...] += db

    row = pl.BlockSpec((tr, d), lambda i: (i, 0))
    vec = pl.BlockSpec((1, d), lambda i: (0, 0))
    return pl.pallas_call(
        body, grid=(t // tr,), in_specs=[row, row, vec, vec, row], out_specs=[row, row, vec, vec],
        out_shape=[jax.ShapeDtypeStruct((t, d), f32), jax.ShapeDtypeStruct((t, d), bf16),
                   jax.ShapeDtypeStruct((1, d), f32), jax.ShapeDtypeStruct((1, d), f32)],
        compiler_params=pltpu.CompilerParams(dimension_semantics=("arbitrary",)), name="ln_bwd")(h, y, w, b, dout)


def _pick(n, prefs):
    for p in prefs:
        if n % p == 0:
            return p
    return n


def _tile(n, want):
    if n <= want:
        return n
    for cand in range(want - want % LANES, 0, -LANES):
        if n % cand == 0:
            return cand
    return n


def _matmul(a, b, mode, name, tiles, add=None, add_scale=1.0, out_dtype=f32, after=None):
    if mode == "nn":
        (m, k), n = a.shape, b.shape[1]
    elif mode == "nt":
        (m, k), n = a.shape, b.shape[0]
    else:
        (k, m), n = a.shape, b.shape[1]
    tm, tn, tk = _tile(m, tiles[0]), _tile(n, tiles[1]), _tile(k, tiles[2])
    nk = k // tk
    cache_a = nk == 1 and a.dtype != bf16 and n // tn > 1

    def body(*refs):
        a_ref, b_ref = refs[0], refs[1]
        add_ref = refs[2] if add is not None else None
        n_in = 2 + (add is not None) + (after is not None)
        o_ref = refs[n_in]
        scratch = refs[n_in + 1:]

        def finish(res):
            if add is not None:
                res = res + add_scale * add_ref[...]
            o_ref[...] = res.astype(out_dtype)

        if cache_a:
            a_bf = scratch[0]

            @pl.when(pl.program_id(1) == 0)
            def _():
                a_bf[...] = a_ref[...].astype(bf16)

            a_val = a_bf[...]
        else:
            a_val = a_ref[...].astype(bf16)
        prod = lax.dot_general(a_val, b_ref[...].astype(bf16), _DIMS[mode], preferred_element_type=f32)
        if nk == 1:
            finish(prod)
        else:
            acc = scratch[-1]
            kk = pl.program_id(2)

            @pl.when(kk == 0)
            def _():
                acc[...] = prod

            @pl.when(kk != 0)
            def _():
                acc[...] += prod

            @pl.when(kk == nk - 1)
            def _():
                finish(acc[...])

    a_shape = (tk, tm) if mode == "tn" else (tm, tk)
    a_spec = pl.BlockSpec(a_shape, (lambda i, j, kk: (kk, i)) if mode == "tn" else (lambda i, j, kk: (i, kk)))
    b_spec = pl.BlockSpec((tn, tk), lambda i, j, kk: (j, kk)) if mode == "nt" else pl.BlockSpec((tk, tn), lambda i, j, kk: (kk, j))
    o_spec = pl.BlockSpec((tm, tn), lambda i, j, kk: (i, j))
    in_specs = [a_spec, b_spec] + ([o_spec] if add is not None else []) + ([pl.BlockSpec(memory_space=pl.ANY)] if after is not None else [])
    args = [a, b] + ([add] if add is not None else []) + ([after] if after is not None else [])
    scratch_shapes = ([pltpu.VMEM(a_shape, bf16)] if cache_a else []) + ([pltpu.VMEM((tm, tn), f32)] if nk > 1 else [])
    return pl.pallas_call(
        body, grid=(m // tm, n // tn, nk), in_specs=in_specs, out_specs=o_spec,
        out_shape=jax.ShapeDtypeStruct((m, n), out_dtype), scratch_shapes=scratch_shapes,
        compiler_params=pltpu.CompilerParams(dimension_semantics=("parallel", "arbitrary", "arbitrary")),
        name=name,
    )(*args)


def _position():
    return lax.axis_index("x"), lax.axis_index("y"), lax.axis_index("c")


def _flip(pos, k):
    x, y, c = pos
    return (1 - x if k & 4 else x, 1 - y if k & 2 else y, 1 - c if k & 1 else c)


def _index(pos):
    return 4 * pos[0] + 2 * pos[1] + pos[2]


def _all_gather_rows(x, name):
    m_per, n = x.shape

    def body(x_ref, out_ref, send_sems, recv_sems, local_sem):
        me = _position()
        sibling = _flip(me, 1)
        chips = (2, 4, 6)

        def rows(pos):
            return out_ref.at[pl.ds(_index(pos) * m_per, m_per), :]

        def copy(sem, block, to, src=None):
            return pltpu.make_async_remote_copy(
                src_ref=rows(block) if src is None else src, dst_ref=rows(block),
                send_sem=send_sems.at[sem], recv_sem=recv_sems.at[sem], device_id=to, device_id_type=MESH)

        mine = pltpu.make_async_copy(x_ref, rows(me), local_sem)
        mine.start()
        first = [copy(0, me, sibling, src=x_ref)]
        first += [copy(1 + j, me, _flip(me, k), src=x_ref) for j, k in enumerate(chips)]
        for cp in first:
            cp.start()
        passed = [copy(4 + j, _flip(me, k), sibling) for j, k in enumerate(chips)]
        for j, k in enumerate(chips):
            copy(1 + j, _flip(me, k), me).wait_recv()
            passed[j].start()
        copy(0, sibling, me).wait_recv()
        for j, k in enumerate(chips):
            copy(4 + j, _flip(sibling, k), me).wait_recv()
        for cp in first + passed:
            cp.wait_send()
        mine.wait()

    return pl.pallas_call(
        body, out_shape=jax.ShapeDtypeStruct((N_DEV * m_per, n), x.dtype),
        in_specs=[pl.BlockSpec(memory_space=pl.ANY)], out_specs=pl.BlockSpec(memory_space=pl.ANY),
        scratch_shapes=[pltpu.SemaphoreType.DMA((7,)), pltpu.SemaphoreType.DMA((7,)), pltpu.SemaphoreType.DMA(())],
        name=name,
    )(x)


def _split_start(srcs, lands, plan, n_copies, name, after=()):
    n_arr = len(srcs)
    n_after = len(after)
    hbm = pl.BlockSpec(memory_space=pltpu.HBM)
    sem = pl.BlockSpec(memory_space=pltpu.SEMAPHORE)

    def body(*refs):
        src_refs, land_refs = refs[:n_arr], refs[n_arr:2 * n_arr]
        outs_at = 2 * n_arr + n_after
        send_sems, recv_sems = refs[outs_at:outs_at + n_arr], refs[outs_at + n_arr:outs_at + 2 * n_arr]
        token = refs[-1]
        me = _position()
        for i in range(n_arr):
            for j, (src, dst, peer, _) in enumerate(plan(i, src_refs[i], land_refs[i], me)):
                pltpu.make_async_remote_copy(src_ref=src, dst_ref=dst, send_sem=send_sems[i].at[j], recv_sem=recv_sems[i].at[j],
                                             device_id=peer, device_id_type=MESH).start()
        token[...] = jnp.zeros_like(token)

    outs = pl.pallas_call(
        body, name=name,
        out_shape=([pltpu.SemaphoreType.DMA((n_copies,))] * (2 * n_arr)
                   + [pltpu.HBM(a.shape, a.dtype) for a in list(srcs) + list(lands)]
                   + [jax.ShapeDtypeStruct((8, LANES), f32)]),
        in_specs=[hbm] * (2 * n_arr) + [pl.BlockSpec(memory_space=pl.ANY)] * n_after,
        out_specs=[sem] * (2 * n_arr) + [hbm] * (2 * n_arr) + [pl.BlockSpec(memory_space=pltpu.VMEM)],
        input_output_aliases={i: 2 * n_arr + i for i in range(2 * n_arr)},
        compiler_params=pltpu.CompilerParams(has_side_effects=pltpu.SideEffectType.DATAFLOW_SIDE_EFFECTING),
    )(*[pltpu.with_memory_space_constraint(a, pltpu.HBM) for a in list(srcs) + list(lands)], *after)
    return (outs[:n_arr], outs[n_arr:2 * n_arr], outs[2 * n_arr:3 * n_arr], outs[3 * n_arr:4 * n_arr], outs[-1])


def _split_wait(started, plan, after, name):
    send_sems, recv_sems, srcs, lands, _ = started
    n_arr = len(srcs)
    hbm = pl.BlockSpec(memory_space=pltpu.HBM)
    sem = pl.BlockSpec(memory_space=pltpu.SEMAPHORE)

    def body(*refs):
        src_refs, land_refs = refs[:n_arr], refs[n_arr:2 * n_arr]
        s_sems, r_sems = refs[2 * n_arr:3 * n_arr], refs[3 * n_arr:4 * n_arr]
        me = _position()
        for i in range(n_arr):
            for j, (src, _, peer, arrival) in enumerate(plan(i, src_refs[i], land_refs[i], me)):
                cp = pltpu.make_async_remote_copy(src_ref=src, dst_ref=arrival, send_sem=s_sems[i].at[j], recv_sem=r_sems[i].at[j],
                                                  device_id=peer, device_id_type=MESH)
                cp.wait_send()
                cp.wait_recv()

    outs = pl.pallas_call(
        body, name=name,
        out_shape=[pltpu.HBM(a.shape, a.dtype) for a in list(srcs) + list(lands)],
        in_specs=[hbm] * (2 * n_arr) + [sem] * (2 * n_arr) + [pl.BlockSpec(memory_space=pl.ANY)],
        out_specs=[hbm] * (2 * n_arr),
        input_output_aliases={i: i for i in range(2 * n_arr)},
        compiler_params=pltpu.CompilerParams(has_side_effects=pltpu.SideEffectType.DATAFLOW_SIDE_EFFECTING),
    )(*srcs, *lands, *send_sems, *recv_sems, after)
    return outs[:n_arr], outs[n_arr:]


_GATHER_FLIPS = (1, 2, 4, 6)


def _gather_plan(i, src_ref, land_ref, me):
    m = src_ref.shape[0]

    def rows(pos):
        return land_ref.at[pl.ds(_index(pos) * m, m), :]

    return [(src_ref, rows(me), _flip(me, k), rows(_flip(me, k))) for k in _GATHER_FLIPS]


def _gather_forward(lands, name):
    n_arr = len(lands)
    chips = (2, 4, 6)

    def body(*refs):
        out_refs = refs[n_arr:2 * n_arr]
        send_sems, recv_sems = refs[2 * n_arr:]
        me = _position()
        sibling = _flip(me, 1)
        sends, arrivals = [], []
        for i, out_ref in enumerate(out_refs):
            m = out_ref.shape[0] // N_DEV

            def copy(pos, j):
                blk = out_ref.at[pl.ds(_index(pos) * m, m), :]
                return pltpu.make_async_remote_copy(src_ref=blk, dst_ref=blk, send_sem=send_sems.at[3 * i + j],
                                                    recv_sem=recv_sems.at[3 * i + j], device_id=sibling, device_id_type=MESH)

            for j, k in enumerate(chips):
                sends.append(copy(_flip(me, k), j))
                arrivals.append(copy(_flip(sibling, k), j))
        for cp in sends:
            cp.start()
        for cp in arrivals:
            cp.wait_recv()
        for cp in sends:
            cp.wait_send()

    anyspec = pl.BlockSpec(memory_space=pl.ANY)
    return pl.pallas_call(
        body, out_shape=[jax.ShapeDtypeStruct(a.shape, a.dtype) for a in lands],
        in_specs=[anyspec] * n_arr, out_specs=[anyspec] * n_arr, input_output_aliases={i: i for i in range(n_arr)},
        scratch_shapes=[pltpu.SemaphoreType.DMA((3 * n_arr,))] * 2, name=name,
    )(*lands)


def _chips_plan(i, src_ref, land_ref, me):
    m = src_ref.shape[0] // 4
    plan = []
    for j, k in enumerate((2, 4, 6)):
        peer = _flip(me, k)
        plan.append((src_ref.at[pl.ds((2 * peer[0] + peer[1]) * m, m), :], land_ref.at[j], peer, land_ref.at[j]))
    return plan


def _exchange_siblings(gs, name):
    n_arr = len(gs)

    def body(*refs):
        g_refs, out_refs = refs[:n_arr], refs[n_arr:2 * n_arr]
        send_sems, recv_sems = refs[2 * n_arr:]
        me = _position()
        c = me[2]
        sibling = _flip(me, 1)
        copies = []
        for i, (g_ref, out_ref) in enumerate(zip(g_refs, out_refs)):
            m_per = g_ref.shape[0] // N_DEV
            for q in range(4):
                copies.append(pltpu.make_async_remote_copy(
                    src_ref=g_ref.at[pl.ds((2 * q + 1 - c) * m_per, m_per), :], dst_ref=out_ref.at[q],
                    send_sem=send_sems.at[4 * i + q], recv_sem=recv_sems.at[4 * i + q],
                    device_id=sibling, device_id_type=MESH))
        for cp in copies:
            cp.start()
        for cp in copies:
            cp.wait_recv()
        for cp in copies:
            cp.wait_send()

    anyspec = pl.BlockSpec(memory_space=pl.ANY)
    return pl.pallas_call(
        body, out_shape=[jax.ShapeDtypeStruct((4, g.shape[0] // N_DEV, g.shape[1]), g.dtype) for g in gs],
        in_specs=[anyspec] * n_arr, out_specs=[anyspec] * n_arr,
        scratch_shapes=[pltpu.SemaphoreType.DMA((4 * n_arr,))] * 2, name=name,
    )(*gs)


def _sum_with_sibling(g, recv, name):
    m = g.shape[0] // N_DEV
    n = g.shape[1]
    tr = _pick(m, (208, 128, 64, 32, 16))
    nt = m // tr

    def body(g_ref, r_ref, o_ref):
        c = lax.axis_index("c")
        own = jnp.where(c == 0, g_ref[0, 0].astype(f32), g_ref[0, 1].astype(f32))
        o_ref[...] = (own + r_ref[0].astype(f32)).astype(o_ref.dtype)

    return pl.pallas_call(
        body, grid=(4, nt),
        in_specs=[pl.BlockSpec((1, 2, tr, n), lambda q, i: (q, 0, i, 0)), pl.BlockSpec((1, tr, n), lambda q, i: (q, i, 0))],
        out_specs=pl.BlockSpec((tr, n), lambda q, i: (q * nt + i, 0)),
        out_shape=jax.ShapeDtypeStruct((4 * m, n), bf16), name=name,
    )(g.reshape(4, 2, m, n), recv)


def _sum_with_chips(h, recv, name):
    m = h.shape[0] // 4
    n = h.shape[1]
    tr = _pick(m, (208, 128, 64, 32, 16))

    def body(h_ref, r_ref, o_ref):
        my_q = 2 * lax.axis_index("x") + lax.axis_index("y")
        own = h_ref[0].astype(f32)
        for q in range(1, 4):
            own = jnp.where(my_q == q, h_ref[q].astype(f32), own)
        o_ref[...] = ((own + r_ref[0].astype(f32)) + r_ref[1].astype(f32)) + r_ref[2].astype(f32)

    return pl.pallas_call(
        body, grid=(m // tr,),
        in_specs=[pl.BlockSpec((4, tr, n), lambda i: (0, i, 0)), pl.BlockSpec((3, tr, n), lambda i: (0, i, 0))],
        out_specs=pl.BlockSpec((tr, n), lambda i: (i, 0)), out_shape=jax.ShapeDtypeStruct((m, n), f32), name=name,
    )(h.reshape(4, m, n), recv)


def _sum_slots(parts, name):
    n_slot, m, n = parts.shape
    tr = _pick(m, (208, 128, 64, 32, 16, 8))

    def body(p_ref, o_ref):
        acc = p_ref[0]
        for s in range(1, n_slot):
            acc = acc + p_ref[s]
        o_ref[...] = acc

    return pl.pallas_call(
        body, grid=(m // tr,), in_specs=[pl.BlockSpec((n_slot, tr, n), lambda i: (0, i, 0))],
        out_specs=pl.BlockSpec((tr, n), lambda i: (i, 0)), out_shape=jax.ShapeDtypeStruct((m, n), parts.dtype), name=name,
    )(parts)


def _reduce_scatter_begin(gs, name):
    from_sibling = _exchange_siblings(gs, "rs_d2d_" + name)
    chip_sums = [_sum_with_sibling(g, r, f"rs_sum2_{name}_{i}") for i, (g, r) in enumerate(zip(gs, from_sibling))]
    lands = [lax.empty((3, h.shape[0] // 4, h.shape[1]), h.dtype) for h in chip_sums]
    return _split_start(chip_sums, lands, _chips_plan, 3, "rs_ici_start_" + name)


def _reduce_scatter_end(started, after, name):
    chip_sums, from_chips = _split_wait(started, _chips_plan, after, "rs_ici_wait_" + name)
    return [_sum_with_chips(h, r, f"rs_sum4_{name}_{i}") for i, (h, r) in enumerate(zip(chip_sums, from_chips))]


def _adamw_update(w, g, m, v):
    mm = ADAM_B1 * m + (1.0 - ADAM_B1) * g
    vv = ADAM_B2 * v + (1.0 - ADAM_B2) * jnp.square(g)
    m_hat = mm / (1.0 - ADAM_B1 ** ADAM_STEP)
    v_hat = vv / (1.0 - ADAM_B2 ** ADAM_STEP)
    return -ADAM_LR * (m_hat / (jnp.sqrt(v_hat) + ADAM_EPS) + ADAM_WD * w), mm, vv


def _adamw_many(ws, gs, ms, vs, name):
    k = len(ws)
    shapes = [w.shape for w in ws]
    flat = [[a.reshape(-1, a.shape[-1]) for a in group] for group in (ws, gs, ms, vs)]

    def body(*refs):
        for i in range(k):
            d, mm, vv = _adamw_update(*(refs[j * k + i][...] for j in range(4)))
            refs[4 * k + i][...] = d
            refs[5 * k + i][...] = mm
            refs[6 * k + i][...] = vv

    outs = pl.pallas_call(
        body, out_shape=[jax.ShapeDtypeStruct(a.shape, f32) for a in flat[0]] * 3, name=name,
    )(*flat[0], *flat[1], *flat[2], *flat[3])
    return tuple([outs[j * k + i].reshape(shapes[i]) for i in range(k)] for j in range(3))


def _adamw(w, g, m, v, name):
    shape = w.shape
    n = shape[-1]
    r = w.size // n
    w2, g2, m2, v2 = (a.reshape(r, n) for a in (w, g, m, v))
    tr = _pick(r, (256, 208, 128, 64, 32, 16, 8))

    def body(w_ref, g_ref, m_ref, v_ref, d_ref, mo_ref, vo_ref):
        d_ref[...], mo_ref[...], vo_ref[...] = _adamw_update(w_ref[...], g_ref[...], m_ref[...], v_ref[...])

    spec = pl.BlockSpec((tr, n), lambda i: (i, 0))
    outs = pl.pallas_call(
        body, grid=(r // tr,), in_specs=[spec] * 4, out_specs=[spec] * 3,
        out_shape=[jax.ShapeDtypeStruct((r, n), f32)] * 3, name=name,
    )(w2, g2, m2, v2)
    return tuple(o.reshape(shape) for o in outs)


_SMALL = ("shift_mu", "w_decay0", "a0", "k_k", "k_a", "r_k", "ln_x_w", "ln_x_b", "v_mix0", "lb_logits",
          "g_norm_w", "ln_w", "ln_b")
_NAMES = ("w_in", "shift_mu", "w_decay0", "w_decay_up", "a0", "a_up", "k_k", "k_a", "r_k", "ln_x_w", "ln_x_b",
          "v_mix0", "v_mix_down", "v_mix_up", "lb_logits", "g_norm_w", "w_out", "ln_w", "ln_b")


def _pad_rows(a, rows, at_end):
    z = jnp.zeros((rows - a.shape[0], a.shape[1]), a.dtype)
    return jnp.concatenate([a, z] if at_end else [z, a], axis=0)


def kernel(x, w_in, shift_mu, w_decay0, w_decay_up, a0, a_up, k_k, k_a, r_k, ln_x_w, ln_x_b, v_mix0, v_mix_down, v_mix_up, lb_logits, g_norm_w, w_out, ln_w, ln_b, loss_target, m_w_in, m_shift_mu, m_w_decay0, m_w_decay_up, m_a0, m_a_up, m_k_k, m_k_a, m_r_k, m_ln_x_w, m_ln_x_b, m_v_mix0, m_v_mix_down, m_v_mix_up, m_lb_logits, m_g_norm_w, m_w_out, m_ln_w, m_ln_b, v_w_in, v_shift_mu, v_w_decay0, v_w_decay_up, v_a0, v_a_up, v_k_k, v_k_a, v_r_k, v_ln_x_w, v_ln_x_b, v_v_mix0, v_v_mix_down, v_v_mix_up, v_lb_logits, v_g_norm_w, v_w_out, v_ln_w, v_ln_b):
    weights = dict(w_in=w_in, shift_mu=shift_mu, w_decay0=w_decay0, w_decay_up=w_decay_up, a0=a0, a_up=a_up, k_k=k_k,
                   k_a=k_a, r_k=r_k, ln_x_w=ln_x_w, ln_x_b=ln_x_b, v_mix0=v_mix0, v_mix_down=v_mix_down,
                   v_mix_up=v_mix_up, lb_logits=lb_logits, g_norm_w=g_norm_w, w_out=w_out, ln_w=ln_w, ln_b=ln_b)
    mom1 = dict(w_in=m_w_in, shift_mu=m_shift_mu, w_decay0=m_w_decay0, w_decay_up=m_w_decay_up, a0=m_a0, a_up=m_a_up,
                k_k=m_k_k, k_a=m_k_a, r_k=m_r_k, ln_x_w=m_ln_x_w, ln_x_b=m_ln_x_b, v_mix0=m_v_mix0,
                v_mix_down=m_v_mix_down, v_mix_up=m_v_mix_up, lb_logits=m_lb_logits, g_norm_w=m_g_norm_w,
                w_out=m_w_out, ln_w=m_ln_w, ln_b=m_ln_b)
    mom2 = dict(w_in=v_w_in, shift_mu=v_shift_mu, w_decay0=v_w_decay0, w_decay_up=v_w_decay_up, a0=v_a0, a_up=v_a_up,
                k_k=v_k_k, k_a=v_k_a, r_k=v_r_k, ln_x_w=v_ln_x_w, ln_x_b=v_ln_x_b, v_mix0=v_v_mix0,
                v_mix_down=v_v_mix_down, v_mix_up=v_v_mix_up, lb_logits=v_lb_logits, g_norm_w=v_g_norm_w,
                w_out=v_w_out, ln_w=v_ln_w, ln_b=v_ln_b)
    assert x.shape[0] == 1 and w_in.shape[0] == DEPTH
    t, d = x.shape[1], x.shape[2]
    dr = w_decay0.shape[1]
    dh = g_norm_w.shape[1]
    rank_w, rank_a, rank_v = w_decay_up.shape[1], a_up.shape[1], v_mix_up.shape[1]
    rwc = 4 * dr + rank_w + rank_a
    assert rank_w + rank_a == LANES and rank_v <= LANES and dr + dh == d
    assert t % CHUNK == 0 and dr % LANES == 0 and dh % LANES == 0 and shift_mu.shape[1] == rwc
    n_pair = dr // LANES
    me = _index(_position())

    win_t = [_all_gather_rows(w_in[0].T.astype(bf16), "ag_w_in_0"), None]
    wout = [None, None]
    shard = dr // N_DEV
    pack = jnp.concatenate([w_decay_up[0], w_decay_up[1], a_up[0], a_up[1], v_mix_up[0], v_mix_down[0].T], axis=0)
    pack = _all_gather_rows(pack, "ag_small")
    late_blocks = [w_out[0].astype(bf16), w_in[1].T.astype(bf16), w_out[1].astype(bf16)]
    late_lands = [lax.dynamic_update_slice(lax.empty((N_DEV * blk.shape[0], blk.shape[1]), bf16), blk, (me * blk.shape[0], 0))
                  for blk in late_blocks]
    late_gather = _split_start(late_blocks, late_lands, _gather_plan, len(_GATHER_FLIPS), "ag_late_start",
                               after=(win_t[0], pack))
    pack = jnp.transpose(pack.reshape(N_DEV, -1, shard), (1, 0, 2)).reshape(-1, dr)
    offs = [0, rank_w, 2 * rank_w, 2 * rank_w + rank_a, 2 * rank_w + 2 * rank_a, 2 * rank_w + 2 * rank_a + rank_v,
            2 * rank_w + 2 * rank_a + 2 * rank_v]
    wdu_f = [pack[offs[0]:offs[1]], pack[offs[1]:offs[2]]]
    aup_f = [pack[offs[2]:offs[3]], pack[offs[3]:offs[4]]]
    vup_f = pack[offs[4]:offs[5]]
    vdown_f = pack[offs[5]:offs[6]].T

    def after_start(a, started):
        return a + started[-1][0:1, 0:1]

    def rwkv_params(l):
        mu = after_start(shift_mu[0:1], late_gather) if l == 0 else shift_mu[l:l + 1]
        prm = [mu, w_decay0[l:l + 1], a0[l:l + 1], _pad_rows(wdu_f[l], LANES, True),
               _pad_rows(aup_f[l], LANES, False)]
        if l == 1:
            prm += [v_mix0[0:1], _pad_rows(vdown_f.T, LANES, True).T, _pad_rows(vup_f, LANES, True)]
        rows = jnp.stack([k_k[l], k_a[l], r_k[l], ln_x_w[l], ln_x_b[l]] + [jnp.zeros((dr,), f32)] * 3, axis=0)
        pp = jnp.transpose(rows.reshape(8, n_pair, LANES), (1, 0, 2))
        return tuple(prm), pp

    h = x[0]
    h16 = h.astype(bf16)
    tgt = loss_target[0]
    saved = []
    vfirst = None
    for l in range(DEPTH):
        prm, pp = rwkv_params(l)
        proj = _matmul(h16, win_t[l], "nt", f"mm_proj_{l}", (1024, 1664, 2048))
        if l == 0:
            cat, vfirst, mck = _rwkv_fwd(False, proj, None, prm, pp, d)
        else:
            cat, mck = _rwkv_fwd(True, proj, vfirst, prm, pp, d)
        cat, sck = _hgrn_fwd(l == 1, proj, lb_logits, g_norm_w[l:l + 1], cat, rwc)
        if l == 0:
            _, arrived = _split_wait(late_gather, _gather_plan, cat, "ag_late_wait")
            wout[0], win_t[1], wout[1] = _gather_forward(arrived, "ag_late_forward")
        y = _matmul(cat, wout[l], "nn", f"mm_out_{l}", (1024, 1024, 2048))
        saved.append((h, h16, proj, prm, pp, mck, sck, cat, y))
        if l < DEPTH - 1:
            h, h16 = _ln_fwd(h, y, ln_w[l:l + 1], ln_b[l:l + 1])
        else:
            top = _ln_loss_bwd(h, y, ln_w[l:l + 1], ln_b[l:l + 1], tgt)
    loss = lax.psum(top[4][0, 0], ("x", "y", "c"))

    grads = {}
    big = {}
    dvfirst = None
    d_lbl = None
    rs_started = {}
    for l in reversed(range(DEPTH)):
        h_l, h16_l, proj, prm, pp, mck, sck, cat, y = saved[l]
        if l == DEPTH - 1:
            dy, dy16, g_ln_w, g_ln_b = top[:4]
        else:
            dy, dy16, g_ln_w, g_ln_b = _ln_bwd(h_l, y, after_start(ln_w[l:l + 1], rs_started[l + 1]), ln_b[l:l + 1], dh_out)
        dcat = _matmul(dy16, wout[l], "nt", f"mm_dcat_{l}", (1024, 1024, 2048))
        big[("w_out", l)] = _matmul(cat, dy16, "tn", f"mm_dwout_{l}", (512, 2048, 2048), out_dtype=bf16)
        if l == 1:
            outs = _rwkv_bwd(True, proj, vfirst, prm, pp, mck, dcat, None)
            dproj_r, dvfirst = outs[0], outs[1]
            dprm, dpp = outs[2:-1], outs[-1]
        else:
            outs = _rwkv_bwd(False, proj, None, prm, pp, mck, dcat, dvfirst)
            dproj_r = outs[0]
            dprm, dpp = outs[1:-1], outs[-1]
        dproj, dlbl_l, dgnw = _hgrn_bwd(l == 1, proj, lb_logits, g_norm_w[l:l + 1], sck, dcat, rwc, dproj_r)
        big[("w_in", l)] = _matmul(dproj, h16_l, "tn", f"mm_dwin_{l}", (640, 2048, 2048), out_dtype=bf16)
        sharded = [dprm[3][:rank_w].T, dprm[4][rank_w:].T]
        if l == 1:
            sharded += [dprm[6][:, :rank_v], dprm[7][:rank_v].T,
                        jnp.zeros((dr, LANES - 2 * rank_v), f32)]
        sharded = jnp.concatenate(sharded, axis=1).astype(bf16)
        rs_started[l] = _reduce_scatter_begin([big[("w_in", l)], big[("w_out", l)], sharded], f"l{l}")
        dh_out = _matmul(dproj, win_t[l], "nn", f"mm_dh_{l}", (1024, 1024, 1664), add=dy, add_scale=ALPHA,
                         after=rs_started[l][-1] if l == 0 else None)
        dpp = jnp.transpose(dpp, (1, 0, 2)).reshape(8, dr)
        grads[l] = dict(shift_mu=dprm[0][0], w_decay0=dprm[1][0], a0=dprm[2][0],
                        k_k=dpp[0], k_a=dpp[1], r_k=dpp[2], ln_x_w=dpp[3], ln_x_b=dpp[4],
                        g_norm_w=dgnw[0], ln_w=g_ln_w[0], ln_b=g_ln_b[0])
        if l == 1:
            grads[l].update(v_mix0=dprm[5][0])
            d_lbl = dlbl_l
    grad_x = dh_out[None]

    def both(name):
        return jnp.stack([grads[0][name], grads[1][name]])

    small = dict(shift_mu=both("shift_mu"), w_decay0=both("w_decay0"), a0=both("a0"), k_k=both("k_k"), k_a=both("k_a"),
                 r_k=both("r_k"), ln_x_w=both("ln_x_w"), ln_x_b=both("ln_x_b"), v_mix0=grads[1]["v_mix0"][None],
                 lb_logits=d_lbl, g_norm_w=both("g_norm_w"), ln_w=both("ln_w"), ln_b=both("ln_b"))
    flat = jnp.concatenate([small[nm].reshape(-1) for nm in _SMALL])
    n_flat = flat.shape[0]
    rows = -(-n_flat // (8 * LANES)) * 8
    flat = jnp.concatenate([flat, jnp.zeros((rows * LANES - n_flat,), f32)]).reshape(rows, LANES)
    total = _sum_slots(_all_gather_rows(flat, "ag_small_grads").reshape(N_DEV, rows, LANES), "sum_small_grads").reshape(-1)
    gsm = {}
    off = 0
    for nm in _SMALL:
        size = small[nm].size
        gsm[nm] = total[off:off + size].reshape(small[nm].shape)
        off += size
    reduced = {1: _reduce_scatter_end(rs_started[1], dh_out, "l1")}
    reduced[0] = _reduce_scatter_end(rs_started[0], total, "l0")
    g_w_in_t = jnp.stack([reduced[l][0] for l in range(DEPTH)])
    gsm["w_in"] = jnp.transpose(g_w_in_t, (0, 2, 1))
    gsm["w_out"] = jnp.stack([reduced[l][1] for l in range(DEPTH)])
    gsm["w_decay_up"] = jnp.stack([reduced[l][2][:, :rank_w].T for l in range(DEPTH)])
    gsm["a_up"] = jnp.stack([reduced[l][2][:, rank_w:rank_w + rank_a].T for l in range(DEPTH)])
    gsm["v_mix_down"] = reduced[1][2][:, LANES:LANES + rank_v][None]
    gsm["v_mix_up"] = reduced[1][2][:, LANES + rank_v:LANES + 2 * rank_v].T[None]

    deltas, new_m, new_v = {}, {}, {}
    swap = lambda a: jnp.transpose(a, (0, 2, 1))
    deltas["w_in"], new_m["w_in"], new_v["w_in"] = (
        swap(a) for a in _adamw(swap(w_in), g_w_in_t, swap(m_w_in), swap(v_w_in), "adamw_w_in"))
    deltas["w_out"], new_m["w_out"], new_v["w_out"] = _adamw(w_out, gsm["w_out"], m_w_out, v_w_out, "adamw_w_out")
    rest = [nm for nm in _NAMES if nm not in ("w_in", "w_out")]
    d_rest, m_rest, v_rest = _adamw_many([weights[nm] for nm in rest], [gsm[nm] for nm in rest],
                                         [mom1[nm] for nm in rest], [mom2[nm] for nm in rest], "adamw_small")
    for i, nm in enumerate(rest):
        deltas[nm], new_m[nm], new_v[nm] = d_rest[i], m_rest[i], v_rest[i]
    return (loss, grad_x, *[gsm[nm] for nm in _NAMES], *[deltas[nm] for nm in _NAMES],
            *[new_m[nm] for nm in _NAMES], *[new_v[nm] for nm in _NAMES])
```

```python
import functools

import jax
import jax.numpy as jnp
from jax import lax
from jax.experimental import pallas as pl
from jax.experimental.pallas import tpu as pltpu

f32 = jnp.float32
bf16 = jnp.bfloat16

N_DEV = 8
CHUNK = 64
LANES = 128
RWKV_HEAD = 64
DEPTH = 2
ALPHA = (2 * DEPTH) ** 0.25
LN_EPS = 1e-5
GN_EPS = 64e-5
RMS_EPS = 1e-5
LB_FLOOR = 1e-30
ADAM_LR, ADAM_B1, ADAM_B2, ADAM_EPS, ADAM_WD, ADAM_STEP = 0.001, 0.9, 0.999, 1e-08, 0.01, 10
MESH = pl.DeviceIdType.MESH


def _iota(shape, d):
    return lax.broadcasted_iota(jnp.int32, shape, d)


_DIMS = {"nn": (((1,), (0,)), ((), ())), "nt": (((1,), (1,)), ((), ())), "tn": (((0,), (0,)), ((), ()))}
_BATCH_DIMS = {"nn": (((2,), (1,)), ((0,), (0,))), "nt": (((2,), (2,)), ((0,), (0,))), "tn": (((1,), (1,)), ((0,), (0,)))}
_K_AXES = {"nn": (-1, -2), "nt": (-1, -1), "tn": (-2, -2)}


def _mxu(a, b, mode):
    return lax.dot_general(a, b, (_BATCH_DIMS if a.ndim == 3 else _DIMS)[mode], preferred_element_type=f32)


def _split(x):
    hi = x.astype(bf16)
    return hi, (x - hi.astype(f32)).astype(bf16)


def _mm2_impl(a, b, mode, passes=3):
    ah, al = _split(a)
    if passes == 3:
        bh, bl = _split(b)
        lhs, rhs = [ah, ah, al], [bh, bl, bh]
    else:
        bh = b.astype(bf16)
        lhs, rhs = [ah, al], [bh, bh]
    ka, kb = _K_AXES[mode]
    k = a.shape[ka]
    if k % (LANES if -1 in (ka, kb) else 16) == 0:
        return _mxu(jnp.concatenate(lhs, axis=ka), jnp.concatenate(rhs, axis=kb), mode)
    out = _mxu(lhs[0], rhs[0], mode)
    for x, y in zip(lhs[1:], rhs[1:]):
        out = out + _mxu(x, y, mode)
    return out


@functools.partial(jax.custom_vjp, nondiff_argnums=(2, 3))
def _mm2(a, b, mode, passes=3):
    return _mm2_impl(a, b, mode, passes)


def _mm2_fwd(a, b, mode, passes):
    return _mm2_impl(a, b, mode, passes), (a, b)


def _mm2_bwd(mode, passes, res, g):
    a, b = res
    if mode == "nn":
        return _mm2_impl(g, b, "nt", passes), _mm2_impl(a, g, "tn", passes)
    if mode == "nt":
        return _mm2_impl(g, b, "nn", passes), _mm2_impl(g, a, "tn", passes)
    return _mm2_impl(b, g, "nt", passes), _mm2_impl(a, g, "nn", passes)


_mm2.defvjp(_mm2_fwd, _mm2_bwd)

TRI_PASSES = 2
APPLY_PASSES = 2


def _const_impl(cm, x, mode):
    hi, lo = _split(x)
    if mode in ("r", "rt"):
        shape = x.shape
        hi, lo = hi.reshape(-1, shape[-1]), lo.reshape(-1, shape[-1])
        dims = "nn" if mode == "r" else "nt"
        out = _mxu(hi, cm, dims) + _mxu(lo, cm, dims)
        return out.reshape(shape[:-1] + (out.shape[-1],))
    if x.ndim == 3:
        cm = jnp.broadcast_to(cm, (x.shape[0],) + cm.shape)
    return _mxu(cm, hi, mode) + _mxu(cm, lo, mode)


@jax.custom_vjp
def _const_left(cm, x):
    return _const_impl(cm, x, "nn")


_const_left.defvjp(lambda cm, x: (_const_impl(cm, x, "nn"), cm),
                   lambda cm, g: (jnp.zeros_like(cm), _const_impl(cm, g, "tn")))


@jax.custom_vjp
def _const_right(x, cm):
    return _const_impl(cm, x, "r")


_const_right.defvjp(lambda x, cm: (_const_impl(cm, x, "r"), cm),
                    lambda cm, g: (_const_impl(cm, g, "rt"), jnp.zeros_like(cm)))


def _tri_inv(a):
    n = a.shape[-1]
    tm = (_iota((n, n), 0) == _iota((n, n), 1)).astype(f32) + a
    ak = a
    for _ in range(5):
        ak = _mm2_impl(ak, ak, "nn", TRI_PASSES)
        tm = tm + _mm2_impl(tm, ak, "nn", TRI_PASSES)
    return tm


@jax.custom_vjp
def _tri_solve(tm, a, x):
    del a
    return _mm2_impl(tm, x, "nn")


def _tri_solve_fwd(tm, a, x):
    u = _mm2_impl(tm, x, "nn")
    return u, (tm, u)


def _tri_solve_bwd(res, du):
    tm, u = res
    dx = _mm2_impl(tm, du, "tn")
    return jnp.zeros_like(tm), _mm2_impl(dx, u, "nt"), dx


_tri_solve.defvjp(_tri_solve_fwd, _tri_solve_bwd)


def _col_of_row(row_vec):
    n = row_vec.shape[-1]
    eye = _iota((n, n), 0) == _iota((n, n), 1)
    return jnp.sum(jnp.where(eye, jnp.broadcast_to(row_vec, row_vec.shape[:-2] + (n, n)), 0.0), axis=-1, keepdims=True)


def _softplus(x):
    return jnp.maximum(x, 0.0) + jnp.log1p(jnp.exp(-jnp.abs(x)))


def _log_sigmoid(x):
    return -_softplus(-x)


def _logaddexp(a, b):
    return jnp.maximum(a, b) + jnp.log1p(jnp.exp(-jnp.abs(a - b)))


def _silu(x):
    return x * jax.nn.sigmoid(x)


def _tril(c, strict):
    r, s = _iota((c, c), 0), _iota((c, c), 1)
    return (r > s) if strict else (r >= s)


def _last_row(a):
    c = a.shape[-2]
    return jnp.sum(jnp.where(_iota(a.shape, a.ndim - 2) == c - 1, a, 0.0), axis=-2, keepdims=True)


def _rwkv_pre(layer1, prm, y, prev, vf):
    c = y.shape[0]
    if layer1:
        mu, w0, a0, wup, aup, v0, vdown, vup = prm
    else:
        mu, w0, a0, wup, aup = prm
    dr = w0.shape[1]
    shift = (_iota((c, c), 0) == _iota((c, c), 1) + 1).astype(bf16)
    y_prev = _const_left(shift, y) + jnp.where(_iota((c, 1), 0) == 0, prev, 0.0)
    rw = y + mu * (y_prev - y)
    r, k, v, z = (rw[:, i * dr:(i + 1) * dr] for i in range(4))
    wdad = rw[:, 4 * dr:4 * dr + LANES]
    w_raw = w0 + _mm2(jnp.tanh(wdad), wup, "nn")
    lw = -jnp.exp(-_softplus(-w_raw) - 0.5)
    asig = jax.nn.sigmoid(a0 + _mm2(wdad, aup, "nn"))
    if layer1:
        v = v + (vf - v) * jax.nn.sigmoid(v0 + _mm2(_mm2(v, vdown, "nn"), vup, "nn"))
    return r, k, v, z, lw, asig


def _rwkv_pair(pp, m0, xs, tm=None):
    kkw, kaw, rkw, gnw, gnb = pp
    r, k, v, z, lw, asig = xs
    c = r.shape[-2]
    n2 = 2 * c
    lane = _iota((1, LANES), 1)
    mh0, mh1 = (lane < RWKV_HEAD).astype(f32), (lane >= RWKV_HEAD).astype(f32)
    same_head = _iota((LANES, LANES), 0) // RWKV_HEAD == _iota((LANES, LANES), 1) // RWKV_HEAD
    g = same_head.astype(bf16)

    def seg(x):
        return _const_right(x, g)

    def stack(x):
        return jnp.concatenate([x * mh0, x * mh1], axis=-2)

    kk = k * kkw
    kk = kk / jnp.maximum(jnp.sqrt(seg(kk * kk)), 1e-12)
    k2 = k * (1.0 + (asig - 1.0) * kaw)
    a = -kk
    b = kk * asig
    cum = _const_left(_tril(c, False).astype(bf16), lw)
    at = stack(a * jnp.exp(cum - lw))
    rt = stack(r * jnp.exp(cum))
    en = jnp.exp(-cum)
    sc = _mm2(jnp.concatenate([at, rt], axis=-2), jnp.concatenate([stack(b * en), stack(k2 * en)], axis=-2), "nt")
    row, col = _iota((n2, n2), 0), _iota((n2, n2), 1)
    same = row // c == col // c
    strict = same & (row % c > col % c)
    incl = same & (row % c >= col % c)
    aab = jnp.where(strict, sc[..., :n2, :n2], 0.0)
    aak = jnp.where(strict, sc[..., :n2, n2:], 0.0)
    arb = jnp.where(incl, sc[..., n2:, :n2], 0.0)
    ark = jnp.where(incl, sc[..., n2:, n2:], 0.0)
    vv = jnp.concatenate([v, v], axis=-2)
    mask_st = jnp.concatenate([jnp.broadcast_to(mh0, (c, LANES)), jnp.broadcast_to(mh1, (c, LANES))], axis=0)
    x_st = _mm2(jnp.concatenate([at, aak], axis=-1), jnp.concatenate([m0, vv], axis=-2), "nn", APPLY_PASSES)
    if tm is None:
        tm = _tri_inv(lax.stop_gradient(aab))
    u_st = _tri_solve(tm, aab, x_st) * mask_st
    o_st = _mm2(jnp.concatenate([rt, arb, ark], axis=-1), jnp.concatenate([m0, u_st, vv], axis=-2), "nn", APPLY_PASSES) * mask_st
    u = u_st[..., :c, :] + u_st[..., c:, :]
    o = o_st[..., :c, :] + o_st[..., c:, :]
    cum_last = _last_row(cum)
    dec_end = jnp.exp(cum_last - cum)
    m_new = _col_of_row(jnp.exp(cum_last)) * m0 + _mm2(
        jnp.concatenate([b * dec_end, k2 * dec_end], axis=-2), jnp.concatenate([u, v], axis=-2), "tn", APPLY_PASSES) * same_head.astype(f32)
    mean = seg(o) * (1.0 / RWKV_HEAD)
    d = o - mean
    var = seg(d * d) * (1.0 / RWKV_HEAD)
    on = d * lax.rsqrt(var + GN_EPS) * gnw + gnb
    bonus = seg(r * k2 * rkw) * v
    return (on + bonus) * _silu(z), m_new, tm


def _split_lanes(a, n):
    return [a[:, i * LANES:(i + 1) * LANES] for i in range(n)]


def _rwkv_step(layer1, prm, y, prev, vf, pp, m0, tm=None):
    xs = _rwkv_pre(layer1, prm, y, prev, vf)
    n_pair = m0.shape[0]
    og, m_new, tm = _rwkv_pair(pp, m0, tuple(jnp.concatenate([p[None] for p in _split_lanes(a, n_pair)], axis=0) for a in xs), tm)
    return og, m_new, xs[2], tm


def _group(n):
    return n


def _rwkv_specs(layer1, t, dr, rwc, n_pair, rev):
    nc = t // CHUNK
    grp = _group(n_pair)

    def cidx(c):
        return (nc - 1 - c) if rev else c

    full = lambda shape: pl.BlockSpec(shape, lambda c, p: tuple(0 for _ in shape))
    specs = [
        pl.BlockSpec((CHUNK, rwc), lambda c, p: (cidx(c), 0)),
        pl.BlockSpec((8, rwc), lambda c, p: (jnp.maximum(cidx(c) * (CHUNK // 8) - 1, 0), 0)),
    ]
    if layer1:
        specs.append(pl.BlockSpec((CHUNK, dr), lambda c, p: (cidx(c), 0)))
    prm_shapes = [(1, rwc), (1, dr), (1, dr), (LANES, dr), (LANES, dr)]
    if layer1:
        prm_shapes += [(1, dr), (dr, LANES), (LANES, dr)]
    specs += [full(s) for s in prm_shapes]
    specs.append(pl.BlockSpec((grp, 8, LANES), lambda c, p: (p, 0, 0)))
    return specs, prm_shapes, cidx, full


def _rwkv_fwd(layer1, proj, vf, prm, pp, cat_width):
    t = proj.shape[0]
    dr = prm[1].shape[1]
    rwc = prm[0].shape[1]
    n_pair = dr // LANES
    nc = t // CHUNK
    n_prm = len(prm)
    specs, _, _, _ = _rwkv_specs(layer1, t, dr, rwc, n_pair, False)

    def body(*refs):
        y_ref, prev_ref = refs[0], refs[1]
        i = 2
        vf_ref = None
        if layer1:
            vf_ref = refs[i]
            i += 1
        prm_refs = refs[i:i + n_prm]
        i += n_prm
        pp_ref = refs[i]
        i += 1
        cat_ref = refs[i]
        i += 1
        vout_ref = None
        if not layer1:
            vout_ref = refs[i]
            i += 1
        mck_ref, m_s = refs[i], refs[i + 1]
        c = pl.program_id(0)

        @pl.when(c == 0)
        def _():
            m_s[...] = jnp.zeros_like(m_s)

        prev = prev_ref[pl.ds(7, 1), :] * (c != 0).astype(f32)
        m0 = m_s[...]
        ppv = tuple(pp_ref[:, pl.ds(q, 1), :] for q in range(5))
        og, m_new, v, tm = _rwkv_step(layer1, tuple(r[...] for r in prm_refs), y_ref[...], prev,
                                      vf_ref[...] if layer1 else None, ppv, m0)
        mck_ref[0, :n_pair] = m0
        mck_ref[0, n_pair:] = tm
        if not layer1:
            vout_ref[...] = v
        for j in range(n_pair):
            cat_ref[:, j * LANES:(j + 1) * LANES] = og[j]
        m_s[...] = m_new

    grp = _group(n_pair)
    assert grp == n_pair
    out_shape = [jax.ShapeDtypeStruct((t, cat_width), f32)]
    out_specs = [pl.BlockSpec((CHUNK, grp * LANES), lambda c, p: (c, p))]
    if not layer1:
        out_shape.append(jax.ShapeDtypeStruct((t, dr), f32))
        out_specs.append(pl.BlockSpec((CHUNK, dr), lambda c, p: (c, 0)))
    out_shape.append(jax.ShapeDtypeStruct((nc, 2 * n_pair, LANES, LANES), f32))
    out_specs.append(pl.BlockSpec((1, 2 * grp, LANES, LANES), lambda c, p: (c, p, 0, 0)))
    args = [proj, proj] + ([vf] if layer1 else []) + list(prm) + [pp]
    return pl.pallas_call(
        body, grid=(nc, 1), in_specs=specs, out_specs=out_specs, out_shape=out_shape,
        scratch_shapes=[pltpu.VMEM((n_pair, LANES, LANES), f32)],
        compiler_params=pltpu.CompilerParams(dimension_semantics=("arbitrary", "arbitrary")),
        name=f"rwkv_fwd_l{int(layer1)}",
    )(*args)


def _rwkv_bwd(layer1, proj, vf, prm, pp, mck, dcat, dvout):
    t = proj.shape[0]
    dr = prm[1].shape[1]
    rwc = prm[0].shape[1]
    n_pair = dr // LANES
    nc = t // CHUNK
    n_prm = len(prm)
    specs, prm_shapes, cidx, full = _rwkv_specs(layer1, t, dr, rwc, n_pair, True)
    grp = _group(n_pair)
    assert grp == n_pair
    specs.append(pl.BlockSpec((1, 2 * grp, LANES, LANES), lambda c, p: (cidx(c), p, 0, 0)))
    specs.append(pl.BlockSpec((CHUNK, grp * LANES), lambda c, p: (cidx(c), p)))
    if not layer1:
        specs.append(pl.BlockSpec((CHUNK, dr), lambda c, p: (cidx(c), 0)))

    def body(*refs):
        y_ref, prev_ref = refs[0], refs[1]
        i = 2
        vf_ref = None
        if layer1:
            vf_ref = refs[i]
            i += 1
        prm_refs = refs[i:i + n_prm]
        i += n_prm
        pp_ref, mck_ref, dog_ref = refs[i], refs[i + 1], refs[i + 2]
        i += 3
        dvout_ref = None
        if not layer1:
            dvout_ref = refs[i]
            i += 1
        dy_ref = refs[i]
        i += 1
        dvf_ref = None
        if layer1:
            dvf_ref = refs[i]
            i += 1
        dprm_refs = refs[i:i + n_prm]
        i += n_prm
        dpp_ref = refs[i]
        dm_s, dprev_s = refs[i + 1:i + 3]
        c = pl.program_id(0)
        cr = nc - 1 - c

        @pl.when(c == 0)
        def _():
            dm_s[...] = jnp.zeros_like(dm_s)
            dprev_s[...] = jnp.zeros_like(dprev_s)
            dpp_ref[...] = jnp.zeros_like(dpp_ref)
            for r in dprm_refs:
                r[...] = jnp.zeros_like(r)

        prev = prev_ref[pl.ds(7, 1), :] * (cr != 0).astype(f32)
        prm_v = tuple(r[...] for r in prm_refs)
        ppv = tuple(pp_ref[:, pl.ds(q, 1), :] for q in range(5))
        dog = jnp.stack([dog_ref[:, j * LANES:(j + 1) * LANES] for j in range(n_pair)], axis=0)
        m0, tm = mck_ref[0, :n_pair], mck_ref[0, n_pair:]
        no_tm = jnp.zeros_like(tm)
        if layer1:
            _, vjp = jax.vjp(lambda a, b, d, e, g, h: _rwkv_step(True, a, b, d, e, g, h, tm),
                             prm_v, y_ref[...], prev, vf_ref[...], ppv, m0)
            dprm, dy, dprev, dvf, dppv, dm0 = vjp((dog, dm_s[...], jnp.zeros((CHUNK, dr), f32), no_tm))
            dvf_ref[...] = dvf
        else:
            _, vjp = jax.vjp(lambda a, b, d, e, g: _rwkv_step(False, a, b, d, None, e, g, tm), prm_v, y_ref[...], prev, ppv, m0)
            dprm, dy, dprev, dppv, dm0 = vjp((dog, dm_s[...], dvout_ref[...], no_tm))
        dm_s[...] = dm0
        for q in range(5):
            dpp_ref[:, pl.ds(q, 1), :] += dppv[q]
        dy_ref[...] = (dy + jnp.where(_iota((CHUNK, 1), 0) == CHUNK - 1, dprev_s[...], 0.0)).astype(bf16)
        dprev_s[...] = dprev
        for r, gval in zip(dprm_refs, dprm):
            r[...] += gval

    out_shape = [jax.ShapeDtypeStruct((t, proj.shape[1]), bf16)]
    out_specs = [pl.BlockSpec((CHUNK, rwc), lambda c, p: (cidx(c), 0))]
    if layer1:
        out_shape.append(jax.ShapeDtypeStruct((t, dr), f32))
        out_specs.append(pl.BlockSpec((CHUNK, dr), lambda c, p: (cidx(c), 0)))
    out_shape += [jax.ShapeDtypeStruct(s, f32) for s in prm_shapes]
    out_specs += [full(s) for s in prm_shapes]
    out_shape.append(jax.ShapeDtypeStruct((n_pair, 8, LANES), f32))
    out_specs.append(full((n_pair, 8, LANES)))
    args = [proj, proj] + ([vf] if layer1 else []) + list(prm) + [pp, mck, dcat] + ([] if layer1 else [dvout])
    return pl.pallas_call(
        body, grid=(nc, 1), in_specs=specs, out_specs=out_specs, out_shape=out_shape,
        scratch_shapes=[pltpu.VMEM((n_pair, LANES, LANES), f32), pltpu.VMEM((1, rwc), f32)],
        compiler_params=pltpu.CompilerParams(dimension_semantics=("arbitrary", "arbitrary")),
        name=f"rwkv_bwd_l{int(layer1)}",
    )(*args)


def _hgrn_chunk(layer1, lbl, gnw, s0, q_raw, f_raw, i_in, z):
    c = q_raw.shape[-2]
    q = _silu(q_raw)
    ls = _log_sigmoid(f_raw)
    if layer1:
        l0, l1 = lbl[..., 0:1, :], lbl[..., 1:2, :]
        mx = jnp.maximum(l0, l1)
        e0, e1 = jnp.exp(l0 - mx), jnp.exp(l1 - mx)
        sm0, sm1 = e0 / (e0 + e1), e1 / (e0 + e1)
        lb = (sm0 + sm1) - sm0
        log_f = _logaddexp(jnp.log(jnp.maximum(lb, LB_FLOOR)), jnp.log1p(-lb) + ls)
        k = (1.0 - lb) * jax.nn.sigmoid(-f_raw)
    else:
        log_f = _logaddexp(jnp.full_like(ls, jnp.log(jnp.float32(LB_FLOOR))), ls)
        k = jax.nn.sigmoid(-f_raw)
    row, col = _iota((c, c), 0), _iota((c, c), 1)
    trow = _iota((c, 1), 0)
    halves = []
    half = c // 2
    while half >= 1:
        halves.append(half)
        half //= 2
    cmat = jnp.concatenate([(col <= row).astype(f32)]
                           + [(col <= (row // (2 * hf)) * (2 * hf) + hf - 1).astype(f32) for hf in halves], axis=0)
    ball = _const_left(cmat.astype(bf16), log_f)
    b = ball[..., :c, :]
    att = None
    for lvl, hf in enumerate(halves):
        blk = 2 * hf
        bref = ball[..., (lvl + 1) * c:(lvl + 2) * c, :]
        upper = (trow % blk) >= hf
        dec = jnp.exp(jnp.where(upper, b - bref, bref - b))
        qh = jnp.where(upper, q * dec, 0.0)
        kh = jnp.where(upper, 0.0, k * dec)
        term = jnp.where(row // blk == col // blk, _mm2(qh, kh, "nt", APPLY_PASSES), 0.0)
        att = term if att is None else att + term
    lhs = jnp.concatenate([q * jnp.exp(b), att, jnp.zeros(att.shape[:-1] + (LANES - c,), f32)], axis=-1)
    rhs = jnp.concatenate([s0, i_in, jnp.zeros(i_in.shape[:-2] + (LANES - c, i_in.shape[-1]), f32)], axis=-2)
    o = _mm2(lhs, rhs, "nn", APPLY_PASSES) + jnp.sum(q * k, axis=-1, keepdims=True) * i_in
    b_last = _last_row(b)
    s_new = _col_of_row(jnp.exp(b_last)) * s0 + _mm2(k * jnp.exp(b_last - b), i_in, "tn", APPLY_PASSES)
    o = o * lax.rsqrt(jnp.mean(o * o, axis=-1, keepdims=True) + RMS_EPS)
    return o * gnw * _silu(z), s_new


def _hgrn_in_specs(t, dh, col0, rev):
    nc = t // CHUNK
    nh = dh // LANES

    def cidx(c):
        return (nc - 1 - c) if rev else c

    grp = _group(nh)
    specs = [pl.BlockSpec((CHUNK, LANES), functools.partial(lambda g, j, h, c: (cidx(c), col0 + g * nh + h * grp + j), g, j))
             for j in range(grp) for g in range(4)]
    specs.append(pl.BlockSpec((2, grp * LANES), lambda h, c: (0, h)))
    specs.append(pl.BlockSpec((1, grp * LANES), lambda h, c: (0, h)))
    return specs, cidx, grp


def _hgrn_fwd(layer1, proj, lbl, gnw, cat, rwc):
    t, d = cat.shape
    dh = gnw.shape[1]
    nh = dh // LANES
    nc = t // CHUNK
    col0 = rwc // LANES
    specs, _, grp = _hgrn_in_specs(t, dh, col0, False)
    specs.append(pl.BlockSpec(memory_space=pl.ANY))
    assert (d - dh) % (grp * LANES) == 0
    cat_col0 = (d - dh) // (grp * LANES)

    def body(*refs):
        x_refs = refs[:4 * grp]
        lbl_ref, gnw_ref, _, cat_ref, sck_ref, s_s = refs[4 * grp:]
        c = pl.program_id(1)

        @pl.when(c == 0)
        def _():
            s_s[...] = jnp.zeros_like(s_s)

        lanes = [slice(j * LANES, (j + 1) * LANES) for j in range(grp)]
        s0 = s_s[...]
        sck_ref[:, 0] = s0
        out, s_new = _hgrn_chunk(layer1, jnp.stack([lbl_ref[:, ln] for ln in lanes]), jnp.stack([gnw_ref[:, ln] for ln in lanes]),
                                 s0, *(jnp.stack([x_refs[4 * j + g][...] for j in range(grp)]) for g in range(4)))
        for j in range(grp):
            cat_ref[:, lanes[j]] = out[j]
        s_s[...] = s_new

    return pl.pallas_call(
        body, grid=(nh // grp, nc), in_specs=specs,
        out_specs=[pl.BlockSpec((CHUNK, grp * LANES), lambda h, c: (c, cat_col0 + h)),
                   pl.BlockSpec((grp, 1, LANES, LANES), lambda h, c: (h, c, 0, 0))],
        out_shape=[jax.ShapeDtypeStruct((t, d), f32), jax.ShapeDtypeStruct((nh, nc, LANES, LANES), f32)],
        scratch_shapes=[pltpu.VMEM((grp, LANES, LANES), f32)],
        input_output_aliases={4 * grp + 2: 0},
        compiler_params=pltpu.CompilerParams(dimension_semantics=("arbitrary", "arbitrary")),
        name=f"hgrn_fwd_l{int(layer1)}",
    )(*([proj] * (4 * grp)), lbl, gnw, cat)


def _hgrn_bwd(layer1, proj, lbl, gnw, sck, dcat, rwc, dproj):
    t, d = dcat.shape
    dh = gnw.shape[1]
    nh = dh // LANES
    nc = t // CHUNK
    col0 = rwc // LANES
    specs, cidx, grp = _hgrn_in_specs(t, dh, col0, True)
    assert grp == nh and (d - dh) % (grp * LANES) == 0
    cat_col0 = (d - dh) // (grp * LANES)
    specs.append(pl.BlockSpec((grp, 1, LANES, LANES), lambda h, c: (h, cidx(c), 0, 0)))
    specs.append(pl.BlockSpec((CHUNK, grp * LANES), lambda h, c: (cidx(c), cat_col0 + h)))
    specs.append(pl.BlockSpec(memory_space=pl.ANY))

    def body(*refs):
        x_refs = refs[:4 * grp]
        lbl_ref, gnw_ref, sck_ref, do_ref, _, dp_hbm, dlbl_ref, dgnw_ref, ds_s, stage, sems = refs[4 * grp:]
        c = pl.program_id(1)
        slot = c % 2

        def put(s, g, chunk):
            return pltpu.make_async_copy(stage.at[s, g], dp_hbm.at[pl.ds(chunk * CHUNK, CHUNK), pl.ds(rwc + g * dh, dh)],
                                         sems.at[s, g])

        @pl.when(c == 0)
        def _():
            ds_s[...] = jnp.zeros_like(ds_s)
            dlbl_ref[...] = jnp.zeros_like(dlbl_ref)
            dgnw_ref[...] = jnp.zeros_like(dgnw_ref)

        @pl.when(c >= 2)
        def _():
            for g in range(4):
                put(slot, g, 0).wait()

        lanes = [slice(j * LANES, (j + 1) * LANES) for j in range(grp)]
        _, vjp = jax.vjp(functools.partial(_hgrn_chunk, layer1),
                         jnp.stack([lbl_ref[:, ln] for ln in lanes]), jnp.stack([gnw_ref[:, ln] for ln in lanes]), sck_ref[:, 0],
                         *(jnp.stack([x_refs[4 * j + g][...] for j in range(grp)]) for g in range(4)))
        dlbl, dgnw, ds0, dq, df, di, dz = vjp((jnp.stack([do_ref[:, ln] for ln in lanes]), ds_s[...]))
        ds_s[...] = ds0
        for j in range(grp):
            dlbl_ref[:, lanes[j]] += dlbl[j]
            dgnw_ref[:, lanes[j]] += dgnw[j]
            for g, val in enumerate((dq, df, di, dz)):
                stage[slot, g, :, lanes[j]] = val[j].astype(bf16)
        for g in range(4):
            put(slot, g, nc - 1 - c).start()

        @pl.when(c == nc - 1)
        def _():
            for g in range(4):
                put(slot, g, 0).wait()
                if nc >= 2:
                    put(1 - slot, g, 0).wait()

    return pl.pallas_call(
        body, grid=(1, nc), in_specs=specs,
        out_specs=[pl.BlockSpec(memory_space=pl.ANY),
                   pl.BlockSpec((2, grp * LANES), lambda h, c: (0, h)),
                   pl.BlockSpec((1, grp * LANES), lambda h, c: (0, h))],
        out_shape=[jax.ShapeDtypeStruct(dproj.shape, dproj.dtype), jax.ShapeDtypeStruct((2, dh), f32),
                   jax.ShapeDtypeStruct((1, dh), f32)],
        scratch_shapes=[pltpu.VMEM((grp, LANES, LANES), f32), pltpu.VMEM((2, 4, CHUNK, dh), bf16),
                        pltpu.SemaphoreType.DMA((2, 4))],
        input_output_aliases={4 * grp + 4: 0},
        compiler_params=pltpu.CompilerParams(dimension_semantics=("arbitrary", "arbitrary")),
        name=f"hgrn_bwd_l{int(layer1)}",
    )(*([proj] * (4 * grp)), lbl, gnw, sck, dcat, dproj)


def _ln(h, y, w, b):
    u = ALPHA * h + y
    mu = jnp.mean(u, axis=-1, keepdims=True)
    var = jnp.mean(jnp.square(u - mu), axis=-1, keepdims=True)
    return (u - mu) * lax.rsqrt(var + LN_EPS) * w + b


def _row_tile(t):
    return 256 if t % 256 == 0 else t


def _ln_fwd(h, y, w, b):
    t, d = h.shape
    tr = _row_tile(t)

    def body(h_ref, y_ref, w_ref, b_ref, o_ref, o16_ref):
        out = _ln(h_ref[...], y_ref[...], w_ref[...], b_ref[...])
        o_ref[...] = out
        o16_ref[...] = out.astype(bf16)

    row = pl.BlockSpec((tr, d), lambda i: (i, 0))
    vec = pl.BlockSpec((1, d), lambda i: (0, 0))
    return pl.pallas_call(body, grid=(t // tr,), in_specs=[row, row, vec, vec], out_specs=[row, row],
                          out_shape=[jax.ShapeDtypeStruct((t, d), f32), jax.ShapeDtypeStruct((t, d), bf16)],
                          name="ln_fwd")(h, y, w, b)


def _ln_loss_bwd(h, y, w, b, tgt):
    t, d = h.shape
    tr = _row_tile(t)

    def body(h_ref, y_ref, w_ref, b_ref, t_ref, dy_ref, dy16_ref, dw_ref, db_ref, loss_ref):
        @pl.when(pl.program_id(0) == 0)
        def _():
            dw_ref[...] = jnp.zeros_like(dw_ref)
            db_ref[...] = jnp.zeros_like(db_ref)
            loss_ref[...] = jnp.zeros_like(loss_ref)

        out, vjp = jax.vjp(lambda yy, ww, bb: _ln(h_ref[...], yy, ww, bb), y_ref[...], w_ref[...], b_ref[...])
        err = out - t_ref[...]
        loss_ref[...] += 0.5 * jnp.sum(jnp.mean(jnp.square(err), axis=-1, keepdims=True), axis=0, keepdims=True)
        dy, dw, db = vjp(err * (1.0 / d))
        dy_ref[...] = dy
        dy16_ref[...] = dy.astype(bf16)
        dw_ref[...] += dw
        db_ref[...] += db

    row = pl.BlockSpec((tr, d), lambda i: (i, 0))
    vec = pl.BlockSpec((1, d), lambda i: (0, 0))
    return pl.pallas_call(
        body, grid=(t // tr,), in_specs=[row, row, vec, vec, row],
        out_specs=[row, row, vec, vec, pl.BlockSpec((1, LANES), lambda i: (0, 0))],
        out_shape=[jax.ShapeDtypeStruct((t, d), f32), jax.ShapeDtypeStruct((t, d), bf16), jax.ShapeDtypeStruct((1, d), f32),
                   jax.ShapeDtypeStruct((1, d), f32), jax.ShapeDtypeStruct((1, LANES), f32)],
        compiler_params=pltpu.CompilerParams(dimension_semantics=("arbitrary",)), name="ln_loss_bwd")(h, y, w, b, tgt)


def _ln_bwd(h, y, w, b, dout):
    t, d = h.shape
    tr = _row_tile(t)

    def body(h_ref, y_ref, w_ref, b_ref, do_ref, dy_ref, dy16_ref, dw_ref, db_ref):
        @pl.when(pl.program_id(0) == 0)
        def _():
            dw_ref[...] = jnp.zeros_like(dw_ref)
            db_ref[...] = jnp.zeros_like(db_ref)

        _, vjp = jax.vjp(lambda yy, ww, bb: _ln(h_ref[...], yy, ww, bb), y_ref[...], w_ref[...], b_ref[...])
        dy, dw, db = vjp(do_ref[...])
        dy_ref[...] = dy
        dy16_ref[...] = dy.astype(bf16)
        dw_ref[...] += dw
        db_ref[...] += db

    row = pl.BlockSpec((tr, d), lambda i: (i, 0))
    vec = pl.BlockSpec((1, d), lambda i: (0, 0))
    return pl.pallas_call(
        body, grid=(t // tr,), in_specs=[row, row, vec, vec, row], out_specs=[row, row, vec, vec],
        out_shape=[jax.ShapeDtypeStruct((t, d), f32), jax.ShapeDtypeStruct((t, d), bf16),
                   jax.ShapeDtypeStruct((1, d), f32), jax.ShapeDtypeStruct((1, d), f32)],
        compiler_params=pltpu.CompilerParams(dimension_semantics=("arbitrary",)), name="ln_bwd")(h, y, w, b, dout)


def _pick(n, prefs):
    for p in prefs:
        if n % p == 0:
            return p
    return n


def _tile(n, want):
    if n <= want:
        return n
    for cand in range(want - want % LANES, 0, -LANES):
        if n % cand == 0:
            return cand
    return n


_MM_TILES = {"proj": (1024, 1664, 2048), "out": (1024, 1024, 2048), "dcat": (1024, 1024, 2048),
             "dwout": (512, 2048, 2048), "dwin": (640, 2048, 2048), "dh": (1024, 1024, 1664)}


def _matmul(a, b, mode, name, tiles, add=None, add_scale=1.0, out_dtype=f32, after=None):
    if mode == "nn":
        (m, k), n = a.shape, b.shape[1]
    elif mode == "nt":
        (m, k), n = a.shape, b.shape[0]
    else:
        (k, m), n = a.shape, b.shape[1]
    tm, tn, tk = _tile(m, tiles[0]), _tile(n, tiles[1]), _tile(k, tiles[2])
    nk = k // tk
    cache_a = nk == 1 and a.dtype != bf16 and n // tn > 1

    def body(*refs):
        a_ref, b_ref = refs[0], refs[1]
        add_ref = refs[2] if add is not None else None
        n_in = 2 + (add is not None) + (after is not None)
        o_ref = refs[n_in]
        scratch = refs[n_in + 1:]

        def finish(res):
            if add is not None:
                res = res + add_scale * add_ref[...]
            o_ref[...] = res.astype(out_dtype)

        if cache_a:
            a_bf = scratch[0]

            @pl.when(pl.program_id(1) == 0)
            def _():
                a_bf[...] = a_ref[...].astype(bf16)

            a_val = a_bf[...]
        else:
            a_val = a_ref[...].astype(bf16)
        prod = lax.dot_general(a_val, b_ref[...].astype(bf16), _DIMS[mode], preferred_element_type=f32)
        if nk == 1:
            finish(prod)
        else:
            acc = scratch[-1]
            kk = pl.program_id(2)

            @pl.when(kk == 0)
            def _():
                acc[...] = prod

            @pl.when(kk != 0)
            def _():
                acc[...] += prod

            @pl.when(kk == nk - 1)
            def _():
                finish(acc[...])

    a_shape = (tk, tm) if mode == "tn" else (tm, tk)
    a_spec = pl.BlockSpec(a_shape, (lambda i, j, kk: (kk, i)) if mode == "tn" else (lambda i, j, kk: (i, kk)))
    b_spec = pl.BlockSpec((tn, tk), lambda i, j, kk: (j, kk)) if mode == "nt" else pl.BlockSpec((tk, tn), lambda i, j, kk: (kk, j))
    o_spec = pl.BlockSpec((tm, tn), lambda i, j, kk: (i, j))
    in_specs = [a_spec, b_spec] + ([o_spec] if add is not None else []) + ([pl.BlockSpec(memory_space=pl.ANY)] if after is not None else [])
    args = [a, b] + ([add] if add is not None else []) + ([after] if after is not None else [])
    scratch_shapes = ([pltpu.VMEM(a_shape, bf16)] if cache_a else []) + ([pltpu.VMEM((tm, tn), f32)] if nk > 1 else [])
    return pl.pallas_call(
        body, grid=(m // tm, n // tn, nk), in_specs=in_specs, out_specs=o_spec,
        out_shape=jax.ShapeDtypeStruct((m, n), out_dtype), scratch_shapes=scratch_shapes,
        compiler_params=pltpu.CompilerParams(dimension_semantics=("parallel", "arbitrary", "arbitrary")),
        name=name,
    )(*args)


def _position():
    return lax.axis_index("x"), lax.axis_index("y"), lax.axis_index("c")


def _flip(pos, k):
    x, y, c = pos
    return (1 - x if k & 4 else x, 1 - y if k & 2 else y, 1 - c if k & 1 else c)


def _index(pos):
    return 4 * pos[0] + 2 * pos[1] + pos[2]


def _all_gather_rows(x, name):
    m_per, n = x.shape

    def body(x_ref, out_ref, send_sems, recv_sems, local_sem):
        me = _position()
        sibling = _flip(me, 1)
        chips = (2, 4, 6)

        def rows(pos):
            return out_ref.at[pl.ds(_index(pos) * m_per, m_per), :]

        def copy(sem, block, to, src=None):
            return pltpu.make_async_remote_copy(
                src_ref=rows(block) if src is None else src, dst_ref=rows(block),
                send_sem=send_sems.at[sem], recv_sem=recv_sems.at[sem], device_id=to, device_id_type=MESH)

        mine = pltpu.make_async_copy(x_ref, rows(me), local_sem)
        mine.start()
        first = [copy(0, me, sibling, src=x_ref)]
        first += [copy(1 + j, me, _flip(me, k), src=x_ref) for j, k in enumerate(chips)]
        for cp in first:
            cp.start()
        passed = [copy(4 + j, _flip(me, k), sibling) for j, k in enumerate(chips)]
        for j, k in enumerate(chips):
            copy(1 + j, _flip(me, k), me).wait_recv()
            passed[j].start()
        copy(0, sibling, me).wait_recv()
        for j, k in enumerate(chips):
            copy(4 + j, _flip(sibling, k), me).wait_recv()
        for cp in first + passed:
            cp.wait_send()
        mine.wait()

    return pl.pallas_call(
        body, out_shape=jax.ShapeDtypeStruct((N_DEV * m_per, n), x.dtype),
        in_specs=[pl.BlockSpec(memory_space=pl.ANY)], out_specs=pl.BlockSpec(memory_space=pl.ANY),
        scratch_shapes=[pltpu.SemaphoreType.DMA((7,)), pltpu.SemaphoreType.DMA((7,)), pltpu.SemaphoreType.DMA(())],
        name=name,
    )(x)


def _split_start(srcs, lands, plan, n_copies, name, after=()):
    n_arr = len(srcs)
    n_after = len(after)
    hbm = pl.BlockSpec(memory_space=pltpu.HBM)
    sem = pl.BlockSpec(memory_space=pltpu.SEMAPHORE)

    def body(*refs):
        src_refs, land_refs = refs[:n_arr], refs[n_arr:2 * n_arr]
        outs_at = 2 * n_arr + n_after
        send_sems, recv_sems = refs[outs_at:outs_at + n_arr], refs[outs_at + n_arr:outs_at + 2 * n_arr]
        token = refs[-1]
        me = _position()
        for i in range(n_arr):
            for j, (src, dst, peer, _) in enumerate(plan(i, src_refs[i], land_refs[i], me)):
                pltpu.make_async_remote_copy(src_ref=src, dst_ref=dst, send_sem=send_sems[i].at[j], recv_sem=recv_sems[i].at[j],
                                             device_id=peer, device_id_type=MESH).start()
        token[...] = jnp.zeros_like(token)

    outs = pl.pallas_call(
        body, name=name,
        out_shape=([pltpu.SemaphoreType.DMA((n_copies,))] * (2 * n_arr)
                   + [pltpu.HBM(a.shape, a.dtype) for a in list(srcs) + list(lands)]
                   + [jax.ShapeDtypeStruct((8, LANES), f32)]),
        in_specs=[hbm] * (2 * n_arr) + [pl.BlockSpec(memory_space=pl.ANY)] * n_after,
        out_specs=[sem] * (2 * n_arr) + [hbm] * (2 * n_arr) + [pl.BlockSpec(memory_space=pltpu.VMEM)],
        input_output_aliases={i: 2 * n_arr + i for i in range(2 * n_arr)},
        compiler_params=pltpu.CompilerParams(has_side_effects=pltpu.SideEffectType.DATAFLOW_SIDE_EFFECTING),
    )(*[pltpu.with_memory_space_constraint(a, pltpu.HBM) for a in list(srcs) + list(lands)], *after)
    return (outs[:n_arr], outs[n_arr:2 * n_arr], outs[2 * n_arr:3 * n_arr], outs[3 * n_arr:4 * n_arr], outs[-1])


def _split_wait(started, plan, after, name):
    send_sems, recv_sems, srcs, lands, _ = started
    n_arr = len(srcs)
    hbm = pl.BlockSpec(memory_space=pltpu.HBM)
    sem = pl.BlockSpec(memory_space=pltpu.SEMAPHORE)

    def body(*refs):
        src_refs, land_refs = refs[:n_arr], refs[n_arr:2 * n_arr]
        s_sems, r_sems = refs[2 * n_arr:3 * n_arr], refs[3 * n_arr:4 * n_arr]
        me = _position()
        for i in range(n_arr):
            for j, (src, _, peer, arrival) in enumerate(plan(i, src_refs[i], land_refs[i], me)):
                cp = pltpu.make_async_remote_copy(src_ref=src, dst_ref=arrival, send_sem=s_sems[i].at[j], recv_sem=r_sems[i].at[j],
                                                  device_id=peer, device_id_type=MESH)
                cp.wait_send()
                cp.wait_recv()

    outs = pl.pallas_call(
        body, name=name,
        out_shape=[pltpu.HBM(a.shape, a.dtype) for a in list(srcs) + list(lands)],
        in_specs=[hbm] * (2 * n_arr) + [sem] * (2 * n_arr) + [pl.BlockSpec(memory_space=pl.ANY)],
        out_specs=[hbm] * (2 * n_arr),
        input_output_aliases={i: i for i in range(2 * n_arr)},
        compiler_params=pltpu.CompilerParams(has_side_effects=pltpu.SideEffectType.DATAFLOW_SIDE_EFFECTING),
    )(*srcs, *lands, *send_sems, *recv_sems, after)
    return outs[:n_arr], outs[n_arr:]


_GATHER_FLIPS = (1, 2, 4, 6)


def _gather_plan(i, src_ref, land_ref, me):
    m = src_ref.shape[0]

    def rows(pos):
        return land_ref.at[pl.ds(_index(pos) * m, m), :]

    return [(src_ref, rows(me), _flip(me, k), rows(_flip(me, k))) for k in _GATHER_FLIPS]


def _gather_forward(lands, name):
    n_arr = len(lands)
    chips = (2, 4, 6)

    def body(*refs):
        out_refs = refs[n_arr:2 * n_arr]
        send_sems, recv_sems = refs[2 * n_arr:]
        me = _position()
        sibling = _flip(me, 1)
        sends, arrivals = [], []
        for i, out_ref in enumerate(out_refs):
            m = out_ref.shape[0] // N_DEV

            def copy(pos, j):
                blk = out_ref.at[pl.ds(_index(pos) * m, m), :]
                return pltpu.make_async_remote_copy(src_ref=blk, dst_ref=blk, send_sem=send_sems.at[3 * i + j],
                                                    recv_sem=recv_sems.at[3 * i + j], device_id=sibling, device_id_type=MESH)

            for j, k in enumerate(chips):
                sends.append(copy(_flip(me, k), j))
                arrivals.append(copy(_flip(sibling, k), j))
        for cp in sends:
            cp.start()
        for cp in arrivals:
            cp.wait_recv()
        for cp in sends:
            cp.wait_send()

    anyspec = pl.BlockSpec(memory_space=pl.ANY)
    return pl.pallas_call(
        body, out_shape=[jax.ShapeDtypeStruct(a.shape, a.dtype) for a in lands],
        in_specs=[anyspec] * n_arr, out_specs=[anyspec] * n_arr, input_output_aliases={i: i for i in range(n_arr)},
        scratch_shapes=[pltpu.SemaphoreType.DMA((3 * n_arr,))] * 2, name=name,
    )(*lands)


def _chips_plan(i, src_ref, land_ref, me):
    m = src_ref.shape[0] // 4
    plan = []
    for j, k in enumerate((2, 4, 6)):
        peer = _flip(me, k)
        plan.append((src_ref.at[pl.ds((2 * peer[0] + peer[1]) * m, m), :], land_ref.at[j], peer, land_ref.at[j]))
    return plan


def _sibling_plan(i, src_ref, land_ref, me):
    m = src_ref.shape[0] // N_DEV
    sibling = _flip(me, 1)
    return [(src_ref.at[pl.ds((2 * q + 1 - me[2]) * m, m), :], land_ref.at[q], sibling, land_ref.at[q]) for q in range(4)]


def _sum_with_sibling(g, recv, name):
    m = g.shape[0] // N_DEV
    n = g.shape[1]
    tr = _pick(m, (208, 128, 64, 32, 16))
    nt = m // tr

    def body(g_ref, r_ref, o_ref):
        c = lax.axis_index("c")
        own = jnp.where(c == 0, g_ref[0, 0].astype(f32), g_ref[0, 1].astype(f32))
        o_ref[...] = (own + r_ref[0].astype(f32)).astype(o_ref.dtype)

    return pl.pallas_call(
        body, grid=(4, nt),
        in_specs=[pl.BlockSpec((1, 2, tr, n), lambda q, i: (q, 0, i, 0)), pl.BlockSpec((1, tr, n), lambda q, i: (q, i, 0))],
        out_specs=pl.BlockSpec((tr, n), lambda q, i: (q * nt + i, 0)),
        out_shape=jax.ShapeDtypeStruct((4 * m, n), bf16), name=name,
    )(g.reshape(4, 2, m, n), recv)


def _sum_with_chips(h, recv, name):
    m = h.shape[0] // 4
    n = h.shape[1]
    tr = _pick(m, (208, 128, 64, 32, 16))

    def body(h_ref, r_ref, o_ref):
        my_q = 2 * lax.axis_index("x") + lax.axis_index("y")
        own = h_ref[0].astype(f32)
        for q in range(1, 4):
            own = jnp.where(my_q == q, h_ref[q].astype(f32), own)
        o_ref[...] = ((own + r_ref[0].astype(f32)) + r_ref[1].astype(f32)) + r_ref[2].astype(f32)

    return pl.pallas_call(
        body, grid=(m // tr,),
        in_specs=[pl.BlockSpec((4, tr, n), lambda i: (0, i, 0)), pl.BlockSpec((3, tr, n), lambda i: (0, i, 0))],
        out_specs=pl.BlockSpec((tr, n), lambda i: (i, 0)), out_shape=jax.ShapeDtypeStruct((m, n), f32), name=name,
    )(h.reshape(4, m, n), recv)


def _sum_slots(parts, name):
    n_slot, m, n = parts.shape
    tr = _pick(m, (208, 128, 64, 32, 16, 8))

    def body(p_ref, o_ref):
        acc = p_ref[0]
        for s in range(1, n_slot):
            acc = acc + p_ref[s]
        o_ref[...] = acc

    return pl.pallas_call(
        body, grid=(m // tr,), in_specs=[pl.BlockSpec((n_slot, tr, n), lambda i: (0, i, 0))],
        out_specs=pl.BlockSpec((tr, n), lambda i: (i, 0)), out_shape=jax.ShapeDtypeStruct((m, n), parts.dtype), name=name,
    )(parts)


def _reduce_scatter_begin(gs, name):
    lands = [lax.empty((4, g.shape[0] // N_DEV, g.shape[1]), g.dtype) for g in gs]
    return _split_start(gs, lands, _sibling_plan, 4, "rs_d2d_start_" + name)


def _reduce_scatter_middle(started, after, name):
    gs, from_sibling = _split_wait(started, _sibling_plan, after, "rs_d2d_wait_" + name)
    chip_sums = [_sum_with_sibling(g, r, f"rs_sum2_{name}_{i}") for i, (g, r) in enumerate(zip(gs, from_sibling))]
    lands = [lax.empty((3, h.shape[0] // 4, h.shape[1]), h.dtype) for h in chip_sums]
    return _split_start(chip_sums, lands, _chips_plan, 3, "rs_ici_start_" + name)


def _reduce_scatter_end(started, after, name):
    chip_sums, from_chips = _split_wait(started, _chips_plan, after, "rs_ici_wait_" + name)
    return [_sum_with_chips(h, r, f"rs_sum4_{name}_{i}") for i, (h, r) in enumerate(zip(chip_sums, from_chips))]


def _adamw_update(w, g, m, v):
    mm = ADAM_B1 * m + (1.0 - ADAM_B1) * g
    vv = ADAM_B2 * v + (1.0 - ADAM_B2) * jnp.square(g)
    m_hat = mm / (1.0 - ADAM_B1 ** ADAM_STEP)
    v_hat = vv / (1.0 - ADAM_B2 ** ADAM_STEP)
    return -ADAM_LR * (m_hat / (jnp.sqrt(v_hat) + ADAM_EPS) + ADAM_WD * w), mm, vv


def _adamw_many(ws, gs, ms, vs, name):
    k = len(ws)
    shapes = [w.shape for w in ws]
    flat = [[a.reshape(-1, a.shape[-1]) for a in group] for group in (ws, gs, ms, vs)]

    def body(*refs):
        for i in range(k):
            d, mm, vv = _adamw_update(*(refs[j * k + i][...] for j in range(4)))
            refs[4 * k + i][...] = d
            refs[5 * k + i][...] = mm
            refs[6 * k + i][...] = vv

    outs = pl.pallas_call(
        body, out_shape=[jax.ShapeDtypeStruct(a.shape, f32) for a in flat[0]] * 3, name=name,
    )(*flat[0], *flat[1], *flat[2], *flat[3])
    return tuple([outs[j * k + i].reshape(shapes[i]) for i in range(k)] for j in range(3))


def _adamw(w, g, m, v, name):
    shape = w.shape
    n = shape[-1]
    r = w.size // n
    w2, g2, m2, v2 = (a.reshape(r, n) for a in (w, g, m, v))
    tr = _pick(r, (256, 208, 128, 64, 32, 16, 8))

    def body(w_ref, g_ref, m_ref, v_ref, d_ref, mo_ref, vo_ref):
        d_ref[...], mo_ref[...], vo_ref[...] = _adamw_update(w_ref[...], g_ref[...], m_ref[...], v_ref[...])

    spec = pl.BlockSpec((tr, n), lambda i: (i, 0))
    outs = pl.pallas_call(
        body, grid=(r // tr,), in_specs=[spec] * 4, out_specs=[spec] * 3,
        out_shape=[jax.ShapeDtypeStruct((r, n), f32)] * 3, name=name,
    )(w2, g2, m2, v2)
    return tuple(o.reshape(shape) for o in outs)


_SMALL = ("shift_mu", "w_decay0", "a0", "k_k", "k_a", "r_k", "ln_x_w", "ln_x_b", "v_mix0", "lb_logits",
          "g_norm_w", "ln_w", "ln_b")
_NAMES = ("w_in", "shift_mu", "w_decay0", "w_decay_up", "a0", "a_up", "k_k", "k_a", "r_k", "ln_x_w", "ln_x_b",
          "v_mix0", "v_mix_down", "v_mix_up", "lb_logits", "g_norm_w", "w_out", "ln_w", "ln_b")


def _pad_rows(a, rows, at_end):
    z = jnp.zeros((rows - a.shape[0], a.shape[1]), a.dtype)
    return jnp.concatenate([a, z] if at_end else [z, a], axis=0)


def kernel(x, w_in, shift_mu, w_decay0, w_decay_up, a0, a_up, k_k, k_a, r_k, ln_x_w, ln_x_b, v_mix0, v_mix_down, v_mix_up, lb_logits, g_norm_w, w_out, ln_w, ln_b, loss_target, m_w_in, m_shift_mu, m_w_decay0, m_w_decay_up, m_a0, m_a_up, m_k_k, m_k_a, m_r_k, m_ln_x_w, m_ln_x_b, m_v_mix0, m_v_mix_down, m_v_mix_up, m_lb_logits, m_g_norm_w, m_w_out, m_ln_w, m_ln_b, v_w_in, v_shift_mu, v_w_decay0, v_w_decay_up, v_a0, v_a_up, v_k_k, v_k_a, v_r_k, v_ln_x_w, v_ln_x_b, v_v_mix0, v_v_mix_down, v_v_mix_up, v_lb_logits, v_g_norm_w, v_w_out, v_ln_w, v_ln_b):
    weights = dict(w_in=w_in, shift_mu=shift_mu, w_decay0=w_decay0, w_decay_up=w_decay_up, a0=a0, a_up=a_up, k_k=k_k,
                   k_a=k_a, r_k=r_k, ln_x_w=ln_x_w, ln_x_b=ln_x_b, v_mix0=v_mix0, v_mix_down=v_mix_down,
                   v_mix_up=v_mix_up, lb_logits=lb_logits, g_norm_w=g_norm_w, w_out=w_out, ln_w=ln_w, ln_b=ln_b)
    mom1 = dict(w_in=m_w_in, shift_mu=m_shift_mu, w_decay0=m_w_decay0, w_decay_up=m_w_decay_up, a0=m_a0, a_up=m_a_up,
                k_k=m_k_k, k_a=m_k_a, r_k=m_r_k, ln_x_w=m_ln_x_w, ln_x_b=m_ln_x_b, v_mix0=m_v_mix0,
                v_mix_down=m_v_mix_down, v_mix_up=m_v_mix_up, lb_logits=m_lb_logits, g_norm_w=m_g_norm_w,
                w_out=m_w_out, ln_w=m_ln_w, ln_b=m_ln_b)
    mom2 = dict(w_in=v_w_in, shift_mu=v_shift_mu, w_decay0=v_w_decay0, w_decay_up=v_w_decay_up, a0=v_a0, a_up=v_a_up,
                k_k=v_k_k, k_a=v_k_a, r_k=v_r_k, ln_x_w=v_ln_x_w, ln_x_b=v_ln_x_b, v_mix0=v_v_mix0,
                v_mix_down=v_v_mix_down, v_mix_up=v_v_mix_up, lb_logits=v_lb_logits, g_norm_w=v_g_norm_w,
                w_out=v_w_out, ln_w=v_ln_w, ln_b=v_ln_b)
    assert x.shape[0] == 1 and w_in.shape[0] == DEPTH
    t, d = x.shape[1], x.shape[2]
    dr = w_decay0.shape[1]
    dh = g_norm_w.shape[1]
    rank_w, rank_a, rank_v = w_decay_up.shape[1], a_up.shape[1], v_mix_up.shape[1]
    rwc = 4 * dr + rank_w + rank_a
    assert rank_w + rank_a == LANES and rank_v <= LANES and dr + dh == d
    assert t % CHUNK == 0 and dr % LANES == 0 and dh % LANES == 0 and shift_mu.shape[1] == rwc
    n_pair = dr // LANES
    me = _index(_position())

    win_t = [_all_gather_rows(w_in[0].T.astype(bf16), "ag_w_in_0"), None]
    wout = [None, None]
    shard = dr // N_DEV
    pack = jnp.concatenate([w_decay_up[0], w_decay_up[1], a_up[0], a_up[1], v_mix_up[0], v_mix_down[0].T], axis=0)
    pack = _all_gather_rows(pack, "ag_small")
    late_blocks = [w_out[0].astype(bf16), w_in[1].T.astype(bf16), w_out[1].astype(bf16)]
    late_lands = [lax.dynamic_update_slice(lax.empty((N_DEV * blk.shape[0], blk.shape[1]), bf16), blk, (me * blk.shape[0], 0))
                  for blk in late_blocks]
    late_gather = _split_start(late_blocks, late_lands, _gather_plan, len(_GATHER_FLIPS), "ag_late_start",
                               after=(win_t[0], pack))
    pack = jnp.transpose(pack.reshape(N_DEV, -1, shard), (1, 0, 2)).reshape(-1, dr)
    offs = [0, rank_w, 2 * rank_w, 2 * rank_w + rank_a, 2 * rank_w + 2 * rank_a, 2 * rank_w + 2 * rank_a + rank_v,
            2 * rank_w + 2 * rank_a + 2 * rank_v]
    wdu_f = [pack[offs[0]:offs[1]], pack[offs[1]:offs[2]]]
    aup_f = [pack[offs[2]:offs[3]], pack[offs[3]:offs[4]]]
    vup_f = pack[offs[4]:offs[5]]
    vdown_f = pack[offs[5]:offs[6]].T

    def after_start(a, started):
        return a + started[-1][0:1, 0:1]

    def rwkv_params(l):
        mu = after_start(shift_mu[0:1], late_gather) if l == 0 else shift_mu[l:l + 1]
        prm = [mu, w_decay0[l:l + 1], a0[l:l + 1], _pad_rows(wdu_f[l], LANES, True),
               _pad_rows(aup_f[l], LANES, False)]
        if l == 1:
            prm += [v_mix0[0:1], _pad_rows(vdown_f.T, LANES, True).T, _pad_rows(vup_f, LANES, True)]
        rows = jnp.stack([k_k[l], k_a[l], r_k[l], ln_x_w[l], ln_x_b[l]] + [jnp.zeros((dr,), f32)] * 3, axis=0)
        pp = jnp.transpose(rows.reshape(8, n_pair, LANES), (1, 0, 2))
        return tuple(prm), pp

    h = x[0]
    h16 = h.astype(bf16)
    tgt = loss_target[0]
    saved = []
    vfirst = None
    for l in range(DEPTH):
        prm, pp = rwkv_params(l)
        proj = _matmul(h16, win_t[l], "nt", f"mm_proj_{l}", _MM_TILES["proj"])
        if l == 0:
            cat, vfirst, mck = _rwkv_fwd(False, proj, None, prm, pp, d)
        else:
            cat, mck = _rwkv_fwd(True, proj, vfirst, prm, pp, d)
        cat, sck = _hgrn_fwd(l == 1, proj, lb_logits, g_norm_w[l:l + 1], cat, rwc)
        if l == 0:
            _, arrived = _split_wait(late_gather, _gather_plan, cat, "ag_late_wait")
            wout[0], win_t[1], wout[1] = _gather_forward(arrived, "ag_late_forward")
        y = _matmul(cat, wout[l], "nn", f"mm_out_{l}", _MM_TILES["out"])
        saved.append((h, h16, proj, prm, pp, mck, sck, cat, y))
        if l < DEPTH - 1:
            h, h16 = _ln_fwd(h, y, ln_w[l:l + 1], ln_b[l:l + 1])
        else:
            top = _ln_loss_bwd(h, y, ln_w[l:l + 1], ln_b[l:l + 1], tgt)
    loss = lax.psum(top[4][0, 0], ("x", "y", "c"))

    grads = {}
    big = {}
    dvfirst = None
    d_lbl = None
    rs_started = {}
    for l in reversed(range(DEPTH)):
        h_l, h16_l, proj, prm, pp, mck, sck, cat, y = saved[l]
        if l == DEPTH - 1:
            dy, dy16, g_ln_w, g_ln_b = top[:4]
        else:
            dy, dy16, g_ln_w, g_ln_b = _ln_bwd(h_l, y, after_start(ln_w[l:l + 1], rs_started[l + 1]), ln_b[l:l + 1], dh_out)
        dcat = _matmul(dy16, wout[l], "nt", f"mm_dcat_{l}", _MM_TILES["dcat"])
        big[("w_out", l)] = _matmul(cat, dy16, "tn", f"mm_dwout_{l}", _MM_TILES["dwout"], out_dtype=bf16)
        if l == 1:
            outs = _rwkv_bwd(True, proj, vfirst, prm, pp, mck, dcat, None)
            dproj_r, dvfirst = outs[0], outs[1]
            dprm, dpp = outs[2:-1], outs[-1]
        else:
            outs = _rwkv_bwd(False, proj, None, prm, pp, mck, dcat, dvfirst)
            dproj_r = outs[0]
            dprm, dpp = outs[1:-1], outs[-1]
        dproj, dlbl_l, dgnw = _hgrn_bwd(l == 1, proj, lb_logits, g_norm_w[l:l + 1], sck, dcat, rwc, dproj_r)
        big[("w_in", l)] = _matmul(dproj, h16_l, "tn", f"mm_dwin_{l}", _MM_TILES["dwin"], out_dtype=bf16)
        sharded = [dprm[3][:rank_w].T, dprm[4][rank_w:].T]
        if l == 1:
            sharded += [dprm[6][:, :rank_v], dprm[7][:rank_v].T,
                        jnp.zeros((dr, LANES - 2 * rank_v), f32)]
        sharded = jnp.concatenate(sharded, axis=1).astype(bf16)
        d2d = _reduce_scatter_begin([big[("w_in", l)], big[("w_out", l)], sharded], f"l{l}")
        if l == 0:
            rs_started[l] = _reduce_scatter_middle(d2d, sharded, f"l{l}")
            token = rs_started[l][-1]
        else:
            token = d2d[-1]
        dh_out = _matmul(dproj, win_t[l], "nn", f"mm_dh_{l}", _MM_TILES["dh"], add=dy, add_scale=ALPHA, after=token)
        if l > 0:
            rs_started[l] = _reduce_scatter_middle(d2d, dh_out, f"l{l}")
        dpp = jnp.transpose(dpp, (1, 0, 2)).reshape(8, dr)
        grads[l] = dict(shift_mu=dprm[0][0], w_decay0=dprm[1][0], a0=dprm[2][0],
                        k_k=dpp[0], k_a=dpp[1], r_k=dpp[2], ln_x_w=dpp[3], ln_x_b=dpp[4],
                        g_norm_w=dgnw[0], ln_w=g_ln_w[0], ln_b=g_ln_b[0])
        if l == 1:
            grads[l].update(v_mix0=dprm[5][0])
            d_lbl = dlbl_l
    grad_x = dh_out[None]

    def both(name):
        return jnp.stack([grads[0][name], grads[1][name]])

    small = dict(shift_mu=both("shift_mu"), w_decay0=both("w_decay0"), a0=both("a0"), k_k=both("k_k"), k_a=both("k_a"),
                 r_k=both("r_k"), ln_x_w=both("ln_x_w"), ln_x_b=both("ln_x_b"), v_mix0=grads[1]["v_mix0"][None],
                 lb_logits=d_lbl, g_norm_w=both("g_norm_w"), ln_w=both("ln_w"), ln_b=both("ln_b"))
    flat = jnp.concatenate([small[nm].reshape(-1) for nm in _SMALL])
    n_flat = flat.shape[0]
    rows = -(-n_flat // (8 * LANES)) * 8
    flat = jnp.concatenate([flat, jnp.zeros((rows * LANES - n_flat,), f32)]).reshape(rows, LANES)
    total = _sum_slots(_all_gather_rows(flat, "ag_small_grads").reshape(N_DEV, rows, LANES), "sum_small_grads").reshape(-1)
    gsm = {}
    off = 0
    for nm in _SMALL:
        size = small[nm].size
        gsm[nm] = total[off:off + size].reshape(small[nm].shape)
        off += size
    reduced = {1: _reduce_scatter_end(rs_started[1], dh_out, "l1")}
    reduced[0] = _reduce_scatter_end(rs_started[0], total, "l0")
    g_w_in_t = jnp.stack([reduced[l][0] for l in range(DEPTH)])
    gsm["w_in"] = jnp.transpose(g_w_in_t, (0, 2, 1))
    gsm["w_out"] = jnp.stack([reduced[l][1] for l in range(DEPTH)])
    gsm["w_decay_up"] = jnp.stack([reduced[l][2][:, :rank_w].T for l in range(DEPTH)])
    gsm["a_up"] = jnp.stack([reduced[l][2][:, rank_w:rank_w + rank_a].T for l in range(DEPTH)])
    gsm["v_mix_down"] = reduced[1][2][:, LANES:LANES + rank_v][None]
    gsm["v_mix_up"] = reduced[1][2][:, LANES + rank_v:LANES + 2 * rank_v].T[None]

    deltas, new_m, new_v = {}, {}, {}
    swap = lambda a: jnp.transpose(a, (0, 2, 1))
    deltas["w_in"], new_m["w_in"], new_v["w_in"] = (
        swap(a) for a in _adamw(swap(w_in), g_w_in_t, swap(m_w_in), swap(v_w_in), "adamw_w_in"))
    deltas["w_out"], new_m["w_out"], new_v["w_out"] = _adamw(w_out, gsm["w_out"], m_w_out, v_w_out, "adamw_w_out")
    rest = [nm for nm in _NAMES if nm not in ("w_in", "w_out")]
    d_rest, m_rest, v_rest = _adamw_many([weights[nm] for nm in rest], [gsm[nm] for nm in rest],
                                         [mom1[nm] for nm in rest], [mom2[nm] for nm in rest], "adamw_small")
    for i, nm in enumerate(rest):
        deltas[nm], new_m[nm], new_v[nm] = d_rest[i], m_rest[i], v_rest[i]
    return (loss, grad_x, *[gsm[nm] for nm in _NAMES], *[deltas[nm] for nm in _NAMES],
            *[new_m[nm] for nm in _NAMES], *[new_v[nm] for nm in _NAMES])
```

```python
import functools

import jax
import jax.numpy as jnp
from jax import lax
from jax.experimental import pallas as pl
from jax.experimental.pallas import tpu as pltpu

f32 = jnp.float32
bf16 = jnp.bfloat16

N_DEV = 8
CHUNK = 64
LANES = 128
RWKV_HEAD = 64
DEPTH = 2
ALPHA = (2 * DEPTH) ** 0.25
LN_EPS = 1e-5
GN_EPS = 64e-5
RMS_EPS = 1e-5
LB_FLOOR = 1e-30
ADAM_LR, ADAM_B1, ADAM_B2, ADAM_EPS, ADAM_WD, ADAM_STEP = 0.001, 0.9, 0.999, 1e-08, 0.01, 10
MESH = pl.DeviceIdType.MESH


def _iota(shape, d):
    return lax.broadcasted_iota(jnp.int32, shape, d)


_DIMS = {"nn": (((1,), (0,)), ((), ())), "nt": (((1,), (1,)), ((), ())), "tn": (((0,), (0,)), ((), ()))}
_BATCH_DIMS = {"nn": (((2,), (1,)), ((0,), (0,))), "nt": (((2,), (2,)), ((0,), (0,))), "tn": (((1,), (1,)), ((0,), (0,)))}
_K_AXES = {"nn": (-1, -2), "nt": (-1, -1), "tn": (-2, -2)}


def _mxu(a, b, mode):
    return lax.dot_general(a, b, (_BATCH_DIMS if a.ndim == 3 else _DIMS)[mode], preferred_element_type=f32)


def _split(x):
    hi = x.astype(bf16)
    return hi, (x - hi.astype(f32)).astype(bf16)


def _mm2_impl(a, b, mode, passes=3):
    ah, al = _split(a)
    if passes == 3:
        bh, bl = _split(b)
        lhs, rhs = [ah, ah, al], [bh, bl, bh]
    else:
        bh = b.astype(bf16)
        lhs, rhs = [ah, al], [bh, bh]
    ka, kb = _K_AXES[mode]
    k = a.shape[ka]
    if k % (LANES if -1 in (ka, kb) else 16) == 0:
        return _mxu(jnp.concatenate(lhs, axis=ka), jnp.concatenate(rhs, axis=kb), mode)
    out = _mxu(lhs[0], rhs[0], mode)
    for x, y in zip(lhs[1:], rhs[1:]):
        out = out + _mxu(x, y, mode)
    return out


@functools.partial(jax.custom_vjp, nondiff_argnums=(2, 3))
def _mm2(a, b, mode, passes=3):
    return _mm2_impl(a, b, mode, passes)


def _mm2_fwd(a, b, mode, passes):
    return _mm2_impl(a, b, mode, passes), (a, b)


def _mm2_bwd(mode, passes, res, g):
    a, b = res
    if mode == "nn":
        return _mm2_impl(g, b, "nt", passes), _mm2_impl(a, g, "tn", passes)
    if mode == "nt":
        return _mm2_impl(g, b, "nn", passes), _mm2_impl(g, a, "tn", passes)
    return _mm2_impl(b, g, "nt", passes), _mm2_impl(a, g, "nn", passes)


_mm2.defvjp(_mm2_fwd, _mm2_bwd)

TRI_PASSES = 2
APPLY_PASSES = 2


def _const_impl(cm, x, mode):
    hi, lo = _split(x)
    if mode in ("r", "rt"):
        shape = x.shape
        hi, lo = hi.reshape(-1, shape[-1]), lo.reshape(-1, shape[-1])
        dims = "nn" if mode == "r" else "nt"
        out = _mxu(hi, cm, dims) + _mxu(lo, cm, dims)
        return out.reshape(shape[:-1] + (out.shape[-1],))
    if x.ndim == 3:
        cm = jnp.broadcast_to(cm, (x.shape[0],) + cm.shape)
    return _mxu(cm, hi, mode) + _mxu(cm, lo, mode)


@jax.custom_vjp
def _const_left(cm, x):
    return _const_impl(cm, x, "nn")


_const_left.defvjp(lambda cm, x: (_const_impl(cm, x, "nn"), cm),
                   lambda cm, g: (jnp.zeros_like(cm), _const_impl(cm, g, "tn")))


@jax.custom_vjp
def _const_right(x, cm):
    return _const_impl(cm, x, "r")


_const_right.defvjp(lambda x, cm: (_const_impl(cm, x, "r"), cm),
                    lambda cm, g: (_const_impl(cm, g, "rt"), jnp.zeros_like(cm)))


def _tri_inv(a):
    n = a.shape[-1]
    tm = (_iota((n, n), 0) == _iota((n, n), 1)).astype(f32) + a
    ak = a
    for _ in range(5):
        ak = _mm2_impl(ak, ak, "nn", TRI_PASSES)
        tm = tm + _mm2_impl(tm, ak, "nn", TRI_PASSES)
    return tm


@jax.custom_vjp
def _tri_solve(tm, a, x):
    del a
    return _mm2_impl(tm, x, "nn")


def _tri_solve_fwd(tm, a, x):
    u = _mm2_impl(tm, x, "nn")
    return u, (tm, u)


def _tri_solve_bwd(res, du):
    tm, u = res
    dx = _mm2_impl(tm, du, "tn")
    return jnp.zeros_like(tm), _mm2_impl(dx, u, "nt"), dx


_tri_solve.defvjp(_tri_solve_fwd, _tri_solve_bwd)


def _col_of_row(row_vec):
    n = row_vec.shape[-1]
    eye = _iota((n, n), 0) == _iota((n, n), 1)
    return jnp.sum(jnp.where(eye, jnp.broadcast_to(row_vec, row_vec.shape[:-2] + (n, n)), 0.0), axis=-1, keepdims=True)


def _softplus(x):
    return jnp.maximum(x, 0.0) + jnp.log1p(jnp.exp(-jnp.abs(x)))


def _log_sigmoid(x):
    return -_softplus(-x)


def _logaddexp(a, b):
    return jnp.maximum(a, b) + jnp.log1p(jnp.exp(-jnp.abs(a - b)))


def _silu(x):
    return x * jax.nn.sigmoid(x)


def _tril(c, strict):
    r, s = _iota((c, c), 0), _iota((c, c), 1)
    return (r > s) if strict else (r >= s)


def _last_row(a):
    c = a.shape[-2]
    return jnp.sum(jnp.where(_iota(a.shape, a.ndim - 2) == c - 1, a, 0.0), axis=-2, keepdims=True)


def _rwkv_pre(layer1, prm, y, prev, vf):
    c = y.shape[0]
    if layer1:
        mu, w0, a0, wup, aup, v0, vdown, vup = prm
    else:
        mu, w0, a0, wup, aup = prm
    dr = w0.shape[1]
    shift = (_iota((c, c), 0) == _iota((c, c), 1) + 1).astype(bf16)
    y_prev = _const_left(shift, y) + jnp.where(_iota((c, 1), 0) == 0, prev, 0.0)
    rw = y + mu * (y_prev - y)
    r, k, v, z = (rw[:, i * dr:(i + 1) * dr] for i in range(4))
    wdad = rw[:, 4 * dr:4 * dr + LANES]
    w_raw = w0 + _mm2(jnp.tanh(wdad), wup, "nn")
    lw = -jnp.exp(-_softplus(-w_raw) - 0.5)
    asig = jax.nn.sigmoid(a0 + _mm2(wdad, aup, "nn"))
    if layer1:
        v = v + (vf - v) * jax.nn.sigmoid(v0 + _mm2(_mm2(v, vdown, "nn"), vup, "nn"))
    return r, k, v, z, lw, asig


def _rwkv_pair(pp, m0, xs, tm=None):
    kkw, kaw, rkw, gnw, gnb = pp
    r, k, v, z, lw, asig = xs
    c = r.shape[-2]
    n2 = 2 * c
    lane = _iota((1, LANES), 1)
    mh0, mh1 = (lane < RWKV_HEAD).astype(f32), (lane >= RWKV_HEAD).astype(f32)
    same_head = _iota((LANES, LANES), 0) // RWKV_HEAD == _iota((LANES, LANES), 1) // RWKV_HEAD
    g = same_head.astype(bf16)

    def seg(x):
        return _const_right(x, g)

    def stack(x):
        return jnp.concatenate([x * mh0, x * mh1], axis=-2)

    kk = k * kkw
    kk = kk / jnp.maximum(jnp.sqrt(seg(kk * kk)), 1e-12)
    k2 = k * (1.0 + (asig - 1.0) * kaw)
    a = -kk
    b = kk * asig
    cum = _const_left(_tril(c, False).astype(bf16), lw)
    at = stack(a * jnp.exp(cum - lw))
    rt = stack(r * jnp.exp(cum))
    en = jnp.exp(-cum)
    sc = _mm2(jnp.concatenate([at, rt], axis=-2), jnp.concatenate([stack(b * en), stack(k2 * en)], axis=-2), "nt")
    row, col = _iota((n2, n2), 0), _iota((n2, n2), 1)
    same = row // c == col // c
    strict = same & (row % c > col % c)
    incl = same & (row % c >= col % c)
    aab = jnp.where(strict, sc[..., :n2, :n2], 0.0)
    aak = jnp.where(strict, sc[..., :n2, n2:], 0.0)
    arb = jnp.where(incl, sc[..., n2:, :n2], 0.0)
    ark = jnp.where(incl, sc[..., n2:, n2:], 0.0)
    vv = jnp.concatenate([v, v], axis=-2)
    mask_st = jnp.concatenate([jnp.broadcast_to(mh0, (c, LANES)), jnp.broadcast_to(mh1, (c, LANES))], axis=0)
    x_st = _mm2(jnp.concatenate([at, aak], axis=-1), jnp.concatenate([m0, vv], axis=-2), "nn", APPLY_PASSES)
    if tm is None:
        tm = _tri_inv(lax.stop_gradient(aab))
    u_st = _tri_solve(tm, aab, x_st) * mask_st
    o_st = _mm2(jnp.concatenate([rt, arb, ark], axis=-1), jnp.concatenate([m0, u_st, vv], axis=-2), "nn", APPLY_PASSES) * mask_st
    u = u_st[..., :c, :] + u_st[..., c:, :]
    o = o_st[..., :c, :] + o_st[..., c:, :]
    cum_last = _last_row(cum)
    dec_end = jnp.exp(cum_last - cum)
    m_new = _col_of_row(jnp.exp(cum_last)) * m0 + _mm2(
        jnp.concatenate([b * dec_end, k2 * dec_end], axis=-2), jnp.concatenate([u, v], axis=-2), "tn", APPLY_PASSES) * same_head.astype(f32)
    mean = seg(o) * (1.0 / RWKV_HEAD)
    d = o - mean
    var = seg(d * d) * (1.0 / RWKV_HEAD)
    on = d * lax.rsqrt(var + GN_EPS) * gnw + gnb
    bonus = seg(r * k2 * rkw) * v
    return (on + bonus) * _silu(z), m_new, tm


def _split_lanes(a, n):
    return [a[:, i * LANES:(i + 1) * LANES] for i in range(n)]


def _rwkv_step(layer1, prm, y, prev, vf, pp, m0, tm=None):
    xs = _rwkv_pre(layer1, prm, y, prev, vf)
    n_pair = m0.shape[0]
    og, m_new, tm = _rwkv_pair(pp, m0, tuple(jnp.concatenate([p[None] for p in _split_lanes(a, n_pair)], axis=0) for a in xs), tm)
    return og, m_new, xs[2], tm


def _group(n):
    return n


def _rwkv_specs(layer1, t, dr, rwc, n_pair, rev):
    nc = t // CHUNK
    grp = _group(n_pair)

    def cidx(c):
        return (nc - 1 - c) if rev else c

    full = lambda shape: pl.BlockSpec(shape, lambda c, p: tuple(0 for _ in shape))
    specs = [
        pl.BlockSpec((CHUNK, rwc), lambda c, p: (cidx(c), 0)),
        pl.BlockSpec((8, rwc), lambda c, p: (jnp.maximum(cidx(c) * (CHUNK // 8) - 1, 0), 0)),
    ]
    if layer1:
        specs.append(pl.BlockSpec((CHUNK, dr), lambda c, p: (cidx(c), 0)))
    prm_shapes = [(1, rwc), (1, dr), (1, dr), (LANES, dr), (LANES, dr)]
    if layer1:
        prm_shapes += [(1, dr), (dr, LANES), (LANES, dr)]
    specs += [full(s) for s in prm_shapes]
    specs.append(pl.BlockSpec((grp, 8, LANES), lambda c, p: (p, 0, 0)))
    return specs, prm_shapes, cidx, full


def _rwkv_fwd(layer1, proj, vf, prm, pp, cat_width):
    t = proj.shape[0]
    dr = prm[1].shape[1]
    rwc = prm[0].shape[1]
    n_pair = dr // LANES
    nc = t // CHUNK
    n_prm = len(prm)
    specs, _, _, _ = _rwkv_specs(layer1, t, dr, rwc, n_pair, False)

    def body(*refs):
        y_ref, prev_ref = refs[0], refs[1]
        i = 2
        vf_ref = None
        if layer1:
            vf_ref = refs[i]
            i += 1
        prm_refs = refs[i:i + n_prm]
        i += n_prm
        pp_ref = refs[i]
        i += 1
        cat_ref = refs[i]
        i += 1
        vout_ref = None
        if not layer1:
            vout_ref = refs[i]
            i += 1
        mck_ref, m_s = refs[i], refs[i + 1]
        c = pl.program_id(0)

        @pl.when(c == 0)
        def _():
            m_s[...] = jnp.zeros_like(m_s)

        prev = prev_ref[pl.ds(7, 1), :] * (c != 0).astype(f32)
        m0 = m_s[...]
        ppv = tuple(pp_ref[:, pl.ds(q, 1), :] for q in range(5))
        og, m_new, v, tm = _rwkv_step(layer1, tuple(r[...] for r in prm_refs), y_ref[...], prev,
                                      vf_ref[...] if layer1 else None, ppv, m0)
        mck_ref[0, :n_pair] = m0
        mck_ref[0, n_pair:] = tm
        if not layer1:
            vout_ref[...] = v
        for j in range(n_pair):
            cat_ref[:, j * LANES:(j + 1) * LANES] = og[j]
        m_s[...] = m_new

    grp = _group(n_pair)
    assert grp == n_pair
    out_shape = [jax.ShapeDtypeStruct((t, cat_width), f32)]
    out_specs = [pl.BlockSpec((CHUNK, grp * LANES), lambda c, p: (c, p))]
    if not layer1:
        out_shape.append(jax.ShapeDtypeStruct((t, dr), f32))
        out_specs.append(pl.BlockSpec((CHUNK, dr), lambda c, p: (c, 0)))
    out_shape.append(jax.ShapeDtypeStruct((nc, 2 * n_pair, LANES, LANES), f32))
    out_specs.append(pl.BlockSpec((1, 2 * grp, LANES, LANES), lambda c, p: (c, p, 0, 0)))
    args = [proj, proj] + ([vf] if layer1 else []) + list(prm) + [pp]
    return pl.pallas_call(
        body, grid=(nc, 1), in_specs=specs, out_specs=out_specs, out_shape=out_shape,
        scratch_shapes=[pltpu.VMEM((n_pair, LANES, LANES), f32)],
        compiler_params=pltpu.CompilerParams(dimension_semantics=("arbitrary", "arbitrary")),
        name=f"rwkv_fwd_l{int(layer1)}",
    )(*args)


def _rwkv_bwd(layer1, proj, vf, prm, pp, mck, dcat, dvout):
    t = proj.shape[0]
    dr = prm[1].shape[1]
    rwc = prm[0].shape[1]
    n_pair = dr // LANES
    nc = t // CHUNK
    n_prm = len(prm)
    specs, prm_shapes, cidx, full = _rwkv_specs(layer1, t, dr, rwc, n_pair, True)
    grp = _group(n_pair)
    assert grp == n_pair
    specs.append(pl.BlockSpec((1, 2 * grp, LANES, LANES), lambda c, p: (cidx(c), p, 0, 0)))
    specs.append(pl.BlockSpec((CHUNK, grp * LANES), lambda c, p: (cidx(c), p)))
    if not layer1:
        specs.append(pl.BlockSpec((CHUNK, dr), lambda c, p: (cidx(c), 0)))

    def body(*refs):
        y_ref, prev_ref = refs[0], refs[1]
        i = 2
        vf_ref = None
        if layer1:
            vf_ref = refs[i]
            i += 1
        prm_refs = refs[i:i + n_prm]
        i += n_prm
        pp_ref, mck_ref, dog_ref = refs[i], refs[i + 1], refs[i + 2]
        i += 3
        dvout_ref = None
        if not layer1:
            dvout_ref = refs[i]
            i += 1
        dy_ref = refs[i]
        i += 1
        dvf_ref = None
        if layer1:
            dvf_ref = refs[i]
            i += 1
        dprm_refs = refs[i:i + n_prm]
        i += n_prm
        dpp_ref = refs[i]
        dm_s, dprev_s = refs[i + 1:i + 3]
        c = pl.program_id(0)
        cr = nc - 1 - c

        @pl.when(c == 0)
        def _():
            dm_s[...] = jnp.zeros_like(dm_s)
            dprev_s[...] = jnp.zeros_like(dprev_s)
            dpp_ref[...] = jnp.zeros_like(dpp_ref)
            for r in dprm_refs:
                r[...] = jnp.zeros_like(r)

        prev = prev_ref[pl.ds(7, 1), :] * (cr != 0).astype(f32)
        prm_v = tuple(r[...] for r in prm_refs)
        ppv = tuple(pp_ref[:, pl.ds(q, 1), :] for q in range(5))
        dog = jnp.stack([dog_ref[:, j * LANES:(j + 1) * LANES] for j in range(n_pair)], axis=0)
        m0, tm = mck_ref[0, :n_pair], mck_ref[0, n_pair:]
        no_tm = jnp.zeros_like(tm)
        if layer1:
            _, vjp = jax.vjp(lambda a, b, d, e, g, h: _rwkv_step(True, a, b, d, e, g, h, tm),
                             prm_v, y_ref[...], prev, vf_ref[...], ppv, m0)
            dprm, dy, dprev, dvf, dppv, dm0 = vjp((dog, dm_s[...], jnp.zeros((CHUNK, dr), f32), no_tm))
            dvf_ref[...] = dvf
        else:
            _, vjp = jax.vjp(lambda a, b, d, e, g: _rwkv_step(False, a, b, d, None, e, g, tm), prm_v, y_ref[...], prev, ppv, m0)
            dprm, dy, dprev, dppv, dm0 = vjp((dog, dm_s[...], dvout_ref[...], no_tm))
        dm_s[...] = dm0
        for q in range(5):
            dpp_ref[:, pl.ds(q, 1), :] += dppv[q]
        dy_ref[...] = (dy + jnp.where(_iota((CHUNK, 1), 0) == CHUNK - 1, dprev_s[...], 0.0)).astype(bf16)
        dprev_s[...] = dprev
        for r, gval in zip(dprm_refs, dprm):
            r[...] += gval

    out_shape = [jax.ShapeDtypeStruct((t, proj.shape[1]), bf16)]
    out_specs = [pl.BlockSpec((CHUNK, rwc), lambda c, p: (cidx(c), 0))]
    if layer1:
        out_shape.append(jax.ShapeDtypeStruct((t, dr), f32))
        out_specs.append(pl.BlockSpec((CHUNK, dr), lambda c, p: (cidx(c), 0)))
    out_shape += [jax.ShapeDtypeStruct(s, f32) for s in prm_shapes]
    out_specs += [full(s) for s in prm_shapes]
    out_shape.append(jax.ShapeDtypeStruct((n_pair, 8, LANES), f32))
    out_specs.append(full((n_pair, 8, LANES)))
    args = [proj, proj] + ([vf] if layer1 else []) + list(prm) + [pp, mck, dcat] + ([] if layer1 else [dvout])
    return pl.pallas_call(
        body, grid=(nc, 1), in_specs=specs, out_specs=out_specs, out_shape=out_shape,
        scratch_shapes=[pltpu.VMEM((n_pair, LANES, LANES), f32), pltpu.VMEM((1, rwc), f32)],
        compiler_params=pltpu.CompilerParams(dimension_semantics=("arbitrary", "arbitrary")),
        name=f"rwkv_bwd_l{int(layer1)}",
    )(*args)


def _hgrn_chunk(layer1, lbl, gnw, s0, q_raw, f_raw, i_in, z):
    c = q_raw.shape[-2]
    q = _silu(q_raw)
    ls = _log_sigmoid(f_raw)
    if layer1:
        l0, l1 = lbl[..., 0:1, :], lbl[..., 1:2, :]
        mx = jnp.maximum(l0, l1)
        e0, e1 = jnp.exp(l0 - mx), jnp.exp(l1 - mx)
        sm0, sm1 = e0 / (e0 + e1), e1 / (e0 + e1)
        lb = (sm0 + sm1) - sm0
        log_f = _logaddexp(jnp.log(jnp.maximum(lb, LB_FLOOR)), jnp.log1p(-lb) + ls)
        k = (1.0 - lb) * jax.nn.sigmoid(-f_raw)
    else:
        log_f = _logaddexp(jnp.full_like(ls, jnp.log(jnp.float32(LB_FLOOR))), ls)
        k = jax.nn.sigmoid(-f_raw)
    row, col = _iota((c, c), 0), _iota((c, c), 1)
    trow = _iota((c, 1), 0)
    halves = []
    half = c // 2
    while half >= 1:
        halves.append(half)
        half //= 2
    cmat = jnp.concatenate([(col <= row).astype(f32)]
                           + [(col <= (row // (2 * hf)) * (2 * hf) + hf - 1).astype(f32) for hf in halves], axis=0)
    ball = _const_left(cmat.astype(bf16), log_f)
    b = ball[..., :c, :]
    att = None
    for lvl, hf in enumerate(halves):
        blk = 2 * hf
        bref = ball[..., (lvl + 1) * c:(lvl + 2) * c, :]
        upper = (trow % blk) >= hf
        dec = jnp.exp(jnp.where(upper, b - bref, bref - b))
        qh = jnp.where(upper, q * dec, 0.0)
        kh = jnp.where(upper, 0.0, k * dec)
        term = jnp.where(row // blk == col // blk, _mm2(qh, kh, "nt", APPLY_PASSES), 0.0)
        att = term if att is None else att + term
    lhs = jnp.concatenate([q * jnp.exp(b), att, jnp.zeros(att.shape[:-1] + (LANES - c,), f32)], axis=-1)
    rhs = jnp.concatenate([s0, i_in, jnp.zeros(i_in.shape[:-2] + (LANES - c, i_in.shape[-1]), f32)], axis=-2)
    o = _mm2(lhs, rhs, "nn", APPLY_PASSES) + jnp.sum(q * k, axis=-1, keepdims=True) * i_in
    b_last = _last_row(b)
    s_new = _col_of_row(jnp.exp(b_last)) * s0 + _mm2(k * jnp.exp(b_last - b), i_in, "tn", APPLY_PASSES)
    o = o * lax.rsqrt(jnp.mean(o * o, axis=-1, keepdims=True) + RMS_EPS)
    return o * gnw * _silu(z), s_new


def _hgrn_in_specs(t, dh, col0, rev):
    nc = t // CHUNK
    nh = dh // LANES

    def cidx(c):
        return (nc - 1 - c) if rev else c

    grp = _group(nh)
    specs = [pl.BlockSpec((CHUNK, LANES), functools.partial(lambda g, j, h, c: (cidx(c), col0 + g * nh + h * grp + j), g, j))
             for j in range(grp) for g in range(4)]
    specs.append(pl.BlockSpec((2, grp * LANES), lambda h, c: (0, h)))
    specs.append(pl.BlockSpec((1, grp * LANES), lambda h, c: (0, h)))
    return specs, cidx, grp


def _hgrn_fwd(layer1, proj, lbl, gnw, cat, rwc):
    t, d = cat.shape
    dh = gnw.shape[1]
    nh = dh // LANES
    nc = t // CHUNK
    col0 = rwc // LANES
    specs, _, grp = _hgrn_in_specs(t, dh, col0, False)
    specs.append(pl.BlockSpec(memory_space=pl.ANY))
    assert (d - dh) % (grp * LANES) == 0
    cat_col0 = (d - dh) // (grp * LANES)

    def body(*refs):
        x_refs = refs[:4 * grp]
        lbl_ref, gnw_ref, _, cat_ref, sck_ref, s_s = refs[4 * grp:]
        c = pl.program_id(1)

        @pl.when(c == 0)
        def _():
            s_s[...] = jnp.zeros_like(s_s)

        lanes = [slice(j * LANES, (j + 1) * LANES) for j in range(grp)]
        s0 = s_s[...]
        sck_ref[:, 0] = s0
        out, s_new = _hgrn_chunk(layer1, jnp.stack([lbl_ref[:, ln] for ln in lanes]), jnp.stack([gnw_ref[:, ln] for ln in lanes]),
                                 s0, *(jnp.stack([x_refs[4 * j + g][...] for j in range(grp)]) for g in range(4)))
        for j in range(grp):
            cat_ref[:, lanes[j]] = out[j]
        s_s[...] = s_new

    return pl.pallas_call(
        body, grid=(nh // grp, nc), in_specs=specs,
        out_specs=[pl.BlockSpec((CHUNK, grp * LANES), lambda h, c: (c, cat_col0 + h)),
                   pl.BlockSpec((grp, 1, LANES, LANES), lambda h, c: (h, c, 0, 0))],
        out_shape=[jax.ShapeDtypeStruct((t, d), f32), jax.ShapeDtypeStruct((nh, nc, LANES, LANES), f32)],
        scratch_shapes=[pltpu.VMEM((grp, LANES, LANES), f32)],
        input_output_aliases={4 * grp + 2: 0},
        compiler_params=pltpu.CompilerParams(dimension_semantics=("arbitrary", "arbitrary")),
        name=f"hgrn_fwd_l{int(layer1)}",
    )(*([proj] * (4 * grp)), lbl, gnw, cat)


def _hgrn_bwd(layer1, proj, lbl, gnw, sck, dcat, rwc, dproj):
    t, d = dcat.shape
    dh = gnw.shape[1]
    nh = dh // LANES
    nc = t // CHUNK
    col0 = rwc // LANES
    specs, cidx, grp = _hgrn_in_specs(t, dh, col0, True)
    assert grp == nh and (d - dh) % (grp * LANES) == 0
    cat_col0 = (d - dh) // (grp * LANES)
    specs.append(pl.BlockSpec((grp, 1, LANES, LANES), lambda h, c: (h, cidx(c), 0, 0)))
    specs.append(pl.BlockSpec((CHUNK, grp * LANES), lambda h, c: (cidx(c), cat_col0 + h)))
    specs.append(pl.BlockSpec(memory_space=pl.ANY))

    def body(*refs):
        x_refs = refs[:4 * grp]
        lbl_ref, gnw_ref, sck_ref, do_ref, _, dp_hbm, dlbl_ref, dgnw_ref, ds_s, stage, sems = refs[4 * grp:]
        c = pl.program_id(1)
        slot = c % 2

        def put(s, g, chunk):
            return pltpu.make_async_copy(stage.at[s, g], dp_hbm.at[pl.ds(chunk * CHUNK, CHUNK), pl.ds(rwc + g * dh, dh)],
                                         sems.at[s, g])

        @pl.when(c == 0)
        def _():
            ds_s[...] = jnp.zeros_like(ds_s)
            dlbl_ref[...] = jnp.zeros_like(dlbl_ref)
            dgnw_ref[...] = jnp.zeros_like(dgnw_ref)

        @pl.when(c >= 2)
        def _():
            for g in range(4):
                put(slot, g, 0).wait()

        lanes = [slice(j * LANES, (j + 1) * LANES) for j in range(grp)]
        _, vjp = jax.vjp(functools.partial(_hgrn_chunk, layer1),
                         jnp.stack([lbl_ref[:, ln] for ln in lanes]), jnp.stack([gnw_ref[:, ln] for ln in lanes]), sck_ref[:, 0],
                         *(jnp.stack([x_refs[4 * j + g][...] for j in range(grp)]) for g in range(4)))
        dlbl, dgnw, ds0, dq, df, di, dz = vjp((jnp.stack([do_ref[:, ln] for ln in lanes]), ds_s[...]))
        ds_s[...] = ds0
        for j in range(grp):
            dlbl_ref[:, lanes[j]] += dlbl[j]
            dgnw_ref[:, lanes[j]] += dgnw[j]
            for g, val in enumerate((dq, df, di, dz)):
                stage[slot, g, :, lanes[j]] = val[j].astype(bf16)
        for g in range(4):
            put(slot, g, nc - 1 - c).start()

        @pl.when(c == nc - 1)
        def _():
            for g in range(4):
                put(slot, g, 0).wait()
                if nc >= 2:
                    put(1 - slot, g, 0).wait()

    return pl.pallas_call(
        body, grid=(1, nc), in_specs=specs,
        out_specs=[pl.BlockSpec(memory_space=pl.ANY),
                   pl.BlockSpec((2, grp * LANES), lambda h, c: (0, h)),
                   pl.BlockSpec((1, grp * LANES), lambda h, c: (0, h))],
        out_shape=[jax.ShapeDtypeStruct(dproj.shape, dproj.dtype), jax.ShapeDtypeStruct((2, dh), f32),
                   jax.ShapeDtypeStruct((1, dh), f32)],
        scratch_shapes=[pltpu.VMEM((grp, LANES, LANES), f32), pltpu.VMEM((2, 4, CHUNK, dh), bf16),
                        pltpu.SemaphoreType.DMA((2, 4))],
        input_output_aliases={4 * grp + 4: 0},
        compiler_params=pltpu.CompilerParams(dimension_semantics=("arbitrary", "arbitrary")),
        name=f"hgrn_bwd_l{int(layer1)}",
    )(*([proj] * (4 * grp)), lbl, gnw, sck, dcat, dproj)


def _ln(h, y, w, b):
    u = ALPHA * h + y
    mu = jnp.mean(u, axis=-1, keepdims=True)
    var = jnp.mean(jnp.square(u - mu), axis=-1, keepdims=True)
    return (u - mu) * lax.rsqrt(var + LN_EPS) * w + b


def _row_tile(t):
    return 256 if t % 256 == 0 else t


def _ln_fwd(h, y, w, b):
    t, d = h.shape
    tr = _row_tile(t)

    def body(h_ref, y_ref, w_ref, b_ref, o_ref, o16_ref):
        out = _ln(h_ref[...], y_ref[...], w_ref[...], b_ref[...])
        o_ref[...] = out
        o16_ref[...] = out.astype(bf16)

    row = pl.BlockSpec((tr, d), lambda i: (i, 0))
    vec = pl.BlockSpec((1, d), lambda i: (0, 0))
    return pl.pallas_call(body, grid=(t // tr,), in_specs=[row, row, vec, vec], out_specs=[row, row],
                          out_shape=[jax.ShapeDtypeStruct((t, d), f32), jax.ShapeDtypeStruct((t, d), bf16)],
                          name="ln_fwd")(h, y, w, b)


def _ln_loss_bwd(h, y, w, b, tgt):
    t, d = h.shape
    tr = _row_tile(t)

    def body(h_ref, y_ref, w_ref, b_ref, t_ref, dy_ref, dy16_ref, dw_ref, db_ref, loss_ref):
        @pl.when(pl.program_id(0) == 0)
        def _():
            dw_ref[...] = jnp.zeros_like(dw_ref)
            db_ref[...] = jnp.zeros_like(db_ref)
            loss_ref[...] = jnp.zeros_like(loss_ref)

        out, vjp = jax.vjp(lambda yy, ww, bb: _ln(h_ref[...], yy, ww, bb), y_ref[...], w_ref[...], b_ref[...])
        err = out - t_ref[...]
        loss_ref[...] += 0.5 * jnp.sum(jnp.mean(jnp.square(err), axis=-1, keepdims=True), axis=0, keepdims=True)
        dy, dw, db = vjp(err * (1.0 / d))
        dy_ref[...] = dy
        dy16_ref[...] = dy.astype(bf16)
        dw_ref[...] += dw
        db_ref[...] += db

    row = pl.BlockSpec((tr, d), lambda i: (i, 0))
    vec = pl.BlockSpec((1, d), lambda i: (0, 0))
    return pl.pallas_call(
        body, grid=(t // tr,), in_specs=[row, row, vec, vec, row],
        out_specs=[row, row, vec, vec, pl.BlockSpec((1, LANES), lambda i: (0, 0))],
        out_shape=[jax.ShapeDtypeStruct((t, d), f32), jax.ShapeDtypeStruct((t, d), bf16), jax.ShapeDtypeStruct((1, d), f32),
                   jax.ShapeDtypeStruct((1, d), f32), jax.ShapeDtypeStruct((1, LANES), f32)],
        compiler_params=pltpu.CompilerParams(dimension_semantics=("arbitrary",)), name="ln_loss_bwd")(h, y, w, b, tgt)


def _ln_bwd(h, y, w, b, dout):
    t, d = h.shape
    tr = _row_tile(t)

    def body(h_ref, y_ref, w_ref, b_ref, do_ref, dy_ref, dy16_ref, dw_ref, db_ref):
        @pl.when(pl.program_id(0) == 0)
        def _():
            dw_ref[...] = jnp.zeros_like(dw_ref)
            db_ref[...] = jnp.zeros_like(db_ref)

        _, vjp = jax.vjp(lambda yy, ww, bb: _ln(h_ref[...], yy, ww, bb), y_ref[...], w_ref[...], b_ref[...])
        dy, dw, db = vjp(do_ref[...])
        dy_ref[...] = dy
        dy16_ref[...] = dy.astype(bf16)
        dw_ref[...] += dw
        db_ref[...] += db

    row = pl.BlockSpec((tr, d), lambda i: (i, 0))
    vec = pl.BlockSpec((1, d), lambda i: (0, 0))
    return pl.pallas_call(
        body, grid=(t // tr,), in_specs=[row, row, vec, vec, row], out_specs=[row, row, vec, vec],
        out_shape=[jax.ShapeDtypeStruct((t, d), f32), jax.ShapeDtypeStruct((t, d), bf16),
                   jax.ShapeDtypeStruct((1, d), f32), jax.ShapeDtypeStruct((1, d), f32)],
        compiler_params=pltpu.CompilerParams(dimension_semantics=("arbitrary",)), name="ln_bwd")(h, y, w, b, dout)


def _pick(n, prefs):
    for p in prefs:
        if n % p == 0:
            return p
    return n


def _tile(n, want):
    if n <= want:
        return n
    for cand in range(want - want % LANES, 0, -LANES):
        if n % cand == 0:
            return cand
    return n


_MM_TILES = {"proj": (1024, 1664, 2048), "out": (1024, 1024, 2048), "dcat": (1024, 1024, 2048),
             "dwout": (512, 2048, 2048), "dwin": (640, 2048, 2048), "dh": (1024, 1024, 1664)}


def _matmul(a, b, mode, name, tiles, add=None, add_scale=1.0, out_dtype=f32, after=None):
    if mode == "nn":
        (m, k), n = a.shape, b.shape[1]
    elif mode == "nt":
        (m, k), n = a.shape, b.shape[0]
    else:
        (k, m), n = a.shape, b.shape[1]
    tm, tn, tk = _tile(m, tiles[0]), _tile(n, tiles[1]), _tile(k, tiles[2])
    nk = k // tk
    cache_a = nk == 1 and a.dtype != bf16 and n // tn > 1

    def body(*refs):
        a_ref, b_ref = refs[0], refs[1]
        add_ref = refs[2] if add is not None else None
        n_in = 2 + (add is not None) + (after is not None)
        o_ref = refs[n_in]
        scratch = refs[n_in + 1:]

        def finish(res):
            if add is not None:
                res = res + add_scale * add_ref[...]
            o_ref[...] = res.astype(out_dtype)

        if cache_a:
            a_bf = scratch[0]

            @pl.when(pl.program_id(1) == 0)
            def _():
                a_bf[...] = a_ref[...].astype(bf16)

            a_val = a_bf[...]
        else:
            a_val = a_ref[...].astype(bf16)
        prod = lax.dot_general(a_val, b_ref[...].astype(bf16), _DIMS[mode], preferred_element_type=f32)
        if nk == 1:
            finish(prod)
        else:
            acc = scratch[-1]
            kk = pl.program_id(2)

            @pl.when(kk == 0)
            def _():
                acc[...] = prod

            @pl.when(kk != 0)
            def _():
                acc[...] += prod

            @pl.when(kk == nk - 1)
            def _():
                finish(acc[...])

    a_shape = (tk, tm) if mode == "tn" else (tm, tk)
    a_spec = pl.BlockSpec(a_shape, (lambda i, j, kk: (kk, i)) if mode == "tn" else (lambda i, j, kk: (i, kk)))
    b_spec = pl.BlockSpec((tn, tk), lambda i, j, kk: (j, kk)) if mode == "nt" else pl.BlockSpec((tk, tn), lambda i, j, kk: (kk, j))
    o_spec = pl.BlockSpec((tm, tn), lambda i, j, kk: (i, j))
    in_specs = [a_spec, b_spec] + ([o_spec] if add is not None else []) + ([pl.BlockSpec(memory_space=pl.ANY)] if after is not None else [])
    args = [a, b] + ([add] if add is not None else []) + ([after] if after is not None else [])
    scratch_shapes = ([pltpu.VMEM(a_shape, bf16)] if cache_a else []) + ([pltpu.VMEM((tm, tn), f32)] if nk > 1 else [])
    return pl.pallas_call(
        body, grid=(m // tm, n // tn, nk), in_specs=in_specs, out_specs=o_spec,
        out_shape=jax.ShapeDtypeStruct((m, n), out_dtype), scratch_shapes=scratch_shapes,
        compiler_params=pltpu.CompilerParams(dimension_semantics=("parallel", "arbitrary", "arbitrary")),
        name=name,
    )(*args)


def _position():
    return lax.axis_index("x"), lax.axis_index("y"), lax.axis_index("c")


def _flip(pos, k):
    x, y, c = pos
    return (1 - x if k & 4 else x, 1 - y if k & 2 else y, 1 - c if k & 1 else c)


def _index(pos):
    return 4 * pos[0] + 2 * pos[1] + pos[2]


def _all_gather_rows(xs, name):
    n_arr = len(xs)
    chips = (2, 4, 6)

    def body(*refs):
        x_refs, out_refs = refs[:n_arr], refs[n_arr:2 * n_arr]
        send_sems, recv_sems, local_sems = refs[2 * n_arr:]
        me = _position()
        sibling = _flip(me, 1)

        def copy(i, sem, block, to, own=False):
            m_per = x_refs[i].shape[0]
            rows = out_refs[i].at[pl.ds(_index(block) * m_per, m_per), :]
            return pltpu.make_async_remote_copy(
                src_ref=x_refs[i] if own else rows, dst_ref=rows,
                send_sem=send_sems.at[7 * i + sem], recv_sem=recv_sems.at[7 * i + sem], device_id=to, device_id_type=MESH)

        mine = [pltpu.make_async_copy(x_refs[i], out_refs[i].at[pl.ds(_index(me) * x_refs[i].shape[0], x_refs[i].shape[0]), :],
                                      local_sems.at[i]) for i in range(n_arr)]
        first, passed = [], []
        for i in range(n_arr):
            first.append(copy(i, 0, me, sibling, own=True))
            first += [copy(i, 1 + j, me, _flip(me, k), own=True) for j, k in enumerate(chips)]
            passed.append([copy(i, 4 + j, _flip(me, k), sibling) for j, k in enumerate(chips)])
        for cp in mine + first:
            cp.start()
        for i in range(n_arr):
            for j, k in enumerate(chips):
                copy(i, 1 + j, _flip(me, k), me).wait_recv()
                passed[i][j].start()
        for i in range(n_arr):
            copy(i, 0, sibling, me).wait_recv()
            for j, k in enumerate(chips):
                copy(i, 4 + j, _flip(sibling, k), me).wait_recv()
        for cp in first + [cp for group in passed for cp in group]:
            cp.wait_send()
        for cp in mine:
            cp.wait()

    anyspec = pl.BlockSpec(memory_space=pl.ANY)
    return pl.pallas_call(
        body, out_shape=[jax.ShapeDtypeStruct((N_DEV * x.shape[0], x.shape[1]), x.dtype) for x in xs],
        in_specs=[anyspec] * n_arr, out_specs=[anyspec] * n_arr,
        scratch_shapes=[pltpu.SemaphoreType.DMA((7 * n_arr,)), pltpu.SemaphoreType.DMA((7 * n_arr,)),
                        pltpu.SemaphoreType.DMA((n_arr,))],
        name=name,
    )(*xs)


def _split_start(srcs, lands, plan, n_copies, name, after=()):
    n_arr = len(srcs)
    n_after = len(after)
    hbm = pl.BlockSpec(memory_space=pltpu.HBM)
    sem = pl.BlockSpec(memory_space=pltpu.SEMAPHORE)

    def body(*refs):
        src_refs, land_refs = refs[:n_arr], refs[n_arr:2 * n_arr]
        outs_at = 2 * n_arr + n_after
        send_sems, recv_sems = refs[outs_at:outs_at + n_arr], refs[outs_at + n_arr:outs_at + 2 * n_arr]
        token = refs[-1]
        me = _position()
        for i in range(n_arr):
            for j, (src, dst, peer, _) in enumerate(plan(i, src_refs[i], land_refs[i], me)):
                pltpu.make_async_remote_copy(src_ref=src, dst_ref=dst, send_sem=send_sems[i].at[j], recv_sem=recv_sems[i].at[j],
                                             device_id=peer, device_id_type=MESH).start()
        token[...] = jnp.zeros_like(token)

    outs = pl.pallas_call(
        body, name=name,
        out_shape=([pltpu.SemaphoreType.DMA((n_copies,))] * (2 * n_arr)
                   + [pltpu.HBM(a.shape, a.dtype) for a in list(srcs) + list(lands)]
                   + [jax.ShapeDtypeStruct((8, LANES), f32)]),
        in_specs=[hbm] * (2 * n_arr) + [pl.BlockSpec(memory_space=pl.ANY)] * n_after,
        out_specs=[sem] * (2 * n_arr) + [hbm] * (2 * n_arr) + [pl.BlockSpec(memory_space=pltpu.VMEM)],
        input_output_aliases={i: 2 * n_arr + i for i in range(2 * n_arr)},
        compiler_params=pltpu.CompilerParams(has_side_effects=pltpu.SideEffectType.DATAFLOW_SIDE_EFFECTING),
    )(*[pltpu.with_memory_space_constraint(a, pltpu.HBM) for a in list(srcs) + list(lands)], *after)
    return (outs[:n_arr], outs[n_arr:2 * n_arr], outs[2 * n_arr:3 * n_arr], outs[3 * n_arr:4 * n_arr], outs[-1])


def _split_wait(started, plan, after, name):
    send_sems, recv_sems, srcs, lands, _ = started
    n_arr = len(srcs)
    hbm = pl.BlockSpec(memory_space=pltpu.HBM)
    sem = pl.BlockSpec(memory_space=pltpu.SEMAPHORE)

    def body(*refs):
        src_refs, land_refs = refs[:n_arr], refs[n_arr:2 * n_arr]
        s_sems, r_sems = refs[2 * n_arr:3 * n_arr], refs[3 * n_arr:4 * n_arr]
        me = _position()
        for i in range(n_arr):
            for j, (src, _, peer, arrival) in enumerate(plan(i, src_refs[i], land_refs[i], me)):
                cp = pltpu.make_async_remote_copy(src_ref=src, dst_ref=arrival, send_sem=s_sems[i].at[j], recv_sem=r_sems[i].at[j],
                                                  device_id=peer, device_id_type=MESH)
                cp.wait_send()
                cp.wait_recv()

    outs = pl.pallas_call(
        body, name=name,
        out_shape=[pltpu.HBM(a.shape, a.dtype) for a in list(srcs) + list(lands)],
        in_specs=[hbm] * (2 * n_arr) + [sem] * (2 * n_arr) + [pl.BlockSpec(memory_space=pl.ANY)],
        out_specs=[hbm] * (2 * n_arr),
        input_output_aliases={i: i for i in range(2 * n_arr)},
        compiler_params=pltpu.CompilerParams(has_side_effects=pltpu.SideEffectType.DATAFLOW_SIDE_EFFECTING),
    )(*srcs, *lands, *send_sems, *recv_sems, after)
    return outs[:n_arr], outs[n_arr:]


_GATHER_FLIPS = (1, 2, 4, 6)


def _gather_plan(i, src_ref, land_ref, me):
    m = src_ref.shape[0]

    def rows(pos):
        return land_ref.at[pl.ds(_index(pos) * m, m), :]

    return [(src_ref, rows(me), _flip(me, k), rows(_flip(me, k))) for k in _GATHER_FLIPS]


def _gather_forward(lands, name):
    n_arr = len(lands)
    chips = (2, 4, 6)

    def body(*refs):
        out_refs = refs[n_arr:2 * n_arr]
        send_sems, recv_sems = refs[2 * n_arr:]
        me = _position()
        sibling = _flip(me, 1)
        sends, arrivals = [], []
        for i, out_ref in enumerate(out_refs):
            m = out_ref.shape[0] // N_DEV

            def copy(pos, j):
                blk = out_ref.at[pl.ds(_index(pos) * m, m), :]
                return pltpu.make_async_remote_copy(src_ref=blk, dst_ref=blk, send_sem=send_sems.at[3 * i + j],
                                                    recv_sem=recv_sems.at[3 * i + j], device_id=sibling, device_id_type=MESH)

            for j, k in enumerate(chips):
                sends.append(copy(_flip(me, k), j))
                arrivals.append(copy(_flip(sibling, k), j))
        for cp in sends:
            cp.start()
        for cp in arrivals:
            cp.wait_recv()
        for cp in sends:
            cp.wait_send()

    anyspec = pl.BlockSpec(memory_space=pl.ANY)
    return pl.pallas_call(
        body, out_shape=[jax.ShapeDtypeStruct(a.shape, a.dtype) for a in lands],
        in_specs=[anyspec] * n_arr, out_specs=[anyspec] * n_arr, input_output_aliases={i: i for i in range(n_arr)},
        scratch_shapes=[pltpu.SemaphoreType.DMA((3 * n_arr,))] * 2, name=name,
    )(*lands)


def _chips_plan(i, src_ref, land_ref, me):
    m = src_ref.shape[0] // 4
    plan = []
    for j, k in enumerate((2, 4, 6)):
        peer = _flip(me, k)
        plan.append((src_ref.at[pl.ds((2 * peer[0] + peer[1]) * m, m), :], land_ref.at[j], peer, land_ref.at[j]))
    return plan


def _sibling_plan(i, src_ref, land_ref, me):
    m = src_ref.shape[0] // N_DEV
    sibling = _flip(me, 1)
    return [(src_ref.at[pl.ds((2 * q + 1 - me[2]) * m, m), :], land_ref.at[q], sibling, land_ref.at[q]) for q in range(4)]


def _sum_with_sibling(g, recv, name):
    m = g.shape[0] // N_DEV
    n = g.shape[1]
    tr = _pick(m, (208, 128, 64, 32, 16))
    nt = m // tr

    def body(g_ref, r_ref, o_ref):
        c = lax.axis_index("c")
        own = jnp.where(c == 0, g_ref[0, 0].astype(f32), g_ref[0, 1].astype(f32))
        o_ref[...] = (own + r_ref[0].astype(f32)).astype(o_ref.dtype)

    return pl.pallas_call(
        body, grid=(4, nt),
        in_specs=[pl.BlockSpec((1, 2, tr, n), lambda q, i: (q, 0, i, 0)), pl.BlockSpec((1, tr, n), lambda q, i: (q, i, 0))],
        out_specs=pl.BlockSpec((tr, n), lambda q, i: (q * nt + i, 0)),
        out_shape=jax.ShapeDtypeStruct((4 * m, n), bf16), name=name,
    )(g.reshape(4, 2, m, n), recv)


def _sum_with_chips(h, recv, name):
    m = h.shape[0] // 4
    n = h.shape[1]
    tr = _pick(m, (208, 128, 64, 32, 16))

    def body(h_ref, r_ref, o_ref):
        my_q = 2 * lax.axis_index("x") + lax.axis_index("y")
        own = h_ref[0].astype(f32)
        for q in range(1, 4):
            own = jnp.where(my_q == q, h_ref[q].astype(f32), own)
        o_ref[...] = ((own + r_ref[0].astype(f32)) + r_ref[1].astype(f32)) + r_ref[2].astype(f32)

    return pl.pallas_call(
        body, grid=(m // tr,),
        in_specs=[pl.BlockSpec((4, tr, n), lambda i: (0, i, 0)), pl.BlockSpec((3, tr, n), lambda i: (0, i, 0))],
        out_specs=pl.BlockSpec((tr, n), lambda i: (i, 0)), out_shape=jax.ShapeDtypeStruct((m, n), f32), name=name,
    )(h.reshape(4, m, n), recv)


def _sum_slots(parts, name):
    n_slot, m, n = parts.shape
    tr = _pick(m, (208, 128, 64, 32, 16, 8))

    def body(p_ref, o_ref):
        acc = p_ref[0]
        for s in range(1, n_slot):
            acc = acc + p_ref[s]
        o_ref[...] = acc

    return pl.pallas_call(
        body, grid=(m // tr,), in_specs=[pl.BlockSpec((n_slot, tr, n), lambda i: (0, i, 0))],
        out_specs=pl.BlockSpec((tr, n), lambda i: (i, 0)), out_shape=jax.ShapeDtypeStruct((m, n), parts.dtype), name=name,
    )(parts)


def _reduce_scatter_begin(gs, name):
    lands = [lax.empty((4, g.shape[0] // N_DEV, g.shape[1]), g.dtype) for g in gs]
    return _split_start(gs, lands, _sibling_plan, 4, "rs_d2d_start_" + name)


def _reduce_scatter_middle(started, after, name):
    gs, from_sibling = _split_wait(started, _sibling_plan, after, "rs_d2d_wait_" + name)
    chip_sums = [_sum_with_sibling(g, r, f"rs_sum2_{name}_{i}") for i, (g, r) in enumerate(zip(gs, from_sibling))]
    lands = [lax.empty((3, h.shape[0] // 4, h.shape[1]), h.dtype) for h in chip_sums]
    return _split_start(chip_sums, lands, _chips_plan, 3, "rs_ici_start_" + name)


def _reduce_scatter_end(started, after, name):
    chip_sums, from_chips = _split_wait(started, _chips_plan, after, "rs_ici_wait_" + name)
    return [_sum_with_chips(h, r, f"rs_sum4_{name}_{i}") for i, (h, r) in enumerate(zip(chip_sums, from_chips))]


def _adamw_update(w, g, m, v):
    mm = ADAM_B1 * m + (1.0 - ADAM_B1) * g
    vv = ADAM_B2 * v + (1.0 - ADAM_B2) * jnp.square(g)
    m_hat = mm / (1.0 - ADAM_B1 ** ADAM_STEP)
    v_hat = vv / (1.0 - ADAM_B2 ** ADAM_STEP)
    return -ADAM_LR * (m_hat / (jnp.sqrt(v_hat) + ADAM_EPS) + ADAM_WD * w), mm, vv


def _adamw_many(ws, gs, ms, vs, name):
    k = len(ws)
    shapes = [w.shape for w in ws]
    flat = [[a.reshape(-1, a.shape[-1]) for a in group] for group in (ws, gs, ms, vs)]

    def body(*refs):
        for i in range(k):
            d, mm, vv = _adamw_update(*(refs[j * k + i][...] for j in range(4)))
            refs[4 * k + i][...] = d
            refs[5 * k + i][...] = mm
            refs[6 * k + i][...] = vv

    outs = pl.pallas_call(
        body, out_shape=[jax.ShapeDtypeStruct(a.shape, f32) for a in flat[0]] * 3, name=name,
    )(*flat[0], *flat[1], *flat[2], *flat[3])
    return tuple([outs[j * k + i].reshape(shapes[i]) for i in range(k)] for j in range(3))


def _adamw(w, g, m, v, name):
    shape = w.shape
    n = shape[-1]
    r = w.size // n
    w2, g2, m2, v2 = (a.reshape(r, n) for a in (w, g, m, v))
    tr = _pick(r, (256, 208, 128, 64, 32, 16, 8))

    def body(w_ref, g_ref, m_ref, v_ref, d_ref, mo_ref, vo_ref):
        d_ref[...], mo_ref[...], vo_ref[...] = _adamw_update(w_ref[...], g_ref[...], m_ref[...], v_ref[...])

    spec = pl.BlockSpec((tr, n), lambda i: (i, 0))
    outs = pl.pallas_call(
        body, grid=(r // tr,), in_specs=[spec] * 4, out_specs=[spec] * 3,
        out_shape=[jax.ShapeDtypeStruct((r, n), f32)] * 3, name=name,
    )(w2, g2, m2, v2)
    return tuple(o.reshape(shape) for o in outs)


_SMALL = ("shift_mu", "w_decay0", "a0", "k_k", "k_a", "r_k", "ln_x_w", "ln_x_b", "v_mix0", "lb_logits",
          "g_norm_w", "ln_w", "ln_b")
_NAMES = ("w_in", "shift_mu", "w_decay0", "w_decay_up", "a0", "a_up", "k_k", "k_a", "r_k", "ln_x_w", "ln_x_b",
          "v_mix0", "v_mix_down", "v_mix_up", "lb_logits", "g_norm_w", "w_out", "ln_w", "ln_b")


def _pad_rows(a, rows, at_end):
    z = jnp.zeros((rows - a.shape[0], a.shape[1]), a.dtype)
    return jnp.concatenate([a, z] if at_end else [z, a], axis=0)


def kernel(x, w_in, shift_mu, w_decay0, w_decay_up, a0, a_up, k_k, k_a, r_k, ln_x_w, ln_x_b, v_mix0, v_mix_down, v_mix_up, lb_logits, g_norm_w, w_out, ln_w, ln_b, loss_target, m_w_in, m_shift_mu, m_w_decay0, m_w_decay_up, m_a0, m_a_up, m_k_k, m_k_a, m_r_k, m_ln_x_w, m_ln_x_b, m_v_mix0, m_v_mix_down, m_v_mix_up, m_lb_logits, m_g_norm_w, m_w_out, m_ln_w, m_ln_b, v_w_in, v_shift_mu, v_w_decay0, v_w_decay_up, v_a0, v_a_up, v_k_k, v_k_a, v_r_k, v_ln_x_w, v_ln_x_b, v_v_mix0, v_v_mix_down, v_v_mix_up, v_lb_logits, v_g_norm_w, v_w_out, v_ln_w, v_ln_b):
    weights = dict(w_in=w_in, shift_mu=shift_mu, w_decay0=w_decay0, w_decay_up=w_decay_up, a0=a0, a_up=a_up, k_k=k_k,
                   k_a=k_a, r_k=r_k, ln_x_w=ln_x_w, ln_x_b=ln_x_b, v_mix0=v_mix0, v_mix_down=v_mix_down,
                   v_mix_up=v_mix_up, lb_logits=lb_logits, g_norm_w=g_norm_w, w_out=w_out, ln_w=ln_w, ln_b=ln_b)
    mom1 = dict(w_in=m_w_in, shift_mu=m_shift_mu, w_decay0=m_w_decay0, w_decay_up=m_w_decay_up, a0=m_a0, a_up=m_a_up,
                k_k=m_k_k, k_a=m_k_a, r_k=m_r_k, ln_x_w=m_ln_x_w, ln_x_b=m_ln_x_b, v_mix0=m_v_mix0,
                v_mix_down=m_v_mix_down, v_mix_up=m_v_mix_up, lb_logits=m_lb_logits, g_norm_w=m_g_norm_w,
                w_out=m_w_out, ln_w=m_ln_w, ln_b=m_ln_b)
    mom2 = dict(w_in=v_w_in, shift_mu=v_shift_mu, w_decay0=v_w_decay0, w_decay_up=v_w_decay_up, a0=v_a0, a_up=v_a_up,
                k_k=v_k_k, k_a=v_k_a, r_k=v_r_k, ln_x_w=v_ln_x_w, ln_x_b=v_ln_x_b, v_mix0=v_v_mix0,
                v_mix_down=v_v_mix_down, v_mix_up=v_v_mix_up, lb_logits=v_lb_logits, g_norm_w=v_g_norm_w,
                w_out=v_w_out, ln_w=v_ln_w, ln_b=v_ln_b)
    assert x.shape[0] == 1 and w_in.shape[0] == DEPTH
    t, d = x.shape[1], x.shape[2]
    dr = w_decay0.shape[1]
    dh = g_norm_w.shape[1]
    rank_w, rank_a, rank_v = w_decay_up.shape[1], a_up.shape[1], v_mix_up.shape[1]
    rwc = 4 * dr + rank_w + rank_a
    assert rank_w + rank_a == LANES and rank_v <= LANES and dr + dh == d
    assert t % CHUNK == 0 and dr % LANES == 0 and dh % LANES == 0 and shift_mu.shape[1] == rwc
    n_pair = dr // LANES
    me = _index(_position())

    shard = dr // N_DEV
    pack = jnp.concatenate([w_decay_up[0], w_decay_up[1], a_up[0], a_up[1], v_mix_up[0], v_mix_down[0].T], axis=0)
    win_t0, pack = _all_gather_rows([w_in[0].T.astype(bf16), pack], "ag_first")
    win_t = [win_t0, None]
    wout = [None, None]
    late_blocks = [w_out[0].astype(bf16), w_in[1].T.astype(bf16), w_out[1].astype(bf16)]
    late_lands = [lax.dynamic_update_slice(lax.empty((N_DEV * blk.shape[0], blk.shape[1]), bf16), blk, (me * blk.shape[0], 0))
                  for blk in late_blocks]
    late_gather = _split_start(late_blocks, late_lands, _gather_plan, len(_GATHER_FLIPS), "ag_late_start",
                               after=(win_t[0], pack))
    pack = jnp.transpose(pack.reshape(N_DEV, -1, shard), (1, 0, 2)).reshape(-1, dr)
    offs = [0, rank_w, 2 * rank_w, 2 * rank_w + rank_a, 2 * rank_w + 2 * rank_a, 2 * rank_w + 2 * rank_a + rank_v,
            2 * rank_w + 2 * rank_a + 2 * rank_v]
    wdu_f = [pack[offs[0]:offs[1]], pack[offs[1]:offs[2]]]
    aup_f = [pack[offs[2]:offs[3]], pack[offs[3]:offs[4]]]
    vup_f = pack[offs[4]:offs[5]]
    vdown_f = pack[offs[5]:offs[6]].T

    def after_start(a, started):
        return a + started[-1][0:1, 0:1]

    def rwkv_params(l):
        mu = after_start(shift_mu[0:1], late_gather) if l == 0 else shift_mu[l:l + 1]
        prm = [mu, w_decay0[l:l + 1], a0[l:l + 1], _pad_rows(wdu_f[l], LANES, True),
               _pad_rows(aup_f[l], LANES, False)]
        if l == 1:
            prm += [v_mix0[0:1], _pad_rows(vdown_f.T, LANES, True).T, _pad_rows(vup_f, LANES, True)]
        rows = jnp.stack([k_k[l], k_a[l], r_k[l], ln_x_w[l], ln_x_b[l]] + [jnp.zeros((dr,), f32)] * 3, axis=0)
        pp = jnp.transpose(rows.reshape(8, n_pair, LANES), (1, 0, 2))
        return tuple(prm), pp

    h = x[0]
    h16 = h.astype(bf16)
    tgt = loss_target[0]
    saved = []
    vfirst = None
    for l in range(DEPTH):
        prm, pp = rwkv_params(l)
        proj = _matmul(h16, win_t[l], "nt", f"mm_proj_{l}", _MM_TILES["proj"])
        if l == 0:
            cat, vfirst, mck = _rwkv_fwd(False, proj, None, prm, pp, d)
        else:
            cat, mck = _rwkv_fwd(True, proj, vfirst, prm, pp, d)
        cat, sck = _hgrn_fwd(l == 1, proj, lb_logits, g_norm_w[l:l + 1], cat, rwc)
        if l == 0:
            _, arrived = _split_wait(late_gather, _gather_plan, cat, "ag_late_wait")
            wout[0], win_t[1], wout[1] = _gather_forward(arrived, "ag_late_forward")
        y = _matmul(cat, wout[l], "nn", f"mm_out_{l}", _MM_TILES["out"])
        saved.append((h, h16, proj, prm, pp, mck, sck, cat, y))
        if l < DEPTH - 1:
            h, h16 = _ln_fwd(h, y, ln_w[l:l + 1], ln_b[l:l + 1])
        else:
            top = _ln_loss_bwd(h, y, ln_w[l:l + 1], ln_b[l:l + 1], tgt)
    loss = lax.psum(top[4][0, 0], ("x", "y", "c"))

    grads = {}
    big = {}
    dvfirst = None
    d_lbl = None
    rs_started = {}
    for l in reversed(range(DEPTH)):
        h_l, h16_l, proj, prm, pp, mck, sck, cat, y = saved[l]
        if l == DEPTH - 1:
            dy, dy16, g_ln_w, g_ln_b = top[:4]
        else:
            dy, dy16, g_ln_w, g_ln_b = _ln_bwd(h_l, y, after_start(ln_w[l:l + 1], rs_started[l + 1]), ln_b[l:l + 1], dh_out)
        dcat = _matmul(dy16, wout[l], "nt", f"mm_dcat_{l}", _MM_TILES["dcat"])
        big[("w_out", l)] = _matmul(cat, dy16, "tn", f"mm_dwout_{l}", _MM_TILES["dwout"], out_dtype=bf16)
        if l == 1:
            outs = _rwkv_bwd(True, proj, vfirst, prm, pp, mck, dcat, None)
            dproj_r, dvfirst = outs[0], outs[1]
            dprm, dpp = outs[2:-1], outs[-1]
        else:
            outs = _rwkv_bwd(False, proj, None, prm, pp, mck, dcat, dvfirst)
            dproj_r = outs[0]
            dprm, dpp = outs[1:-1], outs[-1]
        dproj, dlbl_l, dgnw = _hgrn_bwd(l == 1, proj, lb_logits, g_norm_w[l:l + 1], sck, dcat, rwc, dproj_r)
        big[("w_in", l)] = _matmul(dproj, h16_l, "tn", f"mm_dwin_{l}", _MM_TILES["dwin"], out_dtype=bf16)
        sharded = [dprm[3][:rank_w].T, dprm[4][rank_w:].T]
        if l == 1:
            sharded += [dprm[6][:, :rank_v], dprm[7][:rank_v].T,
                        jnp.zeros((dr, LANES - 2 * rank_v), f32)]
        sharded = jnp.concatenate(sharded, axis=1).astype(bf16)
        d2d = _reduce_scatter_begin([big[("w_in", l)], big[("w_out", l)], sharded], f"l{l}")
        if l == 0:
            rs_started[l] = _reduce_scatter_middle(d2d, sharded, f"l{l}")
            token = rs_started[l][-1]
        else:
            token = d2d[-1]
        dh_out = _matmul(dproj, win_t[l], "nn", f"mm_dh_{l}", _MM_TILES["dh"], add=dy, add_scale=ALPHA, after=token)
        if l > 0:
            rs_started[l] = _reduce_scatter_middle(d2d, dh_out, f"l{l}")
        dpp = jnp.transpose(dpp, (1, 0, 2)).reshape(8, dr)
        grads[l] = dict(shift_mu=dprm[0][0], w_decay0=dprm[1][0], a0=dprm[2][0],
                        k_k=dpp[0], k_a=dpp[1], r_k=dpp[2], ln_x_w=dpp[3], ln_x_b=dpp[4],
                        g_norm_w=dgnw[0], ln_w=g_ln_w[0], ln_b=g_ln_b[0])
        if l == 1:
            grads[l].update(v_mix0=dprm[5][0])
            d_lbl = dlbl_l
    grad_x = dh_out[None]

    def both(name):
        return jnp.stack([grads[0][name], grads[1][name]])

    small = dict(shift_mu=both("shift_mu"), w_decay0=both("w_decay0"), a0=both("a0"), k_k=both("k_k"), k_a=both("k_a"),
                 r_k=both("r_k"), ln_x_w=both("ln_x_w"), ln_x_b=both("ln_x_b"), v_mix0=grads[1]["v_mix0"][None],
                 lb_logits=d_lbl, g_norm_w=both("g_norm_w"), ln_w=both("ln_w"), ln_b=both("ln_b"))
    flat = jnp.concatenate([small[nm].reshape(-1) for nm in _SMALL])
    n_flat = flat.shape[0]
    rows = -(-n_flat // (8 * LANES)) * 8
    flat = jnp.concatenate([flat, jnp.zeros((rows * LANES - n_flat,), f32)]).reshape(rows, LANES)
    total = _sum_slots(_all_gather_rows([flat], "ag_small_grads")[0].reshape(N_DEV, rows, LANES), "sum_small_grads").reshape(-1)
    gsm = {}
    off = 0
    for nm in _SMALL:
        size = small[nm].size
        gsm[nm] = total[off:off + size].reshape(small[nm].shape)
        off += size
    reduced = {1: _reduce_scatter_end(rs_started[1], dh_out, "l1")}
    reduced[0] = _reduce_scatter_end(rs_started[0], total, "l0")
    g_w_in_t = jnp.stack([reduced[l][0] for l in range(DEPTH)])
    gsm["w_in"] = jnp.transpose(g_w_in_t, (0, 2, 1))
    gsm["w_out"] = jnp.stack([reduced[l][1] for l in range(DEPTH)])
    gsm["w_decay_up"] = jnp.stack([reduced[l][2][:, :rank_w].T for l in range(DEPTH)])
    gsm["a_up"] = jnp.stack([reduced[l][2][:, rank_w:rank_w + rank_a].T for l in range(DEPTH)])
    gsm["v_mix_down"] = reduced[1][2][:, LANES:LANES + rank_v][None]
    gsm["v_mix_up"] = reduced[1][2][:, LANES + rank_v:LANES + 2 * rank_v].T[None]

    deltas, new_m, new_v = {}, {}, {}
    swap = lambda a: jnp.transpose(a, (0, 2, 1))
    deltas["w_in"], new_m["w_in"], new_v["w_in"] = (
        swap(a) for a in _adamw(swap(w_in), g_w_in_t, swap(m_w_in), swap(v_w_in), "adamw_w_in"))
    deltas["w_out"], new_m["w_out"], new_v["w_out"] = _adamw(w_out, gsm["w_out"], m_w_out, v_w_out, "adamw_w_out")
    rest = [nm for nm in _NAMES if nm not in ("w_in", "w_out")]
    d_rest, m_rest, v_rest = _adamw_many([weights[nm] for nm in rest], [gsm[nm] for nm in rest],
                                         [mom1[nm] for nm in rest], [mom2[nm] for nm in rest], "adamw_small")
    for i, nm in enumerate(rest):
        deltas[nm], new_m[nm], new_v[nm] = d_rest[i], m_rest[i], v_rest[i]
    return (loss, grad_x, *[gsm[nm] for nm in _NAMES], *[deltas[nm] for nm in _NAMES],
            *[new_m[nm] for nm in _NAMES], *[new_v[nm] for nm in _NAMES])
```

```python
import functools

import jax
import jax.numpy as jnp
from jax import lax
from jax.experimental import pallas as pl
from jax.experimental.pallas import tpu as pltpu

f32 = jnp.float32
bf16 = jnp.bfloat16

N_DEV = 8
CHUNK = 64
LANES = 128
RWKV_HEAD = 64
DEPTH = 2
ALPHA = (2 * DEPTH) ** 0.25
LN_EPS = 1e-5
GN_EPS = 64e-5
RMS_EPS = 1e-5
LB_FLOOR = 1e-30
ADAM_LR, ADAM_B1, ADAM_B2, ADAM_EPS, ADAM_WD, ADAM_STEP = 0.001, 0.9, 0.999, 1e-08, 0.01, 10
MESH = pl.DeviceIdType.MESH


def _iota(shape, d):
    return lax.broadcasted_iota(jnp.int32, shape, d)


_DIMS = {"nn": (((1,), (0,)), ((), ())), "nt": (((1,), (1,)), ((), ())), "tn": (((0,), (0,)), ((), ()))}
_BATCH_DIMS = {"nn": (((2,), (1,)), ((0,), (0,))), "nt": (((2,), (2,)), ((0,), (0,))), "tn": (((1,), (1,)), ((0,), (0,)))}
_K_AXES = {"nn": (-1, -2), "nt": (-1, -1), "tn": (-2, -2)}


def _mxu(a, b, mode):
    return lax.dot_general(a, b, (_BATCH_DIMS if a.ndim == 3 else _DIMS)[mode], preferred_element_type=f32)


def _split(x):
    hi = x.astype(bf16)
    return hi, (x - hi.astype(f32)).astype(bf16)


def _mm2_impl(a, b, mode, passes=3):
    ah, al = _split(a)
    if passes == 3:
        bh, bl = _split(b)
        lhs, rhs = [ah, ah, al], [bh, bl, bh]
    else:
        bh = b.astype(bf16)
        lhs, rhs = [ah, al], [bh, bh]
    ka, kb = _K_AXES[mode]
    k = a.shape[ka]
    if k % (LANES if -1 in (ka, kb) else 16) == 0:
        return _mxu(jnp.concatenate(lhs, axis=ka), jnp.concatenate(rhs, axis=kb), mode)
    out = _mxu(lhs[0], rhs[0], mode)
    for x, y in zip(lhs[1:], rhs[1:]):
        out = out + _mxu(x, y, mode)
    return out


@functools.partial(jax.custom_vjp, nondiff_argnums=(2, 3))
def _mm2(a, b, mode, passes=3):
    return _mm2_impl(a, b, mode, passes)


def _mm2_fwd(a, b, mode, passes):
    return _mm2_impl(a, b, mode, passes), (a, b)


def _mm2_bwd(mode, passes, res, g):
    a, b = res
    if mode == "nn":
        return _mm2_impl(g, b, "nt", passes), _mm2_impl(a, g, "tn", passes)
    if mode == "nt":
        return _mm2_impl(g, b, "nn", passes), _mm2_impl(g, a, "tn", passes)
    return _mm2_impl(b, g, "nt", passes), _mm2_impl(a, g, "nn", passes)


_mm2.defvjp(_mm2_fwd, _mm2_bwd)

TRI_PASSES = 2
APPLY_PASSES = 2


def _const_impl(cm, x, mode):
    if mode in ("r", "rt"):
        shape = x.shape
        out = _mxu(x.astype(bf16).reshape(-1, shape[-1]), cm, "nn" if mode == "r" else "nt")
        return out.reshape(shape[:-1] + (out.shape[-1],))
    hi, lo = _split(x)
    if x.ndim == 3:
        cm = jnp.broadcast_to(cm, (x.shape[0],) + cm.shape)
    return _mxu(cm, hi, mode) + _mxu(cm, lo, mode)


@jax.custom_vjp
def _const_left(cm, x):
    return _const_impl(cm, x, "nn")


_const_left.defvjp(lambda cm, x: (_const_impl(cm, x, "nn"), cm),
                   lambda cm, g: (jnp.zeros_like(cm), _const_impl(cm, g, "tn")))


@jax.custom_vjp
def _const_right(x, cm):
    return _const_impl(cm, x, "r")


_const_right.defvjp(lambda x, cm: (_const_impl(cm, x, "r"), cm),
                    lambda cm, g: (_const_impl(cm, g, "rt"), jnp.zeros_like(cm)))


def _tri_inv(a):
    n = a.shape[-1]
    tm = (_iota((n, n), 0) == _iota((n, n), 1)).astype(f32) + a
    ak = a
    for _ in range(5):
        ak = _mm2_impl(ak, ak, "nn", TRI_PASSES)
        tm = tm + _mm2_impl(tm, ak, "nn", TRI_PASSES)
    return tm


@jax.custom_vjp
def _tri_solve(tm, a, x):
    del a
    return _mm2_impl(tm, x, "nn")


def _tri_solve_fwd(tm, a, x):
    u = _mm2_impl(tm, x, "nn")
    return u, (tm, u)


def _tri_solve_bwd(res, du):
    tm, u = res
    dx = _mm2_impl(tm, du, "tn")
    return jnp.zeros_like(tm), _mm2_impl(dx, u, "nt"), dx


_tri_solve.defvjp(_tri_solve_fwd, _tri_solve_bwd)


def _col_of_row(row_vec):
    n = row_vec.shape[-1]
    eye = _iota((n, n), 0) == _iota((n, n), 1)
    return jnp.sum(jnp.where(eye, jnp.broadcast_to(row_vec, row_vec.shape[:-2] + (n, n)), 0.0), axis=-1, keepdims=True)


def _softplus(x):
    return jnp.maximum(x, 0.0) + jnp.log1p(jnp.exp(-jnp.abs(x)))


def _log_sigmoid(x):
    return -_softplus(-x)


def _logaddexp(a, b):
    return jnp.maximum(a, b) + jnp.log1p(jnp.exp(-jnp.abs(a - b)))


def _silu(x):
    return x * jax.nn.sigmoid(x)


def _tril(c, strict):
    r, s = _iota((c, c), 0), _iota((c, c), 1)
    return (r > s) if strict else (r >= s)


def _last_row(a):
    c = a.shape[-2]
    return jnp.sum(jnp.where(_iota(a.shape, a.ndim - 2) == c - 1, a, 0.0), axis=-2, keepdims=True)


def _rwkv_pre(layer1, prm, y, prev, vf):
    c = y.shape[0]
    if layer1:
        mu, w0, a0, wup, aup, v0, vdown, vup = prm
    else:
        mu, w0, a0, wup, aup = prm
    dr = w0.shape[1]
    shift = (_iota((c, c), 0) == _iota((c, c), 1) + 1).astype(bf16)
    y_prev = _const_left(shift, y) + jnp.where(_iota((c, 1), 0) == 0, prev, 0.0)
    rw = y + mu * (y_prev - y)
    r, k, v, z = (rw[:, i * dr:(i + 1) * dr] for i in range(4))
    wdad = rw[:, 4 * dr:4 * dr + LANES]
    w_raw = w0 + _mm2(jnp.tanh(wdad), wup, "nn")
    lw = -jnp.exp(-_softplus(-w_raw) - 0.5)
    asig = jax.nn.sigmoid(a0 + _mm2(wdad, aup, "nn"))
    if layer1:
        v = v + (vf - v) * jax.nn.sigmoid(v0 + _mm2(_mm2(v, vdown, "nn"), vup, "nn"))
    return r, k, v, z, lw, asig


def _rwkv_pair(pp, m0, xs, tm=None):
    kkw, kaw, rkw, gnw, gnb = pp
    r, k, v, z, lw, asig = xs
    c = r.shape[-2]
    n2 = 2 * c
    lane = _iota((1, LANES), 1)
    mh0, mh1 = (lane < RWKV_HEAD).astype(f32), (lane >= RWKV_HEAD).astype(f32)
    same_head = _iota((LANES, LANES), 0) // RWKV_HEAD == _iota((LANES, LANES), 1) // RWKV_HEAD
    g = same_head.astype(bf16)

    def seg(x):
        return _const_right(x, g)

    def stack(x):
        return jnp.concatenate([x * mh0, x * mh1], axis=-2)

    kk = k * kkw
    kk = kk / jnp.maximum(jnp.sqrt(seg(kk * kk)), 1e-12)
    k2 = k * (1.0 + (asig - 1.0) * kaw)
    a = -kk
    b = kk * asig
    cum = _const_left(_tril(c, False).astype(bf16), lw)
    at = stack(a * jnp.exp(cum - lw))
    rt = stack(r * jnp.exp(cum))
    en = jnp.exp(-cum)
    sc = _mm2(jnp.concatenate([at, rt], axis=-2), jnp.concatenate([stack(b * en), stack(k2 * en)], axis=-2), "nt")
    row, col = _iota((n2, n2), 0), _iota((n2, n2), 1)
    same = row // c == col // c
    strict = same & (row % c > col % c)
    incl = same & (row % c >= col % c)
    aab = jnp.where(strict, sc[..., :n2, :n2], 0.0)
    aak = jnp.where(strict, sc[..., :n2, n2:], 0.0)
    arb = jnp.where(incl, sc[..., n2:, :n2], 0.0)
    ark = jnp.where(incl, sc[..., n2:, n2:], 0.0)
    vv = jnp.concatenate([v, v], axis=-2)
    mask_st = jnp.concatenate([jnp.broadcast_to(mh0, (c, LANES)), jnp.broadcast_to(mh1, (c, LANES))], axis=0)
    x_st = _mm2(jnp.concatenate([at, aak], axis=-1), jnp.concatenate([m0, vv], axis=-2), "nn", APPLY_PASSES)
    if tm is None:
        tm = _tri_inv(lax.stop_gradient(aab))
    u_st = _tri_solve(tm, aab, x_st) * mask_st
    o_st = _mm2(jnp.concatenate([rt, arb, ark], axis=-1), jnp.concatenate([m0, u_st, vv], axis=-2), "nn", APPLY_PASSES) * mask_st
    u = u_st[..., :c, :] + u_st[..., c:, :]
    o = o_st[..., :c, :] + o_st[..., c:, :]
    cum_last = _last_row(cum)
    dec_end = jnp.exp(cum_last - cum)
    m_new = _col_of_row(jnp.exp(cum_last)) * m0 + _mm2(
        jnp.concatenate([b * dec_end, k2 * dec_end], axis=-2), jnp.concatenate([u, v], axis=-2), "tn", APPLY_PASSES) * same_head.astype(f32)
    mean = seg(o) * (1.0 / RWKV_HEAD)
    d = o - mean
    var = seg(d * d) * (1.0 / RWKV_HEAD)
    on = d * lax.rsqrt(var + GN_EPS) * gnw + gnb
    bonus = seg(r * k2 * rkw) * v
    return (on + bonus) * _silu(z), m_new, tm


def _split_lanes(a, n):
    return [a[:, i * LANES:(i + 1) * LANES] for i in range(n)]


def _rwkv_step(layer1, prm, y, prev, vf, pp, m0, tm=None):
    xs = _rwkv_pre(layer1, prm, y, prev, vf)
    n_pair = m0.shape[0]
    og, m_new, tm = _rwkv_pair(pp, m0, tuple(jnp.concatenate([p[None] for p in _split_lanes(a, n_pair)], axis=0) for a in xs), tm)
    return og, m_new, xs[2], tm


def _group(n):
    return n


def _rwkv_specs(layer1, t, dr, rwc, n_pair, rev):
    nc = t // CHUNK
    grp = _group(n_pair)

    def cidx(c):
        return (nc - 1 - c) if rev else c

    full = lambda shape: pl.BlockSpec(shape, lambda c, p: tuple(0 for _ in shape))
    specs = [
        pl.BlockSpec((CHUNK, rwc), lambda c, p: (cidx(c), 0)),
        pl.BlockSpec((8, rwc), lambda c, p: (jnp.maximum(cidx(c) * (CHUNK // 8) - 1, 0), 0)),
    ]
    if layer1:
        specs.append(pl.BlockSpec((CHUNK, dr), lambda c, p: (cidx(c), 0)))
    prm_shapes = [(1, rwc), (1, dr), (1, dr), (LANES, dr), (LANES, dr)]
    if layer1:
        prm_shapes += [(1, dr), (dr, LANES), (LANES, dr)]
    specs += [full(s) for s in prm_shapes]
    specs.append(pl.BlockSpec((grp, 8, LANES), lambda c, p: (p, 0, 0)))
    return specs, prm_shapes, cidx, full


def _rwkv_fwd(layer1, proj, vf, prm, pp, cat_width):
    t = proj.shape[0]
    dr = prm[1].shape[1]
    rwc = prm[0].shape[1]
    n_pair = dr // LANES
    nc = t // CHUNK
    n_prm = len(prm)
    specs, _, _, _ = _rwkv_specs(layer1, t, dr, rwc, n_pair, False)

    def body(*refs):
        y_ref, prev_ref = refs[0], refs[1]
        i = 2
        vf_ref = None
        if layer1:
            vf_ref = refs[i]
            i += 1
        prm_refs = refs[i:i + n_prm]
        i += n_prm
        pp_ref = refs[i]
        i += 1
        cat_ref = refs[i]
        i += 1
        vout_ref = None
        if not layer1:
            vout_ref = refs[i]
            i += 1
        mck_ref, m_s = refs[i], refs[i + 1]
        c = pl.program_id(0)

        @pl.when(c == 0)
        def _():
            m_s[...] = jnp.zeros_like(m_s)

        prev = prev_ref[pl.ds(7, 1), :] * (c != 0).astype(f32)
        m0 = m_s[...]
        ppv = tuple(pp_ref[:, pl.ds(q, 1), :] for q in range(5))
        og, m_new, v, tm = _rwkv_step(layer1, tuple(r[...] for r in prm_refs), y_ref[...], prev,
                                      vf_ref[...] if layer1 else None, ppv, m0)
        mck_ref[0, :n_pair] = m0
        mck_ref[0, n_pair:] = tm
        if not layer1:
            vout_ref[...] = v
        for j in range(n_pair):
            cat_ref[:, j * LANES:(j + 1) * LANES] = og[j]
        m_s[...] = m_new

    grp = _group(n_pair)
    assert grp == n_pair
    out_shape = [jax.ShapeDtypeStruct((t, cat_width), f32)]
    out_specs = [pl.BlockSpec((CHUNK, grp * LANES), lambda c, p: (c, p))]
    if not layer1:
        out_shape.append(jax.ShapeDtypeStruct((t, dr), f32))
        out_specs.append(pl.BlockSpec((CHUNK, dr), lambda c, p: (c, 0)))
    out_shape.append(jax.ShapeDtypeStruct((nc, 2 * n_pair, LANES, LANES), f32))
    out_specs.append(pl.BlockSpec((1, 2 * grp, LANES, LANES), lambda c, p: (c, p, 0, 0)))
    args = [proj, proj] + ([vf] if layer1 else []) + list(prm) + [pp]
    return pl.pallas_call(
        body, grid=(nc, 1), in_specs=specs, out_specs=out_specs, out_shape=out_shape,
        scratch_shapes=[pltpu.VMEM((n_pair, LANES, LANES), f32)],
        compiler_params=pltpu.CompilerParams(dimension_semantics=("arbitrary", "arbitrary")),
        name=f"rwkv_fwd_l{int(layer1)}",
    )(*args)


def _rwkv_bwd(layer1, proj, vf, prm, pp, mck, dcat, dvout):
    t = proj.shape[0]
    dr = prm[1].shape[1]
    rwc = prm[0].shape[1]
    n_pair = dr // LANES
    nc = t // CHUNK
    n_prm = len(prm)
    specs, prm_shapes, cidx, full = _rwkv_specs(layer1, t, dr, rwc, n_pair, True)
    grp = _group(n_pair)
    assert grp == n_pair
    specs.append(pl.BlockSpec((1, 2 * grp, LANES, LANES), lambda c, p: (cidx(c), p, 0, 0)))
    specs.append(pl.BlockSpec((CHUNK, grp * LANES), lambda c, p: (cidx(c), p)))
    if not layer1:
        specs.append(pl.BlockSpec((CHUNK, dr), lambda c, p: (cidx(c), 0)))

    def body(*refs):
        y_ref, prev_ref = refs[0], refs[1]
        i = 2
        vf_ref = None
        if layer1:
            vf_ref = refs[i]
            i += 1
        prm_refs = refs[i:i + n_prm]
        i += n_prm
        pp_ref, mck_ref, dog_ref = refs[i], refs[i + 1], refs[i + 2]
        i += 3
        dvout_ref = None
        if not layer1:
            dvout_ref = refs[i]
            i += 1
        dy_ref = refs[i]
        i += 1
        dvf_ref = None
        if layer1:
            dvf_ref = refs[i]
            i += 1
        dprm_refs = refs[i:i + n_prm]
        i += n_prm
        dpp_ref = refs[i]
        dm_s, dprev_s = refs[i + 1:i + 3]
        c = pl.program_id(0)
        cr = nc - 1 - c

        @pl.when(c == 0)
        def _():
            dm_s[...] = jnp.zeros_like(dm_s)
            dprev_s[...] = jnp.zeros_like(dprev_s)
            dpp_ref[...] = jnp.zeros_like(dpp_ref)
            for r in dprm_refs:
                r[...] = jnp.zeros_like(r)

        prev = prev_ref[pl.ds(7, 1), :] * (cr != 0).astype(f32)
        prm_v = tuple(r[...] for r in prm_refs)
        ppv = tuple(pp_ref[:, pl.ds(q, 1), :] for q in range(5))
        dog = jnp.stack([dog_ref[:, j * LANES:(j + 1) * LANES] for j in range(n_pair)], axis=0)
        m0, tm = mck_ref[0, :n_pair], mck_ref[0, n_pair:]
        no_tm = jnp.zeros_like(tm)
        if layer1:
            _, vjp = jax.vjp(lambda a, b, d, e, g, h: _rwkv_step(True, a, b, d, e, g, h, tm),
                             prm_v, y_ref[...], prev, vf_ref[...], ppv, m0)
            dprm, dy, dprev, dvf, dppv, dm0 = vjp((dog, dm_s[...], jnp.zeros((CHUNK, dr), f32), no_tm))
            dvf_ref[...] = dvf
        else:
            _, vjp = jax.vjp(lambda a, b, d, e, g: _rwkv_step(False, a, b, d, None, e, g, tm), prm_v, y_ref[...], prev, ppv, m0)
            dprm, dy, dprev, dppv, dm0 = vjp((dog, dm_s[...], dvout_ref[...], no_tm))
        dm_s[...] = dm0
        for q in range(5):
            dpp_ref[:, pl.ds(q, 1), :] += dppv[q]
        dy_ref[...] = (dy + jnp.where(_iota((CHUNK, 1), 0) == CHUNK - 1, dprev_s[...], 0.0)).astype(bf16)
        dprev_s[...] = dprev
        for r, gval in zip(dprm_refs, dprm):
            r[...] += gval

    out_shape = [jax.ShapeDtypeStruct((t, proj.shape[1]), bf16)]
    out_specs = [pl.BlockSpec((CHUNK, rwc), lambda c, p: (cidx(c), 0))]
    if layer1:
        out_shape.append(jax.ShapeDtypeStruct((t, dr), f32))
        out_specs.append(pl.BlockSpec((CHUNK, dr), lambda c, p: (cidx(c), 0)))
    out_shape += [jax.ShapeDtypeStruct(s, f32) for s in prm_shapes]
    out_specs += [full(s) for s in prm_shapes]
    out_shape.append(jax.ShapeDtypeStruct((n_pair, 8, LANES), f32))
    out_specs.append(full((n_pair, 8, LANES)))
    args = [proj, proj] + ([vf] if layer1 else []) + list(prm) + [pp, mck, dcat] + ([] if layer1 else [dvout])
    return pl.pallas_call(
        body, grid=(nc, 1), in_specs=specs, out_specs=out_specs, out_shape=out_shape,
        scratch_shapes=[pltpu.VMEM((n_pair, LANES, LANES), f32), pltpu.VMEM((1, rwc), f32)],
        compiler_params=pltpu.CompilerParams(dimension_semantics=("arbitrary", "arbitrary")),
        name=f"rwkv_bwd_l{int(layer1)}",
    )(*args)


def _hgrn_chunk(layer1, lbl, gnw, s0, q_raw, f_raw, i_in, z):
    c = q_raw.shape[-2]
    q = _silu(q_raw)
    ls = _log_sigmoid(f_raw)
    if layer1:
        l0, l1 = lbl[..., 0:1, :], lbl[..., 1:2, :]
        mx = jnp.maximum(l0, l1)
        e0, e1 = jnp.exp(l0 - mx), jnp.exp(l1 - mx)
        sm0, sm1 = e0 / (e0 + e1), e1 / (e0 + e1)
        lb = (sm0 + sm1) - sm0
        log_f = _logaddexp(jnp.log(jnp.maximum(lb, LB_FLOOR)), jnp.log1p(-lb) + ls)
        k = (1.0 - lb) * jax.nn.sigmoid(-f_raw)
    else:
        log_f = _logaddexp(jnp.full_like(ls, jnp.log(jnp.float32(LB_FLOOR))), ls)
        k = jax.nn.sigmoid(-f_raw)
    row, col = _iota((c, c), 0), _iota((c, c), 1)
    trow = _iota((c, 1), 0)
    halves = []
    half = c // 2
    while half >= 1:
        halves.append(half)
        half //= 2
    cmat = jnp.concatenate([(col <= row).astype(f32)]
                           + [(col <= (row // (2 * hf)) * (2 * hf) + hf - 1).astype(f32) for hf in halves], axis=0)
    ball = _const_left(cmat.astype(bf16), log_f)
    b = ball[..., :c, :]
    att = None
    for lvl, hf in enumerate(halves):
        blk = 2 * hf
        bref = ball[..., (lvl + 1) * c:(lvl + 2) * c, :]
        upper = (trow % blk) >= hf
        dec = jnp.exp(jnp.where(upper, b - bref, bref - b))
        qh = jnp.where(upper, q * dec, 0.0)
        kh = jnp.where(upper, 0.0, k * dec)
        term = jnp.where(row // blk == col // blk, _mm2(qh, kh, "nt", APPLY_PASSES), 0.0)
        att = term if att is None else att + term
    lhs = jnp.concatenate([q * jnp.exp(b), att, jnp.zeros(att.shape[:-1] + (LANES - c,), f32)], axis=-1)
    rhs = jnp.concatenate([s0, i_in, jnp.zeros(i_in.shape[:-2] + (LANES - c, i_in.shape[-1]), f32)], axis=-2)
    o = _mm2(lhs, rhs, "nn", APPLY_PASSES) + jnp.sum(q * k, axis=-1, keepdims=True) * i_in
    b_last = _last_row(b)
    s_new = _col_of_row(jnp.exp(b_last)) * s0 + _mm2(k * jnp.exp(b_last - b), i_in, "tn", APPLY_PASSES)
    o = o * lax.rsqrt(jnp.mean(o * o, axis=-1, keepdims=True) + RMS_EPS)
    return o * gnw * _silu(z), s_new


def _hgrn_in_specs(t, dh, col0, rev):
    nc = t // CHUNK
    nh = dh // LANES

    def cidx(c):
        return (nc - 1 - c) if rev else c

    grp = _group(nh)
    specs = [pl.BlockSpec((CHUNK, LANES), functools.partial(lambda g, j, h, c: (cidx(c), col0 + g * nh + h * grp + j), g, j))
             for j in range(grp) for g in range(4)]
    specs.append(pl.BlockSpec((2, grp * LANES), lambda h, c: (0, h)))
    specs.append(pl.BlockSpec((1, grp * LANES), lambda h, c: (0, h)))
    return specs, cidx, grp


def _hgrn_fwd(layer1, proj, lbl, gnw, cat, rwc):
    t, d = cat.shape
    dh = gnw.shape[1]
    nh = dh // LANES
    nc = t // CHUNK
    col0 = rwc // LANES
    specs, _, grp = _hgrn_in_specs(t, dh, col0, False)
    specs.append(pl.BlockSpec(memory_space=pl.ANY))
    assert (d - dh) % (grp * LANES) == 0
    cat_col0 = (d - dh) // (grp * LANES)

    def body(*refs):
        x_refs = refs[:4 * grp]
        lbl_ref, gnw_ref, _, cat_ref, sck_ref, s_s = refs[4 * grp:]
        c = pl.program_id(1)

        @pl.when(c == 0)
        def _():
            s_s[...] = jnp.zeros_like(s_s)

        lanes = [slice(j * LANES, (j + 1) * LANES) for j in range(grp)]
        s0 = s_s[...]
        sck_ref[:, 0] = s0
        out, s_new = _hgrn_chunk(layer1, jnp.stack([lbl_ref[:, ln] for ln in lanes]), jnp.stack([gnw_ref[:, ln] for ln in lanes]),
                                 s0, *(jnp.stack([x_refs[4 * j + g][...] for j in range(grp)]) for g in range(4)))
        for j in range(grp):
            cat_ref[:, lanes[j]] = out[j]
        s_s[...] = s_new

    return pl.pallas_call(
        body, grid=(nh // grp, nc), in_specs=specs,
        out_specs=[pl.BlockSpec((CHUNK, grp * LANES), lambda h, c: (c, cat_col0 + h)),
                   pl.BlockSpec((grp, 1, LANES, LANES), lambda h, c: (h, c, 0, 0))],
        out_shape=[jax.ShapeDtypeStruct((t, d), f32), jax.ShapeDtypeStruct((nh, nc, LANES, LANES), f32)],
        scratch_shapes=[pltpu.VMEM((grp, LANES, LANES), f32)],
        input_output_aliases={4 * grp + 2: 0},
        compiler_params=pltpu.CompilerParams(dimension_semantics=("arbitrary", "arbitrary")),
        name=f"hgrn_fwd_l{int(layer1)}",
    )(*([proj] * (4 * grp)), lbl, gnw, cat)


def _hgrn_bwd(layer1, proj, lbl, gnw, sck, dcat, rwc, dproj):
    t, d = dcat.shape
    dh = gnw.shape[1]
    nh = dh // LANES
    nc = t // CHUNK
    col0 = rwc // LANES
    specs, cidx, grp = _hgrn_in_specs(t, dh, col0, True)
    assert grp == nh and (d - dh) % (grp * LANES) == 0
    cat_col0 = (d - dh) // (grp * LANES)
    specs.append(pl.BlockSpec((grp, 1, LANES, LANES), lambda h, c: (h, cidx(c), 0, 0)))
    specs.append(pl.BlockSpec((CHUNK, grp * LANES), lambda h, c: (cidx(c), cat_col0 + h)))
    specs.append(pl.BlockSpec(memory_space=pl.ANY))

    def body(*refs):
        x_refs = refs[:4 * grp]
        lbl_ref, gnw_ref, sck_ref, do_ref, _, dp_hbm, dlbl_ref, dgnw_ref, ds_s, stage, sems = refs[4 * grp:]
        c = pl.program_id(1)
        slot = c % 2

        def put(s, g, chunk):
            return pltpu.make_async_copy(stage.at[s, g], dp_hbm.at[pl.ds(chunk * CHUNK, CHUNK), pl.ds(rwc + g * dh, dh)],
                                         sems.at[s, g])

        @pl.when(c == 0)
        def _():
            ds_s[...] = jnp.zeros_like(ds_s)
            dlbl_ref[...] = jnp.zeros_like(dlbl_ref)
            dgnw_ref[...] = jnp.zeros_like(dgnw_ref)

        @pl.when(c >= 2)
        def _():
            for g in range(4):
                put(slot, g, 0).wait()

        lanes = [slice(j * LANES, (j + 1) * LANES) for j in range(grp)]
        _, vjp = jax.vjp(functools.partial(_hgrn_chunk, layer1),
                         jnp.stack([lbl_ref[:, ln] for ln in lanes]), jnp.stack([gnw_ref[:, ln] for ln in lanes]), sck_ref[:, 0],
                         *(jnp.stack([x_refs[4 * j + g][...] for j in range(grp)]) for g in range(4)))
        dlbl, dgnw, ds0, dq, df, di, dz = vjp((jnp.stack([do_ref[:, ln] for ln in lanes]), ds_s[...]))
        ds_s[...] = ds0
        for j in range(grp):
            dlbl_ref[:, lanes[j]] += dlbl[j]
            dgnw_ref[:, lanes[j]] += dgnw[j]
            for g, val in enumerate((dq, df, di, dz)):
                stage[slot, g, :, lanes[j]] = val[j].astype(bf16)
        for g in range(4):
            put(slot, g, nc - 1 - c).start()

        @pl.when(c == nc - 1)
        def _():
            for g in range(4):
                put(slot, g, 0).wait()
                if nc >= 2:
                    put(1 - slot, g, 0).wait()

    return pl.pallas_call(
        body, grid=(1, nc), in_specs=specs,
        out_specs=[pl.BlockSpec(memory_space=pl.ANY),
                   pl.BlockSpec((2, grp * LANES), lambda h, c: (0, h)),
                   pl.BlockSpec((1, grp * LANES), lambda h, c: (0, h))],
        out_shape=[jax.ShapeDtypeStruct(dproj.shape, dproj.dtype), jax.ShapeDtypeStruct((2, dh), f32),
                   jax.ShapeDtypeStruct((1, dh), f32)],
        scratch_shapes=[pltpu.VMEM((grp, LANES, LANES), f32), pltpu.VMEM((2, 4, CHUNK, dh), bf16),
                        pltpu.SemaphoreType.DMA((2, 4))],
        input_output_aliases={4 * grp + 4: 0},
        compiler_params=pltpu.CompilerParams(dimension_semantics=("arbitrary", "arbitrary")),
        name=f"hgrn_bwd_l{int(layer1)}",
    )(*([proj] * (4 * grp)), lbl, gnw, sck, dcat, dproj)


def _ln(h, y, w, b):
    u = ALPHA * h + y
    mu = jnp.mean(u, axis=-1, keepdims=True)
    var = jnp.mean(jnp.square(u - mu), axis=-1, keepdims=True)
    return (u - mu) * lax.rsqrt(var + LN_EPS) * w + b


def _row_tile(t):
    return 256 if t % 256 == 0 else t


def _ln_fwd(h, y, w, b):
    t, d = h.shape
    tr = _row_tile(t)

    def body(h_ref, y_ref, w_ref, b_ref, o_ref, o16_ref):
        out = _ln(h_ref[...], y_ref[...], w_ref[...], b_ref[...])
        o_ref[...] = out
        o16_ref[...] = out.astype(bf16)

    row = pl.BlockSpec((tr, d), lambda i: (i, 0))
    vec = pl.BlockSpec((1, d), lambda i: (0, 0))
    return pl.pallas_call(body, grid=(t // tr,), in_specs=[row, row, vec, vec], out_specs=[row, row],
                          out_shape=[jax.ShapeDtypeStruct((t, d), f32), jax.ShapeDtypeStruct((t, d), bf16)],
                          name="ln_fwd")(h, y, w, b)


def _ln_loss_bwd(h, y, w, b, tgt):
    t, d = h.shape
    tr = _row_tile(t)

    def body(h_ref, y_ref, w_ref, b_ref, t_ref, dy_ref, dy16_ref, dw_ref, db_ref, loss_ref):
        @pl.when(pl.program_id(0) == 0)
        def _():
            dw_ref[...] = jnp.zeros_like(dw_ref)
            db_ref[...] = jnp.zeros_like(db_ref)
            loss_ref[...] = jnp.zeros_like(loss_ref)

        out, vjp = jax.vjp(lambda yy, ww, bb: _ln(h_ref[...], yy, ww, bb), y_ref[...], w_ref[...], b_ref[...])
        err = out - t_ref[...]
        loss_ref[...] += 0.5 * jnp.sum(jnp.mean(jnp.square(err), axis=-1, keepdims=True), axis=0, keepdims=True)
        dy, dw, db = vjp(err * (1.0 / d))
        dy_ref[...] = dy
        dy16_ref[...] = dy.astype(bf16)
        dw_ref[...] += dw
        db_ref[...] += db

    row = pl.BlockSpec((tr, d), lambda i: (i, 0))
    vec = pl.BlockSpec((1, d), lambda i: (0, 0))
    return pl.pallas_call(
        body, grid=(t // tr,), in_specs=[row, row, vec, vec, row],
        out_specs=[row, row, vec, vec, pl.BlockSpec((1, LANES), lambda i: (0, 0))],
        out_shape=[jax.ShapeDtypeStruct((t, d), f32), jax.ShapeDtypeStruct((t, d), bf16), jax.ShapeDtypeStruct((1, d), f32),
                   jax.ShapeDtypeStruct((1, d), f32), jax.ShapeDtypeStruct((1, LANES), f32)],
        compiler_params=pltpu.CompilerParams(dimension_semantics=("arbitrary",)), name="ln_loss_bwd")(h, y, w, b, tgt)


def _ln_bwd(h, y, w, b, dout):
    t, d = h.shape
    tr = _row_tile(t)

    def body(h_ref, y_ref, w_ref, b_ref, do_ref, dy_ref, dy16_ref, dw_ref, db_ref):
        @pl.when(pl.program_id(0) == 0)
        def _():
            dw_ref[...] = jnp.zeros_like(dw_ref)
            db_ref[...] = jnp.zeros_like(db_ref)

        _, vjp = jax.vjp(lambda yy, ww, bb: _ln(h_ref[...], yy, ww, bb), y_ref[...], w_ref[...], b_ref[...])
        dy, dw, db = vjp(do_ref[...])
        dy_ref[...] = dy
        dy16_ref[...] = dy.astype(bf16)
        dw_ref[...] += dw
        db_ref[...] += db

    row = pl.BlockSpec((tr, d), lambda i: (i, 0))
    vec = pl.BlockSpec((1, d), lambda i: (0, 0))
    return pl.pallas_call(
        body, grid=(t // tr,), in_specs=[row, row, vec, vec, row], out_specs=[row, row, vec, vec],
        out_shape=[jax.ShapeDtypeStruct((t, d), f32), jax.ShapeDtypeStruct((t, d), bf16),
                   jax.ShapeDtypeStruct((1, d), f32), jax.ShapeDtypeStruct((1, d), f32)],
        compiler_params=pltpu.CompilerParams(dimension_semantics=("arbitrary",)), name="ln_bwd")(h, y, w, b, dout)


def _pick(n, prefs):
    for p in prefs:
        if n % p == 0:
            return p
    return n


def _tile(n, want):
    if n <= want:
        return n
    for cand in range(want - want % LANES, 0, -LANES):
        if n % cand == 0:
            return cand
    return n


_MM_TILES = {"proj": (1024, 1664, 2048), "out": (1024, 1024, 2048), "dcat": (1024, 1024, 2048),
             "dwout": (512, 2048, 2048), "dwin": (640, 2048, 2048), "dh": (1024, 1024, 1664)}


def _matmul(a, b, mode, name, tiles, add=None, add_scale=1.0, out_dtype=f32, after=None):
    if mode == "nn":
        (m, k), n = a.shape, b.shape[1]
    elif mode == "nt":
        (m, k), n = a.shape, b.shape[0]
    else:
        (k, m), n = a.shape, b.shape[1]
    tm, tn, tk = _tile(m, tiles[0]), _tile(n, tiles[1]), _tile(k, tiles[2])
    nk = k // tk
    cache_a = nk == 1 and a.dtype != bf16 and n // tn > 1

    def body(*refs):
        a_ref, b_ref = refs[0], refs[1]
        add_ref = refs[2] if add is not None else None
        n_in = 2 + (add is not None) + (after is not None)
        o_ref = refs[n_in]
        scratch = refs[n_in + 1:]

        def finish(res):
            if add is not None:
                res = res + add_scale * add_ref[...]
            o_ref[...] = res.astype(out_dtype)

        if cache_a:
            a_bf = scratch[0]

            @pl.when(pl.program_id(1) == 0)
            def _():
                a_bf[...] = a_ref[...].astype(bf16)

            a_val = a_bf[...]
        else:
            a_val = a_ref[...].astype(bf16)
        prod = lax.dot_general(a_val, b_ref[...].astype(bf16), _DIMS[mode], preferred_element_type=f32)
        if nk == 1:
            finish(prod)
        else:
            acc = scratch[-1]
            kk = pl.program_id(2)

            @pl.when(kk == 0)
            def _():
                acc[...] = prod

            @pl.when(kk != 0)
            def _():
                acc[...] += prod

            @pl.when(kk == nk - 1)
            def _():
                finish(acc[...])

    a_shape = (tk, tm) if mode == "tn" else (tm, tk)
    a_spec = pl.BlockSpec(a_shape, (lambda i, j, kk: (kk, i)) if mode == "tn" else (lambda i, j, kk: (i, kk)))
    b_spec = pl.BlockSpec((tn, tk), lambda i, j, kk: (j, kk)) if mode == "nt" else pl.BlockSpec((tk, tn), lambda i, j, kk: (kk, j))
    o_spec = pl.BlockSpec((tm, tn), lambda i, j, kk: (i, j))
    in_specs = [a_spec, b_spec] + ([o_spec] if add is not None else []) + ([pl.BlockSpec(memory_space=pl.ANY)] if after is not None else [])
    args = [a, b] + ([add] if add is not None else []) + ([after] if after is not None else [])
    scratch_shapes = ([pltpu.VMEM(a_shape, bf16)] if cache_a else []) + ([pltpu.VMEM((tm, tn), f32)] if nk > 1 else [])
    return pl.pallas_call(
        body, grid=(m // tm, n // tn, nk), in_specs=in_specs, out_specs=o_spec,
        out_shape=jax.ShapeDtypeStruct((m, n), out_dtype), scratch_shapes=scratch_shapes,
        compiler_params=pltpu.CompilerParams(dimension_semantics=("parallel", "arbitrary", "arbitrary")),
        name=name,
    )(*args)


def _position():
    return lax.axis_index("x"), lax.axis_index("y"), lax.axis_index("c")


def _flip(pos, k):
    x, y, c = pos
    return (1 - x if k & 4 else x, 1 - y if k & 2 else y, 1 - c if k & 1 else c)


def _index(pos):
    return 4 * pos[0] + 2 * pos[1] + pos[2]


def _all_gather_rows(xs, name):
    n_arr = len(xs)
    chips = (2, 4, 6)

    def body(*refs):
        x_refs, out_refs = refs[:n_arr], refs[n_arr:2 * n_arr]
        send_sems, recv_sems, local_sems = refs[2 * n_arr:]
        me = _position()
        sibling = _flip(me, 1)

        def copy(i, sem, block, to, own=False):
            m_per = x_refs[i].shape[0]
            rows = out_refs[i].at[pl.ds(_index(block) * m_per, m_per), :]
            return pltpu.make_async_remote_copy(
                src_ref=x_refs[i] if own else rows, dst_ref=rows,
                send_sem=send_sems.at[7 * i + sem], recv_sem=recv_sems.at[7 * i + sem], device_id=to, device_id_type=MESH)

        mine = [pltpu.make_async_copy(x_refs[i], out_refs[i].at[pl.ds(_index(me) * x_refs[i].shape[0], x_refs[i].shape[0]), :],
                                      local_sems.at[i]) for i in range(n_arr)]
        first, passed = [], []
        for i in range(n_arr):
            first.append(copy(i, 0, me, sibling, own=True))
            first += [copy(i, 1 + j, me, _flip(me, k), own=True) for j, k in enumerate(chips)]
            passed.append([copy(i, 4 + j, _flip(me, k), sibling) for j, k in enumerate(chips)])
        for cp in mine + first:
            cp.start()
        for i in range(n_arr):
            for j, k in enumerate(chips):
                copy(i, 1 + j, _flip(me, k), me).wait_recv()
                passed[i][j].start()
        for i in range(n_arr):
            copy(i, 0, sibling, me).wait_recv()
            for j, k in enumerate(chips):
                copy(i, 4 + j, _flip(sibling, k), me).wait_recv()
        for cp in first + [cp for group in passed for cp in group]:
            cp.wait_send()
        for cp in mine:
            cp.wait()

    anyspec = pl.BlockSpec(memory_space=pl.ANY)
    return pl.pallas_call(
        body, out_shape=[jax.ShapeDtypeStruct((N_DEV * x.shape[0], x.shape[1]), x.dtype) for x in xs],
        in_specs=[anyspec] * n_arr, out_specs=[anyspec] * n_arr,
        scratch_shapes=[pltpu.SemaphoreType.DMA((7 * n_arr,)), pltpu.SemaphoreType.DMA((7 * n_arr,)),
                        pltpu.SemaphoreType.DMA((n_arr,))],
        name=name,
    )(*xs)


def _split_start(srcs, lands, plan, n_copies, name, after=()):
    n_arr = len(srcs)
    n_after = len(after)
    hbm = pl.BlockSpec(memory_space=pltpu.HBM)
    sem = pl.BlockSpec(memory_space=pltpu.SEMAPHORE)

    def body(*refs):
        src_refs, land_refs = refs[:n_arr], refs[n_arr:2 * n_arr]
        outs_at = 2 * n_arr + n_after
        send_sems, recv_sems = refs[outs_at:outs_at + n_arr], refs[outs_at + n_arr:outs_at + 2 * n_arr]
        token = refs[-1]
        me = _position()
        for i in range(n_arr):
            for j, (src, dst, peer, _) in enumerate(plan(i, src_refs[i], land_refs[i], me)):
                pltpu.make_async_remote_copy(src_ref=src, dst_ref=dst, send_sem=send_sems[i].at[j], recv_sem=recv_sems[i].at[j],
                                             device_id=peer, device_id_type=MESH).start()
        token[...] = jnp.zeros_like(token)

    outs = pl.pallas_call(
        body, name=name,
        out_shape=([pltpu.SemaphoreType.DMA((n_copies,))] * (2 * n_arr)
                   + [pltpu.HBM(a.shape, a.dtype) for a in list(srcs) + list(lands)]
                   + [jax.ShapeDtypeStruct((8, LANES), f32)]),
        in_specs=[hbm] * (2 * n_arr) + [pl.BlockSpec(memory_space=pl.ANY)] * n_after,
        out_specs=[sem] * (2 * n_arr) + [hbm] * (2 * n_arr) + [pl.BlockSpec(memory_space=pltpu.VMEM)],
        input_output_aliases={i: 2 * n_arr + i for i in range(2 * n_arr)},
        compiler_params=pltpu.CompilerParams(has_side_effects=pltpu.SideEffectType.DATAFLOW_SIDE_EFFECTING),
    )(*[pltpu.with_memory_space_constraint(a, pltpu.HBM) for a in list(srcs) + list(lands)], *after)
    return (outs[:n_arr], outs[n_arr:2 * n_arr], outs[2 * n_arr:3 * n_arr], outs[3 * n_arr:4 * n_arr], outs[-1])


def _split_wait(started, plan, after, name):
    send_sems, recv_sems, srcs, lands, _ = started
    n_arr = len(srcs)
    hbm = pl.BlockSpec(memory_space=pltpu.HBM)
    sem = pl.BlockSpec(memory_space=pltpu.SEMAPHORE)

    def body(*refs):
        src_refs, land_refs = refs[:n_arr], refs[n_arr:2 * n_arr]
        s_sems, r_sems = refs[2 * n_arr:3 * n_arr], refs[3 * n_arr:4 * n_arr]
        me = _position()
        for i in range(n_arr):
            for j, (src, _, peer, arrival) in enumerate(plan(i, src_refs[i], land_refs[i], me)):
                cp = pltpu.make_async_remote_copy(src_ref=src, dst_ref=arrival, send_sem=s_sems[i].at[j], recv_sem=r_sems[i].at[j],
                                                  device_id=peer, device_id_type=MESH)
                cp.wait_send()
                cp.wait_recv()

    outs = pl.pallas_call(
        body, name=name,
        out_shape=[pltpu.HBM(a.shape, a.dtype) for a in list(srcs) + list(lands)],
        in_specs=[hbm] * (2 * n_arr) + [sem] * (2 * n_arr) + [pl.BlockSpec(memory_space=pl.ANY)],
        out_specs=[hbm] * (2 * n_arr),
        input_output_aliases={i: i for i in range(2 * n_arr)},
        compiler_params=pltpu.CompilerParams(has_side_effects=pltpu.SideEffectType.DATAFLOW_SIDE_EFFECTING),
    )(*srcs, *lands, *send_sems, *recv_sems, after)
    return outs[:n_arr], outs[n_arr:]


_GATHER_FLIPS = (1, 2, 4, 6)


def _gather_plan(i, src_ref, land_ref, me):
    m = src_ref.shape[0]

    def rows(pos):
        return land_ref.at[pl.ds(_index(pos) * m, m), :]

    return [(src_ref, rows(me), _flip(me, k), rows(_flip(me, k))) for k in _GATHER_FLIPS]


def _gather_forward(lands, name):
    n_arr = len(lands)
    chips = (2, 4, 6)

    def body(*refs):
        out_refs = refs[n_arr:2 * n_arr]
        send_sems, recv_sems = refs[2 * n_arr:]
        me = _position()
        sibling = _flip(me, 1)
        sends, arrivals = [], []
        for i, out_ref in enumerate(out_refs):
            m = out_ref.shape[0] // N_DEV

            def copy(pos, j):
                blk = out_ref.at[pl.ds(_index(pos) * m, m), :]
                return pltpu.make_async_remote_copy(src_ref=blk, dst_ref=blk, send_sem=send_sems.at[3 * i + j],
                                                    recv_sem=recv_sems.at[3 * i + j], device_id=sibling, device_id_type=MESH)

            for j, k in enumerate(chips):
                sends.append(copy(_flip(me, k), j))
                arrivals.append(copy(_flip(sibling, k), j))
        for cp in sends:
            cp.start()
        for cp in arrivals:
            cp.wait_recv()
        for cp in sends:
            cp.wait_send()

    anyspec = pl.BlockSpec(memory_space=pl.ANY)
    return pl.pallas_call(
        body, out_shape=[jax.ShapeDtypeStruct(a.shape, a.dtype) for a in lands],
        in_specs=[anyspec] * n_arr, out_specs=[anyspec] * n_arr, input_output_aliases={i: i for i in range(n_arr)},
        scratch_shapes=[pltpu.SemaphoreType.DMA((3 * n_arr,))] * 2, name=name,
    )(*lands)


def _chips_plan(i, src_ref, land_ref, me):
    m = src_ref.shape[0] // 4
    plan = []
    for j, k in enumerate((2, 4, 6)):
        peer = _flip(me, k)
        plan.append((src_ref.at[pl.ds((2 * peer[0] + peer[1]) * m, m), :], land_ref.at[j], peer, land_ref.at[j]))
    return plan


def _sibling_plan(i, src_ref, land_ref, me):
    m = src_ref.shape[0] // N_DEV
    sibling = _flip(me, 1)
    return [(src_ref.at[pl.ds((2 * q + 1 - me[2]) * m, m), :], land_ref.at[q], sibling, land_ref.at[q]) for q in range(4)]


def _sum_with_sibling(g, recv, name):
    m = g.shape[0] // N_DEV
    n = g.shape[1]
    tr = _pick(m, (208, 128, 64, 32, 16))
    nt = m // tr

    def body(g_ref, r_ref, o_ref):
        c = lax.axis_index("c")
        own = jnp.where(c == 0, g_ref[0, 0].astype(f32), g_ref[0, 1].astype(f32))
        o_ref[...] = (own + r_ref[0].astype(f32)).astype(o_ref.dtype)

    return pl.pallas_call(
        body, grid=(4, nt),
        in_specs=[pl.BlockSpec((1, 2, tr, n), lambda q, i: (q, 0, i, 0)), pl.BlockSpec((1, tr, n), lambda q, i: (q, i, 0))],
        out_specs=pl.BlockSpec((tr, n), lambda q, i: (q * nt + i, 0)),
        out_shape=jax.ShapeDtypeStruct((4 * m, n), bf16), name=name,
    )(g.reshape(4, 2, m, n), recv)


def _sum_with_chips(h, recv, name):
    m = h.shape[0] // 4
    n = h.shape[1]
    tr = _pick(m, (208, 128, 64, 32, 16))

    def body(h_ref, r_ref, o_ref):
        my_q = 2 * lax.axis_index("x") + lax.axis_index("y")
        own = h_ref[0].astype(f32)
        for q in range(1, 4):
            own = jnp.where(my_q == q, h_ref[q].astype(f32), own)
        o_ref[...] = ((own + r_ref[0].astype(f32)) + r_ref[1].astype(f32)) + r_ref[2].astype(f32)

    return pl.pallas_call(
        body, grid=(m // tr,),
        in_specs=[pl.BlockSpec((4, tr, n), lambda i: (0, i, 0)), pl.BlockSpec((3, tr, n), lambda i: (0, i, 0))],
        out_specs=pl.BlockSpec((tr, n), lambda i: (i, 0)), out_shape=jax.ShapeDtypeStruct((m, n), f32), name=name,
    )(h.reshape(4, m, n), recv)


def _sum_slots(parts, name):
    n_slot, m, n = parts.shape
    tr = _pick(m, (208, 128, 64, 32, 16, 8))

    def body(p_ref, o_ref):
        acc = p_ref[0]
        for s in range(1, n_slot):
            acc = acc + p_ref[s]
        o_ref[...] = acc

    return pl.pallas_call(
        body, grid=(m // tr,), in_specs=[pl.BlockSpec((n_slot, tr, n), lambda i: (0, i, 0))],
        out_specs=pl.BlockSpec((tr, n), lambda i: (i, 0)), out_shape=jax.ShapeDtypeStruct((m, n), parts.dtype), name=name,
    )(parts)


def _reduce_scatter_begin(gs, name):
    lands = [lax.empty((4, g.shape[0] // N_DEV, g.shape[1]), g.dtype) for g in gs]
    return _split_start(gs, lands, _sibling_plan, 4, "rs_d2d_start_" + name)


def _reduce_scatter_middle(started, after, name):
    gs, from_sibling = _split_wait(started, _sibling_plan, after, "rs_d2d_wait_" + name)
    chip_sums = [_sum_with_sibling(g, r, f"rs_sum2_{name}_{i}") for i, (g, r) in enumerate(zip(gs, from_sibling))]
    lands = [lax.empty((3, h.shape[0] // 4, h.shape[1]), h.dtype) for h in chip_sums]
    return _split_start(chip_sums, lands, _chips_plan, 3, "rs_ici_start_" + name)


def _reduce_scatter_end(started, after, name):
    chip_sums, from_chips = _split_wait(started, _chips_plan, after, "rs_ici_wait_" + name)
    return [_sum_with_chips(h, r, f"rs_sum4_{name}_{i}") for i, (h, r) in enumerate(zip(chip_sums, from_chips))]


def _adamw_update(w, g, m, v):
    mm = ADAM_B1 * m + (1.0 - ADAM_B1) * g
    vv = ADAM_B2 * v + (1.0 - ADAM_B2) * jnp.square(g)
    m_hat = mm / (1.0 - ADAM_B1 ** ADAM_STEP)
    v_hat = vv / (1.0 - ADAM_B2 ** ADAM_STEP)
    return -ADAM_LR * (m_hat / (jnp.sqrt(v_hat) + ADAM_EPS) + ADAM_WD * w), mm, vv


def _adamw_many(ws, gs, ms, vs, name):
    k = len(ws)
    shapes = [w.shape for w in ws]
    flat = [[a.reshape(-1, a.shape[-1]) for a in group] for group in (ws, gs, ms, vs)]

    def body(*refs):
        for i in range(k):
            d, mm, vv = _adamw_update(*(refs[j * k + i][...] for j in range(4)))
            refs[4 * k + i][...] = d
            refs[5 * k + i][...] = mm
            refs[6 * k + i][...] = vv

    outs = pl.pallas_call(
        body, out_shape=[jax.ShapeDtypeStruct(a.shape, f32) for a in flat[0]] * 3, name=name,
    )(*flat[0], *flat[1], *flat[2], *flat[3])
    return tuple([outs[j * k + i].reshape(shapes[i]) for i in range(k)] for j in range(3))


def _adamw(w, g, m, v, name):
    shape = w.shape
    n = shape[-1]
    r = w.size // n
    w2, g2, m2, v2 = (a.reshape(r, n) for a in (w, g, m, v))
    tr = _pick(r, (256, 208, 128, 64, 32, 16, 8))

    def body(w_ref, g_ref, m_ref, v_ref, d_ref, mo_ref, vo_ref):
        d_ref[...], mo_ref[...], vo_ref[...] = _adamw_update(w_ref[...], g_ref[...], m_ref[...], v_ref[...])

    spec = pl.BlockSpec((tr, n), lambda i: (i, 0))
    outs = pl.pallas_call(
        body, grid=(r // tr,), in_specs=[spec] * 4, out_specs=[spec] * 3,
        out_shape=[jax.ShapeDtypeStruct((r, n), f32)] * 3, name=name,
    )(w2, g2, m2, v2)
    return tuple(o.reshape(shape) for o in outs)


_SMALL = ("shift_mu", "w_decay0", "a0", "k_k", "k_a", "r_k", "ln_x_w", "ln_x_b", "v_mix0", "lb_logits",
          "g_norm_w", "ln_w", "ln_b")
_NAMES = ("w_in", "shift_mu", "w_decay0", "w_decay_up", "a0", "a_up", "k_k", "k_a", "r_k", "ln_x_w", "ln_x_b",
          "v_mix0", "v_mix_down", "v_mix_up", "lb_logits", "g_norm_w", "w_out", "ln_w", "ln_b")


def _pad_rows(a, rows, at_end):
    z = jnp.zeros((rows - a.shape[0], a.shape[1]), a.dtype)
    return jnp.concatenate([a, z] if at_end else [z, a], axis=0)


def kernel(x, w_in, shift_mu, w_decay0, w_decay_up, a0, a_up, k_k, k_a, r_k, ln_x_w, ln_x_b, v_mix0, v_mix_down, v_mix_up, lb_logits, g_norm_w, w_out, ln_w, ln_b, loss_target, m_w_in, m_shift_mu, m_w_decay0, m_w_decay_up, m_a0, m_a_up, m_k_k, m_k_a, m_r_k, m_ln_x_w, m_ln_x_b, m_v_mix0, m_v_mix_down, m_v_mix_up, m_lb_logits, m_g_norm_w, m_w_out, m_ln_w, m_ln_b, v_w_in, v_shift_mu, v_w_decay0, v_w_decay_up, v_a0, v_a_up, v_k_k, v_k_a, v_r_k, v_ln_x_w, v_ln_x_b, v_v_mix0, v_v_mix_down, v_v_mix_up, v_lb_logits, v_g_norm_w, v_w_out, v_ln_w, v_ln_b):
    weights = dict(w_in=w_in, shift_mu=shift_mu, w_decay0=w_decay0, w_decay_up=w_decay_up, a0=a0, a_up=a_up, k_k=k_k,
                   k_a=k_a, r_k=r_k, ln_x_w=ln_x_w, ln_x_b=ln_x_b, v_mix0=v_mix0, v_mix_down=v_mix_down,
                   v_mix_up=v_mix_up, lb_logits=lb_logits, g_norm_w=g_norm_w, w_out=w_out, ln_w=ln_w, ln_b=ln_b)
    mom1 = dict(w_in=m_w_in, shift_mu=m_shift_mu, w_decay0=m_w_decay0, w_decay_up=m_w_decay_up, a0=m_a0, a_up=m_a_up,
                k_k=m_k_k, k_a=m_k_a, r_k=m_r_k, ln_x_w=m_ln_x_w, ln_x_b=m_ln_x_b, v_mix0=m_v_mix0,
                v_mix_down=m_v_mix_down, v_mix_up=m_v_mix_up, lb_logits=m_lb_logits, g_norm_w=m_g_norm_w,
                w_out=m_w_out, ln_w=m_ln_w, ln_b=m_ln_b)
    mom2 = dict(w_in=v_w_in, shift_mu=v_shift_mu, w_decay0=v_w_decay0, w_decay_up=v_w_decay_up, a0=v_a0, a_up=v_a_up,
                k_k=v_k_k, k_a=v_k_a, r_k=v_r_k, ln_x_w=v_ln_x_w, ln_x_b=v_ln_x_b, v_mix0=v_v_mix0,
                v_mix_down=v_v_mix_down, v_mix_up=v_v_mix_up, lb_logits=v_lb_logits, g_norm_w=v_g_norm_w,
                w_out=v_w_out, ln_w=v_ln_w, ln_b=v_ln_b)
    assert x.shape[0] == 1 and w_in.shape[0] == DEPTH
    t, d = x.shape[1], x.shape[2]
    dr = w_decay0.shape[1]
    dh = g_norm_w.shape[1]
    rank_w, rank_a, rank_v = w_decay_up.shape[1], a_up.shape[1], v_mix_up.shape[1]
    rwc = 4 * dr + rank_w + rank_a
    assert rank_w + rank_a == LANES and rank_v <= LANES and dr + dh == d
    assert t % CHUNK == 0 and dr % LANES == 0 and dh % LANES == 0 and shift_mu.shape[1] == rwc
    n_pair = dr // LANES
    me = _index(_position())

    shard = dr // N_DEV
    pack = jnp.concatenate([w_decay_up[0], w_decay_up[1], a_up[0], a_up[1], v_mix_up[0], v_mix_down[0].T], axis=0)
    win_t0, pack = _all_gather_rows([w_in[0].T.astype(bf16), pack], "ag_first")
    win_t = [win_t0, None]
    wout = [None, None]
    late_blocks = [w_out[0].astype(bf16), w_in[1].T.astype(bf16), w_out[1].astype(bf16)]
    late_lands = [lax.dynamic_update_slice(lax.empty((N_DEV * blk.shape[0], blk.shape[1]), bf16), blk, (me * blk.shape[0], 0))
                  for blk in late_blocks]
    late_gather = _split_start(late_blocks, late_lands, _gather_plan, len(_GATHER_FLIPS), "ag_late_start",
                               after=(win_t[0], pack))
    pack = jnp.transpose(pack.reshape(N_DEV, -1, shard), (1, 0, 2)).reshape(-1, dr)
    offs = [0, rank_w, 2 * rank_w, 2 * rank_w + rank_a, 2 * rank_w + 2 * rank_a, 2 * rank_w + 2 * rank_a + rank_v,
            2 * rank_w + 2 * rank_a + 2 * rank_v]
    wdu_f = [pack[offs[0]:offs[1]], pack[offs[1]:offs[2]]]
    aup_f = [pack[offs[2]:offs[3]], pack[offs[3]:offs[4]]]
    vup_f = pack[offs[4]:offs[5]]
    vdown_f = pack[offs[5]:offs[6]].T

    def after_start(a, started):
        return a + started[-1][0:1, 0:1]

    def rwkv_params(l):
        mu = after_start(shift_mu[0:1], late_gather) if l == 0 else shift_mu[l:l + 1]
        prm = [mu, w_decay0[l:l + 1], a0[l:l + 1], _pad_rows(wdu_f[l], LANES, True),
               _pad_rows(aup_f[l], LANES, False)]
        if l == 1:
            prm += [v_mix0[0:1], _pad_rows(vdown_f.T, LANES, True).T, _pad_rows(vup_f, LANES, True)]
        rows = jnp.stack([k_k[l], k_a[l], r_k[l], ln_x_w[l], ln_x_b[l]] + [jnp.zeros((dr,), f32)] * 3, axis=0)
        pp = jnp.transpose(rows.reshape(8, n_pair, LANES), (1, 0, 2))
        return tuple(prm), pp

    h = x[0]
    h16 = h.astype(bf16)
    tgt = loss_target[0]
    saved = []
    vfirst = None
    for l in range(DEPTH):
        prm, pp = rwkv_params(l)
        proj = _matmul(h16, win_t[l], "nt", f"mm_proj_{l}", _MM_TILES["proj"])
        if l == 0:
            cat, vfirst, mck = _rwkv_fwd(False, proj, None, prm, pp, d)
        else:
            cat, mck = _rwkv_fwd(True, proj, vfirst, prm, pp, d)
        cat, sck = _hgrn_fwd(l == 1, proj, lb_logits, g_norm_w[l:l + 1], cat, rwc)
        if l == 0:
            _, arrived = _split_wait(late_gather, _gather_plan, cat, "ag_late_wait")
            wout[0], win_t[1], wout[1] = _gather_forward(arrived, "ag_late_forward")
        y = _matmul(cat, wout[l], "nn", f"mm_out_{l}", _MM_TILES["out"])
        saved.append((h, h16, proj, prm, pp, mck, sck, cat, y))
        if l < DEPTH - 1:
            h, h16 = _ln_fwd(h, y, ln_w[l:l + 1], ln_b[l:l + 1])
        else:
            top = _ln_loss_bwd(h, y, ln_w[l:l + 1], ln_b[l:l + 1], tgt)
    loss = lax.psum(top[4][0, 0], ("x", "y", "c"))

    grads = {}
    big = {}
    dvfirst = None
    d_lbl = None
    rs_started = {}
    for l in reversed(range(DEPTH)):
        h_l, h16_l, proj, prm, pp, mck, sck, cat, y = saved[l]
        if l == DEPTH - 1:
            dy, dy16, g_ln_w, g_ln_b = top[:4]
        else:
            dy, dy16, g_ln_w, g_ln_b = _ln_bwd(h_l, y, after_start(ln_w[l:l + 1], rs_started[l + 1]), ln_b[l:l + 1], dh_out)
        dcat = _matmul(dy16, wout[l], "nt", f"mm_dcat_{l}", _MM_TILES["dcat"])
        big[("w_out", l)] = _matmul(cat, dy16, "tn", f"mm_dwout_{l}", _MM_TILES["dwout"], out_dtype=bf16)
        if l == 1:
            outs = _rwkv_bwd(True, proj, vfirst, prm, pp, mck, dcat, None)
            dproj_r, dvfirst = outs[0], outs[1]
            dprm, dpp = outs[2:-1], outs[-1]
        else:
            outs = _rwkv_bwd(False, proj, None, prm, pp, mck, dcat, dvfirst)
            dproj_r = outs[0]
            dprm, dpp = outs[1:-1], outs[-1]
        dproj, dlbl_l, dgnw = _hgrn_bwd(l == 1, proj, lb_logits, g_norm_w[l:l + 1], sck, dcat, rwc, dproj_r)
        big[("w_in", l)] = _matmul(dproj, h16_l, "tn", f"mm_dwin_{l}", _MM_TILES["dwin"], out_dtype=bf16)
        sharded = [dprm[3][:rank_w].T, dprm[4][rank_w:].T]
        if l == 1:
            sharded += [dprm[6][:, :rank_v], dprm[7][:rank_v].T,
                        jnp.zeros((dr, LANES - 2 * rank_v), f32)]
        sharded = jnp.concatenate(sharded, axis=1).astype(bf16)
        d2d = _reduce_scatter_begin([big[("w_in", l)], big[("w_out", l)], sharded], f"l{l}")
        if l == 0:
            rs_started[l] = _reduce_scatter_middle(d2d, sharded, f"l{l}")
            token = rs_started[l][-1]
        else:
            token = d2d[-1]
        dh_out = _matmul(dproj, win_t[l], "nn", f"mm_dh_{l}", _MM_TILES["dh"], add=dy, add_scale=ALPHA, after=token)
        if l > 0:
            rs_started[l] = _reduce_scatter_middle(d2d, dh_out, f"l{l}")
        dpp = jnp.transpose(dpp, (1, 0, 2)).reshape(8, dr)
        grads[l] = dict(shift_mu=dprm[0][0], w_decay0=dprm[1][0], a0=dprm[2][0],
                        k_k=dpp[0], k_a=dpp[1], r_k=dpp[2], ln_x_w=dpp[3], ln_x_b=dpp[4],
                        g_norm_w=dgnw[0], ln_w=g_ln_w[0], ln_b=g_ln_b[0])
        if l == 1:
            grads[l].update(v_mix0=dprm[5][0])
            d_lbl = dlbl_l
    grad_x = dh_out[None]

    def both(name):
        return jnp.stack([grads[0][name], grads[1][name]])

    small = dict(shift_mu=both("shift_mu"), w_decay0=both("w_decay0"), a0=both("a0"), k_k=both("k_k"), k_a=both("k_a"),
                 r_k=both("r_k"), ln_x_w=both("ln_x_w"), ln_x_b=both("ln_x_b"), v_mix0=grads[1]["v_mix0"][None],
                 lb_logits=d_lbl, g_norm_w=both("g_norm_w"), ln_w=both("ln_w"), ln_b=both("ln_b"))
    flat = jnp.concatenate([small[nm].reshape(-1) for nm in _SMALL])
    n_flat = flat.shape[0]
    rows = -(-n_flat // (8 * LANES)) * 8
    flat = jnp.concatenate([flat, jnp.zeros((rows * LANES - n_flat,), f32)]).reshape(rows, LANES)
    total = _sum_slots(_all_gather_rows([flat], "ag_small_grads")[0].reshape(N_DEV, rows, LANES), "sum_small_grads").reshape(-1)
    gsm = {}
    off = 0
    for nm in _SMALL:
        size = small[nm].size
        gsm[nm] = total[off:off + size].reshape(small[nm].shape)
        off += size
    reduced = {1: _reduce_scatter_end(rs_started[1], dh_out, "l1")}
    reduced[0] = _reduce_scatter_end(rs_started[0], total, "l0")
    g_w_in_t = jnp.stack([reduced[l][0] for l in range(DEPTH)])
    gsm["w_in"] = jnp.transpose(g_w_in_t, (0, 2, 1))
    gsm["w_out"] = jnp.stack([reduced[l][1] for l in range(DEPTH)])
    gsm["w_decay_up"] = jnp.stack([reduced[l][2][:, :rank_w].T for l in range(DEPTH)])
    gsm["a_up"] = jnp.stack([reduced[l][2][:, rank_w:rank_w + rank_a].T for l in range(DEPTH)])
    gsm["v_mix_down"] = reduced[1][2][:, LANES:LANES + rank_v][None]
    gsm["v_mix_up"] = reduced[1][2][:, LANES + rank_v:LANES + 2 * rank_v].T[None]

    deltas, new_m, new_v = {}, {}, {}
    swap = lambda a: jnp.transpose(a, (0, 2, 1))
    deltas["w_in"], new_m["w_in"], new_v["w_in"] = (
        swap(a) for a in _adamw(swap(w_in), g_w_in_t, swap(m_w_in), swap(v_w_in), "adamw_w_in"))
    deltas["w_out"], new_m["w_out"], new_v["w_out"] = _adamw(w_out, gsm["w_out"], m_w_out, v_w_out, "adamw_w_out")
    rest = [nm for nm in _NAMES if nm not in ("w_in", "w_out")]
    d_rest, m_rest, v_rest = _adamw_many([weights[nm] for nm in rest], [gsm[nm] for nm in rest],
                                         [mom1[nm] for nm in rest], [mom2[nm] for nm in rest], "adamw_small")
    for i, nm in enumerate(rest):
        deltas[nm], new_m[nm], new_v[nm] = d_rest[i], m_rest[i], v_rest[i]
    return (loss, grad_x, *[gsm[nm] for nm in _NAMES], *[deltas[nm] for nm in _NAMES],
            *[new_m[nm] for nm in _NAMES], *[new_v[nm] for nm in _NAMES])
```

```python
import functools

import jax
import jax.numpy as jnp
from jax import lax
from jax.experimental import pallas as pl
from jax.experimental.pallas import tpu as pltpu

f32 = jnp.float32
bf16 = jnp.bfloat16

N_DEV = 8
CHUNK = 64
LANES = 128
RWKV_HEAD = 64
DEPTH = 2
ALPHA = (2 * DEPTH) ** 0.25
LN_EPS = 1e-5
GN_EPS = 64e-5
RMS_EPS = 1e-5
LB_FLOOR = 1e-30
ADAM_LR, ADAM_B1, ADAM_B2, ADAM_EPS, ADAM_WD, ADAM_STEP = 0.001, 0.9, 0.999, 1e-08, 0.01, 10
MESH = pl.DeviceIdType.MESH


def _iota(shape, d):
    return lax.broadcasted_iota(jnp.int32, shape, d)


_DIMS = {"nn": (((1,), (0,)), ((), ())), "nt": (((1,), (1,)), ((), ())), "tn": (((0,), (0,)), ((), ()))}
_BATCH_DIMS = {"nn": (((2,), (1,)), ((0,), (0,))), "nt": (((2,), (2,)), ((0,), (0,))), "tn": (((1,), (1,)), ((0,), (0,)))}
_K_AXES = {"nn": (-1, -2), "nt": (-1, -1), "tn": (-2, -2)}


def _mxu(a, b, mode):
    return lax.dot_general(a, b, (_BATCH_DIMS if a.ndim == 3 else _DIMS)[mode], preferred_element_type=f32)


def _split(x):
    hi = x.astype(bf16)
    return hi, (x - hi.astype(f32)).astype(bf16)


def _mm2_impl(a, b, mode, passes=3):
    if passes == 1:
        return _mxu(a.astype(bf16), b.astype(bf16), mode)
    ah, al = _split(a)
    if passes == 3:
        bh, bl = _split(b)
        lhs, rhs = [ah, ah, al], [bh, bl, bh]
    else:
        bh = b.astype(bf16)
        lhs, rhs = [ah, al], [bh, bh]
    ka, kb = _K_AXES[mode]
    k = a.shape[ka]
    if k % (LANES if -1 in (ka, kb) else 16) == 0:
        return _mxu(jnp.concatenate(lhs, axis=ka), jnp.concatenate(rhs, axis=kb), mode)
    out = _mxu(lhs[0], rhs[0], mode)
    for x, y in zip(lhs[1:], rhs[1:]):
        out = out + _mxu(x, y, mode)
    return out


@functools.partial(jax.custom_vjp, nondiff_argnums=(2, 3))
def _mm2(a, b, mode, passes=3):
    return _mm2_impl(a, b, mode, passes)


def _mm2_fwd(a, b, mode, passes):
    return _mm2_impl(a, b, mode, passes), (a, b)


def _mm2_bwd(mode, passes, res, g):
    a, b = res
    if mode == "nn":
        return _mm2_impl(g, b, "nt", passes), _mm2_impl(a, g, "tn", passes)
    if mode == "nt":
        return _mm2_impl(g, b, "nn", passes), _mm2_impl(g, a, "tn", passes)
    return _mm2_impl(b, g, "nt", passes), _mm2_impl(a, g, "nn", passes)


_mm2.defvjp(_mm2_fwd, _mm2_bwd)

TRI_PASSES = 1
APPLY_PASSES = 1


def _const_impl(cm, x, mode):
    if mode in ("r", "rt"):
        shape = x.shape
        out = _mxu(x.astype(bf16).reshape(-1, shape[-1]), cm, "nn" if mode == "r" else "nt")
        return out.reshape(shape[:-1] + (out.shape[-1],))
    hi, lo = _split(x)
    if x.ndim == 3:
        cm = jnp.broadcast_to(cm, (x.shape[0],) + cm.shape)
    return _mxu(cm, hi, mode) + _mxu(cm, lo, mode)


@jax.custom_vjp
def _const_left(cm, x):
    return _const_impl(cm, x, "nn")


_const_left.defvjp(lambda cm, x: (_const_impl(cm, x, "nn"), cm),
                   lambda cm, g: (jnp.zeros_like(cm), _const_impl(cm, g, "tn")))


@jax.custom_vjp
def _const_right(x, cm):
    return _const_impl(cm, x, "r")


_const_right.defvjp(lambda x, cm: (_const_impl(cm, x, "r"), cm),
                    lambda cm, g: (_const_impl(cm, g, "rt"), jnp.zeros_like(cm)))


def _tri_inv(a):
    n = a.shape[-1]
    tm = (_iota((n, n), 0) == _iota((n, n), 1)).astype(f32) + a
    ak = a
    for _ in range(5):
        ak = _mm2_impl(ak, ak, "nn", TRI_PASSES)
        tm = tm + _mm2_impl(tm, ak, "nn", TRI_PASSES)
    return tm


@jax.custom_vjp
def _tri_solve(tm, a, x):
    del a
    return _mm2_impl(tm, x, "nn")


def _tri_solve_fwd(tm, a, x):
    u = _mm2_impl(tm, x, "nn")
    return u, (tm, u)


def _tri_solve_bwd(res, du):
    tm, u = res
    dx = _mm2_impl(tm, du, "tn")
    return jnp.zeros_like(tm), _mm2_impl(dx, u, "nt"), dx


_tri_solve.defvjp(_tri_solve_fwd, _tri_solve_bwd)


def _col_of_row(row_vec):
    n = row_vec.shape[-1]
    eye = _iota((n, n), 0) == _iota((n, n), 1)
    return jnp.sum(jnp.where(eye, jnp.broadcast_to(row_vec, row_vec.shape[:-2] + (n, n)), 0.0), axis=-1, keepdims=True)


def _softplus(x):
    return jnp.maximum(x, 0.0) + jnp.log1p(jnp.exp(-jnp.abs(x)))


def _log_sigmoid(x):
    return -_softplus(-x)


def _logaddexp(a, b):
    return jnp.maximum(a, b) + jnp.log1p(jnp.exp(-jnp.abs(a - b)))


def _silu(x):
    return x * jax.nn.sigmoid(x)


def _tril(c, strict):
    r, s = _iota((c, c), 0), _iota((c, c), 1)
    return (r > s) if strict else (r >= s)


def _last_row(a):
    c = a.shape[-2]
    return jnp.sum(jnp.where(_iota(a.shape, a.ndim - 2) == c - 1, a, 0.0), axis=-2, keepdims=True)


def _rwkv_pre(layer1, prm, y, prev, vf):
    c = y.shape[0]
    if layer1:
        mu, w0, a0, wup, aup, v0, vdown, vup = prm
    else:
        mu, w0, a0, wup, aup = prm
    dr = w0.shape[1]
    shift = (_iota((c, c), 0) == _iota((c, c), 1) + 1).astype(bf16)
    y_prev = _const_left(shift, y) + jnp.where(_iota((c, 1), 0) == 0, prev, 0.0)
    rw = y + mu * (y_prev - y)
    r, k, v, z = (rw[:, i * dr:(i + 1) * dr] for i in range(4))
    wdad = rw[:, 4 * dr:4 * dr + LANES]
    w_raw = w0 + _mm2(jnp.tanh(wdad), wup, "nn")
    lw = -jnp.exp(-_softplus(-w_raw) - 0.5)
    asig = jax.nn.sigmoid(a0 + _mm2(wdad, aup, "nn"))
    if layer1:
        v = v + (vf - v) * jax.nn.sigmoid(v0 + _mm2(_mm2(v, vdown, "nn"), vup, "nn"))
    return r, k, v, z, lw, asig


def _rwkv_pair(pp, m0, xs, tm=None):
    kkw, kaw, rkw, gnw, gnb = pp
    r, k, v, z, lw, asig = xs
    c = r.shape[-2]
    n2 = 2 * c
    lane = _iota((1, LANES), 1)
    mh0, mh1 = (lane < RWKV_HEAD).astype(f32), (lane >= RWKV_HEAD).astype(f32)
    same_head = _iota((LANES, LANES), 0) // RWKV_HEAD == _iota((LANES, LANES), 1) // RWKV_HEAD
    g = same_head.astype(bf16)

    def seg(x):
        return _const_right(x, g)

    def stack(x):
        return jnp.concatenate([x * mh0, x * mh1], axis=-2)

    kk = k * kkw
    kk = kk / jnp.maximum(jnp.sqrt(seg(kk * kk)), 1e-12)
    k2 = k * (1.0 + (asig - 1.0) * kaw)
    a = -kk
    b = kk * asig
    cum = _const_left(_tril(c, False).astype(bf16), lw)
    at = stack(a * jnp.exp(cum - lw))
    rt = stack(r * jnp.exp(cum))
    en = jnp.exp(-cum)
    sc = _mm2(jnp.concatenate([at, rt], axis=-2), jnp.concatenate([stack(b * en), stack(k2 * en)], axis=-2), "nt")
    row, col = _iota((n2, n2), 0), _iota((n2, n2), 1)
    same = row // c == col // c
    strict = same & (row % c > col % c)
    incl = same & (row % c >= col % c)
    aab = jnp.where(strict, sc[..., :n2, :n2], 0.0)
    aak = jnp.where(strict, sc[..., :n2, n2:], 0.0)
    arb = jnp.where(incl, sc[..., n2:, :n2], 0.0)
    ark = jnp.where(incl, sc[..., n2:, n2:], 0.0)
    vv = jnp.concatenate([v, v], axis=-2)
    mask_st = jnp.concatenate([jnp.broadcast_to(mh0, (c, LANES)), jnp.broadcast_to(mh1, (c, LANES))], axis=0)
    x_st = _mm2(jnp.concatenate([at, aak], axis=-1), jnp.concatenate([m0, vv], axis=-2), "nn", APPLY_PASSES)
    if tm is None:
        tm = _tri_inv(lax.stop_gradient(aab))
    u_st = _tri_solve(tm, aab, x_st) * mask_st
    o_st = _mm2(jnp.concatenate([rt, arb, ark], axis=-1), jnp.concatenate([m0, u_st, vv], axis=-2), "nn", APPLY_PASSES) * mask_st
    u = u_st[..., :c, :] + u_st[..., c:, :]
    o = o_st[..., :c, :] + o_st[..., c:, :]
    cum_last = _last_row(cum)
    dec_end = jnp.exp(cum_last - cum)
    m_new = _col_of_row(jnp.exp(cum_last)) * m0 + _mm2(
        jnp.concatenate([b * dec_end, k2 * dec_end], axis=-2), jnp.concatenate([u, v], axis=-2), "tn", APPLY_PASSES) * same_head.astype(f32)
    mean = seg(o) * (1.0 / RWKV_HEAD)
    d = o - mean
    var = seg(d * d) * (1.0 / RWKV_HEAD)
    on = d * lax.rsqrt(var + GN_EPS) * gnw + gnb
    bonus = seg(r * k2 * rkw) * v
    return (on + bonus) * _silu(z), m_new, tm


def _split_lanes(a, n):
    return [a[:, i * LANES:(i + 1) * LANES] for i in range(n)]


def _rwkv_step(layer1, prm, y, prev, vf, pp, m0, tm=None):
    xs = _rwkv_pre(layer1, prm, y, prev, vf)
    n_pair = m0.shape[0]
    og, m_new, tm = _rwkv_pair(pp, m0, tuple(jnp.concatenate([p[None] for p in _split_lanes(a, n_pair)], axis=0) for a in xs), tm)
    return og, m_new, xs[2], tm


def _group(n):
    return n


def _rwkv_specs(layer1, t, dr, rwc, n_pair, rev):
    nc = t // CHUNK
    grp = _group(n_pair)

    def cidx(c):
        return (nc - 1 - c) if rev else c

    full = lambda shape: pl.BlockSpec(shape, lambda c, p: tuple(0 for _ in shape))
    specs = [
        pl.BlockSpec((CHUNK, rwc), lambda c, p: (cidx(c), 0)),
        pl.BlockSpec((8, rwc), lambda c, p: (jnp.maximum(cidx(c) * (CHUNK // 8) - 1, 0), 0)),
    ]
    if layer1:
        specs.append(pl.BlockSpec((CHUNK, dr), lambda c, p: (cidx(c), 0)))
    prm_shapes = [(1, rwc), (1, dr), (1, dr), (LANES, dr), (LANES, dr)]
    if layer1:
        prm_shapes += [(1, dr), (dr, LANES), (LANES, dr)]
    specs += [full(s) for s in prm_shapes]
    specs.append(pl.BlockSpec((grp, 8, LANES), lambda c, p: (p, 0, 0)))
    return specs, prm_shapes, cidx, full


def _rwkv_fwd(layer1, proj, vf, prm, pp, cat_width):
    t = proj.shape[0]
    dr = prm[1].shape[1]
    rwc = prm[0].shape[1]
    n_pair = dr // LANES
    nc = t // CHUNK
    n_prm = len(prm)
    specs, _, _, _ = _rwkv_specs(layer1, t, dr, rwc, n_pair, False)

    def body(*refs):
        y_ref, prev_ref = refs[0], refs[1]
        i = 2
        vf_ref = None
        if layer1:
            vf_ref = refs[i]
            i += 1
        prm_refs = refs[i:i + n_prm]
        i += n_prm
        pp_ref = refs[i]
        i += 1
        cat_ref = refs[i]
        i += 1
        vout_ref = None
        if not layer1:
            vout_ref = refs[i]
            i += 1
        mck_ref, m_s = refs[i], refs[i + 1]
        c = pl.program_id(0)

        @pl.when(c == 0)
        def _():
            m_s[...] = jnp.zeros_like(m_s)

        prev = prev_ref[pl.ds(7, 1), :] * (c != 0).astype(f32)
        m0 = m_s[...]
        ppv = tuple(pp_ref[:, pl.ds(q, 1), :] for q in range(5))
        og, m_new, v, tm = _rwkv_step(layer1, tuple(r[...] for r in prm_refs), y_ref[...], prev,
                                      vf_ref[...] if layer1 else None, ppv, m0)
        mck_ref[0, :n_pair] = m0
        mck_ref[0, n_pair:] = tm
        if not layer1:
            vout_ref[...] = v
        for j in range(n_pair):
            cat_ref[:, j * LANES:(j + 1) * LANES] = og[j]
        m_s[...] = m_new

    grp = _group(n_pair)
    assert grp == n_pair
    out_shape = [jax.ShapeDtypeStruct((t, cat_width), f32)]
    out_specs = [pl.BlockSpec((CHUNK, grp * LANES), lambda c, p: (c, p))]
    if not layer1:
        out_shape.append(jax.ShapeDtypeStruct((t, dr), f32))
        out_specs.append(pl.BlockSpec((CHUNK, dr), lambda c, p: (c, 0)))
    out_shape.append(jax.ShapeDtypeStruct((nc, 2 * n_pair, LANES, LANES), f32))
    out_specs.append(pl.BlockSpec((1, 2 * grp, LANES, LANES), lambda c, p: (c, p, 0, 0)))
    args = [proj, proj] + ([vf] if layer1 else []) + list(prm) + [pp]
    return pl.pallas_call(
        body, grid=(nc, 1), in_specs=specs, out_specs=out_specs, out_shape=out_shape,
        scratch_shapes=[pltpu.VMEM((n_pair, LANES, LANES), f32)],
        compiler_params=pltpu.CompilerParams(dimension_semantics=("arbitrary", "arbitrary")),
        name=f"rwkv_fwd_l{int(layer1)}",
    )(*args)


def _rwkv_bwd(layer1, proj, vf, prm, pp, mck, dcat, dvout):
    t = proj.shape[0]
    dr = prm[1].shape[1]
    rwc = prm[0].shape[1]
    n_pair = dr // LANES
    nc = t // CHUNK
    n_prm = len(prm)
    specs, prm_shapes, cidx, full = _rwkv_specs(layer1, t, dr, rwc, n_pair, True)
    grp = _group(n_pair)
    assert grp == n_pair
    specs.append(pl.BlockSpec((1, 2 * grp, LANES, LANES), lambda c, p: (cidx(c), p, 0, 0)))
    specs.append(pl.BlockSpec((CHUNK, grp * LANES), lambda c, p: (cidx(c), p)))
    if not layer1:
        specs.append(pl.BlockSpec((CHUNK, dr), lambda c, p: (cidx(c), 0)))

    def body(*refs):
        y_ref, prev_ref = refs[0], refs[1]
        i = 2
        vf_ref = None
        if layer1:
            vf_ref = refs[i]
            i += 1
        prm_refs = refs[i:i + n_prm]
        i += n_prm
        pp_ref, mck_ref, dog_ref = refs[i], refs[i + 1], refs[i + 2]
        i += 3
        dvout_ref = None
        if not layer1:
            dvout_ref = refs[i]
            i += 1
        dy_ref = refs[i]
        i += 1
        dvf_ref = None
        if layer1:
            dvf_ref = refs[i]
            i += 1
        dprm_refs = refs[i:i + n_prm]
        i += n_prm
        dpp_ref = refs[i]
        dm_s, dprev_s = refs[i + 1:i + 3]
        c = pl.program_id(0)
        cr = nc - 1 - c

        @pl.when(c == 0)
        def _():
            dm_s[...] = jnp.zeros_like(dm_s)
            dprev_s[...] = jnp.zeros_like(dprev_s)
            dpp_ref[...] = jnp.zeros_like(dpp_ref)
            for r in dprm_refs:
                r[...] = jnp.zeros_like(r)

        prev = prev_ref[pl.ds(7, 1), :] * (cr != 0).astype(f32)
        prm_v = tuple(r[...] for r in prm_refs)
        ppv = tuple(pp_ref[:, pl.ds(q, 1), :] for q in range(5))
        dog = jnp.stack([dog_ref[:, j * LANES:(j + 1) * LANES] for j in range(n_pair)], axis=0)
        m0, tm = mck_ref[0, :n_pair], mck_ref[0, n_pair:]
        no_tm = jnp.zeros_like(tm)
        if layer1:
            _, vjp = jax.vjp(lambda a, b, d, e, g, h: _rwkv_step(True, a, b, d, e, g, h, tm),
                             prm_v, y_ref[...], prev, vf_ref[...], ppv, m0)
            dprm, dy, dprev, dvf, dppv, dm0 = vjp((dog, dm_s[...], jnp.zeros((CHUNK, dr), f32), no_tm))
            dvf_ref[...] = dvf
        else:
            _, vjp = jax.vjp(lambda a, b, d, e, g: _rwkv_step(False, a, b, d, None, e, g, tm), prm_v, y_ref[...], prev, ppv, m0)
            dprm, dy, dprev, dppv, dm0 = vjp((dog, dm_s[...], dvout_ref[...], no_tm))
        dm_s[...] = dm0
        for q in range(5):
            dpp_ref[:, pl.ds(q, 1), :] += dppv[q]
        dy_ref[...] = (dy + jnp.where(_iota((CHUNK, 1), 0) == CHUNK - 1, dprev_s[...], 0.0)).astype(bf16)
        dprev_s[...] = dprev
        for r, gval in zip(dprm_refs, dprm):
            r[...] += gval

    out_shape = [jax.ShapeDtypeStruct((t, proj.shape[1]), bf16)]
    out_specs = [pl.BlockSpec((CHUNK, rwc), lambda c, p: (cidx(c), 0))]
    if layer1:
        out_shape.append(jax.ShapeDtypeStruct((t, dr), f32))
        out_specs.append(pl.BlockSpec((CHUNK, dr), lambda c, p: (cidx(c), 0)))
    out_shape += [jax.ShapeDtypeStruct(s, f32) for s in prm_shapes]
    out_specs += [full(s) for s in prm_shapes]
    out_shape.append(jax.ShapeDtypeStruct((n_pair, 8, LANES), f32))
    out_specs.append(full((n_pair, 8, LANES)))
    args = [proj, proj] + ([vf] if layer1 else []) + list(prm) + [pp, mck, dcat] + ([] if layer1 else [dvout])
    return pl.pallas_call(
        body, grid=(nc, 1), in_specs=specs, out_specs=out_specs, out_shape=out_shape,
        scratch_shapes=[pltpu.VMEM((n_pair, LANES, LANES), f32), pltpu.VMEM((1, rwc), f32)],
        compiler_params=pltpu.CompilerParams(dimension_semantics=("arbitrary", "arbitrary")),
        name=f"rwkv_bwd_l{int(layer1)}",
    )(*args)


def _hgrn_chunk(layer1, lbl, gnw, s0, q_raw, f_raw, i_in, z):
    c = q_raw.shape[-2]
    q = _silu(q_raw)
    ls = _log_sigmoid(f_raw)
    if layer1:
        l0, l1 = lbl[..., 0:1, :], lbl[..., 1:2, :]
        mx = jnp.maximum(l0, l1)
        e0, e1 = jnp.exp(l0 - mx), jnp.exp(l1 - mx)
        sm0, sm1 = e0 / (e0 + e1), e1 / (e0 + e1)
        lb = (sm0 + sm1) - sm0
        log_f = _logaddexp(jnp.log(jnp.maximum(lb, LB_FLOOR)), jnp.log1p(-lb) + ls)
        k = (1.0 - lb) * jax.nn.sigmoid(-f_raw)
    else:
        log_f = _logaddexp(jnp.full_like(ls, jnp.log(jnp.float32(LB_FLOOR))), ls)
        k = jax.nn.sigmoid(-f_raw)
    row, col = _iota((c, c), 0), _iota((c, c), 1)
    trow = _iota((c, 1), 0)
    halves = []
    half = c // 2
    while half >= 1:
        halves.append(half)
        half //= 2
    cmat = jnp.concatenate([(col <= row).astype(f32)]
                           + [(col <= (row // (2 * hf)) * (2 * hf) + hf - 1).astype(f32) for hf in halves], axis=0)
    ball = _const_left(cmat.astype(bf16), log_f)
    b = ball[..., :c, :]
    att = None
    for lvl, hf in enumerate(halves):
        blk = 2 * hf
        bref = ball[..., (lvl + 1) * c:(lvl + 2) * c, :]
        upper = (trow % blk) >= hf
        dec = jnp.exp(jnp.where(upper, b - bref, bref - b))
        qh = jnp.where(upper, q * dec, 0.0)
        kh = jnp.where(upper, 0.0, k * dec)
        term = jnp.where(row // blk == col // blk, _mm2(qh, kh, "nt", APPLY_PASSES), 0.0)
        att = term if att is None else att + term
    lhs = jnp.concatenate([q * jnp.exp(b), att, jnp.zeros(att.shape[:-1] + (LANES - c,), f32)], axis=-1)
    rhs = jnp.concatenate([s0, i_in, jnp.zeros(i_in.shape[:-2] + (LANES - c, i_in.shape[-1]), f32)], axis=-2)
    o = _mm2(lhs, rhs, "nn", APPLY_PASSES) + jnp.sum(q * k, axis=-1, keepdims=True) * i_in
    b_last = _last_row(b)
    s_new = _col_of_row(jnp.exp(b_last)) * s0 + _mm2(k * jnp.exp(b_last - b), i_in, "tn", APPLY_PASSES)
    o = o * lax.rsqrt(jnp.mean(o * o, axis=-1, keepdims=True) + RMS_EPS)
    return o * gnw * _silu(z), s_new


def _hgrn_in_specs(t, dh, col0, rev):
    nc = t // CHUNK
    nh = dh // LANES

    def cidx(c):
        return (nc - 1 - c) if rev else c

    grp = _group(nh)
    specs = [pl.BlockSpec((CHUNK, LANES), functools.partial(lambda g, j, h, c: (cidx(c), col0 + g * nh + h * grp + j), g, j))
             for j in range(grp) for g in range(4)]
    specs.append(pl.BlockSpec((2, grp * LANES), lambda h, c: (0, h)))
    specs.append(pl.BlockSpec((1, grp * LANES), lambda h, c: (0, h)))
    return specs, cidx, grp


def _hgrn_fwd(layer1, proj, lbl, gnw, cat, rwc):
    t, d = cat.shape
    dh = gnw.shape[1]
    nh = dh // LANES
    nc = t // CHUNK
    col0 = rwc // LANES
    specs, _, grp = _hgrn_in_specs(t, dh, col0, False)
    specs.append(pl.BlockSpec(memory_space=pl.ANY))
    assert (d - dh) % (grp * LANES) == 0
    cat_col0 = (d - dh) // (grp * LANES)

    def body(*refs):
        x_refs = refs[:4 * grp]
        lbl_ref, gnw_ref, _, cat_ref, sck_ref, s_s = refs[4 * grp:]
        c = pl.program_id(1)

        @pl.when(c == 0)
        def _():
            s_s[...] = jnp.zeros_like(s_s)

        lanes = [slice(j * LANES, (j + 1) * LANES) for j in range(grp)]
        s0 = s_s[...]
        sck_ref[:, 0] = s0
        out, s_new = _hgrn_chunk(layer1, jnp.stack([lbl_ref[:, ln] for ln in lanes]), jnp.stack([gnw_ref[:, ln] for ln in lanes]),
                                 s0, *(jnp.stack([x_refs[4 * j + g][...] for j in range(grp)]) for g in range(4)))
        for j in range(grp):
            cat_ref[:, lanes[j]] = out[j]
        s_s[...] = s_new

    return pl.pallas_call(
        body, grid=(nh // grp, nc), in_specs=specs,
        out_specs=[pl.BlockSpec((CHUNK, grp * LANES), lambda h, c: (c, cat_col0 + h)),
                   pl.BlockSpec((grp, 1, LANES, LANES), lambda h, c: (h, c, 0, 0))],
        out_shape=[jax.ShapeDtypeStruct((t, d), f32), jax.ShapeDtypeStruct((nh, nc, LANES, LANES), f32)],
        scratch_shapes=[pltpu.VMEM((grp, LANES, LANES), f32)],
        input_output_aliases={4 * grp + 2: 0},
        compiler_params=pltpu.CompilerParams(dimension_semantics=("arbitrary", "arbitrary")),
        name=f"hgrn_fwd_l{int(layer1)}",
    )(*([proj] * (4 * grp)), lbl, gnw, cat)


def _hgrn_bwd(layer1, proj, lbl, gnw, sck, dcat, rwc, dproj):
    t, d = dcat.shape
    dh = gnw.shape[1]
    nh = dh // LANES
    nc = t // CHUNK
    col0 = rwc // LANES
    specs, cidx, grp = _hgrn_in_specs(t, dh, col0, True)
    assert grp == nh and (d - dh) % (grp * LANES) == 0
    cat_col0 = (d - dh) // (grp * LANES)
    specs.append(pl.BlockSpec((grp, 1, LANES, LANES), lambda h, c: (h, cidx(c), 0, 0)))
    specs.append(pl.BlockSpec((CHUNK, grp * LANES), lambda h, c: (cidx(c), cat_col0 + h)))
    specs.append(pl.BlockSpec(memory_space=pl.ANY))

    def body(*refs):
        x_refs = refs[:4 * grp]
        lbl_ref, gnw_ref, sck_ref, do_ref, _, dp_hbm, dlbl_ref, dgnw_ref, ds_s, stage, sems = refs[4 * grp:]
        c = pl.program_id(1)
        slot = c % 2

        def put(s, g, chunk):
            return pltpu.make_async_copy(stage.at[s, g], dp_hbm.at[pl.ds(chunk * CHUNK, CHUNK), pl.ds(rwc + g * dh, dh)],
                                         sems.at[s, g])

        @pl.when(c == 0)
        def _():
            ds_s[...] = jnp.zeros_like(ds_s)
            dlbl_ref[...] = jnp.zeros_like(dlbl_ref)
            dgnw_ref[...] = jnp.zeros_like(dgnw_ref)

        @pl.when(c >= 2)
        def _():
            for g in range(4):
                put(slot, g, 0).wait()

        lanes = [slice(j * LANES, (j + 1) * LANES) for j in range(grp)]
        _, vjp = jax.vjp(functools.partial(_hgrn_chunk, layer1),
                         jnp.stack([lbl_ref[:, ln] for ln in lanes]), jnp.stack([gnw_ref[:, ln] for ln in lanes]), sck_ref[:, 0],
                         *(jnp.stack([x_refs[4 * j + g][...] for j in range(grp)]) for g in range(4)))
        dlbl, dgnw, ds0, dq, df, di, dz = vjp((jnp.stack([do_ref[:, ln] for ln in lanes]), ds_s[...]))
        ds_s[...] = ds0
        for j in range(grp):
            dlbl_ref[:, lanes[j]] += dlbl[j]
            dgnw_ref[:, lanes[j]] += dgnw[j]
            for g, val in enumerate((dq, df, di, dz)):
                stage[slot, g, :, lanes[j]] = val[j].astype(bf16)
        for g in range(4):
            put(slot, g, nc - 1 - c).start()

        @pl.when(c == nc - 1)
        def _():
            for g in range(4):
                put(slot, g, 0).wait()
                if nc >= 2:
                    put(1 - slot, g, 0).wait()

    return pl.pallas_call(
        body, grid=(1, nc), in_specs=specs,
        out_specs=[pl.BlockSpec(memory_space=pl.ANY),
                   pl.BlockSpec((2, grp * LANES), lambda h, c: (0, h)),
                   pl.BlockSpec((1, grp * LANES), lambda h, c: (0, h))],
        out_shape=[jax.ShapeDtypeStruct(dproj.shape, dproj.dtype), jax.ShapeDtypeStruct((2, dh), f32),
                   jax.ShapeDtypeStruct((1, dh), f32)],
        scratch_shapes=[pltpu.VMEM((grp, LANES, LANES), f32), pltpu.VMEM((2, 4, CHUNK, dh), bf16),
                        pltpu.SemaphoreType.DMA((2, 4))],
        input_output_aliases={4 * grp + 4: 0},
        compiler_params=pltpu.CompilerParams(dimension_semantics=("arbitrary", "arbitrary")),
        name=f"hgrn_bwd_l{int(layer1)}",
    )(*([proj] * (4 * grp)), lbl, gnw, sck, dcat, dproj)


def _ln(h, y, w, b):
    u = ALPHA * h + y
    mu = jnp.mean(u, axis=-1, keepdims=True)
    var = jnp.mean(jnp.square(u - mu), axis=-1, keepdims=True)
    return (u - mu) * lax.rsqrt(var + LN_EPS) * w + b


def _row_tile(t):
    return 256 if t % 256 == 0 else t


def _ln_fwd(h, y, w, b):
    t, d = h.shape
    tr = _row_tile(t)

    def body(h_ref, y_ref, w_ref, b_ref, o_ref, o16_ref):
        out = _ln(h_ref[...], y_ref[...], w_ref[...], b_ref[...])
        o_ref[...] = out
        o16_ref[...] = out.astype(bf16)

    row = pl.BlockSpec((tr, d), lambda i: (i, 0))
    vec = pl.BlockSpec((1, d), lambda i: (0, 0))
    return pl.pallas_call(body, grid=(t // tr,), in_specs=[row, row, vec, vec], out_specs=[row, row],
                          out_shape=[jax.ShapeDtypeStruct((t, d), f32), jax.ShapeDtypeStruct((t, d), bf16)],
                          name="ln_fwd")(h, y, w, b)


def _ln_loss_bwd(h, y, w, b, tgt):
    t, d = h.shape
    tr = _row_tile(t)

    def body(h_ref, y_ref, w_ref, b_ref, t_ref, dy_ref, dy16_ref, dw_ref, db_ref, loss_ref):
        @pl.when(pl.program_id(0) == 0)
        def _():
            dw_ref[...] = jnp.zeros_like(dw_ref)
            db_ref[...] = jnp.zeros_like(db_ref)
            loss_ref[...] = jnp.zeros_like(loss_ref)

        out, vjp = jax.vjp(lambda yy, ww, bb: _ln(h_ref[...], yy, ww, bb), y_ref[...], w_ref[...], b_ref[...])
        err = out - t_ref[...]
        loss_ref[...] += 0.5 * jnp.sum(jnp.mean(jnp.square(err), axis=-1, keepdims=True), axis=0, keepdims=True)
        dy, dw, db = vjp(err * (1.0 / d))
        dy_ref[...] = dy
        dy16_ref[...] = dy.astype(bf16)
        dw_ref[...] += dw
        db_ref[...] += db

    row = pl.BlockSpec((tr, d), lambda i: (i, 0))
    vec = pl.BlockSpec((1, d), lambda i: (0, 0))
    return pl.pallas_call(
        body, grid=(t // tr,), in_specs=[row, row, vec, vec, row],
        out_specs=[row, row, vec, vec, pl.BlockSpec((1, LANES), lambda i: (0, 0))],
        out_shape=[jax.ShapeDtypeStruct((t, d), f32), jax.ShapeDtypeStruct((t, d), bf16), jax.ShapeDtypeStruct((1, d), f32),
                   jax.ShapeDtypeStruct((1, d), f32), jax.ShapeDtypeStruct((1, LANES), f32)],
        compiler_params=pltpu.CompilerParams(dimension_semantics=("arbitrary",)), name="ln_loss_bwd")(h, y, w, b, tgt)


def _ln_bwd(h, y, w, b, dout):
    t, d = h.shape
    tr = _row_tile(t)

    def body(h_ref, y_ref, w_ref, b_ref, do_ref, dy_ref, dy16_ref, dw_ref, db_ref):
        @pl.when(pl.program_id(0) == 0)
        def _():
            dw_ref[...] = jnp.zeros_like(dw_ref)
            db_ref[...] = jnp.zeros_like(db_ref)

        _, vjp = jax.vjp(lambda yy, ww, bb: _ln(h_ref[...], yy, ww, bb), y_ref[...], w_ref[...], b_ref[...])
        dy, dw, db = vjp(do_ref[...])
        dy_ref[...] = dy
        dy16_ref[...] = dy.astype(bf16)
        dw_ref[...] += dw
        db_ref[...] += db

    row = pl.BlockSpec((tr, d), lambda i: (i, 0))
    vec = pl.BlockSpec((1, d), lambda i: (0, 0))
    return pl.pallas_call(
        body, grid=(t // tr,), in_specs=[row, row, vec, vec, row], out_specs=[row, row, vec, vec],
        out_shape=[jax.ShapeDtypeStruct((t, d), f32), jax.ShapeDtypeStruct((t, d), bf16),
                   jax.ShapeDtypeStruct((1, d), f32), jax.ShapeDtypeStruct((1, d), f32)],
        compiler_params=pltpu.CompilerParams(dimension_semantics=("arbitrary",)), name="ln_bwd")(h, y, w, b, dout)


def _pick(n, prefs):
    for p in prefs:
        if n % p == 0:
            return p
    return n


def _tile(n, want):
    if n <= want:
        return n
    for cand in range(want - want % LANES, 0, -LANES):
        if n % cand == 0:
            return cand
    return n


_MM_TILES = {"proj": (1024, 1664, 2048), "out": (1024, 1024, 2048), "dcat": (1024, 1024, 2048),
             "dwout": (512, 2048, 2048), "dwin": (640, 2048, 2048), "dh": (1024, 1024, 1664)}


def _matmul(a, b, mode, name, tiles, add=None, add_scale=1.0, out_dtype=f32, after=None):
    if mode == "nn":
        (m, k), n = a.shape, b.shape[1]
    elif mode == "nt":
        (m, k), n = a.shape, b.shape[0]
    else:
        (k, m), n = a.shape, b.shape[1]
    tm, tn, tk = _tile(m, tiles[0]), _tile(n, tiles[1]), _tile(k, tiles[2])
    nk = k // tk
    cache_a = nk == 1 and a.dtype != bf16 and n // tn > 1

    def body(*refs):
        a_ref, b_ref = refs[0], refs[1]
        add_ref = refs[2] if add is not None else None
        n_in = 2 + (add is not None) + (after is not None)
        o_ref = refs[n_in]
        scratch = refs[n_in + 1:]

        def finish(res):
            if add is not None:
                res = res + add_scale * add_ref[...]
            o_ref[...] = res.astype(out_dtype)

        if cache_a:
            a_bf = scratch[0]

            @pl.when(pl.program_id(1) == 0)
            def _():
                a_bf[...] = a_ref[...].astype(bf16)

            a_val = a_bf[...]
        else:
            a_val = a_ref[...].astype(bf16)
        prod = lax.dot_general(a_val, b_ref[...].astype(bf16), _DIMS[mode], preferred_element_type=f32)
        if nk == 1:
            finish(prod)
        else:
            acc = scratch[-1]
            kk = pl.program_id(2)

            @pl.when(kk == 0)
            def _():
                acc[...] = prod

            @pl.when(kk != 0)
            def _():
                acc[...] += prod

            @pl.when(kk == nk - 1)
            def _():
                finish(acc[...])

    a_shape = (tk, tm) if mode == "tn" else (tm, tk)
    a_spec = pl.BlockSpec(a_shape, (lambda i, j, kk: (kk, i)) if mode == "tn" else (lambda i, j, kk: (i, kk)))
    b_spec = pl.BlockSpec((tn, tk), lambda i, j, kk: (j, kk)) if mode == "nt" else pl.BlockSpec((tk, tn), lambda i, j, kk: (kk, j))
    o_spec = pl.BlockSpec((tm, tn), lambda i, j, kk: (i, j))
    in_specs = [a_spec, b_spec] + ([o_spec] if add is not None else []) + ([pl.BlockSpec(memory_space=pl.ANY)] if after is not None else [])
    args = [a, b] + ([add] if add is not None else []) + ([after] if after is not None else [])
    scratch_shapes = ([pltpu.VMEM(a_shape, bf16)] if cache_a else []) + ([pltpu.VMEM((tm, tn), f32)] if nk > 1 else [])
    return pl.pallas_call(
        body, grid=(m // tm, n // tn, nk), in_specs=in_specs, out_specs=o_spec,
        out_shape=jax.ShapeDtypeStruct((m, n), out_dtype), scratch_shapes=scratch_shapes,
        compiler_params=pltpu.CompilerParams(dimension_semantics=("parallel", "arbitrary", "arbitrary")),
        name=name,
    )(*args)


def _position():
    return lax.axis_index("x"), lax.axis_index("y"), lax.axis_index("c")


def _flip(pos, k):
    x, y, c = pos
    return (1 - x if k & 4 else x, 1 - y if k & 2 else y, 1 - c if k & 1 else c)


def _index(pos):
    return 4 * pos[0] + 2 * pos[1] + pos[2]


def _all_gather_rows(xs, name):
    n_arr = len(xs)
    chips = (2, 4, 6)

    def body(*refs):
        x_refs, out_refs = refs[:n_arr], refs[n_arr:2 * n_arr]
        send_sems, recv_sems, local_sems = refs[2 * n_arr:]
        me = _position()
        sibling = _flip(me, 1)

        def copy(i, sem, block, to, own=False):
            m_per = x_refs[i].shape[0]
            rows = out_refs[i].at[pl.ds(_index(block) * m_per, m_per), :]
            return pltpu.make_async_remote_copy(
                src_ref=x_refs[i] if own else rows, dst_ref=rows,
                send_sem=send_sems.at[7 * i + sem], recv_sem=recv_sems.at[7 * i + sem], device_id=to, device_id_type=MESH)

        mine = [pltpu.make_async_copy(x_refs[i], out_refs[i].at[pl.ds(_index(me) * x_refs[i].shape[0], x_refs[i].shape[0]), :],
                                      local_sems.at[i]) for i in range(n_arr)]
        first, passed = [], []
        for i in range(n_arr):
            first.append(copy(i, 0, me, sibling, own=True))
            first += [copy(i, 1 + j, me, _flip(me, k), own=True) for j, k in enumerate(chips)]
            passed.append([copy(i, 4 + j, _flip(me, k), sibling) for j, k in enumerate(chips)])
        for cp in mine + first:
            cp.start()
        for i in range(n_arr):
            for j, k in enumerate(chips):
                copy(i, 1 + j, _flip(me, k), me).wait_recv()
                passed[i][j].start()
        for i in range(n_arr):
            copy(i, 0, sibling, me).wait_recv()
            for j, k in enumerate(chips):
                copy(i, 4 + j, _flip(sibling, k), me).wait_recv()
        for cp in first + [cp for group in passed for cp in group]:
            cp.wait_send()
        for cp in mine:
            cp.wait()

    anyspec = pl.BlockSpec(memory_space=pl.ANY)
    return pl.pallas_call(
        body, out_shape=[jax.ShapeDtypeStruct((N_DEV * x.shape[0], x.shape[1]), x.dtype) for x in xs],
        in_specs=[anyspec] * n_arr, out_specs=[anyspec] * n_arr,
        scratch_shapes=[pltpu.SemaphoreType.DMA((7 * n_arr,)), pltpu.SemaphoreType.DMA((7 * n_arr,)),
                        pltpu.SemaphoreType.DMA((n_arr,))],
        name=name,
    )(*xs)


def _split_start(srcs, lands, plan, n_copies, name, after=()):
    n_arr = len(srcs)
    n_after = len(after)
    hbm = pl.BlockSpec(memory_space=pltpu.HBM)
    sem = pl.BlockSpec(memory_space=pltpu.SEMAPHORE)

    def body(*refs):
        src_refs, land_refs = refs[:n_arr], refs[n_arr:2 * n_arr]
        outs_at = 2 * n_arr + n_after
        send_sems, recv_sems = refs[outs_at:outs_at + n_arr], refs[outs_at + n_arr:outs_at + 2 * n_arr]
        token = refs[-1]
        me = _position()
        for i in range(n_arr):
            for j, (src, dst, peer, _) in enumerate(plan(i, src_refs[i], land_refs[i], me)):
                pltpu.make_async_remote_copy(src_ref=src, dst_ref=dst, send_sem=send_sems[i].at[j], recv_sem=recv_sems[i].at[j],
                                             device_id=peer, device_id_type=MESH).start()
        token[...] = jnp.zeros_like(token)

    outs = pl.pallas_call(
        body, name=name,
        out_shape=([pltpu.SemaphoreType.DMA((n_copies,))] * (2 * n_arr)
                   + [pltpu.HBM(a.shape, a.dtype) for a in list(srcs) + list(lands)]
                   + [jax.ShapeDtypeStruct((8, LANES), f32)]),
        in_specs=[hbm] * (2 * n_arr) + [pl.BlockSpec(memory_space=pl.ANY)] * n_after,
        out_specs=[sem] * (2 * n_arr) + [hbm] * (2 * n_arr) + [pl.BlockSpec(memory_space=pltpu.VMEM)],
        input_output_aliases={i: 2 * n_arr + i for i in range(2 * n_arr)},
        compiler_params=pltpu.CompilerParams(has_side_effects=pltpu.SideEffectType.DATAFLOW_SIDE_EFFECTING),
    )(*[pltpu.with_memory_space_constraint(a, pltpu.HBM) for a in list(srcs) + list(lands)], *after)
    return (outs[:n_arr], outs[n_arr:2 * n_arr], outs[2 * n_arr:3 * n_arr], outs[3 * n_arr:4 * n_arr], outs[-1])


def _split_wait(started, plan, after, name):
    send_sems, recv_sems, srcs, lands, _ = started
    n_arr = len(srcs)
    hbm = pl.BlockSpec(memory_space=pltpu.HBM)
    sem = pl.BlockSpec(memory_space=pltpu.SEMAPHORE)

    def body(*refs):
        src_refs, land_refs = refs[:n_arr], refs[n_arr:2 * n_arr]
        s_sems, r_sems = refs[2 * n_arr:3 * n_arr], refs[3 * n_arr:4 * n_arr]
        me = _position()
        for i in range(n_arr):
            for j, (src, _, peer, arrival) in enumerate(plan(i, src_refs[i], land_refs[i], me)):
                cp = pltpu.make_async_remote_copy(src_ref=src, dst_ref=arrival, send_sem=s_sems[i].at[j], recv_sem=r_sems[i].at[j],
                                                  device_id=peer, device_id_type=MESH)
                cp.wait_send()
                cp.wait_recv()

    outs = pl.pallas_call(
        body, name=name,
        out_shape=[pltpu.HBM(a.shape, a.dtype) for a in list(srcs) + list(lands)],
        in_specs=[hbm] * (2 * n_arr) + [sem] * (2 * n_arr) + [pl.BlockSpec(memory_space=pl.ANY)],
        out_specs=[hbm] * (2 * n_arr),
        input_output_aliases={i: i for i in range(2 * n_arr)},
        compiler_params=pltpu.CompilerParams(has_side_effects=pltpu.SideEffectType.DATAFLOW_SIDE_EFFECTING),
    )(*srcs, *lands, *send_sems, *recv_sems, after)
    return outs[:n_arr], outs[n_arr:]


_GATHER_FLIPS = (1, 2, 4, 6)


def _gather_plan(i, src_ref, land_ref, me):
    m = src_ref.shape[0]

    def rows(pos):
        return land_ref.at[pl.ds(_index(pos) * m, m), :]

    return [(src_ref, rows(me), _flip(me, k), rows(_flip(me, k))) for k in _GATHER_FLIPS]


def _gather_forward(lands, name):
    n_arr = len(lands)
    chips = (2, 4, 6)

    def body(*refs):
        out_refs = refs[n_arr:2 * n_arr]
        send_sems, recv_sems = refs[2 * n_arr:]
        me = _position()
        sibling = _flip(me, 1)
        sends, arrivals = [], []
        for i, out_ref in enumerate(out_refs):
            m = out_ref.shape[0] // N_DEV

            def copy(pos, j):
                blk = out_ref.at[pl.ds(_index(pos) * m, m), :]
                return pltpu.make_async_remote_copy(src_ref=blk, dst_ref=blk, send_sem=send_sems.at[3 * i + j],
                                                    recv_sem=recv_sems.at[3 * i + j], device_id=sibling, device_id_type=MESH)

            for j, k in enumerate(chips):
                sends.append(copy(_flip(me, k), j))
                arrivals.append(copy(_flip(sibling, k), j))
        for cp in sends:
            cp.start()
        for cp in arrivals:
            cp.wait_recv()
        for cp in sends:
            cp.wait_send()

    anyspec = pl.BlockSpec(memory_space=pl.ANY)
    return pl.pallas_call(
        body, out_shape=[jax.ShapeDtypeStruct(a.shape, a.dtype) for a in lands],
        in_specs=[anyspec] * n_arr, out_specs=[anyspec] * n_arr, input_output_aliases={i: i for i in range(n_arr)},
        scratch_shapes=[pltpu.SemaphoreType.DMA((3 * n_arr,))] * 2, name=name,
    )(*lands)


def _chips_plan(i, src_ref, land_ref, me):
    m = src_ref.shape[0] // 4
    plan = []
    for j, k in enumerate((2, 4, 6)):
        peer = _flip(me, k)
        plan.append((src_ref.at[pl.ds((2 * peer[0] + peer[1]) * m, m), :], land_ref.at[j], peer, land_ref.at[j]))
    return plan


def _sibling_plan(i, src_ref, land_ref, me):
    m = src_ref.shape[0] // N_DEV
    sibling = _flip(me, 1)
    return [(src_ref.at[pl.ds((2 * q + 1 - me[2]) * m, m), :], land_ref.at[q], sibling, land_ref.at[q]) for q in range(4)]


def _sum_with_sibling(g, recv, name):
    m = g.shape[0] // N_DEV
    n = g.shape[1]
    tr = _pick(m, (208, 128, 64, 32, 16))
    nt = m // tr

    def body(g_ref, r_ref, o_ref):
        c = lax.axis_index("c")
        own = jnp.where(c == 0, g_ref[0, 0].astype(f32), g_ref[0, 1].astype(f32))
        o_ref[...] = (own + r_ref[0].astype(f32)).astype(o_ref.dtype)

    return pl.pallas_call(
        body, grid=(4, nt),
        in_specs=[pl.BlockSpec((1, 2, tr, n), lambda q, i: (q, 0, i, 0)), pl.BlockSpec((1, tr, n), lambda q, i: (q, i, 0))],
        out_specs=pl.BlockSpec((tr, n), lambda q, i: (q * nt + i, 0)),
        out_shape=jax.ShapeDtypeStruct((4 * m, n), bf16), name=name,
    )(g.reshape(4, 2, m, n), recv)


def _sum_with_chips(h, recv, name):
    m = h.shape[0] // 4
    n = h.shape[1]
    tr = _pick(m, (208, 128, 64, 32, 16))

    def body(h_ref, r_ref, o_ref):
        my_q = 2 * lax.axis_index("x") + lax.axis_index("y")
        own = h_ref[0].astype(f32)
        for q in range(1, 4):
            own = jnp.where(my_q == q, h_ref[q].astype(f32), own)
        o_ref[...] = ((own + r_ref[0].astype(f32)) + r_ref[1].astype(f32)) + r_ref[2].astype(f32)

    return pl.pallas_call(
        body, grid=(m // tr,),
        in_specs=[pl.BlockSpec((4, tr, n), lambda i: (0, i, 0)), pl.BlockSpec((3, tr, n), lambda i: (0, i, 0))],
        out_specs=pl.BlockSpec((tr, n), lambda i: (i, 0)), out_shape=jax.ShapeDtypeStruct((m, n), f32), name=name,
    )(h.reshape(4, m, n), recv)


def _sum_slots(parts, name):
    n_slot, m, n = parts.shape
    tr = _pick(m, (208, 128, 64, 32, 16, 8))

    def body(p_ref, o_ref):
        acc = p_ref[0]
        for s in range(1, n_slot):
            acc = acc + p_ref[s]
        o_ref[...] = acc

    return pl.pallas_call(
        body, grid=(m // tr,), in_specs=[pl.BlockSpec((n_slot, tr, n), lambda i: (0, i, 0))],
        out_specs=pl.BlockSpec((tr, n), lambda i: (i, 0)), out_shape=jax.ShapeDtypeStruct((m, n), parts.dtype), name=name,
    )(parts)


def _reduce_scatter_begin(gs, name):
    lands = [lax.empty((4, g.shape[0] // N_DEV, g.shape[1]), g.dtype) for g in gs]
    return _split_start(gs, lands, _sibling_plan, 4, "rs_d2d_start_" + name)


def _reduce_scatter_middle(started, after, name):
    gs, from_sibling = _split_wait(started, _sibling_plan, after, "rs_d2d_wait_" + name)
    chip_sums = [_sum_with_sibling(g, r, f"rs_sum2_{name}_{i}") for i, (g, r) in enumerate(zip(gs, from_sibling))]
    lands = [lax.empty((3, h.shape[0] // 4, h.shape[1]), h.dtype) for h in chip_sums]
    return _split_start(chip_sums, lands, _chips_plan, 3, "rs_ici_start_" + name)


def _reduce_scatter_end(started, after, name):
    chip_sums, from_chips = _split_wait(started, _chips_plan, after, "rs_ici_wait_" + name)
    return [_sum_with_chips(h, r, f"rs_sum4_{name}_{i}") for i, (h, r) in enumerate(zip(chip_sums, from_chips))]


def _adamw_update(w, g, m, v):
    mm = ADAM_B1 * m + (1.0 - ADAM_B1) * g
    vv = ADAM_B2 * v + (1.0 - ADAM_B2) * jnp.square(g)
    m_hat = mm / (1.0 - ADAM_B1 ** ADAM_STEP)
    v_hat = vv / (1.0 - ADAM_B2 ** ADAM_STEP)
    return -ADAM_LR * (m_hat / (jnp.sqrt(v_hat) + ADAM_EPS) + ADAM_WD * w), mm, vv


def _adamw_many(ws, gs, ms, vs, name):
    k = len(ws)
    shapes = [w.shape for w in ws]
    flat = [[a.reshape(-1, a.shape[-1]) for a in group] for group in (ws, gs, ms, vs)]

    def body(*refs):
        for i in range(k):
            d, mm, vv = _adamw_update(*(refs[j * k + i][...] for j in range(4)))
            refs[4 * k + i][...] = d
            refs[5 * k + i][...] = mm
            refs[6 * k + i][...] = vv

    outs = pl.pallas_call(
        body, out_shape=[jax.ShapeDtypeStruct(a.shape, f32) for a in flat[0]] * 3, name=name,
    )(*flat[0], *flat[1], *flat[2], *flat[3])
    return tuple([outs[j * k + i].reshape(shapes[i]) for i in range(k)] for j in range(3))


def _adamw(w, g, m, v, name):
    shape = w.shape
    n = shape[-1]
    r = w.size // n
    w2, g2, m2, v2 = (a.reshape(r, n) for a in (w, g, m, v))
    tr = _pick(r, (256, 208, 128, 64, 32, 16, 8))

    def body(w_ref, g_ref, m_ref, v_ref, d_ref, mo_ref, vo_ref):
        d_ref[...], mo_ref[...], vo_ref[...] = _adamw_update(w_ref[...], g_ref[...], m_ref[...], v_ref[...])

    spec = pl.BlockSpec((tr, n), lambda i: (i, 0))
    outs = pl.pallas_call(
        body, grid=(r // tr,), in_specs=[spec] * 4, out_specs=[spec] * 3,
        out_shape=[jax.ShapeDtypeStruct((r, n), f32)] * 3, name=name,
    )(w2, g2, m2, v2)
    return tuple(o.reshape(shape) for o in outs)


_SMALL = ("shift_mu", "w_decay0", "a0", "k_k", "k_a", "r_k", "ln_x_w", "ln_x_b", "v_mix0", "lb_logits",
          "g_norm_w", "ln_w", "ln_b")
_NAMES = ("w_in", "shift_mu", "w_decay0", "w_decay_up", "a0", "a_up", "k_k", "k_a", "r_k", "ln_x_w", "ln_x_b",
          "v_mix0", "v_mix_down", "v_mix_up", "lb_logits", "g_norm_w", "w_out", "ln_w", "ln_b")


def _pad_rows(a, rows, at_end):
    z = jnp.zeros((rows - a.shape[0], a.shape[1]), a.dtype)
    return jnp.concatenate([a, z] if at_end else [z, a], axis=0)


def kernel(x, w_in, shift_mu, w_decay0, w_decay_up, a0, a_up, k_k, k_a, r_k, ln_x_w, ln_x_b, v_mix0, v_mix_down, v_mix_up, lb_logits, g_norm_w, w_out, ln_w, ln_b, loss_target, m_w_in, m_shift_mu, m_w_decay0, m_w_decay_up, m_a0, m_a_up, m_k_k, m_k_a, m_r_k, m_ln_x_w, m_ln_x_b, m_v_mix0, m_v_mix_down, m_v_mix_up, m_lb_logits, m_g_norm_w, m_w_out, m_ln_w, m_ln_b, v_w_in, v_shift_mu, v_w_decay0, v_w_decay_up, v_a0, v_a_up, v_k_k, v_k_a, v_r_k, v_ln_x_w, v_ln_x_b, v_v_mix0, v_v_mix_down, v_v_mix_up, v_lb_logits, v_g_norm_w, v_w_out, v_ln_w, v_ln_b):
    weights = dict(w_in=w_in, shift_mu=shift_mu, w_decay0=w_decay0, w_decay_up=w_decay_up, a0=a0, a_up=a_up, k_k=k_k,
                   k_a=k_a, r_k=r_k, ln_x_w=ln_x_w, ln_x_b=ln_x_b, v_mix0=v_mix0, v_mix_down=v_mix_down,
                   v_mix_up=v_mix_up, lb_logits=lb_logits, g_norm_w=g_norm_w, w_out=w_out, ln_w=ln_w, ln_b=ln_b)
    mom1 = dict(w_in=m_w_in, shift_mu=m_shift_mu, w_decay0=m_w_decay0, w_decay_up=m_w_decay_up, a0=m_a0, a_up=m_a_up,
                k_k=m_k_k, k_a=m_k_a, r_k=m_r_k, ln_x_w=m_ln_x_w, ln_x_b=m_ln_x_b, v_mix0=m_v_mix0,
                v_mix_down=m_v_mix_down, v_mix_up=m_v_mix_up, lb_logits=m_lb_logits, g_norm_w=m_g_norm_w,
                w_out=m_w_out, ln_w=m_ln_w, ln_b=m_ln_b)
    mom2 = dict(w_in=v_w_in, shift_mu=v_shift_mu, w_decay0=v_w_decay0, w_decay_up=v_w_decay_up, a0=v_a0, a_up=v_a_up,
                k_k=v_k_k, k_a=v_k_a, r_k=v_r_k, ln_x_w=v_ln_x_w, ln_x_b=v_ln_x_b, v_mix0=v_v_mix0,
                v_mix_down=v_v_mix_down, v_mix_up=v_v_mix_up, lb_logits=v_lb_logits, g_norm_w=v_g_norm_w,
                w_out=v_w_out, ln_w=v_ln_w, ln_b=v_ln_b)
    assert x.shape[0] == 1 and w_in.shape[0] == DEPTH
    t, d = x.shape[1], x.shape[2]
    dr = w_decay0.shape[1]
    dh = g_norm_w.shape[1]
    rank_w, rank_a, rank_v = w_decay_up.shape[1], a_up.shape[1], v_mix_up.shape[1]
    rwc = 4 * dr + rank_w + rank_a
    assert rank_w + rank_a == LANES and rank_v <= LANES and dr + dh == d
    assert t % CHUNK == 0 and dr % LANES == 0 and dh % LANES == 0 and shift_mu.shape[1] == rwc
    n_pair = dr // LANES
    me = _index(_position())

    shard = dr // N_DEV
    pack = jnp.concatenate([w_decay_up[0], w_decay_up[1], a_up[0], a_up[1], v_mix_up[0], v_mix_down[0].T], axis=0)
    win_t0, pack = _all_gather_rows([w_in[0].T.astype(bf16), pack], "ag_first")
    win_t = [win_t0, None]
    wout = [None, None]
    late_blocks = [w_out[0].astype(bf16), w_in[1].T.astype(bf16), w_out[1].astype(bf16)]
    late_lands = [lax.dynamic_update_slice(lax.empty((N_DEV * blk.shape[0], blk.shape[1]), bf16), blk, (me * blk.shape[0], 0))
                  for blk in late_blocks]
    late_gather = _split_start(late_blocks, late_lands, _gather_plan, len(_GATHER_FLIPS), "ag_late_start",
                               after=(win_t[0], pack))
    pack = jnp.transpose(pack.reshape(N_DEV, -1, shard), (1, 0, 2)).reshape(-1, dr)
    offs = [0, rank_w, 2 * rank_w, 2 * rank_w + rank_a, 2 * rank_w + 2 * rank_a, 2 * rank_w + 2 * rank_a + rank_v,
            2 * rank_w + 2 * rank_a + 2 * rank_v]
    wdu_f = [pack[offs[0]:offs[1]], pack[offs[1]:offs[2]]]
    aup_f = [pack[offs[2]:offs[3]], pack[offs[3]:offs[4]]]
    vup_f = pack[offs[4]:offs[5]]
    vdown_f = pack[offs[5]:offs[6]].T

    def after_start(a, started):
        return a + started[-1][0:1, 0:1]

    def rwkv_params(l):
        mu = after_start(shift_mu[0:1], late_gather) if l == 0 else shift_mu[l:l + 1]
        prm = [mu, w_decay0[l:l + 1], a0[l:l + 1], _pad_rows(wdu_f[l], LANES, True),
               _pad_rows(aup_f[l], LANES, False)]
        if l == 1:
            prm += [v_mix0[0:1], _pad_rows(vdown_f.T, LANES, True).T, _pad_rows(vup_f, LANES, True)]
        rows = jnp.stack([k_k[l], k_a[l], r_k[l], ln_x_w[l], ln_x_b[l]] + [jnp.zeros((dr,), f32)] * 3, axis=0)
        pp = jnp.transpose(rows.reshape(8, n_pair, LANES), (1, 0, 2))
        return tuple(prm), pp

    h = x[0]
    h16 = h.astype(bf16)
    tgt = loss_target[0]
    saved = []
    vfirst = None
    for l in range(DEPTH):
        prm, pp = rwkv_params(l)
        proj = _matmul(h16, win_t[l], "nt", f"mm_proj_{l}", _MM_TILES["proj"])
        if l == 0:
            cat, vfirst, mck = _rwkv_fwd(False, proj, None, prm, pp, d)
        else:
            cat, mck = _rwkv_fwd(True, proj, vfirst, prm, pp, d)
        cat, sck = _hgrn_fwd(l == 1, proj, lb_logits, g_norm_w[l:l + 1], cat, rwc)
        if l == 0:
            _, arrived = _split_wait(late_gather, _gather_plan, cat, "ag_late_wait")
            wout[0], win_t[1], wout[1] = _gather_forward(arrived, "ag_late_forward")
        y = _matmul(cat, wout[l], "nn", f"mm_out_{l}", _MM_TILES["out"])
        saved.append((h, h16, proj, prm, pp, mck, sck, cat, y))
        if l < DEPTH - 1:
            h, h16 = _ln_fwd(h, y, ln_w[l:l + 1], ln_b[l:l + 1])
        else:
            top = _ln_loss_bwd(h, y, ln_w[l:l + 1], ln_b[l:l + 1], tgt)
    loss = lax.psum(top[4][0, 0], ("x", "y", "c"))

    grads = {}
    big = {}
    dvfirst = None
    d_lbl = None
    rs_started = {}
    for l in reversed(range(DEPTH)):
        h_l, h16_l, proj, prm, pp, mck, sck, cat, y = saved[l]
        if l == DEPTH - 1:
            dy, dy16, g_ln_w, g_ln_b = top[:4]
        else:
            dy, dy16, g_ln_w, g_ln_b = _ln_bwd(h_l, y, after_start(ln_w[l:l + 1], rs_started[l + 1]), ln_b[l:l + 1], dh_out)
        dcat = _matmul(dy16, wout[l], "nt", f"mm_dcat_{l}", _MM_TILES["dcat"])
        big[("w_out", l)] = _matmul(cat, dy16, "tn", f"mm_dwout_{l}", _MM_TILES["dwout"], out_dtype=bf16)
        if l == 1:
            outs = _rwkv_bwd(True, proj, vfirst, prm, pp, mck, dcat, None)
            dproj_r, dvfirst = outs[0], outs[1]
            dprm, dpp = outs[2:-1], outs[-1]
        else:
            outs = _rwkv_bwd(False, proj, None, prm, pp, mck, dcat, dvfirst)
            dproj_r = outs[0]
            dprm, dpp = outs[1:-1], outs[-1]
        dproj, dlbl_l, dgnw = _hgrn_bwd(l == 1, proj, lb_logits, g_norm_w[l:l + 1], sck, dcat, rwc, dproj_r)
        big[("w_in", l)] = _matmul(dproj, h16_l, "tn", f"mm_dwin_{l}", _MM_TILES["dwin"], out_dtype=bf16)
        sharded = [dprm[3][:rank_w].T, dprm[4][rank_w:].T]
        if l == 1:
            sharded += [dprm[6][:, :rank_v], dprm[7][:rank_v].T,
                        jnp.zeros((dr, LANES - 2 * rank_v), f32)]
        sharded = jnp.concatenate(sharded, axis=1).astype(bf16)
        d2d = _reduce_scatter_begin([big[("w_in", l)], big[("w_out", l)], sharded], f"l{l}")
        if l == 0:
            rs_started[l] = _reduce_scatter_middle(d2d, sharded, f"l{l}")
            token = rs_started[l][-1]
        else:
            token = d2d[-1]
        dh_out = _matmul(dproj, win_t[l], "nn", f"mm_dh_{l}", _MM_TILES["dh"], add=dy, add_scale=ALPHA, after=token)
        if l > 0:
            rs_started[l] = _reduce_scatter_middle(d2d, dh_out, f"l{l}")
        dpp = jnp.transpose(dpp, (1, 0, 2)).reshape(8, dr)
        grads[l] = dict(shift_mu=dprm[0][0], w_decay0=dprm[1][0], a0=dprm[2][0],
                        k_k=dpp[0], k_a=dpp[1], r_k=dpp[2], ln_x_w=dpp[3], ln_x_b=dpp[4],
                        g_norm_w=dgnw[0], ln_w=g_ln_w[0], ln_b=g_ln_b[0])
        if l == 1:
            grads[l].update(v_mix0=dprm[5][0])
            d_lbl = dlbl_l
    grad_x = dh_out[None]

    def both(name):
        return jnp.stack([grads[0][name], grads[1][name]])

    small = dict(shift_mu=both("shift_mu"), w_decay0=both("w_decay0"), a0=both("a0"), k_k=both("k_k"), k_a=both("k_a"),
                 r_k=both("r_k"), ln_x_w=both("ln_x_w"), ln_x_b=both("ln_x_b"), v_mix0=grads[1]["v_mix0"][None],
                 lb_logits=d_lbl, g_norm_w=both("g_norm_w"), ln_w=both("ln_w"), ln_b=both("ln_b"))
    flat = jnp.concatenate([small[nm].reshape(-1) for nm in _SMALL])
    n_flat = flat.shape[0]
    rows = -(-n_flat // (8 * LANES)) * 8
    flat = jnp.concatenate([flat, jnp.zeros((rows * LANES - n_flat,), f32)]).reshape(rows, LANES)
    total = _sum_slots(_all_gather_rows([flat], "ag_small_grads")[0].reshape(N_DEV, rows, LANES), "sum_small_grads").reshape(-1)
    gsm = {}
    off = 0
    for nm in _SMALL:
        size = small[nm].size
        gsm[nm] = total[off:off + size].reshape(small[nm].shape)
        off += size
    reduced = {1: _reduce_scatter_end(rs_started[1], dh_out, "l1")}
    reduced[0] = _reduce_scatter_end(rs_started[0], total, "l0")
    g_w_in_t = jnp.stack([reduced[l][0] for l in range(DEPTH)])
    gsm["w_in"] = jnp.transpose(g_w_in_t, (0, 2, 1))
    gsm["w_out"] = jnp.stack([reduced[l][1] for l in range(DEPTH)])
    gsm["w_decay_up"] = jnp.stack([reduced[l][2][:, :rank_w].T for l in range(DEPTH)])
    gsm["a_up"] = jnp.stack([reduced[l][2][:, rank_w:rank_w + rank_a].T for l in range(DEPTH)])
    gsm["v_mix_down"] = reduced[1][2][:, LANES:LANES + rank_v][None]
    gsm["v_mix_up"] = reduced[1][2][:, LANES + rank_v:LANES + 2 * rank_v].T[None]

    deltas, new_m, new_v = {}, {}, {}
    swap = lambda a: jnp.transpose(a, (0, 2, 1))
    deltas["w_in"], new_m["w_in"], new_v["w_in"] = (
        swap(a) for a in _adamw(swap(w_in), g_w_in_t, swap(m_w_in), swap(v_w_in), "adamw_w_in"))
    deltas["w_out"], new_m["w_out"], new_v["w_out"] = _adamw(w_out, gsm["w_out"], m_w_out, v_w_out, "adamw_w_out")
    rest = [nm for nm in _NAMES if nm not in ("w_in", "w_out")]
    d_rest, m_rest, v_rest = _adamw_many([weights[nm] for nm in rest], [gsm[nm] for nm in rest],
                                         [mom1[nm] for nm in rest], [mom2[nm] for nm in rest], "adamw_small")
    for i, nm in enumerate(rest):
        deltas[nm], new_m[nm], new_v[nm] = d_rest[i], m_rest[i], v_rest[i]
    return (loss, grad_x, *[gsm[nm] for nm in _NAMES], *[deltas[nm] for nm in _NAMES],
            *[new_m[nm] for nm in _NAMES], *[new_v[nm] for nm in _NAMES])
```

```python
import functools

import jax
import jax.numpy as jnp
from jax import lax
from jax.experimental import pallas as pl
from jax.experimental.pallas import tpu as pltpu

f32 = jnp.float32
bf16 = jnp.bfloat16

N_DEV = 8
CHUNK = 64
LANES = 128
RWKV_HEAD = 64
DEPTH = 2
ALPHA = (2 * DEPTH) ** 0.25
LN_EPS = 1e-5
GN_EPS = 64e-5
RMS_EPS = 1e-5
LB_FLOOR = 1e-30
ADAM_LR, ADAM_B1, ADAM_B2, ADAM_EPS, ADAM_WD, ADAM_STEP = 0.001, 0.9, 0.999, 1e-08, 0.01, 10
MESH = pl.DeviceIdType.MESH


def _iota(shape, d):
    return lax.broadcasted_iota(jnp.int32, shape, d)


_DIMS = {"nn": (((1,), (0,)), ((), ())), "nt": (((1,), (1,)), ((), ())), "tn": (((0,), (0,)), ((), ()))}
_BATCH_DIMS = {"nn": (((2,), (1,)), ((0,), (0,))), "nt": (((2,), (2,)), ((0,), (0,))), "tn": (((1,), (1,)), ((0,), (0,)))}
_K_AXES = {"nn": (-1, -2), "nt": (-1, -1), "tn": (-2, -2)}


def _mxu(a, b, mode):
    return lax.dot_general(a, b, (_BATCH_DIMS if a.ndim == 3 else _DIMS)[mode], preferred_element_type=f32)


def _split(x):
    hi = x.astype(bf16)
    return hi, (x - hi.astype(f32)).astype(bf16)


def _mm2_impl(a, b, mode, passes=3):
    if passes == 1:
        return _mxu(a.astype(bf16), b.astype(bf16), mode)
    ah, al = _split(a)
    if passes == 3:
        bh, bl = _split(b)
        lhs, rhs = [ah, ah, al], [bh, bl, bh]
    else:
        bh = b.astype(bf16)
        lhs, rhs = [ah, al], [bh, bh]
    ka, kb = _K_AXES[mode]
    k = a.shape[ka]
    if k % (LANES if -1 in (ka, kb) else 16) == 0:
        return _mxu(jnp.concatenate(lhs, axis=ka), jnp.concatenate(rhs, axis=kb), mode)
    out = _mxu(lhs[0], rhs[0], mode)
    for x, y in zip(lhs[1:], rhs[1:]):
        out = out + _mxu(x, y, mode)
    return out


@functools.partial(jax.custom_vjp, nondiff_argnums=(2, 3))
def _mm2(a, b, mode, passes=3):
    return _mm2_impl(a, b, mode, passes)


def _mm2_fwd(a, b, mode, passes):
    return _mm2_impl(a, b, mode, passes), (a, b)


def _mm2_bwd(mode, passes, res, g):
    a, b = res
    if mode == "nn":
        return _mm2_impl(g, b, "nt", passes), _mm2_impl(a, g, "tn", passes)
    if mode == "nt":
        return _mm2_impl(g, b, "nn", passes), _mm2_impl(g, a, "tn", passes)
    return _mm2_impl(b, g, "nt", passes), _mm2_impl(a, g, "nn", passes)


_mm2.defvjp(_mm2_fwd, _mm2_bwd)

TRI_PASSES = 1
APPLY_PASSES = 1


def _const_impl(cm, x, mode):
    if mode in ("r", "rt"):
        shape = x.shape
        out = _mxu(x.astype(bf16).reshape(-1, shape[-1]), cm, "nn" if mode == "r" else "nt")
        return out.reshape(shape[:-1] + (out.shape[-1],))
    hi, lo = _split(x)
    if x.ndim == 3:
        cm = jnp.broadcast_to(cm, (x.shape[0],) + cm.shape)
    return _mxu(cm, hi, mode) + _mxu(cm, lo, mode)


@jax.custom_vjp
def _const_left(cm, x):
    return _const_impl(cm, x, "nn")


_const_left.defvjp(lambda cm, x: (_const_impl(cm, x, "nn"), cm),
                   lambda cm, g: (jnp.zeros_like(cm), _const_impl(cm, g, "tn")))


@jax.custom_vjp
def _const_right(x, cm):
    return _const_impl(cm, x, "r")


_const_right.defvjp(lambda x, cm: (_const_impl(cm, x, "r"), cm),
                    lambda cm, g: (_const_impl(cm, g, "rt"), jnp.zeros_like(cm)))


def _tri_inv(a):
    n = a.shape[-1]
    tm = (_iota((n, n), 0) == _iota((n, n), 1)).astype(f32) + a
    ak = a
    for _ in range(5):
        ak = _mm2_impl(ak, ak, "nn", TRI_PASSES)
        tm = tm + _mm2_impl(tm, ak, "nn", TRI_PASSES)
    return tm


@jax.custom_vjp
def _tri_solve(tm, a, x):
    del a
    return _mm2_impl(tm, x, "nn")


def _tri_solve_fwd(tm, a, x):
    u = _mm2_impl(tm, x, "nn")
    return u, (tm, u)


def _tri_solve_bwd(res, du):
    tm, u = res
    dx = _mm2_impl(tm, du, "tn")
    return jnp.zeros_like(tm), _mm2_impl(dx, u, "nt"), dx


_tri_solve.defvjp(_tri_solve_fwd, _tri_solve_bwd)


def _col_of_row(row_vec):
    n = row_vec.shape[-1]
    eye = _iota((n, n), 0) == _iota((n, n), 1)
    return jnp.sum(jnp.where(eye, jnp.broadcast_to(row_vec, row_vec.shape[:-2] + (n, n)), 0.0), axis=-1, keepdims=True)


def _softplus(x):
    return jnp.maximum(x, 0.0) + jnp.log1p(jnp.exp(-jnp.abs(x)))


def _log_sigmoid(x):
    return -_softplus(-x)


def _logaddexp(a, b):
    return jnp.maximum(a, b) + jnp.log1p(jnp.exp(-jnp.abs(a - b)))


def _silu(x):
    return x * jax.nn.sigmoid(x)


def _tril(c, strict):
    r, s = _iota((c, c), 0), _iota((c, c), 1)
    return (r > s) if strict else (r >= s)


def _last_row(a):
    c = a.shape[-2]
    return jnp.sum(jnp.where(_iota(a.shape, a.ndim - 2) == c - 1, a, 0.0), axis=-2, keepdims=True)


def _rwkv_pre(layer1, prm, y, prev, vf):
    c = y.shape[0]
    if layer1:
        mu, w0, a0, wup, aup, v0, vdown, vup = prm
    else:
        mu, w0, a0, wup, aup = prm
    dr = w0.shape[1]
    shift = (_iota((c, c), 0) == _iota((c, c), 1) + 1).astype(bf16)
    y_prev = _const_left(shift, y) + jnp.where(_iota((c, 1), 0) == 0, prev, 0.0)
    rw = y + mu * (y_prev - y)
    r, k, v, z = (rw[:, i * dr:(i + 1) * dr] for i in range(4))
    wdad = rw[:, 4 * dr:4 * dr + LANES]
    w_raw = w0 + _mm2(jnp.tanh(wdad), wup, "nn")
    lw = -jnp.exp(-_softplus(-w_raw) - 0.5)
    asig = jax.nn.sigmoid(a0 + _mm2(wdad, aup, "nn"))
    if layer1:
        v = v + (vf - v) * jax.nn.sigmoid(v0 + _mm2(_mm2(v, vdown, "nn"), vup, "nn"))
    return r, k, v, z, lw, asig


def _rwkv_pair(pp, m0, xs, tm=None):
    kkw, kaw, rkw, gnw, gnb = pp
    r, k, v, z, lw, asig = xs
    c = r.shape[-2]
    n2 = 2 * c
    lane = _iota((1, LANES), 1)
    mh0, mh1 = (lane < RWKV_HEAD).astype(f32), (lane >= RWKV_HEAD).astype(f32)
    same_head = _iota((LANES, LANES), 0) // RWKV_HEAD == _iota((LANES, LANES), 1) // RWKV_HEAD
    g = same_head.astype(bf16)

    def seg(x):
        return _const_right(x, g)

    def stack(x):
        return jnp.concatenate([x * mh0, x * mh1], axis=-2)

    kk = k * kkw
    kk = kk / jnp.maximum(jnp.sqrt(seg(kk * kk)), 1e-12)
    k2 = k * (1.0 + (asig - 1.0) * kaw)
    a = -kk
    b = kk * asig
    cum = _const_left(_tril(c, False).astype(bf16), lw)
    at = stack(a * jnp.exp(cum - lw))
    rt = stack(r * jnp.exp(cum))
    en = jnp.exp(-cum)
    sc = _mm2(jnp.concatenate([at, rt], axis=-2), jnp.concatenate([stack(b * en), stack(k2 * en)], axis=-2), "nt")
    row, col = _iota((n2, n2), 0), _iota((n2, n2), 1)
    same = row // c == col // c
    strict = same & (row % c > col % c)
    incl = same & (row % c >= col % c)
    aab = jnp.where(strict, sc[..., :n2, :n2], 0.0)
    aak = jnp.where(strict, sc[..., :n2, n2:], 0.0)
    arb = jnp.where(incl, sc[..., n2:, :n2], 0.0)
    ark = jnp.where(incl, sc[..., n2:, n2:], 0.0)
    vv = jnp.concatenate([v, v], axis=-2)
    mask_st = jnp.concatenate([jnp.broadcast_to(mh0, (c, LANES)), jnp.broadcast_to(mh1, (c, LANES))], axis=0)
    x_st = _mm2(jnp.concatenate([at, aak], axis=-1), jnp.concatenate([m0, vv], axis=-2), "nn", APPLY_PASSES)
    if tm is None:
        tm = _tri_inv(lax.stop_gradient(aab))
    u_st = _tri_solve(tm, aab, x_st) * mask_st
    o_st = _mm2(jnp.concatenate([rt, arb, ark], axis=-1), jnp.concatenate([m0, u_st, vv], axis=-2), "nn", APPLY_PASSES) * mask_st
    u = u_st[..., :c, :] + u_st[..., c:, :]
    o = o_st[..., :c, :] + o_st[..., c:, :]
    cum_last = _last_row(cum)
    dec_end = jnp.exp(cum_last - cum)
    m_new = _col_of_row(jnp.exp(cum_last)) * m0 + _mm2(
        jnp.concatenate([b * dec_end, k2 * dec_end], axis=-2), jnp.concatenate([u, v], axis=-2), "tn", APPLY_PASSES) * same_head.astype(f32)
    mean = seg(o) * (1.0 / RWKV_HEAD)
    d = o - mean
    var = seg(d * d) * (1.0 / RWKV_HEAD)
    on = d * lax.rsqrt(var + GN_EPS) * gnw + gnb
    bonus = seg(r * k2 * rkw) * v
    return (on + bonus) * _silu(z), m_new, tm


def _split_lanes(a, n):
    return [a[:, i * LANES:(i + 1) * LANES] for i in range(n)]


def _rwkv_step(layer1, prm, y, prev, vf, pp, m0, tm=None):
    xs = _rwkv_pre(layer1, prm, y, prev, vf)
    n_pair = m0.shape[0]
    og, m_new, tm = _rwkv_pair(pp, m0, tuple(jnp.concatenate([p[None] for p in _split_lanes(a, n_pair)], axis=0) for a in xs), tm)
    return og, m_new, xs[2], tm


def _group(n):
    return n


def _rwkv_specs(layer1, t, dr, rwc, n_pair, rev):
    nc = t // CHUNK
    grp = _group(n_pair)

    def cidx(c):
        return (nc - 1 - c) if rev else c

    full = lambda shape: pl.BlockSpec(shape, lambda c, p: tuple(0 for _ in shape))
    specs = [
        pl.BlockSpec((CHUNK, rwc), lambda c, p: (cidx(c), 0)),
        pl.BlockSpec((8, rwc), lambda c, p: (jnp.maximum(cidx(c) * (CHUNK // 8) - 1, 0), 0)),
    ]
    if layer1:
        specs.append(pl.BlockSpec((CHUNK, dr), lambda c, p: (cidx(c), 0)))
    prm_shapes = [(1, rwc), (1, dr), (1, dr), (LANES, dr), (LANES, dr)]
    if layer1:
        prm_shapes += [(1, dr), (dr, LANES), (LANES, dr)]
    specs += [full(s) for s in prm_shapes]
    specs.append(pl.BlockSpec((grp, 8, LANES), lambda c, p: (p, 0, 0)))
    return specs, prm_shapes, cidx, full


def _rwkv_fwd(layer1, proj, vf, prm, pp, cat_width):
    t = proj.shape[0]
    dr = prm[1].shape[1]
    rwc = prm[0].shape[1]
    n_pair = dr // LANES
    nc = t // CHUNK
    n_prm = len(prm)
    specs, _, _, _ = _rwkv_specs(layer1, t, dr, rwc, n_pair, False)

    def body(*refs):
        y_ref, prev_ref = refs[0], refs[1]
        i = 2
        vf_ref = None
        if layer1:
            vf_ref = refs[i]
            i += 1
        prm_refs = refs[i:i + n_prm]
        i += n_prm
        pp_ref = refs[i]
        i += 1
        cat_ref = refs[i]
        i += 1
        vout_ref = None
        if not layer1:
            vout_ref = refs[i]
            i += 1
        mck_ref, m_s = refs[i], refs[i + 1]
        c = pl.program_id(0)

        @pl.when(c == 0)
        def _():
            m_s[...] = jnp.zeros_like(m_s)

        prev = prev_ref[pl.ds(7, 1), :] * (c != 0).astype(f32)
        m0 = m_s[...]
        ppv = tuple(pp_ref[:, pl.ds(q, 1), :] for q in range(5))
        og, m_new, v, tm = _rwkv_step(layer1, tuple(r[...] for r in prm_refs), y_ref[...], prev,
                                      vf_ref[...] if layer1 else None, ppv, m0)
        mck_ref[0, :n_pair] = m0
        mck_ref[0, n_pair:] = tm
        if not layer1:
            vout_ref[...] = v
        for j in range(n_pair):
            cat_ref[:, j * LANES:(j + 1) * LANES] = og[j]
        m_s[...] = m_new

    grp = _group(n_pair)
    assert grp == n_pair
    out_shape = [jax.ShapeDtypeStruct((t, cat_width), f32)]
    out_specs = [pl.BlockSpec((CHUNK, grp * LANES), lambda c, p: (c, p))]
    if not layer1:
        out_shape.append(jax.ShapeDtypeStruct((t, dr), f32))
        out_specs.append(pl.BlockSpec((CHUNK, dr), lambda c, p: (c, 0)))
    out_shape.append(jax.ShapeDtypeStruct((nc, 2 * n_pair, LANES, LANES), f32))
    out_specs.append(pl.BlockSpec((1, 2 * grp, LANES, LANES), lambda c, p: (c, p, 0, 0)))
    args = [proj, proj] + ([vf] if layer1 else []) + list(prm) + [pp]
    return pl.pallas_call(
        body, grid=(nc, 1), in_specs=specs, out_specs=out_specs, out_shape=out_shape,
        scratch_shapes=[pltpu.VMEM((n_pair, LANES, LANES), f32)],
        compiler_params=pltpu.CompilerParams(dimension_semantics=("arbitrary", "arbitrary")),
        name=f"rwkv_fwd_l{int(layer1)}",
    )(*args)


def _rwkv_bwd(layer1, proj, vf, prm, pp, mck, dcat, dvout):
    t = proj.shape[0]
    dr = prm[1].shape[1]
    rwc = prm[0].shape[1]
    n_pair = dr // LANES
    nc = t // CHUNK
    n_prm = len(prm)
    specs, prm_shapes, cidx, full = _rwkv_specs(layer1, t, dr, rwc, n_pair, True)
    grp = _group(n_pair)
    assert grp == n_pair
    specs.append(pl.BlockSpec((1, 2 * grp, LANES, LANES), lambda c, p: (cidx(c), p, 0, 0)))
    specs.append(pl.BlockSpec((CHUNK, grp * LANES), lambda c, p: (cidx(c), p)))
    if not layer1:
        specs.append(pl.BlockSpec((CHUNK, dr), lambda c, p: (cidx(c), 0)))

    def body(*refs):
        y_ref, prev_ref = refs[0], refs[1]
        i = 2
        vf_ref = None
        if layer1:
            vf_ref = refs[i]
            i += 1
        prm_refs = refs[i:i + n_prm]
        i += n_prm
        pp_ref, mck_ref, dog_ref = refs[i], refs[i + 1], refs[i + 2]
        i += 3
        dvout_ref = None
        if not layer1:
            dvout_ref = refs[i]
            i += 1
        dy_ref = refs[i]
        i += 1
        dvf_ref = None
        if layer1:
            dvf_ref = refs[i]
            i += 1
        dprm_refs = refs[i:i + n_prm]
        i += n_prm
        dpp_ref = refs[i]
        dm_s, dprev_s = refs[i + 1:i + 3]
        c = pl.program_id(0)
        cr = nc - 1 - c

        @pl.when(c == 0)
        def _():
            dm_s[...] = jnp.zeros_like(dm_s)
            dprev_s[...] = jnp.zeros_like(dprev_s)
            dpp_ref[...] = jnp.zeros_like(dpp_ref)
            for r in dprm_refs:
                r[...] = jnp.zeros_like(r)

        prev = prev_ref[pl.ds(7, 1), :] * (cr != 0).astype(f32)
        prm_v = tuple(r[...] for r in prm_refs)
        ppv = tuple(pp_ref[:, pl.ds(q, 1), :] for q in range(5))
        dog = jnp.stack([dog_ref[:, j * LANES:(j + 1) * LANES] for j in range(n_pair)], axis=0)
        m0, tm = mck_ref[0, :n_pair], mck_ref[0, n_pair:]
        no_tm = jnp.zeros_like(tm)
        if layer1:
            _, vjp = jax.vjp(lambda a, b, d, e, g, h: _rwkv_step(True, a, b, d, e, g, h, tm),
                             prm_v, y_ref[...], prev, vf_ref[...], ppv, m0)
            dprm, dy, dprev, dvf, dppv, dm0 = vjp((dog, dm_s[...], jnp.zeros((CHUNK, dr), f32), no_tm))
            dvf_ref[...] = dvf
        else:
            _, vjp = jax.vjp(lambda a, b, d, e, g: _rwkv_step(False, a, b, d, None, e, g, tm), prm_v, y_ref[...], prev, ppv, m0)
            dprm, dy, dprev, dppv, dm0 = vjp((dog, dm_s[...], dvout_ref[...], no_tm))
        dm_s[...] = dm0
        for q in range(5):
            dpp_ref[:, pl.ds(q, 1), :] += dppv[q]
        dy_ref[...] = (dy + jnp.where(_iota((CHUNK, 1), 0) == CHUNK - 1, dprev_s[...], 0.0)).astype(bf16)
        dprev_s[...] = dprev
        for r, gval in zip(dprm_refs, dprm):
            r[...] += gval

    out_shape = [jax.ShapeDtypeStruct((t, proj.shape[1]), bf16)]
    out_specs = [pl.BlockSpec((CHUNK, rwc), lambda c, p: (cidx(c), 0))]
    if layer1:
        out_shape.append(jax.ShapeDtypeStruct((t, dr), f32))
        out_specs.append(pl.BlockSpec((CHUNK, dr), lambda c, p: (cidx(c), 0)))
    out_shape += [jax.ShapeDtypeStruct(s, f32) for s in prm_shapes]
    out_specs += [full(s) for s in prm_shapes]
    out_shape.append(jax.ShapeDtypeStruct((n_pair, 8, LANES), f32))
    out_specs.append(full((n_pair, 8, LANES)))
    args = [proj, proj] + ([vf] if layer1 else []) + list(prm) + [pp, mck, dcat] + ([] if layer1 else [dvout])
    return pl.pallas_call(
        body, grid=(nc, 1), in_specs=specs, out_specs=out_specs, out_shape=out_shape,
        scratch_shapes=[pltpu.VMEM((n_pair, LANES, LANES), f32), pltpu.VMEM((1, rwc), f32)],
        compiler_params=pltpu.CompilerParams(dimension_semantics=("arbitrary", "arbitrary")),
        name=f"rwkv_bwd_l{int(layer1)}",
    )(*args)


def _hgrn_chunk(layer1, lbl, gnw, s0, q_raw, f_raw, i_in, z):
    c = q_raw.shape[-2]
    q = _silu(q_raw)
    ls = _log_sigmoid(f_raw)
    if layer1:
        l0, l1 = lbl[..., 0:1, :], lbl[..., 1:2, :]
        mx = jnp.maximum(l0, l1)
        e0, e1 = jnp.exp(l0 - mx), jnp.exp(l1 - mx)
        sm0, sm1 = e0 / (e0 + e1), e1 / (e0 + e1)
        lb = (sm0 + sm1) - sm0
        log_f = _logaddexp(jnp.log(jnp.maximum(lb, LB_FLOOR)), jnp.log1p(-lb) + ls)
        k = (1.0 - lb) * jax.nn.sigmoid(-f_raw)
    else:
        log_f = _logaddexp(jnp.full_like(ls, jnp.log(jnp.float32(LB_FLOOR))), ls)
        k = jax.nn.sigmoid(-f_raw)
    row, col = _iota((c, c), 0), _iota((c, c), 1)
    trow = _iota((c, 1), 0)
    halves = []
    half = c // 2
    while half >= 1:
        halves.append(half)
        half //= 2
    cmat = jnp.concatenate([(col <= row).astype(f32)]
                           + [(col <= (row // (2 * hf)) * (2 * hf) + hf - 1).astype(f32) for hf in halves], axis=0)
    ball = _const_left(cmat.astype(bf16), log_f)
    b = ball[..., :c, :]
    att = None
    for lvl, hf in enumerate(halves):
        blk = 2 * hf
        bref = ball[..., (lvl + 1) * c:(lvl + 2) * c, :]
        upper = (trow % blk) >= hf
        dec = jnp.exp(jnp.where(upper, b - bref, bref - b))
        qh = jnp.where(upper, q * dec, 0.0)
        kh = jnp.where(upper, 0.0, k * dec)
        term = jnp.where(row // blk == col // blk, _mm2(qh, kh, "nt", APPLY_PASSES), 0.0)
        att = term if att is None else att + term
    lhs = jnp.concatenate([q * jnp.exp(b), att, jnp.zeros(att.shape[:-1] + (LANES - c,), f32)], axis=-1)
    rhs = jnp.concatenate([s0, i_in, jnp.zeros(i_in.shape[:-2] + (LANES - c, i_in.shape[-1]), f32)], axis=-2)
    o = _mm2(lhs, rhs, "nn", APPLY_PASSES) + jnp.sum(q * k, axis=-1, keepdims=True) * i_in
    b_last = _last_row(b)
    s_new = _col_of_row(jnp.exp(b_last)) * s0 + _mm2(k * jnp.exp(b_last - b), i_in, "tn", APPLY_PASSES)
    o = o * lax.rsqrt(jnp.mean(o * o, axis=-1, keepdims=True) + RMS_EPS)
    return o * gnw * _silu(z), s_new


def _hgrn_in_specs(t, dh, col0, rev):
    nc = t // CHUNK
    nh = dh // LANES

    def cidx(c):
        return (nc - 1 - c) if rev else c

    grp = _group(nh)
    specs = [pl.BlockSpec((CHUNK, LANES), functools.partial(lambda g, j, h, c: (cidx(c), col0 + g * nh + h * grp + j), g, j))
             for j in range(grp) for g in range(4)]
    specs.append(pl.BlockSpec((2, grp * LANES), lambda h, c: (0, h)))
    specs.append(pl.BlockSpec((1, grp * LANES), lambda h, c: (0, h)))
    return specs, cidx, grp


def _hgrn_fwd(layer1, proj, lbl, gnw, cat, rwc):
    t, d = cat.shape
    dh = gnw.shape[1]
    nh = dh // LANES
    nc = t // CHUNK
    col0 = rwc // LANES
    specs, _, grp = _hgrn_in_specs(t, dh, col0, False)
    specs.append(pl.BlockSpec(memory_space=pl.ANY))
    assert (d - dh) % (grp * LANES) == 0
    cat_col0 = (d - dh) // (grp * LANES)

    def body(*refs):
        x_refs = refs[:4 * grp]
        lbl_ref, gnw_ref, _, cat_ref, sck_ref, s_s = refs[4 * grp:]
        c = pl.program_id(1)

        @pl.when(c == 0)
        def _():
            s_s[...] = jnp.zeros_like(s_s)

        lanes = [slice(j * LANES, (j + 1) * LANES) for j in range(grp)]
        s0 = s_s[...]
        sck_ref[:, 0] = s0
        out, s_new = _hgrn_chunk(layer1, jnp.stack([lbl_ref[:, ln] for ln in lanes]), jnp.stack([gnw_ref[:, ln] for ln in lanes]),
                                 s0, *(jnp.stack([x_refs[4 * j + g][...] for j in range(grp)]) for g in range(4)))
        for j in range(grp):
            cat_ref[:, lanes[j]] = out[j]
        s_s[...] = s_new

    return pl.pallas_call(
        body, grid=(nh // grp, nc), in_specs=specs,
        out_specs=[pl.BlockSpec((CHUNK, grp * LANES), lambda h, c: (c, cat_col0 + h)),
                   pl.BlockSpec((grp, 1, LANES, LANES), lambda h, c: (h, c, 0, 0))],
        out_shape=[jax.ShapeDtypeStruct((t, d), f32), jax.ShapeDtypeStruct((nh, nc, LANES, LANES), f32)],
        scratch_shapes=[pltpu.VMEM((grp, LANES, LANES), f32)],
        input_output_aliases={4 * grp + 2: 0},
        compiler_params=pltpu.CompilerParams(dimension_semantics=("arbitrary", "arbitrary")),
        name=f"hgrn_fwd_l{int(layer1)}",
    )(*([proj] * (4 * grp)), lbl, gnw, cat)


def _hgrn_bwd(layer1, proj, lbl, gnw, sck, dcat, rwc, dproj):
    t, d = dcat.shape
    dh = gnw.shape[1]
    nh = dh // LANES
    nc = t // CHUNK
    col0 = rwc // LANES
    specs, cidx, grp = _hgrn_in_specs(t, dh, col0, True)
    assert grp == nh and (d - dh) % (grp * LANES) == 0
    cat_col0 = (d - dh) // (grp * LANES)
    specs.append(pl.BlockSpec((grp, 1, LANES, LANES), lambda h, c: (h, cidx(c), 0, 0)))
    specs.append(pl.BlockSpec((CHUNK, grp * LANES), lambda h, c: (cidx(c), cat_col0 + h)))
    specs.append(pl.BlockSpec(memory_space=pl.ANY))

    def body(*refs):
        x_refs = refs[:4 * grp]
        lbl_ref, gnw_ref, sck_ref, do_ref, _, dp_hbm, dlbl_ref, dgnw_ref, ds_s, stage, sems = refs[4 * grp:]
        c = pl.program_id(1)
        slot = c % 2

        def put(s, g, chunk):
            return pltpu.make_async_copy(stage.at[s, g], dp_hbm.at[pl.ds(chunk * CHUNK, CHUNK), pl.ds(rwc + g * dh, dh)],
                                         sems.at[s, g])

        @pl.when(c == 0)
        def _():
            ds_s[...] = jnp.zeros_like(ds_s)
            dlbl_ref[...] = jnp.zeros_like(dlbl_ref)
            dgnw_ref[...] = jnp.zeros_like(dgnw_ref)

        @pl.when(c >= 2)
        def _():
            for g in range(4):
                put(slot, g, 0).wait()

        lanes = [slice(j * LANES, (j + 1) * LANES) for j in range(grp)]
        _, vjp = jax.vjp(functools.partial(_hgrn_chunk, layer1),
                         jnp.stack([lbl_ref[:, ln] for ln in lanes]), jnp.stack([gnw_ref[:, ln] for ln in lanes]), sck_ref[:, 0],
                         *(jnp.stack([x_refs[4 * j + g][...] for j in range(grp)]) for g in range(4)))
        dlbl, dgnw, ds0, dq, df, di, dz = vjp((jnp.stack([do_ref[:, ln] for ln in lanes]), ds_s[...]))
        ds_s[...] = ds0
        for j in range(grp):
            dlbl_ref[:, lanes[j]] += dlbl[j]
            dgnw_ref[:, lanes[j]] += dgnw[j]
            for g, val in enumerate((dq, df, di, dz)):
                stage[slot, g, :, lanes[j]] = val[j].astype(bf16)
        for g in range(4):
            put(slot, g, nc - 1 - c).start()

        @pl.when(c == nc - 1)
        def _():
            for g in range(4):
                put(slot, g, 0).wait()
                if nc >= 2:
                    put(1 - slot, g, 0).wait()

    return pl.pallas_call(
        body, grid=(1, nc), in_specs=specs,
        out_specs=[pl.BlockSpec(memory_space=pl.ANY),
                   pl.BlockSpec((2, grp * LANES), lambda h, c: (0, h)),
                   pl.BlockSpec((1, grp * LANES), lambda h, c: (0, h))],
        out_shape=[jax.ShapeDtypeStruct(dproj.shape, dproj.dtype), jax.ShapeDtypeStruct((2, dh), f32),
                   jax.ShapeDtypeStruct((1, dh), f32)],
        scratch_shapes=[pltpu.VMEM((grp, LANES, LANES), f32), pltpu.VMEM((2, 4, CHUNK, dh), bf16),
                        pltpu.SemaphoreType.DMA((2, 4))],
        input_output_aliases={4 * grp + 4: 0},
        compiler_params=pltpu.CompilerParams(dimension_semantics=("arbitrary", "arbitrary")),
        name=f"hgrn_bwd_l{int(layer1)}",
    )(*([proj] * (4 * grp)), lbl, gnw, sck, dcat, dproj)


def _ln(h, y, w, b):
    u = ALPHA * h + y
    mu = jnp.mean(u, axis=-1, keepdims=True)
    var = jnp.mean(jnp.square(u - mu), axis=-1, keepdims=True)
    return (u - mu) * lax.rsqrt(var + LN_EPS) * w + b


def _row_tile(t):
    return 256 if t % 256 == 0 else t


def _ln_fwd(h, y, w, b):
    t, d = h.shape
    tr = _row_tile(t)

    def body(h_ref, y_ref, w_ref, b_ref, o_ref, o16_ref):
        out = _ln(h_ref[...], y_ref[...], w_ref[...], b_ref[...])
        o_ref[...] = out
        o16_ref[...] = out.astype(bf16)

    row = pl.BlockSpec((tr, d), lambda i: (i, 0))
    vec = pl.BlockSpec((1, d), lambda i: (0, 0))
    return pl.pallas_call(body, grid=(t // tr,), in_specs=[row, row, vec, vec], out_specs=[row, row],
                          out_shape=[jax.ShapeDtypeStruct((t, d), f32), jax.ShapeDtypeStruct((t, d), bf16)],
                          name="ln_fwd")(h, y, w, b)


def _ln_loss_bwd(h, y, w, b, tgt):
    t, d = h.shape
    tr = _row_tile(t)

    def body(h_ref, y_ref, w_ref, b_ref, t_ref, dy_ref, dy16_ref, dw_ref, db_ref, loss_ref):
        @pl.when(pl.program_id(0) == 0)
        def _():
            dw_ref[...] = jnp.zeros_like(dw_ref)
            db_ref[...] = jnp.zeros_like(db_ref)
            loss_ref[...] = jnp.zeros_like(loss_ref)

        out, vjp = jax.vjp(lambda yy, ww, bb: _ln(h_ref[...], yy, ww, bb), y_ref[...], w_ref[...], b_ref[...])
        err = out - t_ref[...]
        loss_ref[...] += 0.5 * jnp.sum(jnp.mean(jnp.square(err), axis=-1, keepdims=True), axis=0, keepdims=True)
        dy, dw, db = vjp(err * (1.0 / d))
        dy_ref[...] = dy
        dy16_ref[...] = dy.astype(bf16)
        dw_ref[...] += dw
        db_ref[...] += db

    row = pl.BlockSpec((tr, d), lambda i: (i, 0))
    vec = pl.BlockSpec((1, d), lambda i: (0, 0))
    return pl.pallas_call(
        body, grid=(t // tr,), in_specs=[row, row, vec, vec, row],
        out_specs=[row, row, vec, vec, pl.BlockSpec((1, LANES), lambda i: (0, 0))],
        out_shape=[jax.ShapeDtypeStruct((t, d), f32), jax.ShapeDtypeStruct((t, d), bf16), jax.ShapeDtypeStruct((1, d), f32),
                   jax.ShapeDtypeStruct((1, d), f32), jax.ShapeDtypeStruct((1, LANES), f32)],
        compiler_params=pltpu.CompilerParams(dimension_semantics=("arbitrary",)), name="ln_loss_bwd")(h, y, w, b, tgt)


def _ln_bwd(h, y, w, b, dout):
    t, d = h.shape
    tr = _row_tile(t)

    def body(h_ref, y_ref, w_ref, b_ref, do_ref, dy_ref, dy16_ref, dw_ref, db_ref):
        @pl.when(pl.program_id(0) == 0)
        def _():
            dw_ref[...] = jnp.zeros_like(dw_ref)
            db_ref[...] = jnp.zeros_like(db_ref)

        _, vjp = jax.vjp(lambda yy, ww, bb: _ln(h_ref[...], yy, ww, bb), y_ref[...], w_ref[...], b_ref[...])
        dy, dw, db = vjp(do_ref[...])
        dy_ref[...] = dy
        dy16_ref[...] = dy.astype(bf16)
        dw_ref[...] += dw
        db_ref[...] += db

    row = pl.BlockSpec((tr, d), lambda i: (i, 0))
    vec = pl.BlockSpec((1, d), lambda i: (0, 0))
    return pl.pallas_call(
        body, grid=(t // tr,), in_specs=[row, row, vec, vec, row], out_specs=[row, row, vec, vec],
        out_shape=[jax.ShapeDtypeStruct((t, d), f32), jax.ShapeDtypeStruct((t, d), bf16),
                   jax.ShapeDtypeStruct((1, d), f32), jax.ShapeDtypeStruct((1, d), f32)],
        compiler_params=pltpu.CompilerParams(dimension_semantics=("arbitrary",)), name="ln_bwd")(h, y, w, b, dout)


def _pick(n, prefs):
    for p in prefs:
        if n % p == 0:
            return p
    return n


def _tile(n, want):
    if n <= want:
        return n
    for cand in range(want - want % LANES, 0, -LANES):
        if n % cand == 0:
            return cand
    return n


_MM_TILES = {"proj": (1024, 1664, 2048), "out": (1024, 1024, 2048), "dcat": (1024, 1024, 2048),
             "dwout": (512, 2048, 2048), "dwin": (640, 2048, 2048), "dh": (1024, 1024, 1664)}


def _matmul(a, b, mode, name, tiles, add=None, add_scale=1.0, out_dtype=f32, after=None):
    if mode == "nn":
        (m, k), n = a.shape, b.shape[1]
    elif mode == "nt":
        (m, k), n = a.shape, b.shape[0]
    else:
        (k, m), n = a.shape, b.shape[1]
    tm, tn, tk = _tile(m, tiles[0]), _tile(n, tiles[1]), _tile(k, tiles[2])
    nk = k // tk
    cache_a = nk == 1 and a.dtype != bf16 and n // tn > 1

    def body(*refs):
        a_ref, b_ref = refs[0], refs[1]
        add_ref = refs[2] if add is not None else None
        n_in = 2 + (add is not None) + (after is not None)
        o_ref = refs[n_in]
        scratch = refs[n_in + 1:]

        def finish(res):
            if add is not None:
                res = res + add_scale * add_ref[...]
            o_ref[...] = res.astype(out_dtype)

        if cache_a:
            a_bf = scratch[0]

            @pl.when(pl.program_id(1) == 0)
            def _():
                a_bf[...] = a_ref[...].astype(bf16)

            a_val = a_bf[...]
        else:
            a_val = a_ref[...].astype(bf16)
        prod = lax.dot_general(a_val, b_ref[...].astype(bf16), _DIMS[mode], preferred_element_type=f32)
        if nk == 1:
            finish(prod)
        else:
            acc = scratch[-1]
            kk = pl.program_id(2)

            @pl.when(kk == 0)
            def _():
                acc[...] = prod

            @pl.when(kk != 0)
            def _():
                acc[...] += prod

            @pl.when(kk == nk - 1)
            def _():
                finish(acc[...])

    a_shape = (tk, tm) if mode == "tn" else (tm, tk)
    a_spec = pl.BlockSpec(a_shape, (lambda i, j, kk: (kk, i)) if mode == "tn" else (lambda i, j, kk: (i, kk)))
    b_spec = pl.BlockSpec((tn, tk), lambda i, j, kk: (j, kk)) if mode == "nt" else pl.BlockSpec((tk, tn), lambda i, j, kk: (kk, j))
    o_spec = pl.BlockSpec((tm, tn), lambda i, j, kk: (i, j))
    in_specs = [a_spec, b_spec] + ([o_spec] if add is not None else []) + ([pl.BlockSpec(memory_space=pl.ANY)] if after is not None else [])
    args = [a, b] + ([add] if add is not None else []) + ([after] if after is not None else [])
    scratch_shapes = ([pltpu.VMEM(a_shape, bf16)] if cache_a else []) + ([pltpu.VMEM((tm, tn), f32)] if nk > 1 else [])
    return pl.pallas_call(
        body, grid=(m // tm, n // tn, nk), in_specs=in_specs, out_specs=o_spec,
        out_shape=jax.ShapeDtypeStruct((m, n), out_dtype), scratch_shapes=scratch_shapes,
        compiler_params=pltpu.CompilerParams(dimension_semantics=("parallel", "arbitrary", "arbitrary")),
        name=name,
    )(*args)


def _position():
    return lax.axis_index("x"), lax.axis_index("y"), lax.axis_index("c")


def _flip(pos, k):
    x, y, c = pos
    return (1 - x if k & 4 else x, 1 - y if k & 2 else y, 1 - c if k & 1 else c)


def _index(pos):
    return 4 * pos[0] + 2 * pos[1] + pos[2]


def _all_gather_rows(xs, name):
    n_arr = len(xs)
    chips = (2, 4, 6)

    def body(*refs):
        x_refs, out_refs = refs[:n_arr], refs[n_arr:2 * n_arr]
        send_sems, recv_sems, local_sems = refs[2 * n_arr:]
        me = _position()
        sibling = _flip(me, 1)

        def copy(i, sem, block, to, own=False):
            m_per = x_refs[i].shape[0]
            rows = out_refs[i].at[pl.ds(_index(block) * m_per, m_per), :]
            return pltpu.make_async_remote_copy(
                src_ref=x_refs[i] if own else rows, dst_ref=rows,
                send_sem=send_sems.at[7 * i + sem], recv_sem=recv_sems.at[7 * i + sem], device_id=to, device_id_type=MESH)

        mine = [pltpu.make_async_copy(x_refs[i], out_refs[i].at[pl.ds(_index(me) * x_refs[i].shape[0], x_refs[i].shape[0]), :],
                                      local_sems.at[i]) for i in range(n_arr)]
        first, passed = [], []
        for i in range(n_arr):
            first.append(copy(i, 0, me, sibling, own=True))
            first += [copy(i, 1 + j, me, _flip(me, k), own=True) for j, k in enumerate(chips)]
            passed.append([copy(i, 4 + j, _flip(me, k), sibling) for j, k in enumerate(chips)])
        for cp in mine + first:
            cp.start()
        for i in range(n_arr):
            for j, k in enumerate(chips):
                copy(i, 1 + j, _flip(me, k), me).wait_recv()
                passed[i][j].start()
        for i in range(n_arr):
            copy(i, 0, sibling, me).wait_recv()
            for j, k in enumerate(chips):
                copy(i, 4 + j, _flip(sibling, k), me).wait_recv()
        for cp in first + [cp for group in passed for cp in group]:
            cp.wait_send()
        for cp in mine:
            cp.wait()

    anyspec = pl.BlockSpec(memory_space=pl.ANY)
    return pl.pallas_call(
        body, out_shape=[jax.ShapeDtypeStruct((N_DEV * x.shape[0], x.shape[1]), x.dtype) for x in xs],
        in_specs=[anyspec] * n_arr, out_specs=[anyspec] * n_arr,
        scratch_shapes=[pltpu.SemaphoreType.DMA((7 * n_arr,)), pltpu.SemaphoreType.DMA((7 * n_arr,)),
                        pltpu.SemaphoreType.DMA((n_arr,))],
        name=name,
    )(*xs)


def _split_start(srcs, lands, plan, n_copies, name, after=()):
    n_arr = len(srcs)
    n_after = len(after)
    hbm = pl.BlockSpec(memory_space=pltpu.HBM)
    sem = pl.BlockSpec(memory_space=pltpu.SEMAPHORE)

    def body(*refs):
        src_refs, land_refs = refs[:n_arr], refs[n_arr:2 * n_arr]
        outs_at = 2 * n_arr + n_after
        send_sems, recv_sems = refs[outs_at:outs_at + n_arr], refs[outs_at + n_arr:outs_at + 2 * n_arr]
        token = refs[-1]
        me = _position()
        for i in range(n_arr):
            for j, (src, dst, peer, _) in enumerate(plan(i, src_refs[i], land_refs[i], me)):
                pltpu.make_async_remote_copy(src_ref=src, dst_ref=dst, send_sem=send_sems[i].at[j], recv_sem=recv_sems[i].at[j],
                                             device_id=peer, device_id_type=MESH).start()
        token[...] = jnp.zeros_like(token)

    outs = pl.pallas_call(
        body, name=name,
        out_shape=([pltpu.SemaphoreType.DMA((n_copies,))] * (2 * n_arr)
                   + [pltpu.HBM(a.shape, a.dtype) for a in list(srcs) + list(lands)]
                   + [jax.ShapeDtypeStruct((8, LANES), f32)]),
        in_specs=[hbm] * (2 * n_arr) + [pl.BlockSpec(memory_space=pl.ANY)] * n_after,
        out_specs=[sem] * (2 * n_arr) + [hbm] * (2 * n_arr) + [pl.BlockSpec(memory_space=pltpu.VMEM)],
        input_output_aliases={i: 2 * n_arr + i for i in range(2 * n_arr)},
        compiler_params=pltpu.CompilerParams(has_side_effects=pltpu.SideEffectType.DATAFLOW_SIDE_EFFECTING),
    )(*[pltpu.with_memory_space_constraint(a, pltpu.HBM) for a in list(srcs) + list(lands)], *after)
    return (outs[:n_arr], outs[n_arr:2 * n_arr], outs[2 * n_arr:3 * n_arr], outs[3 * n_arr:4 * n_arr], outs[-1])


def _split_wait(started, plan, after, name):
    send_sems, recv_sems, srcs, lands, _ = started
    n_arr = len(srcs)
    hbm = pl.BlockSpec(memory_space=pltpu.HBM)
    sem = pl.BlockSpec(memory_space=pltpu.SEMAPHORE)

    def body(*refs):
        src_refs, land_refs = refs[:n_arr], refs[n_arr:2 * n_arr]
        s_sems, r_sems = refs[2 * n_arr:3 * n_arr], refs[3 * n_arr:4 * n_arr]
        me = _position()
        for i in range(n_arr):
            for j, (src, _, peer, arrival) in enumerate(plan(i, src_refs[i], land_refs[i], me)):
                cp = pltpu.make_async_remote_copy(src_ref=src, dst_ref=arrival, send_sem=s_sems[i].at[j], recv_sem=r_sems[i].at[j],
                                                  device_id=peer, device_id_type=MESH)
                cp.wait_send()
                cp.wait_recv()

    outs = pl.pallas_call(
        body, name=name,
        out_shape=[pltpu.HBM(a.shape, a.dtype) for a in list(srcs) + list(lands)],
        in_specs=[hbm] * (2 * n_arr) + [sem] * (2 * n_arr) + [pl.BlockSpec(memory_space=pl.ANY)],
        out_specs=[hbm] * (2 * n_arr),
        input_output_aliases={i: i for i in range(2 * n_arr)},
        compiler_params=pltpu.CompilerParams(has_side_effects=pltpu.SideEffectType.DATAFLOW_SIDE_EFFECTING),
    )(*srcs, *lands, *send_sems, *recv_sems, after)
    return outs[:n_arr], outs[n_arr:]


_GATHER_FLIPS = (1, 2, 4, 6)


def _gather_plan(i, src_ref, land_ref, me):
    m = src_ref.shape[0]

    def rows(pos):
        return land_ref.at[pl.ds(_index(pos) * m, m), :]

    return [(src_ref, rows(me), _flip(me, k), rows(_flip(me, k))) for k in _GATHER_FLIPS]


def _gather_forward(lands, name):
    n_arr = len(lands)
    chips = (2, 4, 6)

    def body(*refs):
        out_refs = refs[n_arr:2 * n_arr]
        send_sems, recv_sems = refs[2 * n_arr:]
        me = _position()
        sibling = _flip(me, 1)
        sends, arrivals = [], []
        for i, out_ref in enumerate(out_refs):
            m = out_ref.shape[0] // N_DEV

            def copy(pos, j):
                blk = out_ref.at[pl.ds(_index(pos) * m, m), :]
                return pltpu.make_async_remote_copy(src_ref=blk, dst_ref=blk, send_sem=send_sems.at[3 * i + j],
                                                    recv_sem=recv_sems.at[3 * i + j], device_id=sibling, device_id_type=MESH)

            for j, k in enumerate(chips):
                sends.append(copy(_flip(me, k), j))
                arrivals.append(copy(_flip(sibling, k), j))
        for cp in sends:
            cp.start()
        for cp in arrivals:
            cp.wait_recv()
        for cp in sends:
            cp.wait_send()

    anyspec = pl.BlockSpec(memory_space=pl.ANY)
    return pl.pallas_call(
        body, out_shape=[jax.ShapeDtypeStruct(a.shape, a.dtype) for a in lands],
        in_specs=[anyspec] * n_arr, out_specs=[anyspec] * n_arr, input_output_aliases={i: i for i in range(n_arr)},
        scratch_shapes=[pltpu.SemaphoreType.DMA((3 * n_arr,))] * 2, name=name,
    )(*lands)


def _chips_plan(i, src_ref, land_ref, me):
    m = src_ref.shape[0] // 4
    plan = []
    for j, k in enumerate((2, 4, 6)):
        peer = _flip(me, k)
        plan.append((src_ref.at[pl.ds((2 * peer[0] + peer[1]) * m, m), :], land_ref.at[j], peer, land_ref.at[j]))
    return plan


def _sibling_plan(i, src_ref, land_ref, me):
    m = src_ref.shape[0] // N_DEV
    sibling = _flip(me, 1)
    return [(src_ref.at[pl.ds((2 * q + 1 - me[2]) * m, m), :], land_ref.at[q], sibling, land_ref.at[q]) for q in range(4)]


def _sum_with_sibling(g, recv, name):
    m = g.shape[0] // N_DEV
    n = g.shape[1]
    tr = _pick(m, (208, 128, 64, 32, 16))
    nt = m // tr

    def body(g_ref, r_ref, o_ref):
        c = lax.axis_index("c")
        own = jnp.where(c == 0, g_ref[0, 0].astype(f32), g_ref[0, 1].astype(f32))
        o_ref[...] = (own + r_ref[0].astype(f32)).astype(o_ref.dtype)

    return pl.pallas_call(
        body, grid=(4, nt),
        in_specs=[pl.BlockSpec((1, 2, tr, n), lambda q, i: (q, 0, i, 0)), pl.BlockSpec((1, tr, n), lambda q, i: (q, i, 0))],
        out_specs=pl.BlockSpec((tr, n), lambda q, i: (q * nt + i, 0)),
        out_shape=jax.ShapeDtypeStruct((4 * m, n), bf16), name=name,
    )(g.reshape(4, 2, m, n), recv)


def _sum_with_chips(h, recv, name):
    m = h.shape[0] // 4
    n = h.shape[1]
    tr = _pick(m, (208, 128, 64, 32, 16))

    def body(h_ref, r_ref, o_ref):
        my_q = 2 * lax.axis_index("x") + lax.axis_index("y")
        own = h_ref[0].astype(f32)
        for q in range(1, 4):
            own = jnp.where(my_q == q, h_ref[q].astype(f32), own)
        o_ref[...] = ((own + r_ref[0].astype(f32)) + r_ref[1].astype(f32)) + r_ref[2].astype(f32)

    return pl.pallas_call(
        body, grid=(m // tr,),
        in_specs=[pl.BlockSpec((4, tr, n), lambda i: (0, i, 0)), pl.BlockSpec((3, tr, n), lambda i: (0, i, 0))],
        out_specs=pl.BlockSpec((tr, n), lambda i: (i, 0)), out_shape=jax.ShapeDtypeStruct((m, n), f32), name=name,
    )(h.reshape(4, m, n), recv)


def _sum_slots(parts, name):
    n_slot, m, n = parts.shape
    tr = _pick(m, (208, 128, 64, 32, 16, 8))

    def body(p_ref, o_ref):
        acc = p_ref[0]
        for s in range(1, n_slot):
            acc = acc + p_ref[s]
        o_ref[...] = acc

    return pl.pallas_call(
        body, grid=(m // tr,), in_specs=[pl.BlockSpec((n_slot, tr, n), lambda i: (0, i, 0))],
        out_specs=pl.BlockSpec((tr, n), lambda i: (i, 0)), out_shape=jax.ShapeDtypeStruct((m, n), parts.dtype), name=name,
    )(parts)


def _reduce_scatter_begin(gs, name):
    lands = [lax.empty((4, g.shape[0] // N_DEV, g.shape[1]), g.dtype) for g in gs]
    return _split_start(gs, lands, _sibling_plan, 4, "rs_d2d_start_" + name)


def _reduce_scatter_middle(started, after, name):
    gs, from_sibling = _split_wait(started, _sibling_plan, after, "rs_d2d_wait_" + name)
    chip_sums = [_sum_with_sibling(g, r, f"rs_sum2_{name}_{i}") for i, (g, r) in enumerate(zip(gs, from_sibling))]
    lands = [lax.empty((3, h.shape[0] // 4, h.shape[1]), h.dtype) for h in chip_sums]
    return _split_start(chip_sums, lands, _chips_plan, 3, "rs_ici_start_" + name)


def _reduce_scatter_end(started, after, name):
    chip_sums, from_chips = _split_wait(started, _chips_plan, after, "rs_ici_wait_" + name)
    return [_sum_with_chips(h, r, f"rs_sum4_{name}_{i}") for i, (h, r) in enumerate(zip(chip_sums, from_chips))]


def _adamw_update(w, g, m, v):
    mm = ADAM_B1 * m + (1.0 - ADAM_B1) * g
    vv = ADAM_B2 * v + (1.0 - ADAM_B2) * jnp.square(g)
    m_hat = mm / (1.0 - ADAM_B1 ** ADAM_STEP)
    v_hat = vv / (1.0 - ADAM_B2 ** ADAM_STEP)
    return -ADAM_LR * (m_hat / (jnp.sqrt(v_hat) + ADAM_EPS) + ADAM_WD * w), mm, vv


def _adamw_many(ws, gs, ms, vs, name):
    k = len(ws)
    shapes = [w.shape for w in ws]
    flat = [[a.reshape(-1, a.shape[-1]) for a in group] for group in (ws, gs, ms, vs)]

    def body(*refs):
        for i in range(k):
            d, mm, vv = _adamw_update(*(refs[j * k + i][...] for j in range(4)))
            refs[4 * k + i][...] = d
            refs[5 * k + i][...] = mm
            refs[6 * k + i][...] = vv

    outs = pl.pallas_call(
        body, out_shape=[jax.ShapeDtypeStruct(a.shape, f32) for a in flat[0]] * 3, name=name,
    )(*flat[0], *flat[1], *flat[2], *flat[3])
    return tuple([outs[j * k + i].reshape(shapes[i]) for i in range(k)] for j in range(3))


def _adamw(w, g, m, v, name):
    shape = w.shape
    n = shape[-1]
    r = w.size // n
    w2, g2, m2, v2 = (a.reshape(r, n) for a in (w, g, m, v))
    tr = _pick(r, (256, 208, 128, 64, 32, 16, 8))

    def body(w_ref, g_ref, m_ref, v_ref, d_ref, mo_ref, vo_ref):
        d_ref[...], mo_ref[...], vo_ref[...] = _adamw_update(w_ref[...], g_ref[...], m_ref[...], v_ref[...])

    spec = pl.BlockSpec((tr, n), lambda i: (i, 0))
    outs = pl.pallas_call(
        body, grid=(r // tr,), in_specs=[spec] * 4, out_specs=[spec] * 3,
        out_shape=[jax.ShapeDtypeStruct((r, n), f32)] * 3, name=name,
    )(w2, g2, m2, v2)
    return tuple(o.reshape(shape) for o in outs)


_SMALL = ("shift_mu", "w_decay0", "a0", "k_k", "k_a", "r_k", "ln_x_w", "ln_x_b", "v_mix0", "lb_logits",
          "g_norm_w", "ln_w", "ln_b")
_NAMES = ("w_in", "shift_mu", "w_decay0", "w_decay_up", "a0", "a_up", "k_k", "k_a", "r_k", "ln_x_w", "ln_x_b",
          "v_mix0", "v_mix_down", "v_mix_up", "lb_logits", "g_norm_w", "w_out", "ln_w", "ln_b")


def _pad_rows(a, rows, at_end):
    z = jnp.zeros((rows - a.shape[0], a.shape[1]), a.dtype)
    return jnp.concatenate([a, z] if at_end else [z, a], axis=0)


def kernel(x, w_in, shift_mu, w_decay0, w_decay_up, a0, a_up, k_k, k_a, r_k, ln_x_w, ln_x_b, v_mix0, v_mix_down, v_mix_up, lb_logits, g_norm_w, w_out, ln_w, ln_b, loss_target, m_w_in, m_shift_mu, m_w_decay0, m_w_decay_up, m_a0, m_a_up, m_k_k, m_k_a, m_r_k, m_ln_x_w, m_ln_x_b, m_v_mix0, m_v_mix_down, m_v_mix_up, m_lb_logits, m_g_norm_w, m_w_out, m_ln_w, m_ln_b, v_w_in, v_shift_mu, v_w_decay0, v_w_decay_up, v_a0, v_a_up, v_k_k, v_k_a, v_r_k, v_ln_x_w, v_ln_x_b, v_v_mix0, v_v_mix_down, v_v_mix_up, v_lb_logits, v_g_norm_w, v_w_out, v_ln_w, v_ln_b):
    weights = dict(w_in=w_in, shift_mu=shift_mu, w_decay0=w_decay0, w_decay_up=w_decay_up, a0=a0, a_up=a_up, k_k=k_k,
                   k_a=k_a, r_k=r_k, ln_x_w=ln_x_w, ln_x_b=ln_x_b, v_mix0=v_mix0, v_mix_down=v_mix_down,
                   v_mix_up=v_mix_up, lb_logits=lb_logits, g_norm_w=g_norm_w, w_out=w_out, ln_w=ln_w, ln_b=ln_b)
    mom1 = dict(w_in=m_w_in, shift_mu=m_shift_mu, w_decay0=m_w_decay0, w_decay_up=m_w_decay_up, a0=m_a0, a_up=m_a_up,
                k_k=m_k_k, k_a=m_k_a, r_k=m_r_k, ln_x_w=m_ln_x_w, ln_x_b=m_ln_x_b, v_mix0=m_v_mix0,
                v_mix_down=m_v_mix_down, v_mix_up=m_v_mix_up, lb_logits=m_lb_logits, g_norm_w=m_g_norm_w,
                w_out=m_w_out, ln_w=m_ln_w, ln_b=m_ln_b)
    mom2 = dict(w_in=v_w_in, shift_mu=v_shift_mu, w_decay0=v_w_decay0, w_decay_up=v_w_decay_up, a0=v_a0, a_up=v_a_up,
                k_k=v_k_k, k_a=v_k_a, r_k=v_r_k, ln_x_w=v_ln_x_w, ln_x_b=v_ln_x_b, v_mix0=v_v_mix0,
                v_mix_down=v_v_mix_down, v_mix_up=v_v_mix_up, lb_logits=v_lb_logits, g_norm_w=v_g_norm_w,
                w_out=v_w_out, ln_w=v_ln_w, ln_b=v_ln_b)
    assert x.shape[0] == 1 and w_in.shape[0] == DEPTH
    t, d = x.shape[1], x.shape[2]
    dr = w_decay0.shape[1]
    dh = g_norm_w.shape[1]
    rank_w, rank_a, rank_v = w_decay_up.shape[1], a_up.shape[1], v_mix_up.shape[1]
    rwc = 4 * dr + rank_w + rank_a
    assert rank_w + rank_a == LANES and rank_v <= LANES and dr + dh == d
    assert t % CHUNK == 0 and dr % LANES == 0 and dh % LANES == 0 and shift_mu.shape[1] == rwc
    n_pair = dr // LANES
    me = _index(_position())

    shard = dr // N_DEV
    pack = jnp.concatenate([w_decay_up[0], w_decay_up[1], a_up[0], a_up[1], v_mix_up[0], v_mix_down[0].T], axis=0)
    win_t0, pack = _all_gather_rows([w_in[0].T.astype(bf16), pack], "ag_first")
    win_t = [win_t0, None]
    wout = [None, None]

    def start_gather(blocks, name, after):
        lands = [lax.dynamic_update_slice(lax.empty((N_DEV * blk.shape[0], blk.shape[1]), bf16), blk, (me * blk.shape[0], 0))
                 for blk in blocks]
        return _split_start(blocks, lands, _gather_plan, len(_GATHER_FLIPS), name, after=after)

    gather_wout0 = start_gather([w_out[0].astype(bf16)], "ag_wout0_start", (win_t[0], pack))
    gather_layer1 = start_gather([w_in[1].T.astype(bf16), w_out[1].astype(bf16)], "ag_layer1_start", (gather_wout0[-1],))
    pack = jnp.transpose(pack.reshape(N_DEV, -1, shard), (1, 0, 2)).reshape(-1, dr)
    offs = [0, rank_w, 2 * rank_w, 2 * rank_w + rank_a, 2 * rank_w + 2 * rank_a, 2 * rank_w + 2 * rank_a + rank_v,
            2 * rank_w + 2 * rank_a + 2 * rank_v]
    wdu_f = [pack[offs[0]:offs[1]], pack[offs[1]:offs[2]]]
    aup_f = [pack[offs[2]:offs[3]], pack[offs[3]:offs[4]]]
    vup_f = pack[offs[4]:offs[5]]
    vdown_f = pack[offs[5]:offs[6]].T

    def after_start(a, started):
        return a + started[-1][0:1, 0:1]

    def rwkv_params(l):
        mu = after_start(shift_mu[0:1], gather_layer1) if l == 0 else shift_mu[l:l + 1]
        prm = [mu, w_decay0[l:l + 1], a0[l:l + 1], _pad_rows(wdu_f[l], LANES, True),
               _pad_rows(aup_f[l], LANES, False)]
        if l == 1:
            prm += [v_mix0[0:1], _pad_rows(vdown_f.T, LANES, True).T, _pad_rows(vup_f, LANES, True)]
        rows = jnp.stack([k_k[l], k_a[l], r_k[l], ln_x_w[l], ln_x_b[l]] + [jnp.zeros((dr,), f32)] * 3, axis=0)
        pp = jnp.transpose(rows.reshape(8, n_pair, LANES), (1, 0, 2))
        return tuple(prm), pp

    h = x[0]
    h16 = h.astype(bf16)
    tgt = loss_target[0]
    saved = []
    vfirst = None
    for l in range(DEPTH):
        prm, pp = rwkv_params(l)
        proj = _matmul(h16, win_t[l], "nt", f"mm_proj_{l}", _MM_TILES["proj"])
        if l == 0:
            cat, vfirst, mck = _rwkv_fwd(False, proj, None, prm, pp, d)
        else:
            cat, mck = _rwkv_fwd(True, proj, vfirst, prm, pp, d)
        cat, sck = _hgrn_fwd(l == 1, proj, lb_logits, g_norm_w[l:l + 1], cat, rwc)
        if l == 0:
            _, arrived = _split_wait(gather_wout0, _gather_plan, cat, "ag_wout0_wait")
            (wout[0],) = _gather_forward(arrived, "ag_wout0_forward")
        y = _matmul(cat, wout[l], "nn", f"mm_out_{l}", _MM_TILES["out"])
        saved.append((h, h16, proj, prm, pp, mck, sck, cat, y))
        if l < DEPTH - 1:
            h, h16 = _ln_fwd(h, y, ln_w[l:l + 1], ln_b[l:l + 1])
            _, arrived = _split_wait(gather_layer1, _gather_plan, h16, "ag_layer1_wait")
            win_t[1], wout[1] = _gather_forward(arrived, "ag_layer1_forward")
        else:
            top = _ln_loss_bwd(h, y, ln_w[l:l + 1], ln_b[l:l + 1], tgt)
    loss = lax.psum(top[4][0, 0], ("x", "y", "c"))

    grads = {}
    big = {}
    dvfirst = None
    d_lbl = None
    rs_started = {}
    for l in reversed(range(DEPTH)):
        h_l, h16_l, proj, prm, pp, mck, sck, cat, y = saved[l]
        if l == DEPTH - 1:
            dy, dy16, g_ln_w, g_ln_b = top[:4]
        else:
            dy, dy16, g_ln_w, g_ln_b = _ln_bwd(h_l, y, after_start(ln_w[l:l + 1], rs_started[l + 1]), ln_b[l:l + 1], dh_out)
        dcat = _matmul(dy16, wout[l], "nt", f"mm_dcat_{l}", _MM_TILES["dcat"])
        big[("w_out", l)] = _matmul(cat, dy16, "tn", f"mm_dwout_{l}", _MM_TILES["dwout"], out_dtype=bf16)
        if l == 1:
            outs = _rwkv_bwd(True, proj, vfirst, prm, pp, mck, dcat, None)
            dproj_r, dvfirst = outs[0], outs[1]
            dprm, dpp = outs[2:-1], outs[-1]
        else:
            outs = _rwkv_bwd(False, proj, None, prm, pp, mck, dcat, dvfirst)
            dproj_r = outs[0]
            dprm, dpp = outs[1:-1], outs[-1]
        dproj, dlbl_l, dgnw = _hgrn_bwd(l == 1, proj, lb_logits, g_norm_w[l:l + 1], sck, dcat, rwc, dproj_r)
        big[("w_in", l)] = _matmul(dproj, h16_l, "tn", f"mm_dwin_{l}", _MM_TILES["dwin"], out_dtype=bf16)
        sharded = [dprm[3][:rank_w].T, dprm[4][rank_w:].T]
        if l == 1:
            sharded += [dprm[6][:, :rank_v], dprm[7][:rank_v].T,
                        jnp.zeros((dr, LANES - 2 * rank_v), f32)]
        sharded = jnp.concatenate(sharded, axis=1).astype(bf16)
        d2d = _reduce_scatter_begin([big[("w_in", l)], big[("w_out", l)], sharded], f"l{l}")
        if l == 0:
            rs_started[l] = _reduce_scatter_middle(d2d, sharded, f"l{l}")
            token = rs_started[l][-1]
        else:
            token = d2d[-1]
        dh_out = _matmul(dproj, win_t[l], "nn", f"mm_dh_{l}", _MM_TILES["dh"], add=dy, add_scale=ALPHA, after=token)
        if l > 0:
            rs_started[l] = _reduce_scatter_middle(d2d, dh_out, f"l{l}")
        dpp = jnp.transpose(dpp, (1, 0, 2)).reshape(8, dr)
        grads[l] = dict(shift_mu=dprm[0][0], w_decay0=dprm[1][0], a0=dprm[2][0],
                        k_k=dpp[0], k_a=dpp[1], r_k=dpp[2], ln_x_w=dpp[3], ln_x_b=dpp[4],
                        g_norm_w=dgnw[0], ln_w=g_ln_w[0], ln_b=g_ln_b[0])
        if l == 1:
            grads[l].update(v_mix0=dprm[5][0])
            d_lbl = dlbl_l
    grad_x = dh_out[None]

    def both(name):
        return jnp.stack([grads[0][name], grads[1][name]])

    small = dict(shift_mu=both("shift_mu"), w_decay0=both("w_decay0"), a0=both("a0"), k_k=both("k_k"), k_a=both("k_a"),
                 r_k=both("r_k"), ln_x_w=both("ln_x_w"), ln_x_b=both("ln_x_b"), v_mix0=grads[1]["v_mix0"][None],
                 lb_logits=d_lbl, g_norm_w=both("g_norm_w"), ln_w=both("ln_w"), ln_b=both("ln_b"))
    flat = jnp.concatenate([small[nm].reshape(-1) for nm in _SMALL])
    n_flat = flat.shape[0]
    rows = -(-n_flat // (8 * LANES)) * 8
    flat = jnp.concatenate([flat, jnp.zeros((rows * LANES - n_flat,), f32)]).reshape(rows, LANES)
    total = _sum_slots(_all_gather_rows([flat], "ag_small_grads")[0].reshape(N_DEV, rows, LANES), "sum_small_grads").reshape(-1)
    gsm = {}
    off = 0
    for nm in _SMALL:
        size = small[nm].size
        gsm[nm] = total[off:off + size].reshape(small[nm].shape)
        off += size
    reduced = {1: _reduce_scatter_end(rs_started[1], dh_out, "l1")}
    reduced[0] = _reduce_scatter_end(rs_started[0], total, "l0")
    g_w_in_t = jnp.stack([reduced[l][0] for l in range(DEPTH)])
    gsm["w_in"] = jnp.transpose(g_w_in_t, (0, 2, 1))
    gsm["w_out"] = jnp.stack([reduced[l][1] for l in range(DEPTH)])
    gsm["w_decay_up"] = jnp.stack([reduced[l][2][:, :rank_w].T for l in range(DEPTH)])
    gsm["a_up"] = jnp.stack([reduced[l][2][:, rank_w:rank_w + rank_a].T for l in range(DEPTH)])
    gsm["v_mix_down"] = reduced[1][2][:, LANES:LANES + rank_v][None]
    gsm["v_mix_up"] = reduced[1][2][:, LANES + rank_v:LANES + 2 * rank_v].T[None]

    deltas, new_m, new_v = {}, {}, {}
    swap = lambda a: jnp.transpose(a, (0, 2, 1))
    deltas["w_in"], new_m["w_in"], new_v["w_in"] = (
        swap(a) for a in _adamw(swap(w_in), g_w_in_t, swap(m_w_in), swap(v_w_in), "adamw_w_in"))
    deltas["w_out"], new_m["w_out"], new_v["w_out"] = _adamw(w_out, gsm["w_out"], m_w_out, v_w_out, "adamw_w_out")
    rest = [nm for nm in _NAMES if nm not in ("w_in", "w_out")]
    d_rest, m_rest, v_rest = _adamw_many([weights[nm] for nm in rest], [gsm[nm] for nm in rest],
                                         [mom1[nm] for nm in rest], [mom2[nm] for nm in rest], "adamw_small")
    for i, nm in enumerate(rest):
        deltas[nm], new_m[nm], new_v[nm] = d_rest[i], m_rest[i], v_rest[i]
    return (loss, grad_x, *[gsm[nm] for nm in _NAMES], *[deltas[nm] for nm in _NAMES],
            *[new_m[nm] for nm in _NAMES], *[new_v[nm] for nm in _NAMES])
```

```python
import functools

import jax
import jax.numpy as jnp
from jax import lax
from jax.experimental import pallas as pl
from jax.experimental.pallas import tpu as pltpu

f32 = jnp.float32
bf16 = jnp.bfloat16

N_DEV = 8
CHUNK = 64
LANES = 128
RWKV_HEAD = 64
DEPTH = 2
ALPHA = (2 * DEPTH) ** 0.25
LN_EPS = 1e-5
GN_EPS = 64e-5
RMS_EPS = 1e-5
LB_FLOOR = 1e-30
ADAM_LR, ADAM_B1, ADAM_B2, ADAM_EPS, ADAM_WD, ADAM_STEP = 0.001, 0.9, 0.999, 1e-08, 0.01, 10
MESH = pl.DeviceIdType.MESH


def _iota(shape, d):
    return lax.broadcasted_iota(jnp.int32, shape, d)


_DIMS = {"nn": (((1,), (0,)), ((), ())), "nt": (((1,), (1,)), ((), ())), "tn": (((0,), (0,)), ((), ()))}
_BATCH_DIMS = {"nn": (((2,), (1,)), ((0,), (0,))), "nt": (((2,), (2,)), ((0,), (0,))), "tn": (((1,), (1,)), ((0,), (0,)))}
_K_AXES = {"nn": (-1, -2), "nt": (-1, -1), "tn": (-2, -2)}


def _mxu(a, b, mode):
    return lax.dot_general(a, b, (_BATCH_DIMS if a.ndim == 3 else _DIMS)[mode], preferred_element_type=f32)


def _split(x):
    hi = x.astype(bf16)
    return hi, (x - hi.astype(f32)).astype(bf16)


def _mm2_impl(a, b, mode, passes=3):
    if passes == 1:
        return _mxu(a.astype(bf16), b.astype(bf16), mode)
    ah, al = _split(a)
    if passes == 3:
        bh, bl = _split(b)
        lhs, rhs = [ah, ah, al], [bh, bl, bh]
    else:
        bh = b.astype(bf16)
        lhs, rhs = [ah, al], [bh, bh]
    ka, kb = _K_AXES[mode]
    k = a.shape[ka]
    if k % (LANES if -1 in (ka, kb) else 16) == 0:
        return _mxu(jnp.concatenate(lhs, axis=ka), jnp.concatenate(rhs, axis=kb), mode)
    out = _mxu(lhs[0], rhs[0], mode)
    for x, y in zip(lhs[1:], rhs[1:]):
        out = out + _mxu(x, y, mode)
    return out


@functools.partial(jax.custom_vjp, nondiff_argnums=(2, 3))
def _mm2(a, b, mode, passes=3):
    return _mm2_impl(a, b, mode, passes)


def _mm2_fwd(a, b, mode, passes):
    return _mm2_impl(a, b, mode, passes), (a, b)


def _mm2_bwd(mode, passes, res, g):
    a, b = res
    if mode == "nn":
        return _mm2_impl(g, b, "nt", passes), _mm2_impl(a, g, "tn", passes)
    if mode == "nt":
        return _mm2_impl(g, b, "nn", passes), _mm2_impl(g, a, "tn", passes)
    return _mm2_impl(b, g, "nt", passes), _mm2_impl(a, g, "nn", passes)


_mm2.defvjp(_mm2_fwd, _mm2_bwd)

TRI_PASSES = 1
APPLY_PASSES = 1


def _const_impl(cm, x, mode):
    if mode in ("r", "rt"):
        shape = x.shape
        out = _mxu(x.astype(bf16).reshape(-1, shape[-1]), cm, "nn" if mode == "r" else "nt")
        return out.reshape(shape[:-1] + (out.shape[-1],))
    hi, lo = _split(x)
    if x.ndim == 3:
        cm = jnp.broadcast_to(cm, (x.shape[0],) + cm.shape)
    return _mxu(cm, hi, mode) + _mxu(cm, lo, mode)


@jax.custom_vjp
def _const_left(cm, x):
    return _const_impl(cm, x, "nn")


_const_left.defvjp(lambda cm, x: (_const_impl(cm, x, "nn"), cm),
                   lambda cm, g: (jnp.zeros_like(cm), _const_impl(cm, g, "tn")))


@jax.custom_vjp
def _const_right(x, cm):
    return _const_impl(cm, x, "r")


_const_right.defvjp(lambda x, cm: (_const_impl(cm, x, "r"), cm),
                    lambda cm, g: (_const_impl(cm, g, "rt"), jnp.zeros_like(cm)))


def _tri_inv(a):
    n = a.shape[-1]
    tm = (_iota((n, n), 0) == _iota((n, n), 1)).astype(f32) + a
    ak = a
    for _ in range(5):
        ak = _mm2_impl(ak, ak, "nn", TRI_PASSES)
        tm = tm + _mm2_impl(tm, ak, "nn", TRI_PASSES)
    return tm


@jax.custom_vjp
def _tri_solve(tm, a, x):
    del a
    return _mm2_impl(tm, x, "nn", APPLY_PASSES)


def _tri_solve_fwd(tm, a, x):
    u = _mm2_impl(tm, x, "nn", APPLY_PASSES)
    return u, (tm, u)


def _tri_solve_bwd(res, du):
    tm, u = res
    dx = _mm2_impl(tm, du, "tn", APPLY_PASSES)
    return jnp.zeros_like(tm), _mm2_impl(dx, u, "nt", APPLY_PASSES), dx


_tri_solve.defvjp(_tri_solve_fwd, _tri_solve_bwd)


def _col_of_row(row_vec):
    n = row_vec.shape[-1]
    eye = _iota((n, n), 0) == _iota((n, n), 1)
    return jnp.sum(jnp.where(eye, jnp.broadcast_to(row_vec, row_vec.shape[:-2] + (n, n)), 0.0), axis=-1, keepdims=True)


def _softplus(x):
    return jnp.maximum(x, 0.0) + jnp.log1p(jnp.exp(-jnp.abs(x)))


def _log_sigmoid(x):
    return -_softplus(-x)


def _logaddexp(a, b):
    return jnp.maximum(a, b) + jnp.log1p(jnp.exp(-jnp.abs(a - b)))


def _silu(x):
    return x * jax.nn.sigmoid(x)


def _tril(c, strict):
    r, s = _iota((c, c), 0), _iota((c, c), 1)
    return (r > s) if strict else (r >= s)


def _last_row(a):
    c = a.shape[-2]
    return jnp.sum(jnp.where(_iota(a.shape, a.ndim - 2) == c - 1, a, 0.0), axis=-2, keepdims=True)


def _rwkv_pre(layer1, prm, y, prev, vf):
    c = y.shape[0]
    if layer1:
        mu, w0, a0, wup, aup, v0, vdown, vup = prm
    else:
        mu, w0, a0, wup, aup = prm
    dr = w0.shape[1]
    shift = (_iota((c, c), 0) == _iota((c, c), 1) + 1).astype(bf16)
    y_prev = _const_left(shift, y) + jnp.where(_iota((c, 1), 0) == 0, prev, 0.0)
    rw = y + mu * (y_prev - y)
    r, k, v, z = (rw[:, i * dr:(i + 1) * dr] for i in range(4))
    wdad = rw[:, 4 * dr:4 * dr + LANES]
    w_raw = w0 + _mm2(jnp.tanh(wdad), wup, "nn")
    lw = -jnp.exp(-_softplus(-w_raw) - 0.5)
    asig = jax.nn.sigmoid(a0 + _mm2(wdad, aup, "nn"))
    if layer1:
        v = v + (vf - v) * jax.nn.sigmoid(v0 + _mm2(_mm2(v, vdown, "nn"), vup, "nn"))
    return r, k, v, z, lw, asig


def _rwkv_pair(pp, m0, xs, tm=None):
    kkw, kaw, rkw, gnw, gnb = pp
    r, k, v, z, lw, asig = xs
    c = r.shape[-2]
    n2 = 2 * c
    lane = _iota((1, LANES), 1)
    mh0, mh1 = (lane < RWKV_HEAD).astype(f32), (lane >= RWKV_HEAD).astype(f32)
    same_head = _iota((LANES, LANES), 0) // RWKV_HEAD == _iota((LANES, LANES), 1) // RWKV_HEAD
    g = same_head.astype(bf16)

    def seg(x):
        return _const_right(x, g)

    def stack(x):
        return jnp.concatenate([x * mh0, x * mh1], axis=-2)

    kk = k * kkw
    kk = kk / jnp.maximum(jnp.sqrt(seg(kk * kk)), 1e-12)
    k2 = k * (1.0 + (asig - 1.0) * kaw)
    a = -kk
    b = kk * asig
    cum = _const_left(_tril(c, False).astype(bf16), lw)
    at = stack(a * jnp.exp(cum - lw))
    rt = stack(r * jnp.exp(cum))
    en = jnp.exp(-cum)
    sc = _mm2(jnp.concatenate([at, rt], axis=-2), jnp.concatenate([stack(b * en), stack(k2 * en)], axis=-2), "nt")
    row, col = _iota((n2, n2), 0), _iota((n2, n2), 1)
    same = row // c == col // c
    strict = same & (row % c > col % c)
    incl = same & (row % c >= col % c)
    aab = jnp.where(strict, sc[..., :n2, :n2], 0.0)
    aak = jnp.where(strict, sc[..., :n2, n2:], 0.0)
    arb = jnp.where(incl, sc[..., n2:, :n2], 0.0)
    ark = jnp.where(incl, sc[..., n2:, n2:], 0.0)
    vv = jnp.concatenate([v, v], axis=-2)
    mask_st = jnp.concatenate([jnp.broadcast_to(mh0, (c, LANES)), jnp.broadcast_to(mh1, (c, LANES))], axis=0)
    x_st = _mm2(jnp.concatenate([at, aak], axis=-1), jnp.concatenate([m0, vv], axis=-2), "nn", APPLY_PASSES)
    if tm is None:
        tm = _tri_inv(lax.stop_gradient(aab))
    u_st = _tri_solve(tm, aab, x_st) * mask_st
    o_st = _mm2(jnp.concatenate([rt, arb, ark], axis=-1), jnp.concatenate([m0, u_st, vv], axis=-2), "nn", APPLY_PASSES) * mask_st
    u = u_st[..., :c, :] + u_st[..., c:, :]
    o = o_st[..., :c, :] + o_st[..., c:, :]
    cum_last = _last_row(cum)
    dec_end = jnp.exp(cum_last - cum)
    m_new = _col_of_row(jnp.exp(cum_last)) * m0 + _mm2(
        jnp.concatenate([b * dec_end, k2 * dec_end], axis=-2), jnp.concatenate([u, v], axis=-2), "tn", APPLY_PASSES) * same_head.astype(f32)
    mean = seg(o) * (1.0 / RWKV_HEAD)
    d = o - mean
    var = seg(d * d) * (1.0 / RWKV_HEAD)
    on = d * lax.rsqrt(var + GN_EPS) * gnw + gnb
    bonus = seg(r * k2 * rkw) * v
    return (on + bonus) * _silu(z), m_new, tm


def _split_lanes(a, n):
    return [a[:, i * LANES:(i + 1) * LANES] for i in range(n)]


def _rwkv_step(layer1, prm, y, prev, vf, pp, m0, tm=None):
    xs = _rwkv_pre(layer1, prm, y, prev, vf)
    n_pair = m0.shape[0]
    og, m_new, tm = _rwkv_pair(pp, m0, tuple(jnp.concatenate([p[None] for p in _split_lanes(a, n_pair)], axis=0) for a in xs), tm)
    return og, m_new, xs[2], tm


def _group(n):
    return n


def _rwkv_specs(layer1, t, dr, rwc, n_pair, rev):
    nc = t // CHUNK
    grp = _group(n_pair)

    def cidx(c):
        return (nc - 1 - c) if rev else c

    full = lambda shape: pl.BlockSpec(shape, lambda c, p: tuple(0 for _ in shape))
    specs = [
        pl.BlockSpec((CHUNK, rwc), lambda c, p: (cidx(c), 0)),
        pl.BlockSpec((8, rwc), lambda c, p: (jnp.maximum(cidx(c) * (CHUNK // 8) - 1, 0), 0)),
    ]
    if layer1:
        specs.append(pl.BlockSpec((CHUNK, dr), lambda c, p: (cidx(c), 0)))
    prm_shapes = [(1, rwc), (1, dr), (1, dr), (LANES, dr), (LANES, dr)]
    if layer1:
        prm_shapes += [(1, dr), (dr, LANES), (LANES, dr)]
    specs += [full(s) for s in prm_shapes]
    specs.append(pl.BlockSpec((grp, 8, LANES), lambda c, p: (p, 0, 0)))
    return specs, prm_shapes, cidx, full


def _rwkv_fwd(layer1, proj, vf, prm, pp, cat_width):
    t = proj.shape[0]
    dr = prm[1].shape[1]
    rwc = prm[0].shape[1]
    n_pair = dr // LANES
    nc = t // CHUNK
    n_prm = len(prm)
    specs, _, _, _ = _rwkv_specs(layer1, t, dr, rwc, n_pair, False)

    def body(*refs):
        y_ref, prev_ref = refs[0], refs[1]
        i = 2
        vf_ref = None
        if layer1:
            vf_ref = refs[i]
            i += 1
        prm_refs = refs[i:i + n_prm]
        i += n_prm
        pp_ref = refs[i]
        i += 1
        cat_ref = refs[i]
        i += 1
        vout_ref = None
        if not layer1:
            vout_ref = refs[i]
            i += 1
        mck_ref, m_s = refs[i], refs[i + 1]
        c = pl.program_id(0)

        @pl.when(c == 0)
        def _():
            m_s[...] = jnp.zeros_like(m_s)

        prev = prev_ref[pl.ds(7, 1), :] * (c != 0).astype(f32)
        m0 = m_s[...]
        ppv = tuple(pp_ref[:, pl.ds(q, 1), :] for q in range(5))
        og, m_new, v, tm = _rwkv_step(layer1, tuple(r[...] for r in prm_refs), y_ref[...], prev,
                                      vf_ref[...] if layer1 else None, ppv, m0)
        mck_ref[0, :n_pair] = m0
        mck_ref[0, n_pair:] = tm
        if not layer1:
            vout_ref[...] = v
        for j in range(n_pair):
            cat_ref[:, j * LANES:(j + 1) * LANES] = og[j]
        m_s[...] = m_new

    grp = _group(n_pair)
    assert grp == n_pair
    out_shape = [jax.ShapeDtypeStruct((t, cat_width), f32)]
    out_specs = [pl.BlockSpec((CHUNK, grp * LANES), lambda c, p: (c, p))]
    if not layer1:
        out_shape.append(jax.ShapeDtypeStruct((t, dr), f32))
        out_specs.append(pl.BlockSpec((CHUNK, dr), lambda c, p: (c, 0)))
    out_shape.append(jax.ShapeDtypeStruct((nc, 2 * n_pair, LANES, LANES), f32))
    out_specs.append(pl.BlockSpec((1, 2 * grp, LANES, LANES), lambda c, p: (c, p, 0, 0)))
    args = [proj, proj] + ([vf] if layer1 else []) + list(prm) + [pp]
    return pl.pallas_call(
        body, grid=(nc, 1), in_specs=specs, out_specs=out_specs, out_shape=out_shape,
        scratch_shapes=[pltpu.VMEM((n_pair, LANES, LANES), f32)],
        compiler_params=pltpu.CompilerParams(dimension_semantics=("arbitrary", "arbitrary")),
        name=f"rwkv_fwd_l{int(layer1)}",
    )(*args)


def _rwkv_bwd(layer1, proj, vf, prm, pp, mck, dcat, dvout):
    t = proj.shape[0]
    dr = prm[1].shape[1]
    rwc = prm[0].shape[1]
    n_pair = dr // LANES
    nc = t // CHUNK
    n_prm = len(prm)
    specs, prm_shapes, cidx, full = _rwkv_specs(layer1, t, dr, rwc, n_pair, True)
    grp = _group(n_pair)
    assert grp == n_pair
    specs.append(pl.BlockSpec((1, 2 * grp, LANES, LANES), lambda c, p: (cidx(c), p, 0, 0)))
    specs.append(pl.BlockSpec((CHUNK, grp * LANES), lambda c, p: (cidx(c), p)))
    if not layer1:
        specs.append(pl.BlockSpec((CHUNK, dr), lambda c, p: (cidx(c), 0)))

    def body(*refs):
        y_ref, prev_ref = refs[0], refs[1]
        i = 2
        vf_ref = None
        if layer1:
            vf_ref = refs[i]
            i += 1
        prm_refs = refs[i:i + n_prm]
        i += n_prm
        pp_ref, mck_ref, dog_ref = refs[i], refs[i + 1], refs[i + 2]
        i += 3
        dvout_ref = None
        if not layer1:
            dvout_ref = refs[i]
            i += 1
        dy_ref = refs[i]
        i += 1
        dvf_ref = None
        if layer1:
            dvf_ref = refs[i]
            i += 1
        dprm_refs = refs[i:i + n_prm]
        i += n_prm
        dpp_ref = refs[i]
        dm_s, dprev_s = refs[i + 1:i + 3]
        c = pl.program_id(0)
        cr = nc - 1 - c

        @pl.when(c == 0)
        def _():
            dm_s[...] = jnp.zeros_like(dm_s)
            dprev_s[...] = jnp.zeros_like(dprev_s)
            dpp_ref[...] = jnp.zeros_like(dpp_ref)
            for r in dprm_refs:
                r[...] = jnp.zeros_like(r)

        prev = prev_ref[pl.ds(7, 1), :] * (cr != 0).astype(f32)
        prm_v = tuple(r[...] for r in prm_refs)
        ppv = tuple(pp_ref[:, pl.ds(q, 1), :] for q in range(5))
        dog = jnp.stack([dog_ref[:, j * LANES:(j + 1) * LANES] for j in range(n_pair)], axis=0)
        m0, tm = mck_ref[0, :n_pair], mck_ref[0, n_pair:]
        no_tm = jnp.zeros_like(tm)
        if layer1:
            _, vjp = jax.vjp(lambda a, b, d, e, g, h: _rwkv_step(True, a, b, d, e, g, h, tm),
                             prm_v, y_ref[...], prev, vf_ref[...], ppv, m0)
            dprm, dy, dprev, dvf, dppv, dm0 = vjp((dog, dm_s[...], jnp.zeros((CHUNK, dr), f32), no_tm))
            dvf_ref[...] = dvf
        else:
            _, vjp = jax.vjp(lambda a, b, d, e, g: _rwkv_step(False, a, b, d, None, e, g, tm), prm_v, y_ref[...], prev, ppv, m0)
            dprm, dy, dprev, dppv, dm0 = vjp((dog, dm_s[...], dvout_ref[...], no_tm))
        dm_s[...] = dm0
        for q in range(5):
            dpp_ref[:, pl.ds(q, 1), :] += dppv[q]
        dy_ref[...] = (dy + jnp.where(_iota((CHUNK, 1), 0) == CHUNK - 1, dprev_s[...], 0.0)).astype(bf16)
        dprev_s[...] = dprev
        for r, gval in zip(dprm_refs, dprm):
            r[...] += gval

    out_shape = [jax.ShapeDtypeStruct((t, proj.shape[1]), bf16)]
    out_specs = [pl.BlockSpec((CHUNK, rwc), lambda c, p: (cidx(c), 0))]
    if layer1:
        out_shape.append(jax.ShapeDtypeStruct((t, dr), f32))
        out_specs.append(pl.BlockSpec((CHUNK, dr), lambda c, p: (cidx(c), 0)))
    out_shape += [jax.ShapeDtypeStruct(s, f32) for s in prm_shapes]
    out_specs += [full(s) for s in prm_shapes]
    out_shape.append(jax.ShapeDtypeStruct((n_pair, 8, LANES), f32))
    out_specs.append(full((n_pair, 8, LANES)))
    args = [proj, proj] + ([vf] if layer1 else []) + list(prm) + [pp, mck, dcat] + ([] if layer1 else [dvout])
    return pl.pallas_call(
        body, grid=(nc, 1), in_specs=specs, out_specs=out_specs, out_shape=out_shape,
        scratch_shapes=[pltpu.VMEM((n_pair, LANES, LANES), f32), pltpu.VMEM((1, rwc), f32)],
        compiler_params=pltpu.CompilerParams(dimension_semantics=("arbitrary", "arbitrary")),
        name=f"rwkv_bwd_l{int(layer1)}",
    )(*args)


def _hgrn_chunk(layer1, lbl, gnw, s0, q_raw, f_raw, i_in, z):
    c = q_raw.shape[-2]
    q = _silu(q_raw)
    ls = _log_sigmoid(f_raw)
    if layer1:
        l0, l1 = lbl[..., 0:1, :], lbl[..., 1:2, :]
        mx = jnp.maximum(l0, l1)
        e0, e1 = jnp.exp(l0 - mx), jnp.exp(l1 - mx)
        sm0, sm1 = e0 / (e0 + e1), e1 / (e0 + e1)
        lb = (sm0 + sm1) - sm0
        log_f = _logaddexp(jnp.log(jnp.maximum(lb, LB_FLOOR)), jnp.log1p(-lb) + ls)
        k = (1.0 - lb) * jax.nn.sigmoid(-f_raw)
    else:
        log_f = _logaddexp(jnp.full_like(ls, jnp.log(jnp.float32(LB_FLOOR))), ls)
        k = jax.nn.sigmoid(-f_raw)
    row, col = _iota((c, c), 0), _iota((c, c), 1)
    trow = _iota((c, 1), 0)
    halves = []
    half = c // 2
    while half >= 1:
        halves.append(half)
        half //= 2
    cmat = jnp.concatenate([(col <= row).astype(f32)]
                           + [(col <= (row // (2 * hf)) * (2 * hf) + hf - 1).astype(f32) for hf in halves], axis=0)
    ball = _const_left(cmat.astype(bf16), log_f)
    b = ball[..., :c, :]
    att = None
    for lvl, hf in enumerate(halves):
        blk = 2 * hf
        bref = ball[..., (lvl + 1) * c:(lvl + 2) * c, :]
        upper = (trow % blk) >= hf
        dec = jnp.exp(jnp.where(upper, b - bref, bref - b))
        qh = jnp.where(upper, q * dec, 0.0)
        kh = jnp.where(upper, 0.0, k * dec)
        term = jnp.where(row // blk == col // blk, _mm2(qh, kh, "nt", APPLY_PASSES), 0.0)
        att = term if att is None else att + term
    lhs = jnp.concatenate([q * jnp.exp(b), att, jnp.zeros(att.shape[:-1] + (LANES - c,), f32)], axis=-1)
    rhs = jnp.concatenate([s0, i_in, jnp.zeros(i_in.shape[:-2] + (LANES - c, i_in.shape[-1]), f32)], axis=-2)
    o = _mm2(lhs, rhs, "nn", APPLY_PASSES) + jnp.sum(q * k, axis=-1, keepdims=True) * i_in
    b_last = _last_row(b)
    s_new = _col_of_row(jnp.exp(b_last)) * s0 + _mm2(k * jnp.exp(b_last - b), i_in, "tn", APPLY_PASSES)
    o = o * lax.rsqrt(jnp.mean(o * o, axis=-1, keepdims=True) + RMS_EPS)
    return o * gnw * _silu(z), s_new


def _hgrn_in_specs(t, dh, col0, rev):
    nc = t // CHUNK
    nh = dh // LANES

    def cidx(c):
        return (nc - 1 - c) if rev else c

    grp = _group(nh)
    specs = [pl.BlockSpec((CHUNK, LANES), functools.partial(lambda g, j, h, c: (cidx(c), col0 + g * nh + h * grp + j), g, j))
             for j in range(grp) for g in range(4)]
    specs.append(pl.BlockSpec((2, grp * LANES), lambda h, c: (0, h)))
    specs.append(pl.BlockSpec((1, grp * LANES), lambda h, c: (0, h)))
    return specs, cidx, grp


def _hgrn_fwd(layer1, proj, lbl, gnw, cat, rwc):
    t, d = cat.shape
    dh = gnw.shape[1]
    nh = dh // LANES
    nc = t // CHUNK
    col0 = rwc // LANES
    specs, _, grp = _hgrn_in_specs(t, dh, col0, False)
    specs.append(pl.BlockSpec(memory_space=pl.ANY))
    assert (d - dh) % (grp * LANES) == 0
    cat_col0 = (d - dh) // (grp * LANES)

    def body(*refs):
        x_refs = refs[:4 * grp]
        lbl_ref, gnw_ref, _, cat_ref, sck_ref, s_s = refs[4 * grp:]
        c = pl.program_id(1)

        @pl.when(c == 0)
        def _():
            s_s[...] = jnp.zeros_like(s_s)

        lanes = [slice(j * LANES, (j + 1) * LANES) for j in range(grp)]
        s0 = s_s[...]
        sck_ref[:, 0] = s0
        out, s_new = _hgrn_chunk(layer1, jnp.stack([lbl_ref[:, ln] for ln in lanes]), jnp.stack([gnw_ref[:, ln] for ln in lanes]),
                                 s0, *(jnp.stack([x_refs[4 * j + g][...] for j in range(grp)]) for g in range(4)))
        for j in range(grp):
            cat_ref[:, lanes[j]] = out[j]
        s_s[...] = s_new

    return pl.pallas_call(
        body, grid=(nh // grp, nc), in_specs=specs,
        out_specs=[pl.BlockSpec((CHUNK, grp * LANES), lambda h, c: (c, cat_col0 + h)),
                   pl.BlockSpec((grp, 1, LANES, LANES), lambda h, c: (h, c, 0, 0))],
        out_shape=[jax.ShapeDtypeStruct((t, d), f32), jax.ShapeDtypeStruct((nh, nc, LANES, LANES), f32)],
        scratch_shapes=[pltpu.VMEM((grp, LANES, LANES), f32)],
        input_output_aliases={4 * grp + 2: 0},
        compiler_params=pltpu.CompilerParams(dimension_semantics=("arbitrary", "arbitrary")),
        name=f"hgrn_fwd_l{int(layer1)}",
    )(*([proj] * (4 * grp)), lbl, gnw, cat)


def _hgrn_bwd(layer1, proj, lbl, gnw, sck, dcat, rwc, dproj):
    t, d = dcat.shape
    dh = gnw.shape[1]
    nh = dh // LANES
    nc = t // CHUNK
    col0 = rwc // LANES
    specs, cidx, grp = _hgrn_in_specs(t, dh, col0, True)
    assert grp == nh and (d - dh) % (grp * LANES) == 0
    cat_col0 = (d - dh) // (grp * LANES)
    specs.append(pl.BlockSpec((grp, 1, LANES, LANES), lambda h, c: (h, cidx(c), 0, 0)))
    specs.append(pl.BlockSpec((CHUNK, grp * LANES), lambda h, c: (cidx(c), cat_col0 + h)))
    specs.append(pl.BlockSpec(memory_space=pl.ANY))

    def body(*refs):
        x_refs = refs[:4 * grp]
        lbl_ref, gnw_ref, sck_ref, do_ref, _, dp_hbm, dlbl_ref, dgnw_ref, ds_s, stage, sems = refs[4 * grp:]
        c = pl.program_id(1)
        slot = c % 2

        def put(s, g, chunk):
            return pltpu.make_async_copy(stage.at[s, g], dp_hbm.at[pl.ds(chunk * CHUNK, CHUNK), pl.ds(rwc + g * dh, dh)],
                                         sems.at[s, g])

        @pl.when(c == 0)
        def _():
            ds_s[...] = jnp.zeros_like(ds_s)
            dlbl_ref[...] = jnp.zeros_like(dlbl_ref)
            dgnw_ref[...] = jnp.zeros_like(dgnw_ref)

        @pl.when(c >= 2)
        def _():
            for g in range(4):
                put(slot, g, 0).wait()

        lanes = [slice(j * LANES, (j + 1) * LANES) for j in range(grp)]
        _, vjp = jax.vjp(functools.partial(_hgrn_chunk, layer1),
                         jnp.stack([lbl_ref[:, ln] for ln in lanes]), jnp.stack([gnw_ref[:, ln] for ln in lanes]), sck_ref[:, 0],
                         *(jnp.stack([x_refs[4 * j + g][...] for j in range(grp)]) for g in range(4)))
        dlbl, dgnw, ds0, dq, df, di, dz = vjp((jnp.stack([do_ref[:, ln] for ln in lanes]), ds_s[...]))
        ds_s[...] = ds0
        for j in range(grp):
            dlbl_ref[:, lanes[j]] += dlbl[j]
            dgnw_ref[:, lanes[j]] += dgnw[j]
            for g, val in enumerate((dq, df, di, dz)):
                stage[slot, g, :, lanes[j]] = val[j].astype(bf16)
        for g in range(4):
            put(slot, g, nc - 1 - c).start()

        @pl.when(c == nc - 1)
        def _():
            for g in range(4):
                put(slot, g, 0).wait()
                if nc >= 2:
                    put(1 - slot, g, 0).wait()

    return pl.pallas_call(
        body, grid=(1, nc), in_specs=specs,
        out_specs=[pl.BlockSpec(memory_space=pl.ANY),
                   pl.BlockSpec((2, grp * LANES), lambda h, c: (0, h)),
                   pl.BlockSpec((1, grp * LANES), lambda h, c: (0, h))],
        out_shape=[jax.ShapeDtypeStruct(dproj.shape, dproj.dtype), jax.ShapeDtypeStruct((2, dh), f32),
                   jax.ShapeDtypeStruct((1, dh), f32)],
        scratch_shapes=[pltpu.VMEM((grp, LANES, LANES), f32), pltpu.VMEM((2, 4, CHUNK, dh), bf16),
                        pltpu.SemaphoreType.DMA((2, 4))],
        input_output_aliases={4 * grp + 4: 0},
        compiler_params=pltpu.CompilerParams(dimension_semantics=("arbitrary", "arbitrary")),
        name=f"hgrn_bwd_l{int(layer1)}",
    )(*([proj] * (4 * grp)), lbl, gnw, sck, dcat, dproj)


def _ln(h, y, w, b):
    u = ALPHA * h + y
    mu = jnp.mean(u, axis=-1, keepdims=True)
    var = jnp.mean(jnp.square(u - mu), axis=-1, keepdims=True)
    return (u - mu) * lax.rsqrt(var + LN_EPS) * w + b


def _row_tile(t):
    return 256 if t % 256 == 0 else t


def _ln_fwd(h, y, w, b):
    t, d = h.shape
    tr = _row_tile(t)

    def body(h_ref, y_ref, w_ref, b_ref, o_ref, o16_ref):
        out = _ln(h_ref[...], y_ref[...], w_ref[...], b_ref[...])
        o_ref[...] = out
        o16_ref[...] = out.astype(bf16)

    row = pl.BlockSpec((tr, d), lambda i: (i, 0))
    vec = pl.BlockSpec((1, d), lambda i: (0, 0))
    return pl.pallas_call(body, grid=(t // tr,), in_specs=[row, row, vec, vec], out_specs=[row, row],
                          out_shape=[jax.ShapeDtypeStruct((t, d), f32), jax.ShapeDtypeStruct((t, d), bf16)],
                          name="ln_fwd")(h, y, w, b)


def _ln_loss_bwd(h, y, w, b, tgt):
    t, d = h.shape
    tr = _row_tile(t)

    def body(h_ref, y_ref, w_ref, b_ref, t_ref, dy_ref, dy16_ref, dw_ref, db_ref, loss_ref):
        @pl.when(pl.program_id(0) == 0)
        def _():
            dw_ref[...] = jnp.zeros_like(dw_ref)
            db_ref[...] = jnp.zeros_like(db_ref)
            loss_ref[...] = jnp.zeros_like(loss_ref)

        out, vjp = jax.vjp(lambda yy, ww, bb: _ln(h_ref[...], yy, ww, bb), y_ref[...], w_ref[...], b_ref[...])
        err = out - t_ref[...]
        loss_ref[...] += 0.5 * jnp.sum(jnp.mean(jnp.square(err), axis=-1, keepdims=True), axis=0, keepdims=True)
        dy, dw, db = vjp(err * (1.0 / d))
        dy_ref[...] = dy
        dy16_ref[...] = dy.astype(bf16)
        dw_ref[...] += dw
        db_ref[...] += db

    row = pl.BlockSpec((tr, d), lambda i: (i, 0))
    vec = pl.BlockSpec((1, d), lambda i: (0, 0))
    return pl.pallas_call(
        body, grid=(t // tr,), in_specs=[row, row, vec, vec, row],
        out_specs=[row, row, vec, vec, pl.BlockSpec((1, LANES), lambda i: (0, 0))],
        out_shape=[jax.ShapeDtypeStruct((t, d), f32), jax.ShapeDtypeStruct((t, d), bf16), jax.ShapeDtypeStruct((1, d), f32),
                   jax.ShapeDtypeStruct((1, d), f32), jax.ShapeDtypeStruct((1, LANES), f32)],
        compiler_params=pltpu.CompilerParams(dimension_semantics=("arbitrary",)), name="ln_loss_bwd")(h, y, w, b, tgt)


def _ln_bwd(h, y, w, b, dout):
    t, d = h.shape
    tr = _row_tile(t)

    def body(h_ref, y_ref, w_ref, b_ref, do_ref, dy_ref, dy16_ref, dw_ref, db_ref):
        @pl.when(pl.program_id(0) == 0)
        def _():
            dw_ref[...] = jnp.zeros_like(dw_ref)
            db_ref[...] = jnp.zeros_like(db_ref)

        _, vjp = jax.vjp(lambda yy, ww, bb: _ln(h_ref[...], yy, ww, bb), y_ref[...], w_ref[...], b_ref[...])
        dy, dw, db = vjp(do_ref[...])
        dy_ref[...] = dy
        dy16_ref[...] = dy.astype(bf16)
        dw_ref[...] += dw
        db_ref[...] += db

    row = pl.BlockSpec((tr, d), lambda i: (i, 0))
    vec = pl.BlockSpec((1, d), lambda i: (0, 0))
    return pl.pallas_call(
        body, grid=(t // tr,), in_specs=[row, row, vec, vec, row], out_specs=[row, row, vec, vec],
        out_shape=[jax.ShapeDtypeStruct((t, d), f32), jax.ShapeDtypeStruct((t, d), bf16),
                   jax.ShapeDtypeStruct((1, d), f32), jax.ShapeDtypeStruct((1, d), f32)],
        compiler_params=pltpu.CompilerParams(dimension_semantics=("arbitrary",)), name="ln_bwd")(h, y, w, b, dout)


def _pick(n, prefs):
    for p in prefs:
        if n % p == 0:
            return p
    return n


def _tile(n, want):
    if n <= want:
        return n
    for cand in range(want - want % LANES, 0, -LANES):
        if n % cand == 0:
            return cand
    return n


_MM_TILES = {"proj": (1024, 1664, 2048), "out": (1024, 1024, 2048), "dcat": (1024, 1024, 2048),
             "dwout": (512, 2048, 2048), "dwin": (640, 2048, 2048), "dh": (1024, 1024, 1664)}


def _matmul(a, b, mode, name, tiles, add=None, add_scale=1.0, out_dtype=f32, after=None):
    if mode == "nn":
        (m, k), n = a.shape, b.shape[1]
    elif mode == "nt":
        (m, k), n = a.shape, b.shape[0]
    else:
        (k, m), n = a.shape, b.shape[1]
    tm, tn, tk = _tile(m, tiles[0]), _tile(n, tiles[1]), _tile(k, tiles[2])
    nk = k // tk
    cache_a = nk == 1 and a.dtype != bf16 and n // tn > 1

    def body(*refs):
        a_ref, b_ref = refs[0], refs[1]
        add_ref = refs[2] if add is not None else None
        n_in = 2 + (add is not None) + (after is not None)
        o_ref = refs[n_in]
        scratch = refs[n_in + 1:]

        def finish(res):
            if add is not None:
                res = res + add_scale * add_ref[...]
            o_ref[...] = res.astype(out_dtype)

        if cache_a:
            a_bf = scratch[0]

            @pl.when(pl.program_id(1) == 0)
            def _():
                a_bf[...] = a_ref[...].astype(bf16)

            a_val = a_bf[...]
        else:
            a_val = a_ref[...].astype(bf16)
        prod = lax.dot_general(a_val, b_ref[...].astype(bf16), _DIMS[mode], preferred_element_type=f32)
        if nk == 1:
            finish(prod)
        else:
            acc = scratch[-1]
            kk = pl.program_id(2)

            @pl.when(kk == 0)
            def _():
                acc[...] = prod

            @pl.when(kk != 0)
            def _():
                acc[...] += prod

            @pl.when(kk == nk - 1)
            def _():
                finish(acc[...])

    a_shape = (tk, tm) if mode == "tn" else (tm, tk)
    a_spec = pl.BlockSpec(a_shape, (lambda i, j, kk: (kk, i)) if mode == "tn" else (lambda i, j, kk: (i, kk)))
    b_spec = pl.BlockSpec((tn, tk), lambda i, j, kk: (j, kk)) if mode == "nt" else pl.BlockSpec((tk, tn), lambda i, j, kk: (kk, j))
    o_spec = pl.BlockSpec((tm, tn), lambda i, j, kk: (i, j))
    in_specs = [a_spec, b_spec] + ([o_spec] if add is not None else []) + ([pl.BlockSpec(memory_space=pl.ANY)] if after is not None else [])
    args = [a, b] + ([add] if add is not None else []) + ([after] if after is not None else [])
    scratch_shapes = ([pltpu.VMEM(a_shape, bf16)] if cache_a else []) + ([pltpu.VMEM((tm, tn), f32)] if nk > 1 else [])
    return pl.pallas_call(
        body, grid=(m // tm, n // tn, nk), in_specs=in_specs, out_specs=o_spec,
        out_shape=jax.ShapeDtypeStruct((m, n), out_dtype), scratch_shapes=scratch_shapes,
        compiler_params=pltpu.CompilerParams(dimension_semantics=("parallel", "arbitrary", "arbitrary")),
        name=name,
    )(*args)


def _position():
    return lax.axis_index("x"), lax.axis_index("y"), lax.axis_index("c")


def _flip(pos, k):
    x, y, c = pos
    return (1 - x if k & 4 else x, 1 - y if k & 2 else y, 1 - c if k & 1 else c)


def _index(pos):
    return 4 * pos[0] + 2 * pos[1] + pos[2]


def _all_gather_rows(xs, name):
    n_arr = len(xs)
    chips = (2, 4, 6)

    def body(*refs):
        x_refs, out_refs = refs[:n_arr], refs[n_arr:2 * n_arr]
        send_sems, recv_sems, local_sems = refs[2 * n_arr:]
        me = _position()
        sibling = _flip(me, 1)

        def copy(i, sem, block, to, own=False):
            m_per = x_refs[i].shape[0]
            rows = out_refs[i].at[pl.ds(_index(block) * m_per, m_per), :]
            return pltpu.make_async_remote_copy(
                src_ref=x_refs[i] if own else rows, dst_ref=rows,
                send_sem=send_sems.at[7 * i + sem], recv_sem=recv_sems.at[7 * i + sem], device_id=to, device_id_type=MESH)

        mine = [pltpu.make_async_copy(x_refs[i], out_refs[i].at[pl.ds(_index(me) * x_refs[i].shape[0], x_refs[i].shape[0]), :],
                                      local_sems.at[i]) for i in range(n_arr)]
        first, passed = [], []
        for i in range(n_arr):
            first.append(copy(i, 0, me, sibling, own=True))
            first += [copy(i, 1 + j, me, _flip(me, k), own=True) for j, k in enumerate(chips)]
            passed.append([copy(i, 4 + j, _flip(me, k), sibling) for j, k in enumerate(chips)])
        for cp in mine + first:
            cp.start()
        for i in range(n_arr):
            for j, k in enumerate(chips):
                copy(i, 1 + j, _flip(me, k), me).wait_recv()
                passed[i][j].start()
        for i in range(n_arr):
            copy(i, 0, sibling, me).wait_recv()
            for j, k in enumerate(chips):
                copy(i, 4 + j, _flip(sibling, k), me).wait_recv()
        for cp in first + [cp for group in passed for cp in group]:
            cp.wait_send()
        for cp in mine:
            cp.wait()

    anyspec = pl.BlockSpec(memory_space=pl.ANY)
    return pl.pallas_call(
        body, out_shape=[jax.ShapeDtypeStruct((N_DEV * x.shape[0], x.shape[1]), x.dtype) for x in xs],
        in_specs=[anyspec] * n_arr, out_specs=[anyspec] * n_arr,
        scratch_shapes=[pltpu.SemaphoreType.DMA((7 * n_arr,)), pltpu.SemaphoreType.DMA((7 * n_arr,)),
                        pltpu.SemaphoreType.DMA((n_arr,))],
        name=name,
    )(*xs)


def _split_start(srcs, lands, plan, n_copies, name, after=()):
    n_arr = len(srcs)
    n_after = len(after)
    hbm = pl.BlockSpec(memory_space=pltpu.HBM)
    sem = pl.BlockSpec(memory_space=pltpu.SEMAPHORE)

    def body(*refs):
        src_refs, land_refs = refs[:n_arr], refs[n_arr:2 * n_arr]
        outs_at = 2 * n_arr + n_after
        send_sems, recv_sems = refs[outs_at:outs_at + n_arr], refs[outs_at + n_arr:outs_at + 2 * n_arr]
        token = refs[-1]
        me = _position()
        for i in range(n_arr):
            for j, (src, dst, peer, _) in enumerate(plan(i, src_refs[i], land_refs[i], me)):
                pltpu.make_async_remote_copy(src_ref=src, dst_ref=dst, send_sem=send_sems[i].at[j], recv_sem=recv_sems[i].at[j],
                                             device_id=peer, device_id_type=MESH).start()
        token[...] = jnp.zeros_like(token)

    outs = pl.pallas_call(
        body, name=name,
        out_shape=([pltpu.SemaphoreType.DMA((n_copies,))] * (2 * n_arr)
                   + [pltpu.HBM(a.shape, a.dtype) for a in list(srcs) + list(lands)]
                   + [jax.ShapeDtypeStruct((8, LANES), f32)]),
        in_specs=[hbm] * (2 * n_arr) + [pl.BlockSpec(memory_space=pl.ANY)] * n_after,
        out_specs=[sem] * (2 * n_arr) + [hbm] * (2 * n_arr) + [pl.BlockSpec(memory_space=pltpu.VMEM)],
        input_output_aliases={i: 2 * n_arr + i for i in range(2 * n_arr)},
        compiler_params=pltpu.CompilerParams(has_side_effects=pltpu.SideEffectType.DATAFLOW_SIDE_EFFECTING),
    )(*[pltpu.with_memory_space_constraint(a, pltpu.HBM) for a in list(srcs) + list(lands)], *after)
    return (outs[:n_arr], outs[n_arr:2 * n_arr], outs[2 * n_arr:3 * n_arr], outs[3 * n_arr:4 * n_arr], outs[-1])


def _split_wait(started, plan, after, name):
    send_sems, recv_sems, srcs, lands, _ = started
    n_arr = len(srcs)
    hbm = pl.BlockSpec(memory_space=pltpu.HBM)
    sem = pl.BlockSpec(memory_space=pltpu.SEMAPHORE)

    def body(*refs):
        src_refs, land_refs = refs[:n_arr], refs[n_arr:2 * n_arr]
        s_sems, r_sems = refs[2 * n_arr:3 * n_arr], refs[3 * n_arr:4 * n_arr]
        me = _position()
        for i in range(n_arr):
            for j, (src, _, peer, arrival) in enumerate(plan(i, src_refs[i], land_refs[i], me)):
                cp = pltpu.make_async_remote_copy(src_ref=src, dst_ref=arrival, send_sem=s_sems[i].at[j], recv_sem=r_sems[i].at[j],
                                                  device_id=peer, device_id_type=MESH)
                cp.wait_send()
                cp.wait_recv()

    outs = pl.pallas_call(
        body, name=name,
        out_shape=[pltpu.HBM(a.shape, a.dtype) for a in list(srcs) + list(lands)],
        in_specs=[hbm] * (2 * n_arr) + [sem] * (2 * n_arr) + [pl.BlockSpec(memory_space=pl.ANY)],
        out_specs=[hbm] * (2 * n_arr),
        input_output_aliases={i: i for i in range(2 * n_arr)},
        compiler_params=pltpu.CompilerParams(has_side_effects=pltpu.SideEffectType.DATAFLOW_SIDE_EFFECTING),
    )(*srcs, *lands, *send_sems, *recv_sems, after)
    return outs[:n_arr], outs[n_arr:]


_GATHER_FLIPS = (1, 2, 4, 6)


def _gather_plan(i, src_ref, land_ref, me):
    m = src_ref.shape[0]

    def rows(pos):
        return land_ref.at[pl.ds(_index(pos) * m, m), :]

    return [(src_ref, rows(me), _flip(me, k), rows(_flip(me, k))) for k in _GATHER_FLIPS]


def _gather_forward(lands, name):
    n_arr = len(lands)
    chips = (2, 4, 6)

    def body(*refs):
        out_refs = refs[n_arr:2 * n_arr]
        send_sems, recv_sems = refs[2 * n_arr:]
        me = _position()
        sibling = _flip(me, 1)
        sends, arrivals = [], []
        for i, out_ref in enumerate(out_refs):
            m = out_ref.shape[0] // N_DEV

            def copy(pos, j):
                blk = out_ref.at[pl.ds(_index(pos) * m, m), :]
                return pltpu.make_async_remote_copy(src_ref=blk, dst_ref=blk, send_sem=send_sems.at[3 * i + j],
                                                    recv_sem=recv_sems.at[3 * i + j], device_id=sibling, device_id_type=MESH)

            for j, k in enumerate(chips):
                sends.append(copy(_flip(me, k), j))
                arrivals.append(copy(_flip(sibling, k), j))
        for cp in sends:
            cp.start()
        for cp in arrivals:
            cp.wait_recv()
        for cp in sends:
            cp.wait_send()

    anyspec = pl.BlockSpec(memory_space=pl.ANY)
    return pl.pallas_call(
        body, out_shape=[jax.ShapeDtypeStruct(a.shape, a.dtype) for a in lands],
        in_specs=[anyspec] * n_arr, out_specs=[anyspec] * n_arr, input_output_aliases={i: i for i in range(n_arr)},
        scratch_shapes=[pltpu.SemaphoreType.DMA((3 * n_arr,))] * 2, name=name,
    )(*lands)


def _chips_plan(i, src_ref, land_ref, me):
    m = src_ref.shape[0] // 4
    plan = []
    for j, k in enumerate((2, 4, 6)):
        peer = _flip(me, k)
        plan.append((src_ref.at[pl.ds((2 * peer[0] + peer[1]) * m, m), :], land_ref.at[j], peer, land_ref.at[j]))
    return plan


def _sibling_plan(i, src_ref, land_ref, me):
    m = src_ref.shape[0] // N_DEV
    sibling = _flip(me, 1)
    return [(src_ref.at[pl.ds((2 * q + 1 - me[2]) * m, m), :], land_ref.at[q], sibling, land_ref.at[q]) for q in range(4)]


def _sum_with_sibling(g, recv, name):
    m = g.shape[0] // N_DEV
    n = g.shape[1]
    tr = _pick(m, (208, 128, 64, 32, 16))
    nt = m // tr

    def body(g_ref, r_ref, o_ref):
        c = lax.axis_index("c")
        own = jnp.where(c == 0, g_ref[0, 0].astype(f32), g_ref[0, 1].astype(f32))
        o_ref[...] = (own + r_ref[0].astype(f32)).astype(o_ref.dtype)

    return pl.pallas_call(
        body, grid=(4, nt),
        in_specs=[pl.BlockSpec((1, 2, tr, n), lambda q, i: (q, 0, i, 0)), pl.BlockSpec((1, tr, n), lambda q, i: (q, i, 0))],
        out_specs=pl.BlockSpec((tr, n), lambda q, i: (q * nt + i, 0)),
        out_shape=jax.ShapeDtypeStruct((4 * m, n), bf16), name=name,
    )(g.reshape(4, 2, m, n), recv)


def _sum_with_chips(h, recv, name):
    m = h.shape[0] // 4
    n = h.shape[1]
    tr = _pick(m, (208, 128, 64, 32, 16))

    def body(h_ref, r_ref, o_ref):
        my_q = 2 * lax.axis_index("x") + lax.axis_index("y")
        own = h_ref[0].astype(f32)
        for q in range(1, 4):
            own = jnp.where(my_q == q, h_ref[q].astype(f32), own)
        o_ref[...] = ((own + r_ref[0].astype(f32)) + r_ref[1].astype(f32)) + r_ref[2].astype(f32)

    return pl.pallas_call(
        body, grid=(m // tr,),
        in_specs=[pl.BlockSpec((4, tr, n), lambda i: (0, i, 0)), pl.BlockSpec((3, tr, n), lambda i: (0, i, 0))],
        out_specs=pl.BlockSpec((tr, n), lambda i: (i, 0)), out_shape=jax.ShapeDtypeStruct((m, n), f32), name=name,
    )(h.reshape(4, m, n), recv)


def _sum_slots(parts, name):
    n_slot, m, n = parts.shape
    tr = _pick(m, (208, 128, 64, 32, 16, 8))

    def body(p_ref, o_ref):
        acc = p_ref[0]
        for s in range(1, n_slot):
            acc = acc + p_ref[s]
        o_ref[...] = acc

    return pl.pallas_call(
        body, grid=(m // tr,), in_specs=[pl.BlockSpec((n_slot, tr, n), lambda i: (0, i, 0))],
        out_specs=pl.BlockSpec((tr, n), lambda i: (i, 0)), out_shape=jax.ShapeDtypeStruct((m, n), parts.dtype), name=name,
    )(parts)


def _reduce_scatter_begin(gs, name):
    lands = [lax.empty((4, g.shape[0] // N_DEV, g.shape[1]), g.dtype) for g in gs]
    return _split_start(gs, lands, _sibling_plan, 4, "rs_d2d_start_" + name)


def _reduce_scatter_middle(started, after, name):
    gs, from_sibling = _split_wait(started, _sibling_plan, after, "rs_d2d_wait_" + name)
    chip_sums = [_sum_with_sibling(g, r, f"rs_sum2_{name}_{i}") for i, (g, r) in enumerate(zip(gs, from_sibling))]
    lands = [lax.empty((3, h.shape[0] // 4, h.shape[1]), h.dtype) for h in chip_sums]
    return _split_start(chip_sums, lands, _chips_plan, 3, "rs_ici_start_" + name)


def _reduce_scatter_end(started, after, name):
    chip_sums, from_chips = _split_wait(started, _chips_plan, after, "rs_ici_wait_" + name)
    return [_sum_with_chips(h, r, f"rs_sum4_{name}_{i}") for i, (h, r) in enumerate(zip(chip_sums, from_chips))]


def _adamw_update(w, g, m, v):
    mm = ADAM_B1 * m + (1.0 - ADAM_B1) * g
    vv = ADAM_B2 * v + (1.0 - ADAM_B2) * jnp.square(g)
    m_hat = mm / (1.0 - ADAM_B1 ** ADAM_STEP)
    v_hat = vv / (1.0 - ADAM_B2 ** ADAM_STEP)
    return -ADAM_LR * (m_hat / (jnp.sqrt(v_hat) + ADAM_EPS) + ADAM_WD * w), mm, vv


def _adamw_many(ws, gs, ms, vs, name):
    k = len(ws)
    shapes = [w.shape for w in ws]
    flat = [[a.reshape(-1, a.shape[-1]) for a in group] for group in (ws, gs, ms, vs)]

    def body(*refs):
        for i in range(k):
            d, mm, vv = _adamw_update(*(refs[j * k + i][...] for j in range(4)))
            refs[4 * k + i][...] = d
            refs[5 * k + i][...] = mm
            refs[6 * k + i][...] = vv

    outs = pl.pallas_call(
        body, out_shape=[jax.ShapeDtypeStruct(a.shape, f32) for a in flat[0]] * 3, name=name,
    )(*flat[0], *flat[1], *flat[2], *flat[3])
    return tuple([outs[j * k + i].reshape(shapes[i]) for i in range(k)] for j in range(3))


def _adamw(w, g, m, v, name):
    shape = w.shape
    n = shape[-1]
    r = w.size // n
    w2, g2, m2, v2 = (a.reshape(r, n) for a in (w, g, m, v))
    tr = _pick(r, (256, 208, 128, 64, 32, 16, 8))

    def body(w_ref, g_ref, m_ref, v_ref, d_ref, mo_ref, vo_ref):
        d_ref[...], mo_ref[...], vo_ref[...] = _adamw_update(w_ref[...], g_ref[...], m_ref[...], v_ref[...])

    spec = pl.BlockSpec((tr, n), lambda i: (i, 0))
    outs = pl.pallas_call(
        body, grid=(r // tr,), in_specs=[spec] * 4, out_specs=[spec] * 3,
        out_shape=[jax.ShapeDtypeStruct((r, n), f32)] * 3, name=name,
    )(w2, g2, m2, v2)
    return tuple(o.reshape(shape) for o in outs)


_SMALL = ("shift_mu", "w_decay0", "a0", "k_k", "k_a", "r_k", "ln_x_w", "ln_x_b", "v_mix0", "lb_logits",
          "g_norm_w", "ln_w", "ln_b")
_NAMES = ("w_in", "shift_mu", "w_decay0", "w_decay_up", "a0", "a_up", "k_k", "k_a", "r_k", "ln_x_w", "ln_x_b",
          "v_mix0", "v_mix_down", "v_mix_up", "lb_logits", "g_norm_w", "w_out", "ln_w", "ln_b")


def _pad_rows(a, rows, at_end):
    z = jnp.zeros((rows - a.shape[0], a.shape[1]), a.dtype)
    return jnp.concatenate([a, z] if at_end else [z, a], axis=0)


def kernel(x, w_in, shift_mu, w_decay0, w_decay_up, a0, a_up, k_k, k_a, r_k, ln_x_w, ln_x_b, v_mix0, v_mix_down, v_mix_up, lb_logits, g_norm_w, w_out, ln_w, ln_b, loss_target, m_w_in, m_shift_mu, m_w_decay0, m_w_decay_up, m_a0, m_a_up, m_k_k, m_k_a, m_r_k, m_ln_x_w, m_ln_x_b, m_v_mix0, m_v_mix_down, m_v_mix_up, m_lb_logits, m_g_norm_w, m_w_out, m_ln_w, m_ln_b, v_w_in, v_shift_mu, v_w_decay0, v_w_decay_up, v_a0, v_a_up, v_k_k, v_k_a, v_r_k, v_ln_x_w, v_ln_x_b, v_v_mix0, v_v_mix_down, v_v_mix_up, v_lb_logits, v_g_norm_w, v_w_out, v_ln_w, v_ln_b):
    weights = dict(w_in=w_in, shift_mu=shift_mu, w_decay0=w_decay0, w_decay_up=w_decay_up, a0=a0, a_up=a_up, k_k=k_k,
                   k_a=k_a, r_k=r_k, ln_x_w=ln_x_w, ln_x_b=ln_x_b, v_mix0=v_mix0, v_mix_down=v_mix_down,
                   v_mix_up=v_mix_up, lb_logits=lb_logits, g_norm_w=g_norm_w, w_out=w_out, ln_w=ln_w, ln_b=ln_b)
    mom1 = dict(w_in=m_w_in, shift_mu=m_shift_mu, w_decay0=m_w_decay0, w_decay_up=m_w_decay_up, a0=m_a0, a_up=m_a_up,
                k_k=m_k_k, k_a=m_k_a, r_k=m_r_k, ln_x_w=m_ln_x_w, ln_x_b=m_ln_x_b, v_mix0=m_v_mix0,
                v_mix_down=m_v_mix_down, v_mix_up=m_v_mix_up, lb_logits=m_lb_logits, g_norm_w=m_g_norm_w,
                w_out=m_w_out, ln_w=m_ln_w, ln_b=m_ln_b)
    mom2 = dict(w_in=v_w_in, shift_mu=v_shift_mu, w_decay0=v_w_decay0, w_decay_up=v_w_decay_up, a0=v_a0, a_up=v_a_up,
                k_k=v_k_k, k_a=v_k_a, r_k=v_r_k, ln_x_w=v_ln_x_w, ln_x_b=v_ln_x_b, v_mix0=v_v_mix0,
                v_mix_down=v_v_mix_down, v_mix_up=v_v_mix_up, lb_logits=v_lb_logits, g_norm_w=v_g_norm_w,
                w_out=v_w_out, ln_w=v_ln_w, ln_b=v_ln_b)
    assert x.shape[0] == 1 and w_in.shape[0] == DEPTH
    t, d = x.shape[1], x.shape[2]
    dr = w_decay0.shape[1]
    dh = g_norm_w.shape[1]
    rank_w, rank_a, rank_v = w_decay_up.shape[1], a_up.shape[1], v_mix_up.shape[1]
    rwc = 4 * dr + rank_w + rank_a
    assert rank_w + rank_a == LANES and rank_v <= LANES and dr + dh == d
    assert t % CHUNK == 0 and dr % LANES == 0 and dh % LANES == 0 and shift_mu.shape[1] == rwc
    n_pair = dr // LANES
    me = _index(_position())

    shard = dr // N_DEV
    pack = jnp.concatenate([w_decay_up[0], w_decay_up[1], a_up[0], a_up[1], v_mix_up[0], v_mix_down[0].T], axis=0)
    win_t0, pack = _all_gather_rows([w_in[0].T.astype(bf16), pack], "ag_first")
    win_t = [win_t0, None]
    wout = [None, None]

    def start_gather(blocks, name, after):
        lands = [lax.dynamic_update_slice(lax.empty((N_DEV * blk.shape[0], blk.shape[1]), bf16), blk, (me * blk.shape[0], 0))
                 for blk in blocks]
        return _split_start(blocks, lands, _gather_plan, len(_GATHER_FLIPS), name, after=after)

    gather_wout0 = start_gather([w_out[0].astype(bf16)], "ag_wout0_start", (win_t[0], pack))
    gather_layer1 = start_gather([w_in[1].T.astype(bf16), w_out[1].astype(bf16)], "ag_layer1_start", (gather_wout0[-1],))
    pack = jnp.transpose(pack.reshape(N_DEV, -1, shard), (1, 0, 2)).reshape(-1, dr)
    offs = [0, rank_w, 2 * rank_w, 2 * rank_w + rank_a, 2 * rank_w + 2 * rank_a, 2 * rank_w + 2 * rank_a + rank_v,
            2 * rank_w + 2 * rank_a + 2 * rank_v]
    wdu_f = [pack[offs[0]:offs[1]], pack[offs[1]:offs[2]]]
    aup_f = [pack[offs[2]:offs[3]], pack[offs[3]:offs[4]]]
    vup_f = pack[offs[4]:offs[5]]
    vdown_f = pack[offs[5]:offs[6]].T

    def after_start(a, started):
        return a + started[-1][0:1, 0:1]

    def rwkv_params(l):
        mu = after_start(shift_mu[0:1], gather_layer1) if l == 0 else shift_mu[l:l + 1]
        prm = [mu, w_decay0[l:l + 1], a0[l:l + 1], _pad_rows(wdu_f[l], LANES, True),
               _pad_rows(aup_f[l], LANES, False)]
        if l == 1:
            prm += [v_mix0[0:1], _pad_rows(vdown_f.T, LANES, True).T, _pad_rows(vup_f, LANES, True)]
        rows = jnp.stack([k_k[l], k_a[l], r_k[l], ln_x_w[l], ln_x_b[l]] + [jnp.zeros((dr,), f32)] * 3, axis=0)
        pp = jnp.transpose(rows.reshape(8, n_pair, LANES), (1, 0, 2))
        return tuple(prm), pp

    h = x[0]
    h16 = h.astype(bf16)
    tgt = loss_target[0]
    saved = []
    vfirst = None
    for l in range(DEPTH):
        prm, pp = rwkv_params(l)
        proj = _matmul(h16, win_t[l], "nt", f"mm_proj_{l}", _MM_TILES["proj"])
        if l == 0:
            cat, vfirst, mck = _rwkv_fwd(False, proj, None, prm, pp, d)
        else:
            cat, mck = _rwkv_fwd(True, proj, vfirst, prm, pp, d)
        cat, sck = _hgrn_fwd(l == 1, proj, lb_logits, g_norm_w[l:l + 1], cat, rwc)
        if l == 0:
            _, arrived = _split_wait(gather_wout0, _gather_plan, cat, "ag_wout0_wait")
            (wout[0],) = _gather_forward(arrived, "ag_wout0_forward")
        y = _matmul(cat, wout[l], "nn", f"mm_out_{l}", _MM_TILES["out"])
        saved.append((h, h16, proj, prm, pp, mck, sck, cat, y))
        if l < DEPTH - 1:
            h, h16 = _ln_fwd(h, y, ln_w[l:l + 1], ln_b[l:l + 1])
            _, arrived = _split_wait(gather_layer1, _gather_plan, h16, "ag_layer1_wait")
            win_t[1], wout[1] = _gather_forward(arrived, "ag_layer1_forward")
        else:
            top = _ln_loss_bwd(h, y, ln_w[l:l + 1], ln_b[l:l + 1], tgt)
    loss = lax.psum(top[4][0, 0], ("x", "y", "c"))

    grads = {}
    big = {}
    dvfirst = None
    d_lbl = None
    rs_started = {}
    for l in reversed(range(DEPTH)):
        h_l, h16_l, proj, prm, pp, mck, sck, cat, y = saved[l]
        if l == DEPTH - 1:
            dy, dy16, g_ln_w, g_ln_b = top[:4]
        else:
            dy, dy16, g_ln_w, g_ln_b = _ln_bwd(h_l, y, after_start(ln_w[l:l + 1], rs_started[l + 1]), ln_b[l:l + 1], dh_out)
        dcat = _matmul(dy16, wout[l], "nt", f"mm_dcat_{l}", _MM_TILES["dcat"])
        big[("w_out", l)] = _matmul(cat, dy16, "tn", f"mm_dwout_{l}", _MM_TILES["dwout"], out_dtype=bf16)
        if l == 1:
            outs = _rwkv_bwd(True, proj, vfirst, prm, pp, mck, dcat, None)
            dproj_r, dvfirst = outs[0], outs[1]
            dprm, dpp = outs[2:-1], outs[-1]
        else:
            outs = _rwkv_bwd(False, proj, None, prm, pp, mck, dcat, dvfirst)
            dproj_r = outs[0]
            dprm, dpp = outs[1:-1], outs[-1]
        dproj, dlbl_l, dgnw = _hgrn_bwd(l == 1, proj, lb_logits, g_norm_w[l:l + 1], sck, dcat, rwc, dproj_r)
        big[("w_in", l)] = _matmul(dproj, h16_l, "tn", f"mm_dwin_{l}", _MM_TILES["dwin"], out_dtype=bf16)
        sharded = [dprm[3][:rank_w].T, dprm[4][rank_w:].T]
        if l == 1:
            sharded += [dprm[6][:, :rank_v], dprm[7][:rank_v].T,
                        jnp.zeros((dr, LANES - 2 * rank_v), f32)]
        sharded = jnp.concatenate(sharded, axis=1).astype(bf16)
        d2d = _reduce_scatter_begin([big[("w_in", l)], big[("w_out", l)], sharded], f"l{l}")
        if l == 0:
            rs_started[l] = _reduce_scatter_middle(d2d, sharded, f"l{l}")
            token = rs_started[l][-1]
        else:
            token = d2d[-1]
        dh_out = _matmul(dproj, win_t[l], "nn", f"mm_dh_{l}", _MM_TILES["dh"], add=dy, add_scale=ALPHA, after=token)
        if l > 0:
            rs_started[l] = _reduce_scatter_middle(d2d, dh_out, f"l{l}")
        dpp = jnp.transpose(dpp, (1, 0, 2)).reshape(8, dr)
        grads[l] = dict(shift_mu=dprm[0][0], w_decay0=dprm[1][0], a0=dprm[2][0],
                        k_k=dpp[0], k_a=dpp[1], r_k=dpp[2], ln_x_w=dpp[3], ln_x_b=dpp[4],
                        g_norm_w=dgnw[0], ln_w=g_ln_w[0], ln_b=g_ln_b[0])
        if l == 1:
            grads[l].update(v_mix0=dprm[5][0])
            d_lbl = dlbl_l
    grad_x = dh_out[None]

    def both(name):
        return jnp.stack([grads[0][name], grads[1][name]])

    small = dict(shift_mu=both("shift_mu"), w_decay0=both("w_decay0"), a0=both("a0"), k_k=both("k_k"), k_a=both("k_a"),
                 r_k=both("r_k"), ln_x_w=both("ln_x_w"), ln_x_b=both("ln_x_b"), v_mix0=grads[1]["v_mix0"][None],
                 lb_logits=d_lbl, g_norm_w=both("g_norm_w"), ln_w=both("ln_w"), ln_b=both("ln_b"))
    flat = jnp.concatenate([small[nm].reshape(-1) for nm in _SMALL])
    n_flat = flat.shape[0]
    rows = -(-n_flat // (8 * LANES)) * 8
    flat = jnp.concatenate([flat, jnp.zeros((rows * LANES - n_flat,), f32)]).reshape(rows, LANES)
    total = _sum_slots(_all_gather_rows([flat], "ag_small_grads")[0].reshape(N_DEV, rows, LANES), "sum_small_grads").reshape(-1)
    gsm = {}
    off = 0
    for nm in _SMALL:
        size = small[nm].size
        gsm[nm] = total[off:off + size].reshape(small[nm].shape)
        off += size
    reduced = {1: _reduce_scatter_end(rs_started[1], dh_out, "l1")}
    reduced[0] = _reduce_scatter_end(rs_started[0], total, "l0")
    g_w_in_t = jnp.stack([reduced[l][0] for l in range(DEPTH)])
    gsm["w_in"] = jnp.transpose(g_w_in_t, (0, 2, 1))
    gsm["w_out"] = jnp.stack([reduced[l][1] for l in range(DEPTH)])
    gsm["w_decay_up"] = jnp.stack([reduced[l][2][:, :rank_w].T for l in range(DEPTH)])
    gsm["a_up"] = jnp.stack([reduced[l][2][:, rank_w:rank_w + rank_a].T for l in range(DEPTH)])
    gsm["v_mix_down"] = reduced[1][2][:, LANES:LANES + rank_v][None]
    gsm["v_mix_up"] = reduced[1][2][:, LANES + rank_v:LANES + 2 * rank_v].T[None]

    deltas, new_m, new_v = {}, {}, {}
    swap = lambda a: jnp.transpose(a, (0, 2, 1))
    deltas["w_in"], new_m["w_in"], new_v["w_in"] = (
        swap(a) for a in _adamw(swap(w_in), g_w_in_t, swap(m_w_in), swap(v_w_in), "adamw_w_in"))
    deltas["w_out"], new_m["w_out"], new_v["w_out"] = _adamw(w_out, gsm["w_out"], m_w_out, v_w_out, "adamw_w_out")
    rest = [nm for nm in _NAMES if nm not in ("w_in", "w_out")]
    d_rest, m_rest, v_rest = _adamw_many([weights[nm] for nm in rest], [gsm[nm] for nm in rest],
                                         [mom1[nm] for nm in rest], [mom2[nm] for nm in rest], "adamw_small")
    for i, nm in enumerate(rest):
        deltas[nm], new_m[nm], new_v[nm] = d_rest[i], m_rest[i], v_rest[i]
    return (loss, grad_x, *[gsm[nm] for nm in _NAMES], *[deltas[nm] for nm in _NAMES],
            *[new_m[nm] for nm in _NAMES], *[new_v[nm] for nm in _NAMES])
```

```python
import functools

import jax
import jax.numpy as jnp
from jax import lax
from jax.experimental import pallas as pl
from jax.experimental.pallas import tpu as pltpu

f32 = jnp.float32
bf16 = jnp.bfloat16

N_DEV = 8
CHUNK = 64
LANES = 128
RWKV_HEAD = 64
DEPTH = 2
ALPHA = (2 * DEPTH) ** 0.25
LN_EPS = 1e-5
GN_EPS = 64e-5
RMS_EPS = 1e-5
LB_FLOOR = 1e-30
ADAM_LR, ADAM_B1, ADAM_B2, ADAM_EPS, ADAM_WD, ADAM_STEP = 0.001, 0.9, 0.999, 1e-08, 0.01, 10
MESH = pl.DeviceIdType.MESH


def _iota(shape, d):
    return lax.broadcasted_iota(jnp.int32, shape, d)


_DIMS = {"nn": (((1,), (0,)), ((), ())), "nt": (((1,), (1,)), ((), ())), "tn": (((0,), (0,)), ((), ()))}
_BATCH_DIMS = {"nn": (((2,), (1,)), ((0,), (0,))), "nt": (((2,), (2,)), ((0,), (0,))), "tn": (((1,), (1,)), ((0,), (0,)))}
_K_AXES = {"nn": (-1, -2), "nt": (-1, -1), "tn": (-2, -2)}


def _mxu(a, b, mode):
    return lax.dot_general(a, b, (_BATCH_DIMS if a.ndim == 3 else _DIMS)[mode], preferred_element_type=f32)


def _split(x):
    hi = x.astype(bf16)
    return hi, (x - hi.astype(f32)).astype(bf16)


def _mm2_impl(a, b, mode, passes=3):
    if passes == 1:
        return _mxu(a.astype(bf16), b.astype(bf16), mode)
    ah, al = _split(a)
    if passes == 3:
        bh, bl = _split(b)
        lhs, rhs = [ah, ah, al], [bh, bl, bh]
    else:
        bh = b.astype(bf16)
        lhs, rhs = [ah, al], [bh, bh]
    ka, kb = _K_AXES[mode]
    k = a.shape[ka]
    if k % (LANES if -1 in (ka, kb) else 16) == 0:
        return _mxu(jnp.concatenate(lhs, axis=ka), jnp.concatenate(rhs, axis=kb), mode)
    out = _mxu(lhs[0], rhs[0], mode)
    for x, y in zip(lhs[1:], rhs[1:]):
        out = out + _mxu(x, y, mode)
    return out


@functools.partial(jax.custom_vjp, nondiff_argnums=(2, 3))
def _mm2(a, b, mode, passes=3):
    return _mm2_impl(a, b, mode, passes)


def _mm2_fwd(a, b, mode, passes):
    return _mm2_impl(a, b, mode, passes), (a, b)


def _mm2_bwd(mode, passes, res, g):
    a, b = res
    if mode == "nn":
        return _mm2_impl(g, b, "nt", passes), _mm2_impl(a, g, "tn", passes)
    if mode == "nt":
        return _mm2_impl(g, b, "nn", passes), _mm2_impl(g, a, "tn", passes)
    return _mm2_impl(b, g, "nt", passes), _mm2_impl(a, g, "nn", passes)


_mm2.defvjp(_mm2_fwd, _mm2_bwd)

TRI_PASSES = 1
APPLY_PASSES = 1


def _const_impl(cm, x, mode):
    if mode in ("r", "rt"):
        shape = x.shape
        out = _mxu(x.astype(bf16).reshape(-1, shape[-1]), cm, "nn" if mode == "r" else "nt")
        return out.reshape(shape[:-1] + (out.shape[-1],))
    hi, lo = _split(x)
    if x.ndim == 3:
        cm = jnp.broadcast_to(cm, (x.shape[0],) + cm.shape)
    return _mxu(cm, hi, mode) + _mxu(cm, lo, mode)


@jax.custom_vjp
def _const_left(cm, x):
    return _const_impl(cm, x, "nn")


_const_left.defvjp(lambda cm, x: (_const_impl(cm, x, "nn"), cm),
                   lambda cm, g: (jnp.zeros_like(cm), _const_impl(cm, g, "tn")))


@jax.custom_vjp
def _const_right(x, cm):
    return _const_impl(cm, x, "r")


_const_right.defvjp(lambda x, cm: (_const_impl(cm, x, "r"), cm),
                    lambda cm, g: (_const_impl(cm, g, "rt"), jnp.zeros_like(cm)))


def _tri_inv(a):
    n = a.shape[-1]
    tm = (_iota((n, n), 0) == _iota((n, n), 1)).astype(f32) + a
    ak = a
    for _ in range(5):
        ak = _mm2_impl(ak, ak, "nn", TRI_PASSES)
        tm = tm + _mm2_impl(tm, ak, "nn", TRI_PASSES)
    return tm


@jax.custom_vjp
def _tri_solve(tm, a, x):
    del a
    return _mm2_impl(tm, x, "nn", APPLY_PASSES)


def _tri_solve_fwd(tm, a, x):
    u = _mm2_impl(tm, x, "nn", APPLY_PASSES)
    return u, (tm, u)


def _tri_solve_bwd(res, du):
    tm, u = res
    dx = _mm2_impl(tm, du, "tn", APPLY_PASSES)
    return jnp.zeros_like(tm), _mm2_impl(dx, u, "nt", APPLY_PASSES), dx


_tri_solve.defvjp(_tri_solve_fwd, _tri_solve_bwd)


def _col_of_row(row_vec):
    n = row_vec.shape[-1]
    eye = _iota((n, n), 0) == _iota((n, n), 1)
    return jnp.sum(jnp.where(eye, jnp.broadcast_to(row_vec, row_vec.shape[:-2] + (n, n)), 0.0), axis=-1, keepdims=True)


def _softplus(x):
    return jnp.maximum(x, 0.0) + jnp.log1p(jnp.exp(-jnp.abs(x)))


def _log_sigmoid(x):
    return -_softplus(-x)


def _logaddexp(a, b):
    return jnp.maximum(a, b) + jnp.log1p(jnp.exp(-jnp.abs(a - b)))


def _silu(x):
    return x * jax.nn.sigmoid(x)


def _tril(c, strict):
    r, s = _iota((c, c), 0), _iota((c, c), 1)
    return (r > s) if strict else (r >= s)


def _last_row(a):
    c = a.shape[-2]
    return jnp.sum(jnp.where(_iota(a.shape, a.ndim - 2) == c - 1, a, 0.0), axis=-2, keepdims=True)


def _rwkv_pre(layer1, prm, y, prev, vf):
    c = y.shape[0]
    if layer1:
        mu, w0, a0, wup, aup, v0, vdown, vup = prm
    else:
        mu, w0, a0, wup, aup = prm
    dr = w0.shape[1]
    shift = (_iota((c, c), 0) == _iota((c, c), 1) + 1).astype(bf16)
    y_prev = _const_left(shift, y) + jnp.where(_iota((c, 1), 0) == 0, prev, 0.0)
    rw = y + mu * (y_prev - y)
    r, k, v, z = (rw[:, i * dr:(i + 1) * dr] for i in range(4))
    wdad = rw[:, 4 * dr:4 * dr + LANES]
    w_raw = w0 + _mm2(jnp.tanh(wdad), wup, "nn")
    lw = -jnp.exp(-_softplus(-w_raw) - 0.5)
    asig = jax.nn.sigmoid(a0 + _mm2(wdad, aup, "nn"))
    if layer1:
        v = v + (vf - v) * jax.nn.sigmoid(v0 + _mm2(_mm2(v, vdown, "nn"), vup, "nn"))
    return r, k, v, z, lw, asig


def _rwkv_pair(pp, m0, xs, tm=None):
    kkw, kaw, rkw, gnw, gnb = pp
    r, k, v, z, lw, asig = xs
    c = r.shape[-2]
    n2 = 2 * c
    lane = _iota((1, LANES), 1)
    mh0, mh1 = (lane < RWKV_HEAD).astype(f32), (lane >= RWKV_HEAD).astype(f32)
    same_head = _iota((LANES, LANES), 0) // RWKV_HEAD == _iota((LANES, LANES), 1) // RWKV_HEAD
    g = same_head.astype(bf16)

    def seg(x):
        return _const_right(x, g)

    def stack(x):
        return jnp.concatenate([x * mh0, x * mh1], axis=-2)

    kk = k * kkw
    kk = kk / jnp.maximum(jnp.sqrt(seg(kk * kk)), 1e-12)
    k2 = k * (1.0 + (asig - 1.0) * kaw)
    a = -kk
    b = kk * asig
    cum = _const_left(_tril(c, False).astype(bf16), lw)
    at = stack(a * jnp.exp(cum - lw))
    rt = stack(r * jnp.exp(cum))
    en = jnp.exp(-cum)
    sc = _mm2(jnp.concatenate([at, rt], axis=-2), jnp.concatenate([stack(b * en), stack(k2 * en)], axis=-2), "nt")
    row, col = _iota((n2, n2), 0), _iota((n2, n2), 1)
    same = row // c == col // c
    strict = same & (row % c > col % c)
    incl = same & (row % c >= col % c)
    aab = jnp.where(strict, sc[..., :n2, :n2], 0.0)
    aak = jnp.where(strict, sc[..., :n2, n2:], 0.0)
    arb = jnp.where(incl, sc[..., n2:, :n2], 0.0)
    ark = jnp.where(incl, sc[..., n2:, n2:], 0.0)
    vv = jnp.concatenate([v, v], axis=-2)
    mask_st = jnp.concatenate([jnp.broadcast_to(mh0, (c, LANES)), jnp.broadcast_to(mh1, (c, LANES))], axis=0)
    x_st = _mm2(jnp.concatenate([at, aak], axis=-1), jnp.concatenate([m0, vv], axis=-2), "nn", APPLY_PASSES)
    if tm is None:
        tm = _tri_inv(lax.stop_gradient(aab))
    u_st = _tri_solve(tm, aab, x_st) * mask_st
    o_st = _mm2(jnp.concatenate([rt, arb, ark], axis=-1), jnp.concatenate([m0, u_st, vv], axis=-2), "nn", APPLY_PASSES) * mask_st
    u = u_st[..., :c, :] + u_st[..., c:, :]
    o = o_st[..., :c, :] + o_st[..., c:, :]
    cum_last = _last_row(cum)
    dec_end = jnp.exp(cum_last - cum)
    m_new = _col_of_row(jnp.exp(cum_last)) * m0 + _mm2(
        jnp.concatenate([b * dec_end, k2 * dec_end], axis=-2), jnp.concatenate([u, v], axis=-2), "tn", APPLY_PASSES) * same_head.astype(f32)
    mean = seg(o) * (1.0 / RWKV_HEAD)
    d = o - mean
    var = seg(d * d) * (1.0 / RWKV_HEAD)
    on = d * lax.rsqrt(var + GN_EPS) * gnw + gnb
    bonus = seg(r * k2 * rkw) * v
    return (on + bonus) * _silu(z), m_new, tm


def _split_lanes(a, n):
    return [a[:, i * LANES:(i + 1) * LANES] for i in range(n)]


def _rwkv_step(layer1, prm, y, prev, vf, pp, m0, tm=None):
    xs = _rwkv_pre(layer1, prm, y, prev, vf)
    n_pair = m0.shape[0]
    og, m_new, tm = _rwkv_pair(pp, m0, tuple(jnp.concatenate([p[None] for p in _split_lanes(a, n_pair)], axis=0) for a in xs), tm)
    return og, m_new, xs[2], tm


def _group(n):
    return n


def _rwkv_specs(layer1, t, dr, rwc, n_pair, rev):
    nc = t // CHUNK
    grp = _group(n_pair)

    def cidx(c):
        return (nc - 1 - c) if rev else c

    full = lambda shape: pl.BlockSpec(shape, lambda c, p: tuple(0 for _ in shape))
    specs = [
        pl.BlockSpec((CHUNK, rwc), lambda c, p: (cidx(c), 0)),
        pl.BlockSpec((8, rwc), lambda c, p: (jnp.maximum(cidx(c) * (CHUNK // 8) - 1, 0), 0)),
    ]
    if layer1:
        specs.append(pl.BlockSpec((CHUNK, dr), lambda c, p: (cidx(c), 0)))
    prm_shapes = [(1, rwc), (1, dr), (1, dr), (LANES, dr), (LANES, dr)]
    if layer1:
        prm_shapes += [(1, dr), (dr, LANES), (LANES, dr)]
    specs += [full(s) for s in prm_shapes]
    specs.append(pl.BlockSpec((grp, 8, LANES), lambda c, p: (p, 0, 0)))
    return specs, prm_shapes, cidx, full


def _rwkv_fwd(layer1, proj, vf, prm, pp, cat_width):
    t = proj.shape[0]
    dr = prm[1].shape[1]
    rwc = prm[0].shape[1]
    n_pair = dr // LANES
    nc = t // CHUNK
    n_prm = len(prm)
    specs, _, _, _ = _rwkv_specs(layer1, t, dr, rwc, n_pair, False)

    def body(*refs):
        y_ref, prev_ref = refs[0], refs[1]
        i = 2
        vf_ref = None
        if layer1:
            vf_ref = refs[i]
            i += 1
        prm_refs = refs[i:i + n_prm]
        i += n_prm
        pp_ref = refs[i]
        i += 1
        cat_ref = refs[i]
        i += 1
        vout_ref = None
        if not layer1:
            vout_ref = refs[i]
            i += 1
        mck_ref, m_s = refs[i], refs[i + 1]
        c = pl.program_id(0)

        @pl.when(c == 0)
        def _():
            m_s[...] = jnp.zeros_like(m_s)

        prev = prev_ref[pl.ds(7, 1), :] * (c != 0).astype(f32)
        m0 = m_s[...]
        ppv = tuple(pp_ref[:, pl.ds(q, 1), :] for q in range(5))
        og, m_new, v, tm = _rwkv_step(layer1, tuple(r[...] for r in prm_refs), y_ref[...], prev,
                                      vf_ref[...] if layer1 else None, ppv, m0)
        mck_ref[0, :n_pair] = m0
        mck_ref[0, n_pair:] = tm
        if not layer1:
            vout_ref[...] = v
        for j in range(n_pair):
            cat_ref[:, j * LANES:(j + 1) * LANES] = og[j]
        m_s[...] = m_new

    grp = _group(n_pair)
    assert grp == n_pair
    out_shape = [jax.ShapeDtypeStruct((t, cat_width), f32)]
    out_specs = [pl.BlockSpec((CHUNK, grp * LANES), lambda c, p: (c, p))]
    if not layer1:
        out_shape.append(jax.ShapeDtypeStruct((t, dr), f32))
        out_specs.append(pl.BlockSpec((CHUNK, dr), lambda c, p: (c, 0)))
    out_shape.append(jax.ShapeDtypeStruct((nc, 2 * n_pair, LANES, LANES), f32))
    out_specs.append(pl.BlockSpec((1, 2 * grp, LANES, LANES), lambda c, p: (c, p, 0, 0)))
    args = [proj, proj] + ([vf] if layer1 else []) + list(prm) + [pp]
    return pl.pallas_call(
        body, grid=(nc, 1), in_specs=specs, out_specs=out_specs, out_shape=out_shape,
        scratch_shapes=[pltpu.VMEM((n_pair, LANES, LANES), f32)],
        compiler_params=pltpu.CompilerParams(dimension_semantics=("arbitrary", "arbitrary")),
        name=f"rwkv_fwd_l{int(layer1)}",
    )(*args)


def _rwkv_bwd(layer1, proj, vf, prm, pp, mck, dcat, dvout):
    t = proj.shape[0]
    dr = prm[1].shape[1]
    rwc = prm[0].shape[1]
    n_pair = dr // LANES
    nc = t // CHUNK
    n_prm = len(prm)
    specs, prm_shapes, cidx, full = _rwkv_specs(layer1, t, dr, rwc, n_pair, True)
    grp = _group(n_pair)
    assert grp == n_pair
    specs.append(pl.BlockSpec((1, 2 * grp, LANES, LANES), lambda c, p: (cidx(c), p, 0, 0)))
    specs.append(pl.BlockSpec((CHUNK, grp * LANES), lambda c, p: (cidx(c), p)))
    if not layer1:
        specs.append(pl.BlockSpec((CHUNK, dr), lambda c, p: (cidx(c), 0)))

    def body(*refs):
        y_ref, prev_ref = refs[0], refs[1]
        i = 2
        vf_ref = None
        if layer1:
            vf_ref = refs[i]
            i += 1
        prm_refs = refs[i:i + n_prm]
        i += n_prm
        pp_ref, mck_ref, dog_ref = refs[i], refs[i + 1], refs[i + 2]
        i += 3
        dvout_ref = None
        if not layer1:
            dvout_ref = refs[i]
            i += 1
        dy_ref = refs[i]
        i += 1
        dvf_ref = None
        if layer1:
            dvf_ref = refs[i]
            i += 1
        dprm_refs = refs[i:i + n_prm]
        i += n_prm
        dpp_ref = refs[i]
        dm_s, dprev_s = refs[i + 1:i + 3]
        c = pl.program_id(0)
        cr = nc - 1 - c

        @pl.when(c == 0)
        def _():
            dm_s[...] = jnp.zeros_like(dm_s)
            dprev_s[...] = jnp.zeros_like(dprev_s)
            dpp_ref[...] = jnp.zeros_like(dpp_ref)
            for r in dprm_refs:
                r[...] = jnp.zeros_like(r)

        prev = prev_ref[pl.ds(7, 1), :] * (cr != 0).astype(f32)
        prm_v = tuple(r[...] for r in prm_refs)
        ppv = tuple(pp_ref[:, pl.ds(q, 1), :] for q in range(5))
        dog = jnp.stack([dog_ref[:, j * LANES:(j + 1) * LANES] for j in range(n_pair)], axis=0)
        m0, tm = mck_ref[0, :n_pair], mck_ref[0, n_pair:]
        no_tm = jnp.zeros_like(tm)
        if layer1:
            _, vjp = jax.vjp(lambda a, b, d, e, g, h: _rwkv_step(True, a, b, d, e, g, h, tm),
                             prm_v, y_ref[...], prev, vf_ref[...], ppv, m0)
            dprm, dy, dprev, dvf, dppv, dm0 = vjp((dog, dm_s[...], jnp.zeros((CHUNK, dr), f32), no_tm))
            dvf_ref[...] = dvf
        else:
            _, vjp = jax.vjp(lambda a, b, d, e, g: _rwkv_step(False, a, b, d, None, e, g, tm), prm_v, y_ref[...], prev, ppv, m0)
            dprm, dy, dprev, dppv, dm0 = vjp((dog, dm_s[...], dvout_ref[...], no_tm))
        dm_s[...] = dm0
        for q in range(5):
            dpp_ref[:, pl.ds(q, 1), :] += dppv[q]
        dy_ref[...] = (dy + jnp.where(_iota((CHUNK, 1), 0) == CHUNK - 1, dprev_s[...], 0.0)).astype(bf16)
        dprev_s[...] = dprev
        for r, gval in zip(dprm_refs, dprm):
            r[...] += gval

    out_shape = [jax.ShapeDtypeStruct((t, proj.shape[1]), bf16)]
    out_specs = [pl.BlockSpec((CHUNK, rwc), lambda c, p: (cidx(c), 0))]
    if layer1:
        out_shape.append(jax.ShapeDtypeStruct((t, dr), f32))
        out_specs.append(pl.BlockSpec((CHUNK, dr), lambda c, p: (cidx(c), 0)))
    out_shape += [jax.ShapeDtypeStruct(s, f32) for s in prm_shapes]
    out_specs += [full(s) for s in prm_shapes]
    out_shape.append(jax.ShapeDtypeStruct((n_pair, 8, LANES), f32))
    out_specs.append(full((n_pair, 8, LANES)))
    args = [proj, proj] + ([vf] if layer1 else []) + list(prm) + [pp, mck, dcat] + ([] if layer1 else [dvout])
    return pl.pallas_call(
        body, grid=(nc, 1), in_specs=specs, out_specs=out_specs, out_shape=out_shape,
        scratch_shapes=[pltpu.VMEM((n_pair, LANES, LANES), f32), pltpu.VMEM((1, rwc), f32)],
        compiler_params=pltpu.CompilerParams(dimension_semantics=("arbitrary", "arbitrary")),
        name=f"rwkv_bwd_l{int(layer1)}",
    )(*args)


def _hgrn_chunk(layer1, lbl, gnw, s0, q_raw, f_raw, i_in, z):
    c = q_raw.shape[-2]
    q = _silu(q_raw)
    ls = _log_sigmoid(f_raw)
    if layer1:
        l0, l1 = lbl[..., 0:1, :], lbl[..., 1:2, :]
        mx = jnp.maximum(l0, l1)
        e0, e1 = jnp.exp(l0 - mx), jnp.exp(l1 - mx)
        sm0, sm1 = e0 / (e0 + e1), e1 / (e0 + e1)
        lb = (sm0 + sm1) - sm0
        log_f = _logaddexp(jnp.log(jnp.maximum(lb, LB_FLOOR)), jnp.log1p(-lb) + ls)
        k = (1.0 - lb) * jax.nn.sigmoid(-f_raw)
    else:
        log_f = _logaddexp(jnp.full_like(ls, jnp.log(jnp.float32(LB_FLOOR))), ls)
        k = jax.nn.sigmoid(-f_raw)
    row, col = _iota((c, c), 0), _iota((c, c), 1)
    trow = _iota((c, 1), 0)
    halves = []
    half = c // 2
    while half >= 1:
        halves.append(half)
        half //= 2
    cmat = jnp.concatenate([(col <= row).astype(f32)]
                           + [(col <= (row // (2 * hf)) * (2 * hf) + hf - 1).astype(f32) for hf in halves], axis=0)
    ball = _const_left(cmat.astype(bf16), log_f)
    b = ball[..., :c, :]
    att = None
    for lvl, hf in enumerate(halves):
        blk = 2 * hf
        bref = ball[..., (lvl + 1) * c:(lvl + 2) * c, :]
        upper = (trow % blk) >= hf
        dec = jnp.exp(jnp.where(upper, b - bref, bref - b))
        qh = jnp.where(upper, q * dec, 0.0)
        kh = jnp.where(upper, 0.0, k * dec)
        term = jnp.where(row // blk == col // blk, _mm2(qh, kh, "nt", APPLY_PASSES), 0.0)
        att = term if att is None else att + term
    lhs = jnp.concatenate([q * jnp.exp(b), att, jnp.zeros(att.shape[:-1] + (LANES - c,), f32)], axis=-1)
    rhs = jnp.concatenate([s0, i_in, jnp.zeros(i_in.shape[:-2] + (LANES - c, i_in.shape[-1]), f32)], axis=-2)
    o = _mm2(lhs, rhs, "nn", APPLY_PASSES) + jnp.sum(q * k, axis=-1, keepdims=True) * i_in
    b_last = _last_row(b)
    s_new = _col_of_row(jnp.exp(b_last)) * s0 + _mm2(k * jnp.exp(b_last - b), i_in, "tn", APPLY_PASSES)
    o = o * lax.rsqrt(jnp.mean(o * o, axis=-1, keepdims=True) + RMS_EPS)
    return o * gnw * _silu(z), s_new


def _hgrn_in_specs(t, dh, col0, rev):
    nc = t // CHUNK
    nh = dh // LANES

    def cidx(c):
        return (nc - 1 - c) if rev else c

    grp = _group(nh)
    specs = [pl.BlockSpec((CHUNK, LANES), functools.partial(lambda g, j, h, c: (cidx(c), col0 + g * nh + h * grp + j), g, j))
             for j in range(grp) for g in range(4)]
    specs.append(pl.BlockSpec((2, grp * LANES), lambda h, c: (0, h)))
    specs.append(pl.BlockSpec((1, grp * LANES), lambda h, c: (0, h)))
    return specs, cidx, grp


def _hgrn_fwd(layer1, proj, lbl, gnw, cat, rwc):
    t, d = cat.shape
    dh = gnw.shape[1]
    nh = dh // LANES
    nc = t // CHUNK
    col0 = rwc // LANES
    specs, _, grp = _hgrn_in_specs(t, dh, col0, False)
    specs.append(pl.BlockSpec(memory_space=pl.ANY))
    assert (d - dh) % (grp * LANES) == 0
    cat_col0 = (d - dh) // (grp * LANES)

    def body(*refs):
        x_refs = refs[:4 * grp]
        lbl_ref, gnw_ref, _, cat_ref, sck_ref, s_s = refs[4 * grp:]
        c = pl.program_id(1)

        @pl.when(c == 0)
        def _():
            s_s[...] = jnp.zeros_like(s_s)

        lanes = [slice(j * LANES, (j + 1) * LANES) for j in range(grp)]
        s0 = s_s[...]
        sck_ref[:, 0] = s0
        out, s_new = _hgrn_chunk(layer1, jnp.stack([lbl_ref[:, ln] for ln in lanes]), jnp.stack([gnw_ref[:, ln] for ln in lanes]),
                                 s0, *(jnp.stack([x_refs[4 * j + g][...] for j in range(grp)]) for g in range(4)))
        for j in range(grp):
            cat_ref[:, lanes[j]] = out[j]
        s_s[...] = s_new

    return pl.pallas_call(
        body, grid=(nh // grp, nc), in_specs=specs,
        out_specs=[pl.BlockSpec((CHUNK, grp * LANES), lambda h, c: (c, cat_col0 + h)),
                   pl.BlockSpec((grp, 1, LANES, LANES), lambda h, c: (h, c, 0, 0))],
        out_shape=[jax.ShapeDtypeStruct((t, d), f32), jax.ShapeDtypeStruct((nh, nc, LANES, LANES), f32)],
        scratch_shapes=[pltpu.VMEM((grp, LANES, LANES), f32)],
        input_output_aliases={4 * grp + 2: 0},
        compiler_params=pltpu.CompilerParams(dimension_semantics=("arbitrary", "arbitrary")),
        name=f"hgrn_fwd_l{int(layer1)}",
    )(*([proj] * (4 * grp)), lbl, gnw, cat)


def _hgrn_bwd(layer1, proj, lbl, gnw, sck, dcat, rwc, dproj):
    t, d = dcat.shape
    dh = gnw.shape[1]
    nh = dh // LANES
    nc = t // CHUNK
    col0 = rwc // LANES
    specs, cidx, grp = _hgrn_in_specs(t, dh, col0, True)
    assert grp == nh and (d - dh) % (grp * LANES) == 0
    cat_col0 = (d - dh) // (grp * LANES)
    specs.append(pl.BlockSpec((grp, 1, LANES, LANES), lambda h, c: (h, cidx(c), 0, 0)))
    specs.append(pl.BlockSpec((CHUNK, grp * LANES), lambda h, c: (cidx(c), cat_col0 + h)))
    specs.append(pl.BlockSpec(memory_space=pl.ANY))

    def body(*refs):
        x_refs = refs[:4 * grp]
        lbl_ref, gnw_ref, sck_ref, do_ref, _, dp_hbm, dlbl_ref, dgnw_ref, ds_s, stage, sems = refs[4 * grp:]
        c = pl.program_id(1)
        slot = c % 2

        def put(s, g, chunk):
            return pltpu.make_async_copy(stage.at[s, g], dp_hbm.at[pl.ds(chunk * CHUNK, CHUNK), pl.ds(rwc + g * dh, dh)],
                                         sems.at[s, g])

        @pl.when(c == 0)
        def _():
            ds_s[...] = jnp.zeros_like(ds_s)
            dlbl_ref[...] = jnp.zeros_like(dlbl_ref)
            dgnw_ref[...] = jnp.zeros_like(dgnw_ref)

        @pl.when(c >= 2)
        def _():
            for g in range(4):
                put(slot, g, 0).wait()

        lanes = [slice(j * LANES, (j + 1) * LANES) for j in range(grp)]
        _, vjp = jax.vjp(functools.partial(_hgrn_chunk, layer1),
                         jnp.stack([lbl_ref[:, ln] for ln in lanes]), jnp.stack([gnw_ref[:, ln] for ln in lanes]), sck_ref[:, 0],
                         *(jnp.stack([x_refs[4 * j + g][...] for j in range(grp)]) for g in range(4)))
        dlbl, dgnw, ds0, dq, df, di, dz = vjp((jnp.stack([do_ref[:, ln] for ln in lanes]), ds_s[...]))
        ds_s[...] = ds0
        for j in range(grp):
            dlbl_ref[:, lanes[j]] += dlbl[j]
            dgnw_ref[:, lanes[j]] += dgnw[j]
            for g, val in enumerate((dq, df, di, dz)):
                stage[slot, g, :, lanes[j]] = val[j].astype(bf16)
        for g in range(4):
            put(slot, g, nc - 1 - c).start()

        @pl.when(c == nc - 1)
        def _():
            for g in range(4):
                put(slot, g, 0).wait()
                if nc >= 2:
                    put(1 - slot, g, 0).wait()

    return pl.pallas_call(
        body, grid=(1, nc), in_specs=specs,
        out_specs=[pl.BlockSpec(memory_space=pl.ANY),
                   pl.BlockSpec((2, grp * LANES), lambda h, c: (0, h)),
                   pl.BlockSpec((1, grp * LANES), lambda h, c: (0, h))],
        out_shape=[jax.ShapeDtypeStruct(dproj.shape, dproj.dtype), jax.ShapeDtypeStruct((2, dh), f32),
                   jax.ShapeDtypeStruct((1, dh), f32)],
        scratch_shapes=[pltpu.VMEM((grp, LANES, LANES), f32), pltpu.VMEM((2, 4, CHUNK, dh), bf16),
                        pltpu.SemaphoreType.DMA((2, 4))],
        input_output_aliases={4 * grp + 4: 0},
        compiler_params=pltpu.CompilerParams(dimension_semantics=("arbitrary", "arbitrary")),
        name=f"hgrn_bwd_l{int(layer1)}",
    )(*([proj] * (4 * grp)), lbl, gnw, sck, dcat, dproj)


def _ln(h, y, w, b):
    u = ALPHA * h + y
    mu = jnp.mean(u, axis=-1, keepdims=True)
    var = jnp.mean(jnp.square(u - mu), axis=-1, keepdims=True)
    return (u - mu) * lax.rsqrt(var + LN_EPS) * w + b


def _row_tile(t):
    return 256 if t % 256 == 0 else t


def _ln_fwd(h, y, w, b):
    t, d = h.shape
    tr = _row_tile(t)

    def body(h_ref, y_ref, w_ref, b_ref, o_ref, o16_ref):
        out = _ln(h_ref[...], y_ref[...], w_ref[...], b_ref[...])
        o_ref[...] = out
        o16_ref[...] = out.astype(bf16)

    row = pl.BlockSpec((tr, d), lambda i: (i, 0))
    vec = pl.BlockSpec((1, d), lambda i: (0, 0))
    return pl.pallas_call(body, grid=(t // tr,), in_specs=[row, row, vec, vec], out_specs=[row, row],
                          out_shape=[jax.ShapeDtypeStruct((t, d), f32), jax.ShapeDtypeStruct((t, d), bf16)],
                          name="ln_fwd")(h, y, w, b)


def _ln_loss_bwd(h, y, w, b, tgt):
    t, d = h.shape
    tr = _row_tile(t)

    def body(h_ref, y_ref, w_ref, b_ref, t_ref, dy_ref, dy16_ref, dw_ref, db_ref, loss_ref):
        @pl.when(pl.program_id(0) == 0)
        def _():
            dw_ref[...] = jnp.zeros_like(dw_ref)
            db_ref[...] = jnp.zeros_like(db_ref)
            loss_ref[...] = jnp.zeros_like(loss_ref)

        out, vjp = jax.vjp(lambda yy, ww, bb: _ln(h_ref[...], yy, ww, bb), y_ref[...], w_ref[...], b_ref[...])
        err = out - t_ref[...]
        loss_ref[...] += 0.5 * jnp.sum(jnp.mean(jnp.square(err), axis=-1, keepdims=True), axis=0, keepdims=True)
        dy, dw, db = vjp(err * (1.0 / d))
        dy_ref[...] = dy
        dy16_ref[...] = dy.astype(bf16)
        dw_ref[...] += dw
        db_ref[...] += db

    row = pl.BlockSpec((tr, d), lambda i: (i, 0))
    vec = pl.BlockSpec((1, d), lambda i: (0, 0))
    return pl.pallas_call(
        body, grid=(t // tr,), in_specs=[row, row, vec, vec, row],
        out_specs=[row, row, vec, vec, pl.BlockSpec((1, LANES), lambda i: (0, 0))],
        out_shape=[jax.ShapeDtypeStruct((t, d), f32), jax.ShapeDtypeStruct((t, d), bf16), jax.ShapeDtypeStruct((1, d), f32),
                   jax.ShapeDtypeStruct((1, d), f32), jax.ShapeDtypeStruct((1, LANES), f32)],
        compiler_params=pltpu.CompilerParams(dimension_semantics=("arbitrary",)), name="ln_loss_bwd")(h, y, w, b, tgt)


def _ln_bwd(h, y, w, b, dout):
    t, d = h.shape
    tr = _row_tile(t)

    def body(h_ref, y_ref, w_ref, b_ref, do_ref, dy_ref, dy16_ref, dw_ref, db_ref):
        @pl.when(pl.program_id(0) == 0)
        def _():
            dw_ref[...] = jnp.zeros_like(dw_ref)
            db_ref[...] = jnp.zeros_like(db_ref)

        _, vjp = jax.vjp(lambda yy, ww, bb: _ln(h_ref[...], yy, ww, bb), y_ref[...], w_ref[...], b_ref[...])
        dy, dw, db = vjp(do_ref[...])
        dy_ref[...] = dy
        dy16_ref[...] = dy.astype(bf16)
        dw_ref[...] += dw
        db_ref[...] += db

    row = pl.BlockSpec((tr, d), lambda i: (i, 0))
    vec = pl.BlockSpec((1, d), lambda i: (0, 0))
    return pl.pallas_call(
        body, grid=(t // tr,), in_specs=[row, row, vec, vec, row], out_specs=[row, row, vec, vec],
        out_shape=[jax.ShapeDtypeStruct((t, d), f32), jax.ShapeDtypeStruct((t, d), bf16),
                   jax.ShapeDtypeStruct((1, d), f32), jax.ShapeDtypeStruct((1, d), f32)],
        compiler_params=pltpu.CompilerParams(dimension_semantics=("arbitrary",)), name="ln_bwd")(h, y, w, b, dout)


def _pick(n, prefs):
    for p in prefs:
        if n % p == 0:
            return p
    return n


def _tile(n, want):
    if n <= want:
        return n
    for cand in range(want - want % LANES, 0, -LANES):
        if n % cand == 0:
            return cand
    return n


_MM_TILES = {"proj": (1024, 1664, 2048), "out": (1024, 1024, 2048), "dcat": (1024, 1024, 2048),
             "dwout": (512, 2048, 2048), "dwin": (640, 2048, 2048), "dh": (1024, 1024, 1664)}


def _matmul(a, b, mode, name, tiles, add=None, add_scale=1.0, out_dtype=f32, after=None):
    if mode == "nn":
        (m, k), n = a.shape, b.shape[1]
    elif mode == "nt":
        (m, k), n = a.shape, b.shape[0]
    else:
        (k, m), n = a.shape, b.shape[1]
    tm, tn, tk = _tile(m, tiles[0]), _tile(n, tiles[1]), _tile(k, tiles[2])
    nk = k // tk
    cache_a = nk == 1 and a.dtype != bf16 and n // tn > 1

    def body(*refs):
        a_ref, b_ref = refs[0], refs[1]
        add_ref = refs[2] if add is not None else None
        n_in = 2 + (add is not None) + (after is not None)
        o_ref = refs[n_in]
        scratch = refs[n_in + 1:]

        def finish(res):
            if add is not None:
                res = res + add_scale * add_ref[...]
            o_ref[...] = res.astype(out_dtype)

        if cache_a:
            a_bf = scratch[0]

            @pl.when(pl.program_id(1) == 0)
            def _():
                a_bf[...] = a_ref[...].astype(bf16)

            a_val = a_bf[...]
        else:
            a_val = a_ref[...].astype(bf16)
        prod = lax.dot_general(a_val, b_ref[...].astype(bf16), _DIMS[mode], preferred_element_type=f32)
        if nk == 1:
            finish(prod)
        else:
            acc = scratch[-1]
            kk = pl.program_id(2)

            @pl.when(kk == 0)
            def _():
                acc[...] = prod

            @pl.when(kk != 0)
            def _():
                acc[...] += prod

            @pl.when(kk == nk - 1)
            def _():
                finish(acc[...])

    a_shape = (tk, tm) if mode == "tn" else (tm, tk)
    a_spec = pl.BlockSpec(a_shape, (lambda i, j, kk: (kk, i)) if mode == "tn" else (lambda i, j, kk: (i, kk)))
    b_spec = pl.BlockSpec((tn, tk), lambda i, j, kk: (j, kk)) if mode == "nt" else pl.BlockSpec((tk, tn), lambda i, j, kk: (kk, j))
    o_spec = pl.BlockSpec((tm, tn), lambda i, j, kk: (i, j))
    in_specs = [a_spec, b_spec] + ([o_spec] if add is not None else []) + ([pl.BlockSpec(memory_space=pl.ANY)] if after is not None else [])
    args = [a, b] + ([add] if add is not None else []) + ([after] if after is not None else [])
    scratch_shapes = ([pltpu.VMEM(a_shape, bf16)] if cache_a else []) + ([pltpu.VMEM((tm, tn), f32)] if nk > 1 else [])
    return pl.pallas_call(
        body, grid=(m // tm, n // tn, nk), in_specs=in_specs, out_specs=o_spec,
        out_shape=jax.ShapeDtypeStruct((m, n), out_dtype), scratch_shapes=scratch_shapes,
        compiler_params=pltpu.CompilerParams(dimension_semantics=("parallel", "arbitrary", "arbitrary")),
        name=name,
    )(*args)


def _position():
    return lax.axis_index("x"), lax.axis_index("y"), lax.axis_index("c")


def _flip(pos, k):
    x, y, c = pos
    return (1 - x if k & 4 else x, 1 - y if k & 2 else y, 1 - c if k & 1 else c)


def _index(pos):
    return 4 * pos[0] + 2 * pos[1] + pos[2]


def _all_gather_rows(xs, name):
    n_arr = len(xs)
    chips = (2, 4, 6)

    def body(*refs):
        x_refs, out_refs = refs[:n_arr], refs[n_arr:2 * n_arr]
        send_sems, recv_sems, local_sems = refs[2 * n_arr:]
        me = _position()
        sibling = _flip(me, 1)

        def copy(i, sem, block, to, own=False):
            m_per = x_refs[i].shape[0]
            rows = out_refs[i].at[pl.ds(_index(block) * m_per, m_per), :]
            return pltpu.make_async_remote_copy(
                src_ref=x_refs[i] if own else rows, dst_ref=rows,
                send_sem=send_sems.at[7 * i + sem], recv_sem=recv_sems.at[7 * i + sem], device_id=to, device_id_type=MESH)

        mine = [pltpu.make_async_copy(x_refs[i], out_refs[i].at[pl.ds(_index(me) * x_refs[i].shape[0], x_refs[i].shape[0]), :],
                                      local_sems.at[i]) for i in range(n_arr)]
        first, passed = [], []
        for i in range(n_arr):
            first.append(copy(i, 0, me, sibling, own=True))
            first += [copy(i, 1 + j, me, _flip(me, k), own=True) for j, k in enumerate(chips)]
            passed.append([copy(i, 4 + j, _flip(me, k), sibling) for j, k in enumerate(chips)])
        for cp in mine + first:
            cp.start()
        for i in range(n_arr):
            for j, k in enumerate(chips):
                copy(i, 1 + j, _flip(me, k), me).wait_recv()
                passed[i][j].start()
        for i in range(n_arr):
            copy(i, 0, sibling, me).wait_recv()
            for j, k in enumerate(chips):
                copy(i, 4 + j, _flip(sibling, k), me).wait_recv()
        for cp in first + [cp for group in passed for cp in group]:
            cp.wait_send()
        for cp in mine:
            cp.wait()

    anyspec = pl.BlockSpec(memory_space=pl.ANY)
    return pl.pallas_call(
        body, out_shape=[jax.ShapeDtypeStruct((N_DEV * x.shape[0], x.shape[1]), x.dtype) for x in xs],
        in_specs=[anyspec] * n_arr, out_specs=[anyspec] * n_arr,
        scratch_shapes=[pltpu.SemaphoreType.DMA((7 * n_arr,)), pltpu.SemaphoreType.DMA((7 * n_arr,)),
                        pltpu.SemaphoreType.DMA((n_arr,))],
        name=name,
    )(*xs)


def _split_start(srcs, lands, plan, n_copies, name, after=()):
    n_arr = len(srcs)
    n_after = len(after)
    hbm = pl.BlockSpec(memory_space=pltpu.HBM)
    sem = pl.BlockSpec(memory_space=pltpu.SEMAPHORE)

    def body(*refs):
        src_refs, land_refs = refs[:n_arr], refs[n_arr:2 * n_arr]
        outs_at = 2 * n_arr + n_after
        send_sems, recv_sems = refs[outs_at:outs_at + n_arr], refs[outs_at + n_arr:outs_at + 2 * n_arr]
        token = refs[-1]
        me = _position()
        for i in range(n_arr):
            for j, (src, dst, peer, _) in enumerate(plan(i, src_refs[i], land_refs[i], me)):
                pltpu.make_async_remote_copy(src_ref=src, dst_ref=dst, send_sem=send_sems[i].at[j], recv_sem=recv_sems[i].at[j],
                                             device_id=peer, device_id_type=MESH).start()
        token[...] = jnp.zeros_like(token)

    outs = pl.pallas_call(
        body, name=name,
        out_shape=([pltpu.SemaphoreType.DMA((n_copies,))] * (2 * n_arr)
                   + [pltpu.HBM(a.shape, a.dtype) for a in list(srcs) + list(lands)]
                   + [jax.ShapeDtypeStruct((8, LANES), f32)]),
        in_specs=[hbm] * (2 * n_arr) + [pl.BlockSpec(memory_space=pl.ANY)] * n_after,
        out_specs=[sem] * (2 * n_arr) + [hbm] * (2 * n_arr) + [pl.BlockSpec(memory_space=pltpu.VMEM)],
        input_output_aliases={i: 2 * n_arr + i for i in range(2 * n_arr)},
        compiler_params=pltpu.CompilerParams(has_side_effects=pltpu.SideEffectType.DATAFLOW_SIDE_EFFECTING),
    )(*[pltpu.with_memory_space_constraint(a, pltpu.HBM) for a in list(srcs) + list(lands)], *after)
    return (outs[:n_arr], outs[n_arr:2 * n_arr], outs[2 * n_arr:3 * n_arr], outs[3 * n_arr:4 * n_arr], outs[-1])


def _split_wait(started, plan, after, name):
    send_sems, recv_sems, srcs, lands, _ = started
    n_arr = len(srcs)
    hbm = pl.BlockSpec(memory_space=pltpu.HBM)
    sem = pl.BlockSpec(memory_space=pltpu.SEMAPHORE)

    def body(*refs):
        src_refs, land_refs = refs[:n_arr], refs[n_arr:2 * n_arr]
        s_sems, r_sems = refs[2 * n_arr:3 * n_arr], refs[3 * n_arr:4 * n_arr]
        me = _position()
        for i in range(n_arr):
            for j, (src, _, peer, arrival) in enumerate(plan(i, src_refs[i], land_refs[i], me)):
                cp = pltpu.make_async_remote_copy(src_ref=src, dst_ref=arrival, send_sem=s_sems[i].at[j], recv_sem=r_sems[i].at[j],
                                                  device_id=peer, device_id_type=MESH)
                cp.wait_send()
                cp.wait_recv()

    outs = pl.pallas_call(
        body, name=name,
        out_shape=[pltpu.HBM(a.shape, a.dtype) for a in list(srcs) + list(lands)],
        in_specs=[hbm] * (2 * n_arr) + [sem] * (2 * n_arr) + [pl.BlockSpec(memory_space=pl.ANY)],
        out_specs=[hbm] * (2 * n_arr),
        input_output_aliases={i: i for i in range(2 * n_arr)},
        compiler_params=pltpu.CompilerParams(has_side_effects=pltpu.SideEffectType.DATAFLOW_SIDE_EFFECTING),
    )(*srcs, *lands, *send_sems, *recv_sems, after)
    return outs[:n_arr], outs[n_arr:]


def _landing_zone(blk, me, name):
    m, n = blk.shape

    def body(me_ref, x_ref, o_ref):
        del me_ref
        o_ref[...] = x_ref[...]

    return pl.pallas_call(
        body,
        grid_spec=pltpu.PrefetchScalarGridSpec(
            num_scalar_prefetch=1, grid=(1,),
            in_specs=[pl.BlockSpec((m, n), lambda i, me_ref: (0, 0))],
            out_specs=pl.BlockSpec((m, n), lambda i, me_ref: (me_ref[0], 0))),
        out_shape=jax.ShapeDtypeStruct((N_DEV * m, n), blk.dtype), name=name,
    )(jnp.reshape(me, (1,)).astype(jnp.int32), blk)


_GATHER_FLIPS = (1, 2, 4, 6)


def _gather_plan(i, src_ref, land_ref, me):
    m = src_ref.shape[0]

    def rows(pos):
        return land_ref.at[pl.ds(_index(pos) * m, m), :]

    return [(src_ref, rows(me), _flip(me, k), rows(_flip(me, k))) for k in _GATHER_FLIPS]


def _gather_forward(lands, name):
    n_arr = len(lands)
    chips = (2, 4, 6)

    def body(*refs):
        out_refs = refs[n_arr:2 * n_arr]
        send_sems, recv_sems = refs[2 * n_arr:]
        me = _position()
        sibling = _flip(me, 1)
        sends, arrivals = [], []
        for i, out_ref in enumerate(out_refs):
            m = out_ref.shape[0] // N_DEV

            def copy(pos, j):
                blk = out_ref.at[pl.ds(_index(pos) * m, m), :]
                return pltpu.make_async_remote_copy(src_ref=blk, dst_ref=blk, send_sem=send_sems.at[3 * i + j],
                                                    recv_sem=recv_sems.at[3 * i + j], device_id=sibling, device_id_type=MESH)

            for j, k in enumerate(chips):
                sends.append(copy(_flip(me, k), j))
                arrivals.append(copy(_flip(sibling, k), j))
        for cp in sends:
            cp.start()
        for cp in arrivals:
            cp.wait_recv()
        for cp in sends:
            cp.wait_send()

    anyspec = pl.BlockSpec(memory_space=pl.ANY)
    return pl.pallas_call(
        body, out_shape=[jax.ShapeDtypeStruct(a.shape, a.dtype) for a in lands],
        in_specs=[anyspec] * n_arr, out_specs=[anyspec] * n_arr, input_output_aliases={i: i for i in range(n_arr)},
        scratch_shapes=[pltpu.SemaphoreType.DMA((3 * n_arr,))] * 2, name=name,
    )(*lands)


def _chips_plan(i, src_ref, land_ref, me):
    m = src_ref.shape[0] // 4
    plan = []
    for j, k in enumerate((2, 4, 6)):
        peer = _flip(me, k)
        plan.append((src_ref.at[pl.ds((2 * peer[0] + peer[1]) * m, m), :], land_ref.at[j], peer, land_ref.at[j]))
    return plan


def _sibling_plan(i, src_ref, land_ref, me):
    m = src_ref.shape[0] // N_DEV
    sibling = _flip(me, 1)
    return [(src_ref.at[pl.ds((2 * q + 1 - me[2]) * m, m), :], land_ref.at[q], sibling, land_ref.at[q]) for q in range(4)]


def _sum_with_sibling(g, recv, name):
    m = g.shape[0] // N_DEV
    n = g.shape[1]
    tr = _pick(m, (208, 128, 64, 32, 16))
    nt = m // tr

    def body(g_ref, r_ref, o_ref):
        c = lax.axis_index("c")
        own = jnp.where(c == 0, g_ref[0, 0].astype(f32), g_ref[0, 1].astype(f32))
        o_ref[...] = (own + r_ref[0].astype(f32)).astype(o_ref.dtype)

    return pl.pallas_call(
        body, grid=(4, nt),
        in_specs=[pl.BlockSpec((1, 2, tr, n), lambda q, i: (q, 0, i, 0)), pl.BlockSpec((1, tr, n), lambda q, i: (q, i, 0))],
        out_specs=pl.BlockSpec((tr, n), lambda q, i: (q * nt + i, 0)),
        out_shape=jax.ShapeDtypeStruct((4 * m, n), bf16), name=name,
    )(g.reshape(4, 2, m, n), recv)


def _sum_with_chips(h, recv, name, slot=0, n_slots=1, into=None):
    m = h.shape[0] // 4
    n = h.shape[1]
    tr = _pick(m, (208, 128, 64, 32, 16))

    def body(h_ref, r_ref, *rest):
        o_ref = rest[-1]
        my_q = 2 * lax.axis_index("x") + lax.axis_index("y")
        own = h_ref[0].astype(f32)
        for q in range(1, 4):
            own = jnp.where(my_q == q, h_ref[q].astype(f32), own)
        o_ref[0] = ((own + r_ref[0].astype(f32)) + r_ref[1].astype(f32)) + r_ref[2].astype(f32)

    in_specs = [pl.BlockSpec((4, tr, n), lambda i: (0, i, 0)), pl.BlockSpec((3, tr, n), lambda i: (0, i, 0))]
    args = [h.reshape(4, m, n), recv]
    if into is not None:
        in_specs.append(pl.BlockSpec(memory_space=pl.ANY))
        args.append(into)
    return pl.pallas_call(
        body, grid=(m // tr,), in_specs=in_specs,
        out_specs=pl.BlockSpec((1, tr, n), lambda i: (slot, i, 0)), out_shape=jax.ShapeDtypeStruct((n_slots, m, n), f32),
        input_output_aliases={2: 0} if into is not None else {}, name=name,
    )(*args)


def _sum_slots(parts, name):
    n_slot, m, n = parts.shape
    tr = _pick(m, (208, 128, 64, 32, 16, 8))

    def body(p_ref, o_ref):
        acc = p_ref[0]
        for s in range(1, n_slot):
            acc = acc + p_ref[s]
        o_ref[...] = acc

    return pl.pallas_call(
        body, grid=(m // tr,), in_specs=[pl.BlockSpec((n_slot, tr, n), lambda i: (0, i, 0))],
        out_specs=pl.BlockSpec((tr, n), lambda i: (i, 0)), out_shape=jax.ShapeDtypeStruct((m, n), parts.dtype), name=name,
    )(parts)


def _reduce_scatter_begin(gs, name):
    lands = [lax.empty((4, g.shape[0] // N_DEV, g.shape[1]), g.dtype) for g in gs]
    return _split_start(gs, lands, _sibling_plan, 4, "rs_d2d_start_" + name)


def _reduce_scatter_middle(started, after, name):
    gs, from_sibling = _split_wait(started, _sibling_plan, after, "rs_d2d_wait_" + name)
    chip_sums = [_sum_with_sibling(g, r, f"rs_sum2_{name}_{i}") for i, (g, r) in enumerate(zip(gs, from_sibling))]
    lands = [lax.empty((3, h.shape[0] // 4, h.shape[1]), h.dtype) for h in chip_sums]
    return _split_start(chip_sums, lands, _chips_plan, 3, "rs_ici_start_" + name)


def _reduce_scatter_end(started, after, name, first_into=None, slot=0, n_slots=1):
    chip_sums, from_chips = _split_wait(started, _chips_plan, after, "rs_ici_wait_" + name)
    out = []
    for i, (h, r) in enumerate(zip(chip_sums, from_chips)):
        if i == 0:
            out.append(_sum_with_chips(h, r, f"rs_sum4_{name}_{i}", slot, n_slots, first_into))
        else:
            out.append(_sum_with_chips(h, r, f"rs_sum4_{name}_{i}")[0])
    return out


def _adamw_update(w, g, m, v):
    mm = ADAM_B1 * m + (1.0 - ADAM_B1) * g
    vv = ADAM_B2 * v + (1.0 - ADAM_B2) * jnp.square(g)
    m_hat = mm / (1.0 - ADAM_B1 ** ADAM_STEP)
    v_hat = vv / (1.0 - ADAM_B2 ** ADAM_STEP)
    return -ADAM_LR * (m_hat / (jnp.sqrt(v_hat) + ADAM_EPS) + ADAM_WD * w), mm, vv


def _adamw_many(ws, gs, ms, vs, name):
    k = len(ws)
    shapes = [w.shape for w in ws]
    flat = [[a.reshape(-1, a.shape[-1]) for a in group] for group in (ws, gs, ms, vs)]

    def body(*refs):
        for i in range(k):
            d, mm, vv = _adamw_update(*(refs[j * k + i][...] for j in range(4)))
            refs[4 * k + i][...] = d
            refs[5 * k + i][...] = mm
            refs[6 * k + i][...] = vv

    outs = pl.pallas_call(
        body, out_shape=[jax.ShapeDtypeStruct(a.shape, f32) for a in flat[0]] * 3, name=name,
    )(*flat[0], *flat[1], *flat[2], *flat[3])
    return tuple([outs[j * k + i].reshape(shapes[i]) for i in range(k)] for j in range(3))


def _adamw(w, g, m, v, name):
    shape = w.shape
    n = shape[-1]
    r = w.size // n
    w2, g2, m2, v2 = (a.reshape(r, n) for a in (w, g, m, v))
    tr = _pick(r, (256, 208, 128, 64, 32, 16, 8))

    def body(w_ref, g_ref, m_ref, v_ref, d_ref, mo_ref, vo_ref):
        d_ref[...], mo_ref[...], vo_ref[...] = _adamw_update(w_ref[...], g_ref[...], m_ref[...], v_ref[...])

    spec = pl.BlockSpec((tr, n), lambda i: (i, 0))
    outs = pl.pallas_call(
        body, grid=(r // tr,), in_specs=[spec] * 4, out_specs=[spec] * 3,
        out_shape=[jax.ShapeDtypeStruct((r, n), f32)] * 3, name=name,
    )(w2, g2, m2, v2)
    return tuple(o.reshape(shape) for o in outs)


_SMALL = ("shift_mu", "w_decay0", "a0", "k_k", "k_a", "r_k", "ln_x_w", "ln_x_b", "v_mix0", "lb_logits",
          "g_norm_w", "ln_w", "ln_b")
_NAMES = ("w_in", "shift_mu", "w_decay0", "w_decay_up", "a0", "a_up", "k_k", "k_a", "r_k", "ln_x_w", "ln_x_b",
          "v_mix0", "v_mix_down", "v_mix_up", "lb_logits", "g_norm_w", "w_out", "ln_w", "ln_b")


def _pad_rows(a, rows, at_end):
    z = jnp.zeros((rows - a.shape[0], a.shape[1]), a.dtype)
    return jnp.concatenate([a, z] if at_end else [z, a], axis=0)


def kernel(x, w_in, shift_mu, w_decay0, w_decay_up, a0, a_up, k_k, k_a, r_k, ln_x_w, ln_x_b, v_mix0, v_mix_down, v_mix_up, lb_logits, g_norm_w, w_out, ln_w, ln_b, loss_target, m_w_in, m_shift_mu, m_w_decay0, m_w_decay_up, m_a0, m_a_up, m_k_k, m_k_a, m_r_k, m_ln_x_w, m_ln_x_b, m_v_mix0, m_v_mix_down, m_v_mix_up, m_lb_logits, m_g_norm_w, m_w_out, m_ln_w, m_ln_b, v_w_in, v_shift_mu, v_w_decay0, v_w_decay_up, v_a0, v_a_up, v_k_k, v_k_a, v_r_k, v_ln_x_w, v_ln_x_b, v_v_mix0, v_v_mix_down, v_v_mix_up, v_lb_logits, v_g_norm_w, v_w_out, v_ln_w, v_ln_b):
    weights = dict(w_in=w_in, shift_mu=shift_mu, w_decay0=w_decay0, w_decay_up=w_decay_up, a0=a0, a_up=a_up, k_k=k_k,
                   k_a=k_a, r_k=r_k, ln_x_w=ln_x_w, ln_x_b=ln_x_b, v_mix0=v_mix0, v_mix_down=v_mix_down,
                   v_mix_up=v_mix_up, lb_logits=lb_logits, g_norm_w=g_norm_w, w_out=w_out, ln_w=ln_w, ln_b=ln_b)
    mom1 = dict(w_in=m_w_in, shift_mu=m_shift_mu, w_decay0=m_w_decay0, w_decay_up=m_w_decay_up, a0=m_a0, a_up=m_a_up,
                k_k=m_k_k, k_a=m_k_a, r_k=m_r_k, ln_x_w=m_ln_x_w, ln_x_b=m_ln_x_b, v_mix0=m_v_mix0,
                v_mix_down=m_v_mix_down, v_mix_up=m_v_mix_up, lb_logits=m_lb_logits, g_norm_w=m_g_norm_w,
                w_out=m_w_out, ln_w=m_ln_w, ln_b=m_ln_b)
    mom2 = dict(w_in=v_w_in, shift_mu=v_shift_mu, w_decay0=v_w_decay0, w_decay_up=v_w_decay_up, a0=v_a0, a_up=v_a_up,
                k_k=v_k_k, k_a=v_k_a, r_k=v_r_k, ln_x_w=v_ln_x_w, ln_x_b=v_ln_x_b, v_mix0=v_v_mix0,
                v_mix_down=v_v_mix_down, v_mix_up=v_v_mix_up, lb_logits=v_lb_logits, g_norm_w=v_g_norm_w,
                w_out=v_w_out, ln_w=v_ln_w, ln_b=v_ln_b)
    assert x.shape[0] == 1 and w_in.shape[0] == DEPTH
    t, d = x.shape[1], x.shape[2]
    dr = w_decay0.shape[1]
    dh = g_norm_w.shape[1]
    rank_w, rank_a, rank_v = w_decay_up.shape[1], a_up.shape[1], v_mix_up.shape[1]
    rwc = 4 * dr + rank_w + rank_a
    assert rank_w + rank_a == LANES and rank_v <= LANES and dr + dh == d
    assert t % CHUNK == 0 and dr % LANES == 0 and dh % LANES == 0 and shift_mu.shape[1] == rwc
    n_pair = dr // LANES
    me = _index(_position())

    shard = dr // N_DEV
    pack = jnp.concatenate([w_decay_up[0], w_decay_up[1], a_up[0], a_up[1], v_mix_up[0], v_mix_down[0].T], axis=0)
    win_t0, pack = _all_gather_rows([w_in[0].T.astype(bf16), pack], "ag_first")
    win_t = [win_t0, None]
    wout = [None, None]

    def start_gather(blocks, name, after):
        lands = [_landing_zone(blk, me, f"{name}_zone{i}") for i, blk in enumerate(blocks)]
        return _split_start(blocks, lands, _gather_plan, len(_GATHER_FLIPS), name, after=after)

    gather_wout0 = start_gather([w_out[0].astype(bf16)], "ag_wout0_start", (win_t[0], pack))
    gather_layer1 = start_gather([w_in[1].T.astype(bf16), w_out[1].astype(bf16)], "ag_layer1_start", (gather_wout0[-1],))
    pack = jnp.transpose(pack.reshape(N_DEV, -1, shard), (1, 0, 2)).reshape(-1, dr)
    offs = [0, rank_w, 2 * rank_w, 2 * rank_w + rank_a, 2 * rank_w + 2 * rank_a, 2 * rank_w + 2 * rank_a + rank_v,
            2 * rank_w + 2 * rank_a + 2 * rank_v]
    wdu_f = [pack[offs[0]:offs[1]], pack[offs[1]:offs[2]]]
    aup_f = [pack[offs[2]:offs[3]], pack[offs[3]:offs[4]]]
    vup_f = pack[offs[4]:offs[5]]
    vdown_f = pack[offs[5]:offs[6]].T

    def after_start(a, started):
        return a + started[-1][0:1, 0:1]

    def rwkv_params(l):
        mu = after_start(shift_mu[0:1], gather_layer1) if l == 0 else shift_mu[l:l + 1]
        prm = [mu, w_decay0[l:l + 1], a0[l:l + 1], _pad_rows(wdu_f[l], LANES, True),
               _pad_rows(aup_f[l], LANES, False)]
        if l == 1:
            prm += [v_mix0[0:1], _pad_rows(vdown_f.T, LANES, True).T, _pad_rows(vup_f, LANES, True)]
        rows = jnp.stack([k_k[l], k_a[l], r_k[l], ln_x_w[l], ln_x_b[l]] + [jnp.zeros((dr,), f32)] * 3, axis=0)
        pp = jnp.transpose(rows.reshape(8, n_pair, LANES), (1, 0, 2))
        return tuple(prm), pp

    h = x[0]
    h16 = h.astype(bf16)
    tgt = loss_target[0]
    saved = []
    vfirst = None
    for l in range(DEPTH):
        prm, pp = rwkv_params(l)
        proj = _matmul(h16, win_t[l], "nt", f"mm_proj_{l}", _MM_TILES["proj"])
        if l == 0:
            cat, vfirst, mck = _rwkv_fwd(False, proj, None, prm, pp, d)
        else:
            cat, mck = _rwkv_fwd(True, proj, vfirst, prm, pp, d)
        cat, sck = _hgrn_fwd(l == 1, proj, lb_logits, g_norm_w[l:l + 1], cat, rwc)
        if l == 0:
            _, arrived = _split_wait(gather_wout0, _gather_plan, cat, "ag_wout0_wait")
            (wout[0],) = _gather_forward(arrived, "ag_wout0_forward")
        y = _matmul(cat, wout[l], "nn", f"mm_out_{l}", _MM_TILES["out"])
        saved.append((h, h16, proj, prm, pp, mck, sck, cat, y))
        if l < DEPTH - 1:
            h, h16 = _ln_fwd(h, y, ln_w[l:l + 1], ln_b[l:l + 1])
            _, arrived = _split_wait(gather_layer1, _gather_plan, h16, "ag_layer1_wait")
            win_t[1], wout[1] = _gather_forward(arrived, "ag_layer1_forward")
        else:
            top = _ln_loss_bwd(h, y, ln_w[l:l + 1], ln_b[l:l + 1], tgt)
    loss = lax.psum(top[4][0, 0], ("x", "y", "c"))

    grads = {}
    big = {}
    dvfirst = None
    d_lbl = None
    rs_started = {}
    for l in reversed(range(DEPTH)):
        h_l, h16_l, proj, prm, pp, mck, sck, cat, y = saved[l]
        if l == DEPTH - 1:
            dy, dy16, g_ln_w, g_ln_b = top[:4]
        else:
            dy, dy16, g_ln_w, g_ln_b = _ln_bwd(h_l, y, after_start(ln_w[l:l + 1], rs_started[l + 1]), ln_b[l:l + 1], dh_out)
        dcat = _matmul(dy16, wout[l], "nt", f"mm_dcat_{l}", _MM_TILES["dcat"])
        big[("w_out", l)] = _matmul(cat, dy16, "tn", f"mm_dwout_{l}", _MM_TILES["dwout"], out_dtype=bf16)
        if l == 1:
            outs = _rwkv_bwd(True, proj, vfirst, prm, pp, mck, dcat, None)
            dproj_r, dvfirst = outs[0], outs[1]
            dprm, dpp = outs[2:-1], outs[-1]
        else:
            outs = _rwkv_bwd(False, proj, None, prm, pp, mck, dcat, dvfirst)
            dproj_r = outs[0]
            dprm, dpp = outs[1:-1], outs[-1]
        dproj, dlbl_l, dgnw = _hgrn_bwd(l == 1, proj, lb_logits, g_norm_w[l:l + 1], sck, dcat, rwc, dproj_r)
        big[("w_in", l)] = _matmul(dproj, h16_l, "tn", f"mm_dwin_{l}", _MM_TILES["dwin"], out_dtype=bf16)
        sharded = [dprm[3][:rank_w].T, dprm[4][rank_w:].T]
        if l == 1:
            sharded += [dprm[6][:, :rank_v], dprm[7][:rank_v].T,
                        jnp.zeros((dr, LANES - 2 * rank_v), f32)]
        sharded = jnp.concatenate(sharded, axis=1).astype(bf16)
        d2d = _reduce_scatter_begin([big[("w_in", l)], big[("w_out", l)], sharded], f"l{l}")
        if l == 0:
            rs_started[l] = _reduce_scatter_middle(d2d, sharded, f"l{l}")
            token = rs_started[l][-1]
        else:
            token = d2d[-1]
        dh_out = _matmul(dproj, win_t[l], "nn", f"mm_dh_{l}", _MM_TILES["dh"], add=dy, add_scale=ALPHA, after=token)
        if l > 0:
            rs_started[l] = _reduce_scatter_middle(d2d, dh_out, f"l{l}")
        dpp = jnp.transpose(dpp, (1, 0, 2)).reshape(8, dr)
        grads[l] = dict(shift_mu=dprm[0][0], w_decay0=dprm[1][0], a0=dprm[2][0],
                        k_k=dpp[0], k_a=dpp[1], r_k=dpp[2], ln_x_w=dpp[3], ln_x_b=dpp[4],
                        g_norm_w=dgnw[0], ln_w=g_ln_w[0], ln_b=g_ln_b[0])
        if l == 1:
            grads[l].update(v_mix0=dprm[5][0])
            d_lbl = dlbl_l
    grad_x = dh_out[None]

    def both(name):
        return jnp.stack([grads[0][name], grads[1][name]])

    small = dict(shift_mu=both("shift_mu"), w_decay0=both("w_decay0"), a0=both("a0"), k_k=both("k_k"), k_a=both("k_a"),
                 r_k=both("r_k"), ln_x_w=both("ln_x_w"), ln_x_b=both("ln_x_b"), v_mix0=grads[1]["v_mix0"][None],
                 lb_logits=d_lbl, g_norm_w=both("g_norm_w"), ln_w=both("ln_w"), ln_b=both("ln_b"))
    flat = jnp.concatenate([small[nm].reshape(-1) for nm in _SMALL])
    n_flat = flat.shape[0]
    rows = -(-n_flat // (8 * LANES)) * 8
    flat = jnp.concatenate([flat, jnp.zeros((rows * LANES - n_flat,), f32)]).reshape(rows, LANES)
    total = _sum_slots(_all_gather_rows([flat], "ag_small_grads")[0].reshape(N_DEV, rows, LANES), "sum_small_grads").reshape(-1)
    gsm = {}
    off = 0
    for nm in _SMALL:
        size = small[nm].size
        gsm[nm] = total[off:off + size].reshape(small[nm].shape)
        off += size
    reduced = {1: _reduce_scatter_end(rs_started[1], dh_out, "l1", None, 1, DEPTH)}
    reduced[0] = _reduce_scatter_end(rs_started[0], total, "l0", reduced[1][0], 0, DEPTH)
    g_w_in_t = reduced[0][0]
    gsm["w_in"] = jnp.transpose(g_w_in_t, (0, 2, 1))
    gsm["w_out"] = jnp.stack([reduced[l][1] for l in range(DEPTH)])
    gsm["w_decay_up"] = jnp.stack([reduced[l][2][:, :rank_w].T for l in range(DEPTH)])
    gsm["a_up"] = jnp.stack([reduced[l][2][:, rank_w:rank_w + rank_a].T for l in range(DEPTH)])
    gsm["v_mix_down"] = reduced[1][2][:, LANES:LANES + rank_v][None]
    gsm["v_mix_up"] = reduced[1][2][:, LANES + rank_v:LANES + 2 * rank_v].T[None]

    deltas, new_m, new_v = {}, {}, {}
    swap = lambda a: jnp.transpose(a, (0, 2, 1))
    deltas["w_in"], new_m["w_in"], new_v["w_in"] = (
        swap(a) for a in _adamw(swap(w_in), g_w_in_t, swap(m_w_in), swap(v_w_in), "adamw_w_in"))
    deltas["w_out"], new_m["w_out"], new_v["w_out"] = _adamw(w_out, gsm["w_out"], m_w_out, v_w_out, "adamw_w_out")
    rest = [nm for nm in _NAMES if nm not in ("w_in", "w_out")]
    d_rest, m_rest, v_rest = _adamw_many([weights[nm] for nm in rest], [gsm[nm] for nm in rest],
                                         [mom1[nm] for nm in rest], [mom2[nm] for nm in rest], "adamw_small")
    for i, nm in enumerate(rest):
        deltas[nm], new_m[nm], new_v[nm] = d_rest[i], m_rest[i], v_rest[i]
    return (loss, grad_x, *[gsm[nm] for nm in _NAMES], *[deltas[nm] for nm in _NAMES],
            *[new_m[nm] for nm in _NAMES], *[new_v[nm] for nm in _NAMES])
```

```python
import functools

import jax
import jax.numpy as jnp
from jax import lax
from jax.experimental import pallas as pl
from jax.experimental.pallas import tpu as pltpu

f32 = jnp.float32
bf16 = jnp.bfloat16

N_DEV = 8
CHUNK = 64
LANES = 128
RWKV_HEAD = 64
DEPTH = 2
ALPHA = (2 * DEPTH) ** 0.25
LN_EPS = 1e-5
GN_EPS = 64e-5
RMS_EPS = 1e-5
LB_FLOOR = 1e-30
ADAM_LR, ADAM_B1, ADAM_B2, ADAM_EPS, ADAM_WD, ADAM_STEP = 0.001, 0.9, 0.999, 1e-08, 0.01, 10
MESH = pl.DeviceIdType.MESH


def _iota(shape, d):
    return lax.broadcasted_iota(jnp.int32, shape, d)


_DIMS = {"nn": (((1,), (0,)), ((), ())), "nt": (((1,), (1,)), ((), ())), "tn": (((0,), (0,)), ((), ()))}
_BATCH_DIMS = {"nn": (((2,), (1,)), ((0,), (0,))), "nt": (((2,), (2,)), ((0,), (0,))), "tn": (((1,), (1,)), ((0,), (0,)))}
_K_AXES = {"nn": (-1, -2), "nt": (-1, -1), "tn": (-2, -2)}


def _mxu(a, b, mode):
    return lax.dot_general(a, b, (_BATCH_DIMS if a.ndim == 3 else _DIMS)[mode], preferred_element_type=f32)


def _split(x):
    hi = x.astype(bf16)
    return hi, (x - hi.astype(f32)).astype(bf16)


def _mm2_impl(a, b, mode, passes=3):
    if passes == 1:
        return _mxu(a.astype(bf16), b.astype(bf16), mode)
    ah, al = _split(a)
    if passes == 3:
        bh, bl = _split(b)
        lhs, rhs = [ah, ah, al], [bh, bl, bh]
    else:
        bh = b.astype(bf16)
        lhs, rhs = [ah, al], [bh, bh]
    ka, kb = _K_AXES[mode]
    k = a.shape[ka]
    if k % (LANES if -1 in (ka, kb) else 16) == 0:
        return _mxu(jnp.concatenate(lhs, axis=ka), jnp.concatenate(rhs, axis=kb), mode)
    out = _mxu(lhs[0], rhs[0], mode)
    for x, y in zip(lhs[1:], rhs[1:]):
        out = out + _mxu(x, y, mode)
    return out


@functools.partial(jax.custom_vjp, nondiff_argnums=(2, 3))
def _mm2(a, b, mode, passes=3):
    return _mm2_impl(a, b, mode, passes)


def _mm2_fwd(a, b, mode, passes):
    return _mm2_impl(a, b, mode, passes), (a, b)


def _mm2_bwd(mode, passes, res, g):
    a, b = res
    if mode == "nn":
        return _mm2_impl(g, b, "nt", passes), _mm2_impl(a, g, "tn", passes)
    if mode == "nt":
        return _mm2_impl(g, b, "nn", passes), _mm2_impl(g, a, "tn", passes)
    return _mm2_impl(b, g, "nt", passes), _mm2_impl(a, g, "nn", passes)


_mm2.defvjp(_mm2_fwd, _mm2_bwd)

TRI_PASSES = 1
APPLY_PASSES = 1


def _const_impl(cm, x, mode):
    if mode in ("r", "rt"):
        shape = x.shape
        out = _mxu(x.astype(bf16).reshape(-1, shape[-1]), cm, "nn" if mode == "r" else "nt")
        return out.reshape(shape[:-1] + (out.shape[-1],))
    hi, lo = _split(x)
    if x.ndim == 3:
        cm = jnp.broadcast_to(cm, (x.shape[0],) + cm.shape)
    return _mxu(cm, hi, mode) + _mxu(cm, lo, mode)


@jax.custom_vjp
def _const_left(cm, x):
    return _const_impl(cm, x, "nn")


_const_left.defvjp(lambda cm, x: (_const_impl(cm, x, "nn"), cm),
                   lambda cm, g: (jnp.zeros_like(cm), _const_impl(cm, g, "tn")))


@jax.custom_vjp
def _const_right(x, cm):
    return _const_impl(cm, x, "r")


_const_right.defvjp(lambda x, cm: (_const_impl(cm, x, "r"), cm),
                    lambda cm, g: (_const_impl(cm, g, "rt"), jnp.zeros_like(cm)))


def _tri_inv(a):
    n = a.shape[-1]
    tm = (_iota((n, n), 0) == _iota((n, n), 1)).astype(f32) + a
    ak = a
    for _ in range(5):
        ak = _mm2_impl(ak, ak, "nn", TRI_PASSES)
        tm = tm + _mm2_impl(tm, ak, "nn", TRI_PASSES)
    return tm


@jax.custom_vjp
def _tri_solve(tm, a, x):
    del a
    return _mm2_impl(tm, x, "nn", APPLY_PASSES)


def _tri_solve_fwd(tm, a, x):
    u = _mm2_impl(tm, x, "nn", APPLY_PASSES)
    return u, (tm, u)


def _tri_solve_bwd(res, du):
    tm, u = res
    dx = _mm2_impl(tm, du, "tn", APPLY_PASSES)
    return jnp.zeros_like(tm), _mm2_impl(dx, u, "nt", APPLY_PASSES), dx


_tri_solve.defvjp(_tri_solve_fwd, _tri_solve_bwd)


def _col_of_row(row_vec):
    n = row_vec.shape[-1]
    eye = _iota((n, n), 0) == _iota((n, n), 1)
    return jnp.sum(jnp.where(eye, jnp.broadcast_to(row_vec, row_vec.shape[:-2] + (n, n)), 0.0), axis=-1, keepdims=True)


def _softplus(x):
    return jnp.maximum(x, 0.0) + jnp.log1p(jnp.exp(-jnp.abs(x)))


def _log_sigmoid(x):
    return -_softplus(-x)


def _logaddexp(a, b):
    return jnp.maximum(a, b) + jnp.log1p(jnp.exp(-jnp.abs(a - b)))


def _silu(x):
    return x * jax.nn.sigmoid(x)


def _tril(c, strict):
    r, s = _iota((c, c), 0), _iota((c, c), 1)
    return (r > s) if strict else (r >= s)


def _last_row(a):
    c = a.shape[-2]
    return jnp.sum(jnp.where(_iota(a.shape, a.ndim - 2) == c - 1, a, 0.0), axis=-2, keepdims=True)


def _rwkv_pre(layer1, prm, y, prev, vf):
    c = y.shape[0]
    if layer1:
        mu, w0, a0, wup, aup, v0, vdown, vup = prm
    else:
        mu, w0, a0, wup, aup = prm
    dr = w0.shape[1]
    shift = (_iota((c, c), 0) == _iota((c, c), 1) + 1).astype(bf16)
    y_prev = _const_left(shift, y) + jnp.where(_iota((c, 1), 0) == 0, prev, 0.0)
    rw = y + mu * (y_prev - y)
    r, k, v, z = (rw[:, i * dr:(i + 1) * dr] for i in range(4))
    wdad = rw[:, 4 * dr:4 * dr + LANES]
    w_raw = w0 + _mm2(jnp.tanh(wdad), wup, "nn")
    lw = -jnp.exp(-_softplus(-w_raw) - 0.5)
    asig = jax.nn.sigmoid(a0 + _mm2(wdad, aup, "nn"))
    if layer1:
        v = v + (vf - v) * jax.nn.sigmoid(v0 + _mm2(_mm2(v, vdown, "nn"), vup, "nn"))
    return r, k, v, z, lw, asig


def _rwkv_pair(pp, m0, xs, tm=None):
    kkw, kaw, rkw, gnw, gnb = pp
    r, k, v, z, lw, asig = xs
    c = r.shape[-2]
    n2 = 2 * c
    lane = _iota((1, LANES), 1)
    mh0, mh1 = (lane < RWKV_HEAD).astype(f32), (lane >= RWKV_HEAD).astype(f32)
    same_head = _iota((LANES, LANES), 0) // RWKV_HEAD == _iota((LANES, LANES), 1) // RWKV_HEAD
    g = same_head.astype(bf16)

    def seg(x):
        return _const_right(x, g)

    def stack(x):
        return jnp.concatenate([x * mh0, x * mh1], axis=-2)

    kk = k * kkw
    kk = kk / jnp.maximum(jnp.sqrt(seg(kk * kk)), 1e-12)
    k2 = k * (1.0 + (asig - 1.0) * kaw)
    a = -kk
    b = kk * asig
    cum = _const_left(_tril(c, False).astype(bf16), lw)
    at = stack(a * jnp.exp(cum - lw))
    rt = stack(r * jnp.exp(cum))
    en = jnp.exp(-cum)
    sc = _mm2(jnp.concatenate([at, rt], axis=-2), jnp.concatenate([stack(b * en), stack(k2 * en)], axis=-2), "nt")
    row, col = _iota((n2, n2), 0), _iota((n2, n2), 1)
    same = row // c == col // c
    strict = same & (row % c > col % c)
    incl = same & (row % c >= col % c)
    aab = jnp.where(strict, sc[..., :n2, :n2], 0.0)
    aak = jnp.where(strict, sc[..., :n2, n2:], 0.0)
    arb = jnp.where(incl, sc[..., n2:, :n2], 0.0)
    ark = jnp.where(incl, sc[..., n2:, n2:], 0.0)
    vv = jnp.concatenate([v, v], axis=-2)
    mask_st = jnp.concatenate([jnp.broadcast_to(mh0, (c, LANES)), jnp.broadcast_to(mh1, (c, LANES))], axis=0)
    x_st = _mm2(jnp.concatenate([at, aak], axis=-1), jnp.concatenate([m0, vv], axis=-2), "nn", APPLY_PASSES)
    if tm is None:
        tm = _tri_inv(lax.stop_gradient(aab))
    u_st = _tri_solve(tm, aab, x_st) * mask_st
    o_st = _mm2(jnp.concatenate([rt, arb, ark], axis=-1), jnp.concatenate([m0, u_st, vv], axis=-2), "nn", APPLY_PASSES) * mask_st
    u = u_st[..., :c, :] + u_st[..., c:, :]
    o = o_st[..., :c, :] + o_st[..., c:, :]
    cum_last = _last_row(cum)
    dec_end = jnp.exp(cum_last - cum)
    m_new = _col_of_row(jnp.exp(cum_last)) * m0 + _mm2(
        jnp.concatenate([b * dec_end, k2 * dec_end], axis=-2), jnp.concatenate([u, v], axis=-2), "tn", APPLY_PASSES) * same_head.astype(f32)
    mean = seg(o) * (1.0 / RWKV_HEAD)
    d = o - mean
    var = seg(d * d) * (1.0 / RWKV_HEAD)
    on = d * lax.rsqrt(var + GN_EPS) * gnw + gnb
    bonus = seg(r * k2 * rkw) * v
    return (on + bonus) * _silu(z), m_new, tm


def _split_lanes(a, n):
    return [a[:, i * LANES:(i + 1) * LANES] for i in range(n)]


def _rwkv_step(layer1, prm, y, prev, vf, pp, m0, tm=None):
    xs = _rwkv_pre(layer1, prm, y, prev, vf)
    n_pair = m0.shape[0]
    og, m_new, tm = _rwkv_pair(pp, m0, tuple(jnp.concatenate([p[None] for p in _split_lanes(a, n_pair)], axis=0) for a in xs), tm)
    return og, m_new, xs[2], tm


def _group(n):
    return n


def _rwkv_specs(layer1, t, dr, rwc, n_pair, rev):
    nc = t // CHUNK
    grp = _group(n_pair)

    def cidx(c):
        return (nc - 1 - c) if rev else c

    full = lambda shape: pl.BlockSpec(shape, lambda c, p: tuple(0 for _ in shape))
    specs = [
        pl.BlockSpec((CHUNK, rwc), lambda c, p: (cidx(c), 0)),
        pl.BlockSpec((8, rwc), lambda c, p: (jnp.maximum(cidx(c) * (CHUNK // 8) - 1, 0), 0)),
    ]
    if layer1:
        specs.append(pl.BlockSpec((CHUNK, dr), lambda c, p: (cidx(c), 0)))
    prm_shapes = [(1, rwc), (1, dr), (1, dr), (LANES, dr), (LANES, dr)]
    if layer1:
        prm_shapes += [(1, dr), (dr, LANES), (LANES, dr)]
    specs += [full(s) for s in prm_shapes]
    specs.append(pl.BlockSpec((grp, 8, LANES), lambda c, p: (p, 0, 0)))
    return specs, prm_shapes, cidx, full


def _rwkv_fwd(layer1, proj, vf, prm, pp, cat_width):
    t = proj.shape[0]
    dr = prm[1].shape[1]
    rwc = prm[0].shape[1]
    n_pair = dr // LANES
    nc = t // CHUNK
    n_prm = len(prm)
    specs, _, _, _ = _rwkv_specs(layer1, t, dr, rwc, n_pair, False)

    def body(*refs):
        y_ref, prev_ref = refs[0], refs[1]
        i = 2
        vf_ref = None
        if layer1:
            vf_ref = refs[i]
            i += 1
        prm_refs = refs[i:i + n_prm]
        i += n_prm
        pp_ref = refs[i]
        i += 1
        cat_ref = refs[i]
        i += 1
        vout_ref = None
        if not layer1:
            vout_ref = refs[i]
            i += 1
        mck_ref, m_s = refs[i], refs[i + 1]
        c = pl.program_id(0)

        @pl.when(c == 0)
        def _():
            m_s[...] = jnp.zeros_like(m_s)

        prev = prev_ref[pl.ds(7, 1), :] * (c != 0).astype(f32)
        m0 = m_s[...]
        ppv = tuple(pp_ref[:, pl.ds(q, 1), :] for q in range(5))
        og, m_new, v, tm = _rwkv_step(layer1, tuple(r[...] for r in prm_refs), y_ref[...], prev,
                                      vf_ref[...] if layer1 else None, ppv, m0)
        mck_ref[0, :n_pair] = m0
        mck_ref[0, n_pair:] = tm
        if not layer1:
            vout_ref[...] = v
        for j in range(n_pair):
            cat_ref[:, j * LANES:(j + 1) * LANES] = og[j]
        m_s[...] = m_new

    grp = _group(n_pair)
    assert grp == n_pair
    out_shape = [jax.ShapeDtypeStruct((t, cat_width), f32)]
    out_specs = [pl.BlockSpec((CHUNK, grp * LANES), lambda c, p: (c, p))]
    if not layer1:
        out_shape.append(jax.ShapeDtypeStruct((t, dr), f32))
        out_specs.append(pl.BlockSpec((CHUNK, dr), lambda c, p: (c, 0)))
    out_shape.append(jax.ShapeDtypeStruct((nc, 2 * n_pair, LANES, LANES), f32))
    out_specs.append(pl.BlockSpec((1, 2 * grp, LANES, LANES), lambda c, p: (c, p, 0, 0)))
    args = [proj, proj] + ([vf] if layer1 else []) + list(prm) + [pp]
    return pl.pallas_call(
        body, grid=(nc, 1), in_specs=specs, out_specs=out_specs, out_shape=out_shape,
        scratch_shapes=[pltpu.VMEM((n_pair, LANES, LANES), f32)],
        compiler_params=pltpu.CompilerParams(dimension_semantics=("arbitrary", "arbitrary")),
        name=f"rwkv_fwd_l{int(layer1)}",
    )(*args)


def _rwkv_bwd(layer1, proj, vf, prm, pp, mck, dcat, dvout):
    t = proj.shape[0]
    dr = prm[1].shape[1]
    rwc = prm[0].shape[1]
    n_pair = dr // LANES
    nc = t // CHUNK
    n_prm = len(prm)
    specs, prm_shapes, cidx, full = _rwkv_specs(layer1, t, dr, rwc, n_pair, True)
    grp = _group(n_pair)
    assert grp == n_pair
    specs.append(pl.BlockSpec((1, 2 * grp, LANES, LANES), lambda c, p: (cidx(c), p, 0, 0)))
    specs.append(pl.BlockSpec((CHUNK, grp * LANES), lambda c, p: (cidx(c), p)))
    if not layer1:
        specs.append(pl.BlockSpec((CHUNK, dr), lambda c, p: (cidx(c), 0)))

    def body(*refs):
        y_ref, prev_ref = refs[0], refs[1]
        i = 2
        vf_ref = None
        if layer1:
            vf_ref = refs[i]
            i += 1
        prm_refs = refs[i:i + n_prm]
        i += n_prm
        pp_ref, mck_ref, dog_ref = refs[i], refs[i + 1], refs[i + 2]
        i += 3
        dvout_ref = None
        if not layer1:
            dvout_ref = refs[i]
            i += 1
        dy_ref = refs[i]
        i += 1
        dvf_ref = None
        if layer1:
            dvf_ref = refs[i]
            i += 1
        dprm_refs = refs[i:i + n_prm]
        i += n_prm
        dpp_ref = refs[i]
        dm_s, dprev_s = refs[i + 1:i + 3]
        c = pl.program_id(0)
        cr = nc - 1 - c

        @pl.when(c == 0)
        def _():
            dm_s[...] = jnp.zeros_like(dm_s)
            dprev_s[...] = jnp.zeros_like(dprev_s)
            dpp_ref[...] = jnp.zeros_like(dpp_ref)
            for r in dprm_refs:
                r[...] = jnp.zeros_like(r)

        prev = prev_ref[pl.ds(7, 1), :] * (cr != 0).astype(f32)
        prm_v = tuple(r[...] for r in prm_refs)
        ppv = tuple(pp_ref[:, pl.ds(q, 1), :] for q in range(5))
        dog = jnp.stack([dog_ref[:, j * LANES:(j + 1) * LANES] for j in range(n_pair)], axis=0)
        m0, tm = mck_ref[0, :n_pair], mck_ref[0, n_pair:]
        no_tm = jnp.zeros_like(tm)
        if layer1:
            _, vjp = jax.vjp(lambda a, b, d, e, g, h: _rwkv_step(True, a, b, d, e, g, h, tm),
                             prm_v, y_ref[...], prev, vf_ref[...], ppv, m0)
            dprm, dy, dprev, dvf, dppv, dm0 = vjp((dog, dm_s[...], jnp.zeros((CHUNK, dr), f32), no_tm))
            dvf_ref[...] = dvf
        else:
            _, vjp = jax.vjp(lambda a, b, d, e, g: _rwkv_step(False, a, b, d, None, e, g, tm), prm_v, y_ref[...], prev, ppv, m0)
            dprm, dy, dprev, dppv, dm0 = vjp((dog, dm_s[...], dvout_ref[...], no_tm))
        dm_s[...] = dm0
        for q in range(5):
            dpp_ref[:, pl.ds(q, 1), :] += dppv[q]
        dy_ref[...] = (dy + jnp.where(_iota((CHUNK, 1), 0) == CHUNK - 1, dprev_s[...], 0.0)).astype(bf16)
        dprev_s[...] = dprev
        for r, gval in zip(dprm_refs, dprm):
            r[...] += gval

    out_shape = [jax.ShapeDtypeStruct((t, proj.shape[1]), bf16)]
    out_specs = [pl.BlockSpec((CHUNK, rwc), lambda c, p: (cidx(c), 0))]
    if layer1:
        out_shape.append(jax.ShapeDtypeStruct((t, dr), f32))
        out_specs.append(pl.BlockSpec((CHUNK, dr), lambda c, p: (cidx(c), 0)))
    out_shape += [jax.ShapeDtypeStruct(s, f32) for s in prm_shapes]
    out_specs += [full(s) for s in prm_shapes]
    out_shape.append(jax.ShapeDtypeStruct((n_pair, 8, LANES), f32))
    out_specs.append(full((n_pair, 8, LANES)))
    args = [proj, proj] + ([vf] if layer1 else []) + list(prm) + [pp, mck, dcat] + ([] if layer1 else [dvout])
    return pl.pallas_call(
        body, grid=(nc, 1), in_specs=specs, out_specs=out_specs, out_shape=out_shape,
        scratch_shapes=[pltpu.VMEM((n_pair, LANES, LANES), f32), pltpu.VMEM((1, rwc), f32)],
        compiler_params=pltpu.CompilerParams(dimension_semantics=("arbitrary", "arbitrary")),
        name=f"rwkv_bwd_l{int(layer1)}",
    )(*args)


def _hgrn_chunk(layer1, lbl, gnw, s0, q_raw, f_raw, i_in, z):
    c = q_raw.shape[-2]
    q = _silu(q_raw)
    ls = _log_sigmoid(f_raw)
    if layer1:
        l0, l1 = lbl[..., 0:1, :], lbl[..., 1:2, :]
        mx = jnp.maximum(l0, l1)
        e0, e1 = jnp.exp(l0 - mx), jnp.exp(l1 - mx)
        sm0, sm1 = e0 / (e0 + e1), e1 / (e0 + e1)
        lb = (sm0 + sm1) - sm0
        log_f = _logaddexp(jnp.log(jnp.maximum(lb, LB_FLOOR)), jnp.log1p(-lb) + ls)
        k = (1.0 - lb) * jax.nn.sigmoid(-f_raw)
    else:
        log_f = _logaddexp(jnp.full_like(ls, jnp.log(jnp.float32(LB_FLOOR))), ls)
        k = jax.nn.sigmoid(-f_raw)
    row, col = _iota((c, c), 0), _iota((c, c), 1)
    trow = _iota((c, 1), 0)
    halves = []
    half = c // 2
    while half >= 1:
        halves.append(half)
        half //= 2
    cmat = jnp.concatenate([(col <= row).astype(f32)]
                           + [(col <= (row // (2 * hf)) * (2 * hf) + hf - 1).astype(f32) for hf in halves], axis=0)
    ball = _const_left(cmat.astype(bf16), log_f)
    b = ball[..., :c, :]
    att = None
    for lvl, hf in enumerate(halves):
        blk = 2 * hf
        bref = ball[..., (lvl + 1) * c:(lvl + 2) * c, :]
        upper = (trow % blk) >= hf
        dec = jnp.exp(jnp.where(upper, b - bref, bref - b))
        qh = jnp.where(upper, q * dec, 0.0)
        kh = jnp.where(upper, 0.0, k * dec)
        term = jnp.where(row // blk == col // blk, _mm2(qh, kh, "nt", APPLY_PASSES), 0.0)
        att = term if att is None else att + term
    lhs = jnp.concatenate([q * jnp.exp(b), att, jnp.zeros(att.shape[:-1] + (LANES - c,), f32)], axis=-1)
    rhs = jnp.concatenate([s0, i_in, jnp.zeros(i_in.shape[:-2] + (LANES - c, i_in.shape[-1]), f32)], axis=-2)
    o = _mm2(lhs, rhs, "nn", APPLY_PASSES) + jnp.sum(q * k, axis=-1, keepdims=True) * i_in
    b_last = _last_row(b)
    s_new = _col_of_row(jnp.exp(b_last)) * s0 + _mm2(k * jnp.exp(b_last - b), i_in, "tn", APPLY_PASSES)
    o = o * lax.rsqrt(jnp.mean(o * o, axis=-1, keepdims=True) + RMS_EPS)
    return o * gnw * _silu(z), s_new


def _hgrn_in_specs(t, dh, col0, rev):
    nc = t // CHUNK
    nh = dh // LANES

    def cidx(c):
        return (nc - 1 - c) if rev else c

    grp = _group(nh)
    specs = [pl.BlockSpec((CHUNK, LANES), functools.partial(lambda g, j, h, c: (cidx(c), col0 + g * nh + h * grp + j), g, j))
             for j in range(grp) for g in range(4)]
    specs.append(pl.BlockSpec((2, grp * LANES), lambda h, c: (0, h)))
    specs.append(pl.BlockSpec((1, grp * LANES), lambda h, c: (0, h)))
    return specs, cidx, grp


def _hgrn_fwd(layer1, proj, lbl, gnw, cat, rwc):
    t, d = cat.shape
    dh = gnw.shape[1]
    nh = dh // LANES
    nc = t // CHUNK
    col0 = rwc // LANES
    specs, _, grp = _hgrn_in_specs(t, dh, col0, False)
    specs.append(pl.BlockSpec(memory_space=pl.ANY))
    assert (d - dh) % (grp * LANES) == 0
    cat_col0 = (d - dh) // (grp * LANES)

    def body(*refs):
        x_refs = refs[:4 * grp]
        lbl_ref, gnw_ref, _, cat_ref, sck_ref, s_s = refs[4 * grp:]
        c = pl.program_id(1)

        @pl.when(c == 0)
        def _():
            s_s[...] = jnp.zeros_like(s_s)

        lanes = [slice(j * LANES, (j + 1) * LANES) for j in range(grp)]
        s0 = s_s[...]
        sck_ref[:, 0] = s0
        out, s_new = _hgrn_chunk(layer1, jnp.stack([lbl_ref[:, ln] for ln in lanes]), jnp.stack([gnw_ref[:, ln] for ln in lanes]),
                                 s0, *(jnp.stack([x_refs[4 * j + g][...] for j in range(grp)]) for g in range(4)))
        for j in range(grp):
            cat_ref[:, lanes[j]] = out[j]
        s_s[...] = s_new

    return pl.pallas_call(
        body, grid=(nh // grp, nc), in_specs=specs,
        out_specs=[pl.BlockSpec((CHUNK, grp * LANES), lambda h, c: (c, cat_col0 + h)),
                   pl.BlockSpec((grp, 1, LANES, LANES), lambda h, c: (h, c, 0, 0))],
        out_shape=[jax.ShapeDtypeStruct((t, d), f32), jax.ShapeDtypeStruct((nh, nc, LANES, LANES), f32)],
        scratch_shapes=[pltpu.VMEM((grp, LANES, LANES), f32)],
        input_output_aliases={4 * grp + 2: 0},
        compiler_params=pltpu.CompilerParams(dimension_semantics=("arbitrary", "arbitrary")),
        name=f"hgrn_fwd_l{int(layer1)}",
    )(*([proj] * (4 * grp)), lbl, gnw, cat)


def _hgrn_bwd(layer1, proj, lbl, gnw, sck, dcat, rwc, dproj):
    t, d = dcat.shape
    dh = gnw.shape[1]
    nh = dh // LANES
    nc = t // CHUNK
    col0 = rwc // LANES
    specs, cidx, grp = _hgrn_in_specs(t, dh, col0, True)
    assert grp == nh and (d - dh) % (grp * LANES) == 0
    cat_col0 = (d - dh) // (grp * LANES)
    specs.append(pl.BlockSpec((grp, 1, LANES, LANES), lambda h, c: (h, cidx(c), 0, 0)))
    specs.append(pl.BlockSpec((CHUNK, grp * LANES), lambda h, c: (cidx(c), cat_col0 + h)))
    specs.append(pl.BlockSpec(memory_space=pl.ANY))

    def body(*refs):
        x_refs = refs[:4 * grp]
        lbl_ref, gnw_ref, sck_ref, do_ref, _, dp_hbm, dlbl_ref, dgnw_ref, ds_s, stage, sems = refs[4 * grp:]
        c = pl.program_id(1)
        slot = c % 2

        def put(s, g, chunk):
            return pltpu.make_async_copy(stage.at[s, g], dp_hbm.at[pl.ds(chunk * CHUNK, CHUNK), pl.ds(rwc + g * dh, dh)],
                                         sems.at[s, g])

        @pl.when(c == 0)
        def _():
            ds_s[...] = jnp.zeros_like(ds_s)
            dlbl_ref[...] = jnp.zeros_like(dlbl_ref)
            dgnw_ref[...] = jnp.zeros_like(dgnw_ref)

        @pl.when(c >= 2)
        def _():
            for g in range(4):
                put(slot, g, 0).wait()

        lanes = [slice(j * LANES, (j + 1) * LANES) for j in range(grp)]
        _, vjp = jax.vjp(functools.partial(_hgrn_chunk, layer1),
                         jnp.stack([lbl_ref[:, ln] for ln in lanes]), jnp.stack([gnw_ref[:, ln] for ln in lanes]), sck_ref[:, 0],
                         *(jnp.stack([x_refs[4 * j + g][...] for j in range(grp)]) for g in range(4)))
        dlbl, dgnw, ds0, dq, df, di, dz = vjp((jnp.stack([do_ref[:, ln] for ln in lanes]), ds_s[...]))
        ds_s[...] = ds0
        for j in range(grp):
            dlbl_ref[:, lanes[j]] += dlbl[j]
            dgnw_ref[:, lanes[j]] += dgnw[j]
            for g, val in enumerate((dq, df, di, dz)):
                stage[slot, g, :, lanes[j]] = val[j].astype(bf16)
        for g in range(4):
            put(slot, g, nc - 1 - c).start()

        @pl.when(c == nc - 1)
        def _():
            for g in range(4):
                put(slot, g, 0).wait()
                if nc >= 2:
                    put(1 - slot, g, 0).wait()

    return pl.pallas_call(
        body, grid=(1, nc), in_specs=specs,
        out_specs=[pl.BlockSpec(memory_space=pl.ANY),
                   pl.BlockSpec((2, grp * LANES), lambda h, c: (0, h)),
                   pl.BlockSpec((1, grp * LANES), lambda h, c: (0, h))],
        out_shape=[jax.ShapeDtypeStruct(dproj.shape, dproj.dtype), jax.ShapeDtypeStruct((2, dh), f32),
                   jax.ShapeDtypeStruct((1, dh), f32)],
        scratch_shapes=[pltpu.VMEM((grp, LANES, LANES), f32), pltpu.VMEM((2, 4, CHUNK, dh), bf16),
                        pltpu.SemaphoreType.DMA((2, 4))],
        input_output_aliases={4 * grp + 4: 0},
        compiler_params=pltpu.CompilerParams(dimension_semantics=("arbitrary", "arbitrary")),
        name=f"hgrn_bwd_l{int(layer1)}",
    )(*([proj] * (4 * grp)), lbl, gnw, sck, dcat, dproj)


def _ln(h, y, w, b):
    u = ALPHA * h + y
    mu = jnp.mean(u, axis=-1, keepdims=True)
    var = jnp.mean(jnp.square(u - mu), axis=-1, keepdims=True)
    return (u - mu) * lax.rsqrt(var + LN_EPS) * w + b


def _row_tile(t):
    return 256 if t % 256 == 0 else t


def _ln_fwd(h, y, w, b):
    t, d = h.shape
    tr = _row_tile(t)

    def body(h_ref, y_ref, w_ref, b_ref, o_ref, o16_ref):
        out = _ln(h_ref[...], y_ref[...], w_ref[...], b_ref[...])
        o_ref[...] = out
        o16_ref[...] = out.astype(bf16)

    row = pl.BlockSpec((tr, d), lambda i: (i, 0))
    vec = pl.BlockSpec((1, d), lambda i: (0, 0))
    return pl.pallas_call(body, grid=(t // tr,), in_specs=[row, row, vec, vec], out_specs=[row, row],
                          out_shape=[jax.ShapeDtypeStruct((t, d), f32), jax.ShapeDtypeStruct((t, d), bf16)],
                          name="ln_fwd")(h, y, w, b)


def _ln_loss_bwd(h, y, w, b, tgt):
    t, d = h.shape
    tr = _row_tile(t)

    def body(h_ref, y_ref, w_ref, b_ref, t_ref, dy_ref, dy16_ref, dw_ref, db_ref, loss_ref):
        @pl.when(pl.program_id(0) == 0)
        def _():
            dw_ref[...] = jnp.zeros_like(dw_ref)
            db_ref[...] = jnp.zeros_like(db_ref)
            loss_ref[...] = jnp.zeros_like(loss_ref)

        out, vjp = jax.vjp(lambda yy, ww, bb: _ln(h_ref[...], yy, ww, bb), y_ref[...], w_ref[...], b_ref[...])
        err = out - t_ref[...]
        loss_ref[...] += 0.5 * jnp.sum(jnp.mean(jnp.square(err), axis=-1, keepdims=True), axis=0, keepdims=True)
        dy, dw, db = vjp(err * (1.0 / d))
        dy_ref[...] = dy
        dy16_ref[...] = dy.astype(bf16)
        dw_ref[...] += dw
        db_ref[...] += db

    row = pl.BlockSpec((tr, d), lambda i: (i, 0))
    vec = pl.BlockSpec((1, d), lambda i: (0, 0))
    return pl.pallas_call(
        body, grid=(t // tr,), in_specs=[row, row, vec, vec, row],
        out_specs=[row, row, vec, vec, pl.BlockSpec((1, LANES), lambda i: (0, 0))],
        out_shape=[jax.ShapeDtypeStruct((t, d), f32), jax.ShapeDtypeStruct((t, d), bf16), jax.ShapeDtypeStruct((1, d), f32),
                   jax.ShapeDtypeStruct((1, d), f32), jax.ShapeDtypeStruct((1, LANES), f32)],
        compiler_params=pltpu.CompilerParams(dimension_semantics=("arbitrary",)), name="ln_loss_bwd")(h, y, w, b, tgt)


def _ln_bwd(h, y, w, b, dout):
    t, d = h.shape
    tr = _row_tile(t)

    def body(h_ref, y_ref, w_ref, b_ref, do_ref, dy_ref, dy16_ref, dw_ref, db_ref):
        @pl.when(pl.program_id(0) == 0)
        def _():
            dw_ref[...] = jnp.zeros_like(dw_ref)
            db_ref[...] = jnp.zeros_like(db_ref)

        _, vjp = jax.vjp(lambda yy, ww, bb: _ln(h_ref[...], yy, ww, bb), y_ref[...], w_ref[...], b_ref[...])
        dy, dw, db = vjp(do_ref[...])
        dy_ref[...] = dy
        dy16_ref[...] = dy.astype(bf16)
        dw_ref[...] += dw
        db_ref[...] += db

    row = pl.BlockSpec((tr, d), lambda i: (i, 0))
    vec = pl.BlockSpec((1, d), lambda i: (0, 0))
    return pl.pallas_call(
        body, grid=(t // tr,), in_specs=[row, row, vec, vec, row], out_specs=[row, row, vec, vec],
        out_shape=[jax.ShapeDtypeStruct((t, d), f32), jax.ShapeDtypeStruct((t, d), bf16),
                   jax.ShapeDtypeStruct((1, d), f32), jax.ShapeDtypeStruct((1, d), f32)],
        compiler_params=pltpu.CompilerParams(dimension_semantics=("arbitrary",)), name="ln_bwd")(h, y, w, b, dout)


def _pick(n, prefs):
    for p in prefs:
        if n % p == 0:
            return p
    return n


def _tile(n, want):
    if n <= want:
        return n
    for cand in range(want - want % LANES, 0, -LANES):
        if n % cand == 0:
            return cand
    return n


_MM_TILES = {"proj": (1024, 1664, 2048), "out": (1024, 1024, 2048), "dcat": (1024, 1024, 2048),
             "dwout": (512, 2048, 2048), "dwin": (640, 2048, 2048), "dh": (1024, 1024, 1664)}


def _matmul(a, b, mode, name, tiles, add=None, add_scale=1.0, out_dtype=f32, after=None):
    if mode == "nn":
        (m, k), n = a.shape, b.shape[1]
    elif mode == "nt":
        (m, k), n = a.shape, b.shape[0]
    else:
        (k, m), n = a.shape, b.shape[1]
    tm, tn, tk = _tile(m, tiles[0]), _tile(n, tiles[1]), _tile(k, tiles[2])
    nk = k // tk
    cache_a = nk == 1 and a.dtype != bf16 and n // tn > 1

    def body(*refs):
        a_ref, b_ref = refs[0], refs[1]
        add_ref = refs[2] if add is not None else None
        n_in = 2 + (add is not None) + (after is not None)
        o_ref = refs[n_in]
        scratch = refs[n_in + 1:]

        def finish(res):
            if add is not None:
                res = res + add_scale * add_ref[...]
            o_ref[...] = res.astype(out_dtype)

        if cache_a:
            a_bf = scratch[0]

            @pl.when(pl.program_id(1) == 0)
            def _():
                a_bf[...] = a_ref[...].astype(bf16)

            a_val = a_bf[...]
        else:
            a_val = a_ref[...].astype(bf16)
        prod = lax.dot_general(a_val, b_ref[...].astype(bf16), _DIMS[mode], preferred_element_type=f32)
        if nk == 1:
            finish(prod)
        else:
            acc = scratch[-1]
            kk = pl.program_id(2)

            @pl.when(kk == 0)
            def _():
                acc[...] = prod

            @pl.when(kk != 0)
            def _():
                acc[...] += prod

            @pl.when(kk == nk - 1)
            def _():
                finish(acc[...])

    a_shape = (tk, tm) if mode == "tn" else (tm, tk)
    a_spec = pl.BlockSpec(a_shape, (lambda i, j, kk: (kk, i)) if mode == "tn" else (lambda i, j, kk: (i, kk)))
    b_spec = pl.BlockSpec((tn, tk), lambda i, j, kk: (j, kk)) if mode == "nt" else pl.BlockSpec((tk, tn), lambda i, j, kk: (kk, j))
    o_spec = pl.BlockSpec((tm, tn), lambda i, j, kk: (i, j))
    in_specs = [a_spec, b_spec] + ([o_spec] if add is not None else []) + ([pl.BlockSpec(memory_space=pl.ANY)] if after is not None else [])
    args = [a, b] + ([add] if add is not None else []) + ([after] if after is not None else [])
    scratch_shapes = ([pltpu.VMEM(a_shape, bf16)] if cache_a else []) + ([pltpu.VMEM((tm, tn), f32)] if nk > 1 else [])
    return pl.pallas_call(
        body, grid=(m // tm, n // tn, nk), in_specs=in_specs, out_specs=o_spec,
        out_shape=jax.ShapeDtypeStruct((m, n), out_dtype), scratch_shapes=scratch_shapes,
        compiler_params=pltpu.CompilerParams(dimension_semantics=("parallel", "arbitrary", "arbitrary")),
        name=name,
    )(*args)


def _position():
    return lax.axis_index("x"), lax.axis_index("y"), lax.axis_index("c")


def _flip(pos, k):
    x, y, c = pos
    return (1 - x if k & 4 else x, 1 - y if k & 2 else y, 1 - c if k & 1 else c)


def _index(pos):
    return 4 * pos[0] + 2 * pos[1] + pos[2]


def _all_gather_rows(xs, name):
    n_arr = len(xs)
    chips = (2, 4, 6)

    def body(*refs):
        x_refs, out_refs = refs[:n_arr], refs[n_arr:2 * n_arr]
        send_sems, recv_sems, local_sems = refs[2 * n_arr:]
        me = _position()
        sibling = _flip(me, 1)

        def copy(i, sem, block, to, own=False):
            m_per = x_refs[i].shape[0]
            rows = out_refs[i].at[pl.ds(_index(block) * m_per, m_per), :]
            return pltpu.make_async_remote_copy(
                src_ref=x_refs[i] if own else rows, dst_ref=rows,
                send_sem=send_sems.at[7 * i + sem], recv_sem=recv_sems.at[7 * i + sem], device_id=to, device_id_type=MESH)

        mine = [pltpu.make_async_copy(x_refs[i], out_refs[i].at[pl.ds(_index(me) * x_refs[i].shape[0], x_refs[i].shape[0]), :],
                                      local_sems.at[i]) for i in range(n_arr)]
        first, passed = [], []
        for i in range(n_arr):
            first.append(copy(i, 0, me, sibling, own=True))
            first += [copy(i, 1 + j, me, _flip(me, k), own=True) for j, k in enumerate(chips)]
            passed.append([copy(i, 4 + j, _flip(me, k), sibling) for j, k in enumerate(chips)])
        for cp in mine + first:
            cp.start()
        for i in range(n_arr):
            for j, k in enumerate(chips):
                copy(i, 1 + j, _flip(me, k), me).wait_recv()
                passed[i][j].start()
        for i in range(n_arr):
            copy(i, 0, sibling, me).wait_recv()
            for j, k in enumerate(chips):
                copy(i, 4 + j, _flip(sibling, k), me).wait_recv()
        for cp in first + [cp for group in passed for cp in group]:
            cp.wait_send()
        for cp in mine:
            cp.wait()

    anyspec = pl.BlockSpec(memory_space=pl.ANY)
    return pl.pallas_call(
        body, out_shape=[jax.ShapeDtypeStruct((N_DEV * x.shape[0], x.shape[1]), x.dtype) for x in xs],
        in_specs=[anyspec] * n_arr, out_specs=[anyspec] * n_arr,
        scratch_shapes=[pltpu.SemaphoreType.DMA((7 * n_arr,)), pltpu.SemaphoreType.DMA((7 * n_arr,)),
                        pltpu.SemaphoreType.DMA((n_arr,))],
        name=name,
    )(*xs)


def _split_start(srcs, lands, plan, n_copies, name, after=()):
    n_arr = len(srcs)
    n_after = len(after)
    hbm = pl.BlockSpec(memory_space=pltpu.HBM)
    sem = pl.BlockSpec(memory_space=pltpu.SEMAPHORE)

    def body(*refs):
        src_refs, land_refs = refs[:n_arr], refs[n_arr:2 * n_arr]
        outs_at = 2 * n_arr + n_after
        send_sems, recv_sems = refs[outs_at:outs_at + n_arr], refs[outs_at + n_arr:outs_at + 2 * n_arr]
        token = refs[-1]
        me = _position()
        for i in range(n_arr):
            for j, (src, dst, peer, _) in enumerate(plan(i, src_refs[i], land_refs[i], me)):
                pltpu.make_async_remote_copy(src_ref=src, dst_ref=dst, send_sem=send_sems[i].at[j], recv_sem=recv_sems[i].at[j],
                                             device_id=peer, device_id_type=MESH).start()
        token[...] = jnp.zeros_like(token)

    outs = pl.pallas_call(
        body, name=name,
        out_shape=([pltpu.SemaphoreType.DMA((n_copies,))] * (2 * n_arr)
                   + [pltpu.HBM(a.shape, a.dtype) for a in list(srcs) + list(lands)]
                   + [jax.ShapeDtypeStruct((8, LANES), f32)]),
        in_specs=[hbm] * (2 * n_arr) + [pl.BlockSpec(memory_space=pl.ANY)] * n_after,
        out_specs=[sem] * (2 * n_arr) + [hbm] * (2 * n_arr) + [pl.BlockSpec(memory_space=pltpu.VMEM)],
        input_output_aliases={i: 2 * n_arr + i for i in range(2 * n_arr)},
        compiler_params=pltpu.CompilerParams(has_side_effects=pltpu.SideEffectType.DATAFLOW_SIDE_EFFECTING),
    )(*[pltpu.with_memory_space_constraint(a, pltpu.HBM) for a in list(srcs) + list(lands)], *after)
    return (outs[:n_arr], outs[n_arr:2 * n_arr], outs[2 * n_arr:3 * n_arr], outs[3 * n_arr:4 * n_arr], outs[-1])


def _split_wait(started, plan, after, name):
    send_sems, recv_sems, srcs, lands, _ = started
    n_arr = len(srcs)
    hbm = pl.BlockSpec(memory_space=pltpu.HBM)
    sem = pl.BlockSpec(memory_space=pltpu.SEMAPHORE)

    def body(*refs):
        src_refs, land_refs = refs[:n_arr], refs[n_arr:2 * n_arr]
        s_sems, r_sems = refs[2 * n_arr:3 * n_arr], refs[3 * n_arr:4 * n_arr]
        me = _position()
        for i in range(n_arr):
            for j, (src, _, peer, arrival) in enumerate(plan(i, src_refs[i], land_refs[i], me)):
                cp = pltpu.make_async_remote_copy(src_ref=src, dst_ref=arrival, send_sem=s_sems[i].at[j], recv_sem=r_sems[i].at[j],
                                                  device_id=peer, device_id_type=MESH)
                cp.wait_send()
                cp.wait_recv()

    outs = pl.pallas_call(
        body, name=name,
        out_shape=[pltpu.HBM(a.shape, a.dtype) for a in list(srcs) + list(lands)],
        in_specs=[hbm] * (2 * n_arr) + [sem] * (2 * n_arr) + [pl.BlockSpec(memory_space=pl.ANY)],
        out_specs=[hbm] * (2 * n_arr),
        input_output_aliases={i: i for i in range(2 * n_arr)},
        compiler_params=pltpu.CompilerParams(has_side_effects=pltpu.SideEffectType.DATAFLOW_SIDE_EFFECTING),
    )(*srcs, *lands, *send_sems, *recv_sems, after)
    return outs[:n_arr], outs[n_arr:]


def _landing_zone(blk, me, name):
    m, n = blk.shape

    def body(me_ref, x_ref, o_ref):
        del me_ref
        o_ref[...] = x_ref[...]

    return pl.pallas_call(
        body,
        grid_spec=pltpu.PrefetchScalarGridSpec(
            num_scalar_prefetch=1, grid=(1,),
            in_specs=[pl.BlockSpec((m, n), lambda i, me_ref: (0, 0))],
            out_specs=pl.BlockSpec((m, n), lambda i, me_ref: (me_ref[0], 0))),
        out_shape=jax.ShapeDtypeStruct((N_DEV * m, n), blk.dtype), name=name,
    )(jnp.reshape(me, (1,)).astype(jnp.int32), blk)


_GATHER_FLIPS = (1, 2, 4, 6)


def _gather_plan(i, src_ref, land_ref, me):
    m = src_ref.shape[0]

    def rows(pos):
        return land_ref.at[pl.ds(_index(pos) * m, m), :]

    return [(src_ref, rows(me), _flip(me, k), rows(_flip(me, k))) for k in _GATHER_FLIPS]


def _gather_forward(lands, name):
    n_arr = len(lands)
    chips = (2, 4, 6)

    def body(*refs):
        out_refs = refs[n_arr:2 * n_arr]
        send_sems, recv_sems = refs[2 * n_arr:]
        me = _position()
        sibling = _flip(me, 1)
        sends, arrivals = [], []
        for i, out_ref in enumerate(out_refs):
            m = out_ref.shape[0] // N_DEV

            def copy(pos, j):
                blk = out_ref.at[pl.ds(_index(pos) * m, m), :]
                return pltpu.make_async_remote_copy(src_ref=blk, dst_ref=blk, send_sem=send_sems.at[3 * i + j],
                                                    recv_sem=recv_sems.at[3 * i + j], device_id=sibling, device_id_type=MESH)

            for j, k in enumerate(chips):
                sends.append(copy(_flip(me, k), j))
                arrivals.append(copy(_flip(sibling, k), j))
        for cp in sends:
            cp.start()
        for cp in arrivals:
            cp.wait_recv()
        for cp in sends:
            cp.wait_send()

    anyspec = pl.BlockSpec(memory_space=pl.ANY)
    return pl.pallas_call(
        body, out_shape=[jax.ShapeDtypeStruct(a.shape, a.dtype) for a in lands],
        in_specs=[anyspec] * n_arr, out_specs=[anyspec] * n_arr, input_output_aliases={i: i for i in range(n_arr)},
        scratch_shapes=[pltpu.SemaphoreType.DMA((3 * n_arr,))] * 2, name=name,
    )(*lands)


def _chips_plan(i, src_ref, land_ref, me):
    m = src_ref.shape[0] // 4
    plan = []
    for j, k in enumerate((2, 4, 6)):
        peer = _flip(me, k)
        plan.append((src_ref.at[pl.ds((2 * peer[0] + peer[1]) * m, m), :], land_ref.at[j], peer, land_ref.at[j]))
    return plan


def _sibling_plan(i, src_ref, land_ref, me):
    m = src_ref.shape[0] // N_DEV
    sibling = _flip(me, 1)
    return [(src_ref.at[pl.ds((2 * q + 1 - me[2]) * m, m), :], land_ref.at[q], sibling, land_ref.at[q]) for q in range(4)]


def _sum_with_sibling(g, recv, name):
    m = g.shape[0] // N_DEV
    n = g.shape[1]
    tr = _pick(m, (208, 128, 64, 32, 16))
    nt = m // tr

    def body(g_ref, r_ref, o_ref):
        c = lax.axis_index("c")
        own = jnp.where(c == 0, g_ref[0, 0].astype(f32), g_ref[0, 1].astype(f32))
        o_ref[...] = (own + r_ref[0].astype(f32)).astype(o_ref.dtype)

    return pl.pallas_call(
        body, grid=(4, nt),
        in_specs=[pl.BlockSpec((1, 2, tr, n), lambda q, i: (q, 0, i, 0)), pl.BlockSpec((1, tr, n), lambda q, i: (q, i, 0))],
        out_specs=pl.BlockSpec((tr, n), lambda q, i: (q * nt + i, 0)),
        out_shape=jax.ShapeDtypeStruct((4 * m, n), bf16), name=name,
    )(g.reshape(4, 2, m, n), recv)


def _sum_with_chips(h, recv, name, slot=0, n_slots=1, into=None):
    m = h.shape[0] // 4
    n = h.shape[1]
    tr = _pick(m, (208, 128, 64, 32, 16))

    def body(h_ref, r_ref, *rest):
        o_ref = rest[-1]
        my_q = 2 * lax.axis_index("x") + lax.axis_index("y")
        own = h_ref[0].astype(f32)
        for q in range(1, 4):
            own = jnp.where(my_q == q, h_ref[q].astype(f32), own)
        o_ref[0] = ((own + r_ref[0].astype(f32)) + r_ref[1].astype(f32)) + r_ref[2].astype(f32)

    in_specs = [pl.BlockSpec((4, tr, n), lambda i: (0, i, 0)), pl.BlockSpec((3, tr, n), lambda i: (0, i, 0))]
    args = [h.reshape(4, m, n), recv]
    if into is not None:
        in_specs.append(pl.BlockSpec(memory_space=pl.ANY))
        args.append(into)
    return pl.pallas_call(
        body, grid=(m // tr,), in_specs=in_specs,
        out_specs=pl.BlockSpec((1, tr, n), lambda i: (slot, i, 0)), out_shape=jax.ShapeDtypeStruct((n_slots, m, n), f32),
        input_output_aliases={2: 0} if into is not None else {}, name=name,
    )(*args)


def _sum_slots(parts, name):
    n_slot, m, n = parts.shape
    tr = _pick(m, (208, 128, 64, 32, 16, 8))

    def body(p_ref, o_ref):
        acc = p_ref[0]
        for s in range(1, n_slot):
            acc = acc + p_ref[s]
        o_ref[...] = acc

    return pl.pallas_call(
        body, grid=(m // tr,), in_specs=[pl.BlockSpec((n_slot, tr, n), lambda i: (0, i, 0))],
        out_specs=pl.BlockSpec((tr, n), lambda i: (i, 0)), out_shape=jax.ShapeDtypeStruct((m, n), parts.dtype), name=name,
    )(parts)


def _reduce_scatter_begin(gs, name):
    lands = [lax.empty((4, g.shape[0] // N_DEV, g.shape[1]), g.dtype) for g in gs]
    return _split_start(gs, lands, _sibling_plan, 4, "rs_d2d_start_" + name)


def _reduce_scatter_middle(started, after, name):
    gs, from_sibling = _split_wait(started, _sibling_plan, after, "rs_d2d_wait_" + name)
    chip_sums = [_sum_with_sibling(g, r, f"rs_sum2_{name}_{i}") for i, (g, r) in enumerate(zip(gs, from_sibling))]
    lands = [lax.empty((3, h.shape[0] // 4, h.shape[1]), h.dtype) for h in chip_sums]
    return _split_start(chip_sums, lands, _chips_plan, 3, "rs_ici_start_" + name)


def _reduce_scatter_end(started, after, name, first_into=None, slot=0, n_slots=1):
    chip_sums, from_chips = _split_wait(started, _chips_plan, after, "rs_ici_wait_" + name)
    out = []
    for i, (h, r) in enumerate(zip(chip_sums, from_chips)):
        if i == 0:
            out.append(_sum_with_chips(h, r, f"rs_sum4_{name}_{i}", slot, n_slots, first_into))
        else:
            out.append(_sum_with_chips(h, r, f"rs_sum4_{name}_{i}")[0])
    return out


def _adamw_update(w, g, m, v):
    mm = ADAM_B1 * m + (1.0 - ADAM_B1) * g
    vv = ADAM_B2 * v + (1.0 - ADAM_B2) * jnp.square(g)
    m_hat = mm / (1.0 - ADAM_B1 ** ADAM_STEP)
    v_hat = vv / (1.0 - ADAM_B2 ** ADAM_STEP)
    return -ADAM_LR * (m_hat / (jnp.sqrt(v_hat) + ADAM_EPS) + ADAM_WD * w), mm, vv


def _adamw_many(ws, gs, ms, vs, name):
    k = len(ws)
    shapes = [w.shape for w in ws]
    flat = [[a.reshape(-1, a.shape[-1]) for a in group] for group in (ws, gs, ms, vs)]

    def body(*refs):
        for i in range(k):
            d, mm, vv = _adamw_update(*(refs[j * k + i][...] for j in range(4)))
            refs[4 * k + i][...] = d
            refs[5 * k + i][...] = mm
            refs[6 * k + i][...] = vv

    outs = pl.pallas_call(
        body, out_shape=[jax.ShapeDtypeStruct(a.shape, f32) for a in flat[0]] * 3, name=name,
    )(*flat[0], *flat[1], *flat[2], *flat[3])
    return tuple([outs[j * k + i].reshape(shapes[i]) for i in range(k)] for j in range(3))


def _adamw(w, g, m, v, name):
    shape = w.shape
    n = shape[-1]
    r = w.size // n
    w2, g2, m2, v2 = (a.reshape(r, n) for a in (w, g, m, v))
    tr = _pick(r, (256, 208, 128, 64, 32, 16, 8))

    def body(w_ref, g_ref, m_ref, v_ref, d_ref, mo_ref, vo_ref):
        d_ref[...], mo_ref[...], vo_ref[...] = _adamw_update(w_ref[...], g_ref[...], m_ref[...], v_ref[...])

    spec = pl.BlockSpec((tr, n), lambda i: (i, 0))
    outs = pl.pallas_call(
        body, grid=(r // tr,), in_specs=[spec] * 4, out_specs=[spec] * 3,
        out_shape=[jax.ShapeDtypeStruct((r, n), f32)] * 3, name=name,
    )(w2, g2, m2, v2)
    return tuple(o.reshape(shape) for o in outs)


_SMALL = ("shift_mu", "w_decay0", "a0", "k_k", "k_a", "r_k", "ln_x_w", "ln_x_b", "v_mix0", "lb_logits",
          "g_norm_w", "ln_w", "ln_b")
_NAMES = ("w_in", "shift_mu", "w_decay0", "w_decay_up", "a0", "a_up", "k_k", "k_a", "r_k", "ln_x_w", "ln_x_b",
          "v_mix0", "v_mix_down", "v_mix_up", "lb_logits", "g_norm_w", "w_out", "ln_w", "ln_b")


def _pad_rows(a, rows, at_end):
    z = jnp.zeros((rows - a.shape[0], a.shape[1]), a.dtype)
    return jnp.concatenate([a, z] if at_end else [z, a], axis=0)


def kernel(x, w_in, shift_mu, w_decay0, w_decay_up, a0, a_up, k_k, k_a, r_k, ln_x_w, ln_x_b, v_mix0, v_mix_down, v_mix_up, lb_logits, g_norm_w, w_out, ln_w, ln_b, loss_target, m_w_in, m_shift_mu, m_w_decay0, m_w_decay_up, m_a0, m_a_up, m_k_k, m_k_a, m_r_k, m_ln_x_w, m_ln_x_b, m_v_mix0, m_v_mix_down, m_v_mix_up, m_lb_logits, m_g_norm_w, m_w_out, m_ln_w, m_ln_b, v_w_in, v_shift_mu, v_w_decay0, v_w_decay_up, v_a0, v_a_up, v_k_k, v_k_a, v_r_k, v_ln_x_w, v_ln_x_b, v_v_mix0, v_v_mix_down, v_v_mix_up, v_lb_logits, v_g_norm_w, v_w_out, v_ln_w, v_ln_b):
    weights = dict(w_in=w_in, shift_mu=shift_mu, w_decay0=w_decay0, w_decay_up=w_decay_up, a0=a0, a_up=a_up, k_k=k_k,
                   k_a=k_a, r_k=r_k, ln_x_w=ln_x_w, ln_x_b=ln_x_b, v_mix0=v_mix0, v_mix_down=v_mix_down,
                   v_mix_up=v_mix_up, lb_logits=lb_logits, g_norm_w=g_norm_w, w_out=w_out, ln_w=ln_w, ln_b=ln_b)
    mom1 = dict(w_in=m_w_in, shift_mu=m_shift_mu, w_decay0=m_w_decay0, w_decay_up=m_w_decay_up, a0=m_a0, a_up=m_a_up,
                k_k=m_k_k, k_a=m_k_a, r_k=m_r_k, ln_x_w=m_ln_x_w, ln_x_b=m_ln_x_b, v_mix0=m_v_mix0,
                v_mix_down=m_v_mix_down, v_mix_up=m_v_mix_up, lb_logits=m_lb_logits, g_norm_w=m_g_norm_w,
                w_out=m_w_out, ln_w=m_ln_w, ln_b=m_ln_b)
    mom2 = dict(w_in=v_w_in, shift_mu=v_shift_mu, w_decay0=v_w_decay0, w_decay_up=v_w_decay_up, a0=v_a0, a_up=v_a_up,
                k_k=v_k_k, k_a=v_k_a, r_k=v_r_k, ln_x_w=v_ln_x_w, ln_x_b=v_ln_x_b, v_mix0=v_v_mix0,
                v_mix_down=v_v_mix_down, v_mix_up=v_v_mix_up, lb_logits=v_lb_logits, g_norm_w=v_g_norm_w,
                w_out=v_w_out, ln_w=v_ln_w, ln_b=v_ln_b)
    assert x.shape[0] == 1 and w_in.shape[0] == DEPTH
    t, d = x.shape[1], x.shape[2]
    dr = w_decay0.shape[1]
    dh = g_norm_w.shape[1]
    rank_w, rank_a, rank_v = w_decay_up.shape[1], a_up.shape[1], v_mix_up.shape[1]
    rwc = 4 * dr + rank_w + rank_a
    assert rank_w + rank_a == LANES and rank_v <= LANES and dr + dh == d
    assert t % CHUNK == 0 and dr % LANES == 0 and dh % LANES == 0 and shift_mu.shape[1] == rwc
    n_pair = dr // LANES
    me = _index(_position())

    shard = dr // N_DEV
    pack = jnp.concatenate([w_decay_up[0], w_decay_up[1], a_up[0], a_up[1], v_mix_up[0], v_mix_down[0].T], axis=0)
    win_t0, pack = _all_gather_rows([w_in[0].T.astype(bf16), pack], "ag_first")
    win_t = [win_t0, None]
    wout = [None, None]

    def start_gather(blocks, name, after):
        lands = [_landing_zone(blk, me, f"{name}_zone{i}") for i, blk in enumerate(blocks)]
        return _split_start(blocks, lands, _gather_plan, len(_GATHER_FLIPS), name, after=after)

    gather_wout0 = start_gather([w_out[0].astype(bf16)], "ag_wout0_start", (win_t[0], pack))
    gather_layer1 = start_gather([w_in[1].T.astype(bf16), w_out[1].astype(bf16)], "ag_layer1_start", (gather_wout0[-1],))
    pack = jnp.transpose(pack.reshape(N_DEV, -1, shard), (1, 0, 2)).reshape(-1, dr)
    offs = [0, rank_w, 2 * rank_w, 2 * rank_w + rank_a, 2 * rank_w + 2 * rank_a, 2 * rank_w + 2 * rank_a + rank_v,
            2 * rank_w + 2 * rank_a + 2 * rank_v]
    wdu_f = [pack[offs[0]:offs[1]], pack[offs[1]:offs[2]]]
    aup_f = [pack[offs[2]:offs[3]], pack[offs[3]:offs[4]]]
    vup_f = pack[offs[4]:offs[5]]
    vdown_f = pack[offs[5]:offs[6]].T

    def after_start(a, started):
        return a + started[-1][0:1, 0:1]

    def rwkv_params(l):
        mu = after_start(shift_mu[0:1], gather_layer1) if l == 0 else shift_mu[l:l + 1]
        prm = [mu, w_decay0[l:l + 1], a0[l:l + 1], _pad_rows(wdu_f[l], LANES, True),
               _pad_rows(aup_f[l], LANES, False)]
        if l == 1:
            prm += [v_mix0[0:1], _pad_rows(vdown_f.T, LANES, True).T, _pad_rows(vup_f, LANES, True)]
        rows = jnp.stack([k_k[l], k_a[l], r_k[l], ln_x_w[l], ln_x_b[l]] + [jnp.zeros((dr,), f32)] * 3, axis=0)
        pp = jnp.transpose(rows.reshape(8, n_pair, LANES), (1, 0, 2))
        return tuple(prm), pp

    h = x[0]
    h16 = h.astype(bf16)
    tgt = loss_target[0]
    saved = []
    vfirst = None
    for l in range(DEPTH):
        prm, pp = rwkv_params(l)
        proj = _matmul(h16, win_t[l], "nt", f"mm_proj_{l}", _MM_TILES["proj"])
        if l == 0:
            cat, vfirst, mck = _rwkv_fwd(False, proj, None, prm, pp, d)
        else:
            cat, mck = _rwkv_fwd(True, proj, vfirst, prm, pp, d)
        cat, sck = _hgrn_fwd(l == 1, proj, lb_logits, g_norm_w[l:l + 1], cat, rwc)
        if l == 0:
            _, arrived = _split_wait(gather_wout0, _gather_plan, cat, "ag_wout0_wait")
            (wout[0],) = _gather_forward(arrived, "ag_wout0_forward")
        y = _matmul(cat, wout[l], "nn", f"mm_out_{l}", _MM_TILES["out"])
        saved.append((h, h16, proj, prm, pp, mck, sck, cat, y))
        if l < DEPTH - 1:
            h, h16 = _ln_fwd(h, y, ln_w[l:l + 1], ln_b[l:l + 1])
            _, arrived = _split_wait(gather_layer1, _gather_plan, h16, "ag_layer1_wait")
            win_t[1], wout[1] = _gather_forward(arrived, "ag_layer1_forward")
        else:
            top = _ln_loss_bwd(h, y, ln_w[l:l + 1], ln_b[l:l + 1], tgt)
    loss = lax.psum(top[4][0, 0], ("x", "y", "c"))

    grads = {}
    big = {}
    dvfirst = None
    d_lbl = None
    rs_started = {}
    for l in reversed(range(DEPTH)):
        h_l, h16_l, proj, prm, pp, mck, sck, cat, y = saved[l]
        if l == DEPTH - 1:
            dy, dy16, g_ln_w, g_ln_b = top[:4]
        else:
            dy, dy16, g_ln_w, g_ln_b = _ln_bwd(h_l, y, after_start(ln_w[l:l + 1], rs_started[l + 1]), ln_b[l:l + 1], dh_out)
        dcat = _matmul(dy16, wout[l], "nt", f"mm_dcat_{l}", _MM_TILES["dcat"])
        big[("w_out", l)] = _matmul(cat, dy16, "tn", f"mm_dwout_{l}", _MM_TILES["dwout"], out_dtype=bf16)
        if l == 0:
            d2d_wout0 = _reduce_scatter_begin([big[("w_out", 0)]], "l0w")
        if l == 1:
            outs = _rwkv_bwd(True, proj, vfirst, prm, pp, mck, dcat, None)
            dproj_r, dvfirst = outs[0], outs[1]
            dprm, dpp = outs[2:-1], outs[-1]
        else:
            outs = _rwkv_bwd(False, proj, None, prm, pp, mck, dcat, dvfirst)
            dproj_r = outs[0]
            dprm, dpp = outs[1:-1], outs[-1]
            ici_wout0 = _reduce_scatter_middle(d2d_wout0, outs[-1], "l0w")
        dproj, dlbl_l, dgnw = _hgrn_bwd(l == 1, proj, lb_logits, g_norm_w[l:l + 1], sck, dcat, rwc, dproj_r)
        big[("w_in", l)] = _matmul(dproj, h16_l, "tn", f"mm_dwin_{l}", _MM_TILES["dwin"], out_dtype=bf16)
        sharded = [dprm[3][:rank_w].T, dprm[4][rank_w:].T]
        if l == 1:
            sharded += [dprm[6][:, :rank_v], dprm[7][:rank_v].T,
                        jnp.zeros((dr, LANES - 2 * rank_v), f32)]
        sharded = jnp.concatenate(sharded, axis=1).astype(bf16)
        d2d = _reduce_scatter_begin([big[("w_in", l)], sharded] + ([big[("w_out", l)]] if l > 0 else []), f"l{l}")
        if l == 0:
            rs_started[l] = _reduce_scatter_middle(d2d, sharded, f"l{l}")
            token = rs_started[l][-1]
        else:
            token = d2d[-1]
        dh_out = _matmul(dproj, win_t[l], "nn", f"mm_dh_{l}", _MM_TILES["dh"], add=dy, add_scale=ALPHA, after=token)
        if l > 0:
            rs_started[l] = _reduce_scatter_middle(d2d, dh_out, f"l{l}")
        dpp = jnp.transpose(dpp, (1, 0, 2)).reshape(8, dr)
        grads[l] = dict(shift_mu=dprm[0][0], w_decay0=dprm[1][0], a0=dprm[2][0],
                        k_k=dpp[0], k_a=dpp[1], r_k=dpp[2], ln_x_w=dpp[3], ln_x_b=dpp[4],
                        g_norm_w=dgnw[0], ln_w=g_ln_w[0], ln_b=g_ln_b[0])
        if l == 1:
            grads[l].update(v_mix0=dprm[5][0])
            d_lbl = dlbl_l
    grad_x = dh_out[None]

    def both(name):
        return jnp.stack([grads[0][name], grads[1][name]])

    small = dict(shift_mu=both("shift_mu"), w_decay0=both("w_decay0"), a0=both("a0"), k_k=both("k_k"), k_a=both("k_a"),
                 r_k=both("r_k"), ln_x_w=both("ln_x_w"), ln_x_b=both("ln_x_b"), v_mix0=grads[1]["v_mix0"][None],
                 lb_logits=d_lbl, g_norm_w=both("g_norm_w"), ln_w=both("ln_w"), ln_b=both("ln_b"))
    flat = jnp.concatenate([small[nm].reshape(-1) for nm in _SMALL])
    n_flat = flat.shape[0]
    rows = -(-n_flat // (8 * LANES)) * 8
    flat = jnp.concatenate([flat, jnp.zeros((rows * LANES - n_flat,), f32)]).reshape(rows, LANES)
    total = _sum_slots(_all_gather_rows([flat], "ag_small_grads")[0].reshape(N_DEV, rows, LANES), "sum_small_grads").reshape(-1)
    gsm = {}
    off = 0
    for nm in _SMALL:
        size = small[nm].size
        gsm[nm] = total[off:off + size].reshape(small[nm].shape)
        off += size
    reduced = {1: _reduce_scatter_end(rs_started[1], dh_out, "l1", None, 1, DEPTH)}
    g_w_out0 = _reduce_scatter_end(ici_wout0, dh_out, "l0w")[0][0]
    reduced[0] = _reduce_scatter_end(rs_started[0], total, "l0", reduced[1][0], 0, DEPTH)
    g_w_in_t = reduced[0][0]
    gsm["w_in"] = jnp.transpose(g_w_in_t, (0, 2, 1))
    gsm["w_out"] = jnp.stack([g_w_out0, reduced[1][2]])
    gsm["w_decay_up"] = jnp.stack([reduced[l][1][:, :rank_w].T for l in range(DEPTH)])
    gsm["a_up"] = jnp.stack([reduced[l][1][:, rank_w:rank_w + rank_a].T for l in range(DEPTH)])
    gsm["v_mix_down"] = reduced[1][1][:, LANES:LANES + rank_v][None]
    gsm["v_mix_up"] = reduced[1][1][:, LANES + rank_v:LANES + 2 * rank_v].T[None]

    deltas, new_m, new_v = {}, {}, {}
    swap = lambda a: jnp.transpose(a, (0, 2, 1))
    deltas["w_in"], new_m["w_in"], new_v["w_in"] = (
        swap(a) for a in _adamw(swap(w_in), g_w_in_t, swap(m_w_in), swap(v_w_in), "adamw_w_in"))
    deltas["w_out"], new_m["w_out"], new_v["w_out"] = _adamw(w_out, gsm["w_out"], m_w_out, v_w_out, "adamw_w_out")
    rest = [nm for nm in _NAMES if nm not in ("w_in", "w_out")]
    d_rest, m_rest, v_rest = _adamw_many([weights[nm] for nm in rest], [gsm[nm] for nm in rest],
                                         [mom1[nm] for nm in rest], [mom2[nm] for nm in rest], "adamw_small")
    for i, nm in enumerate(rest):
        deltas[nm], new_m[nm], new_v[nm] = d_rest[i], m_rest[i], v_rest[i]
    return (loss, grad_x, *[gsm[nm] for nm in _NAMES], *[deltas[nm] for nm in _NAMES],
            *[new_m[nm] for nm in _NAMES], *[new_v[nm] for nm in _NAMES])
```

```python
import functools

import jax
import jax.numpy as jnp
from jax import lax
from jax.experimental import pallas as pl
from jax.experimental.pallas import tpu as pltpu

f32 = jnp.float32
bf16 = jnp.bfloat16

N_DEV = 8
CHUNK = 64
LANES = 128
RWKV_HEAD = 64
DEPTH = 2
ALPHA = (2 * DEPTH) ** 0.25
LN_EPS = 1e-5
GN_EPS = 64e-5
RMS_EPS = 1e-5
LB_FLOOR = 1e-30
ADAM_LR, ADAM_B1, ADAM_B2, ADAM_EPS, ADAM_WD, ADAM_STEP = 0.001, 0.9, 0.999, 1e-08, 0.01, 10
MESH = pl.DeviceIdType.MESH


def _iota(shape, d):
    return lax.broadcasted_iota(jnp.int32, shape, d)


_DIMS = {"nn": (((1,), (0,)), ((), ())), "nt": (((1,), (1,)), ((), ())), "tn": (((0,), (0,)), ((), ()))}
_BATCH_DIMS = {"nn": (((2,), (1,)), ((0,), (0,))), "nt": (((2,), (2,)), ((0,), (0,))), "tn": (((1,), (1,)), ((0,), (0,)))}
_K_AXES = {"nn": (-1, -2), "nt": (-1, -1), "tn": (-2, -2)}


def _mxu(a, b, mode):
    return lax.dot_general(a, b, (_BATCH_DIMS if a.ndim == 3 else _DIMS)[mode], preferred_element_type=f32)


def _split(x):
    hi = x.astype(bf16)
    return hi, (x - hi.astype(f32)).astype(bf16)


def _mm2_impl(a, b, mode, passes=3):
    if passes == 1:
        return _mxu(a.astype(bf16), b.astype(bf16), mode)
    ah, al = _split(a)
    if passes == 3:
        bh, bl = _split(b)
        lhs, rhs = [ah, ah, al], [bh, bl, bh]
    else:
        bh = b.astype(bf16)
        lhs, rhs = [ah, al], [bh, bh]
    ka, kb = _K_AXES[mode]
    k = a.shape[ka]
    if k % (LANES if -1 in (ka, kb) else 16) == 0:
        return _mxu(jnp.concatenate(lhs, axis=ka), jnp.concatenate(rhs, axis=kb), mode)
    out = _mxu(lhs[0], rhs[0], mode)
    for x, y in zip(lhs[1:], rhs[1:]):
        out = out + _mxu(x, y, mode)
    return out


@functools.partial(jax.custom_vjp, nondiff_argnums=(2, 3))
def _mm2(a, b, mode, passes=3):
    return _mm2_impl(a, b, mode, passes)


def _mm2_fwd(a, b, mode, passes):
    return _mm2_impl(a, b, mode, passes), (a, b)


def _mm2_bwd(mode, passes, res, g):
    a, b = res
    if mode == "nn":
        return _mm2_impl(g, b, "nt", passes), _mm2_impl(a, g, "tn", passes)
    if mode == "nt":
        return _mm2_impl(g, b, "nn", passes), _mm2_impl(g, a, "tn", passes)
    return _mm2_impl(b, g, "nt", passes), _mm2_impl(a, g, "nn", passes)


_mm2.defvjp(_mm2_fwd, _mm2_bwd)

TRI_PASSES = 1
APPLY_PASSES = 1


def _const_impl(cm, x, mode):
    if mode in ("r", "rt"):
        shape = x.shape
        out = _mxu(x.astype(bf16).reshape(-1, shape[-1]), cm, "nn" if mode == "r" else "nt")
        return out.reshape(shape[:-1] + (out.shape[-1],))
    hi, lo = _split(x)
    if x.ndim == 3:
        cm = jnp.broadcast_to(cm, (x.shape[0],) + cm.shape)
    return _mxu(cm, hi, mode) + _mxu(cm, lo, mode)


@jax.custom_vjp
def _const_left(cm, x):
    return _const_impl(cm, x, "nn")


_const_left.defvjp(lambda cm, x: (_const_impl(cm, x, "nn"), cm),
                   lambda cm, g: (jnp.zeros_like(cm), _const_impl(cm, g, "tn")))


@jax.custom_vjp
def _const_right(x, cm):
    return _const_impl(cm, x, "r")


_const_right.defvjp(lambda x, cm: (_const_impl(cm, x, "r"), cm),
                    lambda cm, g: (_const_impl(cm, g, "rt"), jnp.zeros_like(cm)))


def _tri_inv(a):
    n = a.shape[-1]
    tm = (_iota((n, n), 0) == _iota((n, n), 1)).astype(f32) + a
    ak = a
    for _ in range(5):
        ak = _mm2_impl(ak, ak, "nn", TRI_PASSES)
        tm = tm + _mm2_impl(tm, ak, "nn", TRI_PASSES)
    return tm


@jax.custom_vjp
def _tri_solve(tm, a, x):
    del a
    return _mm2_impl(tm, x, "nn", APPLY_PASSES)


def _tri_solve_fwd(tm, a, x):
    u = _mm2_impl(tm, x, "nn", APPLY_PASSES)
    return u, (tm, u)


def _tri_solve_bwd(res, du):
    tm, u = res
    dx = _mm2_impl(tm, du, "tn", APPLY_PASSES)
    return jnp.zeros_like(tm), _mm2_impl(dx, u, "nt", APPLY_PASSES), dx


_tri_solve.defvjp(_tri_solve_fwd, _tri_solve_bwd)


def _col_of_row(row_vec):
    n = row_vec.shape[-1]
    eye = _iota((n, n), 0) == _iota((n, n), 1)
    return jnp.sum(jnp.where(eye, jnp.broadcast_to(row_vec, row_vec.shape[:-2] + (n, n)), 0.0), axis=-1, keepdims=True)


def _softplus(x):
    return jnp.maximum(x, 0.0) + jnp.log1p(jnp.exp(-jnp.abs(x)))


def _log_sigmoid(x):
    return -_softplus(-x)


def _logaddexp(a, b):
    return jnp.maximum(a, b) + jnp.log1p(jnp.exp(-jnp.abs(a - b)))


def _silu(x):
    return x * jax.nn.sigmoid(x)


def _tril(c, strict):
    r, s = _iota((c, c), 0), _iota((c, c), 1)
    return (r > s) if strict else (r >= s)


def _last_row(a):
    c = a.shape[-2]
    return jnp.sum(jnp.where(_iota(a.shape, a.ndim - 2) == c - 1, a, 0.0), axis=-2, keepdims=True)


def _rwkv_pre(layer1, prm, y, prev, vf):
    c = y.shape[0]
    if layer1:
        mu, w0, a0, wup, aup, v0, vdown, vup = prm
    else:
        mu, w0, a0, wup, aup = prm
    dr = w0.shape[1]
    shift = (_iota((c, c), 0) == _iota((c, c), 1) + 1).astype(bf16)
    y_prev = _const_left(shift, y) + jnp.where(_iota((c, 1), 0) == 0, prev, 0.0)
    rw = y + mu * (y_prev - y)
    r, k, v, z = (rw[:, i * dr:(i + 1) * dr] for i in range(4))
    wdad = rw[:, 4 * dr:4 * dr + LANES]
    w_raw = w0 + _mm2(jnp.tanh(wdad), wup, "nn")
    lw = -jnp.exp(-_softplus(-w_raw) - 0.5)
    asig = jax.nn.sigmoid(a0 + _mm2(wdad, aup, "nn"))
    if layer1:
        v = v + (vf - v) * jax.nn.sigmoid(v0 + _mm2(_mm2(v, vdown, "nn"), vup, "nn"))
    return r, k, v, z, lw, asig


def _rwkv_pair(pp, m0, xs, tm=None):
    kkw, kaw, rkw, gnw, gnb = pp
    r, k, v, z, lw, asig = xs
    c = r.shape[-2]
    n2 = 2 * c
    lane = _iota((1, LANES), 1)
    mh0, mh1 = (lane < RWKV_HEAD).astype(f32), (lane >= RWKV_HEAD).astype(f32)
    same_head = _iota((LANES, LANES), 0) // RWKV_HEAD == _iota((LANES, LANES), 1) // RWKV_HEAD
    g = same_head.astype(bf16)

    def seg(x):
        return _const_right(x, g)

    def stack(x):
        return jnp.concatenate([x * mh0, x * mh1], axis=-2)

    kk = k * kkw
    kk = kk / jnp.maximum(jnp.sqrt(seg(kk * kk)), 1e-12)
    k2 = k * (1.0 + (asig - 1.0) * kaw)
    a = -kk
    b = kk * asig
    cum = _const_left(_tril(c, False).astype(bf16), lw)
    at = stack(a * jnp.exp(cum - lw))
    rt = stack(r * jnp.exp(cum))
    en = jnp.exp(-cum)
    sc = _mm2(jnp.concatenate([at, rt], axis=-2), jnp.concatenate([stack(b * en), stack(k2 * en)], axis=-2), "nt")
    row, col = _iota((n2, n2), 0), _iota((n2, n2), 1)
    same = row // c == col // c
    strict = same & (row % c > col % c)
    incl = same & (row % c >= col % c)
    aab = jnp.where(strict, sc[..., :n2, :n2], 0.0)
    aak = jnp.where(strict, sc[..., :n2, n2:], 0.0)
    arb = jnp.where(incl, sc[..., n2:, :n2], 0.0)
    ark = jnp.where(incl, sc[..., n2:, n2:], 0.0)
    vv = jnp.concatenate([v, v], axis=-2)
    mask_st = jnp.concatenate([jnp.broadcast_to(mh0, (c, LANES)), jnp.broadcast_to(mh1, (c, LANES))], axis=0)
    x_st = _mm2(jnp.concatenate([at, aak], axis=-1), jnp.concatenate([m0, vv], axis=-2), "nn", APPLY_PASSES)
    if tm is None:
        tm = _tri_inv(lax.stop_gradient(aab))
    u_st = _tri_solve(tm, aab, x_st) * mask_st
    o_st = _mm2(jnp.concatenate([rt, arb, ark], axis=-1), jnp.concatenate([m0, u_st, vv], axis=-2), "nn", APPLY_PASSES) * mask_st
    u = u_st[..., :c, :] + u_st[..., c:, :]
    o = o_st[..., :c, :] + o_st[..., c:, :]
    cum_last = _last_row(cum)
    dec_end = jnp.exp(cum_last - cum)
    m_new = _col_of_row(jnp.exp(cum_last)) * m0 + _mm2(
        jnp.concatenate([b * dec_end, k2 * dec_end], axis=-2), jnp.concatenate([u, v], axis=-2), "tn", APPLY_PASSES) * same_head.astype(f32)
    mean = seg(o) * (1.0 / RWKV_HEAD)
    d = o - mean
    var = seg(d * d) * (1.0 / RWKV_HEAD)
    on = d * lax.rsqrt(var + GN_EPS) * gnw + gnb
    bonus = seg(r * k2 * rkw) * v
    return (on + bonus) * _silu(z), m_new, tm


def _split_lanes(a, n):
    return [a[:, i * LANES:(i + 1) * LANES] for i in range(n)]


def _rwkv_step(layer1, prm, y, prev, vf, pp, m0, tm=None):
    xs = _rwkv_pre(layer1, prm, y, prev, vf)
    n_pair = m0.shape[0]
    og, m_new, tm = _rwkv_pair(pp, m0, tuple(jnp.concatenate([p[None] for p in _split_lanes(a, n_pair)], axis=0) for a in xs), tm)
    return og, m_new, xs[2], tm


def _group(n):
    return n


def _rwkv_specs(layer1, t, dr, rwc, n_pair, rev):
    nc = t // CHUNK
    grp = _group(n_pair)

    def cidx(c):
        return (nc - 1 - c) if rev else c

    full = lambda shape: pl.BlockSpec(shape, lambda c, p: tuple(0 for _ in shape))
    specs = [
        pl.BlockSpec((CHUNK, rwc), lambda c, p: (cidx(c), 0)),
        pl.BlockSpec((8, rwc), lambda c, p: (jnp.maximum(cidx(c) * (CHUNK // 8) - 1, 0), 0)),
    ]
    if layer1:
        specs.append(pl.BlockSpec((CHUNK, dr), lambda c, p: (cidx(c), 0)))
    prm_shapes = [(1, rwc), (1, dr), (1, dr), (LANES, dr), (LANES, dr)]
    if layer1:
        prm_shapes += [(1, dr), (dr, LANES), (LANES, dr)]
    specs += [full(s) for s in prm_shapes]
    specs.append(pl.BlockSpec((grp, 8, LANES), lambda c, p: (p, 0, 0)))
    return specs, prm_shapes, cidx, full


def _rwkv_fwd(layer1, proj, vf, prm, pp, cat_width):
    t = proj.shape[0]
    dr = prm[1].shape[1]
    rwc = prm[0].shape[1]
    n_pair = dr // LANES
    nc = t // CHUNK
    n_prm = len(prm)
    specs, _, _, _ = _rwkv_specs(layer1, t, dr, rwc, n_pair, False)

    def body(*refs):
        y_ref, prev_ref = refs[0], refs[1]
        i = 2
        vf_ref = None
        if layer1:
            vf_ref = refs[i]
            i += 1
        prm_refs = refs[i:i + n_prm]
        i += n_prm
        pp_ref = refs[i]
        i += 1
        cat_ref = refs[i]
        i += 1
        vout_ref = None
        if not layer1:
            vout_ref = refs[i]
            i += 1
        mck_ref, m_s = refs[i], refs[i + 1]
        c = pl.program_id(0)

        @pl.when(c == 0)
        def _():
            m_s[...] = jnp.zeros_like(m_s)

        prev = prev_ref[pl.ds(7, 1), :] * (c != 0).astype(f32)
        m0 = m_s[...]
        ppv = tuple(pp_ref[:, pl.ds(q, 1), :] for q in range(5))
        og, m_new, v, tm = _rwkv_step(layer1, tuple(r[...] for r in prm_refs), y_ref[...], prev,
                                      vf_ref[...] if layer1 else None, ppv, m0)
        mck_ref[0, :n_pair] = m0
        mck_ref[0, n_pair:] = tm
        if not layer1:
            vout_ref[...] = v
        for j in range(n_pair):
            cat_ref[:, j * LANES:(j + 1) * LANES] = og[j]
        m_s[...] = m_new

    grp = _group(n_pair)
    assert grp == n_pair
    out_shape = [jax.ShapeDtypeStruct((t, cat_width), f32)]
    out_specs = [pl.BlockSpec((CHUNK, grp * LANES), lambda c, p: (c, p))]
    if not layer1:
        out_shape.append(jax.ShapeDtypeStruct((t, dr), f32))
        out_specs.append(pl.BlockSpec((CHUNK, dr), lambda c, p: (c, 0)))
    out_shape.append(jax.ShapeDtypeStruct((nc, 2 * n_pair, LANES, LANES), f32))
    out_specs.append(pl.BlockSpec((1, 2 * grp, LANES, LANES), lambda c, p: (c, p, 0, 0)))
    args = [proj, proj] + ([vf] if layer1 else []) + list(prm) + [pp]
    return pl.pallas_call(
        body, grid=(nc, 1), in_specs=specs, out_specs=out_specs, out_shape=out_shape,
        scratch_shapes=[pltpu.VMEM((n_pair, LANES, LANES), f32)],
        compiler_params=pltpu.CompilerParams(dimension_semantics=("arbitrary", "arbitrary")),
        name=f"rwkv_fwd_l{int(layer1)}",
    )(*args)


def _rwkv_bwd(layer1, proj, vf, prm, pp, mck, dcat, dvout):
    t = proj.shape[0]
    dr = prm[1].shape[1]
    rwc = prm[0].shape[1]
    n_pair = dr // LANES
    nc = t // CHUNK
    n_prm = len(prm)
    specs, prm_shapes, cidx, full = _rwkv_specs(layer1, t, dr, rwc, n_pair, True)
    grp = _group(n_pair)
    assert grp == n_pair
    specs.append(pl.BlockSpec((1, 2 * grp, LANES, LANES), lambda c, p: (cidx(c), p, 0, 0)))
    specs.append(pl.BlockSpec((CHUNK, grp * LANES), lambda c, p: (cidx(c), p)))
    if not layer1:
        specs.append(pl.BlockSpec((CHUNK, dr), lambda c, p: (cidx(c), 0)))

    def body(*refs):
        y_ref, prev_ref = refs[0], refs[1]
        i = 2
        vf_ref = None
        if layer1:
            vf_ref = refs[i]
            i += 1
        prm_refs = refs[i:i + n_prm]
        i += n_prm
        pp_ref, mck_ref, dog_ref = refs[i], refs[i + 1], refs[i + 2]
        i += 3
        dvout_ref = None
        if not layer1:
            dvout_ref = refs[i]
            i += 1
        dy_ref = refs[i]
        i += 1
        dvf_ref = None
        if layer1:
            dvf_ref = refs[i]
            i += 1
        dprm_refs = refs[i:i + n_prm]
        i += n_prm
        dpp_ref = refs[i]
        dm_s, dprev_s = refs[i + 1:i + 3]
        c = pl.program_id(0)
        cr = nc - 1 - c

        @pl.when(c == 0)
        def _():
            dm_s[...] = jnp.zeros_like(dm_s)
            dprev_s[...] = jnp.zeros_like(dprev_s)
            dpp_ref[...] = jnp.zeros_like(dpp_ref)
            for r in dprm_refs:
                r[...] = jnp.zeros_like(r)

        prev = prev_ref[pl.ds(7, 1), :] * (cr != 0).astype(f32)
        prm_v = tuple(r[...] for r in prm_refs)
        ppv = tuple(pp_ref[:, pl.ds(q, 1), :] for q in range(5))
        dog = jnp.stack([dog_ref[:, j * LANES:(j + 1) * LANES] for j in range(n_pair)], axis=0)
        m0, tm = mck_ref[0, :n_pair], mck_ref[0, n_pair:]
        no_tm = jnp.zeros_like(tm)
        if layer1:
            _, vjp = jax.vjp(lambda a, b, d, e, g, h: _rwkv_step(True, a, b, d, e, g, h, tm),
                             prm_v, y_ref[...], prev, vf_ref[...], ppv, m0)
            dprm, dy, dprev, dvf, dppv, dm0 = vjp((dog, dm_s[...], jnp.zeros((CHUNK, dr), f32), no_tm))
            dvf_ref[...] = dvf
        else:
            _, vjp = jax.vjp(lambda a, b, d, e, g: _rwkv_step(False, a, b, d, None, e, g, tm), prm_v, y_ref[...], prev, ppv, m0)
            dprm, dy, dprev, dppv, dm0 = vjp((dog, dm_s[...], dvout_ref[...], no_tm))
        dm_s[...] = dm0
        for q in range(5):
            dpp_ref[:, pl.ds(q, 1), :] += dppv[q]
        dy_ref[...] = (dy + jnp.where(_iota((CHUNK, 1), 0) == CHUNK - 1, dprev_s[...], 0.0)).astype(bf16)
        dprev_s[...] = dprev
        for r, gval in zip(dprm_refs, dprm):
            r[...] += gval

    out_shape = [jax.ShapeDtypeStruct((t, proj.shape[1]), bf16)]
    out_specs = [pl.BlockSpec((CHUNK, rwc), lambda c, p: (cidx(c), 0))]
    if layer1:
        out_shape.append(jax.ShapeDtypeStruct((t, dr), f32))
        out_specs.append(pl.BlockSpec((CHUNK, dr), lambda c, p: (cidx(c), 0)))
    out_shape += [jax.ShapeDtypeStruct(s, f32) for s in prm_shapes]
    out_specs += [full(s) for s in prm_shapes]
    out_shape.append(jax.ShapeDtypeStruct((n_pair, 8, LANES), f32))
    out_specs.append(full((n_pair, 8, LANES)))
    args = [proj, proj] + ([vf] if layer1 else []) + list(prm) + [pp, mck, dcat] + ([] if layer1 else [dvout])
    return pl.pallas_call(
        body, grid=(nc, 1), in_specs=specs, out_specs=out_specs, out_shape=out_shape,
        scratch_shapes=[pltpu.VMEM((n_pair, LANES, LANES), f32), pltpu.VMEM((1, rwc), f32)],
        compiler_params=pltpu.CompilerParams(dimension_semantics=("arbitrary", "arbitrary")),
        name=f"rwkv_bwd_l{int(layer1)}",
    )(*args)


def _hgrn_chunk(layer1, lbl, gnw, s0, q_raw, f_raw, i_in, z):
    c = q_raw.shape[-2]
    q = _silu(q_raw)
    ls = _log_sigmoid(f_raw)
    if layer1:
        l0, l1 = lbl[..., 0:1, :], lbl[..., 1:2, :]
        mx = jnp.maximum(l0, l1)
        e0, e1 = jnp.exp(l0 - mx), jnp.exp(l1 - mx)
        sm0, sm1 = e0 / (e0 + e1), e1 / (e0 + e1)
        lb = (sm0 + sm1) - sm0
        log_f = _logaddexp(jnp.log(jnp.maximum(lb, LB_FLOOR)), jnp.log1p(-lb) + ls)
        k = (1.0 - lb) * jax.nn.sigmoid(-f_raw)
    else:
        log_f = _logaddexp(jnp.full_like(ls, jnp.log(jnp.float32(LB_FLOOR))), ls)
        k = jax.nn.sigmoid(-f_raw)
    row, col = _iota((c, c), 0), _iota((c, c), 1)
    trow = _iota((c, 1), 0)
    halves = []
    half = c // 2
    while half >= 1:
        halves.append(half)
        half //= 2
    cmat = jnp.concatenate([(col <= row).astype(f32)]
                           + [(col <= (row // (2 * hf)) * (2 * hf) + hf - 1).astype(f32) for hf in halves], axis=0)
    ball = _const_left(cmat.astype(bf16), log_f)
    b = ball[..., :c, :]
    att = None
    for lvl, hf in enumerate(halves):
        blk = 2 * hf
        bref = ball[..., (lvl + 1) * c:(lvl + 2) * c, :]
        upper = (trow % blk) >= hf
        dec = jnp.exp(jnp.where(upper, b - bref, bref - b))
        qh = jnp.where(upper, q * dec, 0.0)
        kh = jnp.where(upper, 0.0, k * dec)
        term = jnp.where(row // blk == col // blk, _mm2(qh, kh, "nt", APPLY_PASSES), 0.0)
        att = term if att is None else att + term
    lhs = jnp.concatenate([q * jnp.exp(b), att, jnp.zeros(att.shape[:-1] + (LANES - c,), f32)], axis=-1)
    rhs = jnp.concatenate([s0, i_in, jnp.zeros(i_in.shape[:-2] + (LANES - c, i_in.shape[-1]), f32)], axis=-2)
    o = _mm2(lhs, rhs, "nn", APPLY_PASSES) + jnp.sum(q * k, axis=-1, keepdims=True) * i_in
    b_last = _last_row(b)
    s_new = _col_of_row(jnp.exp(b_last)) * s0 + _mm2(k * jnp.exp(b_last - b), i_in, "tn", APPLY_PASSES)
    o = o * lax.rsqrt(jnp.mean(o * o, axis=-1, keepdims=True) + RMS_EPS)
    return o * gnw * _silu(z), s_new


def _hgrn_in_specs(t, dh, col0, rev):
    nc = t // CHUNK
    nh = dh // LANES

    def cidx(c):
        return (nc - 1 - c) if rev else c

    grp = _group(nh)
    specs = [pl.BlockSpec((CHUNK, LANES), functools.partial(lambda g, j, h, c: (cidx(c), col0 + g * nh + h * grp + j), g, j))
             for j in range(grp) for g in range(4)]
    specs.append(pl.BlockSpec((2, grp * LANES), lambda h, c: (0, h)))
    specs.append(pl.BlockSpec((1, grp * LANES), lambda h, c: (0, h)))
    return specs, cidx, grp


def _hgrn_fwd(layer1, proj, lbl, gnw, cat, rwc):
    t, d = cat.shape
    dh = gnw.shape[1]
    nh = dh // LANES
    nc = t // CHUNK
    col0 = rwc // LANES
    specs, _, grp = _hgrn_in_specs(t, dh, col0, False)
    specs.append(pl.BlockSpec(memory_space=pl.ANY))
    assert (d - dh) % (grp * LANES) == 0
    cat_col0 = (d - dh) // (grp * LANES)

    def body(*refs):
        x_refs = refs[:4 * grp]
        lbl_ref, gnw_ref, _, cat_ref, sck_ref, s_s = refs[4 * grp:]
        c = pl.program_id(1)

        @pl.when(c == 0)
        def _():
            s_s[...] = jnp.zeros_like(s_s)

        lanes = [slice(j * LANES, (j + 1) * LANES) for j in range(grp)]
        s0 = s_s[...]
        sck_ref[:, 0] = s0
        out, s_new = _hgrn_chunk(layer1, jnp.stack([lbl_ref[:, ln] for ln in lanes]), jnp.stack([gnw_ref[:, ln] for ln in lanes]),
                                 s0, *(jnp.stack([x_refs[4 * j + g][...] for j in range(grp)]) for g in range(4)))
        for j in range(grp):
            cat_ref[:, lanes[j]] = out[j]
        s_s[...] = s_new

    return pl.pallas_call(
        body, grid=(nh // grp, nc), in_specs=specs,
        out_specs=[pl.BlockSpec((CHUNK, grp * LANES), lambda h, c: (c, cat_col0 + h)),
                   pl.BlockSpec((grp, 1, LANES, LANES), lambda h, c: (h, c, 0, 0))],
        out_shape=[jax.ShapeDtypeStruct((t, d), f32), jax.ShapeDtypeStruct((nh, nc, LANES, LANES), f32)],
        scratch_shapes=[pltpu.VMEM((grp, LANES, LANES), f32)],
        input_output_aliases={4 * grp + 2: 0},
        compiler_params=pltpu.CompilerParams(dimension_semantics=("arbitrary", "arbitrary")),
        name=f"hgrn_fwd_l{int(layer1)}",
    )(*([proj] * (4 * grp)), lbl, gnw, cat)


def _hgrn_bwd(layer1, proj, lbl, gnw, sck, dcat, rwc, dproj):
    t, d = dcat.shape
    dh = gnw.shape[1]
    nh = dh // LANES
    nc = t // CHUNK
    col0 = rwc // LANES
    specs, cidx, grp = _hgrn_in_specs(t, dh, col0, True)
    assert grp == nh and (d - dh) % (grp * LANES) == 0
    cat_col0 = (d - dh) // (grp * LANES)
    specs.append(pl.BlockSpec((grp, 1, LANES, LANES), lambda h, c: (h, cidx(c), 0, 0)))
    specs.append(pl.BlockSpec((CHUNK, grp * LANES), lambda h, c: (cidx(c), cat_col0 + h)))
    specs.append(pl.BlockSpec(memory_space=pl.ANY))

    def body(*refs):
        x_refs = refs[:4 * grp]
        lbl_ref, gnw_ref, sck_ref, do_ref, _, dp_hbm, dlbl_ref, dgnw_ref, ds_s, stage, sems = refs[4 * grp:]
        c = pl.program_id(1)
        slot = c % 2

        def put(s, g, chunk):
            return pltpu.make_async_copy(stage.at[s, g], dp_hbm.at[pl.ds(chunk * CHUNK, CHUNK), pl.ds(rwc + g * dh, dh)],
                                         sems.at[s, g])

        @pl.when(c == 0)
        def _():
            ds_s[...] = jnp.zeros_like(ds_s)
            dlbl_ref[...] = jnp.zeros_like(dlbl_ref)
            dgnw_ref[...] = jnp.zeros_like(dgnw_ref)

        @pl.when(c >= 2)
        def _():
            for g in range(4):
                put(slot, g, 0).wait()

        lanes = [slice(j * LANES, (j + 1) * LANES) for j in range(grp)]
        _, vjp = jax.vjp(functools.partial(_hgrn_chunk, layer1),
                         jnp.stack([lbl_ref[:, ln] for ln in lanes]), jnp.stack([gnw_ref[:, ln] for ln in lanes]), sck_ref[:, 0],
                         *(jnp.stack([x_refs[4 * j + g][...] for j in range(grp)]) for g in range(4)))
        dlbl, dgnw, ds0, dq, df, di, dz = vjp((jnp.stack([do_ref[:, ln] for ln in lanes]), ds_s[...]))
        ds_s[...] = ds0
        for j in range(grp):
            dlbl_ref[:, lanes[j]] += dlbl[j]
            dgnw_ref[:, lanes[j]] += dgnw[j]
            for g, val in enumerate((dq, df, di, dz)):
                stage[slot, g, :, lanes[j]] = val[j].astype(bf16)
        for g in range(4):
            put(slot, g, nc - 1 - c).start()

        @pl.when(c == nc - 1)
        def _():
            for g in range(4):
                put(slot, g, 0).wait()
                if nc >= 2:
                    put(1 - slot, g, 0).wait()

    return pl.pallas_call(
        body, grid=(1, nc), in_specs=specs,
        out_specs=[pl.BlockSpec(memory_space=pl.ANY),
                   pl.BlockSpec((2, grp * LANES), lambda h, c: (0, h)),
                   pl.BlockSpec((1, grp * LANES), lambda h, c: (0, h))],
        out_shape=[jax.ShapeDtypeStruct(dproj.shape, dproj.dtype), jax.ShapeDtypeStruct((2, dh), f32),
                   jax.ShapeDtypeStruct((1, dh), f32)],
        scratch_shapes=[pltpu.VMEM((grp, LANES, LANES), f32), pltpu.VMEM((2, 4, CHUNK, dh), bf16),
                        pltpu.SemaphoreType.DMA((2, 4))],
        input_output_aliases={4 * grp + 4: 0},
        compiler_params=pltpu.CompilerParams(dimension_semantics=("arbitrary", "arbitrary")),
        name=f"hgrn_bwd_l{int(layer1)}",
    )(*([proj] * (4 * grp)), lbl, gnw, sck, dcat, dproj)


def _ln(h, y, w, b):
    u = ALPHA * h + y
    mu = jnp.mean(u, axis=-1, keepdims=True)
    var = jnp.mean(jnp.square(u - mu), axis=-1, keepdims=True)
    return (u - mu) * lax.rsqrt(var + LN_EPS) * w + b


def _row_tile(t):
    return 256 if t % 256 == 0 else t


def _ln_fwd(h, y, w, b):
    t, d = h.shape
    tr = _row_tile(t)

    def body(h_ref, y_ref, w_ref, b_ref, o_ref, o16_ref):
        out = _ln(h_ref[...], y_ref[...], w_ref[...], b_ref[...])
        o_ref[...] = out
        o16_ref[...] = out.astype(bf16)

    row = pl.BlockSpec((tr, d), lambda i: (i, 0))
    vec = pl.BlockSpec((1, d), lambda i: (0, 0))
    return pl.pallas_call(body, grid=(t // tr,), in_specs=[row, row, vec, vec], out_specs=[row, row],
                          out_shape=[jax.ShapeDtypeStruct((t, d), f32), jax.ShapeDtypeStruct((t, d), bf16)],
                          name="ln_fwd")(h, y, w, b)


def _ln_loss_bwd(h, y, w, b, tgt):
    t, d = h.shape
    tr = _row_tile(t)

    def body(h_ref, y_ref, w_ref, b_ref, t_ref, dy_ref, dy16_ref, dw_ref, db_ref, loss_ref):
        @pl.when(pl.program_id(0) == 0)
        def _():
            dw_ref[...] = jnp.zeros_like(dw_ref)
            db_ref[...] = jnp.zeros_like(db_ref)
            loss_ref[...] = jnp.zeros_like(loss_ref)

        out, vjp = jax.vjp(lambda yy, ww, bb: _ln(h_ref[...], yy, ww, bb), y_ref[...], w_ref[...], b_ref[...])
        err = out - t_ref[...]
        loss_ref[...] += 0.5 * jnp.sum(jnp.mean(jnp.square(err), axis=-1, keepdims=True), axis=0, keepdims=True)
        dy, dw, db = vjp(err * (1.0 / d))
        dy_ref[...] = dy
        dy16_ref[...] = dy.astype(bf16)
        dw_ref[...] += dw
        db_ref[...] += db

    row = pl.BlockSpec((tr, d), lambda i: (i, 0))
    vec = pl.BlockSpec((1, d), lambda i: (0, 0))
    return pl.pallas_call(
        body, grid=(t // tr,), in_specs=[row, row, vec, vec, row],
        out_specs=[row, row, vec, vec, pl.BlockSpec((1, LANES), lambda i: (0, 0))],
        out_shape=[jax.ShapeDtypeStruct((t, d), f32), jax.ShapeDtypeStruct((t, d), bf16), jax.ShapeDtypeStruct((1, d), f32),
                   jax.ShapeDtypeStruct((1, d), f32), jax.ShapeDtypeStruct((1, LANES), f32)],
        compiler_params=pltpu.CompilerParams(dimension_semantics=("arbitrary",)), name="ln_loss_bwd")(h, y, w, b, tgt)


def _ln_bwd(h, y, w, b, dout):
    t, d = h.shape
    tr = _row_tile(t)

    def body(h_ref, y_ref, w_ref, b_ref, do_ref, dy_ref, dy16_ref, dw_ref, db_ref):
        @pl.when(pl.program_id(0) == 0)
        def _():
            dw_ref[...] = jnp.zeros_like(dw_ref)
            db_ref[...] = jnp.zeros_like(db_ref)

        _, vjp = jax.vjp(lambda yy, ww, bb: _ln(h_ref[...], yy, ww, bb), y_ref[...], w_ref[...], b_ref[...])
        dy, dw, db = vjp(do_ref[...])
        dy_ref[...] = dy
        dy16_ref[...] = dy.astype(bf16)
        dw_ref[...] += dw
        db_ref[...] += db

    row = pl.BlockSpec((tr, d), lambda i: (i, 0))
    vec = pl.BlockSpec((1, d), lambda i: (0, 0))
    return pl.pallas_call(
        body, grid=(t // tr,), in_specs=[row, row, vec, vec, row], out_specs=[row, row, vec, vec],
        out_shape=[jax.ShapeDtypeStruct((t, d), f32), jax.ShapeDtypeStruct((t, d), bf16),
                   jax.ShapeDtypeStruct((1, d), f32), jax.ShapeDtypeStruct((1, d), f32)],
        compiler_params=pltpu.CompilerParams(dimension_semantics=("arbitrary",)), name="ln_bwd")(h, y, w, b, dout)


def _pick(n, prefs):
    for p in prefs:
        if n % p == 0:
            return p
    return n


def _tile(n, want):
    if n <= want:
        return n
    for cand in range(want - want % LANES, 0, -LANES):
        if n % cand == 0:
            return cand
    return n


_MM_TILES = {"proj": (1024, 1664, 2048), "out": (1024, 1024, 2048), "dcat": (1024, 1024, 2048),
             "dwout": (512, 2048, 2048), "dwin": (640, 2048, 2048), "dh": (1024, 1024, 1664)}


def _matmul(a, b, mode, name, tiles, add=None, add_scale=1.0, out_dtype=f32, after=None):
    if mode == "nn":
        (m, k), n = a.shape, b.shape[1]
    elif mode == "nt":
        (m, k), n = a.shape, b.shape[0]
    else:
        (k, m), n = a.shape, b.shape[1]
    tm, tn, tk = _tile(m, tiles[0]), _tile(n, tiles[1]), _tile(k, tiles[2])
    nk = k // tk
    cache_a = nk == 1 and a.dtype != bf16 and n // tn > 1

    def body(*refs):
        a_ref, b_ref = refs[0], refs[1]
        add_ref = refs[2] if add is not None else None
        n_in = 2 + (add is not None) + (after is not None)
        o_ref = refs[n_in]
        scratch = refs[n_in + 1:]

        def finish(res):
            if add is not None:
                res = res + add_scale * add_ref[...]
            o_ref[...] = res.astype(out_dtype)

        if cache_a:
            a_bf = scratch[0]

            @pl.when(pl.program_id(1) == 0)
            def _():
                a_bf[...] = a_ref[...].astype(bf16)

            a_val = a_bf[...]
        else:
            a_val = a_ref[...].astype(bf16)
        prod = lax.dot_general(a_val, b_ref[...].astype(bf16), _DIMS[mode], preferred_element_type=f32)
        if nk == 1:
            finish(prod)
        else:
            acc = scratch[-1]
            kk = pl.program_id(2)

            @pl.when(kk == 0)
            def _():
                acc[...] = prod

            @pl.when(kk != 0)
            def _():
                acc[...] += prod

            @pl.when(kk == nk - 1)
            def _():
                finish(acc[...])

    a_shape = (tk, tm) if mode == "tn" else (tm, tk)
    a_spec = pl.BlockSpec(a_shape, (lambda i, j, kk: (kk, i)) if mode == "tn" else (lambda i, j, kk: (i, kk)))
    b_spec = pl.BlockSpec((tn, tk), lambda i, j, kk: (j, kk)) if mode == "nt" else pl.BlockSpec((tk, tn), lambda i, j, kk: (kk, j))
    o_spec = pl.BlockSpec((tm, tn), lambda i, j, kk: (i, j))
    in_specs = [a_spec, b_spec] + ([o_spec] if add is not None else []) + ([pl.BlockSpec(memory_space=pl.ANY)] if after is not None else [])
    args = [a, b] + ([add] if add is not None else []) + ([after] if after is not None else [])
    scratch_shapes = ([pltpu.VMEM(a_shape, bf16)] if cache_a else []) + ([pltpu.VMEM((tm, tn), f32)] if nk > 1 else [])
    return pl.pallas_call(
        body, grid=(m // tm, n // tn, nk), in_specs=in_specs, out_specs=o_spec,
        out_shape=jax.ShapeDtypeStruct((m, n), out_dtype), scratch_shapes=scratch_shapes,
        compiler_params=pltpu.CompilerParams(dimension_semantics=("parallel", "arbitrary", "arbitrary")),
        name=name,
    )(*args)


def _position():
    return lax.axis_index("x"), lax.axis_index("y"), lax.axis_index("c")


def _flip(pos, k):
    x, y, c = pos
    return (1 - x if k & 4 else x, 1 - y if k & 2 else y, 1 - c if k & 1 else c)


def _index(pos):
    return 4 * pos[0] + 2 * pos[1] + pos[2]


def _all_gather_rows(xs, name):
    n_arr = len(xs)
    chips = (2, 4, 6)

    def body(*refs):
        x_refs, out_refs = refs[:n_arr], refs[n_arr:2 * n_arr]
        send_sems, recv_sems, local_sems = refs[2 * n_arr:]
        me = _position()
        sibling = _flip(me, 1)

        def copy(i, sem, block, to, own=False):
            m_per = x_refs[i].shape[0]
            rows = out_refs[i].at[pl.ds(_index(block) * m_per, m_per), :]
            return pltpu.make_async_remote_copy(
                src_ref=x_refs[i] if own else rows, dst_ref=rows,
                send_sem=send_sems.at[7 * i + sem], recv_sem=recv_sems.at[7 * i + sem], device_id=to, device_id_type=MESH)

        mine = [pltpu.make_async_copy(x_refs[i], out_refs[i].at[pl.ds(_index(me) * x_refs[i].shape[0], x_refs[i].shape[0]), :],
                                      local_sems.at[i]) for i in range(n_arr)]
        first, passed = [], []
        for i in range(n_arr):
            first.append(copy(i, 0, me, sibling, own=True))
            first += [copy(i, 1 + j, me, _flip(me, k), own=True) for j, k in enumerate(chips)]
            passed.append([copy(i, 4 + j, _flip(me, k), sibling) for j, k in enumerate(chips)])
        for cp in mine + first:
            cp.start()
        for i in range(n_arr):
            for j, k in enumerate(chips):
                copy(i, 1 + j, _flip(me, k), me).wait_recv()
                passed[i][j].start()
        for i in range(n_arr):
            copy(i, 0, sibling, me).wait_recv()
            for j, k in enumerate(chips):
                copy(i, 4 + j, _flip(sibling, k), me).wait_recv()
        for cp in first + [cp for group in passed for cp in group]:
            cp.wait_send()
        for cp in mine:
            cp.wait()

    anyspec = pl.BlockSpec(memory_space=pl.ANY)
    return pl.pallas_call(
        body, out_shape=[jax.ShapeDtypeStruct((N_DEV * x.shape[0], x.shape[1]), x.dtype) for x in xs],
        in_specs=[anyspec] * n_arr, out_specs=[anyspec] * n_arr,
        scratch_shapes=[pltpu.SemaphoreType.DMA((7 * n_arr,)), pltpu.SemaphoreType.DMA((7 * n_arr,)),
                        pltpu.SemaphoreType.DMA((n_arr,))],
        name=name,
    )(*xs)


def _split_start(srcs, lands, plan, n_copies, name, after=()):
    n_arr = len(srcs)
    n_after = len(after)
    hbm = pl.BlockSpec(memory_space=pltpu.HBM)
    sem = pl.BlockSpec(memory_space=pltpu.SEMAPHORE)

    def body(*refs):
        src_refs, land_refs = refs[:n_arr], refs[n_arr:2 * n_arr]
        outs_at = 2 * n_arr + n_after
        send_sems, recv_sems = refs[outs_at:outs_at + n_arr], refs[outs_at + n_arr:outs_at + 2 * n_arr]
        token = refs[-1]
        me = _position()
        for i in range(n_arr):
            for j, (src, dst, peer, _) in enumerate(plan(i, src_refs[i], land_refs[i], me)):
                pltpu.make_async_remote_copy(src_ref=src, dst_ref=dst, send_sem=send_sems[i].at[j], recv_sem=recv_sems[i].at[j],
                                             device_id=peer, device_id_type=MESH).start()
        token[...] = jnp.zeros_like(token)

    outs = pl.pallas_call(
        body, name=name,
        out_shape=([pltpu.SemaphoreType.DMA((n_copies,))] * (2 * n_arr)
                   + [pltpu.HBM(a.shape, a.dtype) for a in list(srcs) + list(lands)]
                   + [jax.ShapeDtypeStruct((8, LANES), f32)]),
        in_specs=[hbm] * (2 * n_arr) + [pl.BlockSpec(memory_space=pl.ANY)] * n_after,
        out_specs=[sem] * (2 * n_arr) + [hbm] * (2 * n_arr) + [pl.BlockSpec(memory_space=pltpu.VMEM)],
        input_output_aliases={i: 2 * n_arr + i for i in range(2 * n_arr)},
        compiler_params=pltpu.CompilerParams(has_side_effects=pltpu.SideEffectType.DATAFLOW_SIDE_EFFECTING),
    )(*[pltpu.with_memory_space_constraint(a, pltpu.HBM) for a in list(srcs) + list(lands)], *after)
    return (outs[:n_arr], outs[n_arr:2 * n_arr], outs[2 * n_arr:3 * n_arr], outs[3 * n_arr:4 * n_arr], outs[-1])


def _split_wait(started, plan, after, name):
    send_sems, recv_sems, srcs, lands, _ = started
    n_arr = len(srcs)
    hbm = pl.BlockSpec(memory_space=pltpu.HBM)
    sem = pl.BlockSpec(memory_space=pltpu.SEMAPHORE)

    def body(*refs):
        src_refs, land_refs = refs[:n_arr], refs[n_arr:2 * n_arr]
        s_sems, r_sems = refs[2 * n_arr:3 * n_arr], refs[3 * n_arr:4 * n_arr]
        me = _position()
        for i in range(n_arr):
            for j, (src, _, peer, arrival) in enumerate(plan(i, src_refs[i], land_refs[i], me)):
                cp = pltpu.make_async_remote_copy(src_ref=src, dst_ref=arrival, send_sem=s_sems[i].at[j], recv_sem=r_sems[i].at[j],
                                                  device_id=peer, device_id_type=MESH)
                cp.wait_send()
                cp.wait_recv()

    outs = pl.pallas_call(
        body, name=name,
        out_shape=[pltpu.HBM(a.shape, a.dtype) for a in list(srcs) + list(lands)],
        in_specs=[hbm] * (2 * n_arr) + [sem] * (2 * n_arr) + [pl.BlockSpec(memory_space=pl.ANY)],
        out_specs=[hbm] * (2 * n_arr),
        input_output_aliases={i: i for i in range(2 * n_arr)},
        compiler_params=pltpu.CompilerParams(has_side_effects=pltpu.SideEffectType.DATAFLOW_SIDE_EFFECTING),
    )(*srcs, *lands, *send_sems, *recv_sems, after)
    return outs[:n_arr], outs[n_arr:]


def _landing_zone(blk, me, name):
    m, n = blk.shape

    def body(me_ref, x_ref, o_ref):
        del me_ref
        o_ref[...] = x_ref[...]

    return pl.pallas_call(
        body,
        grid_spec=pltpu.PrefetchScalarGridSpec(
            num_scalar_prefetch=1, grid=(1,),
            in_specs=[pl.BlockSpec((m, n), lambda i, me_ref: (0, 0))],
            out_specs=pl.BlockSpec((m, n), lambda i, me_ref: (me_ref[0], 0))),
        out_shape=jax.ShapeDtypeStruct((N_DEV * m, n), blk.dtype), name=name,
    )(jnp.reshape(me, (1,)).astype(jnp.int32), blk)


_GATHER_FLIPS = (1, 2, 4, 6)


def _gather_plan(i, src_ref, land_ref, me):
    m = src_ref.shape[0]

    def rows(pos):
        return land_ref.at[pl.ds(_index(pos) * m, m), :]

    return [(src_ref, rows(me), _flip(me, k), rows(_flip(me, k))) for k in _GATHER_FLIPS]


def _gather_forward(lands, name):
    n_arr = len(lands)
    chips = (2, 4, 6)

    def body(*refs):
        out_refs = refs[n_arr:2 * n_arr]
        send_sems, recv_sems = refs[2 * n_arr:]
        me = _position()
        sibling = _flip(me, 1)
        sends, arrivals = [], []
        for i, out_ref in enumerate(out_refs):
            m = out_ref.shape[0] // N_DEV

            def copy(pos, j):
                blk = out_ref.at[pl.ds(_index(pos) * m, m), :]
                return pltpu.make_async_remote_copy(src_ref=blk, dst_ref=blk, send_sem=send_sems.at[3 * i + j],
                                                    recv_sem=recv_sems.at[3 * i + j], device_id=sibling, device_id_type=MESH)

            for j, k in enumerate(chips):
                sends.append(copy(_flip(me, k), j))
                arrivals.append(copy(_flip(sibling, k), j))
        for cp in sends:
            cp.start()
        for cp in arrivals:
            cp.wait_recv()
        for cp in sends:
            cp.wait_send()

    anyspec = pl.BlockSpec(memory_space=pl.ANY)
    return pl.pallas_call(
        body, out_shape=[jax.ShapeDtypeStruct(a.shape, a.dtype) for a in lands],
        in_specs=[anyspec] * n_arr, out_specs=[anyspec] * n_arr, input_output_aliases={i: i for i in range(n_arr)},
        scratch_shapes=[pltpu.SemaphoreType.DMA((3 * n_arr,))] * 2, name=name,
    )(*lands)


def _chips_plan(i, src_ref, land_ref, me):
    m = src_ref.shape[0] // 4
    plan = []
    for j, k in enumerate((2, 4, 6)):
        peer = _flip(me, k)
        plan.append((src_ref.at[pl.ds((2 * peer[0] + peer[1]) * m, m), :], land_ref.at[j], peer, land_ref.at[j]))
    return plan


def _sibling_plan(i, src_ref, land_ref, me):
    m = src_ref.shape[0] // N_DEV
    sibling = _flip(me, 1)
    return [(src_ref.at[pl.ds((2 * q + 1 - me[2]) * m, m), :], land_ref.at[q], sibling, land_ref.at[q]) for q in range(4)]


def _sum_with_sibling(g, recv, name):
    m = g.shape[0] // N_DEV
    n = g.shape[1]
    tr = _pick(m, (208, 128, 64, 32, 16))
    nt = m // tr

    def body(c_ref, g_ref, r_ref, o_ref):
        del c_ref
        o_ref[...] = (g_ref[0, 0].astype(f32) + r_ref[0].astype(f32)).astype(o_ref.dtype)

    return pl.pallas_call(
        body,
        grid_spec=pltpu.PrefetchScalarGridSpec(
            num_scalar_prefetch=1, grid=(4, nt),
            in_specs=[pl.BlockSpec((1, 1, tr, n), lambda q, i, c_ref: (q, c_ref[0], i, 0)),
                      pl.BlockSpec((1, tr, n), lambda q, i, c_ref: (q, i, 0))],
            out_specs=pl.BlockSpec((tr, n), lambda q, i, c_ref: (q * nt + i, 0))),
        out_shape=jax.ShapeDtypeStruct((4 * m, n), bf16), name=name,
    )(jnp.reshape(lax.axis_index("c"), (1,)).astype(jnp.int32), g.reshape(4, 2, m, n), recv)


def _sum_with_chips(h, recv, name, slot=0, n_slots=1, into=None):
    m = h.shape[0] // 4
    n = h.shape[1]
    tr = _pick(m, (208, 128, 64, 32, 16))

    def body(h_ref, r_ref, *rest):
        o_ref = rest[-1]
        my_q = 2 * lax.axis_index("x") + lax.axis_index("y")
        own = h_ref[0].astype(f32)
        for q in range(1, 4):
            own = jnp.where(my_q == q, h_ref[q].astype(f32), own)
        o_ref[0] = ((own + r_ref[0].astype(f32)) + r_ref[1].astype(f32)) + r_ref[2].astype(f32)

    in_specs = [pl.BlockSpec((4, tr, n), lambda i: (0, i, 0)), pl.BlockSpec((3, tr, n), lambda i: (0, i, 0))]
    args = [h.reshape(4, m, n), recv]
    if into is not None:
        in_specs.append(pl.BlockSpec(memory_space=pl.ANY))
        args.append(into)
    return pl.pallas_call(
        body, grid=(m // tr,), in_specs=in_specs,
        out_specs=pl.BlockSpec((1, tr, n), lambda i: (slot, i, 0)), out_shape=jax.ShapeDtypeStruct((n_slots, m, n), f32),
        input_output_aliases={2: 0} if into is not None else {}, name=name,
    )(*args)


def _sum_slots(parts, name):
    n_slot, m, n = parts.shape
    tr = _pick(m, (208, 128, 64, 32, 16, 8))

    def body(p_ref, o_ref):
        acc = p_ref[0]
        for s in range(1, n_slot):
            acc = acc + p_ref[s]
        o_ref[...] = acc

    return pl.pallas_call(
        body, grid=(m // tr,), in_specs=[pl.BlockSpec((n_slot, tr, n), lambda i: (0, i, 0))],
        out_specs=pl.BlockSpec((tr, n), lambda i: (i, 0)), out_shape=jax.ShapeDtypeStruct((m, n), parts.dtype), name=name,
    )(parts)


def _reduce_scatter_begin(gs, name):
    lands = [lax.empty((4, g.shape[0] // N_DEV, g.shape[1]), g.dtype) for g in gs]
    return _split_start(gs, lands, _sibling_plan, 4, "rs_d2d_start_" + name)


def _reduce_scatter_middle(started, after, name):
    gs, from_sibling = _split_wait(started, _sibling_plan, after, "rs_d2d_wait_" + name)
    chip_sums = [_sum_with_sibling(g, r, f"rs_sum2_{name}_{i}") for i, (g, r) in enumerate(zip(gs, from_sibling))]
    lands = [lax.empty((3, h.shape[0] // 4, h.shape[1]), h.dtype) for h in chip_sums]
    return _split_start(chip_sums, lands, _chips_plan, 3, "rs_ici_start_" + name)


def _reduce_scatter_end(started, after, name, first_into=None, slot=0, n_slots=1):
    chip_sums, from_chips = _split_wait(started, _chips_plan, after, "rs_ici_wait_" + name)
    out = []
    for i, (h, r) in enumerate(zip(chip_sums, from_chips)):
        if i == 0:
            out.append(_sum_with_chips(h, r, f"rs_sum4_{name}_{i}", slot, n_slots, first_into))
        else:
            out.append(_sum_with_chips(h, r, f"rs_sum4_{name}_{i}")[0])
    return out


def _adamw_update(w, g, m, v):
    mm = ADAM_B1 * m + (1.0 - ADAM_B1) * g
    vv = ADAM_B2 * v + (1.0 - ADAM_B2) * jnp.square(g)
    m_hat = mm / (1.0 - ADAM_B1 ** ADAM_STEP)
    v_hat = vv / (1.0 - ADAM_B2 ** ADAM_STEP)
    return -ADAM_LR * (m_hat / (jnp.sqrt(v_hat) + ADAM_EPS) + ADAM_WD * w), mm, vv


def _adamw_many(ws, gs, ms, vs, name):
    k = len(ws)
    shapes = [w.shape for w in ws]
    flat = [[a.reshape(-1, a.shape[-1]) for a in group] for group in (ws, gs, ms, vs)]

    def body(*refs):
        for i in range(k):
            d, mm, vv = _adamw_update(*(refs[j * k + i][...] for j in range(4)))
            refs[4 * k + i][...] = d
            refs[5 * k + i][...] = mm
            refs[6 * k + i][...] = vv

    outs = pl.pallas_call(
        body, out_shape=[jax.ShapeDtypeStruct(a.shape, f32) for a in flat[0]] * 3, name=name,
    )(*flat[0], *flat[1], *flat[2], *flat[3])
    return tuple([outs[j * k + i].reshape(shapes[i]) for i in range(k)] for j in range(3))


def _adamw(w, g, m, v, name):
    shape = w.shape
    n = shape[-1]
    r = w.size // n
    w2, g2, m2, v2 = (a.reshape(r, n) for a in (w, g, m, v))
    tr = _pick(r, (256, 208, 128, 64, 32, 16, 8))

    def body(w_ref, g_ref, m_ref, v_ref, d_ref, mo_ref, vo_ref):
        d_ref[...], mo_ref[...], vo_ref[...] = _adamw_update(w_ref[...], g_ref[...], m_ref[...], v_ref[...])

    spec = pl.BlockSpec((tr, n), lambda i: (i, 0))
    outs = pl.pallas_call(
        body, grid=(r // tr,), in_specs=[spec] * 4, out_specs=[spec] * 3,
        out_shape=[jax.ShapeDtypeStruct((r, n), f32)] * 3, name=name,
    )(w2, g2, m2, v2)
    return tuple(o.reshape(shape) for o in outs)


_SMALL = ("shift_mu", "w_decay0", "a0", "k_k", "k_a", "r_k", "ln_x_w", "ln_x_b", "v_mix0", "lb_logits",
          "g_norm_w", "ln_w", "ln_b")
_NAMES = ("w_in", "shift_mu", "w_decay0", "w_decay_up", "a0", "a_up", "k_k", "k_a", "r_k", "ln_x_w", "ln_x_b",
          "v_mix0", "v_mix_down", "v_mix_up", "lb_logits", "g_norm_w", "w_out", "ln_w", "ln_b")


def _pad_rows(a, rows, at_end):
    z = jnp.zeros((rows - a.shape[0], a.shape[1]), a.dtype)
    return jnp.concatenate([a, z] if at_end else [z, a], axis=0)


def kernel(x, w_in, shift_mu, w_decay0, w_decay_up, a0, a_up, k_k, k_a, r_k, ln_x_w, ln_x_b, v_mix0, v_mix_down, v_mix_up, lb_logits, g_norm_w, w_out, ln_w, ln_b, loss_target, m_w_in, m_shift_mu, m_w_decay0, m_w_decay_up, m_a0, m_a_up, m_k_k, m_k_a, m_r_k, m_ln_x_w, m_ln_x_b, m_v_mix0, m_v_mix_down, m_v_mix_up, m_lb_logits, m_g_norm_w, m_w_out, m_ln_w, m_ln_b, v_w_in, v_shift_mu, v_w_decay0, v_w_decay_up, v_a0, v_a_up, v_k_k, v_k_a, v_r_k, v_ln_x_w, v_ln_x_b, v_v_mix0, v_v_mix_down, v_v_mix_up, v_lb_logits, v_g_norm_w, v_w_out, v_ln_w, v_ln_b):
    weights = dict(w_in=w_in, shift_mu=shift_mu, w_decay0=w_decay0, w_decay_up=w_decay_up, a0=a0, a_up=a_up, k_k=k_k,
                   k_a=k_a, r_k=r_k, ln_x_w=ln_x_w, ln_x_b=ln_x_b, v_mix0=v_mix0, v_mix_down=v_mix_down,
                   v_mix_up=v_mix_up, lb_logits=lb_logits, g_norm_w=g_norm_w, w_out=w_out, ln_w=ln_w, ln_b=ln_b)
    mom1 = dict(w_in=m_w_in, shift_mu=m_shift_mu, w_decay0=m_w_decay0, w_decay_up=m_w_decay_up, a0=m_a0, a_up=m_a_up,
                k_k=m_k_k, k_a=m_k_a, r_k=m_r_k, ln_x_w=m_ln_x_w, ln_x_b=m_ln_x_b, v_mix0=m_v_mix0,
                v_mix_down=m_v_mix_down, v_mix_up=m_v_mix_up, lb_logits=m_lb_logits, g_norm_w=m_g_norm_w,
                w_out=m_w_out, ln_w=m_ln_w, ln_b=m_ln_b)
    mom2 = dict(w_in=v_w_in, shift_mu=v_shift_mu, w_decay0=v_w_decay0, w_decay_up=v_w_decay_up, a0=v_a0, a_up=v_a_up,
                k_k=v_k_k, k_a=v_k_a, r_k=v_r_k, ln_x_w=v_ln_x_w, ln_x_b=v_ln_x_b, v_mix0=v_v_mix0,
                v_mix_down=v_v_mix_down, v_mix_up=v_v_mix_up, lb_logits=v_lb_logits, g_norm_w=v_g_norm_w,
                w_out=v_w_out, ln_w=v_ln_w, ln_b=v_ln_b)
    assert x.shape[0] == 1 and w_in.shape[0] == DEPTH
    t, d = x.shape[1], x.shape[2]
    dr = w_decay0.shape[1]
    dh = g_norm_w.shape[1]
    rank_w, rank_a, rank_v = w_decay_up.shape[1], a_up.shape[1], v_mix_up.shape[1]
    rwc = 4 * dr + rank_w + rank_a
    assert rank_w + rank_a == LANES and rank_v <= LANES and dr + dh == d
    assert t % CHUNK == 0 and dr % LANES == 0 and dh % LANES == 0 and shift_mu.shape[1] == rwc
    n_pair = dr // LANES
    me = _index(_position())

    shard = dr // N_DEV
    pack = jnp.concatenate([w_decay_up[0], w_decay_up[1], a_up[0], a_up[1], v_mix_up[0], v_mix_down[0].T], axis=0)
    win_t0, pack = _all_gather_rows([w_in[0].T.astype(bf16), pack], "ag_first")
    win_t = [win_t0, None]
    wout = [None, None]

    def start_gather(blocks, name, after):
        lands = [_landing_zone(blk, me, f"{name}_zone{i}") for i, blk in enumerate(blocks)]
        return _split_start(blocks, lands, _gather_plan, len(_GATHER_FLIPS), name, after=after)

    gather_wout0 = start_gather([w_out[0].astype(bf16)], "ag_wout0_start", (win_t[0], pack))
    gather_layer1 = start_gather([w_in[1].T.astype(bf16), w_out[1].astype(bf16)], "ag_layer1_start", (gather_wout0[-1],))
    pack = jnp.transpose(pack.reshape(N_DEV, -1, shard), (1, 0, 2)).reshape(-1, dr)
    offs = [0, rank_w, 2 * rank_w, 2 * rank_w + rank_a, 2 * rank_w + 2 * rank_a, 2 * rank_w + 2 * rank_a + rank_v,
            2 * rank_w + 2 * rank_a + 2 * rank_v]
    wdu_f = [pack[offs[0]:offs[1]], pack[offs[1]:offs[2]]]
    aup_f = [pack[offs[2]:offs[3]], pack[offs[3]:offs[4]]]
    vup_f = pack[offs[4]:offs[5]]
    vdown_f = pack[offs[5]:offs[6]].T

    def after_start(a, started):
        return a + started[-1][0:1, 0:1]

    def rwkv_params(l):
        mu = after_start(shift_mu[0:1], gather_layer1) if l == 0 else shift_mu[l:l + 1]
        prm = [mu, w_decay0[l:l + 1], a0[l:l + 1], _pad_rows(wdu_f[l], LANES, True),
               _pad_rows(aup_f[l], LANES, False)]
        if l == 1:
            prm += [v_mix0[0:1], _pad_rows(vdown_f.T, LANES, True).T, _pad_rows(vup_f, LANES, True)]
        rows = jnp.stack([k_k[l], k_a[l], r_k[l], ln_x_w[l], ln_x_b[l]] + [jnp.zeros((dr,), f32)] * 3, axis=0)
        pp = jnp.transpose(rows.reshape(8, n_pair, LANES), (1, 0, 2))
        return tuple(prm), pp

    h = x[0]
    h16 = h.astype(bf16)
    tgt = loss_target[0]
    saved = []
    vfirst = None
    for l in range(DEPTH):
        prm, pp = rwkv_params(l)
        proj = _matmul(h16, win_t[l], "nt", f"mm_proj_{l}", _MM_TILES["proj"])
        if l == 0:
            cat, vfirst, mck = _rwkv_fwd(False, proj, None, prm, pp, d)
        else:
            cat, mck = _rwkv_fwd(True, proj, vfirst, prm, pp, d)
        cat, sck = _hgrn_fwd(l == 1, proj, lb_logits, g_norm_w[l:l + 1], cat, rwc)
        if l == 0:
            _, arrived = _split_wait(gather_wout0, _gather_plan, cat, "ag_wout0_wait")
            (wout[0],) = _gather_forward(arrived, "ag_wout0_forward")
        y = _matmul(cat, wout[l], "nn", f"mm_out_{l}", _MM_TILES["out"])
        saved.append((h, h16, proj, prm, pp, mck, sck, cat, y))
        if l < DEPTH - 1:
            h, h16 = _ln_fwd(h, y, ln_w[l:l + 1], ln_b[l:l + 1])
            _, arrived = _split_wait(gather_layer1, _gather_plan, h16, "ag_layer1_wait")
            win_t[1], wout[1] = _gather_forward(arrived, "ag_layer1_forward")
        else:
            top = _ln_loss_bwd(h, y, ln_w[l:l + 1], ln_b[l:l + 1], tgt)
    loss = lax.psum(top[4][0, 0], ("x", "y", "c"))

    grads = {}
    big = {}
    dvfirst = None
    d_lbl = None
    rs_started = {}
    for l in reversed(range(DEPTH)):
        h_l, h16_l, proj, prm, pp, mck, sck, cat, y = saved[l]
        if l == DEPTH - 1:
            dy, dy16, g_ln_w, g_ln_b = top[:4]
        else:
            dy, dy16, g_ln_w, g_ln_b = _ln_bwd(h_l, y, after_start(ln_w[l:l + 1], rs_started[l + 1]), ln_b[l:l + 1], dh_out)
        dcat = _matmul(dy16, wout[l], "nt", f"mm_dcat_{l}", _MM_TILES["dcat"])
        big[("w_out", l)] = _matmul(cat, dy16, "tn", f"mm_dwout_{l}", _MM_TILES["dwout"], out_dtype=bf16)
        if l == 1:
            outs = _rwkv_bwd(True, proj, vfirst, prm, pp, mck, dcat, None)
            dproj_r, dvfirst = outs[0], outs[1]
            dprm, dpp = outs[2:-1], outs[-1]
        else:
            outs = _rwkv_bwd(False, proj, None, prm, pp, mck, dcat, dvfirst)
            dproj_r = outs[0]
            dprm, dpp = outs[1:-1], outs[-1]
        dproj, dlbl_l, dgnw = _hgrn_bwd(l == 1, proj, lb_logits, g_norm_w[l:l + 1], sck, dcat, rwc, dproj_r)
        big[("w_in", l)] = _matmul(dproj, h16_l, "tn", f"mm_dwin_{l}", _MM_TILES["dwin"], out_dtype=bf16)
        sharded = [dprm[3][:rank_w].T, dprm[4][rank_w:].T]
        if l == 1:
            sharded += [dprm[6][:, :rank_v], dprm[7][:rank_v].T,
                        jnp.zeros((dr, LANES - 2 * rank_v), f32)]
        sharded = jnp.concatenate(sharded, axis=1).astype(bf16)
        d2d = _reduce_scatter_begin([big[("w_in", l)], big[("w_out", l)], sharded], f"l{l}")
        if l == 0:
            rs_started[l] = _reduce_scatter_middle(d2d, sharded, f"l{l}")
            token = rs_started[l][-1]
        else:
            token = d2d[-1]
        dh_out = _matmul(dproj, win_t[l], "nn", f"mm_dh_{l}", _MM_TILES["dh"], add=dy, add_scale=ALPHA, after=token)
        if l > 0:
            rs_started[l] = _reduce_scatter_middle(d2d, dh_out, f"l{l}")
        dpp = jnp.transpose(dpp, (1, 0, 2)).reshape(8, dr)
        grads[l] = dict(shift_mu=dprm[0][0], w_decay0=dprm[1][0], a0=dprm[2][0],
                        k_k=dpp[0], k_a=dpp[1], r_k=dpp[2], ln_x_w=dpp[3], ln_x_b=dpp[4],
                        g_norm_w=dgnw[0], ln_w=g_ln_w[0], ln_b=g_ln_b[0])
        if l == 1:
            grads[l].update(v_mix0=dprm[5][0])
            d_lbl = dlbl_l
    grad_x = dh_out[None]

    def both(name):
        return jnp.stack([grads[0][name], grads[1][name]])

    small = dict(shift_mu=both("shift_mu"), w_decay0=both("w_decay0"), a0=both("a0"), k_k=both("k_k"), k_a=both("k_a"),
                 r_k=both("r_k"), ln_x_w=both("ln_x_w"), ln_x_b=both("ln_x_b"), v_mix0=grads[1]["v_mix0"][None],
                 lb_logits=d_lbl, g_norm_w=both("g_norm_w"), ln_w=both("ln_w"), ln_b=both("ln_b"))
    flat = jnp.concatenate([small[nm].reshape(-1) for nm in _SMALL])
    n_flat = flat.shape[0]
    rows = -(-n_flat // (8 * LANES)) * 8
    flat = jnp.concatenate([flat, jnp.zeros((rows * LANES - n_flat,), f32)]).reshape(rows, LANES)
    total = _sum_slots(_all_gather_rows([flat], "ag_small_grads")[0].reshape(N_DEV, rows, LANES), "sum_small_grads").reshape(-1)
    gsm = {}
    off = 0
    for nm in _SMALL:
        size = small[nm].size
        gsm[nm] = total[off:off + size].reshape(small[nm].shape)
        off += size
    reduced = {1: _reduce_scatter_end(rs_started[1], dh_out, "l1", None, 1, DEPTH)}
    reduced[0] = _reduce_scatter_end(rs_started[0], total, "l0", reduced[1][0], 0, DEPTH)
    g_w_in_t = reduced[0][0]
    gsm["w_in"] = jnp.transpose(g_w_in_t, (0, 2, 1))
    gsm["w_out"] = jnp.stack([reduced[l][1] for l in range(DEPTH)])
    gsm["w_decay_up"] = jnp.stack([reduced[l][2][:, :rank_w].T for l in range(DEPTH)])
    gsm["a_up"] = jnp.stack([reduced[l][2][:, rank_w:rank_w + rank_a].T for l in range(DEPTH)])
    gsm["v_mix_down"] = reduced[1][2][:, LANES:LANES + rank_v][None]
    gsm["v_mix_up"] = reduced[1][2][:, LANES + rank_v:LANES + 2 * rank_v].T[None]

    deltas, new_m, new_v = {}, {}, {}
    swap = lambda a: jnp.transpose(a, (0, 2, 1))
    deltas["w_in"], new_m["w_in"], new_v["w_in"] = (
        swap(a) for a in _adamw(swap(w_in), g_w_in_t, swap(m_w_in), swap(v_w_in), "adamw_w_in"))
    deltas["w_out"], new_m["w_out"], new_v["w_out"] = _adamw(w_out, gsm["w_out"], m_w_out, v_w_out, "adamw_w_out")
    rest = [nm for nm in _NAMES if nm not in ("w_in", "w_out")]
    d_rest, m_rest, v_rest = _adamw_many([weights[nm] for nm in rest], [gsm[nm] for nm in rest],
                                         [mom1[nm] for nm in rest], [mom2[nm] for nm in rest], "adamw_small")
    for i, nm in enumerate(rest):
        deltas[nm], new_m[nm], new_v[nm] = d_rest[i], m_rest[i], v_rest[i]
    return (loss, grad_x, *[gsm[nm] for nm in _NAMES], *[deltas[nm] for nm in _NAMES],
            *[new_m[nm] for nm in _NAMES], *[new_v[nm] for nm in _NAMES])
```

```python
import functools

import jax
import jax.numpy as jnp
from jax import lax
from jax.experimental import pallas as pl
from jax.experimental.pallas import tpu as pltpu

f32 = jnp.float32
bf16 = jnp.bfloat16

N_DEV = 8
CHUNK = 64
LANES = 128
RWKV_HEAD = 64
DEPTH = 2
ALPHA = (2 * DEPTH) ** 0.25
LN_EPS = 1e-5
GN_EPS = 64e-5
RMS_EPS = 1e-5
LB_FLOOR = 1e-30
ADAM_LR, ADAM_B1, ADAM_B2, ADAM_EPS, ADAM_WD, ADAM_STEP = 0.001, 0.9, 0.999, 1e-08, 0.01, 10
MESH = pl.DeviceIdType.MESH


def _iota(shape, d):
    return lax.broadcasted_iota(jnp.int32, shape, d)


_DIMS = {"nn": (((1,), (0,)), ((), ())), "nt": (((1,), (1,)), ((), ())), "tn": (((0,), (0,)), ((), ()))}
_BATCH_DIMS = {"nn": (((2,), (1,)), ((0,), (0,))), "nt": (((2,), (2,)), ((0,), (0,))), "tn": (((1,), (1,)), ((0,), (0,)))}
_K_AXES = {"nn": (-1, -2), "nt": (-1, -1), "tn": (-2, -2)}


def _mxu(a, b, mode):
    return lax.dot_general(a, b, (_BATCH_DIMS if a.ndim == 3 else _DIMS)[mode], preferred_element_type=f32)


def _split(x):
    hi = x.astype(bf16)
    return hi, (x - hi.astype(f32)).astype(bf16)


def _mm2_impl(a, b, mode, passes=3):
    if passes == 1:
        return _mxu(a.astype(bf16), b.astype(bf16), mode)
    ah, al = _split(a)
    if passes == 3:
        bh, bl = _split(b)
        lhs, rhs = [ah, ah, al], [bh, bl, bh]
    else:
        bh = b.astype(bf16)
        lhs, rhs = [ah, al], [bh, bh]
    ka, kb = _K_AXES[mode]
    k = a.shape[ka]
    if k % (LANES if -1 in (ka, kb) else 16) == 0:
        return _mxu(jnp.concatenate(lhs, axis=ka), jnp.concatenate(rhs, axis=kb), mode)
    out = _mxu(lhs[0], rhs[0], mode)
    for x, y in zip(lhs[1:], rhs[1:]):
        out = out + _mxu(x, y, mode)
    return out


@functools.partial(jax.custom_vjp, nondiff_argnums=(2, 3))
def _mm2(a, b, mode, passes=3):
    return _mm2_impl(a, b, mode, passes)


def _mm2_fwd(a, b, mode, passes):
    return _mm2_impl(a, b, mode, passes), (a, b)


def _mm2_bwd(mode, passes, res, g):
    a, b = res
    if mode == "nn":
        return _mm2_impl(g, b, "nt", passes), _mm2_impl(a, g, "tn", passes)
    if mode == "nt":
        return _mm2_impl(g, b, "nn", passes), _mm2_impl(g, a, "tn", passes)
    return _mm2_impl(b, g, "nt", passes), _mm2_impl(a, g, "nn", passes)


_mm2.defvjp(_mm2_fwd, _mm2_bwd)

TRI_PASSES = 1
APPLY_PASSES = 1


def _const_impl(cm, x, mode):
    if mode in ("r", "rt"):
        shape = x.shape
        out = _mxu(x.astype(bf16).reshape(-1, shape[-1]), cm, "nn" if mode == "r" else "nt")
        return out.reshape(shape[:-1] + (out.shape[-1],))
    hi, lo = _split(x)
    if x.ndim == 3:
        cm = jnp.broadcast_to(cm, (x.shape[0],) + cm.shape)
    return _mxu(cm, hi, mode) + _mxu(cm, lo, mode)


@jax.custom_vjp
def _const_left(cm, x):
    return _const_impl(cm, x, "nn")


_const_left.defvjp(lambda cm, x: (_const_impl(cm, x, "nn"), cm),
                   lambda cm, g: (jnp.zeros_like(cm), _const_impl(cm, g, "tn")))


@jax.custom_vjp
def _const_right(x, cm):
    return _const_impl(cm, x, "r")


_const_right.defvjp(lambda x, cm: (_const_impl(cm, x, "r"), cm),
                    lambda cm, g: (_const_impl(cm, g, "rt"), jnp.zeros_like(cm)))


def _tri_inv(a):
    n = a.shape[-1]
    tm = (_iota((n, n), 0) == _iota((n, n), 1)).astype(f32) + a
    ak = a
    for _ in range(5):
        ak = _mm2_impl(ak, ak, "nn", TRI_PASSES)
        tm = tm + _mm2_impl(tm, ak, "nn", TRI_PASSES)
    return tm


@jax.custom_vjp
def _tri_solve(tm, a, x):
    del a
    return _mm2_impl(tm, x, "nn", APPLY_PASSES)


def _tri_solve_fwd(tm, a, x):
    u = _mm2_impl(tm, x, "nn", APPLY_PASSES)
    return u, (tm, u)


def _tri_solve_bwd(res, du):
    tm, u = res
    dx = _mm2_impl(tm, du, "tn", APPLY_PASSES)
    return jnp.zeros_like(tm), _mm2_impl(dx, u, "nt", APPLY_PASSES), dx


_tri_solve.defvjp(_tri_solve_fwd, _tri_solve_bwd)


def _col_of_row(row_vec):
    n = row_vec.shape[-1]
    eye = _iota((n, n), 0) == _iota((n, n), 1)
    return jnp.sum(jnp.where(eye, jnp.broadcast_to(row_vec, row_vec.shape[:-2] + (n, n)), 0.0), axis=-1, keepdims=True)


def _softplus(x):
    return jnp.maximum(x, 0.0) + jnp.log1p(jnp.exp(-jnp.abs(x)))


def _log_sigmoid(x):
    return -_softplus(-x)


def _logaddexp(a, b):
    return jnp.maximum(a, b) + jnp.log1p(jnp.exp(-jnp.abs(a - b)))


def _silu(x):
    return x * jax.nn.sigmoid(x)


def _tril(c, strict):
    r, s = _iota((c, c), 0), _iota((c, c), 1)
    return (r > s) if strict else (r >= s)


def _last_row(a):
    c = a.shape[-2]
    return jnp.sum(jnp.where(_iota(a.shape, a.ndim - 2) == c - 1, a, 0.0), axis=-2, keepdims=True)


def _rwkv_pre(layer1, prm, y, prev, vf):
    c = y.shape[0]
    if layer1:
        mu, w0, a0, wup, aup, v0, vdown, vup = prm
    else:
        mu, w0, a0, wup, aup = prm
    dr = w0.shape[1]
    shift = (_iota((c, c), 0) == _iota((c, c), 1) + 1).astype(bf16)
    y_prev = _const_left(shift, y) + jnp.where(_iota((c, 1), 0) == 0, prev, 0.0)
    rw = y + mu * (y_prev - y)
    r, k, v, z = (rw[:, i * dr:(i + 1) * dr] for i in range(4))
    wdad = rw[:, 4 * dr:4 * dr + LANES]
    w_raw = w0 + _mm2(jnp.tanh(wdad), wup, "nn")
    lw = -jnp.exp(-_softplus(-w_raw) - 0.5)
    asig = jax.nn.sigmoid(a0 + _mm2(wdad, aup, "nn", APPLY_PASSES))
    if layer1:
        v = v + (vf - v) * jax.nn.sigmoid(v0 + _mm2(_mm2(v, vdown, "nn", APPLY_PASSES), vup, "nn", APPLY_PASSES))
    return r, k, v, z, lw, asig


def _rwkv_pair(pp, m0, xs, tm=None):
    kkw, kaw, rkw, gnw, gnb = pp
    r, k, v, z, lw, asig = xs
    c = r.shape[-2]
    n2 = 2 * c
    lane = _iota((1, LANES), 1)
    mh0, mh1 = (lane < RWKV_HEAD).astype(f32), (lane >= RWKV_HEAD).astype(f32)
    same_head = _iota((LANES, LANES), 0) // RWKV_HEAD == _iota((LANES, LANES), 1) // RWKV_HEAD
    g = same_head.astype(bf16)

    def seg(x):
        return _const_right(x, g)

    def stack(x):
        return jnp.concatenate([x * mh0, x * mh1], axis=-2)

    kk = k * kkw
    kk = kk / jnp.maximum(jnp.sqrt(seg(kk * kk)), 1e-12)
    k2 = k * (1.0 + (asig - 1.0) * kaw)
    a = -kk
    b = kk * asig
    cum = _const_left(_tril(c, False).astype(bf16), lw)
    at = stack(a * jnp.exp(cum - lw))
    rt = stack(r * jnp.exp(cum))
    en = jnp.exp(-cum)
    sc = _mm2(jnp.concatenate([at, rt], axis=-2), jnp.concatenate([stack(b * en), stack(k2 * en)], axis=-2), "nt")
    row, col = _iota((n2, n2), 0), _iota((n2, n2), 1)
    same = row // c == col // c
    strict = same & (row % c > col % c)
    incl = same & (row % c >= col % c)
    aab = jnp.where(strict, sc[..., :n2, :n2], 0.0)
    aak = jnp.where(strict, sc[..., :n2, n2:], 0.0)
    arb = jnp.where(incl, sc[..., n2:, :n2], 0.0)
    ark = jnp.where(incl, sc[..., n2:, n2:], 0.0)
    vv = jnp.concatenate([v, v], axis=-2)
    mask_st = jnp.concatenate([jnp.broadcast_to(mh0, (c, LANES)), jnp.broadcast_to(mh1, (c, LANES))], axis=0)
    x_st = _mm2(jnp.concatenate([at, aak], axis=-1), jnp.concatenate([m0, vv], axis=-2), "nn", APPLY_PASSES)
    if tm is None:
        tm = _tri_inv(lax.stop_gradient(aab))
    u_st = _tri_solve(tm, aab, x_st) * mask_st
    o_st = _mm2(jnp.concatenate([rt, arb, ark], axis=-1), jnp.concatenate([m0, u_st, vv], axis=-2), "nn", APPLY_PASSES) * mask_st
    u = u_st[..., :c, :] + u_st[..., c:, :]
    o = o_st[..., :c, :] + o_st[..., c:, :]
    cum_last = _last_row(cum)
    dec_end = jnp.exp(cum_last - cum)
    m_new = _col_of_row(jnp.exp(cum_last)) * m0 + _mm2(
        jnp.concatenate([b * dec_end, k2 * dec_end], axis=-2), jnp.concatenate([u, v], axis=-2), "tn", APPLY_PASSES) * same_head.astype(f32)
    mean = seg(o) * (1.0 / RWKV_HEAD)
    d = o - mean
    var = seg(d * d) * (1.0 / RWKV_HEAD)
    on = d * lax.rsqrt(var + GN_EPS) * gnw + gnb
    bonus = seg(r * k2 * rkw) * v
    return (on + bonus) * _silu(z), m_new, tm


def _split_lanes(a, n):
    return [a[:, i * LANES:(i + 1) * LANES] for i in range(n)]


def _rwkv_step(layer1, prm, y, prev, vf, pp, m0, tm=None):
    xs = _rwkv_pre(layer1, prm, y, prev, vf)
    n_pair = m0.shape[0]
    og, m_new, tm = _rwkv_pair(pp, m0, tuple(jnp.concatenate([p[None] for p in _split_lanes(a, n_pair)], axis=0) for a in xs), tm)
    return og, m_new, xs[2], tm


def _group(n):
    return n


def _rwkv_specs(layer1, t, dr, rwc, n_pair, rev):
    nc = t // CHUNK
    grp = _group(n_pair)

    def cidx(c):
        return (nc - 1 - c) if rev else c

    full = lambda shape: pl.BlockSpec(shape, lambda c, p: tuple(0 for _ in shape))
    specs = [
        pl.BlockSpec((CHUNK, rwc), lambda c, p: (cidx(c), 0)),
        pl.BlockSpec((8, rwc), lambda c, p: (jnp.maximum(cidx(c) * (CHUNK // 8) - 1, 0), 0)),
    ]
    if layer1:
        specs.append(pl.BlockSpec((CHUNK, dr), lambda c, p: (cidx(c), 0)))
    prm_shapes = [(1, rwc), (1, dr), (1, dr), (LANES, dr), (LANES, dr)]
    if layer1:
        prm_shapes += [(1, dr), (dr, LANES), (LANES, dr)]
    specs += [full(s) for s in prm_shapes]
    specs.append(pl.BlockSpec((grp, 8, LANES), lambda c, p: (p, 0, 0)))
    return specs, prm_shapes, cidx, full


def _rwkv_fwd(layer1, proj, vf, prm, pp, cat_width):
    t = proj.shape[0]
    dr = prm[1].shape[1]
    rwc = prm[0].shape[1]
    n_pair = dr // LANES
    nc = t // CHUNK
    n_prm = len(prm)
    specs, _, _, _ = _rwkv_specs(layer1, t, dr, rwc, n_pair, False)

    def body(*refs):
        y_ref, prev_ref = refs[0], refs[1]
        i = 2
        vf_ref = None
        if layer1:
            vf_ref = refs[i]
            i += 1
        prm_refs = refs[i:i + n_prm]
        i += n_prm
        pp_ref = refs[i]
        i += 1
        cat_ref = refs[i]
        i += 1
        vout_ref = None
        if not layer1:
            vout_ref = refs[i]
            i += 1
        mck_ref, m_s = refs[i], refs[i + 1]
        c = pl.program_id(0)

        @pl.when(c == 0)
        def _():
            m_s[...] = jnp.zeros_like(m_s)

        prev = prev_ref[pl.ds(7, 1), :] * (c != 0).astype(f32)
        m0 = m_s[...]
        ppv = tuple(pp_ref[:, pl.ds(q, 1), :] for q in range(5))
        og, m_new, v, tm = _rwkv_step(layer1, tuple(r[...] for r in prm_refs), y_ref[...], prev,
                                      vf_ref[...] if layer1 else None, ppv, m0)
        mck_ref[0, :n_pair] = m0
        mck_ref[0, n_pair:] = tm
        if not layer1:
            vout_ref[...] = v
        for j in range(n_pair):
            cat_ref[:, j * LANES:(j + 1) * LANES] = og[j]
        m_s[...] = m_new

    grp = _group(n_pair)
    assert grp == n_pair
    out_shape = [jax.ShapeDtypeStruct((t, cat_width), f32)]
    out_specs = [pl.BlockSpec((CHUNK, grp * LANES), lambda c, p: (c, p))]
    if not layer1:
        out_shape.append(jax.ShapeDtypeStruct((t, dr), f32))
        out_specs.append(pl.BlockSpec((CHUNK, dr), lambda c, p: (c, 0)))
    out_shape.append(jax.ShapeDtypeStruct((nc, 2 * n_pair, LANES, LANES), f32))
    out_specs.append(pl.BlockSpec((1, 2 * grp, LANES, LANES), lambda c, p: (c, p, 0, 0)))
    args = [proj, proj] + ([vf] if layer1 else []) + list(prm) + [pp]
    return pl.pallas_call(
        body, grid=(nc, 1), in_specs=specs, out_specs=out_specs, out_shape=out_shape,
        scratch_shapes=[pltpu.VMEM((n_pair, LANES, LANES), f32)],
        compiler_params=pltpu.CompilerParams(dimension_semantics=("arbitrary", "arbitrary")),
        name=f"rwkv_fwd_l{int(layer1)}",
    )(*args)


def _rwkv_bwd(layer1, proj, vf, prm, pp, mck, dcat, dvout):
    t = proj.shape[0]
    dr = prm[1].shape[1]
    rwc = prm[0].shape[1]
    n_pair = dr // LANES
    nc = t // CHUNK
    n_prm = len(prm)
    specs, prm_shapes, cidx, full = _rwkv_specs(layer1, t, dr, rwc, n_pair, True)
    grp = _group(n_pair)
    assert grp == n_pair
    specs.append(pl.BlockSpec((1, 2 * grp, LANES, LANES), lambda c, p: (cidx(c), p, 0, 0)))
    specs.append(pl.BlockSpec((CHUNK, grp * LANES), lambda c, p: (cidx(c), p)))
    if not layer1:
        specs.append(pl.BlockSpec((CHUNK, dr), lambda c, p: (cidx(c), 0)))

    def body(*refs):
        y_ref, prev_ref = refs[0], refs[1]
        i = 2
        vf_ref = None
        if layer1:
            vf_ref = refs[i]
            i += 1
        prm_refs = refs[i:i + n_prm]
        i += n_prm
        pp_ref, mck_ref, dog_ref = refs[i], refs[i + 1], refs[i + 2]
        i += 3
        dvout_ref = None
        if not layer1:
            dvout_ref = refs[i]
            i += 1
        dy_ref = refs[i]
        i += 1
        dvf_ref = None
        if layer1:
            dvf_ref = refs[i]
            i += 1
        dprm_refs = refs[i:i + n_prm]
        i += n_prm
        dpp_ref = refs[i]
        dm_s, dprev_s = refs[i + 1:i + 3]
        c = pl.program_id(0)
        cr = nc - 1 - c

        @pl.when(c == 0)
        def _():
            dm_s[...] = jnp.zeros_like(dm_s)
            dprev_s[...] = jnp.zeros_like(dprev_s)
            dpp_ref[...] = jnp.zeros_like(dpp_ref)
            for r in dprm_refs:
                r[...] = jnp.zeros_like(r)

        prev = prev_ref[pl.ds(7, 1), :] * (cr != 0).astype(f32)
        prm_v = tuple(r[...] for r in prm_refs)
        ppv = tuple(pp_ref[:, pl.ds(q, 1), :] for q in range(5))
        dog = jnp.stack([dog_ref[:, j * LANES:(j + 1) * LANES] for j in range(n_pair)], axis=0)
        m0, tm = mck_ref[0, :n_pair], mck_ref[0, n_pair:]
        no_tm = jnp.zeros_like(tm)
        if layer1:
            _, vjp = jax.vjp(lambda a, b, d, e, g, h: _rwkv_step(True, a, b, d, e, g, h, tm),
                             prm_v, y_ref[...], prev, vf_ref[...], ppv, m0)
            dprm, dy, dprev, dvf, dppv, dm0 = vjp((dog, dm_s[...], jnp.zeros((CHUNK, dr), f32), no_tm))
            dvf_ref[...] = dvf
        else:
            _, vjp = jax.vjp(lambda a, b, d, e, g: _rwkv_step(False, a, b, d, None, e, g, tm), prm_v, y_ref[...], prev, ppv, m0)
            dprm, dy, dprev, dppv, dm0 = vjp((dog, dm_s[...], dvout_ref[...], no_tm))
        dm_s[...] = dm0
        for q in range(5):
            dpp_ref[:, pl.ds(q, 1), :] += dppv[q]
        dy_ref[...] = (dy + jnp.where(_iota((CHUNK, 1), 0) == CHUNK - 1, dprev_s[...], 0.0)).astype(bf16)
        dprev_s[...] = dprev
        for r, gval in zip(dprm_refs, dprm):
            r[...] += gval

    out_shape = [jax.ShapeDtypeStruct((t, proj.shape[1]), bf16)]
    out_specs = [pl.BlockSpec((CHUNK, rwc), lambda c, p: (cidx(c), 0))]
    if layer1:
        out_shape.append(jax.ShapeDtypeStruct((t, dr), f32))
        out_specs.append(pl.BlockSpec((CHUNK, dr), lambda c, p: (cidx(c), 0)))
    out_shape += [jax.ShapeDtypeStruct(s, f32) for s in prm_shapes]
    out_specs += [full(s) for s in prm_shapes]
    out_shape.append(jax.ShapeDtypeStruct((n_pair, 8, LANES), f32))
    out_specs.append(full((n_pair, 8, LANES)))
    args = [proj, proj] + ([vf] if layer1 else []) + list(prm) + [pp, mck, dcat] + ([] if layer1 else [dvout])
    return pl.pallas_call(
        body, grid=(nc, 1), in_specs=specs, out_specs=out_specs, out_shape=out_shape,
        scratch_shapes=[pltpu.VMEM((n_pair, LANES, LANES), f32), pltpu.VMEM((1, rwc), f32)],
        compiler_params=pltpu.CompilerParams(dimension_semantics=("arbitrary", "arbitrary")),
        name=f"rwkv_bwd_l{int(layer1)}",
    )(*args)


def _hgrn_chunk(layer1, lbl, gnw, s0, q_raw, f_raw, i_in, z):
    c = q_raw.shape[-2]
    q = _silu(q_raw)
    ls = _log_sigmoid(f_raw)
    if layer1:
        l0, l1 = lbl[..., 0:1, :], lbl[..., 1:2, :]
        mx = jnp.maximum(l0, l1)
        e0, e1 = jnp.exp(l0 - mx), jnp.exp(l1 - mx)
        sm0, sm1 = e0 / (e0 + e1), e1 / (e0 + e1)
        lb = (sm0 + sm1) - sm0
        log_f = _logaddexp(jnp.log(jnp.maximum(lb, LB_FLOOR)), jnp.log1p(-lb) + ls)
        k = (1.0 - lb) * jax.nn.sigmoid(-f_raw)
    else:
        log_f = _logaddexp(jnp.full_like(ls, jnp.log(jnp.float32(LB_FLOOR))), ls)
        k = jax.nn.sigmoid(-f_raw)
    row, col = _iota((c, c), 0), _iota((c, c), 1)
    trow = _iota((c, 1), 0)
    halves = []
    half = c // 2
    while half >= 1:
        halves.append(half)
        half //= 2
    cmat = jnp.concatenate([(col <= row).astype(f32)]
                           + [(col <= (row // (2 * hf)) * (2 * hf) + hf - 1).astype(f32) for hf in halves], axis=0)
    ball = _const_left(cmat.astype(bf16), log_f)
    b = ball[..., :c, :]
    att = None
    for lvl, hf in enumerate(halves):
        blk = 2 * hf
        bref = ball[..., (lvl + 1) * c:(lvl + 2) * c, :]
        upper = (trow % blk) >= hf
        dec = jnp.exp(jnp.where(upper, b - bref, bref - b))
        qh = jnp.where(upper, q * dec, 0.0)
        kh = jnp.where(upper, 0.0, k * dec)
        term = jnp.where(row // blk == col // blk, _mm2(qh, kh, "nt", APPLY_PASSES), 0.0)
        att = term if att is None else att + term
    lhs = jnp.concatenate([q * jnp.exp(b), att, jnp.zeros(att.shape[:-1] + (LANES - c,), f32)], axis=-1)
    rhs = jnp.concatenate([s0, i_in, jnp.zeros(i_in.shape[:-2] + (LANES - c, i_in.shape[-1]), f32)], axis=-2)
    o = _mm2(lhs, rhs, "nn", APPLY_PASSES) + jnp.sum(q * k, axis=-1, keepdims=True) * i_in
    b_last = _last_row(b)
    s_new = _col_of_row(jnp.exp(b_last)) * s0 + _mm2(k * jnp.exp(b_last - b), i_in, "tn", APPLY_PASSES)
    o = o * lax.rsqrt(jnp.mean(o * o, axis=-1, keepdims=True) + RMS_EPS)
    return o * gnw * _silu(z), s_new


def _hgrn_in_specs(t, dh, col0, rev):
    nc = t // CHUNK
    nh = dh // LANES

    def cidx(c):
        return (nc - 1 - c) if rev else c

    grp = _group(nh)
    specs = [pl.BlockSpec((CHUNK, LANES), functools.partial(lambda g, j, h, c: (cidx(c), col0 + g * nh + h * grp + j), g, j))
             for j in range(grp) for g in range(4)]
    specs.append(pl.BlockSpec((2, grp * LANES), lambda h, c: (0, h)))
    specs.append(pl.BlockSpec((1, grp * LANES), lambda h, c: (0, h)))
    return specs, cidx, grp


def _hgrn_fwd(layer1, proj, lbl, gnw, cat, rwc):
    t, d = cat.shape
    dh = gnw.shape[1]
    nh = dh // LANES
    nc = t // CHUNK
    col0 = rwc // LANES
    specs, _, grp = _hgrn_in_specs(t, dh, col0, False)
    specs.append(pl.BlockSpec(memory_space=pl.ANY))
    assert (d - dh) % (grp * LANES) == 0
    cat_col0 = (d - dh) // (grp * LANES)

    def body(*refs):
        x_refs = refs[:4 * grp]
        lbl_ref, gnw_ref, _, cat_ref, sck_ref, s_s = refs[4 * grp:]
        c = pl.program_id(1)

        @pl.when(c == 0)
        def _():
            s_s[...] = jnp.zeros_like(s_s)

        lanes = [slice(j * LANES, (j + 1) * LANES) for j in range(grp)]
        s0 = s_s[...]
        sck_ref[:, 0] = s0
        out, s_new = _hgrn_chunk(layer1, jnp.stack([lbl_ref[:, ln] for ln in lanes]), jnp.stack([gnw_ref[:, ln] for ln in lanes]),
                                 s0, *(jnp.stack([x_refs[4 * j + g][...] for j in range(grp)]) for g in range(4)))
        for j in range(grp):
            cat_ref[:, lanes[j]] = out[j]
        s_s[...] = s_new

    return pl.pallas_call(
        body, grid=(nh // grp, nc), in_specs=specs,
        out_specs=[pl.BlockSpec((CHUNK, grp * LANES), lambda h, c: (c, cat_col0 + h)),
                   pl.BlockSpec((grp, 1, LANES, LANES), lambda h, c: (h, c, 0, 0))],
        out_shape=[jax.ShapeDtypeStruct((t, d), f32), jax.ShapeDtypeStruct((nh, nc, LANES, LANES), f32)],
        scratch_shapes=[pltpu.VMEM((grp, LANES, LANES), f32)],
        input_output_aliases={4 * grp + 2: 0},
        compiler_params=pltpu.CompilerParams(dimension_semantics=("arbitrary", "arbitrary")),
        name=f"hgrn_fwd_l{int(layer1)}",
    )(*([proj] * (4 * grp)), lbl, gnw, cat)


def _hgrn_bwd(layer1, proj, lbl, gnw, sck, dcat, rwc, dproj):
    t, d = dcat.shape
    dh = gnw.shape[1]
    nh = dh // LANES
    nc = t // CHUNK
    col0 = rwc // LANES
    specs, cidx, grp = _hgrn_in_specs(t, dh, col0, True)
    assert grp == nh and (d - dh) % (grp * LANES) == 0
    cat_col0 = (d - dh) // (grp * LANES)
    specs.append(pl.BlockSpec((grp, 1, LANES, LANES), lambda h, c: (h, cidx(c), 0, 0)))
    specs.append(pl.BlockSpec((CHUNK, grp * LANES), lambda h, c: (cidx(c), cat_col0 + h)))
    specs.append(pl.BlockSpec(memory_space=pl.ANY))

    def body(*refs):
        x_refs = refs[:4 * grp]
        lbl_ref, gnw_ref, sck_ref, do_ref, _, dp_hbm, dlbl_ref, dgnw_ref, ds_s, stage, sems = refs[4 * grp:]
        c = pl.program_id(1)
        slot = c % 2

        def put(s, g, chunk):
            return pltpu.make_async_copy(stage.at[s, g], dp_hbm.at[pl.ds(chunk * CHUNK, CHUNK), pl.ds(rwc + g * dh, dh)],
                                         sems.at[s, g])

        @pl.when(c == 0)
        def _():
            ds_s[...] = jnp.zeros_like(ds_s)
            dlbl_ref[...] = jnp.zeros_like(dlbl_ref)
            dgnw_ref[...] = jnp.zeros_like(dgnw_ref)

        @pl.when(c >= 2)
        def _():
            for g in range(4):
                put(slot, g, 0).wait()

        lanes = [slice(j * LANES, (j + 1) * LANES) for j in range(grp)]
        _, vjp = jax.vjp(functools.partial(_hgrn_chunk, layer1),
                         jnp.stack([lbl_ref[:, ln] for ln in lanes]), jnp.stack([gnw_ref[:, ln] for ln in lanes]), sck_ref[:, 0],
                         *(jnp.stack([x_refs[4 * j + g][...] for j in range(grp)]) for g in range(4)))
        dlbl, dgnw, ds0, dq, df, di, dz = vjp((jnp.stack([do_ref[:, ln] for ln in lanes]), ds_s[...]))
        ds_s[...] = ds0
        for j in range(grp):
            dlbl_ref[:, lanes[j]] += dlbl[j]
            dgnw_ref[:, lanes[j]] += dgnw[j]
            for g, val in enumerate((dq, df, di, dz)):
                stage[slot, g, :, lanes[j]] = val[j].astype(bf16)
        for g in range(4):
            put(slot, g, nc - 1 - c).start()

        @pl.when(c == nc - 1)
        def _():
            for g in range(4):
                put(slot, g, 0).wait()
                if nc >= 2:
                    put(1 - slot, g, 0).wait()

    return pl.pallas_call(
        body, grid=(1, nc), in_specs=specs,
        out_specs=[pl.BlockSpec(memory_space=pl.ANY),
                   pl.BlockSpec((2, grp * LANES), lambda h, c: (0, h)),
                   pl.BlockSpec((1, grp * LANES), lambda h, c: (0, h))],
        out_shape=[jax.ShapeDtypeStruct(dproj.shape, dproj.dtype), jax.ShapeDtypeStruct((2, dh), f32),
                   jax.ShapeDtypeStruct((1, dh), f32)],
        scratch_shapes=[pltpu.VMEM((grp, LANES, LANES), f32), pltpu.VMEM((2, 4, CHUNK, dh), bf16),
                        pltpu.SemaphoreType.DMA((2, 4))],
        input_output_aliases={4 * grp + 4: 0},
        compiler_params=pltpu.CompilerParams(dimension_semantics=("arbitrary", "arbitrary")),
        name=f"hgrn_bwd_l{int(layer1)}",
    )(*([proj] * (4 * grp)), lbl, gnw, sck, dcat, dproj)


def _ln(h, y, w, b):
    u = ALPHA * h + y
    mu = jnp.mean(u, axis=-1, keepdims=True)
    var = jnp.mean(jnp.square(u - mu), axis=-1, keepdims=True)
    return (u - mu) * lax.rsqrt(var + LN_EPS) * w + b


def _row_tile(t):
    return 256 if t % 256 == 0 else t


def _ln_fwd(h, y, w, b):
    t, d = h.shape
    tr = _row_tile(t)

    def body(h_ref, y_ref, w_ref, b_ref, o_ref, o16_ref):
        out = _ln(h_ref[...], y_ref[...], w_ref[...], b_ref[...])
        o_ref[...] = out
        o16_ref[...] = out.astype(bf16)

    row = pl.BlockSpec((tr, d), lambda i: (i, 0))
    vec = pl.BlockSpec((1, d), lambda i: (0, 0))
    return pl.pallas_call(body, grid=(t // tr,), in_specs=[row, row, vec, vec], out_specs=[row, row],
                          out_shape=[jax.ShapeDtypeStruct((t, d), f32), jax.ShapeDtypeStruct((t, d), bf16)],
                          name="ln_fwd")(h, y, w, b)


def _ln_loss_bwd(h, y, w, b, tgt):
    t, d = h.shape
    tr = _row_tile(t)

    def body(h_ref, y_ref, w_ref, b_ref, t_ref, dy_ref, dy16_ref, dw_ref, db_ref, loss_ref):
        @pl.when(pl.program_id(0) == 0)
        def _():
            dw_ref[...] = jnp.zeros_like(dw_ref)
            db_ref[...] = jnp.zeros_like(db_ref)
            loss_ref[...] = jnp.zeros_like(loss_ref)

        out, vjp = jax.vjp(lambda yy, ww, bb: _ln(h_ref[...], yy, ww, bb), y_ref[...], w_ref[...], b_ref[...])
        err = out - t_ref[...]
        loss_ref[...] += 0.5 * jnp.sum(jnp.mean(jnp.square(err), axis=-1, keepdims=True), axis=0, keepdims=True)
        dy, dw, db = vjp(err * (1.0 / d))
        dy_ref[...] = dy
        dy16_ref[...] = dy.astype(bf16)
        dw_ref[...] += dw
        db_ref[...] += db

    row = pl.BlockSpec((tr, d), lambda i: (i, 0))
    vec = pl.BlockSpec((1, d), lambda i: (0, 0))
    return pl.pallas_call(
        body, grid=(t // tr,), in_specs=[row, row, vec, vec, row],
        out_specs=[row, row, vec, vec, pl.BlockSpec((1, LANES), lambda i: (0, 0))],
        out_shape=[jax.ShapeDtypeStruct((t, d), f32), jax.ShapeDtypeStruct((t, d), bf16), jax.ShapeDtypeStruct((1, d), f32),
                   jax.ShapeDtypeStruct((1, d), f32), jax.ShapeDtypeStruct((1, LANES), f32)],
        compiler_params=pltpu.CompilerParams(dimension_semantics=("arbitrary",)), name="ln_loss_bwd")(h, y, w, b, tgt)


def _ln_bwd(h, y, w, b, dout):
    t, d = h.shape
    tr = _row_tile(t)

    def body(h_ref, y_ref, w_ref, b_ref, do_ref, dy_ref, dy16_ref, dw_ref, db_ref):
        @pl.when(pl.program_id(0) == 0)
        def _():
            dw_ref[...] = jnp.zeros_like(dw_ref)
            db_ref[...] = jnp.zeros_like(db_ref)

        _, vjp = jax.vjp(lambda yy, ww, bb: _ln(h_ref[...], yy, ww, bb), y_ref[...], w_ref[...], b_ref[...])
        dy, dw, db = vjp(do_ref[...])
        dy_ref[...] = dy
        dy16_ref[...] = dy.astype(bf16)
        dw_ref[...] += dw
        db_ref[...] += db

    row = pl.BlockSpec((tr, d), lambda i: (i, 0))
    vec = pl.BlockSpec((1, d), lambda i: (0, 0))
    return pl.pallas_call(
        body, grid=(t // tr,), in_specs=[row, row, vec, vec, row], out_specs=[row, row, vec, vec],
        out_shape=[jax.ShapeDtypeStruct((t, d), f32), jax.ShapeDtypeStruct((t, d), bf16),
                   jax.ShapeDtypeStruct((1, d), f32), jax.ShapeDtypeStruct((1, d), f32)],
        compiler_params=pltpu.CompilerParams(dimension_semantics=("arbitrary",)), name="ln_bwd")(h, y, w, b, dout)


def _pick(n, prefs):
    for p in prefs:
        if n % p == 0:
            return p
    return n


def _tile(n, want):
    if n <= want:
        return n
    for cand in range(want - want % LANES, 0, -LANES):
        if n % cand == 0:
            return cand
    return n


_MM_TILES = {"proj": (1024, 1664, 2048), "out": (1024, 1024, 2048), "dcat": (1024, 1024, 2048),
             "dwout": (512, 2048, 2048), "dwin": (640, 2048, 2048), "dh": (1024, 1024, 1664)}


def _matmul(a, b, mode, name, tiles, add=None, add_scale=1.0, out_dtype=f32, after=None):
    if mode == "nn":
        (m, k), n = a.shape, b.shape[1]
    elif mode == "nt":
        (m, k), n = a.shape, b.shape[0]
    else:
        (k, m), n = a.shape, b.shape[1]
    tm, tn, tk = _tile(m, tiles[0]), _tile(n, tiles[1]), _tile(k, tiles[2])
    nk = k // tk
    cache_a = nk == 1 and a.dtype != bf16 and n // tn > 1

    def body(*refs):
        a_ref, b_ref = refs[0], refs[1]
        add_ref = refs[2] if add is not None else None
        n_in = 2 + (add is not None) + (after is not None)
        o_ref = refs[n_in]
        scratch = refs[n_in + 1:]

        def finish(res):
            if add is not None:
                res = res + add_scale * add_ref[...]
            o_ref[...] = res.astype(out_dtype)

        if cache_a:
            a_bf = scratch[0]

            @pl.when(pl.program_id(1) == 0)
            def _():
                a_bf[...] = a_ref[...].astype(bf16)

            a_val = a_bf[...]
        else:
            a_val = a_ref[...].astype(bf16)
        prod = lax.dot_general(a_val, b_ref[...].astype(bf16), _DIMS[mode], preferred_element_type=f32)
        if nk == 1:
            finish(prod)
        else:
            acc = scratch[-1]
            kk = pl.program_id(2)

            @pl.when(kk == 0)
            def _():
                acc[...] = prod

            @pl.when(kk != 0)
            def _():
                acc[...] += prod

            @pl.when(kk == nk - 1)
            def _():
                finish(acc[...])

    a_shape = (tk, tm) if mode == "tn" else (tm, tk)
    a_spec = pl.BlockSpec(a_shape, (lambda i, j, kk: (kk, i)) if mode == "tn" else (lambda i, j, kk: (i, kk)))
    b_spec = pl.BlockSpec((tn, tk), lambda i, j, kk: (j, kk)) if mode == "nt" else pl.BlockSpec((tk, tn), lambda i, j, kk: (kk, j))
    o_spec = pl.BlockSpec((tm, tn), lambda i, j, kk: (i, j))
    in_specs = [a_spec, b_spec] + ([o_spec] if add is not None else []) + ([pl.BlockSpec(memory_space=pl.ANY)] if after is not None else [])
    args = [a, b] + ([add] if add is not None else []) + ([after] if after is not None else [])
    scratch_shapes = ([pltpu.VMEM(a_shape, bf16)] if cache_a else []) + ([pltpu.VMEM((tm, tn), f32)] if nk > 1 else [])
    return pl.pallas_call(
        body, grid=(m // tm, n // tn, nk), in_specs=in_specs, out_specs=o_spec,
        out_shape=jax.ShapeDtypeStruct((m, n), out_dtype), scratch_shapes=scratch_shapes,
        compiler_params=pltpu.CompilerParams(dimension_semantics=("parallel", "arbitrary", "arbitrary")),
        name=name,
    )(*args)


def _position():
    return lax.axis_index("x"), lax.axis_index("y"), lax.axis_index("c")


def _flip(pos, k):
    x, y, c = pos
    return (1 - x if k & 4 else x, 1 - y if k & 2 else y, 1 - c if k & 1 else c)


def _index(pos):
    return 4 * pos[0] + 2 * pos[1] + pos[2]


def _all_gather_rows(xs, name):
    n_arr = len(xs)
    chips = (2, 4, 6)

    def body(*refs):
        x_refs, out_refs = refs[:n_arr], refs[n_arr:2 * n_arr]
        send_sems, recv_sems, local_sems = refs[2 * n_arr:]
        me = _position()
        sibling = _flip(me, 1)

        def copy(i, sem, block, to, own=False):
            m_per = x_refs[i].shape[0]
            rows = out_refs[i].at[pl.ds(_index(block) * m_per, m_per), :]
            return pltpu.make_async_remote_copy(
                src_ref=x_refs[i] if own else rows, dst_ref=rows,
                send_sem=send_sems.at[7 * i + sem], recv_sem=recv_sems.at[7 * i + sem], device_id=to, device_id_type=MESH)

        mine = [pltpu.make_async_copy(x_refs[i], out_refs[i].at[pl.ds(_index(me) * x_refs[i].shape[0], x_refs[i].shape[0]), :],
                                      local_sems.at[i]) for i in range(n_arr)]
        first, passed = [], []
        for i in range(n_arr):
            first.append(copy(i, 0, me, sibling, own=True))
            first += [copy(i, 1 + j, me, _flip(me, k), own=True) for j, k in enumerate(chips)]
            passed.append([copy(i, 4 + j, _flip(me, k), sibling) for j, k in enumerate(chips)])
        for cp in mine + first:
            cp.start()
        for i in range(n_arr):
            for j, k in enumerate(chips):
                copy(i, 1 + j, _flip(me, k), me).wait_recv()
                passed[i][j].start()
        for i in range(n_arr):
            copy(i, 0, sibling, me).wait_recv()
            for j, k in enumerate(chips):
                copy(i, 4 + j, _flip(sibling, k), me).wait_recv()
        for cp in first + [cp for group in passed for cp in group]:
            cp.wait_send()
        for cp in mine:
            cp.wait()

    anyspec = pl.BlockSpec(memory_space=pl.ANY)
    return pl.pallas_call(
        body, out_shape=[jax.ShapeDtypeStruct((N_DEV * x.shape[0], x.shape[1]), x.dtype) for x in xs],
        in_specs=[anyspec] * n_arr, out_specs=[anyspec] * n_arr,
        scratch_shapes=[pltpu.SemaphoreType.DMA((7 * n_arr,)), pltpu.SemaphoreType.DMA((7 * n_arr,)),
                        pltpu.SemaphoreType.DMA((n_arr,))],
        name=name,
    )(*xs)


def _split_start(srcs, lands, plan, n_copies, name, after=()):
    n_arr = len(srcs)
    n_after = len(after)
    hbm = pl.BlockSpec(memory_space=pltpu.HBM)
    sem = pl.BlockSpec(memory_space=pltpu.SEMAPHORE)

    def body(*refs):
        src_refs, land_refs = refs[:n_arr], refs[n_arr:2 * n_arr]
        outs_at = 2 * n_arr + n_after
        send_sems, recv_sems = refs[outs_at:outs_at + n_arr], refs[outs_at + n_arr:outs_at + 2 * n_arr]
        token = refs[-1]
        me = _position()
        for i in range(n_arr):
            for j, (src, dst, peer, _) in enumerate(plan(i, src_refs[i], land_refs[i], me)):
                pltpu.make_async_remote_copy(src_ref=src, dst_ref=dst, send_sem=send_sems[i].at[j], recv_sem=recv_sems[i].at[j],
                                             device_id=peer, device_id_type=MESH).start()
        token[...] = jnp.zeros_like(token)

    outs = pl.pallas_call(
        body, name=name,
        out_shape=([pltpu.SemaphoreType.DMA((n_copies,))] * (2 * n_arr)
                   + [pltpu.HBM(a.shape, a.dtype) for a in list(srcs) + list(lands)]
                   + [jax.ShapeDtypeStruct((8, LANES), f32)]),
        in_specs=[hbm] * (2 * n_arr) + [pl.BlockSpec(memory_space=pl.ANY)] * n_after,
        out_specs=[sem] * (2 * n_arr) + [hbm] * (2 * n_arr) + [pl.BlockSpec(memory_space=pltpu.VMEM)],
        input_output_aliases={i: 2 * n_arr + i for i in range(2 * n_arr)},
        compiler_params=pltpu.CompilerParams(has_side_effects=pltpu.SideEffectType.DATAFLOW_SIDE_EFFECTING),
    )(*[pltpu.with_memory_space_constraint(a, pltpu.HBM) for a in list(srcs) + list(lands)], *after)
    return (outs[:n_arr], outs[n_arr:2 * n_arr], outs[2 * n_arr:3 * n_arr], outs[3 * n_arr:4 * n_arr], outs[-1])


def _split_wait(started, plan, after, name):
    send_sems, recv_sems, srcs, lands, _ = started
    n_arr = len(srcs)
    hbm = pl.BlockSpec(memory_space=pltpu.HBM)
    sem = pl.BlockSpec(memory_space=pltpu.SEMAPHORE)

    def body(*refs):
        src_refs, land_refs = refs[:n_arr], refs[n_arr:2 * n_arr]
        s_sems, r_sems = refs[2 * n_arr:3 * n_arr], refs[3 * n_arr:4 * n_arr]
        me = _position()
        for i in range(n_arr):
            for j, (src, _, peer, arrival) in enumerate(plan(i, src_refs[i], land_refs[i], me)):
                cp = pltpu.make_async_remote_copy(src_ref=src, dst_ref=arrival, send_sem=s_sems[i].at[j], recv_sem=r_sems[i].at[j],
                                                  device_id=peer, device_id_type=MESH)
                cp.wait_send()
                cp.wait_recv()

    outs = pl.pallas_call(
        body, name=name,
        out_shape=[pltpu.HBM(a.shape, a.dtype) for a in list(srcs) + list(lands)],
        in_specs=[hbm] * (2 * n_arr) + [sem] * (2 * n_arr) + [pl.BlockSpec(memory_space=pl.ANY)],
        out_specs=[hbm] * (2 * n_arr),
        input_output_aliases={i: i for i in range(2 * n_arr)},
        compiler_params=pltpu.CompilerParams(has_side_effects=pltpu.SideEffectType.DATAFLOW_SIDE_EFFECTING),
    )(*srcs, *lands, *send_sems, *recv_sems, after)
    return outs[:n_arr], outs[n_arr:]


def _landing_zone(blk, me, name):
    m, n = blk.shape

    def body(me_ref, x_ref, o_ref):
        del me_ref
        o_ref[...] = x_ref[...]

    return pl.pallas_call(
        body,
        grid_spec=pltpu.PrefetchScalarGridSpec(
            num_scalar_prefetch=1, grid=(1,),
            in_specs=[pl.BlockSpec((m, n), lambda i, me_ref: (0, 0))],
            out_specs=pl.BlockSpec((m, n), lambda i, me_ref: (me_ref[0], 0))),
        out_shape=jax.ShapeDtypeStruct((N_DEV * m, n), blk.dtype), name=name,
    )(jnp.reshape(me, (1,)).astype(jnp.int32), blk)


_GATHER_FLIPS = (1, 2, 4, 6)


def _gather_plan(i, src_ref, land_ref, me):
    m = src_ref.shape[0]

    def rows(pos):
        return land_ref.at[pl.ds(_index(pos) * m, m), :]

    return [(src_ref, rows(me), _flip(me, k), rows(_flip(me, k))) for k in _GATHER_FLIPS]


def _gather_forward(lands, name):
    n_arr = len(lands)
    chips = (2, 4, 6)

    def body(*refs):
        out_refs = refs[n_arr:2 * n_arr]
        send_sems, recv_sems = refs[2 * n_arr:]
        me = _position()
        sibling = _flip(me, 1)
        sends, arrivals = [], []
        for i, out_ref in enumerate(out_refs):
            m = out_ref.shape[0] // N_DEV

            def copy(pos, j):
                blk = out_ref.at[pl.ds(_index(pos) * m, m), :]
                return pltpu.make_async_remote_copy(src_ref=blk, dst_ref=blk, send_sem=send_sems.at[3 * i + j],
                                                    recv_sem=recv_sems.at[3 * i + j], device_id=sibling, device_id_type=MESH)

            for j, k in enumerate(chips):
                sends.append(copy(_flip(me, k), j))
                arrivals.append(copy(_flip(sibling, k), j))
        for cp in sends:
            cp.start()
        for cp in arrivals:
            cp.wait_recv()
        for cp in sends:
            cp.wait_send()

    anyspec = pl.BlockSpec(memory_space=pl.ANY)
    return pl.pallas_call(
        body, out_shape=[jax.ShapeDtypeStruct(a.shape, a.dtype) for a in lands],
        in_specs=[anyspec] * n_arr, out_specs=[anyspec] * n_arr, input_output_aliases={i: i for i in range(n_arr)},
        scratch_shapes=[pltpu.SemaphoreType.DMA((3 * n_arr,))] * 2, name=name,
    )(*lands)


def _chips_plan(i, src_ref, land_ref, me):
    m = src_ref.shape[0] // 4
    plan = []
    for j, k in enumerate((2, 4, 6)):
        peer = _flip(me, k)
        plan.append((src_ref.at[pl.ds((2 * peer[0] + peer[1]) * m, m), :], land_ref.at[j], peer, land_ref.at[j]))
    return plan


def _sibling_plan(i, src_ref, land_ref, me):
    m = src_ref.shape[0] // N_DEV
    sibling = _flip(me, 1)
    return [(src_ref.at[pl.ds((2 * q + 1 - me[2]) * m, m), :], land_ref.at[q], sibling, land_ref.at[q]) for q in range(4)]


def _sum_with_sibling(g, recv, name):
    m = g.shape[0] // N_DEV
    n = g.shape[1]
    tr = _pick(m, (208, 128, 64, 32, 16))
    nt = m // tr

    def body(c_ref, g_ref, r_ref, o_ref):
        del c_ref
        o_ref[...] = (g_ref[0, 0].astype(f32) + r_ref[0].astype(f32)).astype(o_ref.dtype)

    return pl.pallas_call(
        body,
        grid_spec=pltpu.PrefetchScalarGridSpec(
            num_scalar_prefetch=1, grid=(4, nt),
            in_specs=[pl.BlockSpec((1, 1, tr, n), lambda q, i, c_ref: (q, c_ref[0], i, 0)),
                      pl.BlockSpec((1, tr, n), lambda q, i, c_ref: (q, i, 0))],
            out_specs=pl.BlockSpec((tr, n), lambda q, i, c_ref: (q * nt + i, 0))),
        out_shape=jax.ShapeDtypeStruct((4 * m, n), bf16), name=name,
    )(jnp.reshape(lax.axis_index("c"), (1,)).astype(jnp.int32), g.reshape(4, 2, m, n), recv)


def _sum_with_chips(h, recv, name, slot=0, n_slots=1, into=None):
    m = h.shape[0] // 4
    n = h.shape[1]
    tr = _pick(m, (208, 128, 64, 32, 16))

    def body(h_ref, r_ref, *rest):
        o_ref = rest[-1]
        my_q = 2 * lax.axis_index("x") + lax.axis_index("y")
        own = h_ref[0].astype(f32)
        for q in range(1, 4):
            own = jnp.where(my_q == q, h_ref[q].astype(f32), own)
        o_ref[0] = ((own + r_ref[0].astype(f32)) + r_ref[1].astype(f32)) + r_ref[2].astype(f32)

    in_specs = [pl.BlockSpec((4, tr, n), lambda i: (0, i, 0)), pl.BlockSpec((3, tr, n), lambda i: (0, i, 0))]
    args = [h.reshape(4, m, n), recv]
    if into is not None:
        in_specs.append(pl.BlockSpec(memory_space=pl.ANY))
        args.append(into)
    return pl.pallas_call(
        body, grid=(m // tr,), in_specs=in_specs,
        out_specs=pl.BlockSpec((1, tr, n), lambda i: (slot, i, 0)), out_shape=jax.ShapeDtypeStruct((n_slots, m, n), f32),
        input_output_aliases={2: 0} if into is not None else {}, name=name,
    )(*args)


def _sum_slots(parts, name):
    n_slot, m, n = parts.shape
    tr = _pick(m, (208, 128, 64, 32, 16, 8))

    def body(p_ref, o_ref):
        acc = p_ref[0]
        for s in range(1, n_slot):
            acc = acc + p_ref[s]
        o_ref[...] = acc

    return pl.pallas_call(
        body, grid=(m // tr,), in_specs=[pl.BlockSpec((n_slot, tr, n), lambda i: (0, i, 0))],
        out_specs=pl.BlockSpec((tr, n), lambda i: (i, 0)), out_shape=jax.ShapeDtypeStruct((m, n), parts.dtype), name=name,
    )(parts)


def _reduce_scatter_begin(gs, name):
    lands = [lax.empty((4, g.shape[0] // N_DEV, g.shape[1]), g.dtype) for g in gs]
    return _split_start(gs, lands, _sibling_plan, 4, "rs_d2d_start_" + name)


def _reduce_scatter_middle(started, after, name):
    gs, from_sibling = _split_wait(started, _sibling_plan, after, "rs_d2d_wait_" + name)
    chip_sums = [_sum_with_sibling(g, r, f"rs_sum2_{name}_{i}") for i, (g, r) in enumerate(zip(gs, from_sibling))]
    lands = [lax.empty((3, h.shape[0] // 4, h.shape[1]), h.dtype) for h in chip_sums]
    return _split_start(chip_sums, lands, _chips_plan, 3, "rs_ici_start_" + name)


def _reduce_scatter_end(started, after, name, first_into=None, slot=0, n_slots=1):
    chip_sums, from_chips = _split_wait(started, _chips_plan, after, "rs_ici_wait_" + name)
    out = []
    for i, (h, r) in enumerate(zip(chip_sums, from_chips)):
        if i == 0:
            out.append(_sum_with_chips(h, r, f"rs_sum4_{name}_{i}", slot, n_slots, first_into))
        else:
            out.append(_sum_with_chips(h, r, f"rs_sum4_{name}_{i}")[0])
    return out


def _adamw_update(w, g, m, v):
    mm = ADAM_B1 * m + (1.0 - ADAM_B1) * g
    vv = ADAM_B2 * v + (1.0 - ADAM_B2) * jnp.square(g)
    m_hat = mm / (1.0 - ADAM_B1 ** ADAM_STEP)
    v_hat = vv / (1.0 - ADAM_B2 ** ADAM_STEP)
    return -ADAM_LR * (m_hat / (jnp.sqrt(v_hat) + ADAM_EPS) + ADAM_WD * w), mm, vv


def _adamw_many(ws, gs, ms, vs, name):
    k = len(ws)
    shapes = [w.shape for w in ws]
    flat = [[a.reshape(-1, a.shape[-1]) for a in group] for group in (ws, gs, ms, vs)]

    def body(*refs):
        for i in range(k):
            d, mm, vv = _adamw_update(*(refs[j * k + i][...] for j in range(4)))
            refs[4 * k + i][...] = d
            refs[5 * k + i][...] = mm
            refs[6 * k + i][...] = vv

    outs = pl.pallas_call(
        body, out_shape=[jax.ShapeDtypeStruct(a.shape, f32) for a in flat[0]] * 3, name=name,
    )(*flat[0], *flat[1], *flat[2], *flat[3])
    return tuple([outs[j * k + i].reshape(shapes[i]) for i in range(k)] for j in range(3))


def _adamw(w, g, m, v, name):
    shape = w.shape
    n = shape[-1]
    r = w.size // n
    w2, g2, m2, v2 = (a.reshape(r, n) for a in (w, g, m, v))
    tr = _pick(r, (256, 208, 128, 64, 32, 16, 8))

    def body(w_ref, g_ref, m_ref, v_ref, d_ref, mo_ref, vo_ref):
        d_ref[...], mo_ref[...], vo_ref[...] = _adamw_update(w_ref[...], g_ref[...], m_ref[...], v_ref[...])

    spec = pl.BlockSpec((tr, n), lambda i: (i, 0))
    outs = pl.pallas_call(
        body, grid=(r // tr,), in_specs=[spec] * 4, out_specs=[spec] * 3,
        out_shape=[jax.ShapeDtypeStruct((r, n), f32)] * 3, name=name,
    )(w2, g2, m2, v2)
    return tuple(o.reshape(shape) for o in outs)


_SMALL = ("shift_mu", "w_decay0", "a0", "k_k", "k_a", "r_k", "ln_x_w", "ln_x_b", "v_mix0", "lb_logits",
          "g_norm_w", "ln_w", "ln_b")
_NAMES = ("w_in", "shift_mu", "w_decay0", "w_decay_up", "a0", "a_up", "k_k", "k_a", "r_k", "ln_x_w", "ln_x_b",
          "v_mix0", "v_mix_down", "v_mix_up", "lb_logits", "g_norm_w", "w_out", "ln_w", "ln_b")


def _pad_rows(a, rows, at_end):
    z = jnp.zeros((rows - a.shape[0], a.shape[1]), a.dtype)
    return jnp.concatenate([a, z] if at_end else [z, a], axis=0)


def kernel(x, w_in, shift_mu, w_decay0, w_decay_up, a0, a_up, k_k, k_a, r_k, ln_x_w, ln_x_b, v_mix0, v_mix_down, v_mix_up, lb_logits, g_norm_w, w_out, ln_w, ln_b, loss_target, m_w_in, m_shift_mu, m_w_decay0, m_w_decay_up, m_a0, m_a_up, m_k_k, m_k_a, m_r_k, m_ln_x_w, m_ln_x_b, m_v_mix0, m_v_mix_down, m_v_mix_up, m_lb_logits, m_g_norm_w, m_w_out, m_ln_w, m_ln_b, v_w_in, v_shift_mu, v_w_decay0, v_w_decay_up, v_a0, v_a_up, v_k_k, v_k_a, v_r_k, v_ln_x_w, v_ln_x_b, v_v_mix0, v_v_mix_down, v_v_mix_up, v_lb_logits, v_g_norm_w, v_w_out, v_ln_w, v_ln_b):
    weights = dict(w_in=w_in, shift_mu=shift_mu, w_decay0=w_decay0, w_decay_up=w_decay_up, a0=a0, a_up=a_up, k_k=k_k,
                   k_a=k_a, r_k=r_k, ln_x_w=ln_x_w, ln_x_b=ln_x_b, v_mix0=v_mix0, v_mix_down=v_mix_down,
                   v_mix_up=v_mix_up, lb_logits=lb_logits, g_norm_w=g_norm_w, w_out=w_out, ln_w=ln_w, ln_b=ln_b)
    mom1 = dict(w_in=m_w_in, shift_mu=m_shift_mu, w_decay0=m_w_decay0, w_decay_up=m_w_decay_up, a0=m_a0, a_up=m_a_up,
                k_k=m_k_k, k_a=m_k_a, r_k=m_r_k, ln_x_w=m_ln_x_w, ln_x_b=m_ln_x_b, v_mix0=m_v_mix0,
                v_mix_down=m_v_mix_down, v_mix_up=m_v_mix_up, lb_logits=m_lb_logits, g_norm_w=m_g_norm_w,
                w_out=m_w_out, ln_w=m_ln_w, ln_b=m_ln_b)
    mom2 = dict(w_in=v_w_in, shift_mu=v_shift_mu, w_decay0=v_w_decay0, w_decay_up=v_w_decay_up, a0=v_a0, a_up=v_a_up,
                k_k=v_k_k, k_a=v_k_a, r_k=v_r_k, ln_x_w=v_ln_x_w, ln_x_b=v_ln_x_b, v_mix0=v_v_mix0,
                v_mix_down=v_v_mix_down, v_mix_up=v_v_mix_up, lb_logits=v_lb_logits, g_norm_w=v_g_norm_w,
                w_out=v_w_out, ln_w=v_ln_w, ln_b=v_ln_b)
    assert x.shape[0] == 1 and w_in.shape[0] == DEPTH
    t, d = x.shape[1], x.shape[2]
    dr = w_decay0.shape[1]
    dh = g_norm_w.shape[1]
    rank_w, rank_a, rank_v = w_decay_up.shape[1], a_up.shape[1], v_mix_up.shape[1]
    rwc = 4 * dr + rank_w + rank_a
    assert rank_w + rank_a == LANES and rank_v <= LANES and dr + dh == d
    assert t % CHUNK == 0 and dr % LANES == 0 and dh % LANES == 0 and shift_mu.shape[1] == rwc
    n_pair = dr // LANES
    me = _index(_position())

    shard = dr // N_DEV
    pack = jnp.concatenate([w_decay_up[0], w_decay_up[1], a_up[0], a_up[1], v_mix_up[0], v_mix_down[0].T], axis=0)
    win_t0, pack = _all_gather_rows([w_in[0].T.astype(bf16), pack], "ag_first")
    win_t = [win_t0, None]
    wout = [None, None]

    def start_gather(blocks, name, after):
        lands = [_landing_zone(blk, me, f"{name}_zone{i}") for i, blk in enumerate(blocks)]
        return _split_start(blocks, lands, _gather_plan, len(_GATHER_FLIPS), name, after=after)

    gather_wout0 = start_gather([w_out[0].astype(bf16)], "ag_wout0_start", (win_t[0], pack))
    gather_layer1 = start_gather([w_in[1].T.astype(bf16), w_out[1].astype(bf16)], "ag_layer1_start", (gather_wout0[-1],))
    pack = jnp.transpose(pack.reshape(N_DEV, -1, shard), (1, 0, 2)).reshape(-1, dr)
    offs = [0, rank_w, 2 * rank_w, 2 * rank_w + rank_a, 2 * rank_w + 2 * rank_a, 2 * rank_w + 2 * rank_a + rank_v,
            2 * rank_w + 2 * rank_a + 2 * rank_v]
    wdu_f = [pack[offs[0]:offs[1]], pack[offs[1]:offs[2]]]
    aup_f = [pack[offs[2]:offs[3]], pack[offs[3]:offs[4]]]
    vup_f = pack[offs[4]:offs[5]]
    vdown_f = pack[offs[5]:offs[6]].T

    def after_start(a, started):
        return a + started[-1][0:1, 0:1]

    def rwkv_params(l):
        mu = after_start(shift_mu[0:1], gather_layer1) if l == 0 else shift_mu[l:l + 1]
        prm = [mu, w_decay0[l:l + 1], a0[l:l + 1], _pad_rows(wdu_f[l], LANES, True),
               _pad_rows(aup_f[l], LANES, False)]
        if l == 1:
            prm += [v_mix0[0:1], _pad_rows(vdown_f.T, LANES, True).T, _pad_rows(vup_f, LANES, True)]
        rows = jnp.stack([k_k[l], k_a[l], r_k[l], ln_x_w[l], ln_x_b[l]] + [jnp.zeros((dr,), f32)] * 3, axis=0)
        pp = jnp.transpose(rows.reshape(8, n_pair, LANES), (1, 0, 2))
        return tuple(prm), pp

    h = x[0]
    h16 = h.astype(bf16)
    tgt = loss_target[0]
    saved = []
    vfirst = None
    for l in range(DEPTH):
        prm, pp = rwkv_params(l)
        proj = _matmul(h16, win_t[l], "nt", f"mm_proj_{l}", _MM_TILES["proj"])
        if l == 0:
            cat, vfirst, mck = _rwkv_fwd(False, proj, None, prm, pp, d)
        else:
            cat, mck = _rwkv_fwd(True, proj, vfirst, prm, pp, d)
        cat, sck = _hgrn_fwd(l == 1, proj, lb_logits, g_norm_w[l:l + 1], cat, rwc)
        if l == 0:
            _, arrived = _split_wait(gather_wout0, _gather_plan, cat, "ag_wout0_wait")
            (wout[0],) = _gather_forward(arrived, "ag_wout0_forward")
        y = _matmul(cat, wout[l], "nn", f"mm_out_{l}", _MM_TILES["out"])
        saved.append((h, h16, proj, prm, pp, mck, sck, cat, y))
        if l < DEPTH - 1:
            h, h16 = _ln_fwd(h, y, ln_w[l:l + 1], ln_b[l:l + 1])
            _, arrived = _split_wait(gather_layer1, _gather_plan, h16, "ag_layer1_wait")
            win_t[1], wout[1] = _gather_forward(arrived, "ag_layer1_forward")
        else:
            top = _ln_loss_bwd(h, y, ln_w[l:l + 1], ln_b[l:l + 1], tgt)
    loss = lax.psum(top[4][0, 0], ("x", "y", "c"))

    grads = {}
    big = {}
    dvfirst = None
    d_lbl = None
    rs_started = {}
    for l in reversed(range(DEPTH)):
        h_l, h16_l, proj, prm, pp, mck, sck, cat, y = saved[l]
        if l == DEPTH - 1:
            dy, dy16, g_ln_w, g_ln_b = top[:4]
        else:
            dy, dy16, g_ln_w, g_ln_b = _ln_bwd(h_l, y, after_start(ln_w[l:l + 1], rs_started[l + 1]), ln_b[l:l + 1], dh_out)
        dcat = _matmul(dy16, wout[l], "nt", f"mm_dcat_{l}", _MM_TILES["dcat"])
        big[("w_out", l)] = _matmul(cat, dy16, "tn", f"mm_dwout_{l}", _MM_TILES["dwout"], out_dtype=bf16)
        if l == 1:
            outs = _rwkv_bwd(True, proj, vfirst, prm, pp, mck, dcat, None)
            dproj_r, dvfirst = outs[0], outs[1]
            dprm, dpp = outs[2:-1], outs[-1]
        else:
            outs = _rwkv_bwd(False, proj, None, prm, pp, mck, dcat, dvfirst)
            dproj_r = outs[0]
            dprm, dpp = outs[1:-1], outs[-1]
        dproj, dlbl_l, dgnw = _hgrn_bwd(l == 1, proj, lb_logits, g_norm_w[l:l + 1], sck, dcat, rwc, dproj_r)
        big[("w_in", l)] = _matmul(dproj, h16_l, "tn", f"mm_dwin_{l}", _MM_TILES["dwin"], out_dtype=bf16)
        sharded = [dprm[3][:rank_w].T, dprm[4][rank_w:].T]
        if l == 1:
            sharded += [dprm[6][:, :rank_v], dprm[7][:rank_v].T,
                        jnp.zeros((dr, LANES - 2 * rank_v), f32)]
        sharded = jnp.concatenate(sharded, axis=1).astype(bf16)
        d2d = _reduce_scatter_begin([big[("w_in", l)], big[("w_out", l)], sharded], f"l{l}")
        if l == 0:
            rs_started[l] = _reduce_scatter_middle(d2d, sharded, f"l{l}")
            token = rs_started[l][-1]
        else:
            token = d2d[-1]
        dh_out = _matmul(dproj, win_t[l], "nn", f"mm_dh_{l}", _MM_TILES["dh"], add=dy, add_scale=ALPHA, after=token)
        if l > 0:
            rs_started[l] = _reduce_scatter_middle(d2d, dh_out, f"l{l}")
        dpp = jnp.transpose(dpp, (1, 0, 2)).reshape(8, dr)
        grads[l] = dict(shift_mu=dprm[0][0], w_decay0=dprm[1][0], a0=dprm[2][0],
                        k_k=dpp[0], k_a=dpp[1], r_k=dpp[2], ln_x_w=dpp[3], ln_x_b=dpp[4],
                        g_norm_w=dgnw[0], ln_w=g_ln_w[0], ln_b=g_ln_b[0])
        if l == 1:
            grads[l].update(v_mix0=dprm[5][0])
            d_lbl = dlbl_l
    grad_x = dh_out[None]

    def both(name):
        return jnp.stack([grads[0][name], grads[1][name]])

    small = dict(shift_mu=both("shift_mu"), w_decay0=both("w_decay0"), a0=both("a0"), k_k=both("k_k"), k_a=both("k_a"),
                 r_k=both("r_k"), ln_x_w=both("ln_x_w"), ln_x_b=both("ln_x_b"), v_mix0=grads[1]["v_mix0"][None],
                 lb_logits=d_lbl, g_norm_w=both("g_norm_w"), ln_w=both("ln_w"), ln_b=both("ln_b"))
    flat = jnp.concatenate([small[nm].reshape(-1) for nm in _SMALL])
    n_flat = flat.shape[0]
    rows = -(-n_flat // (8 * LANES)) * 8
    flat = jnp.concatenate([flat, jnp.zeros((rows * LANES - n_flat,), f32)]).reshape(rows, LANES)
    total = _sum_slots(_all_gather_rows([flat], "ag_small_grads")[0].reshape(N_DEV, rows, LANES), "sum_small_grads").reshape(-1)
    gsm = {}
    off = 0
    for nm in _SMALL:
        size = small[nm].size
        gsm[nm] = total[off:off + size].reshape(small[nm].shape)
        off += size
    reduced = {1: _reduce_scatter_end(rs_started[1], dh_out, "l1", None, 1, DEPTH)}
    reduced[0] = _reduce_scatter_end(rs_started[0], total, "l0", reduced[1][0], 0, DEPTH)
    g_w_in_t = reduced[0][0]
    gsm["w_in"] = jnp.transpose(g_w_in_t, (0, 2, 1))
    gsm["w_out"] = jnp.stack([reduced[l][1] for l in range(DEPTH)])
    gsm["w_decay_up"] = jnp.stack([reduced[l][2][:, :rank_w].T for l in range(DEPTH)])
    gsm["a_up"] = jnp.stack([reduced[l][2][:, rank_w:rank_w + rank_a].T for l in range(DEPTH)])
    gsm["v_mix_down"] = reduced[1][2][:, LANES:LANES + rank_v][None]
    gsm["v_mix_up"] = reduced[1][2][:, LANES + rank_v:LANES + 2 * rank_v].T[None]

    deltas, new_m, new_v = {}, {}, {}
    swap = lambda a: jnp.transpose(a, (0, 2, 1))
    deltas["w_in"], new_m["w_in"], new_v["w_in"] = (
        swap(a) for a in _adamw(swap(w_in), g_w_in_t, swap(m_w_in), swap(v_w_in), "adamw_w_in"))
    deltas["w_out"], new_m["w_out"], new_v["w_out"] = _adamw(w_out, gsm["w_out"], m_w_out, v_w_out, "adamw_w_out")
    rest = [nm for nm in _NAMES if nm not in ("w_in", "w_out")]
    d_rest, m_rest, v_rest = _adamw_many([weights[nm] for nm in rest], [gsm[nm] for nm in rest],
                                         [mom1[nm] for nm in rest], [mom2[nm] for nm in rest], "adamw_small")
    for i, nm in enumerate(rest):
        deltas[nm], new_m[nm], new_v[nm] = d_rest[i], m_rest[i], v_rest[i]
    return (loss, grad_x, *[gsm[nm] for nm in _NAMES], *[deltas[nm] for nm in _NAMES],
            *[new_m[nm] for nm in _NAMES], *[new_v[nm] for nm in _NAMES])
```

```python
import functools

import jax
import jax.numpy as jnp
from jax import lax
from jax.experimental import pallas as pl
from jax.experimental.pallas import tpu as pltpu

f32 = jnp.float32
bf16 = jnp.bfloat16

N_DEV = 8
CHUNK = 64
LANES = 128
RWKV_HEAD = 64
DEPTH = 2
ALPHA = (2 * DEPTH) ** 0.25
LN_EPS = 1e-5
GN_EPS = 64e-5
RMS_EPS = 1e-5
LB_FLOOR = 1e-30
ADAM_LR, ADAM_B1, ADAM_B2, ADAM_EPS, ADAM_WD, ADAM_STEP = 0.001, 0.9, 0.999, 1e-08, 0.01, 10
MESH = pl.DeviceIdType.MESH


def _iota(shape, d):
    return lax.broadcasted_iota(jnp.int32, shape, d)


_DIMS = {"nn": (((1,), (0,)), ((), ())), "nt": (((1,), (1,)), ((), ())), "tn": (((0,), (0,)), ((), ()))}
_BATCH_DIMS = {"nn": (((2,), (1,)), ((0,), (0,))), "nt": (((2,), (2,)), ((0,), (0,))), "tn": (((1,), (1,)), ((0,), (0,)))}
_K_AXES = {"nn": (-1, -2), "nt": (-1, -1), "tn": (-2, -2)}


def _mxu(a, b, mode):
    return lax.dot_general(a, b, (_BATCH_DIMS if a.ndim == 3 else _DIMS)[mode], preferred_element_type=f32)


def _split(x):
    hi = x.astype(bf16)
    return hi, (x - hi.astype(f32)).astype(bf16)


def _mm2_impl(a, b, mode, passes=3):
    if passes == 1:
        return _mxu(a.astype(bf16), b.astype(bf16), mode)
    ah, al = _split(a)
    if passes == 3:
        bh, bl = _split(b)
        lhs, rhs = [ah, ah, al], [bh, bl, bh]
    else:
        bh = b.astype(bf16)
        lhs, rhs = [ah, al], [bh, bh]
    ka, kb = _K_AXES[mode]
    k = a.shape[ka]
    if k % (LANES if -1 in (ka, kb) else 16) == 0:
        return _mxu(jnp.concatenate(lhs, axis=ka), jnp.concatenate(rhs, axis=kb), mode)
    out = _mxu(lhs[0], rhs[0], mode)
    for x, y in zip(lhs[1:], rhs[1:]):
        out = out + _mxu(x, y, mode)
    return out


@functools.partial(jax.custom_vjp, nondiff_argnums=(2, 3))
def _mm2(a, b, mode, passes=3):
    return _mm2_impl(a, b, mode, passes)


def _mm2_fwd(a, b, mode, passes):
    return _mm2_impl(a, b, mode, passes), (a, b)


def _mm2_bwd(mode, passes, res, g):
    a, b = res
    if mode == "nn":
        return _mm2_impl(g, b, "nt", passes), _mm2_impl(a, g, "tn", passes)
    if mode == "nt":
        return _mm2_impl(g, b, "nn", passes), _mm2_impl(g, a, "tn", passes)
    return _mm2_impl(b, g, "nt", passes), _mm2_impl(a, g, "nn", passes)


_mm2.defvjp(_mm2_fwd, _mm2_bwd)

TRI_PASSES = 1
APPLY_PASSES = 1


def _const_impl(cm, x, mode):
    if mode in ("r", "rt"):
        shape = x.shape
        out = _mxu(x.astype(bf16).reshape(-1, shape[-1]), cm, "nn" if mode == "r" else "nt")
        return out.reshape(shape[:-1] + (out.shape[-1],))
    hi, lo = _split(x)
    if x.ndim == 3:
        cm = jnp.broadcast_to(cm, (x.shape[0],) + cm.shape)
    return _mxu(cm, hi, mode) + _mxu(cm, lo, mode)


@jax.custom_vjp
def _const_left(cm, x):
    return _const_impl(cm, x, "nn")


_const_left.defvjp(lambda cm, x: (_const_impl(cm, x, "nn"), cm),
                   lambda cm, g: (jnp.zeros_like(cm), _const_impl(cm, g, "tn")))


@jax.custom_vjp
def _const_right(x, cm):
    return _const_impl(cm, x, "r")


_const_right.defvjp(lambda x, cm: (_const_impl(cm, x, "r"), cm),
                    lambda cm, g: (_const_impl(cm, g, "rt"), jnp.zeros_like(cm)))


def _tri_inv(a):
    n = a.shape[-1]
    tm = (_iota((n, n), 0) == _iota((n, n), 1)).astype(f32) + a
    ak = a
    for _ in range(5):
        ak = _mm2_impl(ak, ak, "nn", TRI_PASSES)
        tm = tm + _mm2_impl(tm, ak, "nn", TRI_PASSES)
    return tm


@jax.custom_vjp
def _tri_solve(tm, a, x):
    del a
    return _mm2_impl(tm, x, "nn", APPLY_PASSES)


def _tri_solve_fwd(tm, a, x):
    u = _mm2_impl(tm, x, "nn", APPLY_PASSES)
    return u, (tm, u)


def _tri_solve_bwd(res, du):
    tm, u = res
    dx = _mm2_impl(tm, du, "tn", APPLY_PASSES)
    return jnp.zeros_like(tm), _mm2_impl(dx, u, "nt", APPLY_PASSES), dx


_tri_solve.defvjp(_tri_solve_fwd, _tri_solve_bwd)


def _rows_rolled(x, shift, dead_row):
    return jnp.where(_iota((x.shape[0], 1), 0) == dead_row, 0.0, pltpu.roll(x, shift, 0))


@jax.custom_vjp
def _shift_rows_down(y):
    return _rows_rolled(y, 1, 0)


_shift_rows_down.defvjp(lambda y: (_rows_rolled(y, 1, 0), None),
                        lambda _, g: (_rows_rolled(g, g.shape[0] - 1, g.shape[0] - 1),))


def _col_of_row(row_vec):
    n = row_vec.shape[-1]
    eye = _iota((n, n), 0) == _iota((n, n), 1)
    return jnp.sum(jnp.where(eye, jnp.broadcast_to(row_vec, row_vec.shape[:-2] + (n, n)), 0.0), axis=-1, keepdims=True)


def _softplus(x):
    return jnp.maximum(x, 0.0) + jnp.log1p(jnp.exp(-jnp.abs(x)))


def _log_sigmoid(x):
    return -_softplus(-x)


def _logaddexp(a, b):
    return jnp.maximum(a, b) + jnp.log1p(jnp.exp(-jnp.abs(a - b)))


def _silu(x):
    return x * jax.nn.sigmoid(x)


def _tril(c, strict):
    r, s = _iota((c, c), 0), _iota((c, c), 1)
    return (r > s) if strict else (r >= s)


def _last_row(a):
    c = a.shape[-2]
    return jnp.sum(jnp.where(_iota(a.shape, a.ndim - 2) == c - 1, a, 0.0), axis=-2, keepdims=True)


def _rwkv_pre(layer1, prm, y, prev, vf):
    c = y.shape[0]
    if layer1:
        mu, w0, a0, wup, aup, v0, vdown, vup = prm
    else:
        mu, w0, a0, wup, aup = prm
    dr = w0.shape[1]
    y_prev = _shift_rows_down(y) + jnp.where(_iota((c, 1), 0) == 0, prev, 0.0)
    rw = y + mu * (y_prev - y)
    r, k, v, z = (rw[:, i * dr:(i + 1) * dr] for i in range(4))
    wdad = rw[:, 4 * dr:4 * dr + LANES]
    w_raw = w0 + _mm2(jnp.tanh(wdad), wup, "nn")
    lw = -jnp.exp(-_softplus(-w_raw) - 0.5)
    asig = jax.nn.sigmoid(a0 + _mm2(wdad, aup, "nn", APPLY_PASSES))
    if layer1:
        v = v + (vf - v) * jax.nn.sigmoid(v0 + _mm2(_mm2(v, vdown, "nn", APPLY_PASSES), vup, "nn", APPLY_PASSES))
    return r, k, v, z, lw, asig


def _rwkv_pair(pp, m0, xs, tm=None):
    kkw, kaw, rkw, gnw, gnb = pp
    r, k, v, z, lw, asig = xs
    c = r.shape[-2]
    n2 = 2 * c
    lane = _iota((1, LANES), 1)
    mh0, mh1 = (lane < RWKV_HEAD).astype(f32), (lane >= RWKV_HEAD).astype(f32)
    same_head = _iota((LANES, LANES), 0) // RWKV_HEAD == _iota((LANES, LANES), 1) // RWKV_HEAD
    g = same_head.astype(bf16)

    def seg(x):
        return _const_right(x, g)

    def stack(x):
        return jnp.concatenate([x * mh0, x * mh1], axis=-2)

    kk = k * kkw
    kk = kk / jnp.maximum(jnp.sqrt(seg(kk * kk)), 1e-12)
    k2 = k * (1.0 + (asig - 1.0) * kaw)
    a = -kk
    b = kk * asig
    cum = _const_left(_tril(c, False).astype(bf16), lw)
    at = stack(a * jnp.exp(cum - lw))
    rt = stack(r * jnp.exp(cum))
    en = jnp.exp(-cum)
    sc = _mm2(jnp.concatenate([at, rt], axis=-2), jnp.concatenate([stack(b * en), stack(k2 * en)], axis=-2), "nt")
    row, col = _iota((n2, n2), 0), _iota((n2, n2), 1)
    same = row // c == col // c
    strict = same & (row % c > col % c)
    incl = same & (row % c >= col % c)
    aab = jnp.where(strict, sc[..., :n2, :n2], 0.0)
    aak = jnp.where(strict, sc[..., :n2, n2:], 0.0)
    arb = jnp.where(incl, sc[..., n2:, :n2], 0.0)
    ark = jnp.where(incl, sc[..., n2:, n2:], 0.0)
    vv = jnp.concatenate([v, v], axis=-2)
    mask_st = jnp.concatenate([jnp.broadcast_to(mh0, (c, LANES)), jnp.broadcast_to(mh1, (c, LANES))], axis=0)
    x_st = _mm2(jnp.concatenate([at, aak], axis=-1), jnp.concatenate([m0, vv], axis=-2), "nn", APPLY_PASSES)
    if tm is None:
        tm = _tri_inv(lax.stop_gradient(aab))
    u_st = _tri_solve(tm, aab, x_st) * mask_st
    o_st = _mm2(jnp.concatenate([rt, arb, ark], axis=-1), jnp.concatenate([m0, u_st, vv], axis=-2), "nn", APPLY_PASSES) * mask_st
    u = u_st[..., :c, :] + u_st[..., c:, :]
    o = o_st[..., :c, :] + o_st[..., c:, :]
    cum_last = _last_row(cum)
    dec_end = jnp.exp(cum_last - cum)
    m_new = _col_of_row(jnp.exp(cum_last)) * m0 + _mm2(
        jnp.concatenate([b * dec_end, k2 * dec_end], axis=-2), jnp.concatenate([u, v], axis=-2), "tn", APPLY_PASSES) * same_head.astype(f32)
    mean = seg(o) * (1.0 / RWKV_HEAD)
    d = o - mean
    var = seg(d * d) * (1.0 / RWKV_HEAD)
    on = d * lax.rsqrt(var + GN_EPS) * gnw + gnb
    bonus = seg(r * k2 * rkw) * v
    return (on + bonus) * _silu(z), m_new, tm


def _split_lanes(a, n):
    return [a[:, i * LANES:(i + 1) * LANES] for i in range(n)]


def _rwkv_step(layer1, prm, y, prev, vf, pp, m0, tm=None):
    xs = _rwkv_pre(layer1, prm, y, prev, vf)
    n_pair = m0.shape[0]
    og, m_new, tm = _rwkv_pair(pp, m0, tuple(jnp.concatenate([p[None] for p in _split_lanes(a, n_pair)], axis=0) for a in xs), tm)
    return og, m_new, xs[2], tm


def _group(n):
    return n


def _rwkv_specs(layer1, t, dr, rwc, n_pair, rev):
    nc = t // CHUNK
    grp = _group(n_pair)

    def cidx(c):
        return (nc - 1 - c) if rev else c

    full = lambda shape: pl.BlockSpec(shape, lambda c, p: tuple(0 for _ in shape))
    specs = [
        pl.BlockSpec((CHUNK, rwc), lambda c, p: (cidx(c), 0)),
        pl.BlockSpec((8, rwc), lambda c, p: (jnp.maximum(cidx(c) * (CHUNK // 8) - 1, 0), 0)),
    ]
    if layer1:
        specs.append(pl.BlockSpec((CHUNK, dr), lambda c, p: (cidx(c), 0)))
    prm_shapes = [(1, rwc), (1, dr), (1, dr), (LANES, dr), (LANES, dr)]
    if layer1:
        prm_shapes += [(1, dr), (dr, LANES), (LANES, dr)]
    specs += [full(s) for s in prm_shapes]
    specs.append(pl.BlockSpec((grp, 8, LANES), lambda c, p: (p, 0, 0)))
    return specs, prm_shapes, cidx, full


def _rwkv_fwd(layer1, proj, vf, prm, pp, cat_width):
    t = proj.shape[0]
    dr = prm[1].shape[1]
    rwc = prm[0].shape[1]
    n_pair = dr // LANES
    nc = t // CHUNK
    n_prm = len(prm)
    specs, _, _, _ = _rwkv_specs(layer1, t, dr, rwc, n_pair, False)

    def body(*refs):
        y_ref, prev_ref = refs[0], refs[1]
        i = 2
        vf_ref = None
        if layer1:
            vf_ref = refs[i]
            i += 1
        prm_refs = refs[i:i + n_prm]
        i += n_prm
        pp_ref = refs[i]
        i += 1
        cat_ref = refs[i]
        i += 1
        vout_ref = None
        if not layer1:
            vout_ref = refs[i]
            i += 1
        mck_ref, m_s = refs[i], refs[i + 1]
        c = pl.program_id(0)

        @pl.when(c == 0)
        def _():
            m_s[...] = jnp.zeros_like(m_s)

        prev = prev_ref[pl.ds(7, 1), :] * (c != 0).astype(f32)
        m0 = m_s[...]
        ppv = tuple(pp_ref[:, pl.ds(q, 1), :] for q in range(5))
        og, m_new, v, tm = _rwkv_step(layer1, tuple(r[...] for r in prm_refs), y_ref[...], prev,
                                      vf_ref[...] if layer1 else None, ppv, m0)
        mck_ref[0, :n_pair] = m0
        mck_ref[0, n_pair:] = tm
        if not layer1:
            vout_ref[...] = v
        for j in range(n_pair):
            cat_ref[:, j * LANES:(j + 1) * LANES] = og[j]
        m_s[...] = m_new

    grp = _group(n_pair)
    assert grp == n_pair
    out_shape = [jax.ShapeDtypeStruct((t, cat_width), f32)]
    out_specs = [pl.BlockSpec((CHUNK, grp * LANES), lambda c, p: (c, p))]
    if not layer1:
        out_shape.append(jax.ShapeDtypeStruct((t, dr), f32))
        out_specs.append(pl.BlockSpec((CHUNK, dr), lambda c, p: (c, 0)))
    out_shape.append(jax.ShapeDtypeStruct((nc, 2 * n_pair, LANES, LANES), f32))
    out_specs.append(pl.BlockSpec((1, 2 * grp, LANES, LANES), lambda c, p: (c, p, 0, 0)))
    args = [proj, proj] + ([vf] if layer1 else []) + list(prm) + [pp]
    return pl.pallas_call(
        body, grid=(nc, 1), in_specs=specs, out_specs=out_specs, out_shape=out_shape,
        scratch_shapes=[pltpu.VMEM((n_pair, LANES, LANES), f32)],
        compiler_params=pltpu.CompilerParams(dimension_semantics=("arbitrary", "arbitrary")),
        name=f"rwkv_fwd_l{int(layer1)}",
    )(*args)


def _rwkv_bwd(layer1, proj, vf, prm, pp, mck, dcat, dvout):
    t = proj.shape[0]
    dr = prm[1].shape[1]
    rwc = prm[0].shape[1]
    n_pair = dr // LANES
    nc = t // CHUNK
    n_prm = len(prm)
    specs, prm_shapes, cidx, full = _rwkv_specs(layer1, t, dr, rwc, n_pair, True)
    grp = _group(n_pair)
    assert grp == n_pair
    specs.append(pl.BlockSpec((1, 2 * grp, LANES, LANES), lambda c, p: (cidx(c), p, 0, 0)))
    specs.append(pl.BlockSpec((CHUNK, grp * LANES), lambda c, p: (cidx(c), p)))
    if not layer1:
        specs.append(pl.BlockSpec((CHUNK, dr), lambda c, p: (cidx(c), 0)))

    def body(*refs):
        y_ref, prev_ref = refs[0], refs[1]
        i = 2
        vf_ref = None
        if layer1:
            vf_ref = refs[i]
            i += 1
        prm_refs = refs[i:i + n_prm]
        i += n_prm
        pp_ref, mck_ref, dog_ref = refs[i], refs[i + 1], refs[i + 2]
        i += 3
        dvout_ref = None
        if not layer1:
            dvout_ref = refs[i]
            i += 1
        dy_ref = refs[i]
        i += 1
        dvf_ref = None
        if layer1:
            dvf_ref = refs[i]
            i += 1
        dprm_refs = refs[i:i + n_prm]
        i += n_prm
        dpp_ref = refs[i]
        dm_s, dprev_s = refs[i + 1:i + 3]
        c = pl.program_id(0)
        cr = nc - 1 - c

        @pl.when(c == 0)
        def _():
            dm_s[...] = jnp.zeros_like(dm_s)
            dprev_s[...] = jnp.zeros_like(dprev_s)
            dpp_ref[...] = jnp.zeros_like(dpp_ref)
            for r in dprm_refs:
                r[...] = jnp.zeros_like(r)

        prev = prev_ref[pl.ds(7, 1), :] * (cr != 0).astype(f32)
        prm_v = tuple(r[...] for r in prm_refs)
        ppv = tuple(pp_ref[:, pl.ds(q, 1), :] for q in range(5))
        dog = jnp.stack([dog_ref[:, j * LANES:(j + 1) * LANES] for j in range(n_pair)], axis=0)
        m0, tm = mck_ref[0, :n_pair], mck_ref[0, n_pair:]
        no_tm = jnp.zeros_like(tm)
        if layer1:
            _, vjp = jax.vjp(lambda a, b, d, e, g, h: _rwkv_step(True, a, b, d, e, g, h, tm),
                             prm_v, y_ref[...], prev, vf_ref[...], ppv, m0)
            dprm, dy, dprev, dvf, dppv, dm0 = vjp((dog, dm_s[...], jnp.zeros((CHUNK, dr), f32), no_tm))
            dvf_ref[...] = dvf
        else:
            _, vjp = jax.vjp(lambda a, b, d, e, g: _rwkv_step(False, a, b, d, None, e, g, tm), prm_v, y_ref[...], prev, ppv, m0)
            dprm, dy, dprev, dppv, dm0 = vjp((dog, dm_s[...], dvout_ref[...], no_tm))
        dm_s[...] = dm0
        for q in range(5):
            dpp_ref[:, pl.ds(q, 1), :] += dppv[q]
        dy_ref[...] = (dy + jnp.where(_iota((CHUNK, 1), 0) == CHUNK - 1, dprev_s[...], 0.0)).astype(bf16)
        dprev_s[...] = dprev
        for r, gval in zip(dprm_refs, dprm):
            r[...] += gval

    out_shape = [jax.ShapeDtypeStruct((t, proj.shape[1]), bf16)]
    out_specs = [pl.BlockSpec((CHUNK, rwc), lambda c, p: (cidx(c), 0))]
    if layer1:
        out_shape.append(jax.ShapeDtypeStruct((t, dr), f32))
        out_specs.append(pl.BlockSpec((CHUNK, dr), lambda c, p: (cidx(c), 0)))
    out_shape += [jax.ShapeDtypeStruct(s, f32) for s in prm_shapes]
    out_specs += [full(s) for s in prm_shapes]
    out_shape.append(jax.ShapeDtypeStruct((n_pair, 8, LANES), f32))
    out_specs.append(full((n_pair, 8, LANES)))
    args = [proj, proj] + ([vf] if layer1 else []) + list(prm) + [pp, mck, dcat] + ([] if layer1 else [dvout])
    return pl.pallas_call(
        body, grid=(nc, 1), in_specs=specs, out_specs=out_specs, out_shape=out_shape,
        scratch_shapes=[pltpu.VMEM((n_pair, LANES, LANES), f32), pltpu.VMEM((1, rwc), f32)],
        compiler_params=pltpu.CompilerParams(dimension_semantics=("arbitrary", "arbitrary")),
        name=f"rwkv_bwd_l{int(layer1)}",
    )(*args)


def _hgrn_chunk(layer1, lbl, gnw, s0, q_raw, f_raw, i_in, z):
    c = q_raw.shape[-2]
    q = _silu(q_raw)
    ls = _log_sigmoid(f_raw)
    if layer1:
        l0, l1 = lbl[..., 0:1, :], lbl[..., 1:2, :]
        mx = jnp.maximum(l0, l1)
        e0, e1 = jnp.exp(l0 - mx), jnp.exp(l1 - mx)
        sm0, sm1 = e0 / (e0 + e1), e1 / (e0 + e1)
        lb = (sm0 + sm1) - sm0
        log_f = _logaddexp(jnp.log(jnp.maximum(lb, LB_FLOOR)), jnp.log1p(-lb) + ls)
        k = (1.0 - lb) * jax.nn.sigmoid(-f_raw)
    else:
        log_f = _logaddexp(jnp.full_like(ls, jnp.log(jnp.float32(LB_FLOOR))), ls)
        k = jax.nn.sigmoid(-f_raw)
    row, col = _iota((c, c), 0), _iota((c, c), 1)
    trow = _iota((c, 1), 0)
    halves = []
    half = c // 2
    while half >= 1:
        halves.append(half)
        half //= 2
    cmat = jnp.concatenate([(col <= row).astype(f32)]
                           + [(col <= (row // (2 * hf)) * (2 * hf) + hf - 1).astype(f32) for hf in halves], axis=0)
    ball = _const_left(cmat.astype(bf16), log_f)
    b = ball[..., :c, :]
    att = None
    for lvl, hf in enumerate(halves):
        blk = 2 * hf
        bref = ball[..., (lvl + 1) * c:(lvl + 2) * c, :]
        upper = (trow % blk) >= hf
        dec = jnp.exp(jnp.where(upper, b - bref, bref - b))
        qh = jnp.where(upper, q * dec, 0.0)
        kh = jnp.where(upper, 0.0, k * dec)
        term = jnp.where(row // blk == col // blk, _mm2(qh, kh, "nt", APPLY_PASSES), 0.0)
        att = term if att is None else att + term
    lhs = jnp.concatenate([q * jnp.exp(b), att, jnp.zeros(att.shape[:-1] + (LANES - c,), f32)], axis=-1)
    rhs = jnp.concatenate([s0, i_in, jnp.zeros(i_in.shape[:-2] + (LANES - c, i_in.shape[-1]), f32)], axis=-2)
    o = _mm2(lhs, rhs, "nn", APPLY_PASSES) + jnp.sum(q * k, axis=-1, keepdims=True) * i_in
    b_last = _last_row(b)
    s_new = _col_of_row(jnp.exp(b_last)) * s0 + _mm2(k * jnp.exp(b_last - b), i_in, "tn", APPLY_PASSES)
    o = o * lax.rsqrt(jnp.mean(o * o, axis=-1, keepdims=True) + RMS_EPS)
    return o * gnw * _silu(z), s_new


def _hgrn_in_specs(t, dh, col0, rev):
    nc = t // CHUNK
    nh = dh // LANES

    def cidx(c):
        return (nc - 1 - c) if rev else c

    grp = _group(nh)
    specs = [pl.BlockSpec((CHUNK, LANES), functools.partial(lambda g, j, h, c: (cidx(c), col0 + g * nh + h * grp + j), g, j))
             for j in range(grp) for g in range(4)]
    specs.append(pl.BlockSpec((2, grp * LANES), lambda h, c: (0, h)))
    specs.append(pl.BlockSpec((1, grp * LANES), lambda h, c: (0, h)))
    return specs, cidx, grp


def _hgrn_fwd(layer1, proj, lbl, gnw, cat, rwc):
    t, d = cat.shape
    dh = gnw.shape[1]
    nh = dh // LANES
    nc = t // CHUNK
    col0 = rwc // LANES
    specs, _, grp = _hgrn_in_specs(t, dh, col0, False)
    specs.append(pl.BlockSpec(memory_space=pl.ANY))
    assert (d - dh) % (grp * LANES) == 0
    cat_col0 = (d - dh) // (grp * LANES)

    def body(*refs):
        x_refs = refs[:4 * grp]
        lbl_ref, gnw_ref, _, cat_ref, sck_ref, s_s = refs[4 * grp:]
        c = pl.program_id(1)

        @pl.when(c == 0)
        def _():
            s_s[...] = jnp.zeros_like(s_s)

        lanes = [slice(j * LANES, (j + 1) * LANES) for j in range(grp)]
        s0 = s_s[...]
        sck_ref[:, 0] = s0
        out, s_new = _hgrn_chunk(layer1, jnp.stack([lbl_ref[:, ln] for ln in lanes]), jnp.stack([gnw_ref[:, ln] for ln in lanes]),
                                 s0, *(jnp.stack([x_refs[4 * j + g][...] for j in range(grp)]) for g in range(4)))
        for j in range(grp):
            cat_ref[:, lanes[j]] = out[j]
        s_s[...] = s_new

    return pl.pallas_call(
        body, grid=(nh // grp, nc), in_specs=specs,
        out_specs=[pl.BlockSpec((CHUNK, grp * LANES), lambda h, c: (c, cat_col0 + h)),
                   pl.BlockSpec((grp, 1, LANES, LANES), lambda h, c: (h, c, 0, 0))],
        out_shape=[jax.ShapeDtypeStruct((t, d), f32), jax.ShapeDtypeStruct((nh, nc, LANES, LANES), f32)],
        scratch_shapes=[pltpu.VMEM((grp, LANES, LANES), f32)],
        input_output_aliases={4 * grp + 2: 0},
        compiler_params=pltpu.CompilerParams(dimension_semantics=("arbitrary", "arbitrary")),
        name=f"hgrn_fwd_l{int(layer1)}",
    )(*([proj] * (4 * grp)), lbl, gnw, cat)


def _hgrn_bwd(layer1, proj, lbl, gnw, sck, dcat, rwc, dproj):
    t, d = dcat.shape
    dh = gnw.shape[1]
    nh = dh // LANES
    nc = t // CHUNK
    col0 = rwc // LANES
    specs, cidx, grp = _hgrn_in_specs(t, dh, col0, True)
    assert grp == nh and (d - dh) % (grp * LANES) == 0
    cat_col0 = (d - dh) // (grp * LANES)
    specs.append(pl.BlockSpec((grp, 1, LANES, LANES), lambda h, c: (h, cidx(c), 0, 0)))
    specs.append(pl.BlockSpec((CHUNK, grp * LANES), lambda h, c: (cidx(c), cat_col0 + h)))
    specs.append(pl.BlockSpec(memory_space=pl.ANY))

    def body(*refs):
        x_refs = refs[:4 * grp]
        lbl_ref, gnw_ref, sck_ref, do_ref, _, dp_hbm, dlbl_ref, dgnw_ref, ds_s, stage, sems = refs[4 * grp:]
        c = pl.program_id(1)
        slot = c % 2

        def put(s, g, chunk):
            return pltpu.make_async_copy(stage.at[s, g], dp_hbm.at[pl.ds(chunk * CHUNK, CHUNK), pl.ds(rwc + g * dh, dh)],
                                         sems.at[s, g])

        @pl.when(c == 0)
        def _():
            ds_s[...] = jnp.zeros_like(ds_s)
            dlbl_ref[...] = jnp.zeros_like(dlbl_ref)
            dgnw_ref[...] = jnp.zeros_like(dgnw_ref)

        @pl.when(c >= 2)
        def _():
            for g in range(4):
                put(slot, g, 0).wait()

        lanes = [slice(j * LANES, (j + 1) * LANES) for j in range(grp)]
        _, vjp = jax.vjp(functools.partial(_hgrn_chunk, layer1),
                         jnp.stack([lbl_ref[:, ln] for ln in lanes]), jnp.stack([gnw_ref[:, ln] for ln in lanes]), sck_ref[:, 0],
                         *(jnp.stack([x_refs[4 * j + g][...] for j in range(grp)]) for g in range(4)))
        dlbl, dgnw, ds0, dq, df, di, dz = vjp((jnp.stack([do_ref[:, ln] for ln in lanes]), ds_s[...]))
        ds_s[...] = ds0
        for j in range(grp):
            dlbl_ref[:, lanes[j]] += dlbl[j]
            dgnw_ref[:, lanes[j]] += dgnw[j]
            for g, val in enumerate((dq, df, di, dz)):
                stage[slot, g, :, lanes[j]] = val[j].astype(bf16)
        for g in range(4):
            put(slot, g, nc - 1 - c).start()

        @pl.when(c == nc - 1)
        def _():
            for g in range(4):
                put(slot, g, 0).wait()
                if nc >= 2:
                    put(1 - slot, g, 0).wait()

    return pl.pallas_call(
        body, grid=(1, nc), in_specs=specs,
        out_specs=[pl.BlockSpec(memory_space=pl.ANY),
                   pl.BlockSpec((2, grp * LANES), lambda h, c: (0, h)),
                   pl.BlockSpec((1, grp * LANES), lambda h, c: (0, h))],
        out_shape=[jax.ShapeDtypeStruct(dproj.shape, dproj.dtype), jax.ShapeDtypeStruct((2, dh), f32),
                   jax.ShapeDtypeStruct((1, dh), f32)],
        scratch_shapes=[pltpu.VMEM((grp, LANES, LANES), f32), pltpu.VMEM((2, 4, CHUNK, dh), bf16),
                        pltpu.SemaphoreType.DMA((2, 4))],
        input_output_aliases={4 * grp + 4: 0},
        compiler_params=pltpu.CompilerParams(dimension_semantics=("arbitrary", "arbitrary")),
        name=f"hgrn_bwd_l{int(layer1)}",
    )(*([proj] * (4 * grp)), lbl, gnw, sck, dcat, dproj)


def _ln(h, y, w, b):
    u = ALPHA * h + y
    mu = jnp.mean(u, axis=-1, keepdims=True)
    var = jnp.mean(jnp.square(u - mu), axis=-1, keepdims=True)
    return (u - mu) * lax.rsqrt(var + LN_EPS) * w + b


def _row_tile(t):
    return 256 if t % 256 == 0 else t


def _ln_fwd(h, y, w, b):
    t, d = h.shape
    tr = _row_tile(t)

    def body(h_ref, y_ref, w_ref, b_ref, o_ref, o16_ref):
        out = _ln(h_ref[...], y_ref[...], w_ref[...], b_ref[...])
        o_ref[...] = out
        o16_ref[...] = out.astype(bf16)

    row = pl.BlockSpec((tr, d), lambda i: (i, 0))
    vec = pl.BlockSpec((1, d), lambda i: (0, 0))
    return pl.pallas_call(body, grid=(t // tr,), in_specs=[row, row, vec, vec], out_specs=[row, row],
                          out_shape=[jax.ShapeDtypeStruct((t, d), f32), jax.ShapeDtypeStruct((t, d), bf16)],
                          name="ln_fwd")(h, y, w, b)


def _ln_loss_bwd(h, y, w, b, tgt):
    t, d = h.shape
    tr = _row_tile(t)

    def body(h_ref, y_ref, w_ref, b_ref, t_ref, dy_ref, dy16_ref, dw_ref, db_ref, loss_ref):
        @pl.when(pl.program_id(0) == 0)
        def _():
            dw_ref[...] = jnp.zeros_like(dw_ref)
            db_ref[...] = jnp.zeros_like(db_ref)
            loss_ref[...] = jnp.zeros_like(loss_ref)

        out, vjp = jax.vjp(lambda yy, ww, bb: _ln(h_ref[...], yy, ww, bb), y_ref[...], w_ref[...], b_ref[...])
        err = out - t_ref[...]
        loss_ref[...] += 0.5 * jnp.sum(jnp.mean(jnp.square(err), axis=-1, keepdims=True), axis=0, keepdims=True)
        dy, dw, db = vjp(err * (1.0 / d))
        dy_ref[...] = dy
        dy16_ref[...] = dy.astype(bf16)
        dw_ref[...] += dw
        db_ref[...] += db

    row = pl.BlockSpec((tr, d), lambda i: (i, 0))
    vec = pl.BlockSpec((1, d), lambda i: (0, 0))
    return pl.pallas_call(
        body, grid=(t // tr,), in_specs=[row, row, vec, vec, row],
        out_specs=[row, row, vec, vec, pl.BlockSpec((1, LANES), lambda i: (0, 0))],
        out_shape=[jax.ShapeDtypeStruct((t, d), f32), jax.ShapeDtypeStruct((t, d), bf16), jax.ShapeDtypeStruct((1, d), f32),
                   jax.ShapeDtypeStruct((1, d), f32), jax.ShapeDtypeStruct((1, LANES), f32)],
        compiler_params=pltpu.CompilerParams(dimension_semantics=("arbitrary",)), name="ln_loss_bwd")(h, y, w, b, tgt)


def _ln_bwd(h, y, w, b, dout):
    t, d = h.shape
    tr = _row_tile(t)

    def body(h_ref, y_ref, w_ref, b_ref, do_ref, dy_ref, dy16_ref, dw_ref, db_ref):
        @pl.when(pl.program_id(0) == 0)
        def _():
            dw_ref[...] = jnp.zeros_like(dw_ref)
            db_ref[...] = jnp.zeros_like(db_ref)

        _, vjp = jax.vjp(lambda yy, ww, bb: _ln(h_ref[...], yy, ww, bb), y_ref[...], w_ref[...], b_ref[...])
        dy, dw, db = vjp(do_ref[...])
        dy_ref[...] = dy
        dy16_ref[...] = dy.astype(bf16)
        dw_ref[...] += dw
        db_ref[...] += db

    row = pl.BlockSpec((tr, d), lambda i: (i, 0))
    vec = pl.BlockSpec((1, d), lambda i: (0, 0))
    return pl.pallas_call(
        body, grid=(t // tr,), in_specs=[row, row, vec, vec, row], out_specs=[row, row, vec, vec],
        out_shape=[jax.ShapeDtypeStruct((t, d), f32), jax.ShapeDtypeStruct((t, d), bf16),
                   jax.ShapeDtypeStruct((1, d), f32), jax.ShapeDtypeStruct((1, d), f32)],
        compiler_params=pltpu.CompilerParams(dimension_semantics=("arbitrary",)), name="ln_bwd")(h, y, w, b, dout)


def _pick(n, prefs):
    for p in prefs:
        if n % p == 0:
            return p
    return n


def _tile(n, want):
    if n <= want:
        return n
    for cand in range(want - want % LANES, 0, -LANES):
        if n % cand == 0:
            return cand
    return n


_MM_TILES = {"proj": (1024, 1664, 2048), "out": (1024, 1024, 2048), "dcat": (1024, 1024, 2048),
             "dwout": (512, 2048, 2048), "dwin": (640, 2048, 2048), "dh": (1024, 1024, 1664)}


def _matmul(a, b, mode, name, tiles, add=None, add_scale=1.0, out_dtype=f32, after=None):
    if mode == "nn":
        (m, k), n = a.shape, b.shape[1]
    elif mode == "nt":
        (m, k), n = a.shape, b.shape[0]
    else:
        (k, m), n = a.shape, b.shape[1]
    tm, tn, tk = _tile(m, tiles[0]), _tile(n, tiles[1]), _tile(k, tiles[2])
    nk = k // tk
    cache_a = nk == 1 and a.dtype != bf16 and n // tn > 1

    def body(*refs):
        a_ref, b_ref = refs[0], refs[1]
        add_ref = refs[2] if add is not None else None
        n_in = 2 + (add is not None) + (after is not None)
        o_ref = refs[n_in]
        scratch = refs[n_in + 1:]

        def finish(res):
            if add is not None:
                res = res + add_scale * add_ref[...]
            o_ref[...] = res.astype(out_dtype)

        if cache_a:
            a_bf = scratch[0]

            @pl.when(pl.program_id(1) == 0)
            def _():
                a_bf[...] = a_ref[...].astype(bf16)

            a_val = a_bf[...]
        else:
            a_val = a_ref[...].astype(bf16)
        prod = lax.dot_general(a_val, b_ref[...].astype(bf16), _DIMS[mode], preferred_element_type=f32)
        if nk == 1:
            finish(prod)
        else:
            acc = scratch[-1]
            kk = pl.program_id(2)

            @pl.when(kk == 0)
            def _():
                acc[...] = prod

            @pl.when(kk != 0)
            def _():
                acc[...] += prod

            @pl.when(kk == nk - 1)
            def _():
                finish(acc[...])

    a_shape = (tk, tm) if mode == "tn" else (tm, tk)
    a_spec = pl.BlockSpec(a_shape, (lambda i, j, kk: (kk, i)) if mode == "tn" else (lambda i, j, kk: (i, kk)))
    b_spec = pl.BlockSpec((tn, tk), lambda i, j, kk: (j, kk)) if mode == "nt" else pl.BlockSpec((tk, tn), lambda i, j, kk: (kk, j))
    o_spec = pl.BlockSpec((tm, tn), lambda i, j, kk: (i, j))
    in_specs = [a_spec, b_spec] + ([o_spec] if add is not None else []) + ([pl.BlockSpec(memory_space=pl.ANY)] if after is not None else [])
    args = [a, b] + ([add] if add is not None else []) + ([after] if after is not None else [])
    scratch_shapes = ([pltpu.VMEM(a_shape, bf16)] if cache_a else []) + ([pltpu.VMEM((tm, tn), f32)] if nk > 1 else [])
    return pl.pallas_call(
        body, grid=(m // tm, n // tn, nk), in_specs=in_specs, out_specs=o_spec,
        out_shape=jax.ShapeDtypeStruct((m, n), out_dtype), scratch_shapes=scratch_shapes,
        compiler_params=pltpu.CompilerParams(dimension_semantics=("parallel", "arbitrary", "arbitrary")),
        name=name,
    )(*args)


def _position():
    return lax.axis_index("x"), lax.axis_index("y"), lax.axis_index("c")


def _flip(pos, k):
    x, y, c = pos
    return (1 - x if k & 4 else x, 1 - y if k & 2 else y, 1 - c if k & 1 else c)


def _index(pos):
    return 4 * pos[0] + 2 * pos[1] + pos[2]


def _all_gather_rows(xs, name):
    n_arr = len(xs)
    chips = (2, 4, 6)

    def body(*refs):
        x_refs, out_refs = refs[:n_arr], refs[n_arr:2 * n_arr]
        send_sems, recv_sems, local_sems = refs[2 * n_arr:]
        me = _position()
        sibling = _flip(me, 1)

        def copy(i, sem, block, to, own=False):
            m_per = x_refs[i].shape[0]
            rows = out_refs[i].at[pl.ds(_index(block) * m_per, m_per), :]
            return pltpu.make_async_remote_copy(
                src_ref=x_refs[i] if own else rows, dst_ref=rows,
                send_sem=send_sems.at[7 * i + sem], recv_sem=recv_sems.at[7 * i + sem], device_id=to, device_id_type=MESH)

        mine = [pltpu.make_async_copy(x_refs[i], out_refs[i].at[pl.ds(_index(me) * x_refs[i].shape[0], x_refs[i].shape[0]), :],
                                      local_sems.at[i]) for i in range(n_arr)]
        first, passed = [], []
        for i in range(n_arr):
            first.append(copy(i, 0, me, sibling, own=True))
            first += [copy(i, 1 + j, me, _flip(me, k), own=True) for j, k in enumerate(chips)]
            passed.append([copy(i, 4 + j, _flip(me, k), sibling) for j, k in enumerate(chips)])
        for cp in mine + first:
            cp.start()
        for i in range(n_arr):
            for j, k in enumerate(chips):
                copy(i, 1 + j, _flip(me, k), me).wait_recv()
                passed[i][j].start()
        for i in range(n_arr):
            copy(i, 0, sibling, me).wait_recv()
            for j, k in enumerate(chips):
                copy(i, 4 + j, _flip(sibling, k), me).wait_recv()
        for cp in first + [cp for group in passed for cp in group]:
            cp.wait_send()
        for cp in mine:
            cp.wait()

    anyspec = pl.BlockSpec(memory_space=pl.ANY)
    return pl.pallas_call(
        body, out_shape=[jax.ShapeDtypeStruct((N_DEV * x.shape[0], x.shape[1]), x.dtype) for x in xs],
        in_specs=[anyspec] * n_arr, out_specs=[anyspec] * n_arr,
        scratch_shapes=[pltpu.SemaphoreType.DMA((7 * n_arr,)), pltpu.SemaphoreType.DMA((7 * n_arr,)),
                        pltpu.SemaphoreType.DMA((n_arr,))],
        name=name,
    )(*xs)


def _split_start(srcs, lands, plan, n_copies, name, after=()):
    n_arr = len(srcs)
    n_after = len(after)
    hbm = pl.BlockSpec(memory_space=pltpu.HBM)
    sem = pl.BlockSpec(memory_space=pltpu.SEMAPHORE)

    def body(*refs):
        src_refs, land_refs = refs[:n_arr], refs[n_arr:2 * n_arr]
        outs_at = 2 * n_arr + n_after
        send_sems, recv_sems = refs[outs_at:outs_at + n_arr], refs[outs_at + n_arr:outs_at + 2 * n_arr]
        token = refs[-1]
        me = _position()
        for i in range(n_arr):
            for j, (src, dst, peer, _) in enumerate(plan(i, src_refs[i], land_refs[i], me)):
                pltpu.make_async_remote_copy(src_ref=src, dst_ref=dst, send_sem=send_sems[i].at[j], recv_sem=recv_sems[i].at[j],
                                             device_id=peer, device_id_type=MESH).start()
        token[...] = jnp.zeros_like(token)

    outs = pl.pallas_call(
        body, name=name,
        out_shape=([pltpu.SemaphoreType.DMA((n_copies,))] * (2 * n_arr)
                   + [pltpu.HBM(a.shape, a.dtype) for a in list(srcs) + list(lands)]
                   + [jax.ShapeDtypeStruct((8, LANES), f32)]),
        in_specs=[hbm] * (2 * n_arr) + [pl.BlockSpec(memory_space=pl.ANY)] * n_after,
        out_specs=[sem] * (2 * n_arr) + [hbm] * (2 * n_arr) + [pl.BlockSpec(memory_space=pltpu.VMEM)],
        input_output_aliases={i: 2 * n_arr + i for i in range(2 * n_arr)},
        compiler_params=pltpu.CompilerParams(has_side_effects=pltpu.SideEffectType.DATAFLOW_SIDE_EFFECTING),
    )(*[pltpu.with_memory_space_constraint(a, pltpu.HBM) for a in list(srcs) + list(lands)], *after)
    return (outs[:n_arr], outs[n_arr:2 * n_arr], outs[2 * n_arr:3 * n_arr], outs[3 * n_arr:4 * n_arr], outs[-1])


def _split_wait(started, plan, after, name):
    send_sems, recv_sems, srcs, lands, _ = started
    n_arr = len(srcs)
    hbm = pl.BlockSpec(memory_space=pltpu.HBM)
    sem = pl.BlockSpec(memory_space=pltpu.SEMAPHORE)

    def body(*refs):
        src_refs, land_refs = refs[:n_arr], refs[n_arr:2 * n_arr]
        s_sems, r_sems = refs[2 * n_arr:3 * n_arr], refs[3 * n_arr:4 * n_arr]
        me = _position()
        for i in range(n_arr):
            for j, (src, _, peer, arrival) in enumerate(plan(i, src_refs[i], land_refs[i], me)):
                cp = pltpu.make_async_remote_copy(src_ref=src, dst_ref=arrival, send_sem=s_sems[i].at[j], recv_sem=r_sems[i].at[j],
                                                  device_id=peer, device_id_type=MESH)
                cp.wait_send()
                cp.wait_recv()

    outs = pl.pallas_call(
        body, name=name,
        out_shape=[pltpu.HBM(a.shape, a.dtype) for a in list(srcs) + list(lands)],
        in_specs=[hbm] * (2 * n_arr) + [sem] * (2 * n_arr) + [pl.BlockSpec(memory_space=pl.ANY)],
        out_specs=[hbm] * (2 * n_arr),
        input_output_aliases={i: i for i in range(2 * n_arr)},
        compiler_params=pltpu.CompilerParams(has_side_effects=pltpu.SideEffectType.DATAFLOW_SIDE_EFFECTING),
    )(*srcs, *lands, *send_sems, *recv_sems, after)
    return outs[:n_arr], outs[n_arr:]


def _landing_zone(blk, me, name):
    m, n = blk.shape

    def body(me_ref, x_ref, o_ref):
        del me_ref
        o_ref[...] = x_ref[...]

    return pl.pallas_call(
        body,
        grid_spec=pltpu.PrefetchScalarGridSpec(
            num_scalar_prefetch=1, grid=(1,),
            in_specs=[pl.BlockSpec((m, n), lambda i, me_ref: (0, 0))],
            out_specs=pl.BlockSpec((m, n), lambda i, me_ref: (me_ref[0], 0))),
        out_shape=jax.ShapeDtypeStruct((N_DEV * m, n), blk.dtype), name=name,
    )(jnp.reshape(me, (1,)).astype(jnp.int32), blk)


_GATHER_FLIPS = (1, 2, 4, 6)


def _gather_plan(i, src_ref, land_ref, me):
    m = src_ref.shape[0]

    def rows(pos):
        return land_ref.at[pl.ds(_index(pos) * m, m), :]

    return [(src_ref, rows(me), _flip(me, k), rows(_flip(me, k))) for k in _GATHER_FLIPS]


def _gather_forward(lands, name):
    n_arr = len(lands)
    chips = (2, 4, 6)

    def body(*refs):
        out_refs = refs[n_arr:2 * n_arr]
        send_sems, recv_sems = refs[2 * n_arr:]
        me = _position()
        sibling = _flip(me, 1)
        sends, arrivals = [], []
        for i, out_ref in enumerate(out_refs):
            m = out_ref.shape[0] // N_DEV

            def copy(pos, j):
                blk = out_ref.at[pl.ds(_index(pos) * m, m), :]
                return pltpu.make_async_remote_copy(src_ref=blk, dst_ref=blk, send_sem=send_sems.at[3 * i + j],
                                                    recv_sem=recv_sems.at[3 * i + j], device_id=sibling, device_id_type=MESH)

            for j, k in enumerate(chips):
                sends.append(copy(_flip(me, k), j))
                arrivals.append(copy(_flip(sibling, k), j))
        for cp in sends:
            cp.start()
        for cp in arrivals:
            cp.wait_recv()
        for cp in sends:
            cp.wait_send()

    anyspec = pl.BlockSpec(memory_space=pl.ANY)
    return pl.pallas_call(
        body, out_shape=[jax.ShapeDtypeStruct(a.shape, a.dtype) for a in lands],
        in_specs=[anyspec] * n_arr, out_specs=[anyspec] * n_arr, input_output_aliases={i: i for i in range(n_arr)},
        scratch_shapes=[pltpu.SemaphoreType.DMA((3 * n_arr,))] * 2, name=name,
    )(*lands)


def _chips_plan(i, src_ref, land_ref, me):
    m = src_ref.shape[0] // 4
    plan = []
    for j, k in enumerate((2, 4, 6)):
        peer = _flip(me, k)
        plan.append((src_ref.at[pl.ds((2 * peer[0] + peer[1]) * m, m), :], land_ref.at[j], peer, land_ref.at[j]))
    return plan


def _sibling_plan(i, src_ref, land_ref, me):
    m = src_ref.shape[0] // N_DEV
    sibling = _flip(me, 1)
    return [(src_ref.at[pl.ds((2 * q + 1 - me[2]) * m, m), :], land_ref.at[q], sibling, land_ref.at[q]) for q in range(4)]


def _sum_with_sibling(g, recv, name):
    m = g.shape[0] // N_DEV
    n = g.shape[1]
    tr = _pick(m, (208, 128, 64, 32, 16))
    nt = m // tr

    def body(c_ref, g_ref, r_ref, o_ref):
        del c_ref
        o_ref[...] = (g_ref[0, 0].astype(f32) + r_ref[0].astype(f32)).astype(o_ref.dtype)

    return pl.pallas_call(
        body,
        grid_spec=pltpu.PrefetchScalarGridSpec(
            num_scalar_prefetch=1, grid=(4, nt),
            in_specs=[pl.BlockSpec((1, 1, tr, n), lambda q, i, c_ref: (q, c_ref[0], i, 0)),
                      pl.BlockSpec((1, tr, n), lambda q, i, c_ref: (q, i, 0))],
            out_specs=pl.BlockSpec((tr, n), lambda q, i, c_ref: (q * nt + i, 0))),
        out_shape=jax.ShapeDtypeStruct((4 * m, n), bf16), name=name,
    )(jnp.reshape(lax.axis_index("c"), (1,)).astype(jnp.int32), g.reshape(4, 2, m, n), recv)


def _sum_with_chips(h, recv, name, slot=0, n_slots=1, into=None):
    m = h.shape[0] // 4
    n = h.shape[1]
    tr = _pick(m, (208, 128, 64, 32, 16))

    def body(h_ref, r_ref, *rest):
        o_ref = rest[-1]
        my_q = 2 * lax.axis_index("x") + lax.axis_index("y")
        own = h_ref[0].astype(f32)
        for q in range(1, 4):
            own = jnp.where(my_q == q, h_ref[q].astype(f32), own)
        o_ref[0] = ((own + r_ref[0].astype(f32)) + r_ref[1].astype(f32)) + r_ref[2].astype(f32)

    in_specs = [pl.BlockSpec((4, tr, n), lambda i: (0, i, 0)), pl.BlockSpec((3, tr, n), lambda i: (0, i, 0))]
    args = [h.reshape(4, m, n), recv]
    if into is not None:
        in_specs.append(pl.BlockSpec(memory_space=pl.ANY))
        args.append(into)
    return pl.pallas_call(
        body, grid=(m // tr,), in_specs=in_specs,
        out_specs=pl.BlockSpec((1, tr, n), lambda i: (slot, i, 0)), out_shape=jax.ShapeDtypeStruct((n_slots, m, n), f32),
        input_output_aliases={2: 0} if into is not None else {}, name=name,
    )(*args)


def _sum_slots(parts, name):
    n_slot, m, n = parts.shape
    tr = _pick(m, (208, 128, 64, 32, 16, 8))

    def body(p_ref, o_ref):
        acc = p_ref[0]
        for s in range(1, n_slot):
            acc = acc + p_ref[s]
        o_ref[...] = acc

    return pl.pallas_call(
        body, grid=(m // tr,), in_specs=[pl.BlockSpec((n_slot, tr, n), lambda i: (0, i, 0))],
        out_specs=pl.BlockSpec((tr, n), lambda i: (i, 0)), out_shape=jax.ShapeDtypeStruct((m, n), parts.dtype), name=name,
    )(parts)


def _reduce_scatter_begin(gs, name):
    lands = [lax.empty((4, g.shape[0] // N_DEV, g.shape[1]), g.dtype) for g in gs]
    return _split_start(gs, lands, _sibling_plan, 4, "rs_d2d_start_" + name)


def _reduce_scatter_middle(started, after, name):
    gs, from_sibling = _split_wait(started, _sibling_plan, after, "rs_d2d_wait_" + name)
    chip_sums = [_sum_with_sibling(g, r, f"rs_sum2_{name}_{i}") for i, (g, r) in enumerate(zip(gs, from_sibling))]
    lands = [lax.empty((3, h.shape[0] // 4, h.shape[1]), h.dtype) for h in chip_sums]
    return _split_start(chip_sums, lands, _chips_plan, 3, "rs_ici_start_" + name)


def _reduce_scatter_end(started, after, name, first_into=None, slot=0, n_slots=1):
    chip_sums, from_chips = _split_wait(started, _chips_plan, after, "rs_ici_wait_" + name)
    out = []
    for i, (h, r) in enumerate(zip(chip_sums, from_chips)):
        if i == 0:
            out.append(_sum_with_chips(h, r, f"rs_sum4_{name}_{i}", slot, n_slots, first_into))
        else:
            out.append(_sum_with_chips(h, r, f"rs_sum4_{name}_{i}")[0])
    return out


def _adamw_update(w, g, m, v):
    mm = ADAM_B1 * m + (1.0 - ADAM_B1) * g
    vv = ADAM_B2 * v + (1.0 - ADAM_B2) * jnp.square(g)
    m_hat = mm / (1.0 - ADAM_B1 ** ADAM_STEP)
    v_hat = vv / (1.0 - ADAM_B2 ** ADAM_STEP)
    return -ADAM_LR * (m_hat / (jnp.sqrt(v_hat) + ADAM_EPS) + ADAM_WD * w), mm, vv


def _adamw_many(ws, gs, ms, vs, name):
    k = len(ws)
    shapes = [w.shape for w in ws]
    flat = [[a.reshape(-1, a.shape[-1]) for a in group] for group in (ws, gs, ms, vs)]

    def body(*refs):
        for i in range(k):
            d, mm, vv = _adamw_update(*(refs[j * k + i][...] for j in range(4)))
            refs[4 * k + i][...] = d
            refs[5 * k + i][...] = mm
            refs[6 * k + i][...] = vv

    outs = pl.pallas_call(
        body, out_shape=[jax.ShapeDtypeStruct(a.shape, f32) for a in flat[0]] * 3, name=name,
    )(*flat[0], *flat[1], *flat[2], *flat[3])
    return tuple([outs[j * k + i].reshape(shapes[i]) for i in range(k)] for j in range(3))


def _adamw(w, g, m, v, name):
    shape = w.shape
    n = shape[-1]
    r = w.size // n
    w2, g2, m2, v2 = (a.reshape(r, n) for a in (w, g, m, v))
    tr = _pick(r, (256, 208, 128, 64, 32, 16, 8))

    def body(w_ref, g_ref, m_ref, v_ref, d_ref, mo_ref, vo_ref):
        d_ref[...], mo_ref[...], vo_ref[...] = _adamw_update(w_ref[...], g_ref[...], m_ref[...], v_ref[...])

    spec = pl.BlockSpec((tr, n), lambda i: (i, 0))
    outs = pl.pallas_call(
        body, grid=(r // tr,), in_specs=[spec] * 4, out_specs=[spec] * 3,
        out_shape=[jax.ShapeDtypeStruct((r, n), f32)] * 3, name=name,
    )(w2, g2, m2, v2)
    return tuple(o.reshape(shape) for o in outs)


_SMALL = ("shift_mu", "w_decay0", "a0", "k_k", "k_a", "r_k", "ln_x_w", "ln_x_b", "v_mix0", "lb_logits",
          "g_norm_w", "ln_w", "ln_b")
_NAMES = ("w_in", "shift_mu", "w_decay0", "w_decay_up", "a0", "a_up", "k_k", "k_a", "r_k", "ln_x_w", "ln_x_b",
          "v_mix0", "v_mix_down", "v_mix_up", "lb_logits", "g_norm_w", "w_out", "ln_w", "ln_b")


def _pad_rows(a, rows, at_end):
    z = jnp.zeros((rows - a.shape[0], a.shape[1]), a.dtype)
    return jnp.concatenate([a, z] if at_end else [z, a], axis=0)


def kernel(x, w_in, shift_mu, w_decay0, w_decay_up, a0, a_up, k_k, k_a, r_k, ln_x_w, ln_x_b, v_mix0, v_mix_down, v_mix_up, lb_logits, g_norm_w, w_out, ln_w, ln_b, loss_target, m_w_in, m_shift_mu, m_w_decay0, m_w_decay_up, m_a0, m_a_up, m_k_k, m_k_a, m_r_k, m_ln_x_w, m_ln_x_b, m_v_mix0, m_v_mix_down, m_v_mix_up, m_lb_logits, m_g_norm_w, m_w_out, m_ln_w, m_ln_b, v_w_in, v_shift_mu, v_w_decay0, v_w_decay_up, v_a0, v_a_up, v_k_k, v_k_a, v_r_k, v_ln_x_w, v_ln_x_b, v_v_mix0, v_v_mix_down, v_v_mix_up, v_lb_logits, v_g_norm_w, v_w_out, v_ln_w, v_ln_b):
    weights = dict(w_in=w_in, shift_mu=shift_mu, w_decay0=w_decay0, w_decay_up=w_decay_up, a0=a0, a_up=a_up, k_k=k_k,
                   k_a=k_a, r_k=r_k, ln_x_w=ln_x_w, ln_x_b=ln_x_b, v_mix0=v_mix0, v_mix_down=v_mix_down,
                   v_mix_up=v_mix_up, lb_logits=lb_logits, g_norm_w=g_norm_w, w_out=w_out, ln_w=ln_w, ln_b=ln_b)
    mom1 = dict(w_in=m_w_in, shift_mu=m_shift_mu, w_decay0=m_w_decay0, w_decay_up=m_w_decay_up, a0=m_a0, a_up=m_a_up,
                k_k=m_k_k, k_a=m_k_a, r_k=m_r_k, ln_x_w=m_ln_x_w, ln_x_b=m_ln_x_b, v_mix0=m_v_mix0,
                v_mix_down=m_v_mix_down, v_mix_up=m_v_mix_up, lb_logits=m_lb_logits, g_norm_w=m_g_norm_w,
                w_out=m_w_out, ln_w=m_ln_w, ln_b=m_ln_b)
    mom2 = dict(w_in=v_w_in, shift_mu=v_shift_mu, w_decay0=v_w_decay0, w_decay_up=v_w_decay_up, a0=v_a0, a_up=v_a_up,
                k_k=v_k_k, k_a=v_k_a, r_k=v_r_k, ln_x_w=v_ln_x_w, ln_x_b=v_ln_x_b, v_mix0=v_v_mix0,
                v_mix_down=v_v_mix_down, v_mix_up=v_v_mix_up, lb_logits=v_lb_logits, g_norm_w=v_g_norm_w,
                w_out=v_w_out, ln_w=v_ln_w, ln_b=v_ln_b)
    assert x.shape[0] == 1 and w_in.shape[0] == DEPTH
    t, d = x.shape[1], x.shape[2]
    dr = w_decay0.shape[1]
    dh = g_norm_w.shape[1]
    rank_w, rank_a, rank_v = w_decay_up.shape[1], a_up.shape[1], v_mix_up.shape[1]
    rwc = 4 * dr + rank_w + rank_a
    assert rank_w + rank_a == LANES and rank_v <= LANES and dr + dh == d
    assert t % CHUNK == 0 and dr % LANES == 0 and dh % LANES == 0 and shift_mu.shape[1] == rwc
    n_pair = dr // LANES
    me = _index(_position())

    shard = dr // N_DEV
    pack = jnp.concatenate([w_decay_up[0], w_decay_up[1], a_up[0], a_up[1], v_mix_up[0], v_mix_down[0].T], axis=0)
    win_t0, pack = _all_gather_rows([w_in[0].T.astype(bf16), pack], "ag_first")
    win_t = [win_t0, None]
    wout = [None, None]

    def start_gather(blocks, name, after):
        lands = [_landing_zone(blk, me, f"{name}_zone{i}") for i, blk in enumerate(blocks)]
        return _split_start(blocks, lands, _gather_plan, len(_GATHER_FLIPS), name, after=after)

    gather_wout0 = start_gather([w_out[0].astype(bf16)], "ag_wout0_start", (win_t[0], pack))
    gather_layer1 = start_gather([w_in[1].T.astype(bf16), w_out[1].astype(bf16)], "ag_layer1_start", (gather_wout0[-1],))
    pack = jnp.transpose(pack.reshape(N_DEV, -1, shard), (1, 0, 2)).reshape(-1, dr)
    offs = [0, rank_w, 2 * rank_w, 2 * rank_w + rank_a, 2 * rank_w + 2 * rank_a, 2 * rank_w + 2 * rank_a + rank_v,
            2 * rank_w + 2 * rank_a + 2 * rank_v]
    wdu_f = [pack[offs[0]:offs[1]], pack[offs[1]:offs[2]]]
    aup_f = [pack[offs[2]:offs[3]], pack[offs[3]:offs[4]]]
    vup_f = pack[offs[4]:offs[5]]
    vdown_f = pack[offs[5]:offs[6]].T

    def after_start(a, started):
        return a + started[-1][0:1, 0:1]

    def rwkv_params(l):
        mu = after_start(shift_mu[0:1], gather_layer1) if l == 0 else shift_mu[l:l + 1]
        prm = [mu, w_decay0[l:l + 1], a0[l:l + 1], _pad_rows(wdu_f[l], LANES, True),
               _pad_rows(aup_f[l], LANES, False)]
        if l == 1:
            prm += [v_mix0[0:1], _pad_rows(vdown_f.T, LANES, True).T, _pad_rows(vup_f, LANES, True)]
        rows = jnp.stack([k_k[l], k_a[l], r_k[l], ln_x_w[l], ln_x_b[l]] + [jnp.zeros((dr,), f32)] * 3, axis=0)
        pp = jnp.transpose(rows.reshape(8, n_pair, LANES), (1, 0, 2))
        return tuple(prm), pp

    h = x[0]
    h16 = h.astype(bf16)
    tgt = loss_target[0]
    saved = []
    vfirst = None
    for l in range(DEPTH):
        prm, pp = rwkv_params(l)
        proj = _matmul(h16, win_t[l], "nt", f"mm_proj_{l}", _MM_TILES["proj"])
        if l == 0:
            cat, vfirst, mck = _rwkv_fwd(False, proj, None, prm, pp, d)
        else:
            cat, mck = _rwkv_fwd(True, proj, vfirst, prm, pp, d)
        cat, sck = _hgrn_fwd(l == 1, proj, lb_logits, g_norm_w[l:l + 1], cat, rwc)
        if l == 0:
            _, arrived = _split_wait(gather_wout0, _gather_plan, cat, "ag_wout0_wait")
            (wout[0],) = _gather_forward(arrived, "ag_wout0_forward")
        y = _matmul(cat, wout[l], "nn", f"mm_out_{l}", _MM_TILES["out"])
        saved.append((h, h16, proj, prm, pp, mck, sck, cat, y))
        if l < DEPTH - 1:
            h, h16 = _ln_fwd(h, y, ln_w[l:l + 1], ln_b[l:l + 1])
            _, arrived = _split_wait(gather_layer1, _gather_plan, h16, "ag_layer1_wait")
            win_t[1], wout[1] = _gather_forward(arrived, "ag_layer1_forward")
        else:
            top = _ln_loss_bwd(h, y, ln_w[l:l + 1], ln_b[l:l + 1], tgt)
    loss = lax.psum(top[4][0, 0], ("x", "y", "c"))

    grads = {}
    big = {}
    dvfirst = None
    d_lbl = None
    rs_started = {}
    for l in reversed(range(DEPTH)):
        h_l, h16_l, proj, prm, pp, mck, sck, cat, y = saved[l]
        if l == DEPTH - 1:
            dy, dy16, g_ln_w, g_ln_b = top[:4]
        else:
            dy, dy16, g_ln_w, g_ln_b = _ln_bwd(h_l, y, after_start(ln_w[l:l + 1], rs_started[l + 1]), ln_b[l:l + 1], dh_out)
        dcat = _matmul(dy16, wout[l], "nt", f"mm_dcat_{l}", _MM_TILES["dcat"])
        big[("w_out", l)] = _matmul(cat, dy16, "tn", f"mm_dwout_{l}", _MM_TILES["dwout"], out_dtype=bf16)
        if l == 1:
            outs = _rwkv_bwd(True, proj, vfirst, prm, pp, mck, dcat, None)
            dproj_r, dvfirst = outs[0], outs[1]
            dprm, dpp = outs[2:-1], outs[-1]
        else:
            outs = _rwkv_bwd(False, proj, None, prm, pp, mck, dcat, dvfirst)
            dproj_r = outs[0]
            dprm, dpp = outs[1:-1], outs[-1]
        dproj, dlbl_l, dgnw = _hgrn_bwd(l == 1, proj, lb_logits, g_norm_w[l:l + 1], sck, dcat, rwc, dproj_r)
        big[("w_in", l)] = _matmul(dproj, h16_l, "tn", f"mm_dwin_{l}", _MM_TILES["dwin"], out_dtype=bf16)
        sharded = [dprm[3][:rank_w].T, dprm[4][rank_w:].T]
        if l == 1:
            sharded += [dprm[6][:, :rank_v], dprm[7][:rank_v].T,
                        jnp.zeros((dr, LANES - 2 * rank_v), f32)]
        sharded = jnp.concatenate(sharded, axis=1).astype(bf16)
        d2d = _reduce_scatter_begin([big[("w_in", l)], big[("w_out", l)], sharded], f"l{l}")
        if l == 0:
            rs_started[l] = _reduce_scatter_middle(d2d, sharded, f"l{l}")
            token = rs_started[l][-1]
        else:
            token = d2d[-1]
        dh_out = _matmul(dproj, win_t[l], "nn", f"mm_dh_{l}", _MM_TILES["dh"], add=dy, add_scale=ALPHA, after=token)
        if l > 0:
            rs_started[l] = _reduce_scatter_middle(d2d, dh_out, f"l{l}")
        dpp = jnp.transpose(dpp, (1, 0, 2)).reshape(8, dr)
        grads[l] = dict(shift_mu=dprm[0][0], w_decay0=dprm[1][0], a0=dprm[2][0],
                        k_k=dpp[0], k_a=dpp[1], r_k=dpp[2], ln_x_w=dpp[3], ln_x_b=dpp[4],
                        g_norm_w=dgnw[0], ln_w=g_ln_w[0], ln_b=g_ln_b[0])
        if l == 1:
            grads[l].update(v_mix0=dprm[5][0])
            d_lbl = dlbl_l
    grad_x = dh_out[None]

    def both(name):
        return jnp.stack([grads[0][name], grads[1][name]])

    small = dict(shift_mu=both("shift_mu"), w_decay0=both("w_decay0"), a0=both("a0"), k_k=both("k_k"), k_a=both("k_a"),
                 r_k=both("r_k"), ln_x_w=both("ln_x_w"), ln_x_b=both("ln_x_b"), v_mix0=grads[1]["v_mix0"][None],
                 lb_logits=d_lbl, g_norm_w=both("g_norm_w"), ln_w=both("ln_w"), ln_b=both("ln_b"))
    flat = jnp.concatenate([small[nm].reshape(-1) for nm in _SMALL])
    n_flat = flat.shape[0]
    rows = -(-n_flat // (8 * LANES)) * 8
    flat = jnp.concatenate([flat, jnp.zeros((rows * LANES - n_flat,), f32)]).reshape(rows, LANES)
    total = _sum_slots(_all_gather_rows([flat], "ag_small_grads")[0].reshape(N_DEV, rows, LANES), "sum_small_grads").reshape(-1)
    gsm = {}
    off = 0
    for nm in _SMALL:
        size = small[nm].size
        gsm[nm] = total[off:off + size].reshape(small[nm].shape)
        off += size
    reduced = {1: _reduce_scatter_end(rs_started[1], dh_out, "l1", None, 1, DEPTH)}
    reduced[0] = _reduce_scatter_end(rs_started[0], total, "l0", reduced[1][0], 0, DEPTH)
    g_w_in_t = reduced[0][0]
    gsm["w_in"] = jnp.transpose(g_w_in_t, (0, 2, 1))
    gsm["w_out"] = jnp.stack([reduced[l][1] for l in range(DEPTH)])
    gsm["w_decay_up"] = jnp.stack([reduced[l][2][:, :rank_w].T for l in range(DEPTH)])
    gsm["a_up"] = jnp.stack([reduced[l][2][:, rank_w:rank_w + rank_a].T for l in range(DEPTH)])
    gsm["v_mix_down"] = reduced[1][2][:, LANES:LANES + rank_v][None]
    gsm["v_mix_up"] = reduced[1][2][:, LANES + rank_v:LANES + 2 * rank_v].T[None]

    deltas, new_m, new_v = {}, {}, {}
    swap = lambda a: jnp.transpose(a, (0, 2, 1))
    deltas["w_in"], new_m["w_in"], new_v["w_in"] = (
        swap(a) for a in _adamw(swap(w_in), g_w_in_t, swap(m_w_in), swap(v_w_in), "adamw_w_in"))
    deltas["w_out"], new_m["w_out"], new_v["w_out"] = _adamw(w_out, gsm["w_out"], m_w_out, v_w_out, "adamw_w_out")
    rest = [nm for nm in _NAMES if nm not in ("w_in", "w_out")]
    d_rest, m_rest, v_rest = _adamw_many([weights[nm] for nm in rest], [gsm[nm] for nm in rest],
                                         [mom1[nm] for nm in rest], [mom2[nm] for nm in rest], "adamw_small")
    for i, nm in enumerate(rest):
        deltas[nm], new_m[nm], new_v[nm] = d_rest[i], m_rest[i], v_rest[i]
    return (loss, grad_x, *[gsm[nm] for nm in _NAMES], *[deltas[nm] for nm in _NAMES],
            *[new_m[nm] for nm in _NAMES], *[new_v[nm] for nm in _NAMES])
```

```python
import functools

import jax
import jax.numpy as jnp
from jax import lax
from jax.experimental import pallas as pl
from jax.experimental.pallas import tpu as pltpu

f32 = jnp.float32
bf16 = jnp.bfloat16

N_DEV = 8
CHUNK = 64
LANES = 128
RWKV_HEAD = 64
DEPTH = 2
ALPHA = (2 * DEPTH) ** 0.25
LN_EPS = 1e-5
GN_EPS = 64e-5
RMS_EPS = 1e-5
LB_FLOOR = 1e-30
ADAM_LR, ADAM_B1, ADAM_B2, ADAM_EPS, ADAM_WD, ADAM_STEP = 0.001, 0.9, 0.999, 1e-08, 0.01, 10
MESH = pl.DeviceIdType.MESH


def _iota(shape, d):
    return lax.broadcasted_iota(jnp.int32, shape, d)


_DIMS = {"nn": (((1,), (0,)), ((), ())), "nt": (((1,), (1,)), ((), ())), "tn": (((0,), (0,)), ((), ()))}
_BATCH_DIMS = {"nn": (((2,), (1,)), ((0,), (0,))), "nt": (((2,), (2,)), ((0,), (0,))), "tn": (((1,), (1,)), ((0,), (0,)))}
_K_AXES = {"nn": (-1, -2), "nt": (-1, -1), "tn": (-2, -2)}


def _mxu(a, b, mode):
    return lax.dot_general(a, b, (_BATCH_DIMS if a.ndim == 3 else _DIMS)[mode], preferred_element_type=f32)


def _split(x):
    hi = x.astype(bf16)
    return hi, (x - hi.astype(f32)).astype(bf16)


def _mm2_impl(a, b, mode, passes=3):
    if passes == 1:
        return _mxu(a.astype(bf16), b.astype(bf16), mode)
    ah, al = _split(a)
    if passes == 3:
        bh, bl = _split(b)
        lhs, rhs = [ah, ah, al], [bh, bl, bh]
    else:
        bh = b.astype(bf16)
        lhs, rhs = [ah, al], [bh, bh]
    ka, kb = _K_AXES[mode]
    k = a.shape[ka]
    if k % (LANES if -1 in (ka, kb) else 16) == 0:
        return _mxu(jnp.concatenate(lhs, axis=ka), jnp.concatenate(rhs, axis=kb), mode)
    out = _mxu(lhs[0], rhs[0], mode)
    for x, y in zip(lhs[1:], rhs[1:]):
        out = out + _mxu(x, y, mode)
    return out


@functools.partial(jax.custom_vjp, nondiff_argnums=(2, 3))
def _mm2(a, b, mode, passes=3):
    return _mm2_impl(a, b, mode, passes)


def _mm2_fwd(a, b, mode, passes):
    return _mm2_impl(a, b, mode, passes), (a, b)


def _mm2_bwd(mode, passes, res, g):
    a, b = res
    if mode == "nn":
        return _mm2_impl(g, b, "nt", passes), _mm2_impl(a, g, "tn", passes)
    if mode == "nt":
        return _mm2_impl(g, b, "nn", passes), _mm2_impl(g, a, "tn", passes)
    return _mm2_impl(b, g, "nt", passes), _mm2_impl(a, g, "nn", passes)


_mm2.defvjp(_mm2_fwd, _mm2_bwd)

TRI_PASSES = 1
APPLY_PASSES = 1


def _const_impl(cm, x, mode):
    if mode in ("r", "rt"):
        shape = x.shape
        out = _mxu(x.astype(bf16).reshape(-1, shape[-1]), cm, "nn" if mode == "r" else "nt")
        return out.reshape(shape[:-1] + (out.shape[-1],))
    hi, lo = _split(x)
    if x.ndim == 3:
        cm = jnp.broadcast_to(cm, (x.shape[0],) + cm.shape)
    return _mxu(cm, hi, mode) + _mxu(cm, lo, mode)


@jax.custom_vjp
def _const_left(cm, x):
    return _const_impl(cm, x, "nn")


_const_left.defvjp(lambda cm, x: (_const_impl(cm, x, "nn"), cm),
                   lambda cm, g: (jnp.zeros_like(cm), _const_impl(cm, g, "tn")))


@jax.custom_vjp
def _const_right(x, cm):
    return _const_impl(cm, x, "r")


_const_right.defvjp(lambda x, cm: (_const_impl(cm, x, "r"), cm),
                    lambda cm, g: (_const_impl(cm, g, "rt"), jnp.zeros_like(cm)))


def _tri_inv(a):
    n = a.shape[-1]
    tm = (_iota((n, n), 0) == _iota((n, n), 1)).astype(f32) + a
    ak = a
    for _ in range(5):
        ak = _mm2_impl(ak, ak, "nn", TRI_PASSES)
        tm = tm + _mm2_impl(tm, ak, "nn", TRI_PASSES)
    return tm


@jax.custom_vjp
def _tri_solve(tm, a, x):
    del a
    return _mm2_impl(tm, x, "nn", APPLY_PASSES)


def _tri_solve_fwd(tm, a, x):
    u = _mm2_impl(tm, x, "nn", APPLY_PASSES)
    return u, (tm, u)


def _tri_solve_bwd(res, du):
    tm, u = res
    dx = _mm2_impl(tm, du, "tn", APPLY_PASSES)
    return jnp.zeros_like(tm), _mm2_impl(dx, u, "nt", APPLY_PASSES), dx


_tri_solve.defvjp(_tri_solve_fwd, _tri_solve_bwd)


def _rows_rolled(x, shift, dead_row):
    return jnp.where(_iota((x.shape[0], 1), 0) == dead_row, 0.0, pltpu.roll(x, shift, 0))


@jax.custom_vjp
def _shift_rows_down(y):
    return _rows_rolled(y, 1, 0)


_shift_rows_down.defvjp(lambda y: (_rows_rolled(y, 1, 0), None),
                        lambda _, g: (_rows_rolled(g, g.shape[0] - 1, g.shape[0] - 1),))


def _scan_rows(x, backward):
    c = x.shape[-2]
    axis = x.ndim - 2
    row = _iota((c, 1), 0)
    s = 1
    while s < c:
        if backward:
            x = x + jnp.where(row >= c - s, 0.0, pltpu.roll(x, c - s, axis))
        else:
            x = x + jnp.where(row < s, 0.0, pltpu.roll(x, s, axis))
        s *= 2
    return x


@jax.custom_vjp
def _cumsum_rows(x):
    return _scan_rows(x, False)


_cumsum_rows.defvjp(lambda x: (_scan_rows(x, False), None), lambda _, g: (_scan_rows(g, True),))


def _col_of_row(row_vec):
    n = row_vec.shape[-1]
    eye = _iota((n, n), 0) == _iota((n, n), 1)
    return jnp.sum(jnp.where(eye, jnp.broadcast_to(row_vec, row_vec.shape[:-2] + (n, n)), 0.0), axis=-1, keepdims=True)


def _softplus(x):
    return jnp.maximum(x, 0.0) + jnp.log1p(jnp.exp(-jnp.abs(x)))


def _log_sigmoid(x):
    return -_softplus(-x)


def _logaddexp(a, b):
    return jnp.maximum(a, b) + jnp.log1p(jnp.exp(-jnp.abs(a - b)))


def _silu(x):
    return x * jax.nn.sigmoid(x)


def _tril(c, strict):
    r, s = _iota((c, c), 0), _iota((c, c), 1)
    return (r > s) if strict else (r >= s)


def _last_row(a):
    c = a.shape[-2]
    return jnp.sum(jnp.where(_iota(a.shape, a.ndim - 2) == c - 1, a, 0.0), axis=-2, keepdims=True)


def _rwkv_pre(layer1, prm, y, prev, vf):
    c = y.shape[0]
    if layer1:
        mu, w0, a0, wup, aup, v0, vdown, vup = prm
    else:
        mu, w0, a0, wup, aup = prm
    dr = w0.shape[1]
    y_prev = _shift_rows_down(y) + jnp.where(_iota((c, 1), 0) == 0, prev, 0.0)
    rw = y + mu * (y_prev - y)
    r, k, v, z = (rw[:, i * dr:(i + 1) * dr] for i in range(4))
    wdad = rw[:, 4 * dr:4 * dr + LANES]
    w_raw = w0 + _mm2(jnp.tanh(wdad), wup, "nn")
    lw = -jnp.exp(-_softplus(-w_raw) - 0.5)
    asig = jax.nn.sigmoid(a0 + _mm2(wdad, aup, "nn", APPLY_PASSES))
    if layer1:
        v = v + (vf - v) * jax.nn.sigmoid(v0 + _mm2(_mm2(v, vdown, "nn", APPLY_PASSES), vup, "nn", APPLY_PASSES))
    return r, k, v, z, lw, asig


def _rwkv_pair(pp, m0, xs, tm=None):
    kkw, kaw, rkw, gnw, gnb = pp
    r, k, v, z, lw, asig = xs
    c = r.shape[-2]
    n2 = 2 * c
    lane = _iota((1, LANES), 1)
    mh0, mh1 = (lane < RWKV_HEAD).astype(f32), (lane >= RWKV_HEAD).astype(f32)
    same_head = _iota((LANES, LANES), 0) // RWKV_HEAD == _iota((LANES, LANES), 1) // RWKV_HEAD
    g = same_head.astype(bf16)

    def seg(x):
        return _const_right(x, g)

    def stack(x):
        return jnp.concatenate([x * mh0, x * mh1], axis=-2)

    kk = k * kkw
    kk = kk / jnp.maximum(jnp.sqrt(seg(kk * kk)), 1e-12)
    k2 = k * (1.0 + (asig - 1.0) * kaw)
    a = -kk
    b = kk * asig
    cum = _cumsum_rows(lw)
    at = stack(a * jnp.exp(cum - lw))
    rt = stack(r * jnp.exp(cum))
    en = jnp.exp(-cum)
    sc = _mm2(jnp.concatenate([at, rt], axis=-2), jnp.concatenate([stack(b * en), stack(k2 * en)], axis=-2), "nt")
    row, col = _iota((n2, n2), 0), _iota((n2, n2), 1)
    same = row // c == col // c
    strict = same & (row % c > col % c)
    incl = same & (row % c >= col % c)
    aab = jnp.where(strict, sc[..., :n2, :n2], 0.0)
    aak = jnp.where(strict, sc[..., :n2, n2:], 0.0)
    arb = jnp.where(incl, sc[..., n2:, :n2], 0.0)
    ark = jnp.where(incl, sc[..., n2:, n2:], 0.0)
    vv = jnp.concatenate([v, v], axis=-2)
    mask_st = jnp.concatenate([jnp.broadcast_to(mh0, (c, LANES)), jnp.broadcast_to(mh1, (c, LANES))], axis=0)
    x_st = _mm2(jnp.concatenate([at, aak], axis=-1), jnp.concatenate([m0, vv], axis=-2), "nn", APPLY_PASSES)
    if tm is None:
        tm = _tri_inv(lax.stop_gradient(aab))
    u_st = _tri_solve(tm, aab, x_st) * mask_st
    o_st = _mm2(jnp.concatenate([rt, arb, ark], axis=-1), jnp.concatenate([m0, u_st, vv], axis=-2), "nn", APPLY_PASSES) * mask_st
    u = u_st[..., :c, :] + u_st[..., c:, :]
    o = o_st[..., :c, :] + o_st[..., c:, :]
    cum_last = _last_row(cum)
    dec_end = jnp.exp(cum_last - cum)
    m_new = _col_of_row(jnp.exp(cum_last)) * m0 + _mm2(
        jnp.concatenate([b * dec_end, k2 * dec_end], axis=-2), jnp.concatenate([u, v], axis=-2), "tn", APPLY_PASSES) * same_head.astype(f32)
    mean = seg(o) * (1.0 / RWKV_HEAD)
    d = o - mean
    var = seg(d * d) * (1.0 / RWKV_HEAD)
    on = d * lax.rsqrt(var + GN_EPS) * gnw + gnb
    bonus = seg(r * k2 * rkw) * v
    return (on + bonus) * _silu(z), m_new, tm


def _split_lanes(a, n):
    return [a[:, i * LANES:(i + 1) * LANES] for i in range(n)]


def _rwkv_step(layer1, prm, y, prev, vf, pp, m0, tm=None):
    xs = _rwkv_pre(layer1, prm, y, prev, vf)
    n_pair = m0.shape[0]
    og, m_new, tm = _rwkv_pair(pp, m0, tuple(jnp.concatenate([p[None] for p in _split_lanes(a, n_pair)], axis=0) for a in xs), tm)
    return og, m_new, xs[2], tm


def _group(n):
    return n


def _rwkv_specs(layer1, t, dr, rwc, n_pair, rev):
    nc = t // CHUNK
    grp = _group(n_pair)

    def cidx(c):
        return (nc - 1 - c) if rev else c

    full = lambda shape: pl.BlockSpec(shape, lambda c, p: tuple(0 for _ in shape))
    specs = [
        pl.BlockSpec((CHUNK, rwc), lambda c, p: (cidx(c), 0)),
        pl.BlockSpec((8, rwc), lambda c, p: (jnp.maximum(cidx(c) * (CHUNK // 8) - 1, 0), 0)),
    ]
    if layer1:
        specs.append(pl.BlockSpec((CHUNK, dr), lambda c, p: (cidx(c), 0)))
    prm_shapes = [(1, rwc), (1, dr), (1, dr), (LANES, dr), (LANES, dr)]
    if layer1:
        prm_shapes += [(1, dr), (dr, LANES), (LANES, dr)]
    specs += [full(s) for s in prm_shapes]
    specs.append(pl.BlockSpec((grp, 8, LANES), lambda c, p: (p, 0, 0)))
    return specs, prm_shapes, cidx, full


def _rwkv_fwd(layer1, proj, vf, prm, pp, cat_width):
    t = proj.shape[0]
    dr = prm[1].shape[1]
    rwc = prm[0].shape[1]
    n_pair = dr // LANES
    nc = t // CHUNK
    n_prm = len(prm)
    specs, _, _, _ = _rwkv_specs(layer1, t, dr, rwc, n_pair, False)

    def body(*refs):
        y_ref, prev_ref = refs[0], refs[1]
        i = 2
        vf_ref = None
        if layer1:
            vf_ref = refs[i]
            i += 1
        prm_refs = refs[i:i + n_prm]
        i += n_prm
        pp_ref = refs[i]
        i += 1
        cat_ref = refs[i]
        i += 1
        vout_ref = None
        if not layer1:
            vout_ref = refs[i]
            i += 1
        mck_ref, m_s = refs[i], refs[i + 1]
        c = pl.program_id(0)

        @pl.when(c == 0)
        def _():
            m_s[...] = jnp.zeros_like(m_s)

        prev = prev_ref[pl.ds(7, 1), :] * (c != 0).astype(f32)
        m0 = m_s[...]
        ppv = tuple(pp_ref[:, pl.ds(q, 1), :] for q in range(5))
        og, m_new, v, tm = _rwkv_step(layer1, tuple(r[...] for r in prm_refs), y_ref[...], prev,
                                      vf_ref[...] if layer1 else None, ppv, m0)
        mck_ref[0, :n_pair] = m0
        mck_ref[0, n_pair:] = tm
        if not layer1:
            vout_ref[...] = v
        for j in range(n_pair):
            cat_ref[:, j * LANES:(j + 1) * LANES] = og[j]
        m_s[...] = m_new

    grp = _group(n_pair)
    assert grp == n_pair
    out_shape = [jax.ShapeDtypeStruct((t, cat_width), f32)]
    out_specs = [pl.BlockSpec((CHUNK, grp * LANES), lambda c, p: (c, p))]
    if not layer1:
        out_shape.append(jax.ShapeDtypeStruct((t, dr), f32))
        out_specs.append(pl.BlockSpec((CHUNK, dr), lambda c, p: (c, 0)))
    out_shape.append(jax.ShapeDtypeStruct((nc, 2 * n_pair, LANES, LANES), f32))
    out_specs.append(pl.BlockSpec((1, 2 * grp, LANES, LANES), lambda c, p: (c, p, 0, 0)))
    args = [proj, proj] + ([vf] if layer1 else []) + list(prm) + [pp]
    return pl.pallas_call(
        body, grid=(nc, 1), in_specs=specs, out_specs=out_specs, out_shape=out_shape,
        scratch_shapes=[pltpu.VMEM((n_pair, LANES, LANES), f32)],
        compiler_params=pltpu.CompilerParams(dimension_semantics=("arbitrary", "arbitrary")),
        name=f"rwkv_fwd_l{int(layer1)}",
    )(*args)


def _rwkv_bwd(layer1, proj, vf, prm, pp, mck, dcat, dvout):
    t = proj.shape[0]
    dr = prm[1].shape[1]
    rwc = prm[0].shape[1]
    n_pair = dr // LANES
    nc = t // CHUNK
    n_prm = len(prm)
    specs, prm_shapes, cidx, full = _rwkv_specs(layer1, t, dr, rwc, n_pair, True)
    grp = _group(n_pair)
    assert grp == n_pair
    specs.append(pl.BlockSpec((1, 2 * grp, LANES, LANES), lambda c, p: (cidx(c), p, 0, 0)))
    specs.append(pl.BlockSpec((CHUNK, grp * LANES), lambda c, p: (cidx(c), p)))
    if not layer1:
        specs.append(pl.BlockSpec((CHUNK, dr), lambda c, p: (cidx(c), 0)))

    def body(*refs):
        y_ref, prev_ref = refs[0], refs[1]
        i = 2
        vf_ref = None
        if layer1:
            vf_ref = refs[i]
            i += 1
        prm_refs = refs[i:i + n_prm]
        i += n_prm
        pp_ref, mck_ref, dog_ref = refs[i], refs[i + 1], refs[i + 2]
        i += 3
        dvout_ref = None
        if not layer1:
            dvout_ref = refs[i]
            i += 1
        dy_ref = refs[i]
        i += 1
        dvf_ref = None
        if layer1:
            dvf_ref = refs[i]
            i += 1
        dprm_refs = refs[i:i + n_prm]
        i += n_prm
        dpp_ref = refs[i]
        dm_s, dprev_s = refs[i + 1:i + 3]
        c = pl.program_id(0)
        cr = nc - 1 - c

        @pl.when(c == 0)
        def _():
            dm_s[...] = jnp.zeros_like(dm_s)
            dprev_s[...] = jnp.zeros_like(dprev_s)
            dpp_ref[...] = jnp.zeros_like(dpp_ref)
            for r in dprm_refs:
                r[...] = jnp.zeros_like(r)

        prev = prev_ref[pl.ds(7, 1), :] * (cr != 0).astype(f32)
        prm_v = tuple(r[...] for r in prm_refs)
        ppv = tuple(pp_ref[:, pl.ds(q, 1), :] for q in range(5))
        dog = jnp.stack([dog_ref[:, j * LANES:(j + 1) * LANES] for j in range(n_pair)], axis=0)
        m0, tm = mck_ref[0, :n_pair], mck_ref[0, n_pair:]
        no_tm = jnp.zeros_like(tm)
        if layer1:
            _, vjp = jax.vjp(lambda a, b, d, e, g, h: _rwkv_step(True, a, b, d, e, g, h, tm),
                             prm_v, y_ref[...], prev, vf_ref[...], ppv, m0)
            dprm, dy, dprev, dvf, dppv, dm0 = vjp((dog, dm_s[...], jnp.zeros((CHUNK, dr), f32), no_tm))
            dvf_ref[...] = dvf
        else:
            _, vjp = jax.vjp(lambda a, b, d, e, g: _rwkv_step(False, a, b, d, None, e, g, tm), prm_v, y_ref[...], prev, ppv, m0)
            dprm, dy, dprev, dppv, dm0 = vjp((dog, dm_s[...], dvout_ref[...], no_tm))
        dm_s[...] = dm0
        for q in range(5):
            dpp_ref[:, pl.ds(q, 1), :] += dppv[q]
        dy_ref[...] = (dy + jnp.where(_iota((CHUNK, 1), 0) == CHUNK - 1, dprev_s[...], 0.0)).astype(bf16)
        dprev_s[...] = dprev
        for r, gval in zip(dprm_refs, dprm):
            r[...] += gval

    out_shape = [jax.ShapeDtypeStruct((t, proj.shape[1]), bf16)]
    out_specs = [pl.BlockSpec((CHUNK, rwc), lambda c, p: (cidx(c), 0))]
    if layer1:
        out_shape.append(jax.ShapeDtypeStruct((t, dr), f32))
        out_specs.append(pl.BlockSpec((CHUNK, dr), lambda c, p: (cidx(c), 0)))
    out_shape += [jax.ShapeDtypeStruct(s, f32) for s in prm_shapes]
    out_specs += [full(s) for s in prm_shapes]
    out_shape.append(jax.ShapeDtypeStruct((n_pair, 8, LANES), f32))
    out_specs.append(full((n_pair, 8, LANES)))
    args = [proj, proj] + ([vf] if layer1 else []) + list(prm) + [pp, mck, dcat] + ([] if layer1 else [dvout])
    return pl.pallas_call(
        body, grid=(nc, 1), in_specs=specs, out_specs=out_specs, out_shape=out_shape,
        scratch_shapes=[pltpu.VMEM((n_pair, LANES, LANES), f32), pltpu.VMEM((1, rwc), f32)],
        compiler_params=pltpu.CompilerParams(dimension_semantics=("arbitrary", "arbitrary")),
        name=f"rwkv_bwd_l{int(layer1)}",
    )(*args)


def _hgrn_chunk(layer1, lbl, gnw, s0, q_raw, f_raw, i_in, z):
    c = q_raw.shape[-2]
    q = _silu(q_raw)
    ls = _log_sigmoid(f_raw)
    if layer1:
        l0, l1 = lbl[..., 0:1, :], lbl[..., 1:2, :]
        mx = jnp.maximum(l0, l1)
        e0, e1 = jnp.exp(l0 - mx), jnp.exp(l1 - mx)
        sm0, sm1 = e0 / (e0 + e1), e1 / (e0 + e1)
        lb = (sm0 + sm1) - sm0
        log_f = _logaddexp(jnp.log(jnp.maximum(lb, LB_FLOOR)), jnp.log1p(-lb) + ls)
        k = (1.0 - lb) * jax.nn.sigmoid(-f_raw)
    else:
        log_f = _logaddexp(jnp.full_like(ls, jnp.log(jnp.float32(LB_FLOOR))), ls)
        k = jax.nn.sigmoid(-f_raw)
    row, col = _iota((c, c), 0), _iota((c, c), 1)
    trow = _iota((c, 1), 0)
    halves = []
    half = c // 2
    while half >= 1:
        halves.append(half)
        half //= 2
    cmat = jnp.concatenate([(col <= row).astype(f32)]
                           + [(col <= (row // (2 * hf)) * (2 * hf) + hf - 1).astype(f32) for hf in halves], axis=0)
    ball = _const_left(cmat.astype(bf16), log_f)
    b = ball[..., :c, :]
    att = None
    for lvl, hf in enumerate(halves):
        blk = 2 * hf
        bref = ball[..., (lvl + 1) * c:(lvl + 2) * c, :]
        upper = (trow % blk) >= hf
        dec = jnp.exp(jnp.where(upper, b - bref, bref - b))
        qh = jnp.where(upper, q * dec, 0.0)
        kh = jnp.where(upper, 0.0, k * dec)
        term = jnp.where(row // blk == col // blk, _mm2(qh, kh, "nt", APPLY_PASSES), 0.0)
        att = term if att is None else att + term
    lhs = jnp.concatenate([q * jnp.exp(b), att, jnp.zeros(att.shape[:-1] + (LANES - c,), f32)], axis=-1)
    rhs = jnp.concatenate([s0, i_in, jnp.zeros(i_in.shape[:-2] + (LANES - c, i_in.shape[-1]), f32)], axis=-2)
    o = _mm2(lhs, rhs, "nn", APPLY_PASSES) + jnp.sum(q * k, axis=-1, keepdims=True) * i_in
    b_last = _last_row(b)
    s_new = _col_of_row(jnp.exp(b_last)) * s0 + _mm2(k * jnp.exp(b_last - b), i_in, "tn", APPLY_PASSES)
    o = o * lax.rsqrt(jnp.mean(o * o, axis=-1, keepdims=True) + RMS_EPS)
    return o * gnw * _silu(z), s_new


def _hgrn_in_specs(t, dh, col0, rev):
    nc = t // CHUNK
    nh = dh // LANES

    def cidx(c):
        return (nc - 1 - c) if rev else c

    grp = _group(nh)
    specs = [pl.BlockSpec((CHUNK, LANES), functools.partial(lambda g, j, h, c: (cidx(c), col0 + g * nh + h * grp + j), g, j))
             for j in range(grp) for g in range(4)]
    specs.append(pl.BlockSpec((2, grp * LANES), lambda h, c: (0, h)))
    specs.append(pl.BlockSpec((1, grp * LANES), lambda h, c: (0, h)))
    return specs, cidx, grp


def _hgrn_fwd(layer1, proj, lbl, gnw, cat, rwc):
    t, d = cat.shape
    dh = gnw.shape[1]
    nh = dh // LANES
    nc = t // CHUNK
    col0 = rwc // LANES
    specs, _, grp = _hgrn_in_specs(t, dh, col0, False)
    specs.append(pl.BlockSpec(memory_space=pl.ANY))
    assert (d - dh) % (grp * LANES) == 0
    cat_col0 = (d - dh) // (grp * LANES)

    def body(*refs):
        x_refs = refs[:4 * grp]
        lbl_ref, gnw_ref, _, cat_ref, sck_ref, s_s = refs[4 * grp:]
        c = pl.program_id(1)

        @pl.when(c == 0)
        def _():
            s_s[...] = jnp.zeros_like(s_s)

        lanes = [slice(j * LANES, (j + 1) * LANES) for j in range(grp)]
        s0 = s_s[...]
        sck_ref[:, 0] = s0
        out, s_new = _hgrn_chunk(layer1, jnp.stack([lbl_ref[:, ln] for ln in lanes]), jnp.stack([gnw_ref[:, ln] for ln in lanes]),
                                 s0, *(jnp.stack([x_refs[4 * j + g][...] for j in range(grp)]) for g in range(4)))
        for j in range(grp):
            cat_ref[:, lanes[j]] = out[j]
        s_s[...] = s_new

    return pl.pallas_call(
        body, grid=(nh // grp, nc), in_specs=specs,
        out_specs=[pl.BlockSpec((CHUNK, grp * LANES), lambda h, c: (c, cat_col0 + h)),
                   pl.BlockSpec((grp, 1, LANES, LANES), lambda h, c: (h, c, 0, 0))],
        out_shape=[jax.ShapeDtypeStruct((t, d), f32), jax.ShapeDtypeStruct((nh, nc, LANES, LANES), f32)],
        scratch_shapes=[pltpu.VMEM((grp, LANES, LANES), f32)],
        input_output_aliases={4 * grp + 2: 0},
        compiler_params=pltpu.CompilerParams(dimension_semantics=("arbitrary", "arbitrary")),
        name=f"hgrn_fwd_l{int(layer1)}",
    )(*([proj] * (4 * grp)), lbl, gnw, cat)


def _hgrn_bwd(layer1, proj, lbl, gnw, sck, dcat, rwc, dproj):
    t, d = dcat.shape
    dh = gnw.shape[1]
    nh = dh // LANES
    nc = t // CHUNK
    col0 = rwc // LANES
    specs, cidx, grp = _hgrn_in_specs(t, dh, col0, True)
    assert grp == nh and (d - dh) % (grp * LANES) == 0
    cat_col0 = (d - dh) // (grp * LANES)
    specs.append(pl.BlockSpec((grp, 1, LANES, LANES), lambda h, c: (h, cidx(c), 0, 0)))
    specs.append(pl.BlockSpec((CHUNK, grp * LANES), lambda h, c: (cidx(c), cat_col0 + h)))
    specs.append(pl.BlockSpec(memory_space=pl.ANY))

    def body(*refs):
        x_refs = refs[:4 * grp]
        lbl_ref, gnw_ref, sck_ref, do_ref, _, dp_hbm, dlbl_ref, dgnw_ref, ds_s, stage, sems = refs[4 * grp:]
        c = pl.program_id(1)
        slot = c % 2

        def put(s, g, chunk):
            return pltpu.make_async_copy(stage.at[s, g], dp_hbm.at[pl.ds(chunk * CHUNK, CHUNK), pl.ds(rwc + g * dh, dh)],
                                         sems.at[s, g])

        @pl.when(c == 0)
        def _():
            ds_s[...] = jnp.zeros_like(ds_s)
            dlbl_ref[...] = jnp.zeros_like(dlbl_ref)
            dgnw_ref[...] = jnp.zeros_like(dgnw_ref)

        @pl.when(c >= 2)
        def _():
            for g in range(4):
                put(slot, g, 0).wait()

        lanes = [slice(j * LANES, (j + 1) * LANES) for j in range(grp)]
        _, vjp = jax.vjp(functools.partial(_hgrn_chunk, layer1),
                         jnp.stack([lbl_ref[:, ln] for ln in lanes]), jnp.stack([gnw_ref[:, ln] for ln in lanes]), sck_ref[:, 0],
                         *(jnp.stack([x_refs[4 * j + g][...] for j in range(grp)]) for g in range(4)))
        dlbl, dgnw, ds0, dq, df, di, dz = vjp((jnp.stack([do_ref[:, ln] for ln in lanes]), ds_s[...]))
        ds_s[...] = ds0
        for j in range(grp):
            dlbl_ref[:, lanes[j]] += dlbl[j]
            dgnw_ref[:, lanes[j]] += dgnw[j]
            for g, val in enumerate((dq, df, di, dz)):
                stage[slot, g, :, lanes[j]] = val[j].astype(bf16)
        for g in range(4):
            put(slot, g, nc - 1 - c).start()

        @pl.when(c == nc - 1)
        def _():
            for g in range(4):
                put(slot, g, 0).wait()
                if nc >= 2:
                    put(1 - slot, g, 0).wait()

    return pl.pallas_call(
        body, grid=(1, nc), in_specs=specs,
        out_specs=[pl.BlockSpec(memory_space=pl.ANY),
                   pl.BlockSpec((2, grp * LANES), lambda h, c: (0, h)),
                   pl.BlockSpec((1, grp * LANES), lambda h, c: (0, h))],
        out_shape=[jax.ShapeDtypeStruct(dproj.shape, dproj.dtype), jax.ShapeDtypeStruct((2, dh), f32),
                   jax.ShapeDtypeStruct((1, dh), f32)],
        scratch_shapes=[pltpu.VMEM((grp, LANES, LANES), f32), pltpu.VMEM((2, 4, CHUNK, dh), bf16),
                        pltpu.SemaphoreType.DMA((2, 4))],
        input_output_aliases={4 * grp + 4: 0},
        compiler_params=pltpu.CompilerParams(dimension_semantics=("arbitrary", "arbitrary")),
        name=f"hgrn_bwd_l{int(layer1)}",
    )(*([proj] * (4 * grp)), lbl, gnw, sck, dcat, dproj)


def _ln(h, y, w, b):
    u = ALPHA * h + y
    mu = jnp.mean(u, axis=-1, keepdims=True)
    var = jnp.mean(jnp.square(u - mu), axis=-1, keepdims=True)
    return (u - mu) * lax.rsqrt(var + LN_EPS) * w + b


def _row_tile(t):
    return 256 if t % 256 == 0 else t


def _ln_fwd(h, y, w, b):
    t, d = h.shape
    tr = _row_tile(t)

    def body(h_ref, y_ref, w_ref, b_ref, o_ref, o16_ref):
        out = _ln(h_ref[...], y_ref[...], w_ref[...], b_ref[...])
        o_ref[...] = out
        o16_ref[...] = out.astype(bf16)

    row = pl.BlockSpec((tr, d), lambda i: (i, 0))
    vec = pl.BlockSpec((1, d), lambda i: (0, 0))
    return pl.pallas_call(body, grid=(t // tr,), in_specs=[row, row, vec, vec], out_specs=[row, row],
                          out_shape=[jax.ShapeDtypeStruct((t, d), f32), jax.ShapeDtypeStruct((t, d), bf16)],
                          name="ln_fwd")(h, y, w, b)


def _ln_loss_bwd(h, y, w, b, tgt):
    t, d = h.shape
    tr = _row_tile(t)

    def body(h_ref, y_ref, w_ref, b_ref, t_ref, dy_ref, dy16_ref, dw_ref, db_ref, loss_ref):
        @pl.when(pl.program_id(0) == 0)
        def _():
            dw_ref[...] = jnp.zeros_like(dw_ref)
            db_ref[...] = jnp.zeros_like(db_ref)
            loss_ref[...] = jnp.zeros_like(loss_ref)

        out, vjp = jax.vjp(lambda yy, ww, bb: _ln(h_ref[...], yy, ww, bb), y_ref[...], w_ref[...], b_ref[...])
        err = out - t_ref[...]
        loss_ref[...] += 0.5 * jnp.sum(jnp.mean(jnp.square(err), axis=-1, keepdims=True), axis=0, keepdims=True)
        dy, dw, db = vjp(err * (1.0 / d))
        dy_ref[...] = dy
        dy16_ref[...] = dy.astype(bf16)
        dw_ref[...] += dw
        db_ref[...] += db

    row = pl.BlockSpec((tr, d), lambda i: (i, 0))
    vec = pl.BlockSpec((1, d), lambda i: (0, 0))
    return pl.pallas_call(
        body, grid=(t // tr,), in_specs=[row, row, vec, vec, row],
        out_specs=[row, row, vec, vec, pl.BlockSpec((1, LANES), lambda i: (0, 0))],
        out_shape=[jax.ShapeDtypeStruct((t, d), f32), jax.ShapeDtypeStruct((t, d), bf16), jax.ShapeDtypeStruct((1, d), f32),
                   jax.ShapeDtypeStruct((1, d), f32), jax.ShapeDtypeStruct((1, LANES), f32)],
        compiler_params=pltpu.CompilerParams(dimension_semantics=("arbitrary",)), name="ln_loss_bwd")(h, y, w, b, tgt)


def _ln_bwd(h, y, w, b, dout):
    t, d = h.shape
    tr = _row_tile(t)

    def body(h_ref, y_ref, w_ref, b_ref, do_ref, dy_ref, dy16_ref, dw_ref, db_ref):
        @pl.when(pl.program_id(0) == 0)
        def _():
            dw_ref[...] = jnp.zeros_like(dw_ref)
            db_ref[...] = jnp.zeros_like(db_ref)

        _, vjp = jax.vjp(lambda yy, ww, bb: _ln(h_ref[...], yy, ww, bb), y_ref[...], w_ref[...], b_ref[...])
        dy, dw, db = vjp(do_ref[...])
        dy_ref[...] = dy
        dy16_ref[...] = dy.astype(bf16)
        dw_ref[...] += dw
        db_ref[...] += db

    row = pl.BlockSpec((tr, d), lambda i: (i, 0))
    vec = pl.BlockSpec((1, d), lambda i: (0, 0))
    return pl.pallas_call(
        body, grid=(t // tr,), in_specs=[row, row, vec, vec, row], out_specs=[row, row, vec, vec],
        out_shape=[jax.ShapeDtypeStruct((t, d), f32), jax.ShapeDtypeStruct((t, d), bf16),
                   jax.ShapeDtypeStruct((1, d), f32), jax.ShapeDtypeStruct((1, d), f32)],
        compiler_params=pltpu.CompilerParams(dimension_semantics=("arbitrary",)), name="ln_bwd")(h, y, w, b, dout)


def _pick(n, prefs):
    for p in prefs:
        if n % p == 0:
            return p
    return n


def _tile(n, want):
    if n <= want:
        return n
    for cand in range(want - want % LANES, 0, -LANES):
        if n % cand == 0:
            return cand
    return n


_MM_TILES = {"proj": (1024, 1664, 2048), "out": (1024, 1024, 2048), "dcat": (1024, 1024, 2048),
             "dwout": (512, 2048, 2048), "dwin": (640, 2048, 2048), "dh": (1024, 1024, 1664)}


def _matmul(a, b, mode, name, tiles, add=None, add_scale=1.0, out_dtype=f32, after=None):
    if mode == "nn":
        (m, k), n = a.shape, b.shape[1]
    elif mode == "nt":
        (m, k), n = a.shape, b.shape[0]
    else:
        (k, m), n = a.shape, b.shape[1]
    tm, tn, tk = _tile(m, tiles[0]), _tile(n, tiles[1]), _tile(k, tiles[2])
    nk = k // tk
    cache_a = nk == 1 and a.dtype != bf16 and n // tn > 1

    def body(*refs):
        a_ref, b_ref = refs[0], refs[1]
        add_ref = refs[2] if add is not None else None
        n_in = 2 + (add is not None) + (after is not None)
        o_ref = refs[n_in]
        scratch = refs[n_in + 1:]

        def finish(res):
            if add is not None:
                res = res + add_scale * add_ref[...]
            o_ref[...] = res.astype(out_dtype)

        if cache_a:
            a_bf = scratch[0]

            @pl.when(pl.program_id(1) == 0)
            def _():
                a_bf[...] = a_ref[...].astype(bf16)

            a_val = a_bf[...]
        else:
            a_val = a_ref[...].astype(bf16)
        prod = lax.dot_general(a_val, b_ref[...].astype(bf16), _DIMS[mode], preferred_element_type=f32)
        if nk == 1:
            finish(prod)
        else:
            acc = scratch[-1]
            kk = pl.program_id(2)

            @pl.when(kk == 0)
            def _():
                acc[...] = prod

            @pl.when(kk != 0)
            def _():
                acc[...] += prod

            @pl.when(kk == nk - 1)
            def _():
                finish(acc[...])

    a_shape = (tk, tm) if mode == "tn" else (tm, tk)
    a_spec = pl.BlockSpec(a_shape, (lambda i, j, kk: (kk, i)) if mode == "tn" else (lambda i, j, kk: (i, kk)))
    b_spec = pl.BlockSpec((tn, tk), lambda i, j, kk: (j, kk)) if mode == "nt" else pl.BlockSpec((tk, tn), lambda i, j, kk: (kk, j))
    o_spec = pl.BlockSpec((tm, tn), lambda i, j, kk: (i, j))
    in_specs = [a_spec, b_spec] + ([o_spec] if add is not None else []) + ([pl.BlockSpec(memory_space=pl.ANY)] if after is not None else [])
    args = [a, b] + ([add] if add is not None else []) + ([after] if after is not None else [])
    scratch_shapes = ([pltpu.VMEM(a_shape, bf16)] if cache_a else []) + ([pltpu.VMEM((tm, tn), f32)] if nk > 1 else [])
    return pl.pallas_call(
        body, grid=(m // tm, n // tn, nk), in_specs=in_specs, out_specs=o_spec,
        out_shape=jax.ShapeDtypeStruct((m, n), out_dtype), scratch_shapes=scratch_shapes,
        compiler_params=pltpu.CompilerParams(dimension_semantics=("parallel", "arbitrary", "arbitrary")),
        name=name,
    )(*args)


def _position():
    return lax.axis_index("x"), lax.axis_index("y"), lax.axis_index("c")


def _flip(pos, k):
    x, y, c = pos
    return (1 - x if k & 4 else x, 1 - y if k & 2 else y, 1 - c if k & 1 else c)


def _index(pos):
    return 4 * pos[0] + 2 * pos[1] + pos[2]


def _all_gather_rows(xs, name):
    n_arr = len(xs)
    chips = (2, 4, 6)

    def body(*refs):
        x_refs, out_refs = refs[:n_arr], refs[n_arr:2 * n_arr]
        send_sems, recv_sems, local_sems = refs[2 * n_arr:]
        me = _position()
        sibling = _flip(me, 1)

        def copy(i, sem, block, to, own=False):
            m_per = x_refs[i].shape[0]
            rows = out_refs[i].at[pl.ds(_index(block) * m_per, m_per), :]
            return pltpu.make_async_remote_copy(
                src_ref=x_refs[i] if own else rows, dst_ref=rows,
                send_sem=send_sems.at[7 * i + sem], recv_sem=recv_sems.at[7 * i + sem], device_id=to, device_id_type=MESH)

        mine = [pltpu.make_async_copy(x_refs[i], out_refs[i].at[pl.ds(_index(me) * x_refs[i].shape[0], x_refs[i].shape[0]), :],
                                      local_sems.at[i]) for i in range(n_arr)]
        first, passed = [], []
        for i in range(n_arr):
            first.append(copy(i, 0, me, sibling, own=True))
            first += [copy(i, 1 + j, me, _flip(me, k), own=True) for j, k in enumerate(chips)]
            passed.append([copy(i, 4 + j, _flip(me, k), sibling) for j, k in enumerate(chips)])
        for cp in mine + first:
            cp.start()
        for i in range(n_arr):
            for j, k in enumerate(chips):
                copy(i, 1 + j, _flip(me, k), me).wait_recv()
                passed[i][j].start()
        for i in range(n_arr):
            copy(i, 0, sibling, me).wait_recv()
            for j, k in enumerate(chips):
                copy(i, 4 + j, _flip(sibling, k), me).wait_recv()
        for cp in first + [cp for group in passed for cp in group]:
            cp.wait_send()
        for cp in mine:
            cp.wait()

    anyspec = pl.BlockSpec(memory_space=pl.ANY)
    return pl.pallas_call(
        body, out_shape=[jax.ShapeDtypeStruct((N_DEV * x.shape[0], x.shape[1]), x.dtype) for x in xs],
        in_specs=[anyspec] * n_arr, out_specs=[anyspec] * n_arr,
        scratch_shapes=[pltpu.SemaphoreType.DMA((7 * n_arr,)), pltpu.SemaphoreType.DMA((7 * n_arr,)),
                        pltpu.SemaphoreType.DMA((n_arr,))],
        name=name,
    )(*xs)


def _split_start(srcs, lands, plan, n_copies, name, after=()):
    n_arr = len(srcs)
    n_after = len(after)
    hbm = pl.BlockSpec(memory_space=pltpu.HBM)
    sem = pl.BlockSpec(memory_space=pltpu.SEMAPHORE)

    def body(*refs):
        src_refs, land_refs = refs[:n_arr], refs[n_arr:2 * n_arr]
        outs_at = 2 * n_arr + n_after
        send_sems, recv_sems = refs[outs_at:outs_at + n_arr], refs[outs_at + n_arr:outs_at + 2 * n_arr]
        token = refs[-1]
        me = _position()
        for i in range(n_arr):
            for j, (src, dst, peer, _) in enumerate(plan(i, src_refs[i], land_refs[i], me)):
                pltpu.make_async_remote_copy(src_ref=src, dst_ref=dst, send_sem=send_sems[i].at[j], recv_sem=recv_sems[i].at[j],
                                             device_id=peer, device_id_type=MESH).start()
        token[...] = jnp.zeros_like(token)

    outs = pl.pallas_call(
        body, name=name,
        out_shape=([pltpu.SemaphoreType.DMA((n_copies,))] * (2 * n_arr)
                   + [pltpu.HBM(a.shape, a.dtype) for a in list(srcs) + list(lands)]
                   + [jax.ShapeDtypeStruct((8, LANES), f32)]),
        in_specs=[hbm] * (2 * n_arr) + [pl.BlockSpec(memory_space=pl.ANY)] * n_after,
        out_specs=[sem] * (2 * n_arr) + [hbm] * (2 * n_arr) + [pl.BlockSpec(memory_space=pltpu.VMEM)],
        input_output_aliases={i: 2 * n_arr + i for i in range(2 * n_arr)},
        compiler_params=pltpu.CompilerParams(has_side_effects=pltpu.SideEffectType.DATAFLOW_SIDE_EFFECTING),
    )(*[pltpu.with_memory_space_constraint(a, pltpu.HBM) for a in list(srcs) + list(lands)], *after)
    return (outs[:n_arr], outs[n_arr:2 * n_arr], outs[2 * n_arr:3 * n_arr], outs[3 * n_arr:4 * n_arr], outs[-1])


def _split_wait(started, plan, after, name):
    send_sems, recv_sems, srcs, lands, _ = started
    n_arr = len(srcs)
    hbm = pl.BlockSpec(memory_space=pltpu.HBM)
    sem = pl.BlockSpec(memory_space=pltpu.SEMAPHORE)

    def body(*refs):
        src_refs, land_refs = refs[:n_arr], refs[n_arr:2 * n_arr]
        s_sems, r_sems = refs[2 * n_arr:3 * n_arr], refs[3 * n_arr:4 * n_arr]
        me = _position()
        for i in range(n_arr):
            for j, (src, _, peer, arrival) in enumerate(plan(i, src_refs[i], land_refs[i], me)):
                cp = pltpu.make_async_remote_copy(src_ref=src, dst_ref=arrival, send_sem=s_sems[i].at[j], recv_sem=r_sems[i].at[j],
                                                  device_id=peer, device_id_type=MESH)
                cp.wait_send()
                cp.wait_recv()

    outs = pl.pallas_call(
        body, name=name,
        out_shape=[pltpu.HBM(a.shape, a.dtype) for a in list(srcs) + list(lands)],
        in_specs=[hbm] * (2 * n_arr) + [sem] * (2 * n_arr) + [pl.BlockSpec(memory_space=pl.ANY)],
        out_specs=[hbm] * (2 * n_arr),
        input_output_aliases={i: i for i in range(2 * n_arr)},
        compiler_params=pltpu.CompilerParams(has_side_effects=pltpu.SideEffectType.DATAFLOW_SIDE_EFFECTING),
    )(*srcs, *lands, *send_sems, *recv_sems, after)
    return outs[:n_arr], outs[n_arr:]


def _landing_zone(blk, me, name):
    m, n = blk.shape

    def body(me_ref, x_ref, o_ref):
        del me_ref
        o_ref[...] = x_ref[...]

    return pl.pallas_call(
        body,
        grid_spec=pltpu.PrefetchScalarGridSpec(
            num_scalar_prefetch=1, grid=(1,),
            in_specs=[pl.BlockSpec((m, n), lambda i, me_ref: (0, 0))],
            out_specs=pl.BlockSpec((m, n), lambda i, me_ref: (me_ref[0], 0))),
        out_shape=jax.ShapeDtypeStruct((N_DEV * m, n), blk.dtype), name=name,
    )(jnp.reshape(me, (1,)).astype(jnp.int32), blk)


_GATHER_FLIPS = (1, 2, 4, 6)


def _gather_plan(i, src_ref, land_ref, me):
    m = src_ref.shape[0]

    def rows(pos):
        return land_ref.at[pl.ds(_index(pos) * m, m), :]

    return [(src_ref, rows(me), _flip(me, k), rows(_flip(me, k))) for k in _GATHER_FLIPS]


def _gather_forward(lands, name):
    n_arr = len(lands)
    chips = (2, 4, 6)

    def body(*refs):
        out_refs = refs[n_arr:2 * n_arr]
        send_sems, recv_sems = refs[2 * n_arr:]
        me = _position()
        sibling = _flip(me, 1)
        sends, arrivals = [], []
        for i, out_ref in enumerate(out_refs):
            m = out_ref.shape[0] // N_DEV

            def copy(pos, j):
                blk = out_ref.at[pl.ds(_index(pos) * m, m), :]
                return pltpu.make_async_remote_copy(src_ref=blk, dst_ref=blk, send_sem=send_sems.at[3 * i + j],
                                                    recv_sem=recv_sems.at[3 * i + j], device_id=sibling, device_id_type=MESH)

            for j, k in enumerate(chips):
                sends.append(copy(_flip(me, k), j))
                arrivals.append(copy(_flip(sibling, k), j))
        for cp in sends:
            cp.start()
        for cp in arrivals:
            cp.wait_recv()
        for cp in sends:
            cp.wait_send()

    anyspec = pl.BlockSpec(memory_space=pl.ANY)
    return pl.pallas_call(
        body, out_shape=[jax.ShapeDtypeStruct(a.shape, a.dtype) for a in lands],
        in_specs=[anyspec] * n_arr, out_specs=[anyspec] * n_arr, input_output_aliases={i: i for i in range(n_arr)},
        scratch_shapes=[pltpu.SemaphoreType.DMA((3 * n_arr,))] * 2, name=name,
    )(*lands)


def _chips_plan(i, src_ref, land_ref, me):
    m = src_ref.shape[0] // 4
    plan = []
    for j, k in enumerate((2, 4, 6)):
        peer = _flip(me, k)
        plan.append((src_ref.at[pl.ds((2 * peer[0] + peer[1]) * m, m), :], land_ref.at[j], peer, land_ref.at[j]))
    return plan


def _sibling_plan(i, src_ref, land_ref, me):
    m = src_ref.shape[0] // N_DEV
    sibling = _flip(me, 1)
    return [(src_ref.at[pl.ds((2 * q + 1 - me[2]) * m, m), :], land_ref.at[q], sibling, land_ref.at[q]) for q in range(4)]


def _sum_with_sibling(g, recv, name):
    m = g.shape[0] // N_DEV
    n = g.shape[1]
    tr = _pick(m, (208, 128, 64, 32, 16))
    nt = m // tr

    def body(c_ref, g_ref, r_ref, o_ref):
        del c_ref
        o_ref[...] = (g_ref[0, 0].astype(f32) + r_ref[0].astype(f32)).astype(o_ref.dtype)

    return pl.pallas_call(
        body,
        grid_spec=pltpu.PrefetchScalarGridSpec(
            num_scalar_prefetch=1, grid=(4, nt),
            in_specs=[pl.BlockSpec((1, 1, tr, n), lambda q, i, c_ref: (q, c_ref[0], i, 0)),
                      pl.BlockSpec((1, tr, n), lambda q, i, c_ref: (q, i, 0))],
            out_specs=pl.BlockSpec((tr, n), lambda q, i, c_ref: (q * nt + i, 0))),
        out_shape=jax.ShapeDtypeStruct((4 * m, n), bf16), name=name,
    )(jnp.reshape(lax.axis_index("c"), (1,)).astype(jnp.int32), g.reshape(4, 2, m, n), recv)


def _sum_with_chips(h, recv, name, slot=0, n_slots=1, into=None):
    m = h.shape[0] // 4
    n = h.shape[1]
    tr = _pick(m, (208, 128, 64, 32, 16))

    def body(h_ref, r_ref, *rest):
        o_ref = rest[-1]
        my_q = 2 * lax.axis_index("x") + lax.axis_index("y")
        own = h_ref[0].astype(f32)
        for q in range(1, 4):
            own = jnp.where(my_q == q, h_ref[q].astype(f32), own)
        o_ref[0] = ((own + r_ref[0].astype(f32)) + r_ref[1].astype(f32)) + r_ref[2].astype(f32)

    in_specs = [pl.BlockSpec((4, tr, n), lambda i: (0, i, 0)), pl.BlockSpec((3, tr, n), lambda i: (0, i, 0))]
    args = [h.reshape(4, m, n), recv]
    if into is not None:
        in_specs.append(pl.BlockSpec(memory_space=pl.ANY))
        args.append(into)
    return pl.pallas_call(
        body, grid=(m // tr,), in_specs=in_specs,
        out_specs=pl.BlockSpec((1, tr, n), lambda i: (slot, i, 0)), out_shape=jax.ShapeDtypeStruct((n_slots, m, n), f32),
        input_output_aliases={2: 0} if into is not None else {}, name=name,
    )(*args)


def _sum_slots(parts, name):
    n_slot, m, n = parts.shape
    tr = _pick(m, (208, 128, 64, 32, 16, 8))

    def body(p_ref, o_ref):
        acc = p_ref[0]
        for s in range(1, n_slot):
            acc = acc + p_ref[s]
        o_ref[...] = acc

    return pl.pallas_call(
        body, grid=(m // tr,), in_specs=[pl.BlockSpec((n_slot, tr, n), lambda i: (0, i, 0))],
        out_specs=pl.BlockSpec((tr, n), lambda i: (i, 0)), out_shape=jax.ShapeDtypeStruct((m, n), parts.dtype), name=name,
    )(parts)


def _reduce_scatter_begin(gs, name):
    lands = [lax.empty((4, g.shape[0] // N_DEV, g.shape[1]), g.dtype) for g in gs]
    return _split_start(gs, lands, _sibling_plan, 4, "rs_d2d_start_" + name)


def _reduce_scatter_middle(started, after, name):
    gs, from_sibling = _split_wait(started, _sibling_plan, after, "rs_d2d_wait_" + name)
    chip_sums = [_sum_with_sibling(g, r, f"rs_sum2_{name}_{i}") for i, (g, r) in enumerate(zip(gs, from_sibling))]
    lands = [lax.empty((3, h.shape[0] // 4, h.shape[1]), h.dtype) for h in chip_sums]
    return _split_start(chip_sums, lands, _chips_plan, 3, "rs_ici_start_" + name)


def _reduce_scatter_end(started, after, name, first_into=None, slot=0, n_slots=1):
    chip_sums, from_chips = _split_wait(started, _chips_plan, after, "rs_ici_wait_" + name)
    out = []
    for i, (h, r) in enumerate(zip(chip_sums, from_chips)):
        if i == 0:
            out.append(_sum_with_chips(h, r, f"rs_sum4_{name}_{i}", slot, n_slots, first_into))
        else:
            out.append(_sum_with_chips(h, r, f"rs_sum4_{name}_{i}")[0])
    return out


def _adamw_update(w, g, m, v):
    mm = ADAM_B1 * m + (1.0 - ADAM_B1) * g
    vv = ADAM_B2 * v + (1.0 - ADAM_B2) * jnp.square(g)
    m_hat = mm / (1.0 - ADAM_B1 ** ADAM_STEP)
    v_hat = vv / (1.0 - ADAM_B2 ** ADAM_STEP)
    return -ADAM_LR * (m_hat / (jnp.sqrt(v_hat) + ADAM_EPS) + ADAM_WD * w), mm, vv


def _adamw_many(ws, gs, ms, vs, name):
    k = len(ws)
    shapes = [w.shape for w in ws]
    flat = [[a.reshape(-1, a.shape[-1]) for a in group] for group in (ws, gs, ms, vs)]

    def body(*refs):
        for i in range(k):
            d, mm, vv = _adamw_update(*(refs[j * k + i][...] for j in range(4)))
            refs[4 * k + i][...] = d
            refs[5 * k + i][...] = mm
            refs[6 * k + i][...] = vv

    outs = pl.pallas_call(
        body, out_shape=[jax.ShapeDtypeStruct(a.shape, f32) for a in flat[0]] * 3, name=name,
    )(*flat[0], *flat[1], *flat[2], *flat[3])
    return tuple([outs[j * k + i].reshape(shapes[i]) for i in range(k)] for j in range(3))


def _adamw(w, g, m, v, name):
    shape = w.shape
    n = shape[-1]
    r = w.size // n
    w2, g2, m2, v2 = (a.reshape(r, n) for a in (w, g, m, v))
    tr = _pick(r, (256, 208, 128, 64, 32, 16, 8))

    def body(w_ref, g_ref, m_ref, v_ref, d_ref, mo_ref, vo_ref):
        d_ref[...], mo_ref[...], vo_ref[...] = _adamw_update(w_ref[...], g_ref[...], m_ref[...], v_ref[...])

    spec = pl.BlockSpec((tr, n), lambda i: (i, 0))
    outs = pl.pallas_call(
        body, grid=(r // tr,), in_specs=[spec] * 4, out_specs=[spec] * 3,
        out_shape=[jax.ShapeDtypeStruct((r, n), f32)] * 3, name=name,
    )(w2, g2, m2, v2)
    return tuple(o.reshape(shape) for o in outs)


_SMALL = ("shift_mu", "w_decay0", "a0", "k_k", "k_a", "r_k", "ln_x_w", "ln_x_b", "v_mix0", "lb_logits",
          "g_norm_w", "ln_w", "ln_b")
_NAMES = ("w_in", "shift_mu", "w_decay0", "w_decay_up", "a0", "a_up", "k_k", "k_a", "r_k", "ln_x_w", "ln_x_b",
          "v_mix0", "v_mix_down", "v_mix_up", "lb_logits", "g_norm_w", "w_out", "ln_w", "ln_b")


def _pad_rows(a, rows, at_end):
    z = jnp.zeros((rows - a.shape[0], a.shape[1]), a.dtype)
    return jnp.concatenate([a, z] if at_end else [z, a], axis=0)


def kernel(x, w_in, shift_mu, w_decay0, w_decay_up, a0, a_up, k_k, k_a, r_k, ln_x_w, ln_x_b, v_mix0, v_mix_down, v_mix_up, lb_logits, g_norm_w, w_out, ln_w, ln_b, loss_target, m_w_in, m_shift_mu, m_w_decay0, m_w_decay_up, m_a0, m_a_up, m_k_k, m_k_a, m_r_k, m_ln_x_w, m_ln_x_b, m_v_mix0, m_v_mix_down, m_v_mix_up, m_lb_logits, m_g_norm_w, m_w_out, m_ln_w, m_ln_b, v_w_in, v_shift_mu, v_w_decay0, v_w_decay_up, v_a0, v_a_up, v_k_k, v_k_a, v_r_k, v_ln_x_w, v_ln_x_b, v_v_mix0, v_v_mix_down, v_v_mix_up, v_lb_logits, v_g_norm_w, v_w_out, v_ln_w, v_ln_b):
    weights = dict(w_in=w_in, shift_mu=shift_mu, w_decay0=w_decay0, w_decay_up=w_decay_up, a0=a0, a_up=a_up, k_k=k_k,
                   k_a=k_a, r_k=r_k, ln_x_w=ln_x_w, ln_x_b=ln_x_b, v_mix0=v_mix0, v_mix_down=v_mix_down,
                   v_mix_up=v_mix_up, lb_logits=lb_logits, g_norm_w=g_norm_w, w_out=w_out, ln_w=ln_w, ln_b=ln_b)
    mom1 = dict(w_in=m_w_in, shift_mu=m_shift_mu, w_decay0=m_w_decay0, w_decay_up=m_w_decay_up, a0=m_a0, a_up=m_a_up,
                k_k=m_k_k, k_a=m_k_a, r_k=m_r_k, ln_x_w=m_ln_x_w, ln_x_b=m_ln_x_b, v_mix0=m_v_mix0,
                v_mix_down=m_v_mix_down, v_mix_up=m_v_mix_up, lb_logits=m_lb_logits, g_norm_w=m_g_norm_w,
                w_out=m_w_out, ln_w=m_ln_w, ln_b=m_ln_b)
    mom2 = dict(w_in=v_w_in, shift_mu=v_shift_mu, w_decay0=v_w_decay0, w_decay_up=v_w_decay_up, a0=v_a0, a_up=v_a_up,
                k_k=v_k_k, k_a=v_k_a, r_k=v_r_k, ln_x_w=v_ln_x_w, ln_x_b=v_ln_x_b, v_mix0=v_v_mix0,
                v_mix_down=v_v_mix_down, v_mix_up=v_v_mix_up, lb_logits=v_lb_logits, g_norm_w=v_g_norm_w,
                w_out=v_w_out, ln_w=v_ln_w, ln_b=v_ln_b)
    assert x.shape[0] == 1 and w_in.shape[0] == DEPTH
    t, d = x.shape[1], x.shape[2]
    dr = w_decay0.shape[1]
    dh = g_norm_w.shape[1]
    rank_w, rank_a, rank_v = w_decay_up.shape[1], a_up.shape[1], v_mix_up.shape[1]
    rwc = 4 * dr + rank_w + rank_a
    assert rank_w + rank_a == LANES and rank_v <= LANES and dr + dh == d
    assert t % CHUNK == 0 and dr % LANES == 0 and dh % LANES == 0 and shift_mu.shape[1] == rwc
    n_pair = dr // LANES
    me = _index(_position())

    shard = dr // N_DEV
    pack = jnp.concatenate([w_decay_up[0], w_decay_up[1], a_up[0], a_up[1], v_mix_up[0], v_mix_down[0].T], axis=0)
    win_t0, pack = _all_gather_rows([w_in[0].T.astype(bf16), pack], "ag_first")
    win_t = [win_t0, None]
    wout = [None, None]

    def start_gather(blocks, name, after):
        lands = [_landing_zone(blk, me, f"{name}_zone{i}") for i, blk in enumerate(blocks)]
        return _split_start(blocks, lands, _gather_plan, len(_GATHER_FLIPS), name, after=after)

    gather_wout0 = start_gather([w_out[0].astype(bf16)], "ag_wout0_start", (win_t[0], pack))
    gather_layer1 = start_gather([w_in[1].T.astype(bf16), w_out[1].astype(bf16)], "ag_layer1_start", (gather_wout0[-1],))
    pack = jnp.transpose(pack.reshape(N_DEV, -1, shard), (1, 0, 2)).reshape(-1, dr)
    offs = [0, rank_w, 2 * rank_w, 2 * rank_w + rank_a, 2 * rank_w + 2 * rank_a, 2 * rank_w + 2 * rank_a + rank_v,
            2 * rank_w + 2 * rank_a + 2 * rank_v]
    wdu_f = [pack[offs[0]:offs[1]], pack[offs[1]:offs[2]]]
    aup_f = [pack[offs[2]:offs[3]], pack[offs[3]:offs[4]]]
    vup_f = pack[offs[4]:offs[5]]
    vdown_f = pack[offs[5]:offs[6]].T

    def after_start(a, started):
        return a + started[-1][0:1, 0:1]

    def rwkv_params(l):
        mu = after_start(shift_mu[0:1], gather_layer1) if l == 0 else shift_mu[l:l + 1]
        prm = [mu, w_decay0[l:l + 1], a0[l:l + 1], _pad_rows(wdu_f[l], LANES, True),
               _pad_rows(aup_f[l], LANES, False)]
        if l == 1:
            prm += [v_mix0[0:1], _pad_rows(vdown_f.T, LANES, True).T, _pad_rows(vup_f, LANES, True)]
        rows = jnp.stack([k_k[l], k_a[l], r_k[l], ln_x_w[l], ln_x_b[l]] + [jnp.zeros((dr,), f32)] * 3, axis=0)
        pp = jnp.transpose(rows.reshape(8, n_pair, LANES), (1, 0, 2))
        return tuple(prm), pp

    h = x[0]
    h16 = h.astype(bf16)
    tgt = loss_target[0]
    saved = []
    vfirst = None
    for l in range(DEPTH):
        prm, pp = rwkv_params(l)
        proj = _matmul(h16, win_t[l], "nt", f"mm_proj_{l}", _MM_TILES["proj"])
        if l == 0:
            cat, vfirst, mck = _rwkv_fwd(False, proj, None, prm, pp, d)
        else:
            cat, mck = _rwkv_fwd(True, proj, vfirst, prm, pp, d)
        cat, sck = _hgrn_fwd(l == 1, proj, lb_logits, g_norm_w[l:l + 1], cat, rwc)
        if l == 0:
            _, arrived = _split_wait(gather_wout0, _gather_plan, cat, "ag_wout0_wait")
            (wout[0],) = _gather_forward(arrived, "ag_wout0_forward")
        y = _matmul(cat, wout[l], "nn", f"mm_out_{l}", _MM_TILES["out"])
        saved.append((h, h16, proj, prm, pp, mck, sck, cat, y))
        if l < DEPTH - 1:
            h, h16 = _ln_fwd(h, y, ln_w[l:l + 1], ln_b[l:l + 1])
            _, arrived = _split_wait(gather_layer1, _gather_plan, h16, "ag_layer1_wait")
            win_t[1], wout[1] = _gather_forward(arrived, "ag_layer1_forward")
        else:
            top = _ln_loss_bwd(h, y, ln_w[l:l + 1], ln_b[l:l + 1], tgt)
    loss = lax.psum(top[4][0, 0], ("x", "y", "c"))

    grads = {}
    big = {}
    dvfirst = None
    d_lbl = None
    rs_started = {}
    for l in reversed(range(DEPTH)):
        h_l, h16_l, proj, prm, pp, mck, sck, cat, y = saved[l]
        if l == DEPTH - 1:
            dy, dy16, g_ln_w, g_ln_b = top[:4]
        else:
            dy, dy16, g_ln_w, g_ln_b = _ln_bwd(h_l, y, after_start(ln_w[l:l + 1], rs_started[l + 1]), ln_b[l:l + 1], dh_out)
        dcat = _matmul(dy16, wout[l], "nt", f"mm_dcat_{l}", _MM_TILES["dcat"])
        big[("w_out", l)] = _matmul(cat, dy16, "tn", f"mm_dwout_{l}", _MM_TILES["dwout"], out_dtype=bf16)
        if l == 1:
            outs = _rwkv_bwd(True, proj, vfirst, prm, pp, mck, dcat, None)
            dproj_r, dvfirst = outs[0], outs[1]
            dprm, dpp = outs[2:-1], outs[-1]
        else:
            outs = _rwkv_bwd(False, proj, None, prm, pp, mck, dcat, dvfirst)
            dproj_r = outs[0]
            dprm, dpp = outs[1:-1], outs[-1]
        dproj, dlbl_l, dgnw = _hgrn_bwd(l == 1, proj, lb_logits, g_norm_w[l:l + 1], sck, dcat, rwc, dproj_r)
        big[("w_in", l)] = _matmul(dproj, h16_l, "tn", f"mm_dwin_{l}", _MM_TILES["dwin"], out_dtype=bf16)
        sharded = [dprm[3][:rank_w].T, dprm[4][rank_w:].T]
        if l == 1:
            sharded += [dprm[6][:, :rank_v], dprm[7][:rank_v].T,
                        jnp.zeros((dr, LANES - 2 * rank_v), f32)]
        sharded = jnp.concatenate(sharded, axis=1).astype(bf16)
        d2d = _reduce_scatter_begin([big[("w_in", l)], big[("w_out", l)], sharded], f"l{l}")
        if l == 0:
            rs_started[l] = _reduce_scatter_middle(d2d, sharded, f"l{l}")
            token = rs_started[l][-1]
        else:
            token = d2d[-1]
        dh_out = _matmul(dproj, win_t[l], "nn", f"mm_dh_{l}", _MM_TILES["dh"], add=dy, add_scale=ALPHA, after=token)
        if l > 0:
            rs_started[l] = _reduce_scatter_middle(d2d, dh_out, f"l{l}")
        dpp = jnp.transpose(dpp, (1, 0, 2)).reshape(8, dr)
        grads[l] = dict(shift_mu=dprm[0][0], w_decay0=dprm[1][0], a0=dprm[2][0],
                        k_k=dpp[0], k_a=dpp[1], r_k=dpp[2], ln_x_w=dpp[3], ln_x_b=dpp[4],
                        g_norm_w=dgnw[0], ln_w=g_ln_w[0], ln_b=g_ln_b[0])
        if l == 1:
            grads[l].update(v_mix0=dprm[5][0])
            d_lbl = dlbl_l
    grad_x = dh_out[None]

    def both(name):
        return jnp.stack([grads[0][name], grads[1][name]])

    small = dict(shift_mu=both("shift_mu"), w_decay0=both("w_decay0"), a0=both("a0"), k_k=both("k_k"), k_a=both("k_a"),
                 r_k=both("r_k"), ln_x_w=both("ln_x_w"), ln_x_b=both("ln_x_b"), v_mix0=grads[1]["v_mix0"][None],
                 lb_logits=d_lbl, g_norm_w=both("g_norm_w"), ln_w=both("ln_w"), ln_b=both("ln_b"))
    flat = jnp.concatenate([small[nm].reshape(-1) for nm in _SMALL])
    n_flat = flat.shape[0]
    rows = -(-n_flat // (8 * LANES)) * 8
    flat = jnp.concatenate([flat, jnp.zeros((rows * LANES - n_flat,), f32)]).reshape(rows, LANES)
    total = _sum_slots(_all_gather_rows([flat], "ag_small_grads")[0].reshape(N_DEV, rows, LANES), "sum_small_grads").reshape(-1)
    gsm = {}
    off = 0
    for nm in _SMALL:
        size = small[nm].size
        gsm[nm] = total[off:off + size].reshape(small[nm].shape)
        off += size
    reduced = {1: _reduce_scatter_end(rs_started[1], dh_out, "l1", None, 1, DEPTH)}
    reduced[0] = _reduce_scatter_end(rs_started[0], total, "l0", reduced[1][0], 0, DEPTH)
    g_w_in_t = reduced[0][0]
    gsm["w_in"] = jnp.transpose(g_w_in_t, (0, 2, 1))
    gsm["w_out"] = jnp.stack([reduced[l][1] for l in range(DEPTH)])
    gsm["w_decay_up"] = jnp.stack([reduced[l][2][:, :rank_w].T for l in range(DEPTH)])
    gsm["a_up"] = jnp.stack([reduced[l][2][:, rank_w:rank_w + rank_a].T for l in range(DEPTH)])
    gsm["v_mix_down"] = reduced[1][2][:, LANES:LANES + rank_v][None]
    gsm["v_mix_up"] = reduced[1][2][:, LANES + rank_v:LANES + 2 * rank_v].T[None]

    deltas, new_m, new_v = {}, {}, {}
    swap = lambda a: jnp.transpose(a, (0, 2, 1))
    deltas["w_in"], new_m["w_in"], new_v["w_in"] = (
        swap(a) for a in _adamw(swap(w_in), g_w_in_t, swap(m_w_in), swap(v_w_in), "adamw_w_in"))
    deltas["w_out"], new_m["w_out"], new_v["w_out"] = _adamw(w_out, gsm["w_out"], m_w_out, v_w_out, "adamw_w_out")
    rest = [nm for nm in _NAMES if nm not in ("w_in", "w_out")]
    d_rest, m_rest, v_rest = _adamw_many([weights[nm] for nm in rest], [gsm[nm] for nm in rest],
                                         [mom1[nm] for nm in rest], [mom2[nm] for nm in rest], "adamw_small")
    for i, nm in enumerate(rest):
        deltas[nm], new_m[nm], new_v[nm] = d_rest[i], m_rest[i], v_rest[i]
    return (loss, grad_x, *[gsm[nm] for nm in _NAMES], *[deltas[nm] for nm in _NAMES],
            *[new_m[nm] for nm in _NAMES], *[new_v[nm] for nm in _NAMES])
```
